```python
import math
import jax, jax.numpy as jnp
from jax import lax
import numpy as np

D_MODEL = 1024
BATCH = 32
SEQ = 2048
DEPTH = 2

N_EVEN = (DEPTH + 1) // 2
N_ODD = DEPTH // 2
NORM_EPS = 1e-6

A_HEADS = 4
A_DK = 128
A_DV = 128
A_WIDTH = A_HEADS * A_DK
A_CHUNK = 32
B_HEADS = 4
B_HD = 128
B_WIDTH = B_HEADS * B_HD
B_DILATIONS = ((128, 1), (512, 4), (2048, 16))
ROPE_THETA = 10000.0
IN_AB_WIDTH = 4 * A_WIDTH + 3 * B_WIDTH
MIX_WIDTH = A_WIDTH + B_WIDTH
C_GROUP = 16
C_GROUPS = D_MODEL // C_GROUP
C_STATE = 64
C_CHUNK = 128
C_MIN_NEG_RE = -1e-4
MEM_LEN = 256
X_HEADS = 4
X_HD = D_MODEL // X_HEADS
D_FF = -(-8 * D_MODEL // (3 * 256)) * 256

kernel_name = "hybrid_hgrn2_dilated_s5_block"


def rms_norm(x, w):
    xf = x.astype(jnp.float32)
    y = xf * lax.rsqrt(jnp.mean(xf * xf, axis=-1, keepdims=True) + NORM_EPS)
    return (y * w.astype(jnp.float32)).astype(x.dtype)


def split_heads(t, n_heads):
    b, l, _ = t.shape
    return t.reshape(b, l, n_heads, -1).transpose(0, 2, 1, 3)


def merge_heads(t):
    b, h, l, d = t.shape
    return t.transpose(0, 2, 1, 3).reshape(b, l, h * d)


def rotary(x, pos):
    half = x.shape[-1] // 2
    inv_freq = ROPE_THETA ** (-jnp.arange(half, dtype=jnp.float32) / half)
    ang = pos.astype(jnp.float32)[:, None] * inv_freq[None, :]
    cos, sin = jnp.cos(ang), jnp.sin(ang)
    x1, x2 = x[..., :half], x[..., half:]
    return jnp.concatenate([x1 * cos - x2 * sin, x1 * sin + x2 * cos], axis=-1)


def hgrn2_mixer(q, f_logit, i_val, g, lb, onorm_w):
    bsz, L, _ = q.shape
    nc = L // A_CHUNK
    f = lb + (1.0 - lb) * jax.nn.sigmoid(f_logit)
    log_f = jnp.log(f)
    k = 1.0 - f

    def chunked(t):
        return split_heads(t, A_HEADS).reshape(bsz, A_HEADS, nc, A_CHUNK, -1)

    qc, kc, vc, lfc = chunked(q), chunked(k), chunked(i_val), chunked(log_f)
    b = jnp.cumsum(lfc, axis=3)
    b_last = b[:, :, :, -1:, :]
    q_dec = qc * jnp.exp(b)
    k_inv = kc * jnp.exp(-b)
    k_end = kc * jnp.exp(b_last - b)
    decay = jnp.exp(b_last[:, :, :, 0, :])
    causal = jnp.tril(jnp.ones((A_CHUNK, A_CHUNK), dtype=bool))
    scores = jnp.einsum('bhncd,bhnsd->bhncs', q_dec, k_inv)
    scores = jnp.where(causal, scores, 0.0)
    o_intra = jnp.einsum('bhncs,bhnsv->bhncv', scores, vc)

    def step(S, inp):
        qd, ke, v, dec = inp
        o = jnp.einsum('bhcd,bhdv->bhcv', qd, S)
        S = dec[..., None] * S + jnp.einsum('bhcd,bhcv->bhdv', ke, v)
        return S, o

    S0 = jnp.zeros((bsz, A_HEADS, A_DK, A_DV), jnp.float32)
    xs = (jnp.moveaxis(q_dec, 2, 0), jnp.moveaxis(k_end, 2, 0),
          jnp.moveaxis(vc, 2, 0), jnp.moveaxis(decay, 2, 0))
    _, o_inter = lax.scan(step, S0, xs)
    o = (o_intra + jnp.moveaxis(o_inter, 0, 2)).reshape(bsz, A_HEADS, L, A_DV)
    o = o * lax.rsqrt(jnp.mean(o * o, axis=-1, keepdims=True) + NORM_EPS)
    o = merge_heads(o) * onorm_w.astype(jnp.float32)
    return o * jax.nn.silu(g)


def dilated_branch(q, k, v, dil, span):
    bsz, H, L, hd = q.shape
    Ls = L // dil
    nb = -(-Ls // span)
    pad = nb * span - Ls

    def strided(t):
        t = t.reshape(bsz, H, Ls, dil, hd).transpose(0, 1, 3, 2, 4)
        t = jnp.pad(t, ((0, 0), (0, 0), (0, 0), (0, pad), (0, 0)))
        return t.reshape(bsz, H, dil, nb, span, hd)

    def with_prev(t):
        prev = jnp.pad(t[:, :, :, :-1], ((0, 0), (0, 0), (0, 0), (1, 0), (0, 0), (0, 0)))
        return jnp.concatenate([prev, t], axis=4)

    qb = strided(q)
    kw, vw = with_prev(strided(k)), with_prev(strided(v))
    s = jnp.einsum('bhrnqd,bhrnkd->bhrnqk', qb, kw) * (hd ** -0.5)
    iq = jnp.arange(span)[:, None]
    ik = jnp.arange(2 * span)[None, :]
    delta = span + iq - ik
    kpos = (jnp.arange(nb)[:, None, None] - 1) * span + ik[None]
    mask = (delta >= 0) & (delta <= span) & (kpos >= 0)
    s = jnp.where(mask, s, -jnp.inf)
    m = jnp.max(s, axis=-1, keepdims=True)
    p = jnp.exp(s - m)
    l = jnp.sum(p, axis=-1)
    o = jnp.einsum('bhrnqk,bhrnkd->bhrnqd', p, vw) / l[..., None]
    lse = m[..., 0] + jnp.log(l)

    def unstride(t):
        t = t.reshape(bsz, H, dil, nb * span, -1)[:, :, :, :Ls]
        return t.transpose(0, 1, 3, 2, 4).reshape(bsz, H, L, -1)

    return unstride(o), unstride(lse[..., None])[..., 0]


def dilated_attention(q, k, v):
    outs, lses = [], []
    for window, dil in B_DILATIONS:
        o, lse = dilated_branch(q, k, v, dil, window // dil)
        outs.append(o)
        lses.append(lse)
    wts = jax.nn.softmax(jnp.stack(lses), axis=0)
    return jnp.einsum('gbhl,gbhld->bhld', wts, jnp.stack(outs))


def mix_ab(h, w_in, w_out, lb, onorm_w, pos):
    z = (h @ w_in).astype(jnp.float32)
    offs = np.cumsum([A_WIDTH] * 4 + [B_WIDTH] * 2).tolist()
    qa, fa, ia, ga, qb, kb, vb = jnp.split(z, offs, axis=-1)
    oa = hgrn2_mixer(qa, fa, ia, ga, lb, onorm_w)
    qh = rotary(split_heads(qb, B_HEADS), pos)
    kh = rotary(split_heads(kb, B_HEADS), pos)
    vh = split_heads(vb, B_HEADS)
    ob = merge_heads(dilated_attention(qh, kh, vh))
    y = jnp.concatenate([oa, ob], axis=-1).astype(h.dtype)
    return y @ w_out


def complex_affine(e1, e2):
    a1r, a1i, b1r, b1i = e1
    a2r, a2i, b2r, b2i = e2
    return (a2r * a1r - a2i * a1i,
            a2r * a1i + a2i * a1r,
            a2r * b1r - a2i * b1i + b2r,
            a2r * b1i + a2i * b1r + b2i)


def s5_mixer(h, lam_re, lam_im, log_dt, b_re, b_im, c_re, c_im, d_skip, w_glu):
    f32 = jnp.float32
    u = h.astype(f32)
    bsz, L, _ = u.shape
    lr = jnp.minimum(lam_re.astype(f32), C_MIN_NEG_RE)
    li = lam_im.astype(f32)
    dt = jnp.exp(log_dt.astype(f32))[:, None]
    mag = jnp.exp(dt * lr)
    ar, ai = mag * jnp.cos(dt * li), mag * jnp.sin(dt * li)
    den = lr * lr + li * li
    zr = ((ar - 1.0) * lr + ai * li) / den
    zi = (ai * lr - (ar - 1.0) * li) / den
    br, bi = b_re.astype(f32), b_im.astype(f32)
    bbr = zr[..., None] * br - zi[..., None] * bi
    bbi = zr[..., None] * bi + zi[..., None] * br
    cr, ci = c_re.astype(f32), c_im.astype(f32)
    nc = L // C_CHUNK
    ug = u.reshape(bsz, nc, C_CHUNK, C_GROUPS, C_GROUP).transpose(1, 0, 2, 3, 4)

    def step(state, uc):
        hr, hi = state
        xr = jnp.einsum('btgc,gpc->btgp', uc, bbr)
        xi = jnp.einsum('btgc,gpc->btgp', uc, bbi)
        a_r = jnp.broadcast_to(ar, xr.shape)
        a_i = jnp.broadcast_to(ai, xr.shape)
        pr, pi_, sr, si = lax.associative_scan(complex_affine, (a_r, a_i, xr, xi), axis=1)
        str_ = pr * hr[:, None] - pi_ * hi[:, None] + sr
        sti = pr * hi[:, None] + pi_ * hr[:, None] + si
        y = (jnp.einsum('btgp,gcp->btgc', str_, cr)
             - jnp.einsum('btgp,gcp->btgc', sti, ci))
        return (str_[:, -1], sti[:, -1]), y

    h0 = (jnp.zeros((bsz, C_GROUPS, C_STATE), f32), jnp.zeros((bsz, C_GROUPS, C_STATE), f32))
    _, y = lax.scan(step, h0, ug)
    y = y.transpose(1, 0, 2, 3, 4).reshape(bsz, L, D_MODEL)
    y = y + d_skip.astype(f32) * u
    gl = jax.nn.gelu(y, approximate=False).astype(h.dtype)
    z = gl @ w_glu
    return z[..., :D_MODEL] * jax.nn.sigmoid(z[..., D_MODEL:])


def memory_cross_attention(h, mem_n, wq, wkv, wo):
    q = split_heads(h @ wq, X_HEADS).astype(jnp.float32)
    kv = mem_n @ wkv
    k = split_heads(kv[..., :D_MODEL], X_HEADS).astype(jnp.float32)
    v = split_heads(kv[..., D_MODEL:], X_HEADS).astype(jnp.float32)
    s = jnp.einsum('bhqd,bhkd->bhqk', q, k) * (X_HD ** -0.5)
    p = jax.nn.softmax(s, axis=-1)
    o = jnp.einsum('bhqk,bhkd->bhqd', p, v).astype(h.dtype)
    return merge_heads(o) @ wo


def swiglu(h, w_in, w_out):
    z = h @ w_in
    return (jax.nn.silu(z[..., :D_FF]) * z[..., D_FF:]) @ w_out


def _fwd_setup_inputs(seed: int = 0) -> dict:
    key = jax.random.key(seed)
    ks = jax.random.split(key, 24)
    f32 = jnp.float32
    nrm = lambda k, shape, scale: jax.random.normal(k, shape, f32) * scale
    return {
        "x": nrm(ks[0], (BATCH, SEQ, D_MODEL), 1.0),
        "mem": nrm(ks[1], (BATCH, MEM_LEN, D_MODEL), 1.0),
        "norm_w": 1.0 + nrm(ks[2], (DEPTH, 6, D_MODEL), 0.05),
        "mem_norm_w": 1.0 + nrm(ks[3], (DEPTH, D_MODEL), 0.05),
        "ab_w_in": nrm(ks[4], (N_EVEN, D_MODEL, IN_AB_WIDTH), D_MODEL ** -0.5),
        "ab_w_out": nrm(ks[5], (N_EVEN, MIX_WIDTH, D_MODEL), MIX_WIDTH ** -0.5),
        "hgrn_lb_logits": nrm(ks[6], (DEPTH + 1, A_WIDTH), 0.1),
        "hgrn_out_norm_w": 1.0 + nrm(ks[7], (N_EVEN, A_WIDTH), 0.05),
        "s5_lambda_re": -0.5 + nrm(ks[8], (N_ODD, C_GROUPS, C_STATE), 0.01),
        "s5_lambda_im": np.pi * jnp.arange(C_STATE, dtype=f32) + nrm(ks[9], (N_ODD, C_GROUPS, C_STATE), 0.01),
        "s5_log_dt": jax.random.uniform(ks[10], (N_ODD, C_GROUPS), f32, math.log(1e-3), math.log(1e-1)),
        "s5_b_re": nrm(ks[11], (N_ODD, C_GROUPS, C_STATE, C_GROUP), (2 * C_GROUP) ** -0.5),
        "s5_b_im": nrm(ks[12], (N_ODD, C_GROUPS, C_STATE, C_GROUP), (2 * C_GROUP) ** -0.5),
        "s5_c_re": nrm(ks[13], (N_ODD, C_GROUPS, C_GROUP, C_STATE), 2.0 ** -0.5),
        "s5_c_im": nrm(ks[14], (N_ODD, C_GROUPS, C_GROUP, C_STATE), 2.0 ** -0.5),
        "s5_d": nrm(ks[15], (N_ODD, D_MODEL), 1.0),
        "s5_w_glu": nrm(ks[16], (N_ODD, D_MODEL, 2 * D_MODEL), D_MODEL ** -0.5),
        "xattn_wq": nrm(ks[17], (DEPTH, D_MODEL, D_MODEL), D_MODEL ** -0.5),
        "xattn_wkv": nrm(ks[18], (DEPTH, D_MODEL, 2 * D_MODEL), D_MODEL ** -0.5),
        "xattn_wo": nrm(ks[19], (DEPTH, D_MODEL, D_MODEL), D_MODEL ** -0.5),
        "ffn_w_in": nrm(ks[20], (DEPTH, D_MODEL, 2 * D_FF), D_MODEL ** -0.5),
        "ffn_w_out": nrm(ks[21], (DEPTH, D_FF, D_MODEL), D_FF ** -0.5),
    }


def _fwd_reference(x, mem, norm_w, mem_norm_w, ab_w_in, ab_w_out, hgrn_lb_logits,
              hgrn_out_norm_w, s5_lambda_re, s5_lambda_im, s5_log_dt, s5_b_re,
              s5_b_im, s5_c_re, s5_c_im, s5_d, s5_w_glu, xattn_wq, xattn_wkv,
              xattn_wo, ffn_w_in, ffn_w_out):
    L = x.shape[1]
    pos = jnp.arange(L, dtype=jnp.int32)
    lb_table = jnp.cumsum(jax.nn.softmax(hgrn_lb_logits.astype(jnp.float32), axis=0), axis=0)
    for layer in range(DEPTH):
        j = layer // 2
        h = rms_norm(x, norm_w[layer, 0])
        if layer % 2 == 0:
            y = mix_ab(h, ab_w_in[j], ab_w_out[j], lb_table[layer], hgrn_out_norm_w[j], pos)
        else:
            y = s5_mixer(h, s5_lambda_re[j], s5_lambda_im[j], s5_log_dt[j], s5_b_re[j],
                         s5_b_im[j], s5_c_re[j], s5_c_im[j], s5_d[j], s5_w_glu[j])
        x = x + rms_norm(y, norm_w[layer, 1])
        h = rms_norm(x, norm_w[layer, 2])
        mem_n = rms_norm(mem, mem_norm_w[layer])
        y = memory_cross_attention(h, mem_n, xattn_wq[layer], xattn_wkv[layer], xattn_wo[layer])
        x = x + rms_norm(y, norm_w[layer, 3])
        h = rms_norm(x, norm_w[layer, 4])
        x = x + rms_norm(swiglu(h, ffn_w_in[layer], ffn_w_out[layer]), norm_w[layer, 5])
    return x


import jax as _jax
import jax.numpy as _jnp

TWIN_FORMAT = 'train_step'
FWD_PARAMS = ['x', 'mem', 'norm_w', 'mem_norm_w', 'ab_w_in', 'ab_w_out', 'hgrn_lb_logits', 'hgrn_out_norm_w', 's5_lambda_re', 's5_lambda_im', 's5_log_dt', 's5_b_re', 's5_b_im', 's5_c_re', 's5_c_im', 's5_d', 's5_w_glu', 'xattn_wq', 'xattn_wkv', 'xattn_wo', 'ffn_w_in', 'ffn_w_out']
TWIN_WEIGHTS = ['norm_w', 'mem_norm_w', 'ab_w_in', 'ab_w_out', 'hgrn_lb_logits', 'hgrn_out_norm_w', 's5_lambda_re', 's5_lambda_im', 's5_log_dt', 's5_b_re', 's5_b_im', 's5_c_re', 's5_c_im', 's5_d', 's5_w_glu', 'xattn_wq', 'xattn_wkv', 'xattn_wo', 'ffn_w_in', 'ffn_w_out']
TWIN_DIFF_INPUT = 'x'
TWIN_INPUTS = ['x', 'mem', 'norm_w', 'mem_norm_w', 'ab_w_in', 'ab_w_out', 'hgrn_lb_logits', 'hgrn_out_norm_w', 's5_lambda_re', 's5_lambda_im', 's5_log_dt', 's5_b_re', 's5_b_im', 's5_c_re', 's5_c_im', 's5_d', 's5_w_glu', 'xattn_wq', 'xattn_wkv', 'xattn_wo', 'ffn_w_in', 'ffn_w_out', 'loss_target', 'm_norm_w', 'm_mem_norm_w', 'm_ab_w_in', 'm_ab_w_out', 'm_hgrn_lb_logits', 'm_hgrn_out_norm_w', 'm_s5_lambda_re', 'm_s5_lambda_im', 'm_s5_log_dt', 'm_s5_b_re', 'm_s5_b_im', 'm_s5_c_re', 'm_s5_c_im', 'm_s5_d', 'm_s5_w_glu', 'm_xattn_wq', 'm_xattn_wkv', 'm_xattn_wo', 'm_ffn_w_in', 'm_ffn_w_out', 'v_norm_w', 'v_mem_norm_w', 'v_ab_w_in', 'v_ab_w_out', 'v_hgrn_lb_logits', 'v_hgrn_out_norm_w', 'v_s5_lambda_re', 'v_s5_lambda_im', 'v_s5_log_dt', 'v_s5_b_re', 'v_s5_b_im', 'v_s5_c_re', 'v_s5_c_im', 'v_s5_d', 'v_s5_w_glu', 'v_xattn_wq', 'v_xattn_wkv', 'v_xattn_wo', 'v_ffn_w_in', 'v_ffn_w_out']
TWIN_OUTPUTS = ['loss', 'grad_x', 'grad_norm_w', 'grad_mem_norm_w', 'grad_ab_w_in', 'grad_ab_w_out', 'grad_hgrn_lb_logits', 'grad_hgrn_out_norm_w', 'grad_s5_lambda_re', 'grad_s5_lambda_im', 'grad_s5_log_dt', 'grad_s5_b_re', 'grad_s5_b_im', 'grad_s5_c_re', 'grad_s5_c_im', 'grad_s5_d', 'grad_s5_w_glu', 'grad_xattn_wq', 'grad_xattn_wkv', 'grad_xattn_wo', 'grad_ffn_w_in', 'grad_ffn_w_out', 'delta_norm_w', 'delta_mem_norm_w', 'delta_ab_w_in', 'delta_ab_w_out', 'delta_hgrn_lb_logits', 'delta_hgrn_out_norm_w', 'delta_s5_lambda_re', 'delta_s5_lambda_im', 'delta_s5_log_dt', 'delta_s5_b_re', 'delta_s5_b_im', 'delta_s5_c_re', 'delta_s5_c_im', 'delta_s5_d', 'delta_s5_w_glu', 'delta_xattn_wq', 'delta_xattn_wkv', 'delta_xattn_wo', 'delta_ffn_w_in', 'delta_ffn_w_out', 'new_m_norm_w', 'new_m_mem_norm_w', 'new_m_ab_w_in', 'new_m_ab_w_out', 'new_m_hgrn_lb_logits', 'new_m_hgrn_out_norm_w', 'new_m_s5_lambda_re', 'new_m_s5_lambda_im', 'new_m_s5_log_dt', 'new_m_s5_b_re', 'new_m_s5_b_im', 'new_m_s5_c_re', 'new_m_s5_c_im', 'new_m_s5_d', 'new_m_s5_w_glu', 'new_m_xattn_wq', 'new_m_xattn_wkv', 'new_m_xattn_wo', 'new_m_ffn_w_in', 'new_m_ffn_w_out', 'new_v_norm_w', 'new_v_mem_norm_w', 'new_v_ab_w_in', 'new_v_ab_w_out', 'new_v_hgrn_lb_logits', 'new_v_hgrn_out_norm_w', 'new_v_s5_lambda_re', 'new_v_s5_lambda_im', 'new_v_s5_log_dt', 'new_v_s5_b_re', 'new_v_s5_b_im', 'new_v_s5_c_re', 'new_v_s5_c_im', 'new_v_s5_d', 'new_v_s5_w_glu', 'new_v_xattn_wq', 'new_v_xattn_wkv', 'new_v_xattn_wo', 'new_v_ffn_w_in', 'new_v_ffn_w_out']
TWIN_LEAF_KINDS = {'loss': 'loss', 'grad_x': 'grad_x', 'grad_norm_w': 'grad_w', 'grad_mem_norm_w': 'grad_w', 'grad_ab_w_in': 'grad_w', 'grad_ab_w_out': 'grad_w', 'grad_hgrn_lb_logits': 'grad_w', 'grad_hgrn_out_norm_w': 'grad_w', 'grad_s5_lambda_re': 'grad_w', 'grad_s5_lambda_im': 'grad_w', 'grad_s5_log_dt': 'grad_w', 'grad_s5_b_re': 'grad_w', 'grad_s5_b_im': 'grad_w', 'grad_s5_c_re': 'grad_w', 'grad_s5_c_im': 'grad_w', 'grad_s5_d': 'grad_w', 'grad_s5_w_glu': 'grad_w', 'grad_xattn_wq': 'grad_w', 'grad_xattn_wkv': 'grad_w', 'grad_xattn_wo': 'grad_w', 'grad_ffn_w_in': 'grad_w', 'grad_ffn_w_out': 'grad_w', 'delta_norm_w': 'delta_w', 'delta_mem_norm_w': 'delta_w', 'delta_ab_w_in': 'delta_w', 'delta_ab_w_out': 'delta_w', 'delta_hgrn_lb_logits': 'delta_w', 'delta_hgrn_out_norm_w': 'delta_w', 'delta_s5_lambda_re': 'delta_w', 'delta_s5_lambda_im': 'delta_w', 'delta_s5_log_dt': 'delta_w', 'delta_s5_b_re': 'delta_w', 'delta_s5_b_im': 'delta_w', 'delta_s5_c_re': 'delta_w', 'delta_s5_c_im': 'delta_w', 'delta_s5_d': 'delta_w', 'delta_s5_w_glu': 'delta_w', 'delta_xattn_wq': 'delta_w', 'delta_xattn_wkv': 'delta_w', 'delta_xattn_wo': 'delta_w', 'delta_ffn_w_in': 'delta_w', 'delta_ffn_w_out': 'delta_w', 'new_m_norm_w': 'new_m', 'new_m_mem_norm_w': 'new_m', 'new_m_ab_w_in': 'new_m', 'new_m_ab_w_out': 'new_m', 'new_m_hgrn_lb_logits': 'new_m', 'new_m_hgrn_out_norm_w': 'new_m', 'new_m_s5_lambda_re': 'new_m', 'new_m_s5_lambda_im': 'new_m', 'new_m_s5_log_dt': 'new_m', 'new_m_s5_b_re': 'new_m', 'new_m_s5_b_im': 'new_m', 'new_m_s5_c_re': 'new_m', 'new_m_s5_c_im': 'new_m', 'new_m_s5_d': 'new_m', 'new_m_s5_w_glu': 'new_m', 'new_m_xattn_wq': 'new_m', 'new_m_xattn_wkv': 'new_m', 'new_m_xattn_wo': 'new_m', 'new_m_ffn_w_in': 'new_m', 'new_m_ffn_w_out': 'new_m', 'new_v_norm_w': 'new_v', 'new_v_mem_norm_w': 'new_v', 'new_v_ab_w_in': 'new_v', 'new_v_ab_w_out': 'new_v', 'new_v_hgrn_lb_logits': 'new_v', 'new_v_hgrn_out_norm_w': 'new_v', 'new_v_s5_lambda_re': 'new_v', 'new_v_s5_lambda_im': 'new_v', 'new_v_s5_log_dt': 'new_v', 'new_v_s5_b_re': 'new_v', 'new_v_s5_b_im': 'new_v', 'new_v_s5_c_re': 'new_v', 'new_v_s5_c_im': 'new_v', 'new_v_s5_d': 'new_v', 'new_v_s5_w_glu': 'new_v', 'new_v_xattn_wq': 'new_v', 'new_v_xattn_wkv': 'new_v', 'new_v_xattn_wo': 'new_v', 'new_v_ffn_w_in': 'new_v', 'new_v_ffn_w_out': 'new_v'}


def _forward(args):
    return _fwd_reference(*[args[k] for k in FWD_PARAMS])


def _output_shape():
    out = _jax.eval_shape(lambda: _forward(_fwd_setup_inputs(0)))
    return out.shape, out.dtype

N_MICROBATCH = 1
ADAM_LR = 0.001
ADAM_B1 = 0.9
ADAM_B2 = 0.999
ADAM_EPS = 1e-08
ADAM_WD = 0.01
ADAM_STEP = 10
PER_EXAMPLE_BATCH_AXIS = {'x': 0, 'mem': 0, 'loss_target': 0}
SHARED_INPUTS = []
_WEIGHT_DTYPES = {'norm_w': _jnp.float32, 'mem_norm_w': _jnp.float32, 'ab_w_in': _jnp.float32, 'ab_w_out': _jnp.float32, 'hgrn_lb_logits': _jnp.float32, 'hgrn_out_norm_w': _jnp.float32, 's5_lambda_re': _jnp.float32, 's5_lambda_im': _jnp.float32, 's5_log_dt': _jnp.float32, 's5_b_re': _jnp.float32, 's5_b_im': _jnp.float32, 's5_c_re': _jnp.float32, 's5_c_im': _jnp.float32, 's5_d': _jnp.float32, 's5_w_glu': _jnp.float32, 'xattn_wq': _jnp.float32, 'xattn_wkv': _jnp.float32, 'xattn_wo': _jnp.float32, 'ffn_w_in': _jnp.float32, 'ffn_w_out': _jnp.float32}
MOMENT_SCALE = {'norm_w': 4.678385e+01, 'mem_norm_w': 1.742135e+01, 'ab_w_in': 1.888811e+00, 'ab_w_out': 2.014151e+00, 'hgrn_lb_logits': 9.691436e-01, 'hgrn_out_norm_w': 3.445607e+00, 's5_lambda_re': 4.842399e+00, 's5_lambda_im': 4.271349e+00, 's5_log_dt': 3.293892e+02, 's5_b_re': 3.444362e+00, 's5_b_im': 3.519795e+00, 's5_c_re': 8.917527e-01, 's5_c_im': 8.677694e-01, 's5_d': 9.988350e+00, 's5_w_glu': 1.228527e+01, 'xattn_wq': 3.612312e+00, 'xattn_wkv': 1.263189e+01, 'xattn_wo': 1.829541e+01, 'ffn_w_in': 2.611395e+00, 'ffn_w_out': 5.298469e+00}


def _to_microbatches(a, axis):
    t = _jnp.moveaxis(a, axis, 0)
    t = t.reshape((N_MICROBATCH, t.shape[0] // N_MICROBATCH) + t.shape[1:])
    return _jnp.moveaxis(t, 1, axis + 1)


def setup_inputs(seed: int = 0) -> dict:
    inp = _fwd_setup_inputs(seed)
    key = _jax.random.fold_in(_jax.random.key(seed), 7919)
    shape, _ = _output_shape()
    out = dict(inp)
    out["loss_target"] = _jax.random.normal(_jax.random.fold_in(key, 0), shape, _jnp.float32)
    for i, name in enumerate(TWIN_WEIGHTS):
        w = inp[name].astype(_jnp.float32)
        if MOMENT_SCALE is None:
            s = _jnp.sqrt(_jnp.mean(_jnp.square(w)) + 1e-30)
        else:
            s = MOMENT_SCALE[name]
        km, kv = _jax.random.split(_jax.random.fold_in(key, i + 1))
        out[name] = w
        out["m_" + name] = s * _jax.random.normal(km, w.shape, _jnp.float32)
        out["v_" + name] = (s * s) * _jax.random.uniform(kv, w.shape, _jnp.float32, 0.5, 1.5)
    if N_MICROBATCH > 1:
        for name, axis in PER_EXAMPLE_BATCH_AXIS.items():
            out[name] = _to_microbatches(out[name], axis)
    return {'x': out['x'], 'mem': out['mem'], 'norm_w': out['norm_w'], 'mem_norm_w': out['mem_norm_w'], 'ab_w_in': out['ab_w_in'], 'ab_w_out': out['ab_w_out'], 'hgrn_lb_logits': out['hgrn_lb_logits'], 'hgrn_out_norm_w': out['hgrn_out_norm_w'], 's5_lambda_re': out['s5_lambda_re'], 's5_lambda_im': out['s5_lambda_im'], 's5_log_dt': out['s5_log_dt'], 's5_b_re': out['s5_b_re'], 's5_b_im': out['s5_b_im'], 's5_c_re': out['s5_c_re'], 's5_c_im': out['s5_c_im'], 's5_d': out['s5_d'], 's5_w_glu': out['s5_w_glu'], 'xattn_wq': out['xattn_wq'], 'xattn_wkv': out['xattn_wkv'], 'xattn_wo': out['xattn_wo'], 'ffn_w_in': out['ffn_w_in'], 'ffn_w_out': out['ffn_w_out'], 'loss_target': out['loss_target'], 'm_norm_w': out['m_norm_w'], 'm_mem_norm_w': out['m_mem_norm_w'], 'm_ab_w_in': out['m_ab_w_in'], 'm_ab_w_out': out['m_ab_w_out'], 'm_hgrn_lb_logits': out['m_hgrn_lb_logits'], 'm_hgrn_out_norm_w': out['m_hgrn_out_norm_w'], 'm_s5_lambda_re': out['m_s5_lambda_re'], 'm_s5_lambda_im': out['m_s5_lambda_im'], 'm_s5_log_dt': out['m_s5_log_dt'], 'm_s5_b_re': out['m_s5_b_re'], 'm_s5_b_im': out['m_s5_b_im'], 'm_s5_c_re': out['m_s5_c_re'], 'm_s5_c_im': out['m_s5_c_im'], 'm_s5_d': out['m_s5_d'], 'm_s5_w_glu': out['m_s5_w_glu'], 'm_xattn_wq': out['m_xattn_wq'], 'm_xattn_wkv': out['m_xattn_wkv'], 'm_xattn_wo': out['m_xattn_wo'], 'm_ffn_w_in': out['m_ffn_w_in'], 'm_ffn_w_out': out['m_ffn_w_out'], 'v_norm_w': out['v_norm_w'], 'v_mem_norm_w': out['v_mem_norm_w'], 'v_ab_w_in': out['v_ab_w_in'], 'v_ab_w_out': out['v_ab_w_out'], 'v_hgrn_lb_logits': out['v_hgrn_lb_logits'], 'v_hgrn_out_norm_w': out['v_hgrn_out_norm_w'], 'v_s5_lambda_re': out['v_s5_lambda_re'], 'v_s5_lambda_im': out['v_s5_lambda_im'], 'v_s5_log_dt': out['v_s5_log_dt'], 'v_s5_b_re': out['v_s5_b_re'], 'v_s5_b_im': out['v_s5_b_im'], 'v_s5_c_re': out['v_s5_c_re'], 'v_s5_c_im': out['v_s5_c_im'], 'v_s5_d': out['v_s5_d'], 'v_s5_w_glu': out['v_s5_w_glu'], 'v_xattn_wq': out['v_xattn_wq'], 'v_xattn_wkv': out['v_xattn_wkv'], 'v_xattn_wo': out['v_xattn_wo'], 'v_ffn_w_in': out['v_ffn_w_in'], 'v_ffn_w_out': out['v_ffn_w_out']}


def _loss(weights, diff, rest, loss_target):
    with _jax.named_scope("forward"):
        args = {**rest, TWIN_DIFF_INPUT: diff, **{k: w.astype(_WEIGHT_DTYPES[k]) for k, w in weights.items()}}
        y = _forward(args)
    with _jax.named_scope("loss_head"):
        err = _jnp.square(y.astype(_jnp.float32) - loss_target)
        return 0.5 * _jnp.sum(_jnp.mean(err, axis=-1)) if err.ndim else 0.5 * err


def _adamw(w, g, m, v):
    m = ADAM_B1 * m + (1.0 - ADAM_B1) * g
    v = ADAM_B2 * v + (1.0 - ADAM_B2) * _jnp.square(g)
    m_hat = m / (1.0 - ADAM_B1 ** ADAM_STEP)
    v_hat = v / (1.0 - ADAM_B2 ** ADAM_STEP)
    delta = -ADAM_LR * (m_hat / (_jnp.sqrt(v_hat) + ADAM_EPS) + ADAM_WD * w)
    return delta, m, v


def reference(x, mem, norm_w, mem_norm_w, ab_w_in, ab_w_out, hgrn_lb_logits, hgrn_out_norm_w, s5_lambda_re, s5_lambda_im, s5_log_dt, s5_b_re, s5_b_im, s5_c_re, s5_c_im, s5_d, s5_w_glu, xattn_wq, xattn_wkv, xattn_wo, ffn_w_in, ffn_w_out, loss_target, m_norm_w, m_mem_norm_w, m_ab_w_in, m_ab_w_out, m_hgrn_lb_logits, m_hgrn_out_norm_w, m_s5_lambda_re, m_s5_lambda_im, m_s5_log_dt, m_s5_b_re, m_s5_b_im, m_s5_c_re, m_s5_c_im, m_s5_d, m_s5_w_glu, m_xattn_wq, m_xattn_wkv, m_xattn_wo, m_ffn_w_in, m_ffn_w_out, v_norm_w, v_mem_norm_w, v_ab_w_in, v_ab_w_out, v_hgrn_lb_logits, v_hgrn_out_norm_w, v_s5_lambda_re, v_s5_lambda_im, v_s5_log_dt, v_s5_b_re, v_s5_b_im, v_s5_c_re, v_s5_c_im, v_s5_d, v_s5_w_glu, v_xattn_wq, v_xattn_wkv, v_xattn_wo, v_ffn_w_in, v_ffn_w_out):
    given = dict(x=x, mem=mem, norm_w=norm_w, mem_norm_w=mem_norm_w, ab_w_in=ab_w_in, ab_w_out=ab_w_out, hgrn_lb_logits=hgrn_lb_logits, hgrn_out_norm_w=hgrn_out_norm_w, s5_lambda_re=s5_lambda_re, s5_lambda_im=s5_lambda_im, s5_log_dt=s5_log_dt, s5_b_re=s5_b_re, s5_b_im=s5_b_im, s5_c_re=s5_c_re, s5_c_im=s5_c_im, s5_d=s5_d, s5_w_glu=s5_w_glu, xattn_wq=xattn_wq, xattn_wkv=xattn_wkv, xattn_wo=xattn_wo, ffn_w_in=ffn_w_in, ffn_w_out=ffn_w_out, loss_target=loss_target, m_norm_w=m_norm_w, m_mem_norm_w=m_mem_norm_w, m_ab_w_in=m_ab_w_in, m_ab_w_out=m_ab_w_out, m_hgrn_lb_logits=m_hgrn_lb_logits, m_hgrn_out_norm_w=m_hgrn_out_norm_w, m_s5_lambda_re=m_s5_lambda_re, m_s5_lambda_im=m_s5_lambda_im, m_s5_log_dt=m_s5_log_dt, m_s5_b_re=m_s5_b_re, m_s5_b_im=m_s5_b_im, m_s5_c_re=m_s5_c_re, m_s5_c_im=m_s5_c_im, m_s5_d=m_s5_d, m_s5_w_glu=m_s5_w_glu, m_xattn_wq=m_xattn_wq, m_xattn_wkv=m_xattn_wkv, m_xattn_wo=m_xattn_wo, m_ffn_w_in=m_ffn_w_in, m_ffn_w_out=m_ffn_w_out, v_norm_w=v_norm_w, v_mem_norm_w=v_mem_norm_w, v_ab_w_in=v_ab_w_in, v_ab_w_out=v_ab_w_out, v_hgrn_lb_logits=v_hgrn_lb_logits, v_hgrn_out_norm_w=v_hgrn_out_norm_w, v_s5_lambda_re=v_s5_lambda_re, v_s5_lambda_im=v_s5_lambda_im, v_s5_log_dt=v_s5_log_dt, v_s5_b_re=v_s5_b_re, v_s5_b_im=v_s5_b_im, v_s5_c_re=v_s5_c_re, v_s5_c_im=v_s5_c_im, v_s5_d=v_s5_d, v_s5_w_glu=v_s5_w_glu, v_xattn_wq=v_xattn_wq, v_xattn_wkv=v_xattn_wkv, v_xattn_wo=v_xattn_wo, v_ffn_w_in=v_ffn_w_in, v_ffn_w_out=v_ffn_w_out)
    weights = {n: given[n] for n in TWIN_WEIGHTS}
    shared = {n: given[n] for n in SHARED_INPUTS}
    per_example = {n: given[n] for n in ['x', 'mem']}
    grad_fn = _jax.value_and_grad(_loss, argnums=(0, 1))

    def one_microbatch(ex, loss_target):
        ex = dict(ex)
        diff = ex.pop(TWIN_DIFF_INPUT)
        return grad_fn(weights, diff, {**shared, **ex}, loss_target)

    if N_MICROBATCH == 1:
        loss, (grad_w, grad_x) = one_microbatch(per_example, given["loss_target"])
    else:
        def body(carry, xs):
            loss_sum, grad_sum = carry
            l_k, (gw_k, gx_k) = one_microbatch(xs[0], xs[1])
            with _jax.named_scope("update"):
                return (loss_sum + l_k, _jax.tree.map(_jnp.add, grad_sum, gw_k)), gx_k

        init = (_jnp.zeros((), _jnp.float32), _jax.tree.map(_jnp.zeros_like, weights))
        (loss, grad_w), grad_x = _jax.lax.scan(body, init, (per_example, given["loss_target"]))
    with _jax.named_scope("update"):
        delta_w, new_m, new_v = {}, {}, {}
        for n in TWIN_WEIGHTS:
            delta_w[n], new_m[n], new_v[n] = _adamw(weights[n], grad_w[n], given["m_" + n], given["v_" + n])
    return (loss, grad_x, *[grad_w[n] for n in TWIN_WEIGHTS], *[delta_w[n] for n in TWIN_WEIGHTS],
            *[new_m[n] for n in TWIN_WEIGHTS], *[new_v[n] for n in TWIN_WEIGHTS])
```

```python
import functools
import math

import numpy as np
import jax
import jax.numpy as jnp
from jax import lax
from jax.experimental import pallas as pl
from jax.experimental.pallas import tpu as pltpu

F32 = jnp.float32
BF16 = jnp.bfloat16
HI = lax.Precision.HIGHEST

D_MODEL = 1024
NORM_EPS = 1e-6
A_WIDTH = 512
A_HEAD = 128
A_CHUNK = 32
A_SUPER = 256
B_SPAN = 128
B_DILS = (1, 4, 16)
ROPE_THETA = 10000.0
C_GROUPS = 64
C_GROUP = 16
C_STATE = 64
C_TC = 16
C_MIN_NEG_RE = -1e-4
MEM_LEN = 256
X_HEADS = 4
X_HD = 256
D_FF = 2816
N_DEV = 8
LANES = 128

ADAM_LR, ADAM_B1, ADAM_B2, ADAM_EPS, ADAM_WD, ADAM_STEP = 0.001, 0.9, 0.999, 1e-08, 0.01, 10

NEG_BIG = -1e30


def _tile(n, pref):
    for p in (pref, 512, 256, 128):
        if p <= pref and n % p == 0:
            return p
    return n


def _cparams(*sem):
    return pltpu.CompilerParams(dimension_semantics=sem, vmem_limit_bytes=56 * 1024 * 1024)


def _sigmoid(x):
    return 1.0 / (1.0 + jnp.exp(-x))


def _erf(x):
    ax = jnp.abs(x)
    t = 1.0 / (1.0 + 0.3275911 * ax)
    poly = t * (0.254829592 + t * (-0.284496736 + t * (1.421413741 + t * (-1.453152027 + t * 1.061405429))))
    y = 1.0 - poly * jnp.exp(-ax * ax)
    return jnp.where(x < 0, -y, y)


def _mm(a, b, *, ta=False, tb=False, out_dtype=F32, name):
    M, K = (a.shape[1], a.shape[0]) if ta else a.shape
    N = b.shape[0] if tb else b.shape[1]
    assert (b.shape[1] if tb else b.shape[0]) == K
    tm, tn, tk = _tile(M, 512), _tile(N, 512), _tile(K, 1024 if not ta else 512)
    nk = K // tk
    a_spec = pl.BlockSpec((tk, tm), lambda i, j, k: (k, i)) if ta else pl.BlockSpec((tm, tk), lambda i, j, k: (i, k))
    b_spec = pl.BlockSpec((tn, tk), lambda i, j, k: (j, k)) if tb else pl.BlockSpec((tk, tn), lambda i, j, k: (k, j))
    dims = (((0 if ta else 1,), (1 if tb else 0,)), ((), ()))

    def body(a_ref, b_ref, o_ref, acc_ref):
        k = pl.program_id(2)
        part = lax.dot_general(a_ref[...].astype(BF16), b_ref[...].astype(BF16), dims, preferred_element_type=F32)

        @pl.when(k == 0)
        def _():
            acc_ref[...] = part

        @pl.when(k > 0)
        def _():
            acc_ref[...] += part

        @pl.when(k == nk - 1)
        def _():
            o_ref[...] = acc_ref[...].astype(o_ref.dtype)

    return pl.pallas_call(
        body, name=name, grid=(M // tm, N // tn, nk),
        in_specs=[a_spec, b_spec], out_specs=pl.BlockSpec((tm, tn), lambda i, j, k: (i, j)),
        out_shape=jax.ShapeDtypeStruct((M, N), out_dtype),
        scratch_shapes=[pltpu.VMEM((tm, tn), F32)],
        compiler_params=_cparams("parallel", "parallel", "arbitrary"),
    )(a, b)


def _rms_fwd(x, w, res=None, *, name):
    T, C = x.shape
    tm = _tile(T, 256)
    has_res = res is not None

    def body(*refs):
        x_ref, w_ref = refs[0], refs[1]
        o_ref = refs[-1]
        xv = x_ref[...]
        r = lax.rsqrt(jnp.mean(xv * xv, axis=-1, keepdims=True) + NORM_EPS)
        y = xv * r * w_ref[...]
        if has_res:
            y = y + refs[2][...]
        o_ref[...] = y

    row = pl.BlockSpec((tm, C), lambda i: (i, 0))
    vec = pl.BlockSpec((1, C), lambda i: (0, 0))
    ins = [x, w] + ([res] if has_res else [])
    return pl.pallas_call(
        body, name=name, grid=(T // tm,), in_specs=[row, vec] + ([row] if has_res else []), out_specs=row,
        out_shape=jax.ShapeDtypeStruct((T, C), F32), compiler_params=_cparams("parallel"),
    )(*ins)


def _rms_bwd(x, w, dy, add=None, *, name):
    T, C = x.shape
    tm = _tile(T, 256)
    has_add = add is not None

    def body(*refs):
        x_ref, w_ref, dy_ref = refs[:3]
        dx_ref, dw_ref = refs[-2], refs[-1]
        xv = x_ref[...]
        r = lax.rsqrt(jnp.mean(xv * xv, axis=-1, keepdims=True) + NORM_EPS)
        xh = xv * r
        g = dy_ref[...]
        part = jnp.sum(g * xh, axis=0, keepdims=True)

        @pl.when(pl.program_id(0) == 0)
        def _():
            dw_ref[...] = part

        @pl.when(pl.program_id(0) > 0)
        def _():
            dw_ref[...] += part

        gx = g * w_ref[...]
        dx = r * (gx - xh * jnp.mean(gx * xh, axis=-1, keepdims=True))
        if has_add:
            dx = dx + refs[3][...]
        dx_ref[...] = dx

    row = pl.BlockSpec((tm, C), lambda i: (i, 0))
    vec = pl.BlockSpec((1, C), lambda i: (0, 0))
    ins = [x, w, dy] + ([add] if has_add else [])
    return pl.pallas_call(
        body, name=name, grid=(T // tm,), in_specs=[row, vec, row] + ([row] if has_add else []),
        out_specs=[row, vec],
        out_shape=[jax.ShapeDtypeStruct((T, C), F32), jax.ShapeDtypeStruct((1, C), F32)],
        compiler_params=_cparams("arbitrary"),
    )(*ins)


def _gated_fwd(z, kind, *, name):
    T, W2 = z.shape
    W = W2 // 2
    tm, tn = _tile(T, 512), _tile(W, 512)
    nb = W // tn

    def body(a_ref, b_ref, o_ref):
        a, b = a_ref[...].astype(F32), b_ref[...].astype(F32)
        if kind == "swiglu":
            o_ref[...] = (a * _sigmoid(a) * b).astype(o_ref.dtype)
        else:
            o_ref[...] = (a * _sigmoid(b)).astype(o_ref.dtype)

    return pl.pallas_call(
        body, name=name, grid=(T // tm, nb),
        in_specs=[pl.BlockSpec((tm, tn), lambda i, j: (i, j)), pl.BlockSpec((tm, tn), lambda i, j: (i, j + nb))],
        out_specs=pl.BlockSpec((tm, tn), lambda i, j: (i, j)),
        out_shape=jax.ShapeDtypeStruct((T, W), F32), compiler_params=_cparams("parallel", "parallel"),
    )(z, z)


def _gated_bwd(z, dout, kind, *, name):
    T, W2 = z.shape
    W = W2 // 2
    tm, tn = _tile(T, 512), _tile(W, 512)
    nb = W // tn

    def body(a_ref, b_ref, d_ref, o_ref):
        a, b, d = a_ref[...].astype(F32), b_ref[...].astype(F32), d_ref[...]
        first = pl.program_id(1) < nb
        if kind == "swiglu":
            s = _sigmoid(a)
            da = d * b * (s * (1.0 + a * (1.0 - s)))
            db = d * a * s
        else:
            s = _sigmoid(b)
            da = d * s
            db = d * a * s * (1.0 - s)
        o_ref[...] = jnp.where(first, da, db).astype(o_ref.dtype)

    return pl.pallas_call(
        body, name=name, grid=(T // tm, 2 * nb),
        in_specs=[pl.BlockSpec((tm, tn), lambda i, j: (i, j % nb)), pl.BlockSpec((tm, tn), lambda i, j: (i, j % nb + nb)),
                  pl.BlockSpec((tm, tn), lambda i, j: (i, j % nb))],
        out_specs=pl.BlockSpec((tm, tn), lambda i, j: (i, j)),
        out_shape=jax.ShapeDtypeStruct((T, W2), F32), compiler_params=_cparams("parallel", "parallel"),
    )(z, z, dout)


_NT = (((1,), (1,)), ((), ()))
_TN = (((0,), (0,)), ((), ()))


def _dot(a, b, dims=None, precision=None):
    if dims is None:
        return jnp.dot(a, b, preferred_element_type=F32, precision=precision)
    return lax.dot_general(a, b, dims, preferred_element_type=F32, precision=precision)


def _xattn_fwd(q, kv, B, L, *, name):
    T = q.shape[0]
    tq = 256
    nq = L // tq
    scale = X_HD ** -0.5

    def body(q_ref, k_ref, v_ref, o_ref):
        for h in range(X_HEADS):
            sl = slice(h * X_HD, (h + 1) * X_HD)
            qh, kh, vh = q_ref[:, sl].astype(BF16), k_ref[:, sl].astype(BF16), v_ref[:, sl].astype(BF16)
            s = _dot(qh, kh, _NT) * scale
            m = jnp.max(s, axis=-1, keepdims=True)
            p = jnp.exp(s - m)
            l = jnp.sum(p, axis=-1, keepdims=True)
            o_ref[:, sl] = _dot(p.astype(BF16), vh) / l

    return pl.pallas_call(
        body, name=name, grid=(B, nq),
        in_specs=[pl.BlockSpec((tq, D_MODEL), lambda b, i: (b * nq + i, 0)),
                  pl.BlockSpec((MEM_LEN, D_MODEL), lambda b, i: (b, 0)),
                  pl.BlockSpec((MEM_LEN, D_MODEL), lambda b, i: (b, 1))],
        out_specs=pl.BlockSpec((tq, D_MODEL), lambda b, i: (b * nq + i, 0)),
        out_shape=jax.ShapeDtypeStruct((T, D_MODEL), F32), compiler_params=_cparams("parallel", "parallel"),
    )(q, kv, kv)


def _xattn_bwd(q, kv, do, B, L, *, name):
    T = q.shape[0]
    tq = 256
    nq = L // tq
    scale = X_HD ** -0.5

    def body(q_ref, k_ref, v_ref, do_ref, dq_ref, dkv_ref):
        @pl.when(pl.program_id(1) == 0)
        def _():
            dkv_ref[...] = jnp.zeros_like(dkv_ref)

        for h in range(X_HEADS):
            sl = slice(h * X_HD, (h + 1) * X_HD)
            slv = slice(D_MODEL + h * X_HD, D_MODEL + (h + 1) * X_HD)
            qh, kh, vh = q_ref[:, sl].astype(BF16), k_ref[:, sl].astype(BF16), v_ref[:, sl].astype(BF16)
            doh = do_ref[:, sl].astype(BF16)
            s = _dot(qh, kh, _NT) * scale
            m = jnp.max(s, axis=-1, keepdims=True)
            e = jnp.exp(s - m)
            p = e / jnp.sum(e, axis=-1, keepdims=True)
            dkv_ref[:, slv] += _dot(p.astype(BF16), doh, _TN)
            dp = _dot(doh, vh, _NT)
            ds = p * (dp - jnp.sum(dp * p, axis=-1, keepdims=True)) * scale
            dsb = ds.astype(BF16)
            dq_ref[:, sl] = _dot(dsb, kh)
            dkv_ref[:, sl] += _dot(dsb, qh, _TN)

    return pl.pallas_call(
        body, name=name, grid=(B, nq),
        in_specs=[pl.BlockSpec((tq, D_MODEL), lambda b, i: (b * nq + i, 0)),
                  pl.BlockSpec((MEM_LEN, D_MODEL), lambda b, i: (b, 0)),
                  pl.BlockSpec((MEM_LEN, D_MODEL), lambda b, i: (b, 1)),
                  pl.BlockSpec((tq, D_MODEL), lambda b, i: (b * nq + i, 0))],
        out_specs=[pl.BlockSpec((tq, D_MODEL), lambda b, i: (b * nq + i, 0)),
                   pl.BlockSpec((MEM_LEN, 2 * D_MODEL), lambda b, i: (b, 0))],
        out_shape=[jax.ShapeDtypeStruct((T, D_MODEL), F32), jax.ShapeDtypeStruct((B * MEM_LEN, 2 * D_MODEL), F32)],
        compiler_params=_cparams("parallel", "arbitrary"),
    )(q, kv, kv, do)


def _chunk_masks():
    row = lax.broadcasted_iota(jnp.int32, (A_SUPER, A_SUPER), 0)
    col = lax.broadcasted_iota(jnp.int32, (A_SUPER, A_SUPER), 1)
    same = jnp.right_shift(row, 5) == jnp.right_shift(col, 5)
    return same, same & (col <= row), same & (col >= row)


def _hgrn_gates(fa, lb):
    sig = _sigmoid(fa)
    f = lb + (1.0 - lb) * sig
    return sig, f, jnp.log(f), 1.0 - f


def _hgrn_fwd(z, lb, onw, B, L, *, name):
    T = B * L
    ns = L // A_SUPER
    nch = A_SUPER // A_CHUNK

    def body(q_ref, f_ref, v_ref, g_ref, lb_ref, w_ref, oa_ref, o_ref, s_ref, st_ref):
        @pl.when(pl.program_id(2) == 0)
        def _():
            st_ref[...] = jnp.zeros_like(st_ref)

        s_ref[0] = st_ref[...]
        same, tril, _ = _chunk_masks()
        q, v = q_ref[...], v_ref[...]
        _, _, lf, k = _hgrn_gates(f_ref[...], lb_ref[...])
        bcs = _dot(tril.astype(F32), lf, precision=HI)
        bl = _dot(same.astype(F32), lf, precision=HI)
        qd = (q * jnp.exp(bcs)).astype(BF16)
        ki = (k * jnp.exp(-bcs)).astype(BF16)
        ke = (k * jnp.exp(bl - bcs)).astype(BF16)
        dec = jnp.exp(bl)
        vb = v.astype(BF16)
        a = jnp.where(tril, _dot(qd, ki, _NT), 0.0)
        o_ref[...] = _dot(a.astype(BF16), vb)
        st = st_ref[...]
        for c in range(nch):
            rs = slice(c * A_CHUNK, (c + 1) * A_CHUNK)
            o_ref[rs, :] += _dot(qd[rs], st.astype(BF16), _NT)
            st = st * dec[c * A_CHUNK:c * A_CHUNK + 1, :] + _dot(vb[rs], ke[rs], _TN)
        st_ref[...] = st
        o = o_ref[...]
        r = lax.rsqrt(jnp.mean(o * o, axis=-1, keepdims=True) + NORM_EPS)
        g = g_ref[...]
        oa_ref[...] = o * r * w_ref[...] * (g * _sigmoid(g))

    def zspec(off):
        return pl.BlockSpec((A_SUPER, A_HEAD), lambda b, h, n: (b * ns + n, off + h))

    hvec = pl.BlockSpec((1, A_HEAD), lambda b, h, n: (0, h))
    ospec = pl.BlockSpec((A_SUPER, A_HEAD), lambda b, h, n: (b * ns + n, h))
    return pl.pallas_call(
        body, name=name, grid=(B, 4, ns),
        in_specs=[zspec(0), zspec(4), zspec(8), zspec(12), hvec, hvec],
        out_specs=[ospec, ospec, pl.BlockSpec((1, A_HEAD, A_HEAD), lambda b, h, n: ((b * 4 + h) * ns + n, 0, 0))],
        out_shape=[jax.ShapeDtypeStruct((T, A_WIDTH), F32), jax.ShapeDtypeStruct((T, A_WIDTH), F32),
                   jax.ShapeDtypeStruct((B * 4 * ns, A_HEAD, A_HEAD), F32)],
        scratch_shapes=[pltpu.VMEM((A_HEAD, A_HEAD), F32)],
        compiler_params=_cparams("parallel", "parallel", "arbitrary"),
    )(z, z, z, z, lb, onw)


def _hgrn_bwd(z, lb, onw, o_raw, s_start, doa, B, L, *, name):
    T = B * L
    ns = L // A_SUPER
    nch = A_SUPER // A_CHUNK

    def body(q_ref, f_ref, v_ref, g_ref, lb_ref, w_ref, o_ref, s_ref, doa_ref,
             dq_ref, df_ref, dv_ref, dg_ref, dw_ref, dlb_ref, dst_ref, sc_ref, dqd_ref, dke_ref, dblx_ref):
        @pl.when(pl.program_id(2) == 0)
        def _():
            dst_ref[...] = jnp.zeros_like(dst_ref)
            dw_ref[...] = jnp.zeros_like(dw_ref)
            dlb_ref[...] = jnp.zeros_like(dlb_ref)

        same, tril, triu = _chunk_masks()
        q, v, g, lb, w = q_ref[...], v_ref[...], g_ref[...], lb_ref[...], w_ref[...]
        sig, f, lf, k = _hgrn_gates(f_ref[...], lb)
        bcs = _dot(tril.astype(F32), lf, precision=HI)
        bl = _dot(same.astype(F32), lf, precision=HI)
        eb, enb, eeb = jnp.exp(bcs), jnp.exp(-bcs), jnp.exp(bl - bcs)
        qd, ki, ke = q * eb, k * enb, k * eeb
        qdb, kib, keb, vb = qd.astype(BF16), ki.astype(BF16), ke.astype(BF16), v.astype(BF16)
        dec = jnp.exp(bl)
        o = o_ref[...]
        r = lax.rsqrt(jnp.mean(o * o, axis=-1, keepdims=True) + NORM_EPS)
        on = o * r
        sg = _sigmoid(g)
        silu_g = g * sg
        doa = doa_ref[...]
        dg_ref[...] = doa * on * w * (sg * (1.0 + g * (1.0 - sg)))
        dw_ref[0] += jnp.sum(doa * on * silu_g, axis=0, keepdims=True)
        don = doa * w * silu_g
        do = r * (don - on * jnp.mean(don * on, axis=-1, keepdims=True))
        dob = do.astype(BF16)
        a = jnp.where(tril, _dot(qdb, kib, _NT), 0.0).astype(BF16)
        da = jnp.where(tril, _dot(dob, vb, _NT), 0.0).astype(BF16)
        dv_ref[...] = _dot(a, dob, _TN)
        dqd_ref[...] = _dot(da, kib)
        dki = _dot(da, qdb, _TN)
        st = s_ref[0]
        for c in range(nch):
            rs = slice(c * A_CHUNK, (c + 1) * A_CHUNK)
            sc_ref[c] = st
            st = st * dec[c * A_CHUNK:c * A_CHUNK + 1, :] + _dot(vb[rs], keb[rs], _TN)
        dst = dst_ref[...]
        for c in reversed(range(nch)):
            rs = slice(c * A_CHUNK, (c + 1) * A_CHUNK)
            dec_c = dec[c * A_CHUNK:c * A_CHUNK + 1, :]
            dstb = dst.astype(BF16)
            stc = sc_ref[c]
            dv_ref[rs, :] += _dot(keb[rs], dstb, _NT)
            dke_ref[rs, :] = _dot(vb[rs], dstb)
            ddec = jnp.sum(dst * stc, axis=0, keepdims=True)
            dqd_ref[rs, :] += _dot(dob[rs], stc.astype(BF16))
            dblx_ref[rs, :] = jnp.broadcast_to(ddec * dec_c, (A_CHUNK, A_HEAD))
            dst = dst * dec_c + _dot(dob[rs], qdb[rs], _TN)
        dst_ref[...] = dst
        dqd, dke = dqd_ref[...], dke_ref[...]
        dq_ref[...] = dqd * eb
        keke = dke * ke
        db = dqd * qd - dki * ki - keke
        dbl = _dot(same.astype(F32), keke, precision=HI) + dblx_ref[...]
        dk = dki * enb + dke * eeb
        dlf = _dot(triu.astype(F32), db, precision=HI) + dbl
        dff = dlf / f - dk
        df_ref[...] = dff * (1.0 - lb) * sig * (1.0 - sig)
        dlb_ref[0] += jnp.sum(dff * (1.0 - sig), axis=0, keepdims=True)

    def rev(n):
        return ns - 1 - n

    def zspec(off):
        return pl.BlockSpec((A_SUPER, A_HEAD), lambda b, h, n: (b * ns + rev(n), off + h))

    hvec = pl.BlockSpec((1, A_HEAD), lambda b, h, n: (0, h))
    ospec = pl.BlockSpec((A_SUPER, A_HEAD), lambda b, h, n: (b * ns + rev(n), h))
    acc = pl.BlockSpec((1, 1, A_HEAD), lambda b, h, n: (b * 4 + h, 0, 0))
    big = jax.ShapeDtypeStruct((T, A_WIDTH), F32)
    small = jax.ShapeDtypeStruct((B * 4, 1, A_HEAD), F32)
    return pl.pallas_call(
        body, name=name, grid=(B, 4, ns),
        in_specs=[zspec(0), zspec(4), zspec(8), zspec(12), hvec, hvec, ospec,
                  pl.BlockSpec((1, A_HEAD, A_HEAD), lambda b, h, n: ((b * 4 + h) * ns + rev(n), 0, 0)), ospec],
        out_specs=[ospec, ospec, ospec, ospec, acc, acc],
        out_shape=[big, big, big, big, small, small],
        scratch_shapes=[pltpu.VMEM((A_HEAD, A_HEAD), F32), pltpu.VMEM((nch, A_HEAD, A_HEAD), F32),
                        pltpu.VMEM((A_SUPER, A_HEAD), F32), pltpu.VMEM((A_SUPER, A_HEAD), F32),
                        pltpu.VMEM((A_SUPER, A_HEAD), F32)],
        compiler_params=_cparams("parallel", "parallel", "arbitrary"),
    )(z, z, z, z, lb, onw, o_raw, s_start, doa)


def _rope_tables(L):
    half = A_HEAD // 2
    inv_freq = ROPE_THETA ** (-jnp.arange(half, dtype=F32) / half)
    ang = jnp.arange(L, dtype=F32)[:, None] * inv_freq[None, :]
    cos, sin = jnp.cos(ang), jnp.sin(ang)
    return jnp.concatenate([cos, cos], axis=-1), jnp.concatenate([-sin, sin], axis=-1)


def _rope_fwd(z, cos2, sin2, col_off, B, L, *, name):
    T = B * L
    tm = 256
    nl = L // tm

    def body(x_ref, c_ref, s_ref, o_ref):
        x = x_ref[...]
        o_ref[...] = x * c_ref[...] + pltpu.roll(x, A_HEAD // 2, 1) * s_ref[...]

    tab = pl.BlockSpec((tm, A_HEAD), lambda i, h: (i % nl, 0))
    return pl.pallas_call(
        body, name=name, grid=(T // tm, 4),
        in_specs=[pl.BlockSpec((tm, A_HEAD), lambda i, h: (i, col_off + h)), tab, tab],
        out_specs=pl.BlockSpec((tm, A_HEAD), lambda i, h: (i, h)),
        out_shape=jax.ShapeDtypeStruct((T, 512), F32), compiler_params=_cparams("parallel", "parallel"),
    )(z, cos2, sin2)


def _sum3(d1, d2, d3, cos2, sin2, rotate, B, L, *, name):
    T = B * L
    tm = 256
    nl = L // tm

    def body(a_ref, b_ref, c_ref, cs_ref, sn_ref, o_ref):
        d = a_ref[...] + b_ref[...] + c_ref[...]
        if rotate:
            d = d * cs_ref[...] - pltpu.roll(d, A_HEAD // 2, 1) * sn_ref[...]
        o_ref[...] = d

    blk = pl.BlockSpec((tm, A_HEAD), lambda i, h: (i, h))
    tab = pl.BlockSpec((tm, A_HEAD), lambda i, h: (i % nl, 0))
    return pl.pallas_call(
        body, name=name, grid=(T // tm, 4), in_specs=[blk, blk, blk, tab, tab], out_specs=blk,
        out_shape=jax.ShapeDtypeStruct((T, 512), F32), compiler_params=_cparams("parallel", "parallel"),
    )(d1, d2, d3, cos2, sin2)


def _band_masks():
    i = lax.broadcasted_iota(jnp.int32, (B_SPAN, B_SPAN), 0)
    j = lax.broadcasted_iota(jnp.int32, (B_SPAN, B_SPAN), 1)
    return i <= j, j <= i


def _dil_fwd(qr, kr, z, dil, B, L, *, name):
    T = B * L
    Ls = L // dil
    nb = Ls // B_SPAN
    scale = A_HEAD ** -0.5
    qv = qr.reshape(B * Ls, dil * 512)
    kv = kr.reshape(B * Ls, dil * 512)
    zv = z.reshape(B * Ls, dil * 3584)

    def body(q_ref, kp_ref, kc_ref, vp_ref, vc_ref, o_ref, l_ref):
        has_prev = pl.program_id(2) > 0
        mp, mc = _band_masks()
        mp = mp & has_prev
        for h in range(4):
            sl = slice(h * A_HEAD, (h + 1) * A_HEAD)
            qh = q_ref[:, sl].astype(BF16)
            sp = jnp.where(mp, _dot(qh, kp_ref[:, sl].astype(BF16), _NT) * scale, NEG_BIG)
            sc = jnp.where(mc, _dot(qh, kc_ref[:, sl].astype(BF16), _NT) * scale, NEG_BIG)
            m = jnp.maximum(jnp.max(sp, axis=-1, keepdims=True), jnp.max(sc, axis=-1, keepdims=True))
            pp, pc = jnp.exp(sp - m), jnp.exp(sc - m)
            l = jnp.sum(pp, axis=-1, keepdims=True) + jnp.sum(pc, axis=-1, keepdims=True)
            o = _dot(pp.astype(BF16), vp_ref[:, sl].astype(BF16)) + _dot(pc.astype(BF16), vc_ref[:, sl].astype(BF16))
            o_ref[:, sl] = o / l
            l_ref[:, sl] = jnp.broadcast_to(m + jnp.log(l), (B_SPAN, A_HEAD))

    def cur(b, r, n):
        return b * nb + n

    def prev(b, r, n):
        return b * nb + jnp.maximum(n - 1, 0)

    blk = (B_SPAN, 512)
    return pl.pallas_call(
        body, name=name, grid=(B, dil, nb),
        in_specs=[pl.BlockSpec(blk, lambda b, r, n: (cur(b, r, n), r)),
                  pl.BlockSpec(blk, lambda b, r, n: (prev(b, r, n), r)),
                  pl.BlockSpec(blk, lambda b, r, n: (cur(b, r, n), r)),
                  pl.BlockSpec(blk, lambda b, r, n: (prev(b, r, n), r * 7 + 6)),
                  pl.BlockSpec(blk, lambda b, r, n: (cur(b, r, n), r * 7 + 6))],
        out_specs=[pl.BlockSpec(blk, lambda b, r, n: (cur(b, r, n), r))] * 2,
        out_shape=[jax.ShapeDtypeStruct((B * Ls, dil * 512), F32)] * 2,
        compiler_params=_cparams("parallel", "parallel", "parallel"),
    )(qv, kv, kv, zv, zv)


def _dil_combine(os_, ls_, *, name):
    T = os_[0].shape[0]
    tm = 256

    def body(o1, o2, o3, l1, l2, l3, ob_ref, lse_ref):
        a1, a2, a3 = l1[...], l2[...], l3[...]
        m = jnp.maximum(jnp.maximum(a1, a2), a3)
        e1, e2, e3 = jnp.exp(a1 - m), jnp.exp(a2 - m), jnp.exp(a3 - m)
        den = e1 + e2 + e3
        ob_ref[...] = (e1 * o1[...] + e2 * o2[...] + e3 * o3[...]) / den
        lse_ref[...] = m + jnp.log(den)

    blk = pl.BlockSpec((tm, 512), lambda i: (i, 0))
    return pl.pallas_call(
        body, name=name, grid=(T // tm,), in_specs=[blk] * 6, out_specs=[blk, blk],
        out_shape=[jax.ShapeDtypeStruct((T, 512), F32)] * 2, compiler_params=_cparams("parallel"),
    )(*[o.reshape(T, 512) for o in os_], *[l.reshape(T, 512) for l in ls_])


def _dil_bwd_q(qr, kr, z, dout, out, lse, dil, B, L, *, name):
    T = B * L
    Ls = L // dil
    nb = Ls // B_SPAN
    scale = A_HEAD ** -0.5
    view = lambda t: t.reshape(B * Ls, dil * 512)
    zv = z.reshape(B * Ls, dil * 3584)

    def body(q_ref, kp_ref, kc_ref, vp_ref, vc_ref, do_ref, out_ref, lse_ref, dq_ref):
        has_prev = pl.program_id(2) > 0
        mp, mc = _band_masks()
        mp = mp & has_prev
        for h in range(4):
            sl = slice(h * A_HEAD, (h + 1) * A_HEAD)
            qh = q_ref[:, sl].astype(BF16)
            kp, kc = kp_ref[:, sl].astype(BF16), kc_ref[:, sl].astype(BF16)
            do = do_ref[:, sl]
            delta = jnp.sum(do * out_ref[:, sl], axis=-1, keepdims=True)
            dob = do.astype(BF16)
            lse_h = lse_ref[:, sl]
            pp = jnp.where(mp, jnp.exp(_dot(qh, kp, _NT) * scale - lse_h), 0.0)
            pc = jnp.where(mc, jnp.exp(_dot(qh, kc, _NT) * scale - lse_h), 0.0)
            dsp = pp * (_dot(dob, vp_ref[:, sl].astype(BF16), _NT) - delta) * scale
            dsc = pc * (_dot(dob, vc_ref[:, sl].astype(BF16), _NT) - delta) * scale
            dq_ref[:, sl] = _dot(dsp.astype(BF16), kp) + _dot(dsc.astype(BF16), kc)

    def cur(b, r, n):
        return b * nb + n

    def prev(b, r, n):
        return b * nb + jnp.maximum(n - 1, 0)

    blk = (B_SPAN, 512)
    cspec = pl.BlockSpec(blk, lambda b, r, n: (cur(b, r, n), r))
    return pl.pallas_call(
        body, name=name, grid=(B, dil, nb),
        in_specs=[cspec, pl.BlockSpec(blk, lambda b, r, n: (prev(b, r, n), r)), cspec,
                  pl.BlockSpec(blk, lambda b, r, n: (prev(b, r, n), r * 7 + 6)),
                  pl.BlockSpec(blk, lambda b, r, n: (cur(b, r, n), r * 7 + 6)), cspec, cspec, cspec],
        out_specs=cspec,
        out_shape=jax.ShapeDtypeStruct((B * Ls, dil * 512), F32),
        compiler_params=_cparams("parallel", "parallel", "parallel"),
    )(view(qr), view(kr), view(kr), zv, zv, view(dout), view(out), view(lse)).reshape(T, 512)


def _dil_bwd_kv(qr, kr, z, dout, out, lse, dil, B, L, *, name):
    T = B * L
    Ls = L // dil
    nb = Ls // B_SPAN
    scale = A_HEAD ** -0.5
    view = lambda t: t.reshape(B * Ls, dil * 512)
    zv = z.reshape(B * Ls, dil * 3584)

    def body(k_ref, v_ref, qa_ref, qb_ref, doa_ref, dob_ref, outa_ref, outb_ref, lsa_ref, lsb_ref, dk_ref, dv_ref):
        has_next = pl.program_id(2) < nb - 1
        mp, mc = _band_masks()
        mp = mp & has_next
        for h in range(4):
            sl = slice(h * A_HEAD, (h + 1) * A_HEAD)
            kh, vh = k_ref[:, sl].astype(BF16), v_ref[:, sl].astype(BF16)
            dk = jnp.zeros((B_SPAN, A_HEAD), F32)
            dv = jnp.zeros((B_SPAN, A_HEAD), F32)
            for q_ref, do_ref, out_ref, lse_ref, mask in ((qa_ref, doa_ref, outa_ref, lsa_ref, mc),
                                                          (qb_ref, dob_ref, outb_ref, lsb_ref, mp)):
                qh = q_ref[:, sl].astype(BF16)
                do = do_ref[:, sl]
                delta = jnp.sum(do * out_ref[:, sl], axis=-1, keepdims=True)
                dob = do.astype(BF16)
                p = jnp.where(mask, jnp.exp(_dot(qh, kh, _NT) * scale - lse_ref[:, sl]), 0.0)
                dv = dv + _dot(p.astype(BF16), dob, _TN)
                ds = p * (_dot(dob, vh, _NT) - delta) * scale
                dk = dk + _dot(ds.astype(BF16), qh, _TN)
            dk_ref[:, sl] = dk
            dv_ref[:, sl] = dv

    def cur(b, r, n):
        return b * nb + n

    def nxt(b, r, n):
        return b * nb + jnp.minimum(n + 1, nb - 1)

    blk = (B_SPAN, 512)
    cspec = pl.BlockSpec(blk, lambda b, r, n: (cur(b, r, n), r))
    nspec = pl.BlockSpec(blk, lambda b, r, n: (nxt(b, r, n), r))
    dk, dv = pl.pallas_call(
        body, name=name, grid=(B, dil, nb),
        in_specs=[cspec, pl.BlockSpec(blk, lambda b, r, n: (cur(b, r, n), r * 7 + 6)),
                  cspec, nspec, cspec, nspec, cspec, nspec, cspec, nspec],
        out_specs=[cspec, cspec],
        out_shape=[jax.ShapeDtypeStruct((B * Ls, dil * 512), F32)] * 2,
        compiler_params=_cparams("parallel", "parallel", "parallel"),
    )(view(kr), zv, view(qr), view(qr), view(dout), view(dout), view(out), view(out), view(lse), view(lse))
    return dk.reshape(T, 512), dv.reshape(T, 512)


def _s5_build(lam_re, lam_im, log_dt, b_re, b_im, c_re, c_im):
    G, P, TC = C_GROUPS, C_STATE, C_TC
    lr = jnp.minimum(lam_re, C_MIN_NEG_RE)
    li = lam_im
    dt = jnp.exp(log_dt)[:, None]
    mag = jnp.exp(dt * lr)
    ar, ai = mag * jnp.cos(dt * li), mag * jnp.sin(dt * li)
    den = lr * lr + li * li
    zr = ((ar - 1.0) * lr + ai * li) / den
    zi = (ai * lr - (ar - 1.0) * li) / den
    bbr = zr[..., None] * b_re - zi[..., None] * b_im
    bbi = zr[..., None] * b_im + zi[..., None] * b_re
    ks = jnp.arange(TC + 1, dtype=F32)[:, None, None]
    pmag = jnp.exp(ks * (dt * lr)[None])
    pr, pi = pmag * jnp.cos(ks * (dt * li)[None]), pmag * jnp.sin(ks * (dt * li)[None])
    car = c_re[None] * pr[:, :, None, :] - c_im[None] * pi[:, :, None, :]
    cai = c_re[None] * pi[:, :, None, :] + c_im[None] * pr[:, :, None, :]
    kern = (jnp.einsum('lgop,gpc->lgco', car[:TC], bbr, precision=HI)
            - jnp.einsum('lgop,gpc->lgco', cai[:TC], bbi, precision=HI))
    s_idx = jnp.arange(TC)[:, None]
    t_idx = jnp.arange(TC)[None, :]
    lag = t_idx - s_idx
    ksel = jnp.where((lag >= 0)[:, :, None, None, None], kern[jnp.clip(lag, 0, TC - 1)], 0.0)
    m_mat = ksel.transpose(2, 0, 3, 1, 4).reshape(G, TC * C_GROUP, TC * C_GROUP)
    pr_e, pi_e = pr[TC - 1 - jnp.arange(TC)], pi[TC - 1 - jnp.arange(TC)]
    er = pr_e[:, :, :, None] * bbr[None] - pi_e[:, :, :, None] * bbi[None]
    ei = pr_e[:, :, :, None] * bbi[None] + pi_e[:, :, :, None] * bbr[None]
    e_mat = jnp.concatenate([er.transpose(1, 0, 3, 2), ei.transpose(1, 0, 3, 2)], axis=-1).reshape(G, TC * C_GROUP, 2 * P)
    fr = car[1:].transpose(1, 3, 0, 2)
    fi = -cai[1:].transpose(1, 3, 0, 2)
    f_mat = jnp.concatenate([fr, fi], axis=1).reshape(G, 2 * P, TC * C_GROUP)
    return m_mat, e_mat, f_mat, pr[TC], pi[TC]


def _s5_scan_tables(lam_re, lam_im, log_dt, nsteps):
    lr = jnp.minimum(lam_re, C_MIN_NEG_RE)
    dt = jnp.exp(log_dt)[:, None]
    ks = (C_TC * 2.0 ** jnp.arange(8, dtype=F32))[None, :, None]
    keep = (jnp.arange(8) < nsteps)[None, :, None]
    pmag = jnp.exp(ks * (dt * lr)[:, None, :])
    ang = ks * (dt * lam_im)[:, None, :]
    pr = jnp.where(keep, pmag * jnp.cos(ang), 0.0)
    pi = jnp.where(keep, pmag * jnp.sin(ang), 0.0)
    return jnp.concatenate([pr, pr], axis=-1), jnp.concatenate([-pi, pi], axis=-1)


def _s5_fwd(uf, m_mat, e_mat, f_mat, tab_r, tab_i, nch, *, name):
    G, R, _ = uf.shape
    nsteps = int(math.log2(nch))

    def body(u_ref, m_ref, e_ref, f_ref, tr_ref, ti_ref, y_ref, xs_ref):
        u = u_ref[0].astype(BF16)
        x = _dot(u, e_ref[0].astype(BF16))
        pos = jnp.bitwise_and(lax.broadcasted_iota(jnp.int32, (R, 2 * C_STATE), 0), nch - 1)
        for k in range(nsteps):
            s = 1 << k
            sh = pltpu.roll(x, s, 0)
            upd = tr_ref[0, k:k + 1, :] * sh + ti_ref[0, k:k + 1, :] * pltpu.roll(sh, C_STATE, 1)
            x = x + jnp.where(pos >= s, upd, 0.0)
        xs = jnp.where(pos >= 1, pltpu.roll(x, 1, 0), 0.0)
        xs_ref[0] = xs
        y_ref[0] = _dot(u, m_ref[0].astype(BF16)) + _dot(xs.astype(BF16), f_ref[0].astype(BF16))

    def g3(shape):
        return pl.BlockSpec((1,) + shape, lambda g: (g, 0, 0))

    return pl.pallas_call(
        body, name=name, grid=(G,),
        in_specs=[g3((R, 256)), g3((256, 256)), g3((256, 128)), g3((128, 256)), g3((8, 128)), g3((8, 128))],
        out_specs=[g3((R, 256)), g3((R, 128))],
        out_shape=[jax.ShapeDtypeStruct((G, R, 256), F32), jax.ShapeDtypeStruct((G, R, 128), F32)],
        compiler_params=_cparams("parallel"),
    )(uf, m_mat, e_mat, f_mat, tab_r, tab_i)


def _s5_bwd(dy, uf, xs, m_mat, e_mat, f_mat, tab_r, tab_i, nch, *, name):
    G, R, _ = uf.shape
    nsteps = int(math.log2(nch))

    def body(dy_ref, u_ref, xs_ref, m_ref, e_ref, f_ref, tr_ref, ti_ref, du_ref, dm_ref, de_ref, df_ref, da_ref):
        dyb = dy_ref[0].astype(BF16)
        u = u_ref[0].astype(BF16)
        xs = xs_ref[0]
        dm_ref[0] = _dot(u, dyb, _TN)
        df_ref[0] = _dot(xs.astype(BF16), dyb, _TN)
        gx = _dot(dyb, f_ref[0].astype(BF16), _NT)
        pos = jnp.bitwise_and(lax.broadcasted_iota(jnp.int32, (R, 2 * C_STATE), 0), nch - 1)
        for k in range(nsteps):
            s = 1 << k
            sh = pltpu.roll(gx, R - s, 0)
            upd = tr_ref[0, k:k + 1, :] * sh - ti_ref[0, k:k + 1, :] * pltpu.roll(sh, C_STATE, 1)
            gx = gx + jnp.where(pos + s < nch, upd, 0.0)
        de_in = jnp.where(pos + 1 < nch, pltpu.roll(gx, R - 1, 0), 0.0)
        deb = de_in.astype(BF16)
        de_ref[0] = _dot(u, deb, _TN)
        du_ref[0] = _dot(dyb, m_ref[0].astype(BF16), _NT) + _dot(deb, e_ref[0].astype(BF16), _NT)
        da_ref[0] = jnp.zeros((8, 128), F32)
        da_ref[0, 0:1, :] = jnp.sum(de_in * xs, axis=0, keepdims=True)
        da_ref[0, 1:2, :] = jnp.sum(de_in * pltpu.roll(xs, C_STATE, 1), axis=0, keepdims=True)

    def g3(shape):
        return pl.BlockSpec((1,) + shape, lambda g: (g, 0, 0))

    return pl.pallas_call(
        body, name=name, grid=(G,),
        in_specs=[g3((R, 256)), g3((R, 256)), g3((R, 128)), g3((256, 256)), g3((256, 128)), g3((128, 256)),
                  g3((8, 128)), g3((8, 128))],
        out_specs=[g3((R, 256)), g3((256, 256)), g3((256, 128)), g3((128, 256)), g3((8, 128))],
        out_shape=[jax.ShapeDtypeStruct((G, R, 256), F32), jax.ShapeDtypeStruct((G, 256, 256), F32),
                   jax.ShapeDtypeStruct((G, 256, 128), F32), jax.ShapeDtypeStruct((G, 128, 256), F32),
                   jax.ShapeDtypeStruct((G, 8, 128), F32)],
        compiler_params=_cparams("parallel"),
    )(dy, uf, xs, m_mat, e_mat, f_mat, tab_r, tab_i)


def _s5_act_fwd(ys, u, dsk, *, name):
    T, C = u.shape
    tm = 256

    def body(ys_ref, u_ref, d_ref, gl_ref, y_ref):
        y = ys_ref[...] + d_ref[...] * u_ref[...]
        y_ref[...] = y
        gl_ref[...] = 0.5 * y * (1.0 + _erf(y * (2.0 ** -0.5)))

    row = pl.BlockSpec((tm, C), lambda i: (i, 0))
    vec = pl.BlockSpec((1, C), lambda i: (0, 0))
    return pl.pallas_call(
        body, name=name, grid=(T // tm,), in_specs=[row, row, vec], out_specs=[row, row],
        out_shape=[jax.ShapeDtypeStruct((T, C), F32)] * 2, compiler_params=_cparams("parallel"),
    )(ys, u, dsk)


def _s5_act_bwd(y, u, dsk, dgl, *, name):
    T, C = u.shape
    tm = 256

    def body(y_ref, u_ref, d_ref, dgl_ref, dy_ref, du_ref, dd_ref):
        y = y_ref[...]
        cdf = 0.5 * (1.0 + _erf(y * (2.0 ** -0.5)))
        pdf = jnp.exp(-0.5 * y * y) * (1.0 / math.sqrt(2.0 * math.pi))
        dy = dgl_ref[...] * (cdf + y * pdf)
        dy_ref[...] = dy
        du_ref[...] = dy * d_ref[...]
        part = jnp.sum(dy * u_ref[...], axis=0, keepdims=True)

        @pl.when(pl.program_id(0) == 0)
        def _():
            dd_ref[...] = part

        @pl.when(pl.program_id(0) > 0)
        def _():
            dd_ref[...] += part

    row = pl.BlockSpec((tm, C), lambda i: (i, 0))
    vec = pl.BlockSpec((1, C), lambda i: (0, 0))
    return pl.pallas_call(
        body, name=name, grid=(T // tm,), in_specs=[row, row, vec, row], out_specs=[row, row, vec],
        out_shape=[jax.ShapeDtypeStruct((T, C), F32), jax.ShapeDtypeStruct((T, C), F32), jax.ShapeDtypeStruct((1, C), F32)],
        compiler_params=_cparams("arbitrary"),
    )(y, u, dsk, dgl)


def _add2(a, b, *, name):
    T, C = a.shape
    tm = 256

    def body(a_ref, b_ref, o_ref):
        o_ref[...] = a_ref[...] + b_ref[...]

    row = pl.BlockSpec((tm, C), lambda i: (i, 0))
    return pl.pallas_call(body, name=name, grid=(T // tm,), in_specs=[row, row], out_specs=row,
                          out_shape=jax.ShapeDtypeStruct((T, C), F32), compiler_params=_cparams("parallel"))(a, b)


def _loss_head(y, target, *, name):
    T, C = y.shape
    tm = 256

    def body(y_ref, t_ref, l_ref, d_ref):
        err = y_ref[...] - t_ref[...]
        d_ref[...] = err * (1.0 / C)
        sq = err * err
        part = jnp.zeros((8, LANES), F32)
        for r in range(0, tm, 8):
            for c in range(0, C, LANES):
                part = part + sq[r:r + 8, c:c + LANES]

        @pl.when(pl.program_id(0) == 0)
        def _():
            l_ref[...] = part

        @pl.when(pl.program_id(0) > 0)
        def _():
            l_ref[...] += part

    row = pl.BlockSpec((tm, C), lambda i: (i, 0))
    acc = pl.BlockSpec((8, LANES), lambda i: (0, 0))
    return pl.pallas_call(
        body, name=name, grid=(T // tm,), in_specs=[row, row], out_specs=[acc, row],
        out_shape=[jax.ShapeDtypeStruct((8, LANES), F32), jax.ShapeDtypeStruct((T, C), F32)],
        compiler_params=_cparams("arbitrary"),
    )(y, target)


def _adamw(w, g, m, v, *, name):
    shape = w.shape
    size = int(np.prod(shape))
    cols = LANES if (shape[-1] < LANES and size % LANES == 0) else shape[-1]
    rows = size // cols
    tm = _tile(rows, 256) if rows % 8 == 0 else rows
    w2, g2, m2, v2 = (t.reshape(rows, cols) for t in (w, g, m, v))

    def body(w_ref, g_ref, m_ref, v_ref, d_ref, nm_ref, nv_ref):
        gg = g_ref[...]
        nm = ADAM_B1 * m_ref[...] + (1.0 - ADAM_B1) * gg
        nv = ADAM_B2 * v_ref[...] + (1.0 - ADAM_B2) * (gg * gg)
        m_hat = nm / (1.0 - ADAM_B1 ** ADAM_STEP)
        v_hat = nv / (1.0 - ADAM_B2 ** ADAM_STEP)
        d_ref[...] = -ADAM_LR * (m_hat / (jnp.sqrt(v_hat) + ADAM_EPS) + ADAM_WD * w_ref[...])
        nm_ref[...] = nm
        nv_ref[...] = nv

    blk = pl.BlockSpec((tm, cols), lambda i: (i, 0))
    outs = pl.pallas_call(
        body, name=name, grid=(rows // tm,), in_specs=[blk] * 4, out_specs=[blk] * 3,
        out_shape=[jax.ShapeDtypeStruct((rows, cols), F32)] * 3, compiler_params=_cparams("parallel"),
    )(w2, g2, m2, v2)
    return tuple(o.reshape(shape) for o in outs)


_HBM = pl.BlockSpec(memory_space=pltpu.HBM)
_MESH = pl.DeviceIdType.MESH


def _logical(px, py, pc):
    return 4 * px + 2 * py + pc


def _all_gather(shard, *, name):
    R, C = shard.shape

    def body(x_ref, out_ref, send_sems, recv_sems, local_sem):
        x, y, c = lax.axis_index("x"), lax.axis_index("y"), lax.axis_index("c")
        me, sibling = (x, y, c), (x, y, 1 - c)
        chips = [(1 - x, y), (x, 1 - y), (1 - x, 1 - y)]

        def rows(px, py, pc):
            return out_ref.at[_logical(px, py, pc)]

        def copy(k, block, to, src=None):
            return pltpu.make_async_remote_copy(
                src_ref=rows(*block) if src is None else src, dst_ref=rows(*block),
                send_sem=send_sems.at[k], recv_sem=recv_sems.at[k], device_id=to, device_id_type=_MESH)

        mine = pltpu.make_async_copy(x_ref, rows(*me), local_sem)
        mine.start()
        first = [copy(0, me, sibling, src=x_ref)]
        first += [copy(1 + j, me, (*chip, c), src=x_ref) for j, chip in enumerate(chips)]
        for cp in first:
            cp.start()
        passed = [copy(4 + j, (*chip, c), sibling) for j, chip in enumerate(chips)]
        for j, chip in enumerate(chips):
            copy(1 + j, (*chip, c), me).wait_recv()
            passed[j].start()
        copy(0, sibling, me).wait_recv()
        for j, chip in enumerate(chips):
            copy(4 + j, (*chip, 1 - c), me).wait_recv()
        for cp in first + passed:
            cp.wait_send()
        mine.wait()

    return pl.pallas_call(
        body, name=name, out_shape=jax.ShapeDtypeStruct((N_DEV, R, C), shard.dtype),
        in_specs=[_HBM], out_specs=_HBM,
        scratch_shapes=[pltpu.SemaphoreType.DMA((7,)), pltpu.SemaphoreType.DMA((7,)), pltpu.SemaphoreType.DMA],
    )(shard)


def _all_to_all(rows_in, *, name):
    _, R, C = rows_in.shape

    def body(in_ref, out_ref, send_sems, recv_sems, local_sem):
        x, y, c = lax.axis_index("x"), lax.axis_index("y"), lax.axis_index("c")
        me = _logical(x, y, c)
        mine = pltpu.make_async_copy(in_ref.at[me], out_ref.at[me], local_sem)
        mine.start()
        copies = []
        for k in range(1, N_DEV):
            px, py, pc = x ^ (k >> 2), y ^ ((k >> 1) & 1), c ^ (k & 1)
            copies.append(pltpu.make_async_remote_copy(
                src_ref=in_ref.at[_logical(px, py, pc)], dst_ref=out_ref.at[me],
                send_sem=send_sems.at[k - 1], recv_sem=recv_sems.at[k - 1],
                device_id=(px, py, pc), device_id_type=_MESH))
        for cp in copies:
            cp.start()
        for k in range(1, N_DEV):
            px, py, pc = x ^ (k >> 2), y ^ ((k >> 1) & 1), c ^ (k & 1)
            pltpu.make_async_remote_copy(
                src_ref=in_ref.at[me], dst_ref=out_ref.at[_logical(px, py, pc)],
                send_sem=send_sems.at[k - 1], recv_sem=recv_sems.at[k - 1],
                device_id=(px, py, pc), device_id_type=_MESH).wait_recv()
        for cp in copies:
            cp.wait_send()
        mine.wait()

    return pl.pallas_call(
        body, name=name, out_shape=jax.ShapeDtypeStruct(rows_in.shape, rows_in.dtype),
        in_specs=[_HBM], out_specs=_HBM,
        scratch_shapes=[pltpu.SemaphoreType.DMA((7,)), pltpu.SemaphoreType.DMA((7,)), pltpu.SemaphoreType.DMA],
    )(rows_in)


def _sum_rows(stacked, *, name):
    _, R, C = stacked.shape
    tr = _tile(R, 2048) if R % 16 == 0 else R
    if R % tr:
        tr = R

    def body(s_ref, o_ref):
        acc = s_ref[0].astype(F32)
        for k in range(1, N_DEV):
            acc = acc + s_ref[k].astype(F32)
        o_ref[...] = acc

    return pl.pallas_call(
        body, name=name, grid=(R // tr,),
        in_specs=[pl.BlockSpec((N_DEV, tr, C), lambda i: (0, i, 0))], out_specs=pl.BlockSpec((tr, C), lambda i: (i, 0)),
        out_shape=jax.ShapeDtypeStruct((R, C), F32), compiler_params=_cparams("parallel"),
    )(stacked)


_SHARDED = (("ab_w_in", 2), ("ab_w_out", 1), ("s5_w_glu", 2), ("xattn_wq", 1), ("xattn_wkv", 2), ("xattn_wo", 1),
            ("ffn_w_in", 2), ("ffn_w_out", 1))
_PACK_ALIGN = 1024 * LANES


def _pack_local(shards, dtype):
    flat = jnp.concatenate([shards[n].astype(dtype).reshape(-1) for n, _ in _SHARDED])
    pad = (-flat.shape[0]) % _PACK_ALIGN
    flat = jnp.pad(flat, (0, pad))
    return flat.reshape(-1, LANES)


def _unpack_full(gathered, shard_shapes):
    flat = gathered.reshape(N_DEV, -1)
    out, off = {}, 0
    for n, ax in _SHARDED:
        shp = shard_shapes[n]
        size = int(np.prod(shp))
        blk = flat[:, off:off + size].reshape((N_DEV,) + tuple(shp))
        out[n] = jnp.concatenate([blk[k] for k in range(N_DEV)], axis=ax)
        off += size
    return out


def _pack_for_owners(full_grads, dtype):
    rows = []
    for k in range(N_DEV):
        parts = []
        for n, ax in _SHARDED:
            g = full_grads[n]
            w = g.shape[ax] // N_DEV
            parts.append(lax.slice_in_dim(g, k * w, (k + 1) * w, axis=ax).astype(dtype).reshape(-1))
        flat = jnp.concatenate(parts)
        rows.append(jnp.pad(flat, (0, (-flat.shape[0]) % _PACK_ALIGN)))
    return jnp.stack(rows).reshape(N_DEV, -1, LANES)


def _unpack_local(packed, shard_shapes):
    flat = packed.reshape(-1)
    out, off = {}, 0
    for n, _ in _SHARDED:
        shp = shard_shapes[n]
        size = int(np.prod(shp))
        out[n] = flat[off:off + size].reshape(shp)
        off += size
    return out


def _lb_from_logits(logits):
    return jnp.cumsum(jax.nn.softmax(logits, axis=0), axis=0)[0:1]


def _to_groups(t, T):
    return t.reshape(T // C_TC, C_TC, C_GROUPS, C_GROUP).transpose(2, 0, 1, 3).reshape(C_GROUPS, T // C_TC, C_TC * C_GROUP)


def _from_groups(t, T):
    return t.reshape(C_GROUPS, T // C_TC, C_TC, C_GROUP).transpose(1, 2, 0, 3).reshape(T, D_MODEL)


def _local_step(x, mem, target, W):
    B, L, _ = x.shape
    T = B * L
    x0 = x.reshape(T, D_MODEL)
    memf = mem.reshape(B * MEM_LEN, D_MODEL)
    nw = W["norm_w"]
    cos2, sin2 = _rope_tables(L)
    G = {}

    def vec(v):
        return v.reshape(1, -1)

    saved = []
    xin = x0
    for layer in range(2):
        s = {"x0": xin}
        tag = f"l{layer}"
        h1 = _rms_fwd(xin, vec(nw[layer, 0]), name=f"norm_pre_mix_{tag}")
        s["h1"] = h1
        if layer == 0:
            lb, lb_vjp = jax.vjp(_lb_from_logits, W["hgrn_lb_logits"])
            onw = W["hgrn_out_norm_w"].reshape(1, A_WIDTH)
            z = _mm(h1, W["ab_w_in"][0], name="ab_in")
            oa, o_raw, s_start = _hgrn_fwd(z, lb, onw, B, L, name="hgrn_fwd")
            qr = _rope_fwd(z, cos2, sin2, 16, B, L, name="rope_q")
            kr = _rope_fwd(z, cos2, sin2, 20, B, L, name="rope_k")
            os_, ls_ = [], []
            for dil in B_DILS:
                o_g, l_g = _dil_fwd(qr, kr, z, dil, B, L, name=f"dil_fwd_{dil}")
                os_.append(o_g)
                ls_.append(l_g)
            ob, lse = _dil_combine(os_, ls_, name="dil_combine")
            ymix = jnp.concatenate([oa, ob], axis=-1)
            y1 = _mm(ymix, W["ab_w_out"][0], name="ab_out")
            s.update(z=z, lb=lb, lb_vjp=lb_vjp, onw=onw, o_raw=o_raw, s_start=s_start, qr=qr, kr=kr, ob=ob, lse=lse, ymix=ymix)
        else:
            p5 = tuple(W[n][0] for n in ("s5_lambda_re", "s5_lambda_im", "s5_log_dt", "s5_b_re", "s5_b_im", "s5_c_re", "s5_c_im"))
            (m_mat, e_mat, f_mat, a16r, a16i), s5_vjp = jax.vjp(_s5_build, *p5)
            nch = L // C_TC
            tab_r, tab_i = _s5_scan_tables(p5[0], p5[1], p5[2], int(math.log2(nch)))
            uf = _to_groups(h1, T)
            yf, xs = _s5_fwd(uf, m_mat, e_mat, f_mat, tab_r, tab_i, nch, name="s5_fwd")
            dsk = W["s5_d"].reshape(1, D_MODEL)
            gl, ypre = _s5_act_fwd(_from_groups(yf, T), h1, dsk, name="s5_act_fwd")
            zg = _mm(gl, W["s5_w_glu"][0], name="s5_glu_in")
            y1 = _gated_fwd(zg, "glu", name="s5_glu")
            s.update(s5_vjp=s5_vjp, mats=(m_mat, e_mat, f_mat, tab_r, tab_i), uf=uf, xs=xs, dsk=dsk, gl=gl, ypre=ypre, zg=zg, nch=nch)
        x1 = _rms_fwd(y1, vec(nw[layer, 1]), xin, name=f"norm_post_mix_{tag}")
        h2 = _rms_fwd(x1, vec(nw[layer, 2]), name=f"norm_pre_x_{tag}")
        memn = _rms_fwd(memf, vec(W["mem_norm_w"][layer]), name=f"norm_mem_{tag}")
        q = _mm(h2, W["xattn_wq"][layer], name=f"x_q_{tag}")
        kv = _mm(memn, W["xattn_wkv"][layer], name=f"x_kv_{tag}")
        o = _xattn_fwd(q, kv, B, L, name=f"x_attn_{tag}")
        y2 = _mm(o, W["xattn_wo"][layer], name=f"x_o_{tag}")
        x2 = _rms_fwd(y2, vec(nw[layer, 3]), x1, name=f"norm_post_x_{tag}")
        h3 = _rms_fwd(x2, vec(nw[layer, 4]), name=f"norm_pre_ffn_{tag}")
        zf = _mm(h3, W["ffn_w_in"][layer], name=f"ffn_in_{tag}")
        u = _gated_fwd(zf, "swiglu", name=f"ffn_act_{tag}")
        y3 = _mm(u, W["ffn_w_out"][layer], name=f"ffn_out_{tag}")
        x3 = _rms_fwd(y3, vec(nw[layer, 5]), x2, name=f"norm_post_ffn_{tag}")
        s.update(y1=y1, x1=x1, h2=h2, memn=memn, q=q, kv=kv, o=o, y2=y2, x2=x2, h3=h3, zf=zf, u=u, y3=y3)
        saved.append(s)
        xin = x3

    loss_parts, dx = _loss_head(xin, target.reshape(T, D_MODEL), name="loss_head")

    d_norm = [[None] * 6 for _ in range(2)]
    d_memn = [None, None]
    for layer in (1, 0):
        s = saved[layer]
        tag = f"l{layer}"
        dy3, d_norm[layer][5] = _rms_bwd(s["y3"], vec(nw[layer, 5]), dx, name=f"bnorm_post_ffn_{tag}")
        du = _mm(dy3, W["ffn_w_out"][layer], tb=True, name=f"b_ffn_out_dx_{tag}")
        g_out = _mm(s["u"], dy3, ta=True, name=f"b_ffn_out_dw_{tag}")
        dzf = _gated_bwd(s["zf"], du, "swiglu", name=f"b_ffn_act_{tag}")
        g_in = _mm(s["h3"], dzf, ta=True, name=f"b_ffn_in_dw_{tag}")
        dh3 = _mm(dzf, W["ffn_w_in"][layer], tb=True, name=f"b_ffn_in_dx_{tag}")
        dx, d_norm[layer][4] = _rms_bwd(s["x2"], vec(nw[layer, 4]), dh3, dx, name=f"bnorm_pre_ffn_{tag}")
        G.setdefault("ffn_w_in", [None, None])[layer] = g_in
        G.setdefault("ffn_w_out", [None, None])[layer] = g_out
        dy2, d_norm[layer][3] = _rms_bwd(s["y2"], vec(nw[layer, 3]), dx, name=f"bnorm_post_x_{tag}")
        do = _mm(dy2, W["xattn_wo"][layer], tb=True, name=f"b_x_o_dx_{tag}")
        g_wo = _mm(s["o"], dy2, ta=True, name=f"b_x_o_dw_{tag}")
        dq, dkv = _xattn_bwd(s["q"], s["kv"], do, B, L, name=f"b_x_attn_{tag}")
        g_wq = _mm(s["h2"], dq, ta=True, name=f"b_x_q_dw_{tag}")
        dh2 = _mm(dq, W["xattn_wq"][layer], tb=True, name=f"b_x_q_dx_{tag}")
        g_wkv = _mm(s["memn"], dkv, ta=True, name=f"b_x_kv_dw_{tag}")
        dmemn = _mm(dkv, W["xattn_wkv"][layer], tb=True, name=f"b_x_kv_dx_{tag}")
        _, d_memn[layer] = _rms_bwd(memf, vec(W["mem_norm_w"][layer]), dmemn, name=f"bnorm_mem_{tag}")
        dx, d_norm[layer][2] = _rms_bwd(s["x1"], vec(nw[layer, 2]), dh2, dx, name=f"bnorm_pre_x_{tag}")
        G.setdefault("xattn_wq", [None, None])[layer] = g_wq
        G.setdefault("xattn_wkv", [None, None])[layer] = g_wkv
        G.setdefault("xattn_wo", [None, None])[layer] = g_wo
        dy1, d_norm[layer][1] = _rms_bwd(s["y1"], vec(nw[layer, 1]), dx, name=f"bnorm_post_mix_{tag}")
        if layer == 0:
            z = s["z"]
            dymix = _mm(dy1, W["ab_w_out"][0], tb=True, name="b_ab_out_dx")
            G["ab_w_out"] = _mm(s["ymix"], dy1, ta=True, name="b_ab_out_dw")[None]
            doa, dob = dymix[:, :A_WIDTH], dymix[:, A_WIDTH:]
            dqa, dfa, dia, dga, d_onw, d_lb = _hgrn_bwd(z, s["lb"], s["onw"], s["o_raw"], s["s_start"], doa, B, L, name="hgrn_bwd")
            dqs, dks, dvs = [], [], []
            for dil in B_DILS:
                dqs.append(_dil_bwd_q(s["qr"], s["kr"], z, dob, s["ob"], s["lse"], dil, B, L, name=f"dil_bwd_q_{dil}"))
                dk_g, dv_g = _dil_bwd_kv(s["qr"], s["kr"], z, dob, s["ob"], s["lse"], dil, B, L, name=f"dil_bwd_kv_{dil}")
                dks.append(dk_g)
                dvs.append(dv_g)
            dqb = _sum3(*dqs, cos2, sin2, True, B, L, name="b_rope_q")
            dkb = _sum3(*dks, cos2, sin2, True, B, L, name="b_rope_k")
            dvb = _sum3(*dvs, cos2, sin2, False, B, L, name="b_sum_v")
            dz = jnp.concatenate([dqa, dfa, dia, dga, dqb, dkb, dvb], axis=-1)
            G["ab_w_in"] = _mm(s["h1"], dz, ta=True, name="b_ab_in_dw")[None]
            dh1 = _mm(dz, W["ab_w_in"][0], tb=True, name="b_ab_in_dx")
            G["hgrn_out_norm_w"] = jnp.sum(d_onw.reshape(B, A_WIDTH), axis=0, keepdims=True)
            d_lb_row = jnp.sum(d_lb.reshape(B, A_WIDTH), axis=0, keepdims=True)
            G["hgrn_lb_logits"] = s["lb_vjp"](d_lb_row)[0]
        else:
            dzg = _gated_bwd(s["zg"], dy1, "glu", name="b_s5_glu")
            G["s5_w_glu"] = _mm(s["gl"], dzg, ta=True, name="b_s5_glu_dw")[None]
            dgl = _mm(dzg, W["s5_w_glu"][0], tb=True, name="b_s5_glu_dx")
            dyp, du_skip, d_dsk = _s5_act_bwd(s["ypre"], s["h1"], s["dsk"], dgl, name="b_s5_act")
            m_mat, e_mat, f_mat, tab_r, tab_i = s["mats"]
            duf, dm, de, df_, da = _s5_bwd(_to_groups(dyp, T), s["uf"], s["xs"], m_mat, e_mat, f_mat, tab_r, tab_i,
                                           s["nch"], name="s5_bwd")
            da_r = da[:, 0, :C_STATE] + da[:, 0, C_STATE:]
            da_i = da[:, 1, C_STATE:] - da[:, 1, :C_STATE]
            gp = s["s5_vjp"]((dm, de, df_, da_r, da_i))
            for n, gv in zip(("s5_lambda_re", "s5_lambda_im", "s5_log_dt", "s5_b_re", "s5_b_im", "s5_c_re", "s5_c_im"), gp):
                G[n] = gv[None]
            G["s5_d"] = d_dsk
            dh1 = _add2(_from_groups(duf, T), du_skip, name="b_s5_du")
        dx, d_norm[layer][0] = _rms_bwd(s["x0"], vec(nw[layer, 0]), dh1, dx, name=f"bnorm_pre_mix_{tag}")

    for n in ("ffn_w_in", "ffn_w_out", "xattn_wq", "xattn_wkv", "xattn_wo"):
        G[n] = jnp.stack(G[n])
    G["norm_w"] = jnp.stack([jnp.concatenate(d_norm[l], axis=0) for l in range(2)])
    G["mem_norm_w"] = jnp.concatenate(d_memn, axis=0)
    return loss_parts, dx.reshape(B, L, D_MODEL), G


_SMALL = (("norm_w", (2, 6, 1024)), ("mem_norm_w", (2, 1024)), ("hgrn_lb_logits", (3, 512)), ("hgrn_out_norm_w", (1, 512)),
          ("s5_lambda_re", (1, 64, 64)), ("s5_lambda_im", (1, 64, 64)), ("s5_log_dt", (1, 64)),
          ("s5_b_re", (1, 64, 64, 16)), ("s5_b_im", (1, 64, 64, 16)), ("s5_c_re", (1, 64, 16, 64)),
          ("s5_c_im", (1, 64, 16, 64)), ("s5_d", (1, 1024)))

_WEIGHT_ORDER = ('norm_w', 'mem_norm_w', 'ab_w_in', 'ab_w_out', 'hgrn_lb_logits', 'hgrn_out_norm_w', 's5_lambda_re',
                 's5_lambda_im', 's5_log_dt', 's5_b_re', 's5_b_im', 's5_c_re', 's5_c_im', 's5_d', 's5_w_glu', 'xattn_wq',
                 'xattn_wkv', 'xattn_wo', 'ffn_w_in', 'ffn_w_out')


def kernel(x, mem, norm_w, mem_norm_w, ab_w_in, ab_w_out, hgrn_lb_logits, hgrn_out_norm_w, s5_lambda_re, s5_lambda_im, s5_log_dt, s5_b_re, s5_b_im, s5_c_re, s5_c_im, s5_d, s5_w_glu, xattn_wq, xattn_wkv, xattn_wo, ffn_w_in, ffn_w_out, loss_target, m_norm_w, m_mem_norm_w, m_ab_w_in, m_ab_w_out, m_hgrn_lb_logits, m_hgrn_out_norm_w, m_s5_lambda_re, m_s5_lambda_im, m_s5_log_dt, m_s5_b_re, m_s5_b_im, m_s5_c_re, m_s5_c_im, m_s5_d, m_s5_w_glu, m_xattn_wq, m_xattn_wkv, m_xattn_wo, m_ffn_w_in, m_ffn_w_out, v_norm_w, v_mem_norm_w, v_ab_w_in, v_ab_w_out, v_hgrn_lb_logits, v_hgrn_out_norm_w, v_s5_lambda_re, v_s5_lambda_im, v_s5_log_dt, v_s5_b_re, v_s5_b_im, v_s5_c_re, v_s5_c_im, v_s5_d, v_s5_w_glu, v_xattn_wq, v_xattn_wkv, v_xattn_wo, v_ffn_w_in, v_ffn_w_out):
    local = dict(norm_w=norm_w, mem_norm_w=mem_norm_w, ab_w_in=ab_w_in, ab_w_out=ab_w_out, hgrn_lb_logits=hgrn_lb_logits,
                 hgrn_out_norm_w=hgrn_out_norm_w, s5_lambda_re=s5_lambda_re, s5_lambda_im=s5_lambda_im, s5_log_dt=s5_log_dt,
                 s5_b_re=s5_b_re, s5_b_im=s5_b_im, s5_c_re=s5_c_re, s5_c_im=s5_c_im, s5_d=s5_d, s5_w_glu=s5_w_glu,
                 xattn_wq=xattn_wq, xattn_wkv=xattn_wkv, xattn_wo=xattn_wo, ffn_w_in=ffn_w_in, ffn_w_out=ffn_w_out)
    mom_m = dict(zip(_WEIGHT_ORDER, (m_norm_w, m_mem_norm_w, m_ab_w_in, m_ab_w_out, m_hgrn_lb_logits, m_hgrn_out_norm_w, m_s5_lambda_re, m_s5_lambda_im, m_s5_log_dt, m_s5_b_re, m_s5_b_im, m_s5_c_re, m_s5_c_im, m_s5_d, m_s5_w_glu, m_xattn_wq, m_xattn_wkv, m_xattn_wo, m_ffn_w_in, m_ffn_w_out)))
    mom_v = dict(zip(_WEIGHT_ORDER, (v_norm_w, v_mem_norm_w, v_ab_w_in, v_ab_w_out, v_hgrn_lb_logits, v_hgrn_out_norm_w, v_s5_lambda_re, v_s5_lambda_im, v_s5_log_dt, v_s5_b_re, v_s5_b_im, v_s5_c_re, v_s5_c_im, v_s5_d, v_s5_w_glu, v_xattn_wq, v_xattn_wkv, v_xattn_wo, v_ffn_w_in, v_ffn_w_out)))
    dev = 4 * lax.axis_index("x") + 2 * lax.axis_index("y") + lax.axis_index("c")

    shard_shapes = {n: local[n].shape for n, _ in _SHARDED}
    gathered = _all_gather(_pack_local(local, BF16), name="gather_weights")
    W = _unpack_full(gathered, shard_shapes)
    tiny = jnp.concatenate([norm_w.reshape(-1), s5_d.reshape(-1)])
    tiny = jnp.pad(tiny, (0, 16 * LANES - tiny.shape[0])).reshape(16, LANES)
    tiny_all = _all_gather(tiny, name="gather_tiny").reshape(N_DEV, 16 * LANES)
    W["norm_w"] = tiny_all[:, :12 * LANES].reshape(N_DEV, 2, 6, LANES).transpose(1, 2, 0, 3).reshape(2, 6, D_MODEL)
    W["s5_d"] = tiny_all[:, 12 * LANES:13 * LANES].reshape(1, D_MODEL)
    for n in ("mem_norm_w", "hgrn_lb_logits", "hgrn_out_norm_w", "s5_lambda_re", "s5_lambda_im", "s5_log_dt",
              "s5_b_re", "s5_b_im", "s5_c_re", "s5_c_im"):
        W[n] = local[n]

    loss_parts, grad_x, G = _local_step(x, mem, loss_target, W)

    recv = _all_to_all(_pack_for_owners(G, BF16), name="scatter_grads")
    g_local = _unpack_local(_sum_rows(recv, name="sum_grads"), shard_shapes)
    small = jnp.concatenate([G[n].reshape(-1) for n, _ in _SMALL] + [0.5 / D_MODEL * jnp.sum(loss_parts).reshape(1)])
    n_small = small.shape[0]
    small = jnp.pad(small, (0, (-n_small) % (8 * LANES))).reshape(-1, LANES)
    small_sum = _sum_rows(_all_gather(small, name="gather_small"), name="sum_small").reshape(-1)
    g_full, off = {}, 0
    for n, shp in _SMALL:
        size = int(np.prod(shp))
        g_full[n] = small_sum[off:off + size].reshape(shp)
        off += size
    loss = small_sum[off]
    grads = dict(g_local)
    for n, shp in _SMALL:
        if n == "norm_w":
            grads[n] = lax.dynamic_slice_in_dim(g_full[n], dev * LANES, LANES, axis=2)
        elif n == "s5_d":
            grads[n] = lax.dynamic_slice_in_dim(g_full[n], dev * LANES, LANES, axis=1)
        else:
            grads[n] = g_full[n]

    delta, new_m, new_v = {}, {}, {}
    for n in _WEIGHT_ORDER:
        delta[n], new_m[n], new_v[n] = _adamw(local[n], grads[n], mom_m[n], mom_v[n], name=f"adamw_{n}")
    return (loss, grad_x, *[grads[n] for n in _WEIGHT_ORDER], *[delta[n] for n in _WEIGHT_ORDER],
            *[new_m[n] for n in _WEIGHT_ORDER], *[new_v[n] for n in _WEIGHT_ORDER])
```

```python
import functools
import math

import numpy as np
import jax
import jax.numpy as jnp
from jax import lax
from jax.experimental import pallas as pl
from jax.experimental.pallas import tpu as pltpu

F32 = jnp.float32
BF16 = jnp.bfloat16
HI = lax.Precision.HIGHEST

D_MODEL = 1024
NORM_EPS = 1e-6
A_WIDTH = 512
A_HEAD = 128
A_CHUNK = 32
A_SUPER = 256
B_SPAN = 128
B_DILS = (1, 4, 16)
ROPE_THETA = 10000.0
C_GROUPS = 64
C_GROUP = 16
C_STATE = 64
C_TC = 16
C_MIN_NEG_RE = -1e-4
MEM_LEN = 256
X_HEADS = 4
X_HD = 256
D_FF = 2816
N_DEV = 8
LANES = 128

ADAM_LR, ADAM_B1, ADAM_B2, ADAM_EPS, ADAM_WD, ADAM_STEP = 0.001, 0.9, 0.999, 1e-08, 0.01, 10

NEG_BIG = -1e30


def _tile(n, pref):
    p = pref
    while p >= 128:
        if n % p == 0:
            return p
        p //= 2
    return n


def _cparams(*sem):
    return pltpu.CompilerParams(dimension_semantics=sem, vmem_limit_bytes=56 * 1024 * 1024)


def _sigmoid(x):
    return 1.0 / (1.0 + jnp.exp(-x))


def _erf(x):
    ax = jnp.abs(x)
    t = 1.0 / (1.0 + 0.3275911 * ax)
    poly = t * (0.254829592 + t * (-0.284496736 + t * (1.421413741 + t * (-1.453152027 + t * 1.061405429))))
    y = 1.0 - poly * jnp.exp(-ax * ax)
    return jnp.where(x < 0, -y, y)


def _mm(a, b, *, ta=False, tb=False, la=None, lb=None, out_dtype=F32, name, tiles=(1024, 512, 2048)):
    ash = a.shape if la is None else a.shape[1:]
    bsh = b.shape if lb is None else b.shape[1:]
    M, K = (ash[1], ash[0]) if ta else ash
    N = bsh[0] if tb else bsh[1]
    assert (bsh[1] if tb else bsh[0]) == K
    tm, tn = _tile(M, tiles[0]), _tile(N, tiles[1])
    tk = K if K <= 2 * tiles[2] else _tile(K, tiles[2])
    nk = K // tk

    def spec(shape, index, lead):
        if lead is None:
            return pl.BlockSpec(shape, index)
        return pl.BlockSpec((None,) + shape, lambda i, j, k: (lead,) + index(i, j, k))

    a_spec = spec((tk, tm), lambda i, j, k: (k, i), la) if ta else spec((tm, tk), lambda i, j, k: (i, k), la)
    b_spec = spec((tn, tk), lambda i, j, k: (j, k), lb) if tb else spec((tk, tn), lambda i, j, k: (k, j), lb)
    dims = (((0 if ta else 1,), (1 if tb else 0,)), ((), ()))

    def body(a_ref, b_ref, o_ref, *acc):
        part = lax.dot_general(a_ref[...].astype(BF16), b_ref[...].astype(BF16), dims, preferred_element_type=F32)
        if nk == 1:
            o_ref[...] = part.astype(o_ref.dtype)
            return
        acc_ref = acc[0]
        k = pl.program_id(2)

        @pl.when(k == 0)
        def _():
            acc_ref[...] = part

        @pl.when(k > 0)
        def _():
            acc_ref[...] += part

        @pl.when(k == nk - 1)
        def _():
            o_ref[...] = acc_ref[...].astype(o_ref.dtype)

    return pl.pallas_call(
        body, name=name, grid=(M // tm, N // tn, nk),
        in_specs=[a_spec, b_spec], out_specs=pl.BlockSpec((tm, tn), lambda i, j, k: (i, j)),
        out_shape=jax.ShapeDtypeStruct((M, N), out_dtype),
        scratch_shapes=[pltpu.VMEM((tm, tn), F32)] if nk > 1 else [],
        compiler_params=_cparams("parallel", "parallel", "arbitrary"),
    )(a, b)


def _rms_fwd(x, w, res=None, *, name, out_dtype=F32):
    T, C = x.shape
    tm = _tile(T, 512)
    has_res = res is not None

    def body(*refs):
        x_ref, w_ref = refs[0], refs[1]
        o_ref = refs[-1]
        xv = x_ref[...].astype(F32)
        r = lax.rsqrt(jnp.mean(xv * xv, axis=-1, keepdims=True) + NORM_EPS)
        y = xv * r * w_ref[...]
        if has_res:
            y = y + refs[2][...]
        o_ref[...] = y.astype(o_ref.dtype)

    row = pl.BlockSpec((tm, C), lambda i: (i, 0))
    vec = pl.BlockSpec((1, C), lambda i: (0, 0))
    ins = [x, w] + ([res] if has_res else [])
    return pl.pallas_call(
        body, name=name, grid=(T // tm,), in_specs=[row, vec] + ([row] if has_res else []), out_specs=row,
        out_shape=jax.ShapeDtypeStruct((T, C), out_dtype), compiler_params=_cparams("parallel"),
    )(*ins)


def _rms_bwd(x, w, dy, add=None, *, name, out_dtype=F32):
    T, C = x.shape
    tm = _tile(T, 512)
    has_add = add is not None

    def body(*refs):
        x_ref, w_ref, dy_ref = refs[:3]
        dx_ref, dw_ref = refs[-2], refs[-1]
        xv = x_ref[...].astype(F32)
        r = lax.rsqrt(jnp.mean(xv * xv, axis=-1, keepdims=True) + NORM_EPS)
        xh = xv * r
        g = dy_ref[...].astype(F32)
        part = jnp.sum(g * xh, axis=0, keepdims=True)

        @pl.when(pl.program_id(0) == 0)
        def _():
            dw_ref[...] = part

        @pl.when(pl.program_id(0) > 0)
        def _():
            dw_ref[...] += part

        gx = g * w_ref[...]
        dx = r * (gx - xh * jnp.mean(gx * xh, axis=-1, keepdims=True))
        if has_add:
            dx = dx + refs[3][...]
        dx_ref[...] = dx.astype(dx_ref.dtype)

    row = pl.BlockSpec((tm, C), lambda i: (i, 0))
    vec = pl.BlockSpec((1, C), lambda i: (0, 0))
    ins = [x, w, dy] + ([add] if has_add else [])
    return pl.pallas_call(
        body, name=name, grid=(T // tm,), in_specs=[row, vec, row] + ([row] if has_add else []),
        out_specs=[row, vec],
        out_shape=[jax.ShapeDtypeStruct((T, C), out_dtype), jax.ShapeDtypeStruct((1, C), F32)],
        compiler_params=_cparams("arbitrary"),
    )(*ins)


def _gated_fwd(z, kind, *, name):
    T, W2 = z.shape
    W = W2 // 2
    tm, tn = _tile(T, 512), _tile(W, 512)
    nb = W // tn

    def body(a_ref, b_ref, o_ref):
        a, b = a_ref[...].astype(F32), b_ref[...].astype(F32)
        if kind == "swiglu":
            o_ref[...] = (a * _sigmoid(a) * b).astype(o_ref.dtype)
        else:
            o_ref[...] = (a * _sigmoid(b)).astype(o_ref.dtype)

    return pl.pallas_call(
        body, name=name, grid=(T // tm, nb),
        in_specs=[pl.BlockSpec((tm, tn), lambda i, j: (i, j)), pl.BlockSpec((tm, tn), lambda i, j: (i, j + nb))],
        out_specs=pl.BlockSpec((tm, tn), lambda i, j: (i, j)),
        out_shape=jax.ShapeDtypeStruct((T, W), BF16), compiler_params=_cparams("parallel", "parallel"),
    )(z, z)


def _gated_bwd(z, dout, kind, *, name):
    T, W2 = z.shape
    W = W2 // 2
    tm, tn = _tile(T, 512), _tile(W, 512)
    nb = W // tn

    def body(a_ref, b_ref, d_ref, o_ref):
        a, b, d = a_ref[...].astype(F32), b_ref[...].astype(F32), d_ref[...].astype(F32)
        first = pl.program_id(1) < nb
        if kind == "swiglu":
            s = _sigmoid(a)
            da = d * b * (s * (1.0 + a * (1.0 - s)))
            db = d * a * s
        else:
            s = _sigmoid(b)
            da = d * s
            db = d * a * s * (1.0 - s)
        o_ref[...] = jnp.where(first, da, db).astype(o_ref.dtype)

    return pl.pallas_call(
        body, name=name, grid=(T // tm, 2 * nb),
        in_specs=[pl.BlockSpec((tm, tn), lambda i, j: (i, j % nb)), pl.BlockSpec((tm, tn), lambda i, j: (i, j % nb + nb)),
                  pl.BlockSpec((tm, tn), lambda i, j: (i, j % nb))],
        out_specs=pl.BlockSpec((tm, tn), lambda i, j: (i, j)),
        out_shape=jax.ShapeDtypeStruct((T, W2), BF16), compiler_params=_cparams("parallel", "parallel"),
    )(z, z, dout)


_NT = (((1,), (1,)), ((), ()))
_TN = (((0,), (0,)), ((), ()))


def _dot(a, b, dims=None, precision=None):
    if dims is None:
        return jnp.dot(a, b, preferred_element_type=F32, precision=precision)
    return lax.dot_general(a, b, dims, preferred_element_type=F32, precision=precision)


def _xattn_fwd(q, kv, B, L, *, name):
    T = q.shape[0]
    tq = 256
    nq = L // tq
    scale = X_HD ** -0.5

    def body(q_ref, k_ref, v_ref, o_ref):
        for h in range(X_HEADS):
            sl = slice(h * X_HD, (h + 1) * X_HD)
            qh, kh, vh = q_ref[:, sl].astype(BF16), k_ref[:, sl].astype(BF16), v_ref[:, sl].astype(BF16)
            s = _dot(qh, kh, _NT) * scale
            m = jnp.max(s, axis=-1, keepdims=True)
            p = jnp.exp(s - m)
            l = jnp.sum(p, axis=-1, keepdims=True)
            o_ref[:, sl] = (_dot(p.astype(BF16), vh) / l).astype(BF16)

    return pl.pallas_call(
        body, name=name, grid=(B, nq),
        in_specs=[pl.BlockSpec((tq, D_MODEL), lambda b, i: (b * nq + i, 0)),
                  pl.BlockSpec((MEM_LEN, D_MODEL), lambda b, i: (b, 0)),
                  pl.BlockSpec((MEM_LEN, D_MODEL), lambda b, i: (b, 1))],
        out_specs=pl.BlockSpec((tq, D_MODEL), lambda b, i: (b * nq + i, 0)),
        out_shape=jax.ShapeDtypeStruct((T, D_MODEL), BF16), compiler_params=_cparams("parallel", "parallel"),
    )(q, kv, kv)


def _xattn_bwd(q, kv, do, B, L, *, name):
    T = q.shape[0]
    tq = 256
    nq = L // tq
    scale = X_HD ** -0.5

    def body(q_ref, k_ref, v_ref, do_ref, dq_ref, dkv_ref):
        @pl.when(pl.program_id(1) == 0)
        def _():
            dkv_ref[...] = jnp.zeros_like(dkv_ref)

        for h in range(X_HEADS):
            sl = slice(h * X_HD, (h + 1) * X_HD)
            slv = slice(D_MODEL + h * X_HD, D_MODEL + (h + 1) * X_HD)
            qh, kh, vh = q_ref[:, sl].astype(BF16), k_ref[:, sl].astype(BF16), v_ref[:, sl].astype(BF16)
            doh = do_ref[:, sl].astype(BF16)
            s = _dot(qh, kh, _NT) * scale
            m = jnp.max(s, axis=-1, keepdims=True)
            e = jnp.exp(s - m)
            p = e / jnp.sum(e, axis=-1, keepdims=True)
            dkv_ref[:, slv] += _dot(p.astype(BF16), doh, _TN)
            dp = _dot(doh, vh, _NT)
            ds = p * (dp - jnp.sum(dp * p, axis=-1, keepdims=True)) * scale
            dsb = ds.astype(BF16)
            dq_ref[:, sl] = _dot(dsb, kh).astype(BF16)
            dkv_ref[:, sl] += _dot(dsb, qh, _TN)

    return pl.pallas_call(
        body, name=name, grid=(B, nq),
        in_specs=[pl.BlockSpec((tq, D_MODEL), lambda b, i: (b * nq + i, 0)),
                  pl.BlockSpec((MEM_LEN, D_MODEL), lambda b, i: (b, 0)),
                  pl.BlockSpec((MEM_LEN, D_MODEL), lambda b, i: (b, 1)),
                  pl.BlockSpec((tq, D_MODEL), lambda b, i: (b * nq + i, 0))],
        out_specs=[pl.BlockSpec((tq, D_MODEL), lambda b, i: (b * nq + i, 0)),
                   pl.BlockSpec((MEM_LEN, 2 * D_MODEL), lambda b, i: (b, 0))],
        out_shape=[jax.ShapeDtypeStruct((T, D_MODEL), BF16), jax.ShapeDtypeStruct((B * MEM_LEN, 2 * D_MODEL), F32)],
        compiler_params=_cparams("parallel", "arbitrary"),
    )(q, kv, kv, do)


def _chunk_masks():
    row = lax.broadcasted_iota(jnp.int32, (A_SUPER, A_SUPER), 0)
    col = lax.broadcasted_iota(jnp.int32, (A_SUPER, A_SUPER), 1)
    same = jnp.right_shift(row, 5) == jnp.right_shift(col, 5)
    return same, same & (col <= row), same & (col >= row)


def _hgrn_gates(fa, lb):
    sig = _sigmoid(fa)
    f = lb + (1.0 - lb) * sig
    return sig, f, jnp.log(f), 1.0 - f


def _hgrn_fwd(z, lb, onw, B, L, *, name):
    T = B * L
    ns = L // A_SUPER
    nch = A_SUPER // A_CHUNK

    def body(q_ref, f_ref, v_ref, g_ref, lb_ref, w_ref, oa_ref, o_ref, s_ref, st_ref):
        @pl.when(pl.program_id(2) == 0)
        def _():
            st_ref[...] = jnp.zeros_like(st_ref)

        s_ref[0] = st_ref[...]
        same, tril, _ = _chunk_masks()
        q, v = q_ref[...], v_ref[...]
        _, _, lf, k = _hgrn_gates(f_ref[...], lb_ref[...])
        bcs = _dot(tril.astype(F32), lf, precision=HI)
        bl = _dot(same.astype(F32), lf, precision=HI)
        qd = (q * jnp.exp(bcs)).astype(BF16)
        ki = (k * jnp.exp(-bcs)).astype(BF16)
        ke = (k * jnp.exp(bl - bcs)).astype(BF16)
        dec = jnp.exp(bl)
        vb = v.astype(BF16)
        a = jnp.where(tril, _dot(qd, ki, _NT), 0.0)
        o_ref[...] = _dot(a.astype(BF16), vb)
        st = st_ref[...]
        for c in range(nch):
            rs = slice(c * A_CHUNK, (c + 1) * A_CHUNK)
            o_ref[rs, :] += _dot(qd[rs], st.astype(BF16), _NT)
            st = st * dec[c * A_CHUNK:c * A_CHUNK + 1, :] + _dot(vb[rs], ke[rs], _TN)
        st_ref[...] = st
        o = o_ref[...]
        r = lax.rsqrt(jnp.mean(o * o, axis=-1, keepdims=True) + NORM_EPS)
        g = g_ref[...]
        oa_ref[...] = o * r * w_ref[...] * (g * _sigmoid(g))

    def zspec(off):
        return pl.BlockSpec((A_SUPER, A_HEAD), lambda b, h, n: (b * ns + n, off + h))

    hvec = pl.BlockSpec((1, A_HEAD), lambda b, h, n: (0, h))
    ospec = pl.BlockSpec((A_SUPER, A_HEAD), lambda b, h, n: (b * ns + n, h))
    return pl.pallas_call(
        body, name=name, grid=(B, 4, ns),
        in_specs=[zspec(0), zspec(4), zspec(8), zspec(12), hvec, hvec],
        out_specs=[ospec, ospec, pl.BlockSpec((1, A_HEAD, A_HEAD), lambda b, h, n: ((b * 4 + h) * ns + n, 0, 0))],
        out_shape=[jax.ShapeDtypeStruct((T, A_WIDTH), F32), jax.ShapeDtypeStruct((T, A_WIDTH), F32),
                   jax.ShapeDtypeStruct((B * 4 * ns, A_HEAD, A_HEAD), F32)],
        scratch_shapes=[pltpu.VMEM((A_HEAD, A_HEAD), F32)],
        compiler_params=_cparams("parallel", "parallel", "arbitrary"),
    )(z, z, z, z, lb, onw)


def _hgrn_bwd(z, lb, onw, o_raw, s_start, doa, B, L, *, name):
    T = B * L
    ns = L // A_SUPER
    nch = A_SUPER // A_CHUNK

    def body(q_ref, f_ref, v_ref, g_ref, lb_ref, w_ref, o_ref, s_ref, doa_ref,
             dq_ref, df_ref, dv_ref, dg_ref, dw_ref, dlb_ref, dst_ref, sc_ref, dqd_ref, dke_ref, dblx_ref):
        @pl.when(pl.program_id(2) == 0)
        def _():
            dst_ref[...] = jnp.zeros_like(dst_ref)
            dw_ref[...] = jnp.zeros_like(dw_ref)
            dlb_ref[...] = jnp.zeros_like(dlb_ref)

        same, tril, triu = _chunk_masks()
        q, v, g, lb, w = q_ref[...], v_ref[...], g_ref[...], lb_ref[...], w_ref[...]
        sig, f, lf, k = _hgrn_gates(f_ref[...], lb)
        bcs = _dot(tril.astype(F32), lf, precision=HI)
        bl = _dot(same.astype(F32), lf, precision=HI)
        eb, enb, eeb = jnp.exp(bcs), jnp.exp(-bcs), jnp.exp(bl - bcs)
        qd, ki, ke = q * eb, k * enb, k * eeb
        qdb, kib, keb, vb = qd.astype(BF16), ki.astype(BF16), ke.astype(BF16), v.astype(BF16)
        dec = jnp.exp(bl)
        o = o_ref[...]
        r = lax.rsqrt(jnp.mean(o * o, axis=-1, keepdims=True) + NORM_EPS)
        on = o * r
        sg = _sigmoid(g)
        silu_g = g * sg
        doa = doa_ref[...]
        dg_ref[...] = doa * on * w * (sg * (1.0 + g * (1.0 - sg)))
        dw_ref[0] += jnp.sum(doa * on * silu_g, axis=0, keepdims=True)
        don = doa * w * silu_g
        do = r * (don - on * jnp.mean(don * on, axis=-1, keepdims=True))
        dob = do.astype(BF16)
        a = jnp.where(tril, _dot(qdb, kib, _NT), 0.0).astype(BF16)
        da = jnp.where(tril, _dot(dob, vb, _NT), 0.0).astype(BF16)
        dv_ref[...] = _dot(a, dob, _TN)
        dqd_ref[...] = _dot(da, kib)
        dki = _dot(da, qdb, _TN)
        st = s_ref[0]
        for c in range(nch):
            rs = slice(c * A_CHUNK, (c + 1) * A_CHUNK)
            sc_ref[c] = st
            st = st * dec[c * A_CHUNK:c * A_CHUNK + 1, :] + _dot(vb[rs], keb[rs], _TN)
        dst = dst_ref[...]
        for c in reversed(range(nch)):
            rs = slice(c * A_CHUNK, (c + 1) * A_CHUNK)
            dec_c = dec[c * A_CHUNK:c * A_CHUNK + 1, :]
            dstb = dst.astype(BF16)
            stc = sc_ref[c]
            dv_ref[rs, :] += _dot(keb[rs], dstb, _NT)
            dke_ref[rs, :] = _dot(vb[rs], dstb)
            ddec = jnp.sum(dst * stc, axis=0, keepdims=True)
            dqd_ref[rs, :] += _dot(dob[rs], stc.astype(BF16))
            dblx_ref[rs, :] = jnp.broadcast_to(ddec * dec_c, (A_CHUNK, A_HEAD))
            dst = dst * dec_c + _dot(dob[rs], qdb[rs], _TN)
        dst_ref[...] = dst
        dqd, dke = dqd_ref[...], dke_ref[...]
        dq_ref[...] = dqd * eb
        keke = dke * ke
        db = dqd * qd - dki * ki - keke
        dbl = _dot(same.astype(F32), keke, precision=HI) + dblx_ref[...]
        dk = dki * enb + dke * eeb
        dlf = _dot(triu.astype(F32), db, precision=HI) + dbl
        dff = dlf / f - dk
        df_ref[...] = dff * (1.0 - lb) * sig * (1.0 - sig)
        dlb_ref[0] += jnp.sum(dff * (1.0 - sig), axis=0, keepdims=True)

    def rev(n):
        return ns - 1 - n

    def zspec(off):
        return pl.BlockSpec((A_SUPER, A_HEAD), lambda b, h, n: (b * ns + rev(n), off + h))

    hvec = pl.BlockSpec((1, A_HEAD), lambda b, h, n: (0, h))
    ospec = pl.BlockSpec((A_SUPER, A_HEAD), lambda b, h, n: (b * ns + rev(n), h))
    acc = pl.BlockSpec((1, 1, A_HEAD), lambda b, h, n: (b * 4 + h, 0, 0))
    big = jax.ShapeDtypeStruct((T, A_WIDTH), F32)
    small = jax.ShapeDtypeStruct((B * 4, 1, A_HEAD), F32)
    return pl.pallas_call(
        body, name=name, grid=(B, 4, ns),
        in_specs=[zspec(0), zspec(4), zspec(8), zspec(12), hvec, hvec, ospec,
                  pl.BlockSpec((1, A_HEAD, A_HEAD), lambda b, h, n: ((b * 4 + h) * ns + rev(n), 0, 0)), ospec],
        out_specs=[ospec, ospec, ospec, ospec, acc, acc],
        out_shape=[big, big, big, big, small, small],
        scratch_shapes=[pltpu.VMEM((A_HEAD, A_HEAD), F32), pltpu.VMEM((nch, A_HEAD, A_HEAD), F32),
                        pltpu.VMEM((A_SUPER, A_HEAD), F32), pltpu.VMEM((A_SUPER, A_HEAD), F32),
                        pltpu.VMEM((A_SUPER, A_HEAD), F32)],
        compiler_params=_cparams("parallel", "parallel", "arbitrary"),
    )(z, z, z, z, lb, onw, o_raw, s_start, doa)


def _rope_tables(L):
    half = A_HEAD // 2
    inv_freq = ROPE_THETA ** (-jnp.arange(half, dtype=F32) / half)
    ang = jnp.arange(L, dtype=F32)[:, None] * inv_freq[None, :]
    cos, sin = jnp.cos(ang), jnp.sin(ang)
    return jnp.concatenate([cos, cos], axis=-1), jnp.concatenate([-sin, sin], axis=-1)


def _rope_fwd(z, cos2, sin2, col_off, B, L, *, name):
    T = B * L
    tm = 256
    nl = L // tm

    def body(x_ref, c_ref, s_ref, o_ref):
        x = x_ref[...]
        o_ref[...] = x * c_ref[...] + pltpu.roll(x, A_HEAD // 2, 1) * s_ref[...]

    tab = pl.BlockSpec((tm, A_HEAD), lambda i, h: (i % nl, 0))
    return pl.pallas_call(
        body, name=name, grid=(T // tm, 4),
        in_specs=[pl.BlockSpec((tm, A_HEAD), lambda i, h: (i, col_off + h)), tab, tab],
        out_specs=pl.BlockSpec((tm, A_HEAD), lambda i, h: (i, h)),
        out_shape=jax.ShapeDtypeStruct((T, 512), F32), compiler_params=_cparams("parallel", "parallel"),
    )(z, cos2, sin2)


def _sum3(d1, d2, d3, cos2, sin2, rotate, B, L, *, name):
    T = B * L
    tm = 256
    nl = L // tm

    def body(a_ref, b_ref, c_ref, cs_ref, sn_ref, o_ref):
        d = a_ref[...] + b_ref[...] + c_ref[...]
        if rotate:
            d = d * cs_ref[...] - pltpu.roll(d, A_HEAD // 2, 1) * sn_ref[...]
        o_ref[...] = d

    blk = pl.BlockSpec((tm, A_HEAD), lambda i, h: (i, h))
    tab = pl.BlockSpec((tm, A_HEAD), lambda i, h: (i % nl, 0))
    return pl.pallas_call(
        body, name=name, grid=(T // tm, 4), in_specs=[blk, blk, blk, tab, tab], out_specs=blk,
        out_shape=jax.ShapeDtypeStruct((T, 512), F32), compiler_params=_cparams("parallel", "parallel"),
    )(d1, d2, d3, cos2, sin2)


def _band_masks():
    i = lax.broadcasted_iota(jnp.int32, (B_SPAN, B_SPAN), 0)
    j = lax.broadcasted_iota(jnp.int32, (B_SPAN, B_SPAN), 1)
    return i <= j, j <= i


def _dil_fwd(qr, kr, z, dil, B, L, *, name):
    T = B * L
    Ls = L // dil
    nb = Ls // B_SPAN
    scale = A_HEAD ** -0.5
    qv = qr.reshape(B * Ls, dil * 512)
    kv = kr.reshape(B * Ls, dil * 512)
    zv = z.reshape(B * Ls, dil * 3584)

    def body(q_ref, kp_ref, kc_ref, vp_ref, vc_ref, o_ref, l_ref):
        has_prev = pl.program_id(2) > 0
        mp, mc = _band_masks()
        mp = mp & has_prev
        for h in range(4):
            sl = slice(h * A_HEAD, (h + 1) * A_HEAD)
            qh = q_ref[:, sl].astype(BF16)
            sp = jnp.where(mp, _dot(qh, kp_ref[:, sl].astype(BF16), _NT) * scale, NEG_BIG)
            sc = jnp.where(mc, _dot(qh, kc_ref[:, sl].astype(BF16), _NT) * scale, NEG_BIG)
            m = jnp.maximum(jnp.max(sp, axis=-1, keepdims=True), jnp.max(sc, axis=-1, keepdims=True))
            pp, pc = jnp.exp(sp - m), jnp.exp(sc - m)
            l = jnp.sum(pp, axis=-1, keepdims=True) + jnp.sum(pc, axis=-1, keepdims=True)
            o = _dot(pp.astype(BF16), vp_ref[:, sl].astype(BF16)) + _dot(pc.astype(BF16), vc_ref[:, sl].astype(BF16))
            o_ref[:, sl] = o / l
            l_ref[:, sl] = jnp.broadcast_to(m + jnp.log(l), (B_SPAN, A_HEAD))

    def cur(b, r, n):
        return b * nb + n

    def prev(b, r, n):
        return b * nb + jnp.maximum(n - 1, 0)

    blk = (B_SPAN, 512)
    return pl.pallas_call(
        body, name=name, grid=(B, dil, nb),
        in_specs=[pl.BlockSpec(blk, lambda b, r, n: (cur(b, r, n), r)),
                  pl.BlockSpec(blk, lambda b, r, n: (prev(b, r, n), r)),
                  pl.BlockSpec(blk, lambda b, r, n: (cur(b, r, n), r)),
                  pl.BlockSpec(blk, lambda b, r, n: (prev(b, r, n), r * 7 + 6)),
                  pl.BlockSpec(blk, lambda b, r, n: (cur(b, r, n), r * 7 + 6))],
        out_specs=[pl.BlockSpec(blk, lambda b, r, n: (cur(b, r, n), r))] * 2,
        out_shape=[jax.ShapeDtypeStruct((B * Ls, dil * 512), F32)] * 2,
        compiler_params=_cparams("parallel", "parallel", "parallel"),
    )(qv, kv, kv, zv, zv)


def _dil_combine(os_, ls_, *, name):
    T = os_[0].shape[0]
    tm = 256

    def body(o1, o2, o3, l1, l2, l3, ob_ref, lse_ref):
        a1, a2, a3 = l1[...], l2[...], l3[...]
        m = jnp.maximum(jnp.maximum(a1, a2), a3)
        e1, e2, e3 = jnp.exp(a1 - m), jnp.exp(a2 - m), jnp.exp(a3 - m)
        den = e1 + e2 + e3
        ob_ref[...] = (e1 * o1[...] + e2 * o2[...] + e3 * o3[...]) / den
        lse_ref[...] = m + jnp.log(den)

    blk = pl.BlockSpec((tm, 512), lambda i: (i, 0))
    return pl.pallas_call(
        body, name=name, grid=(T // tm,), in_specs=[blk] * 6, out_specs=[blk, blk],
        out_shape=[jax.ShapeDtypeStruct((T, 512), F32)] * 2, compiler_params=_cparams("parallel"),
    )(*[o.reshape(T, 512) for o in os_], *[l.reshape(T, 512) for l in ls_])


def _dil_bwd_q(qr, kr, z, dout, out, lse, dil, B, L, *, name):
    T = B * L
    Ls = L // dil
    nb = Ls // B_SPAN
    scale = A_HEAD ** -0.5
    view = lambda t: t.reshape(B * Ls, dil * 512)
    zv = z.reshape(B * Ls, dil * 3584)

    def body(q_ref, kp_ref, kc_ref, vp_ref, vc_ref, do_ref, out_ref, lse_ref, dq_ref):
        has_prev = pl.program_id(2) > 0
        mp, mc = _band_masks()
        mp = mp & has_prev
        for h in range(4):
            sl = slice(h * A_HEAD, (h + 1) * A_HEAD)
            qh = q_ref[:, sl].astype(BF16)
            kp, kc = kp_ref[:, sl].astype(BF16), kc_ref[:, sl].astype(BF16)
            do = do_ref[:, sl]
            delta = jnp.sum(do * out_ref[:, sl], axis=-1, keepdims=True)
            dob = do.astype(BF16)
            lse_h = lse_ref[:, sl]
            pp = jnp.where(mp, jnp.exp(_dot(qh, kp, _NT) * scale - lse_h), 0.0)
            pc = jnp.where(mc, jnp.exp(_dot(qh, kc, _NT) * scale - lse_h), 0.0)
            dsp = pp * (_dot(dob, vp_ref[:, sl].astype(BF16), _NT) - delta) * scale
            dsc = pc * (_dot(dob, vc_ref[:, sl].astype(BF16), _NT) - delta) * scale
            dq_ref[:, sl] = _dot(dsp.astype(BF16), kp) + _dot(dsc.astype(BF16), kc)

    def cur(b, r, n):
        return b * nb + n

    def prev(b, r, n):
        return b * nb + jnp.maximum(n - 1, 0)

    blk = (B_SPAN, 512)
    cspec = pl.BlockSpec(blk, lambda b, r, n: (cur(b, r, n), r))
    return pl.pallas_call(
        body, name=name, grid=(B, dil, nb),
        in_specs=[cspec, pl.BlockSpec(blk, lambda b, r, n: (prev(b, r, n), r)), cspec,
                  pl.BlockSpec(blk, lambda b, r, n: (prev(b, r, n), r * 7 + 6)),
                  pl.BlockSpec(blk, lambda b, r, n: (cur(b, r, n), r * 7 + 6)), cspec, cspec, cspec],
        out_specs=cspec,
        out_shape=jax.ShapeDtypeStruct((B * Ls, dil * 512), F32),
        compiler_params=_cparams("parallel", "parallel", "parallel"),
    )(view(qr), view(kr), view(kr), zv, zv, view(dout), view(out), view(lse)).reshape(T, 512)


def _dil_bwd_kv(qr, kr, z, dout, out, lse, dil, B, L, *, name):
    T = B * L
    Ls = L // dil
    nb = Ls // B_SPAN
    scale = A_HEAD ** -0.5
    view = lambda t: t.reshape(B * Ls, dil * 512)
    zv = z.reshape(B * Ls, dil * 3584)

    def body(k_ref, v_ref, qa_ref, qb_ref, doa_ref, dob_ref, outa_ref, outb_ref, lsa_ref, lsb_ref, dk_ref, dv_ref):
        has_next = pl.program_id(2) < nb - 1
        mp, mc = _band_masks()
        mp = mp & has_next
        for h in range(4):
            sl = slice(h * A_HEAD, (h + 1) * A_HEAD)
            kh, vh = k_ref[:, sl].astype(BF16), v_ref[:, sl].astype(BF16)
            dk = jnp.zeros((B_SPAN, A_HEAD), F32)
            dv = jnp.zeros((B_SPAN, A_HEAD), F32)
            for q_ref, do_ref, out_ref, lse_ref, mask in ((qa_ref, doa_ref, outa_ref, lsa_ref, mc),
                                                          (qb_ref, dob_ref, outb_ref, lsb_ref, mp)):
                qh = q_ref[:, sl].astype(BF16)
                do = do_ref[:, sl]
                delta = jnp.sum(do * out_ref[:, sl], axis=-1, keepdims=True)
                dob = do.astype(BF16)
                p = jnp.where(mask, jnp.exp(_dot(qh, kh, _NT) * scale - lse_ref[:, sl]), 0.0)
                dv = dv + _dot(p.astype(BF16), dob, _TN)
                ds = p * (_dot(dob, vh, _NT) - delta) * scale
                dk = dk + _dot(ds.astype(BF16), qh, _TN)
            dk_ref[:, sl] = dk
            dv_ref[:, sl] = dv

    def cur(b, r, n):
        return b * nb + n

    def nxt(b, r, n):
        return b * nb + jnp.minimum(n + 1, nb - 1)

    blk = (B_SPAN, 512)
    cspec = pl.BlockSpec(blk, lambda b, r, n: (cur(b, r, n), r))
    nspec = pl.BlockSpec(blk, lambda b, r, n: (nxt(b, r, n), r))
    dk, dv = pl.pallas_call(
        body, name=name, grid=(B, dil, nb),
        in_specs=[cspec, pl.BlockSpec(blk, lambda b, r, n: (cur(b, r, n), r * 7 + 6)),
                  cspec, nspec, cspec, nspec, cspec, nspec, cspec, nspec],
        out_specs=[cspec, cspec],
        out_shape=[jax.ShapeDtypeStruct((B * Ls, dil * 512), F32)] * 2,
        compiler_params=_cparams("parallel", "parallel", "parallel"),
    )(view(kr), zv, view(qr), view(qr), view(dout), view(dout), view(out), view(out), view(lse), view(lse))
    return dk.reshape(T, 512), dv.reshape(T, 512)


def _s5_build(lam_re, lam_im, log_dt, b_re, b_im, c_re, c_im):
    G, P, TC = C_GROUPS, C_STATE, C_TC
    lr = jnp.minimum(lam_re, C_MIN_NEG_RE)
    li = lam_im
    dt = jnp.exp(log_dt)[:, None]
    mag = jnp.exp(dt * lr)
    ar, ai = mag * jnp.cos(dt * li), mag * jnp.sin(dt * li)
    den = lr * lr + li * li
    zr = ((ar - 1.0) * lr + ai * li) / den
    zi = (ai * lr - (ar - 1.0) * li) / den
    bbr = zr[..., None] * b_re - zi[..., None] * b_im
    bbi = zr[..., None] * b_im + zi[..., None] * b_re
    ks = jnp.arange(TC + 1, dtype=F32)[:, None, None]
    pmag = jnp.exp(ks * (dt * lr)[None])
    pr, pi = pmag * jnp.cos(ks * (dt * li)[None]), pmag * jnp.sin(ks * (dt * li)[None])
    car = c_re[None] * pr[:, :, None, :] - c_im[None] * pi[:, :, None, :]
    cai = c_re[None] * pi[:, :, None, :] + c_im[None] * pr[:, :, None, :]
    kern = (jnp.einsum('lgop,gpc->lgco', car[:TC], bbr, precision=HI)
            - jnp.einsum('lgop,gpc->lgco', cai[:TC], bbi, precision=HI))
    s_idx = jnp.arange(TC)[:, None]
    t_idx = jnp.arange(TC)[None, :]
    lag = t_idx - s_idx
    ksel = jnp.where((lag >= 0)[:, :, None, None, None], kern[jnp.clip(lag, 0, TC - 1)], 0.0)
    m_mat = ksel.transpose(2, 0, 3, 1, 4).reshape(G, TC * C_GROUP, TC * C_GROUP)
    pr_e, pi_e = pr[TC - 1 - jnp.arange(TC)], pi[TC - 1 - jnp.arange(TC)]
    er = pr_e[:, :, :, None] * bbr[None] - pi_e[:, :, :, None] * bbi[None]
    ei = pr_e[:, :, :, None] * bbi[None] + pi_e[:, :, :, None] * bbr[None]
    e_mat = jnp.concatenate([er.transpose(1, 0, 3, 2), ei.transpose(1, 0, 3, 2)], axis=-1).reshape(G, TC * C_GROUP, 2 * P)
    fr = car[1:].transpose(1, 3, 0, 2)
    fi = -cai[1:].transpose(1, 3, 0, 2)
    f_mat = jnp.concatenate([fr, fi], axis=1).reshape(G, 2 * P, TC * C_GROUP)
    return m_mat, e_mat, f_mat, pr[TC], pi[TC]


def _s5_scan_tables(lam_re, lam_im, log_dt, nsteps):
    lr = jnp.minimum(lam_re, C_MIN_NEG_RE)
    dt = jnp.exp(log_dt)[:, None]
    ks = (C_TC * 2.0 ** jnp.arange(8, dtype=F32))[None, :, None]
    keep = (jnp.arange(8) < nsteps)[None, :, None]
    pmag = jnp.exp(ks * (dt * lr)[:, None, :])
    ang = ks * (dt * lam_im)[:, None, :]
    pr = jnp.where(keep, pmag * jnp.cos(ang), 0.0)
    pi = jnp.where(keep, pmag * jnp.sin(ang), 0.0)
    return jnp.concatenate([pr, pr], axis=-1), jnp.concatenate([-pi, pi], axis=-1)


def _s5_fwd(uf, m_mat, e_mat, f_mat, tab_r, tab_i, nch, *, name):
    G, R, _ = uf.shape
    nsteps = int(math.log2(nch))

    def body(u_ref, m_ref, e_ref, f_ref, tr_ref, ti_ref, y_ref, xs_ref):
        u = u_ref[0].astype(BF16)
        x = _dot(u, e_ref[0].astype(BF16))
        pos = jnp.bitwise_and(lax.broadcasted_iota(jnp.int32, (R, 2 * C_STATE), 0), nch - 1)
        for k in range(nsteps):
            s = 1 << k
            sh = pltpu.roll(x, s, 0)
            upd = tr_ref[0, k:k + 1, :] * sh + ti_ref[0, k:k + 1, :] * pltpu.roll(sh, C_STATE, 1)
            x = x + jnp.where(pos >= s, upd, 0.0)
        xs = jnp.where(pos >= 1, pltpu.roll(x, 1, 0), 0.0)
        xs_ref[0] = xs
        y_ref[0] = _dot(u, m_ref[0].astype(BF16)) + _dot(xs.astype(BF16), f_ref[0].astype(BF16))

    def g3(shape):
        return pl.BlockSpec((1,) + shape, lambda g: (g, 0, 0))

    return pl.pallas_call(
        body, name=name, grid=(G,),
        in_specs=[g3((R, 256)), g3((256, 256)), g3((256, 128)), g3((128, 256)), g3((8, 128)), g3((8, 128))],
        out_specs=[g3((R, 256)), g3((R, 128))],
        out_shape=[jax.ShapeDtypeStruct((G, R, 256), F32), jax.ShapeDtypeStruct((G, R, 128), F32)],
        compiler_params=_cparams("parallel"),
    )(uf, m_mat, e_mat, f_mat, tab_r, tab_i)


def _s5_bwd(dy, uf, xs, m_mat, e_mat, f_mat, tab_r, tab_i, nch, *, name):
    G, R, _ = uf.shape
    nsteps = int(math.log2(nch))

    def body(dy_ref, u_ref, xs_ref, m_ref, e_ref, f_ref, tr_ref, ti_ref, du_ref, dm_ref, de_ref, df_ref, da_ref):
        dyb = dy_ref[0].astype(BF16)
        u = u_ref[0].astype(BF16)
        xs = xs_ref[0]
        dm_ref[0] = _dot(u, dyb, _TN)
        df_ref[0] = _dot(xs.astype(BF16), dyb, _TN)
        gx = _dot(dyb, f_ref[0].astype(BF16), _NT)
        pos = jnp.bitwise_and(lax.broadcasted_iota(jnp.int32, (R, 2 * C_STATE), 0), nch - 1)
        for k in range(nsteps):
            s = 1 << k
            sh = pltpu.roll(gx, R - s, 0)
            upd = tr_ref[0, k:k + 1, :] * sh - ti_ref[0, k:k + 1, :] * pltpu.roll(sh, C_STATE, 1)
            gx = gx + jnp.where(pos + s < nch, upd, 0.0)
        de_in = jnp.where(pos + 1 < nch, pltpu.roll(gx, R - 1, 0), 0.0)
        deb = de_in.astype(BF16)
        de_ref[0] = _dot(u, deb, _TN)
        du_ref[0] = _dot(dyb, m_ref[0].astype(BF16), _NT) + _dot(deb, e_ref[0].astype(BF16), _NT)
        da_ref[0] = jnp.zeros((8, 128), F32)
        da_ref[0, 0:1, :] = jnp.sum(de_in * xs, axis=0, keepdims=True)
        da_ref[0, 1:2, :] = jnp.sum(de_in * pltpu.roll(xs, C_STATE, 1), axis=0, keepdims=True)

    def g3(shape):
        return pl.BlockSpec((1,) + shape, lambda g: (g, 0, 0))

    return pl.pallas_call(
        body, name=name, grid=(G,),
        in_specs=[g3((R, 256)), g3((R, 256)), g3((R, 128)), g3((256, 256)), g3((256, 128)), g3((128, 256)),
                  g3((8, 128)), g3((8, 128))],
        out_specs=[g3((R, 256)), g3((256, 256)), g3((256, 128)), g3((128, 256)), g3((8, 128))],
        out_shape=[jax.ShapeDtypeStruct((G, R, 256), F32), jax.ShapeDtypeStruct((G, 256, 256), F32),
                   jax.ShapeDtypeStruct((G, 256, 128), F32), jax.ShapeDtypeStruct((G, 128, 256), F32),
                   jax.ShapeDtypeStruct((G, 8, 128), F32)],
        compiler_params=_cparams("parallel"),
    )(dy, uf, xs, m_mat, e_mat, f_mat, tab_r, tab_i)


def _s5_act_fwd(ys, u, dsk, *, name):
    T, C = u.shape
    tm = 256

    def body(ys_ref, u_ref, d_ref, gl_ref, y_ref):
        y = ys_ref[...] + d_ref[...] * u_ref[...]
        y_ref[...] = y
        gl_ref[...] = (0.5 * y * (1.0 + _erf(y * (2.0 ** -0.5)))).astype(BF16)

    row = pl.BlockSpec((tm, C), lambda i: (i, 0))
    vec = pl.BlockSpec((1, C), lambda i: (0, 0))
    return pl.pallas_call(
        body, name=name, grid=(T // tm,), in_specs=[row, row, vec], out_specs=[row, row],
        out_shape=[jax.ShapeDtypeStruct((T, C), BF16), jax.ShapeDtypeStruct((T, C), F32)],
        compiler_params=_cparams("parallel"),
    )(ys, u, dsk)


def _s5_act_bwd(y, u, dsk, dgl, *, name):
    T, C = u.shape
    tm = 256

    def body(y_ref, u_ref, d_ref, dgl_ref, dy_ref, du_ref, dd_ref):
        y = y_ref[...]
        cdf = 0.5 * (1.0 + _erf(y * (2.0 ** -0.5)))
        pdf = jnp.exp(-0.5 * y * y) * (1.0 / math.sqrt(2.0 * math.pi))
        dy = dgl_ref[...].astype(F32) * (cdf + y * pdf)
        dy_ref[...] = dy
        du_ref[...] = dy * d_ref[...]
        part = jnp.sum(dy * u_ref[...], axis=0, keepdims=True)

        @pl.when(pl.program_id(0) == 0)
        def _():
            dd_ref[...] = part

        @pl.when(pl.program_id(0) > 0)
        def _():
            dd_ref[...] += part

    row = pl.BlockSpec((tm, C), lambda i: (i, 0))
    vec = pl.BlockSpec((1, C), lambda i: (0, 0))
    return pl.pallas_call(
        body, name=name, grid=(T // tm,), in_specs=[row, row, vec, row], out_specs=[row, row, vec],
        out_shape=[jax.ShapeDtypeStruct((T, C), F32), jax.ShapeDtypeStruct((T, C), F32), jax.ShapeDtypeStruct((1, C), F32)],
        compiler_params=_cparams("arbitrary"),
    )(y, u, dsk, dgl)


def _add2(a, b, *, name):
    T, C = a.shape
    tm = 256

    def body(a_ref, b_ref, o_ref):
        o_ref[...] = a_ref[...] + b_ref[...]

    row = pl.BlockSpec((tm, C), lambda i: (i, 0))
    return pl.pallas_call(body, name=name, grid=(T // tm,), in_specs=[row, row], out_specs=row,
                          out_shape=jax.ShapeDtypeStruct((T, C), F32), compiler_params=_cparams("parallel"))(a, b)


def _loss_head(y, target, *, name):
    T, C = y.shape
    tm = 256

    def body(y_ref, t_ref, l_ref, d_ref):
        err = y_ref[...] - t_ref[...]
        d_ref[...] = err * (1.0 / C)
        sq = err * err
        part = jnp.zeros((8, LANES), F32)
        for r in range(0, tm, 8):
            for c in range(0, C, LANES):
                part = part + sq[r:r + 8, c:c + LANES]

        @pl.when(pl.program_id(0) == 0)
        def _():
            l_ref[...] = part

        @pl.when(pl.program_id(0) > 0)
        def _():
            l_ref[...] += part

    row = pl.BlockSpec((tm, C), lambda i: (i, 0))
    acc = pl.BlockSpec((8, LANES), lambda i: (0, 0))
    return pl.pallas_call(
        body, name=name, grid=(T // tm,), in_specs=[row, row], out_specs=[acc, row],
        out_shape=[jax.ShapeDtypeStruct((8, LANES), F32), jax.ShapeDtypeStruct((T, C), F32)],
        compiler_params=_cparams("arbitrary"),
    )(y, target)


def _adamw(w, g, m, v, *, name):
    shape = w.shape
    size = int(np.prod(shape))
    cols = LANES if (shape[-1] < LANES and size % LANES == 0) else shape[-1]
    rows = size // cols
    tm = _tile(rows, 256) if rows % 8 == 0 else rows
    w2, g2, m2, v2 = (t.reshape(rows, cols) for t in (w, g, m, v))

    def body(w_ref, g_ref, m_ref, v_ref, d_ref, nm_ref, nv_ref):
        gg = g_ref[...]
        nm = ADAM_B1 * m_ref[...] + (1.0 - ADAM_B1) * gg
        nv = ADAM_B2 * v_ref[...] + (1.0 - ADAM_B2) * (gg * gg)
        m_hat = nm / (1.0 - ADAM_B1 ** ADAM_STEP)
        v_hat = nv / (1.0 - ADAM_B2 ** ADAM_STEP)
        d_ref[...] = -ADAM_LR * (m_hat / (jnp.sqrt(v_hat) + ADAM_EPS) + ADAM_WD * w_ref[...])
        nm_ref[...] = nm
        nv_ref[...] = nv

    blk = pl.BlockSpec((tm, cols), lambda i: (i, 0))
    outs = pl.pallas_call(
        body, name=name, grid=(rows // tm,), in_specs=[blk] * 4, out_specs=[blk] * 3,
        out_shape=[jax.ShapeDtypeStruct((rows, cols), F32)] * 3, compiler_params=_cparams("parallel"),
    )(w2, g2, m2, v2)
    return tuple(o.reshape(shape) for o in outs)


_HBM = pl.BlockSpec(memory_space=pltpu.HBM)
_MESH = pl.DeviceIdType.MESH


def _logical(px, py, pc):
    return 4 * px + 2 * py + pc


def _all_gather(shard, *, name):
    R, C = shard.shape

    def body(x_ref, out_ref, send_sems, recv_sems, local_sem):
        x, y, c = lax.axis_index("x"), lax.axis_index("y"), lax.axis_index("c")
        me, sibling = (x, y, c), (x, y, 1 - c)
        chips = [(1 - x, y), (x, 1 - y), (1 - x, 1 - y)]

        def rows(px, py, pc):
            return out_ref.at[_logical(px, py, pc)]

        def copy(k, block, to, src=None):
            return pltpu.make_async_remote_copy(
                src_ref=rows(*block) if src is None else src, dst_ref=rows(*block),
                send_sem=send_sems.at[k], recv_sem=recv_sems.at[k], device_id=to, device_id_type=_MESH)

        mine = pltpu.make_async_copy(x_ref, rows(*me), local_sem)
        mine.start()
        first = [copy(0, me, sibling, src=x_ref)]
        first += [copy(1 + j, me, (*chip, c), src=x_ref) for j, chip in enumerate(chips)]
        for cp in first:
            cp.start()
        passed = [copy(4 + j, (*chip, c), sibling) for j, chip in enumerate(chips)]
        for j, chip in enumerate(chips):
            copy(1 + j, (*chip, c), me).wait_recv()
            passed[j].start()
        copy(0, sibling, me).wait_recv()
        for j, chip in enumerate(chips):
            copy(4 + j, (*chip, 1 - c), me).wait_recv()
        for cp in first + passed:
            cp.wait_send()
        mine.wait()

    return pl.pallas_call(
        body, name=name, out_shape=jax.ShapeDtypeStruct((N_DEV, R, C), shard.dtype),
        in_specs=[_HBM], out_specs=_HBM,
        scratch_shapes=[pltpu.SemaphoreType.DMA((7,)), pltpu.SemaphoreType.DMA((7,)), pltpu.SemaphoreType.DMA],
    )(shard)


def _gather_weights(shards, *, name):
    nt = len(shards)

    def body(*refs):
        ins, outs = refs[:nt], refs[nt:2 * nt]
        send_sems, recv_sems, local_sems = refs[2 * nt:]
        x, y, c = lax.axis_index("x"), lax.axis_index("y"), lax.axis_index("c")
        me, sibling = (x, y, c), (x, y, 1 - c)
        chips = [(1 - x, y), (x, 1 - y), (1 - x, 1 - y)]

        def copy(t, k, block, to, src=None):
            rows = outs[t].at[_logical(*block)]
            return pltpu.make_async_remote_copy(
                src_ref=rows if src is None else src, dst_ref=rows,
                send_sem=send_sems.at[t, k], recv_sem=recv_sems.at[t, k], device_id=to, device_id_type=_MESH)

        mine = [pltpu.make_async_copy(ins[t], outs[t].at[_logical(*me)], local_sems.at[t]) for t in range(nt)]
        for cp in mine:
            cp.start()
        started = []
        for t in range(nt):
            started.append(copy(t, 0, me, sibling, src=ins[t]))
            started += [copy(t, 1 + j, me, (*chip, c), src=ins[t]) for j, chip in enumerate(chips)]
        for cp in started:
            cp.start()
        for j, chip in enumerate(chips):
            for t in range(nt):
                copy(t, 1 + j, (*chip, c), me).wait_recv()
                fwd = copy(t, 4 + j, (*chip, c), sibling)
                fwd.start()
                started.append(fwd)
        for t in range(nt):
            copy(t, 0, sibling, me).wait_recv()
            for j, chip in enumerate(chips):
                copy(t, 4 + j, (*chip, 1 - c), me).wait_recv()
        for cp in started:
            cp.wait_send()
        for cp in mine:
            cp.wait()

    return pl.pallas_call(
        body, name=name, out_shape=[jax.ShapeDtypeStruct((N_DEV,) + s.shape, s.dtype) for s in shards],
        in_specs=[_HBM] * nt, out_specs=[_HBM] * nt,
        scratch_shapes=[pltpu.SemaphoreType.DMA((nt, 7)), pltpu.SemaphoreType.DMA((nt, 7)), pltpu.SemaphoreType.DMA((nt,))],
    )(*shards)


def _scatter_grads(grads, *, name):
    nt = len(grads)

    def body(*refs):
        ins, outs = refs[:nt], refs[nt:2 * nt]
        send_sems, recv_sems, local_sems = refs[2 * nt:]
        x, y, c = lax.axis_index("x"), lax.axis_index("y"), lax.axis_index("c")
        me = _logical(x, y, c)
        mine = [pltpu.make_async_copy(ins[t].at[me], outs[t].at[me], local_sems.at[t]) for t in range(nt)]
        for cp in mine:
            cp.start()
        peers = [(x ^ (k >> 2), y ^ ((k >> 1) & 1), c ^ (k & 1)) for k in range(1, N_DEV)]
        copies = []
        for t in range(nt):
            for k, peer in enumerate(peers):
                copies.append(pltpu.make_async_remote_copy(
                    src_ref=ins[t].at[_logical(*peer)], dst_ref=outs[t].at[me],
                    send_sem=send_sems.at[t, k], recv_sem=recv_sems.at[t, k], device_id=peer, device_id_type=_MESH))
        for cp in copies:
            cp.start()
        for t in range(nt):
            for k, peer in enumerate(peers):
                pltpu.make_async_remote_copy(
                    src_ref=ins[t].at[me], dst_ref=outs[t].at[_logical(*peer)],
                    send_sem=send_sems.at[t, k], recv_sem=recv_sems.at[t, k], device_id=peer, device_id_type=_MESH).wait_recv()
        for cp in copies:
            cp.wait_send()
        for cp in mine:
            cp.wait()

    return pl.pallas_call(
        body, name=name, out_shape=[jax.ShapeDtypeStruct(g.shape, g.dtype) for g in grads],
        in_specs=[_HBM] * nt, out_specs=[_HBM] * nt,
        scratch_shapes=[pltpu.SemaphoreType.DMA((nt, 7)), pltpu.SemaphoreType.DMA((nt, 7)), pltpu.SemaphoreType.DMA((nt,))],
    )(*grads)


def _sum_rows(stacked, *, name):
    _, R, C = stacked.shape
    tr = R
    if N_DEV * R * C * stacked.dtype.itemsize > 12 * 1024 * 1024:
        for cand in range(512, 15, -16):
            if R % cand == 0:
                tr = cand
                break

    def body(s_ref, o_ref):
        acc = s_ref[0].astype(F32)
        for k in range(1, N_DEV):
            acc = acc + s_ref[k].astype(F32)
        o_ref[...] = acc

    return pl.pallas_call(
        body, name=name, grid=(R // tr,),
        in_specs=[pl.BlockSpec((N_DEV, tr, C), lambda i: (0, i, 0))], out_specs=pl.BlockSpec((tr, C), lambda i: (i, 0)),
        out_shape=jax.ShapeDtypeStruct((R, C), F32), compiler_params=_cparams("parallel"),
    )(stacked)


_LARGE = (("ab_w_in", 1, True), ("ab_w_out", 1, False), ("s5_w_glu", 1, True), ("xattn_wq", 2, False),
          ("xattn_wkv", 2, True), ("xattn_wo", 2, False), ("ffn_w_in", 2, True), ("ffn_w_out", 2, False))
_LARGE_KEYS = tuple((n, l) for n, layers, _ in _LARGE for l in range(layers))
_TRANSPOSED = {n: t for n, _, t in _LARGE}


def _owner_major(name, w):
    return w.T if _TRANSPOSED[name] else w


def _lb_from_logits(logits):
    return jnp.cumsum(jax.nn.softmax(logits, axis=0), axis=0)[0:1]


def _to_groups(t, T):
    return t.reshape(T // C_TC, C_TC, C_GROUPS, C_GROUP).transpose(2, 0, 1, 3).reshape(C_GROUPS, T // C_TC, C_TC * C_GROUP)


def _from_groups(t, T):
    return t.reshape(C_GROUPS, T // C_TC, C_TC, C_GROUP).transpose(1, 2, 0, 3).reshape(T, D_MODEL)


def _local_step(x, mem, target, W):
    B, L, _ = x.shape
    T = B * L
    x0 = x.reshape(T, D_MODEL)
    memf = mem.reshape(B * MEM_LEN, D_MODEL)
    nw = W["norm_w"]
    cos2, sin2 = _rope_tables(L)
    G = {}

    def vec(v):
        return v.reshape(1, -1)

    saved = []
    xin = x0
    for layer in range(2):
        s = {"x0": xin}
        tag = f"l{layer}"
        h1 = _rms_fwd(xin, vec(nw[layer, 0]), name=f"norm_pre_mix_{tag}", out_dtype=BF16 if layer == 0 else F32)
        s["h1"] = h1
        if layer == 0:
            lb, lb_vjp = jax.vjp(_lb_from_logits, W["hgrn_lb_logits"])
            onw = W["hgrn_out_norm_w"].reshape(1, A_WIDTH)
            z = _mm(h1, W["ab_w_in", 0], tb=True, name="ab_in")
            oa, o_raw, s_start = _hgrn_fwd(z, lb, onw, B, L, name="hgrn_fwd")
            qr = _rope_fwd(z, cos2, sin2, 16, B, L, name="rope_q")
            kr = _rope_fwd(z, cos2, sin2, 20, B, L, name="rope_k")
            os_, ls_ = [], []
            for dil in B_DILS:
                o_g, l_g = _dil_fwd(qr, kr, z, dil, B, L, name=f"dil_fwd_{dil}")
                os_.append(o_g)
                ls_.append(l_g)
            ob, lse = _dil_combine(os_, ls_, name="dil_combine")
            ymix = jnp.concatenate([oa, ob], axis=-1).astype(BF16)
            y1 = _mm(ymix, W["ab_w_out", 0], name="ab_out")
            s.update(z=z, lb=lb, lb_vjp=lb_vjp, onw=onw, o_raw=o_raw, s_start=s_start, qr=qr, kr=kr, ob=ob, lse=lse, ymix=ymix)
        else:
            p5 = tuple(W[n][0] for n in ("s5_lambda_re", "s5_lambda_im", "s5_log_dt", "s5_b_re", "s5_b_im", "s5_c_re", "s5_c_im"))
            (m_mat, e_mat, f_mat, a16r, a16i), s5_vjp = jax.vjp(_s5_build, *p5)
            nch = L // C_TC
            tab_r, tab_i = _s5_scan_tables(p5[0], p5[1], p5[2], int(math.log2(nch)))
            uf = _to_groups(h1, T).astype(BF16)
            yf, xs = _s5_fwd(uf, m_mat, e_mat, f_mat, tab_r, tab_i, nch, name="s5_fwd")
            dsk = W["s5_d"].reshape(1, D_MODEL)
            gl, ypre = _s5_act_fwd(_from_groups(yf, T), h1, dsk, name="s5_act_fwd")
            zg = _mm(gl, W["s5_w_glu", 0], tb=True, out_dtype=BF16, name="s5_glu_in")
            y1 = _gated_fwd(zg, "glu", name="s5_glu")
            s.update(s5_vjp=s5_vjp, mats=(m_mat, e_mat, f_mat, tab_r, tab_i), uf=uf, xs=xs, dsk=dsk, gl=gl, ypre=ypre, zg=zg, nch=nch)
        x1 = _rms_fwd(y1, vec(nw[layer, 1]), xin, name=f"norm_post_mix_{tag}")
        h2 = _rms_fwd(x1, vec(nw[layer, 2]), name=f"norm_pre_x_{tag}", out_dtype=BF16)
        memn = _rms_fwd(memf, vec(W["mem_norm_w"][layer]), name=f"norm_mem_{tag}", out_dtype=BF16)
        q = _mm(h2, W["xattn_wq", layer], out_dtype=BF16, name=f"x_q_{tag}")
        kv = _mm(memn, W["xattn_wkv", layer], tb=True, out_dtype=BF16, name=f"x_kv_{tag}")
        o = _xattn_fwd(q, kv, B, L, name=f"x_attn_{tag}")
        y2 = _mm(o, W["xattn_wo", layer], name=f"x_o_{tag}")
        x2 = _rms_fwd(y2, vec(nw[layer, 3]), x1, name=f"norm_post_x_{tag}")
        h3 = _rms_fwd(x2, vec(nw[layer, 4]), name=f"norm_pre_ffn_{tag}", out_dtype=BF16)
        zf = _mm(h3, W["ffn_w_in", layer], tb=True, out_dtype=BF16, name=f"ffn_in_{tag}")
        u = _gated_fwd(zf, "swiglu", name=f"ffn_act_{tag}")
        y3 = _mm(u, W["ffn_w_out", layer], name=f"ffn_out_{tag}")
        x3 = _rms_fwd(y3, vec(nw[layer, 5]), x2, name=f"norm_post_ffn_{tag}")
        s.update(y1=y1, x1=x1, h2=h2, memn=memn, q=q, kv=kv, o=o, y2=y2, x2=x2, h3=h3, zf=zf, u=u, y3=y3)
        saved.append(s)
        xin = x3

    loss_parts, dx = _loss_head(xin, target.reshape(T, D_MODEL), name="loss_head")

    d_norm = [[None] * 6 for _ in range(2)]
    d_memn = [None, None]
    for layer in (1, 0):
        s = saved[layer]
        tag = f"l{layer}"
        dy3, d_norm[layer][5] = _rms_bwd(s["y3"], vec(nw[layer, 5]), dx, name=f"bnorm_post_ffn_{tag}", out_dtype=BF16)
        du = _mm(dy3, W["ffn_w_out", layer], tb=True, out_dtype=BF16, name=f"b_ffn_out_dx_{tag}")
        G["ffn_w_out", layer] = _mm(s["u"], dy3, ta=True, out_dtype=BF16, name=f"b_ffn_out_dw_{tag}")
        dzf = _gated_bwd(s["zf"], du, "swiglu", name=f"b_ffn_act_{tag}")
        G["ffn_w_in", layer] = _mm(dzf, s["h3"], ta=True, out_dtype=BF16, name=f"b_ffn_in_dw_{tag}")
        dh3 = _mm(dzf, W["ffn_w_in", layer], out_dtype=BF16, name=f"b_ffn_in_dx_{tag}")
        dx, d_norm[layer][4] = _rms_bwd(s["x2"], vec(nw[layer, 4]), dh3, dx, name=f"bnorm_pre_ffn_{tag}")
        dy2, d_norm[layer][3] = _rms_bwd(s["y2"], vec(nw[layer, 3]), dx, name=f"bnorm_post_x_{tag}", out_dtype=BF16)
        do = _mm(dy2, W["xattn_wo", layer], tb=True, out_dtype=BF16, name=f"b_x_o_dx_{tag}")
        G["xattn_wo", layer] = _mm(s["o"], dy2, ta=True, out_dtype=BF16, name=f"b_x_o_dw_{tag}")
        dq, dkv = _xattn_bwd(s["q"], s["kv"], do, B, L, name=f"b_x_attn_{tag}")
        G["xattn_wq", layer] = _mm(s["h2"], dq, ta=True, out_dtype=BF16, name=f"b_x_q_dw_{tag}")
        dh2 = _mm(dq, W["xattn_wq", layer], tb=True, out_dtype=BF16, name=f"b_x_q_dx_{tag}")
        G["xattn_wkv", layer] = _mm(dkv, s["memn"], ta=True, out_dtype=BF16, name=f"b_x_kv_dw_{tag}")
        dmemn = _mm(dkv, W["xattn_wkv", layer], out_dtype=BF16, name=f"b_x_kv_dx_{tag}")
        _, d_memn[layer] = _rms_bwd(memf, vec(W["mem_norm_w"][layer]), dmemn, name=f"bnorm_mem_{tag}", out_dtype=BF16)
        dx, d_norm[layer][2] = _rms_bwd(s["x1"], vec(nw[layer, 2]), dh2, dx, name=f"bnorm_pre_x_{tag}")
        dy1, d_norm[layer][1] = _rms_bwd(s["y1"], vec(nw[layer, 1]), dx, name=f"bnorm_post_mix_{tag}", out_dtype=BF16)
        if layer == 0:
            z = s["z"]
            dymix = _mm(dy1, W["ab_w_out", 0], tb=True, name="b_ab_out_dx")
            G["ab_w_out", 0] = _mm(s["ymix"], dy1, ta=True, out_dtype=BF16, name="b_ab_out_dw")
            doa, dob = dymix[:, :A_WIDTH], dymix[:, A_WIDTH:]
            dqa, dfa, dia, dga, d_onw, d_lb = _hgrn_bwd(z, s["lb"], s["onw"], s["o_raw"], s["s_start"], doa, B, L, name="hgrn_bwd")
            dqs, dks, dvs = [], [], []
            for dil in B_DILS:
                dqs.append(_dil_bwd_q(s["qr"], s["kr"], z, dob, s["ob"], s["lse"], dil, B, L, name=f"dil_bwd_q_{dil}"))
                dk_g, dv_g = _dil_bwd_kv(s["qr"], s["kr"], z, dob, s["ob"], s["lse"], dil, B, L, name=f"dil_bwd_kv_{dil}")
                dks.append(dk_g)
                dvs.append(dv_g)
            dqb = _sum3(*dqs, cos2, sin2, True, B, L, name="b_rope_q")
            dkb = _sum3(*dks, cos2, sin2, True, B, L, name="b_rope_k")
            dvb = _sum3(*dvs, cos2, sin2, False, B, L, name="b_sum_v")
            dz = jnp.concatenate([dqa, dfa, dia, dga, dqb, dkb, dvb], axis=-1).astype(BF16)
            G["ab_w_in", 0] = _mm(dz, s["h1"], ta=True, out_dtype=BF16, name="b_ab_in_dw")
            dh1 = _mm(dz, W["ab_w_in", 0], out_dtype=BF16, name="b_ab_in_dx")
            G["hgrn_out_norm_w"] = jnp.sum(d_onw.reshape(B, A_WIDTH), axis=0, keepdims=True)
            d_lb_row = jnp.sum(d_lb.reshape(B, A_WIDTH), axis=0, keepdims=True)
            G["hgrn_lb_logits"] = s["lb_vjp"](d_lb_row)[0]
        else:
            dzg = _gated_bwd(s["zg"], dy1, "glu", name="b_s5_glu")
            G["s5_w_glu", 0] = _mm(dzg, s["gl"], ta=True, out_dtype=BF16, name="b_s5_glu_dw")
            dgl = _mm(dzg, W["s5_w_glu", 0], out_dtype=BF16, name="b_s5_glu_dx")
            dyp, du_skip, d_dsk = _s5_act_bwd(s["ypre"], s["h1"], s["dsk"], dgl, name="b_s5_act")
            m_mat, e_mat, f_mat, tab_r, tab_i = s["mats"]
            duf, dm, de, df_, da = _s5_bwd(_to_groups(dyp, T).astype(BF16), s["uf"], s["xs"], m_mat, e_mat, f_mat, tab_r, tab_i,
                                           s["nch"], name="s5_bwd")
            da_r = da[:, 0, :C_STATE] + da[:, 0, C_STATE:]
            da_i = da[:, 1, C_STATE:] - da[:, 1, :C_STATE]
            gp = s["s5_vjp"]((dm, de, df_, da_r, da_i))
            for n, gv in zip(("s5_lambda_re", "s5_lambda_im", "s5_log_dt", "s5_b_re", "s5_b_im", "s5_c_re", "s5_c_im"), gp):
                G[n] = gv[None]
            G["s5_d"] = d_dsk
            dh1 = _add2(_from_groups(duf, T), du_skip, name="b_s5_du")
        dx, d_norm[layer][0] = _rms_bwd(s["x0"], vec(nw[layer, 0]), dh1, dx, name=f"bnorm_pre_mix_{tag}")

    G["norm_w"] = jnp.stack([jnp.concatenate(d_norm[l], axis=0) for l in range(2)])
    G["mem_norm_w"] = jnp.concatenate(d_memn, axis=0)
    return loss_parts, dx.reshape(B, L, D_MODEL), G


_SMALL = (("norm_w", (2, 6, 1024)), ("mem_norm_w", (2, 1024)), ("hgrn_lb_logits", (3, 512)), ("hgrn_out_norm_w", (1, 512)),
          ("s5_lambda_re", (1, 64, 64)), ("s5_lambda_im", (1, 64, 64)), ("s5_log_dt", (1, 64)),
          ("s5_b_re", (1, 64, 64, 16)), ("s5_b_im", (1, 64, 64, 16)), ("s5_c_re", (1, 64, 16, 64)),
          ("s5_c_im", (1, 64, 16, 64)), ("s5_d", (1, 1024)))

_WEIGHT_ORDER = ('norm_w', 'mem_norm_w', 'ab_w_in', 'ab_w_out', 'hgrn_lb_logits', 'hgrn_out_norm_w', 's5_lambda_re',
                 's5_lambda_im', 's5_log_dt', 's5_b_re', 's5_b_im', 's5_c_re', 's5_c_im', 's5_d', 's5_w_glu', 'xattn_wq',
                 'xattn_wkv', 'xattn_wo', 'ffn_w_in', 'ffn_w_out')


def kernel(x, mem, norm_w, mem_norm_w, ab_w_in, ab_w_out, hgrn_lb_logits, hgrn_out_norm_w, s5_lambda_re, s5_lambda_im, s5_log_dt, s5_b_re, s5_b_im, s5_c_re, s5_c_im, s5_d, s5_w_glu, xattn_wq, xattn_wkv, xattn_wo, ffn_w_in, ffn_w_out, loss_target, m_norm_w, m_mem_norm_w, m_ab_w_in, m_ab_w_out, m_hgrn_lb_logits, m_hgrn_out_norm_w, m_s5_lambda_re, m_s5_lambda_im, m_s5_log_dt, m_s5_b_re, m_s5_b_im, m_s5_c_re, m_s5_c_im, m_s5_d, m_s5_w_glu, m_xattn_wq, m_xattn_wkv, m_xattn_wo, m_ffn_w_in, m_ffn_w_out, v_norm_w, v_mem_norm_w, v_ab_w_in, v_ab_w_out, v_hgrn_lb_logits, v_hgrn_out_norm_w, v_s5_lambda_re, v_s5_lambda_im, v_s5_log_dt, v_s5_b_re, v_s5_b_im, v_s5_c_re, v_s5_c_im, v_s5_d, v_s5_w_glu, v_xattn_wq, v_xattn_wkv, v_xattn_wo, v_ffn_w_in, v_ffn_w_out):
    local = dict(norm_w=norm_w, mem_norm_w=mem_norm_w, ab_w_in=ab_w_in, ab_w_out=ab_w_out, hgrn_lb_logits=hgrn_lb_logits,
                 hgrn_out_norm_w=hgrn_out_norm_w, s5_lambda_re=s5_lambda_re, s5_lambda_im=s5_lambda_im, s5_log_dt=s5_log_dt,
                 s5_b_re=s5_b_re, s5_b_im=s5_b_im, s5_c_re=s5_c_re, s5_c_im=s5_c_im, s5_d=s5_d, s5_w_glu=s5_w_glu,
                 xattn_wq=xattn_wq, xattn_wkv=xattn_wkv, xattn_wo=xattn_wo, ffn_w_in=ffn_w_in, ffn_w_out=ffn_w_out)
    mom_m = dict(zip(_WEIGHT_ORDER, (m_norm_w, m_mem_norm_w, m_ab_w_in, m_ab_w_out, m_hgrn_lb_logits, m_hgrn_out_norm_w, m_s5_lambda_re, m_s5_lambda_im, m_s5_log_dt, m_s5_b_re, m_s5_b_im, m_s5_c_re, m_s5_c_im, m_s5_d, m_s5_w_glu, m_xattn_wq, m_xattn_wkv, m_xattn_wo, m_ffn_w_in, m_ffn_w_out)))
    mom_v = dict(zip(_WEIGHT_ORDER, (v_norm_w, v_mem_norm_w, v_ab_w_in, v_ab_w_out, v_hgrn_lb_logits, v_hgrn_out_norm_w, v_s5_lambda_re, v_s5_lambda_im, v_s5_log_dt, v_s5_b_re, v_s5_b_im, v_s5_c_re, v_s5_c_im, v_s5_d, v_s5_w_glu, v_xattn_wq, v_xattn_wkv, v_xattn_wo, v_ffn_w_in, v_ffn_w_out)))
    dev = 4 * lax.axis_index("x") + 2 * lax.axis_index("y") + lax.axis_index("c")

    shards = [_owner_major(n, local[n][l]).astype(BF16) for n, l in _LARGE_KEYS]
    gathered = _gather_weights(shards, name="gather_weights")
    W = {key: g.reshape(-1, D_MODEL) for key, g in zip(_LARGE_KEYS, gathered)}
    tiny =jnp.concatenate([norm_w.reshape(-1), s5_d.reshape(-1)])
    tiny = jnp.pad(tiny, (0, 16 * LANES - tiny.shape[0])).reshape(16, LANES)
    tiny_all = _all_gather(tiny, name="gather_tiny").reshape(N_DEV, 16 * LANES)
    W["norm_w"] = tiny_all[:, :12 * LANES].reshape(N_DEV, 2, 6, LANES).transpose(1, 2, 0, 3).reshape(2, 6, D_MODEL)
    W["s5_d"] = tiny_all[:, 12 * LANES:13 * LANES].reshape(1, D_MODEL)
    for n in ("mem_norm_w", "hgrn_lb_logits", "hgrn_out_norm_w", "s5_lambda_re", "s5_lambda_im", "s5_log_dt",
              "s5_b_re", "s5_b_im", "s5_c_re", "s5_c_im"):
        W[n] = local[n]

    loss_parts, grad_x, G = _local_step(x, mem, loss_target, W)

    recv = _scatter_grads([G[key].reshape(N_DEV, -1, D_MODEL) for key in _LARGE_KEYS], name="scatter_grads")
    g_layers = {}
    for (n, l), r in zip(_LARGE_KEYS, recv):
        g = _sum_rows(r, name=f"sum_grads_{n}_{l}")
        g_layers.setdefault(n, []).append(g.T if _TRANSPOSED[n] else g)
    g_local = {n: jnp.stack(gl) for n, gl in g_layers.items()}
    small =jnp.concatenate([G[n].reshape(-1) for n, _ in _SMALL] + [0.5 / D_MODEL * jnp.sum(loss_parts).reshape(1)])
    n_small = small.shape[0]
    small = jnp.pad(small, (0, (-n_small) % (8 * LANES))).reshape(-1, LANES)
    small_sum = _sum_rows(_all_gather(small, name="gather_small"), name="sum_small").reshape(-1)
    g_full, off = {}, 0
    for n, shp in _SMALL:
        size = int(np.prod(shp))
        g_full[n] = small_sum[off:off + size].reshape(shp)
        off += size
    loss = small_sum[off]
    grads = dict(g_local)
    for n, shp in _SMALL:
        if n == "norm_w":
            grads[n] = lax.dynamic_slice_in_dim(g_full[n], dev * LANES, LANES, axis=2)
        elif n == "s5_d":
            grads[n] = lax.dynamic_slice_in_dim(g_full[n], dev * LANES, LANES, axis=1)
        else:
            grads[n] = g_full[n]

    delta, new_m, new_v = {}, {}, {}
    for n in _WEIGHT_ORDER:
        delta[n], new_m[n], new_v[n] = _adamw(local[n], grads[n], mom_m[n], mom_v[n], name=f"adamw_{n}")
    return (loss, grad_x, *[grads[n] for n in _WEIGHT_ORDER], *[delta[n] for n in _WEIGHT_ORDER],
            *[new_m[n] for n in _WEIGHT_ORDER], *[new_v[n] for n in _WEIGHT_ORDER])
```

```python
import functools
import math

import numpy as np
import jax
import jax.numpy as jnp
from jax import lax
from jax.experimental import pallas as pl
from jax.experimental.pallas import tpu as pltpu

F32 = jnp.float32
BF16 = jnp.bfloat16
HI = lax.Precision.HIGHEST

D_MODEL = 1024
NORM_EPS = 1e-6
A_WIDTH = 512
A_HEAD = 128
A_CHUNK = 32
A_SUPER = 256
B_SPAN = 128
B_DILS = (1, 4, 16)
ROPE_THETA = 10000.0
C_GROUPS = 64
C_GROUP = 16
C_STATE = 64
C_TC = 16
C_MIN_NEG_RE = -1e-4
MEM_LEN = 256
X_HEADS = 4
X_HD = 256
D_FF = 2816
N_DEV = 8
LANES = 128

ADAM_LR, ADAM_B1, ADAM_B2, ADAM_EPS, ADAM_WD, ADAM_STEP = 0.001, 0.9, 0.999, 1e-08, 0.01, 10

NEG_BIG = -1e30


def _tile(n, pref):
    p = pref
    while p >= 128:
        if n % p == 0:
            return p
        p //= 2
    return n


def _cparams(*sem):
    return pltpu.CompilerParams(dimension_semantics=sem, vmem_limit_bytes=56 * 1024 * 1024)


def _sigmoid(x):
    return 1.0 / (1.0 + jnp.exp(-x))


def _erf(x):
    ax = jnp.abs(x)
    t = 1.0 / (1.0 + 0.3275911 * ax)
    poly = t * (0.254829592 + t * (-0.284496736 + t * (1.421413741 + t * (-1.453152027 + t * 1.061405429))))
    y = 1.0 - poly * jnp.exp(-ax * ax)
    return jnp.where(x < 0, -y, y)


def _mm(a, b, *, ta=False, tb=False, la=None, lb=None, out_dtype=F32, name, tiles=(1024, 512, 2048)):
    ash = a.shape if la is None else a.shape[1:]
    bsh = b.shape if lb is None else b.shape[1:]
    M, K = (ash[1], ash[0]) if ta else ash
    N = bsh[0] if tb else bsh[1]
    assert (bsh[1] if tb else bsh[0]) == K
    tm, tn = _tile(M, tiles[0]), _tile(N, tiles[1])
    tk = K if K <= 2 * tiles[2] else _tile(K, tiles[2])
    nk = K // tk

    def spec(shape, index, lead):
        if lead is None:
            return pl.BlockSpec(shape, index)
        return pl.BlockSpec((None,) + shape, lambda i, j, k: (lead,) + index(i, j, k))

    a_spec = spec((tk, tm), lambda i, j, k: (k, i), la) if ta else spec((tm, tk), lambda i, j, k: (i, k), la)
    b_spec = spec((tn, tk), lambda i, j, k: (j, k), lb) if tb else spec((tk, tn), lambda i, j, k: (k, j), lb)
    dims = (((0 if ta else 1,), (1 if tb else 0,)), ((), ()))

    def body(a_ref, b_ref, o_ref, *acc):
        part = lax.dot_general(a_ref[...].astype(BF16), b_ref[...].astype(BF16), dims, preferred_element_type=F32)
        if nk == 1:
            o_ref[...] = part.astype(o_ref.dtype)
            return
        acc_ref = acc[0]
        k = pl.program_id(2)

        @pl.when(k == 0)
        def _():
            acc_ref[...] = part

        @pl.when(k > 0)
        def _():
            acc_ref[...] += part

        @pl.when(k == nk - 1)
        def _():
            o_ref[...] = acc_ref[...].astype(o_ref.dtype)

    return pl.pallas_call(
        body, name=name, grid=(M // tm, N // tn, nk),
        in_specs=[a_spec, b_spec], out_specs=pl.BlockSpec((tm, tn), lambda i, j, k: (i, j)),
        out_shape=jax.ShapeDtypeStruct((M, N), out_dtype),
        scratch_shapes=[pltpu.VMEM((tm, tn), F32)] if nk > 1 else [],
        compiler_params=_cparams("parallel", "parallel", "arbitrary"),
    )(a, b)


def _rms_fwd(x, w, res=None, *, name, out_dtype=F32):
    T, C = x.shape
    tm = _tile(T, 512)
    has_res = res is not None

    def body(*refs):
        x_ref, w_ref = refs[0], refs[1]
        o_ref = refs[-1]
        xv = x_ref[...].astype(F32)
        r = lax.rsqrt(jnp.mean(xv * xv, axis=-1, keepdims=True) + NORM_EPS)
        y = xv * r * w_ref[...]
        if has_res:
            y = y + refs[2][...]
        o_ref[...] = y.astype(o_ref.dtype)

    row = pl.BlockSpec((tm, C), lambda i: (i, 0))
    vec = pl.BlockSpec((1, C), lambda i: (0, 0))
    ins = [x, w] + ([res] if has_res else [])
    return pl.pallas_call(
        body, name=name, grid=(T // tm,), in_specs=[row, vec] + ([row] if has_res else []), out_specs=row,
        out_shape=jax.ShapeDtypeStruct((T, C), out_dtype), compiler_params=_cparams("parallel"),
    )(*ins)


def _rms_bwd(x, w, dy, add=None, *, name, out_dtype=F32):
    T, C = x.shape
    tm = _tile(T, 512)
    has_add = add is not None

    def body(*refs):
        x_ref, w_ref, dy_ref = refs[:3]
        dx_ref, dw_ref = refs[-2], refs[-1]
        xv = x_ref[...].astype(F32)
        r = lax.rsqrt(jnp.mean(xv * xv, axis=-1, keepdims=True) + NORM_EPS)
        xh = xv * r
        g = dy_ref[...].astype(F32)
        part = jnp.sum(g * xh, axis=0, keepdims=True)

        @pl.when(pl.program_id(0) == 0)
        def _():
            dw_ref[...] = part

        @pl.when(pl.program_id(0) > 0)
        def _():
            dw_ref[...] += part

        gx = g * w_ref[...]
        dx = r * (gx - xh * jnp.mean(gx * xh, axis=-1, keepdims=True))
        if has_add:
            dx = dx + refs[3][...]
        dx_ref[...] = dx.astype(dx_ref.dtype)

    row = pl.BlockSpec((tm, C), lambda i: (i, 0))
    vec = pl.BlockSpec((1, C), lambda i: (0, 0))
    ins = [x, w, dy] + ([add] if has_add else [])
    return pl.pallas_call(
        body, name=name, grid=(T // tm,), in_specs=[row, vec, row] + ([row] if has_add else []),
        out_specs=[row, vec],
        out_shape=[jax.ShapeDtypeStruct((T, C), out_dtype), jax.ShapeDtypeStruct((1, C), F32)],
        compiler_params=_cparams("arbitrary"),
    )(*ins)


def _gate_block(width):
    return _tile(width, 256)


def _interleave_rows(w, inverse=False):
    W2, C = w.shape
    bs = _gate_block(W2 // 2)
    nb = W2 // 2 // bs
    shape = (nb, 2, bs, C) if inverse else (2, nb, bs, C)
    return w.reshape(shape).transpose(1, 0, 2, 3).reshape(W2, C)


def _gated_fwd(z, kind, *, name):
    T, W2 = z.shape
    W = W2 // 2
    tm, bs = _tile(T, 512), _gate_block(W)

    def body(z_ref, o_ref):
        a, b = z_ref[:, :bs].astype(F32), z_ref[:, bs:].astype(F32)
        if kind == "swiglu":
            o_ref[...] = (a * _sigmoid(a) * b).astype(o_ref.dtype)
        else:
            o_ref[...] = (a * _sigmoid(b)).astype(o_ref.dtype)

    return pl.pallas_call(
        body, name=name, grid=(T // tm, W // bs),
        in_specs=[pl.BlockSpec((tm, 2 * bs), lambda i, j: (i, j))],
        out_specs=pl.BlockSpec((tm, bs), lambda i, j: (i, j)),
        out_shape=jax.ShapeDtypeStruct((T, W), BF16), compiler_params=_cparams("parallel", "parallel"),
    )(z)


def _gated_bwd(z, dout, kind, *, name):
    T, W2 = z.shape
    W = W2 // 2
    tm, bs = _tile(T, 512), _gate_block(W)

    def body(z_ref, d_ref, o_ref):
        a, b, d = z_ref[:, :bs].astype(F32), z_ref[:, bs:].astype(F32), d_ref[...].astype(F32)
        if kind == "swiglu":
            s = _sigmoid(a)
            da = d * b * (s * (1.0 + a * (1.0 - s)))
            db = d * a * s
        else:
            s = _sigmoid(b)
            da = d * s
            db = d * a * s * (1.0 - s)
        o_ref[:, :bs] = da.astype(o_ref.dtype)
        o_ref[:, bs:] = db.astype(o_ref.dtype)

    return pl.pallas_call(
        body, name=name, grid=(T // tm, W // bs),
        in_specs=[pl.BlockSpec((tm, 2 * bs), lambda i, j: (i, j)), pl.BlockSpec((tm, bs), lambda i, j: (i, j))],
        out_specs=pl.BlockSpec((tm, 2 * bs), lambda i, j: (i, j)),
        out_shape=jax.ShapeDtypeStruct((T, W2), BF16), compiler_params=_cparams("parallel", "parallel"),
    )(z, dout)


_NT = (((1,), (1,)), ((), ()))
_TN = (((0,), (0,)), ((), ()))


def _dot(a, b, dims=None, precision=None):
    if dims is None:
        return jnp.dot(a, b, preferred_element_type=F32, precision=precision)
    return lax.dot_general(a, b, dims, preferred_element_type=F32, precision=precision)


def _xattn_fwd(q, kv, B, L, *, name):
    T = q.shape[0]
    tq = 256
    nq = L // tq
    scale = X_HD ** -0.5

    def body(q_ref, k_ref, v_ref, o_ref):
        for h in range(X_HEADS):
            sl = slice(h * X_HD, (h + 1) * X_HD)
            qh, kh, vh = q_ref[:, sl].astype(BF16), k_ref[:, sl].astype(BF16), v_ref[:, sl].astype(BF16)
            s = _dot(qh, kh, _NT) * scale
            m = jnp.max(s, axis=-1, keepdims=True)
            p = jnp.exp(s - m)
            l = jnp.sum(p, axis=-1, keepdims=True)
            o_ref[:, sl] = (_dot(p.astype(BF16), vh) / l).astype(BF16)

    return pl.pallas_call(
        body, name=name, grid=(B, nq),
        in_specs=[pl.BlockSpec((tq, D_MODEL), lambda b, i: (b * nq + i, 0)),
                  pl.BlockSpec((MEM_LEN, D_MODEL), lambda b, i: (b, 0)),
                  pl.BlockSpec((MEM_LEN, D_MODEL), lambda b, i: (b, 1))],
        out_specs=pl.BlockSpec((tq, D_MODEL), lambda b, i: (b * nq + i, 0)),
        out_shape=jax.ShapeDtypeStruct((T, D_MODEL), BF16), compiler_params=_cparams("parallel", "parallel"),
    )(q, kv, kv)


def _xattn_bwd(q, kv, do, B, L, *, name):
    T = q.shape[0]
    tq = 256
    nq = L // tq
    scale = X_HD ** -0.5

    def body(q_ref, k_ref, v_ref, do_ref, dq_ref, dkv_ref):
        @pl.when(pl.program_id(1) == 0)
        def _():
            dkv_ref[...] = jnp.zeros_like(dkv_ref)

        for h in range(X_HEADS):
            sl = slice(h * X_HD, (h + 1) * X_HD)
            slv = slice(D_MODEL + h * X_HD, D_MODEL + (h + 1) * X_HD)
            qh, kh, vh = q_ref[:, sl].astype(BF16), k_ref[:, sl].astype(BF16), v_ref[:, sl].astype(BF16)
            doh = do_ref[:, sl].astype(BF16)
            s = _dot(qh, kh, _NT) * scale
            m = jnp.max(s, axis=-1, keepdims=True)
            e = jnp.exp(s - m)
            p = e / jnp.sum(e, axis=-1, keepdims=True)
            dkv_ref[:, slv] += _dot(p.astype(BF16), doh, _TN)
            dp = _dot(doh, vh, _NT)
            ds = p * (dp - jnp.sum(dp * p, axis=-1, keepdims=True)) * scale
            dsb = ds.astype(BF16)
            dq_ref[:, sl] = _dot(dsb, kh).astype(BF16)
            dkv_ref[:, sl] += _dot(dsb, qh, _TN)

    return pl.pallas_call(
        body, name=name, grid=(B, nq),
        in_specs=[pl.BlockSpec((tq, D_MODEL), lambda b, i: (b * nq + i, 0)),
                  pl.BlockSpec((MEM_LEN, D_MODEL), lambda b, i: (b, 0)),
                  pl.BlockSpec((MEM_LEN, D_MODEL), lambda b, i: (b, 1)),
                  pl.BlockSpec((tq, D_MODEL), lambda b, i: (b * nq + i, 0))],
        out_specs=[pl.BlockSpec((tq, D_MODEL), lambda b, i: (b * nq + i, 0)),
                   pl.BlockSpec((MEM_LEN, 2 * D_MODEL), lambda b, i: (b, 0))],
        out_shape=[jax.ShapeDtypeStruct((T, D_MODEL), BF16), jax.ShapeDtypeStruct((B * MEM_LEN, 2 * D_MODEL), F32)],
        compiler_params=_cparams("parallel", "arbitrary"),
    )(q, kv, kv, do)


def _chunk_masks():
    row = lax.broadcasted_iota(jnp.int32, (A_SUPER, A_SUPER), 0)
    col = lax.broadcasted_iota(jnp.int32, (A_SUPER, A_SUPER), 1)
    same = jnp.right_shift(row, 5) == jnp.right_shift(col, 5)
    return same, same & (col <= row), same & (col >= row)


def _hgrn_gates(fa, lb):
    sig = _sigmoid(fa)
    f = lb + (1.0 - lb) * sig
    return sig, f, jnp.log(f), 1.0 - f


def _hgrn_fwd(z, lb, onw, B, L, *, name):
    T = B * L
    ns = L // A_SUPER
    nch = A_SUPER // A_CHUNK

    def body(q_ref, f_ref, v_ref, g_ref, lb_ref, w_ref, oa_ref, o_ref, s_ref, st_ref):
        @pl.when(pl.program_id(2) == 0)
        def _():
            st_ref[...] = jnp.zeros_like(st_ref)

        s_ref[0] = st_ref[...]
        same, tril, _ = _chunk_masks()
        q, v = q_ref[...], v_ref[...]
        _, _, lf, k = _hgrn_gates(f_ref[...], lb_ref[...])
        bcs = _dot(tril.astype(F32), lf, precision=HI)
        bl = _dot(same.astype(F32), lf, precision=HI)
        qd = (q * jnp.exp(bcs)).astype(BF16)
        ki = (k * jnp.exp(-bcs)).astype(BF16)
        ke = (k * jnp.exp(bl - bcs)).astype(BF16)
        dec = jnp.exp(bl)
        vb = v.astype(BF16)
        a = jnp.where(tril, _dot(qd, ki, _NT), 0.0)
        o_ref[...] = _dot(a.astype(BF16), vb)
        st = st_ref[...]
        for c in range(nch):
            rs = slice(c * A_CHUNK, (c + 1) * A_CHUNK)
            o_ref[rs, :] += _dot(qd[rs], st.astype(BF16), _NT)
            st = st * dec[c * A_CHUNK:c * A_CHUNK + 1, :] + _dot(vb[rs], ke[rs], _TN)
        st_ref[...] = st
        o = o_ref[...]
        r = lax.rsqrt(jnp.mean(o * o, axis=-1, keepdims=True) + NORM_EPS)
        g = g_ref[...]
        oa_ref[...] = o * r * w_ref[...] * (g * _sigmoid(g))

    def zspec(off):
        return pl.BlockSpec((A_SUPER, A_HEAD), lambda b, h, n: (b * ns + n, off + h))

    hvec = pl.BlockSpec((1, A_HEAD), lambda b, h, n: (0, h))
    ospec = pl.BlockSpec((A_SUPER, A_HEAD), lambda b, h, n: (b * ns + n, h))
    return pl.pallas_call(
        body, name=name, grid=(B, 4, ns),
        in_specs=[zspec(0), zspec(4), zspec(8), zspec(12), hvec, hvec],
        out_specs=[ospec, ospec, pl.BlockSpec((1, A_HEAD, A_HEAD), lambda b, h, n: ((b * 4 + h) * ns + n, 0, 0))],
        out_shape=[jax.ShapeDtypeStruct((T, A_WIDTH), F32), jax.ShapeDtypeStruct((T, A_WIDTH), F32),
                   jax.ShapeDtypeStruct((B * 4 * ns, A_HEAD, A_HEAD), F32)],
        scratch_shapes=[pltpu.VMEM((A_HEAD, A_HEAD), F32)],
        compiler_params=_cparams("parallel", "parallel", "arbitrary"),
    )(z, z, z, z, lb, onw)


def _hgrn_bwd(z, lb, onw, o_raw, s_start, doa, B, L, *, name):
    T = B * L
    ns = L // A_SUPER
    nch = A_SUPER // A_CHUNK

    def body(q_ref, f_ref, v_ref, g_ref, lb_ref, w_ref, o_ref, s_ref, doa_ref,
             dq_ref, df_ref, dv_ref, dg_ref, dw_ref, dlb_ref, dst_ref, sc_ref, dqd_ref, dke_ref, dblx_ref):
        @pl.when(pl.program_id(2) == 0)
        def _():
            dst_ref[...] = jnp.zeros_like(dst_ref)
            dw_ref[...] = jnp.zeros_like(dw_ref)
            dlb_ref[...] = jnp.zeros_like(dlb_ref)

        same, tril, triu = _chunk_masks()
        q, v, g, lb, w = q_ref[...], v_ref[...], g_ref[...], lb_ref[...], w_ref[...]
        sig, f, lf, k = _hgrn_gates(f_ref[...], lb)
        bcs = _dot(tril.astype(F32), lf, precision=HI)
        bl = _dot(same.astype(F32), lf, precision=HI)
        eb, enb, eeb = jnp.exp(bcs), jnp.exp(-bcs), jnp.exp(bl - bcs)
        qd, ki, ke = q * eb, k * enb, k * eeb
        qdb, kib, keb, vb = qd.astype(BF16), ki.astype(BF16), ke.astype(BF16), v.astype(BF16)
        dec = jnp.exp(bl)
        o = o_ref[...]
        r = lax.rsqrt(jnp.mean(o * o, axis=-1, keepdims=True) + NORM_EPS)
        on = o * r
        sg = _sigmoid(g)
        silu_g = g * sg
        doa = doa_ref[...]
        dg_ref[...] = doa * on * w * (sg * (1.0 + g * (1.0 - sg)))
        dw_ref[0] += jnp.sum(doa * on * silu_g, axis=0, keepdims=True)
        don = doa * w * silu_g
        do = r * (don - on * jnp.mean(don * on, axis=-1, keepdims=True))
        dob = do.astype(BF16)
        a = jnp.where(tril, _dot(qdb, kib, _NT), 0.0).astype(BF16)
        da = jnp.where(tril, _dot(dob, vb, _NT), 0.0).astype(BF16)
        dv_ref[...] = _dot(a, dob, _TN)
        dqd_ref[...] = _dot(da, kib)
        dki = _dot(da, qdb, _TN)
        st = s_ref[0]
        for c in range(nch):
            rs = slice(c * A_CHUNK, (c + 1) * A_CHUNK)
            sc_ref[c] = st
            st = st * dec[c * A_CHUNK:c * A_CHUNK + 1, :] + _dot(vb[rs], keb[rs], _TN)
        dst = dst_ref[...]
        for c in reversed(range(nch)):
            rs = slice(c * A_CHUNK, (c + 1) * A_CHUNK)
            dec_c = dec[c * A_CHUNK:c * A_CHUNK + 1, :]
            dstb = dst.astype(BF16)
            stc = sc_ref[c]
            dv_ref[rs, :] += _dot(keb[rs], dstb, _NT)
            dke_ref[rs, :] = _dot(vb[rs], dstb)
            ddec = jnp.sum(dst * stc, axis=0, keepdims=True)
            dqd_ref[rs, :] += _dot(dob[rs], stc.astype(BF16))
            dblx_ref[rs, :] = jnp.broadcast_to(ddec * dec_c, (A_CHUNK, A_HEAD))
            dst = dst * dec_c + _dot(dob[rs], qdb[rs], _TN)
        dst_ref[...] = dst
        dqd, dke = dqd_ref[...], dke_ref[...]
        dq_ref[...] = dqd * eb
        keke = dke * ke
        db = dqd * qd - dki * ki - keke
        dbl = _dot(same.astype(F32), keke, precision=HI) + dblx_ref[...]
        dk = dki * enb + dke * eeb
        dlf = _dot(triu.astype(F32), db, precision=HI) + dbl
        dff = dlf / f - dk
        df_ref[...] = dff * (1.0 - lb) * sig * (1.0 - sig)
        dlb_ref[0] += jnp.sum(dff * (1.0 - sig), axis=0, keepdims=True)

    def rev(n):
        return ns - 1 - n

    def zspec(off):
        return pl.BlockSpec((A_SUPER, A_HEAD), lambda b, h, n: (b * ns + rev(n), off + h))

    hvec = pl.BlockSpec((1, A_HEAD), lambda b, h, n: (0, h))
    ospec = pl.BlockSpec((A_SUPER, A_HEAD), lambda b, h, n: (b * ns + rev(n), h))
    acc = pl.BlockSpec((1, 1, A_HEAD), lambda b, h, n: (b * 4 + h, 0, 0))
    big = jax.ShapeDtypeStruct((T, A_WIDTH), F32)
    small = jax.ShapeDtypeStruct((B * 4, 1, A_HEAD), F32)
    return pl.pallas_call(
        body, name=name, grid=(B, 4, ns),
        in_specs=[zspec(0), zspec(4), zspec(8), zspec(12), hvec, hvec, ospec,
                  pl.BlockSpec((1, A_HEAD, A_HEAD), lambda b, h, n: ((b * 4 + h) * ns + rev(n), 0, 0)), ospec],
        out_specs=[ospec, ospec, ospec, ospec, acc, acc],
        out_shape=[big, big, big, big, small, small],
        scratch_shapes=[pltpu.VMEM((A_HEAD, A_HEAD), F32), pltpu.VMEM((nch, A_HEAD, A_HEAD), F32),
                        pltpu.VMEM((A_SUPER, A_HEAD), F32), pltpu.VMEM((A_SUPER, A_HEAD), F32),
                        pltpu.VMEM((A_SUPER, A_HEAD), F32)],
        compiler_params=_cparams("parallel", "parallel", "arbitrary"),
    )(z, z, z, z, lb, onw, o_raw, s_start, doa)


def _rope_tables(L):
    half = A_HEAD // 2
    inv_freq = ROPE_THETA ** (-jnp.arange(half, dtype=F32) / half)
    ang = jnp.arange(L, dtype=F32)[:, None] * inv_freq[None, :]
    cos, sin = jnp.cos(ang), jnp.sin(ang)
    return jnp.concatenate([cos, cos], axis=-1), jnp.concatenate([-sin, sin], axis=-1)


def _rope_fwd(z, cos2, sin2, col_off, B, L, *, name):
    T = B * L
    tm = 256
    nl = L // tm

    def body(x_ref, c_ref, s_ref, o_ref):
        x = x_ref[...]
        o_ref[...] = x * c_ref[...] + pltpu.roll(x, A_HEAD // 2, 1) * s_ref[...]

    tab = pl.BlockSpec((tm, A_HEAD), lambda i, h: (i % nl, 0))
    return pl.pallas_call(
        body, name=name, grid=(T // tm, 4),
        in_specs=[pl.BlockSpec((tm, A_HEAD), lambda i, h: (i, col_off + h)), tab, tab],
        out_specs=pl.BlockSpec((tm, A_HEAD), lambda i, h: (i, h)),
        out_shape=jax.ShapeDtypeStruct((T, 512), F32), compiler_params=_cparams("parallel", "parallel"),
    )(z, cos2, sin2)


def _sum3(d1, d2, d3, cos2, sin2, rotate, B, L, *, name):
    T = B * L
    tm = 256
    nl = L // tm

    def body(a_ref, b_ref, c_ref, cs_ref, sn_ref, o_ref):
        d = a_ref[...] + b_ref[...] + c_ref[...]
        if rotate:
            d = d * cs_ref[...] - pltpu.roll(d, A_HEAD // 2, 1) * sn_ref[...]
        o_ref[...] = d

    blk = pl.BlockSpec((tm, A_HEAD), lambda i, h: (i, h))
    tab = pl.BlockSpec((tm, A_HEAD), lambda i, h: (i % nl, 0))
    return pl.pallas_call(
        body, name=name, grid=(T // tm, 4), in_specs=[blk, blk, blk, tab, tab], out_specs=blk,
        out_shape=jax.ShapeDtypeStruct((T, 512), F32), compiler_params=_cparams("parallel", "parallel"),
    )(d1, d2, d3, cos2, sin2)


def _band_masks():
    i = lax.broadcasted_iota(jnp.int32, (B_SPAN, B_SPAN), 0)
    j = lax.broadcasted_iota(jnp.int32, (B_SPAN, B_SPAN), 1)
    return i <= j, j <= i


def _dil_geom(dil, L):
    return B_SPAN * dil, L // (B_SPAN * dil), (1 if dil > 1 else 4)


def _dil_rows(r, dil):
    return pl.ds(r, B_SPAN, stride=dil) if dil > 1 else pl.ds(0, B_SPAN)


def _dil_loop(dil, step):
    if dil == 1:
        step(0, 0)
    else:
        lax.fori_loop(0, dil, step, 0)


def _dil_specs(dil, B, L):
    W, nb, hb = _dil_geom(dil, L)
    cw = A_HEAD * hb

    def at(col0, shift):
        def index(b, n, hh):
            return (b * nb + jnp.clip(n + shift, 0, nb - 1), col0 // cw + hh)
        return pl.BlockSpec((W, cw), index)

    return at


def _dil_fwd(qr, kr, z, dil, B, L, *, name):
    T = B * L
    W, nb, hb = _dil_geom(dil, L)
    has_prev = nb > 1
    scale = A_HEAD ** -0.5
    at = _dil_specs(dil, B, L)

    def body(*refs):
        if has_prev:
            q_ref, kc_ref, vc_ref, kp_ref, vp_ref, o_ref, l_ref = refs
        else:
            q_ref, kc_ref, vc_ref, o_ref, l_ref = refs
        mp, mc = _band_masks()
        mp = mp & (pl.program_id(1) > 0)

        def step(r, carry):
            rows = _dil_rows(r, dil)
            for h in range(hb):
                cols = pl.ds(h * A_HEAD, A_HEAD)
                qh = q_ref[rows, cols].astype(BF16)
                sc = jnp.where(mc, _dot(qh, kc_ref[rows, cols].astype(BF16), _NT) * scale, NEG_BIG)
                m = jnp.max(sc, axis=-1, keepdims=True)
                if has_prev:
                    sp = jnp.where(mp, _dot(qh, kp_ref[rows, cols].astype(BF16), _NT) * scale, NEG_BIG)
                    m = jnp.maximum(m, jnp.max(sp, axis=-1, keepdims=True))
                pc = jnp.exp(sc - m)
                l = jnp.sum(pc, axis=-1, keepdims=True)
                o = _dot(pc.astype(BF16), vc_ref[rows, cols].astype(BF16))
                if has_prev:
                    pp = jnp.exp(sp - m)
                    l = l + jnp.sum(pp, axis=-1, keepdims=True)
                    o = o + _dot(pp.astype(BF16), vp_ref[rows, cols].astype(BF16))
                o_ref[rows, cols] = o / l
                l_ref[rows, cols] = jnp.broadcast_to(m + jnp.log(l), (B_SPAN, A_HEAD))
            return carry

        _dil_loop(dil, step)

    ins = [qr, kr, z] + ([kr, z] if has_prev else [])
    in_specs = [at(0, 0), at(0, 0), at(3072, 0)] + ([at(0, -1), at(3072, -1)] if has_prev else [])
    return pl.pallas_call(
        body, name=name, grid=(B, nb, 4 // hb), in_specs=in_specs, out_specs=[at(0, 0), at(0, 0)],
        out_shape=[jax.ShapeDtypeStruct((T, 512), F32)] * 2,
        compiler_params=_cparams("parallel", "parallel", "parallel"),
    )(*ins)


def _dil_combine(os_, ls_, *, name):
    T = os_[0].shape[0]
    tm = 256

    def body(o1, o2, o3, l1, l2, l3, ob_ref, lse_ref):
        a1, a2, a3 = l1[...], l2[...], l3[...]
        m = jnp.maximum(jnp.maximum(a1, a2), a3)
        e1, e2, e3 = jnp.exp(a1 - m), jnp.exp(a2 - m), jnp.exp(a3 - m)
        den = e1 + e2 + e3
        ob_ref[...] = (e1 * o1[...] + e2 * o2[...] + e3 * o3[...]) / den
        lse_ref[...] = m + jnp.log(den)

    blk = pl.BlockSpec((tm, 512), lambda i: (i, 0))
    return pl.pallas_call(
        body, name=name, grid=(T // tm,), in_specs=[blk] * 6, out_specs=[blk, blk],
        out_shape=[jax.ShapeDtypeStruct((T, 512), F32)] * 2, compiler_params=_cparams("parallel"),
    )(*[o.reshape(T, 512) for o in os_], *[l.reshape(T, 512) for l in ls_])


def _dil_bwd_q(qr, kr, z, dymix, out, lse, dil, B, L, *, name):
    T = B * L
    W, nb, hb = _dil_geom(dil, L)
    has_prev = nb > 1
    scale = A_HEAD ** -0.5
    at = _dil_specs(dil, B, L)

    def body(*refs):
        if has_prev:
            q_ref, kc_ref, vc_ref, do_ref, out_ref, lse_ref, kp_ref, vp_ref, dq_ref = refs
        else:
            q_ref, kc_ref, vc_ref, do_ref, out_ref, lse_ref, dq_ref = refs
        mp, mc = _band_masks()
        mp = mp & (pl.program_id(1) > 0)

        def step(r, carry):
            rows = _dil_rows(r, dil)
            for h in range(hb):
                cols = pl.ds(h * A_HEAD, A_HEAD)
                qh = q_ref[rows, cols].astype(BF16)
                do = do_ref[rows, cols]
                delta = jnp.sum(do * out_ref[rows, cols], axis=-1, keepdims=True)
                dob = do.astype(BF16)
                lse_h = lse_ref[rows, cols]
                kc = kc_ref[rows, cols].astype(BF16)
                pc = jnp.where(mc, jnp.exp(_dot(qh, kc, _NT) * scale - lse_h), 0.0)
                dsc = pc * (_dot(dob, vc_ref[rows, cols].astype(BF16), _NT) - delta) * scale
                dq = _dot(dsc.astype(BF16), kc)
                if has_prev:
                    kp = kp_ref[rows, cols].astype(BF16)
                    pp = jnp.where(mp, jnp.exp(_dot(qh, kp, _NT) * scale - lse_h), 0.0)
                    dsp = pp * (_dot(dob, vp_ref[rows, cols].astype(BF16), _NT) - delta) * scale
                    dq = dq + _dot(dsp.astype(BF16), kp)
                dq_ref[rows, cols] = dq
            return carry

        _dil_loop(dil, step)

    ins = [qr, kr, z, dymix, out, lse] + ([kr, z] if has_prev else [])
    in_specs = ([at(0, 0), at(0, 0), at(3072, 0), at(512, 0), at(0, 0), at(0, 0)]
                + ([at(0, -1), at(3072, -1)] if has_prev else []))
    return pl.pallas_call(
        body, name=name, grid=(B, nb, 4 // hb), in_specs=in_specs, out_specs=at(0, 0),
        out_shape=jax.ShapeDtypeStruct((T, 512), F32),
        compiler_params=_cparams("parallel", "parallel", "parallel"),
    )(*ins)


def _dil_bwd_kv(qr, kr, z, dymix, out, lse, dil, B, L, *, name):
    T = B * L
    W, nb, hb = _dil_geom(dil, L)
    has_next = nb > 1
    scale = A_HEAD ** -0.5
    at = _dil_specs(dil, B, L)

    def body(*refs):
        k_ref, v_ref = refs[0], refs[1]
        own = refs[2:6]
        nxt = refs[6:10] if has_next else None
        dk_ref, dv_ref = refs[-2], refs[-1]
        mp, mc = _band_masks()
        mp = mp & (pl.program_id(1) < nb - 1)
        groups = [(own, mc)] + ([(nxt, mp)] if has_next else [])

        def step(r, carry):
            rows = _dil_rows(r, dil)
            for h in range(hb):
                cols = pl.ds(h * A_HEAD, A_HEAD)
                kh, vh = k_ref[rows, cols].astype(BF16), v_ref[rows, cols].astype(BF16)
                dk = jnp.zeros((B_SPAN, A_HEAD), F32)
                dv = jnp.zeros((B_SPAN, A_HEAD), F32)
                for (q_ref, do_ref, out_ref, lse_ref), mask in groups:
                    qh = q_ref[rows, cols].astype(BF16)
                    do = do_ref[rows, cols]
                    delta = jnp.sum(do * out_ref[rows, cols], axis=-1, keepdims=True)
                    dob = do.astype(BF16)
                    p = jnp.where(mask, jnp.exp(_dot(qh, kh, _NT) * scale - lse_ref[rows, cols]), 0.0)
                    dv = dv + _dot(p.astype(BF16), dob, _TN)
                    ds = p * (_dot(dob, vh, _NT) - delta) * scale
                    dk = dk + _dot(ds.astype(BF16), qh, _TN)
                dk_ref[rows, cols] = dk
                dv_ref[rows, cols] = dv
            return carry

        _dil_loop(dil, step)

    ins = [kr, z, qr, dymix, out, lse] + ([qr, dymix, out, lse] if has_next else [])
    in_specs = ([at(0, 0), at(3072, 0), at(0, 0), at(512, 0), at(0, 0), at(0, 0)]
                + ([at(0, 1), at(512, 1), at(0, 1), at(0, 1)] if has_next else []))
    return pl.pallas_call(
        body, name=name, grid=(B, nb, 4 // hb), in_specs=in_specs, out_specs=[at(0, 0), at(0, 0)],
        out_shape=[jax.ShapeDtypeStruct((T, 512), F32)] * 2,
        compiler_params=_cparams("parallel", "parallel", "parallel"),
    )(*ins)


def _s5_build(lam_re, lam_im, log_dt, b_re, b_im, c_re, c_im):
    G, P, TC = C_GROUPS, C_STATE, C_TC
    lr = jnp.minimum(lam_re, C_MIN_NEG_RE)
    li = lam_im
    dt = jnp.exp(log_dt)[:, None]
    mag = jnp.exp(dt * lr)
    ar, ai = mag * jnp.cos(dt * li), mag * jnp.sin(dt * li)
    den = lr * lr + li * li
    zr = ((ar - 1.0) * lr + ai * li) / den
    zi = (ai * lr - (ar - 1.0) * li) / den
    bbr = zr[..., None] * b_re - zi[..., None] * b_im
    bbi = zr[..., None] * b_im + zi[..., None] * b_re
    ks = jnp.arange(TC + 1, dtype=F32)[:, None, None]
    pmag = jnp.exp(ks * (dt * lr)[None])
    pr, pi = pmag * jnp.cos(ks * (dt * li)[None]), pmag * jnp.sin(ks * (dt * li)[None])
    car = c_re[None] * pr[:, :, None, :] - c_im[None] * pi[:, :, None, :]
    cai = c_re[None] * pi[:, :, None, :] + c_im[None] * pr[:, :, None, :]
    kern = (jnp.einsum('lgop,gpc->lgco', car[:TC], bbr, precision=HI)
            - jnp.einsum('lgop,gpc->lgco', cai[:TC], bbi, precision=HI))
    s_idx = jnp.arange(TC)[:, None]
    t_idx = jnp.arange(TC)[None, :]
    lag = t_idx - s_idx
    ksel = jnp.where((lag >= 0)[:, :, None, None, None], kern[jnp.clip(lag, 0, TC - 1)], 0.0)
    m_mat = ksel.transpose(2, 0, 3, 1, 4).reshape(G, TC * C_GROUP, TC * C_GROUP)
    pr_e, pi_e = pr[TC - 1 - jnp.arange(TC)], pi[TC - 1 - jnp.arange(TC)]
    er = pr_e[:, :, :, None] * bbr[None] - pi_e[:, :, :, None] * bbi[None]
    ei = pr_e[:, :, :, None] * bbi[None] + pi_e[:, :, :, None] * bbr[None]
    e_mat = jnp.concatenate([er.transpose(1, 0, 3, 2), ei.transpose(1, 0, 3, 2)], axis=-1).reshape(G, TC * C_GROUP, 2 * P)
    fr = car[1:].transpose(1, 3, 0, 2)
    fi = -cai[1:].transpose(1, 3, 0, 2)
    f_mat = jnp.concatenate([fr, fi], axis=1).reshape(G, 2 * P, TC * C_GROUP)
    return m_mat, e_mat, f_mat, pr[TC], pi[TC]


def _s5_scan_tables(lam_re, lam_im, log_dt, nsteps):
    lr = jnp.minimum(lam_re, C_MIN_NEG_RE)
    dt = jnp.exp(log_dt)[:, None]
    ks = (C_TC * 2.0 ** jnp.arange(8, dtype=F32))[None, :, None]
    keep = (jnp.arange(8) < nsteps)[None, :, None]
    pmag = jnp.exp(ks * (dt * lr)[:, None, :])
    ang = ks * (dt * lam_im)[:, None, :]
    pr = jnp.where(keep, pmag * jnp.cos(ang), 0.0)
    pi = jnp.where(keep, pmag * jnp.sin(ang), 0.0)
    return jnp.concatenate([pr, pr], axis=-1), jnp.concatenate([-pi, pi], axis=-1)


def _s5_fwd(uf, m_mat, e_mat, f_mat, tab_r, tab_i, nch, *, name):
    G, R, _ = uf.shape
    nsteps = int(math.log2(nch))

    def body(u_ref, m_ref, e_ref, f_ref, tr_ref, ti_ref, y_ref, xs_ref):
        u = u_ref[0].astype(BF16)
        x = _dot(u, e_ref[0].astype(BF16))
        pos = jnp.bitwise_and(lax.broadcasted_iota(jnp.int32, (R, 2 * C_STATE), 0), nch - 1)
        for k in range(nsteps):
            s = 1 << k
            sh = pltpu.roll(x, s, 0)
            upd = tr_ref[0, k:k + 1, :] * sh + ti_ref[0, k:k + 1, :] * pltpu.roll(sh, C_STATE, 1)
            x = x + jnp.where(pos >= s, upd, 0.0)
        xs = jnp.where(pos >= 1, pltpu.roll(x, 1, 0), 0.0)
        xs_ref[0] = xs
        y_ref[0] = _dot(u, m_ref[0].astype(BF16)) + _dot(xs.astype(BF16), f_ref[0].astype(BF16))

    def g3(shape):
        return pl.BlockSpec((1,) + shape, lambda g: (g, 0, 0))

    return pl.pallas_call(
        body, name=name, grid=(G,),
        in_specs=[g3((R, 256)), g3((256, 256)), g3((256, 128)), g3((128, 256)), g3((8, 128)), g3((8, 128))],
        out_specs=[g3((R, 256)), g3((R, 128))],
        out_shape=[jax.ShapeDtypeStruct((G, R, 256), F32), jax.ShapeDtypeStruct((G, R, 128), F32)],
        compiler_params=_cparams("parallel"),
    )(uf, m_mat, e_mat, f_mat, tab_r, tab_i)


def _s5_bwd(dy, uf, xs, m_mat, e_mat, f_mat, tab_r, tab_i, nch, *, name):
    G, R, _ = uf.shape
    nsteps = int(math.log2(nch))

    def body(dy_ref, u_ref, xs_ref, m_ref, e_ref, f_ref, tr_ref, ti_ref, du_ref, dm_ref, de_ref, df_ref, da_ref):
        dyb = dy_ref[0].astype(BF16)
        u = u_ref[0].astype(BF16)
        xs = xs_ref[0]
        dm_ref[0] = _dot(u, dyb, _TN)
        df_ref[0] = _dot(xs.astype(BF16), dyb, _TN)
        gx = _dot(dyb, f_ref[0].astype(BF16), _NT)
        pos = jnp.bitwise_and(lax.broadcasted_iota(jnp.int32, (R, 2 * C_STATE), 0), nch - 1)
        for k in range(nsteps):
            s = 1 << k
            sh = pltpu.roll(gx, R - s, 0)
            upd = tr_ref[0, k:k + 1, :] * sh - ti_ref[0, k:k + 1, :] * pltpu.roll(sh, C_STATE, 1)
            gx = gx + jnp.where(pos + s < nch, upd, 0.0)
        de_in = jnp.where(pos + 1 < nch, pltpu.roll(gx, R - 1, 0), 0.0)
        deb = de_in.astype(BF16)
        de_ref[0] = _dot(u, deb, _TN)
        du_ref[0] = _dot(dyb, m_ref[0].astype(BF16), _NT) + _dot(deb, e_ref[0].astype(BF16), _NT)
        da_ref[0] = jnp.zeros((8, 128), F32)
        da_ref[0, 0:1, :] = jnp.sum(de_in * xs, axis=0, keepdims=True)
        da_ref[0, 1:2, :] = jnp.sum(de_in * pltpu.roll(xs, C_STATE, 1), axis=0, keepdims=True)

    def g3(shape):
        return pl.BlockSpec((1,) + shape, lambda g: (g, 0, 0))

    return pl.pallas_call(
        body, name=name, grid=(G,),
        in_specs=[g3((R, 256)), g3((R, 256)), g3((R, 128)), g3((256, 256)), g3((256, 128)), g3((128, 256)),
                  g3((8, 128)), g3((8, 128))],
        out_specs=[g3((R, 256)), g3((256, 256)), g3((256, 128)), g3((128, 256)), g3((8, 128))],
        out_shape=[jax.ShapeDtypeStruct((G, R, 256), F32), jax.ShapeDtypeStruct((G, 256, 256), F32),
                   jax.ShapeDtypeStruct((G, 256, 128), F32), jax.ShapeDtypeStruct((G, 128, 256), F32),
                   jax.ShapeDtypeStruct((G, 8, 128), F32)],
        compiler_params=_cparams("parallel"),
    )(dy, uf, xs, m_mat, e_mat, f_mat, tab_r, tab_i)


def _s5_act_fwd(ys, u, dsk, *, name):
    T, C = u.shape
    tm = 256

    def body(ys_ref, u_ref, d_ref, gl_ref, y_ref):
        y = ys_ref[...] + d_ref[...] * u_ref[...]
        y_ref[...] = y
        gl_ref[...] = (0.5 * y * (1.0 + _erf(y * (2.0 ** -0.5)))).astype(BF16)

    row = pl.BlockSpec((tm, C), lambda i: (i, 0))
    vec = pl.BlockSpec((1, C), lambda i: (0, 0))
    return pl.pallas_call(
        body, name=name, grid=(T // tm,), in_specs=[row, row, vec], out_specs=[row, row],
        out_shape=[jax.ShapeDtypeStruct((T, C), BF16), jax.ShapeDtypeStruct((T, C), F32)],
        compiler_params=_cparams("parallel"),
    )(ys, u, dsk)


def _s5_act_bwd(y, u, dsk, dgl, *, name):
    T, C = u.shape
    tm = 256

    def body(y_ref, u_ref, d_ref, dgl_ref, dy_ref, du_ref, dd_ref):
        y = y_ref[...]
        cdf = 0.5 * (1.0 + _erf(y * (2.0 ** -0.5)))
        pdf = jnp.exp(-0.5 * y * y) * (1.0 / math.sqrt(2.0 * math.pi))
        dy = dgl_ref[...].astype(F32) * (cdf + y * pdf)
        dy_ref[...] = dy
        du_ref[...] = dy * d_ref[...]
        part = jnp.sum(dy * u_ref[...], axis=0, keepdims=True)

        @pl.when(pl.program_id(0) == 0)
        def _():
            dd_ref[...] = part

        @pl.when(pl.program_id(0) > 0)
        def _():
            dd_ref[...] += part

    row = pl.BlockSpec((tm, C), lambda i: (i, 0))
    vec = pl.BlockSpec((1, C), lambda i: (0, 0))
    return pl.pallas_call(
        body, name=name, grid=(T // tm,), in_specs=[row, row, vec, row], out_specs=[row, row, vec],
        out_shape=[jax.ShapeDtypeStruct((T, C), F32), jax.ShapeDtypeStruct((T, C), F32), jax.ShapeDtypeStruct((1, C), F32)],
        compiler_params=_cparams("arbitrary"),
    )(y, u, dsk, dgl)


def _add2(a, b, *, name):
    T, C = a.shape
    tm = 256

    def body(a_ref, b_ref, o_ref):
        o_ref[...] = a_ref[...] + b_ref[...]

    row = pl.BlockSpec((tm, C), lambda i: (i, 0))
    return pl.pallas_call(body, name=name, grid=(T // tm,), in_specs=[row, row], out_specs=row,
                          out_shape=jax.ShapeDtypeStruct((T, C), F32), compiler_params=_cparams("parallel"))(a, b)


def _loss_head(y, target, *, name):
    T, C = y.shape
    tm = 256

    def body(y_ref, t_ref, l_ref, d_ref):
        err = y_ref[...] - t_ref[...]
        d_ref[...] = err * (1.0 / C)
        sq = err * err
        part = jnp.zeros((8, LANES), F32)
        for r in range(0, tm, 8):
            for c in range(0, C, LANES):
                part = part + sq[r:r + 8, c:c + LANES]

        @pl.when(pl.program_id(0) == 0)
        def _():
            l_ref[...] = part

        @pl.when(pl.program_id(0) > 0)
        def _():
            l_ref[...] += part

    row = pl.BlockSpec((tm, C), lambda i: (i, 0))
    acc = pl.BlockSpec((8, LANES), lambda i: (0, 0))
    return pl.pallas_call(
        body, name=name, grid=(T // tm,), in_specs=[row, row], out_specs=[acc, row],
        out_shape=[jax.ShapeDtypeStruct((8, LANES), F32), jax.ShapeDtypeStruct((T, C), F32)],
        compiler_params=_cparams("arbitrary"),
    )(y, target)


def _adamw(w, g, m, v, *, name):
    shape = w.shape
    size = int(np.prod(shape))
    cols = LANES if (shape[-1] < LANES and size % LANES == 0) else shape[-1]
    rows = size // cols
    tm = _tile(rows, 256) if rows % 8 == 0 else rows
    w2, g2, m2, v2 = (t.reshape(rows, cols) for t in (w, g, m, v))

    def body(w_ref, g_ref, m_ref, v_ref, d_ref, nm_ref, nv_ref):
        gg = g_ref[...]
        nm = ADAM_B1 * m_ref[...] + (1.0 - ADAM_B1) * gg
        nv = ADAM_B2 * v_ref[...] + (1.0 - ADAM_B2) * (gg * gg)
        m_hat = nm / (1.0 - ADAM_B1 ** ADAM_STEP)
        v_hat = nv / (1.0 - ADAM_B2 ** ADAM_STEP)
        d_ref[...] = -ADAM_LR * (m_hat / (jnp.sqrt(v_hat) + ADAM_EPS) + ADAM_WD * w_ref[...])
        nm_ref[...] = nm
        nv_ref[...] = nv

    blk = pl.BlockSpec((tm, cols), lambda i: (i, 0))
    outs = pl.pallas_call(
        body, name=name, grid=(rows // tm,), in_specs=[blk] * 4, out_specs=[blk] * 3,
        out_shape=[jax.ShapeDtypeStruct((rows, cols), F32)] * 3, compiler_params=_cparams("parallel"),
    )(w2, g2, m2, v2)
    return tuple(o.reshape(shape) for o in outs)


_HBM = pl.BlockSpec(memory_space=pltpu.HBM)
_MESH = pl.DeviceIdType.MESH


def _logical(px, py, pc):
    return 4 * px + 2 * py + pc


def _all_gather(shard, *, name):
    R, C = shard.shape

    def body(x_ref, out_ref, send_sems, recv_sems, local_sem):
        x, y, c = lax.axis_index("x"), lax.axis_index("y"), lax.axis_index("c")
        me, sibling = (x, y, c), (x, y, 1 - c)
        chips = [(1 - x, y), (x, 1 - y), (1 - x, 1 - y)]

        def rows(px, py, pc):
            return out_ref.at[_logical(px, py, pc)]

        def copy(k, block, to, src=None):
            return pltpu.make_async_remote_copy(
                src_ref=rows(*block) if src is None else src, dst_ref=rows(*block),
                send_sem=send_sems.at[k], recv_sem=recv_sems.at[k], device_id=to, device_id_type=_MESH)

        mine = pltpu.make_async_copy(x_ref, rows(*me), local_sem)
        mine.start()
        first = [copy(0, me, sibling, src=x_ref)]
        first += [copy(1 + j, me, (*chip, c), src=x_ref) for j, chip in enumerate(chips)]
        for cp in first:
            cp.start()
        passed = [copy(4 + j, (*chip, c), sibling) for j, chip in enumerate(chips)]
        for j, chip in enumerate(chips):
            copy(1 + j, (*chip, c), me).wait_recv()
            passed[j].start()
        copy(0, sibling, me).wait_recv()
        for j, chip in enumerate(chips):
            copy(4 + j, (*chip, 1 - c), me).wait_recv()
        for cp in first + passed:
            cp.wait_send()
        mine.wait()

    return pl.pallas_call(
        body, name=name, out_shape=jax.ShapeDtypeStruct((N_DEV, R, C), shard.dtype),
        in_specs=[_HBM], out_specs=_HBM,
        scratch_shapes=[pltpu.SemaphoreType.DMA((7,)), pltpu.SemaphoreType.DMA((7,)), pltpu.SemaphoreType.DMA],
    )(shard)


def _gather_weights(shards, *, name):
    nt = len(shards)

    def body(*refs):
        ins, outs = refs[:nt], refs[nt:2 * nt]
        send_sems, recv_sems, local_sems = refs[2 * nt:]
        x, y, c = lax.axis_index("x"), lax.axis_index("y"), lax.axis_index("c")
        me, sibling = (x, y, c), (x, y, 1 - c)
        chips = [(1 - x, y), (x, 1 - y), (1 - x, 1 - y)]

        def copy(t, k, block, to, src=None):
            rows = outs[t].at[_logical(*block)]
            return pltpu.make_async_remote_copy(
                src_ref=rows if src is None else src, dst_ref=rows,
                send_sem=send_sems.at[t, k], recv_sem=recv_sems.at[t, k], device_id=to, device_id_type=_MESH)

        mine = [pltpu.make_async_copy(ins[t], outs[t].at[_logical(*me)], local_sems.at[t]) for t in range(nt)]
        for cp in mine:
            cp.start()
        started = []
        for t in range(nt):
            started.append(copy(t, 0, me, sibling, src=ins[t]))
            started += [copy(t, 1 + j, me, (*chip, c), src=ins[t]) for j, chip in enumerate(chips)]
        for cp in started:
            cp.start()
        for j, chip in enumerate(chips):
            for t in range(nt):
                copy(t, 1 + j, (*chip, c), me).wait_recv()
                fwd = copy(t, 4 + j, (*chip, c), sibling)
                fwd.start()
                started.append(fwd)
        for t in range(nt):
            copy(t, 0, sibling, me).wait_recv()
            for j, chip in enumerate(chips):
                copy(t, 4 + j, (*chip, 1 - c), me).wait_recv()
        for cp in started:
            cp.wait_send()
        for cp in mine:
            cp.wait()

    return pl.pallas_call(
        body, name=name, out_shape=[jax.ShapeDtypeStruct((N_DEV,) + s.shape, s.dtype) for s in shards],
        in_specs=[_HBM] * nt, out_specs=[_HBM] * nt,
        scratch_shapes=[pltpu.SemaphoreType.DMA((nt, 7)), pltpu.SemaphoreType.DMA((nt, 7)), pltpu.SemaphoreType.DMA((nt,))],
    )(*shards)


def _scatter_grads(grads, *, name):
    nt = len(grads)

    def body(*refs):
        ins, outs = refs[:nt], refs[nt:2 * nt]
        send_sems, recv_sems, local_sems = refs[2 * nt:]
        x, y, c = lax.axis_index("x"), lax.axis_index("y"), lax.axis_index("c")
        me = _logical(x, y, c)
        mine = [pltpu.make_async_copy(ins[t].at[me], outs[t].at[me], local_sems.at[t]) for t in range(nt)]
        for cp in mine:
            cp.start()
        peers = [(x ^ (k >> 2), y ^ ((k >> 1) & 1), c ^ (k & 1)) for k in range(1, N_DEV)]
        copies = []
        for t in range(nt):
            for k, peer in enumerate(peers):
                copies.append(pltpu.make_async_remote_copy(
                    src_ref=ins[t].at[_logical(*peer)], dst_ref=outs[t].at[me],
                    send_sem=send_sems.at[t, k], recv_sem=recv_sems.at[t, k], device_id=peer, device_id_type=_MESH))
        for cp in copies:
            cp.start()
        for t in range(nt):
            for k, peer in enumerate(peers):
                pltpu.make_async_remote_copy(
                    src_ref=ins[t].at[me], dst_ref=outs[t].at[_logical(*peer)],
                    send_sem=send_sems.at[t, k], recv_sem=recv_sems.at[t, k], device_id=peer, device_id_type=_MESH).wait_recv()
        for cp in copies:
            cp.wait_send()
        for cp in mine:
            cp.wait()

    return pl.pallas_call(
        body, name=name, out_shape=[jax.ShapeDtypeStruct(g.shape, g.dtype) for g in grads],
        in_specs=[_HBM] * nt, out_specs=[_HBM] * nt,
        scratch_shapes=[pltpu.SemaphoreType.DMA((nt, 7)), pltpu.SemaphoreType.DMA((nt, 7)), pltpu.SemaphoreType.DMA((nt,))],
    )(*grads)


def _sum_rows(stacked, *, name):
    _, R, C = stacked.shape
    tr = R
    if N_DEV * R * C * stacked.dtype.itemsize > 12 * 1024 * 1024:
        for cand in range(512, 15, -16):
            if R % cand == 0:
                tr = cand
                break

    def body(s_ref, o_ref):
        acc = s_ref[0].astype(F32)
        for k in range(1, N_DEV):
            acc = acc + s_ref[k].astype(F32)
        o_ref[...] = acc

    return pl.pallas_call(
        body, name=name, grid=(R // tr,),
        in_specs=[pl.BlockSpec((N_DEV, tr, C), lambda i: (0, i, 0))], out_specs=pl.BlockSpec((tr, C), lambda i: (i, 0)),
        out_shape=jax.ShapeDtypeStruct((R, C), F32), compiler_params=_cparams("parallel"),
    )(stacked)


_LARGE = (("ab_w_in", 1, True), ("ab_w_out", 1, False), ("s5_w_glu", 1, True), ("xattn_wq", 2, False),
          ("xattn_wkv", 2, True), ("xattn_wo", 2, False), ("ffn_w_in", 2, True), ("ffn_w_out", 2, False))
_LARGE_KEYS = tuple((n, l) for n, layers, _ in _LARGE for l in range(layers))
_TRANSPOSED = {n: t for n, _, t in _LARGE}


def _owner_major(name, w):
    return w.T if _TRANSPOSED[name] else w


def _lb_from_logits(logits):
    return jnp.cumsum(jax.nn.softmax(logits, axis=0), axis=0)[0:1]


def _to_groups(t, T):
    return t.reshape(T // C_TC, C_TC, C_GROUPS, C_GROUP).transpose(2, 0, 1, 3).reshape(C_GROUPS, T // C_TC, C_TC * C_GROUP)


def _from_groups(t, T):
    return t.reshape(C_GROUPS, T // C_TC, C_TC, C_GROUP).transpose(1, 2, 0, 3).reshape(T, D_MODEL)


def _local_step(x, mem, target, W):
    B, L, _ = x.shape
    T = B * L
    x0 = x.reshape(T, D_MODEL)
    memf = mem.reshape(B * MEM_LEN, D_MODEL)
    nw = W["norm_w"]
    cos2, sin2 = _rope_tables(L)
    G = {}
    w_ffn_in = [_interleave_rows(W["ffn_w_in", l]) for l in range(2)]
    w_glu = _interleave_rows(W["s5_w_glu", 0])

    def vec(v):
        return v.reshape(1, -1)

    saved = []
    xin = x0
    for layer in range(2):
        s = {"x0": xin}
        tag = f"l{layer}"
        h1 = _rms_fwd(xin, vec(nw[layer, 0]), name=f"norm_pre_mix_{tag}", out_dtype=BF16 if layer == 0 else F32)
        s["h1"] = h1
        if layer == 0:
            lb, lb_vjp = jax.vjp(_lb_from_logits, W["hgrn_lb_logits"])
            onw = W["hgrn_out_norm_w"].reshape(1, A_WIDTH)
            z = _mm(h1, W["ab_w_in", 0], tb=True, name="ab_in")
            oa, o_raw, s_start = _hgrn_fwd(z, lb, onw, B, L, name="hgrn_fwd")
            qr = _rope_fwd(z, cos2, sin2, 16, B, L, name="rope_q")
            kr = _rope_fwd(z, cos2, sin2, 20, B, L, name="rope_k")
            os_, ls_ = [], []
            for dil in B_DILS:
                o_g, l_g = _dil_fwd(qr, kr, z, dil, B, L, name=f"dil_fwd_{dil}")
                os_.append(o_g)
                ls_.append(l_g)
            ob, lse = _dil_combine(os_, ls_, name="dil_combine")
            ymix = jnp.concatenate([oa, ob], axis=-1).astype(BF16)
            y1 = _mm(ymix, W["ab_w_out", 0], name="ab_out")
            s.update(z=z, lb=lb, lb_vjp=lb_vjp, onw=onw, o_raw=o_raw, s_start=s_start, qr=qr, kr=kr, ob=ob, lse=lse, ymix=ymix)
        else:
            p5 = tuple(W[n][0] for n in ("s5_lambda_re", "s5_lambda_im", "s5_log_dt", "s5_b_re", "s5_b_im", "s5_c_re", "s5_c_im"))
            (m_mat, e_mat, f_mat, a16r, a16i), s5_vjp = jax.vjp(_s5_build, *p5)
            nch = L // C_TC
            tab_r, tab_i = _s5_scan_tables(p5[0], p5[1], p5[2], int(math.log2(nch)))
            uf = _to_groups(h1, T).astype(BF16)
            yf, xs = _s5_fwd(uf, m_mat, e_mat, f_mat, tab_r, tab_i, nch, name="s5_fwd")
            dsk = W["s5_d"].reshape(1, D_MODEL)
            gl, ypre = _s5_act_fwd(_from_groups(yf, T), h1, dsk, name="s5_act_fwd")
            zg = _mm(gl, w_glu, tb=True, out_dtype=BF16, name="s5_glu_in")
            y1 = _gated_fwd(zg, "glu", name="s5_glu")
            s.update(s5_vjp=s5_vjp, mats=(m_mat, e_mat, f_mat, tab_r, tab_i), uf=uf, xs=xs, dsk=dsk, gl=gl, ypre=ypre, zg=zg, nch=nch)
        x1 = _rms_fwd(y1, vec(nw[layer, 1]), xin, name=f"norm_post_mix_{tag}")
        h2 = _rms_fwd(x1, vec(nw[layer, 2]), name=f"norm_pre_x_{tag}", out_dtype=BF16)
        memn = _rms_fwd(memf, vec(W["mem_norm_w"][layer]), name=f"norm_mem_{tag}", out_dtype=BF16)
        q = _mm(h2, W["xattn_wq", layer], out_dtype=BF16, name=f"x_q_{tag}")
        kv = _mm(memn, W["xattn_wkv", layer], tb=True, out_dtype=BF16, name=f"x_kv_{tag}")
        o = _xattn_fwd(q, kv, B, L, name=f"x_attn_{tag}")
        y2 = _mm(o, W["xattn_wo", layer], name=f"x_o_{tag}")
        x2 = _rms_fwd(y2, vec(nw[layer, 3]), x1, name=f"norm_post_x_{tag}")
        h3 = _rms_fwd(x2, vec(nw[layer, 4]), name=f"norm_pre_ffn_{tag}", out_dtype=BF16)
        zf = _mm(h3, w_ffn_in[layer], tb=True, out_dtype=BF16, name=f"ffn_in_{tag}")
        u = _gated_fwd(zf, "swiglu", name=f"ffn_act_{tag}")
        y3 = _mm(u, W["ffn_w_out", layer], name=f"ffn_out_{tag}")
        x3 = _rms_fwd(y3, vec(nw[layer, 5]), x2, name=f"norm_post_ffn_{tag}")
        s.update(y1=y1, x1=x1, h2=h2, memn=memn, q=q, kv=kv, o=o, y2=y2, x2=x2, h3=h3, zf=zf, u=u, y3=y3)
        saved.append(s)
        xin = x3

    loss_parts, dx = _loss_head(xin, target.reshape(T, D_MODEL), name="loss_head")

    d_norm = [[None] * 6 for _ in range(2)]
    d_memn = [None, None]
    for layer in (1, 0):
        s = saved[layer]
        tag = f"l{layer}"
        dy3, d_norm[layer][5] = _rms_bwd(s["y3"], vec(nw[layer, 5]), dx, name=f"bnorm_post_ffn_{tag}", out_dtype=BF16)
        du = _mm(dy3, W["ffn_w_out", layer], tb=True, out_dtype=BF16, name=f"b_ffn_out_dx_{tag}")
        G["ffn_w_out", layer] = _mm(s["u"], dy3, ta=True, out_dtype=BF16, name=f"b_ffn_out_dw_{tag}")
        dzf = _gated_bwd(s["zf"], du, "swiglu", name=f"b_ffn_act_{tag}")
        G["ffn_w_in", layer] = _interleave_rows(
            _mm(dzf, s["h3"], ta=True, out_dtype=BF16, name=f"b_ffn_in_dw_{tag}"), inverse=True)
        dh3 = _mm(dzf, w_ffn_in[layer], out_dtype=BF16, name=f"b_ffn_in_dx_{tag}")
        dx, d_norm[layer][4] = _rms_bwd(s["x2"], vec(nw[layer, 4]), dh3, dx, name=f"bnorm_pre_ffn_{tag}")
        dy2, d_norm[layer][3] = _rms_bwd(s["y2"], vec(nw[layer, 3]), dx, name=f"bnorm_post_x_{tag}", out_dtype=BF16)
        do = _mm(dy2, W["xattn_wo", layer], tb=True, out_dtype=BF16, name=f"b_x_o_dx_{tag}")
        G["xattn_wo", layer] = _mm(s["o"], dy2, ta=True, out_dtype=BF16, name=f"b_x_o_dw_{tag}")
        dq, dkv = _xattn_bwd(s["q"], s["kv"], do, B, L, name=f"b_x_attn_{tag}")
        G["xattn_wq", layer] = _mm(s["h2"], dq, ta=True, out_dtype=BF16, name=f"b_x_q_dw_{tag}")
        dh2 = _mm(dq, W["xattn_wq", layer], tb=True, out_dtype=BF16, name=f"b_x_q_dx_{tag}")
        G["xattn_wkv", layer] = _mm(dkv, s["memn"], ta=True, out_dtype=BF16, name=f"b_x_kv_dw_{tag}")
        dmemn = _mm(dkv, W["xattn_wkv", layer], out_dtype=BF16, name=f"b_x_kv_dx_{tag}")
        _, d_memn[layer] = _rms_bwd(memf, vec(W["mem_norm_w"][layer]), dmemn, name=f"bnorm_mem_{tag}", out_dtype=BF16)
        dx, d_norm[layer][2] = _rms_bwd(s["x1"], vec(nw[layer, 2]), dh2, dx, name=f"bnorm_pre_x_{tag}")
        dy1, d_norm[layer][1] = _rms_bwd(s["y1"], vec(nw[layer, 1]), dx, name=f"bnorm_post_mix_{tag}", out_dtype=BF16)
        if layer == 0:
            z = s["z"]
            dymix = _mm(dy1, W["ab_w_out", 0], tb=True, name="b_ab_out_dx")
            G["ab_w_out", 0] = _mm(s["ymix"], dy1, ta=True, out_dtype=BF16, name="b_ab_out_dw")
            dqa, dfa, dia, dga, d_onw, d_lb = _hgrn_bwd(z, s["lb"], s["onw"], s["o_raw"], s["s_start"], dymix, B, L, name="hgrn_bwd")
            dqs, dks, dvs = [], [], []
            for dil in B_DILS:
                dqs.append(_dil_bwd_q(s["qr"], s["kr"], z, dymix, s["ob"], s["lse"], dil, B, L, name=f"dil_bwd_q_{dil}"))
                dk_g, dv_g = _dil_bwd_kv(s["qr"], s["kr"], z, dymix, s["ob"], s["lse"], dil, B, L, name=f"dil_bwd_kv_{dil}")
                dks.append(dk_g)
                dvs.append(dv_g)
            dqb = _sum3(*dqs, cos2, sin2, True, B, L, name="b_rope_q")
            dkb = _sum3(*dks, cos2, sin2, True, B, L, name="b_rope_k")
            dvb = _sum3(*dvs, cos2, sin2, False, B, L, name="b_sum_v")
            dz = jnp.concatenate([dqa, dfa, dia, dga, dqb, dkb, dvb], axis=-1).astype(BF16)
            G["ab_w_in", 0] = _mm(dz, s["h1"], ta=True, out_dtype=BF16, name="b_ab_in_dw")
            dh1 = _mm(dz, W["ab_w_in", 0], out_dtype=BF16, name="b_ab_in_dx")
            G["hgrn_out_norm_w"] = jnp.sum(d_onw.reshape(B, A_WIDTH), axis=0, keepdims=True)
            d_lb_row = jnp.sum(d_lb.reshape(B, A_WIDTH), axis=0, keepdims=True)
            G["hgrn_lb_logits"] = s["lb_vjp"](d_lb_row)[0]
        else:
            dzg = _gated_bwd(s["zg"], dy1, "glu", name="b_s5_glu")
            G["s5_w_glu", 0] = _interleave_rows(
                _mm(dzg, s["gl"], ta=True, out_dtype=BF16, name="b_s5_glu_dw"), inverse=True)
            dgl = _mm(dzg, w_glu, out_dtype=BF16, name="b_s5_glu_dx")
            dyp, du_skip, d_dsk = _s5_act_bwd(s["ypre"], s["h1"], s["dsk"], dgl, name="b_s5_act")
            m_mat, e_mat, f_mat, tab_r, tab_i = s["mats"]
            duf, dm, de, df_, da = _s5_bwd(_to_groups(dyp, T).astype(BF16), s["uf"], s["xs"], m_mat, e_mat, f_mat, tab_r, tab_i,
                                           s["nch"], name="s5_bwd")
            da_r = da[:, 0, :C_STATE] + da[:, 0, C_STATE:]
            da_i = da[:, 1, C_STATE:] - da[:, 1, :C_STATE]
            gp = s["s5_vjp"]((dm, de, df_, da_r, da_i))
            for n, gv in zip(("s5_lambda_re", "s5_lambda_im", "s5_log_dt", "s5_b_re", "s5_b_im", "s5_c_re", "s5_c_im"), gp):
                G[n] = gv[None]
            G["s5_d"] = d_dsk
            dh1 = _add2(_from_groups(duf, T), du_skip, name="b_s5_du")
        dx, d_norm[layer][0] = _rms_bwd(s["x0"], vec(nw[layer, 0]), dh1, dx, name=f"bnorm_pre_mix_{tag}")

    G["norm_w"] = jnp.stack([jnp.concatenate(d_norm[l], axis=0) for l in range(2)])
    G["mem_norm_w"] = jnp.concatenate(d_memn, axis=0)
    return loss_parts, dx.reshape(B, L, D_MODEL), G


_SMALL = (("norm_w", (2, 6, 1024)), ("mem_norm_w", (2, 1024)), ("hgrn_lb_logits", (3, 512)), ("hgrn_out_norm_w", (1, 512)),
          ("s5_lambda_re", (1, 64, 64)), ("s5_lambda_im", (1, 64, 64)), ("s5_log_dt", (1, 64)),
          ("s5_b_re", (1, 64, 64, 16)), ("s5_b_im", (1, 64, 64, 16)), ("s5_c_re", (1, 64, 16, 64)),
          ("s5_c_im", (1, 64, 16, 64)), ("s5_d", (1, 1024)))

_WEIGHT_ORDER = ('norm_w', 'mem_norm_w', 'ab_w_in', 'ab_w_out', 'hgrn_lb_logits', 'hgrn_out_norm_w', 's5_lambda_re',
                 's5_lambda_im', 's5_log_dt', 's5_b_re', 's5_b_im', 's5_c_re', 's5_c_im', 's5_d', 's5_w_glu', 'xattn_wq',
                 'xattn_wkv', 'xattn_wo', 'ffn_w_in', 'ffn_w_out')


def kernel(x, mem, norm_w, mem_norm_w, ab_w_in, ab_w_out, hgrn_lb_logits, hgrn_out_norm_w, s5_lambda_re, s5_lambda_im, s5_log_dt, s5_b_re, s5_b_im, s5_c_re, s5_c_im, s5_d, s5_w_glu, xattn_wq, xattn_wkv, xattn_wo, ffn_w_in, ffn_w_out, loss_target, m_norm_w, m_mem_norm_w, m_ab_w_in, m_ab_w_out, m_hgrn_lb_logits, m_hgrn_out_norm_w, m_s5_lambda_re, m_s5_lambda_im, m_s5_log_dt, m_s5_b_re, m_s5_b_im, m_s5_c_re, m_s5_c_im, m_s5_d, m_s5_w_glu, m_xattn_wq, m_xattn_wkv, m_xattn_wo, m_ffn_w_in, m_ffn_w_out, v_norm_w, v_mem_norm_w, v_ab_w_in, v_ab_w_out, v_hgrn_lb_logits, v_hgrn_out_norm_w, v_s5_lambda_re, v_s5_lambda_im, v_s5_log_dt, v_s5_b_re, v_s5_b_im, v_s5_c_re, v_s5_c_im, v_s5_d, v_s5_w_glu, v_xattn_wq, v_xattn_wkv, v_xattn_wo, v_ffn_w_in, v_ffn_w_out):
    local = dict(norm_w=norm_w, mem_norm_w=mem_norm_w, ab_w_in=ab_w_in, ab_w_out=ab_w_out, hgrn_lb_logits=hgrn_lb_logits,
                 hgrn_out_norm_w=hgrn_out_norm_w, s5_lambda_re=s5_lambda_re, s5_lambda_im=s5_lambda_im, s5_log_dt=s5_log_dt,
                 s5_b_re=s5_b_re, s5_b_im=s5_b_im, s5_c_re=s5_c_re, s5_c_im=s5_c_im, s5_d=s5_d, s5_w_glu=s5_w_glu,
                 xattn_wq=xattn_wq, xattn_wkv=xattn_wkv, xattn_wo=xattn_wo, ffn_w_in=ffn_w_in, ffn_w_out=ffn_w_out)
    mom_m = dict(zip(_WEIGHT_ORDER, (m_norm_w, m_mem_norm_w, m_ab_w_in, m_ab_w_out, m_hgrn_lb_logits, m_hgrn_out_norm_w, m_s5_lambda_re, m_s5_lambda_im, m_s5_log_dt, m_s5_b_re, m_s5_b_im, m_s5_c_re, m_s5_c_im, m_s5_d, m_s5_w_glu, m_xattn_wq, m_xattn_wkv, m_xattn_wo, m_ffn_w_in, m_ffn_w_out)))
    mom_v = dict(zip(_WEIGHT_ORDER, (v_norm_w, v_mem_norm_w, v_ab_w_in, v_ab_w_out, v_hgrn_lb_logits, v_hgrn_out_norm_w, v_s5_lambda_re, v_s5_lambda_im, v_s5_log_dt, v_s5_b_re, v_s5_b_im, v_s5_c_re, v_s5_c_im, v_s5_d, v_s5_w_glu, v_xattn_wq, v_xattn_wkv, v_xattn_wo, v_ffn_w_in, v_ffn_w_out)))
    dev = 4 * lax.axis_index("x") + 2 * lax.axis_index("y") + lax.axis_index("c")

    shards = [_owner_major(n, local[n][l]).astype(BF16) for n, l in _LARGE_KEYS]
    gathered = _gather_weights(shards, name="gather_weights")
    W = {key: g.reshape(-1, D_MODEL) for key, g in zip(_LARGE_KEYS, gathered)}
    tiny =jnp.concatenate([norm_w.reshape(-1), s5_d.reshape(-1)])
    tiny = jnp.pad(tiny, (0, 16 * LANES - tiny.shape[0])).reshape(16, LANES)
    tiny_all = _all_gather(tiny, name="gather_tiny").reshape(N_DEV, 16 * LANES)
    W["norm_w"] = tiny_all[:, :12 * LANES].reshape(N_DEV, 2, 6, LANES).transpose(1, 2, 0, 3).reshape(2, 6, D_MODEL)
    W["s5_d"] = tiny_all[:, 12 * LANES:13 * LANES].reshape(1, D_MODEL)
    for n in ("mem_norm_w", "hgrn_lb_logits", "hgrn_out_norm_w", "s5_lambda_re", "s5_lambda_im", "s5_log_dt",
              "s5_b_re", "s5_b_im", "s5_c_re", "s5_c_im"):
        W[n] = local[n]

    loss_parts, grad_x, G = _local_step(x, mem, loss_target, W)

    recv = _scatter_grads([G[key].reshape(N_DEV, -1, D_MODEL) for key in _LARGE_KEYS], name="scatter_grads")
    g_layers = {}
    for (n, l), r in zip(_LARGE_KEYS, recv):
        g = _sum_rows(r, name=f"sum_grads_{n}_{l}")
        g_layers.setdefault(n, []).append(g.T if _TRANSPOSED[n] else g)
    g_local = {n: jnp.stack(gl) for n, gl in g_layers.items()}
    small =jnp.concatenate([G[n].reshape(-1) for n, _ in _SMALL] + [0.5 / D_MODEL * jnp.sum(loss_parts).reshape(1)])
    n_small = small.shape[0]
    small = jnp.pad(small, (0, (-n_small) % (8 * LANES))).reshape(-1, LANES)
    small_sum = _sum_rows(_all_gather(small, name="gather_small"), name="sum_small").reshape(-1)
    g_full, off = {}, 0
    for n, shp in _SMALL:
        size = int(np.prod(shp))
        g_full[n] = small_sum[off:off + size].reshape(shp)
        off += size
    loss = small_sum[off]
    grads = dict(g_local)
    for n, shp in _SMALL:
        if n == "norm_w":
            grads[n] = lax.dynamic_slice_in_dim(g_full[n], dev * LANES, LANES, axis=2)
        elif n == "s5_d":
            grads[n] = lax.dynamic_slice_in_dim(g_full[n], dev * LANES, LANES, axis=1)
        else:
            grads[n] = g_full[n]

    delta, new_m, new_v = {}, {}, {}
    for n in _WEIGHT_ORDER:
        delta[n], new_m[n], new_v[n] = _adamw(local[n], grads[n], mom_m[n], mom_v[n], name=f"adamw_{n}")
    return (loss, grad_x, *[grads[n] for n in _WEIGHT_ORDER], *[delta[n] for n in _WEIGHT_ORDER],
            *[new_m[n] for n in _WEIGHT_ORDER], *[new_v[n] for n in _WEIGHT_ORDER])
```

```python
import functools
import math

import numpy as np
import jax
import jax.numpy as jnp
from jax import lax
from jax.experimental import pallas as pl
from jax.experimental.pallas import tpu as pltpu

F32 = jnp.float32
BF16 = jnp.bfloat16
HI = lax.Precision.HIGHEST

D_MODEL = 1024
NORM_EPS = 1e-6
A_WIDTH = 512
A_HEAD = 128
A_CHUNK = 32
A_SUPER = 256
B_SPAN = 128
B_DILS = (1, 4, 16)
ROPE_THETA = 10000.0
C_GROUPS = 64
C_GROUP = 16
C_STATE = 64
C_TC = 16
C_MIN_NEG_RE = -1e-4
MEM_LEN = 256
X_HEADS = 4
X_HD = 256
D_FF = 2816
N_DEV = 8
LANES = 128

ADAM_LR, ADAM_B1, ADAM_B2, ADAM_EPS, ADAM_WD, ADAM_STEP = 0.001, 0.9, 0.999, 1e-08, 0.01, 10

NEG_BIG = -1e30


def _tile(n, pref):
    for d in range(min(pref, n) // LANES * LANES, 0, -LANES):
        if n % d == 0:
            return d
    return n


def _cparams(*sem):
    return pltpu.CompilerParams(dimension_semantics=sem, vmem_limit_bytes=56 * 1024 * 1024)


def _sigmoid(x):
    return 1.0 / (1.0 + jnp.exp(-x))


def _erf(x):
    ax = jnp.abs(x)
    t = 1.0 / (1.0 + 0.3275911 * ax)
    poly = t * (0.254829592 + t * (-0.284496736 + t * (1.421413741 + t * (-1.453152027 + t * 1.061405429))))
    y = 1.0 - poly * jnp.exp(-ax * ax)
    return jnp.where(x < 0, -y, y)


_HBM = pl.BlockSpec(memory_space=pltpu.HBM)
_MESH = pl.DeviceIdType.MESH


def _logical(px, py, pc):
    return 4 * px + 2 * py + pc


class _Exchange:
    def __init__(self, gather=(), scatter=()):
        self.gather, self.scatter = list(gather), list(scatter)
        self.ng, self.n = len(self.gather), len(self.gather) + len(self.scatter)

    def operands(self):
        return self.gather + self.scatter

    def in_specs(self):
        return [_HBM] * self.n

    def out_shapes(self):
        return ([jax.ShapeDtypeStruct((N_DEV,) + g.shape, g.dtype) for g in self.gather]
                + [jax.ShapeDtypeStruct(s.shape, s.dtype) for s in self.scatter])

    def scratch(self):
        if not self.n:
            return []
        return [pltpu.SemaphoreType.DMA((self.n, 7)), pltpu.SemaphoreType.DMA((self.n, 7)), pltpu.SemaphoreType.DMA((self.n,))]

    def split(self, results):
        return list(results[:self.ng]), list(results[self.ng:])

    def run(self, ins, outs, sems, first, last):
        if not self.n:
            return
        send_sems, recv_sems, local_sems = sems
        x, y, c = lax.axis_index("x"), lax.axis_index("y"), lax.axis_index("c")
        me, sibling = _logical(x, y, c), (x, y, 1 - c)
        chips = [(1 - x, y), (x, 1 - y), (1 - x, 1 - y)]
        peers = [(x ^ (k >> 2), y ^ ((k >> 1) & 1), c ^ (k & 1)) for k in range(1, N_DEV)]

        def remote(t, k, src, dst, to):
            return pltpu.make_async_remote_copy(src_ref=src, dst_ref=dst, send_sem=send_sems.at[t, k],
                                                recv_sem=recv_sems.at[t, k], device_id=to, device_id_type=_MESH)

        def local(t):
            src = ins[t] if t < self.ng else ins[t].at[me]
            return pltpu.make_async_copy(src, outs[t].at[me], local_sems.at[t])

        @pl.when(first)
        def _():
            for t in range(self.n):
                local(t).start()
                if t < self.ng:
                    remote(t, 0, ins[t], outs[t].at[me], sibling).start()
                    for j, chip in enumerate(chips):
                        remote(t, 1 + j, ins[t], outs[t].at[me], (*chip, c)).start()
                else:
                    for k, peer in enumerate(peers):
                        remote(t, k, ins[t].at[_logical(*peer)], outs[t].at[me], peer).start()

        @pl.when(last)
        def _():
            for j, chip in enumerate(chips):
                for t in range(self.ng):
                    landed = outs[t].at[_logical(*chip, c)]
                    remote(t, 1 + j, ins[t], landed, sibling).wait_recv()
                    remote(t, 4 + j, landed, landed, sibling).start()
            for t in range(self.n):
                if t < self.ng:
                    remote(t, 0, ins[t], outs[t].at[_logical(*sibling)], sibling).wait_recv()
                    for j, chip in enumerate(chips):
                        remote(t, 4 + j, ins[t], outs[t].at[_logical(*chip, 1 - c)], sibling).wait_recv()
                    for k in range(7):
                        remote(t, k, ins[t], outs[t].at[me], sibling).wait_send()
                else:
                    for k, peer in enumerate(peers):
                        remote(t, k, ins[t].at[me], outs[t].at[_logical(*peer)], peer).wait_recv()
                    for k, peer in enumerate(peers):
                        remote(t, k, ins[t].at[_logical(*peer)], outs[t].at[me], peer).wait_send()
                local(t).wait()


_MM_VMEM_BUDGET = 36 * 1024 * 1024


def _mm(a, b, *, ta=False, tb=False, out_dtype=F32, name, tiles=(1408, 1408, 4096), gather=(), scatter=()):
    M, K = (a.shape[1], a.shape[0]) if ta else a.shape
    N = b.shape[0] if tb else b.shape[1]
    assert (b.shape[1] if tb else b.shape[0]) == K
    tm, tn, tk = _tile(M, tiles[0]), _tile(N, tiles[1]), _tile(K, tiles[2])

    def vmem_bytes():
        out = 2 * tm * tn * jnp.dtype(out_dtype).itemsize + (4 * tm * tn if tk < K else 0)
        return 2 * 2 * (tm * tk + tk * tn) + out

    while vmem_bytes() > _MM_VMEM_BUDGET:
        if tk > 512:
            tk = _tile(K, tk - LANES)
        elif tn > 256:
            tn = _tile(N, tn - LANES)
        else:
            tm = _tile(M, tm - LANES)
    ni, nj, nk = M // tm, N // tn, K // tk
    ex = _Exchange(gather, scatter)

    a_spec = pl.BlockSpec((tk, tm), lambda i, j, k: (k, i)) if ta else pl.BlockSpec((tm, tk), lambda i, j, k: (i, k))
    b_spec = pl.BlockSpec((tn, tk), lambda i, j, k: (j, k)) if tb else pl.BlockSpec((tk, tn), lambda i, j, k: (k, j))
    dims = (((0 if ta else 1,), (1 if tb else 0,)), ((), ()))
    n_acc = 1 if nk > 1 else 0

    def body(a_ref, b_ref, *rest):
        ex_in, o_ref, ex_out = rest[:ex.n], rest[ex.n], rest[ex.n + 1:2 * ex.n + 1]
        scratch = rest[2 * ex.n + 1:]
        i, j, k = pl.program_id(0), pl.program_id(1), pl.program_id(2)
        ex.run(ex_in, ex_out, scratch[n_acc:], (i == 0) & (j == 0) & (k == 0), (i == ni - 1) & (j == nj - 1) & (k == nk - 1))
        part = lax.dot_general(a_ref[...].astype(BF16), b_ref[...].astype(BF16), dims, preferred_element_type=F32)
        if nk == 1:
            o_ref[...] = part.astype(o_ref.dtype)
            return
        acc_ref = scratch[0]

        @pl.when(k == 0)
        def _():
            acc_ref[...] = part

        @pl.when(k > 0)
        def _():
            acc_ref[...] += part

        @pl.when(k == nk - 1)
        def _():
            o_ref[...] = acc_ref[...].astype(o_ref.dtype)

    sem = ("arbitrary",) * 3 if ex.n else ("parallel", "parallel", "arbitrary")
    res = pl.pallas_call(
        body, name=name, grid=(ni, nj, nk),
        in_specs=[a_spec, b_spec] + ex.in_specs(),
        out_specs=[pl.BlockSpec((tm, tn), lambda i, j, k: (i, j))] + ex.in_specs(),
        out_shape=[jax.ShapeDtypeStruct((M, N), out_dtype)] + ex.out_shapes(),
        scratch_shapes=([pltpu.VMEM((tm, tn), F32)] if nk > 1 else []) + ex.scratch(),
        compiler_params=_cparams(*sem),
    )(a, b, *ex.operands())
    if not ex.n:
        return res[0]
    return (res[0],) + tuple(ex.split(res[1:]))


def _rms_fwd(x, w, res=None, *, name, out_dtype=F32):
    T, C = x.shape
    tm = _tile(T, 512)
    has_res = res is not None

    def body(*refs):
        x_ref, w_ref = refs[0], refs[1]
        o_ref = refs[-1]
        xv = x_ref[...].astype(F32)
        r = lax.rsqrt(jnp.mean(xv * xv, axis=-1, keepdims=True) + NORM_EPS)
        y = xv * r * w_ref[...]
        if has_res:
            y = y + refs[2][...]
        o_ref[...] = y.astype(o_ref.dtype)

    row = pl.BlockSpec((tm, C), lambda i: (i, 0))
    vec = pl.BlockSpec((1, C), lambda i: (0, 0))
    ins = [x, w] + ([res] if has_res else [])
    return pl.pallas_call(
        body, name=name, grid=(T // tm,), in_specs=[row, vec] + ([row] if has_res else []), out_specs=row,
        out_shape=jax.ShapeDtypeStruct((T, C), out_dtype), compiler_params=_cparams("parallel"),
    )(*ins)


def _rms_bwd(x, w, dy, add=None, *, name, out_dtype=F32):
    T, C = x.shape
    tm = _tile(T, 512)
    has_add = add is not None

    def body(*refs):
        x_ref, w_ref, dy_ref = refs[:3]
        dx_ref, dw_ref = refs[-2], refs[-1]
        xv = x_ref[...].astype(F32)
        r = lax.rsqrt(jnp.mean(xv * xv, axis=-1, keepdims=True) + NORM_EPS)
        xh = xv * r
        g = dy_ref[...].astype(F32)
        part = jnp.sum(g * xh, axis=0, keepdims=True)

        @pl.when(pl.program_id(0) == 0)
        def _():
            dw_ref[...] = part

        @pl.when(pl.program_id(0) > 0)
        def _():
            dw_ref[...] += part

        gx = g * w_ref[...]
        dx = r * (gx - xh * jnp.mean(gx * xh, axis=-1, keepdims=True))
        if has_add:
            dx = dx + refs[3][...]
        dx_ref[...] = dx.astype(dx_ref.dtype)

    row = pl.BlockSpec((tm, C), lambda i: (i, 0))
    vec = pl.BlockSpec((1, C), lambda i: (0, 0))
    ins = [x, w, dy] + ([add] if has_add else [])
    return pl.pallas_call(
        body, name=name, grid=(T // tm,), in_specs=[row, vec, row] + ([row] if has_add else []),
        out_specs=[row, vec],
        out_shape=[jax.ShapeDtypeStruct((T, C), out_dtype), jax.ShapeDtypeStruct((1, C), F32)],
        compiler_params=_cparams("arbitrary"),
    )(*ins)


def _gate_block(width):
    return _tile(width, 256)


def _interleave_rows(w, inverse=False):
    W2, C = w.shape
    bs = _gate_block(W2 // 2)
    nb = W2 // 2 // bs
    shape = (nb, 2, bs, C) if inverse else (2, nb, bs, C)
    return w.reshape(shape).transpose(1, 0, 2, 3).reshape(W2, C)


def _gated_fwd(z, kind, *, name):
    T, W2 = z.shape
    W = W2 // 2
    tm, bs = _tile(T, 512), _gate_block(W)

    def body(z_ref, o_ref):
        a, b = z_ref[:, :bs].astype(F32), z_ref[:, bs:].astype(F32)
        if kind == "swiglu":
            o_ref[...] = (a * _sigmoid(a) * b).astype(o_ref.dtype)
        else:
            o_ref[...] = (a * _sigmoid(b)).astype(o_ref.dtype)

    return pl.pallas_call(
        body, name=name, grid=(T // tm, W // bs),
        in_specs=[pl.BlockSpec((tm, 2 * bs), lambda i, j: (i, j))],
        out_specs=pl.BlockSpec((tm, bs), lambda i, j: (i, j)),
        out_shape=jax.ShapeDtypeStruct((T, W), BF16), compiler_params=_cparams("parallel", "parallel"),
    )(z)


def _gated_bwd(z, dout, kind, *, name):
    T, W2 = z.shape
    W = W2 // 2
    tm, bs = _tile(T, 512), _gate_block(W)

    def body(z_ref, d_ref, o_ref):
        a, b, d = z_ref[:, :bs].astype(F32), z_ref[:, bs:].astype(F32), d_ref[...].astype(F32)
        if kind == "swiglu":
            s = _sigmoid(a)
            da = d * b * (s * (1.0 + a * (1.0 - s)))
            db = d * a * s
        else:
            s = _sigmoid(b)
            da = d * s
            db = d * a * s * (1.0 - s)
        o_ref[:, :bs] = da.astype(o_ref.dtype)
        o_ref[:, bs:] = db.astype(o_ref.dtype)

    return pl.pallas_call(
        body, name=name, grid=(T // tm, W // bs),
        in_specs=[pl.BlockSpec((tm, 2 * bs), lambda i, j: (i, j)), pl.BlockSpec((tm, bs), lambda i, j: (i, j))],
        out_specs=pl.BlockSpec((tm, 2 * bs), lambda i, j: (i, j)),
        out_shape=jax.ShapeDtypeStruct((T, W2), BF16), compiler_params=_cparams("parallel", "parallel"),
    )(z, dout)


_NT = (((1,), (1,)), ((), ()))
_TN = (((0,), (0,)), ((), ()))


def _dot(a, b, dims=None, precision=None):
    if dims is None:
        return jnp.dot(a, b, preferred_element_type=F32, precision=precision)
    return lax.dot_general(a, b, dims, preferred_element_type=F32, precision=precision)


def _xattn_fwd(q, kv, B, L, *, name):
    T = q.shape[0]
    tq = 256
    nq = L // tq
    scale = X_HD ** -0.5

    def body(q_ref, k_ref, v_ref, o_ref):
        for h in range(X_HEADS):
            sl = slice(h * X_HD, (h + 1) * X_HD)
            qh, kh, vh = q_ref[:, sl].astype(BF16), k_ref[:, sl].astype(BF16), v_ref[:, sl].astype(BF16)
            s = _dot(qh, kh, _NT) * scale
            m = jnp.max(s, axis=-1, keepdims=True)
            p = jnp.exp(s - m)
            l = jnp.sum(p, axis=-1, keepdims=True)
            o_ref[:, sl] = (_dot(p.astype(BF16), vh) / l).astype(BF16)

    return pl.pallas_call(
        body, name=name, grid=(B, nq),
        in_specs=[pl.BlockSpec((tq, D_MODEL), lambda b, i: (b * nq + i, 0)),
                  pl.BlockSpec((MEM_LEN, D_MODEL), lambda b, i: (b, 0)),
                  pl.BlockSpec((MEM_LEN, D_MODEL), lambda b, i: (b, 1))],
        out_specs=pl.BlockSpec((tq, D_MODEL), lambda b, i: (b * nq + i, 0)),
        out_shape=jax.ShapeDtypeStruct((T, D_MODEL), BF16), compiler_params=_cparams("parallel", "parallel"),
    )(q, kv, kv)


def _xattn_bwd(q, kv, do, B, L, *, name):
    T = q.shape[0]
    tq = 256
    nq = L // tq
    scale = X_HD ** -0.5

    def body(q_ref, k_ref, v_ref, do_ref, dq_ref, dkv_ref):
        @pl.when(pl.program_id(1) == 0)
        def _():
            dkv_ref[...] = jnp.zeros_like(dkv_ref)

        for h in range(X_HEADS):
            sl = slice(h * X_HD, (h + 1) * X_HD)
            slv = slice(D_MODEL + h * X_HD, D_MODEL + (h + 1) * X_HD)
            qh, kh, vh = q_ref[:, sl].astype(BF16), k_ref[:, sl].astype(BF16), v_ref[:, sl].astype(BF16)
            doh = do_ref[:, sl].astype(BF16)
            s = _dot(qh, kh, _NT) * scale
            m = jnp.max(s, axis=-1, keepdims=True)
            e = jnp.exp(s - m)
            p = e / jnp.sum(e, axis=-1, keepdims=True)
            dkv_ref[:, slv] += _dot(p.astype(BF16), doh, _TN)
            dp = _dot(doh, vh, _NT)
            ds = p * (dp - jnp.sum(dp * p, axis=-1, keepdims=True)) * scale
            dsb = ds.astype(BF16)
            dq_ref[:, sl] = _dot(dsb, kh).astype(BF16)
            dkv_ref[:, sl] += _dot(dsb, qh, _TN)

    return pl.pallas_call(
        body, name=name, grid=(B, nq),
        in_specs=[pl.BlockSpec((tq, D_MODEL), lambda b, i: (b * nq + i, 0)),
                  pl.BlockSpec((MEM_LEN, D_MODEL), lambda b, i: (b, 0)),
                  pl.BlockSpec((MEM_LEN, D_MODEL), lambda b, i: (b, 1)),
                  pl.BlockSpec((tq, D_MODEL), lambda b, i: (b * nq + i, 0))],
        out_specs=[pl.BlockSpec((tq, D_MODEL), lambda b, i: (b * nq + i, 0)),
                   pl.BlockSpec((MEM_LEN, 2 * D_MODEL), lambda b, i: (b, 0))],
        out_shape=[jax.ShapeDtypeStruct((T, D_MODEL), BF16), jax.ShapeDtypeStruct((B * MEM_LEN, 2 * D_MODEL), F32)],
        compiler_params=_cparams("parallel", "arbitrary"),
    )(q, kv, kv, do)


def _chunk_masks():
    row = lax.broadcasted_iota(jnp.int32, (A_SUPER, A_SUPER), 0)
    col = lax.broadcasted_iota(jnp.int32, (A_SUPER, A_SUPER), 1)
    same = jnp.right_shift(row, 5) == jnp.right_shift(col, 5)
    return same, same & (col <= row), same & (col >= row)


def _hgrn_gates(fa, lb):
    sig = _sigmoid(fa)
    f = lb + (1.0 - lb) * sig
    return sig, f, jnp.log(f), 1.0 - f


def _hgrn_fwd(z, lb, onw, B, L, *, name):
    T = B * L
    ns = L // A_SUPER
    nch = A_SUPER // A_CHUNK

    def body(q_ref, f_ref, v_ref, g_ref, lb_ref, w_ref, oa_ref, o_ref, s_ref, st_ref):
        @pl.when(pl.program_id(2) == 0)
        def _():
            st_ref[...] = jnp.zeros_like(st_ref)

        s_ref[0] = st_ref[...]
        same, tril, _ = _chunk_masks()
        q, v = q_ref[...], v_ref[...]
        _, _, lf, k = _hgrn_gates(f_ref[...], lb_ref[...])
        bcs = _dot(tril.astype(F32), lf, precision=HI)
        bl = _dot(same.astype(F32), lf, precision=HI)
        qd = (q * jnp.exp(bcs)).astype(BF16)
        ki = (k * jnp.exp(-bcs)).astype(BF16)
        ke = (k * jnp.exp(bl - bcs)).astype(BF16)
        dec = jnp.exp(bl)
        vb = v.astype(BF16)
        a = jnp.where(tril, _dot(qd, ki, _NT), 0.0)
        o_ref[...] = _dot(a.astype(BF16), vb)
        st = st_ref[...]
        for c in range(nch):
            rs = slice(c * A_CHUNK, (c + 1) * A_CHUNK)
            o_ref[rs, :] += _dot(qd[rs], st.astype(BF16), _NT)
            st = st * dec[c * A_CHUNK:c * A_CHUNK + 1, :] + _dot(vb[rs], ke[rs], _TN)
        st_ref[...] = st
        o = o_ref[...]
        r = lax.rsqrt(jnp.mean(o * o, axis=-1, keepdims=True) + NORM_EPS)
        g = g_ref[...]
        oa_ref[...] = o * r * w_ref[...] * (g * _sigmoid(g))

    def zspec(off):
        return pl.BlockSpec((A_SUPER, A_HEAD), lambda b, h, n: (b * ns + n, off + h))

    hvec = pl.BlockSpec((1, A_HEAD), lambda b, h, n: (0, h))
    ospec = pl.BlockSpec((A_SUPER, A_HEAD), lambda b, h, n: (b * ns + n, h))
    return pl.pallas_call(
        body, name=name, grid=(B, 4, ns),
        in_specs=[zspec(0), zspec(4), zspec(8), zspec(12), hvec, hvec],
        out_specs=[ospec, ospec, pl.BlockSpec((1, A_HEAD, A_HEAD), lambda b, h, n: ((b * 4 + h) * ns + n, 0, 0))],
        out_shape=[jax.ShapeDtypeStruct((T, A_WIDTH), F32), jax.ShapeDtypeStruct((T, A_WIDTH), F32),
                   jax.ShapeDtypeStruct((B * 4 * ns, A_HEAD, A_HEAD), F32)],
        scratch_shapes=[pltpu.VMEM((A_HEAD, A_HEAD), F32)],
        compiler_params=_cparams("parallel", "parallel", "arbitrary"),
    )(z, z, z, z, lb, onw)


def _hgrn_bwd(z, lb, onw, o_raw, s_start, doa, B, L, *, name):
    T = B * L
    ns = L // A_SUPER
    nch = A_SUPER // A_CHUNK

    def body(q_ref, f_ref, v_ref, g_ref, lb_ref, w_ref, o_ref, s_ref, doa_ref,
             dq_ref, df_ref, dv_ref, dg_ref, dw_ref, dlb_ref, dst_ref, sc_ref, dqd_ref, dke_ref, dblx_ref):
        @pl.when(pl.program_id(2) == 0)
        def _():
            dst_ref[...] = jnp.zeros_like(dst_ref)
            dw_ref[...] = jnp.zeros_like(dw_ref)
            dlb_ref[...] = jnp.zeros_like(dlb_ref)

        same, tril, triu = _chunk_masks()
        q, v, g, lb, w = q_ref[...], v_ref[...], g_ref[...], lb_ref[...], w_ref[...]
        sig, f, lf, k = _hgrn_gates(f_ref[...], lb)
        bcs = _dot(tril.astype(F32), lf, precision=HI)
        bl = _dot(same.astype(F32), lf, precision=HI)
        eb, enb, eeb = jnp.exp(bcs), jnp.exp(-bcs), jnp.exp(bl - bcs)
        qd, ki, ke = q * eb, k * enb, k * eeb
        qdb, kib, keb, vb = qd.astype(BF16), ki.astype(BF16), ke.astype(BF16), v.astype(BF16)
        dec = jnp.exp(bl)
        o = o_ref[...]
        r = lax.rsqrt(jnp.mean(o * o, axis=-1, keepdims=True) + NORM_EPS)
        on = o * r
        sg = _sigmoid(g)
        silu_g = g * sg
        doa = doa_ref[...]
        dg_ref[...] = doa * on * w * (sg * (1.0 + g * (1.0 - sg)))
        dw_ref[0] += jnp.sum(doa * on * silu_g, axis=0, keepdims=True)
        don = doa * w * silu_g
        do = r * (don - on * jnp.mean(don * on, axis=-1, keepdims=True))
        dob = do.astype(BF16)
        a = jnp.where(tril, _dot(qdb, kib, _NT), 0.0).astype(BF16)
        da = jnp.where(tril, _dot(dob, vb, _NT), 0.0).astype(BF16)
        dv_ref[...] = _dot(a, dob, _TN)
        dqd_ref[...] = _dot(da, kib)
        dki = _dot(da, qdb, _TN)
        st = s_ref[0]
        for c in range(nch):
            rs = slice(c * A_CHUNK, (c + 1) * A_CHUNK)
            sc_ref[c] = st
            st = st * dec[c * A_CHUNK:c * A_CHUNK + 1, :] + _dot(vb[rs], keb[rs], _TN)
        dst = dst_ref[...]
        for c in reversed(range(nch)):
            rs = slice(c * A_CHUNK, (c + 1) * A_CHUNK)
            dec_c = dec[c * A_CHUNK:c * A_CHUNK + 1, :]
            dstb = dst.astype(BF16)
            stc = sc_ref[c]
            dv_ref[rs, :] += _dot(keb[rs], dstb, _NT)
            dke_ref[rs, :] = _dot(vb[rs], dstb)
            ddec = jnp.sum(dst * stc, axis=0, keepdims=True)
            dqd_ref[rs, :] += _dot(dob[rs], stc.astype(BF16))
            dblx_ref[rs, :] = jnp.broadcast_to(ddec * dec_c, (A_CHUNK, A_HEAD))
            dst = dst * dec_c + _dot(dob[rs], qdb[rs], _TN)
        dst_ref[...] = dst
        dqd, dke = dqd_ref[...], dke_ref[...]
        dq_ref[...] = dqd * eb
        keke = dke * ke
        db = dqd * qd - dki * ki - keke
        dbl = _dot(same.astype(F32), keke, precision=HI) + dblx_ref[...]
        dk = dki * enb + dke * eeb
        dlf = _dot(triu.astype(F32), db, precision=HI) + dbl
        dff = dlf / f - dk
        df_ref[...] = dff * (1.0 - lb) * sig * (1.0 - sig)
        dlb_ref[0] += jnp.sum(dff * (1.0 - sig), axis=0, keepdims=True)

    def rev(n):
        return ns - 1 - n

    def zspec(off):
        return pl.BlockSpec((A_SUPER, A_HEAD), lambda b, h, n: (b * ns + rev(n), off + h))

    hvec = pl.BlockSpec((1, A_HEAD), lambda b, h, n: (0, h))
    ospec = pl.BlockSpec((A_SUPER, A_HEAD), lambda b, h, n: (b * ns + rev(n), h))
    acc = pl.BlockSpec((1, 1, A_HEAD), lambda b, h, n: (b * 4 + h, 0, 0))
    big = jax.ShapeDtypeStruct((T, A_WIDTH), F32)
    small = jax.ShapeDtypeStruct((B * 4, 1, A_HEAD), F32)
    return pl.pallas_call(
        body, name=name, grid=(B, 4, ns),
        in_specs=[zspec(0), zspec(4), zspec(8), zspec(12), hvec, hvec, ospec,
                  pl.BlockSpec((1, A_HEAD, A_HEAD), lambda b, h, n: ((b * 4 + h) * ns + rev(n), 0, 0)), ospec],
        out_specs=[ospec, ospec, ospec, ospec, acc, acc],
        out_shape=[big, big, big, big, small, small],
        scratch_shapes=[pltpu.VMEM((A_HEAD, A_HEAD), F32), pltpu.VMEM((nch, A_HEAD, A_HEAD), F32),
                        pltpu.VMEM((A_SUPER, A_HEAD), F32), pltpu.VMEM((A_SUPER, A_HEAD), F32),
                        pltpu.VMEM((A_SUPER, A_HEAD), F32)],
        compiler_params=_cparams("parallel", "parallel", "arbitrary"),
    )(z, z, z, z, lb, onw, o_raw, s_start, doa)


def _rope_tables(L):
    half = A_HEAD // 2
    inv_freq = ROPE_THETA ** (-jnp.arange(half, dtype=F32) / half)
    ang = jnp.arange(L, dtype=F32)[:, None] * inv_freq[None, :]
    cos, sin = jnp.cos(ang), jnp.sin(ang)
    return jnp.concatenate([cos, cos], axis=-1), jnp.concatenate([-sin, sin], axis=-1)


def _rope_fwd(z, cos2, sin2, col_off, B, L, *, name):
    T = B * L
    tm = 256
    nl = L // tm

    def body(x_ref, c_ref, s_ref, o_ref):
        x = x_ref[...]
        o_ref[...] = x * c_ref[...] + pltpu.roll(x, A_HEAD // 2, 1) * s_ref[...]

    tab = pl.BlockSpec((tm, A_HEAD), lambda i, h: (i % nl, 0))
    return pl.pallas_call(
        body, name=name, grid=(T // tm, 4),
        in_specs=[pl.BlockSpec((tm, A_HEAD), lambda i, h: (i, col_off + h)), tab, tab],
        out_specs=pl.BlockSpec((tm, A_HEAD), lambda i, h: (i, h)),
        out_shape=jax.ShapeDtypeStruct((T, 512), F32), compiler_params=_cparams("parallel", "parallel"),
    )(z, cos2, sin2)


def _sum3(d1, d2, d3, cos2, sin2, rotate, B, L, *, name):
    T = B * L
    tm = 256
    nl = L // tm

    def body(a_ref, b_ref, c_ref, cs_ref, sn_ref, o_ref):
        d = a_ref[...] + b_ref[...] + c_ref[...]
        if rotate:
            d = d * cs_ref[...] - pltpu.roll(d, A_HEAD // 2, 1) * sn_ref[...]
        o_ref[...] = d

    blk = pl.BlockSpec((tm, A_HEAD), lambda i, h: (i, h))
    tab = pl.BlockSpec((tm, A_HEAD), lambda i, h: (i % nl, 0))
    return pl.pallas_call(
        body, name=name, grid=(T // tm, 4), in_specs=[blk, blk, blk, tab, tab], out_specs=blk,
        out_shape=jax.ShapeDtypeStruct((T, 512), F32), compiler_params=_cparams("parallel", "parallel"),
    )(d1, d2, d3, cos2, sin2)


def _band_masks():
    i = lax.broadcasted_iota(jnp.int32, (B_SPAN, B_SPAN), 0)
    j = lax.broadcasted_iota(jnp.int32, (B_SPAN, B_SPAN), 1)
    return i <= j, j <= i


def _dil_geom(dil, L):
    return B_SPAN * dil, L // (B_SPAN * dil), (1 if dil > 1 else 4)


def _dil_rows(r, dil):
    return pl.ds(r, B_SPAN, stride=dil) if dil > 1 else pl.ds(0, B_SPAN)


def _dil_loop(dil, step):
    if dil == 1:
        step(0, 0)
    else:
        lax.fori_loop(0, dil, step, 0)


def _dil_specs(dil, B, L):
    W, nb, hb = _dil_geom(dil, L)
    cw = A_HEAD * hb

    def at(col0, shift):
        def index(b, n, hh):
            return (b * nb + jnp.clip(n + shift, 0, nb - 1), col0 // cw + hh)
        return pl.BlockSpec((W, cw), index)

    return at


def _dil_fwd(qr, kr, z, dil, B, L, *, name):
    T = B * L
    W, nb, hb = _dil_geom(dil, L)
    has_prev = nb > 1
    scale = A_HEAD ** -0.5
    at = _dil_specs(dil, B, L)

    def body(*refs):
        if has_prev:
            q_ref, kc_ref, vc_ref, kp_ref, vp_ref, o_ref, l_ref = refs
        else:
            q_ref, kc_ref, vc_ref, o_ref, l_ref = refs
        mp, mc = _band_masks()
        mp = mp & (pl.program_id(1) > 0)

        def step(r, carry):
            rows = _dil_rows(r, dil)
            for h in range(hb):
                cols = pl.ds(h * A_HEAD, A_HEAD)
                qh = q_ref[rows, cols].astype(BF16)
                sc = jnp.where(mc, _dot(qh, kc_ref[rows, cols].astype(BF16), _NT) * scale, NEG_BIG)
                m = jnp.max(sc, axis=-1, keepdims=True)
                if has_prev:
                    sp = jnp.where(mp, _dot(qh, kp_ref[rows, cols].astype(BF16), _NT) * scale, NEG_BIG)
                    m = jnp.maximum(m, jnp.max(sp, axis=-1, keepdims=True))
                pc = jnp.exp(sc - m)
                l = jnp.sum(pc, axis=-1, keepdims=True)
                o = _dot(pc.astype(BF16), vc_ref[rows, cols].astype(BF16))
                if has_prev:
                    pp = jnp.exp(sp - m)
                    l = l + jnp.sum(pp, axis=-1, keepdims=True)
                    o = o + _dot(pp.astype(BF16), vp_ref[rows, cols].astype(BF16))
                o_ref[rows, cols] = o / l
                l_ref[rows, cols] = jnp.broadcast_to(m + jnp.log(l), (B_SPAN, A_HEAD))
            return carry

        _dil_loop(dil, step)

    ins = [qr, kr, z] + ([kr, z] if has_prev else [])
    in_specs = [at(0, 0), at(0, 0), at(3072, 0)] + ([at(0, -1), at(3072, -1)] if has_prev else [])
    return pl.pallas_call(
        body, name=name, grid=(B, nb, 4 // hb), in_specs=in_specs, out_specs=[at(0, 0), at(0, 0)],
        out_shape=[jax.ShapeDtypeStruct((T, 512), F32)] * 2,
        compiler_params=_cparams("parallel", "parallel", "parallel"),
    )(*ins)


def _dil_combine(os_, ls_, *, name):
    T = os_[0].shape[0]
    tm = 256

    def body(o1, o2, o3, l1, l2, l3, ob_ref, lse_ref):
        a1, a2, a3 = l1[...], l2[...], l3[...]
        m = jnp.maximum(jnp.maximum(a1, a2), a3)
        e1, e2, e3 = jnp.exp(a1 - m), jnp.exp(a2 - m), jnp.exp(a3 - m)
        den = e1 + e2 + e3
        ob_ref[...] = (e1 * o1[...] + e2 * o2[...] + e3 * o3[...]) / den
        lse_ref[...] = m + jnp.log(den)

    blk = pl.BlockSpec((tm, 512), lambda i: (i, 0))
    return pl.pallas_call(
        body, name=name, grid=(T // tm,), in_specs=[blk] * 6, out_specs=[blk, blk],
        out_shape=[jax.ShapeDtypeStruct((T, 512), F32)] * 2, compiler_params=_cparams("parallel"),
    )(*[o.reshape(T, 512) for o in os_], *[l.reshape(T, 512) for l in ls_])


def _dil_bwd_q(qr, kr, z, dymix, out, lse, dil, B, L, *, name):
    T = B * L
    W, nb, hb = _dil_geom(dil, L)
    has_prev = nb > 1
    scale = A_HEAD ** -0.5
    at = _dil_specs(dil, B, L)

    def body(*refs):
        if has_prev:
            q_ref, kc_ref, vc_ref, do_ref, out_ref, lse_ref, kp_ref, vp_ref, dq_ref = refs
        else:
            q_ref, kc_ref, vc_ref, do_ref, out_ref, lse_ref, dq_ref = refs
        mp, mc = _band_masks()
        mp = mp & (pl.program_id(1) > 0)

        def step(r, carry):
            rows = _dil_rows(r, dil)
            for h in range(hb):
                cols = pl.ds(h * A_HEAD, A_HEAD)
                qh = q_ref[rows, cols].astype(BF16)
                do = do_ref[rows, cols]
                delta = jnp.sum(do * out_ref[rows, cols], axis=-1, keepdims=True)
                dob = do.astype(BF16)
                lse_h = lse_ref[rows, cols]
                kc = kc_ref[rows, cols].astype(BF16)
                pc = jnp.where(mc, jnp.exp(_dot(qh, kc, _NT) * scale - lse_h), 0.0)
                dsc = pc * (_dot(dob, vc_ref[rows, cols].astype(BF16), _NT) - delta) * scale
                dq = _dot(dsc.astype(BF16), kc)
                if has_prev:
                    kp = kp_ref[rows, cols].astype(BF16)
                    pp = jnp.where(mp, jnp.exp(_dot(qh, kp, _NT) * scale - lse_h), 0.0)
                    dsp = pp * (_dot(dob, vp_ref[rows, cols].astype(BF16), _NT) - delta) * scale
                    dq = dq + _dot(dsp.astype(BF16), kp)
                dq_ref[rows, cols] = dq
            return carry

        _dil_loop(dil, step)

    ins = [qr, kr, z, dymix, out, lse] + ([kr, z] if has_prev else [])
    in_specs = ([at(0, 0), at(0, 0), at(3072, 0), at(512, 0), at(0, 0), at(0, 0)]
                + ([at(0, -1), at(3072, -1)] if has_prev else []))
    return pl.pallas_call(
        body, name=name, grid=(B, nb, 4 // hb), in_specs=in_specs, out_specs=at(0, 0),
        out_shape=jax.ShapeDtypeStruct((T, 512), F32),
        compiler_params=_cparams("parallel", "parallel", "parallel"),
    )(*ins)


def _dil_bwd_kv(qr, kr, z, dymix, out, lse, dil, B, L, *, name):
    T = B * L
    W, nb, hb = _dil_geom(dil, L)
    has_next = nb > 1
    scale = A_HEAD ** -0.5
    at = _dil_specs(dil, B, L)

    def body(*refs):
        k_ref, v_ref = refs[0], refs[1]
        own = refs[2:6]
        nxt = refs[6:10] if has_next else None
        dk_ref, dv_ref = refs[-2], refs[-1]
        mp, mc = _band_masks()
        mp = mp & (pl.program_id(1) < nb - 1)
        groups = [(own, mc)] + ([(nxt, mp)] if has_next else [])

        def step(r, carry):
            rows = _dil_rows(r, dil)
            for h in range(hb):
                cols = pl.ds(h * A_HEAD, A_HEAD)
                kh, vh = k_ref[rows, cols].astype(BF16), v_ref[rows, cols].astype(BF16)
                dk = jnp.zeros((B_SPAN, A_HEAD), F32)
                dv = jnp.zeros((B_SPAN, A_HEAD), F32)
                for (q_ref, do_ref, out_ref, lse_ref), mask in groups:
                    qh = q_ref[rows, cols].astype(BF16)
                    do = do_ref[rows, cols]
                    delta = jnp.sum(do * out_ref[rows, cols], axis=-1, keepdims=True)
                    dob = do.astype(BF16)
                    p = jnp.where(mask, jnp.exp(_dot(qh, kh, _NT) * scale - lse_ref[rows, cols]), 0.0)
                    dv = dv + _dot(p.astype(BF16), dob, _TN)
                    ds = p * (_dot(dob, vh, _NT) - delta) * scale
                    dk = dk + _dot(ds.astype(BF16), qh, _TN)
                dk_ref[rows, cols] = dk
                dv_ref[rows, cols] = dv
            return carry

        _dil_loop(dil, step)

    ins = [kr, z, qr, dymix, out, lse] + ([qr, dymix, out, lse] if has_next else [])
    in_specs = ([at(0, 0), at(3072, 0), at(0, 0), at(512, 0), at(0, 0), at(0, 0)]
                + ([at(0, 1), at(512, 1), at(0, 1), at(0, 1)] if has_next else []))
    return pl.pallas_call(
        body, name=name, grid=(B, nb, 4 // hb), in_specs=in_specs, out_specs=[at(0, 0), at(0, 0)],
        out_shape=[jax.ShapeDtypeStruct((T, 512), F32)] * 2,
        compiler_params=_cparams("parallel", "parallel", "parallel"),
    )(*ins)


def _s5_build(lam_re, lam_im, log_dt, b_re, b_im, c_re, c_im):
    G, P, TC = C_GROUPS, C_STATE, C_TC
    lr = jnp.minimum(lam_re, C_MIN_NEG_RE)
    li = lam_im
    dt = jnp.exp(log_dt)[:, None]
    mag = jnp.exp(dt * lr)
    ar, ai = mag * jnp.cos(dt * li), mag * jnp.sin(dt * li)
    den = lr * lr + li * li
    zr = ((ar - 1.0) * lr + ai * li) / den
    zi = (ai * lr - (ar - 1.0) * li) / den
    bbr = zr[..., None] * b_re - zi[..., None] * b_im
    bbi = zr[..., None] * b_im + zi[..., None] * b_re
    ks = jnp.arange(TC + 1, dtype=F32)[:, None, None]
    pmag = jnp.exp(ks * (dt * lr)[None])
    pr, pi = pmag * jnp.cos(ks * (dt * li)[None]), pmag * jnp.sin(ks * (dt * li)[None])
    car = c_re[None] * pr[:, :, None, :] - c_im[None] * pi[:, :, None, :]
    cai = c_re[None] * pi[:, :, None, :] + c_im[None] * pr[:, :, None, :]
    kern = (jnp.einsum('lgop,gpc->lgco', car[:TC], bbr, precision=HI)
            - jnp.einsum('lgop,gpc->lgco', cai[:TC], bbi, precision=HI))
    s_idx = jnp.arange(TC)[:, None]
    t_idx = jnp.arange(TC)[None, :]
    lag = t_idx - s_idx
    ksel = jnp.where((lag >= 0)[:, :, None, None, None], kern[jnp.clip(lag, 0, TC - 1)], 0.0)
    m_mat = ksel.transpose(2, 0, 3, 1, 4).reshape(G, TC * C_GROUP, TC * C_GROUP)
    pr_e, pi_e = pr[TC - 1 - jnp.arange(TC)], pi[TC - 1 - jnp.arange(TC)]
    er = pr_e[:, :, :, None] * bbr[None] - pi_e[:, :, :, None] * bbi[None]
    ei = pr_e[:, :, :, None] * bbi[None] + pi_e[:, :, :, None] * bbr[None]
    e_mat = jnp.concatenate([er.transpose(1, 0, 3, 2), ei.transpose(1, 0, 3, 2)], axis=-1).reshape(G, TC * C_GROUP, 2 * P)
    fr = car[1:].transpose(1, 3, 0, 2)
    fi = -cai[1:].transpose(1, 3, 0, 2)
    f_mat = jnp.concatenate([fr, fi], axis=1).reshape(G, 2 * P, TC * C_GROUP)
    return m_mat, e_mat, f_mat, pr[TC], pi[TC]


def _s5_scan_tables(lam_re, lam_im, log_dt, nsteps):
    lr = jnp.minimum(lam_re, C_MIN_NEG_RE)
    dt = jnp.exp(log_dt)[:, None]
    ks = (C_TC * 2.0 ** jnp.arange(8, dtype=F32))[None, :, None]
    keep = (jnp.arange(8) < nsteps)[None, :, None]
    pmag = jnp.exp(ks * (dt * lr)[:, None, :])
    ang = ks * (dt * lam_im)[:, None, :]
    pr = jnp.where(keep, pmag * jnp.cos(ang), 0.0)
    pi = jnp.where(keep, pmag * jnp.sin(ang), 0.0)
    return jnp.concatenate([pr, pr], axis=-1), jnp.concatenate([-pi, pi], axis=-1)


def _s5_fwd(uf, m_mat, e_mat, f_mat, tab_r, tab_i, nch, *, name):
    G, R, _ = uf.shape
    nsteps = int(math.log2(nch))

    def body(u_ref, m_ref, e_ref, f_ref, tr_ref, ti_ref, y_ref, xs_ref):
        u = u_ref[0].astype(BF16)
        x = _dot(u, e_ref[0].astype(BF16))
        pos = jnp.bitwise_and(lax.broadcasted_iota(jnp.int32, (R, 2 * C_STATE), 0), nch - 1)
        for k in range(nsteps):
            s = 1 << k
            sh = pltpu.roll(x, s, 0)
            upd = tr_ref[0, k:k + 1, :] * sh + ti_ref[0, k:k + 1, :] * pltpu.roll(sh, C_STATE, 1)
            x = x + jnp.where(pos >= s, upd, 0.0)
        xs = jnp.where(pos >= 1, pltpu.roll(x, 1, 0), 0.0)
        xs_ref[0] = xs
        y_ref[0] = (_dot(u, m_ref[0].astype(BF16)) + _dot(xs.astype(BF16), f_ref[0].astype(BF16))).astype(BF16)

    def g3(shape):
        return pl.BlockSpec((1,) + shape, lambda g: (g, 0, 0))

    return pl.pallas_call(
        body, name=name, grid=(G,),
        in_specs=[g3((R, 256)), g3((256, 256)), g3((256, 128)), g3((128, 256)), g3((8, 128)), g3((8, 128))],
        out_specs=[g3((R, 256)), g3((R, 128))],
        out_shape=[jax.ShapeDtypeStruct((G, R, 256), BF16), jax.ShapeDtypeStruct((G, R, 128), F32)],
        compiler_params=_cparams("parallel"),
    )(uf, m_mat, e_mat, f_mat, tab_r, tab_i)


def _s5_bwd(dy, uf, xs, m_mat, e_mat, f_mat, tab_r, tab_i, nch, *, name):
    G, R, _ = uf.shape
    nsteps = int(math.log2(nch))

    def body(dy_ref, u_ref, xs_ref, m_ref, e_ref, f_ref, tr_ref, ti_ref, du_ref, dm_ref, de_ref, df_ref, da_ref):
        dyb = dy_ref[0].astype(BF16)
        u = u_ref[0].astype(BF16)
        xs = xs_ref[0]
        dm_ref[0] = _dot(u, dyb, _TN)
        df_ref[0] = _dot(xs.astype(BF16), dyb, _TN)
        gx = _dot(dyb, f_ref[0].astype(BF16), _NT)
        pos = jnp.bitwise_and(lax.broadcasted_iota(jnp.int32, (R, 2 * C_STATE), 0), nch - 1)
        for k in range(nsteps):
            s = 1 << k
            sh = pltpu.roll(gx, R - s, 0)
            upd = tr_ref[0, k:k + 1, :] * sh - ti_ref[0, k:k + 1, :] * pltpu.roll(sh, C_STATE, 1)
            gx = gx + jnp.where(pos + s < nch, upd, 0.0)
        de_in = jnp.where(pos + 1 < nch, pltpu.roll(gx, R - 1, 0), 0.0)
        deb = de_in.astype(BF16)
        de_ref[0] = _dot(u, deb, _TN)
        du_ref[0] = (_dot(dyb, m_ref[0].astype(BF16), _NT) + _dot(deb, e_ref[0].astype(BF16), _NT)).astype(BF16)
        da_ref[0] = jnp.zeros((8, 128), F32)
        da_ref[0, 0:1, :] = jnp.sum(de_in * xs, axis=0, keepdims=True)
        da_ref[0, 1:2, :] = jnp.sum(de_in * pltpu.roll(xs, C_STATE, 1), axis=0, keepdims=True)

    def g3(shape):
        return pl.BlockSpec((1,) + shape, lambda g: (g, 0, 0))

    return pl.pallas_call(
        body, name=name, grid=(G,),
        in_specs=[g3((R, 256)), g3((R, 256)), g3((R, 128)), g3((256, 256)), g3((256, 128)), g3((128, 256)),
                  g3((8, 128)), g3((8, 128))],
        out_specs=[g3((R, 256)), g3((256, 256)), g3((256, 128)), g3((128, 256)), g3((8, 128))],
        out_shape=[jax.ShapeDtypeStruct((G, R, 256), BF16), jax.ShapeDtypeStruct((G, 256, 256), F32),
                   jax.ShapeDtypeStruct((G, 256, 128), F32), jax.ShapeDtypeStruct((G, 128, 256), F32),
                   jax.ShapeDtypeStruct((G, 8, 128), F32)],
        compiler_params=_cparams("parallel"),
    )(dy, uf, xs, m_mat, e_mat, f_mat, tab_r, tab_i)


def _s5_act_fwd(ys, u, dsk, *, name):
    T, C = u.shape
    tm = 256

    def body(ys_ref, u_ref, d_ref, gl_ref, y_ref):
        y = ys_ref[...].astype(F32) + d_ref[...] * u_ref[...]
        y_ref[...] = y
        gl_ref[...] = (0.5 * y * (1.0 + _erf(y * (2.0 ** -0.5)))).astype(BF16)

    row = pl.BlockSpec((tm, C), lambda i: (i, 0))
    vec = pl.BlockSpec((1, C), lambda i: (0, 0))
    return pl.pallas_call(
        body, name=name, grid=(T // tm,), in_specs=[row, row, vec], out_specs=[row, row],
        out_shape=[jax.ShapeDtypeStruct((T, C), BF16), jax.ShapeDtypeStruct((T, C), F32)],
        compiler_params=_cparams("parallel"),
    )(ys, u, dsk)


def _s5_act_bwd(y, u, dsk, dgl, *, name):
    T, C = u.shape
    tm = 256

    def body(y_ref, u_ref, d_ref, dgl_ref, dy_ref, du_ref, dd_ref):
        y = y_ref[...]
        cdf = 0.5 * (1.0 + _erf(y * (2.0 ** -0.5)))
        pdf = jnp.exp(-0.5 * y * y) * (1.0 / math.sqrt(2.0 * math.pi))
        dy = dgl_ref[...].astype(F32) * (cdf + y * pdf)
        dy_ref[...] = dy
        du_ref[...] = dy * d_ref[...]
        part = jnp.sum(dy * u_ref[...], axis=0, keepdims=True)

        @pl.when(pl.program_id(0) == 0)
        def _():
            dd_ref[...] = part

        @pl.when(pl.program_id(0) > 0)
        def _():
            dd_ref[...] += part

    row = pl.BlockSpec((tm, C), lambda i: (i, 0))
    vec = pl.BlockSpec((1, C), lambda i: (0, 0))
    return pl.pallas_call(
        body, name=name, grid=(T // tm,), in_specs=[row, row, vec, row], out_specs=[row, row, vec],
        out_shape=[jax.ShapeDtypeStruct((T, C), F32), jax.ShapeDtypeStruct((T, C), F32), jax.ShapeDtypeStruct((1, C), F32)],
        compiler_params=_cparams("arbitrary"),
    )(y, u, dsk, dgl)


def _add2(a, b, *, name):
    T, C = a.shape
    tm = 256

    def body(a_ref, b_ref, o_ref):
        o_ref[...] = a_ref[...].astype(F32) + b_ref[...].astype(F32)

    row = pl.BlockSpec((tm, C), lambda i: (i, 0))
    return pl.pallas_call(body, name=name, grid=(T // tm,), in_specs=[row, row], out_specs=row,
                          out_shape=jax.ShapeDtypeStruct((T, C), F32), compiler_params=_cparams("parallel"))(a, b)


def _loss_head(y, target, *, name):
    T, C = y.shape
    tm = 256

    def body(y_ref, t_ref, l_ref, d_ref):
        err = y_ref[...] - t_ref[...]
        d_ref[...] = err * (1.0 / C)
        sq = err * err
        part = jnp.zeros((8, LANES), F32)
        for r in range(0, tm, 8):
            for c in range(0, C, LANES):
                part = part + sq[r:r + 8, c:c + LANES]

        @pl.when(pl.program_id(0) == 0)
        def _():
            l_ref[...] = part

        @pl.when(pl.program_id(0) > 0)
        def _():
            l_ref[...] += part

    row = pl.BlockSpec((tm, C), lambda i: (i, 0))
    acc = pl.BlockSpec((8, LANES), lambda i: (0, 0))
    return pl.pallas_call(
        body, name=name, grid=(T // tm,), in_specs=[row, row], out_specs=[acc, row],
        out_shape=[jax.ShapeDtypeStruct((8, LANES), F32), jax.ShapeDtypeStruct((T, C), F32)],
        compiler_params=_cparams("arbitrary"),
    )(y, target)


def _adamw(w, g, m, v, *, name):
    shape = w.shape
    size = int(np.prod(shape))
    cols = LANES if (shape[-1] < LANES and size % LANES == 0) else shape[-1]
    rows = size // cols
    tm = _tile(rows, 256) if rows % 8 == 0 else rows
    w2, g2, m2, v2 = (t.reshape(rows, cols) for t in (w, g, m, v))

    def body(w_ref, g_ref, m_ref, v_ref, d_ref, nm_ref, nv_ref):
        gg = g_ref[...]
        nm = ADAM_B1 * m_ref[...] + (1.0 - ADAM_B1) * gg
        nv = ADAM_B2 * v_ref[...] + (1.0 - ADAM_B2) * (gg * gg)
        m_hat = nm / (1.0 - ADAM_B1 ** ADAM_STEP)
        v_hat = nv / (1.0 - ADAM_B2 ** ADAM_STEP)
        d_ref[...] = -ADAM_LR * (m_hat / (jnp.sqrt(v_hat) + ADAM_EPS) + ADAM_WD * w_ref[...])
        nm_ref[...] = nm
        nv_ref[...] = nv

    blk = pl.BlockSpec((tm, cols), lambda i: (i, 0))
    outs = pl.pallas_call(
        body, name=name, grid=(rows // tm,), in_specs=[blk] * 4, out_specs=[blk] * 3,
        out_shape=[jax.ShapeDtypeStruct((rows, cols), F32)] * 3, compiler_params=_cparams("parallel"),
    )(w2, g2, m2, v2)
    return tuple(o.reshape(shape) for o in outs)


def _all_gather(shard, *, name):
    R, C = shard.shape

    def body(x_ref, out_ref, send_sems, recv_sems, local_sem):
        x, y, c = lax.axis_index("x"), lax.axis_index("y"), lax.axis_index("c")
        me, sibling = (x, y, c), (x, y, 1 - c)
        chips = [(1 - x, y), (x, 1 - y), (1 - x, 1 - y)]

        def rows(px, py, pc):
            return out_ref.at[_logical(px, py, pc)]

        def copy(k, block, to, src=None):
            return pltpu.make_async_remote_copy(
                src_ref=rows(*block) if src is None else src, dst_ref=rows(*block),
                send_sem=send_sems.at[k], recv_sem=recv_sems.at[k], device_id=to, device_id_type=_MESH)

        mine = pltpu.make_async_copy(x_ref, rows(*me), local_sem)
        mine.start()
        first = [copy(0, me, sibling, src=x_ref)]
        first += [copy(1 + j, me, (*chip, c), src=x_ref) for j, chip in enumerate(chips)]
        for cp in first:
            cp.start()
        passed = [copy(4 + j, (*chip, c), sibling) for j, chip in enumerate(chips)]
        for j, chip in enumerate(chips):
            copy(1 + j, (*chip, c), me).wait_recv()
            passed[j].start()
        copy(0, sibling, me).wait_recv()
        for j, chip in enumerate(chips):
            copy(4 + j, (*chip, 1 - c), me).wait_recv()
        for cp in first + passed:
            cp.wait_send()
        mine.wait()

    return pl.pallas_call(
        body, name=name, out_shape=jax.ShapeDtypeStruct((N_DEV, R, C), shard.dtype),
        in_specs=[_HBM], out_specs=_HBM,
        scratch_shapes=[pltpu.SemaphoreType.DMA((7,)), pltpu.SemaphoreType.DMA((7,)), pltpu.SemaphoreType.DMA],
    )(shard)


def _gather_weights(shards, *, name):
    nt = len(shards)

    def body(*refs):
        ins, outs = refs[:nt], refs[nt:2 * nt]
        send_sems, recv_sems, local_sems = refs[2 * nt:]
        x, y, c = lax.axis_index("x"), lax.axis_index("y"), lax.axis_index("c")
        me, sibling = (x, y, c), (x, y, 1 - c)
        chips = [(1 - x, y), (x, 1 - y), (1 - x, 1 - y)]

        def copy(t, k, block, to, src=None):
            rows = outs[t].at[_logical(*block)]
            return pltpu.make_async_remote_copy(
                src_ref=rows if src is None else src, dst_ref=rows,
                send_sem=send_sems.at[t, k], recv_sem=recv_sems.at[t, k], device_id=to, device_id_type=_MESH)

        mine = [pltpu.make_async_copy(ins[t], outs[t].at[_logical(*me)], local_sems.at[t]) for t in range(nt)]
        for cp in mine:
            cp.start()
        started = []
        for t in range(nt):
            started.append(copy(t, 0, me, sibling, src=ins[t]))
            started += [copy(t, 1 + j, me, (*chip, c), src=ins[t]) for j, chip in enumerate(chips)]
        for cp in started:
            cp.start()
        for j, chip in enumerate(chips):
            for t in range(nt):
                copy(t, 1 + j, (*chip, c), me).wait_recv()
                fwd = copy(t, 4 + j, (*chip, c), sibling)
                fwd.start()
                started.append(fwd)
        for t in range(nt):
            copy(t, 0, sibling, me).wait_recv()
            for j, chip in enumerate(chips):
                copy(t, 4 + j, (*chip, 1 - c), me).wait_recv()
        for cp in started:
            cp.wait_send()
        for cp in mine:
            cp.wait()

    return pl.pallas_call(
        body, name=name, out_shape=[jax.ShapeDtypeStruct((N_DEV,) + s.shape, s.dtype) for s in shards],
        in_specs=[_HBM] * nt, out_specs=[_HBM] * nt,
        scratch_shapes=[pltpu.SemaphoreType.DMA((nt, 7)), pltpu.SemaphoreType.DMA((nt, 7)), pltpu.SemaphoreType.DMA((nt,))],
    )(*shards)


def _sum_rows(stacked, *, name):
    _, R, C = stacked.shape
    tr = R
    if N_DEV * R * C * stacked.dtype.itemsize > 12 * 1024 * 1024:
        for cand in range(512, 15, -16):
            if R % cand == 0:
                tr = cand
                break

    def body(s_ref, o_ref):
        acc = s_ref[0].astype(F32)
        for k in range(1, N_DEV):
            acc = acc + s_ref[k].astype(F32)
        o_ref[...] = acc

    return pl.pallas_call(
        body, name=name, grid=(R // tr,),
        in_specs=[pl.BlockSpec((N_DEV, tr, C), lambda i: (0, i, 0))], out_specs=pl.BlockSpec((tr, C), lambda i: (i, 0)),
        out_shape=jax.ShapeDtypeStruct((R, C), F32), compiler_params=_cparams("parallel"),
    )(stacked)


_LARGE = (("ab_w_in", 1, True), ("ab_w_out", 1, False), ("s5_w_glu", 1, True), ("xattn_wq", 2, False),
          ("xattn_wkv", 2, True), ("xattn_wo", 2, False), ("ffn_w_in", 2, True), ("ffn_w_out", 2, False))
_LARGE_KEYS = tuple((n, l) for n, layers, _ in _LARGE for l in range(layers))
_TRANSPOSED = {n: t for n, _, t in _LARGE}


def _owner_major(name, w):
    return w.T if _TRANSPOSED[name] else w


def _lb_from_logits(logits):
    return jnp.cumsum(jax.nn.softmax(logits, axis=0), axis=0)[0:1]


def _to_groups(t, T):
    return t.reshape(T // C_TC, C_TC, C_GROUPS, C_GROUP).transpose(2, 0, 1, 3).reshape(C_GROUPS, T // C_TC, C_TC * C_GROUP)


def _from_groups(t, T):
    return t.reshape(C_GROUPS, T // C_TC, C_TC, C_GROUP).transpose(1, 2, 0, 3).reshape(T, D_MODEL)


def _local_step(x, mem, target, W, shards=None):
    B, L, _ = x.shape
    T = B * L
    x0 = x.reshape(T, D_MODEL)
    memf = mem.reshape(B * MEM_LEN, D_MODEL)
    nw = W["norm_w"]
    cos2, sin2 = _rope_tables(L)
    W = dict(W)
    G, received = {}, {}

    def mmx(a, b, gather=(), scatter=(), **kw):
        if shards is None or not (gather or scatter):
            return _mm(a, b, **kw)
        out, gathered, got = _mm(a, b, gather=[shards[k] for k in gather],
                                 scatter=[G[k].reshape(N_DEV, -1, D_MODEL) for k in scatter], **kw)
        for k, g in zip(gather, gathered):
            W[k] = g.reshape(-1, D_MODEL)
        for k, r in zip(scatter, got):
            received[k] = r
        return out

    def vec(v):
        return v.reshape(1, -1)

    saved = []
    xin = x0
    for layer in range(2):
        s = {"x0": xin}
        tag = f"l{layer}"
        h1 = _rms_fwd(xin, vec(nw[layer, 0]), name=f"norm_pre_mix_{tag}", out_dtype=BF16 if layer == 0 else F32)
        s["h1"] = h1
        if layer == 0:
            lb, lb_vjp = jax.vjp(_lb_from_logits, W["hgrn_lb_logits"])
            onw = W["hgrn_out_norm_w"].reshape(1, A_WIDTH)
            z = mmx(h1, W["ab_w_in", 0], tb=True, name="ab_in",
                    gather=[("ab_w_out", 0), ("xattn_wq", 0), ("xattn_wkv", 0), ("xattn_wo", 0), ("ffn_w_in", 0)])
            oa, o_raw, s_start = _hgrn_fwd(z, lb, onw, B, L, name="hgrn_fwd")
            qr = _rope_fwd(z, cos2, sin2, 16, B, L, name="rope_q")
            kr = _rope_fwd(z, cos2, sin2, 20, B, L, name="rope_k")
            os_, ls_ = [], []
            for dil in B_DILS:
                o_g, l_g = _dil_fwd(qr, kr, z, dil, B, L, name=f"dil_fwd_{dil}")
                os_.append(o_g)
                ls_.append(l_g)
            ob, lse = _dil_combine(os_, ls_, name="dil_combine")
            ymix = jnp.concatenate([oa, ob], axis=-1).astype(BF16)
            y1 = mmx(ymix, W["ab_w_out", 0], name="ab_out", gather=[("ffn_w_out", 0)])
            s.update(z=z, lb=lb, lb_vjp=lb_vjp, onw=onw, o_raw=o_raw, s_start=s_start, qr=qr, kr=kr, ob=ob, lse=lse, ymix=ymix)
        else:
            p5 = tuple(W[n][0] for n in ("s5_lambda_re", "s5_lambda_im", "s5_log_dt", "s5_b_re", "s5_b_im", "s5_c_re", "s5_c_im"))
            (m_mat, e_mat, f_mat, a16r, a16i), s5_vjp = jax.vjp(_s5_build, *p5)
            nch = L // C_TC
            tab_r, tab_i = _s5_scan_tables(p5[0], p5[1], p5[2], int(math.log2(nch)))
            uf = _to_groups(h1, T).astype(BF16)
            yf, xs = _s5_fwd(uf, m_mat, e_mat, f_mat, tab_r, tab_i, nch, name="s5_fwd")
            dsk = W["s5_d"].reshape(1, D_MODEL)
            gl, ypre = _s5_act_fwd(_from_groups(yf, T), h1, dsk, name="s5_act_fwd")
            w_glu = _interleave_rows(W["s5_w_glu", 0])
            zg = _mm(gl, w_glu, tb=True, out_dtype=BF16, name="s5_glu_in")
            y1 = _gated_fwd(zg, "glu", name="s5_glu")
            s.update(s5_vjp=s5_vjp, mats=(m_mat, e_mat, f_mat, tab_r, tab_i), uf=uf, xs=xs, dsk=dsk, gl=gl, ypre=ypre, zg=zg, nch=nch, w_glu=w_glu)
        x1 = _rms_fwd(y1, vec(nw[layer, 1]), xin, name=f"norm_post_mix_{tag}")
        h2 = _rms_fwd(x1, vec(nw[layer, 2]), name=f"norm_pre_x_{tag}", out_dtype=BF16)
        memn = _rms_fwd(memf, vec(W["mem_norm_w"][layer]), name=f"norm_mem_{tag}", out_dtype=BF16)
        q = mmx(h2, W["xattn_wq", layer], out_dtype=BF16, name=f"x_q_{tag}", gather=[("s5_w_glu", 0)] if layer == 0 else [])
        kv = _mm(memn, W["xattn_wkv", layer], tb=True, out_dtype=BF16, name=f"x_kv_{tag}")
        o = _xattn_fwd(q, kv, B, L, name=f"x_attn_{tag}")
        y2 = mmx(o, W["xattn_wo", layer], name=f"x_o_{tag}", gather=[("xattn_wq", 1), ("xattn_wo", 1)] if layer == 0 else [])
        x2 = _rms_fwd(y2, vec(nw[layer, 3]), x1, name=f"norm_post_x_{tag}")
        h3 = _rms_fwd(x2, vec(nw[layer, 4]), name=f"norm_pre_ffn_{tag}", out_dtype=BF16)
        w_ffn_in = _interleave_rows(W["ffn_w_in", layer])
        zf = mmx(h3, w_ffn_in, tb=True, out_dtype=BF16, name=f"ffn_in_{tag}",
                 gather=[("xattn_wkv", 1), ("ffn_w_in", 1), ("ffn_w_out", 1)] if layer == 0 else [])
        u = _gated_fwd(zf, "swiglu", name=f"ffn_act_{tag}")
        y3 = _mm(u, W["ffn_w_out", layer], name=f"ffn_out_{tag}")
        x3 = _rms_fwd(y3, vec(nw[layer, 5]), x2, name=f"norm_post_ffn_{tag}")
        s.update(y1=y1, x1=x1, h2=h2, memn=memn, q=q, kv=kv, o=o, y2=y2, x2=x2, h3=h3, zf=zf, u=u, y3=y3, w_ffn_in=w_ffn_in)
        saved.append(s)
        xin = x3

    loss_parts, dx = _loss_head(xin, target.reshape(T, D_MODEL), name="loss_head")

    d_norm = [[None] * 6 for _ in range(2)]
    d_memn = [None, None]
    for layer in (1, 0):
        s = saved[layer]
        tag = f"l{layer}"
        dy3, d_norm[layer][5] = _rms_bwd(s["y3"], vec(nw[layer, 5]), dx, name=f"bnorm_post_ffn_{tag}", out_dtype=BF16)
        du = mmx(dy3, W["ffn_w_out", layer], tb=True, out_dtype=BF16, name=f"b_ffn_out_dx_{tag}",
                 scatter=[("xattn_wkv", 1), ("s5_w_glu", 0)] if layer == 0 else [])
        G["ffn_w_out", layer] = _mm(s["u"], dy3, ta=True, out_dtype=BF16, name=f"b_ffn_out_dw_{tag}")
        dzf = _gated_bwd(s["zf"], du, "swiglu", name=f"b_ffn_act_{tag}")
        G["ffn_w_in", layer] = _interleave_rows(
            mmx(dzf, s["h3"], ta=True, out_dtype=BF16, name=f"b_ffn_in_dw_{tag}", scatter=[("ffn_w_out", layer)]), inverse=True)
        dh3 = mmx(dzf, s["w_ffn_in"], out_dtype=BF16, name=f"b_ffn_in_dx_{tag}", scatter=[("ffn_w_in", layer)])
        dx, d_norm[layer][4] = _rms_bwd(s["x2"], vec(nw[layer, 4]), dh3, dx, name=f"bnorm_pre_ffn_{tag}")
        dy2, d_norm[layer][3] = _rms_bwd(s["y2"], vec(nw[layer, 3]), dx, name=f"bnorm_post_x_{tag}", out_dtype=BF16)
        do = _mm(dy2, W["xattn_wo", layer], tb=True, out_dtype=BF16, name=f"b_x_o_dx_{tag}")
        G["xattn_wo", layer] = _mm(s["o"], dy2, ta=True, out_dtype=BF16, name=f"b_x_o_dw_{tag}")
        dq, dkv = _xattn_bwd(s["q"], s["kv"], do, B, L, name=f"b_x_attn_{tag}")
        G["xattn_wq", layer] = _mm(s["h2"], dq, ta=True, out_dtype=BF16, name=f"b_x_q_dw_{tag}")
        dh2 = _mm(dq, W["xattn_wq", layer], tb=True, out_dtype=BF16, name=f"b_x_q_dx_{tag}")
        G["xattn_wkv", layer] = _mm(dkv, s["memn"], ta=True, out_dtype=BF16, name=f"b_x_kv_dw_{tag}")
        dmemn = _mm(dkv, W["xattn_wkv", layer], out_dtype=BF16, name=f"b_x_kv_dx_{tag}")
        _, d_memn[layer] = _rms_bwd(memf, vec(W["mem_norm_w"][layer]), dmemn, name=f"bnorm_mem_{tag}", out_dtype=BF16)
        dx, d_norm[layer][2] = _rms_bwd(s["x1"], vec(nw[layer, 2]), dh2, dx, name=f"bnorm_pre_x_{tag}")
        dy1, d_norm[layer][1] = _rms_bwd(s["y1"], vec(nw[layer, 1]), dx, name=f"bnorm_post_mix_{tag}", out_dtype=BF16)
        if layer == 0:
            z = s["z"]
            dymix = _mm(dy1, W["ab_w_out", 0], tb=True, name="b_ab_out_dx")
            G["ab_w_out", 0] = _mm(s["ymix"], dy1, ta=True, out_dtype=BF16, name="b_ab_out_dw")
            dqa, dfa, dia, dga, d_onw, d_lb = _hgrn_bwd(z, s["lb"], s["onw"], s["o_raw"], s["s_start"], dymix, B, L, name="hgrn_bwd")
            dqs, dks, dvs = [], [], []
            for dil in B_DILS:
                dqs.append(_dil_bwd_q(s["qr"], s["kr"], z, dymix, s["ob"], s["lse"], dil, B, L, name=f"dil_bwd_q_{dil}"))
                dk_g, dv_g = _dil_bwd_kv(s["qr"], s["kr"], z, dymix, s["ob"], s["lse"], dil, B, L, name=f"dil_bwd_kv_{dil}")
                dks.append(dk_g)
                dvs.append(dv_g)
            dqb = _sum3(*dqs, cos2, sin2, True, B, L, name="b_rope_q")
            dkb = _sum3(*dks, cos2, sin2, True, B, L, name="b_rope_k")
            dvb = _sum3(*dvs, cos2, sin2, False, B, L, name="b_sum_v")
            dz = jnp.concatenate([dqa, dfa, dia, dga, dqb, dkb, dvb], axis=-1).astype(BF16)
            G["ab_w_in", 0] = mmx(dz, s["h1"], ta=True, out_dtype=BF16, name="b_ab_in_dw",
                                  scatter=[("xattn_wo", 0), ("xattn_wq", 0), ("xattn_wkv", 0), ("ab_w_out", 0)])
            dh1 = mmx(dz, W["ab_w_in", 0], out_dtype=BF16, name="b_ab_in_dx", scatter=[("ab_w_in", 0)])
            G["hgrn_out_norm_w"] = jnp.sum(d_onw.reshape(B, A_WIDTH), axis=0, keepdims=True)
            d_lb_row = jnp.sum(d_lb.reshape(B, A_WIDTH), axis=0, keepdims=True)
            G["hgrn_lb_logits"] = s["lb_vjp"](d_lb_row)[0]
        else:
            dzg = _gated_bwd(s["zg"], dy1, "glu", name="b_s5_glu")
            G["s5_w_glu", 0] = _interleave_rows(
                _mm(dzg, s["gl"], ta=True, out_dtype=BF16, name="b_s5_glu_dw"), inverse=True)
            dgl = mmx(dzg, s["w_glu"], out_dtype=BF16, name="b_s5_glu_dx", scatter=[("xattn_wo", 1), ("xattn_wq", 1)])
            dyp, du_skip, d_dsk = _s5_act_bwd(s["ypre"], s["h1"], s["dsk"], dgl, name="b_s5_act")
            m_mat, e_mat, f_mat, tab_r, tab_i = s["mats"]
            duf, dm, de, df_, da = _s5_bwd(_to_groups(dyp, T).astype(BF16), s["uf"], s["xs"], m_mat, e_mat, f_mat, tab_r, tab_i,
                                           s["nch"], name="s5_bwd")
            da_r = da[:, 0, :C_STATE] + da[:, 0, C_STATE:]
            da_i = da[:, 1, C_STATE:] - da[:, 1, :C_STATE]
            gp = s["s5_vjp"]((dm, de, df_, da_r, da_i))
            for n, gv in zip(("s5_lambda_re", "s5_lambda_im", "s5_log_dt", "s5_b_re", "s5_b_im", "s5_c_re", "s5_c_im"), gp):
                G[n] = gv[None]
            G["s5_d"] = d_dsk
            dh1 = _add2(_from_groups(duf, T), du_skip, name="b_s5_du")
        dx, d_norm[layer][0] = _rms_bwd(s["x0"], vec(nw[layer, 0]), dh1, dx, name=f"bnorm_pre_mix_{tag}")

    G["norm_w"] = jnp.stack([jnp.concatenate(d_norm[l], axis=0) for l in range(2)])
    G["mem_norm_w"] = jnp.concatenate(d_memn, axis=0)
    if shards is not None:
        G.update(received)
    return loss_parts, dx.reshape(B, L, D_MODEL), G


_SMALL = (("norm_w", (2, 6, 1024)), ("mem_norm_w", (2, 1024)), ("hgrn_lb_logits", (3, 512)), ("hgrn_out_norm_w", (1, 512)),
          ("s5_lambda_re", (1, 64, 64)), ("s5_lambda_im", (1, 64, 64)), ("s5_log_dt", (1, 64)),
          ("s5_b_re", (1, 64, 64, 16)), ("s5_b_im", (1, 64, 64, 16)), ("s5_c_re", (1, 64, 16, 64)),
          ("s5_c_im", (1, 64, 16, 64)), ("s5_d", (1, 1024)))

_WEIGHT_ORDER = ('norm_w', 'mem_norm_w', 'ab_w_in', 'ab_w_out', 'hgrn_lb_logits', 'hgrn_out_norm_w', 's5_lambda_re',
                 's5_lambda_im', 's5_log_dt', 's5_b_re', 's5_b_im', 's5_c_re', 's5_c_im', 's5_d', 's5_w_glu', 'xattn_wq',
                 'xattn_wkv', 'xattn_wo', 'ffn_w_in', 'ffn_w_out')


def kernel(x, mem, norm_w, mem_norm_w, ab_w_in, ab_w_out, hgrn_lb_logits, hgrn_out_norm_w, s5_lambda_re, s5_lambda_im, s5_log_dt, s5_b_re, s5_b_im, s5_c_re, s5_c_im, s5_d, s5_w_glu, xattn_wq, xattn_wkv, xattn_wo, ffn_w_in, ffn_w_out, loss_target, m_norm_w, m_mem_norm_w, m_ab_w_in, m_ab_w_out, m_hgrn_lb_logits, m_hgrn_out_norm_w, m_s5_lambda_re, m_s5_lambda_im, m_s5_log_dt, m_s5_b_re, m_s5_b_im, m_s5_c_re, m_s5_c_im, m_s5_d, m_s5_w_glu, m_xattn_wq, m_xattn_wkv, m_xattn_wo, m_ffn_w_in, m_ffn_w_out, v_norm_w, v_mem_norm_w, v_ab_w_in, v_ab_w_out, v_hgrn_lb_logits, v_hgrn_out_norm_w, v_s5_lambda_re, v_s5_lambda_im, v_s5_log_dt, v_s5_b_re, v_s5_b_im, v_s5_c_re, v_s5_c_im, v_s5_d, v_s5_w_glu, v_xattn_wq, v_xattn_wkv, v_xattn_wo, v_ffn_w_in, v_ffn_w_out):
    local = dict(norm_w=norm_w, mem_norm_w=mem_norm_w, ab_w_in=ab_w_in, ab_w_out=ab_w_out, hgrn_lb_logits=hgrn_lb_logits,
                 hgrn_out_norm_w=hgrn_out_norm_w, s5_lambda_re=s5_lambda_re, s5_lambda_im=s5_lambda_im, s5_log_dt=s5_log_dt,
                 s5_b_re=s5_b_re, s5_b_im=s5_b_im, s5_c_re=s5_c_re, s5_c_im=s5_c_im, s5_d=s5_d, s5_w_glu=s5_w_glu,
                 xattn_wq=xattn_wq, xattn_wkv=xattn_wkv, xattn_wo=xattn_wo, ffn_w_in=ffn_w_in, ffn_w_out=ffn_w_out)
    mom_m = dict(zip(_WEIGHT_ORDER, (m_norm_w, m_mem_norm_w, m_ab_w_in, m_ab_w_out, m_hgrn_lb_logits, m_hgrn_out_norm_w, m_s5_lambda_re, m_s5_lambda_im, m_s5_log_dt, m_s5_b_re, m_s5_b_im, m_s5_c_re, m_s5_c_im, m_s5_d, m_s5_w_glu, m_xattn_wq, m_xattn_wkv, m_xattn_wo, m_ffn_w_in, m_ffn_w_out)))
    mom_v = dict(zip(_WEIGHT_ORDER, (v_norm_w, v_mem_norm_w, v_ab_w_in, v_ab_w_out, v_hgrn_lb_logits, v_hgrn_out_norm_w, v_s5_lambda_re, v_s5_lambda_im, v_s5_log_dt, v_s5_b_re, v_s5_b_im, v_s5_c_re, v_s5_c_im, v_s5_d, v_s5_w_glu, v_xattn_wq, v_xattn_wkv, v_xattn_wo, v_ffn_w_in, v_ffn_w_out)))
    dev = 4 * lax.axis_index("x") + 2 * lax.axis_index("y") + lax.axis_index("c")

    shards = {(n, l): _owner_major(n, local[n][l]).astype(BF16) for n, l in _LARGE_KEYS}
    first = ("ab_w_in", 0)
    W = {first: _gather_weights([shards[first]], name="gather_first")[0].reshape(-1, D_MODEL)}
    tiny =jnp.concatenate([norm_w.reshape(-1), s5_d.reshape(-1)])
    tiny = jnp.pad(tiny, (0, 16 * LANES - tiny.shape[0])).reshape(16, LANES)
    tiny_all = _all_gather(tiny, name="gather_tiny").reshape(N_DEV, 16 * LANES)
    W["norm_w"] = tiny_all[:, :12 * LANES].reshape(N_DEV, 2, 6, LANES).transpose(1, 2, 0, 3).reshape(2, 6, D_MODEL)
    W["s5_d"] = tiny_all[:, 12 * LANES:13 * LANES].reshape(1, D_MODEL)
    for n in ("mem_norm_w", "hgrn_lb_logits", "hgrn_out_norm_w", "s5_lambda_re", "s5_lambda_im", "s5_log_dt",
              "s5_b_re", "s5_b_im", "s5_c_re", "s5_c_im"):
        W[n] = local[n]

    loss_parts, grad_x, G = _local_step(x, mem, loss_target, W, shards)

    g_layers = {}
    for n, l in _LARGE_KEYS:
        g = _sum_rows(G[n, l], name=f"sum_grads_{n}_{l}")
        g_layers.setdefault(n, []).append(g.T if _TRANSPOSED[n] else g)
    g_local = {n: jnp.stack(gl) for n, gl in g_layers.items()}
    small =jnp.concatenate([G[n].reshape(-1) for n, _ in _SMALL] + [0.5 / D_MODEL * jnp.sum(loss_parts).reshape(1)])
    n_small = small.shape[0]
    small = jnp.pad(small, (0, (-n_small) % (8 * LANES))).reshape(-1, LANES)
    small_sum = _sum_rows(_all_gather(small, name="gather_small"), name="sum_small").reshape(-1)
    g_full, off = {}, 0
    for n, shp in _SMALL:
        size = int(np.prod(shp))
        g_full[n] = small_sum[off:off + size].reshape(shp)
        off += size
    loss = small_sum[off]
    grads = dict(g_local)
    for n, shp in _SMALL:
        if n == "norm_w":
            grads[n] = lax.dynamic_slice_in_dim(g_full[n], dev * LANES, LANES, axis=2)
        elif n == "s5_d":
            grads[n] = lax.dynamic_slice_in_dim(g_full[n], dev * LANES, LANES, axis=1)
        else:
            grads[n] = g_full[n]

    delta, new_m, new_v = {}, {}, {}
    for n in _WEIGHT_ORDER:
        delta[n], new_m[n], new_v[n] = _adamw(local[n], grads[n], mom_m[n], mom_v[n], name=f"adamw_{n}")
    return (loss, grad_x, *[grads[n] for n in _WEIGHT_ORDER], *[delta[n] for n in _WEIGHT_ORDER],
            *[new_m[n] for n in _WEIGHT_ORDER], *[new_v[n] for n in _WEIGHT_ORDER])
```

```python
import functools
import math

import numpy as np
import jax
import jax.numpy as jnp
from jax import lax
from jax.experimental import pallas as pl
from jax.experimental.pallas import tpu as pltpu

F32 = jnp.float32
BF16 = jnp.bfloat16
HI = lax.Precision.HIGHEST

D_MODEL = 1024
NORM_EPS = 1e-6
A_WIDTH = 512
A_HEAD = 128
A_CHUNK = 32
A_SUPER = 256
B_SPAN = 128
B_DILS = (1, 4, 16)
ROPE_THETA = 10000.0
C_GROUPS = 64
C_GROUP = 16
C_STATE = 64
C_TC = 16
C_MIN_NEG_RE = -1e-4
MEM_LEN = 256
X_HEADS = 4
X_HD = 256
D_FF = 2816
N_DEV = 8
LANES = 128

ADAM_LR, ADAM_B1, ADAM_B2, ADAM_EPS, ADAM_WD, ADAM_STEP = 0.001, 0.9, 0.999, 1e-08, 0.01, 10

NEG_BIG = -1e30


def _tile(n, pref):
    for d in range(min(pref, n) // LANES * LANES, 0, -LANES):
        if n % d == 0:
            return d
    return n


def _cparams(*sem):
    return pltpu.CompilerParams(dimension_semantics=sem, vmem_limit_bytes=56 * 1024 * 1024)


def _sigmoid(x):
    return 0.5 * jnp.tanh(0.5 * x) + 0.5


def _erf(x):
    ax = jnp.abs(x)
    t = 1.0 / (1.0 + 0.3275911 * ax)
    poly = t * (0.254829592 + t * (-0.284496736 + t * (1.421413741 + t * (-1.453152027 + t * 1.061405429))))
    y = 1.0 - poly * jnp.exp(-ax * ax)
    return jnp.where(x < 0, -y, y)


_HBM = pl.BlockSpec(memory_space=pltpu.HBM)
_MESH = pl.DeviceIdType.MESH


def _logical(px, py, pc):
    return 4 * px + 2 * py + pc


class _Exchange:
    def __init__(self, gather=(), scatter=()):
        self.gather, self.scatter = list(gather), list(scatter)
        self.ng, self.n = len(self.gather), len(self.gather) + len(self.scatter)

    def operands(self):
        return self.gather + self.scatter

    def in_specs(self):
        return [_HBM] * self.n

    def out_shapes(self):
        return ([jax.ShapeDtypeStruct((N_DEV,) + g.shape, g.dtype) for g in self.gather]
                + [jax.ShapeDtypeStruct(s.shape, s.dtype) for s in self.scatter])

    def scratch(self):
        if not self.n:
            return []
        return [pltpu.SemaphoreType.DMA((self.n, 7)), pltpu.SemaphoreType.DMA((self.n, 7)), pltpu.SemaphoreType.DMA((self.n,))]

    def split(self, results):
        return list(results[:self.ng]), list(results[self.ng:])

    def run(self, ins, outs, sems, first, last):
        if not self.n:
            return
        send_sems, recv_sems, local_sems = sems
        x, y, c = lax.axis_index("x"), lax.axis_index("y"), lax.axis_index("c")
        me, sibling = _logical(x, y, c), (x, y, 1 - c)
        chips = [(1 - x, y), (x, 1 - y), (1 - x, 1 - y)]
        peers = [(x ^ (k >> 2), y ^ ((k >> 1) & 1), c ^ (k & 1)) for k in range(1, N_DEV)]

        def remote(t, k, src, dst, to):
            return pltpu.make_async_remote_copy(src_ref=src, dst_ref=dst, send_sem=send_sems.at[t, k],
                                                recv_sem=recv_sems.at[t, k], device_id=to, device_id_type=_MESH)

        def local(t):
            src = ins[t] if t < self.ng else ins[t].at[me]
            return pltpu.make_async_copy(src, outs[t].at[me], local_sems.at[t])

        @pl.when(first)
        def _():
            for t in range(self.n):
                local(t).start()
                if t < self.ng:
                    remote(t, 0, ins[t], outs[t].at[me], sibling).start()
                    for j, chip in enumerate(chips):
                        remote(t, 1 + j, ins[t], outs[t].at[me], (*chip, c)).start()
                else:
                    for k, peer in enumerate(peers):
                        remote(t, k, ins[t].at[_logical(*peer)], outs[t].at[me], peer).start()

        @pl.when(last)
        def _():
            for j, chip in enumerate(chips):
                for t in range(self.ng):
                    landed = outs[t].at[_logical(*chip, c)]
                    remote(t, 1 + j, ins[t], landed, sibling).wait_recv()
                    remote(t, 4 + j, landed, landed, sibling).start()
            for t in range(self.n):
                if t < self.ng:
                    remote(t, 0, ins[t], outs[t].at[_logical(*sibling)], sibling).wait_recv()
                    for j, chip in enumerate(chips):
                        remote(t, 4 + j, ins[t], outs[t].at[_logical(*chip, 1 - c)], sibling).wait_recv()
                    for k in range(7):
                        remote(t, k, ins[t], outs[t].at[me], sibling).wait_send()
                else:
                    for k, peer in enumerate(peers):
                        remote(t, k, ins[t].at[me], outs[t].at[_logical(*peer)], peer).wait_recv()
                    for k, peer in enumerate(peers):
                        remote(t, k, ins[t].at[_logical(*peer)], outs[t].at[me], peer).wait_send()
                local(t).wait()


_MM_VMEM_BUDGET = 36 * 1024 * 1024


def _mm(a, b, *, ta=False, tb=False, out_dtype=F32, name, tiles=(1408, 1408, 4096), gather=(), scatter=(), gate=None):
    M, K = (a.shape[1], a.shape[0]) if ta else a.shape
    N = b.shape[0] if tb else b.shape[1]
    assert (b.shape[1] if tb else b.shape[0]) == K
    gate_mode = gate[0] if gate else None
    tm, tn, tk = _tile(M, tiles[0]), _tile(N, tiles[1]), _tile(K, tiles[2])
    if gate_mode == "fwd":
        bs = _gate_block(N // 2)
        tn = 2 * bs
    elif gate_mode == "bwd":
        bs = _gate_block(N)
        tn = bs

    def vmem_bytes():
        out = 2 * tm * tn * jnp.dtype(out_dtype).itemsize + (4 * tm * tn if tk < K else 0)
        if gate_mode:
            out += 2 * 3 * tm * tn * 2
        return 2 * 2 * (tm * tk + tk * tn) + out

    while vmem_bytes() > _MM_VMEM_BUDGET:
        if tk > 512:
            tk = _tile(K, tk - LANES)
        elif tn > 256 and not gate_mode:
            tn = _tile(N, tn - LANES)
        else:
            tm = _tile(M, tm - LANES)
    ni, nj, nk = M // tm, N // tn, K // tk
    ex = _Exchange(gather, scatter)

    a_spec = pl.BlockSpec((tk, tm), lambda i, j, k: (k, i)) if ta else pl.BlockSpec((tm, tk), lambda i, j, k: (i, k))
    b_spec = pl.BlockSpec((tn, tk), lambda i, j, k: (j, k)) if tb else pl.BlockSpec((tk, tn), lambda i, j, k: (k, j))
    dims = (((0 if ta else 1,), (1 if tb else 0,)), ((), ()))
    n_acc = 1 if nk > 1 else 0
    n_in = 3 if gate_mode == "bwd" else 2
    n_out = 2 if gate_mode == "fwd" else 1

    def finish(val, in_refs, out_refs):
        if gate_mode is None:
            out_refs[0][...] = val.astype(out_refs[0].dtype)
        elif gate_mode == "fwd":
            out_refs[0][...] = val.astype(BF16)
            for p in range(tn // (2 * bs)):
                a_, b_ = val[:, 2 * p * bs:(2 * p + 1) * bs], val[:, (2 * p + 1) * bs:(2 * p + 2) * bs]
                out_refs[1][:, p * bs:(p + 1) * bs] = _gate_value(a_, b_, gate[1]).astype(BF16)
        else:
            z_ref = in_refs[2]
            for p in range(tn // bs):
                a_ = z_ref[:, 2 * p * bs:(2 * p + 1) * bs].astype(F32)
                b_ = z_ref[:, (2 * p + 1) * bs:(2 * p + 2) * bs].astype(F32)
                da, db = _gate_grads(a_, b_, val[:, p * bs:(p + 1) * bs], gate[1])
                out_refs[0][:, 2 * p * bs:(2 * p + 1) * bs] = da.astype(BF16)
                out_refs[0][:, (2 * p + 1) * bs:(2 * p + 2) * bs] = db.astype(BF16)

    def body(*refs):
        in_refs, rest = refs[:n_in], refs[n_in:]
        ex_in, out_refs = rest[:ex.n], rest[ex.n:ex.n + n_out]
        ex_out, scratch = rest[ex.n + n_out:2 * ex.n + n_out], rest[2 * ex.n + n_out:]
        i, j, k = pl.program_id(0), pl.program_id(1), pl.program_id(2)
        ex.run(ex_in, ex_out, scratch[n_acc:], (i == 0) & (j == 0) & (k == 0), (i == ni - 1) & (j == nj - 1) & (k == nk - 1))
        part = lax.dot_general(in_refs[0][...].astype(BF16), in_refs[1][...].astype(BF16), dims, preferred_element_type=F32)
        if nk == 1:
            finish(part, in_refs, out_refs)
            return
        acc_ref = scratch[0]

        @pl.when(k == 0)
        def _():
            acc_ref[...] = part

        @pl.when(k > 0)
        def _():
            acc_ref[...] += part

        @pl.when(k == nk - 1)
        def _():
            finish(acc_ref[...], in_refs, out_refs)

    tile = lambda width: pl.BlockSpec((tm, width), lambda i, j, k: (i, j))
    if gate_mode == "fwd":
        out_specs, out_shape = [tile(tn), tile(tn // 2)], [jax.ShapeDtypeStruct((M, N), BF16), jax.ShapeDtypeStruct((M, N // 2), BF16)]
    elif gate_mode == "bwd":
        out_specs, out_shape = [tile(2 * tn)], [jax.ShapeDtypeStruct((M, 2 * N), BF16)]
    else:
        out_specs, out_shape = [tile(tn)], [jax.ShapeDtypeStruct((M, N), out_dtype)]
    operands = [a, b] + ([gate[2]] if gate_mode == "bwd" else [])
    sem = ("arbitrary",) * 3 if ex.n else ("parallel", "parallel", "arbitrary")
    res = pl.pallas_call(
        body, name=name, grid=(ni, nj, nk),
        in_specs=[a_spec, b_spec] + ([tile(2 * tn)] if gate_mode == "bwd" else []) + ex.in_specs(),
        out_specs=out_specs + ex.in_specs(),
        out_shape=out_shape + ex.out_shapes(),
        scratch_shapes=([pltpu.VMEM((tm, tn), F32)] if nk > 1 else []) + ex.scratch(),
        compiler_params=_cparams(*sem),
    )(*operands, *ex.operands())
    main = res[0] if n_out == 1 else tuple(res[:n_out])
    if not ex.n:
        return main
    return (main,) + tuple(ex.split(res[n_out:]))


def _rms_fwd(x, w, res=None, *, name, out_dtype=F32):
    T, C = x.shape
    tm = _tile(T, 512)
    has_res = res is not None

    def body(*refs):
        x_ref, w_ref = refs[0], refs[1]
        o_ref = refs[-1]
        xv = x_ref[...].astype(F32)
        r = lax.rsqrt(jnp.mean(xv * xv, axis=-1, keepdims=True) + NORM_EPS)
        y = xv * r * w_ref[...]
        if has_res:
            y = y + refs[2][...]
        o_ref[...] = y.astype(o_ref.dtype)

    row = pl.BlockSpec((tm, C), lambda i: (i, 0))
    vec = pl.BlockSpec((1, C), lambda i: (0, 0))
    ins = [x, w] + ([res] if has_res else [])
    return pl.pallas_call(
        body, name=name, grid=(T // tm,), in_specs=[row, vec] + ([row] if has_res else []), out_specs=row,
        out_shape=jax.ShapeDtypeStruct((T, C), out_dtype), compiler_params=_cparams("parallel"),
    )(*ins)


def _rms_bwd(x, w, dy, add=None, *, name, out_dtype=F32):
    T, C = x.shape
    tm = _tile(T, 512)
    has_add = add is not None

    def body(*refs):
        x_ref, w_ref, dy_ref = refs[:3]
        dx_ref, dw_ref = refs[-2], refs[-1]
        xv = x_ref[...].astype(F32)
        r = lax.rsqrt(jnp.mean(xv * xv, axis=-1, keepdims=True) + NORM_EPS)
        xh = xv * r
        g = dy_ref[...].astype(F32)
        part = jnp.sum(g * xh, axis=0, keepdims=True)

        @pl.when(pl.program_id(0) == 0)
        def _():
            dw_ref[...] = part

        @pl.when(pl.program_id(0) > 0)
        def _():
            dw_ref[...] += part

        gx = g * w_ref[...]
        dx = r * (gx - xh * jnp.mean(gx * xh, axis=-1, keepdims=True))
        if has_add:
            dx = dx + refs[3][...]
        dx_ref[...] = dx.astype(dx_ref.dtype)

    row = pl.BlockSpec((tm, C), lambda i: (i, 0))
    vec = pl.BlockSpec((1, C), lambda i: (0, 0))
    ins = [x, w, dy] + ([add] if has_add else [])
    return pl.pallas_call(
        body, name=name, grid=(T // tm,), in_specs=[row, vec, row] + ([row] if has_add else []),
        out_specs=[row, vec],
        out_shape=[jax.ShapeDtypeStruct((T, C), out_dtype), jax.ShapeDtypeStruct((1, C), F32)],
        compiler_params=_cparams("arbitrary"),
    )(*ins)


def _gate_block(width):
    return _tile(width, 256)


def _gate_value(a, b, kind):
    return a * _sigmoid(a) * b if kind == "swiglu" else a * _sigmoid(b)


def _gate_grads(a, b, d, kind):
    if kind == "swiglu":
        s = _sigmoid(a)
        return d * b * (s * (1.0 + a * (1.0 - s))), d * a * s
    s = _sigmoid(b)
    return d * s, d * a * s * (1.0 - s)


def _interleave_rows(w, inverse=False):
    W2, C = w.shape
    bs = _gate_block(W2 // 2)
    nb = W2 // 2 // bs
    shape = (nb, 2, bs, C) if inverse else (2, nb, bs, C)
    return w.reshape(shape).transpose(1, 0, 2, 3).reshape(W2, C)


def _gated_bwd(z, dout, kind, *, name):
    T, W2 = z.shape
    W = W2 // 2
    tm, bs = _tile(T, 512), _gate_block(W)

    def body(z_ref, d_ref, o_ref):
        da, db = _gate_grads(z_ref[:, :bs].astype(F32), z_ref[:, bs:].astype(F32), d_ref[...].astype(F32), kind)
        o_ref[:, :bs] = da.astype(o_ref.dtype)
        o_ref[:, bs:] = db.astype(o_ref.dtype)

    return pl.pallas_call(
        body, name=name, grid=(T // tm, W // bs),
        in_specs=[pl.BlockSpec((tm, 2 * bs), lambda i, j: (i, j)), pl.BlockSpec((tm, bs), lambda i, j: (i, j))],
        out_specs=pl.BlockSpec((tm, 2 * bs), lambda i, j: (i, j)),
        out_shape=jax.ShapeDtypeStruct((T, W2), BF16), compiler_params=_cparams("parallel", "parallel"),
    )(z, dout)


_NT = (((1,), (1,)), ((), ()))
_TN = (((0,), (0,)), ((), ()))


def _dot(a, b, dims=None, precision=None):
    if dims is None:
        return jnp.dot(a, b, preferred_element_type=F32, precision=precision)
    return lax.dot_general(a, b, dims, preferred_element_type=F32, precision=precision)


def _xattn_fwd(q, kv, B, L, *, name):
    T = q.shape[0]
    tq = 256
    nq = L // tq
    scale = X_HD ** -0.5

    def body(q_ref, k_ref, v_ref, o_ref):
        for h in range(X_HEADS):
            sl = slice(h * X_HD, (h + 1) * X_HD)
            qh, kh, vh = q_ref[:, sl].astype(BF16), k_ref[:, sl].astype(BF16), v_ref[:, sl].astype(BF16)
            s = _dot(qh, kh, _NT) * scale
            m = jnp.max(s, axis=-1, keepdims=True)
            p = jnp.exp(s - m)
            l = jnp.sum(p, axis=-1, keepdims=True)
            o_ref[:, sl] = (_dot(p.astype(BF16), vh) / l).astype(BF16)

    return pl.pallas_call(
        body, name=name, grid=(B, nq),
        in_specs=[pl.BlockSpec((tq, D_MODEL), lambda b, i: (b * nq + i, 0)),
                  pl.BlockSpec((MEM_LEN, D_MODEL), lambda b, i: (b, 0)),
                  pl.BlockSpec((MEM_LEN, D_MODEL), lambda b, i: (b, 1))],
        out_specs=pl.BlockSpec((tq, D_MODEL), lambda b, i: (b * nq + i, 0)),
        out_shape=jax.ShapeDtypeStruct((T, D_MODEL), BF16), compiler_params=_cparams("parallel", "parallel"),
    )(q, kv, kv)


def _xattn_bwd(q, kv, do, B, L, *, name):
    T = q.shape[0]
    tq = 256
    nq = L // tq
    scale = X_HD ** -0.5

    def body(q_ref, k_ref, v_ref, do_ref, dq_ref, dkv_ref):
        @pl.when(pl.program_id(1) == 0)
        def _():
            dkv_ref[...] = jnp.zeros_like(dkv_ref)

        for h in range(X_HEADS):
            sl = slice(h * X_HD, (h + 1) * X_HD)
            slv = slice(D_MODEL + h * X_HD, D_MODEL + (h + 1) * X_HD)
            qh, kh, vh = q_ref[:, sl].astype(BF16), k_ref[:, sl].astype(BF16), v_ref[:, sl].astype(BF16)
            doh = do_ref[:, sl].astype(BF16)
            s = _dot(qh, kh, _NT) * scale
            m = jnp.max(s, axis=-1, keepdims=True)
            e = jnp.exp(s - m)
            p = e / jnp.sum(e, axis=-1, keepdims=True)
            dkv_ref[:, slv] += _dot(p.astype(BF16), doh, _TN)
            dp = _dot(doh, vh, _NT)
            ds = p * (dp - jnp.sum(dp * p, axis=-1, keepdims=True)) * scale
            dsb = ds.astype(BF16)
            dq_ref[:, sl] = _dot(dsb, kh).astype(BF16)
            dkv_ref[:, sl] += _dot(dsb, qh, _TN)

    return pl.pallas_call(
        body, name=name, grid=(B, nq),
        in_specs=[pl.BlockSpec((tq, D_MODEL), lambda b, i: (b * nq + i, 0)),
                  pl.BlockSpec((MEM_LEN, D_MODEL), lambda b, i: (b, 0)),
                  pl.BlockSpec((MEM_LEN, D_MODEL), lambda b, i: (b, 1)),
                  pl.BlockSpec((tq, D_MODEL), lambda b, i: (b * nq + i, 0))],
        out_specs=[pl.BlockSpec((tq, D_MODEL), lambda b, i: (b * nq + i, 0)),
                   pl.BlockSpec((MEM_LEN, 2 * D_MODEL), lambda b, i: (b, 0))],
        out_shape=[jax.ShapeDtypeStruct((T, D_MODEL), BF16), jax.ShapeDtypeStruct((B * MEM_LEN, 2 * D_MODEL), F32)],
        compiler_params=_cparams("parallel", "arbitrary"),
    )(q, kv, kv, do)


def _chunk_masks():
    row = lax.broadcasted_iota(jnp.int32, (A_SUPER, A_SUPER), 0)
    col = lax.broadcasted_iota(jnp.int32, (A_SUPER, A_SUPER), 1)
    same = jnp.right_shift(row, 5) == jnp.right_shift(col, 5)
    return same, same & (col <= row), same & (col >= row)


def _hgrn_gates(fa, lb):
    sig = _sigmoid(fa)
    f = lb + (1.0 - lb) * sig
    return sig, f, jnp.log(f), 1.0 - f


def _hgrn_fwd(z, lb, onw, B, L, *, name):
    T = B * L
    ns = L // A_SUPER
    nch = A_SUPER // A_CHUNK

    def body(q_ref, f_ref, v_ref, g_ref, lb_ref, w_ref, oa_ref, o_ref, s_ref, st_ref, sc_ref):
        @pl.when(pl.program_id(2) == 0)
        def _():
            st_ref[...] = jnp.zeros_like(st_ref)

        s_ref[0] = st_ref[...]
        same, tril, _ = _chunk_masks()
        q, v = q_ref[...], v_ref[...]
        _, _, lf, k = _hgrn_gates(f_ref[...], lb_ref[...])
        bcs = _dot(tril.astype(F32), lf, precision=HI)
        bl = _dot(same.astype(F32), lf, precision=HI)
        qd = (q * jnp.exp(bcs)).astype(BF16)
        ki = (k * jnp.exp(-bcs)).astype(BF16)
        ke = (k * jnp.exp(bl - bcs)).astype(BF16)
        dec = jnp.exp(bl)
        vb = v.astype(BF16)
        a = jnp.where(tril, _dot(qd, ki, _NT), 0.0)
        o_ref[...] = _dot(a.astype(BF16), vb)
        chunks = [slice(c * A_CHUNK, (c + 1) * A_CHUNK) for c in range(nch)]
        outer = [_dot(vb[rs], ke[rs], _TN) for rs in chunks]
        st = st_ref[...]
        for c, rs in enumerate(chunks):
            sc_ref[c] = st.astype(BF16)
            st = st * dec[c * A_CHUNK:c * A_CHUNK + 1, :] + outer[c]
        st_ref[...] = st
        for c, rs in enumerate(chunks):
            o_ref[rs, :] += _dot(qd[rs], sc_ref[c], _NT)
        o = o_ref[...]
        r = lax.rsqrt(jnp.mean(o * o, axis=-1, keepdims=True) + NORM_EPS)
        g = g_ref[...]
        oa_ref[...] = o * r * w_ref[...] * (g * _sigmoid(g))

    def zspec(off):
        return pl.BlockSpec((A_SUPER, A_HEAD), lambda b, h, n: (b * ns + n, off + h))

    hvec = pl.BlockSpec((1, A_HEAD), lambda b, h, n: (0, h))
    ospec = pl.BlockSpec((A_SUPER, A_HEAD), lambda b, h, n: (b * ns + n, h))
    return pl.pallas_call(
        body, name=name, grid=(B, 4, ns),
        in_specs=[zspec(0), zspec(4), zspec(8), zspec(12), hvec, hvec],
        out_specs=[ospec, ospec, pl.BlockSpec((1, A_HEAD, A_HEAD), lambda b, h, n: ((b * 4 + h) * ns + n, 0, 0))],
        out_shape=[jax.ShapeDtypeStruct((T, A_WIDTH), F32), jax.ShapeDtypeStruct((T, A_WIDTH), F32),
                   jax.ShapeDtypeStruct((B * 4 * ns, A_HEAD, A_HEAD), F32)],
        scratch_shapes=[pltpu.VMEM((A_HEAD, A_HEAD), F32), pltpu.VMEM((nch, A_HEAD, A_HEAD), BF16)],
        compiler_params=_cparams("parallel", "parallel", "arbitrary"),
    )(z, z, z, z, lb, onw)


def _hgrn_bwd(z, lb, onw, o_raw, s_start, doa, B, L, *, name):
    T = B * L
    ns = L // A_SUPER
    nch = A_SUPER // A_CHUNK

    def body(q_ref, f_ref, v_ref, g_ref, lb_ref, w_ref, o_ref, s_ref, doa_ref,
             dq_ref, df_ref, dv_ref, dg_ref, dw_ref, dlb_ref, dst_ref, sc_ref, dsc_ref, dqd_ref, dke_ref, dblx_ref):
        @pl.when(pl.program_id(2) == 0)
        def _():
            dst_ref[...] = jnp.zeros_like(dst_ref)
            dw_ref[...] = jnp.zeros_like(dw_ref)
            dlb_ref[...] = jnp.zeros_like(dlb_ref)

        same, tril, triu = _chunk_masks()
        q, v, g, lb, w = q_ref[...], v_ref[...], g_ref[...], lb_ref[...], w_ref[...]
        sig, f, lf, k = _hgrn_gates(f_ref[...], lb)
        bcs = _dot(tril.astype(F32), lf, precision=HI)
        bl = _dot(same.astype(F32), lf, precision=HI)
        eb, enb, eeb = jnp.exp(bcs), jnp.exp(-bcs), jnp.exp(bl - bcs)
        qd, ki, ke = q * eb, k * enb, k * eeb
        qdb, kib, keb, vb = qd.astype(BF16), ki.astype(BF16), ke.astype(BF16), v.astype(BF16)
        dec = jnp.exp(bl)
        o = o_ref[...]
        r = lax.rsqrt(jnp.mean(o * o, axis=-1, keepdims=True) + NORM_EPS)
        on = o * r
        sg = _sigmoid(g)
        silu_g = g * sg
        doa = doa_ref[...]
        dg_ref[...] = doa * on * w * (sg * (1.0 + g * (1.0 - sg)))
        dw_ref[0] += jnp.sum(doa * on * silu_g, axis=0, keepdims=True)
        don = doa * w * silu_g
        do = r * (don - on * jnp.mean(don * on, axis=-1, keepdims=True))
        dob = do.astype(BF16)
        a = jnp.where(tril, _dot(qdb, kib, _NT), 0.0).astype(BF16)
        da = jnp.where(tril, _dot(dob, vb, _NT), 0.0).astype(BF16)
        dv_ref[...] = _dot(a, dob, _TN)
        dqd_ref[...] = _dot(da, kib)
        dki = _dot(da, qdb, _TN)
        chunks = [slice(c * A_CHUNK, (c + 1) * A_CHUNK) for c in range(nch)]
        outer = [_dot(vb[rs], keb[rs], _TN) for rs in chunks]
        st = s_ref[0]
        for c in range(nch):
            sc_ref[c] = st
            st = st * dec[c * A_CHUNK:c * A_CHUNK + 1, :] + outer[c]
        outer_g = [_dot(dob[rs], qdb[rs], _TN) for rs in chunks]
        dst = dst_ref[...]
        for c in reversed(range(nch)):
            dsc_ref[c] = dst
            dst = dst * dec[c * A_CHUNK:c * A_CHUNK + 1, :] + outer_g[c]
        dst_ref[...] = dst
        for c, rs in enumerate(chunks):
            dec_c = dec[c * A_CHUNK:c * A_CHUNK + 1, :]
            dsc, stc = dsc_ref[c], sc_ref[c]
            dscb = dsc.astype(BF16)
            dv_ref[rs, :] += _dot(keb[rs], dscb, _NT)
            dke_ref[rs, :] = _dot(vb[rs], dscb)
            ddec = jnp.sum(dsc * stc, axis=0, keepdims=True)
            dqd_ref[rs, :] += _dot(dob[rs], stc.astype(BF16))
            dblx_ref[rs, :] = jnp.broadcast_to(ddec * dec_c, (A_CHUNK, A_HEAD))
        dqd, dke = dqd_ref[...], dke_ref[...]
        dq_ref[...] = dqd * eb
        keke = dke * ke
        db = dqd * qd - dki * ki - keke
        dbl = _dot(same.astype(F32), keke, precision=HI) + dblx_ref[...]
        dk = dki * enb + dke * eeb
        dlf = _dot(triu.astype(F32), db, precision=HI) + dbl
        dff = dlf / f - dk
        df_ref[...] = dff * (1.0 - lb) * sig * (1.0 - sig)
        dlb_ref[0] += jnp.sum(dff * (1.0 - sig), axis=0, keepdims=True)

    def rev(n):
        return ns - 1 - n

    def zspec(off):
        return pl.BlockSpec((A_SUPER, A_HEAD), lambda b, h, n: (b * ns + rev(n), off + h))

    hvec = pl.BlockSpec((1, A_HEAD), lambda b, h, n: (0, h))
    ospec = pl.BlockSpec((A_SUPER, A_HEAD), lambda b, h, n: (b * ns + rev(n), h))
    acc = pl.BlockSpec((1, 1, A_HEAD), lambda b, h, n: (b * 4 + h, 0, 0))
    big = jax.ShapeDtypeStruct((T, A_WIDTH), F32)
    small = jax.ShapeDtypeStruct((B * 4, 1, A_HEAD), F32)
    return pl.pallas_call(
        body, name=name, grid=(B, 4, ns),
        in_specs=[zspec(0), zspec(4), zspec(8), zspec(12), hvec, hvec, ospec,
                  pl.BlockSpec((1, A_HEAD, A_HEAD), lambda b, h, n: ((b * 4 + h) * ns + rev(n), 0, 0)), ospec],
        out_specs=[ospec, ospec, ospec, ospec, acc, acc],
        out_shape=[big, big, big, big, small, small],
        scratch_shapes=[pltpu.VMEM((A_HEAD, A_HEAD), F32), pltpu.VMEM((nch, A_HEAD, A_HEAD), F32),
                        pltpu.VMEM((nch, A_HEAD, A_HEAD), F32),
                        pltpu.VMEM((A_SUPER, A_HEAD), F32), pltpu.VMEM((A_SUPER, A_HEAD), F32),
                        pltpu.VMEM((A_SUPER, A_HEAD), F32)],
        compiler_params=_cparams("parallel", "parallel", "arbitrary"),
    )(z, z, z, z, lb, onw, o_raw, s_start, doa)


def _rope_tables(L):
    half = A_HEAD // 2
    inv_freq = ROPE_THETA ** (-jnp.arange(half, dtype=F32) / half)
    ang = jnp.arange(L, dtype=F32)[:, None] * inv_freq[None, :]
    cos, sin = jnp.cos(ang), jnp.sin(ang)
    return jnp.concatenate([cos, cos], axis=-1), jnp.concatenate([-sin, sin], axis=-1)


def _rope_fwd(z, cos2, sin2, col_off, B, L, *, name):
    T = B * L
    tm = 256
    nl = L // tm

    def body(x_ref, c_ref, s_ref, o_ref):
        x = x_ref[...]
        o_ref[...] = x * c_ref[...] + pltpu.roll(x, A_HEAD // 2, 1) * s_ref[...]

    tab = pl.BlockSpec((tm, A_HEAD), lambda i, h: (i % nl, 0))
    return pl.pallas_call(
        body, name=name, grid=(T // tm, 4),
        in_specs=[pl.BlockSpec((tm, A_HEAD), lambda i, h: (i, col_off + h)), tab, tab],
        out_specs=pl.BlockSpec((tm, A_HEAD), lambda i, h: (i, h)),
        out_shape=jax.ShapeDtypeStruct((T, 512), F32), compiler_params=_cparams("parallel", "parallel"),
    )(z, cos2, sin2)


def _sum3(d1, d2, d3, cos2, sin2, rotate, B, L, *, name):
    T = B * L
    tm = 256
    nl = L // tm

    def body(a_ref, b_ref, c_ref, cs_ref, sn_ref, o_ref):
        d = a_ref[...] + b_ref[...] + c_ref[...]
        if rotate:
            d = d * cs_ref[...] - pltpu.roll(d, A_HEAD // 2, 1) * sn_ref[...]
        o_ref[...] = d

    blk = pl.BlockSpec((tm, A_HEAD), lambda i, h: (i, h))
    tab = pl.BlockSpec((tm, A_HEAD), lambda i, h: (i % nl, 0))
    return pl.pallas_call(
        body, name=name, grid=(T // tm, 4), in_specs=[blk, blk, blk, tab, tab], out_specs=blk,
        out_shape=jax.ShapeDtypeStruct((T, 512), F32), compiler_params=_cparams("parallel", "parallel"),
    )(d1, d2, d3, cos2, sin2)


def _band_masks():
    i = lax.broadcasted_iota(jnp.int32, (B_SPAN, B_SPAN), 0)
    j = lax.broadcasted_iota(jnp.int32, (B_SPAN, B_SPAN), 1)
    return i <= j, j <= i


def _dil_geom(dil, L):
    return B_SPAN * dil, L // (B_SPAN * dil), (1 if dil > 1 else 4)


def _dil_rows(r, dil):
    return pl.ds(r, B_SPAN, stride=dil) if dil > 1 else pl.ds(0, B_SPAN)


def _dil_loop(dil, step):
    if dil == 1:
        step(0, 0)
    else:
        lax.fori_loop(0, dil, step, 0)


def _dil_specs(dil, B, L):
    W, nb, hb = _dil_geom(dil, L)
    cw = A_HEAD * hb

    def at(col0, shift):
        def index(b, n, hh):
            return (b * nb + jnp.clip(n + shift, 0, nb - 1), col0 // cw + hh)
        return pl.BlockSpec((W, cw), index)

    return at


def _dil_fwd(qr, kr, z, dil, B, L, *, name):
    T = B * L
    W, nb, hb = _dil_geom(dil, L)
    has_prev = nb > 1
    scale = A_HEAD ** -0.5
    at = _dil_specs(dil, B, L)

    def body(*refs):
        if has_prev:
            q_ref, kc_ref, vc_ref, kp_ref, vp_ref, o_ref, l_ref = refs
        else:
            q_ref, kc_ref, vc_ref, o_ref, l_ref = refs
        mp, mc = _band_masks()
        mp = mp & (pl.program_id(1) > 0)

        def step(r, carry):
            rows = _dil_rows(r, dil)
            for h in range(hb):
                cols = pl.ds(h * A_HEAD, A_HEAD)
                qh = q_ref[rows, cols].astype(BF16)
                sc = jnp.where(mc, _dot(qh, kc_ref[rows, cols].astype(BF16), _NT) * scale, NEG_BIG)
                m = jnp.max(sc, axis=-1, keepdims=True)
                if has_prev:
                    sp = jnp.where(mp, _dot(qh, kp_ref[rows, cols].astype(BF16), _NT) * scale, NEG_BIG)
                    m = jnp.maximum(m, jnp.max(sp, axis=-1, keepdims=True))
                pc = jnp.exp(sc - m)
                l = jnp.sum(pc, axis=-1, keepdims=True)
                o = _dot(pc.astype(BF16), vc_ref[rows, cols].astype(BF16))
                if has_prev:
                    pp = jnp.exp(sp - m)
                    l = l + jnp.sum(pp, axis=-1, keepdims=True)
                    o = o + _dot(pp.astype(BF16), vp_ref[rows, cols].astype(BF16))
                o_ref[rows, cols] = o / l
                l_ref[rows, cols] = jnp.broadcast_to(m + jnp.log(l), (B_SPAN, A_HEAD))
            return carry

        _dil_loop(dil, step)

    ins = [qr, kr, z] + ([kr, z] if has_prev else [])
    in_specs = [at(0, 0), at(0, 0), at(3072, 0)] + ([at(0, -1), at(3072, -1)] if has_prev else [])
    return pl.pallas_call(
        body, name=name, grid=(B, nb, 4 // hb), in_specs=in_specs, out_specs=[at(0, 0), at(0, 0)],
        out_shape=[jax.ShapeDtypeStruct((T, 512), F32)] * 2,
        compiler_params=_cparams("parallel", "parallel", "parallel"),
    )(*ins)


def _dil_combine(os_, ls_, *, name):
    T = os_[0].shape[0]
    tm = 256

    def body(o1, o2, o3, l1, l2, l3, ob_ref, lse_ref):
        a1, a2, a3 = l1[...], l2[...], l3[...]
        m = jnp.maximum(jnp.maximum(a1, a2), a3)
        e1, e2, e3 = jnp.exp(a1 - m), jnp.exp(a2 - m), jnp.exp(a3 - m)
        den = e1 + e2 + e3
        ob_ref[...] = (e1 * o1[...] + e2 * o2[...] + e3 * o3[...]) / den
        lse_ref[...] = m + jnp.log(den)

    blk = pl.BlockSpec((tm, 512), lambda i: (i, 0))
    return pl.pallas_call(
        body, name=name, grid=(T // tm,), in_specs=[blk] * 6, out_specs=[blk, blk],
        out_shape=[jax.ShapeDtypeStruct((T, 512), F32)] * 2, compiler_params=_cparams("parallel"),
    )(*[o.reshape(T, 512) for o in os_], *[l.reshape(T, 512) for l in ls_])


def _dil_bwd_q(qr, kr, z, dymix, out, lse, dil, B, L, *, name):
    T = B * L
    W, nb, hb = _dil_geom(dil, L)
    has_prev = nb > 1
    scale = A_HEAD ** -0.5
    at = _dil_specs(dil, B, L)

    def body(*refs):
        if has_prev:
            q_ref, kc_ref, vc_ref, do_ref, out_ref, lse_ref, kp_ref, vp_ref, dq_ref = refs
        else:
            q_ref, kc_ref, vc_ref, do_ref, out_ref, lse_ref, dq_ref = refs
        mp, mc = _band_masks()
        mp = mp & (pl.program_id(1) > 0)

        def step(r, carry):
            rows = _dil_rows(r, dil)
            for h in range(hb):
                cols = pl.ds(h * A_HEAD, A_HEAD)
                qh = q_ref[rows, cols].astype(BF16)
                do = do_ref[rows, cols]
                delta = jnp.sum(do * out_ref[rows, cols], axis=-1, keepdims=True)
                dob = do.astype(BF16)
                lse_h = lse_ref[rows, cols]
                kc = kc_ref[rows, cols].astype(BF16)
                pc = jnp.where(mc, jnp.exp(_dot(qh, kc, _NT) * scale - lse_h), 0.0)
                dsc = pc * (_dot(dob, vc_ref[rows, cols].astype(BF16), _NT) - delta) * scale
                dq = _dot(dsc.astype(BF16), kc)
                if has_prev:
                    kp = kp_ref[rows, cols].astype(BF16)
                    pp = jnp.where(mp, jnp.exp(_dot(qh, kp, _NT) * scale - lse_h), 0.0)
                    dsp = pp * (_dot(dob, vp_ref[rows, cols].astype(BF16), _NT) - delta) * scale
                    dq = dq + _dot(dsp.astype(BF16), kp)
                dq_ref[rows, cols] = dq
            return carry

        _dil_loop(dil, step)

    ins = [qr, kr, z, dymix, out, lse] + ([kr, z] if has_prev else [])
    in_specs = ([at(0, 0), at(0, 0), at(3072, 0), at(512, 0), at(0, 0), at(0, 0)]
                + ([at(0, -1), at(3072, -1)] if has_prev else []))
    return pl.pallas_call(
        body, name=name, grid=(B, nb, 4 // hb), in_specs=in_specs, out_specs=at(0, 0),
        out_shape=jax.ShapeDtypeStruct((T, 512), F32),
        compiler_params=_cparams("parallel", "parallel", "parallel"),
    )(*ins)


def _dil_bwd_kv(qr, kr, z, dymix, out, lse, dil, B, L, *, name):
    T = B * L
    W, nb, hb = _dil_geom(dil, L)
    has_next = nb > 1
    scale = A_HEAD ** -0.5
    at = _dil_specs(dil, B, L)

    def body(*refs):
        k_ref, v_ref = refs[0], refs[1]
        own = refs[2:6]
        nxt = refs[6:10] if has_next else None
        dk_ref, dv_ref = refs[-2], refs[-1]
        mp, mc = _band_masks()
        mp = mp & (pl.program_id(1) < nb - 1)
        groups = [(own, mc)] + ([(nxt, mp)] if has_next else [])

        def step(r, carry):
            rows = _dil_rows(r, dil)
            for h in range(hb):
                cols = pl.ds(h * A_HEAD, A_HEAD)
                kh, vh = k_ref[rows, cols].astype(BF16), v_ref[rows, cols].astype(BF16)
                dk = jnp.zeros((B_SPAN, A_HEAD), F32)
                dv = jnp.zeros((B_SPAN, A_HEAD), F32)
                for (q_ref, do_ref, out_ref, lse_ref), mask in groups:
                    qh = q_ref[rows, cols].astype(BF16)
                    do = do_ref[rows, cols]
                    delta = jnp.sum(do * out_ref[rows, cols], axis=-1, keepdims=True)
                    dob = do.astype(BF16)
                    p = jnp.where(mask, jnp.exp(_dot(qh, kh, _NT) * scale - lse_ref[rows, cols]), 0.0)
                    dv = dv + _dot(p.astype(BF16), dob, _TN)
                    ds = p * (_dot(dob, vh, _NT) - delta) * scale
                    dk = dk + _dot(ds.astype(BF16), qh, _TN)
                dk_ref[rows, cols] = dk
                dv_ref[rows, cols] = dv
            return carry

        _dil_loop(dil, step)

    ins = [kr, z, qr, dymix, out, lse] + ([qr, dymix, out, lse] if has_next else [])
    in_specs = ([at(0, 0), at(3072, 0), at(0, 0), at(512, 0), at(0, 0), at(0, 0)]
                + ([at(0, 1), at(512, 1), at(0, 1), at(0, 1)] if has_next else []))
    return pl.pallas_call(
        body, name=name, grid=(B, nb, 4 // hb), in_specs=in_specs, out_specs=[at(0, 0), at(0, 0)],
        out_shape=[jax.ShapeDtypeStruct((T, 512), F32)] * 2,
        compiler_params=_cparams("parallel", "parallel", "parallel"),
    )(*ins)


def _s5_build(lam_re, lam_im, log_dt, b_re, b_im, c_re, c_im):
    G, P, TC = C_GROUPS, C_STATE, C_TC
    lr = jnp.minimum(lam_re, C_MIN_NEG_RE)
    li = lam_im
    dt = jnp.exp(log_dt)[:, None]
    mag = jnp.exp(dt * lr)
    ar, ai = mag * jnp.cos(dt * li), mag * jnp.sin(dt * li)
    den = lr * lr + li * li
    zr = ((ar - 1.0) * lr + ai * li) / den
    zi = (ai * lr - (ar - 1.0) * li) / den
    bbr = zr[..., None] * b_re - zi[..., None] * b_im
    bbi = zr[..., None] * b_im + zi[..., None] * b_re
    ks = jnp.arange(TC + 1, dtype=F32)[:, None, None]
    pmag = jnp.exp(ks * (dt * lr)[None])
    pr, pi = pmag * jnp.cos(ks * (dt * li)[None]), pmag * jnp.sin(ks * (dt * li)[None])
    car = c_re[None] * pr[:, :, None, :] - c_im[None] * pi[:, :, None, :]
    cai = c_re[None] * pi[:, :, None, :] + c_im[None] * pr[:, :, None, :]
    kern = (jnp.einsum('lgop,gpc->lgco', car[:TC], bbr, precision=HI)
            - jnp.einsum('lgop,gpc->lgco', cai[:TC], bbi, precision=HI))
    s_idx = jnp.arange(TC)[:, None]
    t_idx = jnp.arange(TC)[None, :]
    lag = t_idx - s_idx
    ksel = jnp.where((lag >= 0)[:, :, None, None, None], kern[jnp.clip(lag, 0, TC - 1)], 0.0)
    m_mat = ksel.transpose(2, 0, 3, 1, 4).reshape(G, TC * C_GROUP, TC * C_GROUP)
    pr_e, pi_e = pr[TC - 1 - jnp.arange(TC)], pi[TC - 1 - jnp.arange(TC)]
    er = pr_e[:, :, :, None] * bbr[None] - pi_e[:, :, :, None] * bbi[None]
    ei = pr_e[:, :, :, None] * bbi[None] + pi_e[:, :, :, None] * bbr[None]
    e_mat = jnp.concatenate([er.transpose(1, 0, 3, 2), ei.transpose(1, 0, 3, 2)], axis=-1).reshape(G, TC * C_GROUP, 2 * P)
    fr = car[1:].transpose(1, 3, 0, 2)
    fi = -cai[1:].transpose(1, 3, 0, 2)
    f_mat = jnp.concatenate([fr, fi], axis=1).reshape(G, 2 * P, TC * C_GROUP)
    return m_mat, e_mat, f_mat, pr[TC], pi[TC]


def _s5_scan_tables(lam_re, lam_im, log_dt, nsteps):
    lr = jnp.minimum(lam_re, C_MIN_NEG_RE)
    dt = jnp.exp(log_dt)[:, None]
    ks = (C_TC * 2.0 ** jnp.arange(8, dtype=F32))[None, :, None]
    keep = (jnp.arange(8) < nsteps)[None, :, None]
    pmag = jnp.exp(ks * (dt * lr)[:, None, :])
    ang = ks * (dt * lam_im)[:, None, :]
    pr = jnp.where(keep, pmag * jnp.cos(ang), 0.0)
    pi = jnp.where(keep, pmag * jnp.sin(ang), 0.0)
    return jnp.concatenate([pr, pr], axis=-1), jnp.concatenate([-pi, pi], axis=-1)


def _s5_fwd(uf, m_mat, e_mat, f_mat, tab_r, tab_i, nch, *, name):
    G, R, _ = uf.shape
    nsteps = int(math.log2(nch))

    def body(u_ref, m_ref, e_ref, f_ref, tr_ref, ti_ref, y_ref, xs_ref):
        u = u_ref[0].astype(BF16)
        x = _dot(u, e_ref[0].astype(BF16))
        pos = jnp.bitwise_and(lax.broadcasted_iota(jnp.int32, (R, 2 * C_STATE), 0), nch - 1)
        for k in range(nsteps):
            s = 1 << k
            sh = pltpu.roll(x, s, 0)
            upd = tr_ref[0, k:k + 1, :] * sh + ti_ref[0, k:k + 1, :] * pltpu.roll(sh, C_STATE, 1)
            x = x + jnp.where(pos >= s, upd, 0.0)
        xs = jnp.where(pos >= 1, pltpu.roll(x, 1, 0), 0.0)
        xs_ref[0] = xs
        y_ref[0] = (_dot(u, m_ref[0].astype(BF16)) + _dot(xs.astype(BF16), f_ref[0].astype(BF16))).astype(BF16)

    def g3(shape):
        return pl.BlockSpec((1,) + shape, lambda g: (g, 0, 0))

    return pl.pallas_call(
        body, name=name, grid=(G,),
        in_specs=[g3((R, 256)), g3((256, 256)), g3((256, 128)), g3((128, 256)), g3((8, 128)), g3((8, 128))],
        out_specs=[g3((R, 256)), g3((R, 128))],
        out_shape=[jax.ShapeDtypeStruct((G, R, 256), BF16), jax.ShapeDtypeStruct((G, R, 128), F32)],
        compiler_params=_cparams("parallel"),
    )(uf, m_mat, e_mat, f_mat, tab_r, tab_i)


def _s5_bwd(dy, uf, xs, m_mat, e_mat, f_mat, tab_r, tab_i, nch, *, name):
    G, R, _ = uf.shape
    nsteps = int(math.log2(nch))

    def body(dy_ref, u_ref, xs_ref, m_ref, e_ref, f_ref, tr_ref, ti_ref, du_ref, dm_ref, de_ref, df_ref, da_ref):
        dyb = dy_ref[0].astype(BF16)
        u = u_ref[0].astype(BF16)
        xs = xs_ref[0]
        dm_ref[0] = _dot(u, dyb, _TN)
        df_ref[0] = _dot(xs.astype(BF16), dyb, _TN)
        gx = _dot(dyb, f_ref[0].astype(BF16), _NT)
        pos = jnp.bitwise_and(lax.broadcasted_iota(jnp.int32, (R, 2 * C_STATE), 0), nch - 1)
        for k in range(nsteps):
            s = 1 << k
            sh = pltpu.roll(gx, R - s, 0)
            upd = tr_ref[0, k:k + 1, :] * sh - ti_ref[0, k:k + 1, :] * pltpu.roll(sh, C_STATE, 1)
            gx = gx + jnp.where(pos + s < nch, upd, 0.0)
        de_in = jnp.where(pos + 1 < nch, pltpu.roll(gx, R - 1, 0), 0.0)
        deb = de_in.astype(BF16)
        de_ref[0] = _dot(u, deb, _TN)
        du_ref[0] = (_dot(dyb, m_ref[0].astype(BF16), _NT) + _dot(deb, e_ref[0].astype(BF16), _NT)).astype(BF16)
        da_ref[0] = jnp.zeros((8, 128), F32)
        da_ref[0, 0:1, :] = jnp.sum(de_in * xs, axis=0, keepdims=True)
        da_ref[0, 1:2, :] = jnp.sum(de_in * pltpu.roll(xs, C_STATE, 1), axis=0, keepdims=True)

    def g3(shape):
        return pl.BlockSpec((1,) + shape, lambda g: (g, 0, 0))

    return pl.pallas_call(
        body, name=name, grid=(G,),
        in_specs=[g3((R, 256)), g3((R, 256)), g3((R, 128)), g3((256, 256)), g3((256, 128)), g3((128, 256)),
                  g3((8, 128)), g3((8, 128))],
        out_specs=[g3((R, 256)), g3((256, 256)), g3((256, 128)), g3((128, 256)), g3((8, 128))],
        out_shape=[jax.ShapeDtypeStruct((G, R, 256), BF16), jax.ShapeDtypeStruct((G, 256, 256), F32),
                   jax.ShapeDtypeStruct((G, 256, 128), F32), jax.ShapeDtypeStruct((G, 128, 256), F32),
                   jax.ShapeDtypeStruct((G, 8, 128), F32)],
        compiler_params=_cparams("parallel"),
    )(dy, uf, xs, m_mat, e_mat, f_mat, tab_r, tab_i)


def _s5_act_fwd(ys, u, dsk, *, name):
    T, C = u.shape
    tm = 256

    def body(ys_ref, u_ref, d_ref, gl_ref, y_ref):
        y = ys_ref[...].astype(F32) + d_ref[...] * u_ref[...]
        y_ref[...] = y
        gl_ref[...] = (0.5 * y * (1.0 + _erf(y * (2.0 ** -0.5)))).astype(BF16)

    row = pl.BlockSpec((tm, C), lambda i: (i, 0))
    vec = pl.BlockSpec((1, C), lambda i: (0, 0))
    return pl.pallas_call(
        body, name=name, grid=(T // tm,), in_specs=[row, row, vec], out_specs=[row, row],
        out_shape=[jax.ShapeDtypeStruct((T, C), BF16), jax.ShapeDtypeStruct((T, C), F32)],
        compiler_params=_cparams("parallel"),
    )(ys, u, dsk)


def _s5_act_bwd(y, u, dsk, dgl, *, name):
    T, C = u.shape
    tm = 256

    def body(y_ref, u_ref, d_ref, dgl_ref, dy_ref, du_ref, dd_ref):
        y = y_ref[...]
        cdf = 0.5 * (1.0 + _erf(y * (2.0 ** -0.5)))
        pdf = jnp.exp(-0.5 * y * y) * (1.0 / math.sqrt(2.0 * math.pi))
        dy = dgl_ref[...].astype(F32) * (cdf + y * pdf)
        dy_ref[...] = dy
        du_ref[...] = dy * d_ref[...]
        part = jnp.sum(dy * u_ref[...], axis=0, keepdims=True)

        @pl.when(pl.program_id(0) == 0)
        def _():
            dd_ref[...] = part

        @pl.when(pl.program_id(0) > 0)
        def _():
            dd_ref[...] += part

    row = pl.BlockSpec((tm, C), lambda i: (i, 0))
    vec = pl.BlockSpec((1, C), lambda i: (0, 0))
    return pl.pallas_call(
        body, name=name, grid=(T // tm,), in_specs=[row, row, vec, row], out_specs=[row, row, vec],
        out_shape=[jax.ShapeDtypeStruct((T, C), F32), jax.ShapeDtypeStruct((T, C), F32), jax.ShapeDtypeStruct((1, C), F32)],
        compiler_params=_cparams("arbitrary"),
    )(y, u, dsk, dgl)


def _add2(a, b, *, name):
    T, C = a.shape
    tm = 256

    def body(a_ref, b_ref, o_ref):
        o_ref[...] = a_ref[...].astype(F32) + b_ref[...].astype(F32)

    row = pl.BlockSpec((tm, C), lambda i: (i, 0))
    return pl.pallas_call(body, name=name, grid=(T // tm,), in_specs=[row, row], out_specs=row,
                          out_shape=jax.ShapeDtypeStruct((T, C), F32), compiler_params=_cparams("parallel"))(a, b)


def _loss_head(y, target, *, name):
    T, C = y.shape
    tm = 256

    def body(y_ref, t_ref, l_ref, d_ref):
        err = y_ref[...] - t_ref[...]
        d_ref[...] = err * (1.0 / C)
        sq = err * err
        part = jnp.zeros((8, LANES), F32)
        for r in range(0, tm, 8):
            for c in range(0, C, LANES):
                part = part + sq[r:r + 8, c:c + LANES]

        @pl.when(pl.program_id(0) == 0)
        def _():
            l_ref[...] = part

        @pl.when(pl.program_id(0) > 0)
        def _():
            l_ref[...] += part

    row = pl.BlockSpec((tm, C), lambda i: (i, 0))
    acc = pl.BlockSpec((8, LANES), lambda i: (0, 0))
    return pl.pallas_call(
        body, name=name, grid=(T // tm,), in_specs=[row, row], out_specs=[acc, row],
        out_shape=[jax.ShapeDtypeStruct((8, LANES), F32), jax.ShapeDtypeStruct((T, C), F32)],
        compiler_params=_cparams("arbitrary"),
    )(y, target)


def _adamw(w, g, m, v, *, name):
    shape = w.shape
    size = int(np.prod(shape))
    cols = LANES if (shape[-1] < LANES and size % LANES == 0) else shape[-1]
    rows = size // cols
    tm = _tile(rows, 256) if rows % 8 == 0 else rows
    w2, g2, m2, v2 = (t.reshape(rows, cols) for t in (w, g, m, v))

    def body(w_ref, g_ref, m_ref, v_ref, d_ref, nm_ref, nv_ref):
        gg = g_ref[...]
        nm = ADAM_B1 * m_ref[...] + (1.0 - ADAM_B1) * gg
        nv = ADAM_B2 * v_ref[...] + (1.0 - ADAM_B2) * (gg * gg)
        m_hat = nm / (1.0 - ADAM_B1 ** ADAM_STEP)
        v_hat = nv / (1.0 - ADAM_B2 ** ADAM_STEP)
        d_ref[...] = -ADAM_LR * (m_hat / (jnp.sqrt(v_hat) + ADAM_EPS) + ADAM_WD * w_ref[...])
        nm_ref[...] = nm
        nv_ref[...] = nv

    blk = pl.BlockSpec((tm, cols), lambda i: (i, 0))
    outs = pl.pallas_call(
        body, name=name, grid=(rows // tm,), in_specs=[blk] * 4, out_specs=[blk] * 3,
        out_shape=[jax.ShapeDtypeStruct((rows, cols), F32)] * 3, compiler_params=_cparams("parallel"),
    )(w2, g2, m2, v2)
    return tuple(o.reshape(shape) for o in outs)


def _all_gather(shard, *, name):
    R, C = shard.shape

    def body(x_ref, out_ref, send_sems, recv_sems, local_sem):
        x, y, c = lax.axis_index("x"), lax.axis_index("y"), lax.axis_index("c")
        me, sibling = (x, y, c), (x, y, 1 - c)
        chips = [(1 - x, y), (x, 1 - y), (1 - x, 1 - y)]

        def rows(px, py, pc):
            return out_ref.at[_logical(px, py, pc)]

        def copy(k, block, to, src=None):
            return pltpu.make_async_remote_copy(
                src_ref=rows(*block) if src is None else src, dst_ref=rows(*block),
                send_sem=send_sems.at[k], recv_sem=recv_sems.at[k], device_id=to, device_id_type=_MESH)

        mine = pltpu.make_async_copy(x_ref, rows(*me), local_sem)
        mine.start()
        first = [copy(0, me, sibling, src=x_ref)]
        first += [copy(1 + j, me, (*chip, c), src=x_ref) for j, chip in enumerate(chips)]
        for cp in first:
            cp.start()
        passed = [copy(4 + j, (*chip, c), sibling) for j, chip in enumerate(chips)]
        for j, chip in enumerate(chips):
            copy(1 + j, (*chip, c), me).wait_recv()
            passed[j].start()
        copy(0, sibling, me).wait_recv()
        for j, chip in enumerate(chips):
            copy(4 + j, (*chip, 1 - c), me).wait_recv()
        for cp in first + passed:
            cp.wait_send()
        mine.wait()

    return pl.pallas_call(
        body, name=name, out_shape=jax.ShapeDtypeStruct((N_DEV, R, C), shard.dtype),
        in_specs=[_HBM], out_specs=_HBM,
        scratch_shapes=[pltpu.SemaphoreType.DMA((7,)), pltpu.SemaphoreType.DMA((7,)), pltpu.SemaphoreType.DMA],
    )(shard)


def _gather_weights(shards, *, name):
    nt = len(shards)

    def body(*refs):
        ins, outs = refs[:nt], refs[nt:2 * nt]
        send_sems, recv_sems, local_sems = refs[2 * nt:]
        x, y, c = lax.axis_index("x"), lax.axis_index("y"), lax.axis_index("c")
        me, sibling = (x, y, c), (x, y, 1 - c)
        chips = [(1 - x, y), (x, 1 - y), (1 - x, 1 - y)]

        def copy(t, k, block, to, src=None):
            rows = outs[t].at[_logical(*block)]
            return pltpu.make_async_remote_copy(
                src_ref=rows if src is None else src, dst_ref=rows,
                send_sem=send_sems.at[t, k], recv_sem=recv_sems.at[t, k], device_id=to, device_id_type=_MESH)

        mine = [pltpu.make_async_copy(ins[t], outs[t].at[_logical(*me)], local_sems.at[t]) for t in range(nt)]
        for cp in mine:
            cp.start()
        started = []
        for t in range(nt):
            started.append(copy(t, 0, me, sibling, src=ins[t]))
            started += [copy(t, 1 + j, me, (*chip, c), src=ins[t]) for j, chip in enumerate(chips)]
        for cp in started:
            cp.start()
        for j, chip in enumerate(chips):
            for t in range(nt):
                copy(t, 1 + j, (*chip, c), me).wait_recv()
                fwd = copy(t, 4 + j, (*chip, c), sibling)
                fwd.start()
                started.append(fwd)
        for t in range(nt):
            copy(t, 0, sibling, me).wait_recv()
            for j, chip in enumerate(chips):
                copy(t, 4 + j, (*chip, 1 - c), me).wait_recv()
        for cp in started:
            cp.wait_send()
        for cp in mine:
            cp.wait()

    return pl.pallas_call(
        body, name=name, out_shape=[jax.ShapeDtypeStruct((N_DEV,) + s.shape, s.dtype) for s in shards],
        in_specs=[_HBM] * nt, out_specs=[_HBM] * nt,
        scratch_shapes=[pltpu.SemaphoreType.DMA((nt, 7)), pltpu.SemaphoreType.DMA((nt, 7)), pltpu.SemaphoreType.DMA((nt,))],
    )(*shards)


def _sum_rows(stacked, *, name):
    _, R, C = stacked.shape
    tr = R
    if N_DEV * R * C * stacked.dtype.itemsize > 12 * 1024 * 1024:
        for cand in range(512, 15, -16):
            if R % cand == 0:
                tr = cand
                break

    def body(s_ref, o_ref):
        acc = s_ref[0].astype(F32)
        for k in range(1, N_DEV):
            acc = acc + s_ref[k].astype(F32)
        o_ref[...] = acc

    return pl.pallas_call(
        body, name=name, grid=(R // tr,),
        in_specs=[pl.BlockSpec((N_DEV, tr, C), lambda i: (0, i, 0))], out_specs=pl.BlockSpec((tr, C), lambda i: (i, 0)),
        out_shape=jax.ShapeDtypeStruct((R, C), F32), compiler_params=_cparams("parallel"),
    )(stacked)


_LARGE = (("ab_w_in", 1, True), ("ab_w_out", 1, False), ("s5_w_glu", 1, True), ("xattn_wq", 2, False),
          ("xattn_wkv", 2, True), ("xattn_wo", 2, False), ("ffn_w_in", 2, True), ("ffn_w_out", 2, False))
_LARGE_KEYS = tuple((n, l) for n, layers, _ in _LARGE for l in range(layers))
_TRANSPOSED = {n: t for n, _, t in _LARGE}


def _owner_major(name, w):
    return w.T if _TRANSPOSED[name] else w


def _lb_from_logits(logits):
    return jnp.cumsum(jax.nn.softmax(logits, axis=0), axis=0)[0:1]


def _to_groups(t, T):
    return t.reshape(T // C_TC, C_TC, C_GROUPS, C_GROUP).transpose(2, 0, 1, 3).reshape(C_GROUPS, T // C_TC, C_TC * C_GROUP)


def _from_groups(t, T):
    return t.reshape(C_GROUPS, T // C_TC, C_TC, C_GROUP).transpose(1, 2, 0, 3).reshape(T, D_MODEL)


def _local_step(x, mem, target, W, shards=None):
    B, L, _ = x.shape
    T = B * L
    x0 = x.reshape(T, D_MODEL)
    memf = mem.reshape(B * MEM_LEN, D_MODEL)
    nw = W["norm_w"]
    cos2, sin2 = _rope_tables(L)
    W = dict(W)
    G, received = {}, {}

    def mmx(a, b, gather=(), scatter=(), **kw):
        if shards is None or not (gather or scatter):
            return _mm(a, b, **kw)
        out, gathered, got = _mm(a, b, gather=[shards[k] for k in gather],
                                 scatter=[G[k].reshape(N_DEV, -1, D_MODEL) for k in scatter], **kw)
        for k, g in zip(gather, gathered):
            W[k] = g.reshape(-1, D_MODEL)
        for k, r in zip(scatter, got):
            received[k] = r
        return out

    def vec(v):
        return v.reshape(1, -1)

    saved = []
    xin = x0
    for layer in range(2):
        s = {"x0": xin}
        tag = f"l{layer}"
        h1 = _rms_fwd(xin, vec(nw[layer, 0]), name=f"norm_pre_mix_{tag}", out_dtype=BF16 if layer == 0 else F32)
        s["h1"] = h1
        if layer == 0:
            lb, lb_vjp = jax.vjp(_lb_from_logits, W["hgrn_lb_logits"])
            onw = W["hgrn_out_norm_w"].reshape(1, A_WIDTH)
            z = mmx(h1, W["ab_w_in", 0], tb=True, name="ab_in",
                    gather=[("ab_w_out", 0), ("xattn_wq", 0), ("xattn_wkv", 0), ("xattn_wo", 0), ("ffn_w_in", 0)])
            oa, o_raw, s_start = _hgrn_fwd(z, lb, onw, B, L, name="hgrn_fwd")
            qr = _rope_fwd(z, cos2, sin2, 16, B, L, name="rope_q")
            kr = _rope_fwd(z, cos2, sin2, 20, B, L, name="rope_k")
            os_, ls_ = [], []
            for dil in B_DILS:
                o_g, l_g = _dil_fwd(qr, kr, z, dil, B, L, name=f"dil_fwd_{dil}")
                os_.append(o_g)
                ls_.append(l_g)
            ob, lse = _dil_combine(os_, ls_, name="dil_combine")
            ymix = jnp.concatenate([oa, ob], axis=-1).astype(BF16)
            y1 = mmx(ymix, W["ab_w_out", 0], name="ab_out", gather=[("ffn_w_out", 0)])
            s.update(z=z, lb=lb, lb_vjp=lb_vjp, onw=onw, o_raw=o_raw, s_start=s_start, qr=qr, kr=kr, ob=ob, lse=lse, ymix=ymix)
        else:
            p5 = tuple(W[n][0] for n in ("s5_lambda_re", "s5_lambda_im", "s5_log_dt", "s5_b_re", "s5_b_im", "s5_c_re", "s5_c_im"))
            (m_mat, e_mat, f_mat, a16r, a16i), s5_vjp = jax.vjp(_s5_build, *p5)
            nch = L // C_TC
            tab_r, tab_i = _s5_scan_tables(p5[0], p5[1], p5[2], int(math.log2(nch)))
            uf = _to_groups(h1, T).astype(BF16)
            yf, xs = _s5_fwd(uf, m_mat, e_mat, f_mat, tab_r, tab_i, nch, name="s5_fwd")
            dsk = W["s5_d"].reshape(1, D_MODEL)
            gl, ypre = _s5_act_fwd(_from_groups(yf, T), h1, dsk, name="s5_act_fwd")
            w_glu = _interleave_rows(W["s5_w_glu", 0])
            zg, y1 = _mm(gl, w_glu, tb=True, name="s5_glu_in", gate=("fwd", "glu"))
            s.update(s5_vjp=s5_vjp, mats=(m_mat, e_mat, f_mat, tab_r, tab_i), uf=uf, xs=xs, dsk=dsk, gl=gl, ypre=ypre, zg=zg, nch=nch, w_glu=w_glu)
        x1 = _rms_fwd(y1, vec(nw[layer, 1]), xin, name=f"norm_post_mix_{tag}")
        h2 = _rms_fwd(x1, vec(nw[layer, 2]), name=f"norm_pre_x_{tag}", out_dtype=BF16)
        memn = _rms_fwd(memf, vec(W["mem_norm_w"][layer]), name=f"norm_mem_{tag}", out_dtype=BF16)
        q = mmx(h2, W["xattn_wq", layer], out_dtype=BF16, name=f"x_q_{tag}", gather=[("s5_w_glu", 0)] if layer == 0 else [])
        kv = _mm(memn, W["xattn_wkv", layer], tb=True, out_dtype=BF16, name=f"x_kv_{tag}")
        o = _xattn_fwd(q, kv, B, L, name=f"x_attn_{tag}")
        y2 = mmx(o, W["xattn_wo", layer], name=f"x_o_{tag}", gather=[("xattn_wq", 1), ("xattn_wo", 1)] if layer == 0 else [])
        x2 = _rms_fwd(y2, vec(nw[layer, 3]), x1, name=f"norm_post_x_{tag}")
        h3 = _rms_fwd(x2, vec(nw[layer, 4]), name=f"norm_pre_ffn_{tag}", out_dtype=BF16)
        w_ffn_in = _interleave_rows(W["ffn_w_in", layer])
        zf, u = mmx(h3, w_ffn_in, tb=True, name=f"ffn_in_{tag}", gate=("fwd", "swiglu"),
                    gather=[("xattn_wkv", 1), ("ffn_w_in", 1), ("ffn_w_out", 1)] if layer == 0 else [])
        y3 = _mm(u, W["ffn_w_out", layer], name=f"ffn_out_{tag}")
        x3 = _rms_fwd(y3, vec(nw[layer, 5]), x2, name=f"norm_post_ffn_{tag}")
        s.update(y1=y1, x1=x1, h2=h2, memn=memn, q=q, kv=kv, o=o, y2=y2, x2=x2, h3=h3, zf=zf, u=u, y3=y3, w_ffn_in=w_ffn_in)
        saved.append(s)
        xin = x3

    loss_parts, dx = _loss_head(xin, target.reshape(T, D_MODEL), name="loss_head")

    d_norm = [[None] * 6 for _ in range(2)]
    d_memn = [None, None]
    for layer in (1, 0):
        s = saved[layer]
        tag = f"l{layer}"
        dy3, d_norm[layer][5] = _rms_bwd(s["y3"], vec(nw[layer, 5]), dx, name=f"bnorm_post_ffn_{tag}", out_dtype=BF16)
        dzf = mmx(dy3, W["ffn_w_out", layer], tb=True, name=f"b_ffn_out_dx_{tag}", gate=("bwd", "swiglu", s["zf"]),
                  scatter=[("xattn_wkv", 1), ("s5_w_glu", 0)] if layer == 0 else [])
        G["ffn_w_out", layer] = _mm(s["u"], dy3, ta=True, out_dtype=BF16, name=f"b_ffn_out_dw_{tag}")
        G["ffn_w_in", layer] = _interleave_rows(
            mmx(dzf, s["h3"], ta=True, out_dtype=BF16, name=f"b_ffn_in_dw_{tag}", scatter=[("ffn_w_out", layer)]), inverse=True)
        dh3 = mmx(dzf, s["w_ffn_in"], out_dtype=BF16, name=f"b_ffn_in_dx_{tag}", scatter=[("ffn_w_in", layer)])
        dx, d_norm[layer][4] = _rms_bwd(s["x2"], vec(nw[layer, 4]), dh3, dx, name=f"bnorm_pre_ffn_{tag}")
        dy2, d_norm[layer][3] = _rms_bwd(s["y2"], vec(nw[layer, 3]), dx, name=f"bnorm_post_x_{tag}", out_dtype=BF16)
        do = _mm(dy2, W["xattn_wo", layer], tb=True, out_dtype=BF16, name=f"b_x_o_dx_{tag}")
        G["xattn_wo", layer] = _mm(s["o"], dy2, ta=True, out_dtype=BF16, name=f"b_x_o_dw_{tag}")
        dq, dkv = _xattn_bwd(s["q"], s["kv"], do, B, L, name=f"b_x_attn_{tag}")
        G["xattn_wq", layer] = _mm(s["h2"], dq, ta=True, out_dtype=BF16, name=f"b_x_q_dw_{tag}")
        dh2 = _mm(dq, W["xattn_wq", layer], tb=True, out_dtype=BF16, name=f"b_x_q_dx_{tag}")
        G["xattn_wkv", layer] = _mm(dkv, s["memn"], ta=True, out_dtype=BF16, name=f"b_x_kv_dw_{tag}")
        dmemn = _mm(dkv, W["xattn_wkv", layer], out_dtype=BF16, name=f"b_x_kv_dx_{tag}")
        _, d_memn[layer] = _rms_bwd(memf, vec(W["mem_norm_w"][layer]), dmemn, name=f"bnorm_mem_{tag}", out_dtype=BF16)
        dx, d_norm[layer][2] = _rms_bwd(s["x1"], vec(nw[layer, 2]), dh2, dx, name=f"bnorm_pre_x_{tag}")
        dy1, d_norm[layer][1] = _rms_bwd(s["y1"], vec(nw[layer, 1]), dx, name=f"bnorm_post_mix_{tag}", out_dtype=BF16)
        if layer == 0:
            z = s["z"]
            dymix = _mm(dy1, W["ab_w_out", 0], tb=True, name="b_ab_out_dx")
            G["ab_w_out", 0] = _mm(s["ymix"], dy1, ta=True, out_dtype=BF16, name="b_ab_out_dw")
            dqa, dfa, dia, dga, d_onw, d_lb = _hgrn_bwd(z, s["lb"], s["onw"], s["o_raw"], s["s_start"], dymix, B, L, name="hgrn_bwd")
            dqs, dks, dvs = [], [], []
            for dil in B_DILS:
                dqs.append(_dil_bwd_q(s["qr"], s["kr"], z, dymix, s["ob"], s["lse"], dil, B, L, name=f"dil_bwd_q_{dil}"))
                dk_g, dv_g = _dil_bwd_kv(s["qr"], s["kr"], z, dymix, s["ob"], s["lse"], dil, B, L, name=f"dil_bwd_kv_{dil}")
                dks.append(dk_g)
                dvs.append(dv_g)
            dqb = _sum3(*dqs, cos2, sin2, True, B, L, name="b_rope_q")
            dkb = _sum3(*dks, cos2, sin2, True, B, L, name="b_rope_k")
            dvb = _sum3(*dvs, cos2, sin2, False, B, L, name="b_sum_v")
            dz = jnp.concatenate([dqa, dfa, dia, dga, dqb, dkb, dvb], axis=-1).astype(BF16)
            G["ab_w_in", 0] = mmx(dz, s["h1"], ta=True, out_dtype=BF16, name="b_ab_in_dw",
                                  scatter=[("xattn_wo", 0), ("xattn_wq", 0), ("xattn_wkv", 0), ("ab_w_out", 0)])
            dh1 = mmx(dz, W["ab_w_in", 0], out_dtype=BF16, name="b_ab_in_dx", scatter=[("ab_w_in", 0)])
            G["hgrn_out_norm_w"] = jnp.sum(d_onw.reshape(B, A_WIDTH), axis=0, keepdims=True)
            d_lb_row = jnp.sum(d_lb.reshape(B, A_WIDTH), axis=0, keepdims=True)
            G["hgrn_lb_logits"] = s["lb_vjp"](d_lb_row)[0]
        else:
            dzg = _gated_bwd(s["zg"], dy1, "glu", name="b_s5_glu")
            G["s5_w_glu", 0] = _interleave_rows(
                _mm(dzg, s["gl"], ta=True, out_dtype=BF16, name="b_s5_glu_dw"), inverse=True)
            dgl = mmx(dzg, s["w_glu"], out_dtype=BF16, name="b_s5_glu_dx", scatter=[("xattn_wo", 1), ("xattn_wq", 1)])
            dyp, du_skip, d_dsk = _s5_act_bwd(s["ypre"], s["h1"], s["dsk"], dgl, name="b_s5_act")
            m_mat, e_mat, f_mat, tab_r, tab_i = s["mats"]
            duf, dm, de, df_, da = _s5_bwd(_to_groups(dyp, T).astype(BF16), s["uf"], s["xs"], m_mat, e_mat, f_mat, tab_r, tab_i,
                                           s["nch"], name="s5_bwd")
            da_r = da[:, 0, :C_STATE] + da[:, 0, C_STATE:]
            da_i = da[:, 1, C_STATE:] - da[:, 1, :C_STATE]
            gp = s["s5_vjp"]((dm, de, df_, da_r, da_i))
            for n, gv in zip(("s5_lambda_re", "s5_lambda_im", "s5_log_dt", "s5_b_re", "s5_b_im", "s5_c_re", "s5_c_im"), gp):
                G[n] = gv[None]
            G["s5_d"] = d_dsk
            dh1 = _add2(_from_groups(duf, T), du_skip, name="b_s5_du")
        dx, d_norm[layer][0] = _rms_bwd(s["x0"], vec(nw[layer, 0]), dh1, dx, name=f"bnorm_pre_mix_{tag}")

    G["norm_w"] = jnp.stack([jnp.concatenate(d_norm[l], axis=0) for l in range(2)])
    G["mem_norm_w"] = jnp.concatenate(d_memn, axis=0)
    if shards is not None:
        G.update(received)
    return loss_parts, dx.reshape(B, L, D_MODEL), G


_SMALL = (("norm_w", (2, 6, 1024)), ("mem_norm_w", (2, 1024)), ("hgrn_lb_logits", (3, 512)), ("hgrn_out_norm_w", (1, 512)),
          ("s5_lambda_re", (1, 64, 64)), ("s5_lambda_im", (1, 64, 64)), ("s5_log_dt", (1, 64)),
          ("s5_b_re", (1, 64, 64, 16)), ("s5_b_im", (1, 64, 64, 16)), ("s5_c_re", (1, 64, 16, 64)),
          ("s5_c_im", (1, 64, 16, 64)), ("s5_d", (1, 1024)))

_WEIGHT_ORDER = ('norm_w', 'mem_norm_w', 'ab_w_in', 'ab_w_out', 'hgrn_lb_logits', 'hgrn_out_norm_w', 's5_lambda_re',
                 's5_lambda_im', 's5_log_dt', 's5_b_re', 's5_b_im', 's5_c_re', 's5_c_im', 's5_d', 's5_w_glu', 'xattn_wq',
                 'xattn_wkv', 'xattn_wo', 'ffn_w_in', 'ffn_w_out')


def kernel(x, mem, norm_w, mem_norm_w, ab_w_in, ab_w_out, hgrn_lb_logits, hgrn_out_norm_w, s5_lambda_re, s5_lambda_im, s5_log_dt, s5_b_re, s5_b_im, s5_c_re, s5_c_im, s5_d, s5_w_glu, xattn_wq, xattn_wkv, xattn_wo, ffn_w_in, ffn_w_out, loss_target, m_norm_w, m_mem_norm_w, m_ab_w_in, m_ab_w_out, m_hgrn_lb_logits, m_hgrn_out_norm_w, m_s5_lambda_re, m_s5_lambda_im, m_s5_log_dt, m_s5_b_re, m_s5_b_im, m_s5_c_re, m_s5_c_im, m_s5_d, m_s5_w_glu, m_xattn_wq, m_xattn_wkv, m_xattn_wo, m_ffn_w_in, m_ffn_w_out, v_norm_w, v_mem_norm_w, v_ab_w_in, v_ab_w_out, v_hgrn_lb_logits, v_hgrn_out_norm_w, v_s5_lambda_re, v_s5_lambda_im, v_s5_log_dt, v_s5_b_re, v_s5_b_im, v_s5_c_re, v_s5_c_im, v_s5_d, v_s5_w_glu, v_xattn_wq, v_xattn_wkv, v_xattn_wo, v_ffn_w_in, v_ffn_w_out):
    local = dict(norm_w=norm_w, mem_norm_w=mem_norm_w, ab_w_in=ab_w_in, ab_w_out=ab_w_out, hgrn_lb_logits=hgrn_lb_logits,
                 hgrn_out_norm_w=hgrn_out_norm_w, s5_lambda_re=s5_lambda_re, s5_lambda_im=s5_lambda_im, s5_log_dt=s5_log_dt,
                 s5_b_re=s5_b_re, s5_b_im=s5_b_im, s5_c_re=s5_c_re, s5_c_im=s5_c_im, s5_d=s5_d, s5_w_glu=s5_w_glu,
                 xattn_wq=xattn_wq, xattn_wkv=xattn_wkv, xattn_wo=xattn_wo, ffn_w_in=ffn_w_in, ffn_w_out=ffn_w_out)
    mom_m = dict(zip(_WEIGHT_ORDER, (m_norm_w, m_mem_norm_w, m_ab_w_in, m_ab_w_out, m_hgrn_lb_logits, m_hgrn_out_norm_w, m_s5_lambda_re, m_s5_lambda_im, m_s5_log_dt, m_s5_b_re, m_s5_b_im, m_s5_c_re, m_s5_c_im, m_s5_d, m_s5_w_glu, m_xattn_wq, m_xattn_wkv, m_xattn_wo, m_ffn_w_in, m_ffn_w_out)))
    mom_v = dict(zip(_WEIGHT_ORDER, (v_norm_w, v_mem_norm_w, v_ab_w_in, v_ab_w_out, v_hgrn_lb_logits, v_hgrn_out_norm_w, v_s5_lambda_re, v_s5_lambda_im, v_s5_log_dt, v_s5_b_re, v_s5_b_im, v_s5_c_re, v_s5_c_im, v_s5_d, v_s5_w_glu, v_xattn_wq, v_xattn_wkv, v_xattn_wo, v_ffn_w_in, v_ffn_w_out)))
    dev = 4 * lax.axis_index("x") + 2 * lax.axis_index("y") + lax.axis_index("c")

    shards = {(n, l): _owner_major(n, local[n][l]).astype(BF16) for n, l in _LARGE_KEYS}
    first = ("ab_w_in", 0)
    W = {first: _gather_weights([shards[first]], name="gather_first")[0].reshape(-1, D_MODEL)}
    tiny =jnp.concatenate([norm_w.reshape(-1), s5_d.reshape(-1)])
    tiny = jnp.pad(tiny, (0, 16 * LANES - tiny.shape[0])).reshape(16, LANES)
    tiny_all = _all_gather(tiny, name="gather_tiny").reshape(N_DEV, 16 * LANES)
    W["norm_w"] = tiny_all[:, :12 * LANES].reshape(N_DEV, 2, 6, LANES).transpose(1, 2, 0, 3).reshape(2, 6, D_MODEL)
    W["s5_d"] = tiny_all[:, 12 * LANES:13 * LANES].reshape(1, D_MODEL)
    for n in ("mem_norm_w", "hgrn_lb_logits", "hgrn_out_norm_w", "s5_lambda_re", "s5_lambda_im", "s5_log_dt",
              "s5_b_re", "s5_b_im", "s5_c_re", "s5_c_im"):
        W[n] = local[n]

    loss_parts, grad_x, G = _local_step(x, mem, loss_target, W, shards)

    g_layers = {}
    for n, l in _LARGE_KEYS:
        g = _sum_rows(G[n, l], name=f"sum_grads_{n}_{l}")
        g_layers.setdefault(n, []).append(g.T if _TRANSPOSED[n] else g)
    g_local = {n: jnp.stack(gl) for n, gl in g_layers.items()}
    small =jnp.concatenate([G[n].reshape(-1) for n, _ in _SMALL] + [0.5 / D_MODEL * jnp.sum(loss_parts).reshape(1)])
    n_small = small.shape[0]
    small = jnp.pad(small, (0, (-n_small) % (8 * LANES))).reshape(-1, LANES)
    small_sum = _sum_rows(_all_gather(small, name="gather_small"), name="sum_small").reshape(-1)
    g_full, off = {}, 0
    for n, shp in _SMALL:
        size = int(np.prod(shp))
        g_full[n] = small_sum[off:off + size].reshape(shp)
        off += size
    loss = small_sum[off]
    grads = dict(g_local)
    for n, shp in _SMALL:
        if n == "norm_w":
            grads[n] = lax.dynamic_slice_in_dim(g_full[n], dev * LANES, LANES, axis=2)
        elif n == "s5_d":
            grads[n] = lax.dynamic_slice_in_dim(g_full[n], dev * LANES, LANES, axis=1)
        else:
            grads[n] = g_full[n]

    delta, new_m, new_v = {}, {}, {}
    for n in _WEIGHT_ORDER:
        delta[n], new_m[n], new_v[n] = _adamw(local[n], grads[n], mom_m[n], mom_v[n], name=f"adamw_{n}")
    return (loss, grad_x, *[grads[n] for n in _WEIGHT_ORDER], *[delta[n] for n in _WEIGHT_ORDER],
            *[new_m[n] for n in _WEIGHT_ORDER], *[new_v[n] for n in _WEIGHT_ORDER])
```

```python
import functools
import math

import numpy as np
import jax
import jax.numpy as jnp
from jax import lax
from jax.experimental import pallas as pl
from jax.experimental.pallas import tpu as pltpu

F32 = jnp.float32
BF16 = jnp.bfloat16
HI = lax.Precision.HIGHEST

D_MODEL = 1024
NORM_EPS = 1e-6
A_WIDTH = 512
A_HEAD = 128
A_CHUNK = 32
A_SUPER = 256
B_SPAN = 128
B_DILS = (1, 4, 16)
ROPE_THETA = 10000.0
C_GROUPS = 64
C_GROUP = 16
C_STATE = 64
C_TC = 8
C_MIN_NEG_RE = -1e-4
MEM_LEN = 256
X_HEADS = 4
X_HD = 256
D_FF = 2816
N_DEV = 8
LANES = 128

ADAM_LR, ADAM_B1, ADAM_B2, ADAM_EPS, ADAM_WD, ADAM_STEP = 0.001, 0.9, 0.999, 1e-08, 0.01, 10

NEG_BIG = -1e30


def _tile(n, pref):
    for d in range(min(pref, n) // LANES * LANES, 0, -LANES):
        if n % d == 0:
            return d
    return n


def _cparams(*sem):
    return pltpu.CompilerParams(dimension_semantics=sem, vmem_limit_bytes=56 * 1024 * 1024)


def _sigmoid(x):
    return 0.5 * jnp.tanh(0.5 * x) + 0.5


def _erf(x):
    ax = jnp.abs(x)
    t = 1.0 / (1.0 + 0.3275911 * ax)
    poly = t * (0.254829592 + t * (-0.284496736 + t * (1.421413741 + t * (-1.453152027 + t * 1.061405429))))
    y = 1.0 - poly * jnp.exp(-ax * ax)
    return jnp.where(x < 0, -y, y)


_HBM = pl.BlockSpec(memory_space=pltpu.HBM)
_MESH = pl.DeviceIdType.MESH


def _logical(px, py, pc):
    return 4 * px + 2 * py + pc


class _Exchange:
    def __init__(self, gather=(), scatter=()):
        self.gather, self.scatter = list(gather), list(scatter)
        self.ng, self.n = len(self.gather), len(self.gather) + len(self.scatter)

    def operands(self):
        return self.gather + self.scatter

    def in_specs(self):
        return [_HBM] * self.n

    def out_shapes(self):
        return ([jax.ShapeDtypeStruct((N_DEV,) + g.shape, g.dtype) for g in self.gather]
                + [jax.ShapeDtypeStruct(s.shape, s.dtype) for s in self.scatter])

    def scratch(self):
        if not self.n:
            return []
        return [pltpu.SemaphoreType.DMA((self.n, 7)), pltpu.SemaphoreType.DMA((self.n, 7)), pltpu.SemaphoreType.DMA((self.n,))]

    def split(self, results):
        return list(results[:self.ng]), list(results[self.ng:])

    def run(self, ins, outs, sems, first, last):
        if not self.n:
            return
        send_sems, recv_sems, local_sems = sems
        x, y, c = lax.axis_index("x"), lax.axis_index("y"), lax.axis_index("c")
        me, sibling = _logical(x, y, c), (x, y, 1 - c)
        chips = [(1 - x, y), (x, 1 - y), (1 - x, 1 - y)]
        peers = [(x ^ (k >> 2), y ^ ((k >> 1) & 1), c ^ (k & 1)) for k in range(1, N_DEV)]

        def remote(t, k, src, dst, to):
            return pltpu.make_async_remote_copy(src_ref=src, dst_ref=dst, send_sem=send_sems.at[t, k],
                                                recv_sem=recv_sems.at[t, k], device_id=to, device_id_type=_MESH)

        def local(t):
            src = ins[t] if t < self.ng else ins[t].at[me]
            return pltpu.make_async_copy(src, outs[t].at[me], local_sems.at[t])

        @pl.when(first)
        def _():
            for t in range(self.n):
                local(t).start()
                if t < self.ng:
                    remote(t, 0, ins[t], outs[t].at[me], sibling).start()
                    for j, chip in enumerate(chips):
                        remote(t, 1 + j, ins[t], outs[t].at[me], (*chip, c)).start()
                else:
                    for k, peer in enumerate(peers):
                        remote(t, k, ins[t].at[_logical(*peer)], outs[t].at[me], peer).start()

        @pl.when(last)
        def _():
            for j, chip in enumerate(chips):
                for t in range(self.ng):
                    landed = outs[t].at[_logical(*chip, c)]
                    remote(t, 1 + j, ins[t], landed, sibling).wait_recv()
                    remote(t, 4 + j, landed, landed, sibling).start()
            for t in range(self.n):
                if t < self.ng:
                    remote(t, 0, ins[t], outs[t].at[_logical(*sibling)], sibling).wait_recv()
                    for j, chip in enumerate(chips):
                        remote(t, 4 + j, ins[t], outs[t].at[_logical(*chip, 1 - c)], sibling).wait_recv()
                    for k in range(7):
                        remote(t, k, ins[t], outs[t].at[me], sibling).wait_send()
                else:
                    for k, peer in enumerate(peers):
                        remote(t, k, ins[t].at[me], outs[t].at[_logical(*peer)], peer).wait_recv()
                    for k, peer in enumerate(peers):
                        remote(t, k, ins[t].at[_logical(*peer)], outs[t].at[me], peer).wait_send()
                local(t).wait()


_MM_VMEM_BUDGET = 36 * 1024 * 1024


def _mm(a, b, *, ta=False, tb=False, out_dtype=F32, name, tiles=(1408, 1408, 4096), gather=(), scatter=(), gate=None):
    M, K = (a.shape[1], a.shape[0]) if ta else a.shape
    N = b.shape[0] if tb else b.shape[1]
    assert (b.shape[1] if tb else b.shape[0]) == K
    gate_mode = gate[0] if gate else None
    tm, tn, tk = _tile(M, tiles[0]), _tile(N, tiles[1]), _tile(K, tiles[2])
    if gate_mode == "fwd":
        bs = _gate_block(N // 2)
        tn = 2 * bs
    elif gate_mode == "bwd":
        bs = _gate_block(N)
        tn = bs

    def vmem_bytes():
        out = 2 * tm * tn * jnp.dtype(out_dtype).itemsize + (4 * tm * tn if tk < K else 0)
        if gate_mode:
            out += 2 * 3 * tm * tn * 2
        return 2 * 2 * (tm * tk + tk * tn) + out

    while vmem_bytes() > _MM_VMEM_BUDGET:
        if tk > 512:
            tk = _tile(K, tk - LANES)
        elif tn > 256 and not gate_mode:
            tn = _tile(N, tn - LANES)
        else:
            tm = _tile(M, tm - LANES)
    ni, nj, nk = M // tm, N // tn, K // tk
    ex = _Exchange(gather, scatter)

    a_spec = pl.BlockSpec((tk, tm), lambda i, j, k: (k, i)) if ta else pl.BlockSpec((tm, tk), lambda i, j, k: (i, k))
    b_spec = pl.BlockSpec((tn, tk), lambda i, j, k: (j, k)) if tb else pl.BlockSpec((tk, tn), lambda i, j, k: (k, j))
    dims = (((0 if ta else 1,), (1 if tb else 0,)), ((), ()))
    n_acc = 1 if nk > 1 else 0
    n_in = 3 if gate_mode == "bwd" else 2
    n_out = 2 if gate_mode == "fwd" else 1

    def finish(val, in_refs, out_refs):
        if gate_mode is None:
            out_refs[0][...] = val.astype(out_refs[0].dtype)
        elif gate_mode == "fwd":
            out_refs[0][...] = val.astype(BF16)
            for p in range(tn // (2 * bs)):
                a_, b_ = val[:, 2 * p * bs:(2 * p + 1) * bs], val[:, (2 * p + 1) * bs:(2 * p + 2) * bs]
                out_refs[1][:, p * bs:(p + 1) * bs] = _gate_value(a_, b_, gate[1]).astype(BF16)
        else:
            z_ref = in_refs[2]
            for p in range(tn // bs):
                a_ = z_ref[:, 2 * p * bs:(2 * p + 1) * bs].astype(F32)
                b_ = z_ref[:, (2 * p + 1) * bs:(2 * p + 2) * bs].astype(F32)
                da, db = _gate_grads(a_, b_, val[:, p * bs:(p + 1) * bs], gate[1])
                out_refs[0][:, 2 * p * bs:(2 * p + 1) * bs] = da.astype(BF16)
                out_refs[0][:, (2 * p + 1) * bs:(2 * p + 2) * bs] = db.astype(BF16)

    def body(*refs):
        in_refs, rest = refs[:n_in], refs[n_in:]
        ex_in, out_refs = rest[:ex.n], rest[ex.n:ex.n + n_out]
        ex_out, scratch = rest[ex.n + n_out:2 * ex.n + n_out], rest[2 * ex.n + n_out:]
        i, j, k = pl.program_id(0), pl.program_id(1), pl.program_id(2)
        ex.run(ex_in, ex_out, scratch[n_acc:], (i == 0) & (j == 0) & (k == 0), (i == ni - 1) & (j == nj - 1) & (k == nk - 1))
        part = lax.dot_general(in_refs[0][...].astype(BF16), in_refs[1][...].astype(BF16), dims, preferred_element_type=F32)
        if nk == 1:
            finish(part, in_refs, out_refs)
            return
        acc_ref = scratch[0]

        @pl.when(k == 0)
        def _():
            acc_ref[...] = part

        @pl.when(k > 0)
        def _():
            acc_ref[...] += part

        @pl.when(k == nk - 1)
        def _():
            finish(acc_ref[...], in_refs, out_refs)

    tile = lambda width: pl.BlockSpec((tm, width), lambda i, j, k: (i, j))
    if gate_mode == "fwd":
        out_specs, out_shape = [tile(tn), tile(tn // 2)], [jax.ShapeDtypeStruct((M, N), BF16), jax.ShapeDtypeStruct((M, N // 2), BF16)]
    elif gate_mode == "bwd":
        out_specs, out_shape = [tile(2 * tn)], [jax.ShapeDtypeStruct((M, 2 * N), BF16)]
    else:
        out_specs, out_shape = [tile(tn)], [jax.ShapeDtypeStruct((M, N), out_dtype)]
    operands = [a, b] + ([gate[2]] if gate_mode == "bwd" else [])
    sem = ("arbitrary",) * 3 if ex.n else ("parallel", "parallel", "arbitrary")
    res = pl.pallas_call(
        body, name=name, grid=(ni, nj, nk),
        in_specs=[a_spec, b_spec] + ([tile(2 * tn)] if gate_mode == "bwd" else []) + ex.in_specs(),
        out_specs=out_specs + ex.in_specs(),
        out_shape=out_shape + ex.out_shapes(),
        scratch_shapes=([pltpu.VMEM((tm, tn), F32)] if nk > 1 else []) + ex.scratch(),
        compiler_params=_cparams(*sem),
    )(*operands, *ex.operands())
    main = res[0] if n_out == 1 else tuple(res[:n_out])
    if not ex.n:
        return main
    return (main,) + tuple(ex.split(res[n_out:]))


def _rms_fwd(x, w, res=None, *, name, out_dtype=F32):
    T, C = x.shape
    tm = _tile(T, 512)
    has_res = res is not None

    def body(*refs):
        x_ref, w_ref = refs[0], refs[1]
        o_ref = refs[-1]
        xv = x_ref[...].astype(F32)
        r = lax.rsqrt(jnp.mean(xv * xv, axis=-1, keepdims=True) + NORM_EPS)
        y = xv * r * w_ref[...]
        if has_res:
            y = y + refs[2][...]
        o_ref[...] = y.astype(o_ref.dtype)

    row = pl.BlockSpec((tm, C), lambda i: (i, 0))
    vec = pl.BlockSpec((1, C), lambda i: (0, 0))
    ins = [x, w] + ([res] if has_res else [])
    return pl.pallas_call(
        body, name=name, grid=(T // tm,), in_specs=[row, vec] + ([row] if has_res else []), out_specs=row,
        out_shape=jax.ShapeDtypeStruct((T, C), out_dtype), compiler_params=_cparams("parallel"),
    )(*ins)


def _rms_bwd(x, w, dy, add=None, *, name, out_dtype=F32):
    T, C = x.shape
    tm = _tile(T, 512)
    has_add = add is not None

    def body(*refs):
        x_ref, w_ref, dy_ref = refs[:3]
        dx_ref, dw_ref = refs[-2], refs[-1]
        xv = x_ref[...].astype(F32)
        r = lax.rsqrt(jnp.mean(xv * xv, axis=-1, keepdims=True) + NORM_EPS)
        xh = xv * r
        g = dy_ref[...].astype(F32)
        part = jnp.sum(g * xh, axis=0, keepdims=True)

        @pl.when(pl.program_id(0) == 0)
        def _():
            dw_ref[...] = part

        @pl.when(pl.program_id(0) > 0)
        def _():
            dw_ref[...] += part

        gx = g * w_ref[...]
        dx = r * (gx - xh * jnp.mean(gx * xh, axis=-1, keepdims=True))
        if has_add:
            dx = dx + refs[3][...]
        dx_ref[...] = dx.astype(dx_ref.dtype)

    row = pl.BlockSpec((tm, C), lambda i: (i, 0))
    vec = pl.BlockSpec((1, C), lambda i: (0, 0))
    ins = [x, w, dy] + ([add] if has_add else [])
    return pl.pallas_call(
        body, name=name, grid=(T // tm,), in_specs=[row, vec, row] + ([row] if has_add else []),
        out_specs=[row, vec],
        out_shape=[jax.ShapeDtypeStruct((T, C), out_dtype), jax.ShapeDtypeStruct((1, C), F32)],
        compiler_params=_cparams("arbitrary"),
    )(*ins)


def _gate_block(width):
    return _tile(width, 256)


def _gate_value(a, b, kind):
    return a * _sigmoid(a) * b if kind == "swiglu" else a * _sigmoid(b)


def _gate_grads(a, b, d, kind):
    if kind == "swiglu":
        s = _sigmoid(a)
        return d * b * (s * (1.0 + a * (1.0 - s))), d * a * s
    s = _sigmoid(b)
    return d * s, d * a * s * (1.0 - s)


def _interleave_rows(w, inverse=False):
    W2, C = w.shape
    bs = _gate_block(W2 // 2)
    nb = W2 // 2 // bs
    shape = (nb, 2, bs, C) if inverse else (2, nb, bs, C)
    return w.reshape(shape).transpose(1, 0, 2, 3).reshape(W2, C)


def _gated_bwd(z, dout, kind, *, name):
    T, W2 = z.shape
    W = W2 // 2
    tm, bs = _tile(T, 512), _gate_block(W)

    def body(z_ref, d_ref, o_ref):
        da, db = _gate_grads(z_ref[:, :bs].astype(F32), z_ref[:, bs:].astype(F32), d_ref[...].astype(F32), kind)
        o_ref[:, :bs] = da.astype(o_ref.dtype)
        o_ref[:, bs:] = db.astype(o_ref.dtype)

    return pl.pallas_call(
        body, name=name, grid=(T // tm, W // bs),
        in_specs=[pl.BlockSpec((tm, 2 * bs), lambda i, j: (i, j)), pl.BlockSpec((tm, bs), lambda i, j: (i, j))],
        out_specs=pl.BlockSpec((tm, 2 * bs), lambda i, j: (i, j)),
        out_shape=jax.ShapeDtypeStruct((T, W2), BF16), compiler_params=_cparams("parallel", "parallel"),
    )(z, dout)


_NT = (((1,), (1,)), ((), ()))
_TN = (((0,), (0,)), ((), ()))


def _dot(a, b, dims=None, precision=None):
    if dims is None:
        return jnp.dot(a, b, preferred_element_type=F32, precision=precision)
    return lax.dot_general(a, b, dims, preferred_element_type=F32, precision=precision)


def _xattn_fwd(q, kv, B, L, *, name):
    T = q.shape[0]
    tq = 256
    nq = L // tq
    scale = X_HD ** -0.5

    def body(q_ref, k_ref, v_ref, o_ref):
        for h in range(X_HEADS):
            sl = slice(h * X_HD, (h + 1) * X_HD)
            qh, kh, vh = q_ref[:, sl].astype(BF16), k_ref[:, sl].astype(BF16), v_ref[:, sl].astype(BF16)
            s = _dot(qh, kh, _NT) * scale
            m = jnp.max(s, axis=-1, keepdims=True)
            p = jnp.exp(s - m)
            l = jnp.sum(p, axis=-1, keepdims=True)
            o_ref[:, sl] = (_dot(p.astype(BF16), vh) / l).astype(BF16)

    return pl.pallas_call(
        body, name=name, grid=(B, nq),
        in_specs=[pl.BlockSpec((tq, D_MODEL), lambda b, i: (b * nq + i, 0)),
                  pl.BlockSpec((MEM_LEN, D_MODEL), lambda b, i: (b, 0)),
                  pl.BlockSpec((MEM_LEN, D_MODEL), lambda b, i: (b, 1))],
        out_specs=pl.BlockSpec((tq, D_MODEL), lambda b, i: (b * nq + i, 0)),
        out_shape=jax.ShapeDtypeStruct((T, D_MODEL), BF16), compiler_params=_cparams("parallel", "parallel"),
    )(q, kv, kv)


def _xattn_bwd(q, kv, do, B, L, *, name):
    T = q.shape[0]
    tq = 256
    nq = L // tq
    scale = X_HD ** -0.5

    def body(q_ref, k_ref, v_ref, do_ref, dq_ref, dkv_ref):
        @pl.when(pl.program_id(1) == 0)
        def _():
            dkv_ref[...] = jnp.zeros_like(dkv_ref)

        for h in range(X_HEADS):
            sl = slice(h * X_HD, (h + 1) * X_HD)
            slv = slice(D_MODEL + h * X_HD, D_MODEL + (h + 1) * X_HD)
            qh, kh, vh = q_ref[:, sl].astype(BF16), k_ref[:, sl].astype(BF16), v_ref[:, sl].astype(BF16)
            doh = do_ref[:, sl].astype(BF16)
            s = _dot(qh, kh, _NT) * scale
            m = jnp.max(s, axis=-1, keepdims=True)
            e = jnp.exp(s - m)
            p = e / jnp.sum(e, axis=-1, keepdims=True)
            dkv_ref[:, slv] += _dot(p.astype(BF16), doh, _TN)
            dp = _dot(doh, vh, _NT)
            ds = p * (dp - jnp.sum(dp * p, axis=-1, keepdims=True)) * scale
            dsb = ds.astype(BF16)
            dq_ref[:, sl] = _dot(dsb, kh).astype(BF16)
            dkv_ref[:, sl] += _dot(dsb, qh, _TN)

    return pl.pallas_call(
        body, name=name, grid=(B, nq),
        in_specs=[pl.BlockSpec((tq, D_MODEL), lambda b, i: (b * nq + i, 0)),
                  pl.BlockSpec((MEM_LEN, D_MODEL), lambda b, i: (b, 0)),
                  pl.BlockSpec((MEM_LEN, D_MODEL), lambda b, i: (b, 1)),
                  pl.BlockSpec((tq, D_MODEL), lambda b, i: (b * nq + i, 0))],
        out_specs=[pl.BlockSpec((tq, D_MODEL), lambda b, i: (b * nq + i, 0)),
                   pl.BlockSpec((MEM_LEN, 2 * D_MODEL), lambda b, i: (b, 0))],
        out_shape=[jax.ShapeDtypeStruct((T, D_MODEL), BF16), jax.ShapeDtypeStruct((B * MEM_LEN, 2 * D_MODEL), F32)],
        compiler_params=_cparams("parallel", "arbitrary"),
    )(q, kv, kv, do)


def _chunk_masks():
    row = lax.broadcasted_iota(jnp.int32, (A_SUPER, A_SUPER), 0)
    col = lax.broadcasted_iota(jnp.int32, (A_SUPER, A_SUPER), 1)
    same = jnp.right_shift(row, 5) == jnp.right_shift(col, 5)
    return same, same & (col <= row), same & (col >= row)


def _hgrn_gates(fa, lb):
    sig = _sigmoid(fa)
    f = lb + (1.0 - lb) * sig
    return sig, f, jnp.log(f), 1.0 - f


def _hgrn_fwd(z, lb, onw, B, L, *, name):
    T = B * L
    ns = L // A_SUPER
    nch = A_SUPER // A_CHUNK

    def body(q_ref, f_ref, v_ref, g_ref, lb_ref, w_ref, oa_ref, o_ref, s_ref, st_ref, sc_ref):
        @pl.when(pl.program_id(2) == 0)
        def _():
            st_ref[...] = jnp.zeros_like(st_ref)

        s_ref[0] = st_ref[...]
        same, tril, _ = _chunk_masks()
        q, v = q_ref[...], v_ref[...]
        _, _, lf, k = _hgrn_gates(f_ref[...], lb_ref[...])
        bcs = _dot(tril.astype(F32), lf, precision=HI)
        bl = _dot(same.astype(F32), lf, precision=HI)
        qd = (q * jnp.exp(bcs)).astype(BF16)
        ki = (k * jnp.exp(-bcs)).astype(BF16)
        ke = (k * jnp.exp(bl - bcs)).astype(BF16)
        dec = jnp.exp(bl)
        vb = v.astype(BF16)
        a = jnp.where(tril, _dot(qd, ki, _NT), 0.0)
        o_ref[...] = _dot(a.astype(BF16), vb)
        chunks = [slice(c * A_CHUNK, (c + 1) * A_CHUNK) for c in range(nch)]
        outer = [_dot(vb[rs], ke[rs], _TN) for rs in chunks]
        st = st_ref[...]
        for c, rs in enumerate(chunks):
            sc_ref[c] = st.astype(BF16)
            st = st * dec[c * A_CHUNK:c * A_CHUNK + 1, :] + outer[c]
        st_ref[...] = st
        for c, rs in enumerate(chunks):
            o_ref[rs, :] += _dot(qd[rs], sc_ref[c], _NT)
        o = o_ref[...]
        r = lax.rsqrt(jnp.mean(o * o, axis=-1, keepdims=True) + NORM_EPS)
        g = g_ref[...]
        oa_ref[...] = o * r * w_ref[...] * (g * _sigmoid(g))

    def zspec(off):
        return pl.BlockSpec((A_SUPER, A_HEAD), lambda b, h, n: (b * ns + n, off + h))

    hvec = pl.BlockSpec((1, A_HEAD), lambda b, h, n: (0, h))
    ospec = pl.BlockSpec((A_SUPER, A_HEAD), lambda b, h, n: (b * ns + n, h))
    return pl.pallas_call(
        body, name=name, grid=(B, 4, ns),
        in_specs=[zspec(0), zspec(4), zspec(8), zspec(12), hvec, hvec],
        out_specs=[ospec, ospec, pl.BlockSpec((1, A_HEAD, A_HEAD), lambda b, h, n: ((b * 4 + h) * ns + n, 0, 0))],
        out_shape=[jax.ShapeDtypeStruct((T, A_WIDTH), F32), jax.ShapeDtypeStruct((T, A_WIDTH), F32),
                   jax.ShapeDtypeStruct((B * 4 * ns, A_HEAD, A_HEAD), F32)],
        scratch_shapes=[pltpu.VMEM((A_HEAD, A_HEAD), F32), pltpu.VMEM((nch, A_HEAD, A_HEAD), BF16)],
        compiler_params=_cparams("parallel", "parallel", "arbitrary"),
    )(z, z, z, z, lb, onw)


def _hgrn_bwd(z, lb, onw, o_raw, s_start, doa, B, L, *, name):
    T = B * L
    ns = L // A_SUPER
    nch = A_SUPER // A_CHUNK

    def body(q_ref, f_ref, v_ref, g_ref, lb_ref, w_ref, o_ref, s_ref, doa_ref,
             dq_ref, df_ref, dv_ref, dg_ref, dw_ref, dlb_ref, dst_ref, sc_ref, dsc_ref, dqd_ref, dke_ref, dblx_ref):
        @pl.when(pl.program_id(2) == 0)
        def _():
            dst_ref[...] = jnp.zeros_like(dst_ref)
            dw_ref[...] = jnp.zeros_like(dw_ref)
            dlb_ref[...] = jnp.zeros_like(dlb_ref)

        same, tril, triu = _chunk_masks()
        q, v, g, lb, w = q_ref[...], v_ref[...], g_ref[...], lb_ref[...], w_ref[...]
        sig, f, lf, k = _hgrn_gates(f_ref[...], lb)
        bcs = _dot(tril.astype(F32), lf, precision=HI)
        bl = _dot(same.astype(F32), lf, precision=HI)
        eb, enb, eeb = jnp.exp(bcs), jnp.exp(-bcs), jnp.exp(bl - bcs)
        qd, ki, ke = q * eb, k * enb, k * eeb
        qdb, kib, keb, vb = qd.astype(BF16), ki.astype(BF16), ke.astype(BF16), v.astype(BF16)
        dec = jnp.exp(bl)
        o = o_ref[...]
        r = lax.rsqrt(jnp.mean(o * o, axis=-1, keepdims=True) + NORM_EPS)
        on = o * r
        sg = _sigmoid(g)
        silu_g = g * sg
        doa = doa_ref[...]
        dg_ref[...] = doa * on * w * (sg * (1.0 + g * (1.0 - sg)))
        dw_ref[0] += jnp.sum(doa * on * silu_g, axis=0, keepdims=True)
        don = doa * w * silu_g
        do = r * (don - on * jnp.mean(don * on, axis=-1, keepdims=True))
        dob = do.astype(BF16)
        a = jnp.where(tril, _dot(qdb, kib, _NT), 0.0).astype(BF16)
        da = jnp.where(tril, _dot(dob, vb, _NT), 0.0).astype(BF16)
        dv_ref[...] = _dot(a, dob, _TN)
        dqd_ref[...] = _dot(da, kib)
        dki = _dot(da, qdb, _TN)
        chunks = [slice(c * A_CHUNK, (c + 1) * A_CHUNK) for c in range(nch)]
        outer = [_dot(vb[rs], keb[rs], _TN) for rs in chunks]
        st = s_ref[0]
        for c in range(nch):
            sc_ref[c] = st
            st = st * dec[c * A_CHUNK:c * A_CHUNK + 1, :] + outer[c]
        outer_g = [_dot(dob[rs], qdb[rs], _TN) for rs in chunks]
        dst = dst_ref[...]
        for c in reversed(range(nch)):
            dsc_ref[c] = dst
            dst = dst * dec[c * A_CHUNK:c * A_CHUNK + 1, :] + outer_g[c]
        dst_ref[...] = dst
        for c, rs in enumerate(chunks):
            dec_c = dec[c * A_CHUNK:c * A_CHUNK + 1, :]
            dsc, stc = dsc_ref[c], sc_ref[c]
            dscb = dsc.astype(BF16)
            dv_ref[rs, :] += _dot(keb[rs], dscb, _NT)
            dke_ref[rs, :] = _dot(vb[rs], dscb)
            ddec = jnp.sum(dsc * stc, axis=0, keepdims=True)
            dqd_ref[rs, :] += _dot(dob[rs], stc.astype(BF16))
            dblx_ref[rs, :] = jnp.broadcast_to(ddec * dec_c, (A_CHUNK, A_HEAD))
        dqd, dke = dqd_ref[...], dke_ref[...]
        dq_ref[...] = dqd * eb
        keke = dke * ke
        db = dqd * qd - dki * ki - keke
        dbl = _dot(same.astype(F32), keke, precision=HI) + dblx_ref[...]
        dk = dki * enb + dke * eeb
        dlf = _dot(triu.astype(F32), db, precision=HI) + dbl
        dff = dlf / f - dk
        df_ref[...] = dff * (1.0 - lb) * sig * (1.0 - sig)
        dlb_ref[0] += jnp.sum(dff * (1.0 - sig), axis=0, keepdims=True)

    def rev(n):
        return ns - 1 - n

    def zspec(off):
        return pl.BlockSpec((A_SUPER, A_HEAD), lambda b, h, n: (b * ns + rev(n), off + h))

    hvec = pl.BlockSpec((1, A_HEAD), lambda b, h, n: (0, h))
    ospec = pl.BlockSpec((A_SUPER, A_HEAD), lambda b, h, n: (b * ns + rev(n), h))
    acc = pl.BlockSpec((1, 1, A_HEAD), lambda b, h, n: (b * 4 + h, 0, 0))
    big = jax.ShapeDtypeStruct((T, A_WIDTH), F32)
    small = jax.ShapeDtypeStruct((B * 4, 1, A_HEAD), F32)
    return pl.pallas_call(
        body, name=name, grid=(B, 4, ns),
        in_specs=[zspec(0), zspec(4), zspec(8), zspec(12), hvec, hvec, ospec,
                  pl.BlockSpec((1, A_HEAD, A_HEAD), lambda b, h, n: ((b * 4 + h) * ns + rev(n), 0, 0)), ospec],
        out_specs=[ospec, ospec, ospec, ospec, acc, acc],
        out_shape=[big, big, big, big, small, small],
        scratch_shapes=[pltpu.VMEM((A_HEAD, A_HEAD), F32), pltpu.VMEM((nch, A_HEAD, A_HEAD), F32),
                        pltpu.VMEM((nch, A_HEAD, A_HEAD), F32),
                        pltpu.VMEM((A_SUPER, A_HEAD), F32), pltpu.VMEM((A_SUPER, A_HEAD), F32),
                        pltpu.VMEM((A_SUPER, A_HEAD), F32)],
        compiler_params=_cparams("parallel", "parallel", "arbitrary"),
    )(z, z, z, z, lb, onw, o_raw, s_start, doa)


def _rope_tables(L):
    half = A_HEAD // 2
    inv_freq = ROPE_THETA ** (-jnp.arange(half, dtype=F32) / half)
    ang = jnp.arange(L, dtype=F32)[:, None] * inv_freq[None, :]
    cos, sin = jnp.cos(ang), jnp.sin(ang)
    return jnp.concatenate([cos, cos], axis=-1), jnp.concatenate([-sin, sin], axis=-1)


def _rope_fwd(z, cos2, sin2, col_off, B, L, *, name):
    T = B * L
    tm = 256
    nl = L // tm

    def body(x_ref, c_ref, s_ref, o_ref):
        x = x_ref[...]
        o_ref[...] = x * c_ref[...] + pltpu.roll(x, A_HEAD // 2, 1) * s_ref[...]

    tab = pl.BlockSpec((tm, A_HEAD), lambda i, h: (i % nl, 0))
    return pl.pallas_call(
        body, name=name, grid=(T // tm, 4),
        in_specs=[pl.BlockSpec((tm, A_HEAD), lambda i, h: (i, col_off + h)), tab, tab],
        out_specs=pl.BlockSpec((tm, A_HEAD), lambda i, h: (i, h)),
        out_shape=jax.ShapeDtypeStruct((T, 512), F32), compiler_params=_cparams("parallel", "parallel"),
    )(z, cos2, sin2)


def _sum3(d1, d2, d3, cos2, sin2, rotate, B, L, *, name):
    T = B * L
    tm = 256
    nl = L // tm

    def body(a_ref, b_ref, c_ref, cs_ref, sn_ref, o_ref):
        d = a_ref[...] + b_ref[...] + c_ref[...]
        if rotate:
            d = d * cs_ref[...] - pltpu.roll(d, A_HEAD // 2, 1) * sn_ref[...]
        o_ref[...] = d

    blk = pl.BlockSpec((tm, A_HEAD), lambda i, h: (i, h))
    tab = pl.BlockSpec((tm, A_HEAD), lambda i, h: (i % nl, 0))
    return pl.pallas_call(
        body, name=name, grid=(T // tm, 4), in_specs=[blk, blk, blk, tab, tab], out_specs=blk,
        out_shape=jax.ShapeDtypeStruct((T, 512), F32), compiler_params=_cparams("parallel", "parallel"),
    )(d1, d2, d3, cos2, sin2)


def _band_masks():
    i = lax.broadcasted_iota(jnp.int32, (B_SPAN, B_SPAN), 0)
    j = lax.broadcasted_iota(jnp.int32, (B_SPAN, B_SPAN), 1)
    return i <= j, j <= i


def _dil_geom(dil, L):
    return B_SPAN * dil, L // (B_SPAN * dil), (1 if dil > 1 else 4)


def _dil_rows(r, dil):
    return pl.ds(r, B_SPAN, stride=dil) if dil > 1 else pl.ds(0, B_SPAN)


def _dil_loop(dil, step):
    for r in range(dil):
        step(r, 0)


def _dil_specs(dil, B, L):
    W, nb, hb = _dil_geom(dil, L)
    cw = A_HEAD * hb

    def at(col0, shift):
        def index(b, n, hh):
            return (b * nb + jnp.clip(n + shift, 0, nb - 1), col0 // cw + hh)
        return pl.BlockSpec((W, cw), index)

    return at


def _dil_fwd(qr, kr, z, dil, B, L, *, name):
    T = B * L
    W, nb, hb = _dil_geom(dil, L)
    has_prev = nb > 1
    scale = A_HEAD ** -0.5
    at = _dil_specs(dil, B, L)

    def body(*refs):
        if has_prev:
            q_ref, kc_ref, vc_ref, kp_ref, vp_ref, o_ref, l_ref = refs
        else:
            q_ref, kc_ref, vc_ref, o_ref, l_ref = refs
        mp, mc = _band_masks()
        mp = mp & (pl.program_id(1) > 0)

        def step(r, carry):
            rows = _dil_rows(r, dil)
            for h in range(hb):
                cols = pl.ds(h * A_HEAD, A_HEAD)
                qh = q_ref[rows, cols].astype(BF16)
                sc = jnp.where(mc, _dot(qh, kc_ref[rows, cols].astype(BF16), _NT) * scale, NEG_BIG)
                m = jnp.max(sc, axis=-1, keepdims=True)
                if has_prev:
                    sp = jnp.where(mp, _dot(qh, kp_ref[rows, cols].astype(BF16), _NT) * scale, NEG_BIG)
                    m = jnp.maximum(m, jnp.max(sp, axis=-1, keepdims=True))
                pc = jnp.exp(sc - m)
                l = jnp.sum(pc, axis=-1, keepdims=True)
                o = _dot(pc.astype(BF16), vc_ref[rows, cols].astype(BF16))
                if has_prev:
                    pp = jnp.exp(sp - m)
                    l = l + jnp.sum(pp, axis=-1, keepdims=True)
                    o = o + _dot(pp.astype(BF16), vp_ref[rows, cols].astype(BF16))
                o_ref[rows, cols] = o / l
                l_ref[rows, cols] = jnp.broadcast_to(m + jnp.log(l), (B_SPAN, A_HEAD))
            return carry

        _dil_loop(dil, step)

    ins = [qr, kr, z] + ([kr, z] if has_prev else [])
    in_specs = [at(0, 0), at(0, 0), at(3072, 0)] + ([at(0, -1), at(3072, -1)] if has_prev else [])
    return pl.pallas_call(
        body, name=name, grid=(B, nb, 4 // hb), in_specs=in_specs, out_specs=[at(0, 0), at(0, 0)],
        out_shape=[jax.ShapeDtypeStruct((T, 512), F32)] * 2,
        compiler_params=_cparams("parallel", "parallel", "parallel"),
    )(*ins)


def _dil_combine(os_, ls_, *, name):
    T = os_[0].shape[0]
    tm = 256

    def body(o1, o2, o3, l1, l2, l3, ob_ref, lse_ref):
        a1, a2, a3 = l1[...], l2[...], l3[...]
        m = jnp.maximum(jnp.maximum(a1, a2), a3)
        e1, e2, e3 = jnp.exp(a1 - m), jnp.exp(a2 - m), jnp.exp(a3 - m)
        den = e1 + e2 + e3
        ob_ref[...] = (e1 * o1[...] + e2 * o2[...] + e3 * o3[...]) / den
        lse_ref[...] = m + jnp.log(den)

    blk = pl.BlockSpec((tm, 512), lambda i: (i, 0))
    return pl.pallas_call(
        body, name=name, grid=(T // tm,), in_specs=[blk] * 6, out_specs=[blk, blk],
        out_shape=[jax.ShapeDtypeStruct((T, 512), F32)] * 2, compiler_params=_cparams("parallel"),
    )(*[o.reshape(T, 512) for o in os_], *[l.reshape(T, 512) for l in ls_])


def _dil_bwd_q(qr, kr, z, dymix, out, lse, dil, B, L, *, name):
    T = B * L
    W, nb, hb = _dil_geom(dil, L)
    has_prev = nb > 1
    scale = A_HEAD ** -0.5
    at = _dil_specs(dil, B, L)

    def body(*refs):
        if has_prev:
            q_ref, kc_ref, vc_ref, do_ref, out_ref, lse_ref, kp_ref, vp_ref, dq_ref = refs
        else:
            q_ref, kc_ref, vc_ref, do_ref, out_ref, lse_ref, dq_ref = refs
        mp, mc = _band_masks()
        mp = mp & (pl.program_id(1) > 0)

        def step(r, carry):
            rows = _dil_rows(r, dil)
            for h in range(hb):
                cols = pl.ds(h * A_HEAD, A_HEAD)
                qh = q_ref[rows, cols].astype(BF16)
                do = do_ref[rows, cols]
                delta = jnp.sum(do * out_ref[rows, cols], axis=-1, keepdims=True)
                dob = do.astype(BF16)
                lse_h = lse_ref[rows, cols]
                kc = kc_ref[rows, cols].astype(BF16)
                pc = jnp.where(mc, jnp.exp(_dot(qh, kc, _NT) * scale - lse_h), 0.0)
                dsc = pc * (_dot(dob, vc_ref[rows, cols].astype(BF16), _NT) - delta) * scale
                dq = _dot(dsc.astype(BF16), kc)
                if has_prev:
                    kp = kp_ref[rows, cols].astype(BF16)
                    pp = jnp.where(mp, jnp.exp(_dot(qh, kp, _NT) * scale - lse_h), 0.0)
                    dsp = pp * (_dot(dob, vp_ref[rows, cols].astype(BF16), _NT) - delta) * scale
                    dq = dq + _dot(dsp.astype(BF16), kp)
                dq_ref[rows, cols] = dq
            return carry

        _dil_loop(dil, step)

    ins = [qr, kr, z, dymix, out, lse] + ([kr, z] if has_prev else [])
    in_specs = ([at(0, 0), at(0, 0), at(3072, 0), at(512, 0), at(0, 0), at(0, 0)]
                + ([at(0, -1), at(3072, -1)] if has_prev else []))
    return pl.pallas_call(
        body, name=name, grid=(B, nb, 4 // hb), in_specs=in_specs, out_specs=at(0, 0),
        out_shape=jax.ShapeDtypeStruct((T, 512), F32),
        compiler_params=_cparams("parallel", "parallel", "parallel"),
    )(*ins)


def _dil_bwd_kv(qr, kr, z, dymix, out, lse, dil, B, L, *, name):
    T = B * L
    W, nb, hb = _dil_geom(dil, L)
    has_next = nb > 1
    scale = A_HEAD ** -0.5
    at = _dil_specs(dil, B, L)

    def body(*refs):
        k_ref, v_ref = refs[0], refs[1]
        own = refs[2:6]
        nxt = refs[6:10] if has_next else None
        dk_ref, dv_ref = refs[-2], refs[-1]
        mp, mc = _band_masks()
        mp = mp & (pl.program_id(1) < nb - 1)
        groups = [(own, mc)] + ([(nxt, mp)] if has_next else [])

        def step(r, carry):
            rows = _dil_rows(r, dil)
            for h in range(hb):
                cols = pl.ds(h * A_HEAD, A_HEAD)
                kh, vh = k_ref[rows, cols].astype(BF16), v_ref[rows, cols].astype(BF16)
                dk = jnp.zeros((B_SPAN, A_HEAD), F32)
                dv = jnp.zeros((B_SPAN, A_HEAD), F32)
                for (q_ref, do_ref, out_ref, lse_ref), mask in groups:
                    qh = q_ref[rows, cols].astype(BF16)
                    do = do_ref[rows, cols]
                    delta = jnp.sum(do * out_ref[rows, cols], axis=-1, keepdims=True)
                    dob = do.astype(BF16)
                    p = jnp.where(mask, jnp.exp(_dot(qh, kh, _NT) * scale - lse_ref[rows, cols]), 0.0)
                    dv = dv + _dot(p.astype(BF16), dob, _TN)
                    ds = p * (_dot(dob, vh, _NT) - delta) * scale
                    dk = dk + _dot(ds.astype(BF16), qh, _TN)
                dk_ref[rows, cols] = dk
                dv_ref[rows, cols] = dv
            return carry

        _dil_loop(dil, step)

    ins = [kr, z, qr, dymix, out, lse] + ([qr, dymix, out, lse] if has_next else [])
    in_specs = ([at(0, 0), at(3072, 0), at(0, 0), at(512, 0), at(0, 0), at(0, 0)]
                + ([at(0, 1), at(512, 1), at(0, 1), at(0, 1)] if has_next else []))
    return pl.pallas_call(
        body, name=name, grid=(B, nb, 4 // hb), in_specs=in_specs, out_specs=[at(0, 0), at(0, 0)],
        out_shape=[jax.ShapeDtypeStruct((T, 512), F32)] * 2,
        compiler_params=_cparams("parallel", "parallel", "parallel"),
    )(*ins)


def _s5_build(lam_re, lam_im, log_dt, b_re, b_im, c_re, c_im):
    G, P, TC = C_GROUPS, C_STATE, C_TC
    lr = jnp.minimum(lam_re, C_MIN_NEG_RE)
    li = lam_im
    dt = jnp.exp(log_dt)[:, None]
    mag = jnp.exp(dt * lr)
    ar, ai = mag * jnp.cos(dt * li), mag * jnp.sin(dt * li)
    den = lr * lr + li * li
    zr = ((ar - 1.0) * lr + ai * li) / den
    zi = (ai * lr - (ar - 1.0) * li) / den
    bbr = zr[..., None] * b_re - zi[..., None] * b_im
    bbi = zr[..., None] * b_im + zi[..., None] * b_re
    ks = jnp.arange(TC + 1, dtype=F32)[:, None, None]
    pmag = jnp.exp(ks * (dt * lr)[None])
    pr, pi = pmag * jnp.cos(ks * (dt * li)[None]), pmag * jnp.sin(ks * (dt * li)[None])
    car = c_re[None] * pr[:, :, None, :] - c_im[None] * pi[:, :, None, :]
    cai = c_re[None] * pi[:, :, None, :] + c_im[None] * pr[:, :, None, :]
    kern = (jnp.einsum('lgop,gpc->lgco', car[:TC], bbr, precision=HI)
            - jnp.einsum('lgop,gpc->lgco', cai[:TC], bbi, precision=HI))
    s_idx = jnp.arange(TC)[:, None]
    t_idx = jnp.arange(TC)[None, :]
    lag = t_idx - s_idx
    ksel = jnp.where((lag >= 0)[:, :, None, None, None], kern[jnp.clip(lag, 0, TC - 1)], 0.0)
    m_mat = ksel.transpose(2, 0, 3, 1, 4).reshape(G, TC * C_GROUP, TC * C_GROUP)
    pr_e, pi_e = pr[TC - 1 - jnp.arange(TC)], pi[TC - 1 - jnp.arange(TC)]
    er = pr_e[:, :, :, None] * bbr[None] - pi_e[:, :, :, None] * bbi[None]
    ei = pr_e[:, :, :, None] * bbi[None] + pi_e[:, :, :, None] * bbr[None]
    e_mat = jnp.concatenate([er.transpose(1, 0, 3, 2), ei.transpose(1, 0, 3, 2)], axis=-1).reshape(G, TC * C_GROUP, 2 * P)
    fr = car[1:].transpose(1, 3, 0, 2)
    fi = -cai[1:].transpose(1, 3, 0, 2)
    f_mat = jnp.concatenate([fr, fi], axis=1).reshape(G, 2 * P, TC * C_GROUP)
    return m_mat, e_mat, f_mat, pr[TC], pi[TC]


C_NB = C_GROUPS * C_GROUP // LANES
C_GB = C_GROUPS // C_NB
C_W8 = C_TC * LANES
C_S8 = 2 * C_GB * C_STATE


def _s5_embed(m_mat, e_mat, f_mat):
    eye = jnp.eye(C_GB, dtype=F32)
    mr = m_mat.reshape(C_NB, C_GB, C_TC, C_GROUP, C_TC, C_GROUP)
    m8 = mr[:, :, :, :, :, None, :] * eye[None, :, None, None, None, :, None]
    m8 = m8.transpose(0, 2, 1, 3, 4, 5, 6).reshape(C_NB, C_W8, C_W8)
    er = e_mat.reshape(C_NB, C_GB, C_TC, C_GROUP, 2, C_STATE)
    e8 = er[:, :, :, :, :, None, :] * eye[None, :, None, None, None, :, None]
    e8 = e8.transpose(0, 2, 1, 3, 4, 5, 6).reshape(C_NB, C_W8, C_S8)
    fr = f_mat.reshape(C_NB, C_GB, 2, C_STATE, C_TC, C_GROUP)
    f8 = fr[:, :, :, :, :, None, :] * eye[None, :, None, None, None, :, None]
    f8 = f8.transpose(0, 2, 1, 3, 4, 5, 6).reshape(C_NB, C_S8, C_W8)
    return m8, e8, f8


def _s5_scan_tables(lam_re, lam_im, log_dt, nsteps):
    lr = jnp.minimum(lam_re, C_MIN_NEG_RE)
    dt = jnp.exp(log_dt)[:, None]
    ks = (C_TC * 2.0 ** jnp.arange(8, dtype=F32))[None, :, None]
    keep = (jnp.arange(8) < nsteps)[None, :, None]
    pmag = jnp.exp(ks * (dt * lr)[:, None, :])
    ang = ks * (dt * lam_im)[:, None, :]

    def blocks(t):
        return t.reshape(C_NB, C_GB, 8, C_STATE).transpose(0, 2, 1, 3).reshape(C_NB, 8, C_GB * C_STATE)

    pr = blocks(jnp.where(keep, pmag * jnp.cos(ang), 0.0))
    pi = blocks(jnp.where(keep, pmag * jnp.sin(ang), 0.0))
    return jnp.concatenate([pr, pr], axis=-1), jnp.concatenate([-pi, pi], axis=-1)


def _s5_rows(t, R):
    return pl.ds(t, R, stride=C_TC)


def _s5_fwd(u, dsk, m8, e8, f8, tab_r, tab_i, B, L, *, name):
    T = B * L
    R = L // C_TC
    nsteps = int(math.log2(R))

    def body(u_ref, d_ref, m_ref, e_ref, f_ref, tr_ref, ti_ref, gl_ref, y_ref, x8_ref, xs_ref):
        for t in range(C_TC):
            x8_ref[0, :, t * LANES:(t + 1) * LANES] = u_ref[_s5_rows(t, R), :].astype(BF16)
        x8 = x8_ref[0]
        x = _dot(x8, e_ref[0])
        row = lax.broadcasted_iota(jnp.int32, (R, C_S8), 0)
        for k in range(nsteps):
            s = 1 << k
            sh = pltpu.roll(x, s, 0)
            upd = tr_ref[0, k:k + 1, :] * sh + ti_ref[0, k:k + 1, :] * pltpu.roll(sh, C_S8 // 2, 1)
            x = x + jnp.where(row >= s, upd, 0.0)
        xs = jnp.where(row >= 1, pltpu.roll(x, 1, 0), 0.0)
        xs_ref[0] = xs
        y8 = _dot(x8, m_ref[0]) + _dot(xs.astype(BF16), f_ref[0])
        d = d_ref[...]
        for t in range(C_TC):
            rows = _s5_rows(t, R)
            y = y8[:, t * LANES:(t + 1) * LANES] + d * u_ref[rows, :]
            y_ref[rows, :] = y
            gl_ref[rows, :] = 0.5 * y * (1.0 + _erf(y * (2.0 ** -0.5)))

    tok = pl.BlockSpec((L, LANES), lambda c, b: (b, c))
    per_block = lambda shape: pl.BlockSpec((1,) + shape, lambda c, b: (c, 0, 0))
    per_step = lambda shape: pl.BlockSpec((1,) + shape, lambda c, b: (c * B + b, 0, 0))
    return pl.pallas_call(
        body, name=name, grid=(C_NB, B),
        in_specs=[tok, pl.BlockSpec((1, LANES), lambda c, b: (0, c)), per_block((C_W8, C_W8)), per_block((C_W8, C_S8)),
                  per_block((C_S8, C_W8)), per_block((8, C_S8)), per_block((8, C_S8))],
        out_specs=[tok, tok, per_step((R, C_W8)), per_step((R, C_S8))],
        out_shape=[jax.ShapeDtypeStruct((T, D_MODEL), F32), jax.ShapeDtypeStruct((T, D_MODEL), F32),
                   jax.ShapeDtypeStruct((C_NB * B, R, C_W8), BF16), jax.ShapeDtypeStruct((C_NB * B, R, C_S8), F32)],
        compiler_params=_cparams("parallel", "parallel"),
    )(u, dsk, m8, e8, f8, tab_r, tab_i)


def _s5_bwd(dgl, y, u, dsk, xs, m8, e8, f8, tab_r, tab_i, B, L, *, name):
    T = B * L
    R = L // C_TC
    nsteps = int(math.log2(R))

    def body(dgl_ref, y_ref, u_ref, d_ref, xs_ref, m_ref, e_ref, f_ref, tr_ref, ti_ref,
             du_ref, dy8_ref, de_ref, da_ref, dd_ref, dyf_ref):
        @pl.when(pl.program_id(1) == 0)
        def _():
            da_ref[...] = jnp.zeros_like(da_ref)
            dd_ref[...] = jnp.zeros_like(dd_ref)

        dd = jnp.zeros((1, LANES), F32)
        for t in range(C_TC):
            rows = _s5_rows(t, R)
            yv = y_ref[rows, :]
            cdf = 0.5 * (1.0 + _erf(yv * (2.0 ** -0.5)))
            pdf = jnp.exp(-0.5 * yv * yv) * (1.0 / math.sqrt(2.0 * math.pi))
            dy = dgl_ref[rows, :] * (cdf + yv * pdf)
            dd = dd + jnp.sum(dy * u_ref[rows, :], axis=0, keepdims=True)
            dyf_ref[:, t * LANES:(t + 1) * LANES] = dy
        dd_ref[...] += dd
        dy8 = dyf_ref[...].astype(BF16)
        dy8_ref[0] = dy8
        xs = xs_ref[0]
        gx = _dot(dy8, f_ref[0], _NT)
        row = lax.broadcasted_iota(jnp.int32, (R, C_S8), 0)
        for k in range(nsteps):
            s = 1 << k
            sh = pltpu.roll(gx, R - s, 0)
            upd = tr_ref[0, k:k + 1, :] * sh - ti_ref[0, k:k + 1, :] * pltpu.roll(sh, C_S8 // 2, 1)
            gx = gx + jnp.where(row + s < R, upd, 0.0)
        de_in = jnp.where(row + 1 < R, pltpu.roll(gx, R - 1, 0), 0.0)
        deb = de_in.astype(BF16)
        de_ref[0] = deb
        da_ref[0, 0:1, :] += jnp.sum(de_in * xs, axis=0, keepdims=True)
        da_ref[0, 1:2, :] += jnp.sum(de_in * pltpu.roll(xs, C_S8 // 2, 1), axis=0, keepdims=True)
        dx8 = _dot(dy8, m_ref[0], _NT) + _dot(deb, e_ref[0], _NT)
        d = d_ref[...]
        for t in range(C_TC):
            cols = slice(t * LANES, (t + 1) * LANES)
            du_ref[_s5_rows(t, R), :] = dx8[:, cols] + d * dyf_ref[:, cols]

    tok = pl.BlockSpec((L, LANES), lambda c, b: (b, c))
    vec = pl.BlockSpec((1, LANES), lambda c, b: (0, c))
    per_block = lambda shape: pl.BlockSpec((1,) + shape, lambda c, b: (c, 0, 0))
    per_step = lambda shape: pl.BlockSpec((1,) + shape, lambda c, b: (c * B + b, 0, 0))
    return pl.pallas_call(
        body, name=name, grid=(C_NB, B),
        in_specs=[tok, tok, tok, vec, per_step((R, C_S8)), per_block((C_W8, C_W8)),
                  per_block((C_W8, C_S8)), per_block((C_S8, C_W8)), per_block((8, C_S8)), per_block((8, C_S8))],
        out_specs=[tok, per_step((R, C_W8)), per_step((R, C_S8)), per_block((8, C_S8)), vec],
        out_shape=[jax.ShapeDtypeStruct((T, D_MODEL), F32), jax.ShapeDtypeStruct((C_NB * B, R, C_W8), BF16),
                   jax.ShapeDtypeStruct((C_NB * B, R, C_S8), BF16), jax.ShapeDtypeStruct((C_NB, 8, C_S8), F32),
                   jax.ShapeDtypeStruct((1, D_MODEL), F32)],
        scratch_shapes=[pltpu.VMEM((R, C_W8), F32)],
        compiler_params=_cparams("parallel", "arbitrary"),
    )(dgl, y, u, dsk, xs, m8, e8, f8, tab_r, tab_i)


def _bmm_tn(a, b, nb, *, name):
    a = a.reshape(nb, -1, a.shape[-1])
    b = b.reshape(nb, -1, b.shape[-1])
    K, M, N = a.shape[1], a.shape[2], b.shape[2]

    def body(a_ref, b_ref, o_ref):
        o_ref[0] = _dot(a_ref[0].astype(BF16), b_ref[0].astype(BF16), _TN)

    return pl.pallas_call(
        body, name=name, grid=(nb,),
        in_specs=[pl.BlockSpec((1, K, M), lambda c: (c, 0, 0)), pl.BlockSpec((1, K, N), lambda c: (c, 0, 0))],
        out_specs=pl.BlockSpec((1, M, N), lambda c: (c, 0, 0)),
        out_shape=jax.ShapeDtypeStruct((nb, M, N), F32), compiler_params=_cparams("parallel"),
    )(a, b)


def _loss_head(y, target, *, name):
    T, C = y.shape
    tm = 256

    def body(y_ref, t_ref, l_ref, d_ref):
        err = y_ref[...] - t_ref[...]
        d_ref[...] = err * (1.0 / C)
        sq = err * err
        part = jnp.zeros((8, LANES), F32)
        for r in range(0, tm, 8):
            for c in range(0, C, LANES):
                part = part + sq[r:r + 8, c:c + LANES]

        @pl.when(pl.program_id(0) == 0)
        def _():
            l_ref[...] = part

        @pl.when(pl.program_id(0) > 0)
        def _():
            l_ref[...] += part

    row = pl.BlockSpec((tm, C), lambda i: (i, 0))
    acc = pl.BlockSpec((8, LANES), lambda i: (0, 0))
    return pl.pallas_call(
        body, name=name, grid=(T // tm,), in_specs=[row, row], out_specs=[acc, row],
        out_shape=[jax.ShapeDtypeStruct((8, LANES), F32), jax.ShapeDtypeStruct((T, C), F32)],
        compiler_params=_cparams("arbitrary"),
    )(y, target)


def _adamw(w, g, m, v, *, name):
    shape = w.shape
    size = int(np.prod(shape))
    cols = LANES if (shape[-1] < LANES and size % LANES == 0) else shape[-1]
    rows = size // cols
    tm = _tile(rows, 256) if rows % 8 == 0 else rows
    w2, g2, m2, v2 = (t.reshape(rows, cols) for t in (w, g, m, v))

    def body(w_ref, g_ref, m_ref, v_ref, d_ref, nm_ref, nv_ref):
        gg = g_ref[...]
        nm = ADAM_B1 * m_ref[...] + (1.0 - ADAM_B1) * gg
        nv = ADAM_B2 * v_ref[...] + (1.0 - ADAM_B2) * (gg * gg)
        m_hat = nm / (1.0 - ADAM_B1 ** ADAM_STEP)
        v_hat = nv / (1.0 - ADAM_B2 ** ADAM_STEP)
        d_ref[...] = -ADAM_LR * (m_hat / (jnp.sqrt(v_hat) + ADAM_EPS) + ADAM_WD * w_ref[...])
        nm_ref[...] = nm
        nv_ref[...] = nv

    blk = pl.BlockSpec((tm, cols), lambda i: (i, 0))
    outs = pl.pallas_call(
        body, name=name, grid=(rows // tm,), in_specs=[blk] * 4, out_specs=[blk] * 3,
        out_shape=[jax.ShapeDtypeStruct((rows, cols), F32)] * 3, compiler_params=_cparams("parallel"),
    )(w2, g2, m2, v2)
    return tuple(o.reshape(shape) for o in outs)


def _all_gather(shard, *, name):
    R, C = shard.shape

    def body(x_ref, out_ref, send_sems, recv_sems, local_sem):
        x, y, c = lax.axis_index("x"), lax.axis_index("y"), lax.axis_index("c")
        me, sibling = (x, y, c), (x, y, 1 - c)
        chips = [(1 - x, y), (x, 1 - y), (1 - x, 1 - y)]

        def rows(px, py, pc):
            return out_ref.at[_logical(px, py, pc)]

        def copy(k, block, to, src=None):
            return pltpu.make_async_remote_copy(
                src_ref=rows(*block) if src is None else src, dst_ref=rows(*block),
                send_sem=send_sems.at[k], recv_sem=recv_sems.at[k], device_id=to, device_id_type=_MESH)

        mine = pltpu.make_async_copy(x_ref, rows(*me), local_sem)
        mine.start()
        first = [copy(0, me, sibling, src=x_ref)]
        first += [copy(1 + j, me, (*chip, c), src=x_ref) for j, chip in enumerate(chips)]
        for cp in first:
            cp.start()
        passed = [copy(4 + j, (*chip, c), sibling) for j, chip in enumerate(chips)]
        for j, chip in enumerate(chips):
            copy(1 + j, (*chip, c), me).wait_recv()
            passed[j].start()
        copy(0, sibling, me).wait_recv()
        for j, chip in enumerate(chips):
            copy(4 + j, (*chip, 1 - c), me).wait_recv()
        for cp in first + passed:
            cp.wait_send()
        mine.wait()

    return pl.pallas_call(
        body, name=name, out_shape=jax.ShapeDtypeStruct((N_DEV, R, C), shard.dtype),
        in_specs=[_HBM], out_specs=_HBM,
        scratch_shapes=[pltpu.SemaphoreType.DMA((7,)), pltpu.SemaphoreType.DMA((7,)), pltpu.SemaphoreType.DMA],
    )(shard)


def _gather_weights(shards, *, name):
    nt = len(shards)

    def body(*refs):
        ins, outs = refs[:nt], refs[nt:2 * nt]
        send_sems, recv_sems, local_sems = refs[2 * nt:]
        x, y, c = lax.axis_index("x"), lax.axis_index("y"), lax.axis_index("c")
        me, sibling = (x, y, c), (x, y, 1 - c)
        chips = [(1 - x, y), (x, 1 - y), (1 - x, 1 - y)]

        def copy(t, k, block, to, src=None):
            rows = outs[t].at[_logical(*block)]
            return pltpu.make_async_remote_copy(
                src_ref=rows if src is None else src, dst_ref=rows,
                send_sem=send_sems.at[t, k], recv_sem=recv_sems.at[t, k], device_id=to, device_id_type=_MESH)

        mine = [pltpu.make_async_copy(ins[t], outs[t].at[_logical(*me)], local_sems.at[t]) for t in range(nt)]
        for cp in mine:
            cp.start()
        started = []
        for t in range(nt):
            started.append(copy(t, 0, me, sibling, src=ins[t]))
            started += [copy(t, 1 + j, me, (*chip, c), src=ins[t]) for j, chip in enumerate(chips)]
        for cp in started:
            cp.start()
        for j, chip in enumerate(chips):
            for t in range(nt):
                copy(t, 1 + j, (*chip, c), me).wait_recv()
                fwd = copy(t, 4 + j, (*chip, c), sibling)
                fwd.start()
                started.append(fwd)
        for t in range(nt):
            copy(t, 0, sibling, me).wait_recv()
            for j, chip in enumerate(chips):
                copy(t, 4 + j, (*chip, 1 - c), me).wait_recv()
        for cp in started:
            cp.wait_send()
        for cp in mine:
            cp.wait()

    return pl.pallas_call(
        body, name=name, out_shape=[jax.ShapeDtypeStruct((N_DEV,) + s.shape, s.dtype) for s in shards],
        in_specs=[_HBM] * nt, out_specs=[_HBM] * nt,
        scratch_shapes=[pltpu.SemaphoreType.DMA((nt, 7)), pltpu.SemaphoreType.DMA((nt, 7)), pltpu.SemaphoreType.DMA((nt,))],
    )(*shards)


def _sum_rows(stacked, *, name):
    _, R, C = stacked.shape
    tr = R
    if N_DEV * R * C * stacked.dtype.itemsize > 12 * 1024 * 1024:
        for cand in range(512, 15, -16):
            if R % cand == 0:
                tr = cand
                break

    def body(s_ref, o_ref):
        acc = s_ref[0].astype(F32)
        for k in range(1, N_DEV):
            acc = acc + s_ref[k].astype(F32)
        o_ref[...] = acc

    return pl.pallas_call(
        body, name=name, grid=(R // tr,),
        in_specs=[pl.BlockSpec((N_DEV, tr, C), lambda i: (0, i, 0))], out_specs=pl.BlockSpec((tr, C), lambda i: (i, 0)),
        out_shape=jax.ShapeDtypeStruct((R, C), F32), compiler_params=_cparams("parallel"),
    )(stacked)


_LARGE = (("ab_w_in", 1, True), ("ab_w_out", 1, False), ("s5_w_glu", 1, True), ("xattn_wq", 2, False),
          ("xattn_wkv", 2, True), ("xattn_wo", 2, False), ("ffn_w_in", 2, True), ("ffn_w_out", 2, False))
_LARGE_KEYS = tuple((n, l) for n, layers, _ in _LARGE for l in range(layers))
_TRANSPOSED = {n: t for n, _, t in _LARGE}


def _owner_major(name, w):
    return w.T if _TRANSPOSED[name] else w


def _lb_from_logits(logits):
    return jnp.cumsum(jax.nn.softmax(logits, axis=0), axis=0)[0:1]


def _s5_maps(*params):
    m_mat, e_mat, f_mat, ar, ai = _s5_build(*params)
    return _s5_embed(m_mat, e_mat, f_mat) + (ar, ai)


def _local_step(x, mem, target, W, shards=None):
    B, L, _ = x.shape
    T = B * L
    x0 = x.reshape(T, D_MODEL)
    memf = mem.reshape(B * MEM_LEN, D_MODEL)
    nw = W["norm_w"]
    cos2, sin2 = _rope_tables(L)
    W = dict(W)
    G, received = {}, {}

    def mmx(a, b, gather=(), scatter=(), **kw):
        if shards is None or not (gather or scatter):
            return _mm(a, b, **kw)
        out, gathered, got = _mm(a, b, gather=[shards[k] for k in gather],
                                 scatter=[G[k].reshape(N_DEV, -1, D_MODEL) for k in scatter], **kw)
        for k, g in zip(gather, gathered):
            W[k] = g.reshape(-1, D_MODEL)
        for k, r in zip(scatter, got):
            received[k] = r
        return out

    def vec(v):
        return v.reshape(1, -1)

    saved = []
    xin = x0
    for layer in range(2):
        s = {"x0": xin}
        tag = f"l{layer}"
        h1 = _rms_fwd(xin, vec(nw[layer, 0]), name=f"norm_pre_mix_{tag}", out_dtype=BF16 if layer == 0 else F32)
        s["h1"] = h1
        if layer == 0:
            lb, lb_vjp = jax.vjp(_lb_from_logits, W["hgrn_lb_logits"])
            onw = W["hgrn_out_norm_w"].reshape(1, A_WIDTH)
            z = mmx(h1, W["ab_w_in", 0], tb=True, name="ab_in",
                    gather=[("ab_w_out", 0), ("xattn_wq", 0), ("xattn_wkv", 0), ("xattn_wo", 0), ("ffn_w_in", 0)])
            oa, o_raw, s_start = _hgrn_fwd(z, lb, onw, B, L, name="hgrn_fwd")
            qr = _rope_fwd(z, cos2, sin2, 16, B, L, name="rope_q")
            kr = _rope_fwd(z, cos2, sin2, 20, B, L, name="rope_k")
            os_, ls_ = [], []
            for dil in B_DILS:
                o_g, l_g = _dil_fwd(qr, kr, z, dil, B, L, name=f"dil_fwd_{dil}")
                os_.append(o_g)
                ls_.append(l_g)
            ob, lse = _dil_combine(os_, ls_, name="dil_combine")
            ymix = jnp.concatenate([oa, ob], axis=-1).astype(BF16)
            y1 = mmx(ymix, W["ab_w_out", 0], name="ab_out", gather=[("ffn_w_out", 0)])
            s.update(z=z, lb=lb, lb_vjp=lb_vjp, onw=onw, o_raw=o_raw, s_start=s_start, qr=qr, kr=kr, ob=ob, lse=lse, ymix=ymix)
        else:
            p5 = tuple(W[n][0] for n in ("s5_lambda_re", "s5_lambda_im", "s5_log_dt", "s5_b_re", "s5_b_im", "s5_c_re", "s5_c_im"))
            (m8, e8, f8, _, _), s5_vjp = jax.vjp(_s5_maps, *p5)
            tab_r, tab_i = _s5_scan_tables(p5[0], p5[1], p5[2], int(math.log2(L // C_TC)))
            mats = (m8.astype(BF16), e8.astype(BF16), f8.astype(BF16), tab_r, tab_i)
            dsk = W["s5_d"].reshape(1, D_MODEL)
            gl, ypre, x8, xs = _s5_fwd(h1, dsk, *mats, B, L, name="s5_fwd")
            w_glu = _interleave_rows(W["s5_w_glu", 0])
            zg, y1 = _mm(gl, w_glu, tb=True, name="s5_glu_in", gate=("fwd", "glu"))
            s.update(s5_vjp=s5_vjp, mats=mats, x8=x8, xs=xs, dsk=dsk, gl=gl, ypre=ypre, zg=zg, w_glu=w_glu)
        x1 = _rms_fwd(y1, vec(nw[layer, 1]), xin, name=f"norm_post_mix_{tag}")
        h2 = _rms_fwd(x1, vec(nw[layer, 2]), name=f"norm_pre_x_{tag}", out_dtype=BF16)
        memn = _rms_fwd(memf, vec(W["mem_norm_w"][layer]), name=f"norm_mem_{tag}", out_dtype=BF16)
        q = mmx(h2, W["xattn_wq", layer], out_dtype=BF16, name=f"x_q_{tag}", gather=[("s5_w_glu", 0)] if layer == 0 else [])
        kv = _mm(memn, W["xattn_wkv", layer], tb=True, out_dtype=BF16, name=f"x_kv_{tag}")
        o = _xattn_fwd(q, kv, B, L, name=f"x_attn_{tag}")
        y2 = mmx(o, W["xattn_wo", layer], name=f"x_o_{tag}", gather=[("xattn_wq", 1), ("xattn_wo", 1)] if layer == 0 else [])
        x2 = _rms_fwd(y2, vec(nw[layer, 3]), x1, name=f"norm_post_x_{tag}")
        h3 = _rms_fwd(x2, vec(nw[layer, 4]), name=f"norm_pre_ffn_{tag}", out_dtype=BF16)
        w_ffn_in = _interleave_rows(W["ffn_w_in", layer])
        zf, u = mmx(h3, w_ffn_in, tb=True, name=f"ffn_in_{tag}", gate=("fwd", "swiglu"),
                    gather=[("xattn_wkv", 1), ("ffn_w_in", 1), ("ffn_w_out", 1)] if layer == 0 else [])
        y3 = _mm(u, W["ffn_w_out", layer], name=f"ffn_out_{tag}")
        x3 = _rms_fwd(y3, vec(nw[layer, 5]), x2, name=f"norm_post_ffn_{tag}")
        s.update(y1=y1, x1=x1, h2=h2, memn=memn, q=q, kv=kv, o=o, y2=y2, x2=x2, h3=h3, zf=zf, u=u, y3=y3, w_ffn_in=w_ffn_in)
        saved.append(s)
        xin = x3

    loss_parts, dx = _loss_head(xin, target.reshape(T, D_MODEL), name="loss_head")

    d_norm = [[None] * 6 for _ in range(2)]
    d_memn = [None, None]
    for layer in (1, 0):
        s = saved[layer]
        tag = f"l{layer}"
        dy3, d_norm[layer][5] = _rms_bwd(s["y3"], vec(nw[layer, 5]), dx, name=f"bnorm_post_ffn_{tag}", out_dtype=BF16)
        dzf = mmx(dy3, W["ffn_w_out", layer], tb=True, name=f"b_ffn_out_dx_{tag}", gate=("bwd", "swiglu", s["zf"]),
                  scatter=[("xattn_wkv", 1), ("s5_w_glu", 0)] if layer == 0 else [])
        G["ffn_w_out", layer] = _mm(s["u"], dy3, ta=True, out_dtype=BF16, name=f"b_ffn_out_dw_{tag}")
        G["ffn_w_in", layer] = _interleave_rows(
            mmx(dzf, s["h3"], ta=True, out_dtype=BF16, name=f"b_ffn_in_dw_{tag}", scatter=[("ffn_w_out", layer)]), inverse=True)
        dh3 = mmx(dzf, s["w_ffn_in"], out_dtype=BF16, name=f"b_ffn_in_dx_{tag}", scatter=[("ffn_w_in", layer)])
        dx, d_norm[layer][4] = _rms_bwd(s["x2"], vec(nw[layer, 4]), dh3, dx, name=f"bnorm_pre_ffn_{tag}")
        dy2, d_norm[layer][3] = _rms_bwd(s["y2"], vec(nw[layer, 3]), dx, name=f"bnorm_post_x_{tag}", out_dtype=BF16)
        do = _mm(dy2, W["xattn_wo", layer], tb=True, out_dtype=BF16, name=f"b_x_o_dx_{tag}")
        G["xattn_wo", layer] = _mm(s["o"], dy2, ta=True, out_dtype=BF16, name=f"b_x_o_dw_{tag}")
        dq, dkv = _xattn_bwd(s["q"], s["kv"], do, B, L, name=f"b_x_attn_{tag}")
        G["xattn_wq", layer] = _mm(s["h2"], dq, ta=True, out_dtype=BF16, name=f"b_x_q_dw_{tag}")
        dh2 = _mm(dq, W["xattn_wq", layer], tb=True, out_dtype=BF16, name=f"b_x_q_dx_{tag}")
        G["xattn_wkv", layer] = _mm(dkv, s["memn"], ta=True, out_dtype=BF16, name=f"b_x_kv_dw_{tag}")
        dmemn = _mm(dkv, W["xattn_wkv", layer], out_dtype=BF16, name=f"b_x_kv_dx_{tag}")
        _, d_memn[layer] = _rms_bwd(memf, vec(W["mem_norm_w"][layer]), dmemn, name=f"bnorm_mem_{tag}", out_dtype=BF16)
        dx, d_norm[layer][2] = _rms_bwd(s["x1"], vec(nw[layer, 2]), dh2, dx, name=f"bnorm_pre_x_{tag}")
        dy1, d_norm[layer][1] = _rms_bwd(s["y1"], vec(nw[layer, 1]), dx, name=f"bnorm_post_mix_{tag}", out_dtype=BF16)
        if layer == 0:
            z = s["z"]
            dymix = _mm(dy1, W["ab_w_out", 0], tb=True, name="b_ab_out_dx")
            G["ab_w_out", 0] = _mm(s["ymix"], dy1, ta=True, out_dtype=BF16, name="b_ab_out_dw")
            dqa, dfa, dia, dga, d_onw, d_lb = _hgrn_bwd(z, s["lb"], s["onw"], s["o_raw"], s["s_start"], dymix, B, L, name="hgrn_bwd")
            dqs, dks, dvs = [], [], []
            for dil in B_DILS:
                dqs.append(_dil_bwd_q(s["qr"], s["kr"], z, dymix, s["ob"], s["lse"], dil, B, L, name=f"dil_bwd_q_{dil}"))
                dk_g, dv_g = _dil_bwd_kv(s["qr"], s["kr"], z, dymix, s["ob"], s["lse"], dil, B, L, name=f"dil_bwd_kv_{dil}")
                dks.append(dk_g)
                dvs.append(dv_g)
            dqb = _sum3(*dqs, cos2, sin2, True, B, L, name="b_rope_q")
            dkb = _sum3(*dks, cos2, sin2, True, B, L, name="b_rope_k")
            dvb = _sum3(*dvs, cos2, sin2, False, B, L, name="b_sum_v")
            dz = jnp.concatenate([dqa, dfa, dia, dga, dqb, dkb, dvb], axis=-1).astype(BF16)
            G["ab_w_in", 0] = mmx(dz, s["h1"], ta=True, out_dtype=BF16, name="b_ab_in_dw",
                                  scatter=[("xattn_wo", 0), ("xattn_wq", 0), ("xattn_wkv", 0), ("ab_w_out", 0)])
            dh1 = mmx(dz, W["ab_w_in", 0], out_dtype=BF16, name="b_ab_in_dx", scatter=[("ab_w_in", 0)])
            G["hgrn_out_norm_w"] = jnp.sum(d_onw.reshape(B, A_WIDTH), axis=0, keepdims=True)
            d_lb_row = jnp.sum(d_lb.reshape(B, A_WIDTH), axis=0, keepdims=True)
            G["hgrn_lb_logits"] = s["lb_vjp"](d_lb_row)[0]
        else:
            dzg = _gated_bwd(s["zg"], dy1, "glu", name="b_s5_glu")
            G["s5_w_glu", 0] = _interleave_rows(
                _mm(dzg, s["gl"], ta=True, out_dtype=BF16, name="b_s5_glu_dw"), inverse=True)
            dgl = mmx(dzg, s["w_glu"], name="b_s5_glu_dx", scatter=[("xattn_wo", 1), ("xattn_wq", 1)])
            dh1, dy8, de_in, da, d_dsk = _s5_bwd(dgl, s["ypre"], s["h1"], s["dsk"], s["xs"], *s["mats"], B, L, name="s5_bwd")
            dm8 = _bmm_tn(s["x8"], dy8, C_NB, name="s5_bwd_dm")
            df8 = _bmm_tn(s["xs"], dy8, C_NB, name="s5_bwd_df")
            de8 = _bmm_tn(s["x8"], de_in, C_NB, name="s5_bwd_de")
            half = C_S8 // 2
            da_r = (da[:, 0, :half] + da[:, 0, half:]).reshape(C_GROUPS, C_STATE)
            da_i = (da[:, 1, half:] - da[:, 1, :half]).reshape(C_GROUPS, C_STATE)
            gp = s["s5_vjp"]((dm8, de8, df8, da_r, da_i))
            for n, gv in zip(("s5_lambda_re", "s5_lambda_im", "s5_log_dt", "s5_b_re", "s5_b_im", "s5_c_re", "s5_c_im"), gp):
                G[n] = gv[None]
            G["s5_d"] = d_dsk
        dx, d_norm[layer][0] = _rms_bwd(s["x0"], vec(nw[layer, 0]), dh1, dx, name=f"bnorm_pre_mix_{tag}")

    G["norm_w"] = jnp.stack([jnp.concatenate(d_norm[l], axis=0) for l in range(2)])
    G["mem_norm_w"] = jnp.concatenate(d_memn, axis=0)
    if shards is not None:
        G.update(received)
    return loss_parts, dx.reshape(B, L, D_MODEL), G


_SMALL = (("norm_w", (2, 6, 1024)), ("mem_norm_w", (2, 1024)), ("hgrn_lb_logits", (3, 512)), ("hgrn_out_norm_w", (1, 512)),
          ("s5_lambda_re", (1, 64, 64)), ("s5_lambda_im", (1, 64, 64)), ("s5_log_dt", (1, 64)),
          ("s5_b_re", (1, 64, 64, 16)), ("s5_b_im", (1, 64, 64, 16)), ("s5_c_re", (1, 64, 16, 64)),
          ("s5_c_im", (1, 64, 16, 64)), ("s5_d", (1, 1024)))

_WEIGHT_ORDER = ('norm_w', 'mem_norm_w', 'ab_w_in', 'ab_w_out', 'hgrn_lb_logits', 'hgrn_out_norm_w', 's5_lambda_re',
                 's5_lambda_im', 's5_log_dt', 's5_b_re', 's5_b_im', 's5_c_re', 's5_c_im', 's5_d', 's5_w_glu', 'xattn_wq',
                 'xattn_wkv', 'xattn_wo', 'ffn_w_in', 'ffn_w_out')


def kernel(x, mem, norm_w, mem_norm_w, ab_w_in, ab_w_out, hgrn_lb_logits, hgrn_out_norm_w, s5_lambda_re, s5_lambda_im, s5_log_dt, s5_b_re, s5_b_im, s5_c_re, s5_c_im, s5_d, s5_w_glu, xattn_wq, xattn_wkv, xattn_wo, ffn_w_in, ffn_w_out, loss_target, m_norm_w, m_mem_norm_w, m_ab_w_in, m_ab_w_out, m_hgrn_lb_logits, m_hgrn_out_norm_w, m_s5_lambda_re, m_s5_lambda_im, m_s5_log_dt, m_s5_b_re, m_s5_b_im, m_s5_c_re, m_s5_c_im, m_s5_d, m_s5_w_glu, m_xattn_wq, m_xattn_wkv, m_xattn_wo, m_ffn_w_in, m_ffn_w_out, v_norm_w, v_mem_norm_w, v_ab_w_in, v_ab_w_out, v_hgrn_lb_logits, v_hgrn_out_norm_w, v_s5_lambda_re, v_s5_lambda_im, v_s5_log_dt, v_s5_b_re, v_s5_b_im, v_s5_c_re, v_s5_c_im, v_s5_d, v_s5_w_glu, v_xattn_wq, v_xattn_wkv, v_xattn_wo, v_ffn_w_in, v_ffn_w_out):
    local = dict(norm_w=norm_w, mem_norm_w=mem_norm_w, ab_w_in=ab_w_in, ab_w_out=ab_w_out, hgrn_lb_logits=hgrn_lb_logits,
                 hgrn_out_norm_w=hgrn_out_norm_w, s5_lambda_re=s5_lambda_re, s5_lambda_im=s5_lambda_im, s5_log_dt=s5_log_dt,
                 s5_b_re=s5_b_re, s5_b_im=s5_b_im, s5_c_re=s5_c_re, s5_c_im=s5_c_im, s5_d=s5_d, s5_w_glu=s5_w_glu,
                 xattn_wq=xattn_wq, xattn_wkv=xattn_wkv, xattn_wo=xattn_wo, ffn_w_in=ffn_w_in, ffn_w_out=ffn_w_out)
    mom_m = dict(zip(_WEIGHT_ORDER, (m_norm_w, m_mem_norm_w, m_ab_w_in, m_ab_w_out, m_hgrn_lb_logits, m_hgrn_out_norm_w, m_s5_lambda_re, m_s5_lambda_im, m_s5_log_dt, m_s5_b_re, m_s5_b_im, m_s5_c_re, m_s5_c_im, m_s5_d, m_s5_w_glu, m_xattn_wq, m_xattn_wkv, m_xattn_wo, m_ffn_w_in, m_ffn_w_out)))
    mom_v = dict(zip(_WEIGHT_ORDER, (v_norm_w, v_mem_norm_w, v_ab_w_in, v_ab_w_out, v_hgrn_lb_logits, v_hgrn_out_norm_w, v_s5_lambda_re, v_s5_lambda_im, v_s5_log_dt, v_s5_b_re, v_s5_b_im, v_s5_c_re, v_s5_c_im, v_s5_d, v_s5_w_glu, v_xattn_wq, v_xattn_wkv, v_xattn_wo, v_ffn_w_in, v_ffn_w_out)))
    dev = 4 * lax.axis_index("x") + 2 * lax.axis_index("y") + lax.axis_index("c")

    shards = {(n, l): _owner_major(n, local[n][l]).astype(BF16) for n, l in _LARGE_KEYS}
    first = ("ab_w_in", 0)
    W = {first: _gather_weights([shards[first]], name="gather_first")[0].reshape(-1, D_MODEL)}
    tiny =jnp.concatenate([norm_w.reshape(-1), s5_d.reshape(-1)])
    tiny = jnp.pad(tiny, (0, 16 * LANES - tiny.shape[0])).reshape(16, LANES)
    tiny_all = _all_gather(tiny, name="gather_tiny").reshape(N_DEV, 16 * LANES)
    W["norm_w"] = tiny_all[:, :12 * LANES].reshape(N_DEV, 2, 6, LANES).transpose(1, 2, 0, 3).reshape(2, 6, D_MODEL)
    W["s5_d"] = tiny_all[:, 12 * LANES:13 * LANES].reshape(1, D_MODEL)
    for n in ("mem_norm_w", "hgrn_lb_logits", "hgrn_out_norm_w", "s5_lambda_re", "s5_lambda_im", "s5_log_dt",
              "s5_b_re", "s5_b_im", "s5_c_re", "s5_c_im"):
        W[n] = local[n]

    loss_parts, grad_x, G = _local_step(x, mem, loss_target, W, shards)

    g_layers = {}
    for n, l in _LARGE_KEYS:
        g = _sum_rows(G[n, l], name=f"sum_grads_{n}_{l}")
        g_layers.setdefault(n, []).append(g.T if _TRANSPOSED[n] else g)
    g_local = {n: jnp.stack(gl) for n, gl in g_layers.items()}
    small =jnp.concatenate([G[n].reshape(-1) for n, _ in _SMALL] + [0.5 / D_MODEL * jnp.sum(loss_parts).reshape(1)])
    n_small = small.shape[0]
    small = jnp.pad(small, (0, (-n_small) % (8 * LANES))).reshape(-1, LANES)
    small_sum = _sum_rows(_all_gather(small, name="gather_small"), name="sum_small").reshape(-1)
    g_full, off = {}, 0
    for n, shp in _SMALL:
        size = int(np.prod(shp))
        g_full[n] = small_sum[off:off + size].reshape(shp)
        off += size
    loss = small_sum[off]
    grads = dict(g_local)
    for n, shp in _SMALL:
        if n == "norm_w":
            grads[n] = lax.dynamic_slice_in_dim(g_full[n], dev * LANES, LANES, axis=2)
        elif n == "s5_d":
            grads[n] = lax.dynamic_slice_in_dim(g_full[n], dev * LANES, LANES, axis=1)
        else:
            grads[n] = g_full[n]

    delta, new_m, new_v = {}, {}, {}
    for n in _WEIGHT_ORDER:
        delta[n], new_m[n], new_v[n] = _adamw(local[n], grads[n], mom_m[n], mom_v[n], name=f"adamw_{n}")
    return (loss, grad_x, *[grads[n] for n in _WEIGHT_ORDER], *[delta[n] for n in _WEIGHT_ORDER],
            *[new_m[n] for n in _WEIGHT_ORDER], *[new_v[n] for n in _WEIGHT_ORDER])
```

```python
import functools
import math

import numpy as np
import jax
import jax.numpy as jnp
from jax import lax
from jax.experimental import pallas as pl
from jax.experimental.pallas import tpu as pltpu

F32 = jnp.float32
BF16 = jnp.bfloat16
HI = lax.Precision.HIGHEST

D_MODEL = 1024
NORM_EPS = 1e-6
A_WIDTH = 512
A_HEAD = 128
A_CHUNK = 32
A_SUPER = 256
B_SPAN = 128
B_DILS = (1, 4, 16)
ROPE_THETA = 10000.0
C_GROUPS = 64
C_GROUP = 16
C_STATE = 64
C_TC = 8
C_MIN_NEG_RE = -1e-4
MEM_LEN = 256
X_HEADS = 4
X_HD = 256
D_FF = 2816
N_DEV = 8
LANES = 128

ADAM_LR, ADAM_B1, ADAM_B2, ADAM_EPS, ADAM_WD, ADAM_STEP = 0.001, 0.9, 0.999, 1e-08, 0.01, 10

NEG_BIG = -1e30


def _tile(n, pref):
    for d in range(min(pref, n) // LANES * LANES, 0, -LANES):
        if n % d == 0:
            return d
    return n


def _cparams(*sem):
    return pltpu.CompilerParams(dimension_semantics=sem, vmem_limit_bytes=56 * 1024 * 1024)


def _sigmoid(x):
    return 0.5 * jnp.tanh(0.5 * x) + 0.5


def _erf(x):
    ax = jnp.abs(x)
    t = 1.0 / (1.0 + 0.3275911 * ax)
    poly = t * (0.254829592 + t * (-0.284496736 + t * (1.421413741 + t * (-1.453152027 + t * 1.061405429))))
    y = 1.0 - poly * jnp.exp(-ax * ax)
    return jnp.where(x < 0, -y, y)


_HBM = pl.BlockSpec(memory_space=pltpu.HBM)
_MESH = pl.DeviceIdType.MESH


def _logical(px, py, pc):
    return 4 * px + 2 * py + pc


class _Exchange:
    def __init__(self, gather=(), scatter=()):
        self.gather, self.scatter = list(gather), list(scatter)
        self.ng, self.n = len(self.gather), len(self.gather) + len(self.scatter)

    def operands(self):
        return self.gather + self.scatter

    def in_specs(self):
        return [_HBM] * self.n

    def out_shapes(self):
        return ([jax.ShapeDtypeStruct((N_DEV,) + g.shape, g.dtype) for g in self.gather]
                + [jax.ShapeDtypeStruct(s.shape, s.dtype) for s in self.scatter])

    def scratch(self):
        if not self.n:
            return []
        return [pltpu.SemaphoreType.DMA((self.n, 7)), pltpu.SemaphoreType.DMA((self.n, 7)), pltpu.SemaphoreType.DMA((self.n,))]

    def split(self, results):
        return list(results[:self.ng]), list(results[self.ng:])

    def run(self, ins, outs, sems, first, last):
        if not self.n:
            return
        send_sems, recv_sems, local_sems = sems
        x, y, c = lax.axis_index("x"), lax.axis_index("y"), lax.axis_index("c")
        me, sibling = _logical(x, y, c), (x, y, 1 - c)
        chips = [(1 - x, y), (x, 1 - y), (1 - x, 1 - y)]
        peers = [(x ^ (k >> 2), y ^ ((k >> 1) & 1), c ^ (k & 1)) for k in range(1, N_DEV)]

        def remote(t, k, src, dst, to):
            return pltpu.make_async_remote_copy(src_ref=src, dst_ref=dst, send_sem=send_sems.at[t, k],
                                                recv_sem=recv_sems.at[t, k], device_id=to, device_id_type=_MESH)

        def local(t):
            src = ins[t] if t < self.ng else ins[t].at[me]
            return pltpu.make_async_copy(src, outs[t].at[me], local_sems.at[t])

        @pl.when(first)
        def _():
            for t in range(self.n):
                local(t).start()
                if t < self.ng:
                    remote(t, 0, ins[t], outs[t].at[me], sibling).start()
                    for j, chip in enumerate(chips):
                        remote(t, 1 + j, ins[t], outs[t].at[me], (*chip, c)).start()
                else:
                    for k, peer in enumerate(peers):
                        remote(t, k, ins[t].at[_logical(*peer)], outs[t].at[me], peer).start()

        @pl.when(last)
        def _():
            for j, chip in enumerate(chips):
                for t in range(self.ng):
                    landed = outs[t].at[_logical(*chip, c)]
                    remote(t, 1 + j, ins[t], landed, sibling).wait_recv()
                    remote(t, 4 + j, landed, landed, sibling).start()
            for t in range(self.n):
                if t < self.ng:
                    remote(t, 0, ins[t], outs[t].at[_logical(*sibling)], sibling).wait_recv()
                    for j, chip in enumerate(chips):
                        remote(t, 4 + j, ins[t], outs[t].at[_logical(*chip, 1 - c)], sibling).wait_recv()
                    for k in range(7):
                        remote(t, k, ins[t], outs[t].at[me], sibling).wait_send()
                else:
                    for k, peer in enumerate(peers):
                        remote(t, k, ins[t].at[me], outs[t].at[_logical(*peer)], peer).wait_recv()
                    for k, peer in enumerate(peers):
                        remote(t, k, ins[t].at[_logical(*peer)], outs[t].at[me], peer).wait_send()
                local(t).wait()


_MM_VMEM_BUDGET = 36 * 1024 * 1024


def _mm(a, b, *, ta=False, tb=False, out_dtype=F32, name, tiles=(1408, 1408, 4096), gather=(), scatter=(), gate=None):
    M, K = (a.shape[1], a.shape[0]) if ta else a.shape
    N = b.shape[0] if tb else b.shape[1]
    assert (b.shape[1] if tb else b.shape[0]) == K
    gate_mode = gate[0] if gate else None
    tm, tn, tk = _tile(M, tiles[0]), _tile(N, tiles[1]), _tile(K, tiles[2])
    if gate_mode == "fwd":
        bs = _gate_block(N // 2)
        tn = 2 * bs
    elif gate_mode == "bwd":
        bs = _gate_block(N)
        tn = bs

    def vmem_bytes():
        out = 2 * tm * tn * jnp.dtype(out_dtype).itemsize + (4 * tm * tn if tk < K else 0)
        if gate_mode:
            out += 2 * 3 * tm * tn * 2
        return 2 * 2 * (tm * tk + tk * tn) + out

    while vmem_bytes() > _MM_VMEM_BUDGET:
        if tk > 512:
            tk = _tile(K, tk - LANES)
        elif tn > 256 and not gate_mode:
            tn = _tile(N, tn - LANES)
        else:
            tm = _tile(M, tm - LANES)
    ni, nj, nk = M // tm, N // tn, K // tk
    ex = _Exchange(gather, scatter)

    a_spec = pl.BlockSpec((tk, tm), lambda i, j, k: (k, i)) if ta else pl.BlockSpec((tm, tk), lambda i, j, k: (i, k))
    b_spec = pl.BlockSpec((tn, tk), lambda i, j, k: (j, k)) if tb else pl.BlockSpec((tk, tn), lambda i, j, k: (k, j))
    dims = (((0 if ta else 1,), (1 if tb else 0,)), ((), ()))
    n_acc = 1 if nk > 1 else 0
    n_in = 3 if gate_mode == "bwd" else 2
    n_out = 2 if gate_mode == "fwd" else 1

    def finish(val, in_refs, out_refs):
        if gate_mode is None:
            out_refs[0][...] = val.astype(out_refs[0].dtype)
        elif gate_mode == "fwd":
            out_refs[0][...] = val.astype(BF16)
            for p in range(tn // (2 * bs)):
                a_, b_ = val[:, 2 * p * bs:(2 * p + 1) * bs], val[:, (2 * p + 1) * bs:(2 * p + 2) * bs]
                out_refs[1][:, p * bs:(p + 1) * bs] = _gate_value(a_, b_, gate[1]).astype(BF16)
        else:
            z_ref = in_refs[2]
            for p in range(tn // bs):
                a_ = z_ref[:, 2 * p * bs:(2 * p + 1) * bs].astype(F32)
                b_ = z_ref[:, (2 * p + 1) * bs:(2 * p + 2) * bs].astype(F32)
                da, db = _gate_grads(a_, b_, val[:, p * bs:(p + 1) * bs], gate[1])
                out_refs[0][:, 2 * p * bs:(2 * p + 1) * bs] = da.astype(BF16)
                out_refs[0][:, (2 * p + 1) * bs:(2 * p + 2) * bs] = db.astype(BF16)

    def body(*refs):
        in_refs, rest = refs[:n_in], refs[n_in:]
        ex_in, out_refs = rest[:ex.n], rest[ex.n:ex.n + n_out]
        ex_out, scratch = rest[ex.n + n_out:2 * ex.n + n_out], rest[2 * ex.n + n_out:]
        i, j, k = pl.program_id(0), pl.program_id(1), pl.program_id(2)
        ex.run(ex_in, ex_out, scratch[n_acc:], (i == 0) & (j == 0) & (k == 0), (i == ni - 1) & (j == nj - 1) & (k == nk - 1))
        part = lax.dot_general(in_refs[0][...].astype(BF16), in_refs[1][...].astype(BF16), dims, preferred_element_type=F32)
        if nk == 1:
            finish(part, in_refs, out_refs)
            return
        acc_ref = scratch[0]

        @pl.when(k == 0)
        def _():
            acc_ref[...] = part

        @pl.when(k > 0)
        def _():
            acc_ref[...] += part

        @pl.when(k == nk - 1)
        def _():
            finish(acc_ref[...], in_refs, out_refs)

    tile = lambda width: pl.BlockSpec((tm, width), lambda i, j, k: (i, j))
    if gate_mode == "fwd":
        out_specs, out_shape = [tile(tn), tile(tn // 2)], [jax.ShapeDtypeStruct((M, N), BF16), jax.ShapeDtypeStruct((M, N // 2), BF16)]
    elif gate_mode == "bwd":
        out_specs, out_shape = [tile(2 * tn)], [jax.ShapeDtypeStruct((M, 2 * N), BF16)]
    else:
        out_specs, out_shape = [tile(tn)], [jax.ShapeDtypeStruct((M, N), out_dtype)]
    operands = [a, b] + ([gate[2]] if gate_mode == "bwd" else [])
    sem = ("arbitrary",) * 3 if ex.n else ("parallel", "parallel", "arbitrary")
    res = pl.pallas_call(
        body, name=name, grid=(ni, nj, nk),
        in_specs=[a_spec, b_spec] + ([tile(2 * tn)] if gate_mode == "bwd" else []) + ex.in_specs(),
        out_specs=out_specs + ex.in_specs(),
        out_shape=out_shape + ex.out_shapes(),
        scratch_shapes=([pltpu.VMEM((tm, tn), F32)] if nk > 1 else []) + ex.scratch(),
        compiler_params=_cparams(*sem),
    )(*operands, *ex.operands())
    main = res[0] if n_out == 1 else tuple(res[:n_out])
    if not ex.n:
        return main
    return (main,) + tuple(ex.split(res[n_out:]))


def _rms_fwd(x, w, res=None, *, name, out_dtype=F32):
    T, C = x.shape
    tm = _tile(T, 512)
    has_res = res is not None

    def body(*refs):
        x_ref, w_ref = refs[0], refs[1]
        o_ref = refs[-1]
        xv = x_ref[...].astype(F32)
        r = lax.rsqrt(jnp.mean(xv * xv, axis=-1, keepdims=True) + NORM_EPS)
        y = xv * r * w_ref[...]
        if has_res:
            y = y + refs[2][...]
        o_ref[...] = y.astype(o_ref.dtype)

    row = pl.BlockSpec((tm, C), lambda i: (i, 0))
    vec = pl.BlockSpec((1, C), lambda i: (0, 0))
    ins = [x, w] + ([res] if has_res else [])
    return pl.pallas_call(
        body, name=name, grid=(T // tm,), in_specs=[row, vec] + ([row] if has_res else []), out_specs=row,
        out_shape=jax.ShapeDtypeStruct((T, C), out_dtype), compiler_params=_cparams("parallel"),
    )(*ins)


def _rms_bwd(x, w, dy, add=None, *, name, out_dtype=F32):
    T, C = x.shape
    tm = _tile(T, 512)
    has_add = add is not None

    def body(*refs):
        x_ref, w_ref, dy_ref = refs[:3]
        dx_ref, dw_ref = refs[-2], refs[-1]
        xv = x_ref[...].astype(F32)
        r = lax.rsqrt(jnp.mean(xv * xv, axis=-1, keepdims=True) + NORM_EPS)
        xh = xv * r
        g = dy_ref[...].astype(F32)
        part = jnp.sum(g * xh, axis=0, keepdims=True)

        @pl.when(pl.program_id(0) == 0)
        def _():
            dw_ref[...] = part

        @pl.when(pl.program_id(0) > 0)
        def _():
            dw_ref[...] += part

        gx = g * w_ref[...]
        dx = r * (gx - xh * jnp.mean(gx * xh, axis=-1, keepdims=True))
        if has_add:
            dx = dx + refs[3][...]
        dx_ref[...] = dx.astype(dx_ref.dtype)

    row = pl.BlockSpec((tm, C), lambda i: (i, 0))
    vec = pl.BlockSpec((1, C), lambda i: (0, 0))
    ins = [x, w, dy] + ([add] if has_add else [])
    return pl.pallas_call(
        body, name=name, grid=(T // tm,), in_specs=[row, vec, row] + ([row] if has_add else []),
        out_specs=[row, vec],
        out_shape=[jax.ShapeDtypeStruct((T, C), out_dtype), jax.ShapeDtypeStruct((1, C), F32)],
        compiler_params=_cparams("arbitrary"),
    )(*ins)


def _gate_block(width):
    return _tile(width, 256)


def _gate_value(a, b, kind):
    return a * _sigmoid(a) * b if kind == "swiglu" else a * _sigmoid(b)


def _gate_grads(a, b, d, kind):
    if kind == "swiglu":
        s = _sigmoid(a)
        return d * b * (s * (1.0 + a * (1.0 - s))), d * a * s
    s = _sigmoid(b)
    return d * s, d * a * s * (1.0 - s)


def _interleave_rows(w, inverse=False):
    W2, C = w.shape
    bs = _gate_block(W2 // 2)
    nb = W2 // 2 // bs
    shape = (nb, 2, bs, C) if inverse else (2, nb, bs, C)
    return w.reshape(shape).transpose(1, 0, 2, 3).reshape(W2, C)


def _gated_bwd(z, dout, kind, *, name):
    T, W2 = z.shape
    W = W2 // 2
    tm, bs = _tile(T, 512), _gate_block(W)

    def body(z_ref, d_ref, o_ref):
        da, db = _gate_grads(z_ref[:, :bs].astype(F32), z_ref[:, bs:].astype(F32), d_ref[...].astype(F32), kind)
        o_ref[:, :bs] = da.astype(o_ref.dtype)
        o_ref[:, bs:] = db.astype(o_ref.dtype)

    return pl.pallas_call(
        body, name=name, grid=(T // tm, W // bs),
        in_specs=[pl.BlockSpec((tm, 2 * bs), lambda i, j: (i, j)), pl.BlockSpec((tm, bs), lambda i, j: (i, j))],
        out_specs=pl.BlockSpec((tm, 2 * bs), lambda i, j: (i, j)),
        out_shape=jax.ShapeDtypeStruct((T, W2), BF16), compiler_params=_cparams("parallel", "parallel"),
    )(z, dout)


_NT = (((1,), (1,)), ((), ()))
_TN = (((0,), (0,)), ((), ()))


def _dot(a, b, dims=None, precision=None):
    if dims is None:
        return jnp.dot(a, b, preferred_element_type=F32, precision=precision)
    return lax.dot_general(a, b, dims, preferred_element_type=F32, precision=precision)


def _xattn_fwd(q, kv, B, L, *, name):
    T = q.shape[0]
    tq = 256
    nq = L // tq
    scale = X_HD ** -0.5

    def body(q_ref, k_ref, v_ref, o_ref):
        for h in range(X_HEADS):
            sl = slice(h * X_HD, (h + 1) * X_HD)
            qh, kh, vh = q_ref[:, sl].astype(BF16), k_ref[:, sl].astype(BF16), v_ref[:, sl].astype(BF16)
            s = _dot(qh, kh, _NT) * scale
            m = jnp.max(s, axis=-1, keepdims=True)
            p = jnp.exp(s - m)
            l = jnp.sum(p, axis=-1, keepdims=True)
            o_ref[:, sl] = (_dot(p.astype(BF16), vh) / l).astype(BF16)

    return pl.pallas_call(
        body, name=name, grid=(B, nq),
        in_specs=[pl.BlockSpec((tq, D_MODEL), lambda b, i: (b * nq + i, 0)),
                  pl.BlockSpec((MEM_LEN, D_MODEL), lambda b, i: (b, 0)),
                  pl.BlockSpec((MEM_LEN, D_MODEL), lambda b, i: (b, 1))],
        out_specs=pl.BlockSpec((tq, D_MODEL), lambda b, i: (b * nq + i, 0)),
        out_shape=jax.ShapeDtypeStruct((T, D_MODEL), BF16), compiler_params=_cparams("parallel", "parallel"),
    )(q, kv, kv)


def _xattn_bwd(q, kv, do, B, L, *, name):
    T = q.shape[0]
    tq = 256
    nq = L // tq
    scale = X_HD ** -0.5

    def body(q_ref, k_ref, v_ref, do_ref, dq_ref, dkv_ref):
        @pl.when(pl.program_id(1) == 0)
        def _():
            dkv_ref[...] = jnp.zeros_like(dkv_ref)

        for h in range(X_HEADS):
            sl = slice(h * X_HD, (h + 1) * X_HD)
            slv = slice(D_MODEL + h * X_HD, D_MODEL + (h + 1) * X_HD)
            qh, kh, vh = q_ref[:, sl].astype(BF16), k_ref[:, sl].astype(BF16), v_ref[:, sl].astype(BF16)
            doh = do_ref[:, sl].astype(BF16)
            s = _dot(qh, kh, _NT) * scale
            m = jnp.max(s, axis=-1, keepdims=True)
            e = jnp.exp(s - m)
            p = e / jnp.sum(e, axis=-1, keepdims=True)
            dkv_ref[:, slv] += _dot(p.astype(BF16), doh, _TN)
            dp = _dot(doh, vh, _NT)
            ds = p * (dp - jnp.sum(dp * p, axis=-1, keepdims=True)) * scale
            dsb = ds.astype(BF16)
            dq_ref[:, sl] = _dot(dsb, kh).astype(BF16)
            dkv_ref[:, sl] += _dot(dsb, qh, _TN)

    return pl.pallas_call(
        body, name=name, grid=(B, nq),
        in_specs=[pl.BlockSpec((tq, D_MODEL), lambda b, i: (b * nq + i, 0)),
                  pl.BlockSpec((MEM_LEN, D_MODEL), lambda b, i: (b, 0)),
                  pl.BlockSpec((MEM_LEN, D_MODEL), lambda b, i: (b, 1)),
                  pl.BlockSpec((tq, D_MODEL), lambda b, i: (b * nq + i, 0))],
        out_specs=[pl.BlockSpec((tq, D_MODEL), lambda b, i: (b * nq + i, 0)),
                   pl.BlockSpec((MEM_LEN, 2 * D_MODEL), lambda b, i: (b, 0))],
        out_shape=[jax.ShapeDtypeStruct((T, D_MODEL), BF16), jax.ShapeDtypeStruct((B * MEM_LEN, 2 * D_MODEL), F32)],
        compiler_params=_cparams("parallel", "arbitrary"),
    )(q, kv, kv, do)


def _chunk_masks():
    row = lax.broadcasted_iota(jnp.int32, (A_SUPER, A_SUPER), 0)
    col = lax.broadcasted_iota(jnp.int32, (A_SUPER, A_SUPER), 1)
    same = jnp.right_shift(row, 5) == jnp.right_shift(col, 5)
    return same, same & (col <= row), same & (col >= row)


def _hgrn_gates(fa, lb):
    sig = _sigmoid(fa)
    f = lb + (1.0 - lb) * sig
    return sig, f, jnp.log(f), 1.0 - f


def _hgrn_fwd(z, lb, onw, B, L, *, name):
    T = B * L
    ns = L // A_SUPER
    nch = A_SUPER // A_CHUNK

    def body(q_ref, f_ref, v_ref, g_ref, lb_ref, w_ref, oa_ref, o_ref, s_ref, st_ref, sc_ref):
        @pl.when(pl.program_id(2) == 0)
        def _():
            st_ref[...] = jnp.zeros_like(st_ref)

        s_ref[0] = st_ref[...]
        same, tril, _ = _chunk_masks()
        q, v = q_ref[...], v_ref[...]
        _, _, lf, k = _hgrn_gates(f_ref[...], lb_ref[...])
        bcs = _dot(tril.astype(F32), lf, precision=HI)
        bl = _dot(same.astype(F32), lf, precision=HI)
        qd = (q * jnp.exp(bcs)).astype(BF16)
        ki = (k * jnp.exp(-bcs)).astype(BF16)
        ke = (k * jnp.exp(bl - bcs)).astype(BF16)
        dec = jnp.exp(bl)
        vb = v.astype(BF16)
        a = jnp.where(tril, _dot(qd, ki, _NT), 0.0)
        o_ref[...] = _dot(a.astype(BF16), vb)
        chunks = [slice(c * A_CHUNK, (c + 1) * A_CHUNK) for c in range(nch)]
        outer = [_dot(vb[rs], ke[rs], _TN) for rs in chunks]
        st = st_ref[...]
        for c, rs in enumerate(chunks):
            sc_ref[c] = st.astype(BF16)
            st = st * dec[c * A_CHUNK:c * A_CHUNK + 1, :] + outer[c]
        st_ref[...] = st
        for c, rs in enumerate(chunks):
            o_ref[rs, :] += _dot(qd[rs], sc_ref[c], _NT)
        o = o_ref[...]
        r = lax.rsqrt(jnp.mean(o * o, axis=-1, keepdims=True) + NORM_EPS)
        g = g_ref[...]
        oa_ref[...] = o * r * w_ref[...] * (g * _sigmoid(g))

    def zspec(off):
        return pl.BlockSpec((A_SUPER, A_HEAD), lambda b, h, n: (b * ns + n, off + h))

    hvec = pl.BlockSpec((1, A_HEAD), lambda b, h, n: (0, h))
    ospec = pl.BlockSpec((A_SUPER, A_HEAD), lambda b, h, n: (b * ns + n, h))
    return pl.pallas_call(
        body, name=name, grid=(B, 4, ns),
        in_specs=[zspec(0), zspec(4), zspec(8), zspec(12), hvec, hvec],
        out_specs=[ospec, ospec, pl.BlockSpec((1, A_HEAD, A_HEAD), lambda b, h, n: ((b * 4 + h) * ns + n, 0, 0))],
        out_shape=[jax.ShapeDtypeStruct((T, A_WIDTH), F32), jax.ShapeDtypeStruct((T, A_WIDTH), F32),
                   jax.ShapeDtypeStruct((B * 4 * ns, A_HEAD, A_HEAD), F32)],
        scratch_shapes=[pltpu.VMEM((A_HEAD, A_HEAD), F32), pltpu.VMEM((nch, A_HEAD, A_HEAD), BF16)],
        compiler_params=_cparams("parallel", "parallel", "arbitrary"),
    )(z, z, z, z, lb, onw)


def _hgrn_bwd(z, lb, onw, o_raw, s_start, doa, B, L, *, name):
    T = B * L
    ns = L // A_SUPER
    nch = A_SUPER // A_CHUNK

    def body(q_ref, f_ref, v_ref, g_ref, lb_ref, w_ref, o_ref, s_ref, doa_ref,
             dq_ref, df_ref, dv_ref, dg_ref, dw_ref, dlb_ref, dst_ref, sc_ref, dsc_ref, dqd_ref, dke_ref, dblx_ref):
        @pl.when(pl.program_id(2) == 0)
        def _():
            dst_ref[...] = jnp.zeros_like(dst_ref)
            dw_ref[...] = jnp.zeros_like(dw_ref)
            dlb_ref[...] = jnp.zeros_like(dlb_ref)

        same, tril, triu = _chunk_masks()
        q, v, g, lb, w = q_ref[...], v_ref[...], g_ref[...], lb_ref[...], w_ref[...]
        sig, f, lf, k = _hgrn_gates(f_ref[...], lb)
        bcs = _dot(tril.astype(F32), lf, precision=HI)
        bl = _dot(same.astype(F32), lf, precision=HI)
        eb, enb, eeb = jnp.exp(bcs), jnp.exp(-bcs), jnp.exp(bl - bcs)
        qd, ki, ke = q * eb, k * enb, k * eeb
        qdb, kib, keb, vb = qd.astype(BF16), ki.astype(BF16), ke.astype(BF16), v.astype(BF16)
        dec = jnp.exp(bl)
        o = o_ref[...]
        r = lax.rsqrt(jnp.mean(o * o, axis=-1, keepdims=True) + NORM_EPS)
        on = o * r
        sg = _sigmoid(g)
        silu_g = g * sg
        doa = doa_ref[...]
        dg_ref[...] = doa * on * w * (sg * (1.0 + g * (1.0 - sg)))
        dw_ref[0] += jnp.sum(doa * on * silu_g, axis=0, keepdims=True)
        don = doa * w * silu_g
        do = r * (don - on * jnp.mean(don * on, axis=-1, keepdims=True))
        dob = do.astype(BF16)
        a = jnp.where(tril, _dot(qdb, kib, _NT), 0.0).astype(BF16)
        da = jnp.where(tril, _dot(dob, vb, _NT), 0.0).astype(BF16)
        dv_ref[...] = _dot(a, dob, _TN)
        dqd_ref[...] = _dot(da, kib)
        dki = _dot(da, qdb, _TN)
        chunks = [slice(c * A_CHUNK, (c + 1) * A_CHUNK) for c in range(nch)]
        outer = [_dot(vb[rs], keb[rs], _TN) for rs in chunks]
        st = s_ref[0]
        for c in range(nch):
            sc_ref[c] = st
            st = st * dec[c * A_CHUNK:c * A_CHUNK + 1, :] + outer[c]
        outer_g = [_dot(dob[rs], qdb[rs], _TN) for rs in chunks]
        dst = dst_ref[...]
        for c in reversed(range(nch)):
            dsc_ref[c] = dst
            dst = dst * dec[c * A_CHUNK:c * A_CHUNK + 1, :] + outer_g[c]
        dst_ref[...] = dst
        for c, rs in enumerate(chunks):
            dec_c = dec[c * A_CHUNK:c * A_CHUNK + 1, :]
            dsc, stc = dsc_ref[c], sc_ref[c]
            dscb = dsc.astype(BF16)
            dv_ref[rs, :] += _dot(keb[rs], dscb, _NT)
            dke_ref[rs, :] = _dot(vb[rs], dscb)
            ddec = jnp.sum(dsc * stc, axis=0, keepdims=True)
            dqd_ref[rs, :] += _dot(dob[rs], stc.astype(BF16))
            dblx_ref[rs, :] = jnp.broadcast_to(ddec * dec_c, (A_CHUNK, A_HEAD))
        dqd, dke = dqd_ref[...], dke_ref[...]
        dq_ref[...] = dqd * eb
        keke = dke * ke
        db = dqd * qd - dki * ki - keke
        dbl = _dot(same.astype(F32), keke, precision=HI) + dblx_ref[...]
        dk = dki * enb + dke * eeb
        dlf = _dot(triu.astype(F32), db, precision=HI) + dbl
        dff = dlf / f - dk
        df_ref[...] = dff * (1.0 - lb) * sig * (1.0 - sig)
        dlb_ref[0] += jnp.sum(dff * (1.0 - sig), axis=0, keepdims=True)

    def rev(n):
        return ns - 1 - n

    def zspec(off):
        return pl.BlockSpec((A_SUPER, A_HEAD), lambda b, h, n: (b * ns + rev(n), off + h))

    hvec = pl.BlockSpec((1, A_HEAD), lambda b, h, n: (0, h))
    ospec = pl.BlockSpec((A_SUPER, A_HEAD), lambda b, h, n: (b * ns + rev(n), h))
    acc = pl.BlockSpec((1, 1, A_HEAD), lambda b, h, n: (b * 4 + h, 0, 0))
    big = jax.ShapeDtypeStruct((T, A_WIDTH), F32)
    small = jax.ShapeDtypeStruct((B * 4, 1, A_HEAD), F32)
    return pl.pallas_call(
        body, name=name, grid=(B, 4, ns),
        in_specs=[zspec(0), zspec(4), zspec(8), zspec(12), hvec, hvec, ospec,
                  pl.BlockSpec((1, A_HEAD, A_HEAD), lambda b, h, n: ((b * 4 + h) * ns + rev(n), 0, 0)), ospec],
        out_specs=[ospec, ospec, ospec, ospec, acc, acc],
        out_shape=[big, big, big, big, small, small],
        scratch_shapes=[pltpu.VMEM((A_HEAD, A_HEAD), F32), pltpu.VMEM((nch, A_HEAD, A_HEAD), F32),
                        pltpu.VMEM((nch, A_HEAD, A_HEAD), F32),
                        pltpu.VMEM((A_SUPER, A_HEAD), F32), pltpu.VMEM((A_SUPER, A_HEAD), F32),
                        pltpu.VMEM((A_SUPER, A_HEAD), F32)],
        compiler_params=_cparams("parallel", "parallel", "arbitrary"),
    )(z, z, z, z, lb, onw, o_raw, s_start, doa)


def _rope_tables(L):
    half = A_HEAD // 2
    inv_freq = ROPE_THETA ** (-jnp.arange(half, dtype=F32) / half)
    ang = jnp.arange(L, dtype=F32)[:, None] * inv_freq[None, :]
    cos, sin = jnp.cos(ang), jnp.sin(ang)
    return jnp.concatenate([cos, cos], axis=-1), jnp.concatenate([-sin, sin], axis=-1)


def _rope_fwd(z, cos2, sin2, col_off, B, L, *, name):
    T = B * L
    tm = 256
    nl = L // tm

    def body(x_ref, c_ref, s_ref, o_ref):
        x = x_ref[...]
        o_ref[...] = x * c_ref[...] + pltpu.roll(x, A_HEAD // 2, 1) * s_ref[...]

    tab = pl.BlockSpec((tm, A_HEAD), lambda i, h: (i % nl, 0))
    return pl.pallas_call(
        body, name=name, grid=(T // tm, 4),
        in_specs=[pl.BlockSpec((tm, A_HEAD), lambda i, h: (i, col_off + h)), tab, tab],
        out_specs=pl.BlockSpec((tm, A_HEAD), lambda i, h: (i, h)),
        out_shape=jax.ShapeDtypeStruct((T, 512), F32), compiler_params=_cparams("parallel", "parallel"),
    )(z, cos2, sin2)


def _sum3(d1, d2, d3, cos2, sin2, rotate, B, L, *, name):
    T = B * L
    tm = 256
    nl = L // tm

    def body(a_ref, b_ref, c_ref, cs_ref, sn_ref, o_ref):
        d = a_ref[...] + b_ref[...] + c_ref[...]
        if rotate:
            d = d * cs_ref[...] - pltpu.roll(d, A_HEAD // 2, 1) * sn_ref[...]
        o_ref[...] = d

    blk = pl.BlockSpec((tm, A_HEAD), lambda i, h: (i, h))
    tab = pl.BlockSpec((tm, A_HEAD), lambda i, h: (i % nl, 0))
    return pl.pallas_call(
        body, name=name, grid=(T // tm, 4), in_specs=[blk, blk, blk, tab, tab], out_specs=blk,
        out_shape=jax.ShapeDtypeStruct((T, 512), F32), compiler_params=_cparams("parallel", "parallel"),
    )(d1, d2, d3, cos2, sin2)


def _band_masks():
    i = lax.broadcasted_iota(jnp.int32, (B_SPAN, B_SPAN), 0)
    j = lax.broadcasted_iota(jnp.int32, (B_SPAN, B_SPAN), 1)
    return i <= j, j <= i


def _dil_geom(dil, L):
    return B_SPAN * dil, L // (B_SPAN * dil), (1 if dil > 1 else 4)


def _dil_rows(r, dil):
    return pl.ds(r, B_SPAN, stride=dil) if dil > 1 else pl.ds(0, B_SPAN)


def _dil_loop(dil, step):
    for r in range(dil):
        step(r, 0)


def _dil_specs(dil, B, L):
    W, nb, hb = _dil_geom(dil, L)
    cw = A_HEAD * hb

    def at(col0, shift):
        def index(b, n, hh):
            return (b * nb + jnp.clip(n + shift, 0, nb - 1), col0 // cw + hh)
        return pl.BlockSpec((W, cw), index)

    return at


def _dil_fwd(qr, kr, z, dil, B, L, *, name):
    T = B * L
    W, nb, hb = _dil_geom(dil, L)
    has_prev = nb > 1
    scale = A_HEAD ** -0.5
    at = _dil_specs(dil, B, L)

    def body(*refs):
        if has_prev:
            q_ref, kc_ref, vc_ref, kp_ref, vp_ref, o_ref, l_ref = refs
        else:
            q_ref, kc_ref, vc_ref, o_ref, l_ref = refs
        mp, mc = _band_masks()
        mp = mp & (pl.program_id(1) > 0)

        def step(r, carry):
            rows = _dil_rows(r, dil)
            for h in range(hb):
                cols = pl.ds(h * A_HEAD, A_HEAD)
                qh = q_ref[rows, cols].astype(BF16)
                sc = jnp.where(mc, _dot(qh, kc_ref[rows, cols].astype(BF16), _NT) * scale, NEG_BIG)
                m = jnp.max(sc, axis=-1, keepdims=True)
                if has_prev:
                    sp = jnp.where(mp, _dot(qh, kp_ref[rows, cols].astype(BF16), _NT) * scale, NEG_BIG)
                    m = jnp.maximum(m, jnp.max(sp, axis=-1, keepdims=True))
                pc = jnp.exp(sc - m)
                l = jnp.sum(pc, axis=-1, keepdims=True)
                o = _dot(pc.astype(BF16), vc_ref[rows, cols].astype(BF16))
                if has_prev:
                    pp = jnp.exp(sp - m)
                    l = l + jnp.sum(pp, axis=-1, keepdims=True)
                    o = o + _dot(pp.astype(BF16), vp_ref[rows, cols].astype(BF16))
                o_ref[rows, cols] = o / l
                l_ref[rows, cols] = jnp.broadcast_to(m + jnp.log(l), (B_SPAN, A_HEAD))
            return carry

        _dil_loop(dil, step)

    ins = [qr, kr, z] + ([kr, z] if has_prev else [])
    in_specs = [at(0, 0), at(0, 0), at(3072, 0)] + ([at(0, -1), at(3072, -1)] if has_prev else [])
    return pl.pallas_call(
        body, name=name, grid=(B, nb, 4 // hb), in_specs=in_specs, out_specs=[at(0, 0), at(0, 0)],
        out_shape=[jax.ShapeDtypeStruct((T, 512), F32)] * 2,
        compiler_params=_cparams("parallel", "parallel", "parallel"),
    )(*ins)


def _dil_combine(os_, ls_, *, name):
    T = os_[0].shape[0]
    tm = 256

    def body(o1, o2, o3, l1, l2, l3, ob_ref, lse_ref):
        a1, a2, a3 = l1[...], l2[...], l3[...]
        m = jnp.maximum(jnp.maximum(a1, a2), a3)
        e1, e2, e3 = jnp.exp(a1 - m), jnp.exp(a2 - m), jnp.exp(a3 - m)
        den = e1 + e2 + e3
        ob_ref[...] = (e1 * o1[...] + e2 * o2[...] + e3 * o3[...]) / den
        lse_ref[...] = m + jnp.log(den)

    blk = pl.BlockSpec((tm, 512), lambda i: (i, 0))
    return pl.pallas_call(
        body, name=name, grid=(T // tm,), in_specs=[blk] * 6, out_specs=[blk, blk],
        out_shape=[jax.ShapeDtypeStruct((T, 512), F32)] * 2, compiler_params=_cparams("parallel"),
    )(*[o.reshape(T, 512) for o in os_], *[l.reshape(T, 512) for l in ls_])


def _dil_bwd_q(qr, kr, z, dymix, out, lse, dil, B, L, *, name):
    T = B * L
    W, nb, hb = _dil_geom(dil, L)
    has_prev = nb > 1
    scale = A_HEAD ** -0.5
    at = _dil_specs(dil, B, L)

    def body(*refs):
        if has_prev:
            q_ref, kc_ref, vc_ref, do_ref, out_ref, lse_ref, kp_ref, vp_ref, dq_ref = refs
        else:
            q_ref, kc_ref, vc_ref, do_ref, out_ref, lse_ref, dq_ref = refs
        mp, mc = _band_masks()
        mp = mp & (pl.program_id(1) > 0)

        def step(r, carry):
            rows = _dil_rows(r, dil)
            for h in range(hb):
                cols = pl.ds(h * A_HEAD, A_HEAD)
                qh = q_ref[rows, cols].astype(BF16)
                do = do_ref[rows, cols]
                delta = jnp.sum(do * out_ref[rows, cols], axis=-1, keepdims=True)
                dob = do.astype(BF16)
                lse_h = lse_ref[rows, cols]
                kc = kc_ref[rows, cols].astype(BF16)
                pc = jnp.where(mc, jnp.exp(_dot(qh, kc, _NT) * scale - lse_h), 0.0)
                dsc = pc * (_dot(dob, vc_ref[rows, cols].astype(BF16), _NT) - delta) * scale
                dq = _dot(dsc.astype(BF16), kc)
                if has_prev:
                    kp = kp_ref[rows, cols].astype(BF16)
                    pp = jnp.where(mp, jnp.exp(_dot(qh, kp, _NT) * scale - lse_h), 0.0)
                    dsp = pp * (_dot(dob, vp_ref[rows, cols].astype(BF16), _NT) - delta) * scale
                    dq = dq + _dot(dsp.astype(BF16), kp)
                dq_ref[rows, cols] = dq
            return carry

        _dil_loop(dil, step)

    ins = [qr, kr, z, dymix, out, lse] + ([kr, z] if has_prev else [])
    in_specs = ([at(0, 0), at(0, 0), at(3072, 0), at(512, 0), at(0, 0), at(0, 0)]
                + ([at(0, -1), at(3072, -1)] if has_prev else []))
    return pl.pallas_call(
        body, name=name, grid=(B, nb, 4 // hb), in_specs=in_specs, out_specs=at(0, 0),
        out_shape=jax.ShapeDtypeStruct((T, 512), F32),
        compiler_params=_cparams("parallel", "parallel", "parallel"),
    )(*ins)


def _dil_bwd_kv(qr, kr, z, dymix, out, lse, dil, B, L, *, name):
    T = B * L
    W, nb, hb = _dil_geom(dil, L)
    has_next = nb > 1
    scale = A_HEAD ** -0.5
    at = _dil_specs(dil, B, L)

    def body(*refs):
        k_ref, v_ref = refs[0], refs[1]
        own = refs[2:6]
        nxt = refs[6:10] if has_next else None
        dk_ref, dv_ref = refs[-2], refs[-1]
        mp, mc = _band_masks()
        mp = mp & (pl.program_id(1) < nb - 1)
        groups = [(own, mc)] + ([(nxt, mp)] if has_next else [])

        def step(r, carry):
            rows = _dil_rows(r, dil)
            for h in range(hb):
                cols = pl.ds(h * A_HEAD, A_HEAD)
                kh, vh = k_ref[rows, cols].astype(BF16), v_ref[rows, cols].astype(BF16)
                dk = jnp.zeros((B_SPAN, A_HEAD), F32)
                dv = jnp.zeros((B_SPAN, A_HEAD), F32)
                for (q_ref, do_ref, out_ref, lse_ref), mask in groups:
                    qh = q_ref[rows, cols].astype(BF16)
                    do = do_ref[rows, cols]
                    delta = jnp.sum(do * out_ref[rows, cols], axis=-1, keepdims=True)
                    dob = do.astype(BF16)
                    p = jnp.where(mask, jnp.exp(_dot(qh, kh, _NT) * scale - lse_ref[rows, cols]), 0.0)
                    dv = dv + _dot(p.astype(BF16), dob, _TN)
                    ds = p * (_dot(dob, vh, _NT) - delta) * scale
                    dk = dk + _dot(ds.astype(BF16), qh, _TN)
                dk_ref[rows, cols] = dk
                dv_ref[rows, cols] = dv
            return carry

        _dil_loop(dil, step)

    ins = [kr, z, qr, dymix, out, lse] + ([qr, dymix, out, lse] if has_next else [])
    in_specs = ([at(0, 0), at(3072, 0), at(0, 0), at(512, 0), at(0, 0), at(0, 0)]
                + ([at(0, 1), at(512, 1), at(0, 1), at(0, 1)] if has_next else []))
    return pl.pallas_call(
        body, name=name, grid=(B, nb, 4 // hb), in_specs=in_specs, out_specs=[at(0, 0), at(0, 0)],
        out_shape=[jax.ShapeDtypeStruct((T, 512), F32)] * 2,
        compiler_params=_cparams("parallel", "parallel", "parallel"),
    )(*ins)


def _s5_build(lam_re, lam_im, log_dt, b_re, b_im, c_re, c_im):
    G, P, TC = C_GROUPS, C_STATE, C_TC
    lr = jnp.minimum(lam_re, C_MIN_NEG_RE)
    li = lam_im
    dt = jnp.exp(log_dt)[:, None]
    mag = jnp.exp(dt * lr)
    ar, ai = mag * jnp.cos(dt * li), mag * jnp.sin(dt * li)
    den = lr * lr + li * li
    zr = ((ar - 1.0) * lr + ai * li) / den
    zi = (ai * lr - (ar - 1.0) * li) / den
    bbr = zr[..., None] * b_re - zi[..., None] * b_im
    bbi = zr[..., None] * b_im + zi[..., None] * b_re
    ks = jnp.arange(TC + 1, dtype=F32)[:, None, None]
    pmag = jnp.exp(ks * (dt * lr)[None])
    pr, pi = pmag * jnp.cos(ks * (dt * li)[None]), pmag * jnp.sin(ks * (dt * li)[None])
    car = c_re[None] * pr[:, :, None, :] - c_im[None] * pi[:, :, None, :]
    cai = c_re[None] * pi[:, :, None, :] + c_im[None] * pr[:, :, None, :]
    kern = (jnp.einsum('lgop,gpc->lgco', car[:TC], bbr, precision=HI)
            - jnp.einsum('lgop,gpc->lgco', cai[:TC], bbi, precision=HI))
    eye = jnp.eye(C_GB, dtype=F32)
    kbd = (kern.reshape(TC, C_NB, C_GB, C_GROUP, C_GROUP)[:, :, :, :, None, :]
           * eye[None, None, :, None, :, None]).reshape(TC, C_NB, LANES, LANES)
    lag = jnp.arange(TC)[None, :] - jnp.arange(TC)[:, None]
    ksel = jnp.where((lag >= 0)[:, :, None, None, None], kbd[jnp.clip(lag, 0, TC - 1)], 0.0)
    m8 = ksel.transpose(2, 0, 3, 1, 4).reshape(C_NB, C_W8, C_W8)
    pr_e, pi_e = pr[TC - 1 - jnp.arange(TC)], pi[TC - 1 - jnp.arange(TC)]
    er = pr_e[:, :, :, None] * bbr[None] - pi_e[:, :, :, None] * bbi[None]
    ei = pr_e[:, :, :, None] * bbi[None] + pi_e[:, :, :, None] * bbr[None]
    ez = jnp.stack([er, ei], axis=2).reshape(TC, C_NB, C_GB, 2, P, C_GROUP).transpose(1, 0, 2, 5, 3, 4)
    e8 = (ez[:, :, :, :, :, None, :] * eye[None, None, :, None, None, :, None]).reshape(C_NB, C_W8, C_S8)
    fz = jnp.stack([car[1:], -cai[1:]], axis=0).reshape(2, TC, C_NB, C_GB, C_GROUP, P).transpose(2, 0, 3, 5, 1, 4)
    f8 = (fz[:, :, :, :, :, None, :] * eye[None, None, :, None, None, :, None]).reshape(C_NB, C_S8, C_W8)
    return m8, e8, f8, pr[TC], pi[TC]


C_NB = C_GROUPS * C_GROUP // LANES
C_GB = C_GROUPS // C_NB
C_W8 = C_TC * LANES
C_S8 = 2 * C_GB * C_STATE


def _s5_scan_tables(lam_re, lam_im, log_dt, nsteps):
    lr = jnp.minimum(lam_re, C_MIN_NEG_RE)
    dt = jnp.exp(log_dt)[:, None]
    ks = (C_TC * 2.0 ** jnp.arange(8, dtype=F32))[None, :, None]
    keep = (jnp.arange(8) < nsteps)[None, :, None]
    pmag = jnp.exp(ks * (dt * lr)[:, None, :])
    ang = ks * (dt * lam_im)[:, None, :]

    def blocks(t):
        return t.reshape(C_NB, C_GB, 8, C_STATE).transpose(0, 2, 1, 3).reshape(C_NB, 8, C_GB * C_STATE)

    pr = blocks(jnp.where(keep, pmag * jnp.cos(ang), 0.0))
    pi = blocks(jnp.where(keep, pmag * jnp.sin(ang), 0.0))
    return jnp.concatenate([pr, pr], axis=-1), jnp.concatenate([-pi, pi], axis=-1)


def _s5_rows(t, R):
    return pl.ds(t, R, stride=C_TC)


def _s5_fwd(u, dsk, m8, e8, f8, tab_r, tab_i, B, L, *, name):
    T = B * L
    R = L // C_TC
    nsteps = int(math.log2(R))

    def body(u_ref, d_ref, m_ref, e_ref, f_ref, tr_ref, ti_ref, gl_ref, y_ref, x8_ref, xs_ref):
        for t in range(C_TC):
            x8_ref[0, :, t * LANES:(t + 1) * LANES] = u_ref[_s5_rows(t, R), :].astype(BF16)
        x8 = x8_ref[0]
        x = _dot(x8, e_ref[0])
        row = lax.broadcasted_iota(jnp.int32, (R, C_S8), 0)
        for k in range(nsteps):
            s = 1 << k
            sh = pltpu.roll(x, s, 0)
            upd = tr_ref[0, k:k + 1, :] * sh + ti_ref[0, k:k + 1, :] * pltpu.roll(sh, C_S8 // 2, 1)
            x = x + jnp.where(row >= s, upd, 0.0)
        xs = jnp.where(row >= 1, pltpu.roll(x, 1, 0), 0.0)
        xs_ref[0] = xs
        y8 = _dot(x8, m_ref[0]) + _dot(xs.astype(BF16), f_ref[0])
        d = d_ref[...]
        for t in range(C_TC):
            rows = _s5_rows(t, R)
            y = y8[:, t * LANES:(t + 1) * LANES] + d * u_ref[rows, :]
            y_ref[rows, :] = y
            gl_ref[rows, :] = 0.5 * y * (1.0 + _erf(y * (2.0 ** -0.5)))

    tok = pl.BlockSpec((L, LANES), lambda c, b: (b, c))
    per_block = lambda shape: pl.BlockSpec((1,) + shape, lambda c, b: (c, 0, 0))
    per_step = lambda shape: pl.BlockSpec((1,) + shape, lambda c, b: (c * B + b, 0, 0))
    return pl.pallas_call(
        body, name=name, grid=(C_NB, B),
        in_specs=[tok, pl.BlockSpec((1, LANES), lambda c, b: (0, c)), per_block((C_W8, C_W8)), per_block((C_W8, C_S8)),
                  per_block((C_S8, C_W8)), per_block((8, C_S8)), per_block((8, C_S8))],
        out_specs=[tok, tok, per_step((R, C_W8)), per_step((R, C_S8))],
        out_shape=[jax.ShapeDtypeStruct((T, D_MODEL), F32), jax.ShapeDtypeStruct((T, D_MODEL), F32),
                   jax.ShapeDtypeStruct((C_NB * B, R, C_W8), BF16), jax.ShapeDtypeStruct((C_NB * B, R, C_S8), F32)],
        compiler_params=_cparams("parallel", "parallel"),
    )(u, dsk, m8, e8, f8, tab_r, tab_i)


def _s5_bwd(dgl, y, u, dsk, xs, m8, e8, f8, tab_r, tab_i, B, L, *, name):
    T = B * L
    R = L // C_TC
    nsteps = int(math.log2(R))

    def body(dgl_ref, y_ref, u_ref, d_ref, xs_ref, m_ref, e_ref, f_ref, tr_ref, ti_ref,
             du_ref, dy8_ref, de_ref, da_ref, dd_ref, dyf_ref):
        @pl.when(pl.program_id(1) == 0)
        def _():
            da_ref[...] = jnp.zeros_like(da_ref)
            dd_ref[...] = jnp.zeros_like(dd_ref)

        dd = jnp.zeros((1, LANES), F32)
        for t in range(C_TC):
            rows = _s5_rows(t, R)
            yv = y_ref[rows, :]
            cdf = 0.5 * (1.0 + _erf(yv * (2.0 ** -0.5)))
            pdf = jnp.exp(-0.5 * yv * yv) * (1.0 / math.sqrt(2.0 * math.pi))
            dy = dgl_ref[rows, :] * (cdf + yv * pdf)
            dd = dd + jnp.sum(dy * u_ref[rows, :], axis=0, keepdims=True)
            dyf_ref[:, t * LANES:(t + 1) * LANES] = dy
        dd_ref[...] += dd
        dy8 = dyf_ref[...].astype(BF16)
        dy8_ref[0] = dy8
        xs = xs_ref[0]
        gx = _dot(dy8, f_ref[0], _NT)
        row = lax.broadcasted_iota(jnp.int32, (R, C_S8), 0)
        for k in range(nsteps):
            s = 1 << k
            sh = pltpu.roll(gx, R - s, 0)
            upd = tr_ref[0, k:k + 1, :] * sh - ti_ref[0, k:k + 1, :] * pltpu.roll(sh, C_S8 // 2, 1)
            gx = gx + jnp.where(row + s < R, upd, 0.0)
        de_in = jnp.where(row + 1 < R, pltpu.roll(gx, R - 1, 0), 0.0)
        deb = de_in.astype(BF16)
        de_ref[0] = deb
        da_ref[0, 0:1, :] += jnp.sum(de_in * xs, axis=0, keepdims=True)
        da_ref[0, 1:2, :] += jnp.sum(de_in * pltpu.roll(xs, C_S8 // 2, 1), axis=0, keepdims=True)
        dx8 = _dot(dy8, m_ref[0], _NT) + _dot(deb, e_ref[0], _NT)
        d = d_ref[...]
        for t in range(C_TC):
            cols = slice(t * LANES, (t + 1) * LANES)
            du_ref[_s5_rows(t, R), :] = dx8[:, cols] + d * dyf_ref[:, cols]

    tok = pl.BlockSpec((L, LANES), lambda c, b: (b, c))
    vec = pl.BlockSpec((1, LANES), lambda c, b: (0, c))
    per_block = lambda shape: pl.BlockSpec((1,) + shape, lambda c, b: (c, 0, 0))
    per_step = lambda shape: pl.BlockSpec((1,) + shape, lambda c, b: (c * B + b, 0, 0))
    return pl.pallas_call(
        body, name=name, grid=(C_NB, B),
        in_specs=[tok, tok, tok, vec, per_step((R, C_S8)), per_block((C_W8, C_W8)),
                  per_block((C_W8, C_S8)), per_block((C_S8, C_W8)), per_block((8, C_S8)), per_block((8, C_S8))],
        out_specs=[tok, per_step((R, C_W8)), per_step((R, C_S8)), per_block((8, C_S8)), vec],
        out_shape=[jax.ShapeDtypeStruct((T, D_MODEL), F32), jax.ShapeDtypeStruct((C_NB * B, R, C_W8), BF16),
                   jax.ShapeDtypeStruct((C_NB * B, R, C_S8), BF16), jax.ShapeDtypeStruct((C_NB, 8, C_S8), F32),
                   jax.ShapeDtypeStruct((1, D_MODEL), F32)],
        scratch_shapes=[pltpu.VMEM((R, C_W8), F32)],
        compiler_params=_cparams("parallel", "arbitrary"),
    )(dgl, y, u, dsk, xs, m8, e8, f8, tab_r, tab_i)


def _bmm_tn(a, b, nb, *, name):
    a = a.reshape(nb, -1, a.shape[-1])
    b = b.reshape(nb, -1, b.shape[-1])
    K, M, N = a.shape[1], a.shape[2], b.shape[2]

    def body(a_ref, b_ref, o_ref):
        o_ref[0] = _dot(a_ref[0].astype(BF16), b_ref[0].astype(BF16), _TN)

    return pl.pallas_call(
        body, name=name, grid=(nb,),
        in_specs=[pl.BlockSpec((1, K, M), lambda c: (c, 0, 0)), pl.BlockSpec((1, K, N), lambda c: (c, 0, 0))],
        out_specs=pl.BlockSpec((1, M, N), lambda c: (c, 0, 0)),
        out_shape=jax.ShapeDtypeStruct((nb, M, N), F32), compiler_params=_cparams("parallel"),
    )(a, b)


def _loss_head(y, target, *, name):
    T, C = y.shape
    tm = 256

    def body(y_ref, t_ref, l_ref, d_ref):
        err = y_ref[...] - t_ref[...]
        d_ref[...] = err * (1.0 / C)
        sq = err * err
        part = jnp.zeros((8, LANES), F32)
        for r in range(0, tm, 8):
            for c in range(0, C, LANES):
                part = part + sq[r:r + 8, c:c + LANES]

        @pl.when(pl.program_id(0) == 0)
        def _():
            l_ref[...] = part

        @pl.when(pl.program_id(0) > 0)
        def _():
            l_ref[...] += part

    row = pl.BlockSpec((tm, C), lambda i: (i, 0))
    acc = pl.BlockSpec((8, LANES), lambda i: (0, 0))
    return pl.pallas_call(
        body, name=name, grid=(T // tm,), in_specs=[row, row], out_specs=[acc, row],
        out_shape=[jax.ShapeDtypeStruct((8, LANES), F32), jax.ShapeDtypeStruct((T, C), F32)],
        compiler_params=_cparams("arbitrary"),
    )(y, target)


def _adamw(w, g, m, v, *, name):
    shape = w.shape
    size = int(np.prod(shape))
    cols = LANES if (shape[-1] < LANES and size % LANES == 0) else shape[-1]
    rows = size // cols
    tm = _tile(rows, 256) if rows % 8 == 0 else rows
    w2, g2, m2, v2 = (t.reshape(rows, cols) for t in (w, g, m, v))

    def body(w_ref, g_ref, m_ref, v_ref, d_ref, nm_ref, nv_ref):
        gg = g_ref[...]
        nm = ADAM_B1 * m_ref[...] + (1.0 - ADAM_B1) * gg
        nv = ADAM_B2 * v_ref[...] + (1.0 - ADAM_B2) * (gg * gg)
        m_hat = nm / (1.0 - ADAM_B1 ** ADAM_STEP)
        v_hat = nv / (1.0 - ADAM_B2 ** ADAM_STEP)
        d_ref[...] = -ADAM_LR * (m_hat / (jnp.sqrt(v_hat) + ADAM_EPS) + ADAM_WD * w_ref[...])
        nm_ref[...] = nm
        nv_ref[...] = nv

    blk = pl.BlockSpec((tm, cols), lambda i: (i, 0))
    outs = pl.pallas_call(
        body, name=name, grid=(rows // tm,), in_specs=[blk] * 4, out_specs=[blk] * 3,
        out_shape=[jax.ShapeDtypeStruct((rows, cols), F32)] * 3, compiler_params=_cparams("parallel"),
    )(w2, g2, m2, v2)
    return tuple(o.reshape(shape) for o in outs)


def _all_gather(shard, *, name):
    R, C = shard.shape

    def body(x_ref, out_ref, send_sems, recv_sems, local_sem):
        x, y, c = lax.axis_index("x"), lax.axis_index("y"), lax.axis_index("c")
        me, sibling = (x, y, c), (x, y, 1 - c)
        chips = [(1 - x, y), (x, 1 - y), (1 - x, 1 - y)]

        def rows(px, py, pc):
            return out_ref.at[_logical(px, py, pc)]

        def copy(k, block, to, src=None):
            return pltpu.make_async_remote_copy(
                src_ref=rows(*block) if src is None else src, dst_ref=rows(*block),
                send_sem=send_sems.at[k], recv_sem=recv_sems.at[k], device_id=to, device_id_type=_MESH)

        mine = pltpu.make_async_copy(x_ref, rows(*me), local_sem)
        mine.start()
        first = [copy(0, me, sibling, src=x_ref)]
        first += [copy(1 + j, me, (*chip, c), src=x_ref) for j, chip in enumerate(chips)]
        for cp in first:
            cp.start()
        passed = [copy(4 + j, (*chip, c), sibling) for j, chip in enumerate(chips)]
        for j, chip in enumerate(chips):
            copy(1 + j, (*chip, c), me).wait_recv()
            passed[j].start()
        copy(0, sibling, me).wait_recv()
        for j, chip in enumerate(chips):
            copy(4 + j, (*chip, 1 - c), me).wait_recv()
        for cp in first + passed:
            cp.wait_send()
        mine.wait()

    return pl.pallas_call(
        body, name=name, out_shape=jax.ShapeDtypeStruct((N_DEV, R, C), shard.dtype),
        in_specs=[_HBM], out_specs=_HBM,
        scratch_shapes=[pltpu.SemaphoreType.DMA((7,)), pltpu.SemaphoreType.DMA((7,)), pltpu.SemaphoreType.DMA],
    )(shard)


def _gather_weights(shards, *, name):
    nt = len(shards)

    def body(*refs):
        ins, outs = refs[:nt], refs[nt:2 * nt]
        send_sems, recv_sems, local_sems = refs[2 * nt:]
        x, y, c = lax.axis_index("x"), lax.axis_index("y"), lax.axis_index("c")
        me, sibling = (x, y, c), (x, y, 1 - c)
        chips = [(1 - x, y), (x, 1 - y), (1 - x, 1 - y)]

        def copy(t, k, block, to, src=None):
            rows = outs[t].at[_logical(*block)]
            return pltpu.make_async_remote_copy(
                src_ref=rows if src is None else src, dst_ref=rows,
                send_sem=send_sems.at[t, k], recv_sem=recv_sems.at[t, k], device_id=to, device_id_type=_MESH)

        mine = [pltpu.make_async_copy(ins[t], outs[t].at[_logical(*me)], local_sems.at[t]) for t in range(nt)]
        for cp in mine:
            cp.start()
        started = []
        for t in range(nt):
            started.append(copy(t, 0, me, sibling, src=ins[t]))
            started += [copy(t, 1 + j, me, (*chip, c), src=ins[t]) for j, chip in enumerate(chips)]
        for cp in started:
            cp.start()
        for j, chip in enumerate(chips):
            for t in range(nt):
                copy(t, 1 + j, (*chip, c), me).wait_recv()
                fwd = copy(t, 4 + j, (*chip, c), sibling)
                fwd.start()
                started.append(fwd)
        for t in range(nt):
            copy(t, 0, sibling, me).wait_recv()
            for j, chip in enumerate(chips):
                copy(t, 4 + j, (*chip, 1 - c), me).wait_recv()
        for cp in started:
            cp.wait_send()
        for cp in mine:
            cp.wait()

    return pl.pallas_call(
        body, name=name, out_shape=[jax.ShapeDtypeStruct((N_DEV,) + s.shape, s.dtype) for s in shards],
        in_specs=[_HBM] * nt, out_specs=[_HBM] * nt,
        scratch_shapes=[pltpu.SemaphoreType.DMA((nt, 7)), pltpu.SemaphoreType.DMA((nt, 7)), pltpu.SemaphoreType.DMA((nt,))],
    )(*shards)


def _sum_rows(stacked, *, name):
    _, R, C = stacked.shape
    tr = R
    if N_DEV * R * C * stacked.dtype.itemsize > 12 * 1024 * 1024:
        for cand in range(512, 15, -16):
            if R % cand == 0:
                tr = cand
                break

    def body(s_ref, o_ref):
        acc = s_ref[0].astype(F32)
        for k in range(1, N_DEV):
            acc = acc + s_ref[k].astype(F32)
        o_ref[...] = acc

    return pl.pallas_call(
        body, name=name, grid=(R // tr,),
        in_specs=[pl.BlockSpec((N_DEV, tr, C), lambda i: (0, i, 0))], out_specs=pl.BlockSpec((tr, C), lambda i: (i, 0)),
        out_shape=jax.ShapeDtypeStruct((R, C), F32), compiler_params=_cparams("parallel"),
    )(stacked)


_LARGE = (("ab_w_in", 1, True), ("ab_w_out", 1, False), ("s5_w_glu", 1, True), ("xattn_wq", 2, False),
          ("xattn_wkv", 2, True), ("xattn_wo", 2, False), ("ffn_w_in", 2, True), ("ffn_w_out", 2, False))
_LARGE_KEYS = tuple((n, l) for n, layers, _ in _LARGE for l in range(layers))
_TRANSPOSED = {n: t for n, _, t in _LARGE}


def _owner_major(name, w):
    return w.T if _TRANSPOSED[name] else w


def _lb_from_logits(logits):
    return jnp.cumsum(jax.nn.softmax(logits, axis=0), axis=0)[0:1]


def _local_step(x, mem, target, W, shards=None):
    B, L, _ = x.shape
    T = B * L
    x0 = x.reshape(T, D_MODEL)
    memf = mem.reshape(B * MEM_LEN, D_MODEL)
    nw = W["norm_w"]
    cos2, sin2 = _rope_tables(L)
    W = dict(W)
    G, received = {}, {}

    def mmx(a, b, gather=(), scatter=(), **kw):
        if shards is None or not (gather or scatter):
            return _mm(a, b, **kw)
        out, gathered, got = _mm(a, b, gather=[shards[k] for k in gather],
                                 scatter=[G[k].reshape(N_DEV, -1, D_MODEL) for k in scatter], **kw)
        for k, g in zip(gather, gathered):
            W[k] = g.reshape(-1, D_MODEL)
        for k, r in zip(scatter, got):
            received[k] = r
        return out

    def vec(v):
        return v.reshape(1, -1)

    saved = []
    xin = x0
    for layer in range(2):
        s = {"x0": xin}
        tag = f"l{layer}"
        h1 = _rms_fwd(xin, vec(nw[layer, 0]), name=f"norm_pre_mix_{tag}", out_dtype=BF16 if layer == 0 else F32)
        s["h1"] = h1
        if layer == 0:
            lb, lb_vjp = jax.vjp(_lb_from_logits, W["hgrn_lb_logits"])
            onw = W["hgrn_out_norm_w"].reshape(1, A_WIDTH)
            z = mmx(h1, W["ab_w_in", 0], tb=True, name="ab_in",
                    gather=[("ab_w_out", 0), ("xattn_wq", 0), ("xattn_wkv", 0), ("xattn_wo", 0), ("ffn_w_in", 0)])
            oa, o_raw, s_start = _hgrn_fwd(z, lb, onw, B, L, name="hgrn_fwd")
            qr = _rope_fwd(z, cos2, sin2, 16, B, L, name="rope_q")
            kr = _rope_fwd(z, cos2, sin2, 20, B, L, name="rope_k")
            os_, ls_ = [], []
            for dil in B_DILS:
                o_g, l_g = _dil_fwd(qr, kr, z, dil, B, L, name=f"dil_fwd_{dil}")
                os_.append(o_g)
                ls_.append(l_g)
            ob, lse = _dil_combine(os_, ls_, name="dil_combine")
            ymix = jnp.concatenate([oa, ob], axis=-1).astype(BF16)
            y1 = mmx(ymix, W["ab_w_out", 0], name="ab_out", gather=[("ffn_w_out", 0)])
            s.update(z=z, lb=lb, lb_vjp=lb_vjp, onw=onw, o_raw=o_raw, s_start=s_start, qr=qr, kr=kr, ob=ob, lse=lse, ymix=ymix)
        else:
            p5 = tuple(W[n][0] for n in ("s5_lambda_re", "s5_lambda_im", "s5_log_dt", "s5_b_re", "s5_b_im", "s5_c_re", "s5_c_im"))
            (m8, e8, f8, _, _), s5_vjp = jax.vjp(_s5_build, *p5)
            tab_r, tab_i = _s5_scan_tables(p5[0], p5[1], p5[2], int(math.log2(L // C_TC)))
            mats = (m8.astype(BF16), e8.astype(BF16), f8.astype(BF16), tab_r, tab_i)
            dsk = W["s5_d"].reshape(1, D_MODEL)
            gl, ypre, x8, xs = _s5_fwd(h1, dsk, *mats, B, L, name="s5_fwd")
            w_glu = _interleave_rows(W["s5_w_glu", 0])
            zg, y1 = _mm(gl, w_glu, tb=True, name="s5_glu_in", gate=("fwd", "glu"))
            s.update(s5_vjp=s5_vjp, mats=mats, x8=x8, xs=xs, dsk=dsk, gl=gl, ypre=ypre, zg=zg, w_glu=w_glu)
        x1 = _rms_fwd(y1, vec(nw[layer, 1]), xin, name=f"norm_post_mix_{tag}")
        h2 = _rms_fwd(x1, vec(nw[layer, 2]), name=f"norm_pre_x_{tag}", out_dtype=BF16)
        memn = _rms_fwd(memf, vec(W["mem_norm_w"][layer]), name=f"norm_mem_{tag}", out_dtype=BF16)
        q = mmx(h2, W["xattn_wq", layer], out_dtype=BF16, name=f"x_q_{tag}", gather=[("s5_w_glu", 0)] if layer == 0 else [])
        kv = _mm(memn, W["xattn_wkv", layer], tb=True, out_dtype=BF16, name=f"x_kv_{tag}")
        o = _xattn_fwd(q, kv, B, L, name=f"x_attn_{tag}")
        y2 = mmx(o, W["xattn_wo", layer], name=f"x_o_{tag}", gather=[("xattn_wq", 1), ("xattn_wo", 1)] if layer == 0 else [])
        x2 = _rms_fwd(y2, vec(nw[layer, 3]), x1, name=f"norm_post_x_{tag}")
        h3 = _rms_fwd(x2, vec(nw[layer, 4]), name=f"norm_pre_ffn_{tag}", out_dtype=BF16)
        w_ffn_in = _interleave_rows(W["ffn_w_in", layer])
        zf, u = mmx(h3, w_ffn_in, tb=True, name=f"ffn_in_{tag}", gate=("fwd", "swiglu"),
                    gather=[("xattn_wkv", 1), ("ffn_w_in", 1), ("ffn_w_out", 1)] if layer == 0 else [])
        y3 = _mm(u, W["ffn_w_out", layer], name=f"ffn_out_{tag}")
        x3 = _rms_fwd(y3, vec(nw[layer, 5]), x2, name=f"norm_post_ffn_{tag}")
        s.update(y1=y1, x1=x1, h2=h2, memn=memn, q=q, kv=kv, o=o, y2=y2, x2=x2, h3=h3, zf=zf, u=u, y3=y3, w_ffn_in=w_ffn_in)
        saved.append(s)
        xin = x3

    loss_parts, dx = _loss_head(xin, target.reshape(T, D_MODEL), name="loss_head")

    d_norm = [[None] * 6 for _ in range(2)]
    d_memn = [None, None]
    for layer in (1, 0):
        s = saved[layer]
        tag = f"l{layer}"
        dy3, d_norm[layer][5] = _rms_bwd(s["y3"], vec(nw[layer, 5]), dx, name=f"bnorm_post_ffn_{tag}", out_dtype=BF16)
        dzf = mmx(dy3, W["ffn_w_out", layer], tb=True, name=f"b_ffn_out_dx_{tag}", gate=("bwd", "swiglu", s["zf"]),
                  scatter=[("xattn_wkv", 1), ("s5_w_glu", 0)] if layer == 0 else [])
        G["ffn_w_out", layer] = _mm(s["u"], dy3, ta=True, out_dtype=BF16, name=f"b_ffn_out_dw_{tag}")
        G["ffn_w_in", layer] = _interleave_rows(
            mmx(dzf, s["h3"], ta=True, out_dtype=BF16, name=f"b_ffn_in_dw_{tag}", scatter=[("ffn_w_out", layer)]), inverse=True)
        dh3 = mmx(dzf, s["w_ffn_in"], out_dtype=BF16, name=f"b_ffn_in_dx_{tag}", scatter=[("ffn_w_in", layer)])
        dx, d_norm[layer][4] = _rms_bwd(s["x2"], vec(nw[layer, 4]), dh3, dx, name=f"bnorm_pre_ffn_{tag}")
        dy2, d_norm[layer][3] = _rms_bwd(s["y2"], vec(nw[layer, 3]), dx, name=f"bnorm_post_x_{tag}", out_dtype=BF16)
        do = _mm(dy2, W["xattn_wo", layer], tb=True, out_dtype=BF16, name=f"b_x_o_dx_{tag}")
        G["xattn_wo", layer] = _mm(s["o"], dy2, ta=True, out_dtype=BF16, name=f"b_x_o_dw_{tag}")
        dq, dkv = _xattn_bwd(s["q"], s["kv"], do, B, L, name=f"b_x_attn_{tag}")
        G["xattn_wq", layer] = _mm(s["h2"], dq, ta=True, out_dtype=BF16, name=f"b_x_q_dw_{tag}")
        dh2 = _mm(dq, W["xattn_wq", layer], tb=True, out_dtype=BF16, name=f"b_x_q_dx_{tag}")
        G["xattn_wkv", layer] = _mm(dkv, s["memn"], ta=True, out_dtype=BF16, name=f"b_x_kv_dw_{tag}")
        dmemn = _mm(dkv, W["xattn_wkv", layer], out_dtype=BF16, name=f"b_x_kv_dx_{tag}")
        _, d_memn[layer] = _rms_bwd(memf, vec(W["mem_norm_w"][layer]), dmemn, name=f"bnorm_mem_{tag}", out_dtype=BF16)
        dx, d_norm[layer][2] = _rms_bwd(s["x1"], vec(nw[layer, 2]), dh2, dx, name=f"bnorm_pre_x_{tag}")
        dy1, d_norm[layer][1] = _rms_bwd(s["y1"], vec(nw[layer, 1]), dx, name=f"bnorm_post_mix_{tag}", out_dtype=BF16)
        if layer == 0:
            z = s["z"]
            dymix = _mm(dy1, W["ab_w_out", 0], tb=True, name="b_ab_out_dx")
            G["ab_w_out", 0] = _mm(s["ymix"], dy1, ta=True, out_dtype=BF16, name="b_ab_out_dw")
            dqa, dfa, dia, dga, d_onw, d_lb = _hgrn_bwd(z, s["lb"], s["onw"], s["o_raw"], s["s_start"], dymix, B, L, name="hgrn_bwd")
            dqs, dks, dvs = [], [], []
            for dil in B_DILS:
                dqs.append(_dil_bwd_q(s["qr"], s["kr"], z, dymix, s["ob"], s["lse"], dil, B, L, name=f"dil_bwd_q_{dil}"))
                dk_g, dv_g = _dil_bwd_kv(s["qr"], s["kr"], z, dymix, s["ob"], s["lse"], dil, B, L, name=f"dil_bwd_kv_{dil}")
                dks.append(dk_g)
                dvs.append(dv_g)
            dqb = _sum3(*dqs, cos2, sin2, True, B, L, name="b_rope_q")
            dkb = _sum3(*dks, cos2, sin2, True, B, L, name="b_rope_k")
            dvb = _sum3(*dvs, cos2, sin2, False, B, L, name="b_sum_v")
            dz = jnp.concatenate([dqa, dfa, dia, dga, dqb, dkb, dvb], axis=-1).astype(BF16)
            G["ab_w_in", 0] = mmx(dz, s["h1"], ta=True, out_dtype=BF16, name="b_ab_in_dw",
                                  scatter=[("xattn_wo", 0), ("xattn_wq", 0), ("xattn_wkv", 0), ("ab_w_out", 0)])
            dh1 = mmx(dz, W["ab_w_in", 0], out_dtype=BF16, name="b_ab_in_dx", scatter=[("ab_w_in", 0)])
            G["hgrn_out_norm_w"] = jnp.sum(d_onw.reshape(B, A_WIDTH), axis=0, keepdims=True)
            d_lb_row = jnp.sum(d_lb.reshape(B, A_WIDTH), axis=0, keepdims=True)
            G["hgrn_lb_logits"] = s["lb_vjp"](d_lb_row)[0]
        else:
            dzg = _gated_bwd(s["zg"], dy1, "glu", name="b_s5_glu")
            G["s5_w_glu", 0] = _interleave_rows(
                _mm(dzg, s["gl"], ta=True, out_dtype=BF16, name="b_s5_glu_dw"), inverse=True)
            dgl = mmx(dzg, s["w_glu"], name="b_s5_glu_dx", scatter=[("xattn_wo", 1), ("xattn_wq", 1)])
            dh1, dy8, de_in, da, d_dsk = _s5_bwd(dgl, s["ypre"], s["h1"], s["dsk"], s["xs"], *s["mats"], B, L, name="s5_bwd")
            dm8 = _bmm_tn(s["x8"], dy8, C_NB, name="s5_bwd_dm")
            df8 = _bmm_tn(s["xs"], dy8, C_NB, name="s5_bwd_df")
            de8 = _bmm_tn(s["x8"], de_in, C_NB, name="s5_bwd_de")
            half = C_S8 // 2
            da_r = (da[:, 0, :half] + da[:, 0, half:]).reshape(C_GROUPS, C_STATE)
            da_i = (da[:, 1, half:] - da[:, 1, :half]).reshape(C_GROUPS, C_STATE)
            gp = s["s5_vjp"]((dm8, de8, df8, da_r, da_i))
            for n, gv in zip(("s5_lambda_re", "s5_lambda_im", "s5_log_dt", "s5_b_re", "s5_b_im", "s5_c_re", "s5_c_im"), gp):
                G[n] = gv[None]
            G["s5_d"] = d_dsk
        dx, d_norm[layer][0] = _rms_bwd(s["x0"], vec(nw[layer, 0]), dh1, dx, name=f"bnorm_pre_mix_{tag}")

    G["norm_w"] = jnp.stack([jnp.concatenate(d_norm[l], axis=0) for l in range(2)])
    G["mem_norm_w"] = jnp.concatenate(d_memn, axis=0)
    if shards is not None:
        G.update(received)
    return loss_parts, dx.reshape(B, L, D_MODEL), G


_SMALL = (("norm_w", (2, 6, 1024)), ("mem_norm_w", (2, 1024)), ("hgrn_lb_logits", (3, 512)), ("hgrn_out_norm_w", (1, 512)),
          ("s5_lambda_re", (1, 64, 64)), ("s5_lambda_im", (1, 64, 64)), ("s5_log_dt", (1, 64)),
          ("s5_b_re", (1, 64, 64, 16)), ("s5_b_im", (1, 64, 64, 16)), ("s5_c_re", (1, 64, 16, 64)),
          ("s5_c_im", (1, 64, 16, 64)), ("s5_d", (1, 1024)))

_WEIGHT_ORDER = ('norm_w', 'mem_norm_w', 'ab_w_in', 'ab_w_out', 'hgrn_lb_logits', 'hgrn_out_norm_w', 's5_lambda_re',
                 's5_lambda_im', 's5_log_dt', 's5_b_re', 's5_b_im', 's5_c_re', 's5_c_im', 's5_d', 's5_w_glu', 'xattn_wq',
                 'xattn_wkv', 'xattn_wo', 'ffn_w_in', 'ffn_w_out')


def kernel(x, mem, norm_w, mem_norm_w, ab_w_in, ab_w_out, hgrn_lb_logits, hgrn_out_norm_w, s5_lambda_re, s5_lambda_im, s5_log_dt, s5_b_re, s5_b_im, s5_c_re, s5_c_im, s5_d, s5_w_glu, xattn_wq, xattn_wkv, xattn_wo, ffn_w_in, ffn_w_out, loss_target, m_norm_w, m_mem_norm_w, m_ab_w_in, m_ab_w_out, m_hgrn_lb_logits, m_hgrn_out_norm_w, m_s5_lambda_re, m_s5_lambda_im, m_s5_log_dt, m_s5_b_re, m_s5_b_im, m_s5_c_re, m_s5_c_im, m_s5_d, m_s5_w_glu, m_xattn_wq, m_xattn_wkv, m_xattn_wo, m_ffn_w_in, m_ffn_w_out, v_norm_w, v_mem_norm_w, v_ab_w_in, v_ab_w_out, v_hgrn_lb_logits, v_hgrn_out_norm_w, v_s5_lambda_re, v_s5_lambda_im, v_s5_log_dt, v_s5_b_re, v_s5_b_im, v_s5_c_re, v_s5_c_im, v_s5_d, v_s5_w_glu, v_xattn_wq, v_xattn_wkv, v_xattn_wo, v_ffn_w_in, v_ffn_w_out):
    local = dict(norm_w=norm_w, mem_norm_w=mem_norm_w, ab_w_in=ab_w_in, ab_w_out=ab_w_out, hgrn_lb_logits=hgrn_lb_logits,
                 hgrn_out_norm_w=hgrn_out_norm_w, s5_lambda_re=s5_lambda_re, s5_lambda_im=s5_lambda_im, s5_log_dt=s5_log_dt,
                 s5_b_re=s5_b_re, s5_b_im=s5_b_im, s5_c_re=s5_c_re, s5_c_im=s5_c_im, s5_d=s5_d, s5_w_glu=s5_w_glu,
                 xattn_wq=xattn_wq, xattn_wkv=xattn_wkv, xattn_wo=xattn_wo, ffn_w_in=ffn_w_in, ffn_w_out=ffn_w_out)
    mom_m = dict(zip(_WEIGHT_ORDER, (m_norm_w, m_mem_norm_w, m_ab_w_in, m_ab_w_out, m_hgrn_lb_logits, m_hgrn_out_norm_w, m_s5_lambda_re, m_s5_lambda_im, m_s5_log_dt, m_s5_b_re, m_s5_b_im, m_s5_c_re, m_s5_c_im, m_s5_d, m_s5_w_glu, m_xattn_wq, m_xattn_wkv, m_xattn_wo, m_ffn_w_in, m_ffn_w_out)))
    mom_v = dict(zip(_WEIGHT_ORDER, (v_norm_w, v_mem_norm_w, v_ab_w_in, v_ab_w_out, v_hgrn_lb_logits, v_hgrn_out_norm_w, v_s5_lambda_re, v_s5_lambda_im, v_s5_log_dt, v_s5_b_re, v_s5_b_im, v_s5_c_re, v_s5_c_im, v_s5_d, v_s5_w_glu, v_xattn_wq, v_xattn_wkv, v_xattn_wo, v_ffn_w_in, v_ffn_w_out)))
    dev = 4 * lax.axis_index("x") + 2 * lax.axis_index("y") + lax.axis_index("c")

    shards = {(n, l): _owner_major(n, local[n][l]).astype(BF16) for n, l in _LARGE_KEYS}
    first = ("ab_w_in", 0)
    W = {first: _gather_weights([shards[first]], name="gather_first")[0].reshape(-1, D_MODEL)}
    tiny =jnp.concatenate([norm_w.reshape(-1), s5_d.reshape(-1)])
    tiny = jnp.pad(tiny, (0, 16 * LANES - tiny.shape[0])).reshape(16, LANES)
    tiny_all = _all_gather(tiny, name="gather_tiny").reshape(N_DEV, 16 * LANES)
    W["norm_w"] = tiny_all[:, :12 * LANES].reshape(N_DEV, 2, 6, LANES).transpose(1, 2, 0, 3).reshape(2, 6, D_MODEL)
    W["s5_d"] = tiny_all[:, 12 * LANES:13 * LANES].reshape(1, D_MODEL)
    for n in ("mem_norm_w", "hgrn_lb_logits", "hgrn_out_norm_w", "s5_lambda_re", "s5_lambda_im", "s5_log_dt",
              "s5_b_re", "s5_b_im", "s5_c_re", "s5_c_im"):
        W[n] = local[n]

    loss_parts, grad_x, G = _local_step(x, mem, loss_target, W, shards)

    g_layers = {}
    for n, l in _LARGE_KEYS:
        g = _sum_rows(G[n, l], name=f"sum_grads_{n}_{l}")
        g_layers.setdefault(n, []).append(g.T if _TRANSPOSED[n] else g)
    g_local = {n: jnp.stack(gl) for n, gl in g_layers.items()}
    small =jnp.concatenate([G[n].reshape(-1) for n, _ in _SMALL] + [0.5 / D_MODEL * jnp.sum(loss_parts).reshape(1)])
    n_small = small.shape[0]
    small = jnp.pad(small, (0, (-n_small) % (8 * LANES))).reshape(-1, LANES)
    small_sum = _sum_rows(_all_gather(small, name="gather_small"), name="sum_small").reshape(-1)
    g_full, off = {}, 0
    for n, shp in _SMALL:
        size = int(np.prod(shp))
        g_full[n] = small_sum[off:off + size].reshape(shp)
        off += size
    loss = small_sum[off]
    grads = dict(g_local)
    for n, shp in _SMALL:
        if n == "norm_w":
            grads[n] = lax.dynamic_slice_in_dim(g_full[n], dev * LANES, LANES, axis=2)
        elif n == "s5_d":
            grads[n] = lax.dynamic_slice_in_dim(g_full[n], dev * LANES, LANES, axis=1)
        else:
            grads[n] = g_full[n]

    delta, new_m, new_v = {}, {}, {}
    for n in _WEIGHT_ORDER:
        delta[n], new_m[n], new_v[n] = _adamw(local[n], grads[n], mom_m[n], mom_v[n], name=f"adamw_{n}")
    return (loss, grad_x, *[grads[n] for n in _WEIGHT_ORDER], *[delta[n] for n in _WEIGHT_ORDER],
            *[new_m[n] for n in _WEIGHT_ORDER], *[new_v[n] for n in _WEIGHT_ORDER])
```

```python
import functools
import math

import numpy as np
import jax
import jax.numpy as jnp
from jax import lax
from jax.experimental import pallas as pl
from jax.experimental.pallas import tpu as pltpu

F32 = jnp.float32
BF16 = jnp.bfloat16
HI = lax.Precision.HIGHEST

D_MODEL = 1024
NORM_EPS = 1e-6
A_WIDTH = 512
A_HEAD = 128
A_CHUNK = 32
A_SUPER = 256
B_SPAN = 128
B_DILS = (1, 4, 16)
ROPE_THETA = 10000.0
C_GROUPS = 64
C_GROUP = 16
C_STATE = 64
C_TC = 8
C_MIN_NEG_RE = -1e-4
MEM_LEN = 256
X_HEADS = 4
X_HD = 256
D_FF = 2816
N_DEV = 8
LANES = 128

ADAM_LR, ADAM_B1, ADAM_B2, ADAM_EPS, ADAM_WD, ADAM_STEP = 0.001, 0.9, 0.999, 1e-08, 0.01, 10

NEG_BIG = -1e30


def _tile(n, pref):
    for d in range(min(pref, n) // LANES * LANES, 0, -LANES):
        if n % d == 0:
            return d
    return n


def _cparams(*sem):
    return pltpu.CompilerParams(dimension_semantics=sem, vmem_limit_bytes=56 * 1024 * 1024)


def _sigmoid(x):
    return 0.5 * jnp.tanh(0.5 * x) + 0.5


def _erf(x):
    ax = jnp.abs(x)
    t = 1.0 / (1.0 + 0.3275911 * ax)
    poly = t * (0.254829592 + t * (-0.284496736 + t * (1.421413741 + t * (-1.453152027 + t * 1.061405429))))
    y = 1.0 - poly * jnp.exp(-ax * ax)
    return jnp.where(x < 0, -y, y)


_HBM = pl.BlockSpec(memory_space=pltpu.HBM)
_MESH = pl.DeviceIdType.MESH


def _logical(px, py, pc):
    return 4 * px + 2 * py + pc


class _Exchange:
    def __init__(self, gather=(), scatter=()):
        self.gather, self.scatter = list(gather), list(scatter)
        self.ng, self.n = len(self.gather), len(self.gather) + len(self.scatter)

    def operands(self):
        return self.gather + self.scatter

    def in_specs(self):
        return [_HBM] * self.n

    def out_shapes(self):
        return ([jax.ShapeDtypeStruct((N_DEV,) + g.shape, g.dtype) for g in self.gather]
                + [jax.ShapeDtypeStruct(s.shape, s.dtype) for s in self.scatter])

    def scratch(self):
        if not self.n:
            return []
        return [pltpu.SemaphoreType.DMA((self.n, 7)), pltpu.SemaphoreType.DMA((self.n, 7)), pltpu.SemaphoreType.DMA((self.n,))]

    def split(self, results):
        return list(results[:self.ng]), list(results[self.ng:])

    def run(self, ins, outs, sems, first, last):
        if not self.n:
            return
        send_sems, recv_sems, local_sems = sems
        x, y, c = lax.axis_index("x"), lax.axis_index("y"), lax.axis_index("c")
        me, sibling = _logical(x, y, c), (x, y, 1 - c)
        chips = [(1 - x, y), (x, 1 - y), (1 - x, 1 - y)]
        peers = [(x ^ (k >> 2), y ^ ((k >> 1) & 1), c ^ (k & 1)) for k in range(1, N_DEV)]

        def remote(t, k, src, dst, to):
            return pltpu.make_async_remote_copy(src_ref=src, dst_ref=dst, send_sem=send_sems.at[t, k],
                                                recv_sem=recv_sems.at[t, k], device_id=to, device_id_type=_MESH)

        def local(t):
            src = ins[t] if t < self.ng else ins[t].at[me]
            return pltpu.make_async_copy(src, outs[t].at[me], local_sems.at[t])

        @pl.when(first)
        def _():
            for t in range(self.n):
                local(t).start()
                if t < self.ng:
                    remote(t, 0, ins[t], outs[t].at[me], sibling).start()
                    for j, chip in enumerate(chips):
                        remote(t, 1 + j, ins[t], outs[t].at[me], (*chip, c)).start()
                else:
                    for k, peer in enumerate(peers):
                        remote(t, k, ins[t].at[_logical(*peer)], outs[t].at[me], peer).start()

        @pl.when(last)
        def _():
            for j, chip in enumerate(chips):
                for t in range(self.ng):
                    landed = outs[t].at[_logical(*chip, c)]
                    remote(t, 1 + j, ins[t], landed, sibling).wait_recv()
                    remote(t, 4 + j, landed, landed, sibling).start()
            for t in range(self.n):
                if t < self.ng:
                    remote(t, 0, ins[t], outs[t].at[_logical(*sibling)], sibling).wait_recv()
                    for j, chip in enumerate(chips):
                        remote(t, 4 + j, ins[t], outs[t].at[_logical(*chip, 1 - c)], sibling).wait_recv()
                    for k in range(7):
                        remote(t, k, ins[t], outs[t].at[me], sibling).wait_send()
                else:
                    for k, peer in enumerate(peers):
                        remote(t, k, ins[t].at[me], outs[t].at[_logical(*peer)], peer).wait_recv()
                    for k, peer in enumerate(peers):
                        remote(t, k, ins[t].at[_logical(*peer)], outs[t].at[me], peer).wait_send()
                local(t).wait()


_MM_VMEM_BUDGET = 36 * 1024 * 1024


def _mm(a, b, *, ta=False, tb=False, out_dtype=F32, name, tiles=(1408, 1408, 4096), gather=(), scatter=(), gate=None):
    M, K = (a.shape[1], a.shape[0]) if ta else a.shape
    N = b.shape[0] if tb else b.shape[1]
    assert (b.shape[1] if tb else b.shape[0]) == K
    gate_mode = gate[0] if gate else None
    tm, tn, tk = _tile(M, tiles[0]), _tile(N, tiles[1]), _tile(K, tiles[2])
    if gate_mode == "fwd":
        bs = _gate_block(N // 2)
        tn = 2 * bs
    elif gate_mode == "bwd":
        bs = _gate_block(N)
        tn = bs

    def vmem_bytes():
        out = 2 * tm * tn * jnp.dtype(out_dtype).itemsize + (4 * tm * tn if tk < K else 0)
        if gate_mode:
            out += 2 * 3 * tm * tn * 2
        return 2 * 2 * (tm * tk + tk * tn) + out

    while vmem_bytes() > _MM_VMEM_BUDGET:
        if tk > 512:
            tk = _tile(K, tk - LANES)
        elif tn > 256 and not gate_mode:
            tn = _tile(N, tn - LANES)
        else:
            tm = _tile(M, tm - LANES)
    ni, nj, nk = M // tm, N // tn, K // tk
    ex = _Exchange(gather, scatter)

    a_spec = pl.BlockSpec((tk, tm), lambda i, j, k: (k, i)) if ta else pl.BlockSpec((tm, tk), lambda i, j, k: (i, k))
    b_spec = pl.BlockSpec((tn, tk), lambda i, j, k: (j, k)) if tb else pl.BlockSpec((tk, tn), lambda i, j, k: (k, j))
    dims = (((0 if ta else 1,), (1 if tb else 0,)), ((), ()))
    n_acc = 1 if nk > 1 else 0
    n_in = 3 if gate_mode == "bwd" else 2
    n_out = 2 if gate_mode == "fwd" else 1

    def finish(val, in_refs, out_refs):
        if gate_mode is None:
            out_refs[0][...] = val.astype(out_refs[0].dtype)
        elif gate_mode == "fwd":
            out_refs[0][...] = val.astype(BF16)
            for p in range(tn // (2 * bs)):
                a_, b_ = val[:, 2 * p * bs:(2 * p + 1) * bs], val[:, (2 * p + 1) * bs:(2 * p + 2) * bs]
                out_refs[1][:, p * bs:(p + 1) * bs] = _gate_value(a_, b_, gate[1]).astype(BF16)
        else:
            z_ref = in_refs[2]
            for p in range(tn // bs):
                a_ = z_ref[:, 2 * p * bs:(2 * p + 1) * bs].astype(F32)
                b_ = z_ref[:, (2 * p + 1) * bs:(2 * p + 2) * bs].astype(F32)
                da, db = _gate_grads(a_, b_, val[:, p * bs:(p + 1) * bs], gate[1])
                out_refs[0][:, 2 * p * bs:(2 * p + 1) * bs] = da.astype(BF16)
                out_refs[0][:, (2 * p + 1) * bs:(2 * p + 2) * bs] = db.astype(BF16)

    def body(*refs):
        in_refs, rest = refs[:n_in], refs[n_in:]
        ex_in, out_refs = rest[:ex.n], rest[ex.n:ex.n + n_out]
        ex_out, scratch = rest[ex.n + n_out:2 * ex.n + n_out], rest[2 * ex.n + n_out:]
        i, j, k = pl.program_id(0), pl.program_id(1), pl.program_id(2)
        ex.run(ex_in, ex_out, scratch[n_acc:], (i == 0) & (j == 0) & (k == 0), (i == ni - 1) & (j == nj - 1) & (k == nk - 1))
        part = lax.dot_general(in_refs[0][...].astype(BF16), in_refs[1][...].astype(BF16), dims, preferred_element_type=F32)
        if nk == 1:
            finish(part, in_refs, out_refs)
            return
        acc_ref = scratch[0]

        @pl.when(k == 0)
        def _():
            acc_ref[...] = part

        @pl.when(k > 0)
        def _():
            acc_ref[...] += part

        @pl.when(k == nk - 1)
        def _():
            finish(acc_ref[...], in_refs, out_refs)

    tile = lambda width: pl.BlockSpec((tm, width), lambda i, j, k: (i, j))
    if gate_mode == "fwd":
        out_specs, out_shape = [tile(tn), tile(tn // 2)], [jax.ShapeDtypeStruct((M, N), BF16), jax.ShapeDtypeStruct((M, N // 2), BF16)]
    elif gate_mode == "bwd":
        out_specs, out_shape = [tile(2 * tn)], [jax.ShapeDtypeStruct((M, 2 * N), BF16)]
    else:
        out_specs, out_shape = [tile(tn)], [jax.ShapeDtypeStruct((M, N), out_dtype)]
    operands = [a, b] + ([gate[2]] if gate_mode == "bwd" else [])
    sem = ("arbitrary",) * 3 if ex.n else ("parallel", "parallel", "arbitrary")
    res = pl.pallas_call(
        body, name=name, grid=(ni, nj, nk),
        in_specs=[a_spec, b_spec] + ([tile(2 * tn)] if gate_mode == "bwd" else []) + ex.in_specs(),
        out_specs=out_specs + ex.in_specs(),
        out_shape=out_shape + ex.out_shapes(),
        scratch_shapes=([pltpu.VMEM((tm, tn), F32)] if nk > 1 else []) + ex.scratch(),
        compiler_params=_cparams(*sem),
    )(*operands, *ex.operands())
    main = res[0] if n_out == 1 else tuple(res[:n_out])
    if not ex.n:
        return main
    return (main,) + tuple(ex.split(res[n_out:]))


def _rms_fwd(x, w, res=None, *, name, out_dtype=F32):
    T, C = x.shape
    tm = _tile(T, 512)
    has_res = res is not None

    def body(*refs):
        x_ref, w_ref = refs[0], refs[1]
        o_ref = refs[-1]
        xv = x_ref[...].astype(F32)
        r = lax.rsqrt(jnp.mean(xv * xv, axis=-1, keepdims=True) + NORM_EPS)
        y = xv * r * w_ref[...]
        if has_res:
            y = y + refs[2][...]
        o_ref[...] = y.astype(o_ref.dtype)

    row = pl.BlockSpec((tm, C), lambda i: (i, 0))
    vec = pl.BlockSpec((1, C), lambda i: (0, 0))
    ins = [x, w] + ([res] if has_res else [])
    return pl.pallas_call(
        body, name=name, grid=(T // tm,), in_specs=[row, vec] + ([row] if has_res else []), out_specs=row,
        out_shape=jax.ShapeDtypeStruct((T, C), out_dtype), compiler_params=_cparams("parallel"),
    )(*ins)


def _rms_bwd(x, w, dy, add=None, *, name, out_dtype=F32):
    T, C = x.shape
    tm = _tile(T, 512)
    has_add = add is not None

    def body(*refs):
        x_ref, w_ref, dy_ref = refs[:3]
        dx_ref, dw_ref = refs[-2], refs[-1]
        xv = x_ref[...].astype(F32)
        r = lax.rsqrt(jnp.mean(xv * xv, axis=-1, keepdims=True) + NORM_EPS)
        xh = xv * r
        g = dy_ref[...].astype(F32)
        part = jnp.sum(g * xh, axis=0, keepdims=True)

        @pl.when(pl.program_id(0) == 0)
        def _():
            dw_ref[...] = part

        @pl.when(pl.program_id(0) > 0)
        def _():
            dw_ref[...] += part

        gx = g * w_ref[...]
        dx = r * (gx - xh * jnp.mean(gx * xh, axis=-1, keepdims=True))
        if has_add:
            dx = dx + refs[3][...]
        dx_ref[...] = dx.astype(dx_ref.dtype)

    row = pl.BlockSpec((tm, C), lambda i: (i, 0))
    vec = pl.BlockSpec((1, C), lambda i: (0, 0))
    ins = [x, w, dy] + ([add] if has_add else [])
    return pl.pallas_call(
        body, name=name, grid=(T // tm,), in_specs=[row, vec, row] + ([row] if has_add else []),
        out_specs=[row, vec],
        out_shape=[jax.ShapeDtypeStruct((T, C), out_dtype), jax.ShapeDtypeStruct((1, C), F32)],
        compiler_params=_cparams("arbitrary"),
    )(*ins)


def _gate_block(width):
    return _tile(width, 256)


def _gate_value(a, b, kind):
    return a * _sigmoid(a) * b if kind == "swiglu" else a * _sigmoid(b)


def _gate_grads(a, b, d, kind):
    if kind == "swiglu":
        s = _sigmoid(a)
        return d * b * (s * (1.0 + a * (1.0 - s))), d * a * s
    s = _sigmoid(b)
    return d * s, d * a * s * (1.0 - s)


def _interleave_rows(w, inverse=False):
    W2, C = w.shape
    bs = _gate_block(W2 // 2)
    nb = W2 // 2 // bs
    shape = (nb, 2, bs, C) if inverse else (2, nb, bs, C)
    return w.reshape(shape).transpose(1, 0, 2, 3).reshape(W2, C)


def _gated_bwd(z, dout, kind, *, name):
    T, W2 = z.shape
    W = W2 // 2
    tm, bs = _tile(T, 512), _gate_block(W)

    def body(z_ref, d_ref, o_ref):
        da, db = _gate_grads(z_ref[:, :bs].astype(F32), z_ref[:, bs:].astype(F32), d_ref[...].astype(F32), kind)
        o_ref[:, :bs] = da.astype(o_ref.dtype)
        o_ref[:, bs:] = db.astype(o_ref.dtype)

    return pl.pallas_call(
        body, name=name, grid=(T // tm, W // bs),
        in_specs=[pl.BlockSpec((tm, 2 * bs), lambda i, j: (i, j)), pl.BlockSpec((tm, bs), lambda i, j: (i, j))],
        out_specs=pl.BlockSpec((tm, 2 * bs), lambda i, j: (i, j)),
        out_shape=jax.ShapeDtypeStruct((T, W2), BF16), compiler_params=_cparams("parallel", "parallel"),
    )(z, dout)


_NT = (((1,), (1,)), ((), ()))
_TN = (((0,), (0,)), ((), ()))


def _dot(a, b, dims=None, precision=None):
    if dims is None:
        return jnp.dot(a, b, preferred_element_type=F32, precision=precision)
    return lax.dot_general(a, b, dims, preferred_element_type=F32, precision=precision)


def _xattn_fwd(q, kv, B, L, *, name):
    T = q.shape[0]
    tq = 256
    nq = L // tq
    scale = X_HD ** -0.5

    def body(q_ref, k_ref, v_ref, o_ref):
        for h in range(X_HEADS):
            sl = slice(h * X_HD, (h + 1) * X_HD)
            qh, kh, vh = q_ref[:, sl].astype(BF16), k_ref[:, sl].astype(BF16), v_ref[:, sl].astype(BF16)
            s = _dot(qh, kh, _NT) * scale
            m = jnp.max(s, axis=-1, keepdims=True)
            p = jnp.exp(s - m)
            l = jnp.sum(p, axis=-1, keepdims=True)
            o_ref[:, sl] = (_dot(p.astype(BF16), vh) / l).astype(BF16)

    return pl.pallas_call(
        body, name=name, grid=(B, nq),
        in_specs=[pl.BlockSpec((tq, D_MODEL), lambda b, i: (b * nq + i, 0)),
                  pl.BlockSpec((MEM_LEN, D_MODEL), lambda b, i: (b, 0)),
                  pl.BlockSpec((MEM_LEN, D_MODEL), lambda b, i: (b, 1))],
        out_specs=pl.BlockSpec((tq, D_MODEL), lambda b, i: (b * nq + i, 0)),
        out_shape=jax.ShapeDtypeStruct((T, D_MODEL), BF16), compiler_params=_cparams("parallel", "parallel"),
    )(q, kv, kv)


def _xattn_bwd(q, kv, do, B, L, *, name):
    T = q.shape[0]
    tq = 256
    nq = L // tq
    scale = X_HD ** -0.5

    def body(q_ref, k_ref, v_ref, do_ref, dq_ref, dkv_ref):
        @pl.when(pl.program_id(1) == 0)
        def _():
            dkv_ref[...] = jnp.zeros_like(dkv_ref)

        for h in range(X_HEADS):
            sl = slice(h * X_HD, (h + 1) * X_HD)
            slv = slice(D_MODEL + h * X_HD, D_MODEL + (h + 1) * X_HD)
            qh, kh, vh = q_ref[:, sl].astype(BF16), k_ref[:, sl].astype(BF16), v_ref[:, sl].astype(BF16)
            doh = do_ref[:, sl].astype(BF16)
            s = _dot(qh, kh, _NT) * scale
            m = jnp.max(s, axis=-1, keepdims=True)
            e = jnp.exp(s - m)
            p = e / jnp.sum(e, axis=-1, keepdims=True)
            dkv_ref[:, slv] += _dot(p.astype(BF16), doh, _TN)
            dp = _dot(doh, vh, _NT)
            ds = p * (dp - jnp.sum(dp * p, axis=-1, keepdims=True)) * scale
            dsb = ds.astype(BF16)
            dq_ref[:, sl] = _dot(dsb, kh).astype(BF16)
            dkv_ref[:, sl] += _dot(dsb, qh, _TN)

    return pl.pallas_call(
        body, name=name, grid=(B, nq),
        in_specs=[pl.BlockSpec((tq, D_MODEL), lambda b, i: (b * nq + i, 0)),
                  pl.BlockSpec((MEM_LEN, D_MODEL), lambda b, i: (b, 0)),
                  pl.BlockSpec((MEM_LEN, D_MODEL), lambda b, i: (b, 1)),
                  pl.BlockSpec((tq, D_MODEL), lambda b, i: (b * nq + i, 0))],
        out_specs=[pl.BlockSpec((tq, D_MODEL), lambda b, i: (b * nq + i, 0)),
                   pl.BlockSpec((MEM_LEN, 2 * D_MODEL), lambda b, i: (b, 0))],
        out_shape=[jax.ShapeDtypeStruct((T, D_MODEL), BF16), jax.ShapeDtypeStruct((B * MEM_LEN, 2 * D_MODEL), F32)],
        compiler_params=_cparams("parallel", "arbitrary"),
    )(q, kv, kv, do)


def _chunk_masks():
    row = lax.broadcasted_iota(jnp.int32, (A_SUPER, A_SUPER), 0)
    col = lax.broadcasted_iota(jnp.int32, (A_SUPER, A_SUPER), 1)
    same = jnp.right_shift(row, 5) == jnp.right_shift(col, 5)
    return same, same & (col <= row), same & (col >= row)


def _dot_mask(mask, x):
    m = mask.astype(BF16)
    hi = x.astype(BF16)
    rest = x - hi.astype(F32)
    mid = rest.astype(BF16)
    lo = (rest - mid.astype(F32)).astype(BF16)
    return _dot(m, hi) + _dot(m, mid) + _dot(m, lo)


def _chunk_row(x, which):
    rows = [x[c * A_CHUNK + which % A_CHUNK:c * A_CHUNK + which % A_CHUNK + 1, :] for c in range(A_SUPER // A_CHUNK)]
    return jnp.concatenate([jnp.broadcast_to(r, (A_CHUNK, x.shape[1])) for r in rows], axis=0)


def _hgrn_gates(fa, lb):
    sig = _sigmoid(fa)
    f = lb + (1.0 - lb) * sig
    return sig, f, jnp.log(f), 1.0 - f


def _hgrn_fwd(z, lb, onw, B, L, *, name):
    T = B * L
    ns = L // A_SUPER
    nch = A_SUPER // A_CHUNK

    def body(q_ref, f_ref, v_ref, g_ref, lb_ref, w_ref, oa_ref, o_ref, s_ref, st_ref, sc_ref):
        @pl.when(pl.program_id(2) == 0)
        def _():
            st_ref[...] = jnp.zeros_like(st_ref)

        s_ref[0] = st_ref[...]
        same, tril, _ = _chunk_masks()
        q, v = q_ref[...], v_ref[...]
        _, _, lf, k = _hgrn_gates(f_ref[...], lb_ref[...])
        bcs = _dot_mask(tril, lf)
        bl = _chunk_row(bcs, -1)
        qd = (q * jnp.exp(bcs)).astype(BF16)
        ki = (k * jnp.exp(-bcs)).astype(BF16)
        ke = (k * jnp.exp(bl - bcs)).astype(BF16)
        dec = jnp.exp(bl)
        vb = v.astype(BF16)
        a = jnp.where(tril, _dot(qd, ki, _NT), 0.0)
        o_ref[...] = _dot(a.astype(BF16), vb)
        chunks = [slice(c * A_CHUNK, (c + 1) * A_CHUNK) for c in range(nch)]
        outer = [_dot(vb[rs], ke[rs], _TN) for rs in chunks]
        st = st_ref[...]
        for c, rs in enumerate(chunks):
            sc_ref[c] = st.astype(BF16)
            st = st * dec[c * A_CHUNK:c * A_CHUNK + 1, :] + outer[c]
        st_ref[...] = st
        for c, rs in enumerate(chunks):
            o_ref[rs, :] += _dot(qd[rs], sc_ref[c], _NT)
        o = o_ref[...]
        r = lax.rsqrt(jnp.mean(o * o, axis=-1, keepdims=True) + NORM_EPS)
        g = g_ref[...]
        oa_ref[...] = o * r * w_ref[...] * (g * _sigmoid(g))

    def zspec(off):
        return pl.BlockSpec((A_SUPER, A_HEAD), lambda b, h, n: (b * ns + n, off + h))

    hvec = pl.BlockSpec((1, A_HEAD), lambda b, h, n: (0, h))
    ospec = pl.BlockSpec((A_SUPER, A_HEAD), lambda b, h, n: (b * ns + n, h))
    return pl.pallas_call(
        body, name=name, grid=(B, 4, ns),
        in_specs=[zspec(0), zspec(4), zspec(8), zspec(12), hvec, hvec],
        out_specs=[ospec, ospec, pl.BlockSpec((1, A_HEAD, A_HEAD), lambda b, h, n: ((b * 4 + h) * ns + n, 0, 0))],
        out_shape=[jax.ShapeDtypeStruct((T, A_WIDTH), F32), jax.ShapeDtypeStruct((T, A_WIDTH), F32),
                   jax.ShapeDtypeStruct((B * 4 * ns, A_HEAD, A_HEAD), F32)],
        scratch_shapes=[pltpu.VMEM((A_HEAD, A_HEAD), F32), pltpu.VMEM((nch, A_HEAD, A_HEAD), BF16)],
        compiler_params=_cparams("parallel", "parallel", "arbitrary"),
    )(z, z, z, z, lb, onw)


def _hgrn_bwd(z, lb, onw, o_raw, s_start, doa, B, L, *, name):
    T = B * L
    ns = L // A_SUPER
    nch = A_SUPER // A_CHUNK

    def body(q_ref, f_ref, v_ref, g_ref, lb_ref, w_ref, o_ref, s_ref, doa_ref,
             dq_ref, df_ref, dv_ref, dg_ref, dw_ref, dlb_ref, dst_ref, sc_ref, dsc_ref, dqd_ref, dke_ref, dblx_ref):
        @pl.when(pl.program_id(2) == 0)
        def _():
            dst_ref[...] = jnp.zeros_like(dst_ref)
            dw_ref[...] = jnp.zeros_like(dw_ref)
            dlb_ref[...] = jnp.zeros_like(dlb_ref)

        same, tril, triu = _chunk_masks()
        q, v, g, lb, w = q_ref[...], v_ref[...], g_ref[...], lb_ref[...], w_ref[...]
        sig, f, lf, k = _hgrn_gates(f_ref[...], lb)
        bcs = _dot_mask(tril, lf)
        bl = _chunk_row(bcs, -1)
        eb, enb, eeb = jnp.exp(bcs), jnp.exp(-bcs), jnp.exp(bl - bcs)
        qd, ki, ke = q * eb, k * enb, k * eeb
        qdb, kib, keb, vb = qd.astype(BF16), ki.astype(BF16), ke.astype(BF16), v.astype(BF16)
        dec = jnp.exp(bl)
        o = o_ref[...]
        r = lax.rsqrt(jnp.mean(o * o, axis=-1, keepdims=True) + NORM_EPS)
        on = o * r
        sg = _sigmoid(g)
        silu_g = g * sg
        doa = doa_ref[...]
        dg_ref[...] = doa * on * w * (sg * (1.0 + g * (1.0 - sg)))
        dw_ref[0] += jnp.sum(doa * on * silu_g, axis=0, keepdims=True)
        don = doa * w * silu_g
        do = r * (don - on * jnp.mean(don * on, axis=-1, keepdims=True))
        dob = do.astype(BF16)
        a = jnp.where(tril, _dot(qdb, kib, _NT), 0.0).astype(BF16)
        da = jnp.where(tril, _dot(dob, vb, _NT), 0.0).astype(BF16)
        dv_ref[...] = _dot(a, dob, _TN)
        dqd_ref[...] = _dot(da, kib)
        dki = _dot(da, qdb, _TN)
        chunks = [slice(c * A_CHUNK, (c + 1) * A_CHUNK) for c in range(nch)]
        outer = [_dot(vb[rs], keb[rs], _TN) for rs in chunks]
        st = s_ref[0]
        for c in range(nch):
            sc_ref[c] = st
            st = st * dec[c * A_CHUNK:c * A_CHUNK + 1, :] + outer[c]
        outer_g = [_dot(dob[rs], qdb[rs], _TN) for rs in chunks]
        dst = dst_ref[...]
        for c in reversed(range(nch)):
            dsc_ref[c] = dst
            dst = dst * dec[c * A_CHUNK:c * A_CHUNK + 1, :] + outer_g[c]
        dst_ref[...] = dst
        for c, rs in enumerate(chunks):
            dec_c = dec[c * A_CHUNK:c * A_CHUNK + 1, :]
            dsc, stc = dsc_ref[c], sc_ref[c]
            dscb = dsc.astype(BF16)
            dv_ref[rs, :] += _dot(keb[rs], dscb, _NT)
            dke_ref[rs, :] = _dot(vb[rs], dscb)
            ddec = jnp.sum(dsc * stc, axis=0, keepdims=True)
            dqd_ref[rs, :] += _dot(dob[rs], stc.astype(BF16))
            dblx_ref[rs, :] = jnp.broadcast_to(ddec * dec_c, (A_CHUNK, A_HEAD))
        dqd, dke = dqd_ref[...], dke_ref[...]
        dq_ref[...] = dqd * eb
        keke = dke * ke
        db = dqd * qd - dki * ki - keke
        sums = _dot_mask(triu, jnp.concatenate([db, keke], axis=1))
        dk = dki * enb + dke * eeb
        dlf = sums[:, :A_HEAD] + _chunk_row(sums[:, A_HEAD:], 0) + dblx_ref[...]
        dff = dlf / f - dk
        df_ref[...] = dff * (1.0 - lb) * sig * (1.0 - sig)
        dlb_ref[0] += jnp.sum(dff * (1.0 - sig), axis=0, keepdims=True)

    def rev(n):
        return ns - 1 - n

    def zspec(off):
        return pl.BlockSpec((A_SUPER, A_HEAD), lambda b, h, n: (b * ns + rev(n), off + h))

    hvec = pl.BlockSpec((1, A_HEAD), lambda b, h, n: (0, h))
    ospec = pl.BlockSpec((A_SUPER, A_HEAD), lambda b, h, n: (b * ns + rev(n), h))
    acc = pl.BlockSpec((1, 1, A_HEAD), lambda b, h, n: (b * 4 + h, 0, 0))
    big = jax.ShapeDtypeStruct((T, A_WIDTH), F32)
    small = jax.ShapeDtypeStruct((B * 4, 1, A_HEAD), F32)
    return pl.pallas_call(
        body, name=name, grid=(B, 4, ns),
        in_specs=[zspec(0), zspec(4), zspec(8), zspec(12), hvec, hvec, ospec,
                  pl.BlockSpec((1, A_HEAD, A_HEAD), lambda b, h, n: ((b * 4 + h) * ns + rev(n), 0, 0)), ospec],
        out_specs=[ospec, ospec, ospec, ospec, acc, acc],
        out_shape=[big, big, big, big, small, small],
        scratch_shapes=[pltpu.VMEM((A_HEAD, A_HEAD), F32), pltpu.VMEM((nch, A_HEAD, A_HEAD), F32),
                        pltpu.VMEM((nch, A_HEAD, A_HEAD), F32),
                        pltpu.VMEM((A_SUPER, A_HEAD), F32), pltpu.VMEM((A_SUPER, A_HEAD), F32),
                        pltpu.VMEM((A_SUPER, A_HEAD), F32)],
        compiler_params=_cparams("parallel", "parallel", "arbitrary"),
    )(z, z, z, z, lb, onw, o_raw, s_start, doa)


def _rope_tables(L):
    half = A_HEAD // 2
    inv_freq = ROPE_THETA ** (-jnp.arange(half, dtype=F32) / half)
    ang = jnp.arange(L, dtype=F32)[:, None] * inv_freq[None, :]
    cos, sin = jnp.cos(ang), jnp.sin(ang)
    return jnp.concatenate([cos, cos], axis=-1), jnp.concatenate([-sin, sin], axis=-1)


def _rope_fwd(z, cos2, sin2, B, L, *, name):
    T = B * L
    tm = 512
    nl = L // tm

    def body(x_ref, c_ref, s_ref, q_ref, k_ref):
        c, s = c_ref[...], s_ref[...]
        for h in range(8):
            x = x_ref[:, h * A_HEAD:(h + 1) * A_HEAD]
            out = x * c + pltpu.roll(x, A_HEAD // 2, 1) * s
            o_ref = q_ref if h < 4 else k_ref
            o_ref[:, (h % 4) * A_HEAD:(h % 4 + 1) * A_HEAD] = out

    tab = pl.BlockSpec((tm, A_HEAD), lambda i: (i % nl, 0))
    out = pl.BlockSpec((tm, 512), lambda i: (i, 0))
    return pl.pallas_call(
        body, name=name, grid=(T // tm,),
        in_specs=[pl.BlockSpec((tm, 1024), lambda i: (i, 2)), tab, tab], out_specs=[out, out],
        out_shape=[jax.ShapeDtypeStruct((T, 512), F32)] * 2, compiler_params=_cparams("parallel"),
    )(z, cos2, sin2)


def _rope_bwd(dqs, dks, dvs, cos2, sin2, B, L, *, name):
    T = B * L
    tm = 256
    nl = L // tm

    def body(*refs):
        c, s = refs[9][...], refs[10][...]
        o_ref = refs[11]
        for part in range(3):
            a_ref, b_ref, c_ref = refs[3 * part:3 * part + 3]
            for h in range(4):
                cols = slice(h * A_HEAD, (h + 1) * A_HEAD)
                d = a_ref[:, cols] + b_ref[:, cols] + c_ref[:, cols]
                if part < 2:
                    d = d * c - pltpu.roll(d, A_HEAD // 2, 1) * s
                o_ref[:, part * 512 + h * A_HEAD:part * 512 + (h + 1) * A_HEAD] = d

    blk = pl.BlockSpec((tm, 512), lambda i: (i, 0))
    tab = pl.BlockSpec((tm, A_HEAD), lambda i: (i % nl, 0))
    return pl.pallas_call(
        body, name=name, grid=(T // tm,), in_specs=[blk] * 9 + [tab, tab],
        out_specs=pl.BlockSpec((tm, 1536), lambda i: (i, 0)),
        out_shape=jax.ShapeDtypeStruct((T, 1536), F32), compiler_params=_cparams("parallel"),
    )(*dqs, *dks, *dvs, cos2, sin2)


def _band_masks():
    i = lax.broadcasted_iota(jnp.int32, (B_SPAN, B_SPAN), 0)
    j = lax.broadcasted_iota(jnp.int32, (B_SPAN, B_SPAN), 1)
    return i <= j, j <= i


def _dil_geom(dil, L):
    return B_SPAN * dil, L // (B_SPAN * dil), (1 if dil > 1 else 4)


def _dil_rows(r, dil):
    return pl.ds(r, B_SPAN, stride=dil) if dil > 1 else pl.ds(0, B_SPAN)


def _dil_loop(dil, step):
    for r in range(dil):
        step(r, 0)


def _dil_specs(dil, B, L):
    W, nb, hb = _dil_geom(dil, L)
    cw = A_HEAD * hb

    def at(col0, shift):
        def index(b, n, hh):
            return (b * nb + jnp.clip(n + shift, 0, nb - 1), col0 // cw + hh)
        return pl.BlockSpec((W, cw), index)

    return at


def _dil_fwd(qr, kr, z, dil, B, L, *, name):
    T = B * L
    W, nb, hb = _dil_geom(dil, L)
    has_prev = nb > 1
    scale = A_HEAD ** -0.5
    at = _dil_specs(dil, B, L)

    def body(*refs):
        if has_prev:
            q_ref, kc_ref, vc_ref, kp_ref, vp_ref, o_ref, l_ref = refs
        else:
            q_ref, kc_ref, vc_ref, o_ref, l_ref = refs
        mp, mc = _band_masks()
        mp = mp & (pl.program_id(1) > 0)

        def step(r, carry):
            rows = _dil_rows(r, dil)
            for h in range(hb):
                cols = pl.ds(h * A_HEAD, A_HEAD)
                qh = q_ref[rows, cols].astype(BF16)
                sc = jnp.where(mc, _dot(qh, kc_ref[rows, cols].astype(BF16), _NT) * scale, NEG_BIG)
                m = jnp.max(sc, axis=-1, keepdims=True)
                if has_prev:
                    sp = jnp.where(mp, _dot(qh, kp_ref[rows, cols].astype(BF16), _NT) * scale, NEG_BIG)
                    m = jnp.maximum(m, jnp.max(sp, axis=-1, keepdims=True))
                pc = jnp.exp(sc - m)
                l = jnp.sum(pc, axis=-1, keepdims=True)
                o = _dot(pc.astype(BF16), vc_ref[rows, cols].astype(BF16))
                if has_prev:
                    pp = jnp.exp(sp - m)
                    l = l + jnp.sum(pp, axis=-1, keepdims=True)
                    o = o + _dot(pp.astype(BF16), vp_ref[rows, cols].astype(BF16))
                o_ref[rows, cols] = o / l
                l_ref[rows, cols] = jnp.broadcast_to(m + jnp.log(l), (B_SPAN, A_HEAD))
            return carry

        _dil_loop(dil, step)

    ins = [qr, kr, z] + ([kr, z] if has_prev else [])
    in_specs = [at(0, 0), at(0, 0), at(3072, 0)] + ([at(0, -1), at(3072, -1)] if has_prev else [])
    return pl.pallas_call(
        body, name=name, grid=(B, nb, 4 // hb), in_specs=in_specs, out_specs=[at(0, 0), at(0, 0)],
        out_shape=[jax.ShapeDtypeStruct((T, 512), F32)] * 2,
        compiler_params=_cparams("parallel", "parallel", "parallel"),
    )(*ins)


def _dil_combine(os_, ls_, *, name):
    T = os_[0].shape[0]
    tm = 256

    def body(o1, o2, o3, l1, l2, l3, ob_ref, lse_ref):
        a1, a2, a3 = l1[...], l2[...], l3[...]
        m = jnp.maximum(jnp.maximum(a1, a2), a3)
        e1, e2, e3 = jnp.exp(a1 - m), jnp.exp(a2 - m), jnp.exp(a3 - m)
        den = e1 + e2 + e3
        ob_ref[...] = (e1 * o1[...] + e2 * o2[...] + e3 * o3[...]) / den
        lse_ref[...] = m + jnp.log(den)

    blk = pl.BlockSpec((tm, 512), lambda i: (i, 0))
    return pl.pallas_call(
        body, name=name, grid=(T // tm,), in_specs=[blk] * 6, out_specs=[blk, blk],
        out_shape=[jax.ShapeDtypeStruct((T, 512), F32)] * 2, compiler_params=_cparams("parallel"),
    )(*[o.reshape(T, 512) for o in os_], *[l.reshape(T, 512) for l in ls_])


def _dil_bwd_q(qr, kr, z, dymix, out, lse, dil, B, L, *, name):
    T = B * L
    W, nb, hb = _dil_geom(dil, L)
    has_prev = nb > 1
    scale = A_HEAD ** -0.5
    at = _dil_specs(dil, B, L)

    def body(*refs):
        if has_prev:
            q_ref, kc_ref, vc_ref, do_ref, out_ref, lse_ref, kp_ref, vp_ref, dq_ref = refs
        else:
            q_ref, kc_ref, vc_ref, do_ref, out_ref, lse_ref, dq_ref = refs
        mp, mc = _band_masks()
        mp = mp & (pl.program_id(1) > 0)

        def step(r, carry):
            rows = _dil_rows(r, dil)
            for h in range(hb):
                cols = pl.ds(h * A_HEAD, A_HEAD)
                qh = q_ref[rows, cols].astype(BF16)
                do = do_ref[rows, cols]
                delta = jnp.sum(do * out_ref[rows, cols], axis=-1, keepdims=True)
                dob = do.astype(BF16)
                lse_h = lse_ref[rows, cols]
                kc = kc_ref[rows, cols].astype(BF16)
                pc = jnp.where(mc, jnp.exp(_dot(qh, kc, _NT) * scale - lse_h), 0.0)
                dsc = pc * (_dot(dob, vc_ref[rows, cols].astype(BF16), _NT) - delta) * scale
                dq = _dot(dsc.astype(BF16), kc)
                if has_prev:
                    kp = kp_ref[rows, cols].astype(BF16)
                    pp = jnp.where(mp, jnp.exp(_dot(qh, kp, _NT) * scale - lse_h), 0.0)
                    dsp = pp * (_dot(dob, vp_ref[rows, cols].astype(BF16), _NT) - delta) * scale
                    dq = dq + _dot(dsp.astype(BF16), kp)
                dq_ref[rows, cols] = dq
            return carry

        _dil_loop(dil, step)

    ins = [qr, kr, z, dymix, out, lse] + ([kr, z] if has_prev else [])
    in_specs = ([at(0, 0), at(0, 0), at(3072, 0), at(512, 0), at(0, 0), at(0, 0)]
                + ([at(0, -1), at(3072, -1)] if has_prev else []))
    return pl.pallas_call(
        body, name=name, grid=(B, nb, 4 // hb), in_specs=in_specs, out_specs=at(0, 0),
        out_shape=jax.ShapeDtypeStruct((T, 512), F32),
        compiler_params=_cparams("parallel", "parallel", "parallel"),
    )(*ins)


def _dil_bwd_kv(qr, kr, z, dymix, out, lse, dil, B, L, *, name):
    T = B * L
    W, nb, hb = _dil_geom(dil, L)
    has_next = nb > 1
    scale = A_HEAD ** -0.5
    at = _dil_specs(dil, B, L)

    def body(*refs):
        k_ref, v_ref = refs[0], refs[1]
        own = refs[2:6]
        nxt = refs[6:10] if has_next else None
        dk_ref, dv_ref = refs[-2], refs[-1]
        mp, mc = _band_masks()
        mp = mp & (pl.program_id(1) < nb - 1)
        groups = [(own, mc)] + ([(nxt, mp)] if has_next else [])

        def step(r, carry):
            rows = _dil_rows(r, dil)
            for h in range(hb):
                cols = pl.ds(h * A_HEAD, A_HEAD)
                kh, vh = k_ref[rows, cols].astype(BF16), v_ref[rows, cols].astype(BF16)
                dk = jnp.zeros((B_SPAN, A_HEAD), F32)
                dv = jnp.zeros((B_SPAN, A_HEAD), F32)
                for (q_ref, do_ref, out_ref, lse_ref), mask in groups:
                    qh = q_ref[rows, cols].astype(BF16)
                    do = do_ref[rows, cols]
                    delta = jnp.sum(do * out_ref[rows, cols], axis=-1, keepdims=True)
                    dob = do.astype(BF16)
                    p = jnp.where(mask, jnp.exp(_dot(qh, kh, _NT) * scale - lse_ref[rows, cols]), 0.0)
                    dv = dv + _dot(p.astype(BF16), dob, _TN)
                    ds = p * (_dot(dob, vh, _NT) - delta) * scale
                    dk = dk + _dot(ds.astype(BF16), qh, _TN)
                dk_ref[rows, cols] = dk
                dv_ref[rows, cols] = dv
            return carry

        _dil_loop(dil, step)

    ins = [kr, z, qr, dymix, out, lse] + ([qr, dymix, out, lse] if has_next else [])
    in_specs = ([at(0, 0), at(3072, 0), at(0, 0), at(512, 0), at(0, 0), at(0, 0)]
                + ([at(0, 1), at(512, 1), at(0, 1), at(0, 1)] if has_next else []))
    return pl.pallas_call(
        body, name=name, grid=(B, nb, 4 // hb), in_specs=in_specs, out_specs=[at(0, 0), at(0, 0)],
        out_shape=[jax.ShapeDtypeStruct((T, 512), F32)] * 2,
        compiler_params=_cparams("parallel", "parallel", "parallel"),
    )(*ins)


def _s5_build(lam_re, lam_im, log_dt, b_re, b_im, c_re, c_im):
    G, P, TC = C_GROUPS, C_STATE, C_TC
    lr = jnp.minimum(lam_re, C_MIN_NEG_RE)
    li = lam_im
    dt = jnp.exp(log_dt)[:, None]
    mag = jnp.exp(dt * lr)
    ar, ai = mag * jnp.cos(dt * li), mag * jnp.sin(dt * li)
    den = lr * lr + li * li
    zr = ((ar - 1.0) * lr + ai * li) / den
    zi = (ai * lr - (ar - 1.0) * li) / den
    bbr = zr[..., None] * b_re - zi[..., None] * b_im
    bbi = zr[..., None] * b_im + zi[..., None] * b_re
    ks = jnp.arange(TC + 1, dtype=F32)[:, None, None]
    pmag = jnp.exp(ks * (dt * lr)[None])
    pr, pi = pmag * jnp.cos(ks * (dt * li)[None]), pmag * jnp.sin(ks * (dt * li)[None])
    car = c_re[None] * pr[:, :, None, :] - c_im[None] * pi[:, :, None, :]
    cai = c_re[None] * pi[:, :, None, :] + c_im[None] * pr[:, :, None, :]
    kern = (jnp.einsum('lgop,gpc->lgco', car[:TC], bbr, precision=HI)
            - jnp.einsum('lgop,gpc->lgco', cai[:TC], bbi, precision=HI))
    eye = jnp.eye(C_GB, dtype=F32)
    kbd = (kern.reshape(TC, C_NB, C_GB, C_GROUP, C_GROUP)[:, :, :, :, None, :]
           * eye[None, None, :, None, :, None]).reshape(TC, C_NB, LANES, LANES)
    lag = jnp.arange(TC)[None, :] - jnp.arange(TC)[:, None]
    ksel = jnp.where((lag >= 0)[:, :, None, None, None], kbd[jnp.clip(lag, 0, TC - 1)], 0.0)
    m8 = ksel.transpose(2, 0, 3, 1, 4).reshape(C_NB, C_W8, C_W8)
    pr_e, pi_e = pr[TC - 1 - jnp.arange(TC)], pi[TC - 1 - jnp.arange(TC)]
    er = pr_e[:, :, :, None] * bbr[None] - pi_e[:, :, :, None] * bbi[None]
    ei = pr_e[:, :, :, None] * bbi[None] + pi_e[:, :, :, None] * bbr[None]
    ez = jnp.stack([er, ei], axis=2).reshape(TC, C_NB, C_GB, 2, P, C_GROUP).transpose(1, 0, 2, 5, 3, 4)
    e8 = (ez[:, :, :, :, :, None, :] * eye[None, None, :, None, None, :, None]).reshape(C_NB, C_W8, C_S8)
    fz = jnp.stack([car[1:], -cai[1:]], axis=0).reshape(2, TC, C_NB, C_GB, C_GROUP, P).transpose(2, 0, 3, 5, 1, 4)
    f8 = (fz[:, :, :, :, :, None, :] * eye[None, None, :, None, None, :, None]).reshape(C_NB, C_S8, C_W8)
    return m8, e8, f8, pr[TC], pi[TC]


C_NB = C_GROUPS * C_GROUP // LANES
C_GB = C_GROUPS // C_NB
C_W8 = C_TC * LANES
C_S8 = 2 * C_GB * C_STATE


def _s5_scan_tables(lam_re, lam_im, log_dt, nsteps):
    lr = jnp.minimum(lam_re, C_MIN_NEG_RE)
    dt = jnp.exp(log_dt)[:, None]
    ks = (C_TC * 2.0 ** jnp.arange(8, dtype=F32))[None, :, None]
    keep = (jnp.arange(8) < nsteps)[None, :, None]
    pmag = jnp.exp(ks * (dt * lr)[:, None, :])
    ang = ks * (dt * lam_im)[:, None, :]

    def blocks(t):
        return t.reshape(C_NB, C_GB, 8, C_STATE).transpose(0, 2, 1, 3).reshape(C_NB, 8, C_GB * C_STATE)

    pr = blocks(jnp.where(keep, pmag * jnp.cos(ang), 0.0))
    pi = blocks(jnp.where(keep, pmag * jnp.sin(ang), 0.0))
    return jnp.concatenate([pr, pr], axis=-1), jnp.concatenate([-pi, pi], axis=-1)


def _s5_rows(t, R):
    return pl.ds(t, R, stride=C_TC)


def _s5_fwd(u, dsk, m8, e8, f8, tab_r, tab_i, B, L, *, name):
    T = B * L
    R = L // C_TC
    nsteps = int(math.log2(R))

    def body(u_ref, d_ref, m_ref, e_ref, f_ref, tr_ref, ti_ref, gl_ref, y_ref, x8_ref, xs_ref):
        for t in range(C_TC):
            x8_ref[0, :, t * LANES:(t + 1) * LANES] = u_ref[_s5_rows(t, R), :].astype(BF16)
        x8 = x8_ref[0]
        x = _dot(x8, e_ref[0])
        row = lax.broadcasted_iota(jnp.int32, (R, C_S8), 0)
        for k in range(nsteps):
            s = 1 << k
            sh = pltpu.roll(x, s, 0)
            upd = tr_ref[0, k:k + 1, :] * sh + ti_ref[0, k:k + 1, :] * pltpu.roll(sh, C_S8 // 2, 1)
            x = x + jnp.where(row >= s, upd, 0.0)
        xs = jnp.where(row >= 1, pltpu.roll(x, 1, 0), 0.0)
        xs_ref[0] = xs
        y8 = _dot(x8, m_ref[0]) + _dot(xs.astype(BF16), f_ref[0])
        d = d_ref[...]
        for t in range(C_TC):
            rows = _s5_rows(t, R)
            y = y8[:, t * LANES:(t + 1) * LANES] + d * u_ref[rows, :]
            y_ref[rows, :] = y
            gl_ref[rows, :] = 0.5 * y * (1.0 + _erf(y * (2.0 ** -0.5)))

    tok = pl.BlockSpec((L, LANES), lambda c, b: (b, c))
    per_block = lambda shape: pl.BlockSpec((1,) + shape, lambda c, b: (c, 0, 0))
    per_step = lambda shape: pl.BlockSpec((1,) + shape, lambda c, b: (c * B + b, 0, 0))
    return pl.pallas_call(
        body, name=name, grid=(C_NB, B),
        in_specs=[tok, pl.BlockSpec((1, LANES), lambda c, b: (0, c)), per_block((C_W8, C_W8)), per_block((C_W8, C_S8)),
                  per_block((C_S8, C_W8)), per_block((8, C_S8)), per_block((8, C_S8))],
        out_specs=[tok, tok, per_step((R, C_W8)), per_step((R, C_S8))],
        out_shape=[jax.ShapeDtypeStruct((T, D_MODEL), F32), jax.ShapeDtypeStruct((T, D_MODEL), F32),
                   jax.ShapeDtypeStruct((C_NB * B, R, C_W8), BF16), jax.ShapeDtypeStruct((C_NB * B, R, C_S8), F32)],
        compiler_params=_cparams("parallel", "parallel"),
    )(u, dsk, m8, e8, f8, tab_r, tab_i)


def _s5_bwd(dgl, y, u, dsk, xs, m8, e8, f8, tab_r, tab_i, B, L, *, name):
    T = B * L
    R = L // C_TC
    nsteps = int(math.log2(R))

    def body(dgl_ref, y_ref, u_ref, d_ref, xs_ref, m_ref, e_ref, f_ref, tr_ref, ti_ref,
             du_ref, dy8_ref, de_ref, da_ref, dd_ref, dyf_ref):
        @pl.when(pl.program_id(1) == 0)
        def _():
            da_ref[...] = jnp.zeros_like(da_ref)
            dd_ref[...] = jnp.zeros_like(dd_ref)

        dd = jnp.zeros((1, LANES), F32)
        for t in range(C_TC):
            rows = _s5_rows(t, R)
            yv = y_ref[rows, :]
            cdf = 0.5 * (1.0 + _erf(yv * (2.0 ** -0.5)))
            pdf = jnp.exp(-0.5 * yv * yv) * (1.0 / math.sqrt(2.0 * math.pi))
            dy = dgl_ref[rows, :] * (cdf + yv * pdf)
            dd = dd + jnp.sum(dy * u_ref[rows, :], axis=0, keepdims=True)
            dyf_ref[:, t * LANES:(t + 1) * LANES] = dy
        dd_ref[...] += dd
        dy8 = dyf_ref[...].astype(BF16)
        dy8_ref[0] = dy8
        xs = xs_ref[0]
        gx = _dot(dy8, f_ref[0], _NT)
        row = lax.broadcasted_iota(jnp.int32, (R, C_S8), 0)
        for k in range(nsteps):
            s = 1 << k
            sh = pltpu.roll(gx, R - s, 0)
            upd = tr_ref[0, k:k + 1, :] * sh - ti_ref[0, k:k + 1, :] * pltpu.roll(sh, C_S8 // 2, 1)
            gx = gx + jnp.where(row + s < R, upd, 0.0)
        de_in = jnp.where(row + 1 < R, pltpu.roll(gx, R - 1, 0), 0.0)
        deb = de_in.astype(BF16)
        de_ref[0] = deb
        da_ref[0, 0:1, :] += jnp.sum(de_in * xs, axis=0, keepdims=True)
        da_ref[0, 1:2, :] += jnp.sum(de_in * pltpu.roll(xs, C_S8 // 2, 1), axis=0, keepdims=True)
        dx8 = _dot(dy8, m_ref[0], _NT) + _dot(deb, e_ref[0], _NT)
        d = d_ref[...]
        for t in range(C_TC):
            cols = slice(t * LANES, (t + 1) * LANES)
            du_ref[_s5_rows(t, R), :] = dx8[:, cols] + d * dyf_ref[:, cols]

    tok = pl.BlockSpec((L, LANES), lambda c, b: (b, c))
    vec = pl.BlockSpec((1, LANES), lambda c, b: (0, c))
    per_block = lambda shape: pl.BlockSpec((1,) + shape, lambda c, b: (c, 0, 0))
    per_step = lambda shape: pl.BlockSpec((1,) + shape, lambda c, b: (c * B + b, 0, 0))
    return pl.pallas_call(
        body, name=name, grid=(C_NB, B),
        in_specs=[tok, tok, tok, vec, per_step((R, C_S8)), per_block((C_W8, C_W8)),
                  per_block((C_W8, C_S8)), per_block((C_S8, C_W8)), per_block((8, C_S8)), per_block((8, C_S8))],
        out_specs=[tok, per_step((R, C_W8)), per_step((R, C_S8)), per_block((8, C_S8)), vec],
        out_shape=[jax.ShapeDtypeStruct((T, D_MODEL), F32), jax.ShapeDtypeStruct((C_NB * B, R, C_W8), BF16),
                   jax.ShapeDtypeStruct((C_NB * B, R, C_S8), BF16), jax.ShapeDtypeStruct((C_NB, 8, C_S8), F32),
                   jax.ShapeDtypeStruct((1, D_MODEL), F32)],
        scratch_shapes=[pltpu.VMEM((R, C_W8), F32)],
        compiler_params=_cparams("parallel", "arbitrary"),
    )(dgl, y, u, dsk, xs, m8, e8, f8, tab_r, tab_i)


def _bmm_tn(a, b, nb, *, name):
    a = a.reshape(nb, -1, a.shape[-1])
    b = b.reshape(nb, -1, b.shape[-1])
    K, M, N = a.shape[1], a.shape[2], b.shape[2]

    def body(a_ref, b_ref, o_ref):
        o_ref[0] = _dot(a_ref[0].astype(BF16), b_ref[0].astype(BF16), _TN)

    return pl.pallas_call(
        body, name=name, grid=(nb,),
        in_specs=[pl.BlockSpec((1, K, M), lambda c: (c, 0, 0)), pl.BlockSpec((1, K, N), lambda c: (c, 0, 0))],
        out_specs=pl.BlockSpec((1, M, N), lambda c: (c, 0, 0)),
        out_shape=jax.ShapeDtypeStruct((nb, M, N), F32), compiler_params=_cparams("parallel"),
    )(a, b)


def _loss_head(y, target, *, name):
    T, C = y.shape
    tm = 256

    def body(y_ref, t_ref, l_ref, d_ref):
        err = y_ref[...] - t_ref[...]
        d_ref[...] = err * (1.0 / C)
        sq = err * err
        part = jnp.zeros((8, LANES), F32)
        for r in range(0, tm, 8):
            for c in range(0, C, LANES):
                part = part + sq[r:r + 8, c:c + LANES]

        @pl.when(pl.program_id(0) == 0)
        def _():
            l_ref[...] = part

        @pl.when(pl.program_id(0) > 0)
        def _():
            l_ref[...] += part

    row = pl.BlockSpec((tm, C), lambda i: (i, 0))
    acc = pl.BlockSpec((8, LANES), lambda i: (0, 0))
    return pl.pallas_call(
        body, name=name, grid=(T // tm,), in_specs=[row, row], out_specs=[acc, row],
        out_shape=[jax.ShapeDtypeStruct((8, LANES), F32), jax.ShapeDtypeStruct((T, C), F32)],
        compiler_params=_cparams("arbitrary"),
    )(y, target)


def _adamw(w, g, m, v, *, name):
    shape = w.shape
    size = int(np.prod(shape))
    cols = LANES if (shape[-1] < LANES and size % LANES == 0) else shape[-1]
    rows = size // cols
    tm = _tile(rows, 256) if rows % 8 == 0 else rows
    w2, g2, m2, v2 = (t.reshape(rows, cols) for t in (w, g, m, v))

    def body(w_ref, g_ref, m_ref, v_ref, d_ref, nm_ref, nv_ref):
        gg = g_ref[...]
        nm = ADAM_B1 * m_ref[...] + (1.0 - ADAM_B1) * gg
        nv = ADAM_B2 * v_ref[...] + (1.0 - ADAM_B2) * (gg * gg)
        m_hat = nm / (1.0 - ADAM_B1 ** ADAM_STEP)
        v_hat = nv / (1.0 - ADAM_B2 ** ADAM_STEP)
        d_ref[...] = -ADAM_LR * (m_hat / (jnp.sqrt(v_hat) + ADAM_EPS) + ADAM_WD * w_ref[...])
        nm_ref[...] = nm
        nv_ref[...] = nv

    blk = pl.BlockSpec((tm, cols), lambda i: (i, 0))
    outs = pl.pallas_call(
        body, name=name, grid=(rows // tm,), in_specs=[blk] * 4, out_specs=[blk] * 3,
        out_shape=[jax.ShapeDtypeStruct((rows, cols), F32)] * 3, compiler_params=_cparams("parallel"),
    )(w2, g2, m2, v2)
    return tuple(o.reshape(shape) for o in outs)


def _all_gather(shard, *, name):
    R, C = shard.shape

    def body(x_ref, out_ref, send_sems, recv_sems, local_sem):
        x, y, c = lax.axis_index("x"), lax.axis_index("y"), lax.axis_index("c")
        me, sibling = (x, y, c), (x, y, 1 - c)
        chips = [(1 - x, y), (x, 1 - y), (1 - x, 1 - y)]

        def rows(px, py, pc):
            return out_ref.at[_logical(px, py, pc)]

        def copy(k, block, to, src=None):
            return pltpu.make_async_remote_copy(
                src_ref=rows(*block) if src is None else src, dst_ref=rows(*block),
                send_sem=send_sems.at[k], recv_sem=recv_sems.at[k], device_id=to, device_id_type=_MESH)

        mine = pltpu.make_async_copy(x_ref, rows(*me), local_sem)
        mine.start()
        first = [copy(0, me, sibling, src=x_ref)]
        first += [copy(1 + j, me, (*chip, c), src=x_ref) for j, chip in enumerate(chips)]
        for cp in first:
            cp.start()
        passed = [copy(4 + j, (*chip, c), sibling) for j, chip in enumerate(chips)]
        for j, chip in enumerate(chips):
            copy(1 + j, (*chip, c), me).wait_recv()
            passed[j].start()
        copy(0, sibling, me).wait_recv()
        for j, chip in enumerate(chips):
            copy(4 + j, (*chip, 1 - c), me).wait_recv()
        for cp in first + passed:
            cp.wait_send()
        mine.wait()

    return pl.pallas_call(
        body, name=name, out_shape=jax.ShapeDtypeStruct((N_DEV, R, C), shard.dtype),
        in_specs=[_HBM], out_specs=_HBM,
        scratch_shapes=[pltpu.SemaphoreType.DMA((7,)), pltpu.SemaphoreType.DMA((7,)), pltpu.SemaphoreType.DMA],
    )(shard)


def _gather_weights(shards, *, name):
    nt = len(shards)

    def body(*refs):
        ins, outs = refs[:nt], refs[nt:2 * nt]
        send_sems, recv_sems, local_sems = refs[2 * nt:]
        x, y, c = lax.axis_index("x"), lax.axis_index("y"), lax.axis_index("c")
        me, sibling = (x, y, c), (x, y, 1 - c)
        chips = [(1 - x, y), (x, 1 - y), (1 - x, 1 - y)]

        def copy(t, k, block, to, src=None):
            rows = outs[t].at[_logical(*block)]
            return pltpu.make_async_remote_copy(
                src_ref=rows if src is None else src, dst_ref=rows,
                send_sem=send_sems.at[t, k], recv_sem=recv_sems.at[t, k], device_id=to, device_id_type=_MESH)

        mine = [pltpu.make_async_copy(ins[t], outs[t].at[_logical(*me)], local_sems.at[t]) for t in range(nt)]
        for cp in mine:
            cp.start()
        started = []
        for t in range(nt):
            started.append(copy(t, 0, me, sibling, src=ins[t]))
            started += [copy(t, 1 + j, me, (*chip, c), src=ins[t]) for j, chip in enumerate(chips)]
        for cp in started:
            cp.start()
        for j, chip in enumerate(chips):
            for t in range(nt):
                copy(t, 1 + j, (*chip, c), me).wait_recv()
                fwd = copy(t, 4 + j, (*chip, c), sibling)
                fwd.start()
                started.append(fwd)
        for t in range(nt):
            copy(t, 0, sibling, me).wait_recv()
            for j, chip in enumerate(chips):
                copy(t, 4 + j, (*chip, 1 - c), me).wait_recv()
        for cp in started:
            cp.wait_send()
        for cp in mine:
            cp.wait()

    return pl.pallas_call(
        body, name=name, out_shape=[jax.ShapeDtypeStruct((N_DEV,) + s.shape, s.dtype) for s in shards],
        in_specs=[_HBM] * nt, out_specs=[_HBM] * nt,
        scratch_shapes=[pltpu.SemaphoreType.DMA((nt, 7)), pltpu.SemaphoreType.DMA((nt, 7)), pltpu.SemaphoreType.DMA((nt,))],
    )(*shards)


def _sum_rows(stacked, *, name):
    _, R, C = stacked.shape
    tr = R
    if N_DEV * R * C * stacked.dtype.itemsize > 12 * 1024 * 1024:
        for cand in range(512, 15, -16):
            if R % cand == 0:
                tr = cand
                break

    def body(s_ref, o_ref):
        acc = s_ref[0].astype(F32)
        for k in range(1, N_DEV):
            acc = acc + s_ref[k].astype(F32)
        o_ref[...] = acc

    return pl.pallas_call(
        body, name=name, grid=(R // tr,),
        in_specs=[pl.BlockSpec((N_DEV, tr, C), lambda i: (0, i, 0))], out_specs=pl.BlockSpec((tr, C), lambda i: (i, 0)),
        out_shape=jax.ShapeDtypeStruct((R, C), F32), compiler_params=_cparams("parallel"),
    )(stacked)


_LARGE = (("ab_w_in", 1, True), ("ab_w_out", 1, False), ("s5_w_glu", 1, True), ("xattn_wq", 2, False),
          ("xattn_wkv", 2, True), ("xattn_wo", 2, False), ("ffn_w_in", 2, True), ("ffn_w_out", 2, False))
_LARGE_KEYS = tuple((n, l) for n, layers, _ in _LARGE for l in range(layers))
_TRANSPOSED = {n: t for n, _, t in _LARGE}


def _owner_major(name, w):
    return w.T if _TRANSPOSED[name] else w


def _lb_from_logits(logits):
    return jnp.cumsum(jax.nn.softmax(logits, axis=0), axis=0)[0:1]


def _local_step(x, mem, target, W, shards=None):
    B, L, _ = x.shape
    T = B * L
    x0 = x.reshape(T, D_MODEL)
    memf = mem.reshape(B * MEM_LEN, D_MODEL)
    nw = W["norm_w"]
    cos2, sin2 = _rope_tables(L)
    W = dict(W)
    G, received = {}, {}

    def mmx(a, b, gather=(), scatter=(), **kw):
        if shards is None or not (gather or scatter):
            return _mm(a, b, **kw)
        out, gathered, got = _mm(a, b, gather=[shards[k] for k in gather],
                                 scatter=[G[k].reshape(N_DEV, -1, D_MODEL) for k in scatter], **kw)
        for k, g in zip(gather, gathered):
            W[k] = g.reshape(-1, D_MODEL)
        for k, r in zip(scatter, got):
            received[k] = r
        return out

    def vec(v):
        return v.reshape(1, -1)

    saved = []
    xin = x0
    for layer in range(2):
        s = {"x0": xin}
        tag = f"l{layer}"
        h1 = _rms_fwd(xin, vec(nw[layer, 0]), name=f"norm_pre_mix_{tag}", out_dtype=BF16 if layer == 0 else F32)
        s["h1"] = h1
        if layer == 0:
            lb, lb_vjp = jax.vjp(_lb_from_logits, W["hgrn_lb_logits"])
            onw = W["hgrn_out_norm_w"].reshape(1, A_WIDTH)
            z = mmx(h1, W["ab_w_in", 0], tb=True, name="ab_in",
                    gather=[("ab_w_out", 0), ("xattn_wq", 0), ("xattn_wkv", 0), ("xattn_wo", 0), ("ffn_w_in", 0)])
            oa, o_raw, s_start = _hgrn_fwd(z, lb, onw, B, L, name="hgrn_fwd")
            qr, kr = _rope_fwd(z, cos2, sin2, B, L, name="rope_qk")
            os_, ls_ = [], []
            for dil in B_DILS:
                o_g, l_g = _dil_fwd(qr, kr, z, dil, B, L, name=f"dil_fwd_{dil}")
                os_.append(o_g)
                ls_.append(l_g)
            ob, lse = _dil_combine(os_, ls_, name="dil_combine")
            ymix = jnp.concatenate([oa, ob], axis=-1).astype(BF16)
            y1 = mmx(ymix, W["ab_w_out", 0], out_dtype=BF16, name="ab_out", gather=[("ffn_w_out", 0)])
            s.update(z=z, lb=lb, lb_vjp=lb_vjp, onw=onw, o_raw=o_raw, s_start=s_start, qr=qr, kr=kr, ob=ob, lse=lse, ymix=ymix)
        else:
            p5 = tuple(W[n][0] for n in ("s5_lambda_re", "s5_lambda_im", "s5_log_dt", "s5_b_re", "s5_b_im", "s5_c_re", "s5_c_im"))
            (m8, e8, f8, _, _), s5_vjp = jax.vjp(_s5_build, *p5)
            tab_r, tab_i = _s5_scan_tables(p5[0], p5[1], p5[2], int(math.log2(L // C_TC)))
            mats = (m8.astype(BF16), e8.astype(BF16), f8.astype(BF16), tab_r, tab_i)
            dsk = W["s5_d"].reshape(1, D_MODEL)
            gl, ypre, x8, xs = _s5_fwd(h1, dsk, *mats, B, L, name="s5_fwd")
            w_glu = _interleave_rows(W["s5_w_glu", 0])
            zg, y1 = _mm(gl, w_glu, tb=True, name="s5_glu_in", gate=("fwd", "glu"))
            s.update(s5_vjp=s5_vjp, mats=mats, x8=x8, xs=xs, dsk=dsk, gl=gl, ypre=ypre, zg=zg, w_glu=w_glu)
        x1 = _rms_fwd(y1, vec(nw[layer, 1]), xin, name=f"norm_post_mix_{tag}")
        h2 = _rms_fwd(x1, vec(nw[layer, 2]), name=f"norm_pre_x_{tag}", out_dtype=BF16)
        memn = _rms_fwd(memf, vec(W["mem_norm_w"][layer]), name=f"norm_mem_{tag}", out_dtype=BF16)
        q = mmx(h2, W["xattn_wq", layer], out_dtype=BF16, name=f"x_q_{tag}", gather=[("s5_w_glu", 0)] if layer == 0 else [])
        kv = _mm(memn, W["xattn_wkv", layer], tb=True, out_dtype=BF16, name=f"x_kv_{tag}")
        o = _xattn_fwd(q, kv, B, L, name=f"x_attn_{tag}")
        y2 = mmx(o, W["xattn_wo", layer], out_dtype=BF16, name=f"x_o_{tag}", gather=[("xattn_wq", 1), ("xattn_wo", 1)] if layer == 0 else [])
        x2 = _rms_fwd(y2, vec(nw[layer, 3]), x1, name=f"norm_post_x_{tag}")
        h3 = _rms_fwd(x2, vec(nw[layer, 4]), name=f"norm_pre_ffn_{tag}", out_dtype=BF16)
        w_ffn_in = _interleave_rows(W["ffn_w_in", layer])
        zf, u = mmx(h3, w_ffn_in, tb=True, name=f"ffn_in_{tag}", gate=("fwd", "swiglu"),
                    gather=[("xattn_wkv", 1), ("ffn_w_in", 1), ("ffn_w_out", 1)] if layer == 0 else [])
        y3 = _mm(u, W["ffn_w_out", layer], out_dtype=BF16, name=f"ffn_out_{tag}")
        x3 = _rms_fwd(y3, vec(nw[layer, 5]), x2, name=f"norm_post_ffn_{tag}")
        s.update(y1=y1, x1=x1, h2=h2, memn=memn, q=q, kv=kv, o=o, y2=y2, x2=x2, h3=h3, zf=zf, u=u, y3=y3, w_ffn_in=w_ffn_in)
        saved.append(s)
        xin = x3

    loss_parts, dx = _loss_head(xin, target.reshape(T, D_MODEL), name="loss_head")

    d_norm = [[None] * 6 for _ in range(2)]
    d_memn = [None, None]
    for layer in (1, 0):
        s = saved[layer]
        tag = f"l{layer}"
        dy3, d_norm[layer][5] = _rms_bwd(s["y3"], vec(nw[layer, 5]), dx, name=f"bnorm_post_ffn_{tag}", out_dtype=BF16)
        dzf = mmx(dy3, W["ffn_w_out", layer], tb=True, name=f"b_ffn_out_dx_{tag}", gate=("bwd", "swiglu", s["zf"]),
                  scatter=[("xattn_wkv", 1), ("s5_w_glu", 0)] if layer == 0 else [])
        G["ffn_w_out", layer] = _mm(s["u"], dy3, ta=True, out_dtype=BF16, name=f"b_ffn_out_dw_{tag}")
        G["ffn_w_in", layer] = _interleave_rows(
            mmx(dzf, s["h3"], ta=True, out_dtype=BF16, name=f"b_ffn_in_dw_{tag}", scatter=[("ffn_w_out", layer)]), inverse=True)
        dh3 = mmx(dzf, s["w_ffn_in"], out_dtype=BF16, name=f"b_ffn_in_dx_{tag}", scatter=[("ffn_w_in", layer)])
        dx, d_norm[layer][4] = _rms_bwd(s["x2"], vec(nw[layer, 4]), dh3, dx, name=f"bnorm_pre_ffn_{tag}")
        dy2, d_norm[layer][3] = _rms_bwd(s["y2"], vec(nw[layer, 3]), dx, name=f"bnorm_post_x_{tag}", out_dtype=BF16)
        do = _mm(dy2, W["xattn_wo", layer], tb=True, out_dtype=BF16, name=f"b_x_o_dx_{tag}")
        G["xattn_wo", layer] = _mm(s["o"], dy2, ta=True, out_dtype=BF16, name=f"b_x_o_dw_{tag}")
        dq, dkv = _xattn_bwd(s["q"], s["kv"], do, B, L, name=f"b_x_attn_{tag}")
        G["xattn_wq", layer] = _mm(s["h2"], dq, ta=True, out_dtype=BF16, name=f"b_x_q_dw_{tag}")
        dh2 = _mm(dq, W["xattn_wq", layer], tb=True, out_dtype=BF16, name=f"b_x_q_dx_{tag}")
        G["xattn_wkv", layer] = _mm(dkv, s["memn"], ta=True, out_dtype=BF16, name=f"b_x_kv_dw_{tag}")
        dmemn = _mm(dkv, W["xattn_wkv", layer], out_dtype=BF16, name=f"b_x_kv_dx_{tag}")
        _, d_memn[layer] = _rms_bwd(memf, vec(W["mem_norm_w"][layer]), dmemn, name=f"bnorm_mem_{tag}", out_dtype=BF16)
        dx, d_norm[layer][2] = _rms_bwd(s["x1"], vec(nw[layer, 2]), dh2, dx, name=f"bnorm_pre_x_{tag}")
        dy1, d_norm[layer][1] = _rms_bwd(s["y1"], vec(nw[layer, 1]), dx, name=f"bnorm_post_mix_{tag}", out_dtype=BF16)
        if layer == 0:
            z = s["z"]
            dymix = _mm(dy1, W["ab_w_out", 0], tb=True, name="b_ab_out_dx")
            G["ab_w_out", 0] = _mm(s["ymix"], dy1, ta=True, out_dtype=BF16, name="b_ab_out_dw")
            dqa, dfa, dia, dga, d_onw, d_lb = _hgrn_bwd(z, s["lb"], s["onw"], s["o_raw"], s["s_start"], dymix, B, L, name="hgrn_bwd")
            dqs, dks, dvs = [], [], []
            for dil in B_DILS:
                dqs.append(_dil_bwd_q(s["qr"], s["kr"], z, dymix, s["ob"], s["lse"], dil, B, L, name=f"dil_bwd_q_{dil}"))
                dk_g, dv_g = _dil_bwd_kv(s["qr"], s["kr"], z, dymix, s["ob"], s["lse"], dil, B, L, name=f"dil_bwd_kv_{dil}")
                dks.append(dk_g)
                dvs.append(dv_g)
            dqkv = _rope_bwd(dqs, dks, dvs, cos2, sin2, B, L, name="b_rope")
            dz = jnp.concatenate([dqa, dfa, dia, dga, dqkv], axis=-1).astype(BF16)
            G["ab_w_in", 0] = mmx(dz, s["h1"], ta=True, out_dtype=BF16, name="b_ab_in_dw",
                                  scatter=[("xattn_wo", 0), ("xattn_wq", 0), ("xattn_wkv", 0), ("ab_w_out", 0)])
            dh1 = mmx(dz, W["ab_w_in", 0], out_dtype=BF16, name="b_ab_in_dx", scatter=[("ab_w_in", 0)])
            G["hgrn_out_norm_w"] = jnp.sum(d_onw.reshape(B, A_WIDTH), axis=0, keepdims=True)
            d_lb_row = jnp.sum(d_lb.reshape(B, A_WIDTH), axis=0, keepdims=True)
            G["hgrn_lb_logits"] = s["lb_vjp"](d_lb_row)[0]
        else:
            dzg = _gated_bwd(s["zg"], dy1, "glu", name="b_s5_glu")
            G["s5_w_glu", 0] = _interleave_rows(
                _mm(dzg, s["gl"], ta=True, out_dtype=BF16, name="b_s5_glu_dw"), inverse=True)
            dgl = mmx(dzg, s["w_glu"], name="b_s5_glu_dx", scatter=[("xattn_wo", 1), ("xattn_wq", 1)])
            dh1, dy8, de_in, da, d_dsk = _s5_bwd(dgl, s["ypre"], s["h1"], s["dsk"], s["xs"], *s["mats"], B, L, name="s5_bwd")
            dm8 = _bmm_tn(s["x8"], dy8, C_NB, name="s5_bwd_dm")
            df8 = _bmm_tn(s["xs"], dy8, C_NB, name="s5_bwd_df")
            de8 = _bmm_tn(s["x8"], de_in, C_NB, name="s5_bwd_de")
            half = C_S8 // 2
            da_r = (da[:, 0, :half] + da[:, 0, half:]).reshape(C_GROUPS, C_STATE)
            da_i = (da[:, 1, half:] - da[:, 1, :half]).reshape(C_GROUPS, C_STATE)
            gp = s["s5_vjp"]((dm8, de8, df8, da_r, da_i))
            for n, gv in zip(("s5_lambda_re", "s5_lambda_im", "s5_log_dt", "s5_b_re", "s5_b_im", "s5_c_re", "s5_c_im"), gp):
                G[n] = gv[None]
            G["s5_d"] = d_dsk
        dx, d_norm[layer][0] = _rms_bwd(s["x0"], vec(nw[layer, 0]), dh1, dx, name=f"bnorm_pre_mix_{tag}")

    G["norm_w"] = jnp.stack([jnp.concatenate(d_norm[l], axis=0) for l in range(2)])
    G["mem_norm_w"] = jnp.concatenate(d_memn, axis=0)
    if shards is not None:
        G.update(received)
    return loss_parts, dx.reshape(B, L, D_MODEL), G


_SMALL = (("norm_w", (2, 6, 1024)), ("mem_norm_w", (2, 1024)), ("hgrn_lb_logits", (3, 512)), ("hgrn_out_norm_w", (1, 512)),
          ("s5_lambda_re", (1, 64, 64)), ("s5_lambda_im", (1, 64, 64)), ("s5_log_dt", (1, 64)),
          ("s5_b_re", (1, 64, 64, 16)), ("s5_b_im", (1, 64, 64, 16)), ("s5_c_re", (1, 64, 16, 64)),
          ("s5_c_im", (1, 64, 16, 64)), ("s5_d", (1, 1024)))

_WEIGHT_ORDER = ('norm_w', 'mem_norm_w', 'ab_w_in', 'ab_w_out', 'hgrn_lb_logits', 'hgrn_out_norm_w', 's5_lambda_re',
                 's5_lambda_im', 's5_log_dt', 's5_b_re', 's5_b_im', 's5_c_re', 's5_c_im', 's5_d', 's5_w_glu', 'xattn_wq',
                 'xattn_wkv', 'xattn_wo', 'ffn_w_in', 'ffn_w_out')


def kernel(x, mem, norm_w, mem_norm_w, ab_w_in, ab_w_out, hgrn_lb_logits, hgrn_out_norm_w, s5_lambda_re, s5_lambda_im, s5_log_dt, s5_b_re, s5_b_im, s5_c_re, s5_c_im, s5_d, s5_w_glu, xattn_wq, xattn_wkv, xattn_wo, ffn_w_in, ffn_w_out, loss_target, m_norm_w, m_mem_norm_w, m_ab_w_in, m_ab_w_out, m_hgrn_lb_logits, m_hgrn_out_norm_w, m_s5_lambda_re, m_s5_lambda_im, m_s5_log_dt, m_s5_b_re, m_s5_b_im, m_s5_c_re, m_s5_c_im, m_s5_d, m_s5_w_glu, m_xattn_wq, m_xattn_wkv, m_xattn_wo, m_ffn_w_in, m_ffn_w_out, v_norm_w, v_mem_norm_w, v_ab_w_in, v_ab_w_out, v_hgrn_lb_logits, v_hgrn_out_norm_w, v_s5_lambda_re, v_s5_lambda_im, v_s5_log_dt, v_s5_b_re, v_s5_b_im, v_s5_c_re, v_s5_c_im, v_s5_d, v_s5_w_glu, v_xattn_wq, v_xattn_wkv, v_xattn_wo, v_ffn_w_in, v_ffn_w_out):
    local = dict(norm_w=norm_w, mem_norm_w=mem_norm_w, ab_w_in=ab_w_in, ab_w_out=ab_w_out, hgrn_lb_logits=hgrn_lb_logits,
                 hgrn_out_norm_w=hgrn_out_norm_w, s5_lambda_re=s5_lambda_re, s5_lambda_im=s5_lambda_im, s5_log_dt=s5_log_dt,
                 s5_b_re=s5_b_re, s5_b_im=s5_b_im, s5_c_re=s5_c_re, s5_c_im=s5_c_im, s5_d=s5_d, s5_w_glu=s5_w_glu,
                 xattn_wq=xattn_wq, xattn_wkv=xattn_wkv, xattn_wo=xattn_wo, ffn_w_in=ffn_w_in, ffn_w_out=ffn_w_out)
    mom_m = dict(zip(_WEIGHT_ORDER, (m_norm_w, m_mem_norm_w, m_ab_w_in, m_ab_w_out, m_hgrn_lb_logits, m_hgrn_out_norm_w, m_s5_lambda_re, m_s5_lambda_im, m_s5_log_dt, m_s5_b_re, m_s5_b_im, m_s5_c_re, m_s5_c_im, m_s5_d, m_s5_w_glu, m_xattn_wq, m_xattn_wkv, m_xattn_wo, m_ffn_w_in, m_ffn_w_out)))
    mom_v = dict(zip(_WEIGHT_ORDER, (v_norm_w, v_mem_norm_w, v_ab_w_in, v_ab_w_out, v_hgrn_lb_logits, v_hgrn_out_norm_w, v_s5_lambda_re, v_s5_lambda_im, v_s5_log_dt, v_s5_b_re, v_s5_b_im, v_s5_c_re, v_s5_c_im, v_s5_d, v_s5_w_glu, v_xattn_wq, v_xattn_wkv, v_xattn_wo, v_ffn_w_in, v_ffn_w_out)))
    dev = 4 * lax.axis_index("x") + 2 * lax.axis_index("y") + lax.axis_index("c")

    shards = {(n, l): _owner_major(n, local[n][l]).astype(BF16) for n, l in _LARGE_KEYS}
    first = ("ab_w_in", 0)
    W = {first: _gather_weights([shards[first]], name="gather_first")[0].reshape(-1, D_MODEL)}
    tiny =jnp.concatenate([norm_w.reshape(-1), s5_d.reshape(-1)])
    tiny = jnp.pad(tiny, (0, 16 * LANES - tiny.shape[0])).reshape(16, LANES)
    tiny_all = _all_gather(tiny, name="gather_tiny").reshape(N_DEV, 16 * LANES)
    W["norm_w"] = tiny_all[:, :12 * LANES].reshape(N_DEV, 2, 6, LANES).transpose(1, 2, 0, 3).reshape(2, 6, D_MODEL)
    W["s5_d"] = tiny_all[:, 12 * LANES:13 * LANES].reshape(1, D_MODEL)
    for n in ("mem_norm_w", "hgrn_lb_logits", "hgrn_out_norm_w", "s5_lambda_re", "s5_lambda_im", "s5_log_dt",
              "s5_b_re", "s5_b_im", "s5_c_re", "s5_c_im"):
        W[n] = local[n]

    loss_parts, grad_x, G = _local_step(x, mem, loss_target, W, shards)

    g_layers = {}
    for n, l in _LARGE_KEYS:
        g = _sum_rows(G[n, l], name=f"sum_grads_{n}_{l}")
        g_layers.setdefault(n, []).append(g.T if _TRANSPOSED[n] else g)
    g_local = {n: jnp.stack(gl) for n, gl in g_layers.items()}
    small =jnp.concatenate([G[n].reshape(-1) for n, _ in _SMALL] + [0.5 / D_MODEL * jnp.sum(loss_parts).reshape(1)])
    n_small = small.shape[0]
    small = jnp.pad(small, (0, (-n_small) % (8 * LANES))).reshape(-1, LANES)
    small_sum = _sum_rows(_all_gather(small, name="gather_small"), name="sum_small").reshape(-1)
    g_full, off = {}, 0
    for n, shp in _SMALL:
        size = int(np.prod(shp))
        g_full[n] = small_sum[off:off + size].reshape(shp)
        off += size
    loss = small_sum[off]
    grads = dict(g_local)
    for n, shp in _SMALL:
        if n == "norm_w":
            grads[n] = lax.dynamic_slice_in_dim(g_full[n], dev * LANES, LANES, axis=2)
        elif n == "s5_d":
            grads[n] = lax.dynamic_slice_in_dim(g_full[n], dev * LANES, LANES, axis=1)
        else:
            grads[n] = g_full[n]

    delta, new_m, new_v = {}, {}, {}
    for n in _WEIGHT_ORDER:
        delta[n], new_m[n], new_v[n] = _adamw(local[n], grads[n], mom_m[n], mom_v[n], name=f"adamw_{n}")
    return (loss, grad_x, *[grads[n] for n in _WEIGHT_ORDER], *[delta[n] for n in _WEIGHT_ORDER],
            *[new_m[n] for n in _WEIGHT_ORDER], *[new_v[n] for n in _WEIGHT_ORDER])
```

```python
import functools
import math

import numpy as np
import jax
import jax.numpy as jnp
from jax import lax
from jax.experimental import pallas as pl
from jax.experimental.pallas import tpu as pltpu

F32 = jnp.float32
BF16 = jnp.bfloat16
HI = lax.Precision.HIGHEST

D_MODEL = 1024
NORM_EPS = 1e-6
A_WIDTH = 512
A_HEAD = 128
A_CHUNK = 32
A_SUPER = 256
B_SPAN = 128
B_DILS = (1, 4, 16)
ROPE_THETA = 10000.0
C_GROUPS = 64
C_GROUP = 16
C_STATE = 64
C_TC = 8
C_MIN_NEG_RE = -1e-4
MEM_LEN = 256
X_HEADS = 4
X_HD = 256
D_FF = 2816
N_DEV = 8
LANES = 128

ADAM_LR, ADAM_B1, ADAM_B2, ADAM_EPS, ADAM_WD, ADAM_STEP = 0.001, 0.9, 0.999, 1e-08, 0.01, 10

NEG_BIG = -1e30


def _tile(n, pref):
    for d in range(min(pref, n) // LANES * LANES, 0, -LANES):
        if n % d == 0:
            return d
    return n


def _cparams(*sem):
    return pltpu.CompilerParams(dimension_semantics=sem, vmem_limit_bytes=56 * 1024 * 1024)


def _sigmoid(x):
    return 0.5 * jnp.tanh(0.5 * x) + 0.5


def _erf(x):
    ax = jnp.abs(x)
    t = 1.0 / (1.0 + 0.3275911 * ax)
    poly = t * (0.254829592 + t * (-0.284496736 + t * (1.421413741 + t * (-1.453152027 + t * 1.061405429))))
    y = 1.0 - poly * jnp.exp(-ax * ax)
    return jnp.where(x < 0, -y, y)


_HBM = pl.BlockSpec(memory_space=pltpu.HBM)
_MESH = pl.DeviceIdType.MESH


def _logical(px, py, pc):
    return 4 * px + 2 * py + pc


class _Exchange:
    def __init__(self, gather=(), scatter=()):
        self.gather, self.scatter = list(gather), list(scatter)
        self.ng, self.n = len(self.gather), len(self.gather) + len(self.scatter)

    def operands(self):
        return self.gather + self.scatter

    def in_specs(self):
        return [_HBM] * self.n

    def out_shapes(self):
        return ([jax.ShapeDtypeStruct((N_DEV,) + g.shape, g.dtype) for g in self.gather]
                + [jax.ShapeDtypeStruct(s.shape, s.dtype) for s in self.scatter])

    def scratch(self):
        if not self.n:
            return []
        return [pltpu.SemaphoreType.DMA((self.n, 7)), pltpu.SemaphoreType.DMA((self.n, 7)), pltpu.SemaphoreType.DMA((self.n,))]

    def split(self, results):
        return list(results[:self.ng]), list(results[self.ng:])

    def run(self, ins, outs, sems, first, last):
        if not self.n:
            return
        send_sems, recv_sems, local_sems = sems
        x, y, c = lax.axis_index("x"), lax.axis_index("y"), lax.axis_index("c")
        me, sibling = _logical(x, y, c), (x, y, 1 - c)
        chips = [(1 - x, y), (x, 1 - y), (1 - x, 1 - y)]
        peers = [(x ^ (k >> 2), y ^ ((k >> 1) & 1), c ^ (k & 1)) for k in range(1, N_DEV)]

        def remote(t, k, src, dst, to):
            return pltpu.make_async_remote_copy(src_ref=src, dst_ref=dst, send_sem=send_sems.at[t, k],
                                                recv_sem=recv_sems.at[t, k], device_id=to, device_id_type=_MESH)

        def local(t):
            src = ins[t] if t < self.ng else ins[t].at[me]
            return pltpu.make_async_copy(src, outs[t].at[me], local_sems.at[t])

        @pl.when(first)
        def _():
            for t in range(self.n):
                local(t).start()
                if t < self.ng:
                    remote(t, 0, ins[t], outs[t].at[me], sibling).start()
                    for j, chip in enumerate(chips):
                        remote(t, 1 + j, ins[t], outs[t].at[me], (*chip, c)).start()
                else:
                    for k, peer in enumerate(peers):
                        remote(t, k, ins[t].at[_logical(*peer)], outs[t].at[me], peer).start()

        @pl.when(last)
        def _():
            for j, chip in enumerate(chips):
                for t in range(self.ng):
                    landed = outs[t].at[_logical(*chip, c)]
                    remote(t, 1 + j, ins[t], landed, sibling).wait_recv()
                    remote(t, 4 + j, landed, landed, sibling).start()
            for t in range(self.n):
                if t < self.ng:
                    remote(t, 0, ins[t], outs[t].at[_logical(*sibling)], sibling).wait_recv()
                    for j, chip in enumerate(chips):
                        remote(t, 4 + j, ins[t], outs[t].at[_logical(*chip, 1 - c)], sibling).wait_recv()
                    for k in range(7):
                        remote(t, k, ins[t], outs[t].at[me], sibling).wait_send()
                else:
                    for k, peer in enumerate(peers):
                        remote(t, k, ins[t].at[me], outs[t].at[_logical(*peer)], peer).wait_recv()
                    for k, peer in enumerate(peers):
                        remote(t, k, ins[t].at[_logical(*peer)], outs[t].at[me], peer).wait_send()
                local(t).wait()


_MM_VMEM_BUDGET = 36 * 1024 * 1024


def _mm(a, b, *, ta=False, tb=False, out_dtype=F32, name, tiles=(1408, 1408, 4096), gather=(), scatter=(), gate=None):
    M, K = (a.shape[1], a.shape[0]) if ta else a.shape
    N = b.shape[0] if tb else b.shape[1]
    assert (b.shape[1] if tb else b.shape[0]) == K
    gate_mode = gate[0] if gate else None
    tm, tn, tk = _tile(M, tiles[0]), _tile(N, tiles[1]), _tile(K, tiles[2])
    if gate_mode == "fwd":
        bs = _gate_block(N // 2)
        tn = 2 * bs
    elif gate_mode == "bwd":
        bs = _gate_block(N)
        tn = bs

    def vmem_bytes():
        acc = 4 * tm * tn if tk < K else 0
        if gate_mode == "fwd":
            io = (2 * (2 + 1) + 4) * tm * tn
        elif gate_mode == "bwd":
            io = (2 * (4 + 4) + 4) * tm * tn
        else:
            io = 2 * tm * tn * jnp.dtype(out_dtype).itemsize
        return 2 * 2 * (tm * tk + tk * tn) + acc + io

    while vmem_bytes() > _MM_VMEM_BUDGET:
        if gate_mode and tm > 256:
            tm = _tile(M, tm - LANES)
        elif tk > 512:
            tk = _tile(K, tk - LANES)
        elif tn > 256 and not gate_mode:
            tn = _tile(N, tn - LANES)
        else:
            tm = _tile(M, tm - LANES)
    ni, nj, nk = M // tm, N // tn, K // tk
    ex = _Exchange(gather, scatter)

    a_spec = pl.BlockSpec((tk, tm), lambda i, j, k: (k, i)) if ta else pl.BlockSpec((tm, tk), lambda i, j, k: (i, k))
    b_spec = pl.BlockSpec((tn, tk), lambda i, j, k: (j, k)) if tb else pl.BlockSpec((tk, tn), lambda i, j, k: (k, j))
    dims = (((0 if ta else 1,), (1 if tb else 0,)), ((), ()))
    n_acc = 1 if nk > 1 else 0
    n_in = 3 if gate_mode == "bwd" else 2
    n_out = 2 if gate_mode == "fwd" else 1

    def finish(val, in_refs, out_refs):
        if gate_mode is None:
            out_refs[0][...] = val.astype(out_refs[0].dtype)
        elif gate_mode == "fwd":
            out_refs[0][...] = val.astype(BF16)
            for p in range(tn // (2 * bs)):
                a_, b_ = val[:, 2 * p * bs:(2 * p + 1) * bs], val[:, (2 * p + 1) * bs:(2 * p + 2) * bs]
                out_refs[1][:, p * bs:(p + 1) * bs] = _gate_value(a_, b_, gate[1]).astype(BF16)
        else:
            z_ref = in_refs[2]
            for p in range(tn // bs):
                a_ = z_ref[:, 2 * p * bs:(2 * p + 1) * bs].astype(F32)
                b_ = z_ref[:, (2 * p + 1) * bs:(2 * p + 2) * bs].astype(F32)
                da, db = _gate_grads(a_, b_, val[:, p * bs:(p + 1) * bs], gate[1])
                out_refs[0][:, 2 * p * bs:(2 * p + 1) * bs] = da.astype(BF16)
                out_refs[0][:, (2 * p + 1) * bs:(2 * p + 2) * bs] = db.astype(BF16)

    def body(*refs):
        in_refs, rest = refs[:n_in], refs[n_in:]
        ex_in, out_refs = rest[:ex.n], rest[ex.n:ex.n + n_out]
        ex_out, scratch = rest[ex.n + n_out:2 * ex.n + n_out], rest[2 * ex.n + n_out:]
        i, j, k = pl.program_id(0), pl.program_id(1), pl.program_id(2)
        ex.run(ex_in, ex_out, scratch[n_acc:], (i == 0) & (j == 0) & (k == 0), (i == ni - 1) & (j == nj - 1) & (k == nk - 1))
        part = lax.dot_general(in_refs[0][...].astype(BF16), in_refs[1][...].astype(BF16), dims, preferred_element_type=F32)
        if nk == 1:
            finish(part, in_refs, out_refs)
            return
        acc_ref = scratch[0]

        @pl.when(k == 0)
        def _():
            acc_ref[...] = part

        @pl.when(k > 0)
        def _():
            acc_ref[...] += part

        @pl.when(k == nk - 1)
        def _():
            finish(acc_ref[...], in_refs, out_refs)

    tile = lambda width: pl.BlockSpec((tm, width), lambda i, j, k: (i, j))
    if gate_mode == "fwd":
        out_specs, out_shape = [tile(tn), tile(tn // 2)], [jax.ShapeDtypeStruct((M, N), BF16), jax.ShapeDtypeStruct((M, N // 2), BF16)]
    elif gate_mode == "bwd":
        out_specs, out_shape = [tile(2 * tn)], [jax.ShapeDtypeStruct((M, 2 * N), BF16)]
    else:
        out_specs, out_shape = [tile(tn)], [jax.ShapeDtypeStruct((M, N), out_dtype)]
    operands = [a, b] + ([gate[2]] if gate_mode == "bwd" else [])
    sem = ("arbitrary",) * 3 if ex.n else ("parallel", "parallel", "arbitrary")
    res = pl.pallas_call(
        body, name=name, grid=(ni, nj, nk),
        in_specs=[a_spec, b_spec] + ([tile(2 * tn)] if gate_mode == "bwd" else []) + ex.in_specs(),
        out_specs=out_specs + ex.in_specs(),
        out_shape=out_shape + ex.out_shapes(),
        scratch_shapes=([pltpu.VMEM((tm, tn), F32)] if nk > 1 else []) + ex.scratch(),
        compiler_params=_cparams(*sem),
    )(*operands, *ex.operands())
    main = res[0] if n_out == 1 else tuple(res[:n_out])
    if not ex.n:
        return main
    return (main,) + tuple(ex.split(res[n_out:]))


def _rms_fwd(x, w, res=None, *, name, out_dtype=F32, then=None):
    T, C = x.shape
    tm = _tile(T, 512)
    has_res = res is not None

    def norm(v, w_ref):
        return v * lax.rsqrt(jnp.mean(v * v, axis=-1, keepdims=True) + NORM_EPS) * w_ref[...]

    def body(*refs):
        x_ref, w_ref = refs[0], refs[1]
        y = norm(x_ref[...].astype(F32), w_ref)
        if has_res:
            y = y + refs[2][...]
        if then is None:
            refs[-1][...] = y.astype(refs[-1].dtype)
        else:
            refs[-2][...] = y.astype(refs[-2].dtype)
            refs[-1][...] = norm(y, refs[-3]).astype(refs[-1].dtype)

    row = pl.BlockSpec((tm, C), lambda i: (i, 0))
    vec = pl.BlockSpec((1, C), lambda i: (0, 0))
    ins = [x, w] + ([res] if has_res else []) + ([then[0]] if then else [])
    in_specs = [row, vec] + ([row] if has_res else []) + ([vec] if then else [])
    out_shape = [jax.ShapeDtypeStruct((T, C), out_dtype)] + ([jax.ShapeDtypeStruct((T, C), then[1])] if then else [])
    res_ = pl.pallas_call(
        body, name=name, grid=(T // tm,), in_specs=in_specs, out_specs=[row] * len(out_shape),
        out_shape=out_shape, compiler_params=_cparams("parallel"),
    )(*ins)
    return tuple(res_) if then else res_[0]


def _rms_bwd(x, w, dy, add=None, *, name, out_dtype=F32):
    T, C = x.shape
    tm = _tile(T, 512)
    has_add = add is not None

    def body(*refs):
        x_ref, w_ref, dy_ref = refs[:3]
        dx_ref, dw_ref = refs[-2], refs[-1]
        xv = x_ref[...].astype(F32)
        r = lax.rsqrt(jnp.mean(xv * xv, axis=-1, keepdims=True) + NORM_EPS)
        xh = xv * r
        g = dy_ref[...].astype(F32)
        part = jnp.sum(g * xh, axis=0, keepdims=True)

        @pl.when(pl.program_id(0) == 0)
        def _():
            dw_ref[...] = part

        @pl.when(pl.program_id(0) > 0)
        def _():
            dw_ref[...] += part

        gx = g * w_ref[...]
        dx = r * (gx - xh * jnp.mean(gx * xh, axis=-1, keepdims=True))
        if has_add:
            dx = dx + refs[3][...]
        dx_ref[...] = dx.astype(dx_ref.dtype)

    row = pl.BlockSpec((tm, C), lambda i: (i, 0))
    vec = pl.BlockSpec((1, C), lambda i: (0, 0))
    ins = [x, w, dy] + ([add] if has_add else [])
    return pl.pallas_call(
        body, name=name, grid=(T // tm,), in_specs=[row, vec, row] + ([row] if has_add else []),
        out_specs=[row, vec],
        out_shape=[jax.ShapeDtypeStruct((T, C), out_dtype), jax.ShapeDtypeStruct((1, C), F32)],
        compiler_params=_cparams("arbitrary"),
    )(*ins)


def _gate_block(width):
    return _tile(width, 1408)


def _gate_value(a, b, kind):
    return a * _sigmoid(a) * b if kind == "swiglu" else a * _sigmoid(b)


def _gate_grads(a, b, d, kind):
    if kind == "swiglu":
        s = _sigmoid(a)
        return d * b * (s * (1.0 + a * (1.0 - s))), d * a * s
    s = _sigmoid(b)
    return d * s, d * a * s * (1.0 - s)


def _interleave_rows(w, inverse=False):
    W2, C = w.shape
    bs = _gate_block(W2 // 2)
    nb = W2 // 2 // bs
    shape = (nb, 2, bs, C) if inverse else (2, nb, bs, C)
    return w.reshape(shape).transpose(1, 0, 2, 3).reshape(W2, C)


def _gated_bwd(z, dout, kind, *, name):
    T, W2 = z.shape
    W = W2 // 2
    tm, bs = _tile(T, 512), _gate_block(W)

    def body(z_ref, d_ref, o_ref):
        da, db = _gate_grads(z_ref[:, :bs].astype(F32), z_ref[:, bs:].astype(F32), d_ref[...].astype(F32), kind)
        o_ref[:, :bs] = da.astype(o_ref.dtype)
        o_ref[:, bs:] = db.astype(o_ref.dtype)

    return pl.pallas_call(
        body, name=name, grid=(T // tm, W // bs),
        in_specs=[pl.BlockSpec((tm, 2 * bs), lambda i, j: (i, j)), pl.BlockSpec((tm, bs), lambda i, j: (i, j))],
        out_specs=pl.BlockSpec((tm, 2 * bs), lambda i, j: (i, j)),
        out_shape=jax.ShapeDtypeStruct((T, W2), BF16), compiler_params=_cparams("parallel", "parallel"),
    )(z, dout)


_NT = (((1,), (1,)), ((), ()))
_TN = (((0,), (0,)), ((), ()))


def _dot(a, b, dims=None, precision=None):
    if dims is None:
        return jnp.dot(a, b, preferred_element_type=F32, precision=precision)
    return lax.dot_general(a, b, dims, preferred_element_type=F32, precision=precision)


def _xattn_fwd(q, kv, B, L, *, name):
    T = q.shape[0]
    tq = 256
    nq = L // tq
    scale = X_HD ** -0.5

    def body(q_ref, k_ref, v_ref, o_ref):
        heads = [slice(h * X_HD, (h + 1) * X_HD) for h in range(X_HEADS)]
        s = [_dot(q_ref[:, sl].astype(BF16), k_ref[:, sl].astype(BF16), _NT) * scale for sl in heads]
        m = [jnp.max(a, axis=-1, keepdims=True) for a in s]
        p = [jnp.exp(a - b) for a, b in zip(s, m)]
        l = [jnp.sum(a, axis=-1, keepdims=True) for a in p]
        o = [_dot(a.astype(BF16), v_ref[:, sl].astype(BF16)) for a, sl in zip(p, heads)]
        for sl, a, b in zip(heads, o, l):
            o_ref[:, sl] = (a / b).astype(BF16)

    return pl.pallas_call(
        body, name=name, grid=(B, nq),
        in_specs=[pl.BlockSpec((tq, D_MODEL), lambda b, i: (b * nq + i, 0)),
                  pl.BlockSpec((MEM_LEN, D_MODEL), lambda b, i: (b, 0)),
                  pl.BlockSpec((MEM_LEN, D_MODEL), lambda b, i: (b, 1))],
        out_specs=pl.BlockSpec((tq, D_MODEL), lambda b, i: (b * nq + i, 0)),
        out_shape=jax.ShapeDtypeStruct((T, D_MODEL), BF16), compiler_params=_cparams("parallel", "parallel"),
    )(q, kv, kv)


def _xattn_bwd(q, kv, do, B, L, *, name):
    T = q.shape[0]
    tq = 256
    nq = L // tq
    scale = X_HD ** -0.5

    def body(q_ref, k_ref, v_ref, do_ref, dq_ref, dkv_ref):
        @pl.when(pl.program_id(1) == 0)
        def _():
            dkv_ref[...] = jnp.zeros_like(dkv_ref)

        heads = [slice(h * X_HD, (h + 1) * X_HD) for h in range(X_HEADS)]
        qs = [q_ref[:, sl].astype(BF16) for sl in heads]
        ks = [k_ref[:, sl].astype(BF16) for sl in heads]
        dos = [do_ref[:, sl].astype(BF16) for sl in heads]
        s = [_dot(a, b, _NT) * scale for a, b in zip(qs, ks)]
        dp = [_dot(a, v_ref[:, sl].astype(BF16), _NT) for a, sl in zip(dos, heads)]
        e = [jnp.exp(a - jnp.max(a, axis=-1, keepdims=True)) for a in s]
        p = [a / jnp.sum(a, axis=-1, keepdims=True) for a in e]
        ds = [(a * (b - jnp.sum(b * a, axis=-1, keepdims=True)) * scale).astype(BF16) for a, b in zip(p, dp)]
        dv = [_dot(a.astype(BF16), b, _TN) for a, b in zip(p, dos)]
        dq = [_dot(a, b) for a, b in zip(ds, ks)]
        dk = [_dot(a, b, _TN) for a, b in zip(ds, qs)]
        for h, sl in enumerate(heads):
            dq_ref[:, sl] = dq[h].astype(BF16)
            dkv_ref[:, sl] += dk[h]
            dkv_ref[:, D_MODEL + h * X_HD:D_MODEL + (h + 1) * X_HD] += dv[h]

    return pl.pallas_call(
        body, name=name, grid=(B, nq),
        in_specs=[pl.BlockSpec((tq, D_MODEL), lambda b, i: (b * nq + i, 0)),
                  pl.BlockSpec((MEM_LEN, D_MODEL), lambda b, i: (b, 0)),
                  pl.BlockSpec((MEM_LEN, D_MODEL), lambda b, i: (b, 1)),
                  pl.BlockSpec((tq, D_MODEL), lambda b, i: (b * nq + i, 0))],
        out_specs=[pl.BlockSpec((tq, D_MODEL), lambda b, i: (b * nq + i, 0)),
                   pl.BlockSpec((MEM_LEN, 2 * D_MODEL), lambda b, i: (b, 0))],
        out_shape=[jax.ShapeDtypeStruct((T, D_MODEL), BF16), jax.ShapeDtypeStruct((B * MEM_LEN, 2 * D_MODEL), F32)],
        compiler_params=_cparams("parallel", "arbitrary"),
    )(q, kv, kv, do)


def _chunk_masks():
    row = lax.broadcasted_iota(jnp.int32, (A_SUPER, A_SUPER), 0)
    col = lax.broadcasted_iota(jnp.int32, (A_SUPER, A_SUPER), 1)
    same = jnp.right_shift(row, 5) == jnp.right_shift(col, 5)
    return same, same & (col <= row), same & (col >= row)


def _dot_mask(mask, x):
    m = mask.astype(BF16)
    hi = x.astype(BF16)
    rest = x - hi.astype(F32)
    mid = rest.astype(BF16)
    lo = (rest - mid.astype(F32)).astype(BF16)
    return _dot(m, hi) + _dot(m, mid) + _dot(m, lo)


def _chunk_row(x, which):
    rows = [x[c * A_CHUNK + which % A_CHUNK:c * A_CHUNK + which % A_CHUNK + 1, :] for c in range(A_SUPER // A_CHUNK)]
    return jnp.concatenate([jnp.broadcast_to(r, (A_CHUNK, x.shape[1])) for r in rows], axis=0)


def _hgrn_gates(fa, lb):
    sig = _sigmoid(fa)
    f = lb + (1.0 - lb) * sig
    return sig, f, jnp.log(f), 1.0 - f


def _hgrn_fwd(z, lb, onw, B, L, *, name):
    T = B * L
    ns = L // A_SUPER
    nch = A_SUPER // A_CHUNK

    def body(q_ref, f_ref, v_ref, g_ref, lb_ref, w_ref, oa_ref, o_ref, s_ref, st_ref, sc_ref):
        @pl.when(pl.program_id(2) == 0)
        def _():
            st_ref[...] = jnp.zeros_like(st_ref)

        s_ref[0] = st_ref[...]
        same, tril, _ = _chunk_masks()
        q, v = q_ref[...], v_ref[...]
        _, _, lf, k = _hgrn_gates(f_ref[...], lb_ref[...])
        bcs = _dot_mask(tril, lf)
        bl = _chunk_row(bcs, -1)
        qd = (q * jnp.exp(bcs)).astype(BF16)
        ki = (k * jnp.exp(-bcs)).astype(BF16)
        ke = (k * jnp.exp(bl - bcs)).astype(BF16)
        dec = jnp.exp(bl)
        vb = v.astype(BF16)
        a = jnp.where(tril, _dot(qd, ki, _NT), 0.0)
        o_ref[...] = _dot(a.astype(BF16), vb)
        chunks = [slice(c * A_CHUNK, (c + 1) * A_CHUNK) for c in range(nch)]
        outer = [_dot(vb[rs], ke[rs], _TN) for rs in chunks]
        st = st_ref[...]
        for c, rs in enumerate(chunks):
            sc_ref[c] = st.astype(BF16)
            st = st * dec[c * A_CHUNK:c * A_CHUNK + 1, :] + outer[c]
        st_ref[...] = st
        for c, rs in enumerate(chunks):
            o_ref[rs, :] += _dot(qd[rs], sc_ref[c], _NT)
        o = o_ref[...]
        r = lax.rsqrt(jnp.mean(o * o, axis=-1, keepdims=True) + NORM_EPS)
        g = g_ref[...]
        oa_ref[...] = o * r * w_ref[...] * (g * _sigmoid(g))

    def zspec(off):
        return pl.BlockSpec((A_SUPER, A_HEAD), lambda b, h, n: (b * ns + n, off + h))

    hvec = pl.BlockSpec((1, A_HEAD), lambda b, h, n: (0, h))
    ospec = pl.BlockSpec((A_SUPER, A_HEAD), lambda b, h, n: (b * ns + n, h))
    return pl.pallas_call(
        body, name=name, grid=(B, 4, ns),
        in_specs=[zspec(0), zspec(4), zspec(8), zspec(12), hvec, hvec],
        out_specs=[ospec, ospec, pl.BlockSpec((1, A_HEAD, A_HEAD), lambda b, h, n: ((b * 4 + h) * ns + n, 0, 0))],
        out_shape=[jax.ShapeDtypeStruct((T, A_WIDTH), F32), jax.ShapeDtypeStruct((T, A_WIDTH), F32),
                   jax.ShapeDtypeStruct((B * 4 * ns, A_HEAD, A_HEAD), F32)],
        scratch_shapes=[pltpu.VMEM((A_HEAD, A_HEAD), F32), pltpu.VMEM((nch, A_HEAD, A_HEAD), BF16)],
        compiler_params=_cparams("parallel", "parallel", "arbitrary"),
    )(z, z, z, z, lb, onw)


def _hgrn_bwd(z, lb, onw, o_raw, s_start, doa, B, L, *, name):
    T = B * L
    ns = L // A_SUPER
    nch = A_SUPER // A_CHUNK

    def body(q_ref, f_ref, v_ref, g_ref, lb_ref, w_ref, o_ref, s_ref, doa_ref,
             dq_ref, df_ref, dv_ref, dg_ref, dw_ref, dlb_ref, dst_ref, sc_ref, dsc_ref, dqd_ref, dke_ref, dblx_ref):
        @pl.when(pl.program_id(2) == 0)
        def _():
            dst_ref[...] = jnp.zeros_like(dst_ref)
            dw_ref[...] = jnp.zeros_like(dw_ref)
            dlb_ref[...] = jnp.zeros_like(dlb_ref)

        same, tril, triu = _chunk_masks()
        q, v, g, lb, w = q_ref[...], v_ref[...], g_ref[...], lb_ref[...], w_ref[...]
        sig, f, lf, k = _hgrn_gates(f_ref[...], lb)
        bcs = _dot_mask(tril, lf)
        bl = _chunk_row(bcs, -1)
        eb, enb, eeb = jnp.exp(bcs), jnp.exp(-bcs), jnp.exp(bl - bcs)
        qd, ki, ke = q * eb, k * enb, k * eeb
        qdb, kib, keb, vb = qd.astype(BF16), ki.astype(BF16), ke.astype(BF16), v.astype(BF16)
        dec = jnp.exp(bl)
        o = o_ref[...]
        r = lax.rsqrt(jnp.mean(o * o, axis=-1, keepdims=True) + NORM_EPS)
        on = o * r
        sg = _sigmoid(g)
        silu_g = g * sg
        doa = doa_ref[...]
        dg_ref[...] = doa * on * w * (sg * (1.0 + g * (1.0 - sg)))
        dw_ref[0] += jnp.sum(doa * on * silu_g, axis=0, keepdims=True)
        don = doa * w * silu_g
        do = r * (don - on * jnp.mean(don * on, axis=-1, keepdims=True))
        dob = do.astype(BF16)
        a = jnp.where(tril, _dot(qdb, kib, _NT), 0.0).astype(BF16)
        da = jnp.where(tril, _dot(dob, vb, _NT), 0.0).astype(BF16)
        dv_ref[...] = _dot(a, dob, _TN)
        dqd_ref[...] = _dot(da, kib)
        dki = _dot(da, qdb, _TN)
        chunks = [slice(c * A_CHUNK, (c + 1) * A_CHUNK) for c in range(nch)]
        outer = [_dot(vb[rs], keb[rs], _TN) for rs in chunks]
        st = s_ref[0]
        for c in range(nch):
            sc_ref[c] = st
            st = st * dec[c * A_CHUNK:c * A_CHUNK + 1, :] + outer[c]
        outer_g = [_dot(dob[rs], qdb[rs], _TN) for rs in chunks]
        dst = dst_ref[...]
        for c in reversed(range(nch)):
            dsc_ref[c] = dst
            dst = dst * dec[c * A_CHUNK:c * A_CHUNK + 1, :] + outer_g[c]
        dst_ref[...] = dst
        for c, rs in enumerate(chunks):
            dec_c = dec[c * A_CHUNK:c * A_CHUNK + 1, :]
            dsc, stc = dsc_ref[c], sc_ref[c]
            dscb = dsc.astype(BF16)
            dv_ref[rs, :] += _dot(keb[rs], dscb, _NT)
            dke_ref[rs, :] = _dot(vb[rs], dscb)
            ddec = jnp.sum(dsc * stc, axis=0, keepdims=True)
            dqd_ref[rs, :] += _dot(dob[rs], stc.astype(BF16))
            dblx_ref[rs, :] = jnp.broadcast_to(ddec * dec_c, (A_CHUNK, A_HEAD))
        dqd, dke = dqd_ref[...], dke_ref[...]
        dq_ref[...] = dqd * eb
        keke = dke * ke
        db = dqd * qd - dki * ki - keke
        sums = _dot_mask(triu, jnp.concatenate([db, keke], axis=1))
        dk = dki * enb + dke * eeb
        dlf = sums[:, :A_HEAD] + _chunk_row(sums[:, A_HEAD:], 0) + dblx_ref[...]
        dff = dlf / f - dk
        df_ref[...] = dff * (1.0 - lb) * sig * (1.0 - sig)
        dlb_ref[0] += jnp.sum(dff * (1.0 - sig), axis=0, keepdims=True)

    def rev(n):
        return ns - 1 - n

    def zspec(off):
        return pl.BlockSpec((A_SUPER, A_HEAD), lambda b, h, n: (b * ns + rev(n), off + h))

    hvec = pl.BlockSpec((1, A_HEAD), lambda b, h, n: (0, h))
    ospec = pl.BlockSpec((A_SUPER, A_HEAD), lambda b, h, n: (b * ns + rev(n), h))
    acc = pl.BlockSpec((1, 1, A_HEAD), lambda b, h, n: (b * 4 + h, 0, 0))
    big = jax.ShapeDtypeStruct((T, A_WIDTH), F32)
    small = jax.ShapeDtypeStruct((B * 4, 1, A_HEAD), F32)
    return pl.pallas_call(
        body, name=name, grid=(B, 4, ns),
        in_specs=[zspec(0), zspec(4), zspec(8), zspec(12), hvec, hvec, ospec,
                  pl.BlockSpec((1, A_HEAD, A_HEAD), lambda b, h, n: ((b * 4 + h) * ns + rev(n), 0, 0)), ospec],
        out_specs=[ospec, ospec, ospec, ospec, acc, acc],
        out_shape=[big, big, big, big, small, small],
        scratch_shapes=[pltpu.VMEM((A_HEAD, A_HEAD), F32), pltpu.VMEM((nch, A_HEAD, A_HEAD), F32),
                        pltpu.VMEM((nch, A_HEAD, A_HEAD), F32),
                        pltpu.VMEM((A_SUPER, A_HEAD), F32), pltpu.VMEM((A_SUPER, A_HEAD), F32),
                        pltpu.VMEM((A_SUPER, A_HEAD), F32)],
        compiler_params=_cparams("parallel", "parallel", "arbitrary"),
    )(z, z, z, z, lb, onw, o_raw, s_start, doa)


def _rope_tables(L):
    half = A_HEAD // 2
    inv_freq = ROPE_THETA ** (-jnp.arange(half, dtype=F32) / half)
    ang = jnp.arange(L, dtype=F32)[:, None] * inv_freq[None, :]
    cos, sin = jnp.cos(ang), jnp.sin(ang)
    return jnp.concatenate([cos, cos], axis=-1), jnp.concatenate([-sin, sin], axis=-1)


def _rope_fwd(z, cos2, sin2, B, L, *, name):
    T = B * L
    tm = 512
    nl = L // tm

    def body(x_ref, c_ref, s_ref, q_ref, k_ref):
        c, s = c_ref[...], s_ref[...]
        for h in range(8):
            x = x_ref[:, h * A_HEAD:(h + 1) * A_HEAD]
            out = x * c + pltpu.roll(x, A_HEAD // 2, 1) * s
            o_ref = q_ref if h < 4 else k_ref
            o_ref[:, (h % 4) * A_HEAD:(h % 4 + 1) * A_HEAD] = out

    tab = pl.BlockSpec((tm, A_HEAD), lambda i: (i % nl, 0))
    out = pl.BlockSpec((tm, 512), lambda i: (i, 0))
    return pl.pallas_call(
        body, name=name, grid=(T // tm,),
        in_specs=[pl.BlockSpec((tm, 1024), lambda i: (i, 2)), tab, tab], out_specs=[out, out],
        out_shape=[jax.ShapeDtypeStruct((T, 512), F32)] * 2, compiler_params=_cparams("parallel"),
    )(z, cos2, sin2)


def _rope_bwd(dqs, dks, dvs, cos2, sin2, B, L, *, name):
    T = B * L
    tm = 256
    nl = L // tm

    def body(*refs):
        c, s = refs[9][...], refs[10][...]
        o_ref = refs[11]
        for part in range(3):
            a_ref, b_ref, c_ref = refs[3 * part:3 * part + 3]
            for h in range(4):
                cols = slice(h * A_HEAD, (h + 1) * A_HEAD)
                d = a_ref[:, cols] + b_ref[:, cols] + c_ref[:, cols]
                if part < 2:
                    d = d * c - pltpu.roll(d, A_HEAD // 2, 1) * s
                o_ref[:, part * 512 + h * A_HEAD:part * 512 + (h + 1) * A_HEAD] = d

    blk = pl.BlockSpec((tm, 512), lambda i: (i, 0))
    tab = pl.BlockSpec((tm, A_HEAD), lambda i: (i % nl, 0))
    return pl.pallas_call(
        body, name=name, grid=(T // tm,), in_specs=[blk] * 9 + [tab, tab],
        out_specs=pl.BlockSpec((tm, 1536), lambda i: (i, 0)),
        out_shape=jax.ShapeDtypeStruct((T, 1536), F32), compiler_params=_cparams("parallel"),
    )(*dqs, *dks, *dvs, cos2, sin2)


def _band_masks():
    i = lax.broadcasted_iota(jnp.int32, (B_SPAN, B_SPAN), 0)
    j = lax.broadcasted_iota(jnp.int32, (B_SPAN, B_SPAN), 1)
    return i <= j, j <= i


def _dil_geom(dil, L):
    return B_SPAN * dil, L // (B_SPAN * dil), (1 if dil > 1 else 4)


def _dil_rows(r, dil):
    return pl.ds(r, B_SPAN, stride=dil) if dil > 1 else pl.ds(0, B_SPAN)


def _dil_groups(dil, hb, size=4):
    items = [(_dil_rows(r, dil), pl.ds(h * A_HEAD, A_HEAD)) for r in range(dil) for h in range(hb)]
    return [items[i:i + size] for i in range(0, len(items), size)]


def _dil_specs(dil, B, L):
    W, nb, hb = _dil_geom(dil, L)
    cw = A_HEAD * hb

    def at(col0, shift):
        def index(b, n, hh):
            return (b * nb + jnp.clip(n + shift, 0, nb - 1), col0 // cw + hh)
        return pl.BlockSpec((W, cw), index)

    return at


def _dil_fwd(qr, kr, z, dil, B, L, *, name):
    T = B * L
    W, nb, hb = _dil_geom(dil, L)
    has_prev = nb > 1
    scale = A_HEAD ** -0.5
    at = _dil_specs(dil, B, L)

    def body(*refs):
        if has_prev:
            q_ref, kc_ref, vc_ref, kp_ref, vp_ref, o_ref, l_ref = refs
        else:
            q_ref, kc_ref, vc_ref, o_ref, l_ref = refs
        mp, mc = _band_masks()
        mp = mp & (pl.program_id(1) > 0)

        def load(ref, at_):
            return ref[at_].astype(BF16)

        for group in _dil_groups(dil, hb):
            qs = [load(q_ref, i) for i in group]
            sc = [jnp.where(mc, _dot(q, load(kc_ref, i), _NT) * scale, NEG_BIG) for q, i in zip(qs, group)]
            m = [jnp.max(s, axis=-1, keepdims=True) for s in sc]
            if has_prev:
                sp = [jnp.where(mp, _dot(q, load(kp_ref, i), _NT) * scale, NEG_BIG) for q, i in zip(qs, group)]
                m = [jnp.maximum(a, jnp.max(s, axis=-1, keepdims=True)) for a, s in zip(m, sp)]
            pc = [jnp.exp(s - a) for s, a in zip(sc, m)]
            l = [jnp.sum(p, axis=-1, keepdims=True) for p in pc]
            o = [_dot(p.astype(BF16), load(vc_ref, i)) for p, i in zip(pc, group)]
            if has_prev:
                pp = [jnp.exp(s - a) for s, a in zip(sp, m)]
                l = [a + jnp.sum(p, axis=-1, keepdims=True) for a, p in zip(l, pp)]
                o = [a + _dot(p.astype(BF16), load(vp_ref, i)) for a, p, i in zip(o, pp, group)]
            for i, oi, li, mi in zip(group, o, l, m):
                o_ref[i] = oi / li
                l_ref[i] = jnp.broadcast_to(mi + jnp.log(li), (B_SPAN, A_HEAD))

    ins = [qr, kr, z] + ([kr, z] if has_prev else [])
    in_specs = [at(0, 0), at(0, 0), at(3072, 0)] + ([at(0, -1), at(3072, -1)] if has_prev else [])
    return pl.pallas_call(
        body, name=name, grid=(B, nb, 4 // hb), in_specs=in_specs, out_specs=[at(0, 0), at(0, 0)],
        out_shape=[jax.ShapeDtypeStruct((T, 512), F32)] * 2,
        compiler_params=_cparams("parallel", "parallel", "parallel"),
    )(*ins)


def _dil_combine(os_, ls_, *, name):
    T = os_[0].shape[0]
    tm = 256

    def body(o1, o2, o3, l1, l2, l3, ob_ref, lse_ref):
        a1, a2, a3 = l1[...], l2[...], l3[...]
        m = jnp.maximum(jnp.maximum(a1, a2), a3)
        e1, e2, e3 = jnp.exp(a1 - m), jnp.exp(a2 - m), jnp.exp(a3 - m)
        den = e1 + e2 + e3
        ob_ref[...] = (e1 * o1[...] + e2 * o2[...] + e3 * o3[...]) / den
        lse_ref[...] = m + jnp.log(den)

    blk = pl.BlockSpec((tm, 512), lambda i: (i, 0))
    return pl.pallas_call(
        body, name=name, grid=(T // tm,), in_specs=[blk] * 6, out_specs=[blk, blk],
        out_shape=[jax.ShapeDtypeStruct((T, 512), F32)] * 2, compiler_params=_cparams("parallel"),
    )(*[o.reshape(T, 512) for o in os_], *[l.reshape(T, 512) for l in ls_])


def _dil_bwd_q(qr, kr, z, dymix, out, lse, dil, B, L, *, name):
    T = B * L
    W, nb, hb = _dil_geom(dil, L)
    has_prev = nb > 1
    scale = A_HEAD ** -0.5
    at = _dil_specs(dil, B, L)

    def body(*refs):
        if has_prev:
            q_ref, kc_ref, vc_ref, do_ref, out_ref, lse_ref, kp_ref, vp_ref, dq_ref = refs
        else:
            q_ref, kc_ref, vc_ref, do_ref, out_ref, lse_ref, dq_ref = refs
        mp, mc = _band_masks()
        mp = mp & (pl.program_id(1) > 0)

        def load(ref, at_):
            return ref[at_].astype(BF16)

        for group in _dil_groups(dil, hb):
            qs = [load(q_ref, i) for i in group]
            dos = [do_ref[i] for i in group]
            delta = [jnp.sum(d * out_ref[i], axis=-1, keepdims=True) for d, i in zip(dos, group)]
            dob = [d.astype(BF16) for d in dos]
            lse = [lse_ref[i] for i in group]
            kc = [load(kc_ref, i) for i in group]
            pc = [jnp.where(mc, jnp.exp(_dot(q, k, _NT) * scale - a), 0.0) for q, k, a in zip(qs, kc, lse)]
            dsc = [p * (_dot(d, load(vc_ref, i), _NT) - dl) * scale for p, d, i, dl in zip(pc, dob, group, delta)]
            dq = [_dot(d.astype(BF16), k) for d, k in zip(dsc, kc)]
            if has_prev:
                kp = [load(kp_ref, i) for i in group]
                pp = [jnp.where(mp, jnp.exp(_dot(q, k, _NT) * scale - a), 0.0) for q, k, a in zip(qs, kp, lse)]
                dsp = [p * (_dot(d, load(vp_ref, i), _NT) - dl) * scale for p, d, i, dl in zip(pp, dob, group, delta)]
                dq = [a + _dot(d.astype(BF16), k) for a, d, k in zip(dq, dsp, kp)]
            for i, d in zip(group, dq):
                dq_ref[i] = d

    ins = [qr, kr, z, dymix, out, lse] + ([kr, z] if has_prev else [])
    in_specs = ([at(0, 0), at(0, 0), at(3072, 0), at(512, 0), at(0, 0), at(0, 0)]
                + ([at(0, -1), at(3072, -1)] if has_prev else []))
    return pl.pallas_call(
        body, name=name, grid=(B, nb, 4 // hb), in_specs=in_specs, out_specs=at(0, 0),
        out_shape=jax.ShapeDtypeStruct((T, 512), F32),
        compiler_params=_cparams("parallel", "parallel", "parallel"),
    )(*ins)


def _dil_bwd_kv(qr, kr, z, dymix, out, lse, dil, B, L, *, name):
    T = B * L
    W, nb, hb = _dil_geom(dil, L)
    has_next = nb > 1
    scale = A_HEAD ** -0.5
    at = _dil_specs(dil, B, L)

    def body(*refs):
        k_ref, v_ref = refs[0], refs[1]
        own = refs[2:6]
        nxt = refs[6:10] if has_next else None
        dk_ref, dv_ref = refs[-2], refs[-1]
        mp, mc = _band_masks()
        mp = mp & (pl.program_id(1) < nb - 1)
        groups = [(own, mc)] + ([(nxt, mp)] if has_next else [])

        for group in _dil_groups(dil, hb):
            kh = [k_ref[i].astype(BF16) for i in group]
            vh = [v_ref[i].astype(BF16) for i in group]
            dk, dv = [None] * len(group), [None] * len(group)
            for (q_ref, do_ref, out_ref, lse_ref), mask in groups:
                qs = [q_ref[i].astype(BF16) for i in group]
                dos = [do_ref[i] for i in group]
                delta = [jnp.sum(d * out_ref[i], axis=-1, keepdims=True) for d, i in zip(dos, group)]
                dob = [d.astype(BF16) for d in dos]
                p = [jnp.where(mask, jnp.exp(_dot(q, k, _NT) * scale - lse_ref[i]), 0.0) for q, k, i in zip(qs, kh, group)]
                dvn = [_dot(a.astype(BF16), d, _TN) for a, d in zip(p, dob)]
                ds = [a * (_dot(d, v, _NT) - dl) * scale for a, d, v, dl in zip(p, dob, vh, delta)]
                dkn = [_dot(d.astype(BF16), q, _TN) for d, q in zip(ds, qs)]
                dv = [n if o is None else o + n for o, n in zip(dv, dvn)]
                dk = [n if o is None else o + n for o, n in zip(dk, dkn)]
            for i, a, b in zip(group, dk, dv):
                dk_ref[i] = a
                dv_ref[i] = b

    ins = [kr, z, qr, dymix, out, lse] + ([qr, dymix, out, lse] if has_next else [])
    in_specs = ([at(0, 0), at(3072, 0), at(0, 0), at(512, 0), at(0, 0), at(0, 0)]
                + ([at(0, 1), at(512, 1), at(0, 1), at(0, 1)] if has_next else []))
    return pl.pallas_call(
        body, name=name, grid=(B, nb, 4 // hb), in_specs=in_specs, out_specs=[at(0, 0), at(0, 0)],
        out_shape=[jax.ShapeDtypeStruct((T, 512), F32)] * 2,
        compiler_params=_cparams("parallel", "parallel", "parallel"),
    )(*ins)


def _s5_build(lam_re, lam_im, log_dt, b_re, b_im, c_re, c_im):
    G, P, TC = C_GROUPS, C_STATE, C_TC
    lr = jnp.minimum(lam_re, C_MIN_NEG_RE)
    li = lam_im
    dt = jnp.exp(log_dt)[:, None]
    mag = jnp.exp(dt * lr)
    ar, ai = mag * jnp.cos(dt * li), mag * jnp.sin(dt * li)
    den = lr * lr + li * li
    zr = ((ar - 1.0) * lr + ai * li) / den
    zi = (ai * lr - (ar - 1.0) * li) / den
    bbr = zr[..., None] * b_re - zi[..., None] * b_im
    bbi = zr[..., None] * b_im + zi[..., None] * b_re
    ks = jnp.arange(TC + 1, dtype=F32)[:, None, None]
    pmag = jnp.exp(ks * (dt * lr)[None])
    pr, pi = pmag * jnp.cos(ks * (dt * li)[None]), pmag * jnp.sin(ks * (dt * li)[None])
    car = c_re[None] * pr[:, :, None, :] - c_im[None] * pi[:, :, None, :]
    cai = c_re[None] * pi[:, :, None, :] + c_im[None] * pr[:, :, None, :]
    kern = (jnp.einsum('lgop,gpc->lgco', car[:TC], bbr, precision=HI)
            - jnp.einsum('lgop,gpc->lgco', cai[:TC], bbi, precision=HI))
    eye = jnp.eye(C_GB, dtype=F32)
    kbd = (kern.reshape(TC, C_NB, C_GB, C_GROUP, C_GROUP)[:, :, :, :, None, :]
           * eye[None, None, :, None, :, None]).reshape(TC, C_NB, LANES, LANES)
    lag = jnp.arange(TC)[None, :] - jnp.arange(TC)[:, None]
    ksel = jnp.where((lag >= 0)[:, :, None, None, None], kbd[jnp.clip(lag, 0, TC - 1)], 0.0)
    m8 = ksel.transpose(2, 0, 3, 1, 4).reshape(C_NB, C_W8, C_W8)
    pr_e, pi_e = pr[TC - 1 - jnp.arange(TC)], pi[TC - 1 - jnp.arange(TC)]
    er = pr_e[:, :, :, None] * bbr[None] - pi_e[:, :, :, None] * bbi[None]
    ei = pr_e[:, :, :, None] * bbi[None] + pi_e[:, :, :, None] * bbr[None]
    ez = jnp.stack([er, ei], axis=2).reshape(TC, C_NB, C_GB, 2, P, C_GROUP).transpose(1, 0, 2, 5, 3, 4)
    e8 = (ez[:, :, :, :, :, None, :] * eye[None, None, :, None, None, :, None]).reshape(C_NB, C_W8, C_S8)
    fz = jnp.stack([car[1:], -cai[1:]], axis=0).reshape(2, TC, C_NB, C_GB, C_GROUP, P).transpose(2, 0, 3, 5, 1, 4)
    f8 = (fz[:, :, :, :, :, None, :] * eye[None, None, :, None, None, :, None]).reshape(C_NB, C_S8, C_W8)
    return m8, e8, f8, pr[TC], pi[TC]


C_NB = C_GROUPS * C_GROUP // LANES
C_GB = C_GROUPS // C_NB
C_W8 = C_TC * LANES
C_S8 = 2 * C_GB * C_STATE


def _s5_scan_tables(lam_re, lam_im, log_dt, nsteps):
    lr = jnp.minimum(lam_re, C_MIN_NEG_RE)
    dt = jnp.exp(log_dt)[:, None]
    ks = (C_TC * 2.0 ** jnp.arange(8, dtype=F32))[None, :, None]
    keep = (jnp.arange(8) < nsteps)[None, :, None]
    pmag = jnp.exp(ks * (dt * lr)[:, None, :])
    ang = ks * (dt * lam_im)[:, None, :]

    def blocks(t):
        return t.reshape(C_NB, C_GB, 8, C_STATE).transpose(0, 2, 1, 3).reshape(C_NB, 8, C_GB * C_STATE)

    pr = blocks(jnp.where(keep, pmag * jnp.cos(ang), 0.0))
    pi = blocks(jnp.where(keep, pmag * jnp.sin(ang), 0.0))
    return jnp.concatenate([pr, pr], axis=-1), jnp.concatenate([-pi, pi], axis=-1)


def _s5_rows(t, R):
    return pl.ds(t, R, stride=C_TC)


def _s5_fwd(u, dsk, m8, e8, f8, tab_r, tab_i, B, L, *, name):
    T = B * L
    R = L // C_TC
    nsteps = int(math.log2(R))

    def body(u_ref, d_ref, m_ref, e_ref, f_ref, tr_ref, ti_ref, gl_ref, y_ref, x8_ref, xs_ref):
        for t in range(C_TC):
            x8_ref[0, :, t * LANES:(t + 1) * LANES] = u_ref[_s5_rows(t, R), :].astype(BF16)
        x8 = x8_ref[0]
        x = _dot(x8, e_ref[0])
        row = lax.broadcasted_iota(jnp.int32, (R, C_S8), 0)
        for k in range(nsteps):
            s = 1 << k
            sh = pltpu.roll(x, s, 0)
            upd = tr_ref[0, k:k + 1, :] * sh + ti_ref[0, k:k + 1, :] * pltpu.roll(sh, C_S8 // 2, 1)
            x = x + jnp.where(row >= s, upd, 0.0)
        xs = jnp.where(row >= 1, pltpu.roll(x, 1, 0), 0.0)
        xs_ref[0] = xs
        y8 = _dot(x8, m_ref[0]) + _dot(xs.astype(BF16), f_ref[0])
        d = d_ref[...]
        for t in range(C_TC):
            rows = _s5_rows(t, R)
            y = y8[:, t * LANES:(t + 1) * LANES] + d * u_ref[rows, :]
            y_ref[rows, :] = y
            gl_ref[rows, :] = 0.5 * y * (1.0 + _erf(y * (2.0 ** -0.5)))

    tok = pl.BlockSpec((L, LANES), lambda c, b: (b, c))
    per_block = lambda shape: pl.BlockSpec((1,) + shape, lambda c, b: (c, 0, 0))
    per_step = lambda shape: pl.BlockSpec((1,) + shape, lambda c, b: (c * B + b, 0, 0))
    return pl.pallas_call(
        body, name=name, grid=(C_NB, B),
        in_specs=[tok, pl.BlockSpec((1, LANES), lambda c, b: (0, c)), per_block((C_W8, C_W8)), per_block((C_W8, C_S8)),
                  per_block((C_S8, C_W8)), per_block((8, C_S8)), per_block((8, C_S8))],
        out_specs=[tok, tok, per_step((R, C_W8)), per_step((R, C_S8))],
        out_shape=[jax.ShapeDtypeStruct((T, D_MODEL), F32), jax.ShapeDtypeStruct((T, D_MODEL), F32),
                   jax.ShapeDtypeStruct((C_NB * B, R, C_W8), BF16), jax.ShapeDtypeStruct((C_NB * B, R, C_S8), F32)],
        compiler_params=_cparams("parallel", "parallel"),
    )(u, dsk, m8, e8, f8, tab_r, tab_i)


def _s5_bwd(dgl, y, u, dsk, xs, m8, e8, f8, tab_r, tab_i, B, L, *, name):
    T = B * L
    R = L // C_TC
    nsteps = int(math.log2(R))

    def body(dgl_ref, y_ref, u_ref, d_ref, xs_ref, m_ref, e_ref, f_ref, tr_ref, ti_ref,
             du_ref, dy8_ref, de_ref, da_ref, dd_ref, dyf_ref):
        @pl.when(pl.program_id(1) == 0)
        def _():
            da_ref[...] = jnp.zeros_like(da_ref)
            dd_ref[...] = jnp.zeros_like(dd_ref)

        dd = jnp.zeros((1, LANES), F32)
        for t in range(C_TC):
            rows = _s5_rows(t, R)
            yv = y_ref[rows, :]
            cdf = 0.5 * (1.0 + _erf(yv * (2.0 ** -0.5)))
            pdf = jnp.exp(-0.5 * yv * yv) * (1.0 / math.sqrt(2.0 * math.pi))
            dy = dgl_ref[rows, :] * (cdf + yv * pdf)
            dd = dd + jnp.sum(dy * u_ref[rows, :], axis=0, keepdims=True)
            dyf_ref[:, t * LANES:(t + 1) * LANES] = dy
        dd_ref[...] += dd
        dy8 = dyf_ref[...].astype(BF16)
        dy8_ref[0] = dy8
        xs = xs_ref[0]
        gx = _dot(dy8, f_ref[0], _NT)
        row = lax.broadcasted_iota(jnp.int32, (R, C_S8), 0)
        for k in range(nsteps):
            s = 1 << k
            sh = pltpu.roll(gx, R - s, 0)
            upd = tr_ref[0, k:k + 1, :] * sh - ti_ref[0, k:k + 1, :] * pltpu.roll(sh, C_S8 // 2, 1)
            gx = gx + jnp.where(row + s < R, upd, 0.0)
        de_in = jnp.where(row + 1 < R, pltpu.roll(gx, R - 1, 0), 0.0)
        deb = de_in.astype(BF16)
        de_ref[0] = deb
        da_ref[0, 0:1, :] += jnp.sum(de_in * xs, axis=0, keepdims=True)
        da_ref[0, 1:2, :] += jnp.sum(de_in * pltpu.roll(xs, C_S8 // 2, 1), axis=0, keepdims=True)
        dx8 = _dot(dy8, m_ref[0], _NT) + _dot(deb, e_ref[0], _NT)
        d = d_ref[...]
        for t in range(C_TC):
            cols = slice(t * LANES, (t + 1) * LANES)
            du_ref[_s5_rows(t, R), :] = dx8[:, cols] + d * dyf_ref[:, cols]

    tok = pl.BlockSpec((L, LANES), lambda c, b: (b, c))
    vec = pl.BlockSpec((1, LANES), lambda c, b: (0, c))
    per_block = lambda shape: pl.BlockSpec((1,) + shape, lambda c, b: (c, 0, 0))
    per_step = lambda shape: pl.BlockSpec((1,) + shape, lambda c, b: (c * B + b, 0, 0))
    return pl.pallas_call(
        body, name=name, grid=(C_NB, B),
        in_specs=[tok, tok, tok, vec, per_step((R, C_S8)), per_block((C_W8, C_W8)),
                  per_block((C_W8, C_S8)), per_block((C_S8, C_W8)), per_block((8, C_S8)), per_block((8, C_S8))],
        out_specs=[tok, per_step((R, C_W8)), per_step((R, C_S8)), per_block((8, C_S8)), vec],
        out_shape=[jax.ShapeDtypeStruct((T, D_MODEL), F32), jax.ShapeDtypeStruct((C_NB * B, R, C_W8), BF16),
                   jax.ShapeDtypeStruct((C_NB * B, R, C_S8), BF16), jax.ShapeDtypeStruct((C_NB, 8, C_S8), F32),
                   jax.ShapeDtypeStruct((1, D_MODEL), F32)],
        scratch_shapes=[pltpu.VMEM((R, C_W8), F32)],
        compiler_params=_cparams("parallel", "arbitrary"),
    )(dgl, y, u, dsk, xs, m8, e8, f8, tab_r, tab_i)


def _bmm_tn(a, b, nb, *, name):
    a = a.reshape(nb, -1, a.shape[-1])
    b = b.reshape(nb, -1, b.shape[-1])
    K, M, N = a.shape[1], a.shape[2], b.shape[2]

    def body(a_ref, b_ref, o_ref):
        o_ref[0] = _dot(a_ref[0].astype(BF16), b_ref[0].astype(BF16), _TN)

    return pl.pallas_call(
        body, name=name, grid=(nb,),
        in_specs=[pl.BlockSpec((1, K, M), lambda c: (c, 0, 0)), pl.BlockSpec((1, K, N), lambda c: (c, 0, 0))],
        out_specs=pl.BlockSpec((1, M, N), lambda c: (c, 0, 0)),
        out_shape=jax.ShapeDtypeStruct((nb, M, N), F32), compiler_params=_cparams("parallel"),
    )(a, b)


def _loss_head(y, target, *, name):
    T, C = y.shape
    tm = 256

    def body(y_ref, t_ref, l_ref, d_ref):
        err = y_ref[...] - t_ref[...]
        d_ref[...] = err * (1.0 / C)
        sq = err * err
        part = jnp.zeros((8, LANES), F32)
        for r in range(0, tm, 8):
            for c in range(0, C, LANES):
                part = part + sq[r:r + 8, c:c + LANES]

        @pl.when(pl.program_id(0) == 0)
        def _():
            l_ref[...] = part

        @pl.when(pl.program_id(0) > 0)
        def _():
            l_ref[...] += part

    row = pl.BlockSpec((tm, C), lambda i: (i, 0))
    acc = pl.BlockSpec((8, LANES), lambda i: (0, 0))
    return pl.pallas_call(
        body, name=name, grid=(T // tm,), in_specs=[row, row], out_specs=[acc, row],
        out_shape=[jax.ShapeDtypeStruct((8, LANES), F32), jax.ShapeDtypeStruct((T, C), F32)],
        compiler_params=_cparams("arbitrary"),
    )(y, target)


def _adamw(w, g, m, v, *, name):
    shape = w.shape
    size = int(np.prod(shape))
    cols = LANES if (shape[-1] < LANES and size % LANES == 0) else shape[-1]
    rows = size // cols
    tm = _tile(rows, 256) if rows % 8 == 0 else rows
    w2, g2, m2, v2 = (t.reshape(rows, cols) for t in (w, g, m, v))

    def body(w_ref, g_ref, m_ref, v_ref, d_ref, nm_ref, nv_ref):
        gg = g_ref[...]
        nm = ADAM_B1 * m_ref[...] + (1.0 - ADAM_B1) * gg
        nv = ADAM_B2 * v_ref[...] + (1.0 - ADAM_B2) * (gg * gg)
        m_hat = nm / (1.0 - ADAM_B1 ** ADAM_STEP)
        v_hat = nv / (1.0 - ADAM_B2 ** ADAM_STEP)
        d_ref[...] = -ADAM_LR * (m_hat / (jnp.sqrt(v_hat) + ADAM_EPS) + ADAM_WD * w_ref[...])
        nm_ref[...] = nm
        nv_ref[...] = nv

    blk = pl.BlockSpec((tm, cols), lambda i: (i, 0))
    outs = pl.pallas_call(
        body, name=name, grid=(rows // tm,), in_specs=[blk] * 4, out_specs=[blk] * 3,
        out_shape=[jax.ShapeDtypeStruct((rows, cols), F32)] * 3, compiler_params=_cparams("parallel"),
    )(w2, g2, m2, v2)
    return tuple(o.reshape(shape) for o in outs)


def _all_gather(shard, *, name):
    R, C = shard.shape

    def body(x_ref, out_ref, send_sems, recv_sems, local_sem):
        x, y, c = lax.axis_index("x"), lax.axis_index("y"), lax.axis_index("c")
        me, sibling = (x, y, c), (x, y, 1 - c)
        chips = [(1 - x, y), (x, 1 - y), (1 - x, 1 - y)]

        def rows(px, py, pc):
            return out_ref.at[_logical(px, py, pc)]

        def copy(k, block, to, src=None):
            return pltpu.make_async_remote_copy(
                src_ref=rows(*block) if src is None else src, dst_ref=rows(*block),
                send_sem=send_sems.at[k], recv_sem=recv_sems.at[k], device_id=to, device_id_type=_MESH)

        mine = pltpu.make_async_copy(x_ref, rows(*me), local_sem)
        mine.start()
        first = [copy(0, me, sibling, src=x_ref)]
        first += [copy(1 + j, me, (*chip, c), src=x_ref) for j, chip in enumerate(chips)]
        for cp in first:
            cp.start()
        passed = [copy(4 + j, (*chip, c), sibling) for j, chip in enumerate(chips)]
        for j, chip in enumerate(chips):
            copy(1 + j, (*chip, c), me).wait_recv()
            passed[j].start()
        copy(0, sibling, me).wait_recv()
        for j, chip in enumerate(chips):
            copy(4 + j, (*chip, 1 - c), me).wait_recv()
        for cp in first + passed:
            cp.wait_send()
        mine.wait()

    return pl.pallas_call(
        body, name=name, out_shape=jax.ShapeDtypeStruct((N_DEV, R, C), shard.dtype),
        in_specs=[_HBM], out_specs=_HBM,
        scratch_shapes=[pltpu.SemaphoreType.DMA((7,)), pltpu.SemaphoreType.DMA((7,)), pltpu.SemaphoreType.DMA],
    )(shard)


def _gather_weights(shards, *, name):
    nt = len(shards)

    def body(*refs):
        ins, outs = refs[:nt], refs[nt:2 * nt]
        send_sems, recv_sems, local_sems = refs[2 * nt:]
        x, y, c = lax.axis_index("x"), lax.axis_index("y"), lax.axis_index("c")
        me, sibling = (x, y, c), (x, y, 1 - c)
        chips = [(1 - x, y), (x, 1 - y), (1 - x, 1 - y)]

        def copy(t, k, block, to, src=None):
            rows = outs[t].at[_logical(*block)]
            return pltpu.make_async_remote_copy(
                src_ref=rows if src is None else src, dst_ref=rows,
                send_sem=send_sems.at[t, k], recv_sem=recv_sems.at[t, k], device_id=to, device_id_type=_MESH)

        mine = [pltpu.make_async_copy(ins[t], outs[t].at[_logical(*me)], local_sems.at[t]) for t in range(nt)]
        for cp in mine:
            cp.start()
        started = []
        for t in range(nt):
            started.append(copy(t, 0, me, sibling, src=ins[t]))
            started += [copy(t, 1 + j, me, (*chip, c), src=ins[t]) for j, chip in enumerate(chips)]
        for cp in started:
            cp.start()
        for j, chip in enumerate(chips):
            for t in range(nt):
                copy(t, 1 + j, (*chip, c), me).wait_recv()
                fwd = copy(t, 4 + j, (*chip, c), sibling)
                fwd.start()
                started.append(fwd)
        for t in range(nt):
            copy(t, 0, sibling, me).wait_recv()
            for j, chip in enumerate(chips):
                copy(t, 4 + j, (*chip, 1 - c), me).wait_recv()
        for cp in started:
            cp.wait_send()
        for cp in mine:
            cp.wait()

    return pl.pallas_call(
        body, name=name, out_shape=[jax.ShapeDtypeStruct((N_DEV,) + s.shape, s.dtype) for s in shards],
        in_specs=[_HBM] * nt, out_specs=[_HBM] * nt,
        scratch_shapes=[pltpu.SemaphoreType.DMA((nt, 7)), pltpu.SemaphoreType.DMA((nt, 7)), pltpu.SemaphoreType.DMA((nt,))],
    )(*shards)


def _sum_rows(stacked, *, name):
    _, R, C = stacked.shape
    tr = R
    if N_DEV * R * C * stacked.dtype.itemsize > 12 * 1024 * 1024:
        for cand in range(512, 15, -16):
            if R % cand == 0:
                tr = cand
                break

    def body(s_ref, o_ref):
        acc = s_ref[0].astype(F32)
        for k in range(1, N_DEV):
            acc = acc + s_ref[k].astype(F32)
        o_ref[...] = acc

    return pl.pallas_call(
        body, name=name, grid=(R // tr,),
        in_specs=[pl.BlockSpec((N_DEV, tr, C), lambda i: (0, i, 0))], out_specs=pl.BlockSpec((tr, C), lambda i: (i, 0)),
        out_shape=jax.ShapeDtypeStruct((R, C), F32), compiler_params=_cparams("parallel"),
    )(stacked)


_LARGE = (("ab_w_in", 1, True), ("ab_w_out", 1, False), ("s5_w_glu", 1, True), ("xattn_wq", 2, False),
          ("xattn_wkv", 2, True), ("xattn_wo", 2, False), ("ffn_w_in", 2, True), ("ffn_w_out", 2, False))
_LARGE_KEYS = tuple((n, l) for n, layers, _ in _LARGE for l in range(layers))
_TRANSPOSED = {n: t for n, _, t in _LARGE}


def _owner_major(name, w):
    return w.T if _TRANSPOSED[name] else w


def _lb_from_logits(logits):
    return jnp.cumsum(jax.nn.softmax(logits, axis=0), axis=0)[0:1]


def _local_step(x, mem, target, W, shards=None):
    B, L, _ = x.shape
    T = B * L
    x0 = x.reshape(T, D_MODEL)
    memf = mem.reshape(B * MEM_LEN, D_MODEL)
    nw = W["norm_w"]
    cos2, sin2 = _rope_tables(L)
    W = dict(W)
    G, received = {}, {}

    def mmx(a, b, gather=(), scatter=(), **kw):
        if shards is None or not (gather or scatter):
            return _mm(a, b, **kw)
        out, gathered, got = _mm(a, b, gather=[shards[k] for k in gather],
                                 scatter=[G[k].reshape(N_DEV, -1, D_MODEL) for k in scatter], **kw)
        for k, g in zip(gather, gathered):
            W[k] = g.reshape(-1, D_MODEL)
        for k, r in zip(scatter, got):
            received[k] = r
        return out

    def vec(v):
        return v.reshape(1, -1)

    saved = []
    xin = x0
    for layer in range(2):
        s = {"x0": xin}
        tag = f"l{layer}"
        if layer == 0:
            h1 = _rms_fwd(xin, vec(nw[0, 0]), name="norm_pre_mix_l0", out_dtype=BF16)
        s["h1"] = h1
        if layer == 0:
            lb, lb_vjp = jax.vjp(_lb_from_logits, W["hgrn_lb_logits"])
            onw = W["hgrn_out_norm_w"].reshape(1, A_WIDTH)
            z = mmx(h1, W["ab_w_in", 0], tb=True, name="ab_in",
                    gather=[("ab_w_out", 0), ("xattn_wq", 0), ("xattn_wkv", 0), ("xattn_wo", 0), ("ffn_w_in", 0)])
            oa, o_raw, s_start = _hgrn_fwd(z, lb, onw, B, L, name="hgrn_fwd")
            qr, kr = _rope_fwd(z, cos2, sin2, B, L, name="rope_qk")
            os_, ls_ = [], []
            for dil in B_DILS:
                o_g, l_g = _dil_fwd(qr, kr, z, dil, B, L, name=f"dil_fwd_{dil}")
                os_.append(o_g)
                ls_.append(l_g)
            ob, lse = _dil_combine(os_, ls_, name="dil_combine")
            ymix = jnp.concatenate([oa, ob], axis=-1).astype(BF16)
            y1 = mmx(ymix, W["ab_w_out", 0], out_dtype=BF16, name="ab_out", gather=[("ffn_w_out", 0)])
            s.update(z=z, lb=lb, lb_vjp=lb_vjp, onw=onw, o_raw=o_raw, s_start=s_start, qr=qr, kr=kr, ob=ob, lse=lse, ymix=ymix)
        else:
            p5 = tuple(W[n][0] for n in ("s5_lambda_re", "s5_lambda_im", "s5_log_dt", "s5_b_re", "s5_b_im", "s5_c_re", "s5_c_im"))
            (m8, e8, f8, _, _), s5_vjp = jax.vjp(_s5_build, *p5)
            tab_r, tab_i = _s5_scan_tables(p5[0], p5[1], p5[2], int(math.log2(L // C_TC)))
            mats = (m8.astype(BF16), e8.astype(BF16), f8.astype(BF16), tab_r, tab_i)
            dsk = W["s5_d"].reshape(1, D_MODEL)
            gl, ypre, x8, xs = _s5_fwd(h1, dsk, *mats, B, L, name="s5_fwd")
            w_glu = _interleave_rows(W["s5_w_glu", 0])
            zg, y1 = _mm(gl, w_glu, tb=True, name="s5_glu_in", gate=("fwd", "glu"))
            s.update(s5_vjp=s5_vjp, mats=mats, x8=x8, xs=xs, dsk=dsk, gl=gl, ypre=ypre, zg=zg, w_glu=w_glu)
        x1, h2 = _rms_fwd(y1, vec(nw[layer, 1]), xin, name=f"norm_post_mix_{tag}", then=(vec(nw[layer, 2]), BF16))
        memn = _rms_fwd(memf, vec(W["mem_norm_w"][layer]), name=f"norm_mem_{tag}", out_dtype=BF16)
        q = mmx(h2, W["xattn_wq", layer], out_dtype=BF16, name=f"x_q_{tag}", gather=[("s5_w_glu", 0)] if layer == 0 else [])
        kv = _mm(memn, W["xattn_wkv", layer], tb=True, out_dtype=BF16, name=f"x_kv_{tag}")
        o = _xattn_fwd(q, kv, B, L, name=f"x_attn_{tag}")
        y2 = mmx(o, W["xattn_wo", layer], out_dtype=BF16, name=f"x_o_{tag}", gather=[("xattn_wq", 1), ("xattn_wo", 1)] if layer == 0 else [])
        x2, h3 = _rms_fwd(y2, vec(nw[layer, 3]), x1, name=f"norm_post_x_{tag}", then=(vec(nw[layer, 4]), BF16))
        w_ffn_in = _interleave_rows(W["ffn_w_in", layer])
        zf, u = mmx(h3, w_ffn_in, tb=True, name=f"ffn_in_{tag}", gate=("fwd", "swiglu"),
                    gather=[("xattn_wkv", 1), ("ffn_w_in", 1), ("ffn_w_out", 1)] if layer == 0 else [])
        y3 = _mm(u, W["ffn_w_out", layer], out_dtype=BF16, name=f"ffn_out_{tag}")
        if layer == 0:
            x3, h1 = _rms_fwd(y3, vec(nw[0, 5]), x2, name="norm_post_ffn_l0", then=(vec(nw[1, 0]), F32))
        else:
            x3 = _rms_fwd(y3, vec(nw[layer, 5]), x2, name=f"norm_post_ffn_{tag}")
        s.update(y1=y1, x1=x1, h2=h2, memn=memn, q=q, kv=kv, o=o, y2=y2, x2=x2, h3=h3, zf=zf, u=u, y3=y3, w_ffn_in=w_ffn_in)
        saved.append(s)
        xin = x3

    loss_parts, dx = _loss_head(xin, target.reshape(T, D_MODEL), name="loss_head")

    d_norm = [[None] * 6 for _ in range(2)]
    d_memn = [None, None]
    for layer in (1, 0):
        s = saved[layer]
        tag = f"l{layer}"
        dy3, d_norm[layer][5] = _rms_bwd(s["y3"], vec(nw[layer, 5]), dx, name=f"bnorm_post_ffn_{tag}", out_dtype=BF16)
        dzf = mmx(dy3, W["ffn_w_out", layer], tb=True, name=f"b_ffn_out_dx_{tag}", gate=("bwd", "swiglu", s["zf"]),
                  scatter=[("xattn_wkv", 1), ("s5_w_glu", 0)] if layer == 0 else [])
        G["ffn_w_out", layer] = _mm(s["u"], dy3, ta=True, out_dtype=BF16, name=f"b_ffn_out_dw_{tag}")
        G["ffn_w_in", layer] = _interleave_rows(
            mmx(dzf, s["h3"], ta=True, out_dtype=BF16, name=f"b_ffn_in_dw_{tag}", scatter=[("ffn_w_out", layer)]), inverse=True)
        dh3 = mmx(dzf, s["w_ffn_in"], out_dtype=BF16, name=f"b_ffn_in_dx_{tag}", scatter=[("ffn_w_in", layer)])
        dx, d_norm[layer][4] = _rms_bwd(s["x2"], vec(nw[layer, 4]), dh3, dx, name=f"bnorm_pre_ffn_{tag}")
        dy2, d_norm[layer][3] = _rms_bwd(s["y2"], vec(nw[layer, 3]), dx, name=f"bnorm_post_x_{tag}", out_dtype=BF16)
        do = _mm(dy2, W["xattn_wo", layer], tb=True, out_dtype=BF16, name=f"b_x_o_dx_{tag}")
        G["xattn_wo", layer] = _mm(s["o"], dy2, ta=True, out_dtype=BF16, name=f"b_x_o_dw_{tag}")
        dq, dkv = _xattn_bwd(s["q"], s["kv"], do, B, L, name=f"b_x_attn_{tag}")
        G["xattn_wq", layer] = _mm(s["h2"], dq, ta=True, out_dtype=BF16, name=f"b_x_q_dw_{tag}")
        dh2 = _mm(dq, W["xattn_wq", layer], tb=True, out_dtype=BF16, name=f"b_x_q_dx_{tag}")
        G["xattn_wkv", layer] = _mm(dkv, s["memn"], ta=True, out_dtype=BF16, name=f"b_x_kv_dw_{tag}")
        dmemn = _mm(dkv, W["xattn_wkv", layer], out_dtype=BF16, name=f"b_x_kv_dx_{tag}")
        _, d_memn[layer] = _rms_bwd(memf, vec(W["mem_norm_w"][layer]), dmemn, name=f"bnorm_mem_{tag}", out_dtype=BF16)
        dx, d_norm[layer][2] = _rms_bwd(s["x1"], vec(nw[layer, 2]), dh2, dx, name=f"bnorm_pre_x_{tag}")
        dy1, d_norm[layer][1] = _rms_bwd(s["y1"], vec(nw[layer, 1]), dx, name=f"bnorm_post_mix_{tag}", out_dtype=BF16)
        if layer == 0:
            z = s["z"]
            dymix = _mm(dy1, W["ab_w_out", 0], tb=True, name="b_ab_out_dx")
            G["ab_w_out", 0] = _mm(s["ymix"], dy1, ta=True, out_dtype=BF16, name="b_ab_out_dw")
            dqa, dfa, dia, dga, d_onw, d_lb = _hgrn_bwd(z, s["lb"], s["onw"], s["o_raw"], s["s_start"], dymix, B, L, name="hgrn_bwd")
            dqs, dks, dvs = [], [], []
            for dil in B_DILS:
                dqs.append(_dil_bwd_q(s["qr"], s["kr"], z, dymix, s["ob"], s["lse"], dil, B, L, name=f"dil_bwd_q_{dil}"))
                dk_g, dv_g = _dil_bwd_kv(s["qr"], s["kr"], z, dymix, s["ob"], s["lse"], dil, B, L, name=f"dil_bwd_kv_{dil}")
                dks.append(dk_g)
                dvs.append(dv_g)
            dqkv = _rope_bwd(dqs, dks, dvs, cos2, sin2, B, L, name="b_rope")
            dz = jnp.concatenate([dqa, dfa, dia, dga, dqkv], axis=-1).astype(BF16)
            G["ab_w_in", 0] = mmx(dz, s["h1"], ta=True, out_dtype=BF16, name="b_ab_in_dw",
                                  scatter=[("xattn_wo", 0), ("xattn_wq", 0), ("xattn_wkv", 0), ("ab_w_out", 0)])
            dh1 = mmx(dz, W["ab_w_in", 0], out_dtype=BF16, name="b_ab_in_dx", scatter=[("ab_w_in", 0)])
            G["hgrn_out_norm_w"] = jnp.sum(d_onw.reshape(B, A_WIDTH), axis=0, keepdims=True)
            d_lb_row = jnp.sum(d_lb.reshape(B, A_WIDTH), axis=0, keepdims=True)
            G["hgrn_lb_logits"] = s["lb_vjp"](d_lb_row)[0]
        else:
            dzg = _gated_bwd(s["zg"], dy1, "glu", name="b_s5_glu")
            G["s5_w_glu", 0] = _interleave_rows(
                _mm(dzg, s["gl"], ta=True, out_dtype=BF16, name="b_s5_glu_dw"), inverse=True)
            dgl = mmx(dzg, s["w_glu"], name="b_s5_glu_dx", scatter=[("xattn_wo", 1), ("xattn_wq", 1)])
            dh1, dy8, de_in, da, d_dsk = _s5_bwd(dgl, s["ypre"], s["h1"], s["dsk"], s["xs"], *s["mats"], B, L, name="s5_bwd")
            dm8 = _bmm_tn(s["x8"], dy8, C_NB, name="s5_bwd_dm")
            df8 = _bmm_tn(s["xs"], dy8, C_NB, name="s5_bwd_df")
            de8 = _bmm_tn(s["x8"], de_in, C_NB, name="s5_bwd_de")
            half = C_S8 // 2
            da_r = (da[:, 0, :half] + da[:, 0, half:]).reshape(C_GROUPS, C_STATE)
            da_i = (da[:, 1, half:] - da[:, 1, :half]).reshape(C_GROUPS, C_STATE)
            gp = s["s5_vjp"]((dm8, de8, df8, da_r, da_i))
            for n, gv in zip(("s5_lambda_re", "s5_lambda_im", "s5_log_dt", "s5_b_re", "s5_b_im", "s5_c_re", "s5_c_im"), gp):
                G[n] = gv[None]
            G["s5_d"] = d_dsk
        dx, d_norm[layer][0] = _rms_bwd(s["x0"], vec(nw[layer, 0]), dh1, dx, name=f"bnorm_pre_mix_{tag}")

    G["norm_w"] = jnp.stack([jnp.concatenate(d_norm[l], axis=0) for l in range(2)])
    G["mem_norm_w"] = jnp.concatenate(d_memn, axis=0)
    if shards is not None:
        G.update(received)
    return loss_parts, dx.reshape(B, L, D_MODEL), G


_SMALL = (("norm_w", (2, 6, 1024)), ("mem_norm_w", (2, 1024)), ("hgrn_lb_logits", (3, 512)), ("hgrn_out_norm_w", (1, 512)),
          ("s5_lambda_re", (1, 64, 64)), ("s5_lambda_im", (1, 64, 64)), ("s5_log_dt", (1, 64)),
          ("s5_b_re", (1, 64, 64, 16)), ("s5_b_im", (1, 64, 64, 16)), ("s5_c_re", (1, 64, 16, 64)),
          ("s5_c_im", (1, 64, 16, 64)), ("s5_d", (1, 1024)))

_WEIGHT_ORDER = ('norm_w', 'mem_norm_w', 'ab_w_in', 'ab_w_out', 'hgrn_lb_logits', 'hgrn_out_norm_w', 's5_lambda_re',
                 's5_lambda_im', 's5_log_dt', 's5_b_re', 's5_b_im', 's5_c_re', 's5_c_im', 's5_d', 's5_w_glu', 'xattn_wq',
                 'xattn_wkv', 'xattn_wo', 'ffn_w_in', 'ffn_w_out')


def kernel(x, mem, norm_w, mem_norm_w, ab_w_in, ab_w_out, hgrn_lb_logits, hgrn_out_norm_w, s5_lambda_re, s5_lambda_im, s5_log_dt, s5_b_re, s5_b_im, s5_c_re, s5_c_im, s5_d, s5_w_glu, xattn_wq, xattn_wkv, xattn_wo, ffn_w_in, ffn_w_out, loss_target, m_norm_w, m_mem_norm_w, m_ab_w_in, m_ab_w_out, m_hgrn_lb_logits, m_hgrn_out_norm_w, m_s5_lambda_re, m_s5_lambda_im, m_s5_log_dt, m_s5_b_re, m_s5_b_im, m_s5_c_re, m_s5_c_im, m_s5_d, m_s5_w_glu, m_xattn_wq, m_xattn_wkv, m_xattn_wo, m_ffn_w_in, m_ffn_w_out, v_norm_w, v_mem_norm_w, v_ab_w_in, v_ab_w_out, v_hgrn_lb_logits, v_hgrn_out_norm_w, v_s5_lambda_re, v_s5_lambda_im, v_s5_log_dt, v_s5_b_re, v_s5_b_im, v_s5_c_re, v_s5_c_im, v_s5_d, v_s5_w_glu, v_xattn_wq, v_xattn_wkv, v_xattn_wo, v_ffn_w_in, v_ffn_w_out):
    local = dict(norm_w=norm_w, mem_norm_w=mem_norm_w, ab_w_in=ab_w_in, ab_w_out=ab_w_out, hgrn_lb_logits=hgrn_lb_logits,
                 hgrn_out_norm_w=hgrn_out_norm_w, s5_lambda_re=s5_lambda_re, s5_lambda_im=s5_lambda_im, s5_log_dt=s5_log_dt,
                 s5_b_re=s5_b_re, s5_b_im=s5_b_im, s5_c_re=s5_c_re, s5_c_im=s5_c_im, s5_d=s5_d, s5_w_glu=s5_w_glu,
                 xattn_wq=xattn_wq, xattn_wkv=xattn_wkv, xattn_wo=xattn_wo, ffn_w_in=ffn_w_in, ffn_w_out=ffn_w_out)
    mom_m = dict(zip(_WEIGHT_ORDER, (m_norm_w, m_mem_norm_w, m_ab_w_in, m_ab_w_out, m_hgrn_lb_logits, m_hgrn_out_norm_w, m_s5_lambda_re, m_s5_lambda_im, m_s5_log_dt, m_s5_b_re, m_s5_b_im, m_s5_c_re, m_s5_c_im, m_s5_d, m_s5_w_glu, m_xattn_wq, m_xattn_wkv, m_xattn_wo, m_ffn_w_in, m_ffn_w_out)))
    mom_v = dict(zip(_WEIGHT_ORDER, (v_norm_w, v_mem_norm_w, v_ab_w_in, v_ab_w_out, v_hgrn_lb_logits, v_hgrn_out_norm_w, v_s5_lambda_re, v_s5_lambda_im, v_s5_log_dt, v_s5_b_re, v_s5_b_im, v_s5_c_re, v_s5_c_im, v_s5_d, v_s5_w_glu, v_xattn_wq, v_xattn_wkv, v_xattn_wo, v_ffn_w_in, v_ffn_w_out)))
    dev = 4 * lax.axis_index("x") + 2 * lax.axis_index("y") + lax.axis_index("c")

    shards = {(n, l): _owner_major(n, local[n][l]).astype(BF16) for n, l in _LARGE_KEYS}
    first = ("ab_w_in", 0)
    W = {first: _gather_weights([shards[first]], name="gather_first")[0].reshape(-1, D_MODEL)}
    tiny =jnp.concatenate([norm_w.reshape(-1), s5_d.reshape(-1)])
    tiny = jnp.pad(tiny, (0, 16 * LANES - tiny.shape[0])).reshape(16, LANES)
    tiny_all = _all_gather(tiny, name="gather_tiny").reshape(N_DEV, 16 * LANES)
    W["norm_w"] = tiny_all[:, :12 * LANES].reshape(N_DEV, 2, 6, LANES).transpose(1, 2, 0, 3).reshape(2, 6, D_MODEL)
    W["s5_d"] = tiny_all[:, 12 * LANES:13 * LANES].reshape(1, D_MODEL)
    for n in ("mem_norm_w", "hgrn_lb_logits", "hgrn_out_norm_w", "s5_lambda_re", "s5_lambda_im", "s5_log_dt",
              "s5_b_re", "s5_b_im", "s5_c_re", "s5_c_im"):
        W[n] = local[n]

    loss_parts, grad_x, G = _local_step(x, mem, loss_target, W, shards)

    g_layers = {}
    for n, l in _LARGE_KEYS:
        g = _sum_rows(G[n, l], name=f"sum_grads_{n}_{l}")
        g_layers.setdefault(n, []).append(g.T if _TRANSPOSED[n] else g)
    g_local = {n: jnp.stack(gl) for n, gl in g_layers.items()}
    small =jnp.concatenate([G[n].reshape(-1) for n, _ in _SMALL] + [0.5 / D_MODEL * jnp.sum(loss_parts).reshape(1)])
    n_small = small.shape[0]
    small = jnp.pad(small, (0, (-n_small) % (8 * LANES))).reshape(-1, LANES)
    small_sum = _sum_rows(_all_gather(small, name="gather_small"), name="sum_small").reshape(-1)
    g_full, off = {}, 0
    for n, shp in _SMALL:
        size = int(np.prod(shp))
        g_full[n] = small_sum[off:off + size].reshape(shp)
        off += size
    loss = small_sum[off]
    grads = dict(g_local)
    for n, shp in _SMALL:
        if n == "norm_w":
            grads[n] = lax.dynamic_slice_in_dim(g_full[n], dev * LANES, LANES, axis=2)
        elif n == "s5_d":
            grads[n] = lax.dynamic_slice_in_dim(g_full[n], dev * LANES, LANES, axis=1)
        else:
            grads[n] = g_full[n]

    delta, new_m, new_v = {}, {}, {}
    for n in _WEIGHT_ORDER:
        delta[n], new_m[n], new_v[n] = _adamw(local[n], grads[n], mom_m[n], mom_v[n], name=f"adamw_{n}")
    return (loss, grad_x, *[grads[n] for n in _WEIGHT_ORDER], *[delta[n] for n in _WEIGHT_ORDER],
            *[new_m[n] for n in _WEIGHT_ORDER], *[new_v[n] for n in _WEIGHT_ORDER])
```

```python
import functools
import math

import numpy as np
import jax
import jax.numpy as jnp
from jax import lax
from jax.experimental import pallas as pl
from jax.experimental.pallas import tpu as pltpu

F32 = jnp.float32
BF16 = jnp.bfloat16
HI = lax.Precision.HIGHEST

D_MODEL = 1024
NORM_EPS = 1e-6
A_WIDTH = 512
A_HEAD = 128
A_CHUNK = 32
A_SUPER = 256
B_SPAN = 128
B_DILS = (1, 4, 16)
ROPE_THETA = 10000.0
C_GROUPS = 64
C_GROUP = 16
C_STATE = 64
C_TC = 8
C_MIN_NEG_RE = -1e-4
MEM_LEN = 256
X_HEADS = 4
X_HD = 256
D_FF = 2816
N_DEV = 8
LANES = 128

ADAM_LR, ADAM_B1, ADAM_B2, ADAM_EPS, ADAM_WD, ADAM_STEP = 0.001, 0.9, 0.999, 1e-08, 0.01, 10

NEG_BIG = -1e30


def _tile(n, pref):
    for d in range(min(pref, n) // LANES * LANES, 0, -LANES):
        if n % d == 0:
            return d
    return n


def _cparams(*sem):
    return pltpu.CompilerParams(dimension_semantics=sem, vmem_limit_bytes=56 * 1024 * 1024)


def _sigmoid(x):
    return 0.5 * jnp.tanh(0.5 * x) + 0.5


def _erf(x):
    ax = jnp.abs(x)
    t = 1.0 / (1.0 + 0.3275911 * ax)
    poly = t * (0.254829592 + t * (-0.284496736 + t * (1.421413741 + t * (-1.453152027 + t * 1.061405429))))
    y = 1.0 - poly * jnp.exp(-ax * ax)
    return jnp.where(x < 0, -y, y)


_HBM = pl.BlockSpec(memory_space=pltpu.HBM)
_MESH = pl.DeviceIdType.MESH


def _logical(px, py, pc):
    return 4 * px + 2 * py + pc


class _Exchange:
    def __init__(self, gather=(), scatter=()):
        self.gather, self.scatter = list(gather), list(scatter)
        self.ng, self.n = len(self.gather), len(self.gather) + len(self.scatter)

    def operands(self):
        return self.gather + self.scatter

    def in_specs(self):
        return [_HBM] * self.n

    def out_shapes(self):
        return ([jax.ShapeDtypeStruct((N_DEV,) + g.shape, g.dtype) for g in self.gather]
                + [jax.ShapeDtypeStruct(s.shape, s.dtype) for s in self.scatter])

    def scratch(self):
        if not self.n:
            return []
        return [pltpu.SemaphoreType.DMA((self.n, 7)), pltpu.SemaphoreType.DMA((self.n, 7)), pltpu.SemaphoreType.DMA((self.n,))]

    def split(self, results):
        return list(results[:self.ng]), list(results[self.ng:])

    def run(self, ins, outs, sems, first, last):
        if not self.n:
            return
        send_sems, recv_sems, local_sems = sems
        x, y, c = lax.axis_index("x"), lax.axis_index("y"), lax.axis_index("c")
        me, sibling = _logical(x, y, c), (x, y, 1 - c)
        chips = [(1 - x, y), (x, 1 - y), (1 - x, 1 - y)]
        peers = [(x ^ (k >> 2), y ^ ((k >> 1) & 1), c ^ (k & 1)) for k in range(1, N_DEV)]

        def remote(t, k, src, dst, to):
            return pltpu.make_async_remote_copy(src_ref=src, dst_ref=dst, send_sem=send_sems.at[t, k],
                                                recv_sem=recv_sems.at[t, k], device_id=to, device_id_type=_MESH)

        def local(t):
            src = ins[t] if t < self.ng else ins[t].at[me]
            return pltpu.make_async_copy(src, outs[t].at[me], local_sems.at[t])

        @pl.when(first)
        def _():
            for t in range(self.n):
                local(t).start()
                if t < self.ng:
                    remote(t, 0, ins[t], outs[t].at[me], sibling).start()
                    for j, chip in enumerate(chips):
                        remote(t, 1 + j, ins[t], outs[t].at[me], (*chip, c)).start()
                else:
                    for k, peer in enumerate(peers):
                        remote(t, k, ins[t].at[_logical(*peer)], outs[t].at[me], peer).start()

        @pl.when(last)
        def _():
            for j, chip in enumerate(chips):
                for t in range(self.ng):
                    landed = outs[t].at[_logical(*chip, c)]
                    remote(t, 1 + j, ins[t], landed, sibling).wait_recv()
                    remote(t, 4 + j, landed, landed, sibling).start()
            for t in range(self.n):
                if t < self.ng:
                    remote(t, 0, ins[t], outs[t].at[_logical(*sibling)], sibling).wait_recv()
                    for j, chip in enumerate(chips):
                        remote(t, 4 + j, ins[t], outs[t].at[_logical(*chip, 1 - c)], sibling).wait_recv()
                    for k in range(7):
                        remote(t, k, ins[t], outs[t].at[me], sibling).wait_send()
                else:
                    for k, peer in enumerate(peers):
                        remote(t, k, ins[t].at[me], outs[t].at[_logical(*peer)], peer).wait_recv()
                    for k, peer in enumerate(peers):
                        remote(t, k, ins[t].at[_logical(*peer)], outs[t].at[me], peer).wait_send()
                local(t).wait()


_MM_VMEM_BUDGET = 36 * 1024 * 1024


def _mm(a, b, *, ta=False, tb=False, out_dtype=F32, name, tiles=(1408, 1408, 4096), gather=(), scatter=(), gate=None):
    M, K = (a.shape[1], a.shape[0]) if ta else a.shape
    N = b.shape[0] if tb else b.shape[1]
    assert (b.shape[1] if tb else b.shape[0]) == K
    gate_mode = gate[0] if gate else None
    tm, tn, tk = _tile(M, tiles[0]), _tile(N, tiles[1]), _tile(K, tiles[2])
    if gate_mode == "fwd":
        bs = _gate_block(N // 2)
        tn = 2 * bs
    elif gate_mode == "bwd":
        bs = _gate_block(N)
        tn = bs

    def vmem_bytes():
        acc = 4 * tm * tn if tk < K else 0
        if gate_mode == "fwd":
            io = (2 * (2 + 1) + 4) * tm * tn
        elif gate_mode == "bwd":
            io = (2 * (4 + 4) + 4) * tm * tn
        else:
            io = 2 * tm * tn * jnp.dtype(out_dtype).itemsize
        return 2 * 2 * (tm * tk + tk * tn) + acc + io

    while vmem_bytes() > _MM_VMEM_BUDGET:
        if gate_mode and tm > 256:
            tm = _tile(M, tm - LANES)
        elif tk > 512:
            tk = _tile(K, tk - LANES)
        elif tn > 256 and not gate_mode:
            tn = _tile(N, tn - LANES)
        else:
            tm = _tile(M, tm - LANES)
    ni, nj, nk = M // tm, N // tn, K // tk
    ex = _Exchange(gather, scatter)

    a_spec = pl.BlockSpec((tk, tm), lambda i, j, k: (k, i)) if ta else pl.BlockSpec((tm, tk), lambda i, j, k: (i, k))
    b_spec = pl.BlockSpec((tn, tk), lambda i, j, k: (j, k)) if tb else pl.BlockSpec((tk, tn), lambda i, j, k: (k, j))
    dims = (((0 if ta else 1,), (1 if tb else 0,)), ((), ()))
    n_acc = 1 if nk > 1 else 0
    n_in = 3 if gate_mode == "bwd" else 2
    n_out = 2 if gate_mode == "fwd" else 1

    def finish(val, in_refs, out_refs, rows=slice(None)):
        if gate_mode is None:
            out_refs[0][rows, :] = val.astype(out_refs[0].dtype)
        elif gate_mode == "fwd":
            out_refs[0][rows, :] = val.astype(BF16)
            for p in range(tn // (2 * bs)):
                a_, b_ = val[:, 2 * p * bs:(2 * p + 1) * bs], val[:, (2 * p + 1) * bs:(2 * p + 2) * bs]
                out_refs[1][rows, p * bs:(p + 1) * bs] = _gate_value(a_, b_, gate[1]).astype(BF16)
        else:
            z_ref = in_refs[2]
            for p in range(tn // bs):
                a_ = z_ref[rows, 2 * p * bs:(2 * p + 1) * bs].astype(F32)
                b_ = z_ref[rows, (2 * p + 1) * bs:(2 * p + 2) * bs].astype(F32)
                da, db = _gate_grads(a_, b_, val[:, p * bs:(p + 1) * bs], gate[1])
                out_refs[0][rows, 2 * p * bs:(2 * p + 1) * bs] = da.astype(BF16)
                out_refs[0][rows, (2 * p + 1) * bs:(2 * p + 2) * bs] = db.astype(BF16)

    halves = 2 if (gate_mode and nk == 1 and not ta and tm % 32 == 0) else 1

    def body(*refs):
        in_refs, rest = refs[:n_in], refs[n_in:]
        ex_in, out_refs = rest[:ex.n], rest[ex.n:ex.n + n_out]
        ex_out, scratch = rest[ex.n + n_out:2 * ex.n + n_out], rest[2 * ex.n + n_out:]
        i, j, k = pl.program_id(0), pl.program_id(1), pl.program_id(2)
        ex.run(ex_in, ex_out, scratch[n_acc:], (i == 0) & (j == 0) & (k == 0), (i == ni - 1) & (j == nj - 1) & (k == nk - 1))
        if halves > 1:
            rh = tm // halves
            rhs = in_refs[1][...].astype(BF16)
            parts = [lax.dot_general(in_refs[0][r * rh:(r + 1) * rh, :].astype(BF16), rhs, dims, preferred_element_type=F32)
                     for r in range(halves)]
            for r, p in enumerate(parts):
                finish(p, in_refs, out_refs, slice(r * rh, (r + 1) * rh))
            return
        part = lax.dot_general(in_refs[0][...].astype(BF16), in_refs[1][...].astype(BF16), dims, preferred_element_type=F32)
        if nk == 1:
            finish(part, in_refs, out_refs)
            return
        acc_ref = scratch[0]

        @pl.when(k == 0)
        def _():
            acc_ref[...] = part

        @pl.when(k > 0)
        def _():
            acc_ref[...] += part

        @pl.when(k == nk - 1)
        def _():
            finish(acc_ref[...], in_refs, out_refs)

    tile = lambda width: pl.BlockSpec((tm, width), lambda i, j, k: (i, j))
    if gate_mode == "fwd":
        out_specs, out_shape = [tile(tn), tile(tn // 2)], [jax.ShapeDtypeStruct((M, N), BF16), jax.ShapeDtypeStruct((M, N // 2), BF16)]
    elif gate_mode == "bwd":
        out_specs, out_shape = [tile(2 * tn)], [jax.ShapeDtypeStruct((M, 2 * N), BF16)]
    else:
        out_specs, out_shape = [tile(tn)], [jax.ShapeDtypeStruct((M, N), out_dtype)]
    operands = [a, b] + ([gate[2]] if gate_mode == "bwd" else [])
    sem = ("arbitrary",) * 3 if ex.n else ("parallel", "parallel", "arbitrary")
    res = pl.pallas_call(
        body, name=name, grid=(ni, nj, nk),
        in_specs=[a_spec, b_spec] + ([tile(2 * tn)] if gate_mode == "bwd" else []) + ex.in_specs(),
        out_specs=out_specs + ex.in_specs(),
        out_shape=out_shape + ex.out_shapes(),
        scratch_shapes=([pltpu.VMEM((tm, tn), F32)] if nk > 1 else []) + ex.scratch(),
        compiler_params=_cparams(*sem),
    )(*operands, *ex.operands())
    main = res[0] if n_out == 1 else tuple(res[:n_out])
    if not ex.n:
        return main
    return (main,) + tuple(ex.split(res[n_out:]))


def _rms_fwd(x, w, res=None, *, name, out_dtype=F32, then=None):
    T, C = x.shape
    tm = _tile(T, 512)
    has_res = res is not None

    def norm(v, w_ref):
        return v * lax.rsqrt(jnp.mean(v * v, axis=-1, keepdims=True) + NORM_EPS) * w_ref[...]

    def body(*refs):
        x_ref, w_ref = refs[0], refs[1]
        y = norm(x_ref[...].astype(F32), w_ref)
        if has_res:
            y = y + refs[2][...]
        if then is None:
            refs[-1][...] = y.astype(refs[-1].dtype)
        else:
            refs[-2][...] = y.astype(refs[-2].dtype)
            refs[-1][...] = norm(y, refs[-3]).astype(refs[-1].dtype)

    row = pl.BlockSpec((tm, C), lambda i: (i, 0))
    vec = pl.BlockSpec((1, C), lambda i: (0, 0))
    ins = [x, w] + ([res] if has_res else []) + ([then[0]] if then else [])
    in_specs = [row, vec] + ([row] if has_res else []) + ([vec] if then else [])
    out_shape = [jax.ShapeDtypeStruct((T, C), out_dtype)] + ([jax.ShapeDtypeStruct((T, C), then[1])] if then else [])
    res_ = pl.pallas_call(
        body, name=name, grid=(T // tm,), in_specs=in_specs, out_specs=[row] * len(out_shape),
        out_shape=out_shape, compiler_params=_cparams("parallel"),
    )(*ins)
    return tuple(res_) if then else res_[0]


def _rms_bwd(x, w, dy, add=None, *, name, out_dtype=F32, then=None):
    T, C = x.shape
    tm = _tile(T, 512)
    has_add = add is not None
    n_in = 3 + has_add + (2 if then else 0)

    def grads(x_ref, w_ref, g, dw_ref):
        xv = x_ref[...].astype(F32)
        r = lax.rsqrt(jnp.mean(xv * xv, axis=-1, keepdims=True) + NORM_EPS)
        xh = xv * r
        part = jnp.sum(g * xh, axis=0, keepdims=True)

        @pl.when(pl.program_id(0) == 0)
        def _():
            dw_ref[...] = part

        @pl.when(pl.program_id(0) > 0)
        def _():
            dw_ref[...] += part

        gx = g * w_ref[...]
        return r * (gx - xh * jnp.mean(gx * xh, axis=-1, keepdims=True))

    def body(*refs):
        ins, outs = refs[:n_in], refs[n_in:]
        dx = grads(ins[0], ins[1], ins[2][...].astype(F32), outs[1])
        if has_add:
            dx = dx + ins[3][...]
        outs[0][...] = dx.astype(outs[0].dtype)
        if then:
            outs[2][...] = grads(ins[-2], ins[-1], dx, outs[3]).astype(BF16)

    row = pl.BlockSpec((tm, C), lambda i: (i, 0))
    vec = pl.BlockSpec((1, C), lambda i: (0, 0))
    ins = [x, w, dy] + ([add] if has_add else []) + (list(then) if then else [])
    big, small = jax.ShapeDtypeStruct((T, C), out_dtype), jax.ShapeDtypeStruct((1, C), F32)
    return pl.pallas_call(
        body, name=name, grid=(T // tm,),
        in_specs=[row, vec, row] + ([row] if has_add else []) + ([row, vec] if then else []),
        out_specs=[row, vec] + ([row, vec] if then else []),
        out_shape=[big, small] + ([jax.ShapeDtypeStruct((T, C), BF16), small] if then else []),
        compiler_params=_cparams("arbitrary"),
    )(*ins)


def _gate_block(width):
    return _tile(width, 1408)


def _gate_value(a, b, kind):
    return a * _sigmoid(a) * b if kind == "swiglu" else a * _sigmoid(b)


def _gate_grads(a, b, d, kind):
    if kind == "swiglu":
        s = _sigmoid(a)
        return d * b * (s * (1.0 + a * (1.0 - s))), d * a * s
    s = _sigmoid(b)
    return d * s, d * a * s * (1.0 - s)


def _interleave_rows(w, inverse=False):
    W2, C = w.shape
    bs = _gate_block(W2 // 2)
    nb = W2 // 2 // bs
    shape = (nb, 2, bs, C) if inverse else (2, nb, bs, C)
    return w.reshape(shape).transpose(1, 0, 2, 3).reshape(W2, C)


def _gated_bwd(z, dout, kind, *, name):
    T, W2 = z.shape
    W = W2 // 2
    tm, bs = _tile(T, 512), _gate_block(W)

    def body(z_ref, d_ref, o_ref):
        da, db = _gate_grads(z_ref[:, :bs].astype(F32), z_ref[:, bs:].astype(F32), d_ref[...].astype(F32), kind)
        o_ref[:, :bs] = da.astype(o_ref.dtype)
        o_ref[:, bs:] = db.astype(o_ref.dtype)

    return pl.pallas_call(
        body, name=name, grid=(T // tm, W // bs),
        in_specs=[pl.BlockSpec((tm, 2 * bs), lambda i, j: (i, j)), pl.BlockSpec((tm, bs), lambda i, j: (i, j))],
        out_specs=pl.BlockSpec((tm, 2 * bs), lambda i, j: (i, j)),
        out_shape=jax.ShapeDtypeStruct((T, W2), BF16), compiler_params=_cparams("parallel", "parallel"),
    )(z, dout)


_NT = (((1,), (1,)), ((), ()))
_TN = (((0,), (0,)), ((), ()))


def _dot(a, b, dims=None, precision=None):
    if dims is None:
        return jnp.dot(a, b, preferred_element_type=F32, precision=precision)
    return lax.dot_general(a, b, dims, preferred_element_type=F32, precision=precision)


def _xattn_fwd(q, kv, B, L, *, name):
    T = q.shape[0]
    tq = 256
    nq = L // tq
    scale = X_HD ** -0.5

    def body(q_ref, k_ref, v_ref, o_ref):
        heads = [slice(h * X_HD, (h + 1) * X_HD) for h in range(X_HEADS)]
        s = [_dot(q_ref[:, sl].astype(BF16), k_ref[:, sl].astype(BF16), _NT) * scale for sl in heads]
        m = [jnp.max(a, axis=-1, keepdims=True) for a in s]
        p = [jnp.exp(a - b) for a, b in zip(s, m)]
        l = [jnp.sum(a, axis=-1, keepdims=True) for a in p]
        o = [_dot(a.astype(BF16), v_ref[:, sl].astype(BF16)) for a, sl in zip(p, heads)]
        for sl, a, b in zip(heads, o, l):
            o_ref[:, sl] = (a / b).astype(BF16)

    return pl.pallas_call(
        body, name=name, grid=(B, nq),
        in_specs=[pl.BlockSpec((tq, D_MODEL), lambda b, i: (b * nq + i, 0)),
                  pl.BlockSpec((MEM_LEN, D_MODEL), lambda b, i: (b, 0)),
                  pl.BlockSpec((MEM_LEN, D_MODEL), lambda b, i: (b, 1))],
        out_specs=pl.BlockSpec((tq, D_MODEL), lambda b, i: (b * nq + i, 0)),
        out_shape=jax.ShapeDtypeStruct((T, D_MODEL), BF16), compiler_params=_cparams("parallel", "parallel"),
    )(q, kv, kv)


def _xattn_bwd(q, kv, do, B, L, *, name):
    T = q.shape[0]
    tq = 256
    nq = L // tq
    scale = X_HD ** -0.5

    def body(q_ref, k_ref, v_ref, do_ref, dq_ref, dkv_ref):
        @pl.when(pl.program_id(1) == 0)
        def _():
            dkv_ref[...] = jnp.zeros_like(dkv_ref)

        heads = [slice(h * X_HD, (h + 1) * X_HD) for h in range(X_HEADS)]
        qs = [q_ref[:, sl].astype(BF16) for sl in heads]
        ks = [k_ref[:, sl].astype(BF16) for sl in heads]
        dos = [do_ref[:, sl].astype(BF16) for sl in heads]
        s = [_dot(a, b, _NT) * scale for a, b in zip(qs, ks)]
        dp = [_dot(a, v_ref[:, sl].astype(BF16), _NT) for a, sl in zip(dos, heads)]
        e = [jnp.exp(a - jnp.max(a, axis=-1, keepdims=True)) for a in s]
        p = [a / jnp.sum(a, axis=-1, keepdims=True) for a in e]
        ds = [(a * (b - jnp.sum(b * a, axis=-1, keepdims=True)) * scale).astype(BF16) for a, b in zip(p, dp)]
        dv = [_dot(a.astype(BF16), b, _TN) for a, b in zip(p, dos)]
        dq = [_dot(a, b) for a, b in zip(ds, ks)]
        dk = [_dot(a, b, _TN) for a, b in zip(ds, qs)]
        for h, sl in enumerate(heads):
            dq_ref[:, sl] = dq[h].astype(BF16)
            dkv_ref[:, sl] += dk[h]
            dkv_ref[:, D_MODEL + h * X_HD:D_MODEL + (h + 1) * X_HD] += dv[h]

    return pl.pallas_call(
        body, name=name, grid=(B, nq),
        in_specs=[pl.BlockSpec((tq, D_MODEL), lambda b, i: (b * nq + i, 0)),
                  pl.BlockSpec((MEM_LEN, D_MODEL), lambda b, i: (b, 0)),
                  pl.BlockSpec((MEM_LEN, D_MODEL), lambda b, i: (b, 1)),
                  pl.BlockSpec((tq, D_MODEL), lambda b, i: (b * nq + i, 0))],
        out_specs=[pl.BlockSpec((tq, D_MODEL), lambda b, i: (b * nq + i, 0)),
                   pl.BlockSpec((MEM_LEN, 2 * D_MODEL), lambda b, i: (b, 0))],
        out_shape=[jax.ShapeDtypeStruct((T, D_MODEL), BF16), jax.ShapeDtypeStruct((B * MEM_LEN, 2 * D_MODEL), F32)],
        compiler_params=_cparams("parallel", "arbitrary"),
    )(q, kv, kv, do)


def _chunk_masks():
    row = lax.broadcasted_iota(jnp.int32, (A_SUPER, A_SUPER), 0)
    col = lax.broadcasted_iota(jnp.int32, (A_SUPER, A_SUPER), 1)
    same = jnp.right_shift(row, 5) == jnp.right_shift(col, 5)
    return same, same & (col <= row), same & (col >= row)


def _dot_mask(mask, x):
    m = mask.astype(BF16)
    hi = x.astype(BF16)
    rest = x - hi.astype(F32)
    mid = rest.astype(BF16)
    lo = (rest - mid.astype(F32)).astype(BF16)
    return _dot(m, hi) + _dot(m, mid) + _dot(m, lo)


def _chunk_row(x, which):
    rows = [x[c * A_CHUNK + which % A_CHUNK:c * A_CHUNK + which % A_CHUNK + 1, :] for c in range(A_SUPER // A_CHUNK)]
    return jnp.concatenate([jnp.broadcast_to(r, (A_CHUNK, x.shape[1])) for r in rows], axis=0)


def _hgrn_gates(fa, lb):
    sig = _sigmoid(fa)
    f = lb + (1.0 - lb) * sig
    return sig, f, jnp.log(f), 1.0 - f


def _hgrn_fwd(z, lb, onw, B, L, *, name):
    T = B * L
    ns = L // A_SUPER
    nch = A_SUPER // A_CHUNK

    def body(q_ref, f_ref, v_ref, g_ref, lb_ref, w_ref, oa_ref, o_ref, s_ref, st_ref, sc_ref):
        @pl.when(pl.program_id(2) == 0)
        def _():
            st_ref[...] = jnp.zeros_like(st_ref)

        s_ref[0] = st_ref[...]
        same, tril, _ = _chunk_masks()
        q, v = q_ref[...], v_ref[...]
        _, _, lf, k = _hgrn_gates(f_ref[...], lb_ref[...])
        bcs = _dot_mask(tril, lf)
        bl = _chunk_row(bcs, -1)
        qd = (q * jnp.exp(bcs)).astype(BF16)
        ki = (k * jnp.exp(-bcs)).astype(BF16)
        ke = (k * jnp.exp(bl - bcs)).astype(BF16)
        dec = jnp.exp(bl)
        vb = v.astype(BF16)
        a = jnp.where(tril, _dot(qd, ki, _NT), 0.0)
        o_ref[...] = _dot(a.astype(BF16), vb)
        chunks = [slice(c * A_CHUNK, (c + 1) * A_CHUNK) for c in range(nch)]
        outer = [_dot(vb[rs], ke[rs], _TN) for rs in chunks]
        st = st_ref[...]
        for c, rs in enumerate(chunks):
            sc_ref[c] = st.astype(BF16)
            st = st * dec[c * A_CHUNK:c * A_CHUNK + 1, :] + outer[c]
        st_ref[...] = st
        for c, rs in enumerate(chunks):
            o_ref[rs, :] += _dot(qd[rs], sc_ref[c], _NT)
        o = o_ref[...]
        r = lax.rsqrt(jnp.mean(o * o, axis=-1, keepdims=True) + NORM_EPS)
        g = g_ref[...]
        oa_ref[...] = o * r * w_ref[...] * (g * _sigmoid(g))

    def zspec(off):
        return pl.BlockSpec((A_SUPER, A_HEAD), lambda b, h, n: (b * ns + n, off + h))

    hvec = pl.BlockSpec((1, A_HEAD), lambda b, h, n: (0, h))
    ospec = pl.BlockSpec((A_SUPER, A_HEAD), lambda b, h, n: (b * ns + n, h))
    return pl.pallas_call(
        body, name=name, grid=(B, 4, ns),
        in_specs=[zspec(0), zspec(4), zspec(8), zspec(12), hvec, hvec],
        out_specs=[ospec, ospec, pl.BlockSpec((1, A_HEAD, A_HEAD), lambda b, h, n: ((b * 4 + h) * ns + n, 0, 0))],
        out_shape=[jax.ShapeDtypeStruct((T, A_WIDTH), F32), jax.ShapeDtypeStruct((T, A_WIDTH), F32),
                   jax.ShapeDtypeStruct((B * 4 * ns, A_HEAD, A_HEAD), F32)],
        scratch_shapes=[pltpu.VMEM((A_HEAD, A_HEAD), F32), pltpu.VMEM((nch, A_HEAD, A_HEAD), BF16)],
        compiler_params=_cparams("parallel", "parallel", "arbitrary"),
    )(z, z, z, z, lb, onw)


def _hgrn_bwd(z, lb, onw, o_raw, s_start, doa, B, L, *, name):
    T = B * L
    ns = L // A_SUPER
    nch = A_SUPER // A_CHUNK

    def body(q_ref, f_ref, v_ref, g_ref, lb_ref, w_ref, o_ref, s_ref, doa_ref,
             dq_ref, df_ref, dv_ref, dg_ref, dw_ref, dlb_ref, dst_ref, sc_ref, dsc_ref, dqd_ref, dke_ref, dblx_ref):
        @pl.when(pl.program_id(2) == 0)
        def _():
            dst_ref[...] = jnp.zeros_like(dst_ref)
            dw_ref[...] = jnp.zeros_like(dw_ref)
            dlb_ref[...] = jnp.zeros_like(dlb_ref)

        same, tril, triu = _chunk_masks()
        q, v, g, lb, w = q_ref[...], v_ref[...], g_ref[...], lb_ref[...], w_ref[...]
        sig, f, lf, k = _hgrn_gates(f_ref[...], lb)
        bcs = _dot_mask(tril, lf)
        bl = _chunk_row(bcs, -1)
        eb, enb, eeb = jnp.exp(bcs), jnp.exp(-bcs), jnp.exp(bl - bcs)
        qd, ki, ke = q * eb, k * enb, k * eeb
        qdb, kib, keb, vb = qd.astype(BF16), ki.astype(BF16), ke.astype(BF16), v.astype(BF16)
        dec = jnp.exp(bl)
        o = o_ref[...]
        r = lax.rsqrt(jnp.mean(o * o, axis=-1, keepdims=True) + NORM_EPS)
        on = o * r
        sg = _sigmoid(g)
        silu_g = g * sg
        doa = doa_ref[...]
        dg_ref[...] = doa * on * w * (sg * (1.0 + g * (1.0 - sg)))
        dw_ref[0] += jnp.sum(doa * on * silu_g, axis=0, keepdims=True)
        don = doa * w * silu_g
        do = r * (don - on * jnp.mean(don * on, axis=-1, keepdims=True))
        dob = do.astype(BF16)
        a = jnp.where(tril, _dot(qdb, kib, _NT), 0.0).astype(BF16)
        da = jnp.where(tril, _dot(dob, vb, _NT), 0.0).astype(BF16)
        dv_ref[...] = _dot(a, dob, _TN)
        dqd_ref[...] = _dot(da, kib)
        dki = _dot(da, qdb, _TN)
        chunks = [slice(c * A_CHUNK, (c + 1) * A_CHUNK) for c in range(nch)]
        outer = [_dot(vb[rs], keb[rs], _TN) for rs in chunks]
        st = s_ref[0]
        for c in range(nch):
            sc_ref[c] = st
            st = st * dec[c * A_CHUNK:c * A_CHUNK + 1, :] + outer[c]
        outer_g = [_dot(dob[rs], qdb[rs], _TN) for rs in chunks]
        dst = dst_ref[...]
        for c in reversed(range(nch)):
            dsc_ref[c] = dst
            dst = dst * dec[c * A_CHUNK:c * A_CHUNK + 1, :] + outer_g[c]
        dst_ref[...] = dst
        for c, rs in enumerate(chunks):
            dec_c = dec[c * A_CHUNK:c * A_CHUNK + 1, :]
            dsc, stc = dsc_ref[c], sc_ref[c]
            dscb = dsc.astype(BF16)
            dv_ref[rs, :] += _dot(keb[rs], dscb, _NT)
            dke_ref[rs, :] = _dot(vb[rs], dscb)
            ddec = jnp.sum(dsc * stc, axis=0, keepdims=True)
            dqd_ref[rs, :] += _dot(dob[rs], stc.astype(BF16))
            dblx_ref[rs, :] = jnp.broadcast_to(ddec * dec_c, (A_CHUNK, A_HEAD))
        dqd, dke = dqd_ref[...], dke_ref[...]
        dq_ref[...] = dqd * eb
        keke = dke * ke
        db = dqd * qd - dki * ki - keke
        sums = _dot_mask(triu, jnp.concatenate([db, keke], axis=1))
        dk = dki * enb + dke * eeb
        dlf = sums[:, :A_HEAD] + _chunk_row(sums[:, A_HEAD:], 0) + dblx_ref[...]
        dff = dlf / f - dk
        df_ref[...] = dff * (1.0 - lb) * sig * (1.0 - sig)
        dlb_ref[0] += jnp.sum(dff * (1.0 - sig), axis=0, keepdims=True)

    def rev(n):
        return ns - 1 - n

    def zspec(off):
        return pl.BlockSpec((A_SUPER, A_HEAD), lambda b, h, n: (b * ns + rev(n), off + h))

    hvec = pl.BlockSpec((1, A_HEAD), lambda b, h, n: (0, h))
    ospec = pl.BlockSpec((A_SUPER, A_HEAD), lambda b, h, n: (b * ns + rev(n), h))
    acc = pl.BlockSpec((1, 1, A_HEAD), lambda b, h, n: (b * 4 + h, 0, 0))
    big = jax.ShapeDtypeStruct((T, A_WIDTH), F32)
    small = jax.ShapeDtypeStruct((B * 4, 1, A_HEAD), F32)
    return pl.pallas_call(
        body, name=name, grid=(B, 4, ns),
        in_specs=[zspec(0), zspec(4), zspec(8), zspec(12), hvec, hvec, ospec,
                  pl.BlockSpec((1, A_HEAD, A_HEAD), lambda b, h, n: ((b * 4 + h) * ns + rev(n), 0, 0)), ospec],
        out_specs=[ospec, ospec, ospec, ospec, acc, acc],
        out_shape=[big, big, big, big, small, small],
        scratch_shapes=[pltpu.VMEM((A_HEAD, A_HEAD), F32), pltpu.VMEM((nch, A_HEAD, A_HEAD), F32),
                        pltpu.VMEM((nch, A_HEAD, A_HEAD), F32),
                        pltpu.VMEM((A_SUPER, A_HEAD), F32), pltpu.VMEM((A_SUPER, A_HEAD), F32),
                        pltpu.VMEM((A_SUPER, A_HEAD), F32)],
        compiler_params=_cparams("parallel", "parallel", "arbitrary"),
    )(z, z, z, z, lb, onw, o_raw, s_start, doa)


def _rope_tables(L):
    half = A_HEAD // 2
    inv_freq = ROPE_THETA ** (-jnp.arange(half, dtype=F32) / half)
    ang = jnp.arange(L, dtype=F32)[:, None] * inv_freq[None, :]
    cos, sin = jnp.cos(ang), jnp.sin(ang)
    return jnp.concatenate([cos, cos], axis=-1), jnp.concatenate([-sin, sin], axis=-1)


def _rope_fwd(z, cos2, sin2, B, L, *, name):
    T = B * L
    tm = 512
    nl = L // tm

    def body(x_ref, c_ref, s_ref, q_ref, k_ref):
        c, s = c_ref[...], s_ref[...]
        for h in range(8):
            x = x_ref[:, h * A_HEAD:(h + 1) * A_HEAD]
            out = x * c + pltpu.roll(x, A_HEAD // 2, 1) * s
            o_ref = q_ref if h < 4 else k_ref
            o_ref[:, (h % 4) * A_HEAD:(h % 4 + 1) * A_HEAD] = out

    tab = pl.BlockSpec((tm, A_HEAD), lambda i: (i % nl, 0))
    out = pl.BlockSpec((tm, 512), lambda i: (i, 0))
    return pl.pallas_call(
        body, name=name, grid=(T // tm,),
        in_specs=[pl.BlockSpec((tm, 1024), lambda i: (i, 2)), tab, tab], out_specs=[out, out],
        out_shape=[jax.ShapeDtypeStruct((T, 512), F32)] * 2, compiler_params=_cparams("parallel"),
    )(z, cos2, sin2)


def _rope_bwd(dqs, dks, dvs, cos2, sin2, B, L, *, name):
    T = B * L
    tm = 256
    nl = L // tm

    def body(*refs):
        c, s = refs[9][...], refs[10][...]
        o_ref = refs[11]
        for part in range(3):
            a_ref, b_ref, c_ref = refs[3 * part:3 * part + 3]
            for h in range(4):
                cols = slice(h * A_HEAD, (h + 1) * A_HEAD)
                d = a_ref[:, cols] + b_ref[:, cols] + c_ref[:, cols]
                if part < 2:
                    d = d * c - pltpu.roll(d, A_HEAD // 2, 1) * s
                o_ref[:, part * 512 + h * A_HEAD:part * 512 + (h + 1) * A_HEAD] = d

    blk = pl.BlockSpec((tm, 512), lambda i: (i, 0))
    tab = pl.BlockSpec((tm, A_HEAD), lambda i: (i % nl, 0))
    return pl.pallas_call(
        body, name=name, grid=(T // tm,), in_specs=[blk] * 9 + [tab, tab],
        out_specs=pl.BlockSpec((tm, 1536), lambda i: (i, 0)),
        out_shape=jax.ShapeDtypeStruct((T, 1536), F32), compiler_params=_cparams("parallel"),
    )(*dqs, *dks, *dvs, cos2, sin2)


def _band_masks():
    i = lax.broadcasted_iota(jnp.int32, (B_SPAN, B_SPAN), 0)
    j = lax.broadcasted_iota(jnp.int32, (B_SPAN, B_SPAN), 1)
    return i <= j, j <= i


def _dil_geom(dil, L):
    return B_SPAN * dil, L // (B_SPAN * dil), (1 if dil > 1 else 4)


def _dil_rows(r, dil):
    return pl.ds(r, B_SPAN, stride=dil) if dil > 1 else pl.ds(0, B_SPAN)


def _dil_groups(dil, hb, size=4):
    items = [(_dil_rows(r, dil), pl.ds(h * A_HEAD, A_HEAD)) for r in range(dil) for h in range(hb)]
    return [items[i:i + size] for i in range(0, len(items), size)]


def _dil_specs(dil, B, L):
    W, nb, hb = _dil_geom(dil, L)
    cw = A_HEAD * hb

    def at(col0, shift):
        def index(b, n, hh):
            return (b * nb + jnp.clip(n + shift, 0, nb - 1), col0 // cw + hh)
        return pl.BlockSpec((W, cw), index)

    return at


def _dil_fwd(qr, kr, z, dil, B, L, *, name):
    T = B * L
    W, nb, hb = _dil_geom(dil, L)
    has_prev = nb > 1
    scale = A_HEAD ** -0.5
    at = _dil_specs(dil, B, L)

    def body(*refs):
        if has_prev:
            q_ref, kc_ref, vc_ref, kp_ref, vp_ref, o_ref, l_ref = refs
        else:
            q_ref, kc_ref, vc_ref, o_ref, l_ref = refs
        mp, mc = _band_masks()
        mp = mp & (pl.program_id(1) > 0)

        def load(ref, at_):
            return ref[at_].astype(BF16)

        for group in _dil_groups(dil, hb):
            qs = [load(q_ref, i) for i in group]
            sc = [jnp.where(mc, _dot(q, load(kc_ref, i), _NT) * scale, NEG_BIG) for q, i in zip(qs, group)]
            m = [jnp.max(s, axis=-1, keepdims=True) for s in sc]
            if has_prev:
                sp = [jnp.where(mp, _dot(q, load(kp_ref, i), _NT) * scale, NEG_BIG) for q, i in zip(qs, group)]
                m = [jnp.maximum(a, jnp.max(s, axis=-1, keepdims=True)) for a, s in zip(m, sp)]
            pc = [jnp.exp(s - a) for s, a in zip(sc, m)]
            l = [jnp.sum(p, axis=-1, keepdims=True) for p in pc]
            o = [_dot(p.astype(BF16), load(vc_ref, i)) for p, i in zip(pc, group)]
            if has_prev:
                pp = [jnp.exp(s - a) for s, a in zip(sp, m)]
                l = [a + jnp.sum(p, axis=-1, keepdims=True) for a, p in zip(l, pp)]
                o = [a + _dot(p.astype(BF16), load(vp_ref, i)) for a, p, i in zip(o, pp, group)]
            for i, oi, li, mi in zip(group, o, l, m):
                o_ref[i] = oi / li
                l_ref[i] = jnp.broadcast_to(mi + jnp.log(li), (B_SPAN, A_HEAD))

    ins = [qr, kr, z] + ([kr, z] if has_prev else [])
    in_specs = [at(0, 0), at(0, 0), at(3072, 0)] + ([at(0, -1), at(3072, -1)] if has_prev else [])
    return pl.pallas_call(
        body, name=name, grid=(B, nb, 4 // hb), in_specs=in_specs, out_specs=[at(0, 0), at(0, 0)],
        out_shape=[jax.ShapeDtypeStruct((T, 512), F32)] * 2,
        compiler_params=_cparams("parallel", "parallel", "parallel"),
    )(*ins)


def _dil_combine(os_, ls_, *, name):
    T = os_[0].shape[0]
    tm = 256

    def body(o1, o2, o3, l1, l2, l3, ob_ref, lse_ref):
        a1, a2, a3 = l1[...], l2[...], l3[...]
        m = jnp.maximum(jnp.maximum(a1, a2), a3)
        e1, e2, e3 = jnp.exp(a1 - m), jnp.exp(a2 - m), jnp.exp(a3 - m)
        den = e1 + e2 + e3
        ob_ref[...] = (e1 * o1[...] + e2 * o2[...] + e3 * o3[...]) / den
        lse_ref[...] = m + jnp.log(den)

    blk = pl.BlockSpec((tm, 512), lambda i: (i, 0))
    return pl.pallas_call(
        body, name=name, grid=(T // tm,), in_specs=[blk] * 6, out_specs=[blk, blk],
        out_shape=[jax.ShapeDtypeStruct((T, 512), F32)] * 2, compiler_params=_cparams("parallel"),
    )(*[o.reshape(T, 512) for o in os_], *[l.reshape(T, 512) for l in ls_])


def _dil_bwd_q(qr, kr, z, dymix, out, lse, dil, B, L, *, name):
    T = B * L
    W, nb, hb = _dil_geom(dil, L)
    has_prev = nb > 1
    scale = A_HEAD ** -0.5
    at = _dil_specs(dil, B, L)

    def body(*refs):
        if has_prev:
            q_ref, kc_ref, vc_ref, do_ref, out_ref, lse_ref, kp_ref, vp_ref, dq_ref = refs
        else:
            q_ref, kc_ref, vc_ref, do_ref, out_ref, lse_ref, dq_ref = refs
        mp, mc = _band_masks()
        mp = mp & (pl.program_id(1) > 0)

        def load(ref, at_):
            return ref[at_].astype(BF16)

        for group in _dil_groups(dil, hb):
            qs = [load(q_ref, i) for i in group]
            dos = [do_ref[i] for i in group]
            delta = [jnp.sum(d * out_ref[i], axis=-1, keepdims=True) for d, i in zip(dos, group)]
            dob = [d.astype(BF16) for d in dos]
            lse = [lse_ref[i] for i in group]
            kc = [load(kc_ref, i) for i in group]
            pc = [jnp.where(mc, jnp.exp(_dot(q, k, _NT) * scale - a), 0.0) for q, k, a in zip(qs, kc, lse)]
            dsc = [p * (_dot(d, load(vc_ref, i), _NT) - dl) * scale for p, d, i, dl in zip(pc, dob, group, delta)]
            dq = [_dot(d.astype(BF16), k) for d, k in zip(dsc, kc)]
            if has_prev:
                kp = [load(kp_ref, i) for i in group]
                pp = [jnp.where(mp, jnp.exp(_dot(q, k, _NT) * scale - a), 0.0) for q, k, a in zip(qs, kp, lse)]
                dsp = [p * (_dot(d, load(vp_ref, i), _NT) - dl) * scale for p, d, i, dl in zip(pp, dob, group, delta)]
                dq = [a + _dot(d.astype(BF16), k) for a, d, k in zip(dq, dsp, kp)]
            for i, d in zip(group, dq):
                dq_ref[i] = d

    ins = [qr, kr, z, dymix, out, lse] + ([kr, z] if has_prev else [])
    in_specs = ([at(0, 0), at(0, 0), at(3072, 0), at(512, 0), at(0, 0), at(0, 0)]
                + ([at(0, -1), at(3072, -1)] if has_prev else []))
    return pl.pallas_call(
        body, name=name, grid=(B, nb, 4 // hb), in_specs=in_specs, out_specs=at(0, 0),
        out_shape=jax.ShapeDtypeStruct((T, 512), F32),
        compiler_params=_cparams("parallel", "parallel", "parallel"),
    )(*ins)


def _dil_bwd_kv(qr, kr, z, dymix, out, lse, dil, B, L, *, name):
    T = B * L
    W, nb, hb = _dil_geom(dil, L)
    has_next = nb > 1
    scale = A_HEAD ** -0.5
    at = _dil_specs(dil, B, L)

    def body(*refs):
        k_ref, v_ref = refs[0], refs[1]
        own = refs[2:6]
        nxt = refs[6:10] if has_next else None
        dk_ref, dv_ref = refs[-2], refs[-1]
        mp, mc = _band_masks()
        mp = mp & (pl.program_id(1) < nb - 1)
        groups = [(own, mc)] + ([(nxt, mp)] if has_next else [])

        for group in _dil_groups(dil, hb):
            kh = [k_ref[i].astype(BF16) for i in group]
            vh = [v_ref[i].astype(BF16) for i in group]
            dk, dv = [None] * len(group), [None] * len(group)
            for (q_ref, do_ref, out_ref, lse_ref), mask in groups:
                qs = [q_ref[i].astype(BF16) for i in group]
                dos = [do_ref[i] for i in group]
                delta = [jnp.sum(d * out_ref[i], axis=-1, keepdims=True) for d, i in zip(dos, group)]
                dob = [d.astype(BF16) for d in dos]
                p = [jnp.where(mask, jnp.exp(_dot(q, k, _NT) * scale - lse_ref[i]), 0.0) for q, k, i in zip(qs, kh, group)]
                dvn = [_dot(a.astype(BF16), d, _TN) for a, d in zip(p, dob)]
                ds = [a * (_dot(d, v, _NT) - dl) * scale for a, d, v, dl in zip(p, dob, vh, delta)]
                dkn = [_dot(d.astype(BF16), q, _TN) for d, q in zip(ds, qs)]
                dv = [n if o is None else o + n for o, n in zip(dv, dvn)]
                dk = [n if o is None else o + n for o, n in zip(dk, dkn)]
            for i, a, b in zip(group, dk, dv):
                dk_ref[i] = a
                dv_ref[i] = b

    ins = [kr, z, qr, dymix, out, lse] + ([qr, dymix, out, lse] if has_next else [])
    in_specs = ([at(0, 0), at(3072, 0), at(0, 0), at(512, 0), at(0, 0), at(0, 0)]
                + ([at(0, 1), at(512, 1), at(0, 1), at(0, 1)] if has_next else []))
    return pl.pallas_call(
        body, name=name, grid=(B, nb, 4 // hb), in_specs=in_specs, out_specs=[at(0, 0), at(0, 0)],
        out_shape=[jax.ShapeDtypeStruct((T, 512), F32)] * 2,
        compiler_params=_cparams("parallel", "parallel", "parallel"),
    )(*ins)


def _s5_build(lam_re, lam_im, log_dt, b_re, b_im, c_re, c_im):
    G, P, TC = C_GROUPS, C_STATE, C_TC
    lr = jnp.minimum(lam_re, C_MIN_NEG_RE)
    li = lam_im
    dt = jnp.exp(log_dt)[:, None]
    mag = jnp.exp(dt * lr)
    ar, ai = mag * jnp.cos(dt * li), mag * jnp.sin(dt * li)
    den = lr * lr + li * li
    zr = ((ar - 1.0) * lr + ai * li) / den
    zi = (ai * lr - (ar - 1.0) * li) / den
    bbr = zr[..., None] * b_re - zi[..., None] * b_im
    bbi = zr[..., None] * b_im + zi[..., None] * b_re
    ks = jnp.arange(TC + 1, dtype=F32)[:, None, None]
    pmag = jnp.exp(ks * (dt * lr)[None])
    pr, pi = pmag * jnp.cos(ks * (dt * li)[None]), pmag * jnp.sin(ks * (dt * li)[None])
    car = c_re[None] * pr[:, :, None, :] - c_im[None] * pi[:, :, None, :]
    cai = c_re[None] * pi[:, :, None, :] + c_im[None] * pr[:, :, None, :]
    kern = (jnp.einsum('lgop,gpc->lgco', car[:TC], bbr, precision=HI)
            - jnp.einsum('lgop,gpc->lgco', cai[:TC], bbi, precision=HI))
    pr_e, pi_e = pr[TC - 1 - jnp.arange(TC)], pi[TC - 1 - jnp.arange(TC)]
    er = pr_e[:, :, :, None] * bbr[None] - pi_e[:, :, :, None] * bbi[None]
    ei = pr_e[:, :, :, None] * bbi[None] + pi_e[:, :, :, None] * bbr[None]
    ez = jnp.stack([er, ei], axis=2).reshape(TC, C_NB, C_GB, 2, P, C_GROUP).transpose(1, 0, 2, 5, 3, 4)
    fz = jnp.stack([car[1:], -cai[1:]], axis=0).reshape(2, TC, C_NB, C_GB, C_GROUP, P).transpose(2, 0, 3, 5, 1, 4)
    return kern, ez, fz, pr[TC], pi[TC]


def _s5_dense(kern, ez, fz):
    TC = C_TC
    eye = jnp.eye(C_GB, dtype=kern.dtype)
    kbd = (kern.reshape(TC, C_NB, C_GB, C_GROUP, C_GROUP)[:, :, :, :, None, :]
           * eye[None, None, :, None, :, None]).reshape(TC, C_NB, LANES, LANES)
    lag = jnp.arange(TC)[None, :] - jnp.arange(TC)[:, None]
    ksel = jnp.where((lag >= 0)[:, :, None, None, None], kbd[jnp.clip(lag, 0, TC - 1)], 0)
    m8 = ksel.transpose(2, 0, 3, 1, 4).reshape(C_NB, C_W8, C_W8)
    e8 = (ez[:, :, :, :, :, None, :] * eye[None, None, :, None, None, :, None]).reshape(C_NB, C_W8, C_S8)
    f8 = (fz[:, :, :, :, :, None, :] * eye[None, None, :, None, None, :, None]).reshape(C_NB, C_S8, C_W8)
    return m8, e8, f8


C_NB = C_GROUPS * C_GROUP // LANES
C_GB = C_GROUPS // C_NB
C_W8 = C_TC * LANES
C_S8 = 2 * C_GB * C_STATE


def _s5_scan_tables(lam_re, lam_im, log_dt, nsteps):
    lr = jnp.minimum(lam_re, C_MIN_NEG_RE)
    dt = jnp.exp(log_dt)[:, None]
    ks = (C_TC * 2.0 ** jnp.arange(8, dtype=F32))[None, :, None]
    keep = (jnp.arange(8) < nsteps)[None, :, None]
    pmag = jnp.exp(ks * (dt * lr)[:, None, :])
    ang = ks * (dt * lam_im)[:, None, :]

    def blocks(t):
        return t.reshape(C_NB, C_GB, 8, C_STATE).transpose(0, 2, 1, 3).reshape(C_NB, 8, C_GB * C_STATE)

    pr = blocks(jnp.where(keep, pmag * jnp.cos(ang), 0.0))
    pi = blocks(jnp.where(keep, pmag * jnp.sin(ang), 0.0))
    return jnp.concatenate([pr, pr], axis=-1), jnp.concatenate([-pi, pi], axis=-1)


def _s5_rows(t, R):
    return pl.ds(t, R, stride=C_TC)


def _s5_fwd(u, dsk, m8, e8, f8, tab_r, tab_i, B, L, *, name):
    T = B * L
    R = L // C_TC
    nsteps = int(math.log2(R))

    def body(u_ref, d_ref, m_ref, e_ref, f_ref, tr_ref, ti_ref, gl_ref, y_ref, x8_ref, xs_ref):
        for t in range(C_TC):
            x8_ref[0, :, t * LANES:(t + 1) * LANES] = u_ref[_s5_rows(t, R), :].astype(BF16)
        x8 = x8_ref[0]
        x = _dot(x8, e_ref[0])
        row = lax.broadcasted_iota(jnp.int32, (R, C_S8), 0)
        for k in range(nsteps):
            s = 1 << k
            sh = pltpu.roll(x, s, 0)
            upd = tr_ref[0, k:k + 1, :] * sh + ti_ref[0, k:k + 1, :] * pltpu.roll(sh, C_S8 // 2, 1)
            x = x + jnp.where(row >= s, upd, 0.0)
        xs = jnp.where(row >= 1, pltpu.roll(x, 1, 0), 0.0)
        xs_ref[0] = xs
        y8 = _dot(x8, m_ref[0]) + _dot(xs.astype(BF16), f_ref[0])
        d = d_ref[...]
        for t in range(C_TC):
            rows = _s5_rows(t, R)
            y = y8[:, t * LANES:(t + 1) * LANES] + d * u_ref[rows, :]
            y_ref[rows, :] = y
            gl_ref[rows, :] = 0.5 * y * (1.0 + _erf(y * (2.0 ** -0.5)))

    tok = pl.BlockSpec((L, LANES), lambda c, b: (b, c))
    per_block = lambda shape: pl.BlockSpec((1,) + shape, lambda c, b: (c, 0, 0))
    per_step = lambda shape: pl.BlockSpec((1,) + shape, lambda c, b: (c * B + b, 0, 0))
    return pl.pallas_call(
        body, name=name, grid=(C_NB, B),
        in_specs=[tok, pl.BlockSpec((1, LANES), lambda c, b: (0, c)), per_block((C_W8, C_W8)), per_block((C_W8, C_S8)),
                  per_block((C_S8, C_W8)), per_block((8, C_S8)), per_block((8, C_S8))],
        out_specs=[tok, tok, per_step((R, C_W8)), per_step((R, C_S8))],
        out_shape=[jax.ShapeDtypeStruct((T, D_MODEL), F32), jax.ShapeDtypeStruct((T, D_MODEL), F32),
                   jax.ShapeDtypeStruct((C_NB * B, R, C_W8), BF16), jax.ShapeDtypeStruct((C_NB * B, R, C_S8), F32)],
        compiler_params=_cparams("parallel", "parallel"),
    )(u, dsk, m8, e8, f8, tab_r, tab_i)


def _s5_bwd(dgl, y, u, dsk, xs, m8, e8, f8, tab_r, tab_i, B, L, *, name):
    T = B * L
    R = L // C_TC
    nsteps = int(math.log2(R))

    def body(dgl_ref, y_ref, u_ref, d_ref, xs_ref, m_ref, e_ref, f_ref, tr_ref, ti_ref,
             du_ref, dy8_ref, de_ref, da_ref, dd_ref, dyf_ref):
        @pl.when(pl.program_id(1) == 0)
        def _():
            da_ref[...] = jnp.zeros_like(da_ref)
            dd_ref[...] = jnp.zeros_like(dd_ref)

        dd = jnp.zeros((1, LANES), F32)
        for t in range(C_TC):
            rows = _s5_rows(t, R)
            yv = y_ref[rows, :]
            cdf = 0.5 * (1.0 + _erf(yv * (2.0 ** -0.5)))
            pdf = jnp.exp(-0.5 * yv * yv) * (1.0 / math.sqrt(2.0 * math.pi))
            dy = dgl_ref[rows, :] * (cdf + yv * pdf)
            dd = dd + jnp.sum(dy * u_ref[rows, :], axis=0, keepdims=True)
            dyf_ref[:, t * LANES:(t + 1) * LANES] = dy
        dd_ref[...] += dd
        dy8 = dyf_ref[...].astype(BF16)
        dy8_ref[0] = dy8
        xs = xs_ref[0]
        gx = _dot(dy8, f_ref[0], _NT)
        row = lax.broadcasted_iota(jnp.int32, (R, C_S8), 0)
        for k in range(nsteps):
            s = 1 << k
            sh = pltpu.roll(gx, R - s, 0)
            upd = tr_ref[0, k:k + 1, :] * sh - ti_ref[0, k:k + 1, :] * pltpu.roll(sh, C_S8 // 2, 1)
            gx = gx + jnp.where(row + s < R, upd, 0.0)
        de_in = jnp.where(row + 1 < R, pltpu.roll(gx, R - 1, 0), 0.0)
        deb = de_in.astype(BF16)
        de_ref[0] = deb
        da_ref[0, 0:1, :] += jnp.sum(de_in * xs, axis=0, keepdims=True)
        da_ref[0, 1:2, :] += jnp.sum(de_in * pltpu.roll(xs, C_S8 // 2, 1), axis=0, keepdims=True)
        dx8 = _dot(dy8, m_ref[0], _NT) + _dot(deb, e_ref[0], _NT)
        d = d_ref[...]
        for t in range(C_TC):
            cols = slice(t * LANES, (t + 1) * LANES)
            du_ref[_s5_rows(t, R), :] = dx8[:, cols] + d * dyf_ref[:, cols]

    tok = pl.BlockSpec((L, LANES), lambda c, b: (b, c))
    vec = pl.BlockSpec((1, LANES), lambda c, b: (0, c))
    per_block = lambda shape: pl.BlockSpec((1,) + shape, lambda c, b: (c, 0, 0))
    per_step = lambda shape: pl.BlockSpec((1,) + shape, lambda c, b: (c * B + b, 0, 0))
    return pl.pallas_call(
        body, name=name, grid=(C_NB, B),
        in_specs=[tok, tok, tok, vec, per_step((R, C_S8)), per_block((C_W8, C_W8)),
                  per_block((C_W8, C_S8)), per_block((C_S8, C_W8)), per_block((8, C_S8)), per_block((8, C_S8))],
        out_specs=[tok, per_step((R, C_W8)), per_step((R, C_S8)), per_block((8, C_S8)), vec],
        out_shape=[jax.ShapeDtypeStruct((T, D_MODEL), F32), jax.ShapeDtypeStruct((C_NB * B, R, C_W8), BF16),
                   jax.ShapeDtypeStruct((C_NB * B, R, C_S8), BF16), jax.ShapeDtypeStruct((C_NB, 8, C_S8), F32),
                   jax.ShapeDtypeStruct((1, D_MODEL), F32)],
        scratch_shapes=[pltpu.VMEM((R, C_W8), F32)],
        compiler_params=_cparams("parallel", "arbitrary"),
    )(dgl, y, u, dsk, xs, m8, e8, f8, tab_r, tab_i)


def _bmm_tn(a, b, nb, *, name):
    a = a.reshape(nb, -1, a.shape[-1])
    b = b.reshape(nb, -1, b.shape[-1])
    K, M, N = a.shape[1], a.shape[2], b.shape[2]

    def body(a_ref, b_ref, o_ref):
        o_ref[0] = _dot(a_ref[0].astype(BF16), b_ref[0].astype(BF16), _TN)

    return pl.pallas_call(
        body, name=name, grid=(nb,),
        in_specs=[pl.BlockSpec((1, K, M), lambda c: (c, 0, 0)), pl.BlockSpec((1, K, N), lambda c: (c, 0, 0))],
        out_specs=pl.BlockSpec((1, M, N), lambda c: (c, 0, 0)),
        out_shape=jax.ShapeDtypeStruct((nb, M, N), F32), compiler_params=_cparams("parallel"),
    )(a, b)


def _loss_head(y, target, *, name):
    T, C = y.shape
    tm = 256

    def body(y_ref, t_ref, l_ref, d_ref):
        err = y_ref[...] - t_ref[...]
        d_ref[...] = err * (1.0 / C)
        sq = err * err
        part = jnp.zeros((8, LANES), F32)
        for r in range(0, tm, 8):
            for c in range(0, C, LANES):
                part = part + sq[r:r + 8, c:c + LANES]

        @pl.when(pl.program_id(0) == 0)
        def _():
            l_ref[...] = part

        @pl.when(pl.program_id(0) > 0)
        def _():
            l_ref[...] += part

    row = pl.BlockSpec((tm, C), lambda i: (i, 0))
    acc = pl.BlockSpec((8, LANES), lambda i: (0, 0))
    return pl.pallas_call(
        body, name=name, grid=(T // tm,), in_specs=[row, row], out_specs=[acc, row],
        out_shape=[jax.ShapeDtypeStruct((8, LANES), F32), jax.ShapeDtypeStruct((T, C), F32)],
        compiler_params=_cparams("arbitrary"),
    )(y, target)


def _adamw(w, g, m, v, *, name):
    shape = w.shape
    size = int(np.prod(shape))
    cols = LANES if (shape[-1] < LANES and size % LANES == 0) else shape[-1]
    rows = size // cols
    tm = _tile(rows, 256) if rows % 8 == 0 else rows
    w2, g2, m2, v2 = (t.reshape(rows, cols) for t in (w, g, m, v))

    def body(w_ref, g_ref, m_ref, v_ref, d_ref, nm_ref, nv_ref):
        gg = g_ref[...]
        nm = ADAM_B1 * m_ref[...] + (1.0 - ADAM_B1) * gg
        nv = ADAM_B2 * v_ref[...] + (1.0 - ADAM_B2) * (gg * gg)
        m_hat = nm / (1.0 - ADAM_B1 ** ADAM_STEP)
        v_hat = nv / (1.0 - ADAM_B2 ** ADAM_STEP)
        d_ref[...] = -ADAM_LR * (m_hat / (jnp.sqrt(v_hat) + ADAM_EPS) + ADAM_WD * w_ref[...])
        nm_ref[...] = nm
        nv_ref[...] = nv

    blk = pl.BlockSpec((tm, cols), lambda i: (i, 0))
    outs = pl.pallas_call(
        body, name=name, grid=(rows // tm,), in_specs=[blk] * 4, out_specs=[blk] * 3,
        out_shape=[jax.ShapeDtypeStruct((rows, cols), F32)] * 3, compiler_params=_cparams("parallel"),
    )(w2, g2, m2, v2)
    return tuple(o.reshape(shape) for o in outs)


def _all_gather(shard, *, name):
    R, C = shard.shape

    def body(x_ref, out_ref, send_sems, recv_sems, local_sem):
        x, y, c = lax.axis_index("x"), lax.axis_index("y"), lax.axis_index("c")
        me, sibling = (x, y, c), (x, y, 1 - c)
        chips = [(1 - x, y), (x, 1 - y), (1 - x, 1 - y)]

        def rows(px, py, pc):
            return out_ref.at[_logical(px, py, pc)]

        def copy(k, block, to, src=None):
            return pltpu.make_async_remote_copy(
                src_ref=rows(*block) if src is None else src, dst_ref=rows(*block),
                send_sem=send_sems.at[k], recv_sem=recv_sems.at[k], device_id=to, device_id_type=_MESH)

        mine = pltpu.make_async_copy(x_ref, rows(*me), local_sem)
        mine.start()
        first = [copy(0, me, sibling, src=x_ref)]
        first += [copy(1 + j, me, (*chip, c), src=x_ref) for j, chip in enumerate(chips)]
        for cp in first:
            cp.start()
        passed = [copy(4 + j, (*chip, c), sibling) for j, chip in enumerate(chips)]
        for j, chip in enumerate(chips):
            copy(1 + j, (*chip, c), me).wait_recv()
            passed[j].start()
        copy(0, sibling, me).wait_recv()
        for j, chip in enumerate(chips):
            copy(4 + j, (*chip, 1 - c), me).wait_recv()
        for cp in first + passed:
            cp.wait_send()
        mine.wait()

    return pl.pallas_call(
        body, name=name, out_shape=jax.ShapeDtypeStruct((N_DEV, R, C), shard.dtype),
        in_specs=[_HBM], out_specs=_HBM,
        scratch_shapes=[pltpu.SemaphoreType.DMA((7,)), pltpu.SemaphoreType.DMA((7,)), pltpu.SemaphoreType.DMA],
    )(shard)


def _gather_weights(shards, *, name):
    nt = len(shards)

    def body(*refs):
        ins, outs = refs[:nt], refs[nt:2 * nt]
        send_sems, recv_sems, local_sems = refs[2 * nt:]
        x, y, c = lax.axis_index("x"), lax.axis_index("y"), lax.axis_index("c")
        me, sibling = (x, y, c), (x, y, 1 - c)
        chips = [(1 - x, y), (x, 1 - y), (1 - x, 1 - y)]

        def copy(t, k, block, to, src=None):
            rows = outs[t].at[_logical(*block)]
            return pltpu.make_async_remote_copy(
                src_ref=rows if src is None else src, dst_ref=rows,
                send_sem=send_sems.at[t, k], recv_sem=recv_sems.at[t, k], device_id=to, device_id_type=_MESH)

        mine = [pltpu.make_async_copy(ins[t], outs[t].at[_logical(*me)], local_sems.at[t]) for t in range(nt)]
        for cp in mine:
            cp.start()
        started = []
        for t in range(nt):
            started.append(copy(t, 0, me, sibling, src=ins[t]))
            started += [copy(t, 1 + j, me, (*chip, c), src=ins[t]) for j, chip in enumerate(chips)]
        for cp in started:
            cp.start()
        for j, chip in enumerate(chips):
            for t in range(nt):
                copy(t, 1 + j, (*chip, c), me).wait_recv()
                fwd = copy(t, 4 + j, (*chip, c), sibling)
                fwd.start()
                started.append(fwd)
        for t in range(nt):
            copy(t, 0, sibling, me).wait_recv()
            for j, chip in enumerate(chips):
                copy(t, 4 + j, (*chip, 1 - c), me).wait_recv()
        for cp in started:
            cp.wait_send()
        for cp in mine:
            cp.wait()

    return pl.pallas_call(
        body, name=name, out_shape=[jax.ShapeDtypeStruct((N_DEV,) + s.shape, s.dtype) for s in shards],
        in_specs=[_HBM] * nt, out_specs=[_HBM] * nt,
        scratch_shapes=[pltpu.SemaphoreType.DMA((nt, 7)), pltpu.SemaphoreType.DMA((nt, 7)), pltpu.SemaphoreType.DMA((nt,))],
    )(*shards)


def _sum_rows(stacked, *, name):
    _, R, C = stacked.shape
    tr = R
    if N_DEV * R * C * stacked.dtype.itemsize > 12 * 1024 * 1024:
        for cand in range(512, 15, -16):
            if R % cand == 0:
                tr = cand
                break

    def body(s_ref, o_ref):
        acc = s_ref[0].astype(F32)
        for k in range(1, N_DEV):
            acc = acc + s_ref[k].astype(F32)
        o_ref[...] = acc

    return pl.pallas_call(
        body, name=name, grid=(R // tr,),
        in_specs=[pl.BlockSpec((N_DEV, tr, C), lambda i: (0, i, 0))], out_specs=pl.BlockSpec((tr, C), lambda i: (i, 0)),
        out_shape=jax.ShapeDtypeStruct((R, C), F32), compiler_params=_cparams("parallel"),
    )(stacked)


_LARGE = (("ab_w_in", 1, True), ("ab_w_out", 1, False), ("s5_w_glu", 1, True), ("xattn_wq", 2, False),
          ("xattn_wkv", 2, True), ("xattn_wo", 2, False), ("ffn_w_in", 2, True), ("ffn_w_out", 2, False))
_LARGE_KEYS = tuple((n, l) for n, layers, _ in _LARGE for l in range(layers))
_TRANSPOSED = {n: t for n, _, t in _LARGE}


def _owner_major(name, w):
    return w.T if _TRANSPOSED[name] else w


def _lb_from_logits(logits):
    return jnp.cumsum(jax.nn.softmax(logits, axis=0), axis=0)[0:1]


def _local_step(x, mem, target, W, shards=None):
    B, L, _ = x.shape
    T = B * L
    x0 = x.reshape(T, D_MODEL)
    memf = mem.reshape(B * MEM_LEN, D_MODEL)
    nw = W["norm_w"]
    cos2, sin2 = _rope_tables(L)
    W = dict(W)
    G, received = {}, {}

    def mmx(a, b, gather=(), scatter=(), **kw):
        if shards is None or not (gather or scatter):
            return _mm(a, b, **kw)
        out, gathered, got = _mm(a, b, gather=[shards[k] for k in gather],
                                 scatter=[G[k].reshape(N_DEV, -1, D_MODEL) for k in scatter], **kw)
        for k, g in zip(gather, gathered):
            W[k] = g.reshape(-1, D_MODEL)
        for k, r in zip(scatter, got):
            received[k] = r
        return out

    def vec(v):
        return v.reshape(1, -1)

    saved = []
    xin = x0
    for layer in range(2):
        s = {"x0": xin}
        tag = f"l{layer}"
        if layer == 0:
            h1 = _rms_fwd(xin, vec(nw[0, 0]), name="norm_pre_mix_l0", out_dtype=BF16)
        s["h1"] = h1
        if layer == 0:
            lb, lb_vjp = jax.vjp(_lb_from_logits, W["hgrn_lb_logits"])
            onw = W["hgrn_out_norm_w"].reshape(1, A_WIDTH)
            z = mmx(h1, W["ab_w_in", 0], tb=True, name="ab_in",
                    gather=[("ab_w_out", 0), ("xattn_wq", 0), ("xattn_wkv", 0), ("xattn_wo", 0), ("ffn_w_in", 0)])
            oa, o_raw, s_start = _hgrn_fwd(z, lb, onw, B, L, name="hgrn_fwd")
            qr, kr = _rope_fwd(z, cos2, sin2, B, L, name="rope_qk")
            os_, ls_ = [], []
            for dil in B_DILS:
                o_g, l_g = _dil_fwd(qr, kr, z, dil, B, L, name=f"dil_fwd_{dil}")
                os_.append(o_g)
                ls_.append(l_g)
            ob, lse = _dil_combine(os_, ls_, name="dil_combine")
            ymix = jnp.concatenate([oa, ob], axis=-1).astype(BF16)
            y1 = mmx(ymix, W["ab_w_out", 0], out_dtype=BF16, name="ab_out", gather=[("ffn_w_out", 0)])
            s.update(z=z, lb=lb, lb_vjp=lb_vjp, onw=onw, o_raw=o_raw, s_start=s_start, qr=qr, kr=kr, ob=ob, lse=lse, ymix=ymix)
        else:
            p5 = tuple(W[n][0] for n in ("s5_lambda_re", "s5_lambda_im", "s5_log_dt", "s5_b_re", "s5_b_im", "s5_c_re", "s5_c_im"))
            (kern, ez, fz, _, _), s5_vjp = jax.vjp(_s5_build, *p5)
            tab_r, tab_i = _s5_scan_tables(p5[0], p5[1], p5[2], int(math.log2(L // C_TC)))
            mats = _s5_dense(kern.astype(BF16), ez.astype(BF16), fz.astype(BF16)) + (tab_r, tab_i)
            _, dense_vjp = jax.vjp(_s5_dense, kern, ez, fz)
            dsk = W["s5_d"].reshape(1, D_MODEL)
            gl, ypre, x8, xs = _s5_fwd(h1, dsk, *mats, B, L, name="s5_fwd")
            w_glu = _interleave_rows(W["s5_w_glu", 0])
            zg, y1 = _mm(gl, w_glu, tb=True, name="s5_glu_in", gate=("fwd", "glu"))
            s.update(s5_vjp=s5_vjp, dense_vjp=dense_vjp, mats=mats, x8=x8, xs=xs, dsk=dsk, gl=gl, ypre=ypre, zg=zg, w_glu=w_glu)
        x1, h2 = _rms_fwd(y1, vec(nw[layer, 1]), xin, name=f"norm_post_mix_{tag}", then=(vec(nw[layer, 2]), BF16))
        memn = _rms_fwd(memf, vec(W["mem_norm_w"][layer]), name=f"norm_mem_{tag}", out_dtype=BF16)
        q = mmx(h2, W["xattn_wq", layer], out_dtype=BF16, name=f"x_q_{tag}", gather=[("s5_w_glu", 0)] if layer == 0 else [])
        kv = _mm(memn, W["xattn_wkv", layer], tb=True, out_dtype=BF16, name=f"x_kv_{tag}")
        o = _xattn_fwd(q, kv, B, L, name=f"x_attn_{tag}")
        y2 = mmx(o, W["xattn_wo", layer], out_dtype=BF16, name=f"x_o_{tag}", gather=[("xattn_wq", 1), ("xattn_wo", 1)] if layer == 0 else [])
        x2, h3 = _rms_fwd(y2, vec(nw[layer, 3]), x1, name=f"norm_post_x_{tag}", then=(vec(nw[layer, 4]), BF16))
        w_ffn_in = _interleave_rows(W["ffn_w_in", layer])
        zf, u = mmx(h3, w_ffn_in, tb=True, name=f"ffn_in_{tag}", gate=("fwd", "swiglu"),
                    gather=[("xattn_wkv", 1), ("ffn_w_in", 1), ("ffn_w_out", 1)] if layer == 0 else [])
        y3 = _mm(u, W["ffn_w_out", layer], out_dtype=BF16, name=f"ffn_out_{tag}")
        if layer == 0:
            x3, h1 = _rms_fwd(y3, vec(nw[0, 5]), x2, name="norm_post_ffn_l0", then=(vec(nw[1, 0]), F32))
        else:
            x3 = _rms_fwd(y3, vec(nw[layer, 5]), x2, name=f"norm_post_ffn_{tag}")
        s.update(y1=y1, x1=x1, h2=h2, memn=memn, q=q, kv=kv, o=o, y2=y2, x2=x2, h3=h3, zf=zf, u=u, y3=y3, w_ffn_in=w_ffn_in)
        saved.append(s)
        xin = x3

    loss_parts, dx = _loss_head(xin, target.reshape(T, D_MODEL), name="loss_head")

    d_norm = [[None] * 6 for _ in range(2)]
    d_memn = [None, None]
    for layer in (1, 0):
        s = saved[layer]
        tag = f"l{layer}"
        if layer == 1:
            dy3, d_norm[1][5] = _rms_bwd(s["y3"], vec(nw[1, 5]), dx, name="bnorm_post_ffn_l1", out_dtype=BF16)
        dzf = mmx(dy3, W["ffn_w_out", layer], tb=True, name=f"b_ffn_out_dx_{tag}", gate=("bwd", "swiglu", s["zf"]),
                  scatter=[("xattn_wkv", 1), ("s5_w_glu", 0)] if layer == 0 else [])
        G["ffn_w_out", layer] = _mm(s["u"], dy3, ta=True, out_dtype=BF16, name=f"b_ffn_out_dw_{tag}")
        G["ffn_w_in", layer] = _interleave_rows(
            mmx(dzf, s["h3"], ta=True, out_dtype=BF16, name=f"b_ffn_in_dw_{tag}", scatter=[("ffn_w_out", layer)]), inverse=True)
        dh3 = mmx(dzf, s["w_ffn_in"], out_dtype=BF16, name=f"b_ffn_in_dx_{tag}", scatter=[("ffn_w_in", layer)])
        dx, d_norm[layer][4], dy2, d_norm[layer][3] = _rms_bwd(
            s["x2"], vec(nw[layer, 4]), dh3, dx, name=f"bnorm_pre_ffn_{tag}", then=(s["y2"], vec(nw[layer, 3])))
        do = _mm(dy2, W["xattn_wo", layer], tb=True, out_dtype=BF16, name=f"b_x_o_dx_{tag}")
        G["xattn_wo", layer] = _mm(s["o"], dy2, ta=True, out_dtype=BF16, name=f"b_x_o_dw_{tag}")
        dq, dkv = _xattn_bwd(s["q"], s["kv"], do, B, L, name=f"b_x_attn_{tag}")
        G["xattn_wq", layer] = _mm(s["h2"], dq, ta=True, out_dtype=BF16, name=f"b_x_q_dw_{tag}")
        dh2 = _mm(dq, W["xattn_wq", layer], tb=True, out_dtype=BF16, name=f"b_x_q_dx_{tag}")
        G["xattn_wkv", layer] = _mm(dkv, s["memn"], ta=True, out_dtype=BF16, name=f"b_x_kv_dw_{tag}")
        dmemn = _mm(dkv, W["xattn_wkv", layer], out_dtype=BF16, name=f"b_x_kv_dx_{tag}")
        _, d_memn[layer] = _rms_bwd(memf, vec(W["mem_norm_w"][layer]), dmemn, name=f"bnorm_mem_{tag}", out_dtype=BF16)
        dx, d_norm[layer][2], dy1, d_norm[layer][1] = _rms_bwd(
            s["x1"], vec(nw[layer, 2]), dh2, dx, name=f"bnorm_pre_x_{tag}", then=(s["y1"], vec(nw[layer, 1])))
        if layer == 0:
            z = s["z"]
            dymix = _mm(dy1, W["ab_w_out", 0], tb=True, name="b_ab_out_dx")
            G["ab_w_out", 0] = _mm(s["ymix"], dy1, ta=True, out_dtype=BF16, name="b_ab_out_dw")
            dqa, dfa, dia, dga, d_onw, d_lb = _hgrn_bwd(z, s["lb"], s["onw"], s["o_raw"], s["s_start"], dymix, B, L, name="hgrn_bwd")
            dqs, dks, dvs = [], [], []
            for dil in B_DILS:
                dqs.append(_dil_bwd_q(s["qr"], s["kr"], z, dymix, s["ob"], s["lse"], dil, B, L, name=f"dil_bwd_q_{dil}"))
                dk_g, dv_g = _dil_bwd_kv(s["qr"], s["kr"], z, dymix, s["ob"], s["lse"], dil, B, L, name=f"dil_bwd_kv_{dil}")
                dks.append(dk_g)
                dvs.append(dv_g)
            dqkv = _rope_bwd(dqs, dks, dvs, cos2, sin2, B, L, name="b_rope")
            dz = jnp.concatenate([dqa, dfa, dia, dga, dqkv], axis=-1).astype(BF16)
            G["ab_w_in", 0] = mmx(dz, s["h1"], ta=True, out_dtype=BF16, name="b_ab_in_dw",
                                  scatter=[("xattn_wo", 0), ("xattn_wq", 0), ("xattn_wkv", 0), ("ab_w_out", 0)])
            dh1 = mmx(dz, W["ab_w_in", 0], out_dtype=BF16, name="b_ab_in_dx", scatter=[("ab_w_in", 0)])
            G["hgrn_out_norm_w"] = jnp.sum(d_onw.reshape(B, A_WIDTH), axis=0, keepdims=True)
            d_lb_row = jnp.sum(d_lb.reshape(B, A_WIDTH), axis=0, keepdims=True)
            G["hgrn_lb_logits"] = s["lb_vjp"](d_lb_row)[0]
        else:
            dzg = _gated_bwd(s["zg"], dy1, "glu", name="b_s5_glu")
            G["s5_w_glu", 0] = _interleave_rows(
                _mm(dzg, s["gl"], ta=True, out_dtype=BF16, name="b_s5_glu_dw"), inverse=True)
            dgl = mmx(dzg, s["w_glu"], name="b_s5_glu_dx", scatter=[("xattn_wo", 1), ("xattn_wq", 1)])
            dh1, dy8, de_in, da, d_dsk = _s5_bwd(dgl, s["ypre"], s["h1"], s["dsk"], s["xs"], *s["mats"], B, L, name="s5_bwd")
            dm8 = _bmm_tn(s["x8"], dy8, C_NB, name="s5_bwd_dm")
            df8 = _bmm_tn(s["xs"], dy8, C_NB, name="s5_bwd_df")
            de8 = _bmm_tn(s["x8"], de_in, C_NB, name="s5_bwd_de")
            half = C_S8 // 2
            da_r = (da[:, 0, :half] + da[:, 0, half:]).reshape(C_GROUPS, C_STATE)
            da_i = (da[:, 1, half:] - da[:, 1, :half]).reshape(C_GROUPS, C_STATE)
            gp = s["s5_vjp"](s["dense_vjp"]((dm8, de8, df8)) + (da_r, da_i))
            for n, gv in zip(("s5_lambda_re", "s5_lambda_im", "s5_log_dt", "s5_b_re", "s5_b_im", "s5_c_re", "s5_c_im"), gp):
                G[n] = gv[None]
            G["s5_d"] = d_dsk
        if layer == 1:
            dx, d_norm[1][0], dy3, d_norm[0][5] = _rms_bwd(
                s["x0"], vec(nw[1, 0]), dh1, dx, name="bnorm_pre_mix_l1", then=(saved[0]["y3"], vec(nw[0, 5])))
        else:
            dx, d_norm[0][0] = _rms_bwd(s["x0"], vec(nw[0, 0]), dh1, dx, name="bnorm_pre_mix_l0")

    G["norm_w"] = jnp.stack([jnp.concatenate(d_norm[l], axis=0) for l in range(2)])
    G["mem_norm_w"] = jnp.concatenate(d_memn, axis=0)
    if shards is not None:
        G.update(received)
    return loss_parts, dx.reshape(B, L, D_MODEL), G


_SMALL = (("norm_w", (2, 6, 1024)), ("mem_norm_w", (2, 1024)), ("hgrn_lb_logits", (3, 512)), ("hgrn_out_norm_w", (1, 512)),
          ("s5_lambda_re", (1, 64, 64)), ("s5_lambda_im", (1, 64, 64)), ("s5_log_dt", (1, 64)),
          ("s5_b_re", (1, 64, 64, 16)), ("s5_b_im", (1, 64, 64, 16)), ("s5_c_re", (1, 64, 16, 64)),
          ("s5_c_im", (1, 64, 16, 64)), ("s5_d", (1, 1024)))

_WEIGHT_ORDER = ('norm_w', 'mem_norm_w', 'ab_w_in', 'ab_w_out', 'hgrn_lb_logits', 'hgrn_out_norm_w', 's5_lambda_re',
                 's5_lambda_im', 's5_log_dt', 's5_b_re', 's5_b_im', 's5_c_re', 's5_c_im', 's5_d', 's5_w_glu', 'xattn_wq',
                 'xattn_wkv', 'xattn_wo', 'ffn_w_in', 'ffn_w_out')


def kernel(x, mem, norm_w, mem_norm_w, ab_w_in, ab_w_out, hgrn_lb_logits, hgrn_out_norm_w, s5_lambda_re, s5_lambda_im, s5_log_dt, s5_b_re, s5_b_im, s5_c_re, s5_c_im, s5_d, s5_w_glu, xattn_wq, xattn_wkv, xattn_wo, ffn_w_in, ffn_w_out, loss_target, m_norm_w, m_mem_norm_w, m_ab_w_in, m_ab_w_out, m_hgrn_lb_logits, m_hgrn_out_norm_w, m_s5_lambda_re, m_s5_lambda_im, m_s5_log_dt, m_s5_b_re, m_s5_b_im, m_s5_c_re, m_s5_c_im, m_s5_d, m_s5_w_glu, m_xattn_wq, m_xattn_wkv, m_xattn_wo, m_ffn_w_in, m_ffn_w_out, v_norm_w, v_mem_norm_w, v_ab_w_in, v_ab_w_out, v_hgrn_lb_logits, v_hgrn_out_norm_w, v_s5_lambda_re, v_s5_lambda_im, v_s5_log_dt, v_s5_b_re, v_s5_b_im, v_s5_c_re, v_s5_c_im, v_s5_d, v_s5_w_glu, v_xattn_wq, v_xattn_wkv, v_xattn_wo, v_ffn_w_in, v_ffn_w_out):
    local = dict(norm_w=norm_w, mem_norm_w=mem_norm_w, ab_w_in=ab_w_in, ab_w_out=ab_w_out, hgrn_lb_logits=hgrn_lb_logits,
                 hgrn_out_norm_w=hgrn_out_norm_w, s5_lambda_re=s5_lambda_re, s5_lambda_im=s5_lambda_im, s5_log_dt=s5_log_dt,
                 s5_b_re=s5_b_re, s5_b_im=s5_b_im, s5_c_re=s5_c_re, s5_c_im=s5_c_im, s5_d=s5_d, s5_w_glu=s5_w_glu,
                 xattn_wq=xattn_wq, xattn_wkv=xattn_wkv, xattn_wo=xattn_wo, ffn_w_in=ffn_w_in, ffn_w_out=ffn_w_out)
    mom_m = dict(zip(_WEIGHT_ORDER, (m_norm_w, m_mem_norm_w, m_ab_w_in, m_ab_w_out, m_hgrn_lb_logits, m_hgrn_out_norm_w, m_s5_lambda_re, m_s5_lambda_im, m_s5_log_dt, m_s5_b_re, m_s5_b_im, m_s5_c_re, m_s5_c_im, m_s5_d, m_s5_w_glu, m_xattn_wq, m_xattn_wkv, m_xattn_wo, m_ffn_w_in, m_ffn_w_out)))
    mom_v = dict(zip(_WEIGHT_ORDER, (v_norm_w, v_mem_norm_w, v_ab_w_in, v_ab_w_out, v_hgrn_lb_logits, v_hgrn_out_norm_w, v_s5_lambda_re, v_s5_lambda_im, v_s5_log_dt, v_s5_b_re, v_s5_b_im, v_s5_c_re, v_s5_c_im, v_s5_d, v_s5_w_glu, v_xattn_wq, v_xattn_wkv, v_xattn_wo, v_ffn_w_in, v_ffn_w_out)))
    dev = 4 * lax.axis_index("x") + 2 * lax.axis_index("y") + lax.axis_index("c")

    shards = {(n, l): _owner_major(n, local[n][l]).astype(BF16) for n, l in _LARGE_KEYS}
    first = ("ab_w_in", 0)
    W = {first: _gather_weights([shards[first]], name="gather_first")[0].reshape(-1, D_MODEL)}
    tiny =jnp.concatenate([norm_w.reshape(-1), s5_d.reshape(-1)])
    tiny = jnp.pad(tiny, (0, 16 * LANES - tiny.shape[0])).reshape(16, LANES)
    tiny_all = _all_gather(tiny, name="gather_tiny").reshape(N_DEV, 16 * LANES)
    W["norm_w"] = tiny_all[:, :12 * LANES].reshape(N_DEV, 2, 6, LANES).transpose(1, 2, 0, 3).reshape(2, 6, D_MODEL)
    W["s5_d"] = tiny_all[:, 12 * LANES:13 * LANES].reshape(1, D_MODEL)
    for n in ("mem_norm_w", "hgrn_lb_logits", "hgrn_out_norm_w", "s5_lambda_re", "s5_lambda_im", "s5_log_dt",
              "s5_b_re", "s5_b_im", "s5_c_re", "s5_c_im"):
        W[n] = local[n]

    loss_parts, grad_x, G = _local_step(x, mem, loss_target, W, shards)

    g_layers = {}
    for n, l in _LARGE_KEYS:
        g = _sum_rows(G[n, l], name=f"sum_grads_{n}_{l}")
        g_layers.setdefault(n, []).append(g.T if _TRANSPOSED[n] else g)
    g_local = {n: jnp.stack(gl) for n, gl in g_layers.items()}
    small =jnp.concatenate([G[n].reshape(-1) for n, _ in _SMALL] + [0.5 / D_MODEL * jnp.sum(loss_parts).reshape(1)])
    n_small = small.shape[0]
    small = jnp.pad(small, (0, (-n_small) % (8 * LANES))).reshape(-1, LANES)
    small_sum = _sum_rows(_all_gather(small, name="gather_small"), name="sum_small").reshape(-1)
    g_full, off = {}, 0
    for n, shp in _SMALL:
        size = int(np.prod(shp))
        g_full[n] = small_sum[off:off + size].reshape(shp)
        off += size
    loss = small_sum[off]
    grads = dict(g_local)
    for n, shp in _SMALL:
        if n == "norm_w":
            grads[n] = lax.dynamic_slice_in_dim(g_full[n], dev * LANES, LANES, axis=2)
        elif n == "s5_d":
            grads[n] = lax.dynamic_slice_in_dim(g_full[n], dev * LANES, LANES, axis=1)
        else:
            grads[n] = g_full[n]

    delta, new_m, new_v = {}, {}, {}
    for n in _WEIGHT_ORDER:
        delta[n], new_m[n], new_v[n] = _adamw(local[n], grads[n], mom_m[n], mom_v[n], name=f"adamw_{n}")
    return (loss, grad_x, *[grads[n] for n in _WEIGHT_ORDER], *[delta[n] for n in _WEIGHT_ORDER],
            *[new_m[n] for n in _WEIGHT_ORDER], *[new_v[n] for n in _WEIGHT_ORDER])
```

```python
import functools
import math

import numpy as np
import jax
import jax.numpy as jnp
from jax import lax
from jax.experimental import pallas as pl
from jax.experimental.pallas import tpu as pltpu

F32 = jnp.float32
BF16 = jnp.bfloat16
HI = lax.Precision.HIGHEST

D_MODEL = 1024
NORM_EPS = 1e-6
A_WIDTH = 512
A_HEAD = 128
A_CHUNK = 32
A_SUPER = 256
B_SPAN = 128
B_DILS = (1, 4, 16)
ROPE_THETA = 10000.0
C_GROUPS = 64
C_GROUP = 16
C_STATE = 64
C_TC = 8
C_MIN_NEG_RE = -1e-4
MEM_LEN = 256
X_HEADS = 4
X_HD = 256
D_FF = 2816
N_DEV = 8
LANES = 128

ADAM_LR, ADAM_B1, ADAM_B2, ADAM_EPS, ADAM_WD, ADAM_STEP = 0.001, 0.9, 0.999, 1e-08, 0.01, 10

NEG_BIG = -1e30


def _tile(n, pref):
    for d in range(min(pref, n) // LANES * LANES, 0, -LANES):
        if n % d == 0:
            return d
    return n


def _cparams(*sem):
    return pltpu.CompilerParams(dimension_semantics=sem, vmem_limit_bytes=56 * 1024 * 1024)


def _sigmoid(x):
    return 0.5 * jnp.tanh(0.5 * x) + 0.5


def _erf(x):
    ax = jnp.abs(x)
    t = 1.0 / (1.0 + 0.3275911 * ax)
    poly = t * (0.254829592 + t * (-0.284496736 + t * (1.421413741 + t * (-1.453152027 + t * 1.061405429))))
    y = 1.0 - poly * jnp.exp(-ax * ax)
    return jnp.where(x < 0, -y, y)


_HBM = pl.BlockSpec(memory_space=pltpu.HBM)
_MESH = pl.DeviceIdType.MESH


def _logical(px, py, pc):
    return 4 * px + 2 * py + pc


class _Exchange:
    def __init__(self, gather=(), scatter=()):
        self.gather, self.scatter = list(gather), list(scatter)
        self.ng, self.n = len(self.gather), len(self.gather) + len(self.scatter)

    def operands(self):
        return self.gather + self.scatter

    def in_specs(self):
        return [_HBM] * self.n

    def out_shapes(self):
        return ([jax.ShapeDtypeStruct((N_DEV,) + g.shape, g.dtype) for g in self.gather]
                + [jax.ShapeDtypeStruct(s.shape, s.dtype) for s in self.scatter])

    def scratch(self):
        if not self.n:
            return []
        return [pltpu.SemaphoreType.DMA((self.n, 7)), pltpu.SemaphoreType.DMA((self.n, 7)), pltpu.SemaphoreType.DMA((self.n,))]

    def split(self, results):
        return list(results[:self.ng]), list(results[self.ng:])

    def run(self, ins, outs, sems, first, last):
        if not self.n:
            return
        send_sems, recv_sems, local_sems = sems
        x, y, c = lax.axis_index("x"), lax.axis_index("y"), lax.axis_index("c")
        me, sibling = _logical(x, y, c), (x, y, 1 - c)
        chips = [(1 - x, y), (x, 1 - y), (1 - x, 1 - y)]
        peers = [(x ^ (k >> 2), y ^ ((k >> 1) & 1), c ^ (k & 1)) for k in range(1, N_DEV)]

        def remote(t, k, src, dst, to):
            return pltpu.make_async_remote_copy(src_ref=src, dst_ref=dst, send_sem=send_sems.at[t, k],
                                                recv_sem=recv_sems.at[t, k], device_id=to, device_id_type=_MESH)

        def local(t):
            src = ins[t] if t < self.ng else ins[t].at[me]
            return pltpu.make_async_copy(src, outs[t].at[me], local_sems.at[t])

        @pl.when(first)
        def _():
            for t in range(self.n):
                local(t).start()
                if t < self.ng:
                    remote(t, 0, ins[t], outs[t].at[me], sibling).start()
                    for j, chip in enumerate(chips):
                        remote(t, 1 + j, ins[t], outs[t].at[me], (*chip, c)).start()
                else:
                    for k, peer in enumerate(peers):
                        remote(t, k, ins[t].at[_logical(*peer)], outs[t].at[me], peer).start()

        @pl.when(last)
        def _():
            for j, chip in enumerate(chips):
                for t in range(self.ng):
                    landed = outs[t].at[_logical(*chip, c)]
                    remote(t, 1 + j, ins[t], landed, sibling).wait_recv()
                    remote(t, 4 + j, landed, landed, sibling).start()
            for t in range(self.n):
                if t < self.ng:
                    remote(t, 0, ins[t], outs[t].at[_logical(*sibling)], sibling).wait_recv()
                    for j, chip in enumerate(chips):
                        remote(t, 4 + j, ins[t], outs[t].at[_logical(*chip, 1 - c)], sibling).wait_recv()
                    for k in range(7):
                        remote(t, k, ins[t], outs[t].at[me], sibling).wait_send()
                else:
                    for k, peer in enumerate(peers):
                        remote(t, k, ins[t].at[me], outs[t].at[_logical(*peer)], peer).wait_recv()
                    for k, peer in enumerate(peers):
                        remote(t, k, ins[t].at[_logical(*peer)], outs[t].at[me], peer).wait_send()
                local(t).wait()


_MM_VMEM_BUDGET = 36 * 1024 * 1024


def _mm(a, b, *, ta=False, tb=False, out_dtype=F32, name, tiles=(1408, 1408, 4096), gather=(), scatter=(), gate=None):
    M, K = (a.shape[1], a.shape[0]) if ta else a.shape
    N = b.shape[0] if tb else b.shape[1]
    assert (b.shape[1] if tb else b.shape[0]) == K
    gate_mode = gate[0] if gate else None
    tm, tn, tk = _tile(M, tiles[0]), _tile(N, tiles[1]), _tile(K, tiles[2])
    if gate_mode == "fwd":
        bs = _gate_block(N // 2)
        tn = 2 * bs
    elif gate_mode == "bwd":
        bs = _gate_block(N)
        tn = bs

    def vmem_bytes():
        acc = 4 * tm * tn if tk < K else 0
        if gate_mode == "fwd":
            io = (2 * (2 + 1) + 4) * tm * tn
        elif gate_mode == "bwd":
            io = (2 * (4 + 4) + 4) * tm * tn
        else:
            io = 2 * tm * tn * jnp.dtype(out_dtype).itemsize
        return 2 * 2 * (tm * tk + tk * tn) + acc + io

    while vmem_bytes() > _MM_VMEM_BUDGET:
        if gate_mode and tm > 256:
            tm = _tile(M, tm - LANES)
        elif tk > 512:
            tk = _tile(K, tk - LANES)
        elif tn > 256 and not gate_mode:
            tn = _tile(N, tn - LANES)
        else:
            tm = _tile(M, tm - LANES)
    ni, nj, nk = M // tm, N // tn, K // tk
    ex = _Exchange(gather, scatter)

    a_spec = pl.BlockSpec((tk, tm), lambda i, j, k: (k, i)) if ta else pl.BlockSpec((tm, tk), lambda i, j, k: (i, k))
    b_spec = pl.BlockSpec((tn, tk), lambda i, j, k: (j, k)) if tb else pl.BlockSpec((tk, tn), lambda i, j, k: (k, j))
    dims = (((0 if ta else 1,), (1 if tb else 0,)), ((), ()))
    n_acc = 1 if nk > 1 else 0
    n_in = 3 if gate_mode == "bwd" else 2
    n_out = 2 if gate_mode == "fwd" else 1

    def finish(val, in_refs, out_refs, rows=slice(None)):
        if gate_mode is None:
            out_refs[0][rows, :] = val.astype(out_refs[0].dtype)
        elif gate_mode == "fwd":
            out_refs[0][rows, :] = val.astype(BF16)
            for p in range(tn // (2 * bs)):
                a_, b_ = val[:, 2 * p * bs:(2 * p + 1) * bs], val[:, (2 * p + 1) * bs:(2 * p + 2) * bs]
                out_refs[1][rows, p * bs:(p + 1) * bs] = _gate_value(a_, b_, gate[1]).astype(BF16)
        else:
            z_ref = in_refs[2]
            for p in range(tn // bs):
                a_ = z_ref[rows, 2 * p * bs:(2 * p + 1) * bs].astype(F32)
                b_ = z_ref[rows, (2 * p + 1) * bs:(2 * p + 2) * bs].astype(F32)
                da, db = _gate_grads(a_, b_, val[:, p * bs:(p + 1) * bs], gate[1])
                out_refs[0][rows, 2 * p * bs:(2 * p + 1) * bs] = da.astype(BF16)
                out_refs[0][rows, (2 * p + 1) * bs:(2 * p + 2) * bs] = db.astype(BF16)

    halves = 2 if (gate_mode and nk == 1 and not ta and tm % 32 == 0) else 1

    def body(*refs):
        in_refs, rest = refs[:n_in], refs[n_in:]
        ex_in, out_refs = rest[:ex.n], rest[ex.n:ex.n + n_out]
        ex_out, scratch = rest[ex.n + n_out:2 * ex.n + n_out], rest[2 * ex.n + n_out:]
        i, j, k = pl.program_id(0), pl.program_id(1), pl.program_id(2)
        ex.run(ex_in, ex_out, scratch[n_acc:], (i == 0) & (j == 0) & (k == 0), (i == ni - 1) & (j == nj - 1) & (k == nk - 1))
        if halves > 1:
            rh = tm // halves
            rhs = in_refs[1][...].astype(BF16)
            parts = [lax.dot_general(in_refs[0][r * rh:(r + 1) * rh, :].astype(BF16), rhs, dims, preferred_element_type=F32)
                     for r in range(halves)]
            for r, p in enumerate(parts):
                finish(p, in_refs, out_refs, slice(r * rh, (r + 1) * rh))
            return
        part = lax.dot_general(in_refs[0][...].astype(BF16), in_refs[1][...].astype(BF16), dims, preferred_element_type=F32)
        if nk == 1:
            finish(part, in_refs, out_refs)
            return
        acc_ref = scratch[0]

        @pl.when(k == 0)
        def _():
            acc_ref[...] = part

        @pl.when(k > 0)
        def _():
            acc_ref[...] += part

        @pl.when(k == nk - 1)
        def _():
            finish(acc_ref[...], in_refs, out_refs)

    tile = lambda width: pl.BlockSpec((tm, width), lambda i, j, k: (i, j))
    if gate_mode == "fwd":
        out_specs, out_shape = [tile(tn), tile(tn // 2)], [jax.ShapeDtypeStruct((M, N), BF16), jax.ShapeDtypeStruct((M, N // 2), BF16)]
    elif gate_mode == "bwd":
        out_specs, out_shape = [tile(2 * tn)], [jax.ShapeDtypeStruct((M, 2 * N), BF16)]
    else:
        out_specs, out_shape = [tile(tn)], [jax.ShapeDtypeStruct((M, N), out_dtype)]
    operands = [a, b] + ([gate[2]] if gate_mode == "bwd" else [])
    sem = ("arbitrary",) * 3 if ex.n else ("parallel", "parallel", "arbitrary")
    res = pl.pallas_call(
        body, name=name, grid=(ni, nj, nk),
        in_specs=[a_spec, b_spec] + ([tile(2 * tn)] if gate_mode == "bwd" else []) + ex.in_specs(),
        out_specs=out_specs + ex.in_specs(),
        out_shape=out_shape + ex.out_shapes(),
        scratch_shapes=([pltpu.VMEM((tm, tn), F32)] if nk > 1 else []) + ex.scratch(),
        compiler_params=_cparams(*sem),
    )(*operands, *ex.operands())
    main = res[0] if n_out == 1 else tuple(res[:n_out])
    if not ex.n:
        return main
    return (main,) + tuple(ex.split(res[n_out:]))


def _rms_fwd(x, w, res=None, *, name, out_dtype=F32, then=None):
    T, C = x.shape
    tm = _tile(T, 512)
    has_res = res is not None

    def norm(v, w_ref):
        return v * lax.rsqrt(jnp.mean(v * v, axis=-1, keepdims=True) + NORM_EPS) * w_ref[...]

    def body(*refs):
        x_ref, w_ref = refs[0], refs[1]
        y = norm(x_ref[...].astype(F32), w_ref)
        if has_res:
            y = y + refs[2][...]
        if then is None:
            refs[-1][...] = y.astype(refs[-1].dtype)
        else:
            refs[-2][...] = y.astype(refs[-2].dtype)
            refs[-1][...] = norm(y, refs[-3]).astype(refs[-1].dtype)

    row = pl.BlockSpec((tm, C), lambda i: (i, 0))
    vec = pl.BlockSpec((1, C), lambda i: (0, 0))
    ins = [x, w] + ([res] if has_res else []) + ([then[0]] if then else [])
    in_specs = [row, vec] + ([row] if has_res else []) + ([vec] if then else [])
    out_shape = [jax.ShapeDtypeStruct((T, C), out_dtype)] + ([jax.ShapeDtypeStruct((T, C), then[1])] if then else [])
    res_ = pl.pallas_call(
        body, name=name, grid=(T // tm,), in_specs=in_specs, out_specs=[row] * len(out_shape),
        out_shape=out_shape, compiler_params=_cparams("parallel"),
    )(*ins)
    return tuple(res_) if then else res_[0]


def _rms_bwd(x, w, dy, add=None, *, name, out_dtype=F32, then=None):
    T, C = x.shape
    tm = _tile(T, 512)
    has_add = add is not None
    n_in = 3 + has_add + (2 if then else 0)

    def grads(x_ref, w_ref, g, dw_ref):
        xv = x_ref[...].astype(F32)
        r = lax.rsqrt(jnp.mean(xv * xv, axis=-1, keepdims=True) + NORM_EPS)
        xh = xv * r
        part = jnp.sum(g * xh, axis=0, keepdims=True)

        @pl.when(pl.program_id(0) == 0)
        def _():
            dw_ref[...] = part

        @pl.when(pl.program_id(0) > 0)
        def _():
            dw_ref[...] += part

        gx = g * w_ref[...]
        return r * (gx - xh * jnp.mean(gx * xh, axis=-1, keepdims=True))

    def body(*refs):
        ins, outs = refs[:n_in], refs[n_in:]
        dx = grads(ins[0], ins[1], ins[2][...].astype(F32), outs[1])
        if has_add:
            dx = dx + ins[3][...]
        outs[0][...] = dx.astype(outs[0].dtype)
        if then:
            outs[2][...] = grads(ins[-2], ins[-1], dx, outs[3]).astype(BF16)

    row = pl.BlockSpec((tm, C), lambda i: (i, 0))
    vec = pl.BlockSpec((1, C), lambda i: (0, 0))
    ins = [x, w, dy] + ([add] if has_add else []) + (list(then) if then else [])
    big, small = jax.ShapeDtypeStruct((T, C), out_dtype), jax.ShapeDtypeStruct((1, C), F32)
    return pl.pallas_call(
        body, name=name, grid=(T // tm,),
        in_specs=[row, vec, row] + ([row] if has_add else []) + ([row, vec] if then else []),
        out_specs=[row, vec] + ([row, vec] if then else []),
        out_shape=[big, small] + ([jax.ShapeDtypeStruct((T, C), BF16), small] if then else []),
        compiler_params=_cparams("arbitrary"),
    )(*ins)


def _gate_block(width):
    return _tile(width, 1408)


def _gate_value(a, b, kind):
    return a * _sigmoid(a) * b if kind == "swiglu" else a * _sigmoid(b)


def _gate_grads(a, b, d, kind):
    if kind == "swiglu":
        s = _sigmoid(a)
        return d * b * (s * (1.0 + a * (1.0 - s))), d * a * s
    s = _sigmoid(b)
    return d * s, d * a * s * (1.0 - s)


def _interleave_rows(w, inverse=False):
    W2, C = w.shape
    bs = _gate_block(W2 // 2)
    nb = W2 // 2 // bs
    shape = (nb, 2, bs, C) if inverse else (2, nb, bs, C)
    return w.reshape(shape).transpose(1, 0, 2, 3).reshape(W2, C)


def _gated_bwd(z, dout, kind, *, name):
    T, W2 = z.shape
    W = W2 // 2
    tm, bs = _tile(T, 512), _gate_block(W)

    def body(z_ref, d_ref, o_ref):
        da, db = _gate_grads(z_ref[:, :bs].astype(F32), z_ref[:, bs:].astype(F32), d_ref[...].astype(F32), kind)
        o_ref[:, :bs] = da.astype(o_ref.dtype)
        o_ref[:, bs:] = db.astype(o_ref.dtype)

    return pl.pallas_call(
        body, name=name, grid=(T // tm, W // bs),
        in_specs=[pl.BlockSpec((tm, 2 * bs), lambda i, j: (i, j)), pl.BlockSpec((tm, bs), lambda i, j: (i, j))],
        out_specs=pl.BlockSpec((tm, 2 * bs), lambda i, j: (i, j)),
        out_shape=jax.ShapeDtypeStruct((T, W2), BF16), compiler_params=_cparams("parallel", "parallel"),
    )(z, dout)


_NT = (((1,), (1,)), ((), ()))
_TN = (((0,), (0,)), ((), ()))


def _dot(a, b, dims=None, precision=None):
    if dims is None:
        return jnp.dot(a, b, preferred_element_type=F32, precision=precision)
    return lax.dot_general(a, b, dims, preferred_element_type=F32, precision=precision)


def _xattn_fwd(q, kv, B, L, *, name):
    T = q.shape[0]
    tq = 256
    nq = L // tq
    scale = X_HD ** -0.5

    def body(q_ref, k_ref, v_ref, o_ref):
        heads = [slice(h * X_HD, (h + 1) * X_HD) for h in range(X_HEADS)]
        s = [_dot(q_ref[:, sl].astype(BF16), k_ref[:, sl].astype(BF16), _NT) * scale for sl in heads]
        m = [jnp.max(a, axis=-1, keepdims=True) for a in s]
        p = [jnp.exp(a - b) for a, b in zip(s, m)]
        l = [jnp.sum(a, axis=-1, keepdims=True) for a in p]
        o = [_dot(a.astype(BF16), v_ref[:, sl].astype(BF16)) for a, sl in zip(p, heads)]
        for sl, a, b in zip(heads, o, l):
            o_ref[:, sl] = (a / b).astype(BF16)

    return pl.pallas_call(
        body, name=name, grid=(B, nq),
        in_specs=[pl.BlockSpec((tq, D_MODEL), lambda b, i: (b * nq + i, 0)),
                  pl.BlockSpec((MEM_LEN, D_MODEL), lambda b, i: (b, 0)),
                  pl.BlockSpec((MEM_LEN, D_MODEL), lambda b, i: (b, 1))],
        out_specs=pl.BlockSpec((tq, D_MODEL), lambda b, i: (b * nq + i, 0)),
        out_shape=jax.ShapeDtypeStruct((T, D_MODEL), BF16), compiler_params=_cparams("parallel", "parallel"),
    )(q, kv, kv)


def _xattn_bwd(q, kv, do, B, L, *, name):
    T = q.shape[0]
    tq = 256
    nq = L // tq
    scale = X_HD ** -0.5

    def body(q_ref, k_ref, v_ref, do_ref, dq_ref, dkv_ref):
        @pl.when(pl.program_id(1) == 0)
        def _():
            dkv_ref[...] = jnp.zeros_like(dkv_ref)

        heads = [slice(h * X_HD, (h + 1) * X_HD) for h in range(X_HEADS)]
        qs = [q_ref[:, sl].astype(BF16) for sl in heads]
        ks = [k_ref[:, sl].astype(BF16) for sl in heads]
        dos = [do_ref[:, sl].astype(BF16) for sl in heads]
        s = [_dot(a, b, _NT) * scale for a, b in zip(qs, ks)]
        dp = [_dot(a, v_ref[:, sl].astype(BF16), _NT) for a, sl in zip(dos, heads)]
        e = [jnp.exp(a - jnp.max(a, axis=-1, keepdims=True)) for a in s]
        p = [a / jnp.sum(a, axis=-1, keepdims=True) for a in e]
        ds = [(a * (b - jnp.sum(b * a, axis=-1, keepdims=True)) * scale).astype(BF16) for a, b in zip(p, dp)]
        dv = [_dot(a.astype(BF16), b, _TN) for a, b in zip(p, dos)]
        dq = [_dot(a, b) for a, b in zip(ds, ks)]
        dk = [_dot(a, b, _TN) for a, b in zip(ds, qs)]
        for h, sl in enumerate(heads):
            dq_ref[:, sl] = dq[h].astype(BF16)
            dkv_ref[:, sl] += dk[h]
            dkv_ref[:, D_MODEL + h * X_HD:D_MODEL + (h + 1) * X_HD] += dv[h]

    return pl.pallas_call(
        body, name=name, grid=(B, nq),
        in_specs=[pl.BlockSpec((tq, D_MODEL), lambda b, i: (b * nq + i, 0)),
                  pl.BlockSpec((MEM_LEN, D_MODEL), lambda b, i: (b, 0)),
                  pl.BlockSpec((MEM_LEN, D_MODEL), lambda b, i: (b, 1)),
                  pl.BlockSpec((tq, D_MODEL), lambda b, i: (b * nq + i, 0))],
        out_specs=[pl.BlockSpec((tq, D_MODEL), lambda b, i: (b * nq + i, 0)),
                   pl.BlockSpec((MEM_LEN, 2 * D_MODEL), lambda b, i: (b, 0))],
        out_shape=[jax.ShapeDtypeStruct((T, D_MODEL), BF16), jax.ShapeDtypeStruct((B * MEM_LEN, 2 * D_MODEL), F32)],
        compiler_params=_cparams("parallel", "arbitrary"),
    )(q, kv, kv, do)


def _chunk_masks():
    row = lax.broadcasted_iota(jnp.int32, (A_SUPER, A_SUPER), 0)
    col = lax.broadcasted_iota(jnp.int32, (A_SUPER, A_SUPER), 1)
    same = jnp.right_shift(row, 5) == jnp.right_shift(col, 5)
    return same, same & (col <= row), same & (col >= row)


def _dot_mask(mask, x):
    m = mask.astype(BF16)
    hi = x.astype(BF16)
    rest = x - hi.astype(F32)
    mid = rest.astype(BF16)
    lo = (rest - mid.astype(F32)).astype(BF16)
    return _dot(m, hi) + _dot(m, mid) + _dot(m, lo)


def _chunk_row(x, which):
    rows = [x[c * A_CHUNK + which % A_CHUNK:c * A_CHUNK + which % A_CHUNK + 1, :] for c in range(A_SUPER // A_CHUNK)]
    return jnp.concatenate([jnp.broadcast_to(r, (A_CHUNK, x.shape[1])) for r in rows], axis=0)


def _hgrn_gates(fa, lb):
    sig = _sigmoid(fa)
    f = lb + (1.0 - lb) * sig
    return sig, f, jnp.log(f), 1.0 - f


def _hgrn_fwd(z, lb, onw, B, L, *, name):
    T = B * L
    ns = L // A_SUPER
    nch = A_SUPER // A_CHUNK

    def body(q_ref, f_ref, v_ref, g_ref, lb_ref, w_ref, oa_ref, o_ref, s_ref, st_ref, sc_ref):
        @pl.when(pl.program_id(2) == 0)
        def _():
            st_ref[...] = jnp.zeros_like(st_ref)

        s_ref[0] = st_ref[...]
        same, tril, _ = _chunk_masks()
        q, v = q_ref[...], v_ref[...]
        _, _, lf, k = _hgrn_gates(f_ref[...], lb_ref[...])
        bcs = _dot_mask(tril, lf)
        bl = _chunk_row(bcs, -1)
        qd = (q * jnp.exp(bcs)).astype(BF16)
        ki = (k * jnp.exp(-bcs)).astype(BF16)
        ke = (k * jnp.exp(bl - bcs)).astype(BF16)
        dec = jnp.exp(bl)
        vb = v.astype(BF16)
        a = jnp.where(tril, _dot(qd, ki, _NT), 0.0)
        o_ref[...] = _dot(a.astype(BF16), vb)
        chunks = [slice(c * A_CHUNK, (c + 1) * A_CHUNK) for c in range(nch)]
        outer = [_dot(vb[rs], ke[rs], _TN) for rs in chunks]
        st = st_ref[...]
        for c, rs in enumerate(chunks):
            sc_ref[c] = st.astype(BF16)
            st = st * dec[c * A_CHUNK:c * A_CHUNK + 1, :] + outer[c]
        st_ref[...] = st
        for c, rs in enumerate(chunks):
            o_ref[rs, :] += _dot(qd[rs], sc_ref[c], _NT)
        o = o_ref[...]
        r = lax.rsqrt(jnp.mean(o * o, axis=-1, keepdims=True) + NORM_EPS)
        g = g_ref[...]
        oa_ref[...] = o * r * w_ref[...] * (g * _sigmoid(g))

    def zspec(off):
        return pl.BlockSpec((A_SUPER, A_HEAD), lambda b, h, n: (b * ns + n, off + h))

    hvec = pl.BlockSpec((1, A_HEAD), lambda b, h, n: (0, h))
    ospec = pl.BlockSpec((A_SUPER, A_HEAD), lambda b, h, n: (b * ns + n, h))
    return pl.pallas_call(
        body, name=name, grid=(B, 4, ns),
        in_specs=[zspec(0), zspec(4), zspec(8), zspec(12), hvec, hvec],
        out_specs=[ospec, ospec, pl.BlockSpec((1, A_HEAD, A_HEAD), lambda b, h, n: ((b * 4 + h) * ns + n, 0, 0))],
        out_shape=[jax.ShapeDtypeStruct((T, A_WIDTH), F32), jax.ShapeDtypeStruct((T, A_WIDTH), F32),
                   jax.ShapeDtypeStruct((B * 4 * ns, A_HEAD, A_HEAD), F32)],
        scratch_shapes=[pltpu.VMEM((A_HEAD, A_HEAD), F32), pltpu.VMEM((nch, A_HEAD, A_HEAD), BF16)],
        compiler_params=_cparams("parallel", "parallel", "arbitrary"),
    )(z, z, z, z, lb, onw)


def _hgrn_bwd(z, lb, onw, o_raw, s_start, doa, B, L, *, name):
    T = B * L
    ns = L // A_SUPER
    nch = A_SUPER // A_CHUNK

    def body(q_ref, f_ref, v_ref, g_ref, lb_ref, w_ref, o_ref, s_ref, doa_ref,
             dq_ref, df_ref, dv_ref, dg_ref, dw_ref, dlb_ref, dst_ref, sc_ref, dsc_ref, dqd_ref, dke_ref, dblx_ref):
        @pl.when(pl.program_id(2) == 0)
        def _():
            dst_ref[...] = jnp.zeros_like(dst_ref)
            dw_ref[...] = jnp.zeros_like(dw_ref)
            dlb_ref[...] = jnp.zeros_like(dlb_ref)

        same, tril, triu = _chunk_masks()
        q, v, g, lb, w = q_ref[...], v_ref[...], g_ref[...], lb_ref[...], w_ref[...]
        sig, f, lf, k = _hgrn_gates(f_ref[...], lb)
        bcs = _dot_mask(tril, lf)
        bl = _chunk_row(bcs, -1)
        eb, enb, eeb = jnp.exp(bcs), jnp.exp(-bcs), jnp.exp(bl - bcs)
        qd, ki, ke = q * eb, k * enb, k * eeb
        qdb, kib, keb, vb = qd.astype(BF16), ki.astype(BF16), ke.astype(BF16), v.astype(BF16)
        dec = jnp.exp(bl)
        o = o_ref[...]
        r = lax.rsqrt(jnp.mean(o * o, axis=-1, keepdims=True) + NORM_EPS)
        on = o * r
        sg = _sigmoid(g)
        silu_g = g * sg
        doa = doa_ref[...]
        dg_ref[...] = doa * on * w * (sg * (1.0 + g * (1.0 - sg)))
        dw_ref[0] += jnp.sum(doa * on * silu_g, axis=0, keepdims=True)
        don = doa * w * silu_g
        do = r * (don - on * jnp.mean(don * on, axis=-1, keepdims=True))
        dob = do.astype(BF16)
        a = jnp.where(tril, _dot(qdb, kib, _NT), 0.0).astype(BF16)
        da = jnp.where(tril, _dot(dob, vb, _NT), 0.0).astype(BF16)
        dv_ref[...] = _dot(a, dob, _TN)
        dqd_ref[...] = _dot(da, kib)
        dki = _dot(da, qdb, _TN)
        chunks = [slice(c * A_CHUNK, (c + 1) * A_CHUNK) for c in range(nch)]
        outer = [_dot(vb[rs], keb[rs], _TN) for rs in chunks]
        st = s_ref[0]
        for c in range(nch):
            sc_ref[c] = st
            st = st * dec[c * A_CHUNK:c * A_CHUNK + 1, :] + outer[c]
        outer_g = [_dot(dob[rs], qdb[rs], _TN) for rs in chunks]
        dst = dst_ref[...]
        for c in reversed(range(nch)):
            dsc_ref[c] = dst
            dst = dst * dec[c * A_CHUNK:c * A_CHUNK + 1, :] + outer_g[c]
        dst_ref[...] = dst
        for c, rs in enumerate(chunks):
            dec_c = dec[c * A_CHUNK:c * A_CHUNK + 1, :]
            dsc, stc = dsc_ref[c], sc_ref[c]
            dscb = dsc.astype(BF16)
            dv_ref[rs, :] += _dot(keb[rs], dscb, _NT)
            dke_ref[rs, :] = _dot(vb[rs], dscb)
            ddec = jnp.sum(dsc * stc, axis=0, keepdims=True)
            dqd_ref[rs, :] += _dot(dob[rs], stc.astype(BF16))
            dblx_ref[rs, :] = jnp.broadcast_to(ddec * dec_c, (A_CHUNK, A_HEAD))
        dqd, dke = dqd_ref[...], dke_ref[...]
        dq_ref[...] = dqd * eb
        keke = dke * ke
        db = dqd * qd - dki * ki - keke
        sums = _dot_mask(triu, jnp.concatenate([db, keke], axis=1))
        dk = dki * enb + dke * eeb
        dlf = sums[:, :A_HEAD] + _chunk_row(sums[:, A_HEAD:], 0) + dblx_ref[...]
        dff = dlf / f - dk
        df_ref[...] = dff * (1.0 - lb) * sig * (1.0 - sig)
        dlb_ref[0] += jnp.sum(dff * (1.0 - sig), axis=0, keepdims=True)

    def rev(n):
        return ns - 1 - n

    def zspec(off):
        return pl.BlockSpec((A_SUPER, A_HEAD), lambda b, h, n: (b * ns + rev(n), off + h))

    hvec = pl.BlockSpec((1, A_HEAD), lambda b, h, n: (0, h))
    ospec = pl.BlockSpec((A_SUPER, A_HEAD), lambda b, h, n: (b * ns + rev(n), h))
    acc = pl.BlockSpec((1, 1, A_HEAD), lambda b, h, n: (b * 4 + h, 0, 0))
    big = jax.ShapeDtypeStruct((T, A_WIDTH), F32)
    small = jax.ShapeDtypeStruct((B * 4, 1, A_HEAD), F32)
    return pl.pallas_call(
        body, name=name, grid=(B, 4, ns),
        in_specs=[zspec(0), zspec(4), zspec(8), zspec(12), hvec, hvec, ospec,
                  pl.BlockSpec((1, A_HEAD, A_HEAD), lambda b, h, n: ((b * 4 + h) * ns + rev(n), 0, 0)), ospec],
        out_specs=[ospec, ospec, ospec, ospec, acc, acc],
        out_shape=[big, big, big, big, small, small],
        scratch_shapes=[pltpu.VMEM((A_HEAD, A_HEAD), F32), pltpu.VMEM((nch, A_HEAD, A_HEAD), F32),
                        pltpu.VMEM((nch, A_HEAD, A_HEAD), F32),
                        pltpu.VMEM((A_SUPER, A_HEAD), F32), pltpu.VMEM((A_SUPER, A_HEAD), F32),
                        pltpu.VMEM((A_SUPER, A_HEAD), F32)],
        compiler_params=_cparams("parallel", "parallel", "arbitrary"),
    )(z, z, z, z, lb, onw, o_raw, s_start, doa)


def _rope_tables(L):
    half = A_HEAD // 2
    inv_freq = ROPE_THETA ** (-jnp.arange(half, dtype=F32) / half)
    ang = jnp.arange(L, dtype=F32)[:, None] * inv_freq[None, :]
    cos, sin = jnp.cos(ang), jnp.sin(ang)
    return jnp.concatenate([cos, cos], axis=-1), jnp.concatenate([-sin, sin], axis=-1)


def _rope_fwd(z, cos2, sin2, B, L, *, name):
    T = B * L
    tm = 512
    nl = L // tm

    def body(x_ref, c_ref, s_ref, q_ref, k_ref):
        c, s = c_ref[...], s_ref[...]
        for h in range(8):
            x = x_ref[:, h * A_HEAD:(h + 1) * A_HEAD]
            out = x * c + pltpu.roll(x, A_HEAD // 2, 1) * s
            o_ref = q_ref if h < 4 else k_ref
            o_ref[:, (h % 4) * A_HEAD:(h % 4 + 1) * A_HEAD] = out

    tab = pl.BlockSpec((tm, A_HEAD), lambda i: (i % nl, 0))
    out = pl.BlockSpec((tm, 512), lambda i: (i, 0))
    return pl.pallas_call(
        body, name=name, grid=(T // tm,),
        in_specs=[pl.BlockSpec((tm, 1024), lambda i: (i, 2)), tab, tab], out_specs=[out, out],
        out_shape=[jax.ShapeDtypeStruct((T, 512), F32)] * 2, compiler_params=_cparams("parallel"),
    )(z, cos2, sin2)


def _rope_bwd(dqs, dks, dvs, cos2, sin2, B, L, *, name):
    T = B * L
    tm = 256
    nl = L // tm

    def body(*refs):
        c, s = refs[9][...], refs[10][...]
        o_ref = refs[11]
        for part in range(3):
            a_ref, b_ref, c_ref = refs[3 * part:3 * part + 3]
            for h in range(4):
                cols = slice(h * A_HEAD, (h + 1) * A_HEAD)
                d = a_ref[:, cols] + b_ref[:, cols] + c_ref[:, cols]
                if part < 2:
                    d = d * c - pltpu.roll(d, A_HEAD // 2, 1) * s
                o_ref[:, part * 512 + h * A_HEAD:part * 512 + (h + 1) * A_HEAD] = d

    blk = pl.BlockSpec((tm, 512), lambda i: (i, 0))
    tab = pl.BlockSpec((tm, A_HEAD), lambda i: (i % nl, 0))
    return pl.pallas_call(
        body, name=name, grid=(T // tm,), in_specs=[blk] * 9 + [tab, tab],
        out_specs=pl.BlockSpec((tm, 1536), lambda i: (i, 0)),
        out_shape=jax.ShapeDtypeStruct((T, 1536), F32), compiler_params=_cparams("parallel"),
    )(*dqs, *dks, *dvs, cos2, sin2)


def _band_masks():
    i = lax.broadcasted_iota(jnp.int32, (B_SPAN, B_SPAN), 0)
    j = lax.broadcasted_iota(jnp.int32, (B_SPAN, B_SPAN), 1)
    return i <= j, j <= i


class _DilPlan:
    def __init__(self, dil, B, L):
        self.dil, self.B, self.L = dil, B, L
        self.rows = 4 * B_SPAN if dil == 1 else B_SPAN * dil
        self.n = L // self.rows
        self.hr = 4 if dil == 4 else 1
        self.cw = 512 if dil == 1 else A_HEAD
        self.has_other = dil != 16
        self.grid = (B, self.n, 4 if dil == 16 else 1)
        if dil == 1:
            self.items = [(j, 0, h) for j in range(4) for h in range(4)]
        elif dil == 4:
            self.items = [(0, r, h) for r in range(4) for h in range(4)]
        else:
            self.items = [(0, r, 0) for r in range(16)]
        self.groups = [self.items[i:i + 4] for i in range(0, 16, 4)]

    def operands(self, arr):
        return [arr] * self.hr

    def specs(self, col0, role="cur"):
        n, nb128 = self.n, self.L // B_SPAN
        out = []
        for h in range(self.hr):
            cb = col0 // self.cw + h
            if role == "cur":
                out.append(pl.BlockSpec((self.rows, self.cw), lambda b, i, hh, cb=cb: (b * n + i, cb + hh)))
            elif self.dil == 1:
                shift = -1 if role == "prev" else 4
                out.append(pl.BlockSpec((B_SPAN, self.cw),
                                        lambda b, i, hh, cb=cb, shift=shift: (b * nb128 + jnp.clip(4 * i + shift, 0, nb128 - 1), cb)))
            else:
                shift = -1 if role == "prev" else 1
                out.append(pl.BlockSpec((self.rows, self.cw),
                                        lambda b, i, hh, cb=cb, shift=shift: (b * n + jnp.clip(i + shift, 0, n - 1), cb)))
        return out

    def out_spec(self):
        n = self.n
        if self.dil == 4:
            return pl.BlockSpec((self.rows, 512), lambda b, i, hh: (b * n + i, 0))
        return pl.BlockSpec((self.rows, self.cw), lambda b, i, hh: (b * n + i, hh))

    def scratch(self, n_out):
        return [pltpu.VMEM((4, self.rows, A_HEAD), F32)] * n_out if self.dil == 4 else []

    def store(self, out_ref, scr, item, val):
        j, r, h = item
        if self.dil == 1:
            out_ref[pl.ds(j * B_SPAN, B_SPAN), pl.ds(h * A_HEAD, A_HEAD)] = val
        elif self.dil == 4:
            scr.at[h][pl.ds(r, B_SPAN, stride=4), :] = val
        else:
            out_ref[pl.ds(r, B_SPAN, stride=self.dil), :] = val

    def flush(self, out_ref, scr):
        if self.dil == 4:
            for h in range(4):
                out_ref[:, h * A_HEAD:(h + 1) * A_HEAD] = scr[h]

    def cur(self, refs, item):
        j, r, h = item
        if self.dil == 1:
            return refs[0], (pl.ds(j * B_SPAN, B_SPAN), pl.ds(h * A_HEAD, A_HEAD))
        return refs[h], (pl.ds(r, B_SPAN, stride=self.dil), slice(None))

    def other(self, refs, other_refs, item, role):
        j, r, h = item
        if self.dil == 1:
            cols = pl.ds(h * A_HEAD, A_HEAD)
            jj = j - 1 if role == "prev" else j + 1
            if 0 <= jj < 4:
                return refs[0], (pl.ds(jj * B_SPAN, B_SPAN), cols)
            return other_refs[0], (pl.ds(0, B_SPAN), cols)
        return other_refs[h], (pl.ds(r, B_SPAN, stride=self.dil), slice(None))

    def other_valid(self, item, role):
        j = item[0]
        i = pl.program_id(1)
        if role == "prev":
            return True if (self.dil == 1 and j > 0) else i > 0
        return True if (self.dil == 1 and j < 3) else i < self.n - 1


def _ld(pair):
    ref, idx = pair
    return ref[idx]


def _dil_fwd(qr, kr, z, dil, B, L, *, name):
    T = B * L
    plan = _DilPlan(dil, B, L)
    hr, has_prev = plan.hr, plan.has_other
    scale = A_HEAD ** -0.5

    n_t = 5 if has_prev else 3

    def body(*refs):
        lists = [refs[i * hr:(i + 1) * hr] for i in range(n_t)]
        o_ref, l_ref = refs[n_t * hr], refs[n_t * hr + 1]
        scr = refs[n_t * hr + 2:] or (None, None)
        if has_prev:
            q_r, kc_r, vc_r, kp_r, vp_r = lists
        else:
            q_r, kc_r, vc_r = lists
        mp0, mc = _band_masks()

        for group in plan.groups:
            qs = [_ld(plan.cur(q_r, i)).astype(BF16) for i in group]
            sc = [jnp.where(mc, _dot(q, _ld(plan.cur(kc_r, i)).astype(BF16), _NT) * scale, NEG_BIG) for q, i in zip(qs, group)]
            m = [jnp.max(s, axis=-1, keepdims=True) for s in sc]
            if has_prev:
                mps = [mp0 & plan.other_valid(i, "prev") for i in group]
                sp = [jnp.where(mk, _dot(q, _ld(plan.other(kc_r, kp_r, i, "prev")).astype(BF16), _NT) * scale, NEG_BIG)
                      for q, i, mk in zip(qs, group, mps)]
                m = [jnp.maximum(a, jnp.max(s, axis=-1, keepdims=True)) for a, s in zip(m, sp)]
            pc = [jnp.exp(s - a) for s, a in zip(sc, m)]
            l = [jnp.sum(p, axis=-1, keepdims=True) for p in pc]
            o = [_dot(p.astype(BF16), _ld(plan.cur(vc_r, i)).astype(BF16)) for p, i in zip(pc, group)]
            if has_prev:
                pp = [jnp.exp(s - a) for s, a in zip(sp, m)]
                l = [a + jnp.sum(p, axis=-1, keepdims=True) for a, p in zip(l, pp)]
                o = [a + _dot(p.astype(BF16), _ld(plan.other(vc_r, vp_r, i, "prev")).astype(BF16)) for a, p, i in zip(o, pp, group)]
            for i, oi, li, mi in zip(group, o, l, m):
                plan.store(o_ref, scr[0], i, oi / li)
                plan.store(l_ref, scr[1], i, jnp.broadcast_to(mi + jnp.log(li), (B_SPAN, A_HEAD)))
        plan.flush(o_ref, scr[0])
        plan.flush(l_ref, scr[1])

    tensors = [(qr, 0, "cur"), (kr, 0, "cur"), (z, 3072, "cur")] + ([(kr, 0, "prev"), (z, 3072, "prev")] if has_prev else [])
    return pl.pallas_call(
        body, name=name, grid=plan.grid,
        in_specs=[sp for _, c, role in tensors for sp in plan.specs(c, role)],
        out_specs=[plan.out_spec()] * 2, out_shape=[jax.ShapeDtypeStruct((T, 512), F32)] * 2,
        scratch_shapes=plan.scratch(2), compiler_params=_cparams("parallel", "parallel", "parallel"),
    )(*[a for arr, _, _ in tensors for a in plan.operands(arr)])


def _dil_combine(os_, ls_, *, name):
    T = os_[0].shape[0]
    tm = 256

    def body(o1, o2, o3, l1, l2, l3, ob_ref, lse_ref):
        a1, a2, a3 = l1[...], l2[...], l3[...]
        m = jnp.maximum(jnp.maximum(a1, a2), a3)
        e1, e2, e3 = jnp.exp(a1 - m), jnp.exp(a2 - m), jnp.exp(a3 - m)
        den = e1 + e2 + e3
        ob_ref[...] = (e1 * o1[...] + e2 * o2[...] + e3 * o3[...]) / den
        lse_ref[...] = m + jnp.log(den)

    blk = pl.BlockSpec((tm, 512), lambda i: (i, 0))
    return pl.pallas_call(
        body, name=name, grid=(T // tm,), in_specs=[blk] * 6, out_specs=[blk, blk],
        out_shape=[jax.ShapeDtypeStruct((T, 512), F32)] * 2, compiler_params=_cparams("parallel"),
    )(*[o.reshape(T, 512) for o in os_], *[l.reshape(T, 512) for l in ls_])


def _dil_bwd_q(qr, kr, z, dymix, out, lse, dil, B, L, *, name):
    T = B * L
    plan = _DilPlan(dil, B, L)
    hr, has_prev = plan.hr, plan.has_other
    scale = A_HEAD ** -0.5
    n_t = 8 if has_prev else 6

    def body(*refs):
        lists = [refs[i * hr:(i + 1) * hr] for i in range(n_t)]
        dq_ref = refs[n_t * hr]
        scr = refs[n_t * hr + 1:] or (None,)
        q_r, kc_r, vc_r, do_r, out_r, lse_r = lists[:6]
        mp0, mc = _band_masks()

        for group in plan.groups:
            qs = [_ld(plan.cur(q_r, i)).astype(BF16) for i in group]
            dos = [_ld(plan.cur(do_r, i)) for i in group]
            delta = [jnp.sum(d * _ld(plan.cur(out_r, i)), axis=-1, keepdims=True) for d, i in zip(dos, group)]
            dob = [d.astype(BF16) for d in dos]
            lse = [_ld(plan.cur(lse_r, i)) for i in group]
            kc = [_ld(plan.cur(kc_r, i)).astype(BF16) for i in group]
            pc = [jnp.where(mc, jnp.exp(_dot(q, k, _NT) * scale - a), 0.0) for q, k, a in zip(qs, kc, lse)]
            dsc = [p * (_dot(d, _ld(plan.cur(vc_r, i)).astype(BF16), _NT) - dl) * scale
                   for p, d, i, dl in zip(pc, dob, group, delta)]
            dq = [_dot(d.astype(BF16), k) for d, k in zip(dsc, kc)]
            if has_prev:
                kp_r, vp_r = lists[6], lists[7]
                mps = [mp0 & plan.other_valid(i, "prev") for i in group]
                kp = [_ld(plan.other(kc_r, kp_r, i, "prev")).astype(BF16) for i in group]
                pp = [jnp.where(mk, jnp.exp(_dot(q, k, _NT) * scale - a), 0.0) for q, k, a, mk in zip(qs, kp, lse, mps)]
                dsp = [p * (_dot(d, _ld(plan.other(vc_r, vp_r, i, "prev")).astype(BF16), _NT) - dl) * scale
                       for p, d, i, dl in zip(pp, dob, group, delta)]
                dq = [a + _dot(d.astype(BF16), k) for a, d, k in zip(dq, dsp, kp)]
            for i, d in zip(group, dq):
                plan.store(dq_ref, scr[0], i, d)
        plan.flush(dq_ref, scr[0])

    tensors = ([(qr, 0, "cur"), (kr, 0, "cur"), (z, 3072, "cur"), (dymix, 512, "cur"), (out, 0, "cur"), (lse, 0, "cur")]
               + ([(kr, 0, "prev"), (z, 3072, "prev")] if has_prev else []))
    return pl.pallas_call(
        body, name=name, grid=plan.grid,
        in_specs=[sp for _, c, role in tensors for sp in plan.specs(c, role)],
        out_specs=plan.out_spec(), out_shape=jax.ShapeDtypeStruct((T, 512), F32),
        scratch_shapes=plan.scratch(1), compiler_params=_cparams("parallel", "parallel", "parallel"),
    )(*[a for arr, _, _ in tensors for a in plan.operands(arr)])


def _dil_bwd_kv(qr, kr, z, dymix, out, lse, dil, B, L, *, name):
    T = B * L
    plan = _DilPlan(dil, B, L)
    hr, has_next = plan.hr, plan.has_other
    scale = A_HEAD ** -0.5
    n_t = 10 if has_next else 6

    def body(*refs):
        lists = [refs[i * hr:(i + 1) * hr] for i in range(n_t)]
        dk_ref, dv_ref = refs[n_t * hr], refs[n_t * hr + 1]
        scr = refs[n_t * hr + 2:] or (None, None)
        k_r, v_r = lists[0], lists[1]
        own = lists[2:6]
        mp0, mc = _band_masks()

        for group in plan.groups:
            kh = [_ld(plan.cur(k_r, i)).astype(BF16) for i in group]
            vh = [_ld(plan.cur(v_r, i)).astype(BF16) for i in group]
            dk, dv = [None] * len(group), [None] * len(group)
            for role in ("own", "next") if has_next else ("own",):
                if role == "own":
                    get = lambda t, i: _ld(plan.cur(own[t], i))
                    masks = [mc] * len(group)
                else:
                    get = lambda t, i: _ld(plan.other(own[t], lists[6 + t], i, "next"))
                    masks = [mp0 & plan.other_valid(i, "next") for i in group]
                qs = [get(0, i).astype(BF16) for i in group]
                dos = [get(1, i) for i in group]
                delta = [jnp.sum(d * get(2, i), axis=-1, keepdims=True) for d, i in zip(dos, group)]
                dob = [d.astype(BF16) for d in dos]
                p = [jnp.where(mk, jnp.exp(_dot(q, k, _NT) * scale - get(3, i)), 0.0) for q, k, i, mk in zip(qs, kh, group, masks)]
                dvn = [_dot(a.astype(BF16), d, _TN) for a, d in zip(p, dob)]
                ds = [a * (_dot(d, v, _NT) - dl) * scale for a, d, v, dl in zip(p, dob, vh, delta)]
                dkn = [_dot(d.astype(BF16), q, _TN) for d, q in zip(ds, qs)]
                dv = [n if o is None else o + n for o, n in zip(dv, dvn)]
                dk = [n if o is None else o + n for o, n in zip(dk, dkn)]
            for i, a, b in zip(group, dk, dv):
                plan.store(dk_ref, scr[0], i, a)
                plan.store(dv_ref, scr[1], i, b)
        plan.flush(dk_ref, scr[0])
        plan.flush(dv_ref, scr[1])

    queries = [(qr, 0), (dymix, 512), (out, 0), (lse, 0)]
    tensors = ([(kr, 0, "cur"), (z, 3072, "cur")] + [(a, c, "cur") for a, c in queries]
               + ([(a, c, "next") for a, c in queries] if has_next else []))
    return pl.pallas_call(
        body, name=name, grid=plan.grid,
        in_specs=[sp for _, c, role in tensors for sp in plan.specs(c, role)],
        out_specs=[plan.out_spec()] * 2, out_shape=[jax.ShapeDtypeStruct((T, 512), F32)] * 2,
        scratch_shapes=plan.scratch(2), compiler_params=_cparams("parallel", "parallel", "parallel"),
    )(*[a for arr, _, _ in tensors for a in plan.operands(arr)])


def _s5_build(lam_re, lam_im, log_dt, b_re, b_im, c_re, c_im):
    G, P, TC = C_GROUPS, C_STATE, C_TC
    lr = jnp.minimum(lam_re, C_MIN_NEG_RE)
    li = lam_im
    dt = jnp.exp(log_dt)[:, None]
    mag = jnp.exp(dt * lr)
    ar, ai = mag * jnp.cos(dt * li), mag * jnp.sin(dt * li)
    den = lr * lr + li * li
    zr = ((ar - 1.0) * lr + ai * li) / den
    zi = (ai * lr - (ar - 1.0) * li) / den
    bbr = zr[..., None] * b_re - zi[..., None] * b_im
    bbi = zr[..., None] * b_im + zi[..., None] * b_re
    ks = jnp.arange(TC + 1, dtype=F32)[:, None, None]
    pmag = jnp.exp(ks * (dt * lr)[None])
    pr, pi = pmag * jnp.cos(ks * (dt * li)[None]), pmag * jnp.sin(ks * (dt * li)[None])
    car = c_re[None] * pr[:, :, None, :] - c_im[None] * pi[:, :, None, :]
    cai = c_re[None] * pi[:, :, None, :] + c_im[None] * pr[:, :, None, :]
    kern = (jnp.einsum('lgop,gpc->lgco', car[:TC], bbr, precision=HI)
            - jnp.einsum('lgop,gpc->lgco', cai[:TC], bbi, precision=HI))
    pr_e, pi_e = pr[TC - 1 - jnp.arange(TC)], pi[TC - 1 - jnp.arange(TC)]
    er = pr_e[:, :, :, None] * bbr[None] - pi_e[:, :, :, None] * bbi[None]
    ei = pr_e[:, :, :, None] * bbi[None] + pi_e[:, :, :, None] * bbr[None]
    ez = jnp.stack([er, ei], axis=2).reshape(TC, C_NB, C_GB, 2, P, C_GROUP).transpose(1, 0, 2, 5, 3, 4)
    fz = jnp.stack([car[1:], -cai[1:]], axis=0).reshape(2, TC, C_NB, C_GB, C_GROUP, P).transpose(2, 0, 3, 5, 1, 4)
    return kern, ez, fz, pr[TC], pi[TC]


def _s5_lag_blocks(kern):
    eye = jnp.eye(C_GB, dtype=kern.dtype)
    return (kern.reshape(C_TC, C_NB, C_GB, C_GROUP, C_GROUP)[:, :, :, :, None, :]
            * eye[None, None, :, None, :, None]).reshape(C_TC, C_NB, LANES, LANES)


def _s5_state_maps(ez, fz):
    eye = jnp.eye(C_GB, dtype=ez.dtype)
    e8 = (ez[:, :, :, :, :, None, :] * eye[None, None, :, None, None, :, None]).reshape(C_NB, C_W8, C_S8)
    f8 = (fz[:, :, :, :, :, None, :] * eye[None, None, :, None, None, :, None]).reshape(C_NB, C_S8, C_W8)
    return e8, f8


def _s5_dense(kern, ez, fz):
    TC = C_TC
    kbd = _s5_lag_blocks(kern)
    lag = jnp.arange(TC)[None, :] - jnp.arange(TC)[:, None]
    ksel = jnp.where((lag >= 0)[:, :, None, None, None], kbd[jnp.clip(lag, 0, TC - 1)], 0)
    m8 = ksel.transpose(2, 0, 3, 1, 4).reshape(C_NB, C_W8, C_W8)
    return (m8,) + _s5_state_maps(ez, fz)


C_NB = C_GROUPS * C_GROUP // LANES
C_GB = C_GROUPS // C_NB
C_W8 = C_TC * LANES
C_S8 = 2 * C_GB * C_STATE


def _s5_scan_tables(lam_re, lam_im, log_dt, nsteps):
    lr = jnp.minimum(lam_re, C_MIN_NEG_RE)
    dt = jnp.exp(log_dt)[:, None]
    ks = (C_TC * 2.0 ** jnp.arange(8, dtype=F32))[None, :, None]
    keep = (jnp.arange(8) < nsteps)[None, :, None]
    pmag = jnp.exp(ks * (dt * lr)[:, None, :])
    ang = ks * (dt * lam_im)[:, None, :]

    def blocks(t):
        return t.reshape(C_NB, C_GB, 8, C_STATE).transpose(0, 2, 1, 3).reshape(C_NB, 8, C_GB * C_STATE)

    pr = blocks(jnp.where(keep, pmag * jnp.cos(ang), 0.0))
    pi = blocks(jnp.where(keep, pmag * jnp.sin(ang), 0.0))
    return jnp.concatenate([pr, pr], axis=-1), jnp.concatenate([-pi, pi], axis=-1)


def _s5_rows(t, R):
    return pl.ds(t, R, stride=C_TC)


def _s5_fwd(u, dsk, m8, e8, f8, tab_r, tab_i, B, L, *, name):
    T = B * L
    R = L // C_TC
    nsteps = int(math.log2(R))

    def body(u_ref, d_ref, m_ref, e_ref, f_ref, tr_ref, ti_ref, gl_ref, y_ref, x8_ref, xs_ref):
        for t in range(C_TC):
            x8_ref[0, :, t * LANES:(t + 1) * LANES] = u_ref[_s5_rows(t, R), :].astype(BF16)
        x8 = x8_ref[0]
        x = _dot(x8, e_ref[0])
        row = lax.broadcasted_iota(jnp.int32, (R, C_S8), 0)
        for k in range(nsteps):
            s = 1 << k
            sh = pltpu.roll(x, s, 0)
            upd = tr_ref[0, k:k + 1, :] * sh + ti_ref[0, k:k + 1, :] * pltpu.roll(sh, C_S8 // 2, 1)
            x = x + jnp.where(row >= s, upd, 0.0)
        xs = jnp.where(row >= 1, pltpu.roll(x, 1, 0), 0.0)
        xs_ref[0] = xs
        y8 = _dot(x8, m_ref[0]) + _dot(xs.astype(BF16), f_ref[0])
        d = d_ref[...]
        for t in range(C_TC):
            rows = _s5_rows(t, R)
            y = y8[:, t * LANES:(t + 1) * LANES] + d * u_ref[rows, :]
            y_ref[rows, :] = y
            gl_ref[rows, :] = 0.5 * y * (1.0 + _erf(y * (2.0 ** -0.5)))

    tok = pl.BlockSpec((L, LANES), lambda c, b: (b, c))
    per_block = lambda shape: pl.BlockSpec((1,) + shape, lambda c, b: (c, 0, 0))
    per_step = lambda shape: pl.BlockSpec((1,) + shape, lambda c, b: (c * B + b, 0, 0))
    return pl.pallas_call(
        body, name=name, grid=(C_NB, B),
        in_specs=[tok, pl.BlockSpec((1, LANES), lambda c, b: (0, c)), per_block((C_W8, C_W8)), per_block((C_W8, C_S8)),
                  per_block((C_S8, C_W8)), per_block((8, C_S8)), per_block((8, C_S8))],
        out_specs=[tok, tok, per_step((R, C_W8)), per_step((R, C_S8))],
        out_shape=[jax.ShapeDtypeStruct((T, D_MODEL), F32), jax.ShapeDtypeStruct((T, D_MODEL), F32),
                   jax.ShapeDtypeStruct((C_NB * B, R, C_W8), BF16), jax.ShapeDtypeStruct((C_NB * B, R, C_S8), F32)],
        compiler_params=_cparams("parallel", "parallel"),
    )(u, dsk, m8, e8, f8, tab_r, tab_i)


def _s5_bwd(dgl, y, u, dsk, xs, m8, e8, f8, tab_r, tab_i, B, L, *, name):
    T = B * L
    R = L // C_TC
    nsteps = int(math.log2(R))

    def body(dgl_ref, y_ref, u_ref, d_ref, xs_ref, m_ref, e_ref, f_ref, tr_ref, ti_ref,
             du_ref, dy8_ref, de_ref, da_ref, dd_ref, dyf_ref):
        @pl.when(pl.program_id(1) == 0)
        def _():
            da_ref[...] = jnp.zeros_like(da_ref)
            dd_ref[...] = jnp.zeros_like(dd_ref)

        dd = jnp.zeros((1, LANES), F32)
        for t in range(C_TC):
            rows = _s5_rows(t, R)
            yv = y_ref[rows, :]
            cdf = 0.5 * (1.0 + _erf(yv * (2.0 ** -0.5)))
            pdf = jnp.exp(-0.5 * yv * yv) * (1.0 / math.sqrt(2.0 * math.pi))
            dy = dgl_ref[rows, :] * (cdf + yv * pdf)
            dd = dd + jnp.sum(dy * u_ref[rows, :], axis=0, keepdims=True)
            dyf_ref[:, t * LANES:(t + 1) * LANES] = dy
        dd_ref[...] += dd
        dy8 = dyf_ref[...].astype(BF16)
        dy8_ref[0] = dy8
        xs = xs_ref[0]
        gx = _dot(dy8, f_ref[0], _NT)
        row = lax.broadcasted_iota(jnp.int32, (R, C_S8), 0)
        for k in range(nsteps):
            s = 1 << k
            sh = pltpu.roll(gx, R - s, 0)
            upd = tr_ref[0, k:k + 1, :] * sh - ti_ref[0, k:k + 1, :] * pltpu.roll(sh, C_S8 // 2, 1)
            gx = gx + jnp.where(row + s < R, upd, 0.0)
        de_in = jnp.where(row + 1 < R, pltpu.roll(gx, R - 1, 0), 0.0)
        deb = de_in.astype(BF16)
        de_ref[0] = deb
        da_ref[0, 0:1, :] += jnp.sum(de_in * xs, axis=0, keepdims=True)
        da_ref[0, 1:2, :] += jnp.sum(de_in * pltpu.roll(xs, C_S8 // 2, 1), axis=0, keepdims=True)
        dx8 = _dot(dy8, m_ref[0], _NT) + _dot(deb, e_ref[0], _NT)
        d = d_ref[...]
        for t in range(C_TC):
            cols = slice(t * LANES, (t + 1) * LANES)
            du_ref[_s5_rows(t, R), :] = dx8[:, cols] + d * dyf_ref[:, cols]

    tok = pl.BlockSpec((L, LANES), lambda c, b: (b, c))
    vec = pl.BlockSpec((1, LANES), lambda c, b: (0, c))
    per_block = lambda shape: pl.BlockSpec((1,) + shape, lambda c, b: (c, 0, 0))
    per_step = lambda shape: pl.BlockSpec((1,) + shape, lambda c, b: (c * B + b, 0, 0))
    return pl.pallas_call(
        body, name=name, grid=(C_NB, B),
        in_specs=[tok, tok, tok, vec, per_step((R, C_S8)), per_block((C_W8, C_W8)),
                  per_block((C_W8, C_S8)), per_block((C_S8, C_W8)), per_block((8, C_S8)), per_block((8, C_S8))],
        out_specs=[tok, per_step((R, C_W8)), per_step((R, C_S8)), per_block((8, C_S8)), vec],
        out_shape=[jax.ShapeDtypeStruct((T, D_MODEL), F32), jax.ShapeDtypeStruct((C_NB * B, R, C_W8), BF16),
                   jax.ShapeDtypeStruct((C_NB * B, R, C_S8), BF16), jax.ShapeDtypeStruct((C_NB, 8, C_S8), F32),
                   jax.ShapeDtypeStruct((1, D_MODEL), F32)],
        scratch_shapes=[pltpu.VMEM((R, C_W8), F32)],
        compiler_params=_cparams("parallel", "arbitrary"),
    )(dgl, y, u, dsk, xs, m8, e8, f8, tab_r, tab_i)


def _bmm_tn(a, b, nb, *, name, out_dtype=F32, fold_lags=False):
    a = a.reshape(nb, -1, a.shape[-1])
    b = b.reshape(nb, -1, b.shape[-1])
    K, M, N = a.shape[1], a.shape[2], b.shape[2]

    def body(a_ref, b_ref, o_ref):
        prod = _dot(a_ref[0].astype(BF16), b_ref[0].astype(BF16), _TN)
        if not fold_lags:
            o_ref[0] = prod.astype(o_ref.dtype)
            return
        for lag in range(C_TC):
            blocks = [prod[s * LANES:(s + 1) * LANES, (s + lag) * LANES:(s + lag + 1) * LANES] for s in range(C_TC - lag)]
            o_ref[0, lag] = functools.reduce(lambda u, v: u + v, blocks)

    if fold_lags:
        out_spec = pl.BlockSpec((1, C_TC, LANES, LANES), lambda c: (c, 0, 0, 0))
        out_shape = jax.ShapeDtypeStruct((nb, C_TC, LANES, LANES), F32)
    else:
        out_spec = pl.BlockSpec((1, M, N), lambda c: (c, 0, 0))
        out_shape = jax.ShapeDtypeStruct((nb, M, N), out_dtype)
    return pl.pallas_call(
        body, name=name, grid=(nb,),
        in_specs=[pl.BlockSpec((1, K, M), lambda c: (c, 0, 0)), pl.BlockSpec((1, K, N), lambda c: (c, 0, 0))],
        out_specs=out_spec, out_shape=out_shape, compiler_params=_cparams("parallel"),
    )(a, b)


def _loss_head(y, target, *, name):
    T, C = y.shape
    tm = 256

    def body(y_ref, t_ref, l_ref, d_ref):
        err = y_ref[...] - t_ref[...]
        d_ref[...] = err * (1.0 / C)
        sq = err * err
        part = jnp.zeros((8, LANES), F32)
        for r in range(0, tm, 8):
            for c in range(0, C, LANES):
                part = part + sq[r:r + 8, c:c + LANES]

        @pl.when(pl.program_id(0) == 0)
        def _():
            l_ref[...] = part

        @pl.when(pl.program_id(0) > 0)
        def _():
            l_ref[...] += part

    row = pl.BlockSpec((tm, C), lambda i: (i, 0))
    acc = pl.BlockSpec((8, LANES), lambda i: (0, 0))
    return pl.pallas_call(
        body, name=name, grid=(T // tm,), in_specs=[row, row], out_specs=[acc, row],
        out_shape=[jax.ShapeDtypeStruct((8, LANES), F32), jax.ShapeDtypeStruct((T, C), F32)],
        compiler_params=_cparams("arbitrary"),
    )(y, target)


def _adamw(w, g, m, v, *, name):
    shape = w.shape
    size = int(np.prod(shape))
    cols = LANES if (shape[-1] < LANES and size % LANES == 0) else shape[-1]
    rows = size // cols
    tm = _tile(rows, 256) if rows % 8 == 0 else rows
    w2, g2, m2, v2 = (t.reshape(rows, cols) for t in (w, g, m, v))

    def body(w_ref, g_ref, m_ref, v_ref, d_ref, nm_ref, nv_ref):
        gg = g_ref[...]
        nm = ADAM_B1 * m_ref[...] + (1.0 - ADAM_B1) * gg
        nv = ADAM_B2 * v_ref[...] + (1.0 - ADAM_B2) * (gg * gg)
        m_hat = nm / (1.0 - ADAM_B1 ** ADAM_STEP)
        v_hat = nv / (1.0 - ADAM_B2 ** ADAM_STEP)
        d_ref[...] = -ADAM_LR * (m_hat / (jnp.sqrt(v_hat) + ADAM_EPS) + ADAM_WD * w_ref[...])
        nm_ref[...] = nm
        nv_ref[...] = nv

    blk = pl.BlockSpec((tm, cols), lambda i: (i, 0))
    outs = pl.pallas_call(
        body, name=name, grid=(rows // tm,), in_specs=[blk] * 4, out_specs=[blk] * 3,
        out_shape=[jax.ShapeDtypeStruct((rows, cols), F32)] * 3, compiler_params=_cparams("parallel"),
    )(w2, g2, m2, v2)
    return tuple(o.reshape(shape) for o in outs)


def _all_gather(shard, *, name):
    R, C = shard.shape

    def body(x_ref, out_ref, send_sems, recv_sems, local_sem):
        x, y, c = lax.axis_index("x"), lax.axis_index("y"), lax.axis_index("c")
        me, sibling = (x, y, c), (x, y, 1 - c)
        chips = [(1 - x, y), (x, 1 - y), (1 - x, 1 - y)]

        def rows(px, py, pc):
            return out_ref.at[_logical(px, py, pc)]

        def copy(k, block, to, src=None):
            return pltpu.make_async_remote_copy(
                src_ref=rows(*block) if src is None else src, dst_ref=rows(*block),
                send_sem=send_sems.at[k], recv_sem=recv_sems.at[k], device_id=to, device_id_type=_MESH)

        mine = pltpu.make_async_copy(x_ref, rows(*me), local_sem)
        mine.start()
        first = [copy(0, me, sibling, src=x_ref)]
        first += [copy(1 + j, me, (*chip, c), src=x_ref) for j, chip in enumerate(chips)]
        for cp in first:
            cp.start()
        passed = [copy(4 + j, (*chip, c), sibling) for j, chip in enumerate(chips)]
        for j, chip in enumerate(chips):
            copy(1 + j, (*chip, c), me).wait_recv()
            passed[j].start()
        copy(0, sibling, me).wait_recv()
        for j, chip in enumerate(chips):
            copy(4 + j, (*chip, 1 - c), me).wait_recv()
        for cp in first + passed:
            cp.wait_send()
        mine.wait()

    return pl.pallas_call(
        body, name=name, out_shape=jax.ShapeDtypeStruct((N_DEV, R, C), shard.dtype),
        in_specs=[_HBM], out_specs=_HBM,
        scratch_shapes=[pltpu.SemaphoreType.DMA((7,)), pltpu.SemaphoreType.DMA((7,)), pltpu.SemaphoreType.DMA],
    )(shard)


def _gather_weights(shards, *, name):
    nt = len(shards)

    def body(*refs):
        ins, outs = refs[:nt], refs[nt:2 * nt]
        send_sems, recv_sems, local_sems = refs[2 * nt:]
        x, y, c = lax.axis_index("x"), lax.axis_index("y"), lax.axis_index("c")
        me, sibling = (x, y, c), (x, y, 1 - c)
        chips = [(1 - x, y), (x, 1 - y), (1 - x, 1 - y)]

        def copy(t, k, block, to, src=None):
            rows = outs[t].at[_logical(*block)]
            return pltpu.make_async_remote_copy(
                src_ref=rows if src is None else src, dst_ref=rows,
                send_sem=send_sems.at[t, k], recv_sem=recv_sems.at[t, k], device_id=to, device_id_type=_MESH)

        mine = [pltpu.make_async_copy(ins[t], outs[t].at[_logical(*me)], local_sems.at[t]) for t in range(nt)]
        for cp in mine:
            cp.start()
        started = []
        for t in range(nt):
            started.append(copy(t, 0, me, sibling, src=ins[t]))
            started += [copy(t, 1 + j, me, (*chip, c), src=ins[t]) for j, chip in enumerate(chips)]
        for cp in started:
            cp.start()
        for j, chip in enumerate(chips):
            for t in range(nt):
                copy(t, 1 + j, (*chip, c), me).wait_recv()
                fwd = copy(t, 4 + j, (*chip, c), sibling)
                fwd.start()
                started.append(fwd)
        for t in range(nt):
            copy(t, 0, sibling, me).wait_recv()
            for j, chip in enumerate(chips):
                copy(t, 4 + j, (*chip, 1 - c), me).wait_recv()
        for cp in started:
            cp.wait_send()
        for cp in mine:
            cp.wait()

    return pl.pallas_call(
        body, name=name, out_shape=[jax.ShapeDtypeStruct((N_DEV,) + s.shape, s.dtype) for s in shards],
        in_specs=[_HBM] * nt, out_specs=[_HBM] * nt,
        scratch_shapes=[pltpu.SemaphoreType.DMA((nt, 7)), pltpu.SemaphoreType.DMA((nt, 7)), pltpu.SemaphoreType.DMA((nt,))],
    )(*shards)


def _sum_rows(stacked, *, name):
    _, R, C = stacked.shape
    tr = R
    if N_DEV * R * C * stacked.dtype.itemsize > 12 * 1024 * 1024:
        for cand in range(512, 15, -16):
            if R % cand == 0:
                tr = cand
                break

    def body(s_ref, o_ref):
        acc = s_ref[0].astype(F32)
        for k in range(1, N_DEV):
            acc = acc + s_ref[k].astype(F32)
        o_ref[...] = acc

    return pl.pallas_call(
        body, name=name, grid=(R // tr,),
        in_specs=[pl.BlockSpec((N_DEV, tr, C), lambda i: (0, i, 0))], out_specs=pl.BlockSpec((tr, C), lambda i: (i, 0)),
        out_shape=jax.ShapeDtypeStruct((R, C), F32), compiler_params=_cparams("parallel"),
    )(stacked)


_LARGE = (("ab_w_in", 1, True), ("ab_w_out", 1, False), ("s5_w_glu", 1, True), ("xattn_wq", 2, False),
          ("xattn_wkv", 2, True), ("xattn_wo", 2, False), ("ffn_w_in", 2, True), ("ffn_w_out", 2, False))
_LARGE_KEYS = tuple((n, l) for n, layers, _ in _LARGE for l in range(layers))
_TRANSPOSED = {n: t for n, _, t in _LARGE}


def _owner_major(name, w):
    return w.T if _TRANSPOSED[name] else w


def _lb_from_logits(logits):
    return jnp.cumsum(jax.nn.softmax(logits, axis=0), axis=0)[0:1]


def _local_step(x, mem, target, W, shards=None):
    B, L, _ = x.shape
    T = B * L
    x0 = x.reshape(T, D_MODEL)
    memf = mem.reshape(B * MEM_LEN, D_MODEL)
    nw = W["norm_w"]
    cos2, sin2 = _rope_tables(L)
    W = dict(W)
    G, received = {}, {}

    def mmx(a, b, gather=(), scatter=(), **kw):
        if shards is None or not (gather or scatter):
            return _mm(a, b, **kw)
        out, gathered, got = _mm(a, b, gather=[shards[k] for k in gather],
                                 scatter=[G[k].reshape(N_DEV, -1, D_MODEL) for k in scatter], **kw)
        for k, g in zip(gather, gathered):
            W[k] = g.reshape(-1, D_MODEL)
        for k, r in zip(scatter, got):
            received[k] = r
        return out

    def vec(v):
        return v.reshape(1, -1)

    saved = []
    xin = x0
    for layer in range(2):
        s = {"x0": xin}
        tag = f"l{layer}"
        if layer == 0:
            h1 = _rms_fwd(xin, vec(nw[0, 0]), name="norm_pre_mix_l0", out_dtype=BF16)
        s["h1"] = h1
        if layer == 0:
            lb, lb_vjp = jax.vjp(_lb_from_logits, W["hgrn_lb_logits"])
            onw = W["hgrn_out_norm_w"].reshape(1, A_WIDTH)
            z = mmx(h1, W["ab_w_in", 0], tb=True, name="ab_in",
                    gather=[("ab_w_out", 0), ("xattn_wq", 0), ("xattn_wkv", 0), ("xattn_wo", 0), ("ffn_w_in", 0)])
            oa, o_raw, s_start = _hgrn_fwd(z, lb, onw, B, L, name="hgrn_fwd")
            qr, kr = _rope_fwd(z, cos2, sin2, B, L, name="rope_qk")
            os_, ls_ = [], []
            for dil in B_DILS:
                o_g, l_g = _dil_fwd(qr, kr, z, dil, B, L, name=f"dil_fwd_{dil}")
                os_.append(o_g)
                ls_.append(l_g)
            ob, lse = _dil_combine(os_, ls_, name="dil_combine")
            ymix = jnp.concatenate([oa, ob], axis=-1).astype(BF16)
            y1 = mmx(ymix, W["ab_w_out", 0], out_dtype=BF16, name="ab_out", gather=[("ffn_w_out", 0)])
            s.update(z=z, lb=lb, lb_vjp=lb_vjp, onw=onw, o_raw=o_raw, s_start=s_start, qr=qr, kr=kr, ob=ob, lse=lse, ymix=ymix)
        else:
            p5 = tuple(W[n][0] for n in ("s5_lambda_re", "s5_lambda_im", "s5_log_dt", "s5_b_re", "s5_b_im", "s5_c_re", "s5_c_im"))
            (kern, ez, fz, _, _), s5_vjp = jax.vjp(_s5_build, *p5)
            tab_r, tab_i = _s5_scan_tables(p5[0], p5[1], p5[2], int(math.log2(L // C_TC)))
            mats = _s5_dense(kern.astype(BF16), ez.astype(BF16), fz.astype(BF16)) + (tab_r, tab_i)
            _, lag_vjp = jax.vjp(_s5_lag_blocks, kern)
            _, state_vjp = jax.vjp(_s5_state_maps, ez, fz)
            dense_vjp = lambda dkbd, de8, df8: lag_vjp(dkbd) + state_vjp((de8, df8))
            dsk = W["s5_d"].reshape(1, D_MODEL)
            gl, ypre, x8, xs = _s5_fwd(h1, dsk, *mats, B, L, name="s5_fwd")
            w_glu = _interleave_rows(W["s5_w_glu", 0])
            zg, y1 = _mm(gl, w_glu, tb=True, name="s5_glu_in", gate=("fwd", "glu"))
            s.update(s5_vjp=s5_vjp, dense_vjp=dense_vjp, mats=mats, x8=x8, xs=xs, dsk=dsk, gl=gl, ypre=ypre, zg=zg, w_glu=w_glu)
        x1, h2 = _rms_fwd(y1, vec(nw[layer, 1]), xin, name=f"norm_post_mix_{tag}", then=(vec(nw[layer, 2]), BF16))
        memn = _rms_fwd(memf, vec(W["mem_norm_w"][layer]), name=f"norm_mem_{tag}", out_dtype=BF16)
        q = mmx(h2, W["xattn_wq", layer], out_dtype=BF16, name=f"x_q_{tag}", gather=[("s5_w_glu", 0)] if layer == 0 else [])
        kv = _mm(memn, W["xattn_wkv", layer], tb=True, out_dtype=BF16, name=f"x_kv_{tag}")
        o = _xattn_fwd(q, kv, B, L, name=f"x_attn_{tag}")
        y2 = mmx(o, W["xattn_wo", layer], out_dtype=BF16, name=f"x_o_{tag}", gather=[("xattn_wq", 1), ("xattn_wo", 1)] if layer == 0 else [])
        x2, h3 = _rms_fwd(y2, vec(nw[layer, 3]), x1, name=f"norm_post_x_{tag}", then=(vec(nw[layer, 4]), BF16))
        w_ffn_in = _interleave_rows(W["ffn_w_in", layer])
        zf, u = mmx(h3, w_ffn_in, tb=True, name=f"ffn_in_{tag}", gate=("fwd", "swiglu"),
                    gather=[("xattn_wkv", 1), ("ffn_w_in", 1), ("ffn_w_out", 1)] if layer == 0 else [])
        y3 = _mm(u, W["ffn_w_out", layer], out_dtype=BF16, name=f"ffn_out_{tag}")
        if layer == 0:
            x3, h1 = _rms_fwd(y3, vec(nw[0, 5]), x2, name="norm_post_ffn_l0", then=(vec(nw[1, 0]), F32))
        else:
            x3 = _rms_fwd(y3, vec(nw[layer, 5]), x2, name=f"norm_post_ffn_{tag}")
        s.update(y1=y1, x1=x1, h2=h2, memn=memn, q=q, kv=kv, o=o, y2=y2, x2=x2, h3=h3, zf=zf, u=u, y3=y3, w_ffn_in=w_ffn_in)
        saved.append(s)
        xin = x3

    loss_parts, dx = _loss_head(xin, target.reshape(T, D_MODEL), name="loss_head")

    d_norm = [[None] * 6 for _ in range(2)]
    d_memn = [None, None]
    for layer in (1, 0):
        s = saved[layer]
        tag = f"l{layer}"
        if layer == 1:
            dy3, d_norm[1][5] = _rms_bwd(s["y3"], vec(nw[1, 5]), dx, name="bnorm_post_ffn_l1", out_dtype=BF16)
        dzf = mmx(dy3, W["ffn_w_out", layer], tb=True, name=f"b_ffn_out_dx_{tag}", gate=("bwd", "swiglu", s["zf"]),
                  scatter=[("xattn_wkv", 1), ("s5_w_glu", 0)] if layer == 0 else [])
        G["ffn_w_out", layer] = _mm(s["u"], dy3, ta=True, out_dtype=BF16, name=f"b_ffn_out_dw_{tag}")
        G["ffn_w_in", layer] = _interleave_rows(
            mmx(dzf, s["h3"], ta=True, out_dtype=BF16, name=f"b_ffn_in_dw_{tag}", scatter=[("ffn_w_out", layer)]), inverse=True)
        dh3 = mmx(dzf, s["w_ffn_in"], out_dtype=BF16, name=f"b_ffn_in_dx_{tag}", scatter=[("ffn_w_in", layer)])
        dx, d_norm[layer][4], dy2, d_norm[layer][3] = _rms_bwd(
            s["x2"], vec(nw[layer, 4]), dh3, dx, name=f"bnorm_pre_ffn_{tag}", then=(s["y2"], vec(nw[layer, 3])))
        do = _mm(dy2, W["xattn_wo", layer], tb=True, out_dtype=BF16, name=f"b_x_o_dx_{tag}")
        G["xattn_wo", layer] = _mm(s["o"], dy2, ta=True, out_dtype=BF16, name=f"b_x_o_dw_{tag}")
        dq, dkv = _xattn_bwd(s["q"], s["kv"], do, B, L, name=f"b_x_attn_{tag}")
        G["xattn_wq", layer] = _mm(s["h2"], dq, ta=True, out_dtype=BF16, name=f"b_x_q_dw_{tag}")
        dh2 = _mm(dq, W["xattn_wq", layer], tb=True, out_dtype=BF16, name=f"b_x_q_dx_{tag}")
        G["xattn_wkv", layer] = _mm(dkv, s["memn"], ta=True, out_dtype=BF16, name=f"b_x_kv_dw_{tag}")
        dmemn = _mm(dkv, W["xattn_wkv", layer], out_dtype=BF16, name=f"b_x_kv_dx_{tag}")
        _, d_memn[layer] = _rms_bwd(memf, vec(W["mem_norm_w"][layer]), dmemn, name=f"bnorm_mem_{tag}", out_dtype=BF16)
        dx, d_norm[layer][2], dy1, d_norm[layer][1] = _rms_bwd(
            s["x1"], vec(nw[layer, 2]), dh2, dx, name=f"bnorm_pre_x_{tag}", then=(s["y1"], vec(nw[layer, 1])))
        if layer == 0:
            z = s["z"]
            dymix = _mm(dy1, W["ab_w_out", 0], tb=True, name="b_ab_out_dx")
            G["ab_w_out", 0] = _mm(s["ymix"], dy1, ta=True, out_dtype=BF16, name="b_ab_out_dw")
            dqa, dfa, dia, dga, d_onw, d_lb = _hgrn_bwd(z, s["lb"], s["onw"], s["o_raw"], s["s_start"], dymix, B, L, name="hgrn_bwd")
            dqs, dks, dvs = [], [], []
            for dil in B_DILS:
                dqs.append(_dil_bwd_q(s["qr"], s["kr"], z, dymix, s["ob"], s["lse"], dil, B, L, name=f"dil_bwd_q_{dil}"))
                dk_g, dv_g = _dil_bwd_kv(s["qr"], s["kr"], z, dymix, s["ob"], s["lse"], dil, B, L, name=f"dil_bwd_kv_{dil}")
                dks.append(dk_g)
                dvs.append(dv_g)
            dqkv = _rope_bwd(dqs, dks, dvs, cos2, sin2, B, L, name="b_rope")
            dz = jnp.concatenate([dqa, dfa, dia, dga, dqkv], axis=-1).astype(BF16)
            G["ab_w_in", 0] = mmx(dz, s["h1"], ta=True, out_dtype=BF16, name="b_ab_in_dw",
                                  scatter=[("xattn_wo", 0), ("xattn_wq", 0), ("xattn_wkv", 0), ("ab_w_out", 0)])
            dh1 = mmx(dz, W["ab_w_in", 0], out_dtype=BF16, name="b_ab_in_dx", scatter=[("ab_w_in", 0)])
            G["hgrn_out_norm_w"] = jnp.sum(d_onw.reshape(B, A_WIDTH), axis=0, keepdims=True)
            d_lb_row = jnp.sum(d_lb.reshape(B, A_WIDTH), axis=0, keepdims=True)
            G["hgrn_lb_logits"] = s["lb_vjp"](d_lb_row)[0]
        else:
            dzg = _gated_bwd(s["zg"], dy1, "glu", name="b_s5_glu")
            G["s5_w_glu", 0] = _interleave_rows(
                _mm(dzg, s["gl"], ta=True, out_dtype=BF16, name="b_s5_glu_dw"), inverse=True)
            dgl = mmx(dzg, s["w_glu"], name="b_s5_glu_dx", scatter=[("xattn_wo", 1), ("xattn_wq", 1)])
            dh1, dy8, de_in, da, d_dsk = _s5_bwd(dgl, s["ypre"], s["h1"], s["dsk"], s["xs"], *s["mats"], B, L, name="s5_bwd")
            dkbd = _bmm_tn(s["x8"], dy8, C_NB, name="s5_bwd_dm", fold_lags=True).transpose(1, 0, 2, 3)
            df8 = _bmm_tn(s["xs"], dy8, C_NB, name="s5_bwd_df", out_dtype=BF16).astype(F32)
            de8 = _bmm_tn(s["x8"], de_in, C_NB, name="s5_bwd_de", out_dtype=BF16).astype(F32)
            half = C_S8 // 2
            da_r = (da[:, 0, :half] + da[:, 0, half:]).reshape(C_GROUPS, C_STATE)
            da_i = (da[:, 1, half:] - da[:, 1, :half]).reshape(C_GROUPS, C_STATE)
            gp = s["s5_vjp"](s["dense_vjp"](dkbd, de8, df8) + (da_r, da_i))
            for n, gv in zip(("s5_lambda_re", "s5_lambda_im", "s5_log_dt", "s5_b_re", "s5_b_im", "s5_c_re", "s5_c_im"), gp):
                G[n] = gv[None]
            G["s5_d"] = d_dsk
        if layer == 1:
            dx, d_norm[1][0], dy3, d_norm[0][5] = _rms_bwd(
                s["x0"], vec(nw[1, 0]), dh1, dx, name="bnorm_pre_mix_l1", then=(saved[0]["y3"], vec(nw[0, 5])))
        else:
            dx, d_norm[0][0] = _rms_bwd(s["x0"], vec(nw[0, 0]), dh1, dx, name="bnorm_pre_mix_l0")

    G["norm_w"] = jnp.stack([jnp.concatenate(d_norm[l], axis=0) for l in range(2)])
    G["mem_norm_w"] = jnp.concatenate(d_memn, axis=0)
    if shards is not None:
        G.update(received)
    return loss_parts, dx.reshape(B, L, D_MODEL), G


_SMALL = (("norm_w", (2, 6, 1024)), ("mem_norm_w", (2, 1024)), ("hgrn_lb_logits", (3, 512)), ("hgrn_out_norm_w", (1, 512)),
          ("s5_lambda_re", (1, 64, 64)), ("s5_lambda_im", (1, 64, 64)), ("s5_log_dt", (1, 64)),
          ("s5_b_re", (1, 64, 64, 16)), ("s5_b_im", (1, 64, 64, 16)), ("s5_c_re", (1, 64, 16, 64)),
          ("s5_c_im", (1, 64, 16, 64)), ("s5_d", (1, 1024)))

_WEIGHT_ORDER = ('norm_w', 'mem_norm_w', 'ab_w_in', 'ab_w_out', 'hgrn_lb_logits', 'hgrn_out_norm_w', 's5_lambda_re',
                 's5_lambda_im', 's5_log_dt', 's5_b_re', 's5_b_im', 's5_c_re', 's5_c_im', 's5_d', 's5_w_glu', 'xattn_wq',
                 'xattn_wkv', 'xattn_wo', 'ffn_w_in', 'ffn_w_out')


def kernel(x, mem, norm_w, mem_norm_w, ab_w_in, ab_w_out, hgrn_lb_logits, hgrn_out_norm_w, s5_lambda_re, s5_lambda_im, s5_log_dt, s5_b_re, s5_b_im, s5_c_re, s5_c_im, s5_d, s5_w_glu, xattn_wq, xattn_wkv, xattn_wo, ffn_w_in, ffn_w_out, loss_target, m_norm_w, m_mem_norm_w, m_ab_w_in, m_ab_w_out, m_hgrn_lb_logits, m_hgrn_out_norm_w, m_s5_lambda_re, m_s5_lambda_im, m_s5_log_dt, m_s5_b_re, m_s5_b_im, m_s5_c_re, m_s5_c_im, m_s5_d, m_s5_w_glu, m_xattn_wq, m_xattn_wkv, m_xattn_wo, m_ffn_w_in, m_ffn_w_out, v_norm_w, v_mem_norm_w, v_ab_w_in, v_ab_w_out, v_hgrn_lb_logits, v_hgrn_out_norm_w, v_s5_lambda_re, v_s5_lambda_im, v_s5_log_dt, v_s5_b_re, v_s5_b_im, v_s5_c_re, v_s5_c_im, v_s5_d, v_s5_w_glu, v_xattn_wq, v_xattn_wkv, v_xattn_wo, v_ffn_w_in, v_ffn_w_out):
    local = dict(norm_w=norm_w, mem_norm_w=mem_norm_w, ab_w_in=ab_w_in, ab_w_out=ab_w_out, hgrn_lb_logits=hgrn_lb_logits,
                 hgrn_out_norm_w=hgrn_out_norm_w, s5_lambda_re=s5_lambda_re, s5_lambda_im=s5_lambda_im, s5_log_dt=s5_log_dt,
                 s5_b_re=s5_b_re, s5_b_im=s5_b_im, s5_c_re=s5_c_re, s5_c_im=s5_c_im, s5_d=s5_d, s5_w_glu=s5_w_glu,
                 xattn_wq=xattn_wq, xattn_wkv=xattn_wkv, xattn_wo=xattn_wo, ffn_w_in=ffn_w_in, ffn_w_out=ffn_w_out)
    mom_m = dict(zip(_WEIGHT_ORDER, (m_norm_w, m_mem_norm_w, m_ab_w_in, m_ab_w_out, m_hgrn_lb_logits, m_hgrn_out_norm_w, m_s5_lambda_re, m_s5_lambda_im, m_s5_log_dt, m_s5_b_re, m_s5_b_im, m_s5_c_re, m_s5_c_im, m_s5_d, m_s5_w_glu, m_xattn_wq, m_xattn_wkv, m_xattn_wo, m_ffn_w_in, m_ffn_w_out)))
    mom_v = dict(zip(_WEIGHT_ORDER, (v_norm_w, v_mem_norm_w, v_ab_w_in, v_ab_w_out, v_hgrn_lb_logits, v_hgrn_out_norm_w, v_s5_lambda_re, v_s5_lambda_im, v_s5_log_dt, v_s5_b_re, v_s5_b_im, v_s5_c_re, v_s5_c_im, v_s5_d, v_s5_w_glu, v_xattn_wq, v_xattn_wkv, v_xattn_wo, v_ffn_w_in, v_ffn_w_out)))
    dev = 4 * lax.axis_index("x") + 2 * lax.axis_index("y") + lax.axis_index("c")

    shards = {(n, l): _owner_major(n, local[n][l]).astype(BF16) for n, l in _LARGE_KEYS}
    first = ("ab_w_in", 0)
    W = {first: _gather_weights([shards[first]], name="gather_first")[0].reshape(-1, D_MODEL)}
    tiny =jnp.concatenate([norm_w.reshape(-1), s5_d.reshape(-1)])
    tiny = jnp.pad(tiny, (0, 16 * LANES - tiny.shape[0])).reshape(16, LANES)
    tiny_all = _all_gather(tiny, name="gather_tiny").reshape(N_DEV, 16 * LANES)
    W["norm_w"] = tiny_all[:, :12 * LANES].reshape(N_DEV, 2, 6, LANES).transpose(1, 2, 0, 3).reshape(2, 6, D_MODEL)
    W["s5_d"] = tiny_all[:, 12 * LANES:13 * LANES].reshape(1, D_MODEL)
    for n in ("mem_norm_w", "hgrn_lb_logits", "hgrn_out_norm_w", "s5_lambda_re", "s5_lambda_im", "s5_log_dt",
              "s5_b_re", "s5_b_im", "s5_c_re", "s5_c_im"):
        W[n] = local[n]

    loss_parts, grad_x, G = _local_step(x, mem, loss_target, W, shards)

    g_layers = {}
    for n, l in _LARGE_KEYS:
        g = _sum_rows(G[n, l], name=f"sum_grads_{n}_{l}")
        g_layers.setdefault(n, []).append(g.T if _TRANSPOSED[n] else g)
    g_local = {n: jnp.stack(gl) for n, gl in g_layers.items()}
    small =jnp.concatenate([G[n].reshape(-1) for n, _ in _SMALL] + [0.5 / D_MODEL * jnp.sum(loss_parts).reshape(1)])
    n_small = small.shape[0]
    small = jnp.pad(small, (0, (-n_small) % (8 * LANES))).reshape(-1, LANES)
    small_sum = _sum_rows(_all_gather(small, name="gather_small"), name="sum_small").reshape(-1)
    g_full, off = {}, 0
    for n, shp in _SMALL:
        size = int(np.prod(shp))
        g_full[n] = small_sum[off:off + size].reshape(shp)
        off += size
    loss = small_sum[off]
    grads = dict(g_local)
    for n, shp in _SMALL:
        if n == "norm_w":
            grads[n] = lax.dynamic_slice_in_dim(g_full[n], dev * LANES, LANES, axis=2)
        elif n == "s5_d":
            grads[n] = lax.dynamic_slice_in_dim(g_full[n], dev * LANES, LANES, axis=1)
        else:
            grads[n] = g_full[n]

    delta, new_m, new_v = {}, {}, {}
    for n in _WEIGHT_ORDER:
        delta[n], new_m[n], new_v[n] = _adamw(local[n], grads[n], mom_m[n], mom_v[n], name=f"adamw_{n}")
    return (loss, grad_x, *[grads[n] for n in _WEIGHT_ORDER], *[delta[n] for n in _WEIGHT_ORDER],
            *[new_m[n] for n in _WEIGHT_ORDER], *[new_v[n] for n in _WEIGHT_ORDER])
```

```python
import functools
import math

import numpy as np
import jax
import jax.numpy as jnp
from jax import lax
from jax.experimental import pallas as pl
from jax.experimental.pallas import tpu as pltpu

F32 = jnp.float32
BF16 = jnp.bfloat16
HI = lax.Precision.HIGHEST

D_MODEL = 1024
NORM_EPS = 1e-6
A_WIDTH = 512
A_HEAD = 128
A_CHUNK = 32
A_SUPER = 256
B_SPAN = 128
B_DILS = (1, 4, 16)
ROPE_THETA = 10000.0
C_GROUPS = 64
C_GROUP = 16
C_STATE = 64
C_TC = 8
C_MIN_NEG_RE = -1e-4
MEM_LEN = 256
X_HEADS = 4
X_HD = 256
D_FF = 2816
N_DEV = 8
LANES = 128

ADAM_LR, ADAM_B1, ADAM_B2, ADAM_EPS, ADAM_WD, ADAM_STEP = 0.001, 0.9, 0.999, 1e-08, 0.01, 10

NEG_BIG = -1e30


def _tile(n, pref):
    for d in range(min(pref, n) // LANES * LANES, 0, -LANES):
        if n % d == 0:
            return d
    return n


def _cparams(*sem):
    return pltpu.CompilerParams(dimension_semantics=sem, vmem_limit_bytes=56 * 1024 * 1024)


def _sigmoid(x):
    return 0.5 * jnp.tanh(0.5 * x) + 0.5


def _erf(x):
    ax = jnp.abs(x)
    t = 1.0 / (1.0 + 0.3275911 * ax)
    poly = t * (0.254829592 + t * (-0.284496736 + t * (1.421413741 + t * (-1.453152027 + t * 1.061405429))))
    y = 1.0 - poly * jnp.exp(-ax * ax)
    return jnp.where(x < 0, -y, y)


_HBM = pl.BlockSpec(memory_space=pltpu.HBM)
_MESH = pl.DeviceIdType.MESH


def _logical(px, py, pc):
    return 4 * px + 2 * py + pc


class _Exchange:
    def __init__(self, gather=(), scatter=()):
        self.gather, self.scatter = list(gather), list(scatter)
        self.ng, self.n = len(self.gather), len(self.gather) + len(self.scatter)

    def operands(self):
        return self.gather + self.scatter

    def in_specs(self):
        return [_HBM] * self.n

    def out_shapes(self):
        return ([jax.ShapeDtypeStruct((N_DEV,) + g.shape, g.dtype) for g in self.gather]
                + [jax.ShapeDtypeStruct(s.shape, s.dtype) for s in self.scatter])

    def scratch(self):
        if not self.n:
            return []
        return [pltpu.SemaphoreType.DMA((self.n, 7)), pltpu.SemaphoreType.DMA((self.n, 7)), pltpu.SemaphoreType.DMA((self.n,))]

    def split(self, results):
        return list(results[:self.ng]), list(results[self.ng:])

    def run(self, ins, outs, sems, first, last):
        if not self.n:
            return
        send_sems, recv_sems, local_sems = sems
        x, y, c = lax.axis_index("x"), lax.axis_index("y"), lax.axis_index("c")
        me, sibling = _logical(x, y, c), (x, y, 1 - c)
        chips = [(1 - x, y), (x, 1 - y), (1 - x, 1 - y)]
        peers = [(x ^ (k >> 2), y ^ ((k >> 1) & 1), c ^ (k & 1)) for k in range(1, N_DEV)]

        def remote(t, k, src, dst, to):
            return pltpu.make_async_remote_copy(src_ref=src, dst_ref=dst, send_sem=send_sems.at[t, k],
                                                recv_sem=recv_sems.at[t, k], device_id=to, device_id_type=_MESH)

        def local(t):
            src = ins[t] if t < self.ng else ins[t].at[me]
            return pltpu.make_async_copy(src, outs[t].at[me], local_sems.at[t])

        @pl.when(first)
        def _():
            for t in range(self.n):
                local(t).start()
                if t < self.ng:
                    remote(t, 0, ins[t], outs[t].at[me], sibling).start()
                    for j, chip in enumerate(chips):
                        remote(t, 1 + j, ins[t], outs[t].at[me], (*chip, c)).start()
                else:
                    for k, peer in enumerate(peers):
                        remote(t, k, ins[t].at[_logical(*peer)], outs[t].at[me], peer).start()

        @pl.when(last)
        def _():
            for j, chip in enumerate(chips):
                for t in range(self.ng):
                    landed = outs[t].at[_logical(*chip, c)]
                    remote(t, 1 + j, ins[t], landed, sibling).wait_recv()
                    remote(t, 4 + j, landed, landed, sibling).start()
            for t in range(self.n):
                if t < self.ng:
                    remote(t, 0, ins[t], outs[t].at[_logical(*sibling)], sibling).wait_recv()
                    for j, chip in enumerate(chips):
                        remote(t, 4 + j, ins[t], outs[t].at[_logical(*chip, 1 - c)], sibling).wait_recv()
                    for k in range(7):
                        remote(t, k, ins[t], outs[t].at[me], sibling).wait_send()
                else:
                    for k, peer in enumerate(peers):
                        remote(t, k, ins[t].at[me], outs[t].at[_logical(*peer)], peer).wait_recv()
                    for k, peer in enumerate(peers):
                        remote(t, k, ins[t].at[_logical(*peer)], outs[t].at[me], peer).wait_send()
                local(t).wait()


_MM_VMEM_BUDGET = 36 * 1024 * 1024


def _mm(a, b, *, ta=False, tb=False, out_dtype=F32, name, tiles=(1408, 1408, 4096), gather=(), scatter=(), gate=None):
    M, K = (a.shape[1], a.shape[0]) if ta else a.shape
    N = b.shape[0] if tb else b.shape[1]
    assert (b.shape[1] if tb else b.shape[0]) == K
    gate_mode = gate[0] if gate else None
    tm, tn, tk = _tile(M, tiles[0]), _tile(N, tiles[1]), _tile(K, tiles[2])
    if gate_mode == "fwd":
        bs = _gate_block(N // 2)
        tn = 2 * bs
    elif gate_mode == "bwd":
        bs = _gate_block(N)
        tn = bs

    def vmem_bytes():
        acc = 4 * tm * tn if tk < K else 0
        if gate_mode == "fwd":
            io = (2 * (2 + 1) + 4) * tm * tn
        elif gate_mode == "bwd":
            io = (2 * (4 + 4) + 4) * tm * tn
        else:
            io = 2 * tm * tn * jnp.dtype(out_dtype).itemsize
        return 2 * 2 * (tm * tk + tk * tn) + acc + io

    while vmem_bytes() > _MM_VMEM_BUDGET:
        if gate_mode and tm > 256:
            tm = _tile(M, tm - LANES)
        elif tk > 512:
            tk = _tile(K, tk - LANES)
        elif tn > 256 and not gate_mode:
            tn = _tile(N, tn - LANES)
        else:
            tm = _tile(M, tm - LANES)
    ni, nj, nk = M // tm, N // tn, K // tk
    ex = _Exchange(gather, scatter)

    a_spec = pl.BlockSpec((tk, tm), lambda i, j, k: (k, i)) if ta else pl.BlockSpec((tm, tk), lambda i, j, k: (i, k))
    b_spec = pl.BlockSpec((tn, tk), lambda i, j, k: (j, k)) if tb else pl.BlockSpec((tk, tn), lambda i, j, k: (k, j))
    dims = (((0 if ta else 1,), (1 if tb else 0,)), ((), ()))
    n_acc = 1 if nk > 1 else 0
    n_in = 3 if gate_mode == "bwd" else 2
    n_out = 2 if gate_mode == "fwd" else 1

    def finish(val, in_refs, out_refs, rows=slice(None)):
        if gate_mode is None:
            out_refs[0][rows, :] = val.astype(out_refs[0].dtype)
        elif gate_mode == "fwd":
            out_refs[0][rows, :] = val.astype(BF16)
            for p in range(tn // (2 * bs)):
                a_, b_ = val[:, 2 * p * bs:(2 * p + 1) * bs], val[:, (2 * p + 1) * bs:(2 * p + 2) * bs]
                out_refs[1][rows, p * bs:(p + 1) * bs] = _gate_value(a_, b_, gate[1]).astype(BF16)
        else:
            z_ref = in_refs[2]
            for p in range(tn // bs):
                a_ = z_ref[rows, 2 * p * bs:(2 * p + 1) * bs].astype(F32)
                b_ = z_ref[rows, (2 * p + 1) * bs:(2 * p + 2) * bs].astype(F32)
                da, db = _gate_grads(a_, b_, val[:, p * bs:(p + 1) * bs], gate[1])
                out_refs[0][rows, 2 * p * bs:(2 * p + 1) * bs] = da.astype(BF16)
                out_refs[0][rows, (2 * p + 1) * bs:(2 * p + 2) * bs] = db.astype(BF16)

    halves = 2 if (gate_mode and nk == 1 and not ta and tm % 32 == 0) else 1

    def body(*refs):
        in_refs, rest = refs[:n_in], refs[n_in:]
        ex_in, out_refs = rest[:ex.n], rest[ex.n:ex.n + n_out]
        ex_out, scratch = rest[ex.n + n_out:2 * ex.n + n_out], rest[2 * ex.n + n_out:]
        i, j, k = pl.program_id(0), pl.program_id(1), pl.program_id(2)
        ex.run(ex_in, ex_out, scratch[n_acc:], (i == 0) & (j == 0) & (k == 0), (i == ni - 1) & (j == nj - 1) & (k == nk - 1))
        if halves > 1:
            rh = tm // halves
            rhs = in_refs[1][...].astype(BF16)
            parts = [lax.dot_general(in_refs[0][r * rh:(r + 1) * rh, :].astype(BF16), rhs, dims, preferred_element_type=F32)
                     for r in range(halves)]
            for r, p in enumerate(parts):
                finish(p, in_refs, out_refs, slice(r * rh, (r + 1) * rh))
            return
        part = lax.dot_general(in_refs[0][...].astype(BF16), in_refs[1][...].astype(BF16), dims, preferred_element_type=F32)
        if nk == 1:
            finish(part, in_refs, out_refs)
            return
        acc_ref = scratch[0]

        @pl.when(k == 0)
        def _():
            acc_ref[...] = part

        @pl.when(k > 0)
        def _():
            acc_ref[...] += part

        @pl.when(k == nk - 1)
        def _():
            finish(acc_ref[...], in_refs, out_refs)

    tile = lambda width: pl.BlockSpec((tm, width), lambda i, j, k: (i, j))
    if gate_mode == "fwd":
        out_specs, out_shape = [tile(tn), tile(tn // 2)], [jax.ShapeDtypeStruct((M, N), BF16), jax.ShapeDtypeStruct((M, N // 2), BF16)]
    elif gate_mode == "bwd":
        out_specs, out_shape = [tile(2 * tn)], [jax.ShapeDtypeStruct((M, 2 * N), BF16)]
    else:
        out_specs, out_shape = [tile(tn)], [jax.ShapeDtypeStruct((M, N), out_dtype)]
    operands = [a, b] + ([gate[2]] if gate_mode == "bwd" else [])
    sem = ("arbitrary",) * 3 if ex.n else ("parallel", "parallel", "arbitrary")
    res = pl.pallas_call(
        body, name=name, grid=(ni, nj, nk),
        in_specs=[a_spec, b_spec] + ([tile(2 * tn)] if gate_mode == "bwd" else []) + ex.in_specs(),
        out_specs=out_specs + ex.in_specs(),
        out_shape=out_shape + ex.out_shapes(),
        scratch_shapes=([pltpu.VMEM((tm, tn), F32)] if nk > 1 else []) + ex.scratch(),
        compiler_params=_cparams(*sem),
    )(*operands, *ex.operands())
    main = res[0] if n_out == 1 else tuple(res[:n_out])
    if not ex.n:
        return main
    return (main,) + tuple(ex.split(res[n_out:]))


def _rms_fwd(x, w, res=None, *, name, out_dtype=F32, then=None):
    T, C = x.shape
    tm = _tile(T, 512)
    has_res = res is not None

    def norm(v, w_ref):
        return v * lax.rsqrt(jnp.mean(v * v, axis=-1, keepdims=True) + NORM_EPS) * w_ref[...]

    def body(*refs):
        x_ref, w_ref = refs[0], refs[1]
        y = norm(x_ref[...].astype(F32), w_ref)
        if has_res:
            y = y + refs[2][...]
        if then is None:
            refs[-1][...] = y.astype(refs[-1].dtype)
        else:
            refs[-2][...] = y.astype(refs[-2].dtype)
            refs[-1][...] = norm(y, refs[-3]).astype(refs[-1].dtype)

    row = pl.BlockSpec((tm, C), lambda i: (i, 0))
    vec = pl.BlockSpec((1, C), lambda i: (0, 0))
    ins = [x, w] + ([res] if has_res else []) + ([then[0]] if then else [])
    in_specs = [row, vec] + ([row] if has_res else []) + ([vec] if then else [])
    out_shape = [jax.ShapeDtypeStruct((T, C), out_dtype)] + ([jax.ShapeDtypeStruct((T, C), then[1])] if then else [])
    res_ = pl.pallas_call(
        body, name=name, grid=(T // tm,), in_specs=in_specs, out_specs=[row] * len(out_shape),
        out_shape=out_shape, compiler_params=_cparams("parallel"),
    )(*ins)
    return tuple(res_) if then else res_[0]


def _rms_bwd(x, w, dy, add=None, *, name, out_dtype=F32, then=None):
    T, C = x.shape
    tm = _tile(T, 512)
    has_add = add is not None
    n_in = 3 + has_add + (2 if then else 0)

    def grads(x_ref, w_ref, g, dw_ref):
        xv = x_ref[...].astype(F32)
        r = lax.rsqrt(jnp.mean(xv * xv, axis=-1, keepdims=True) + NORM_EPS)
        xh = xv * r
        part = jnp.sum(g * xh, axis=0, keepdims=True)

        @pl.when(pl.program_id(0) == 0)
        def _():
            dw_ref[...] = part

        @pl.when(pl.program_id(0) > 0)
        def _():
            dw_ref[...] += part

        gx = g * w_ref[...]
        return r * (gx - xh * jnp.mean(gx * xh, axis=-1, keepdims=True))

    def body(*refs):
        ins, outs = refs[:n_in], refs[n_in:]
        dx = grads(ins[0], ins[1], ins[2][...].astype(F32), outs[1])
        if has_add:
            dx = dx + ins[3][...]
        outs[0][...] = dx.astype(outs[0].dtype)
        if then:
            outs[2][...] = grads(ins[-2], ins[-1], dx, outs[3]).astype(BF16)

    row = pl.BlockSpec((tm, C), lambda i: (i, 0))
    vec = pl.BlockSpec((1, C), lambda i: (0, 0))
    ins = [x, w, dy] + ([add] if has_add else []) + (list(then) if then else [])
    big, small = jax.ShapeDtypeStruct((T, C), out_dtype), jax.ShapeDtypeStruct((1, C), F32)
    return pl.pallas_call(
        body, name=name, grid=(T // tm,),
        in_specs=[row, vec, row] + ([row] if has_add else []) + ([row, vec] if then else []),
        out_specs=[row, vec] + ([row, vec] if then else []),
        out_shape=[big, small] + ([jax.ShapeDtypeStruct((T, C), BF16), small] if then else []),
        compiler_params=_cparams("arbitrary"),
    )(*ins)


def _gate_block(width):
    return _tile(width, 1408)


def _gate_value(a, b, kind):
    return a * _sigmoid(a) * b if kind == "swiglu" else a * _sigmoid(b)


def _gate_grads(a, b, d, kind):
    if kind == "swiglu":
        s = _sigmoid(a)
        return d * b * (s * (1.0 + a * (1.0 - s))), d * a * s
    s = _sigmoid(b)
    return d * s, d * a * s * (1.0 - s)


def _interleave_rows(w, inverse=False):
    W2, C = w.shape
    bs = _gate_block(W2 // 2)
    nb = W2 // 2 // bs
    shape = (nb, 2, bs, C) if inverse else (2, nb, bs, C)
    return w.reshape(shape).transpose(1, 0, 2, 3).reshape(W2, C)


def _gated_bwd(z, dout, kind, *, name):
    T, W2 = z.shape
    W = W2 // 2
    tm, bs = _tile(T, 512), _gate_block(W)

    def body(z_ref, d_ref, o_ref):
        da, db = _gate_grads(z_ref[:, :bs].astype(F32), z_ref[:, bs:].astype(F32), d_ref[...].astype(F32), kind)
        o_ref[:, :bs] = da.astype(o_ref.dtype)
        o_ref[:, bs:] = db.astype(o_ref.dtype)

    return pl.pallas_call(
        body, name=name, grid=(T // tm, W // bs),
        in_specs=[pl.BlockSpec((tm, 2 * bs), lambda i, j: (i, j)), pl.BlockSpec((tm, bs), lambda i, j: (i, j))],
        out_specs=pl.BlockSpec((tm, 2 * bs), lambda i, j: (i, j)),
        out_shape=jax.ShapeDtypeStruct((T, W2), BF16), compiler_params=_cparams("parallel", "parallel"),
    )(z, dout)


_NT = (((1,), (1,)), ((), ()))
_TN = (((0,), (0,)), ((), ()))


def _dot(a, b, dims=None, precision=None):
    if dims is None:
        return jnp.dot(a, b, preferred_element_type=F32, precision=precision)
    return lax.dot_general(a, b, dims, preferred_element_type=F32, precision=precision)


def _xattn_fwd(q, kv, B, L, *, name):
    T = q.shape[0]
    tq = 256
    nq = L // tq
    scale = X_HD ** -0.5

    def body(q_ref, k_ref, v_ref, o_ref):
        heads = [slice(h * X_HD, (h + 1) * X_HD) for h in range(X_HEADS)]
        s = [_dot(q_ref[:, sl].astype(BF16), k_ref[:, sl].astype(BF16), _NT) * scale for sl in heads]
        m = [jnp.max(a, axis=-1, keepdims=True) for a in s]
        p = [jnp.exp(a - b) for a, b in zip(s, m)]
        l = [jnp.sum(a, axis=-1, keepdims=True) for a in p]
        o = [_dot(a.astype(BF16), v_ref[:, sl].astype(BF16)) for a, sl in zip(p, heads)]
        for sl, a, b in zip(heads, o, l):
            o_ref[:, sl] = (a / b).astype(BF16)

    return pl.pallas_call(
        body, name=name, grid=(B, nq),
        in_specs=[pl.BlockSpec((tq, D_MODEL), lambda b, i: (b * nq + i, 0)),
                  pl.BlockSpec((MEM_LEN, D_MODEL), lambda b, i: (b, 0)),
                  pl.BlockSpec((MEM_LEN, D_MODEL), lambda b, i: (b, 1))],
        out_specs=pl.BlockSpec((tq, D_MODEL), lambda b, i: (b * nq + i, 0)),
        out_shape=jax.ShapeDtypeStruct((T, D_MODEL), BF16), compiler_params=_cparams("parallel", "parallel"),
    )(q, kv, kv)


def _xattn_bwd(q, kv, do, B, L, *, name):
    T = q.shape[0]
    tq = 256
    nq = L // tq
    scale = X_HD ** -0.5

    def body(q_ref, k_ref, v_ref, do_ref, dq_ref, dkv_ref):
        @pl.when(pl.program_id(1) == 0)
        def _():
            dkv_ref[...] = jnp.zeros_like(dkv_ref)

        heads = [slice(h * X_HD, (h + 1) * X_HD) for h in range(X_HEADS)]
        qs = [q_ref[:, sl].astype(BF16) for sl in heads]
        ks = [k_ref[:, sl].astype(BF16) for sl in heads]
        dos = [do_ref[:, sl].astype(BF16) for sl in heads]
        s = [_dot(a, b, _NT) * scale for a, b in zip(qs, ks)]
        dp = [_dot(a, v_ref[:, sl].astype(BF16), _NT) for a, sl in zip(dos, heads)]
        e = [jnp.exp(a - jnp.max(a, axis=-1, keepdims=True)) for a in s]
        p = [a / jnp.sum(a, axis=-1, keepdims=True) for a in e]
        ds = [(a * (b - jnp.sum(b * a, axis=-1, keepdims=True)) * scale).astype(BF16) for a, b in zip(p, dp)]
        dv = [_dot(a.astype(BF16), b, _TN) for a, b in zip(p, dos)]
        dq = [_dot(a, b) for a, b in zip(ds, ks)]
        dk = [_dot(a, b, _TN) for a, b in zip(ds, qs)]
        for h, sl in enumerate(heads):
            dq_ref[:, sl] = dq[h].astype(BF16)
            dkv_ref[:, sl] += dk[h]
            dkv_ref[:, D_MODEL + h * X_HD:D_MODEL + (h + 1) * X_HD] += dv[h]

    return pl.pallas_call(
        body, name=name, grid=(B, nq),
        in_specs=[pl.BlockSpec((tq, D_MODEL), lambda b, i: (b * nq + i, 0)),
                  pl.BlockSpec((MEM_LEN, D_MODEL), lambda b, i: (b, 0)),
                  pl.BlockSpec((MEM_LEN, D_MODEL), lambda b, i: (b, 1)),
                  pl.BlockSpec((tq, D_MODEL), lambda b, i: (b * nq + i, 0))],
        out_specs=[pl.BlockSpec((tq, D_MODEL), lambda b, i: (b * nq + i, 0)),
                   pl.BlockSpec((MEM_LEN, 2 * D_MODEL), lambda b, i: (b, 0))],
        out_shape=[jax.ShapeDtypeStruct((T, D_MODEL), BF16), jax.ShapeDtypeStruct((B * MEM_LEN, 2 * D_MODEL), F32)],
        compiler_params=_cparams("parallel", "arbitrary"),
    )(q, kv, kv, do)


def _chunk_masks():
    row = lax.broadcasted_iota(jnp.int32, (A_SUPER, A_SUPER), 0)
    col = lax.broadcasted_iota(jnp.int32, (A_SUPER, A_SUPER), 1)
    same = jnp.right_shift(row, 5) == jnp.right_shift(col, 5)
    return same, same & (col <= row), same & (col >= row)


def _dot_mask(mask, x):
    m = mask.astype(BF16)
    hi = x.astype(BF16)
    rest = x - hi.astype(F32)
    mid = rest.astype(BF16)
    lo = (rest - mid.astype(F32)).astype(BF16)
    return _dot(m, hi) + _dot(m, mid) + _dot(m, lo)


def _chunk_row(x, which):
    rows = [x[c * A_CHUNK + which % A_CHUNK:c * A_CHUNK + which % A_CHUNK + 1, :] for c in range(A_SUPER // A_CHUNK)]
    return jnp.concatenate([jnp.broadcast_to(r, (A_CHUNK, x.shape[1])) for r in rows], axis=0)


def _hgrn_gates(fa, lb):
    sig = _sigmoid(fa)
    f = lb + (1.0 - lb) * sig
    return sig, f, jnp.log(f), 1.0 - f


def _hgrn_fwd(z, lb, onw, B, L, *, name):
    T = B * L
    ns = L // A_SUPER
    nch = A_SUPER // A_CHUNK

    def body(q_ref, f_ref, v_ref, g_ref, lb_ref, w_ref, oa_ref, o_ref, s_ref, st_ref, sc_ref):
        @pl.when(pl.program_id(2) == 0)
        def _():
            st_ref[...] = jnp.zeros_like(st_ref)

        s_ref[0] = st_ref[...]
        same, tril, _ = _chunk_masks()
        q, v = q_ref[...], v_ref[...]
        _, _, lf, k = _hgrn_gates(f_ref[...], lb_ref[...])
        bcs = _dot_mask(tril, lf)
        bl = _chunk_row(bcs, -1)
        qd = (q * jnp.exp(bcs)).astype(BF16)
        ki = (k * jnp.exp(-bcs)).astype(BF16)
        ke = (k * jnp.exp(bl - bcs)).astype(BF16)
        dec = jnp.exp(bl)
        vb = v.astype(BF16)
        a = jnp.where(tril, _dot(qd, ki, _NT), 0.0)
        o_ref[...] = _dot(a.astype(BF16), vb)
        chunks = [slice(c * A_CHUNK, (c + 1) * A_CHUNK) for c in range(nch)]
        outer = [_dot(vb[rs], ke[rs], _TN) for rs in chunks]
        st = st_ref[...]
        for c, rs in enumerate(chunks):
            sc_ref[c] = st.astype(BF16)
            st = st * dec[c * A_CHUNK:c * A_CHUNK + 1, :] + outer[c]
        st_ref[...] = st
        for c, rs in enumerate(chunks):
            o_ref[rs, :] += _dot(qd[rs], sc_ref[c], _NT)
        o = o_ref[...]
        r = lax.rsqrt(jnp.mean(o * o, axis=-1, keepdims=True) + NORM_EPS)
        g = g_ref[...]
        oa_ref[...] = o * r * w_ref[...] * (g * _sigmoid(g))

    def zspec(off):
        return pl.BlockSpec((A_SUPER, A_HEAD), lambda b, h, n: (b * ns + n, off + h))

    hvec = pl.BlockSpec((1, A_HEAD), lambda b, h, n: (0, h))
    ospec = pl.BlockSpec((A_SUPER, A_HEAD), lambda b, h, n: (b * ns + n, h))
    return pl.pallas_call(
        body, name=name, grid=(B, 4, ns),
        in_specs=[zspec(0), zspec(4), zspec(8), zspec(12), hvec, hvec],
        out_specs=[ospec, ospec, pl.BlockSpec((1, A_HEAD, A_HEAD), lambda b, h, n: ((b * 4 + h) * ns + n, 0, 0))],
        out_shape=[jax.ShapeDtypeStruct((T, A_WIDTH), F32), jax.ShapeDtypeStruct((T, A_WIDTH), F32),
                   jax.ShapeDtypeStruct((B * 4 * ns, A_HEAD, A_HEAD), F32)],
        scratch_shapes=[pltpu.VMEM((A_HEAD, A_HEAD), F32), pltpu.VMEM((nch, A_HEAD, A_HEAD), BF16)],
        compiler_params=_cparams("parallel", "parallel", "arbitrary"),
    )(z, z, z, z, lb, onw)


def _hgrn_bwd(z, lb, onw, o_raw, s_start, doa, B, L, *, name):
    T = B * L
    ns = L // A_SUPER
    nch = A_SUPER // A_CHUNK

    def body(q_ref, f_ref, v_ref, g_ref, lb_ref, w_ref, o_ref, s_ref, doa_ref,
             dq_ref, df_ref, dv_ref, dg_ref, dw_ref, dlb_ref, dst_ref, sc_ref, dsc_ref, dqd_ref, dke_ref, dblx_ref):
        @pl.when(pl.program_id(2) == 0)
        def _():
            dst_ref[...] = jnp.zeros_like(dst_ref)
            dw_ref[...] = jnp.zeros_like(dw_ref)
            dlb_ref[...] = jnp.zeros_like(dlb_ref)

        same, tril, triu = _chunk_masks()
        q, v, g, lb, w = q_ref[...], v_ref[...], g_ref[...], lb_ref[...], w_ref[...]
        sig, f, lf, k = _hgrn_gates(f_ref[...], lb)
        bcs = _dot_mask(tril, lf)
        bl = _chunk_row(bcs, -1)
        eb, enb, eeb = jnp.exp(bcs), jnp.exp(-bcs), jnp.exp(bl - bcs)
        qd, ki, ke = q * eb, k * enb, k * eeb
        qdb, kib, keb, vb = qd.astype(BF16), ki.astype(BF16), ke.astype(BF16), v.astype(BF16)
        dec = jnp.exp(bl)
        o = o_ref[...]
        r = lax.rsqrt(jnp.mean(o * o, axis=-1, keepdims=True) + NORM_EPS)
        on = o * r
        sg = _sigmoid(g)
        silu_g = g * sg
        doa = doa_ref[...]
        dg_ref[...] = doa * on * w * (sg * (1.0 + g * (1.0 - sg)))
        dw_ref[0] += jnp.sum(doa * on * silu_g, axis=0, keepdims=True)
        don = doa * w * silu_g
        do = r * (don - on * jnp.mean(don * on, axis=-1, keepdims=True))
        dob = do.astype(BF16)
        a = jnp.where(tril, _dot(qdb, kib, _NT), 0.0).astype(BF16)
        da = jnp.where(tril, _dot(dob, vb, _NT), 0.0).astype(BF16)
        dv_ref[...] = _dot(a, dob, _TN)
        dqd_ref[...] = _dot(da, kib)
        dki = _dot(da, qdb, _TN)
        chunks = [slice(c * A_CHUNK, (c + 1) * A_CHUNK) for c in range(nch)]
        outer = [_dot(vb[rs], keb[rs], _TN) for rs in chunks]
        st = s_ref[0]
        for c in range(nch):
            sc_ref[c] = st
            st = st * dec[c * A_CHUNK:c * A_CHUNK + 1, :] + outer[c]
        outer_g = [_dot(dob[rs], qdb[rs], _TN) for rs in chunks]
        dst = dst_ref[...]
        for c in reversed(range(nch)):
            dsc_ref[c] = dst
            dst = dst * dec[c * A_CHUNK:c * A_CHUNK + 1, :] + outer_g[c]
        dst_ref[...] = dst
        for c, rs in enumerate(chunks):
            dec_c = dec[c * A_CHUNK:c * A_CHUNK + 1, :]
            dsc, stc = dsc_ref[c], sc_ref[c]
            dscb = dsc.astype(BF16)
            dv_ref[rs, :] += _dot(keb[rs], dscb, _NT)
            dke_ref[rs, :] = _dot(vb[rs], dscb)
            ddec = jnp.sum(dsc * stc, axis=0, keepdims=True)
            dqd_ref[rs, :] += _dot(dob[rs], stc.astype(BF16))
            dblx_ref[rs, :] = jnp.broadcast_to(ddec * dec_c, (A_CHUNK, A_HEAD))
        dqd, dke = dqd_ref[...], dke_ref[...]
        dq_ref[...] = dqd * eb
        keke = dke * ke
        db = dqd * qd - dki * ki - keke
        sums = _dot_mask(triu, jnp.concatenate([db, keke], axis=1))
        dk = dki * enb + dke * eeb
        dlf = sums[:, :A_HEAD] + _chunk_row(sums[:, A_HEAD:], 0) + dblx_ref[...]
        dff = dlf / f - dk
        df_ref[...] = dff * (1.0 - lb) * sig * (1.0 - sig)
        dlb_ref[0] += jnp.sum(dff * (1.0 - sig), axis=0, keepdims=True)

    def rev(n):
        return ns - 1 - n

    def zspec(off):
        return pl.BlockSpec((A_SUPER, A_HEAD), lambda b, h, n: (b * ns + rev(n), off + h))

    hvec = pl.BlockSpec((1, A_HEAD), lambda b, h, n: (0, h))
    ospec = pl.BlockSpec((A_SUPER, A_HEAD), lambda b, h, n: (b * ns + rev(n), h))
    acc = pl.BlockSpec((1, 1, A_HEAD), lambda b, h, n: (b * 4 + h, 0, 0))
    big = jax.ShapeDtypeStruct((T, A_WIDTH), F32)
    small = jax.ShapeDtypeStruct((B * 4, 1, A_HEAD), F32)
    return pl.pallas_call(
        body, name=name, grid=(B, 4, ns),
        in_specs=[zspec(0), zspec(4), zspec(8), zspec(12), hvec, hvec, ospec,
                  pl.BlockSpec((1, A_HEAD, A_HEAD), lambda b, h, n: ((b * 4 + h) * ns + rev(n), 0, 0)), ospec],
        out_specs=[ospec, ospec, ospec, ospec, acc, acc],
        out_shape=[big, big, big, big, small, small],
        scratch_shapes=[pltpu.VMEM((A_HEAD, A_HEAD), F32), pltpu.VMEM((nch, A_HEAD, A_HEAD), F32),
                        pltpu.VMEM((nch, A_HEAD, A_HEAD), F32),
                        pltpu.VMEM((A_SUPER, A_HEAD), F32), pltpu.VMEM((A_SUPER, A_HEAD), F32),
                        pltpu.VMEM((A_SUPER, A_HEAD), F32)],
        compiler_params=_cparams("parallel", "parallel", "arbitrary"),
    )(z, z, z, z, lb, onw, o_raw, s_start, doa)


def _rope_tables(L):
    half = A_HEAD // 2
    inv_freq = ROPE_THETA ** (-jnp.arange(half, dtype=F32) / half)
    ang = jnp.arange(L, dtype=F32)[:, None] * inv_freq[None, :]
    cos, sin = jnp.cos(ang), jnp.sin(ang)
    return jnp.concatenate([cos, cos], axis=-1), jnp.concatenate([-sin, sin], axis=-1)


def _rope_fwd(z, cos2, sin2, B, L, *, name):
    T = B * L
    tm = 512
    nl = L // tm

    def body(x_ref, c_ref, s_ref, q_ref, k_ref):
        c, s = c_ref[...], s_ref[...]
        for h in range(8):
            x = x_ref[:, h * A_HEAD:(h + 1) * A_HEAD]
            out = x * c + pltpu.roll(x, A_HEAD // 2, 1) * s
            o_ref = q_ref if h < 4 else k_ref
            o_ref[:, (h % 4) * A_HEAD:(h % 4 + 1) * A_HEAD] = out

    tab = pl.BlockSpec((tm, A_HEAD), lambda i: (i % nl, 0))
    out = pl.BlockSpec((tm, 512), lambda i: (i, 0))
    return pl.pallas_call(
        body, name=name, grid=(T // tm,),
        in_specs=[pl.BlockSpec((tm, 1024), lambda i: (i, 2)), tab, tab], out_specs=[out, out],
        out_shape=[jax.ShapeDtypeStruct((T, 512), F32)] * 2, compiler_params=_cparams("parallel"),
    )(z, cos2, sin2)


def _rope_bwd(dqs, dks, dvs, cos2, sin2, B, L, *, name):
    T = B * L
    tm = 256
    nl = L // tm

    def body(*refs):
        c, s = refs[9][...], refs[10][...]
        o_ref = refs[11]
        for part in range(3):
            a_ref, b_ref, c_ref = refs[3 * part:3 * part + 3]
            for h in range(4):
                cols = slice(h * A_HEAD, (h + 1) * A_HEAD)
                d = a_ref[:, cols] + b_ref[:, cols] + c_ref[:, cols]
                if part < 2:
                    d = d * c - pltpu.roll(d, A_HEAD // 2, 1) * s
                o_ref[:, part * 512 + h * A_HEAD:part * 512 + (h + 1) * A_HEAD] = d

    blk = pl.BlockSpec((tm, 512), lambda i: (i, 0))
    tab = pl.BlockSpec((tm, A_HEAD), lambda i: (i % nl, 0))
    return pl.pallas_call(
        body, name=name, grid=(T // tm,), in_specs=[blk] * 9 + [tab, tab],
        out_specs=pl.BlockSpec((tm, 1536), lambda i: (i, 0)),
        out_shape=jax.ShapeDtypeStruct((T, 1536), F32), compiler_params=_cparams("parallel"),
    )(*dqs, *dks, *dvs, cos2, sin2)


def _band_masks():
    i = lax.broadcasted_iota(jnp.int32, (B_SPAN, B_SPAN), 0)
    j = lax.broadcasted_iota(jnp.int32, (B_SPAN, B_SPAN), 1)
    return i <= j, j <= i


class _DilPlan:
    def __init__(self, dil, B, L):
        self.dil, self.B, self.L = dil, B, L
        self.rows = 4 * B_SPAN if dil == 1 else B_SPAN * dil
        self.n = L // self.rows
        self.hr = 4 if dil == 4 else 1
        self.cw = 512 if dil == 1 else A_HEAD
        self.has_other = dil != 16
        self.grid = (B, self.n, 4 if dil == 16 else 1)
        if dil == 1:
            self.items = [(j, 0, h) for j in range(4) for h in range(4)]
        elif dil == 4:
            self.items = [(0, r, h) for r in range(4) for h in range(4)]
        else:
            self.items = [(0, r, 0) for r in range(16)]
        self.groups = [self.items[i:i + 4] for i in range(0, 16, 4)]

    def operands(self, arr):
        return [arr] * self.hr

    def specs(self, col0, role="cur"):
        n, nb128 = self.n, self.L // B_SPAN
        out = []
        for h in range(self.hr):
            cb = col0 // self.cw + h
            if role == "cur":
                out.append(pl.BlockSpec((self.rows, self.cw), lambda b, i, hh, cb=cb: (b * n + i, cb + hh)))
            elif self.dil == 1:
                shift = -1 if role == "prev" else 4
                out.append(pl.BlockSpec((B_SPAN, self.cw),
                                        lambda b, i, hh, cb=cb, shift=shift: (b * nb128 + jnp.clip(4 * i + shift, 0, nb128 - 1), cb)))
            else:
                shift = -1 if role == "prev" else 1
                out.append(pl.BlockSpec((self.rows, self.cw),
                                        lambda b, i, hh, cb=cb, shift=shift: (b * n + jnp.clip(i + shift, 0, n - 1), cb)))
        return out

    def out_spec(self):
        n = self.n
        if self.dil == 4:
            return pl.BlockSpec((self.rows, 512), lambda b, i, hh: (b * n + i, 0))
        return pl.BlockSpec((self.rows, self.cw), lambda b, i, hh: (b * n + i, hh))

    def scratch(self, n_out):
        return [pltpu.VMEM((4, self.rows, A_HEAD), F32)] * n_out if self.dil == 4 else []

    def store(self, out_ref, scr, item, val):
        j, r, h = item
        if self.dil == 1:
            out_ref[pl.ds(j * B_SPAN, B_SPAN), pl.ds(h * A_HEAD, A_HEAD)] = val
        elif self.dil == 4:
            scr.at[h][pl.ds(r, B_SPAN, stride=4), :] = val
        else:
            out_ref[pl.ds(r, B_SPAN, stride=self.dil), :] = val

    def flush(self, out_ref, scr):
        if self.dil == 4:
            for h in range(4):
                out_ref[:, h * A_HEAD:(h + 1) * A_HEAD] = scr[h]

    def cur(self, refs, item):
        j, r, h = item
        if self.dil == 1:
            return refs[0], (pl.ds(j * B_SPAN, B_SPAN), pl.ds(h * A_HEAD, A_HEAD))
        return refs[h], (pl.ds(r, B_SPAN, stride=self.dil), slice(None))

    def other(self, refs, other_refs, item, role):
        j, r, h = item
        if self.dil == 1:
            cols = pl.ds(h * A_HEAD, A_HEAD)
            jj = j - 1 if role == "prev" else j + 1
            if 0 <= jj < 4:
                return refs[0], (pl.ds(jj * B_SPAN, B_SPAN), cols)
            return other_refs[0], (pl.ds(0, B_SPAN), cols)
        return other_refs[h], (pl.ds(r, B_SPAN, stride=self.dil), slice(None))

    def other_valid(self, item, role):
        j = item[0]
        i = pl.program_id(1)
        if role == "prev":
            return True if (self.dil == 1 and j > 0) else i > 0
        return True if (self.dil == 1 and j < 3) else i < self.n - 1


def _ld(pair):
    ref, idx = pair
    return ref[idx]


def _dil_fwd(qr, kr, z, dil, B, L, *, name):
    T = B * L
    plan = _DilPlan(dil, B, L)
    hr, has_prev = plan.hr, plan.has_other
    scale = A_HEAD ** -0.5

    n_t = 5 if has_prev else 3

    def body(*refs):
        lists = [refs[i * hr:(i + 1) * hr] for i in range(n_t)]
        o_ref, l_ref = refs[n_t * hr], refs[n_t * hr + 1]
        scr = refs[n_t * hr + 2:] or (None, None)
        if has_prev:
            q_r, kc_r, vc_r, kp_r, vp_r = lists
        else:
            q_r, kc_r, vc_r = lists
        mp0, mc = _band_masks()

        for group in plan.groups:
            qs = [_ld(plan.cur(q_r, i)).astype(BF16) for i in group]
            sc = [jnp.where(mc, _dot(q, _ld(plan.cur(kc_r, i)).astype(BF16), _NT) * scale, NEG_BIG) for q, i in zip(qs, group)]
            m = [jnp.max(s, axis=-1, keepdims=True) for s in sc]
            if has_prev:
                mps = [mp0 & plan.other_valid(i, "prev") for i in group]
                sp = [jnp.where(mk, _dot(q, _ld(plan.other(kc_r, kp_r, i, "prev")).astype(BF16), _NT) * scale, NEG_BIG)
                      for q, i, mk in zip(qs, group, mps)]
                m = [jnp.maximum(a, jnp.max(s, axis=-1, keepdims=True)) for a, s in zip(m, sp)]
            pc = [jnp.exp(s - a) for s, a in zip(sc, m)]
            l = [jnp.sum(p, axis=-1, keepdims=True) for p in pc]
            o = [_dot(p.astype(BF16), _ld(plan.cur(vc_r, i)).astype(BF16)) for p, i in zip(pc, group)]
            if has_prev:
                pp = [jnp.exp(s - a) for s, a in zip(sp, m)]
                l = [a + jnp.sum(p, axis=-1, keepdims=True) for a, p in zip(l, pp)]
                o = [a + _dot(p.astype(BF16), _ld(plan.other(vc_r, vp_r, i, "prev")).astype(BF16)) for a, p, i in zip(o, pp, group)]
            for i, oi, li, mi in zip(group, o, l, m):
                plan.store(o_ref, scr[0], i, oi / li)
                plan.store(l_ref, scr[1], i, jnp.broadcast_to(mi + jnp.log(li), (B_SPAN, A_HEAD)))
        plan.flush(o_ref, scr[0])
        plan.flush(l_ref, scr[1])

    tensors = [(qr, 0, "cur"), (kr, 0, "cur"), (z, 3072, "cur")] + ([(kr, 0, "prev"), (z, 3072, "prev")] if has_prev else [])
    return pl.pallas_call(
        body, name=name, grid=plan.grid,
        in_specs=[sp for _, c, role in tensors for sp in plan.specs(c, role)],
        out_specs=[plan.out_spec()] * 2, out_shape=[jax.ShapeDtypeStruct((T, 512), F32)] * 2,
        scratch_shapes=plan.scratch(2), compiler_params=_cparams("parallel", "parallel", "parallel"),
    )(*[a for arr, _, _ in tensors for a in plan.operands(arr)])


def _dil_combine(os_, ls_, *, name):
    T = os_[0].shape[0]
    tm = 256

    def body(o1, o2, o3, l1, l2, l3, ob_ref, lse_ref):
        a1, a2, a3 = l1[...], l2[...], l3[...]
        m = jnp.maximum(jnp.maximum(a1, a2), a3)
        e1, e2, e3 = jnp.exp(a1 - m), jnp.exp(a2 - m), jnp.exp(a3 - m)
        den = e1 + e2 + e3
        ob_ref[...] = (e1 * o1[...] + e2 * o2[...] + e3 * o3[...]) / den
        lse_ref[...] = m + jnp.log(den)

    blk = pl.BlockSpec((tm, 512), lambda i: (i, 0))
    return pl.pallas_call(
        body, name=name, grid=(T // tm,), in_specs=[blk] * 6, out_specs=[blk, blk],
        out_shape=[jax.ShapeDtypeStruct((T, 512), F32)] * 2, compiler_params=_cparams("parallel"),
    )(*[o.reshape(T, 512) for o in os_], *[l.reshape(T, 512) for l in ls_])


def _dil_bwd_q(qr, kr, z, dymix, out, lse, dil, B, L, *, name):
    T = B * L
    plan = _DilPlan(dil, B, L)
    hr, has_prev = plan.hr, plan.has_other
    scale = A_HEAD ** -0.5
    n_t = 8 if has_prev else 6

    def body(*refs):
        lists = [refs[i * hr:(i + 1) * hr] for i in range(n_t)]
        dq_ref = refs[n_t * hr]
        scr = refs[n_t * hr + 1:] or (None,)
        q_r, kc_r, vc_r, do_r, out_r, lse_r = lists[:6]
        mp0, mc = _band_masks()

        for group in plan.groups:
            qs = [_ld(plan.cur(q_r, i)).astype(BF16) for i in group]
            dos = [_ld(plan.cur(do_r, i)) for i in group]
            delta = [jnp.sum(d * _ld(plan.cur(out_r, i)), axis=-1, keepdims=True) for d, i in zip(dos, group)]
            dob = [d.astype(BF16) for d in dos]
            lse = [_ld(plan.cur(lse_r, i)) for i in group]
            kc = [_ld(plan.cur(kc_r, i)).astype(BF16) for i in group]
            pc = [jnp.where(mc, jnp.exp(_dot(q, k, _NT) * scale - a), 0.0) for q, k, a in zip(qs, kc, lse)]
            dsc = [p * (_dot(d, _ld(plan.cur(vc_r, i)).astype(BF16), _NT) - dl) * scale
                   for p, d, i, dl in zip(pc, dob, group, delta)]
            dq = [_dot(d.astype(BF16), k) for d, k in zip(dsc, kc)]
            if has_prev:
                kp_r, vp_r = lists[6], lists[7]
                mps = [mp0 & plan.other_valid(i, "prev") for i in group]
                kp = [_ld(plan.other(kc_r, kp_r, i, "prev")).astype(BF16) for i in group]
                pp = [jnp.where(mk, jnp.exp(_dot(q, k, _NT) * scale - a), 0.0) for q, k, a, mk in zip(qs, kp, lse, mps)]
                dsp = [p * (_dot(d, _ld(plan.other(vc_r, vp_r, i, "prev")).astype(BF16), _NT) - dl) * scale
                       for p, d, i, dl in zip(pp, dob, group, delta)]
                dq = [a + _dot(d.astype(BF16), k) for a, d, k in zip(dq, dsp, kp)]
            for i, d in zip(group, dq):
                plan.store(dq_ref, scr[0], i, d)
        plan.flush(dq_ref, scr[0])

    tensors = ([(qr, 0, "cur"), (kr, 0, "cur"), (z, 3072, "cur"), (dymix, 512, "cur"), (out, 0, "cur"), (lse, 0, "cur")]
               + ([(kr, 0, "prev"), (z, 3072, "prev")] if has_prev else []))
    return pl.pallas_call(
        body, name=name, grid=plan.grid,
        in_specs=[sp for _, c, role in tensors for sp in plan.specs(c, role)],
        out_specs=plan.out_spec(), out_shape=jax.ShapeDtypeStruct((T, 512), F32),
        scratch_shapes=plan.scratch(1), compiler_params=_cparams("parallel", "parallel", "parallel"),
    )(*[a for arr, _, _ in tensors for a in plan.operands(arr)])


def _dil_bwd_kv(qr, kr, z, dymix, out, lse, dil, B, L, *, name):
    T = B * L
    plan = _DilPlan(dil, B, L)
    hr, has_next = plan.hr, plan.has_other
    scale = A_HEAD ** -0.5
    n_t = 10 if has_next else 6

    def body(*refs):
        lists = [refs[i * hr:(i + 1) * hr] for i in range(n_t)]
        dk_ref, dv_ref = refs[n_t * hr], refs[n_t * hr + 1]
        scr = refs[n_t * hr + 2:] or (None, None)
        k_r, v_r = lists[0], lists[1]
        own = lists[2:6]
        mp0, mc = _band_masks()

        for group in plan.groups:
            kh = [_ld(plan.cur(k_r, i)).astype(BF16) for i in group]
            vh = [_ld(plan.cur(v_r, i)).astype(BF16) for i in group]
            dk, dv = [None] * len(group), [None] * len(group)
            for role in ("own", "next") if has_next else ("own",):
                if role == "own":
                    get = lambda t, i: _ld(plan.cur(own[t], i))
                    masks = [mc] * len(group)
                else:
                    get = lambda t, i: _ld(plan.other(own[t], lists[6 + t], i, "next"))
                    masks = [mp0 & plan.other_valid(i, "next") for i in group]
                qs = [get(0, i).astype(BF16) for i in group]
                dos = [get(1, i) for i in group]
                delta = [jnp.sum(d * get(2, i), axis=-1, keepdims=True) for d, i in zip(dos, group)]
                dob = [d.astype(BF16) for d in dos]
                p = [jnp.where(mk, jnp.exp(_dot(q, k, _NT) * scale - get(3, i)), 0.0) for q, k, i, mk in zip(qs, kh, group, masks)]
                dvn = [_dot(a.astype(BF16), d, _TN) for a, d in zip(p, dob)]
                ds = [a * (_dot(d, v, _NT) - dl) * scale for a, d, v, dl in zip(p, dob, vh, delta)]
                dkn = [_dot(d.astype(BF16), q, _TN) for d, q in zip(ds, qs)]
                dv = [n if o is None else o + n for o, n in zip(dv, dvn)]
                dk = [n if o is None else o + n for o, n in zip(dk, dkn)]
            for i, a, b in zip(group, dk, dv):
                plan.store(dk_ref, scr[0], i, a)
                plan.store(dv_ref, scr[1], i, b)
        plan.flush(dk_ref, scr[0])
        plan.flush(dv_ref, scr[1])

    queries = [(qr, 0), (dymix, 512), (out, 0), (lse, 0)]
    tensors = ([(kr, 0, "cur"), (z, 3072, "cur")] + [(a, c, "cur") for a, c in queries]
               + ([(a, c, "next") for a, c in queries] if has_next else []))
    return pl.pallas_call(
        body, name=name, grid=plan.grid,
        in_specs=[sp for _, c, role in tensors for sp in plan.specs(c, role)],
        out_specs=[plan.out_spec()] * 2, out_shape=[jax.ShapeDtypeStruct((T, 512), F32)] * 2,
        scratch_shapes=plan.scratch(2), compiler_params=_cparams("parallel", "parallel", "parallel"),
    )(*[a for arr, _, _ in tensors for a in plan.operands(arr)])


def _s5_build(lam_re, lam_im, log_dt, b_re, b_im, c_re, c_im):
    G, P, TC = C_GROUPS, C_STATE, C_TC
    lr = jnp.minimum(lam_re, C_MIN_NEG_RE)
    li = lam_im
    dt = jnp.exp(log_dt)[:, None]
    mag = jnp.exp(dt * lr)
    ar, ai = mag * jnp.cos(dt * li), mag * jnp.sin(dt * li)
    den = lr * lr + li * li
    zr = ((ar - 1.0) * lr + ai * li) / den
    zi = (ai * lr - (ar - 1.0) * li) / den
    bbr = zr[..., None] * b_re - zi[..., None] * b_im
    bbi = zr[..., None] * b_im + zi[..., None] * b_re
    ks = jnp.arange(TC + 1, dtype=F32)[:, None, None]
    pmag = jnp.exp(ks * (dt * lr)[None])
    pr, pi = pmag * jnp.cos(ks * (dt * li)[None]), pmag * jnp.sin(ks * (dt * li)[None])
    car = c_re[None] * pr[:, :, None, :] - c_im[None] * pi[:, :, None, :]
    cai = c_re[None] * pi[:, :, None, :] + c_im[None] * pr[:, :, None, :]
    kern = (jnp.einsum('lgop,gpc->lgco', car[:TC], bbr, precision=HI)
            - jnp.einsum('lgop,gpc->lgco', cai[:TC], bbi, precision=HI))
    pr_e, pi_e = pr[TC - 1 - jnp.arange(TC)], pi[TC - 1 - jnp.arange(TC)]
    er = pr_e[:, :, :, None] * bbr[None] - pi_e[:, :, :, None] * bbi[None]
    ei = pr_e[:, :, :, None] * bbi[None] + pi_e[:, :, :, None] * bbr[None]
    ez = jnp.stack([er, ei], axis=2).reshape(TC, C_NB, C_GB, 2, P, C_GROUP).transpose(1, 0, 2, 5, 3, 4)
    fz = jnp.stack([car[1:], -cai[1:]], axis=0).reshape(2, TC, C_NB, C_GB, C_GROUP, P).transpose(2, 0, 3, 5, 1, 4)
    return kern, ez, fz, pr[TC], pi[TC]


def _s5_lag_blocks(kern):
    eye = jnp.eye(C_GB, dtype=kern.dtype)
    return (kern.reshape(C_TC, C_NB, C_GB, C_GROUP, C_GROUP)[:, :, :, :, None, :]
            * eye[None, None, :, None, :, None]).reshape(C_TC, C_NB, LANES, LANES)


def _s5_state_maps(ez, fz):
    eye = jnp.eye(C_GB, dtype=ez.dtype)
    e8 = (ez[:, :, :, :, :, None, :] * eye[None, None, :, None, None, :, None]).reshape(C_NB, C_W8, C_S8)
    f8 = (fz[:, :, :, :, :, None, :] * eye[None, None, :, None, None, :, None]).reshape(C_NB, C_S8, C_W8)
    return e8, f8


def _s5_dense(kern, ez, fz):
    TC = C_TC
    kbd = _s5_lag_blocks(kern)
    lag = jnp.arange(TC)[None, :] - jnp.arange(TC)[:, None]
    ksel = jnp.where((lag >= 0)[:, :, None, None, None], kbd[jnp.clip(lag, 0, TC - 1)], 0)
    m8 = ksel.transpose(2, 0, 3, 1, 4).reshape(C_NB, C_W8, C_W8)
    return (m8,) + _s5_state_maps(ez, fz)


C_NB = C_GROUPS * C_GROUP // LANES
C_GB = C_GROUPS // C_NB
C_W8 = C_TC * LANES
C_S8 = 2 * C_GB * C_STATE


def _s5_scan_tables(lam_re, lam_im, log_dt, nsteps):
    lr = jnp.minimum(lam_re, C_MIN_NEG_RE)
    dt = jnp.exp(log_dt)[:, None]
    ks = (C_TC * 2.0 ** jnp.arange(8, dtype=F32))[None, :, None]
    keep = (jnp.arange(8) < nsteps)[None, :, None]
    pmag = jnp.exp(ks * (dt * lr)[:, None, :])
    ang = ks * (dt * lam_im)[:, None, :]

    def blocks(t):
        return t.reshape(C_NB, C_GB, 8, C_STATE).transpose(0, 2, 1, 3).reshape(C_NB, 8, C_GB * C_STATE)

    pr = blocks(jnp.where(keep, pmag * jnp.cos(ang), 0.0))
    pi = blocks(jnp.where(keep, pmag * jnp.sin(ang), 0.0))
    return jnp.concatenate([pr, pr], axis=-1), jnp.concatenate([-pi, pi], axis=-1)


def _s5_rows(t, R):
    return pl.ds(t, R, stride=C_TC)


def _s5_fwd(u, dsk, m8, e8, f8, tab_r, tab_i, B, L, *, name):
    T = B * L
    R = L // C_TC
    nsteps = int(math.log2(R))

    def body(u_ref, d_ref, m_ref, e_ref, f_ref, tr_ref, ti_ref, gl_ref, y_ref, x8_ref, xs_ref):
        for t in range(C_TC):
            x8_ref[0, :, t * LANES:(t + 1) * LANES] = u_ref[_s5_rows(t, R), :].astype(BF16)
        x8 = x8_ref[0]
        x = _dot(x8, e_ref[0])
        row = lax.broadcasted_iota(jnp.int32, (R, C_S8), 0)
        for k in range(nsteps):
            s = 1 << k
            sh = pltpu.roll(x, s, 0)
            upd = tr_ref[0, k:k + 1, :] * sh + ti_ref[0, k:k + 1, :] * pltpu.roll(sh, C_S8 // 2, 1)
            x = x + jnp.where(row >= s, upd, 0.0)
        xs = jnp.where(row >= 1, pltpu.roll(x, 1, 0), 0.0)
        xs_ref[0] = xs
        y8 = _dot(x8, m_ref[0]) + _dot(xs.astype(BF16), f_ref[0])
        d = d_ref[...]
        for t in range(C_TC):
            rows = _s5_rows(t, R)
            y = y8[:, t * LANES:(t + 1) * LANES] + d * u_ref[rows, :]
            y_ref[rows, :] = y
            gl_ref[rows, :] = 0.5 * y * (1.0 + _erf(y * (2.0 ** -0.5)))

    tok = pl.BlockSpec((L, LANES), lambda c, b: (b, c))
    per_block = lambda shape: pl.BlockSpec((1,) + shape, lambda c, b: (c, 0, 0))
    per_step = lambda shape: pl.BlockSpec((1,) + shape, lambda c, b: (c * B + b, 0, 0))
    return pl.pallas_call(
        body, name=name, grid=(C_NB, B),
        in_specs=[tok, pl.BlockSpec((1, LANES), lambda c, b: (0, c)), per_block((C_W8, C_W8)), per_block((C_W8, C_S8)),
                  per_block((C_S8, C_W8)), per_block((8, C_S8)), per_block((8, C_S8))],
        out_specs=[tok, tok, per_step((R, C_W8)), per_step((R, C_S8))],
        out_shape=[jax.ShapeDtypeStruct((T, D_MODEL), F32), jax.ShapeDtypeStruct((T, D_MODEL), F32),
                   jax.ShapeDtypeStruct((C_NB * B, R, C_W8), BF16), jax.ShapeDtypeStruct((C_NB * B, R, C_S8), F32)],
        compiler_params=_cparams("parallel", "parallel"),
    )(u, dsk, m8, e8, f8, tab_r, tab_i)


def _s5_bwd(dgl, y, u, dsk, xs, m8, e8, f8, tab_r, tab_i, B, L, *, name):
    T = B * L
    R = L // C_TC
    nsteps = int(math.log2(R))

    def body(dgl_ref, y_ref, u_ref, d_ref, xs_ref, m_ref, e_ref, f_ref, tr_ref, ti_ref,
             du_ref, dy8_ref, de_ref, da_ref, dd_ref, dyf_ref):
        @pl.when(pl.program_id(1) == 0)
        def _():
            da_ref[...] = jnp.zeros_like(da_ref)
            dd_ref[...] = jnp.zeros_like(dd_ref)

        dd = jnp.zeros((1, LANES), F32)
        for t in range(C_TC):
            rows = _s5_rows(t, R)
            yv = y_ref[rows, :]
            cdf = 0.5 * (1.0 + _erf(yv * (2.0 ** -0.5)))
            pdf = jnp.exp(-0.5 * yv * yv) * (1.0 / math.sqrt(2.0 * math.pi))
            dy = dgl_ref[rows, :] * (cdf + yv * pdf)
            dd = dd + jnp.sum(dy * u_ref[rows, :], axis=0, keepdims=True)
            dyf_ref[:, t * LANES:(t + 1) * LANES] = dy
        dd_ref[...] += dd
        dy8 = dyf_ref[...].astype(BF16)
        dy8_ref[0] = dy8
        xs = xs_ref[0]
        gx = _dot(dy8, f_ref[0], _NT)
        row = lax.broadcasted_iota(jnp.int32, (R, C_S8), 0)
        for k in range(nsteps):
            s = 1 << k
            sh = pltpu.roll(gx, R - s, 0)
            upd = tr_ref[0, k:k + 1, :] * sh - ti_ref[0, k:k + 1, :] * pltpu.roll(sh, C_S8 // 2, 1)
            gx = gx + jnp.where(row + s < R, upd, 0.0)
        de_in = jnp.where(row + 1 < R, pltpu.roll(gx, R - 1, 0), 0.0)
        deb = de_in.astype(BF16)
        de_ref[0] = deb
        da_ref[0, 0:1, :] += jnp.sum(de_in * xs, axis=0, keepdims=True)
        da_ref[0, 1:2, :] += jnp.sum(de_in * pltpu.roll(xs, C_S8 // 2, 1), axis=0, keepdims=True)
        dx8 = _dot(dy8, m_ref[0], _NT) + _dot(deb, e_ref[0], _NT)
        d = d_ref[...]
        for t in range(C_TC):
            cols = slice(t * LANES, (t + 1) * LANES)
            du_ref[_s5_rows(t, R), :] = dx8[:, cols] + d * dyf_ref[:, cols]

    tok = pl.BlockSpec((L, LANES), lambda c, b: (b, c))
    vec = pl.BlockSpec((1, LANES), lambda c, b: (0, c))
    per_block = lambda shape: pl.BlockSpec((1,) + shape, lambda c, b: (c, 0, 0))
    per_step = lambda shape: pl.BlockSpec((1,) + shape, lambda c, b: (c * B + b, 0, 0))
    return pl.pallas_call(
        body, name=name, grid=(C_NB, B),
        in_specs=[tok, tok, tok, vec, per_step((R, C_S8)), per_block((C_W8, C_W8)),
                  per_block((C_W8, C_S8)), per_block((C_S8, C_W8)), per_block((8, C_S8)), per_block((8, C_S8))],
        out_specs=[tok, per_step((R, C_W8)), per_step((R, C_S8)), per_block((8, C_S8)), vec],
        out_shape=[jax.ShapeDtypeStruct((T, D_MODEL), F32), jax.ShapeDtypeStruct((C_NB * B, R, C_W8), BF16),
                   jax.ShapeDtypeStruct((C_NB * B, R, C_S8), BF16), jax.ShapeDtypeStruct((C_NB, 8, C_S8), F32),
                   jax.ShapeDtypeStruct((1, D_MODEL), F32)],
        scratch_shapes=[pltpu.VMEM((R, C_W8), F32)],
        compiler_params=_cparams("parallel", "arbitrary"),
    )(dgl, y, u, dsk, xs, m8, e8, f8, tab_r, tab_i)


def _bmm_tn(a, b, nb, fold, *, name):
    a = a.reshape(nb, -1, a.shape[-1])
    b = b.reshape(nb, -1, b.shape[-1])
    K, M, N = a.shape[1], a.shape[2], b.shape[2]
    half = C_GB * C_STATE

    def body(a_ref, b_ref, o_ref, p_ref):
        p_ref[...] = _dot(a_ref[0].astype(BF16), b_ref[0].astype(BF16), _TN)
        lane = lax.broadcasted_iota(jnp.int32, (1, LANES), 1)
        if fold == "lags":
            for lag in range(C_TC):
                blocks = [p_ref[s * LANES:(s + 1) * LANES, (s + lag) * LANES:(s + lag + 1) * LANES] for s in range(C_TC - lag)]
                o_ref[0, lag] = functools.reduce(lambda u, v: u + v, blocks)
        elif fold == "e":
            for g in range(C_GB):
                lo, hi = LANES * (g // 2), half + LANES * (g // 2)
                for s in range(C_TC):
                    rows = slice(s * LANES + g * C_GROUP, s * LANES + (g + 1) * C_GROUP)
                    re, im = p_ref[rows, lo:lo + LANES], p_ref[rows, hi:hi + LANES]
                    if g % 2 == 0:
                        im = pltpu.roll(im, C_STATE, 1)
                    else:
                        re = pltpu.roll(re, C_STATE, 1)
                    o_ref[0, rows, :] = jnp.where(lane < C_STATE, re, im)
        else:
            for z in range(2):
                for g in range(C_GB):
                    rows = slice(z * half + g * C_STATE, z * half + (g + 1) * C_STATE)
                    val = jnp.zeros((C_STATE, LANES), F32)
                    for t in range(C_TC):
                        blk = pltpu.roll(p_ref[rows, t * LANES:(t + 1) * LANES], ((t - g) * C_GROUP) % LANES, 1)
                        val = jnp.where((lane >= t * C_GROUP) & (lane < (t + 1) * C_GROUP), blk, val)
                    o_ref[0, rows, :] = val

    if fold == "lags":
        out_spec = pl.BlockSpec((1, C_TC, LANES, LANES), lambda c: (c, 0, 0, 0))
        out_shape = jax.ShapeDtypeStruct((nb, C_TC, LANES, LANES), F32)
    else:
        out_spec = pl.BlockSpec((1, M, LANES), lambda c: (c, 0, 0))
        out_shape = jax.ShapeDtypeStruct((nb, M, LANES), F32)
    return pl.pallas_call(
        body, name=name, grid=(nb,),
        in_specs=[pl.BlockSpec((1, K, M), lambda c: (c, 0, 0)), pl.BlockSpec((1, K, N), lambda c: (c, 0, 0))],
        out_specs=out_spec, out_shape=out_shape, scratch_shapes=[pltpu.VMEM((M, N), F32)],
        compiler_params=_cparams("parallel"),
    )(a, b)


def _loss_head(y, target, *, name):
    T, C = y.shape
    tm = 256

    def body(y_ref, t_ref, l_ref, d_ref):
        err = y_ref[...] - t_ref[...]
        d_ref[...] = err * (1.0 / C)
        sq = err * err
        part = jnp.zeros((8, LANES), F32)
        for r in range(0, tm, 8):
            for c in range(0, C, LANES):
                part = part + sq[r:r + 8, c:c + LANES]

        @pl.when(pl.program_id(0) == 0)
        def _():
            l_ref[...] = part

        @pl.when(pl.program_id(0) > 0)
        def _():
            l_ref[...] += part

    row = pl.BlockSpec((tm, C), lambda i: (i, 0))
    acc = pl.BlockSpec((8, LANES), lambda i: (0, 0))
    return pl.pallas_call(
        body, name=name, grid=(T // tm,), in_specs=[row, row], out_specs=[acc, row],
        out_shape=[jax.ShapeDtypeStruct((8, LANES), F32), jax.ShapeDtypeStruct((T, C), F32)],
        compiler_params=_cparams("arbitrary"),
    )(y, target)


def _adamw(w, g, m, v, *, name):
    shape = w.shape
    size = int(np.prod(shape))
    cols = LANES if (shape[-1] < LANES and size % LANES == 0) else shape[-1]
    rows = size // cols
    tm = _tile(rows, 256) if rows % 8 == 0 else rows
    w2, g2, m2, v2 = (t.reshape(rows, cols) for t in (w, g, m, v))

    def body(w_ref, g_ref, m_ref, v_ref, d_ref, nm_ref, nv_ref):
        gg = g_ref[...]
        nm = ADAM_B1 * m_ref[...] + (1.0 - ADAM_B1) * gg
        nv = ADAM_B2 * v_ref[...] + (1.0 - ADAM_B2) * (gg * gg)
        m_hat = nm / (1.0 - ADAM_B1 ** ADAM_STEP)
        v_hat = nv / (1.0 - ADAM_B2 ** ADAM_STEP)
        d_ref[...] = -ADAM_LR * (m_hat / (jnp.sqrt(v_hat) + ADAM_EPS) + ADAM_WD * w_ref[...])
        nm_ref[...] = nm
        nv_ref[...] = nv

    blk = pl.BlockSpec((tm, cols), lambda i: (i, 0))
    outs = pl.pallas_call(
        body, name=name, grid=(rows // tm,), in_specs=[blk] * 4, out_specs=[blk] * 3,
        out_shape=[jax.ShapeDtypeStruct((rows, cols), F32)] * 3, compiler_params=_cparams("parallel"),
    )(w2, g2, m2, v2)
    return tuple(o.reshape(shape) for o in outs)


def _all_gather(shard, *, name):
    R, C = shard.shape

    def body(x_ref, out_ref, send_sems, recv_sems, local_sem):
        x, y, c = lax.axis_index("x"), lax.axis_index("y"), lax.axis_index("c")
        me, sibling = (x, y, c), (x, y, 1 - c)
        chips = [(1 - x, y), (x, 1 - y), (1 - x, 1 - y)]

        def rows(px, py, pc):
            return out_ref.at[_logical(px, py, pc)]

        def copy(k, block, to, src=None):
            return pltpu.make_async_remote_copy(
                src_ref=rows(*block) if src is None else src, dst_ref=rows(*block),
                send_sem=send_sems.at[k], recv_sem=recv_sems.at[k], device_id=to, device_id_type=_MESH)

        mine = pltpu.make_async_copy(x_ref, rows(*me), local_sem)
        mine.start()
        first = [copy(0, me, sibling, src=x_ref)]
        first += [copy(1 + j, me, (*chip, c), src=x_ref) for j, chip in enumerate(chips)]
        for cp in first:
            cp.start()
        passed = [copy(4 + j, (*chip, c), sibling) for j, chip in enumerate(chips)]
        for j, chip in enumerate(chips):
            copy(1 + j, (*chip, c), me).wait_recv()
            passed[j].start()
        copy(0, sibling, me).wait_recv()
        for j, chip in enumerate(chips):
            copy(4 + j, (*chip, 1 - c), me).wait_recv()
        for cp in first + passed:
            cp.wait_send()
        mine.wait()

    return pl.pallas_call(
        body, name=name, out_shape=jax.ShapeDtypeStruct((N_DEV, R, C), shard.dtype),
        in_specs=[_HBM], out_specs=_HBM,
        scratch_shapes=[pltpu.SemaphoreType.DMA((7,)), pltpu.SemaphoreType.DMA((7,)), pltpu.SemaphoreType.DMA],
    )(shard)


def _gather_weights(shards, *, name):
    nt = len(shards)

    def body(*refs):
        ins, outs = refs[:nt], refs[nt:2 * nt]
        send_sems, recv_sems, local_sems = refs[2 * nt:]
        x, y, c = lax.axis_index("x"), lax.axis_index("y"), lax.axis_index("c")
        me, sibling = (x, y, c), (x, y, 1 - c)
        chips = [(1 - x, y), (x, 1 - y), (1 - x, 1 - y)]

        def copy(t, k, block, to, src=None):
            rows = outs[t].at[_logical(*block)]
            return pltpu.make_async_remote_copy(
                src_ref=rows if src is None else src, dst_ref=rows,
                send_sem=send_sems.at[t, k], recv_sem=recv_sems.at[t, k], device_id=to, device_id_type=_MESH)

        mine = [pltpu.make_async_copy(ins[t], outs[t].at[_logical(*me)], local_sems.at[t]) for t in range(nt)]
        for cp in mine:
            cp.start()
        started = []
        for t in range(nt):
            started.append(copy(t, 0, me, sibling, src=ins[t]))
            started += [copy(t, 1 + j, me, (*chip, c), src=ins[t]) for j, chip in enumerate(chips)]
        for cp in started:
            cp.start()
        for j, chip in enumerate(chips):
            for t in range(nt):
                copy(t, 1 + j, (*chip, c), me).wait_recv()
                fwd = copy(t, 4 + j, (*chip, c), sibling)
                fwd.start()
                started.append(fwd)
        for t in range(nt):
            copy(t, 0, sibling, me).wait_recv()
            for j, chip in enumerate(chips):
                copy(t, 4 + j, (*chip, 1 - c), me).wait_recv()
        for cp in started:
            cp.wait_send()
        for cp in mine:
            cp.wait()

    return pl.pallas_call(
        body, name=name, out_shape=[jax.ShapeDtypeStruct((N_DEV,) + s.shape, s.dtype) for s in shards],
        in_specs=[_HBM] * nt, out_specs=[_HBM] * nt,
        scratch_shapes=[pltpu.SemaphoreType.DMA((nt, 7)), pltpu.SemaphoreType.DMA((nt, 7)), pltpu.SemaphoreType.DMA((nt,))],
    )(*shards)


def _sum_rows(stacked, *, name):
    _, R, C = stacked.shape
    tr = R
    if N_DEV * R * C * stacked.dtype.itemsize > 12 * 1024 * 1024:
        for cand in range(512, 15, -16):
            if R % cand == 0:
                tr = cand
                break

    def body(s_ref, o_ref):
        acc = s_ref[0].astype(F32)
        for k in range(1, N_DEV):
            acc = acc + s_ref[k].astype(F32)
        o_ref[...] = acc

    return pl.pallas_call(
        body, name=name, grid=(R // tr,),
        in_specs=[pl.BlockSpec((N_DEV, tr, C), lambda i: (0, i, 0))], out_specs=pl.BlockSpec((tr, C), lambda i: (i, 0)),
        out_shape=jax.ShapeDtypeStruct((R, C), F32), compiler_params=_cparams("parallel"),
    )(stacked)


_LARGE = (("ab_w_in", 1, True), ("ab_w_out", 1, False), ("s5_w_glu", 1, True), ("xattn_wq", 2, False),
          ("xattn_wkv", 2, True), ("xattn_wo", 2, False), ("ffn_w_in", 2, True), ("ffn_w_out", 2, False))
_LARGE_KEYS = tuple((n, l) for n, layers, _ in _LARGE for l in range(layers))
_TRANSPOSED = {n: t for n, _, t in _LARGE}


def _owner_major(name, w):
    return w.T if _TRANSPOSED[name] else w


def _lb_from_logits(logits):
    return jnp.cumsum(jax.nn.softmax(logits, axis=0), axis=0)[0:1]


def _local_step(x, mem, target, W, shards=None):
    B, L, _ = x.shape
    T = B * L
    x0 = x.reshape(T, D_MODEL)
    memf = mem.reshape(B * MEM_LEN, D_MODEL)
    nw = W["norm_w"]
    cos2, sin2 = _rope_tables(L)
    W = dict(W)
    G, received = {}, {}

    def mmx(a, b, gather=(), scatter=(), **kw):
        if shards is None or not (gather or scatter):
            return _mm(a, b, **kw)
        out, gathered, got = _mm(a, b, gather=[shards[k] for k in gather],
                                 scatter=[G[k].reshape(N_DEV, -1, D_MODEL) for k in scatter], **kw)
        for k, g in zip(gather, gathered):
            W[k] = g.reshape(-1, D_MODEL)
        for k, r in zip(scatter, got):
            received[k] = r
        return out

    def vec(v):
        return v.reshape(1, -1)

    saved = []
    xin = x0
    for layer in range(2):
        s = {"x0": xin}
        tag = f"l{layer}"
        if layer == 0:
            h1 = _rms_fwd(xin, vec(nw[0, 0]), name="norm_pre_mix_l0", out_dtype=BF16)
        s["h1"] = h1
        if layer == 0:
            lb, lb_vjp = jax.vjp(_lb_from_logits, W["hgrn_lb_logits"])
            onw = W["hgrn_out_norm_w"].reshape(1, A_WIDTH)
            z = mmx(h1, W["ab_w_in", 0], tb=True, name="ab_in",
                    gather=[("ab_w_out", 0), ("xattn_wq", 0), ("xattn_wkv", 0), ("xattn_wo", 0), ("ffn_w_in", 0)])
            oa, o_raw, s_start = _hgrn_fwd(z, lb, onw, B, L, name="hgrn_fwd")
            qr, kr = _rope_fwd(z, cos2, sin2, B, L, name="rope_qk")
            os_, ls_ = [], []
            for dil in B_DILS:
                o_g, l_g = _dil_fwd(qr, kr, z, dil, B, L, name=f"dil_fwd_{dil}")
                os_.append(o_g)
                ls_.append(l_g)
            ob, lse = _dil_combine(os_, ls_, name="dil_combine")
            ymix = jnp.concatenate([oa, ob], axis=-1).astype(BF16)
            y1 = mmx(ymix, W["ab_w_out", 0], out_dtype=BF16, name="ab_out", gather=[("ffn_w_out", 0)])
            s.update(z=z, lb=lb, lb_vjp=lb_vjp, onw=onw, o_raw=o_raw, s_start=s_start, qr=qr, kr=kr, ob=ob, lse=lse, ymix=ymix)
        else:
            p5 = tuple(W[n][0] for n in ("s5_lambda_re", "s5_lambda_im", "s5_log_dt", "s5_b_re", "s5_b_im", "s5_c_re", "s5_c_im"))
            (kern, ez, fz, _, _), s5_vjp = jax.vjp(_s5_build, *p5)
            tab_r, tab_i = _s5_scan_tables(p5[0], p5[1], p5[2], int(math.log2(L // C_TC)))
            mats = _s5_dense(kern.astype(BF16), ez.astype(BF16), fz.astype(BF16)) + (tab_r, tab_i)
            _, lag_vjp = jax.vjp(_s5_lag_blocks, kern)
            dsk = W["s5_d"].reshape(1, D_MODEL)
            gl, ypre, x8, xs = _s5_fwd(h1, dsk, *mats, B, L, name="s5_fwd")
            w_glu = _interleave_rows(W["s5_w_glu", 0])
            zg, y1 = _mm(gl, w_glu, tb=True, name="s5_glu_in", gate=("fwd", "glu"))
            s.update(s5_vjp=s5_vjp, lag_vjp=lag_vjp, mats=mats, x8=x8, xs=xs, dsk=dsk, gl=gl, ypre=ypre, zg=zg, w_glu=w_glu)
        x1, h2 = _rms_fwd(y1, vec(nw[layer, 1]), xin, name=f"norm_post_mix_{tag}", then=(vec(nw[layer, 2]), BF16))
        memn = _rms_fwd(memf, vec(W["mem_norm_w"][layer]), name=f"norm_mem_{tag}", out_dtype=BF16)
        q = mmx(h2, W["xattn_wq", layer], out_dtype=BF16, name=f"x_q_{tag}", gather=[("s5_w_glu", 0)] if layer == 0 else [])
        kv = _mm(memn, W["xattn_wkv", layer], tb=True, out_dtype=BF16, name=f"x_kv_{tag}")
        o = _xattn_fwd(q, kv, B, L, name=f"x_attn_{tag}")
        y2 = mmx(o, W["xattn_wo", layer], out_dtype=BF16, name=f"x_o_{tag}", gather=[("xattn_wq", 1), ("xattn_wo", 1)] if layer == 0 else [])
        x2, h3 = _rms_fwd(y2, vec(nw[layer, 3]), x1, name=f"norm_post_x_{tag}", then=(vec(nw[layer, 4]), BF16))
        w_ffn_in = _interleave_rows(W["ffn_w_in", layer])
        zf, u = mmx(h3, w_ffn_in, tb=True, name=f"ffn_in_{tag}", gate=("fwd", "swiglu"),
                    gather=[("xattn_wkv", 1), ("ffn_w_in", 1), ("ffn_w_out", 1)] if layer == 0 else [])
        y3 = _mm(u, W["ffn_w_out", layer], out_dtype=BF16, name=f"ffn_out_{tag}")
        if layer == 0:
            x3, h1 = _rms_fwd(y3, vec(nw[0, 5]), x2, name="norm_post_ffn_l0", then=(vec(nw[1, 0]), F32))
        else:
            x3 = _rms_fwd(y3, vec(nw[layer, 5]), x2, name=f"norm_post_ffn_{tag}")
        s.update(y1=y1, x1=x1, h2=h2, memn=memn, q=q, kv=kv, o=o, y2=y2, x2=x2, h3=h3, zf=zf, u=u, y3=y3, w_ffn_in=w_ffn_in)
        saved.append(s)
        xin = x3

    loss_parts, dx = _loss_head(xin, target.reshape(T, D_MODEL), name="loss_head")

    d_norm = [[None] * 6 for _ in range(2)]
    d_memn = [None, None]
    for layer in (1, 0):
        s = saved[layer]
        tag = f"l{layer}"
        if layer == 1:
            dy3, d_norm[1][5] = _rms_bwd(s["y3"], vec(nw[1, 5]), dx, name="bnorm_post_ffn_l1", out_dtype=BF16)
        dzf = mmx(dy3, W["ffn_w_out", layer], tb=True, name=f"b_ffn_out_dx_{tag}", gate=("bwd", "swiglu", s["zf"]),
                  scatter=[("xattn_wkv", 1), ("s5_w_glu", 0)] if layer == 0 else [])
        G["ffn_w_out", layer] = _mm(s["u"], dy3, ta=True, out_dtype=BF16, name=f"b_ffn_out_dw_{tag}")
        G["ffn_w_in", layer] = _interleave_rows(
            mmx(dzf, s["h3"], ta=True, out_dtype=BF16, name=f"b_ffn_in_dw_{tag}", scatter=[("ffn_w_out", layer)]), inverse=True)
        dh3 = mmx(dzf, s["w_ffn_in"], out_dtype=BF16, name=f"b_ffn_in_dx_{tag}", scatter=[("ffn_w_in", layer)])
        dx, d_norm[layer][4], dy2, d_norm[layer][3] = _rms_bwd(
            s["x2"], vec(nw[layer, 4]), dh3, dx, name=f"bnorm_pre_ffn_{tag}", then=(s["y2"], vec(nw[layer, 3])))
        do = _mm(dy2, W["xattn_wo", layer], tb=True, out_dtype=BF16, name=f"b_x_o_dx_{tag}")
        G["xattn_wo", layer] = _mm(s["o"], dy2, ta=True, out_dtype=BF16, name=f"b_x_o_dw_{tag}")
        dq, dkv = _xattn_bwd(s["q"], s["kv"], do, B, L, name=f"b_x_attn_{tag}")
        G["xattn_wq", layer] = _mm(s["h2"], dq, ta=True, out_dtype=BF16, name=f"b_x_q_dw_{tag}")
        dh2 = _mm(dq, W["xattn_wq", layer], tb=True, out_dtype=BF16, name=f"b_x_q_dx_{tag}")
        G["xattn_wkv", layer] = _mm(dkv, s["memn"], ta=True, out_dtype=BF16, name=f"b_x_kv_dw_{tag}")
        dmemn = _mm(dkv, W["xattn_wkv", layer], out_dtype=BF16, name=f"b_x_kv_dx_{tag}")
        _, d_memn[layer] = _rms_bwd(memf, vec(W["mem_norm_w"][layer]), dmemn, name=f"bnorm_mem_{tag}", out_dtype=BF16)
        dx, d_norm[layer][2], dy1, d_norm[layer][1] = _rms_bwd(
            s["x1"], vec(nw[layer, 2]), dh2, dx, name=f"bnorm_pre_x_{tag}", then=(s["y1"], vec(nw[layer, 1])))
        if layer == 0:
            z = s["z"]
            dymix = _mm(dy1, W["ab_w_out", 0], tb=True, name="b_ab_out_dx")
            G["ab_w_out", 0] = _mm(s["ymix"], dy1, ta=True, out_dtype=BF16, name="b_ab_out_dw")
            dqa, dfa, dia, dga, d_onw, d_lb = _hgrn_bwd(z, s["lb"], s["onw"], s["o_raw"], s["s_start"], dymix, B, L, name="hgrn_bwd")
            dqs, dks, dvs = [], [], []
            for dil in B_DILS:
                dqs.append(_dil_bwd_q(s["qr"], s["kr"], z, dymix, s["ob"], s["lse"], dil, B, L, name=f"dil_bwd_q_{dil}"))
                dk_g, dv_g = _dil_bwd_kv(s["qr"], s["kr"], z, dymix, s["ob"], s["lse"], dil, B, L, name=f"dil_bwd_kv_{dil}")
                dks.append(dk_g)
                dvs.append(dv_g)
            dqkv = _rope_bwd(dqs, dks, dvs, cos2, sin2, B, L, name="b_rope")
            dz = jnp.concatenate([dqa, dfa, dia, dga, dqkv], axis=-1).astype(BF16)
            G["ab_w_in", 0] = mmx(dz, s["h1"], ta=True, out_dtype=BF16, name="b_ab_in_dw",
                                  scatter=[("xattn_wo", 0), ("xattn_wq", 0), ("xattn_wkv", 0), ("ab_w_out", 0)])
            dh1 = mmx(dz, W["ab_w_in", 0], out_dtype=BF16, name="b_ab_in_dx", scatter=[("ab_w_in", 0)])
            G["hgrn_out_norm_w"] = jnp.sum(d_onw.reshape(B, A_WIDTH), axis=0, keepdims=True)
            d_lb_row = jnp.sum(d_lb.reshape(B, A_WIDTH), axis=0, keepdims=True)
            G["hgrn_lb_logits"] = s["lb_vjp"](d_lb_row)[0]
        else:
            dzg = _gated_bwd(s["zg"], dy1, "glu", name="b_s5_glu")
            G["s5_w_glu", 0] = _interleave_rows(
                _mm(dzg, s["gl"], ta=True, out_dtype=BF16, name="b_s5_glu_dw"), inverse=True)
            dgl = mmx(dzg, s["w_glu"], name="b_s5_glu_dx", scatter=[("xattn_wo", 1), ("xattn_wq", 1)])
            dh1, dy8, de_in, da, d_dsk = _s5_bwd(dgl, s["ypre"], s["h1"], s["dsk"], s["xs"], *s["mats"], B, L, name="s5_bwd")
            dkbd = _bmm_tn(s["x8"], dy8, C_NB, "lags", name="s5_bwd_dm").transpose(1, 0, 2, 3)
            dfz = _bmm_tn(s["xs"], dy8, C_NB, "f", name="s5_bwd_df").reshape(C_NB, 2, C_GB, C_STATE, C_TC, C_GROUP)
            dez = _bmm_tn(s["x8"], de_in, C_NB, "e", name="s5_bwd_de").reshape(C_NB, C_TC, C_GB, C_GROUP, 2, C_STATE)
            half = C_S8 // 2
            da_r = (da[:, 0, :half] + da[:, 0, half:]).reshape(C_GROUPS, C_STATE)
            da_i = (da[:, 1, half:] - da[:, 1, :half]).reshape(C_GROUPS, C_STATE)
            gp = s["s5_vjp"](s["lag_vjp"](dkbd) + (dez, dfz, da_r, da_i))
            for n, gv in zip(("s5_lambda_re", "s5_lambda_im", "s5_log_dt", "s5_b_re", "s5_b_im", "s5_c_re", "s5_c_im"), gp):
                G[n] = gv[None]
            G["s5_d"] = d_dsk
        if layer == 1:
            dx, d_norm[1][0], dy3, d_norm[0][5] = _rms_bwd(
                s["x0"], vec(nw[1, 0]), dh1, dx, name="bnorm_pre_mix_l1", then=(saved[0]["y3"], vec(nw[0, 5])))
        else:
            dx, d_norm[0][0] = _rms_bwd(s["x0"], vec(nw[0, 0]), dh1, dx, name="bnorm_pre_mix_l0")

    G["norm_w"] = jnp.stack([jnp.concatenate(d_norm[l], axis=0) for l in range(2)])
    G["mem_norm_w"] = jnp.concatenate(d_memn, axis=0)
    if shards is not None:
        G.update(received)
    return loss_parts, dx.reshape(B, L, D_MODEL), G


_SMALL = (("norm_w", (2, 6, 1024)), ("mem_norm_w", (2, 1024)), ("hgrn_lb_logits", (3, 512)), ("hgrn_out_norm_w", (1, 512)),
          ("s5_lambda_re", (1, 64, 64)), ("s5_lambda_im", (1, 64, 64)), ("s5_log_dt", (1, 64)),
          ("s5_b_re", (1, 64, 64, 16)), ("s5_b_im", (1, 64, 64, 16)), ("s5_c_re", (1, 64, 16, 64)),
          ("s5_c_im", (1, 64, 16, 64)), ("s5_d", (1, 1024)))

_WEIGHT_ORDER = ('norm_w', 'mem_norm_w', 'ab_w_in', 'ab_w_out', 'hgrn_lb_logits', 'hgrn_out_norm_w', 's5_lambda_re',
                 's5_lambda_im', 's5_log_dt', 's5_b_re', 's5_b_im', 's5_c_re', 's5_c_im', 's5_d', 's5_w_glu', 'xattn_wq',
                 'xattn_wkv', 'xattn_wo', 'ffn_w_in', 'ffn_w_out')


def kernel(x, mem, norm_w, mem_norm_w, ab_w_in, ab_w_out, hgrn_lb_logits, hgrn_out_norm_w, s5_lambda_re, s5_lambda_im, s5_log_dt, s5_b_re, s5_b_im, s5_c_re, s5_c_im, s5_d, s5_w_glu, xattn_wq, xattn_wkv, xattn_wo, ffn_w_in, ffn_w_out, loss_target, m_norm_w, m_mem_norm_w, m_ab_w_in, m_ab_w_out, m_hgrn_lb_logits, m_hgrn_out_norm_w, m_s5_lambda_re, m_s5_lambda_im, m_s5_log_dt, m_s5_b_re, m_s5_b_im, m_s5_c_re, m_s5_c_im, m_s5_d, m_s5_w_glu, m_xattn_wq, m_xattn_wkv, m_xattn_wo, m_ffn_w_in, m_ffn_w_out, v_norm_w, v_mem_norm_w, v_ab_w_in, v_ab_w_out, v_hgrn_lb_logits, v_hgrn_out_norm_w, v_s5_lambda_re, v_s5_lambda_im, v_s5_log_dt, v_s5_b_re, v_s5_b_im, v_s5_c_re, v_s5_c_im, v_s5_d, v_s5_w_glu, v_xattn_wq, v_xattn_wkv, v_xattn_wo, v_ffn_w_in, v_ffn_w_out):
    local = dict(norm_w=norm_w, mem_norm_w=mem_norm_w, ab_w_in=ab_w_in, ab_w_out=ab_w_out, hgrn_lb_logits=hgrn_lb_logits,
                 hgrn_out_norm_w=hgrn_out_norm_w, s5_lambda_re=s5_lambda_re, s5_lambda_im=s5_lambda_im, s5_log_dt=s5_log_dt,
                 s5_b_re=s5_b_re, s5_b_im=s5_b_im, s5_c_re=s5_c_re, s5_c_im=s5_c_im, s5_d=s5_d, s5_w_glu=s5_w_glu,
                 xattn_wq=xattn_wq, xattn_wkv=xattn_wkv, xattn_wo=xattn_wo, ffn_w_in=ffn_w_in, ffn_w_out=ffn_w_out)
    mom_m = dict(zip(_WEIGHT_ORDER, (m_norm_w, m_mem_norm_w, m_ab_w_in, m_ab_w_out, m_hgrn_lb_logits, m_hgrn_out_norm_w, m_s5_lambda_re, m_s5_lambda_im, m_s5_log_dt, m_s5_b_re, m_s5_b_im, m_s5_c_re, m_s5_c_im, m_s5_d, m_s5_w_glu, m_xattn_wq, m_xattn_wkv, m_xattn_wo, m_ffn_w_in, m_ffn_w_out)))
    mom_v = dict(zip(_WEIGHT_ORDER, (v_norm_w, v_mem_norm_w, v_ab_w_in, v_ab_w_out, v_hgrn_lb_logits, v_hgrn_out_norm_w, v_s5_lambda_re, v_s5_lambda_im, v_s5_log_dt, v_s5_b_re, v_s5_b_im, v_s5_c_re, v_s5_c_im, v_s5_d, v_s5_w_glu, v_xattn_wq, v_xattn_wkv, v_xattn_wo, v_ffn_w_in, v_ffn_w_out)))
    dev = 4 * lax.axis_index("x") + 2 * lax.axis_index("y") + lax.axis_index("c")

    shards = {(n, l): _owner_major(n, local[n][l]).astype(BF16) for n, l in _LARGE_KEYS}
    first = ("ab_w_in", 0)
    W = {first: _gather_weights([shards[first]], name="gather_first")[0].reshape(-1, D_MODEL)}
    tiny =jnp.concatenate([norm_w.reshape(-1), s5_d.reshape(-1)])
    tiny = jnp.pad(tiny, (0, 16 * LANES - tiny.shape[0])).reshape(16, LANES)
    tiny_all = _all_gather(tiny, name="gather_tiny").reshape(N_DEV, 16 * LANES)
    W["norm_w"] = tiny_all[:, :12 * LANES].reshape(N_DEV, 2, 6, LANES).transpose(1, 2, 0, 3).reshape(2, 6, D_MODEL)
    W["s5_d"] = tiny_all[:, 12 * LANES:13 * LANES].reshape(1, D_MODEL)
    for n in ("mem_norm_w", "hgrn_lb_logits", "hgrn_out_norm_w", "s5_lambda_re", "s5_lambda_im", "s5_log_dt",
              "s5_b_re", "s5_b_im", "s5_c_re", "s5_c_im"):
        W[n] = local[n]

    loss_parts, grad_x, G = _local_step(x, mem, loss_target, W, shards)

    g_layers = {}
    for n, l in _LARGE_KEYS:
        g = _sum_rows(G[n, l], name=f"sum_grads_{n}_{l}")
        g_layers.setdefault(n, []).append(g.T if _TRANSPOSED[n] else g)
    g_local = {n: jnp.stack(gl) for n, gl in g_layers.items()}
    small =jnp.concatenate([G[n].reshape(-1) for n, _ in _SMALL] + [0.5 / D_MODEL * jnp.sum(loss_parts).reshape(1)])
    n_small = small.shape[0]
    small = jnp.pad(small, (0, (-n_small) % (8 * LANES))).reshape(-1, LANES)
    small_sum = _sum_rows(_all_gather(small, name="gather_small"), name="sum_small").reshape(-1)
    g_full, off = {}, 0
    for n, shp in _SMALL:
        size = int(np.prod(shp))
        g_full[n] = small_sum[off:off + size].reshape(shp)
        off += size
    loss = small_sum[off]
    grads = dict(g_local)
    for n, shp in _SMALL:
        if n == "norm_w":
            grads[n] = lax.dynamic_slice_in_dim(g_full[n], dev * LANES, LANES, axis=2)
        elif n == "s5_d":
            grads[n] = lax.dynamic_slice_in_dim(g_full[n], dev * LANES, LANES, axis=1)
        else:
            grads[n] = g_full[n]

    delta, new_m, new_v = {}, {}, {}
    for n in _WEIGHT_ORDER:
        delta[n], new_m[n], new_v[n] = _adamw(local[n], grads[n], mom_m[n], mom_v[n], name=f"adamw_{n}")
    return (loss, grad_x, *[grads[n] for n in _WEIGHT_ORDER], *[delta[n] for n in _WEIGHT_ORDER],
            *[new_m[n] for n in _WEIGHT_ORDER], *[new_v[n] for n in _WEIGHT_ORDER])
```

```python
import functools
import math

import numpy as np
import jax
import jax.numpy as jnp
from jax import lax
from jax.experimental import pallas as pl
from jax.experimental.pallas import tpu as pltpu

F32 = jnp.float32
BF16 = jnp.bfloat16
HI = lax.Precision.HIGHEST

D_MODEL = 1024
NORM_EPS = 1e-6
A_WIDTH = 512
A_HEAD = 128
A_CHUNK = 32
A_SUPER = 256
B_SPAN = 128
B_DILS = (1, 4, 16)
ROPE_THETA = 10000.0
C_GROUPS = 64
C_GROUP = 16
C_STATE = 64
C_TC = 8
C_MIN_NEG_RE = -1e-4
MEM_LEN = 256
X_HEADS = 4
X_HD = 256
D_FF = 2816
N_DEV = 8
LANES = 128

ADAM_LR, ADAM_B1, ADAM_B2, ADAM_EPS, ADAM_WD, ADAM_STEP = 0.001, 0.9, 0.999, 1e-08, 0.01, 10

NEG_BIG = -1e30


def _tile(n, pref):
    for d in range(min(pref, n) // LANES * LANES, 0, -LANES):
        if n % d == 0:
            return d
    return n


def _cparams(*sem):
    return pltpu.CompilerParams(dimension_semantics=sem, vmem_limit_bytes=56 * 1024 * 1024)


def _sigmoid(x):
    return 0.5 * jnp.tanh(0.5 * x) + 0.5


def _erf(x):
    ax = jnp.abs(x)
    t = 1.0 / (1.0 + 0.3275911 * ax)
    poly = t * (0.254829592 + t * (-0.284496736 + t * (1.421413741 + t * (-1.453152027 + t * 1.061405429))))
    y = 1.0 - poly * jnp.exp(-ax * ax)
    return jnp.where(x < 0, -y, y)


_HBM = pl.BlockSpec(memory_space=pltpu.HBM)
_MESH = pl.DeviceIdType.MESH


def _logical(px, py, pc):
    return 4 * px + 2 * py + pc


class _Exchange:
    def __init__(self, gather=(), scatter=()):
        self.gather, self.scatter = list(gather), list(scatter)
        self.ng, self.n = len(self.gather), len(self.gather) + len(self.scatter)

    def operands(self):
        return self.gather + self.scatter

    def in_specs(self):
        return [_HBM] * self.n

    def out_shapes(self):
        return ([jax.ShapeDtypeStruct((N_DEV,) + g.shape, g.dtype) for g in self.gather]
                + [jax.ShapeDtypeStruct(s.shape, s.dtype) for s in self.scatter])

    def scratch(self):
        if not self.n:
            return []
        return [pltpu.SemaphoreType.DMA((self.n, 7)), pltpu.SemaphoreType.DMA((self.n, 7)), pltpu.SemaphoreType.DMA((self.n,))]

    def split(self, results):
        return list(results[:self.ng]), list(results[self.ng:])

    def run(self, ins, outs, sems, first, last):
        if not self.n:
            return
        send_sems, recv_sems, local_sems = sems
        x, y, c = lax.axis_index("x"), lax.axis_index("y"), lax.axis_index("c")
        me, sibling = _logical(x, y, c), (x, y, 1 - c)
        chips = [(1 - x, y), (x, 1 - y), (1 - x, 1 - y)]
        peers = [(x ^ (k >> 2), y ^ ((k >> 1) & 1), c ^ (k & 1)) for k in range(1, N_DEV)]

        def remote(t, k, src, dst, to):
            return pltpu.make_async_remote_copy(src_ref=src, dst_ref=dst, send_sem=send_sems.at[t, k],
                                                recv_sem=recv_sems.at[t, k], device_id=to, device_id_type=_MESH)

        def local(t):
            src = ins[t] if t < self.ng else ins[t].at[me]
            return pltpu.make_async_copy(src, outs[t].at[me], local_sems.at[t])

        @pl.when(first)
        def _():
            for t in range(self.n):
                local(t).start()
                if t < self.ng:
                    remote(t, 0, ins[t], outs[t].at[me], sibling).start()
                    for j, chip in enumerate(chips):
                        remote(t, 1 + j, ins[t], outs[t].at[me], (*chip, c)).start()
                else:
                    for k, peer in enumerate(peers):
                        remote(t, k, ins[t].at[_logical(*peer)], outs[t].at[me], peer).start()

        @pl.when(last)
        def _():
            for j, chip in enumerate(chips):
                for t in range(self.ng):
                    landed = outs[t].at[_logical(*chip, c)]
                    remote(t, 1 + j, ins[t], landed, sibling).wait_recv()
                    remote(t, 4 + j, landed, landed, sibling).start()
            for t in range(self.n):
                if t < self.ng:
                    remote(t, 0, ins[t], outs[t].at[_logical(*sibling)], sibling).wait_recv()
                    for j, chip in enumerate(chips):
                        remote(t, 4 + j, ins[t], outs[t].at[_logical(*chip, 1 - c)], sibling).wait_recv()
                    for k in range(7):
                        remote(t, k, ins[t], outs[t].at[me], sibling).wait_send()
                else:
                    for k, peer in enumerate(peers):
                        remote(t, k, ins[t].at[me], outs[t].at[_logical(*peer)], peer).wait_recv()
                    for k, peer in enumerate(peers):
                        remote(t, k, ins[t].at[_logical(*peer)], outs[t].at[me], peer).wait_send()
                local(t).wait()


_MM_VMEM_BUDGET = 36 * 1024 * 1024


def _mm(a, b, *, ta=False, tb=False, out_dtype=F32, name, tiles=(1408, 1408, 4096), gather=(), scatter=(), gate=None):
    M, K = (a.shape[1], a.shape[0]) if ta else a.shape
    N = b.shape[0] if tb else b.shape[1]
    assert (b.shape[1] if tb else b.shape[0]) == K
    gate_mode = gate[0] if gate else None
    tm, tn, tk = _tile(M, tiles[0]), _tile(N, tiles[1]), _tile(K, tiles[2])
    if gate_mode == "fwd":
        bs = _gate_block(N // 2)
        tn = 2 * bs
    elif gate_mode == "bwd":
        bs = _gate_block(N)
        tn = bs

    def vmem_bytes():
        acc = 4 * tm * tn if tk < K else 0
        if gate_mode == "fwd":
            io = (2 * (2 + 1) + 4) * tm * tn
        elif gate_mode == "bwd":
            io = (2 * (4 + 4) + 4) * tm * tn
        else:
            io = 2 * tm * tn * jnp.dtype(out_dtype).itemsize
        return 2 * 2 * (tm * tk + tk * tn) + acc + io

    while vmem_bytes() > _MM_VMEM_BUDGET:
        if gate_mode and tm > 256:
            tm = _tile(M, tm - LANES)
        elif tk > 512:
            tk = _tile(K, tk - LANES)
        elif tn > 256 and not gate_mode:
            tn = _tile(N, tn - LANES)
        else:
            tm = _tile(M, tm - LANES)
    ni, nj, nk = M // tm, N // tn, K // tk
    ex = _Exchange(gather, scatter)

    a_spec = pl.BlockSpec((tk, tm), lambda i, j, k: (k, i)) if ta else pl.BlockSpec((tm, tk), lambda i, j, k: (i, k))
    b_spec = pl.BlockSpec((tn, tk), lambda i, j, k: (j, k)) if tb else pl.BlockSpec((tk, tn), lambda i, j, k: (k, j))
    dims = (((0 if ta else 1,), (1 if tb else 0,)), ((), ()))
    n_acc = 1 if nk > 1 else 0
    n_in = 3 if gate_mode == "bwd" else 2
    n_out = 2 if gate_mode == "fwd" else 1

    def finish(val, in_refs, out_refs, rows=slice(None)):
        if gate_mode is None:
            out_refs[0][rows, :] = val.astype(out_refs[0].dtype)
        elif gate_mode == "fwd":
            out_refs[0][rows, :] = val.astype(BF16)
            for p in range(tn // (2 * bs)):
                a_, b_ = val[:, 2 * p * bs:(2 * p + 1) * bs], val[:, (2 * p + 1) * bs:(2 * p + 2) * bs]
                out_refs[1][rows, p * bs:(p + 1) * bs] = _gate_value(a_, b_, gate[1]).astype(BF16)
        else:
            z_ref = in_refs[2]
            for p in range(tn // bs):
                a_ = z_ref[rows, 2 * p * bs:(2 * p + 1) * bs].astype(F32)
                b_ = z_ref[rows, (2 * p + 1) * bs:(2 * p + 2) * bs].astype(F32)
                da, db = _gate_grads(a_, b_, val[:, p * bs:(p + 1) * bs], gate[1])
                out_refs[0][rows, 2 * p * bs:(2 * p + 1) * bs] = da.astype(BF16)
                out_refs[0][rows, (2 * p + 1) * bs:(2 * p + 2) * bs] = db.astype(BF16)

    halves = 2 if (gate_mode and nk == 1 and not ta and tm % 32 == 0) else 1

    def body(*refs):
        in_refs, rest = refs[:n_in], refs[n_in:]
        ex_in, out_refs = rest[:ex.n], rest[ex.n:ex.n + n_out]
        ex_out, scratch = rest[ex.n + n_out:2 * ex.n + n_out], rest[2 * ex.n + n_out:]
        i, j, k = pl.program_id(0), pl.program_id(1), pl.program_id(2)
        ex.run(ex_in, ex_out, scratch[n_acc:], (i == 0) & (j == 0) & (k == 0), (i == ni - 1) & (j == nj - 1) & (k == nk - 1))
        if halves > 1:
            rh = tm // halves
            rhs = in_refs[1][...].astype(BF16)
            parts = [lax.dot_general(in_refs[0][r * rh:(r + 1) * rh, :].astype(BF16), rhs, dims, preferred_element_type=F32)
                     for r in range(halves)]
            for r, p in enumerate(parts):
                finish(p, in_refs, out_refs, slice(r * rh, (r + 1) * rh))
            return
        part = lax.dot_general(in_refs[0][...].astype(BF16), in_refs[1][...].astype(BF16), dims, preferred_element_type=F32)
        if nk == 1:
            finish(part, in_refs, out_refs)
            return
        acc_ref = scratch[0]

        @pl.when(k == 0)
        def _():
            acc_ref[...] = part

        @pl.when(k > 0)
        def _():
            acc_ref[...] += part

        @pl.when(k == nk - 1)
        def _():
            finish(acc_ref[...], in_refs, out_refs)

    tile = lambda width: pl.BlockSpec((tm, width), lambda i, j, k: (i, j))
    if gate_mode == "fwd":
        out_specs, out_shape = [tile(tn), tile(tn // 2)], [jax.ShapeDtypeStruct((M, N), BF16), jax.ShapeDtypeStruct((M, N // 2), BF16)]
    elif gate_mode == "bwd":
        out_specs, out_shape = [tile(2 * tn)], [jax.ShapeDtypeStruct((M, 2 * N), BF16)]
    else:
        out_specs, out_shape = [tile(tn)], [jax.ShapeDtypeStruct((M, N), out_dtype)]
    operands = [a, b] + ([gate[2]] if gate_mode == "bwd" else [])
    sem = ("arbitrary",) * 3 if ex.n else ("parallel", "parallel", "arbitrary")
    res = pl.pallas_call(
        body, name=name, grid=(ni, nj, nk),
        in_specs=[a_spec, b_spec] + ([tile(2 * tn)] if gate_mode == "bwd" else []) + ex.in_specs(),
        out_specs=out_specs + ex.in_specs(),
        out_shape=out_shape + ex.out_shapes(),
        scratch_shapes=([pltpu.VMEM((tm, tn), F32)] if nk > 1 else []) + ex.scratch(),
        compiler_params=_cparams(*sem),
    )(*operands, *ex.operands())
    main = res[0] if n_out == 1 else tuple(res[:n_out])
    if not ex.n:
        return main
    return (main,) + tuple(ex.split(res[n_out:]))


def _rms_fwd(x, w, res=None, *, name, out_dtype=F32, then=None):
    T, C = x.shape
    tm = _tile(T, 512)
    has_res = res is not None

    def norm(v, w_ref):
        return v * lax.rsqrt(jnp.mean(v * v, axis=-1, keepdims=True) + NORM_EPS) * w_ref[...]

    def body(*refs):
        x_ref, w_ref = refs[0], refs[1]
        y = norm(x_ref[...].astype(F32), w_ref)
        if has_res:
            y = y + refs[2][...]
        if then is None:
            refs[-1][...] = y.astype(refs[-1].dtype)
        else:
            refs[-2][...] = y.astype(refs[-2].dtype)
            refs[-1][...] = norm(y, refs[-3]).astype(refs[-1].dtype)

    row = pl.BlockSpec((tm, C), lambda i: (i, 0))
    vec = pl.BlockSpec((1, C), lambda i: (0, 0))
    ins = [x, w] + ([res] if has_res else []) + ([then[0]] if then else [])
    in_specs = [row, vec] + ([row] if has_res else []) + ([vec] if then else [])
    out_shape = [jax.ShapeDtypeStruct((T, C), out_dtype)] + ([jax.ShapeDtypeStruct((T, C), then[1])] if then else [])
    res_ = pl.pallas_call(
        body, name=name, grid=(T // tm,), in_specs=in_specs, out_specs=[row] * len(out_shape),
        out_shape=out_shape, compiler_params=_cparams("parallel"),
    )(*ins)
    return tuple(res_) if then else res_[0]


def _rms_bwd(x, w, dy, add=None, *, name, out_dtype=F32, then=None):
    T, C = x.shape
    tm = _tile(T, 512)
    has_add = add is not None
    n_in = 3 + has_add + (2 if then else 0)

    def grads(x_ref, w_ref, g, dw_ref):
        xv = x_ref[...].astype(F32)
        r = lax.rsqrt(jnp.mean(xv * xv, axis=-1, keepdims=True) + NORM_EPS)
        xh = xv * r
        part = jnp.sum(g * xh, axis=0, keepdims=True)

        @pl.when(pl.program_id(0) == 0)
        def _():
            dw_ref[...] = part

        @pl.when(pl.program_id(0) > 0)
        def _():
            dw_ref[...] += part

        gx = g * w_ref[...]
        return r * (gx - xh * jnp.mean(gx * xh, axis=-1, keepdims=True))

    def body(*refs):
        ins, outs = refs[:n_in], refs[n_in:]
        dx = grads(ins[0], ins[1], ins[2][...].astype(F32), outs[1])
        if has_add:
            dx = dx + ins[3][...]
        outs[0][...] = dx.astype(outs[0].dtype)
        if then:
            outs[2][...] = grads(ins[-2], ins[-1], dx, outs[3]).astype(BF16)

    row = pl.BlockSpec((tm, C), lambda i: (i, 0))
    vec = pl.BlockSpec((1, C), lambda i: (0, 0))
    ins = [x, w, dy] + ([add] if has_add else []) + (list(then) if then else [])
    big, small = jax.ShapeDtypeStruct((T, C), out_dtype), jax.ShapeDtypeStruct((1, C), F32)
    return pl.pallas_call(
        body, name=name, grid=(T // tm,),
        in_specs=[row, vec, row] + ([row] if has_add else []) + ([row, vec] if then else []),
        out_specs=[row, vec] + ([row, vec] if then else []),
        out_shape=[big, small] + ([jax.ShapeDtypeStruct((T, C), BF16), small] if then else []),
        compiler_params=_cparams("arbitrary"),
    )(*ins)


def _gate_block(width):
    return _tile(width, 1408)


def _gate_value(a, b, kind):
    return a * _sigmoid(a) * b if kind == "swiglu" else a * _sigmoid(b)


def _gate_grads(a, b, d, kind):
    if kind == "swiglu":
        s = _sigmoid(a)
        return d * b * (s * (1.0 + a * (1.0 - s))), d * a * s
    s = _sigmoid(b)
    return d * s, d * a * s * (1.0 - s)


def _interleave_rows(w, inverse=False):
    W2, C = w.shape
    bs = _gate_block(W2 // 2)
    nb = W2 // 2 // bs
    shape = (nb, 2, bs, C) if inverse else (2, nb, bs, C)
    return w.reshape(shape).transpose(1, 0, 2, 3).reshape(W2, C)


def _gated_bwd(z, dout, kind, *, name):
    T, W2 = z.shape
    W = W2 // 2
    tm, bs = _tile(T, 512), _gate_block(W)

    def body(z_ref, d_ref, o_ref):
        da, db = _gate_grads(z_ref[:, :bs].astype(F32), z_ref[:, bs:].astype(F32), d_ref[...].astype(F32), kind)
        o_ref[:, :bs] = da.astype(o_ref.dtype)
        o_ref[:, bs:] = db.astype(o_ref.dtype)

    return pl.pallas_call(
        body, name=name, grid=(T // tm, W // bs),
        in_specs=[pl.BlockSpec((tm, 2 * bs), lambda i, j: (i, j)), pl.BlockSpec((tm, bs), lambda i, j: (i, j))],
        out_specs=pl.BlockSpec((tm, 2 * bs), lambda i, j: (i, j)),
        out_shape=jax.ShapeDtypeStruct((T, W2), BF16), compiler_params=_cparams("parallel", "parallel"),
    )(z, dout)


_NT = (((1,), (1,)), ((), ()))
_TN = (((0,), (0,)), ((), ()))


def _dot(a, b, dims=None, precision=None):
    if dims is None:
        return jnp.dot(a, b, preferred_element_type=F32, precision=precision)
    return lax.dot_general(a, b, dims, preferred_element_type=F32, precision=precision)


def _xattn_fwd(q, kv, B, L, *, name):
    T = q.shape[0]
    tq = 256
    nq = L // tq
    scale = X_HD ** -0.5

    def body(q_ref, k_ref, v_ref, o_ref):
        heads = [slice(h * X_HD, (h + 1) * X_HD) for h in range(X_HEADS)]
        s = [_dot(q_ref[:, sl].astype(BF16), k_ref[:, sl].astype(BF16), _NT) * scale for sl in heads]
        m = [jnp.max(a, axis=-1, keepdims=True) for a in s]
        p = [jnp.exp(a - b) for a, b in zip(s, m)]
        l = [jnp.sum(a, axis=-1, keepdims=True) for a in p]
        o = [_dot(a.astype(BF16), v_ref[:, sl].astype(BF16)) for a, sl in zip(p, heads)]
        for sl, a, b in zip(heads, o, l):
            o_ref[:, sl] = (a / b).astype(BF16)

    return pl.pallas_call(
        body, name=name, grid=(B, nq),
        in_specs=[pl.BlockSpec((tq, D_MODEL), lambda b, i: (b * nq + i, 0)),
                  pl.BlockSpec((MEM_LEN, D_MODEL), lambda b, i: (b, 0)),
                  pl.BlockSpec((MEM_LEN, D_MODEL), lambda b, i: (b, 1))],
        out_specs=pl.BlockSpec((tq, D_MODEL), lambda b, i: (b * nq + i, 0)),
        out_shape=jax.ShapeDtypeStruct((T, D_MODEL), BF16), compiler_params=_cparams("parallel", "parallel"),
    )(q, kv, kv)


def _xattn_bwd(q, kv, do, B, L, *, name):
    T = q.shape[0]
    tq = 256
    nq = L // tq
    scale = X_HD ** -0.5

    def body(q_ref, k_ref, v_ref, do_ref, dq_ref, dkv_ref):
        @pl.when(pl.program_id(1) == 0)
        def _():
            dkv_ref[...] = jnp.zeros_like(dkv_ref)

        heads = [slice(h * X_HD, (h + 1) * X_HD) for h in range(X_HEADS)]
        qs = [q_ref[:, sl].astype(BF16) for sl in heads]
        ks = [k_ref[:, sl].astype(BF16) for sl in heads]
        dos = [do_ref[:, sl].astype(BF16) for sl in heads]
        s = [_dot(a, b, _NT) * scale for a, b in zip(qs, ks)]
        dp = [_dot(a, v_ref[:, sl].astype(BF16), _NT) for a, sl in zip(dos, heads)]
        e = [jnp.exp(a - jnp.max(a, axis=-1, keepdims=True)) for a in s]
        p = [a / jnp.sum(a, axis=-1, keepdims=True) for a in e]
        ds = [(a * (b - jnp.sum(b * a, axis=-1, keepdims=True)) * scale).astype(BF16) for a, b in zip(p, dp)]
        dv = [_dot(a.astype(BF16), b, _TN) for a, b in zip(p, dos)]
        dq = [_dot(a, b) for a, b in zip(ds, ks)]
        dk = [_dot(a, b, _TN) for a, b in zip(ds, qs)]
        for h, sl in enumerate(heads):
            dq_ref[:, sl] = dq[h].astype(BF16)
            dkv_ref[:, sl] += dk[h]
            dkv_ref[:, D_MODEL + h * X_HD:D_MODEL + (h + 1) * X_HD] += dv[h]

    return pl.pallas_call(
        body, name=name, grid=(B, nq),
        in_specs=[pl.BlockSpec((tq, D_MODEL), lambda b, i: (b * nq + i, 0)),
                  pl.BlockSpec((MEM_LEN, D_MODEL), lambda b, i: (b, 0)),
                  pl.BlockSpec((MEM_LEN, D_MODEL), lambda b, i: (b, 1)),
                  pl.BlockSpec((tq, D_MODEL), lambda b, i: (b * nq + i, 0))],
        out_specs=[pl.BlockSpec((tq, D_MODEL), lambda b, i: (b * nq + i, 0)),
                   pl.BlockSpec((MEM_LEN, 2 * D_MODEL), lambda b, i: (b, 0))],
        out_shape=[jax.ShapeDtypeStruct((T, D_MODEL), BF16), jax.ShapeDtypeStruct((B * MEM_LEN, 2 * D_MODEL), F32)],
        compiler_params=_cparams("parallel", "arbitrary"),
    )(q, kv, kv, do)


def _chunk_masks():
    row = lax.broadcasted_iota(jnp.int32, (A_SUPER, A_SUPER), 0)
    col = lax.broadcasted_iota(jnp.int32, (A_SUPER, A_SUPER), 1)
    same = jnp.right_shift(row, 5) == jnp.right_shift(col, 5)
    return same, same & (col <= row), same & (col >= row)


def _dot_mask(mask, x):
    m = mask.astype(BF16)
    hi = x.astype(BF16)
    rest = x - hi.astype(F32)
    mid = rest.astype(BF16)
    lo = (rest - mid.astype(F32)).astype(BF16)
    return _dot(m, hi) + _dot(m, mid) + _dot(m, lo)


def _chunk_row(x, which):
    rows = [x[c * A_CHUNK + which % A_CHUNK:c * A_CHUNK + which % A_CHUNK + 1, :] for c in range(A_SUPER // A_CHUNK)]
    return jnp.concatenate([jnp.broadcast_to(r, (A_CHUNK, x.shape[1])) for r in rows], axis=0)


def _hgrn_gates(fa, lb):
    sig = _sigmoid(fa)
    f = lb + (1.0 - lb) * sig
    return sig, f, jnp.log(f), 1.0 - f


def _hgrn_fwd(z, lb, onw, B, L, *, name):
    T = B * L
    ns = L // A_SUPER
    nch = A_SUPER // A_CHUNK

    def body(q_ref, f_ref, v_ref, g_ref, lb_ref, w_ref, oa_ref, o_ref, s_ref, st_ref, sc_ref):
        @pl.when(pl.program_id(2) == 0)
        def _():
            st_ref[...] = jnp.zeros_like(st_ref)

        s_ref[0] = st_ref[...]
        same, tril, _ = _chunk_masks()
        q, v = q_ref[...], v_ref[...]
        _, _, lf, k = _hgrn_gates(f_ref[...], lb_ref[...])
        bcs = _dot_mask(tril, lf)
        bl = _chunk_row(bcs, -1)
        qd = (q * jnp.exp(bcs)).astype(BF16)
        ki = (k * jnp.exp(-bcs)).astype(BF16)
        ke = (k * jnp.exp(bl - bcs)).astype(BF16)
        dec = jnp.exp(bl)
        vb = v.astype(BF16)
        a = jnp.where(tril, _dot(qd, ki, _NT), 0.0)
        o_ref[...] = _dot(a.astype(BF16), vb)
        chunks = [slice(c * A_CHUNK, (c + 1) * A_CHUNK) for c in range(nch)]
        outer = [_dot(vb[rs], ke[rs], _TN) for rs in chunks]
        st = st_ref[...]
        for c, rs in enumerate(chunks):
            sc_ref[c] = st.astype(BF16)
            st = st * dec[c * A_CHUNK:c * A_CHUNK + 1, :] + outer[c]
        st_ref[...] = st
        for c, rs in enumerate(chunks):
            o_ref[rs, :] += _dot(qd[rs], sc_ref[c], _NT)
        o = o_ref[...]
        r = lax.rsqrt(jnp.mean(o * o, axis=-1, keepdims=True) + NORM_EPS)
        g = g_ref[...]
        oa_ref[...] = o * r * w_ref[...] * (g * _sigmoid(g))

    def zspec(off):
        return pl.BlockSpec((A_SUPER, A_HEAD), lambda b, h, n: (b * ns + n, off + h))

    hvec = pl.BlockSpec((1, A_HEAD), lambda b, h, n: (0, h))
    ospec = pl.BlockSpec((A_SUPER, A_HEAD), lambda b, h, n: (b * ns + n, h))
    return pl.pallas_call(
        body, name=name, grid=(B, 4, ns),
        in_specs=[zspec(0), zspec(4), zspec(8), zspec(12), hvec, hvec],
        out_specs=[ospec, ospec, pl.BlockSpec((1, A_HEAD, A_HEAD), lambda b, h, n: ((b * 4 + h) * ns + n, 0, 0))],
        out_shape=[jax.ShapeDtypeStruct((T, A_WIDTH), F32), jax.ShapeDtypeStruct((T, A_WIDTH), F32),
                   jax.ShapeDtypeStruct((B * 4 * ns, A_HEAD, A_HEAD), F32)],
        scratch_shapes=[pltpu.VMEM((A_HEAD, A_HEAD), F32), pltpu.VMEM((nch, A_HEAD, A_HEAD), BF16)],
        compiler_params=_cparams("parallel", "parallel", "arbitrary"),
    )(z, z, z, z, lb, onw)


def _hgrn_bwd(z, lb, onw, o_raw, s_start, doa, B, L, *, name):
    T = B * L
    ns = L // A_SUPER
    nch = A_SUPER // A_CHUNK

    def body(q_ref, f_ref, v_ref, g_ref, lb_ref, w_ref, o_ref, s_ref, doa_ref,
             dq_ref, df_ref, dv_ref, dg_ref, dw_ref, dlb_ref, dst_ref, sc_ref, dsc_ref, dqd_ref, dke_ref, dblx_ref):
        @pl.when(pl.program_id(2) == 0)
        def _():
            dst_ref[...] = jnp.zeros_like(dst_ref)
            dw_ref[...] = jnp.zeros_like(dw_ref)
            dlb_ref[...] = jnp.zeros_like(dlb_ref)

        same, tril, triu = _chunk_masks()
        q, v, g, lb, w = q_ref[...], v_ref[...], g_ref[...], lb_ref[...], w_ref[...]
        sig, f, lf, k = _hgrn_gates(f_ref[...], lb)
        bcs = _dot_mask(tril, lf)
        bl = _chunk_row(bcs, -1)
        eb, enb, eeb = jnp.exp(bcs), jnp.exp(-bcs), jnp.exp(bl - bcs)
        qd, ki, ke = q * eb, k * enb, k * eeb
        qdb, kib, keb, vb = qd.astype(BF16), ki.astype(BF16), ke.astype(BF16), v.astype(BF16)
        dec = jnp.exp(bl)
        o = o_ref[...]
        r = lax.rsqrt(jnp.mean(o * o, axis=-1, keepdims=True) + NORM_EPS)
        on = o * r
        sg = _sigmoid(g)
        silu_g = g * sg
        doa = doa_ref[...]
        dg_ref[...] = doa * on * w * (sg * (1.0 + g * (1.0 - sg)))
        dw_ref[0] += jnp.sum(doa * on * silu_g, axis=0, keepdims=True)
        don = doa * w * silu_g
        do = r * (don - on * jnp.mean(don * on, axis=-1, keepdims=True))
        dob = do.astype(BF16)
        a = jnp.where(tril, _dot(qdb, kib, _NT), 0.0).astype(BF16)
        da = jnp.where(tril, _dot(dob, vb, _NT), 0.0).astype(BF16)
        dv_ref[...] = _dot(a, dob, _TN)
        dqd_ref[...] = _dot(da, kib)
        dki = _dot(da, qdb, _TN)
        chunks = [slice(c * A_CHUNK, (c + 1) * A_CHUNK) for c in range(nch)]
        outer = [_dot(vb[rs], keb[rs], _TN) for rs in chunks]
        st = s_ref[0]
        for c in range(nch):
            sc_ref[c] = st
            st = st * dec[c * A_CHUNK:c * A_CHUNK + 1, :] + outer[c]
        outer_g = [_dot(dob[rs], qdb[rs], _TN) for rs in chunks]
        dst = dst_ref[...]
        for c in reversed(range(nch)):
            dsc_ref[c] = dst
            dst = dst * dec[c * A_CHUNK:c * A_CHUNK + 1, :] + outer_g[c]
        dst_ref[...] = dst
        for c, rs in enumerate(chunks):
            dec_c = dec[c * A_CHUNK:c * A_CHUNK + 1, :]
            dsc, stc = dsc_ref[c], sc_ref[c]
            dscb = dsc.astype(BF16)
            dv_ref[rs, :] += _dot(keb[rs], dscb, _NT)
            dke_ref[rs, :] = _dot(vb[rs], dscb)
            ddec = jnp.sum(dsc * stc, axis=0, keepdims=True)
            dqd_ref[rs, :] += _dot(dob[rs], stc.astype(BF16))
            dblx_ref[rs, :] = jnp.broadcast_to(ddec * dec_c, (A_CHUNK, A_HEAD))
        dqd, dke = dqd_ref[...], dke_ref[...]
        dq_ref[...] = dqd * eb
        keke = dke * ke
        db = dqd * qd - dki * ki - keke
        sums = _dot_mask(triu, jnp.concatenate([db, keke], axis=1))
        dk = dki * enb + dke * eeb
        dlf = sums[:, :A_HEAD] + _chunk_row(sums[:, A_HEAD:], 0) + dblx_ref[...]
        dff = dlf / f - dk
        df_ref[...] = dff * (1.0 - lb) * sig * (1.0 - sig)
        dlb_ref[0] += jnp.sum(dff * (1.0 - sig), axis=0, keepdims=True)

    def rev(n):
        return ns - 1 - n

    def zspec(off):
        return pl.BlockSpec((A_SUPER, A_HEAD), lambda b, h, n: (b * ns + rev(n), off + h))

    hvec = pl.BlockSpec((1, A_HEAD), lambda b, h, n: (0, h))
    ospec = pl.BlockSpec((A_SUPER, A_HEAD), lambda b, h, n: (b * ns + rev(n), h))
    acc = pl.BlockSpec((1, 1, A_HEAD), lambda b, h, n: (b * 4 + h, 0, 0))
    big = jax.ShapeDtypeStruct((T, A_WIDTH), F32)
    small = jax.ShapeDtypeStruct((B * 4, 1, A_HEAD), F32)
    return pl.pallas_call(
        body, name=name, grid=(B, 4, ns),
        in_specs=[zspec(0), zspec(4), zspec(8), zspec(12), hvec, hvec, ospec,
                  pl.BlockSpec((1, A_HEAD, A_HEAD), lambda b, h, n: ((b * 4 + h) * ns + rev(n), 0, 0)), ospec],
        out_specs=[ospec, ospec, ospec, ospec, acc, acc],
        out_shape=[big, big, big, big, small, small],
        scratch_shapes=[pltpu.VMEM((A_HEAD, A_HEAD), F32), pltpu.VMEM((nch, A_HEAD, A_HEAD), F32),
                        pltpu.VMEM((nch, A_HEAD, A_HEAD), F32),
                        pltpu.VMEM((A_SUPER, A_HEAD), F32), pltpu.VMEM((A_SUPER, A_HEAD), F32),
                        pltpu.VMEM((A_SUPER, A_HEAD), F32)],
        compiler_params=_cparams("parallel", "parallel", "arbitrary"),
    )(z, z, z, z, lb, onw, o_raw, s_start, doa)


def _rope_tables(L):
    half = A_HEAD // 2
    inv_freq = ROPE_THETA ** (-jnp.arange(half, dtype=F32) / half)
    ang = jnp.arange(L, dtype=F32)[:, None] * inv_freq[None, :]
    cos, sin = jnp.cos(ang), jnp.sin(ang)
    return jnp.concatenate([cos, cos], axis=-1), jnp.concatenate([-sin, sin], axis=-1)


def _rope_fwd(z, cos2, sin2, B, L, *, name):
    T = B * L
    tm = 512
    nl = L // tm

    def body(x_ref, c_ref, s_ref, q_ref, k_ref):
        c, s = c_ref[...], s_ref[...]
        for h in range(8):
            x = x_ref[:, h * A_HEAD:(h + 1) * A_HEAD]
            out = x * c + pltpu.roll(x, A_HEAD // 2, 1) * s
            o_ref = q_ref if h < 4 else k_ref
            o_ref[:, (h % 4) * A_HEAD:(h % 4 + 1) * A_HEAD] = out

    tab = pl.BlockSpec((tm, A_HEAD), lambda i: (i % nl, 0))
    out = pl.BlockSpec((tm, 512), lambda i: (i, 0))
    return pl.pallas_call(
        body, name=name, grid=(T // tm,),
        in_specs=[pl.BlockSpec((tm, 1024), lambda i: (i, 2)), tab, tab], out_specs=[out, out],
        out_shape=[jax.ShapeDtypeStruct((T, 512), F32)] * 2, compiler_params=_cparams("parallel"),
    )(z, cos2, sin2)


def _rope_bwd(dqs, dks, dvs, cos2, sin2, B, L, *, name):
    T = B * L
    tm = 256
    nl = L // tm

    def body(*refs):
        c, s = refs[9][...], refs[10][...]
        o_ref = refs[11]
        for part in range(3):
            a_ref, b_ref, c_ref = refs[3 * part:3 * part + 3]
            for h in range(4):
                cols = slice(h * A_HEAD, (h + 1) * A_HEAD)
                d = a_ref[:, cols] + b_ref[:, cols] + c_ref[:, cols]
                if part < 2:
                    d = d * c - pltpu.roll(d, A_HEAD // 2, 1) * s
                o_ref[:, part * 512 + h * A_HEAD:part * 512 + (h + 1) * A_HEAD] = d

    blk = pl.BlockSpec((tm, 512), lambda i: (i, 0))
    tab = pl.BlockSpec((tm, A_HEAD), lambda i: (i % nl, 0))
    return pl.pallas_call(
        body, name=name, grid=(T // tm,), in_specs=[blk] * 9 + [tab, tab],
        out_specs=pl.BlockSpec((tm, 1536), lambda i: (i, 0)),
        out_shape=jax.ShapeDtypeStruct((T, 1536), F32), compiler_params=_cparams("parallel"),
    )(*dqs, *dks, *dvs, cos2, sin2)


def _band_masks():
    i = lax.broadcasted_iota(jnp.int32, (B_SPAN, B_SPAN), 0)
    j = lax.broadcasted_iota(jnp.int32, (B_SPAN, B_SPAN), 1)
    return i <= j, j <= i


class _DilPlan:
    def __init__(self, dil, B, L):
        self.dil, self.B, self.L = dil, B, L
        self.rows = 4 * B_SPAN if dil == 1 else B_SPAN * dil
        self.n = L // self.rows
        self.hr = 4 if dil == 4 else 1
        self.cw = 512 if dil == 1 else A_HEAD
        self.has_other = dil != 16
        self.grid = (B, self.n, 4 if dil == 16 else 1)
        if dil == 1:
            self.items = [(j, 0, h) for j in range(4) for h in range(4)]
        elif dil == 4:
            self.items = [(0, r, h) for r in range(4) for h in range(4)]
        else:
            self.items = [(0, r, 0) for r in range(16)]
        self.groups = [self.items[i:i + 4] for i in range(0, 16, 4)]

    def operands(self, arr):
        return [arr] * self.hr

    def specs(self, col0, role="cur"):
        n, nb128 = self.n, self.L // B_SPAN
        out = []
        for h in range(self.hr):
            cb = col0 // self.cw + h
            if role == "cur":
                out.append(pl.BlockSpec((self.rows, self.cw), lambda b, i, hh, cb=cb: (b * n + i, cb + hh)))
            elif self.dil == 1:
                shift = -1 if role == "prev" else 4
                out.append(pl.BlockSpec((B_SPAN, self.cw),
                                        lambda b, i, hh, cb=cb, shift=shift: (b * nb128 + jnp.clip(4 * i + shift, 0, nb128 - 1), cb)))
            else:
                shift = -1 if role == "prev" else 1
                out.append(pl.BlockSpec((self.rows, self.cw),
                                        lambda b, i, hh, cb=cb, shift=shift: (b * n + jnp.clip(i + shift, 0, n - 1), cb)))
        return out

    def out_spec(self):
        n = self.n
        if self.dil == 4:
            return pl.BlockSpec((self.rows, 512), lambda b, i, hh: (b * n + i, 0))
        return pl.BlockSpec((self.rows, self.cw), lambda b, i, hh: (b * n + i, hh))

    def scratch(self, n_out):
        return [pltpu.VMEM((4, self.rows, A_HEAD), F32)] * n_out if self.dil == 4 else []

    def store(self, out_ref, scr, item, val):
        j, r, h = item
        if self.dil == 1:
            out_ref[pl.ds(j * B_SPAN, B_SPAN), pl.ds(h * A_HEAD, A_HEAD)] = val
        elif self.dil == 4:
            scr.at[h][pl.ds(r, B_SPAN, stride=4), :] = val
        else:
            out_ref[pl.ds(r, B_SPAN, stride=self.dil), :] = val

    def flush(self, out_ref, scr):
        if self.dil == 4:
            for h in range(4):
                out_ref[:, h * A_HEAD:(h + 1) * A_HEAD] = scr[h]

    def cur(self, refs, item):
        j, r, h = item
        if self.dil == 1:
            return refs[0], (pl.ds(j * B_SPAN, B_SPAN), pl.ds(h * A_HEAD, A_HEAD))
        return refs[h], (pl.ds(r, B_SPAN, stride=self.dil), slice(None))

    def other(self, refs, other_refs, item, role):
        j, r, h = item
        if self.dil == 1:
            cols = pl.ds(h * A_HEAD, A_HEAD)
            jj = j - 1 if role == "prev" else j + 1
            if 0 <= jj < 4:
                return refs[0], (pl.ds(jj * B_SPAN, B_SPAN), cols)
            return other_refs[0], (pl.ds(0, B_SPAN), cols)
        return other_refs[h], (pl.ds(r, B_SPAN, stride=self.dil), slice(None))

    def other_valid(self, item, role):
        j = item[0]
        i = pl.program_id(1)
        if role == "prev":
            return True if (self.dil == 1 and j > 0) else i > 0
        return True if (self.dil == 1 and j < 3) else i < self.n - 1


def _ld(pair):
    ref, idx = pair
    return ref[idx]


def _dil_fwd(qr, kr, z, dil, B, L, *, name):
    T = B * L
    plan = _DilPlan(dil, B, L)
    hr, has_prev = plan.hr, plan.has_other
    scale = A_HEAD ** -0.5

    n_t = 5 if has_prev else 3

    def body(*refs):
        lists = [refs[i * hr:(i + 1) * hr] for i in range(n_t)]
        o_ref, l_ref = refs[n_t * hr], refs[n_t * hr + 1]
        scr = refs[n_t * hr + 2:] or (None, None)
        if has_prev:
            q_r, kc_r, vc_r, kp_r, vp_r = lists
        else:
            q_r, kc_r, vc_r = lists
        mp0, mc = _band_masks()

        for group in plan.groups:
            qs = [_ld(plan.cur(q_r, i)).astype(BF16) for i in group]
            sc = [jnp.where(mc, _dot(q, _ld(plan.cur(kc_r, i)).astype(BF16), _NT) * scale, NEG_BIG) for q, i in zip(qs, group)]
            m = [jnp.max(s, axis=-1, keepdims=True) for s in sc]
            if has_prev:
                mps = [mp0 & plan.other_valid(i, "prev") for i in group]
                sp = [jnp.where(mk, _dot(q, _ld(plan.other(kc_r, kp_r, i, "prev")).astype(BF16), _NT) * scale, NEG_BIG)
                      for q, i, mk in zip(qs, group, mps)]
                m = [jnp.maximum(a, jnp.max(s, axis=-1, keepdims=True)) for a, s in zip(m, sp)]
            pc = [jnp.exp(s - a) for s, a in zip(sc, m)]
            l = [jnp.sum(p, axis=-1, keepdims=True) for p in pc]
            o = [_dot(p.astype(BF16), _ld(plan.cur(vc_r, i)).astype(BF16)) for p, i in zip(pc, group)]
            if has_prev:
                pp = [jnp.exp(s - a) for s, a in zip(sp, m)]
                l = [a + jnp.sum(p, axis=-1, keepdims=True) for a, p in zip(l, pp)]
                o = [a + _dot(p.astype(BF16), _ld(plan.other(vc_r, vp_r, i, "prev")).astype(BF16)) for a, p, i in zip(o, pp, group)]
            for i, oi, li, mi in zip(group, o, l, m):
                plan.store(o_ref, scr[0], i, oi / li)
                plan.store(l_ref, scr[1], i, jnp.broadcast_to(mi + jnp.log(li), (B_SPAN, A_HEAD)))
        plan.flush(o_ref, scr[0])
        plan.flush(l_ref, scr[1])

    tensors = [(qr, 0, "cur"), (kr, 0, "cur"), (z, 3072, "cur")] + ([(kr, 0, "prev"), (z, 3072, "prev")] if has_prev else [])
    return pl.pallas_call(
        body, name=name, grid=plan.grid,
        in_specs=[sp for _, c, role in tensors for sp in plan.specs(c, role)],
        out_specs=[plan.out_spec()] * 2, out_shape=[jax.ShapeDtypeStruct((T, 512), F32)] * 2,
        scratch_shapes=plan.scratch(2), compiler_params=_cparams("parallel", "parallel", "parallel"),
    )(*[a for arr, _, _ in tensors for a in plan.operands(arr)])


def _dil_combine(os_, ls_, *, name):
    T = os_[0].shape[0]
    tm = 256

    def body(o1, o2, o3, l1, l2, l3, ob_ref, lse_ref):
        a1, a2, a3 = l1[...], l2[...], l3[...]
        m = jnp.maximum(jnp.maximum(a1, a2), a3)
        e1, e2, e3 = jnp.exp(a1 - m), jnp.exp(a2 - m), jnp.exp(a3 - m)
        den = e1 + e2 + e3
        ob_ref[...] = (e1 * o1[...] + e2 * o2[...] + e3 * o3[...]) / den
        lse_ref[...] = m + jnp.log(den)

    blk = pl.BlockSpec((tm, 512), lambda i: (i, 0))
    return pl.pallas_call(
        body, name=name, grid=(T // tm,), in_specs=[blk] * 6, out_specs=[blk, blk],
        out_shape=[jax.ShapeDtypeStruct((T, 512), F32)] * 2, compiler_params=_cparams("parallel"),
    )(*[o.reshape(T, 512) for o in os_], *[l.reshape(T, 512) for l in ls_])


def _dil_bwd_q(qr, kr, z, dymix, out, lse, dil, B, L, *, name):
    T = B * L
    plan = _DilPlan(dil, B, L)
    hr, has_prev = plan.hr, plan.has_other
    scale = A_HEAD ** -0.5
    n_t = 8 if has_prev else 6

    def body(*refs):
        lists = [refs[i * hr:(i + 1) * hr] for i in range(n_t)]
        dq_ref = refs[n_t * hr]
        scr = refs[n_t * hr + 1:] or (None,)
        q_r, kc_r, vc_r, do_r, out_r, lse_r = lists[:6]
        mp0, mc = _band_masks()

        for group in plan.groups:
            qs = [_ld(plan.cur(q_r, i)).astype(BF16) for i in group]
            dos = [_ld(plan.cur(do_r, i)) for i in group]
            delta = [jnp.sum(d * _ld(plan.cur(out_r, i)), axis=-1, keepdims=True) for d, i in zip(dos, group)]
            dob = [d.astype(BF16) for d in dos]
            lse = [_ld(plan.cur(lse_r, i)) for i in group]
            kc = [_ld(plan.cur(kc_r, i)).astype(BF16) for i in group]
            pc = [jnp.where(mc, jnp.exp(_dot(q, k, _NT) * scale - a), 0.0) for q, k, a in zip(qs, kc, lse)]
            dsc = [p * (_dot(d, _ld(plan.cur(vc_r, i)).astype(BF16), _NT) - dl) * scale
                   for p, d, i, dl in zip(pc, dob, group, delta)]
            dq = [_dot(d.astype(BF16), k) for d, k in zip(dsc, kc)]
            if has_prev:
                kp_r, vp_r = lists[6], lists[7]
                mps = [mp0 & plan.other_valid(i, "prev") for i in group]
                kp = [_ld(plan.other(kc_r, kp_r, i, "prev")).astype(BF16) for i in group]
                pp = [jnp.where(mk, jnp.exp(_dot(q, k, _NT) * scale - a), 0.0) for q, k, a, mk in zip(qs, kp, lse, mps)]
                dsp = [p * (_dot(d, _ld(plan.other(vc_r, vp_r, i, "prev")).astype(BF16), _NT) - dl) * scale
                       for p, d, i, dl in zip(pp, dob, group, delta)]
                dq = [a + _dot(d.astype(BF16), k) for a, d, k in zip(dq, dsp, kp)]
            for i, d in zip(group, dq):
                plan.store(dq_ref, scr[0], i, d)
        plan.flush(dq_ref, scr[0])

    tensors = ([(qr, 0, "cur"), (kr, 0, "cur"), (z, 3072, "cur"), (dymix, 512, "cur"), (out, 0, "cur"), (lse, 0, "cur")]
               + ([(kr, 0, "prev"), (z, 3072, "prev")] if has_prev else []))
    return pl.pallas_call(
        body, name=name, grid=plan.grid,
        in_specs=[sp for _, c, role in tensors for sp in plan.specs(c, role)],
        out_specs=plan.out_spec(), out_shape=jax.ShapeDtypeStruct((T, 512), F32),
        scratch_shapes=plan.scratch(1), compiler_params=_cparams("parallel", "parallel", "parallel"),
    )(*[a for arr, _, _ in tensors for a in plan.operands(arr)])


def _dil_bwd_kv(qr, kr, z, dymix, out, lse, dil, B, L, *, name):
    T = B * L
    plan = _DilPlan(dil, B, L)
    hr, has_next = plan.hr, plan.has_other
    scale = A_HEAD ** -0.5
    n_t = 10 if has_next else 6

    def body(*refs):
        lists = [refs[i * hr:(i + 1) * hr] for i in range(n_t)]
        dk_ref, dv_ref = refs[n_t * hr], refs[n_t * hr + 1]
        scr = refs[n_t * hr + 2:] or (None, None)
        k_r, v_r = lists[0], lists[1]
        own = lists[2:6]
        mp0, mc = _band_masks()

        for group in plan.groups:
            kh = [_ld(plan.cur(k_r, i)).astype(BF16) for i in group]
            vh = [_ld(plan.cur(v_r, i)).astype(BF16) for i in group]
            dk, dv = [None] * len(group), [None] * len(group)
            for role in ("own", "next") if has_next else ("own",):
                if role == "own":
                    get = lambda t, i: _ld(plan.cur(own[t], i))
                    masks = [mc] * len(group)
                else:
                    get = lambda t, i: _ld(plan.other(own[t], lists[6 + t], i, "next"))
                    masks = [mp0 & plan.other_valid(i, "next") for i in group]
                qs = [get(0, i).astype(BF16) for i in group]
                dos = [get(1, i) for i in group]
                delta = [jnp.sum(d * get(2, i), axis=-1, keepdims=True) for d, i in zip(dos, group)]
                dob = [d.astype(BF16) for d in dos]
                p = [jnp.where(mk, jnp.exp(_dot(q, k, _NT) * scale - get(3, i)), 0.0) for q, k, i, mk in zip(qs, kh, group, masks)]
                dvn = [_dot(a.astype(BF16), d, _TN) for a, d in zip(p, dob)]
                ds = [a * (_dot(d, v, _NT) - dl) * scale for a, d, v, dl in zip(p, dob, vh, delta)]
                dkn = [_dot(d.astype(BF16), q, _TN) for d, q in zip(ds, qs)]
                dv = [n if o is None else o + n for o, n in zip(dv, dvn)]
                dk = [n if o is None else o + n for o, n in zip(dk, dkn)]
            for i, a, b in zip(group, dk, dv):
                plan.store(dk_ref, scr[0], i, a)
                plan.store(dv_ref, scr[1], i, b)
        plan.flush(dk_ref, scr[0])
        plan.flush(dv_ref, scr[1])

    queries = [(qr, 0), (dymix, 512), (out, 0), (lse, 0)]
    tensors = ([(kr, 0, "cur"), (z, 3072, "cur")] + [(a, c, "cur") for a, c in queries]
               + ([(a, c, "next") for a, c in queries] if has_next else []))
    return pl.pallas_call(
        body, name=name, grid=plan.grid,
        in_specs=[sp for _, c, role in tensors for sp in plan.specs(c, role)],
        out_specs=[plan.out_spec()] * 2, out_shape=[jax.ShapeDtypeStruct((T, 512), F32)] * 2,
        scratch_shapes=plan.scratch(2), compiler_params=_cparams("parallel", "parallel", "parallel"),
    )(*[a for arr, _, _ in tensors for a in plan.operands(arr)])


def _s5_build(lam_re, lam_im, log_dt, b_re, b_im, c_re, c_im):
    G, P, TC = C_GROUPS, C_STATE, C_TC
    lr = jnp.minimum(lam_re, C_MIN_NEG_RE)
    li = lam_im
    dt = jnp.exp(log_dt)[:, None]
    mag = jnp.exp(dt * lr)
    ar, ai = mag * jnp.cos(dt * li), mag * jnp.sin(dt * li)
    den = lr * lr + li * li
    zr = ((ar - 1.0) * lr + ai * li) / den
    zi = (ai * lr - (ar - 1.0) * li) / den
    bbr = zr[..., None] * b_re - zi[..., None] * b_im
    bbi = zr[..., None] * b_im + zi[..., None] * b_re
    ks = jnp.arange(TC + 1, dtype=F32)[:, None, None]
    pmag = jnp.exp(ks * (dt * lr)[None])
    pr, pi = pmag * jnp.cos(ks * (dt * li)[None]), pmag * jnp.sin(ks * (dt * li)[None])
    car = c_re[None] * pr[:, :, None, :] - c_im[None] * pi[:, :, None, :]
    cai = c_re[None] * pi[:, :, None, :] + c_im[None] * pr[:, :, None, :]
    kern = (jnp.einsum('lgop,gpc->lgco', car[:TC], bbr, precision=HI)
            - jnp.einsum('lgop,gpc->lgco', cai[:TC], bbi, precision=HI))
    pr_e, pi_e = pr[TC - 1 - jnp.arange(TC)], pi[TC - 1 - jnp.arange(TC)]
    er = pr_e[:, :, :, None] * bbr[None] - pi_e[:, :, :, None] * bbi[None]
    ei = pr_e[:, :, :, None] * bbi[None] + pi_e[:, :, :, None] * bbr[None]
    ez = jnp.stack([er, ei], axis=2).reshape(TC, C_NB, C_GB, 2, P, C_GROUP).transpose(1, 0, 2, 5, 3, 4)
    fz = jnp.stack([car[1:], -cai[1:]], axis=0).reshape(2, TC, C_NB, C_GB, C_GROUP, P).transpose(2, 0, 3, 5, 1, 4)
    return kern, ez, fz, pr[TC], pi[TC]


def _s5_lag_blocks(kern):
    eye = jnp.eye(C_GB, dtype=kern.dtype)
    return (kern.reshape(C_TC, C_NB, C_GB, C_GROUP, C_GROUP)[:, :, :, :, None, :]
            * eye[None, None, :, None, :, None]).reshape(C_TC, C_NB, LANES, LANES)


def _s5_dense(kbd, ez, fz, *, name):
    ezc = ez.reshape(C_NB, C_W8, LANES)
    fzc = fz.reshape(C_NB, C_S8, LANES)
    half = C_GB * C_STATE

    def body(k_ref, e_ref, f_ref, m8_ref, e8_ref, f8_ref):
        zero = jnp.zeros((LANES, LANES), BF16)
        for s in range(C_TC):
            for t in range(C_TC):
                m8_ref[0, s * LANES:(s + 1) * LANES, t * LANES:(t + 1) * LANES] = (
                    k_ref[t - s, 0].astype(BF16) if t >= s else zero)
        lane = lax.broadcasted_iota(jnp.int32, (1, LANES), 1)
        src = e_ref[0]
        swapped = pltpu.roll(src, C_STATE, 1)
        rowg = jnp.bitwise_and(jnp.right_shift(lax.broadcasted_iota(jnp.int32, (C_W8, LANES), 0), 4), C_GB - 1)
        for kb in range(C_S8 // LANES):
            z, g0 = kb // (C_GB // 2), 2 * (kb % (C_GB // 2))
            first = jnp.where(lane < C_STATE, src if z == 0 else swapped, 0.0)
            second = jnp.where(lane >= C_STATE, swapped if z == 0 else src, 0.0)
            e8_ref[0, :, kb * LANES:(kb + 1) * LANES] = jnp.where(
                rowg == g0, first, jnp.where(rowg == g0 + 1, second, 0.0)).astype(BF16)
        for z in range(2):
            for g in range(C_GB):
                rows = slice(z * half + g * C_STATE, z * half + (g + 1) * C_STATE)
                piece = f_ref[0, rows, :]
                mine = (lane >= g * C_GROUP) & (lane < (g + 1) * C_GROUP)
                for t in range(C_TC):
                    f8_ref[0, rows, t * LANES:(t + 1) * LANES] = jnp.where(
                        mine, pltpu.roll(piece, ((g - t) * C_GROUP) % LANES, 1), 0.0).astype(BF16)

    blk = lambda r, c: pl.BlockSpec((1, r, c), lambda b: (b, 0, 0))
    return pl.pallas_call(
        body, name=name, grid=(C_NB,),
        in_specs=[pl.BlockSpec((C_TC, 1, LANES, LANES), lambda b: (0, b, 0, 0)), blk(C_W8, LANES), blk(C_S8, LANES)],
        out_specs=[blk(C_W8, C_W8), blk(C_W8, C_S8), blk(C_S8, C_W8)],
        out_shape=[jax.ShapeDtypeStruct((C_NB, C_W8, C_W8), BF16), jax.ShapeDtypeStruct((C_NB, C_W8, C_S8), BF16),
                   jax.ShapeDtypeStruct((C_NB, C_S8, C_W8), BF16)],
        compiler_params=_cparams("parallel"),
    )(kbd, ezc, fzc)


C_NB = C_GROUPS * C_GROUP // LANES
C_GB = C_GROUPS // C_NB
C_W8 = C_TC * LANES
C_S8 = 2 * C_GB * C_STATE


def _s5_scan_tables(lam_re, lam_im, log_dt, nsteps):
    lr = jnp.minimum(lam_re, C_MIN_NEG_RE)
    dt = jnp.exp(log_dt)[:, None]
    ks = (C_TC * 2.0 ** jnp.arange(8, dtype=F32))[None, :, None]
    keep = (jnp.arange(8) < nsteps)[None, :, None]
    pmag = jnp.exp(ks * (dt * lr)[:, None, :])
    ang = ks * (dt * lam_im)[:, None, :]

    def blocks(t):
        return t.reshape(C_NB, C_GB, 8, C_STATE).transpose(0, 2, 1, 3).reshape(C_NB, 8, C_GB * C_STATE)

    pr = blocks(jnp.where(keep, pmag * jnp.cos(ang), 0.0))
    pi = blocks(jnp.where(keep, pmag * jnp.sin(ang), 0.0))
    return jnp.concatenate([pr, pr], axis=-1), jnp.concatenate([-pi, pi], axis=-1)


def _s5_rows(t, R):
    return pl.ds(t, R, stride=C_TC)


def _s5_fwd(u, dsk, m8, e8, f8, tab_r, tab_i, B, L, *, name):
    T = B * L
    R = L // C_TC
    nsteps = int(math.log2(R))

    def body(u_ref, d_ref, m_ref, e_ref, f_ref, tr_ref, ti_ref, gl_ref, y_ref, x8_ref, xs_ref):
        for t in range(C_TC):
            x8_ref[0, :, t * LANES:(t + 1) * LANES] = u_ref[_s5_rows(t, R), :].astype(BF16)
        x8 = x8_ref[0]
        x = _dot(x8, e_ref[0])
        row = lax.broadcasted_iota(jnp.int32, (R, C_S8), 0)
        for k in range(nsteps):
            s = 1 << k
            sh = pltpu.roll(x, s, 0)
            upd = tr_ref[0, k:k + 1, :] * sh + ti_ref[0, k:k + 1, :] * pltpu.roll(sh, C_S8 // 2, 1)
            x = x + jnp.where(row >= s, upd, 0.0)
        xs = jnp.where(row >= 1, pltpu.roll(x, 1, 0), 0.0)
        xs_ref[0] = xs
        y8 = _dot(x8, m_ref[0]) + _dot(xs.astype(BF16), f_ref[0])
        d = d_ref[...]
        for t in range(C_TC):
            rows = _s5_rows(t, R)
            y = y8[:, t * LANES:(t + 1) * LANES] + d * u_ref[rows, :]
            y_ref[rows, :] = y
            gl_ref[rows, :] = 0.5 * y * (1.0 + _erf(y * (2.0 ** -0.5)))

    tok = pl.BlockSpec((L, LANES), lambda c, b: (b, c))
    per_block = lambda shape: pl.BlockSpec((1,) + shape, lambda c, b: (c, 0, 0))
    per_step = lambda shape: pl.BlockSpec((1,) + shape, lambda c, b: (c * B + b, 0, 0))
    return pl.pallas_call(
        body, name=name, grid=(C_NB, B),
        in_specs=[tok, pl.BlockSpec((1, LANES), lambda c, b: (0, c)), per_block((C_W8, C_W8)), per_block((C_W8, C_S8)),
                  per_block((C_S8, C_W8)), per_block((8, C_S8)), per_block((8, C_S8))],
        out_specs=[tok, tok, per_step((R, C_W8)), per_step((R, C_S8))],
        out_shape=[jax.ShapeDtypeStruct((T, D_MODEL), F32), jax.ShapeDtypeStruct((T, D_MODEL), F32),
                   jax.ShapeDtypeStruct((C_NB * B, R, C_W8), BF16), jax.ShapeDtypeStruct((C_NB * B, R, C_S8), F32)],
        compiler_params=_cparams("parallel", "parallel"),
    )(u, dsk, m8, e8, f8, tab_r, tab_i)


def _s5_bwd(dgl, y, u, dsk, xs, m8, e8, f8, tab_r, tab_i, B, L, *, name):
    T = B * L
    R = L // C_TC
    nsteps = int(math.log2(R))

    def body(dgl_ref, y_ref, u_ref, d_ref, xs_ref, m_ref, e_ref, f_ref, tr_ref, ti_ref,
             du_ref, dy8_ref, de_ref, da_ref, dd_ref, dyf_ref):
        @pl.when(pl.program_id(1) == 0)
        def _():
            da_ref[...] = jnp.zeros_like(da_ref)
            dd_ref[...] = jnp.zeros_like(dd_ref)

        dd = jnp.zeros((1, LANES), F32)
        for t in range(C_TC):
            rows = _s5_rows(t, R)
            yv = y_ref[rows, :]
            cdf = 0.5 * (1.0 + _erf(yv * (2.0 ** -0.5)))
            pdf = jnp.exp(-0.5 * yv * yv) * (1.0 / math.sqrt(2.0 * math.pi))
            dy = dgl_ref[rows, :] * (cdf + yv * pdf)
            dd = dd + jnp.sum(dy * u_ref[rows, :], axis=0, keepdims=True)
            dyf_ref[:, t * LANES:(t + 1) * LANES] = dy
        dd_ref[...] += dd
        dy8 = dyf_ref[...].astype(BF16)
        dy8_ref[0] = dy8
        xs = xs_ref[0]
        gx = _dot(dy8, f_ref[0], _NT)
        row = lax.broadcasted_iota(jnp.int32, (R, C_S8), 0)
        for k in range(nsteps):
            s = 1 << k
            sh = pltpu.roll(gx, R - s, 0)
            upd = tr_ref[0, k:k + 1, :] * sh - ti_ref[0, k:k + 1, :] * pltpu.roll(sh, C_S8 // 2, 1)
            gx = gx + jnp.where(row + s < R, upd, 0.0)
        de_in = jnp.where(row + 1 < R, pltpu.roll(gx, R - 1, 0), 0.0)
        deb = de_in.astype(BF16)
        de_ref[0] = deb
        da_ref[0, 0:1, :] += jnp.sum(de_in * xs, axis=0, keepdims=True)
        da_ref[0, 1:2, :] += jnp.sum(de_in * pltpu.roll(xs, C_S8 // 2, 1), axis=0, keepdims=True)
        dx8 = _dot(dy8, m_ref[0], _NT) + _dot(deb, e_ref[0], _NT)
        d = d_ref[...]
        for t in range(C_TC):
            cols = slice(t * LANES, (t + 1) * LANES)
            du_ref[_s5_rows(t, R), :] = dx8[:, cols] + d * dyf_ref[:, cols]

    tok = pl.BlockSpec((L, LANES), lambda c, b: (b, c))
    vec = pl.BlockSpec((1, LANES), lambda c, b: (0, c))
    per_block = lambda shape: pl.BlockSpec((1,) + shape, lambda c, b: (c, 0, 0))
    per_step = lambda shape: pl.BlockSpec((1,) + shape, lambda c, b: (c * B + b, 0, 0))
    return pl.pallas_call(
        body, name=name, grid=(C_NB, B),
        in_specs=[tok, tok, tok, vec, per_step((R, C_S8)), per_block((C_W8, C_W8)),
                  per_block((C_W8, C_S8)), per_block((C_S8, C_W8)), per_block((8, C_S8)), per_block((8, C_S8))],
        out_specs=[tok, per_step((R, C_W8)), per_step((R, C_S8)), per_block((8, C_S8)), vec],
        out_shape=[jax.ShapeDtypeStruct((T, D_MODEL), F32), jax.ShapeDtypeStruct((C_NB * B, R, C_W8), BF16),
                   jax.ShapeDtypeStruct((C_NB * B, R, C_S8), BF16), jax.ShapeDtypeStruct((C_NB, 8, C_S8), F32),
                   jax.ShapeDtypeStruct((1, D_MODEL), F32)],
        scratch_shapes=[pltpu.VMEM((R, C_W8), F32)],
        compiler_params=_cparams("parallel", "arbitrary"),
    )(dgl, y, u, dsk, xs, m8, e8, f8, tab_r, tab_i)


def _bmm_tn(a, b, nb, fold, *, name):
    a = a.reshape(nb, -1, a.shape[-1])
    b = b.reshape(nb, -1, b.shape[-1])
    K, M, N = a.shape[1], a.shape[2], b.shape[2]
    half = C_GB * C_STATE

    def body(a_ref, b_ref, o_ref, p_ref):
        p_ref[...] = _dot(a_ref[0].astype(BF16), b_ref[0].astype(BF16), _TN)
        lane = lax.broadcasted_iota(jnp.int32, (1, LANES), 1)
        if fold == "lags":
            for lag in range(C_TC):
                blocks = [p_ref[s * LANES:(s + 1) * LANES, (s + lag) * LANES:(s + lag + 1) * LANES] for s in range(C_TC - lag)]
                o_ref[0, lag] = functools.reduce(lambda u, v: u + v, blocks)
        elif fold == "e":
            for g in range(C_GB):
                lo, hi = LANES * (g // 2), half + LANES * (g // 2)
                for s in range(C_TC):
                    rows = slice(s * LANES + g * C_GROUP, s * LANES + (g + 1) * C_GROUP)
                    re, im = p_ref[rows, lo:lo + LANES], p_ref[rows, hi:hi + LANES]
                    if g % 2 == 0:
                        im = pltpu.roll(im, C_STATE, 1)
                    else:
                        re = pltpu.roll(re, C_STATE, 1)
                    o_ref[0, rows, :] = jnp.where(lane < C_STATE, re, im)
        else:
            for z in range(2):
                for g in range(C_GB):
                    rows = slice(z * half + g * C_STATE, z * half + (g + 1) * C_STATE)
                    val = jnp.zeros((C_STATE, LANES), F32)
                    for t in range(C_TC):
                        blk = pltpu.roll(p_ref[rows, t * LANES:(t + 1) * LANES], ((t - g) * C_GROUP) % LANES, 1)
                        val = jnp.where((lane >= t * C_GROUP) & (lane < (t + 1) * C_GROUP), blk, val)
                    o_ref[0, rows, :] = val

    if fold == "lags":
        out_spec = pl.BlockSpec((1, C_TC, LANES, LANES), lambda c: (c, 0, 0, 0))
        out_shape = jax.ShapeDtypeStruct((nb, C_TC, LANES, LANES), F32)
    else:
        out_spec = pl.BlockSpec((1, M, LANES), lambda c: (c, 0, 0))
        out_shape = jax.ShapeDtypeStruct((nb, M, LANES), F32)
    return pl.pallas_call(
        body, name=name, grid=(nb,),
        in_specs=[pl.BlockSpec((1, K, M), lambda c: (c, 0, 0)), pl.BlockSpec((1, K, N), lambda c: (c, 0, 0))],
        out_specs=out_spec, out_shape=out_shape, scratch_shapes=[pltpu.VMEM((M, N), F32)],
        compiler_params=_cparams("parallel"),
    )(a, b)


def _loss_head(y, target, *, name):
    T, C = y.shape
    tm = 256

    def body(y_ref, t_ref, l_ref, d_ref):
        err = y_ref[...] - t_ref[...]
        d_ref[...] = err * (1.0 / C)
        sq = err * err
        part = jnp.zeros((8, LANES), F32)
        for r in range(0, tm, 8):
            for c in range(0, C, LANES):
                part = part + sq[r:r + 8, c:c + LANES]

        @pl.when(pl.program_id(0) == 0)
        def _():
            l_ref[...] = part

        @pl.when(pl.program_id(0) > 0)
        def _():
            l_ref[...] += part

    row = pl.BlockSpec((tm, C), lambda i: (i, 0))
    acc = pl.BlockSpec((8, LANES), lambda i: (0, 0))
    return pl.pallas_call(
        body, name=name, grid=(T // tm,), in_specs=[row, row], out_specs=[acc, row],
        out_shape=[jax.ShapeDtypeStruct((8, LANES), F32), jax.ShapeDtypeStruct((T, C), F32)],
        compiler_params=_cparams("arbitrary"),
    )(y, target)


def _adamw(w, g, m, v, *, name):
    shape = w.shape
    size = int(np.prod(shape))
    cols = LANES if (shape[-1] < LANES and size % LANES == 0) else shape[-1]
    rows = size // cols
    tm = _tile(rows, 256) if rows % 8 == 0 else rows
    w2, g2, m2, v2 = (t.reshape(rows, cols) for t in (w, g, m, v))

    def body(w_ref, g_ref, m_ref, v_ref, d_ref, nm_ref, nv_ref):
        gg = g_ref[...]
        nm = ADAM_B1 * m_ref[...] + (1.0 - ADAM_B1) * gg
        nv = ADAM_B2 * v_ref[...] + (1.0 - ADAM_B2) * (gg * gg)
        m_hat = nm / (1.0 - ADAM_B1 ** ADAM_STEP)
        v_hat = nv / (1.0 - ADAM_B2 ** ADAM_STEP)
        d_ref[...] = -ADAM_LR * (m_hat / (jnp.sqrt(v_hat) + ADAM_EPS) + ADAM_WD * w_ref[...])
        nm_ref[...] = nm
        nv_ref[...] = nv

    blk = pl.BlockSpec((tm, cols), lambda i: (i, 0))
    outs = pl.pallas_call(
        body, name=name, grid=(rows // tm,), in_specs=[blk] * 4, out_specs=[blk] * 3,
        out_shape=[jax.ShapeDtypeStruct((rows, cols), F32)] * 3, compiler_params=_cparams("parallel"),
    )(w2, g2, m2, v2)
    return tuple(o.reshape(shape) for o in outs)


def _all_gather(shard, *, name):
    R, C = shard.shape

    def body(x_ref, out_ref, send_sems, recv_sems, local_sem):
        x, y, c = lax.axis_index("x"), lax.axis_index("y"), lax.axis_index("c")
        me, sibling = (x, y, c), (x, y, 1 - c)
        chips = [(1 - x, y), (x, 1 - y), (1 - x, 1 - y)]

        def rows(px, py, pc):
            return out_ref.at[_logical(px, py, pc)]

        def copy(k, block, to, src=None):
            return pltpu.make_async_remote_copy(
                src_ref=rows(*block) if src is None else src, dst_ref=rows(*block),
                send_sem=send_sems.at[k], recv_sem=recv_sems.at[k], device_id=to, device_id_type=_MESH)

        mine = pltpu.make_async_copy(x_ref, rows(*me), local_sem)
        mine.start()
        first = [copy(0, me, sibling, src=x_ref)]
        first += [copy(1 + j, me, (*chip, c), src=x_ref) for j, chip in enumerate(chips)]
        for cp in first:
            cp.start()
        passed = [copy(4 + j, (*chip, c), sibling) for j, chip in enumerate(chips)]
        for j, chip in enumerate(chips):
            copy(1 + j, (*chip, c), me).wait_recv()
            passed[j].start()
        copy(0, sibling, me).wait_recv()
        for j, chip in enumerate(chips):
            copy(4 + j, (*chip, 1 - c), me).wait_recv()
        for cp in first + passed:
            cp.wait_send()
        mine.wait()

    return pl.pallas_call(
        body, name=name, out_shape=jax.ShapeDtypeStruct((N_DEV, R, C), shard.dtype),
        in_specs=[_HBM], out_specs=_HBM,
        scratch_shapes=[pltpu.SemaphoreType.DMA((7,)), pltpu.SemaphoreType.DMA((7,)), pltpu.SemaphoreType.DMA],
    )(shard)


def _gather_weights(shards, *, name):
    nt = len(shards)

    def body(*refs):
        ins, outs = refs[:nt], refs[nt:2 * nt]
        send_sems, recv_sems, local_sems = refs[2 * nt:]
        x, y, c = lax.axis_index("x"), lax.axis_index("y"), lax.axis_index("c")
        me, sibling = (x, y, c), (x, y, 1 - c)
        chips = [(1 - x, y), (x, 1 - y), (1 - x, 1 - y)]

        def copy(t, k, block, to, src=None):
            rows = outs[t].at[_logical(*block)]
            return pltpu.make_async_remote_copy(
                src_ref=rows if src is None else src, dst_ref=rows,
                send_sem=send_sems.at[t, k], recv_sem=recv_sems.at[t, k], device_id=to, device_id_type=_MESH)

        mine = [pltpu.make_async_copy(ins[t], outs[t].at[_logical(*me)], local_sems.at[t]) for t in range(nt)]
        for cp in mine:
            cp.start()
        started = []
        for t in range(nt):
            started.append(copy(t, 0, me, sibling, src=ins[t]))
            started += [copy(t, 1 + j, me, (*chip, c), src=ins[t]) for j, chip in enumerate(chips)]
        for cp in started:
            cp.start()
        for j, chip in enumerate(chips):
            for t in range(nt):
                copy(t, 1 + j, (*chip, c), me).wait_recv()
                fwd = copy(t, 4 + j, (*chip, c), sibling)
                fwd.start()
                started.append(fwd)
        for t in range(nt):
            copy(t, 0, sibling, me).wait_recv()
            for j, chip in enumerate(chips):
                copy(t, 4 + j, (*chip, 1 - c), me).wait_recv()
        for cp in started:
            cp.wait_send()
        for cp in mine:
            cp.wait()

    return pl.pallas_call(
        body, name=name, out_shape=[jax.ShapeDtypeStruct((N_DEV,) + s.shape, s.dtype) for s in shards],
        in_specs=[_HBM] * nt, out_specs=[_HBM] * nt,
        scratch_shapes=[pltpu.SemaphoreType.DMA((nt, 7)), pltpu.SemaphoreType.DMA((nt, 7)), pltpu.SemaphoreType.DMA((nt,))],
    )(*shards)


def _sum_rows(stacked, *, name):
    _, R, C = stacked.shape
    tr = R
    if N_DEV * R * C * stacked.dtype.itemsize > 12 * 1024 * 1024:
        for cand in range(512, 15, -16):
            if R % cand == 0:
                tr = cand
                break

    def body(s_ref, o_ref):
        acc = s_ref[0].astype(F32)
        for k in range(1, N_DEV):
            acc = acc + s_ref[k].astype(F32)
        o_ref[...] = acc

    return pl.pallas_call(
        body, name=name, grid=(R // tr,),
        in_specs=[pl.BlockSpec((N_DEV, tr, C), lambda i: (0, i, 0))], out_specs=pl.BlockSpec((tr, C), lambda i: (i, 0)),
        out_shape=jax.ShapeDtypeStruct((R, C), F32), compiler_params=_cparams("parallel"),
    )(stacked)


_LARGE = (("ab_w_in", 1, True), ("ab_w_out", 1, False), ("s5_w_glu", 1, True), ("xattn_wq", 2, False),
          ("xattn_wkv", 2, True), ("xattn_wo", 2, False), ("ffn_w_in", 2, True), ("ffn_w_out", 2, False))
_LARGE_KEYS = tuple((n, l) for n, layers, _ in _LARGE for l in range(layers))
_TRANSPOSED = {n: t for n, _, t in _LARGE}


def _owner_major(name, w):
    return w.T if _TRANSPOSED[name] else w


def _lb_from_logits(logits):
    return jnp.cumsum(jax.nn.softmax(logits, axis=0), axis=0)[0:1]


def _local_step(x, mem, target, W, shards=None):
    B, L, _ = x.shape
    T = B * L
    x0 = x.reshape(T, D_MODEL)
    memf = mem.reshape(B * MEM_LEN, D_MODEL)
    nw = W["norm_w"]
    cos2, sin2 = _rope_tables(L)
    W = dict(W)
    G, received = {}, {}

    def mmx(a, b, gather=(), scatter=(), **kw):
        if shards is None or not (gather or scatter):
            return _mm(a, b, **kw)
        out, gathered, got = _mm(a, b, gather=[shards[k] for k in gather],
                                 scatter=[G[k].reshape(N_DEV, -1, D_MODEL) for k in scatter], **kw)
        for k, g in zip(gather, gathered):
            W[k] = g.reshape(-1, D_MODEL)
        for k, r in zip(scatter, got):
            received[k] = r
        return out

    def vec(v):
        return v.reshape(1, -1)

    saved = []
    xin = x0
    for layer in range(2):
        s = {"x0": xin}
        tag = f"l{layer}"
        if layer == 0:
            h1 = _rms_fwd(xin, vec(nw[0, 0]), name="norm_pre_mix_l0", out_dtype=BF16)
        s["h1"] = h1
        if layer == 0:
            lb, lb_vjp = jax.vjp(_lb_from_logits, W["hgrn_lb_logits"])
            onw = W["hgrn_out_norm_w"].reshape(1, A_WIDTH)
            z = mmx(h1, W["ab_w_in", 0], tb=True, name="ab_in",
                    gather=[("ab_w_out", 0), ("xattn_wq", 0), ("xattn_wkv", 0), ("xattn_wo", 0), ("ffn_w_in", 0)])
            oa, o_raw, s_start = _hgrn_fwd(z, lb, onw, B, L, name="hgrn_fwd")
            qr, kr = _rope_fwd(z, cos2, sin2, B, L, name="rope_qk")
            os_, ls_ = [], []
            for dil in B_DILS:
                o_g, l_g = _dil_fwd(qr, kr, z, dil, B, L, name=f"dil_fwd_{dil}")
                os_.append(o_g)
                ls_.append(l_g)
            ob, lse = _dil_combine(os_, ls_, name="dil_combine")
            ymix = jnp.concatenate([oa, ob], axis=-1).astype(BF16)
            y1 = mmx(ymix, W["ab_w_out", 0], out_dtype=BF16, name="ab_out", gather=[("ffn_w_out", 0)])
            s.update(z=z, lb=lb, lb_vjp=lb_vjp, onw=onw, o_raw=o_raw, s_start=s_start, qr=qr, kr=kr, ob=ob, lse=lse, ymix=ymix)
        else:
            p5 = tuple(W[n][0] for n in ("s5_lambda_re", "s5_lambda_im", "s5_log_dt", "s5_b_re", "s5_b_im", "s5_c_re", "s5_c_im"))
            (kern, ez, fz, _, _), s5_vjp = jax.vjp(_s5_build, *p5)
            tab_r, tab_i = _s5_scan_tables(p5[0], p5[1], p5[2], int(math.log2(L // C_TC)))
            kbd, lag_vjp = jax.vjp(_s5_lag_blocks, kern)
            mats = tuple(_s5_dense(kbd, ez, fz, name="s5_maps")) + (tab_r, tab_i)
            dsk = W["s5_d"].reshape(1, D_MODEL)
            gl, ypre, x8, xs = _s5_fwd(h1, dsk, *mats, B, L, name="s5_fwd")
            w_glu = _interleave_rows(W["s5_w_glu", 0])
            zg, y1 = _mm(gl, w_glu, tb=True, name="s5_glu_in", gate=("fwd", "glu"))
            s.update(s5_vjp=s5_vjp, lag_vjp=lag_vjp, mats=mats, x8=x8, xs=xs, dsk=dsk, gl=gl, ypre=ypre, zg=zg, w_glu=w_glu)
        x1, h2 = _rms_fwd(y1, vec(nw[layer, 1]), xin, name=f"norm_post_mix_{tag}", then=(vec(nw[layer, 2]), BF16))
        memn = _rms_fwd(memf, vec(W["mem_norm_w"][layer]), name=f"norm_mem_{tag}", out_dtype=BF16)
        q = mmx(h2, W["xattn_wq", layer], out_dtype=BF16, name=f"x_q_{tag}", gather=[("s5_w_glu", 0)] if layer == 0 else [])
        kv = _mm(memn, W["xattn_wkv", layer], tb=True, out_dtype=BF16, name=f"x_kv_{tag}")
        o = _xattn_fwd(q, kv, B, L, name=f"x_attn_{tag}")
        y2 = mmx(o, W["xattn_wo", layer], out_dtype=BF16, name=f"x_o_{tag}", gather=[("xattn_wq", 1), ("xattn_wo", 1)] if layer == 0 else [])
        x2, h3 = _rms_fwd(y2, vec(nw[layer, 3]), x1, name=f"norm_post_x_{tag}", then=(vec(nw[layer, 4]), BF16))
        w_ffn_in = _interleave_rows(W["ffn_w_in", layer])
        zf, u = mmx(h3, w_ffn_in, tb=True, name=f"ffn_in_{tag}", gate=("fwd", "swiglu"),
                    gather=[("xattn_wkv", 1), ("ffn_w_in", 1), ("ffn_w_out", 1)] if layer == 0 else [])
        y3 = _mm(u, W["ffn_w_out", layer], out_dtype=BF16, name=f"ffn_out_{tag}")
        if layer == 0:
            x3, h1 = _rms_fwd(y3, vec(nw[0, 5]), x2, name="norm_post_ffn_l0", then=(vec(nw[1, 0]), F32))
        else:
            x3 = _rms_fwd(y3, vec(nw[layer, 5]), x2, name=f"norm_post_ffn_{tag}")
        s.update(y1=y1, x1=x1, h2=h2, memn=memn, q=q, kv=kv, o=o, y2=y2, x2=x2, h3=h3, zf=zf, u=u, y3=y3, w_ffn_in=w_ffn_in)
        saved.append(s)
        xin = x3

    loss_parts, dx = _loss_head(xin, target.reshape(T, D_MODEL), name="loss_head")

    d_norm = [[None] * 6 for _ in range(2)]
    d_memn = [None, None]
    for layer in (1, 0):
        s = saved[layer]
        tag = f"l{layer}"
        if layer == 1:
            dy3, d_norm[1][5] = _rms_bwd(s["y3"], vec(nw[1, 5]), dx, name="bnorm_post_ffn_l1", out_dtype=BF16)
        dzf = mmx(dy3, W["ffn_w_out", layer], tb=True, name=f"b_ffn_out_dx_{tag}", gate=("bwd", "swiglu", s["zf"]),
                  scatter=[("xattn_wkv", 1), ("s5_w_glu", 0)] if layer == 0 else [])
        G["ffn_w_out", layer] = _mm(s["u"], dy3, ta=True, out_dtype=BF16, name=f"b_ffn_out_dw_{tag}")
        G["ffn_w_in", layer] = _interleave_rows(
            mmx(dzf, s["h3"], ta=True, out_dtype=BF16, name=f"b_ffn_in_dw_{tag}", scatter=[("ffn_w_out", layer)]), inverse=True)
        dh3 = mmx(dzf, s["w_ffn_in"], out_dtype=BF16, name=f"b_ffn_in_dx_{tag}", scatter=[("ffn_w_in", layer)])
        dx, d_norm[layer][4], dy2, d_norm[layer][3] = _rms_bwd(
            s["x2"], vec(nw[layer, 4]), dh3, dx, name=f"bnorm_pre_ffn_{tag}", then=(s["y2"], vec(nw[layer, 3])))
        do = _mm(dy2, W["xattn_wo", layer], tb=True, out_dtype=BF16, name=f"b_x_o_dx_{tag}")
        G["xattn_wo", layer] = _mm(s["o"], dy2, ta=True, out_dtype=BF16, name=f"b_x_o_dw_{tag}")
        dq, dkv = _xattn_bwd(s["q"], s["kv"], do, B, L, name=f"b_x_attn_{tag}")
        G["xattn_wq", layer] = _mm(s["h2"], dq, ta=True, out_dtype=BF16, name=f"b_x_q_dw_{tag}")
        dh2 = _mm(dq, W["xattn_wq", layer], tb=True, out_dtype=BF16, name=f"b_x_q_dx_{tag}")
        G["xattn_wkv", layer] = _mm(dkv, s["memn"], ta=True, out_dtype=BF16, name=f"b_x_kv_dw_{tag}")
        dmemn = _mm(dkv, W["xattn_wkv", layer], out_dtype=BF16, name=f"b_x_kv_dx_{tag}")
        _, d_memn[layer] = _rms_bwd(memf, vec(W["mem_norm_w"][layer]), dmemn, name=f"bnorm_mem_{tag}", out_dtype=BF16)
        dx, d_norm[layer][2], dy1, d_norm[layer][1] = _rms_bwd(
            s["x1"], vec(nw[layer, 2]), dh2, dx, name=f"bnorm_pre_x_{tag}", then=(s["y1"], vec(nw[layer, 1])))
        if layer == 0:
            z = s["z"]
            dymix = _mm(dy1, W["ab_w_out", 0], tb=True, name="b_ab_out_dx")
            G["ab_w_out", 0] = _mm(s["ymix"], dy1, ta=True, out_dtype=BF16, name="b_ab_out_dw")
            dqa, dfa, dia, dga, d_onw, d_lb = _hgrn_bwd(z, s["lb"], s["onw"], s["o_raw"], s["s_start"], dymix, B, L, name="hgrn_bwd")
            dqs, dks, dvs = [], [], []
            for dil in B_DILS:
                dqs.append(_dil_bwd_q(s["qr"], s["kr"], z, dymix, s["ob"], s["lse"], dil, B, L, name=f"dil_bwd_q_{dil}"))
                dk_g, dv_g = _dil_bwd_kv(s["qr"], s["kr"], z, dymix, s["ob"], s["lse"], dil, B, L, name=f"dil_bwd_kv_{dil}")
                dks.append(dk_g)
                dvs.append(dv_g)
            dqkv = _rope_bwd(dqs, dks, dvs, cos2, sin2, B, L, name="b_rope")
            dz = jnp.concatenate([dqa, dfa, dia, dga, dqkv], axis=-1).astype(BF16)
            G["ab_w_in", 0] = mmx(dz, s["h1"], ta=True, out_dtype=BF16, name="b_ab_in_dw",
                                  scatter=[("xattn_wo", 0), ("xattn_wq", 0), ("xattn_wkv", 0), ("ab_w_out", 0)])
            dh1 = mmx(dz, W["ab_w_in", 0], out_dtype=BF16, name="b_ab_in_dx", scatter=[("ab_w_in", 0)])
            G["hgrn_out_norm_w"] = jnp.sum(d_onw.reshape(B, A_WIDTH), axis=0, keepdims=True)
            d_lb_row = jnp.sum(d_lb.reshape(B, A_WIDTH), axis=0, keepdims=True)
            G["hgrn_lb_logits"] = s["lb_vjp"](d_lb_row)[0]
        else:
            dzg = _gated_bwd(s["zg"], dy1, "glu", name="b_s5_glu")
            G["s5_w_glu", 0] = _interleave_rows(
                _mm(dzg, s["gl"], ta=True, out_dtype=BF16, name="b_s5_glu_dw"), inverse=True)
            dgl = mmx(dzg, s["w_glu"], name="b_s5_glu_dx", scatter=[("xattn_wo", 1), ("xattn_wq", 1)])
            dh1, dy8, de_in, da, d_dsk = _s5_bwd(dgl, s["ypre"], s["h1"], s["dsk"], s["xs"], *s["mats"], B, L, name="s5_bwd")
            dkbd = _bmm_tn(s["x8"], dy8, C_NB, "lags", name="s5_bwd_dm").transpose(1, 0, 2, 3)
            dfz = _bmm_tn(s["xs"], dy8, C_NB, "f", name="s5_bwd_df").reshape(C_NB, 2, C_GB, C_STATE, C_TC, C_GROUP)
            dez = _bmm_tn(s["x8"], de_in, C_NB, "e", name="s5_bwd_de").reshape(C_NB, C_TC, C_GB, C_GROUP, 2, C_STATE)
            half = C_S8 // 2
            da_r = (da[:, 0, :half] + da[:, 0, half:]).reshape(C_GROUPS, C_STATE)
            da_i = (da[:, 1, half:] - da[:, 1, :half]).reshape(C_GROUPS, C_STATE)
            gp = s["s5_vjp"](s["lag_vjp"](dkbd) + (dez, dfz, da_r, da_i))
            for n, gv in zip(("s5_lambda_re", "s5_lambda_im", "s5_log_dt", "s5_b_re", "s5_b_im", "s5_c_re", "s5_c_im"), gp):
                G[n] = gv[None]
            G["s5_d"] = d_dsk
        if layer == 1:
            dx, d_norm[1][0], dy3, d_norm[0][5] = _rms_bwd(
                s["x0"], vec(nw[1, 0]), dh1, dx, name="bnorm_pre_mix_l1", then=(saved[0]["y3"], vec(nw[0, 5])))
        else:
            dx, d_norm[0][0] = _rms_bwd(s["x0"], vec(nw[0, 0]), dh1, dx, name="bnorm_pre_mix_l0")

    G["norm_w"] = jnp.stack([jnp.concatenate(d_norm[l], axis=0) for l in range(2)])
    G["mem_norm_w"] = jnp.concatenate(d_memn, axis=0)
    if shards is not None:
        G.update(received)
    return loss_parts, dx.reshape(B, L, D_MODEL), G


_SMALL = (("norm_w", (2, 6, 1024)), ("mem_norm_w", (2, 1024)), ("hgrn_lb_logits", (3, 512)), ("hgrn_out_norm_w", (1, 512)),
          ("s5_lambda_re", (1, 64, 64)), ("s5_lambda_im", (1, 64, 64)), ("s5_log_dt", (1, 64)),
          ("s5_b_re", (1, 64, 64, 16)), ("s5_b_im", (1, 64, 64, 16)), ("s5_c_re", (1, 64, 16, 64)),
          ("s5_c_im", (1, 64, 16, 64)), ("s5_d", (1, 1024)))

_WEIGHT_ORDER = ('norm_w', 'mem_norm_w', 'ab_w_in', 'ab_w_out', 'hgrn_lb_logits', 'hgrn_out_norm_w', 's5_lambda_re',
                 's5_lambda_im', 's5_log_dt', 's5_b_re', 's5_b_im', 's5_c_re', 's5_c_im', 's5_d', 's5_w_glu', 'xattn_wq',
                 'xattn_wkv', 'xattn_wo', 'ffn_w_in', 'ffn_w_out')


def kernel(x, mem, norm_w, mem_norm_w, ab_w_in, ab_w_out, hgrn_lb_logits, hgrn_out_norm_w, s5_lambda_re, s5_lambda_im, s5_log_dt, s5_b_re, s5_b_im, s5_c_re, s5_c_im, s5_d, s5_w_glu, xattn_wq, xattn_wkv, xattn_wo, ffn_w_in, ffn_w_out, loss_target, m_norm_w, m_mem_norm_w, m_ab_w_in, m_ab_w_out, m_hgrn_lb_logits, m_hgrn_out_norm_w, m_s5_lambda_re, m_s5_lambda_im, m_s5_log_dt, m_s5_b_re, m_s5_b_im, m_s5_c_re, m_s5_c_im, m_s5_d, m_s5_w_glu, m_xattn_wq, m_xattn_wkv, m_xattn_wo, m_ffn_w_in, m_ffn_w_out, v_norm_w, v_mem_norm_w, v_ab_w_in, v_ab_w_out, v_hgrn_lb_logits, v_hgrn_out_norm_w, v_s5_lambda_re, v_s5_lambda_im, v_s5_log_dt, v_s5_b_re, v_s5_b_im, v_s5_c_re, v_s5_c_im, v_s5_d, v_s5_w_glu, v_xattn_wq, v_xattn_wkv, v_xattn_wo, v_ffn_w_in, v_ffn_w_out):
    local = dict(norm_w=norm_w, mem_norm_w=mem_norm_w, ab_w_in=ab_w_in, ab_w_out=ab_w_out, hgrn_lb_logits=hgrn_lb_logits,
                 hgrn_out_norm_w=hgrn_out_norm_w, s5_lambda_re=s5_lambda_re, s5_lambda_im=s5_lambda_im, s5_log_dt=s5_log_dt,
                 s5_b_re=s5_b_re, s5_b_im=s5_b_im, s5_c_re=s5_c_re, s5_c_im=s5_c_im, s5_d=s5_d, s5_w_glu=s5_w_glu,
                 xattn_wq=xattn_wq, xattn_wkv=xattn_wkv, xattn_wo=xattn_wo, ffn_w_in=ffn_w_in, ffn_w_out=ffn_w_out)
    mom_m = dict(zip(_WEIGHT_ORDER, (m_norm_w, m_mem_norm_w, m_ab_w_in, m_ab_w_out, m_hgrn_lb_logits, m_hgrn_out_norm_w, m_s5_lambda_re, m_s5_lambda_im, m_s5_log_dt, m_s5_b_re, m_s5_b_im, m_s5_c_re, m_s5_c_im, m_s5_d, m_s5_w_glu, m_xattn_wq, m_xattn_wkv, m_xattn_wo, m_ffn_w_in, m_ffn_w_out)))
    mom_v = dict(zip(_WEIGHT_ORDER, (v_norm_w, v_mem_norm_w, v_ab_w_in, v_ab_w_out, v_hgrn_lb_logits, v_hgrn_out_norm_w, v_s5_lambda_re, v_s5_lambda_im, v_s5_log_dt, v_s5_b_re, v_s5_b_im, v_s5_c_re, v_s5_c_im, v_s5_d, v_s5_w_glu, v_xattn_wq, v_xattn_wkv, v_xattn_wo, v_ffn_w_in, v_ffn_w_out)))
    dev = 4 * lax.axis_index("x") + 2 * lax.axis_index("y") + lax.axis_index("c")

    shards = {(n, l): _owner_major(n, local[n][l]).astype(BF16) for n, l in _LARGE_KEYS}
    first = ("ab_w_in", 0)
    W = {first: _gather_weights([shards[first]], name="gather_first")[0].reshape(-1, D_MODEL)}
    tiny =jnp.concatenate([norm_w.reshape(-1), s5_d.reshape(-1)])
    tiny = jnp.pad(tiny, (0, 16 * LANES - tiny.shape[0])).reshape(16, LANES)
    tiny_all = _all_gather(tiny, name="gather_tiny").reshape(N_DEV, 16 * LANES)
    W["norm_w"] = tiny_all[:, :12 * LANES].reshape(N_DEV, 2, 6, LANES).transpose(1, 2, 0, 3).reshape(2, 6, D_MODEL)
    W["s5_d"] = tiny_all[:, 12 * LANES:13 * LANES].reshape(1, D_MODEL)
    for n in ("mem_norm_w", "hgrn_lb_logits", "hgrn_out_norm_w", "s5_lambda_re", "s5_lambda_im", "s5_log_dt",
              "s5_b_re", "s5_b_im", "s5_c_re", "s5_c_im"):
        W[n] = local[n]

    loss_parts, grad_x, G = _local_step(x, mem, loss_target, W, shards)

    g_layers = {}
    for n, l in _LARGE_KEYS:
        g = _sum_rows(G[n, l], name=f"sum_grads_{n}_{l}")
        g_layers.setdefault(n, []).append(g.T if _TRANSPOSED[n] else g)
    g_local = {n: jnp.stack(gl) for n, gl in g_layers.items()}
    small =jnp.concatenate([G[n].reshape(-1) for n, _ in _SMALL] + [0.5 / D_MODEL * jnp.sum(loss_parts).reshape(1)])
    n_small = small.shape[0]
    small = jnp.pad(small, (0, (-n_small) % (8 * LANES))).reshape(-1, LANES)
    small_sum = _sum_rows(_all_gather(small, name="gather_small"), name="sum_small").reshape(-1)
    g_full, off = {}, 0
    for n, shp in _SMALL:
        size = int(np.prod(shp))
        g_full[n] = small_sum[off:off + size].reshape(shp)
        off += size
    loss = small_sum[off]
    grads = dict(g_local)
    for n, shp in _SMALL:
        if n == "norm_w":
            grads[n] = lax.dynamic_slice_in_dim(g_full[n], dev * LANES, LANES, axis=2)
        elif n == "s5_d":
            grads[n] = lax.dynamic_slice_in_dim(g_full[n], dev * LANES, LANES, axis=1)
        else:
            grads[n] = g_full[n]

    delta, new_m, new_v = {}, {}, {}
    for n in _WEIGHT_ORDER:
        delta[n], new_m[n], new_v[n] = _adamw(local[n], grads[n], mom_m[n], mom_v[n], name=f"adamw_{n}")
    return (loss, grad_x, *[grads[n] for n in _WEIGHT_ORDER], *[delta[n] for n in _WEIGHT_ORDER],
            *[new_m[n] for n in _WEIGHT_ORDER], *[new_v[n] for n in _WEIGHT_ORDER])
```

```python
import functools
import math

import numpy as np
import jax
import jax.numpy as jnp
from jax import lax
from jax.experimental import pallas as pl
from jax.experimental.pallas import tpu as pltpu

F32 = jnp.float32
BF16 = jnp.bfloat16
HI = lax.Precision.HIGHEST

D_MODEL = 1024
NORM_EPS = 1e-6
A_WIDTH = 512
A_HEAD = 128
A_CHUNK = 32
A_SUPER = 256
B_SPAN = 128
B_DILS = (1, 4, 16)
ROPE_THETA = 10000.0
C_GROUPS = 64
C_GROUP = 16
C_STATE = 64
C_TC = 8
C_MIN_NEG_RE = -1e-4
MEM_LEN = 256
X_HEADS = 4
X_HD = 256
D_FF = 2816
N_DEV = 8
LANES = 128

ADAM_LR, ADAM_B1, ADAM_B2, ADAM_EPS, ADAM_WD, ADAM_STEP = 0.001, 0.9, 0.999, 1e-08, 0.01, 10

NEG_BIG = -1e30


def _tile(n, pref):
    for d in range(min(pref, n) // LANES * LANES, 0, -LANES):
        if n % d == 0:
            return d
    return n


def _cparams(*sem):
    return pltpu.CompilerParams(dimension_semantics=sem, vmem_limit_bytes=56 * 1024 * 1024)


def _sigmoid(x):
    return 0.5 * jnp.tanh(0.5 * x) + 0.5


def _erf(x):
    ax = jnp.abs(x)
    t = 1.0 / (1.0 + 0.3275911 * ax)
    poly = t * (0.254829592 + t * (-0.284496736 + t * (1.421413741 + t * (-1.453152027 + t * 1.061405429))))
    y = 1.0 - poly * jnp.exp(-ax * ax)
    return jnp.where(x < 0, -y, y)


_HBM = pl.BlockSpec(memory_space=pltpu.HBM)
_MESH = pl.DeviceIdType.MESH


def _logical(px, py, pc):
    return 4 * px + 2 * py + pc


class _Exchange:
    def __init__(self, gather=(), scatter=()):
        self.gather, self.scatter = list(gather), list(scatter)
        self.ng, self.n = len(self.gather), len(self.gather) + len(self.scatter)

    def operands(self):
        return self.gather + self.scatter

    def in_specs(self):
        return [_HBM] * self.n

    def out_shapes(self):
        return ([jax.ShapeDtypeStruct((N_DEV,) + g.shape, g.dtype) for g in self.gather]
                + [jax.ShapeDtypeStruct(s.shape, s.dtype) for s in self.scatter])

    def scratch(self):
        if not self.n:
            return []
        return [pltpu.SemaphoreType.DMA((self.n, 7)), pltpu.SemaphoreType.DMA((self.n, 7)), pltpu.SemaphoreType.DMA((self.n,))]

    def split(self, results):
        return list(results[:self.ng]), list(results[self.ng:])

    def run(self, ins, outs, sems, first, last):
        if not self.n:
            return
        send_sems, recv_sems, local_sems = sems
        x, y, c = lax.axis_index("x"), lax.axis_index("y"), lax.axis_index("c")
        me, sibling = _logical(x, y, c), (x, y, 1 - c)
        chips = [(1 - x, y), (x, 1 - y), (1 - x, 1 - y)]
        peers = [(x ^ (k >> 2), y ^ ((k >> 1) & 1), c ^ (k & 1)) for k in range(1, N_DEV)]

        def remote(t, k, src, dst, to):
            return pltpu.make_async_remote_copy(src_ref=src, dst_ref=dst, send_sem=send_sems.at[t, k],
                                                recv_sem=recv_sems.at[t, k], device_id=to, device_id_type=_MESH)

        def local(t):
            src = ins[t] if t < self.ng else ins[t].at[me]
            return pltpu.make_async_copy(src, outs[t].at[me], local_sems.at[t])

        @pl.when(first)
        def _():
            for t in range(self.n):
                local(t).start()
                if t < self.ng:
                    remote(t, 0, ins[t], outs[t].at[me], sibling).start()
                    for j, chip in enumerate(chips):
                        remote(t, 1 + j, ins[t], outs[t].at[me], (*chip, c)).start()
                else:
                    for k, peer in enumerate(peers):
                        remote(t, k, ins[t].at[_logical(*peer)], outs[t].at[me], peer).start()

        @pl.when(last)
        def _():
            for j, chip in enumerate(chips):
                for t in range(self.ng):
                    landed = outs[t].at[_logical(*chip, c)]
                    remote(t, 1 + j, ins[t], landed, sibling).wait_recv()
                    remote(t, 4 + j, landed, landed, sibling).start()
            for t in range(self.n):
                if t < self.ng:
                    remote(t, 0, ins[t], outs[t].at[_logical(*sibling)], sibling).wait_recv()
                    for j, chip in enumerate(chips):
                        remote(t, 4 + j, ins[t], outs[t].at[_logical(*chip, 1 - c)], sibling).wait_recv()
                    for k in range(7):
                        remote(t, k, ins[t], outs[t].at[me], sibling).wait_send()
                else:
                    for k, peer in enumerate(peers):
                        remote(t, k, ins[t].at[me], outs[t].at[_logical(*peer)], peer).wait_recv()
                    for k, peer in enumerate(peers):
                        remote(t, k, ins[t].at[_logical(*peer)], outs[t].at[me], peer).wait_send()
                local(t).wait()


def _carrying(body, n_in, n_out, ex, grid):
    n_sems = len(ex.scratch())

    def wrapped(*refs):
        ins, ex_in = refs[:n_in], refs[n_in:n_in + ex.n]
        outs = refs[n_in + ex.n:n_in + ex.n + n_out]
        ex_out = refs[n_in + ex.n + n_out:n_in + 2 * ex.n + n_out]
        rest = refs[n_in + 2 * ex.n + n_out:]
        scratch, sems = rest[:len(rest) - n_sems], rest[len(rest) - n_sems:]
        ids = [pl.program_id(a) for a in range(len(grid))]
        first = functools.reduce(lambda u, v: u & v, [i == 0 for i in ids])
        last = functools.reduce(lambda u, v: u & v, [i == g - 1 for i, g in zip(ids, grid)])
        ex.run(ex_in, ex_out, sems, first, last)
        body(*ins, *outs, *scratch)

    return wrapped


_MM_VMEM_BUDGET = 36 * 1024 * 1024


def _mm(a, b, *, ta=False, tb=False, out_dtype=F32, name, tiles=(1408, 1408, 4096), gather=(), scatter=(), gate=None):
    M, K = (a.shape[1], a.shape[0]) if ta else a.shape
    N = b.shape[0] if tb else b.shape[1]
    assert (b.shape[1] if tb else b.shape[0]) == K
    gate_mode = gate[0] if gate else None
    tm, tn, tk = _tile(M, tiles[0]), _tile(N, tiles[1]), _tile(K, tiles[2])
    if gate_mode == "fwd":
        bs = _gate_block(N // 2)
        tn = 2 * bs
    elif gate_mode == "bwd":
        bs = _gate_block(N)
        tn = bs

    def vmem_bytes():
        acc = 4 * tm * tn if tk < K else 0
        if gate_mode == "fwd":
            io = (2 * (2 + 1) + 4) * tm * tn
        elif gate_mode == "bwd":
            io = (2 * (4 + 4) + 4) * tm * tn
        else:
            io = 2 * tm * tn * jnp.dtype(out_dtype).itemsize
        return 2 * 2 * (tm * tk + tk * tn) + acc + io

    while vmem_bytes() > _MM_VMEM_BUDGET:
        if gate_mode and tm > 256:
            tm = _tile(M, tm - LANES)
        elif tk > 512:
            tk = _tile(K, tk - LANES)
        elif tn > 256 and not gate_mode:
            tn = _tile(N, tn - LANES)
        else:
            tm = _tile(M, tm - LANES)
    ni, nj, nk = M // tm, N // tn, K // tk
    ex = _Exchange(gather, scatter)

    a_spec = pl.BlockSpec((tk, tm), lambda i, j, k: (k, i)) if ta else pl.BlockSpec((tm, tk), lambda i, j, k: (i, k))
    b_spec = pl.BlockSpec((tn, tk), lambda i, j, k: (j, k)) if tb else pl.BlockSpec((tk, tn), lambda i, j, k: (k, j))
    dims = (((0 if ta else 1,), (1 if tb else 0,)), ((), ()))
    n_acc = 1 if nk > 1 else 0
    n_in = 3 if gate_mode == "bwd" else 2
    n_out = 2 if gate_mode == "fwd" else 1

    def finish(val, in_refs, out_refs, rows=slice(None)):
        if gate_mode is None:
            out_refs[0][rows, :] = val.astype(out_refs[0].dtype)
        elif gate_mode == "fwd":
            out_refs[0][rows, :] = val.astype(BF16)
            for p in range(tn // (2 * bs)):
                a_, b_ = val[:, 2 * p * bs:(2 * p + 1) * bs], val[:, (2 * p + 1) * bs:(2 * p + 2) * bs]
                out_refs[1][rows, p * bs:(p + 1) * bs] = _gate_value(a_, b_, gate[1]).astype(BF16)
        else:
            z_ref = in_refs[2]
            for p in range(tn // bs):
                a_ = z_ref[rows, 2 * p * bs:(2 * p + 1) * bs].astype(F32)
                b_ = z_ref[rows, (2 * p + 1) * bs:(2 * p + 2) * bs].astype(F32)
                da, db = _gate_grads(a_, b_, val[:, p * bs:(p + 1) * bs], gate[1])
                out_refs[0][rows, 2 * p * bs:(2 * p + 1) * bs] = da.astype(BF16)
                out_refs[0][rows, (2 * p + 1) * bs:(2 * p + 2) * bs] = db.astype(BF16)

    halves = 2 if (gate_mode and nk == 1 and not ta and tm % 32 == 0) else 1

    def body(*refs):
        in_refs, rest = refs[:n_in], refs[n_in:]
        ex_in, out_refs = rest[:ex.n], rest[ex.n:ex.n + n_out]
        ex_out, scratch = rest[ex.n + n_out:2 * ex.n + n_out], rest[2 * ex.n + n_out:]
        i, j, k = pl.program_id(0), pl.program_id(1), pl.program_id(2)
        ex.run(ex_in, ex_out, scratch[n_acc:], (i == 0) & (j == 0) & (k == 0), (i == ni - 1) & (j == nj - 1) & (k == nk - 1))
        if halves > 1:
            rh = tm // halves
            rhs = in_refs[1][...].astype(BF16)
            parts = [lax.dot_general(in_refs[0][r * rh:(r + 1) * rh, :].astype(BF16), rhs, dims, preferred_element_type=F32)
                     for r in range(halves)]
            for r, p in enumerate(parts):
                finish(p, in_refs, out_refs, slice(r * rh, (r + 1) * rh))
            return
        part = lax.dot_general(in_refs[0][...].astype(BF16), in_refs[1][...].astype(BF16), dims, preferred_element_type=F32)
        if nk == 1:
            finish(part, in_refs, out_refs)
            return
        acc_ref = scratch[0]

        @pl.when(k == 0)
        def _():
            acc_ref[...] = part

        @pl.when(k > 0)
        def _():
            acc_ref[...] += part

        @pl.when(k == nk - 1)
        def _():
            finish(acc_ref[...], in_refs, out_refs)

    tile = lambda width: pl.BlockSpec((tm, width), lambda i, j, k: (i, j))
    if gate_mode == "fwd":
        out_specs, out_shape = [tile(tn), tile(tn // 2)], [jax.ShapeDtypeStruct((M, N), BF16), jax.ShapeDtypeStruct((M, N // 2), BF16)]
    elif gate_mode == "bwd":
        out_specs, out_shape = [tile(2 * tn)], [jax.ShapeDtypeStruct((M, 2 * N), BF16)]
    else:
        out_specs, out_shape = [tile(tn)], [jax.ShapeDtypeStruct((M, N), out_dtype)]
    operands = [a, b] + ([gate[2]] if gate_mode == "bwd" else [])
    sem = ("arbitrary",) * 3 if ex.n else ("parallel", "parallel", "arbitrary")
    res = pl.pallas_call(
        body, name=name, grid=(ni, nj, nk),
        in_specs=[a_spec, b_spec] + ([tile(2 * tn)] if gate_mode == "bwd" else []) + ex.in_specs(),
        out_specs=out_specs + ex.in_specs(),
        out_shape=out_shape + ex.out_shapes(),
        scratch_shapes=([pltpu.VMEM((tm, tn), F32)] if nk > 1 else []) + ex.scratch(),
        compiler_params=_cparams(*sem),
    )(*operands, *ex.operands())
    main = res[0] if n_out == 1 else tuple(res[:n_out])
    if not ex.n:
        return main
    return (main,) + tuple(ex.split(res[n_out:]))


def _rms_fwd(x, w, res=None, *, name, out_dtype=F32, then=None):
    T, C = x.shape
    tm = _tile(T, 512)
    has_res = res is not None

    def norm(v, w_ref):
        return v * lax.rsqrt(jnp.mean(v * v, axis=-1, keepdims=True) + NORM_EPS) * w_ref[...]

    def body(*refs):
        x_ref, w_ref = refs[0], refs[1]
        y = norm(x_ref[...].astype(F32), w_ref)
        if has_res:
            y = y + refs[2][...]
        if then is None:
            refs[-1][...] = y.astype(refs[-1].dtype)
        else:
            refs[-2][...] = y.astype(refs[-2].dtype)
            refs[-1][...] = norm(y, refs[-3]).astype(refs[-1].dtype)

    row = pl.BlockSpec((tm, C), lambda i: (i, 0))
    vec = pl.BlockSpec((1, C), lambda i: (0, 0))
    ins = [x, w] + ([res] if has_res else []) + ([then[0]] if then else [])
    in_specs = [row, vec] + ([row] if has_res else []) + ([vec] if then else [])
    out_shape = [jax.ShapeDtypeStruct((T, C), out_dtype)] + ([jax.ShapeDtypeStruct((T, C), then[1])] if then else [])
    res_ = pl.pallas_call(
        body, name=name, grid=(T // tm,), in_specs=in_specs, out_specs=[row] * len(out_shape),
        out_shape=out_shape, compiler_params=_cparams("parallel"),
    )(*ins)
    return tuple(res_) if then else res_[0]


def _rms_bwd(x, w, dy, add=None, *, name, out_dtype=F32, then=None):
    T, C = x.shape
    tm = _tile(T, 512)
    has_add = add is not None
    n_in = 3 + has_add + (2 if then else 0)

    def grads(x_ref, w_ref, g, dw_ref):
        xv = x_ref[...].astype(F32)
        r = lax.rsqrt(jnp.mean(xv * xv, axis=-1, keepdims=True) + NORM_EPS)
        xh = xv * r
        part = jnp.sum(g * xh, axis=0, keepdims=True)

        @pl.when(pl.program_id(0) == 0)
        def _():
            dw_ref[...] = part

        @pl.when(pl.program_id(0) > 0)
        def _():
            dw_ref[...] += part

        gx = g * w_ref[...]
        return r * (gx - xh * jnp.mean(gx * xh, axis=-1, keepdims=True))

    def body(*refs):
        ins, outs = refs[:n_in], refs[n_in:]
        dx = grads(ins[0], ins[1], ins[2][...].astype(F32), outs[1])
        if has_add:
            dx = dx + ins[3][...]
        outs[0][...] = dx.astype(outs[0].dtype)
        if then:
            outs[2][...] = grads(ins[-2], ins[-1], dx, outs[3]).astype(BF16)

    row = pl.BlockSpec((tm, C), lambda i: (i, 0))
    vec = pl.BlockSpec((1, C), lambda i: (0, 0))
    ins = [x, w, dy] + ([add] if has_add else []) + (list(then) if then else [])
    big, small = jax.ShapeDtypeStruct((T, C), out_dtype), jax.ShapeDtypeStruct((1, C), F32)
    return pl.pallas_call(
        body, name=name, grid=(T // tm,),
        in_specs=[row, vec, row] + ([row] if has_add else []) + ([row, vec] if then else []),
        out_specs=[row, vec] + ([row, vec] if then else []),
        out_shape=[big, small] + ([jax.ShapeDtypeStruct((T, C), BF16), small] if then else []),
        compiler_params=_cparams("arbitrary"),
    )(*ins)


def _gate_block(width):
    return _tile(width, 1408)


def _gate_value(a, b, kind):
    return a * _sigmoid(a) * b if kind == "swiglu" else a * _sigmoid(b)


def _gate_grads(a, b, d, kind):
    if kind == "swiglu":
        s = _sigmoid(a)
        return d * b * (s * (1.0 + a * (1.0 - s))), d * a * s
    s = _sigmoid(b)
    return d * s, d * a * s * (1.0 - s)


def _interleave_rows(w, inverse=False):
    W2, C = w.shape
    bs = _gate_block(W2 // 2)
    nb = W2 // 2 // bs
    shape = (nb, 2, bs, C) if inverse else (2, nb, bs, C)
    return w.reshape(shape).transpose(1, 0, 2, 3).reshape(W2, C)


def _gated_bwd(z, dout, kind, *, name):
    T, W2 = z.shape
    W = W2 // 2
    tm, bs = _tile(T, 512), _gate_block(W)

    def body(z_ref, d_ref, o_ref):
        da, db = _gate_grads(z_ref[:, :bs].astype(F32), z_ref[:, bs:].astype(F32), d_ref[...].astype(F32), kind)
        o_ref[:, :bs] = da.astype(o_ref.dtype)
        o_ref[:, bs:] = db.astype(o_ref.dtype)

    return pl.pallas_call(
        body, name=name, grid=(T // tm, W // bs),
        in_specs=[pl.BlockSpec((tm, 2 * bs), lambda i, j: (i, j)), pl.BlockSpec((tm, bs), lambda i, j: (i, j))],
        out_specs=pl.BlockSpec((tm, 2 * bs), lambda i, j: (i, j)),
        out_shape=jax.ShapeDtypeStruct((T, W2), BF16), compiler_params=_cparams("parallel", "parallel"),
    )(z, dout)


_NT = (((1,), (1,)), ((), ()))
_TN = (((0,), (0,)), ((), ()))


def _dot(a, b, dims=None, precision=None):
    if dims is None:
        return jnp.dot(a, b, preferred_element_type=F32, precision=precision)
    return lax.dot_general(a, b, dims, preferred_element_type=F32, precision=precision)


def _xattn_fwd(q, kv, B, L, *, name):
    T = q.shape[0]
    tq = 256
    nq = L // tq
    scale = X_HD ** -0.5

    def body(q_ref, k_ref, v_ref, o_ref):
        heads = [slice(h * X_HD, (h + 1) * X_HD) for h in range(X_HEADS)]
        s = [_dot(q_ref[:, sl].astype(BF16), k_ref[:, sl].astype(BF16), _NT) * scale for sl in heads]
        m = [jnp.max(a, axis=-1, keepdims=True) for a in s]
        p = [jnp.exp(a - b) for a, b in zip(s, m)]
        l = [jnp.sum(a, axis=-1, keepdims=True) for a in p]
        o = [_dot(a.astype(BF16), v_ref[:, sl].astype(BF16)) for a, sl in zip(p, heads)]
        for sl, a, b in zip(heads, o, l):
            o_ref[:, sl] = (a / b).astype(BF16)

    return pl.pallas_call(
        body, name=name, grid=(B, nq),
        in_specs=[pl.BlockSpec((tq, D_MODEL), lambda b, i: (b * nq + i, 0)),
                  pl.BlockSpec((MEM_LEN, D_MODEL), lambda b, i: (b, 0)),
                  pl.BlockSpec((MEM_LEN, D_MODEL), lambda b, i: (b, 1))],
        out_specs=pl.BlockSpec((tq, D_MODEL), lambda b, i: (b * nq + i, 0)),
        out_shape=jax.ShapeDtypeStruct((T, D_MODEL), BF16), compiler_params=_cparams("parallel", "parallel"),
    )(q, kv, kv)


def _xattn_bwd(q, kv, do, B, L, *, name):
    T = q.shape[0]
    tq = 256
    nq = L // tq
    scale = X_HD ** -0.5

    def body(q_ref, k_ref, v_ref, do_ref, dq_ref, dkv_ref):
        @pl.when(pl.program_id(1) == 0)
        def _():
            dkv_ref[...] = jnp.zeros_like(dkv_ref)

        heads = [slice(h * X_HD, (h + 1) * X_HD) for h in range(X_HEADS)]
        qs = [q_ref[:, sl].astype(BF16) for sl in heads]
        ks = [k_ref[:, sl].astype(BF16) for sl in heads]
        dos = [do_ref[:, sl].astype(BF16) for sl in heads]
        s = [_dot(a, b, _NT) * scale for a, b in zip(qs, ks)]
        dp = [_dot(a, v_ref[:, sl].astype(BF16), _NT) for a, sl in zip(dos, heads)]
        e = [jnp.exp(a - jnp.max(a, axis=-1, keepdims=True)) for a in s]
        p = [a / jnp.sum(a, axis=-1, keepdims=True) for a in e]
        ds = [(a * (b - jnp.sum(b * a, axis=-1, keepdims=True)) * scale).astype(BF16) for a, b in zip(p, dp)]
        dv = [_dot(a.astype(BF16), b, _TN) for a, b in zip(p, dos)]
        dq = [_dot(a, b) for a, b in zip(ds, ks)]
        dk = [_dot(a, b, _TN) for a, b in zip(ds, qs)]
        for h, sl in enumerate(heads):
            dq_ref[:, sl] = dq[h].astype(BF16)
            dkv_ref[:, sl] += dk[h]
            dkv_ref[:, D_MODEL + h * X_HD:D_MODEL + (h + 1) * X_HD] += dv[h]

    return pl.pallas_call(
        body, name=name, grid=(B, nq),
        in_specs=[pl.BlockSpec((tq, D_MODEL), lambda b, i: (b * nq + i, 0)),
                  pl.BlockSpec((MEM_LEN, D_MODEL), lambda b, i: (b, 0)),
                  pl.BlockSpec((MEM_LEN, D_MODEL), lambda b, i: (b, 1)),
                  pl.BlockSpec((tq, D_MODEL), lambda b, i: (b * nq + i, 0))],
        out_specs=[pl.BlockSpec((tq, D_MODEL), lambda b, i: (b * nq + i, 0)),
                   pl.BlockSpec((MEM_LEN, 2 * D_MODEL), lambda b, i: (b, 0))],
        out_shape=[jax.ShapeDtypeStruct((T, D_MODEL), BF16), jax.ShapeDtypeStruct((B * MEM_LEN, 2 * D_MODEL), F32)],
        compiler_params=_cparams("parallel", "arbitrary"),
    )(q, kv, kv, do)


def _chunk_masks():
    row = lax.broadcasted_iota(jnp.int32, (A_SUPER, A_SUPER), 0)
    col = lax.broadcasted_iota(jnp.int32, (A_SUPER, A_SUPER), 1)
    same = jnp.right_shift(row, 5) == jnp.right_shift(col, 5)
    return same, same & (col <= row), same & (col >= row)


def _dot_mask(mask, x):
    m = mask.astype(BF16)
    hi = x.astype(BF16)
    rest = x - hi.astype(F32)
    mid = rest.astype(BF16)
    lo = (rest - mid.astype(F32)).astype(BF16)
    return _dot(m, hi) + _dot(m, mid) + _dot(m, lo)


def _chunk_row(x, which):
    rows = [x[c * A_CHUNK + which % A_CHUNK:c * A_CHUNK + which % A_CHUNK + 1, :] for c in range(A_SUPER // A_CHUNK)]
    return jnp.concatenate([jnp.broadcast_to(r, (A_CHUNK, x.shape[1])) for r in rows], axis=0)


def _hgrn_gates(fa, lb):
    sig = _sigmoid(fa)
    f = lb + (1.0 - lb) * sig
    return sig, f, jnp.log(f), 1.0 - f


def _hgrn_fwd(z, lb, onw, B, L, *, name, gather=()):
    T = B * L
    ns = L // A_SUPER
    nch = A_SUPER // A_CHUNK
    ex = _Exchange(gather=gather)
    grid = (B, 4, ns)

    def body(q_ref, f_ref, v_ref, g_ref, lb_ref, w_ref, oa_ref, o_ref, s_ref, st_ref, sc_ref):
        @pl.when(pl.program_id(2) == 0)
        def _():
            st_ref[...] = jnp.zeros_like(st_ref)

        s_ref[0] = st_ref[...]
        same, tril, _ = _chunk_masks()
        q, v = q_ref[...], v_ref[...]
        _, _, lf, k = _hgrn_gates(f_ref[...], lb_ref[...])
        bcs = _dot_mask(tril, lf)
        bl = _chunk_row(bcs, -1)
        qd = (q * jnp.exp(bcs)).astype(BF16)
        ki = (k * jnp.exp(-bcs)).astype(BF16)
        ke = (k * jnp.exp(bl - bcs)).astype(BF16)
        dec = jnp.exp(bl)
        vb = v.astype(BF16)
        a = jnp.where(tril, _dot(qd, ki, _NT), 0.0)
        o_ref[...] = _dot(a.astype(BF16), vb)
        chunks = [slice(c * A_CHUNK, (c + 1) * A_CHUNK) for c in range(nch)]
        outer = [_dot(vb[rs], ke[rs], _TN) for rs in chunks]
        st = st_ref[...]
        for c, rs in enumerate(chunks):
            sc_ref[c] = st.astype(BF16)
            st = st * dec[c * A_CHUNK:c * A_CHUNK + 1, :] + outer[c]
        st_ref[...] = st
        for c, rs in enumerate(chunks):
            o_ref[rs, :] += _dot(qd[rs], sc_ref[c], _NT)
        o = o_ref[...]
        r = lax.rsqrt(jnp.mean(o * o, axis=-1, keepdims=True) + NORM_EPS)
        g = g_ref[...]
        oa_ref[...] = o * r * w_ref[...] * (g * _sigmoid(g))

    def zspec(off):
        return pl.BlockSpec((A_SUPER, A_HEAD), lambda b, h, n: (b * ns + n, off + h))

    hvec = pl.BlockSpec((1, A_HEAD), lambda b, h, n: (0, h))
    ospec = pl.BlockSpec((A_SUPER, A_HEAD), lambda b, h, n: (b * ns + n, h))
    res = pl.pallas_call(
        _carrying(body, 6, 3, ex, grid) if ex.n else body, name=name, grid=grid,
        in_specs=[zspec(0), zspec(4), zspec(8), zspec(12), hvec, hvec] + ex.in_specs(),
        out_specs=[ospec, ospec, pl.BlockSpec((1, A_HEAD, A_HEAD), lambda b, h, n: ((b * 4 + h) * ns + n, 0, 0))] + ex.in_specs(),
        out_shape=[jax.ShapeDtypeStruct((T, A_WIDTH), F32), jax.ShapeDtypeStruct((T, A_WIDTH), F32),
                   jax.ShapeDtypeStruct((B * 4 * ns, A_HEAD, A_HEAD), F32)] + ex.out_shapes(),
        scratch_shapes=[pltpu.VMEM((A_HEAD, A_HEAD), F32), pltpu.VMEM((nch, A_HEAD, A_HEAD), BF16)] + ex.scratch(),
        compiler_params=_cparams(*(("arbitrary",) * 3 if ex.n else ("parallel", "parallel", "arbitrary"))),
    )(z, z, z, z, lb, onw, *ex.operands())
    return tuple(res[:3]) + (list(res[3:]),)


def _hgrn_bwd(z, lb, onw, o_raw, s_start, doa, B, L, *, name, scatter=()):
    T = B * L
    ns = L // A_SUPER
    nch = A_SUPER // A_CHUNK
    ex = _Exchange(scatter=scatter)
    grid = (B, 4, ns)

    def body(q_ref, f_ref, v_ref, g_ref, lb_ref, w_ref, o_ref, s_ref, doa_ref,
             dq_ref, df_ref, dv_ref, dg_ref, dw_ref, dlb_ref, dst_ref, sc_ref, dsc_ref, dqd_ref, dke_ref, dblx_ref):
        @pl.when(pl.program_id(2) == 0)
        def _():
            dst_ref[...] = jnp.zeros_like(dst_ref)
            dw_ref[...] = jnp.zeros_like(dw_ref)
            dlb_ref[...] = jnp.zeros_like(dlb_ref)

        same, tril, triu = _chunk_masks()
        q, v, g, lb, w = q_ref[...], v_ref[...], g_ref[...], lb_ref[...], w_ref[...]
        sig, f, lf, k = _hgrn_gates(f_ref[...], lb)
        bcs = _dot_mask(tril, lf)
        bl = _chunk_row(bcs, -1)
        eb, enb, eeb = jnp.exp(bcs), jnp.exp(-bcs), jnp.exp(bl - bcs)
        qd, ki, ke = q * eb, k * enb, k * eeb
        qdb, kib, keb, vb = qd.astype(BF16), ki.astype(BF16), ke.astype(BF16), v.astype(BF16)
        dec = jnp.exp(bl)
        o = o_ref[...]
        r = lax.rsqrt(jnp.mean(o * o, axis=-1, keepdims=True) + NORM_EPS)
        on = o * r
        sg = _sigmoid(g)
        silu_g = g * sg
        doa = doa_ref[...]
        dg_ref[...] = doa * on * w * (sg * (1.0 + g * (1.0 - sg)))
        dw_ref[0] += jnp.sum(doa * on * silu_g, axis=0, keepdims=True)
        don = doa * w * silu_g
        do = r * (don - on * jnp.mean(don * on, axis=-1, keepdims=True))
        dob = do.astype(BF16)
        a = jnp.where(tril, _dot(qdb, kib, _NT), 0.0).astype(BF16)
        da = jnp.where(tril, _dot(dob, vb, _NT), 0.0).astype(BF16)
        dv_ref[...] = _dot(a, dob, _TN)
        dqd_ref[...] = _dot(da, kib)
        dki = _dot(da, qdb, _TN)
        chunks = [slice(c * A_CHUNK, (c + 1) * A_CHUNK) for c in range(nch)]
        outer = [_dot(vb[rs], keb[rs], _TN) for rs in chunks]
        st = s_ref[0]
        for c in range(nch):
            sc_ref[c] = st
            st = st * dec[c * A_CHUNK:c * A_CHUNK + 1, :] + outer[c]
        outer_g = [_dot(dob[rs], qdb[rs], _TN) for rs in chunks]
        dst = dst_ref[...]
        for c in reversed(range(nch)):
            dsc_ref[c] = dst
            dst = dst * dec[c * A_CHUNK:c * A_CHUNK + 1, :] + outer_g[c]
        dst_ref[...] = dst
        for c, rs in enumerate(chunks):
            dec_c = dec[c * A_CHUNK:c * A_CHUNK + 1, :]
            dsc, stc = dsc_ref[c], sc_ref[c]
            dscb = dsc.astype(BF16)
            dv_ref[rs, :] += _dot(keb[rs], dscb, _NT)
            dke_ref[rs, :] = _dot(vb[rs], dscb)
            ddec = jnp.sum(dsc * stc, axis=0, keepdims=True)
            dqd_ref[rs, :] += _dot(dob[rs], stc.astype(BF16))
            dblx_ref[rs, :] = jnp.broadcast_to(ddec * dec_c, (A_CHUNK, A_HEAD))
        dqd, dke = dqd_ref[...], dke_ref[...]
        dq_ref[...] = dqd * eb
        keke = dke * ke
        db = dqd * qd - dki * ki - keke
        sums = _dot_mask(triu, jnp.concatenate([db, keke], axis=1))
        dk = dki * enb + dke * eeb
        dlf = sums[:, :A_HEAD] + _chunk_row(sums[:, A_HEAD:], 0) + dblx_ref[...]
        dff = dlf / f - dk
        df_ref[...] = dff * (1.0 - lb) * sig * (1.0 - sig)
        dlb_ref[0] += jnp.sum(dff * (1.0 - sig), axis=0, keepdims=True)

    def rev(n):
        return ns - 1 - n

    def zspec(off):
        return pl.BlockSpec((A_SUPER, A_HEAD), lambda b, h, n: (b * ns + rev(n), off + h))

    hvec = pl.BlockSpec((1, A_HEAD), lambda b, h, n: (0, h))
    ospec = pl.BlockSpec((A_SUPER, A_HEAD), lambda b, h, n: (b * ns + rev(n), h))
    acc = pl.BlockSpec((1, 1, A_HEAD), lambda b, h, n: (b * 4 + h, 0, 0))
    big = jax.ShapeDtypeStruct((T, A_WIDTH), F32)
    small = jax.ShapeDtypeStruct((B * 4, 1, A_HEAD), F32)
    res = pl.pallas_call(
        _carrying(body, 9, 6, ex, grid) if ex.n else body, name=name, grid=grid,
        in_specs=[zspec(0), zspec(4), zspec(8), zspec(12), hvec, hvec, ospec,
                  pl.BlockSpec((1, A_HEAD, A_HEAD), lambda b, h, n: ((b * 4 + h) * ns + rev(n), 0, 0)), ospec] + ex.in_specs(),
        out_specs=[ospec, ospec, ospec, ospec, acc, acc] + ex.in_specs(),
        out_shape=[big, big, big, big, small, small] + ex.out_shapes(),
        scratch_shapes=[pltpu.VMEM((A_HEAD, A_HEAD), F32), pltpu.VMEM((nch, A_HEAD, A_HEAD), F32),
                        pltpu.VMEM((nch, A_HEAD, A_HEAD), F32),
                        pltpu.VMEM((A_SUPER, A_HEAD), F32), pltpu.VMEM((A_SUPER, A_HEAD), F32),
                        pltpu.VMEM((A_SUPER, A_HEAD), F32)] + ex.scratch(),
        compiler_params=_cparams(*(("arbitrary",) * 3 if ex.n else ("parallel", "parallel", "arbitrary"))),
    )(z, z, z, z, lb, onw, o_raw, s_start, doa, *ex.operands())
    return tuple(res[:6]) + (list(res[6:]),)


def _rope_tables(L):
    half = A_HEAD // 2
    inv_freq = ROPE_THETA ** (-jnp.arange(half, dtype=F32) / half)
    ang = jnp.arange(L, dtype=F32)[:, None] * inv_freq[None, :]
    cos, sin = jnp.cos(ang), jnp.sin(ang)
    return jnp.concatenate([cos, cos], axis=-1), jnp.concatenate([-sin, sin], axis=-1)


def _rope_fwd(z, cos2, sin2, B, L, *, name):
    T = B * L
    tm = 512
    nl = L // tm

    def body(x_ref, c_ref, s_ref, q_ref, k_ref):
        c, s = c_ref[...], s_ref[...]
        for h in range(8):
            x = x_ref[:, h * A_HEAD:(h + 1) * A_HEAD]
            out = x * c + pltpu.roll(x, A_HEAD // 2, 1) * s
            o_ref = q_ref if h < 4 else k_ref
            o_ref[:, (h % 4) * A_HEAD:(h % 4 + 1) * A_HEAD] = out

    tab = pl.BlockSpec((tm, A_HEAD), lambda i: (i % nl, 0))
    out = pl.BlockSpec((tm, 512), lambda i: (i, 0))
    return pl.pallas_call(
        body, name=name, grid=(T // tm,),
        in_specs=[pl.BlockSpec((tm, 1024), lambda i: (i, 2)), tab, tab], out_specs=[out, out],
        out_shape=[jax.ShapeDtypeStruct((T, 512), F32)] * 2, compiler_params=_cparams("parallel"),
    )(z, cos2, sin2)


def _rope_bwd(dqs, dks, dvs, cos2, sin2, B, L, *, name):
    T = B * L
    tm = 256
    nl = L // tm

    def body(*refs):
        c, s = refs[9][...], refs[10][...]
        o_ref = refs[11]
        for part in range(3):
            a_ref, b_ref, c_ref = refs[3 * part:3 * part + 3]
            for h in range(4):
                cols = slice(h * A_HEAD, (h + 1) * A_HEAD)
                d = a_ref[:, cols] + b_ref[:, cols] + c_ref[:, cols]
                if part < 2:
                    d = d * c - pltpu.roll(d, A_HEAD // 2, 1) * s
                o_ref[:, part * 512 + h * A_HEAD:part * 512 + (h + 1) * A_HEAD] = d

    blk = pl.BlockSpec((tm, 512), lambda i: (i, 0))
    tab = pl.BlockSpec((tm, A_HEAD), lambda i: (i % nl, 0))
    return pl.pallas_call(
        body, name=name, grid=(T // tm,), in_specs=[blk] * 9 + [tab, tab],
        out_specs=pl.BlockSpec((tm, 1536), lambda i: (i, 0)),
        out_shape=jax.ShapeDtypeStruct((T, 1536), F32), compiler_params=_cparams("parallel"),
    )(*dqs, *dks, *dvs, cos2, sin2)


def _band_masks():
    i = lax.broadcasted_iota(jnp.int32, (B_SPAN, B_SPAN), 0)
    j = lax.broadcasted_iota(jnp.int32, (B_SPAN, B_SPAN), 1)
    return i <= j, j <= i


class _DilPlan:
    def __init__(self, dil, B, L):
        self.dil, self.B, self.L = dil, B, L
        self.rows = 4 * B_SPAN if dil == 1 else B_SPAN * dil
        self.n = L // self.rows
        self.hr = 4 if dil == 4 else 1
        self.cw = 512 if dil == 1 else A_HEAD
        self.has_other = dil != 16
        self.grid = (B, self.n, 4 if dil == 16 else 1)
        if dil == 1:
            self.items = [(j, 0, h) for j in range(4) for h in range(4)]
        elif dil == 4:
            self.items = [(0, r, h) for r in range(4) for h in range(4)]
        else:
            self.items = [(0, r, 0) for r in range(16)]
        self.groups = [self.items[i:i + 4] for i in range(0, 16, 4)]

    def operands(self, arr):
        return [arr] * self.hr

    def specs(self, col0, role="cur"):
        n, nb128 = self.n, self.L // B_SPAN
        out = []
        for h in range(self.hr):
            cb = col0 // self.cw + h
            if role == "cur":
                out.append(pl.BlockSpec((self.rows, self.cw), lambda b, i, hh, cb=cb: (b * n + i, cb + hh)))
            elif self.dil == 1:
                shift = -1 if role == "prev" else 4
                out.append(pl.BlockSpec((B_SPAN, self.cw),
                                        lambda b, i, hh, cb=cb, shift=shift: (b * nb128 + jnp.clip(4 * i + shift, 0, nb128 - 1), cb)))
            else:
                shift = -1 if role == "prev" else 1
                out.append(pl.BlockSpec((self.rows, self.cw),
                                        lambda b, i, hh, cb=cb, shift=shift: (b * n + jnp.clip(i + shift, 0, n - 1), cb)))
        return out

    def out_spec(self):
        n = self.n
        if self.dil == 4:
            return pl.BlockSpec((self.rows, 512), lambda b, i, hh: (b * n + i, 0))
        return pl.BlockSpec((self.rows, self.cw), lambda b, i, hh: (b * n + i, hh))

    def scratch(self, n_out):
        return [pltpu.VMEM((4, self.rows, A_HEAD), F32)] * n_out if self.dil == 4 else []

    def store(self, out_ref, scr, item, val):
        j, r, h = item
        if self.dil == 1:
            out_ref[pl.ds(j * B_SPAN, B_SPAN), pl.ds(h * A_HEAD, A_HEAD)] = val
        elif self.dil == 4:
            scr.at[h][pl.ds(r, B_SPAN, stride=4), :] = val
        else:
            out_ref[pl.ds(r, B_SPAN, stride=self.dil), :] = val

    def flush(self, out_ref, scr):
        if self.dil == 4:
            for h in range(4):
                out_ref[:, h * A_HEAD:(h + 1) * A_HEAD] = scr[h]

    def cur(self, refs, item):
        j, r, h = item
        if self.dil == 1:
            return refs[0], (pl.ds(j * B_SPAN, B_SPAN), pl.ds(h * A_HEAD, A_HEAD))
        return refs[h], (pl.ds(r, B_SPAN, stride=self.dil), slice(None))

    def other(self, refs, other_refs, item, role):
        j, r, h = item
        if self.dil == 1:
            cols = pl.ds(h * A_HEAD, A_HEAD)
            jj = j - 1 if role == "prev" else j + 1
            if 0 <= jj < 4:
                return refs[0], (pl.ds(jj * B_SPAN, B_SPAN), cols)
            return other_refs[0], (pl.ds(0, B_SPAN), cols)
        return other_refs[h], (pl.ds(r, B_SPAN, stride=self.dil), slice(None))

    def other_valid(self, item, role):
        j = item[0]
        i = pl.program_id(1)
        if role == "prev":
            return True if (self.dil == 1 and j > 0) else i > 0
        return True if (self.dil == 1 and j < 3) else i < self.n - 1


def _ld(pair):
    ref, idx = pair
    return ref[idx]


def _dil_fwd(qr, kr, z, dil, B, L, *, name):
    T = B * L
    plan = _DilPlan(dil, B, L)
    hr, has_prev = plan.hr, plan.has_other
    scale = A_HEAD ** -0.5

    n_t = 5 if has_prev else 3

    def body(*refs):
        lists = [refs[i * hr:(i + 1) * hr] for i in range(n_t)]
        o_ref, l_ref = refs[n_t * hr], refs[n_t * hr + 1]
        scr = refs[n_t * hr + 2:] or (None, None)
        if has_prev:
            q_r, kc_r, vc_r, kp_r, vp_r = lists
        else:
            q_r, kc_r, vc_r = lists
        mp0, mc = _band_masks()

        for group in plan.groups:
            qs = [_ld(plan.cur(q_r, i)).astype(BF16) for i in group]
            sc = [jnp.where(mc, _dot(q, _ld(plan.cur(kc_r, i)).astype(BF16), _NT) * scale, NEG_BIG) for q, i in zip(qs, group)]
            m = [jnp.max(s, axis=-1, keepdims=True) for s in sc]
            if has_prev:
                mps = [mp0 & plan.other_valid(i, "prev") for i in group]
                sp = [jnp.where(mk, _dot(q, _ld(plan.other(kc_r, kp_r, i, "prev")).astype(BF16), _NT) * scale, NEG_BIG)
                      for q, i, mk in zip(qs, group, mps)]
                m = [jnp.maximum(a, jnp.max(s, axis=-1, keepdims=True)) for a, s in zip(m, sp)]
            pc = [jnp.exp(s - a) for s, a in zip(sc, m)]
            l = [jnp.sum(p, axis=-1, keepdims=True) for p in pc]
            o = [_dot(p.astype(BF16), _ld(plan.cur(vc_r, i)).astype(BF16)) for p, i in zip(pc, group)]
            if has_prev:
                pp = [jnp.exp(s - a) for s, a in zip(sp, m)]
                l = [a + jnp.sum(p, axis=-1, keepdims=True) for a, p in zip(l, pp)]
                o = [a + _dot(p.astype(BF16), _ld(plan.other(vc_r, vp_r, i, "prev")).astype(BF16)) for a, p, i in zip(o, pp, group)]
            for i, oi, li, mi in zip(group, o, l, m):
                plan.store(o_ref, scr[0], i, oi / li)
                plan.store(l_ref, scr[1], i, jnp.broadcast_to(mi + jnp.log(li), (B_SPAN, A_HEAD)))
        plan.flush(o_ref, scr[0])
        plan.flush(l_ref, scr[1])

    tensors = [(qr, 0, "cur"), (kr, 0, "cur"), (z, 3072, "cur")] + ([(kr, 0, "prev"), (z, 3072, "prev")] if has_prev else [])
    return pl.pallas_call(
        body, name=name, grid=plan.grid,
        in_specs=[sp for _, c, role in tensors for sp in plan.specs(c, role)],
        out_specs=[plan.out_spec()] * 2, out_shape=[jax.ShapeDtypeStruct((T, 512), F32)] * 2,
        scratch_shapes=plan.scratch(2), compiler_params=_cparams("parallel", "parallel", "parallel"),
    )(*[a for arr, _, _ in tensors for a in plan.operands(arr)])


def _dil_combine(os_, ls_, *, name):
    T = os_[0].shape[0]
    tm = 256

    def body(o1, o2, o3, l1, l2, l3, ob_ref, lse_ref):
        a1, a2, a3 = l1[...], l2[...], l3[...]
        m = jnp.maximum(jnp.maximum(a1, a2), a3)
        e1, e2, e3 = jnp.exp(a1 - m), jnp.exp(a2 - m), jnp.exp(a3 - m)
        den = e1 + e2 + e3
        ob_ref[...] = (e1 * o1[...] + e2 * o2[...] + e3 * o3[...]) / den
        lse_ref[...] = m + jnp.log(den)

    blk = pl.BlockSpec((tm, 512), lambda i: (i, 0))
    return pl.pallas_call(
        body, name=name, grid=(T // tm,), in_specs=[blk] * 6, out_specs=[blk, blk],
        out_shape=[jax.ShapeDtypeStruct((T, 512), F32)] * 2, compiler_params=_cparams("parallel"),
    )(*[o.reshape(T, 512) for o in os_], *[l.reshape(T, 512) for l in ls_])


def _dil_bwd_q(qr, kr, z, dymix, out, lse, dil, B, L, *, name):
    T = B * L
    plan = _DilPlan(dil, B, L)
    hr, has_prev = plan.hr, plan.has_other
    scale = A_HEAD ** -0.5
    n_t = 8 if has_prev else 6

    def body(*refs):
        lists = [refs[i * hr:(i + 1) * hr] for i in range(n_t)]
        dq_ref = refs[n_t * hr]
        scr = refs[n_t * hr + 1:] or (None,)
        q_r, kc_r, vc_r, do_r, out_r, lse_r = lists[:6]
        mp0, mc = _band_masks()

        for group in plan.groups:
            qs = [_ld(plan.cur(q_r, i)).astype(BF16) for i in group]
            dos = [_ld(plan.cur(do_r, i)) for i in group]
            delta = [jnp.sum(d * _ld(plan.cur(out_r, i)), axis=-1, keepdims=True) for d, i in zip(dos, group)]
            dob = [d.astype(BF16) for d in dos]
            lse = [_ld(plan.cur(lse_r, i)) for i in group]
            kc = [_ld(plan.cur(kc_r, i)).astype(BF16) for i in group]
            pc = [jnp.where(mc, jnp.exp(_dot(q, k, _NT) * scale - a), 0.0) for q, k, a in zip(qs, kc, lse)]
            dsc = [p * (_dot(d, _ld(plan.cur(vc_r, i)).astype(BF16), _NT) - dl) * scale
                   for p, d, i, dl in zip(pc, dob, group, delta)]
            dq = [_dot(d.astype(BF16), k) for d, k in zip(dsc, kc)]
            if has_prev:
                kp_r, vp_r = lists[6], lists[7]
                mps = [mp0 & plan.other_valid(i, "prev") for i in group]
                kp = [_ld(plan.other(kc_r, kp_r, i, "prev")).astype(BF16) for i in group]
                pp = [jnp.where(mk, jnp.exp(_dot(q, k, _NT) * scale - a), 0.0) for q, k, a, mk in zip(qs, kp, lse, mps)]
                dsp = [p * (_dot(d, _ld(plan.other(vc_r, vp_r, i, "prev")).astype(BF16), _NT) - dl) * scale
                       for p, d, i, dl in zip(pp, dob, group, delta)]
                dq = [a + _dot(d.astype(BF16), k) for a, d, k in zip(dq, dsp, kp)]
            for i, d in zip(group, dq):
                plan.store(dq_ref, scr[0], i, d)
        plan.flush(dq_ref, scr[0])

    tensors = ([(qr, 0, "cur"), (kr, 0, "cur"), (z, 3072, "cur"), (dymix, 512, "cur"), (out, 0, "cur"), (lse, 0, "cur")]
               + ([(kr, 0, "prev"), (z, 3072, "prev")] if has_prev else []))
    return pl.pallas_call(
        body, name=name, grid=plan.grid,
        in_specs=[sp for _, c, role in tensors for sp in plan.specs(c, role)],
        out_specs=plan.out_spec(), out_shape=jax.ShapeDtypeStruct((T, 512), F32),
        scratch_shapes=plan.scratch(1), compiler_params=_cparams("parallel", "parallel", "parallel"),
    )(*[a for arr, _, _ in tensors for a in plan.operands(arr)])


def _dil_bwd_kv(qr, kr, z, dymix, out, lse, dil, B, L, *, name):
    T = B * L
    plan = _DilPlan(dil, B, L)
    hr, has_next = plan.hr, plan.has_other
    scale = A_HEAD ** -0.5
    n_t = 10 if has_next else 6

    def body(*refs):
        lists = [refs[i * hr:(i + 1) * hr] for i in range(n_t)]
        dk_ref, dv_ref = refs[n_t * hr], refs[n_t * hr + 1]
        scr = refs[n_t * hr + 2:] or (None, None)
        k_r, v_r = lists[0], lists[1]
        own = lists[2:6]
        mp0, mc = _band_masks()

        for group in plan.groups:
            kh = [_ld(plan.cur(k_r, i)).astype(BF16) for i in group]
            vh = [_ld(plan.cur(v_r, i)).astype(BF16) for i in group]
            dk, dv = [None] * len(group), [None] * len(group)
            for role in ("own", "next") if has_next else ("own",):
                if role == "own":
                    get = lambda t, i: _ld(plan.cur(own[t], i))
                    masks = [mc] * len(group)
                else:
                    get = lambda t, i: _ld(plan.other(own[t], lists[6 + t], i, "next"))
                    masks = [mp0 & plan.other_valid(i, "next") for i in group]
                qs = [get(0, i).astype(BF16) for i in group]
                dos = [get(1, i) for i in group]
                delta = [jnp.sum(d * get(2, i), axis=-1, keepdims=True) for d, i in zip(dos, group)]
                dob = [d.astype(BF16) for d in dos]
                p = [jnp.where(mk, jnp.exp(_dot(q, k, _NT) * scale - get(3, i)), 0.0) for q, k, i, mk in zip(qs, kh, group, masks)]
                dvn = [_dot(a.astype(BF16), d, _TN) for a, d in zip(p, dob)]
                ds = [a * (_dot(d, v, _NT) - dl) * scale for a, d, v, dl in zip(p, dob, vh, delta)]
                dkn = [_dot(d.astype(BF16), q, _TN) for d, q in zip(ds, qs)]
                dv = [n if o is None else o + n for o, n in zip(dv, dvn)]
                dk = [n if o is None else o + n for o, n in zip(dk, dkn)]
            for i, a, b in zip(group, dk, dv):
                plan.store(dk_ref, scr[0], i, a)
                plan.store(dv_ref, scr[1], i, b)
        plan.flush(dk_ref, scr[0])
        plan.flush(dv_ref, scr[1])

    queries = [(qr, 0), (dymix, 512), (out, 0), (lse, 0)]
    tensors = ([(kr, 0, "cur"), (z, 3072, "cur")] + [(a, c, "cur") for a, c in queries]
               + ([(a, c, "next") for a, c in queries] if has_next else []))
    return pl.pallas_call(
        body, name=name, grid=plan.grid,
        in_specs=[sp for _, c, role in tensors for sp in plan.specs(c, role)],
        out_specs=[plan.out_spec()] * 2, out_shape=[jax.ShapeDtypeStruct((T, 512), F32)] * 2,
        scratch_shapes=plan.scratch(2), compiler_params=_cparams("parallel", "parallel", "parallel"),
    )(*[a for arr, _, _ in tensors for a in plan.operands(arr)])


def _s5_build(lam_re, lam_im, log_dt, b_re, b_im, c_re, c_im):
    G, P, TC = C_GROUPS, C_STATE, C_TC
    lr = jnp.minimum(lam_re, C_MIN_NEG_RE)
    li = lam_im
    dt = jnp.exp(log_dt)[:, None]
    mag = jnp.exp(dt * lr)
    ar, ai = mag * jnp.cos(dt * li), mag * jnp.sin(dt * li)
    den = lr * lr + li * li
    zr = ((ar - 1.0) * lr + ai * li) / den
    zi = (ai * lr - (ar - 1.0) * li) / den
    bbr = zr[..., None] * b_re - zi[..., None] * b_im
    bbi = zr[..., None] * b_im + zi[..., None] * b_re
    ks = jnp.arange(TC + 1, dtype=F32)[:, None, None]
    pmag = jnp.exp(ks * (dt * lr)[None])
    pr, pi = pmag * jnp.cos(ks * (dt * li)[None]), pmag * jnp.sin(ks * (dt * li)[None])
    car = c_re[None] * pr[:, :, None, :] - c_im[None] * pi[:, :, None, :]
    cai = c_re[None] * pi[:, :, None, :] + c_im[None] * pr[:, :, None, :]
    kern = (jnp.einsum('lgop,gpc->lgco', car[:TC], bbr, precision=HI)
            - jnp.einsum('lgop,gpc->lgco', cai[:TC], bbi, precision=HI))
    pr_e, pi_e = pr[TC - 1 - jnp.arange(TC)], pi[TC - 1 - jnp.arange(TC)]
    er = pr_e[:, :, :, None] * bbr[None] - pi_e[:, :, :, None] * bbi[None]
    ei = pr_e[:, :, :, None] * bbi[None] + pi_e[:, :, :, None] * bbr[None]
    ez = jnp.stack([er, ei], axis=2).reshape(TC, C_NB, C_GB, 2, P, C_GROUP).transpose(1, 0, 2, 5, 3, 4)
    fz = jnp.stack([car[1:], -cai[1:]], axis=0).reshape(2, TC, C_NB, C_GB, C_GROUP, P).transpose(2, 0, 3, 5, 1, 4)
    return kern, ez, fz, pr[TC], pi[TC]


def _s5_lag_blocks(kern):
    eye = jnp.eye(C_GB, dtype=kern.dtype)
    return (kern.reshape(C_TC, C_NB, C_GB, C_GROUP, C_GROUP)[:, :, :, :, None, :]
            * eye[None, None, :, None, :, None]).reshape(C_TC, C_NB, LANES, LANES)


def _s5_dense(kbd, ez, fz, *, name):
    ezc = ez.reshape(C_NB, C_W8, LANES)
    fzc = fz.reshape(C_NB, C_S8, LANES)
    half = C_GB * C_STATE

    def body(k_ref, e_ref, f_ref, m8_ref, e8_ref, f8_ref):
        zero = jnp.zeros((LANES, LANES), BF16)
        for s in range(C_TC):
            for t in range(C_TC):
                m8_ref[0, s * LANES:(s + 1) * LANES, t * LANES:(t + 1) * LANES] = (
                    k_ref[t - s, 0].astype(BF16) if t >= s else zero)
        lane = lax.broadcasted_iota(jnp.int32, (1, LANES), 1)
        src = e_ref[0]
        swapped = pltpu.roll(src, C_STATE, 1)
        rowg = jnp.bitwise_and(jnp.right_shift(lax.broadcasted_iota(jnp.int32, (C_W8, LANES), 0), 4), C_GB - 1)
        for kb in range(C_S8 // LANES):
            z, g0 = kb // (C_GB // 2), 2 * (kb % (C_GB // 2))
            first = jnp.where(lane < C_STATE, src if z == 0 else swapped, 0.0)
            second = jnp.where(lane >= C_STATE, swapped if z == 0 else src, 0.0)
            e8_ref[0, :, kb * LANES:(kb + 1) * LANES] = jnp.where(
                rowg == g0, first, jnp.where(rowg == g0 + 1, second, 0.0)).astype(BF16)
        for z in range(2):
            for g in range(C_GB):
                rows = slice(z * half + g * C_STATE, z * half + (g + 1) * C_STATE)
                piece = f_ref[0, rows, :]
                mine = (lane >= g * C_GROUP) & (lane < (g + 1) * C_GROUP)
                for t in range(C_TC):
                    f8_ref[0, rows, t * LANES:(t + 1) * LANES] = jnp.where(
                        mine, pltpu.roll(piece, ((g - t) * C_GROUP) % LANES, 1), 0.0).astype(BF16)

    blk = lambda r, c: pl.BlockSpec((1, r, c), lambda b: (b, 0, 0))
    return pl.pallas_call(
        body, name=name, grid=(C_NB,),
        in_specs=[pl.BlockSpec((C_TC, 1, LANES, LANES), lambda b: (0, b, 0, 0)), blk(C_W8, LANES), blk(C_S8, LANES)],
        out_specs=[blk(C_W8, C_W8), blk(C_W8, C_S8), blk(C_S8, C_W8)],
        out_shape=[jax.ShapeDtypeStruct((C_NB, C_W8, C_W8), BF16), jax.ShapeDtypeStruct((C_NB, C_W8, C_S8), BF16),
                   jax.ShapeDtypeStruct((C_NB, C_S8, C_W8), BF16)],
        compiler_params=_cparams("parallel"),
    )(kbd, ezc, fzc)


C_NB = C_GROUPS * C_GROUP // LANES
C_GB = C_GROUPS // C_NB
C_W8 = C_TC * LANES
C_S8 = 2 * C_GB * C_STATE


def _s5_scan_tables(lam_re, lam_im, log_dt, nsteps):
    lr = jnp.minimum(lam_re, C_MIN_NEG_RE)
    dt = jnp.exp(log_dt)[:, None]
    ks = (C_TC * 2.0 ** jnp.arange(8, dtype=F32))[None, :, None]
    keep = (jnp.arange(8) < nsteps)[None, :, None]
    pmag = jnp.exp(ks * (dt * lr)[:, None, :])
    ang = ks * (dt * lam_im)[:, None, :]

    def blocks(t):
        return t.reshape(C_NB, C_GB, 8, C_STATE).transpose(0, 2, 1, 3).reshape(C_NB, 8, C_GB * C_STATE)

    pr = blocks(jnp.where(keep, pmag * jnp.cos(ang), 0.0))
    pi = blocks(jnp.where(keep, pmag * jnp.sin(ang), 0.0))
    return jnp.concatenate([pr, pr], axis=-1), jnp.concatenate([-pi, pi], axis=-1)


def _s5_rows(t, R):
    return pl.ds(t, R, stride=C_TC)


def _s5_fwd(u, dsk, m8, e8, f8, tab_r, tab_i, B, L, *, name):
    T = B * L
    R = L // C_TC
    nsteps = int(math.log2(R))

    def body(u_ref, d_ref, m_ref, e_ref, f_ref, tr_ref, ti_ref, gl_ref, y_ref, x8_ref, xs_ref):
        for t in range(C_TC):
            x8_ref[0, :, t * LANES:(t + 1) * LANES] = u_ref[_s5_rows(t, R), :].astype(BF16)
        x8 = x8_ref[0]
        x = _dot(x8, e_ref[0])
        row = lax.broadcasted_iota(jnp.int32, (R, C_S8), 0)
        for k in range(nsteps):
            s = 1 << k
            sh = pltpu.roll(x, s, 0)
            upd = tr_ref[0, k:k + 1, :] * sh + ti_ref[0, k:k + 1, :] * pltpu.roll(sh, C_S8 // 2, 1)
            x = x + jnp.where(row >= s, upd, 0.0)
        xs = jnp.where(row >= 1, pltpu.roll(x, 1, 0), 0.0)
        xs_ref[0] = xs
        y8 = _dot(x8, m_ref[0]) + _dot(xs.astype(BF16), f_ref[0])
        d = d_ref[...]
        for t in range(C_TC):
            rows = _s5_rows(t, R)
            y = y8[:, t * LANES:(t + 1) * LANES] + d * u_ref[rows, :]
            y_ref[rows, :] = y
            gl_ref[rows, :] = 0.5 * y * (1.0 + _erf(y * (2.0 ** -0.5)))

    tok = pl.BlockSpec((L, LANES), lambda c, b: (b, c))
    per_block = lambda shape: pl.BlockSpec((1,) + shape, lambda c, b: (c, 0, 0))
    per_step = lambda shape: pl.BlockSpec((1,) + shape, lambda c, b: (c * B + b, 0, 0))
    return pl.pallas_call(
        body, name=name, grid=(C_NB, B),
        in_specs=[tok, pl.BlockSpec((1, LANES), lambda c, b: (0, c)), per_block((C_W8, C_W8)), per_block((C_W8, C_S8)),
                  per_block((C_S8, C_W8)), per_block((8, C_S8)), per_block((8, C_S8))],
        out_specs=[tok, tok, per_step((R, C_W8)), per_step((R, C_S8))],
        out_shape=[jax.ShapeDtypeStruct((T, D_MODEL), F32), jax.ShapeDtypeStruct((T, D_MODEL), F32),
                   jax.ShapeDtypeStruct((C_NB * B, R, C_W8), BF16), jax.ShapeDtypeStruct((C_NB * B, R, C_S8), F32)],
        compiler_params=_cparams("parallel", "parallel"),
    )(u, dsk, m8, e8, f8, tab_r, tab_i)


def _s5_bwd(dgl, y, u, dsk, xs, m8, e8, f8, tab_r, tab_i, B, L, *, name):
    T = B * L
    R = L // C_TC
    nsteps = int(math.log2(R))

    def body(dgl_ref, y_ref, u_ref, d_ref, xs_ref, m_ref, e_ref, f_ref, tr_ref, ti_ref,
             du_ref, dy8_ref, de_ref, da_ref, dd_ref, dyf_ref):
        @pl.when(pl.program_id(1) == 0)
        def _():
            da_ref[...] = jnp.zeros_like(da_ref)
            dd_ref[...] = jnp.zeros_like(dd_ref)

        dd = jnp.zeros((1, LANES), F32)
        for t in range(C_TC):
            rows = _s5_rows(t, R)
            yv = y_ref[rows, :]
            cdf = 0.5 * (1.0 + _erf(yv * (2.0 ** -0.5)))
            pdf = jnp.exp(-0.5 * yv * yv) * (1.0 / math.sqrt(2.0 * math.pi))
            dy = dgl_ref[rows, :] * (cdf + yv * pdf)
            dd = dd + jnp.sum(dy * u_ref[rows, :], axis=0, keepdims=True)
            dyf_ref[:, t * LANES:(t + 1) * LANES] = dy
        dd_ref[...] += dd
        dy8 = dyf_ref[...].astype(BF16)
        dy8_ref[0] = dy8
        xs = xs_ref[0]
        gx = _dot(dy8, f_ref[0], _NT)
        row = lax.broadcasted_iota(jnp.int32, (R, C_S8), 0)
        for k in range(nsteps):
            s = 1 << k
            sh = pltpu.roll(gx, R - s, 0)
            upd = tr_ref[0, k:k + 1, :] * sh - ti_ref[0, k:k + 1, :] * pltpu.roll(sh, C_S8 // 2, 1)
            gx = gx + jnp.where(row + s < R, upd, 0.0)
        de_in = jnp.where(row + 1 < R, pltpu.roll(gx, R - 1, 0), 0.0)
        deb = de_in.astype(BF16)
        de_ref[0] = deb
        da_ref[0, 0:1, :] += jnp.sum(de_in * xs, axis=0, keepdims=True)
        da_ref[0, 1:2, :] += jnp.sum(de_in * pltpu.roll(xs, C_S8 // 2, 1), axis=0, keepdims=True)
        dx8 = _dot(dy8, m_ref[0], _NT) + _dot(deb, e_ref[0], _NT)
        d = d_ref[...]
        for t in range(C_TC):
            cols = slice(t * LANES, (t + 1) * LANES)
            du_ref[_s5_rows(t, R), :] = dx8[:, cols] + d * dyf_ref[:, cols]

    tok = pl.BlockSpec((L, LANES), lambda c, b: (b, c))
    vec = pl.BlockSpec((1, LANES), lambda c, b: (0, c))
    per_block = lambda shape: pl.BlockSpec((1,) + shape, lambda c, b: (c, 0, 0))
    per_step = lambda shape: pl.BlockSpec((1,) + shape, lambda c, b: (c * B + b, 0, 0))
    return pl.pallas_call(
        body, name=name, grid=(C_NB, B),
        in_specs=[tok, tok, tok, vec, per_step((R, C_S8)), per_block((C_W8, C_W8)),
                  per_block((C_W8, C_S8)), per_block((C_S8, C_W8)), per_block((8, C_S8)), per_block((8, C_S8))],
        out_specs=[tok, per_step((R, C_W8)), per_step((R, C_S8)), per_block((8, C_S8)), vec],
        out_shape=[jax.ShapeDtypeStruct((T, D_MODEL), F32), jax.ShapeDtypeStruct((C_NB * B, R, C_W8), BF16),
                   jax.ShapeDtypeStruct((C_NB * B, R, C_S8), BF16), jax.ShapeDtypeStruct((C_NB, 8, C_S8), F32),
                   jax.ShapeDtypeStruct((1, D_MODEL), F32)],
        scratch_shapes=[pltpu.VMEM((R, C_W8), F32)],
        compiler_params=_cparams("parallel", "arbitrary"),
    )(dgl, y, u, dsk, xs, m8, e8, f8, tab_r, tab_i)


def _bmm_tn(a, b, nb, fold, *, name):
    a = a.reshape(nb, -1, a.shape[-1])
    b = b.reshape(nb, -1, b.shape[-1])
    K, M, N = a.shape[1], a.shape[2], b.shape[2]
    half = C_GB * C_STATE

    def body(a_ref, b_ref, o_ref, p_ref):
        p_ref[...] = _dot(a_ref[0].astype(BF16), b_ref[0].astype(BF16), _TN)
        lane = lax.broadcasted_iota(jnp.int32, (1, LANES), 1)
        if fold == "lags":
            for lag in range(C_TC):
                blocks = [p_ref[s * LANES:(s + 1) * LANES, (s + lag) * LANES:(s + lag + 1) * LANES] for s in range(C_TC - lag)]
                o_ref[0, lag] = functools.reduce(lambda u, v: u + v, blocks)
        elif fold == "e":
            for g in range(C_GB):
                lo, hi = LANES * (g // 2), half + LANES * (g // 2)
                for s in range(C_TC):
                    rows = slice(s * LANES + g * C_GROUP, s * LANES + (g + 1) * C_GROUP)
                    re, im = p_ref[rows, lo:lo + LANES], p_ref[rows, hi:hi + LANES]
                    if g % 2 == 0:
                        im = pltpu.roll(im, C_STATE, 1)
                    else:
                        re = pltpu.roll(re, C_STATE, 1)
                    o_ref[0, rows, :] = jnp.where(lane < C_STATE, re, im)
        else:
            for z in range(2):
                for g in range(C_GB):
                    rows = slice(z * half + g * C_STATE, z * half + (g + 1) * C_STATE)
                    val = jnp.zeros((C_STATE, LANES), F32)
                    for t in range(C_TC):
                        blk = pltpu.roll(p_ref[rows, t * LANES:(t + 1) * LANES], ((t - g) * C_GROUP) % LANES, 1)
                        val = jnp.where((lane >= t * C_GROUP) & (lane < (t + 1) * C_GROUP), blk, val)
                    o_ref[0, rows, :] = val

    if fold == "lags":
        out_spec = pl.BlockSpec((1, C_TC, LANES, LANES), lambda c: (c, 0, 0, 0))
        out_shape = jax.ShapeDtypeStruct((nb, C_TC, LANES, LANES), F32)
    else:
        out_spec = pl.BlockSpec((1, M, LANES), lambda c: (c, 0, 0))
        out_shape = jax.ShapeDtypeStruct((nb, M, LANES), F32)
    return pl.pallas_call(
        body, name=name, grid=(nb,),
        in_specs=[pl.BlockSpec((1, K, M), lambda c: (c, 0, 0)), pl.BlockSpec((1, K, N), lambda c: (c, 0, 0))],
        out_specs=out_spec, out_shape=out_shape, scratch_shapes=[pltpu.VMEM((M, N), F32)],
        compiler_params=_cparams("parallel"),
    )(a, b)


def _loss_head(y, target, *, name):
    T, C = y.shape
    tm = 256

    def body(y_ref, t_ref, l_ref, d_ref):
        err = y_ref[...] - t_ref[...]
        d_ref[...] = err * (1.0 / C)
        sq = err * err
        part = jnp.zeros((8, LANES), F32)
        for r in range(0, tm, 8):
            for c in range(0, C, LANES):
                part = part + sq[r:r + 8, c:c + LANES]

        @pl.when(pl.program_id(0) == 0)
        def _():
            l_ref[...] = part

        @pl.when(pl.program_id(0) > 0)
        def _():
            l_ref[...] += part

    row = pl.BlockSpec((tm, C), lambda i: (i, 0))
    acc = pl.BlockSpec((8, LANES), lambda i: (0, 0))
    return pl.pallas_call(
        body, name=name, grid=(T // tm,), in_specs=[row, row], out_specs=[acc, row],
        out_shape=[jax.ShapeDtypeStruct((8, LANES), F32), jax.ShapeDtypeStruct((T, C), F32)],
        compiler_params=_cparams("arbitrary"),
    )(y, target)


def _adamw(w, g, m, v, *, name):
    shape = w.shape
    size = int(np.prod(shape))
    cols = LANES if (shape[-1] < LANES and size % LANES == 0) else shape[-1]
    rows = size // cols
    tm = _tile(rows, 256) if rows % 8 == 0 else rows
    w2, g2, m2, v2 = (t.reshape(rows, cols) for t in (w, g, m, v))

    def body(w_ref, g_ref, m_ref, v_ref, d_ref, nm_ref, nv_ref):
        gg = g_ref[...]
        nm = ADAM_B1 * m_ref[...] + (1.0 - ADAM_B1) * gg
        nv = ADAM_B2 * v_ref[...] + (1.0 - ADAM_B2) * (gg * gg)
        m_hat = nm / (1.0 - ADAM_B1 ** ADAM_STEP)
        v_hat = nv / (1.0 - ADAM_B2 ** ADAM_STEP)
        d_ref[...] = -ADAM_LR * (m_hat / (jnp.sqrt(v_hat) + ADAM_EPS) + ADAM_WD * w_ref[...])
        nm_ref[...] = nm
        nv_ref[...] = nv

    blk = pl.BlockSpec((tm, cols), lambda i: (i, 0))
    outs = pl.pallas_call(
        body, name=name, grid=(rows // tm,), in_specs=[blk] * 4, out_specs=[blk] * 3,
        out_shape=[jax.ShapeDtypeStruct((rows, cols), F32)] * 3, compiler_params=_cparams("parallel"),
    )(w2, g2, m2, v2)
    return tuple(o.reshape(shape) for o in outs)


def _all_gather(shard, *, name):
    R, C = shard.shape

    def body(x_ref, out_ref, send_sems, recv_sems, local_sem):
        x, y, c = lax.axis_index("x"), lax.axis_index("y"), lax.axis_index("c")
        me, sibling = (x, y, c), (x, y, 1 - c)
        chips = [(1 - x, y), (x, 1 - y), (1 - x, 1 - y)]

        def rows(px, py, pc):
            return out_ref.at[_logical(px, py, pc)]

        def copy(k, block, to, src=None):
            return pltpu.make_async_remote_copy(
                src_ref=rows(*block) if src is None else src, dst_ref=rows(*block),
                send_sem=send_sems.at[k], recv_sem=recv_sems.at[k], device_id=to, device_id_type=_MESH)

        mine = pltpu.make_async_copy(x_ref, rows(*me), local_sem)
        mine.start()
        first = [copy(0, me, sibling, src=x_ref)]
        first += [copy(1 + j, me, (*chip, c), src=x_ref) for j, chip in enumerate(chips)]
        for cp in first:
            cp.start()
        passed = [copy(4 + j, (*chip, c), sibling) for j, chip in enumerate(chips)]
        for j, chip in enumerate(chips):
            copy(1 + j, (*chip, c), me).wait_recv()
            passed[j].start()
        copy(0, sibling, me).wait_recv()
        for j, chip in enumerate(chips):
            copy(4 + j, (*chip, 1 - c), me).wait_recv()
        for cp in first + passed:
            cp.wait_send()
        mine.wait()

    return pl.pallas_call(
        body, name=name, out_shape=jax.ShapeDtypeStruct((N_DEV, R, C), shard.dtype),
        in_specs=[_HBM], out_specs=_HBM,
        scratch_shapes=[pltpu.SemaphoreType.DMA((7,)), pltpu.SemaphoreType.DMA((7,)), pltpu.SemaphoreType.DMA],
    )(shard)


def _gather_weights(shards, *, name):
    nt = len(shards)

    def body(*refs):
        ins, outs = refs[:nt], refs[nt:2 * nt]
        send_sems, recv_sems, local_sems = refs[2 * nt:]
        x, y, c = lax.axis_index("x"), lax.axis_index("y"), lax.axis_index("c")
        me, sibling = (x, y, c), (x, y, 1 - c)
        chips = [(1 - x, y), (x, 1 - y), (1 - x, 1 - y)]

        def copy(t, k, block, to, src=None):
            rows = outs[t].at[_logical(*block)]
            return pltpu.make_async_remote_copy(
                src_ref=rows if src is None else src, dst_ref=rows,
                send_sem=send_sems.at[t, k], recv_sem=recv_sems.at[t, k], device_id=to, device_id_type=_MESH)

        mine = [pltpu.make_async_copy(ins[t], outs[t].at[_logical(*me)], local_sems.at[t]) for t in range(nt)]
        for cp in mine:
            cp.start()
        started = []
        for t in range(nt):
            started.append(copy(t, 0, me, sibling, src=ins[t]))
            started += [copy(t, 1 + j, me, (*chip, c), src=ins[t]) for j, chip in enumerate(chips)]
        for cp in started:
            cp.start()
        for j, chip in enumerate(chips):
            for t in range(nt):
                copy(t, 1 + j, (*chip, c), me).wait_recv()
                fwd = copy(t, 4 + j, (*chip, c), sibling)
                fwd.start()
                started.append(fwd)
        for t in range(nt):
            copy(t, 0, sibling, me).wait_recv()
            for j, chip in enumerate(chips):
                copy(t, 4 + j, (*chip, 1 - c), me).wait_recv()
        for cp in started:
            cp.wait_send()
        for cp in mine:
            cp.wait()

    return pl.pallas_call(
        body, name=name, out_shape=[jax.ShapeDtypeStruct((N_DEV,) + s.shape, s.dtype) for s in shards],
        in_specs=[_HBM] * nt, out_specs=[_HBM] * nt,
        scratch_shapes=[pltpu.SemaphoreType.DMA((nt, 7)), pltpu.SemaphoreType.DMA((nt, 7)), pltpu.SemaphoreType.DMA((nt,))],
    )(*shards)


def _sum_rows(stacked, *, name):
    _, R, C = stacked.shape
    tr = R
    if N_DEV * R * C * stacked.dtype.itemsize > 12 * 1024 * 1024:
        for cand in range(512, 15, -16):
            if R % cand == 0:
                tr = cand
                break

    def body(s_ref, o_ref):
        acc = s_ref[0].astype(F32)
        for k in range(1, N_DEV):
            acc = acc + s_ref[k].astype(F32)
        o_ref[...] = acc

    return pl.pallas_call(
        body, name=name, grid=(R // tr,),
        in_specs=[pl.BlockSpec((N_DEV, tr, C), lambda i: (0, i, 0))], out_specs=pl.BlockSpec((tr, C), lambda i: (i, 0)),
        out_shape=jax.ShapeDtypeStruct((R, C), F32), compiler_params=_cparams("parallel"),
    )(stacked)


_LARGE = (("ab_w_in", 1, True), ("ab_w_out", 1, False), ("s5_w_glu", 1, True), ("xattn_wq", 2, False),
          ("xattn_wkv", 2, True), ("xattn_wo", 2, False), ("ffn_w_in", 2, True), ("ffn_w_out", 2, False))
_LARGE_KEYS = tuple((n, l) for n, layers, _ in _LARGE for l in range(layers))
_TRANSPOSED = {n: t for n, _, t in _LARGE}


def _owner_major(name, w):
    return w.T if _TRANSPOSED[name] else w


def _lb_from_logits(logits):
    return jnp.cumsum(jax.nn.softmax(logits, axis=0), axis=0)[0:1]


def _local_step(x, mem, target, W, shards=None):
    B, L, _ = x.shape
    T = B * L
    x0 = x.reshape(T, D_MODEL)
    memf = mem.reshape(B * MEM_LEN, D_MODEL)
    nw = W["norm_w"]
    cos2, sin2 = _rope_tables(L)
    W = dict(W)
    G, received = {}, {}

    def mmx(a, b, gather=(), scatter=(), **kw):
        if shards is None or not (gather or scatter):
            return _mm(a, b, **kw)
        out, gathered, got = _mm(a, b, gather=[shards[k] for k in gather],
                                 scatter=[G[k].reshape(N_DEV, -1, D_MODEL) for k in scatter], **kw)
        for k, g in zip(gather, gathered):
            W[k] = g.reshape(-1, D_MODEL)
        for k, r in zip(scatter, got):
            received[k] = r
        return out

    def vec(v):
        return v.reshape(1, -1)

    saved = []
    xin = x0
    for layer in range(2):
        s = {"x0": xin}
        tag = f"l{layer}"
        if layer == 0:
            h1 = _rms_fwd(xin, vec(nw[0, 0]), name="norm_pre_mix_l0", out_dtype=BF16)
        s["h1"] = h1
        if layer == 0:
            lb, lb_vjp = jax.vjp(_lb_from_logits, W["hgrn_lb_logits"])
            onw = W["hgrn_out_norm_w"].reshape(1, A_WIDTH)
            z = mmx(h1, W["ab_w_in", 0], tb=True, name="ab_in",
                    gather=[("ab_w_out", 0), ("xattn_wq", 0), ("xattn_wkv", 0), ("xattn_wo", 0)])
            ffn0 = [("ffn_w_in", 0), ("ffn_w_out", 0)] if shards is not None else []
            oa, o_raw, s_start, gathered = _hgrn_fwd(z, lb, onw, B, L, name="hgrn_fwd", gather=[shards[k] for k in ffn0])
            for key, g in zip(ffn0, gathered):
                W[key] = g.reshape(-1, D_MODEL)
            qr, kr = _rope_fwd(z, cos2, sin2, B, L, name="rope_qk")
            os_, ls_ = [], []
            for dil in B_DILS:
                o_g, l_g = _dil_fwd(qr, kr, z, dil, B, L, name=f"dil_fwd_{dil}")
                os_.append(o_g)
                ls_.append(l_g)
            ob, lse = _dil_combine(os_, ls_, name="dil_combine")
            ymix = jnp.concatenate([oa, ob], axis=-1).astype(BF16)
            y1 = _mm(ymix, W["ab_w_out", 0], out_dtype=BF16, name="ab_out")
            s.update(z=z, lb=lb, lb_vjp=lb_vjp, onw=onw, o_raw=o_raw, s_start=s_start, qr=qr, kr=kr, ob=ob, lse=lse, ymix=ymix)
        else:
            p5 = tuple(W[n][0] for n in ("s5_lambda_re", "s5_lambda_im", "s5_log_dt", "s5_b_re", "s5_b_im", "s5_c_re", "s5_c_im"))
            (kern, ez, fz, _, _), s5_vjp = jax.vjp(_s5_build, *p5)
            tab_r, tab_i = _s5_scan_tables(p5[0], p5[1], p5[2], int(math.log2(L // C_TC)))
            kbd, lag_vjp = jax.vjp(_s5_lag_blocks, kern)
            mats = tuple(_s5_dense(kbd, ez, fz, name="s5_maps")) + (tab_r, tab_i)
            dsk = W["s5_d"].reshape(1, D_MODEL)
            gl, ypre, x8, xs = _s5_fwd(h1, dsk, *mats, B, L, name="s5_fwd")
            w_glu = _interleave_rows(W["s5_w_glu", 0])
            zg, y1 = _mm(gl, w_glu, tb=True, name="s5_glu_in", gate=("fwd", "glu"))
            s.update(s5_vjp=s5_vjp, lag_vjp=lag_vjp, mats=mats, x8=x8, xs=xs, dsk=dsk, gl=gl, ypre=ypre, zg=zg, w_glu=w_glu)
        x1, h2 = _rms_fwd(y1, vec(nw[layer, 1]), xin, name=f"norm_post_mix_{tag}", then=(vec(nw[layer, 2]), BF16))
        memn = _rms_fwd(memf, vec(W["mem_norm_w"][layer]), name=f"norm_mem_{tag}", out_dtype=BF16)
        q = _mm(h2, W["xattn_wq", layer], out_dtype=BF16, name=f"x_q_{tag}")
        kv = _mm(memn, W["xattn_wkv", layer], tb=True, out_dtype=BF16, name=f"x_kv_{tag}")
        o = _xattn_fwd(q, kv, B, L, name=f"x_attn_{tag}")
        y2 = _mm(o, W["xattn_wo", layer], out_dtype=BF16, name=f"x_o_{tag}")
        x2, h3 = _rms_fwd(y2, vec(nw[layer, 3]), x1, name=f"norm_post_x_{tag}", then=(vec(nw[layer, 4]), BF16))
        w_ffn_in = _interleave_rows(W["ffn_w_in", layer])
        zf, u = mmx(h3, w_ffn_in, tb=True, name=f"ffn_in_{tag}", gate=("fwd", "swiglu"),
                    gather=[("s5_w_glu", 0), ("xattn_wq", 1), ("xattn_wkv", 1), ("xattn_wo", 1), ("ffn_w_in", 1)] if layer == 0 else [])
        y3 = mmx(u, W["ffn_w_out", layer], out_dtype=BF16, name=f"ffn_out_{tag}", gather=[("ffn_w_out", 1)] if layer == 0 else [])
        if layer == 0:
            x3, h1 = _rms_fwd(y3, vec(nw[0, 5]), x2, name="norm_post_ffn_l0", then=(vec(nw[1, 0]), F32))
        else:
            x3 = _rms_fwd(y3, vec(nw[layer, 5]), x2, name=f"norm_post_ffn_{tag}")
        s.update(y1=y1, x1=x1, h2=h2, memn=memn, q=q, kv=kv, o=o, y2=y2, x2=x2, h3=h3, zf=zf, u=u, y3=y3, w_ffn_in=w_ffn_in)
        saved.append(s)
        xin = x3

    loss_parts, dx = _loss_head(xin, target.reshape(T, D_MODEL), name="loss_head")

    d_norm = [[None] * 6 for _ in range(2)]
    d_memn = [None, None]
    for layer in (1, 0):
        s = saved[layer]
        tag = f"l{layer}"
        if layer == 1:
            dy3, d_norm[1][5] = _rms_bwd(s["y3"], vec(nw[1, 5]), dx, name="bnorm_post_ffn_l1", out_dtype=BF16)
        dzf = mmx(dy3, W["ffn_w_out", layer], tb=True, name=f"b_ffn_out_dx_{tag}", gate=("bwd", "swiglu", s["zf"]),
                  scatter=[("xattn_wkv", 1), ("s5_w_glu", 0)] if layer == 0 else [])
        G["ffn_w_out", layer] = _mm(s["u"], dy3, ta=True, out_dtype=BF16, name=f"b_ffn_out_dw_{tag}")
        G["ffn_w_in", layer] = _interleave_rows(
            mmx(dzf, s["h3"], ta=True, out_dtype=BF16, name=f"b_ffn_in_dw_{tag}", scatter=[("ffn_w_out", layer)]), inverse=True)
        dh3 = mmx(dzf, s["w_ffn_in"], out_dtype=BF16, name=f"b_ffn_in_dx_{tag}", scatter=[("ffn_w_in", layer)])
        dx, d_norm[layer][4], dy2, d_norm[layer][3] = _rms_bwd(
            s["x2"], vec(nw[layer, 4]), dh3, dx, name=f"bnorm_pre_ffn_{tag}", then=(s["y2"], vec(nw[layer, 3])))
        do = _mm(dy2, W["xattn_wo", layer], tb=True, out_dtype=BF16, name=f"b_x_o_dx_{tag}")
        G["xattn_wo", layer] = _mm(s["o"], dy2, ta=True, out_dtype=BF16, name=f"b_x_o_dw_{tag}")
        dq, dkv = _xattn_bwd(s["q"], s["kv"], do, B, L, name=f"b_x_attn_{tag}")
        G["xattn_wq", layer] = _mm(s["h2"], dq, ta=True, out_dtype=BF16, name=f"b_x_q_dw_{tag}")
        dh2 = _mm(dq, W["xattn_wq", layer], tb=True, out_dtype=BF16, name=f"b_x_q_dx_{tag}")
        G["xattn_wkv", layer] = _mm(dkv, s["memn"], ta=True, out_dtype=BF16, name=f"b_x_kv_dw_{tag}")
        dmemn = _mm(dkv, W["xattn_wkv", layer], out_dtype=BF16, name=f"b_x_kv_dx_{tag}")
        _, d_memn[layer] = _rms_bwd(memf, vec(W["mem_norm_w"][layer]), dmemn, name=f"bnorm_mem_{tag}", out_dtype=BF16)
        dx, d_norm[layer][2], dy1, d_norm[layer][1] = _rms_bwd(
            s["x1"], vec(nw[layer, 2]), dh2, dx, name=f"bnorm_pre_x_{tag}", then=(s["y1"], vec(nw[layer, 1])))
        if layer == 0:
            z = s["z"]
            dymix = _mm(dy1, W["ab_w_out", 0], tb=True, name="b_ab_out_dx")
            G["ab_w_out", 0] = _mm(s["ymix"], dy1, ta=True, out_dtype=BF16, name="b_ab_out_dw")
            early = [("xattn_wo", 0), ("xattn_wq", 0), ("xattn_wkv", 0), ("ab_w_out", 0)] if shards is not None else []
            dqa, dfa, dia, dga, d_onw, d_lb, got = _hgrn_bwd(
                z, s["lb"], s["onw"], s["o_raw"], s["s_start"], dymix, B, L, name="hgrn_bwd",
                scatter=[G[k].reshape(N_DEV, -1, D_MODEL) for k in early])
            received.update(zip(early, got))
            dqs, dks, dvs = [], [], []
            for dil in B_DILS:
                dqs.append(_dil_bwd_q(s["qr"], s["kr"], z, dymix, s["ob"], s["lse"], dil, B, L, name=f"dil_bwd_q_{dil}"))
                dk_g, dv_g = _dil_bwd_kv(s["qr"], s["kr"], z, dymix, s["ob"], s["lse"], dil, B, L, name=f"dil_bwd_kv_{dil}")
                dks.append(dk_g)
                dvs.append(dv_g)
            dqkv = _rope_bwd(dqs, dks, dvs, cos2, sin2, B, L, name="b_rope")
            dz = jnp.concatenate([dqa, dfa, dia, dga, dqkv], axis=-1).astype(BF16)
            G["ab_w_in", 0] = _mm(dz, s["h1"], ta=True, out_dtype=BF16, name="b_ab_in_dw")
            dh1 = mmx(dz, W["ab_w_in", 0], out_dtype=BF16, name="b_ab_in_dx", scatter=[("ab_w_in", 0)])
            G["hgrn_out_norm_w"] = jnp.sum(d_onw.reshape(B, A_WIDTH), axis=0, keepdims=True)
            d_lb_row = jnp.sum(d_lb.reshape(B, A_WIDTH), axis=0, keepdims=True)
            G["hgrn_lb_logits"] = s["lb_vjp"](d_lb_row)[0]
        else:
            dzg = _gated_bwd(s["zg"], dy1, "glu", name="b_s5_glu")
            G["s5_w_glu", 0] = _interleave_rows(
                _mm(dzg, s["gl"], ta=True, out_dtype=BF16, name="b_s5_glu_dw"), inverse=True)
            dgl = mmx(dzg, s["w_glu"], name="b_s5_glu_dx", scatter=[("xattn_wo", 1), ("xattn_wq", 1)])
            dh1, dy8, de_in, da, d_dsk = _s5_bwd(dgl, s["ypre"], s["h1"], s["dsk"], s["xs"], *s["mats"], B, L, name="s5_bwd")
            dkbd = _bmm_tn(s["x8"], dy8, C_NB, "lags", name="s5_bwd_dm").transpose(1, 0, 2, 3)
            dfz = _bmm_tn(s["xs"], dy8, C_NB, "f", name="s5_bwd_df").reshape(C_NB, 2, C_GB, C_STATE, C_TC, C_GROUP)
            dez = _bmm_tn(s["x8"], de_in, C_NB, "e", name="s5_bwd_de").reshape(C_NB, C_TC, C_GB, C_GROUP, 2, C_STATE)
            half = C_S8 // 2
            da_r = (da[:, 0, :half] + da[:, 0, half:]).reshape(C_GROUPS, C_STATE)
            da_i = (da[:, 1, half:] - da[:, 1, :half]).reshape(C_GROUPS, C_STATE)
            gp = s["s5_vjp"](s["lag_vjp"](dkbd) + (dez, dfz, da_r, da_i))
            for n, gv in zip(("s5_lambda_re", "s5_lambda_im", "s5_log_dt", "s5_b_re", "s5_b_im", "s5_c_re", "s5_c_im"), gp):
                G[n] = gv[None]
            G["s5_d"] = d_dsk
        if layer == 1:
            dx, d_norm[1][0], dy3, d_norm[0][5] = _rms_bwd(
                s["x0"], vec(nw[1, 0]), dh1, dx, name="bnorm_pre_mix_l1", then=(saved[0]["y3"], vec(nw[0, 5])))
        else:
            dx, d_norm[0][0] = _rms_bwd(s["x0"], vec(nw[0, 0]), dh1, dx, name="bnorm_pre_mix_l0")

    G["norm_w"] = jnp.stack([jnp.concatenate(d_norm[l], axis=0) for l in range(2)])
    G["mem_norm_w"] = jnp.concatenate(d_memn, axis=0)
    if shards is not None:
        G.update(received)
    return loss_parts, dx.reshape(B, L, D_MODEL), G


_SMALL = (("norm_w", (2, 6, 1024)), ("mem_norm_w", (2, 1024)), ("hgrn_lb_logits", (3, 512)), ("hgrn_out_norm_w", (1, 512)),
          ("s5_lambda_re", (1, 64, 64)), ("s5_lambda_im", (1, 64, 64)), ("s5_log_dt", (1, 64)),
          ("s5_b_re", (1, 64, 64, 16)), ("s5_b_im", (1, 64, 64, 16)), ("s5_c_re", (1, 64, 16, 64)),
          ("s5_c_im", (1, 64, 16, 64)), ("s5_d", (1, 1024)))

_WEIGHT_ORDER = ('norm_w', 'mem_norm_w', 'ab_w_in', 'ab_w_out', 'hgrn_lb_logits', 'hgrn_out_norm_w', 's5_lambda_re',
                 's5_lambda_im', 's5_log_dt', 's5_b_re', 's5_b_im', 's5_c_re', 's5_c_im', 's5_d', 's5_w_glu', 'xattn_wq',
                 'xattn_wkv', 'xattn_wo', 'ffn_w_in', 'ffn_w_out')


def kernel(x, mem, norm_w, mem_norm_w, ab_w_in, ab_w_out, hgrn_lb_logits, hgrn_out_norm_w, s5_lambda_re, s5_lambda_im, s5_log_dt, s5_b_re, s5_b_im, s5_c_re, s5_c_im, s5_d, s5_w_glu, xattn_wq, xattn_wkv, xattn_wo, ffn_w_in, ffn_w_out, loss_target, m_norm_w, m_mem_norm_w, m_ab_w_in, m_ab_w_out, m_hgrn_lb_logits, m_hgrn_out_norm_w, m_s5_lambda_re, m_s5_lambda_im, m_s5_log_dt, m_s5_b_re, m_s5_b_im, m_s5_c_re, m_s5_c_im, m_s5_d, m_s5_w_glu, m_xattn_wq, m_xattn_wkv, m_xattn_wo, m_ffn_w_in, m_ffn_w_out, v_norm_w, v_mem_norm_w, v_ab_w_in, v_ab_w_out, v_hgrn_lb_logits, v_hgrn_out_norm_w, v_s5_lambda_re, v_s5_lambda_im, v_s5_log_dt, v_s5_b_re, v_s5_b_im, v_s5_c_re, v_s5_c_im, v_s5_d, v_s5_w_glu, v_xattn_wq, v_xattn_wkv, v_xattn_wo, v_ffn_w_in, v_ffn_w_out):
    local = dict(norm_w=norm_w, mem_norm_w=mem_norm_w, ab_w_in=ab_w_in, ab_w_out=ab_w_out, hgrn_lb_logits=hgrn_lb_logits,
                 hgrn_out_norm_w=hgrn_out_norm_w, s5_lambda_re=s5_lambda_re, s5_lambda_im=s5_lambda_im, s5_log_dt=s5_log_dt,
                 s5_b_re=s5_b_re, s5_b_im=s5_b_im, s5_c_re=s5_c_re, s5_c_im=s5_c_im, s5_d=s5_d, s5_w_glu=s5_w_glu,
                 xattn_wq=xattn_wq, xattn_wkv=xattn_wkv, xattn_wo=xattn_wo, ffn_w_in=ffn_w_in, ffn_w_out=ffn_w_out)
    mom_m = dict(zip(_WEIGHT_ORDER, (m_norm_w, m_mem_norm_w, m_ab_w_in, m_ab_w_out, m_hgrn_lb_logits, m_hgrn_out_norm_w, m_s5_lambda_re, m_s5_lambda_im, m_s5_log_dt, m_s5_b_re, m_s5_b_im, m_s5_c_re, m_s5_c_im, m_s5_d, m_s5_w_glu, m_xattn_wq, m_xattn_wkv, m_xattn_wo, m_ffn_w_in, m_ffn_w_out)))
    mom_v = dict(zip(_WEIGHT_ORDER, (v_norm_w, v_mem_norm_w, v_ab_w_in, v_ab_w_out, v_hgrn_lb_logits, v_hgrn_out_norm_w, v_s5_lambda_re, v_s5_lambda_im, v_s5_log_dt, v_s5_b_re, v_s5_b_im, v_s5_c_re, v_s5_c_im, v_s5_d, v_s5_w_glu, v_xattn_wq, v_xattn_wkv, v_xattn_wo, v_ffn_w_in, v_ffn_w_out)))
    dev = 4 * lax.axis_index("x") + 2 * lax.axis_index("y") + lax.axis_index("c")

    shards = {(n, l): _owner_major(n, local[n][l]).astype(BF16) for n, l in _LARGE_KEYS}
    first = ("ab_w_in", 0)
    W = {first: _gather_weights([shards[first]], name="gather_first")[0].reshape(-1, D_MODEL)}
    tiny =jnp.concatenate([norm_w.reshape(-1), s5_d.reshape(-1)])
    tiny = jnp.pad(tiny, (0, 16 * LANES - tiny.shape[0])).reshape(16, LANES)
    tiny_all = _all_gather(tiny, name="gather_tiny").reshape(N_DEV, 16 * LANES)
    W["norm_w"] = tiny_all[:, :12 * LANES].reshape(N_DEV, 2, 6, LANES).transpose(1, 2, 0, 3).reshape(2, 6, D_MODEL)
    W["s5_d"] = tiny_all[:, 12 * LANES:13 * LANES].reshape(1, D_MODEL)
    for n in ("mem_norm_w", "hgrn_lb_logits", "hgrn_out_norm_w", "s5_lambda_re", "s5_lambda_im", "s5_log_dt",
              "s5_b_re", "s5_b_im", "s5_c_re", "s5_c_im"):
        W[n] = local[n]

    loss_parts, grad_x, G = _local_step(x, mem, loss_target, W, shards)

    g_layers = {}
    for n, l in _LARGE_KEYS:
        g = _sum_rows(G[n, l], name=f"sum_grads_{n}_{l}")
        g_layers.setdefault(n, []).append(g.T if _TRANSPOSED[n] else g)
    g_local = {n: jnp.stack(gl) for n, gl in g_layers.items()}
    small =jnp.concatenate([G[n].reshape(-1) for n, _ in _SMALL] + [0.5 / D_MODEL * jnp.sum(loss_parts).reshape(1)])
    n_small = small.shape[0]
    small = jnp.pad(small, (0, (-n_small) % (8 * LANES))).reshape(-1, LANES)
    small_sum = _sum_rows(_all_gather(small, name="gather_small"), name="sum_small").reshape(-1)
    g_full, off = {}, 0
    for n, shp in _SMALL:
        size = int(np.prod(shp))
        g_full[n] = small_sum[off:off + size].reshape(shp)
        off += size
    loss = small_sum[off]
    grads = dict(g_local)
    for n, shp in _SMALL:
        if n == "norm_w":
            grads[n] = lax.dynamic_slice_in_dim(g_full[n], dev * LANES, LANES, axis=2)
        elif n == "s5_d":
            grads[n] = lax.dynamic_slice_in_dim(g_full[n], dev * LANES, LANES, axis=1)
        else:
            grads[n] = g_full[n]

    delta, new_m, new_v = {}, {}, {}
    for n in _WEIGHT_ORDER:
        delta[n], new_m[n], new_v[n] = _adamw(local[n], grads[n], mom_m[n], mom_v[n], name=f"adamw_{n}")
    return (loss, grad_x, *[grads[n] for n in _WEIGHT_ORDER], *[delta[n] for n in _WEIGHT_ORDER],
            *[new_m[n] for n in _WEIGHT_ORDER], *[new_v[n] for n in _WEIGHT_ORDER])
```

```python
import functools
import math

import numpy as np
import jax
import jax.numpy as jnp
from jax import lax
from jax.experimental import pallas as pl
from jax.experimental.pallas import tpu as pltpu

F32 = jnp.float32
BF16 = jnp.bfloat16
HI = lax.Precision.HIGHEST

D_MODEL = 1024
NORM_EPS = 1e-6
A_WIDTH = 512
A_HEAD = 128
A_CHUNK = 32
A_SUPER = 256
B_SPAN = 128
B_DILS = (1, 4, 16)
ROPE_THETA = 10000.0
C_GROUPS = 64
C_GROUP = 16
C_STATE = 64
C_TC = 8
C_MIN_NEG_RE = -1e-4
MEM_LEN = 256
X_HEADS = 4
X_HD = 256
D_FF = 2816
N_DEV = 8
LANES = 128

ADAM_LR, ADAM_B1, ADAM_B2, ADAM_EPS, ADAM_WD, ADAM_STEP = 0.001, 0.9, 0.999, 1e-08, 0.01, 10

NEG_BIG = -1e30


def _tile(n, pref):
    for d in range(min(pref, n) // LANES * LANES, 0, -LANES):
        if n % d == 0:
            return d
    return n


def _cparams(*sem):
    return pltpu.CompilerParams(dimension_semantics=sem, vmem_limit_bytes=56 * 1024 * 1024)


def _sigmoid(x):
    return 0.5 * jnp.tanh(0.5 * x) + 0.5


def _erf(x):
    ax = jnp.abs(x)
    t = 1.0 / (1.0 + 0.3275911 * ax)
    poly = t * (0.254829592 + t * (-0.284496736 + t * (1.421413741 + t * (-1.453152027 + t * 1.061405429))))
    y = 1.0 - poly * jnp.exp(-ax * ax)
    return jnp.where(x < 0, -y, y)


_HBM = pl.BlockSpec(memory_space=pltpu.HBM)
_MESH = pl.DeviceIdType.MESH


def _logical(px, py, pc):
    return 4 * px + 2 * py + pc


class _Exchange:
    def __init__(self, gather=(), scatter=()):
        self.gather, self.scatter = list(gather), list(scatter)
        self.ng, self.n = len(self.gather), len(self.gather) + len(self.scatter)

    def operands(self):
        return self.gather + self.scatter

    def in_specs(self):
        return [_HBM] * self.n

    def out_shapes(self):
        return ([jax.ShapeDtypeStruct((N_DEV,) + g.shape, g.dtype) for g in self.gather]
                + [jax.ShapeDtypeStruct(s.shape, s.dtype) for s in self.scatter])

    def scratch(self):
        if not self.n:
            return []
        return [pltpu.SemaphoreType.DMA((self.n, 7)), pltpu.SemaphoreType.DMA((self.n, 7)), pltpu.SemaphoreType.DMA((self.n,))]

    def split(self, results):
        return list(results[:self.ng]), list(results[self.ng:])

    def run(self, ins, outs, sems, first, last):
        if not self.n:
            return
        send_sems, recv_sems, local_sems = sems
        x, y, c = lax.axis_index("x"), lax.axis_index("y"), lax.axis_index("c")
        me, sibling = _logical(x, y, c), (x, y, 1 - c)
        chips = [(1 - x, y), (x, 1 - y), (1 - x, 1 - y)]
        peers = [(x ^ (k >> 2), y ^ ((k >> 1) & 1), c ^ (k & 1)) for k in range(1, N_DEV)]

        def remote(t, k, src, dst, to):
            return pltpu.make_async_remote_copy(src_ref=src, dst_ref=dst, send_sem=send_sems.at[t, k],
                                                recv_sem=recv_sems.at[t, k], device_id=to, device_id_type=_MESH)

        def local(t):
            src = ins[t] if t < self.ng else ins[t].at[me]
            return pltpu.make_async_copy(src, outs[t].at[me], local_sems.at[t])

        @pl.when(first)
        def _():
            for t in range(self.n):
                local(t).start()
                if t < self.ng:
                    remote(t, 0, ins[t], outs[t].at[me], sibling).start()
                    for j, chip in enumerate(chips):
                        remote(t, 1 + j, ins[t], outs[t].at[me], (*chip, c)).start()
                else:
                    for k, peer in enumerate(peers):
                        remote(t, k, ins[t].at[_logical(*peer)], outs[t].at[me], peer).start()

        @pl.when(last)
        def _():
            for j, chip in enumerate(chips):
                for t in range(self.ng):
                    landed = outs[t].at[_logical(*chip, c)]
                    remote(t, 1 + j, ins[t], landed, sibling).wait_recv()
                    remote(t, 4 + j, landed, landed, sibling).start()
            for t in range(self.n):
                if t < self.ng:
                    remote(t, 0, ins[t], outs[t].at[_logical(*sibling)], sibling).wait_recv()
                    for j, chip in enumerate(chips):
                        remote(t, 4 + j, ins[t], outs[t].at[_logical(*chip, 1 - c)], sibling).wait_recv()
                    for k in range(7):
                        remote(t, k, ins[t], outs[t].at[me], sibling).wait_send()
                else:
                    for k, peer in enumerate(peers):
                        remote(t, k, ins[t].at[me], outs[t].at[_logical(*peer)], peer).wait_recv()
                    for k, peer in enumerate(peers):
                        remote(t, k, ins[t].at[_logical(*peer)], outs[t].at[me], peer).wait_send()
                local(t).wait()


def _carrying(body, n_in, n_out, ex, grid):
    n_sems = len(ex.scratch())

    def wrapped(*refs):
        ins, ex_in = refs[:n_in], refs[n_in:n_in + ex.n]
        outs = refs[n_in + ex.n:n_in + ex.n + n_out]
        ex_out = refs[n_in + ex.n + n_out:n_in + 2 * ex.n + n_out]
        rest = refs[n_in + 2 * ex.n + n_out:]
        scratch, sems = rest[:len(rest) - n_sems], rest[len(rest) - n_sems:]
        ids = [pl.program_id(a) for a in range(len(grid))]
        first = functools.reduce(lambda u, v: u & v, [i == 0 for i in ids])
        last = functools.reduce(lambda u, v: u & v, [i == g - 1 for i, g in zip(ids, grid)])
        ex.run(ex_in, ex_out, sems, first, last)
        body(*ins, *outs, *scratch)

    return wrapped


_MM_VMEM_BUDGET = 36 * 1024 * 1024


def _mm(a, b, *, ta=False, tb=False, out_dtype=F32, name, tiles=(1408, 1408, 4096), gather=(), scatter=(), gate=None):
    M, K = (a.shape[1], a.shape[0]) if ta else a.shape
    N = b.shape[0] if tb else b.shape[1]
    assert (b.shape[1] if tb else b.shape[0]) == K
    gate_mode = gate[0] if gate else None
    tm, tn, tk = _tile(M, tiles[0]), _tile(N, tiles[1]), _tile(K, tiles[2])
    if gate_mode == "fwd":
        bs = _gate_block(N // 2)
        tn = 2 * bs
    elif gate_mode == "bwd":
        bs = _gate_block(N)
        tn = bs

    def vmem_bytes():
        acc = 4 * tm * tn if tk < K else 0
        if gate_mode == "fwd":
            io = (2 * (2 + 1) + 4) * tm * tn
        elif gate_mode == "bwd":
            io = (2 * (4 + 4) + 4) * tm * tn
        else:
            io = 2 * tm * tn * jnp.dtype(out_dtype).itemsize
        return 2 * 2 * (tm * tk + tk * tn) + acc + io

    while vmem_bytes() > _MM_VMEM_BUDGET:
        if gate_mode and tm > 256:
            tm = _tile(M, tm - LANES)
        elif tk > 512:
            tk = _tile(K, tk - LANES)
        elif tn > 256 and not gate_mode:
            tn = _tile(N, tn - LANES)
        else:
            tm = _tile(M, tm - LANES)
    ni, nj, nk = M // tm, N // tn, K // tk
    ex = _Exchange(gather, scatter)

    a_spec = pl.BlockSpec((tk, tm), lambda i, j, k: (k, i)) if ta else pl.BlockSpec((tm, tk), lambda i, j, k: (i, k))
    b_spec = pl.BlockSpec((tn, tk), lambda i, j, k: (j, k)) if tb else pl.BlockSpec((tk, tn), lambda i, j, k: (k, j))
    dims = (((0 if ta else 1,), (1 if tb else 0,)), ((), ()))
    n_acc = 1 if nk > 1 else 0
    n_in = 3 if gate_mode == "bwd" else 2
    n_out = 2 if gate_mode == "fwd" else 1

    def finish(val, in_refs, out_refs, rows=slice(None)):
        if gate_mode is None:
            out_refs[0][rows, :] = val.astype(out_refs[0].dtype)
        elif gate_mode == "fwd":
            out_refs[0][rows, :] = val.astype(BF16)
            for p in range(tn // (2 * bs)):
                a_, b_ = val[:, 2 * p * bs:(2 * p + 1) * bs], val[:, (2 * p + 1) * bs:(2 * p + 2) * bs]
                out_refs[1][rows, p * bs:(p + 1) * bs] = _gate_value(a_, b_, gate[1]).astype(BF16)
        else:
            z_ref = in_refs[2]
            for p in range(tn // bs):
                a_ = z_ref[rows, 2 * p * bs:(2 * p + 1) * bs].astype(F32)
                b_ = z_ref[rows, (2 * p + 1) * bs:(2 * p + 2) * bs].astype(F32)
                da, db = _gate_grads(a_, b_, val[:, p * bs:(p + 1) * bs], gate[1])
                out_refs[0][rows, 2 * p * bs:(2 * p + 1) * bs] = da.astype(BF16)
                out_refs[0][rows, (2 * p + 1) * bs:(2 * p + 2) * bs] = db.astype(BF16)

    halves = 2 if (gate_mode and nk == 1 and not ta and tm % 32 == 0) else 1

    def body(*refs):
        in_refs, rest = refs[:n_in], refs[n_in:]
        ex_in, out_refs = rest[:ex.n], rest[ex.n:ex.n + n_out]
        ex_out, scratch = rest[ex.n + n_out:2 * ex.n + n_out], rest[2 * ex.n + n_out:]
        i, j, k = pl.program_id(0), pl.program_id(1), pl.program_id(2)
        ex.run(ex_in, ex_out, scratch[n_acc:], (i == 0) & (j == 0) & (k == 0), (i == ni - 1) & (j == nj - 1) & (k == nk - 1))
        if halves > 1:
            rh = tm // halves
            rhs = in_refs[1][...].astype(BF16)
            parts = [lax.dot_general(in_refs[0][r * rh:(r + 1) * rh, :].astype(BF16), rhs, dims, preferred_element_type=F32)
                     for r in range(halves)]
            for r, p in enumerate(parts):
                finish(p, in_refs, out_refs, slice(r * rh, (r + 1) * rh))
            return
        part = lax.dot_general(in_refs[0][...].astype(BF16), in_refs[1][...].astype(BF16), dims, preferred_element_type=F32)
        if nk == 1:
            finish(part, in_refs, out_refs)
            return
        acc_ref = scratch[0]

        @pl.when(k == 0)
        def _():
            acc_ref[...] = part

        @pl.when(k > 0)
        def _():
            acc_ref[...] += part

        @pl.when(k == nk - 1)
        def _():
            finish(acc_ref[...], in_refs, out_refs)

    tile = lambda width: pl.BlockSpec((tm, width), lambda i, j, k: (i, j))
    if gate_mode == "fwd":
        out_specs, out_shape = [tile(tn), tile(tn // 2)], [jax.ShapeDtypeStruct((M, N), BF16), jax.ShapeDtypeStruct((M, N // 2), BF16)]
    elif gate_mode == "bwd":
        out_specs, out_shape = [tile(2 * tn)], [jax.ShapeDtypeStruct((M, 2 * N), BF16)]
    else:
        out_specs, out_shape = [tile(tn)], [jax.ShapeDtypeStruct((M, N), out_dtype)]
    operands = [a, b] + ([gate[2]] if gate_mode == "bwd" else [])
    sem = ("arbitrary",) * 3 if ex.n else ("parallel", "parallel", "arbitrary")
    res = pl.pallas_call(
        body, name=name, grid=(ni, nj, nk),
        in_specs=[a_spec, b_spec] + ([tile(2 * tn)] if gate_mode == "bwd" else []) + ex.in_specs(),
        out_specs=out_specs + ex.in_specs(),
        out_shape=out_shape + ex.out_shapes(),
        scratch_shapes=([pltpu.VMEM((tm, tn), F32)] if nk > 1 else []) + ex.scratch(),
        compiler_params=_cparams(*sem),
    )(*operands, *ex.operands())
    main = res[0] if n_out == 1 else tuple(res[:n_out])
    if not ex.n:
        return main
    return (main,) + tuple(ex.split(res[n_out:]))


def _rms_fwd(x, w, res=None, *, name, out_dtype=F32, then=None):
    T, C = x.shape
    tm = _tile(T, 512)
    has_res = res is not None

    def norm(v, w_ref):
        return v * lax.rsqrt(jnp.mean(v * v, axis=-1, keepdims=True) + NORM_EPS) * w_ref[...]

    def body(*refs):
        x_ref, w_ref = refs[0], refs[1]
        y = norm(x_ref[...].astype(F32), w_ref)
        if has_res:
            y = y + refs[2][...]
        if then is None:
            refs[-1][...] = y.astype(refs[-1].dtype)
        else:
            refs[-2][...] = y.astype(refs[-2].dtype)
            refs[-1][...] = norm(y, refs[-3]).astype(refs[-1].dtype)

    row = pl.BlockSpec((tm, C), lambda i: (i, 0))
    vec = pl.BlockSpec((1, C), lambda i: (0, 0))
    ins = [x, w] + ([res] if has_res else []) + ([then[0]] if then else [])
    in_specs = [row, vec] + ([row] if has_res else []) + ([vec] if then else [])
    out_shape = [jax.ShapeDtypeStruct((T, C), out_dtype)] + ([jax.ShapeDtypeStruct((T, C), then[1])] if then else [])
    res_ = pl.pallas_call(
        body, name=name, grid=(T // tm,), in_specs=in_specs, out_specs=[row] * len(out_shape),
        out_shape=out_shape, compiler_params=_cparams("parallel"),
    )(*ins)
    return tuple(res_) if then else res_[0]


def _rms_bwd(x, w, dy, add=None, *, name, out_dtype=F32, then=None):
    T, C = x.shape
    tm = _tile(T, 512)
    has_add = add is not None
    n_in = 3 + has_add + (2 if then else 0)

    def grads(x_ref, w_ref, g, dw_ref):
        xv = x_ref[...].astype(F32)
        r = lax.rsqrt(jnp.mean(xv * xv, axis=-1, keepdims=True) + NORM_EPS)
        xh = xv * r
        part = jnp.sum(g * xh, axis=0, keepdims=True)

        @pl.when(pl.program_id(0) == 0)
        def _():
            dw_ref[...] = part

        @pl.when(pl.program_id(0) > 0)
        def _():
            dw_ref[...] += part

        gx = g * w_ref[...]
        return r * (gx - xh * jnp.mean(gx * xh, axis=-1, keepdims=True))

    def body(*refs):
        ins, outs = refs[:n_in], refs[n_in:]
        dx = grads(ins[0], ins[1], ins[2][...].astype(F32), outs[1])
        if has_add:
            dx = dx + ins[3][...]
        outs[0][...] = dx.astype(outs[0].dtype)
        if then:
            outs[2][...] = grads(ins[-2], ins[-1], dx, outs[3]).astype(BF16)

    row = pl.BlockSpec((tm, C), lambda i: (i, 0))
    vec = pl.BlockSpec((1, C), lambda i: (0, 0))
    ins = [x, w, dy] + ([add] if has_add else []) + (list(then) if then else [])
    big, small = jax.ShapeDtypeStruct((T, C), out_dtype), jax.ShapeDtypeStruct((1, C), F32)
    return pl.pallas_call(
        body, name=name, grid=(T // tm,),
        in_specs=[row, vec, row] + ([row] if has_add else []) + ([row, vec] if then else []),
        out_specs=[row, vec] + ([row, vec] if then else []),
        out_shape=[big, small] + ([jax.ShapeDtypeStruct((T, C), BF16), small] if then else []),
        compiler_params=_cparams("arbitrary"),
    )(*ins)


def _gate_block(width):
    return _tile(width, 1408)


def _gate_value(a, b, kind):
    return a * _sigmoid(a) * b if kind == "swiglu" else a * _sigmoid(b)


def _gate_grads(a, b, d, kind):
    if kind == "swiglu":
        s = _sigmoid(a)
        return d * b * (s * (1.0 + a * (1.0 - s))), d * a * s
    s = _sigmoid(b)
    return d * s, d * a * s * (1.0 - s)


def _interleave_rows(w, inverse=False):
    W2, C = w.shape
    bs = _gate_block(W2 // 2)
    nb = W2 // 2 // bs
    shape = (nb, 2, bs, C) if inverse else (2, nb, bs, C)
    return w.reshape(shape).transpose(1, 0, 2, 3).reshape(W2, C)


def _gated_bwd(z, dout, kind, *, name):
    T, W2 = z.shape
    W = W2 // 2
    tm, bs = _tile(T, 512), _gate_block(W)

    def body(z_ref, d_ref, o_ref):
        da, db = _gate_grads(z_ref[:, :bs].astype(F32), z_ref[:, bs:].astype(F32), d_ref[...].astype(F32), kind)
        o_ref[:, :bs] = da.astype(o_ref.dtype)
        o_ref[:, bs:] = db.astype(o_ref.dtype)

    return pl.pallas_call(
        body, name=name, grid=(T // tm, W // bs),
        in_specs=[pl.BlockSpec((tm, 2 * bs), lambda i, j: (i, j)), pl.BlockSpec((tm, bs), lambda i, j: (i, j))],
        out_specs=pl.BlockSpec((tm, 2 * bs), lambda i, j: (i, j)),
        out_shape=jax.ShapeDtypeStruct((T, W2), BF16), compiler_params=_cparams("parallel", "parallel"),
    )(z, dout)


_NT = (((1,), (1,)), ((), ()))
_TN = (((0,), (0,)), ((), ()))


def _dot(a, b, dims=None, precision=None):
    if dims is None:
        return jnp.dot(a, b, preferred_element_type=F32, precision=precision)
    return lax.dot_general(a, b, dims, preferred_element_type=F32, precision=precision)


def _xattn_fwd(q, kv, B, L, *, name):
    T = q.shape[0]
    tq = 256
    nq = L // tq
    scale = X_HD ** -0.5

    def body(q_ref, k_ref, v_ref, o_ref):
        heads = [slice(h * X_HD, (h + 1) * X_HD) for h in range(X_HEADS)]
        s = [_dot(q_ref[:, sl].astype(BF16), k_ref[:, sl].astype(BF16), _NT) * scale for sl in heads]
        m = [jnp.max(a, axis=-1, keepdims=True) for a in s]
        p = [jnp.exp(a - b) for a, b in zip(s, m)]
        l = [jnp.sum(a, axis=-1, keepdims=True) for a in p]
        o = [_dot(a.astype(BF16), v_ref[:, sl].astype(BF16)) for a, sl in zip(p, heads)]
        for sl, a, b in zip(heads, o, l):
            o_ref[:, sl] = (a / b).astype(BF16)

    return pl.pallas_call(
        body, name=name, grid=(B, nq),
        in_specs=[pl.BlockSpec((tq, D_MODEL), lambda b, i: (b * nq + i, 0)),
                  pl.BlockSpec((MEM_LEN, D_MODEL), lambda b, i: (b, 0)),
                  pl.BlockSpec((MEM_LEN, D_MODEL), lambda b, i: (b, 1))],
        out_specs=pl.BlockSpec((tq, D_MODEL), lambda b, i: (b * nq + i, 0)),
        out_shape=jax.ShapeDtypeStruct((T, D_MODEL), BF16), compiler_params=_cparams("parallel", "parallel"),
    )(q, kv, kv)


def _xattn_bwd(q, kv, do, B, L, *, name):
    T = q.shape[0]
    tq = 256
    nq = L // tq
    scale = X_HD ** -0.5

    def body(q_ref, k_ref, v_ref, do_ref, dq_ref, dkv_ref):
        @pl.when(pl.program_id(1) == 0)
        def _():
            dkv_ref[...] = jnp.zeros_like(dkv_ref)

        heads = [slice(h * X_HD, (h + 1) * X_HD) for h in range(X_HEADS)]
        qs = [q_ref[:, sl].astype(BF16) for sl in heads]
        ks = [k_ref[:, sl].astype(BF16) for sl in heads]
        dos = [do_ref[:, sl].astype(BF16) for sl in heads]
        s = [_dot(a, b, _NT) * scale for a, b in zip(qs, ks)]
        dp = [_dot(a, v_ref[:, sl].astype(BF16), _NT) for a, sl in zip(dos, heads)]
        e = [jnp.exp(a - jnp.max(a, axis=-1, keepdims=True)) for a in s]
        p = [a / jnp.sum(a, axis=-1, keepdims=True) for a in e]
        ds = [(a * (b - jnp.sum(b * a, axis=-1, keepdims=True)) * scale).astype(BF16) for a, b in zip(p, dp)]
        dv = [_dot(a.astype(BF16), b, _TN) for a, b in zip(p, dos)]
        dq = [_dot(a, b) for a, b in zip(ds, ks)]
        dk = [_dot(a, b, _TN) for a, b in zip(ds, qs)]
        for h, sl in enumerate(heads):
            dq_ref[:, sl] = dq[h].astype(BF16)
            dkv_ref[:, sl] += dk[h]
            dkv_ref[:, D_MODEL + h * X_HD:D_MODEL + (h + 1) * X_HD] += dv[h]

    return pl.pallas_call(
        body, name=name, grid=(B, nq),
        in_specs=[pl.BlockSpec((tq, D_MODEL), lambda b, i: (b * nq + i, 0)),
                  pl.BlockSpec((MEM_LEN, D_MODEL), lambda b, i: (b, 0)),
                  pl.BlockSpec((MEM_LEN, D_MODEL), lambda b, i: (b, 1)),
                  pl.BlockSpec((tq, D_MODEL), lambda b, i: (b * nq + i, 0))],
        out_specs=[pl.BlockSpec((tq, D_MODEL), lambda b, i: (b * nq + i, 0)),
                   pl.BlockSpec((MEM_LEN, 2 * D_MODEL), lambda b, i: (b, 0))],
        out_shape=[jax.ShapeDtypeStruct((T, D_MODEL), BF16), jax.ShapeDtypeStruct((B * MEM_LEN, 2 * D_MODEL), F32)],
        compiler_params=_cparams("parallel", "arbitrary"),
    )(q, kv, kv, do)


def _chunk_masks():
    row = lax.broadcasted_iota(jnp.int32, (A_SUPER, A_SUPER), 0)
    col = lax.broadcasted_iota(jnp.int32, (A_SUPER, A_SUPER), 1)
    same = jnp.right_shift(row, 5) == jnp.right_shift(col, 5)
    return same, same & (col <= row), same & (col >= row)


def _dot_mask(mask, x):
    m = mask.astype(BF16)
    hi = x.astype(BF16)
    rest = x - hi.astype(F32)
    mid = rest.astype(BF16)
    lo = (rest - mid.astype(F32)).astype(BF16)
    return _dot(m, hi) + _dot(m, mid) + _dot(m, lo)


def _chunk_row(x, which):
    rows = [x[c * A_CHUNK + which % A_CHUNK:c * A_CHUNK + which % A_CHUNK + 1, :] for c in range(A_SUPER // A_CHUNK)]
    return jnp.concatenate([jnp.broadcast_to(r, (A_CHUNK, x.shape[1])) for r in rows], axis=0)


def _hgrn_gates(fa, lb):
    sig = _sigmoid(fa)
    f = lb + (1.0 - lb) * sig
    return sig, f, jnp.log(f), 1.0 - f


def _hgrn_fwd(z, lb, onw, B, L, *, name, gather=()):
    T = B * L
    ns = L // A_SUPER
    nch = A_SUPER // A_CHUNK
    ex = _Exchange(gather=gather)
    grid = (B, 4, ns)

    def body(q_ref, f_ref, v_ref, g_ref, lb_ref, w_ref, oa_ref, o_ref, s_ref, st_ref, sc_ref):
        @pl.when(pl.program_id(2) == 0)
        def _():
            st_ref[...] = jnp.zeros_like(st_ref)

        s_ref[0] = st_ref[...]
        same, tril, _ = _chunk_masks()
        q, v = q_ref[...], v_ref[...]
        _, _, lf, k = _hgrn_gates(f_ref[...], lb_ref[...])
        bcs = _dot_mask(tril, lf)
        bl = _chunk_row(bcs, -1)
        qd = (q * jnp.exp(bcs)).astype(BF16)
        ki = (k * jnp.exp(-bcs)).astype(BF16)
        ke = (k * jnp.exp(bl - bcs)).astype(BF16)
        dec = jnp.exp(bl)
        vb = v.astype(BF16)
        a = jnp.where(tril, _dot(qd, ki, _NT), 0.0)
        o_ref[...] = _dot(a.astype(BF16), vb)
        chunks = [slice(c * A_CHUNK, (c + 1) * A_CHUNK) for c in range(nch)]
        outer = [_dot(vb[rs], ke[rs], _TN) for rs in chunks]
        st = st_ref[...]
        for c, rs in enumerate(chunks):
            sc_ref[c] = st.astype(BF16)
            st = st * dec[c * A_CHUNK:c * A_CHUNK + 1, :] + outer[c]
        st_ref[...] = st
        for c, rs in enumerate(chunks):
            o_ref[rs, :] += _dot(qd[rs], sc_ref[c], _NT)
        o = o_ref[...]
        r = lax.rsqrt(jnp.mean(o * o, axis=-1, keepdims=True) + NORM_EPS)
        g = g_ref[...]
        oa_ref[...] = o * r * w_ref[...] * (g * _sigmoid(g))

    def zspec(off):
        return pl.BlockSpec((A_SUPER, A_HEAD), lambda b, h, n: (b * ns + n, off + h))

    hvec = pl.BlockSpec((1, A_HEAD), lambda b, h, n: (0, h))
    ospec = pl.BlockSpec((A_SUPER, A_HEAD), lambda b, h, n: (b * ns + n, h))
    res = pl.pallas_call(
        _carrying(body, 6, 3, ex, grid) if ex.n else body, name=name, grid=grid,
        in_specs=[zspec(0), zspec(4), zspec(8), zspec(12), hvec, hvec] + ex.in_specs(),
        out_specs=[ospec, ospec, pl.BlockSpec((1, A_HEAD, A_HEAD), lambda b, h, n: ((b * 4 + h) * ns + n, 0, 0))] + ex.in_specs(),
        out_shape=[jax.ShapeDtypeStruct((T, A_WIDTH), F32), jax.ShapeDtypeStruct((T, A_WIDTH), F32),
                   jax.ShapeDtypeStruct((B * 4 * ns, A_HEAD, A_HEAD), F32)] + ex.out_shapes(),
        scratch_shapes=[pltpu.VMEM((A_HEAD, A_HEAD), F32), pltpu.VMEM((nch, A_HEAD, A_HEAD), BF16)] + ex.scratch(),
        compiler_params=_cparams(*(("arbitrary",) * 3 if ex.n else ("parallel", "parallel", "arbitrary"))),
    )(z, z, z, z, lb, onw, *ex.operands())
    return tuple(res[:3]) + (list(res[3:]),)


def _hgrn_bwd(z, lb, onw, o_raw, s_start, doa, B, L, *, name, scatter=()):
    T = B * L
    ns = L // A_SUPER
    nch = A_SUPER // A_CHUNK
    ex = _Exchange(scatter=scatter)
    grid = (B, 4, ns)

    def body(q_ref, f_ref, v_ref, g_ref, lb_ref, w_ref, o_ref, s_ref, doa_ref,
             dq_ref, df_ref, dv_ref, dg_ref, dw_ref, dlb_ref, dst_ref, sc_ref, dsc_ref, dqd_ref, dke_ref, dblx_ref):
        @pl.when(pl.program_id(2) == 0)
        def _():
            dst_ref[...] = jnp.zeros_like(dst_ref)
            dw_ref[...] = jnp.zeros_like(dw_ref)
            dlb_ref[...] = jnp.zeros_like(dlb_ref)

        same, tril, triu = _chunk_masks()
        q, v, g, lb, w = q_ref[...], v_ref[...], g_ref[...], lb_ref[...], w_ref[...]
        sig, f, lf, k = _hgrn_gates(f_ref[...], lb)
        bcs = _dot_mask(tril, lf)
        bl = _chunk_row(bcs, -1)
        eb, enb, eeb = jnp.exp(bcs), jnp.exp(-bcs), jnp.exp(bl - bcs)
        qd, ki, ke = q * eb, k * enb, k * eeb
        qdb, kib, keb, vb = qd.astype(BF16), ki.astype(BF16), ke.astype(BF16), v.astype(BF16)
        dec = jnp.exp(bl)
        o = o_ref[...]
        r = lax.rsqrt(jnp.mean(o * o, axis=-1, keepdims=True) + NORM_EPS)
        on = o * r
        sg = _sigmoid(g)
        silu_g = g * sg
        doa = doa_ref[...]
        dg_ref[...] = doa * on * w * (sg * (1.0 + g * (1.0 - sg)))
        dw_ref[0] += jnp.sum(doa * on * silu_g, axis=0, keepdims=True)
        don = doa * w * silu_g
        do = r * (don - on * jnp.mean(don * on, axis=-1, keepdims=True))
        dob = do.astype(BF16)
        a = jnp.where(tril, _dot(qdb, kib, _NT), 0.0).astype(BF16)
        da = jnp.where(tril, _dot(dob, vb, _NT), 0.0).astype(BF16)
        dv_ref[...] = _dot(a, dob, _TN)
        dqd_ref[...] = _dot(da, kib)
        dki = _dot(da, qdb, _TN)
        chunks = [slice(c * A_CHUNK, (c + 1) * A_CHUNK) for c in range(nch)]
        outer = [_dot(vb[rs], keb[rs], _TN) for rs in chunks]
        st = s_ref[0]
        for c in range(nch):
            sc_ref[c] = st
            st = st * dec[c * A_CHUNK:c * A_CHUNK + 1, :] + outer[c]
        outer_g = [_dot(dob[rs], qdb[rs], _TN) for rs in chunks]
        dst = dst_ref[...]
        for c in reversed(range(nch)):
            dsc_ref[c] = dst
            dst = dst * dec[c * A_CHUNK:c * A_CHUNK + 1, :] + outer_g[c]
        dst_ref[...] = dst
        for c, rs in enumerate(chunks):
            dec_c = dec[c * A_CHUNK:c * A_CHUNK + 1, :]
            dsc, stc = dsc_ref[c], sc_ref[c]
            dscb = dsc.astype(BF16)
            dv_ref[rs, :] += _dot(keb[rs], dscb, _NT)
            dke_ref[rs, :] = _dot(vb[rs], dscb)
            ddec = jnp.sum(dsc * stc, axis=0, keepdims=True)
            dqd_ref[rs, :] += _dot(dob[rs], stc.astype(BF16))
            dblx_ref[rs, :] = jnp.broadcast_to(ddec * dec_c, (A_CHUNK, A_HEAD))
        dqd, dke = dqd_ref[...], dke_ref[...]
        dq_ref[...] = dqd * eb
        keke = dke * ke
        db = dqd * qd - dki * ki - keke
        sums = _dot_mask(triu, jnp.concatenate([db, keke], axis=1))
        dk = dki * enb + dke * eeb
        dlf = sums[:, :A_HEAD] + _chunk_row(sums[:, A_HEAD:], 0) + dblx_ref[...]
        dff = dlf / f - dk
        df_ref[...] = dff * (1.0 - lb) * sig * (1.0 - sig)
        dlb_ref[0] += jnp.sum(dff * (1.0 - sig), axis=0, keepdims=True)

    def rev(n):
        return ns - 1 - n

    def zspec(off):
        return pl.BlockSpec((A_SUPER, A_HEAD), lambda b, h, n: (b * ns + rev(n), off + h))

    hvec = pl.BlockSpec((1, A_HEAD), lambda b, h, n: (0, h))
    ospec = pl.BlockSpec((A_SUPER, A_HEAD), lambda b, h, n: (b * ns + rev(n), h))
    acc = pl.BlockSpec((1, 1, A_HEAD), lambda b, h, n: (b * 4 + h, 0, 0))
    big = jax.ShapeDtypeStruct((T, A_WIDTH), F32)
    small = jax.ShapeDtypeStruct((B * 4, 1, A_HEAD), F32)
    res = pl.pallas_call(
        _carrying(body, 9, 6, ex, grid) if ex.n else body, name=name, grid=grid,
        in_specs=[zspec(0), zspec(4), zspec(8), zspec(12), hvec, hvec, ospec,
                  pl.BlockSpec((1, A_HEAD, A_HEAD), lambda b, h, n: ((b * 4 + h) * ns + rev(n), 0, 0)), ospec] + ex.in_specs(),
        out_specs=[ospec, ospec, ospec, ospec, acc, acc] + ex.in_specs(),
        out_shape=[big, big, big, big, small, small] + ex.out_shapes(),
        scratch_shapes=[pltpu.VMEM((A_HEAD, A_HEAD), F32), pltpu.VMEM((nch, A_HEAD, A_HEAD), F32),
                        pltpu.VMEM((nch, A_HEAD, A_HEAD), F32),
                        pltpu.VMEM((A_SUPER, A_HEAD), F32), pltpu.VMEM((A_SUPER, A_HEAD), F32),
                        pltpu.VMEM((A_SUPER, A_HEAD), F32)] + ex.scratch(),
        compiler_params=_cparams(*(("arbitrary",) * 3 if ex.n else ("parallel", "parallel", "arbitrary"))),
    )(z, z, z, z, lb, onw, o_raw, s_start, doa, *ex.operands())
    return tuple(res[:6]) + (list(res[6:]),)


def _rope_tables(L):
    half = A_HEAD // 2
    inv_freq = ROPE_THETA ** (-jnp.arange(half, dtype=F32) / half)
    ang = jnp.arange(L, dtype=F32)[:, None] * inv_freq[None, :]
    cos, sin = jnp.cos(ang), jnp.sin(ang)
    return jnp.concatenate([cos, cos], axis=-1), jnp.concatenate([-sin, sin], axis=-1)


def _rope_fwd(z, cos2, sin2, B, L, *, name):
    T = B * L
    tm = 512
    nl = L // tm

    def body(x_ref, c_ref, s_ref, q_ref, k_ref):
        c, s = c_ref[...], s_ref[...]
        for h in range(8):
            x = x_ref[:, h * A_HEAD:(h + 1) * A_HEAD]
            out = x * c + pltpu.roll(x, A_HEAD // 2, 1) * s
            o_ref = q_ref if h < 4 else k_ref
            o_ref[:, (h % 4) * A_HEAD:(h % 4 + 1) * A_HEAD] = out

    tab = pl.BlockSpec((tm, A_HEAD), lambda i: (i % nl, 0))
    out = pl.BlockSpec((tm, 512), lambda i: (i, 0))
    return pl.pallas_call(
        body, name=name, grid=(T // tm,),
        in_specs=[pl.BlockSpec((tm, 1024), lambda i: (i, 2)), tab, tab], out_specs=[out, out],
        out_shape=[jax.ShapeDtypeStruct((T, 512), F32)] * 2, compiler_params=_cparams("parallel"),
    )(z, cos2, sin2)


def _rope_bwd(dqs, dks, dvs, cos2, sin2, B, L, *, name):
    T = B * L
    tm = 256
    nl = L // tm

    def body(*refs):
        c, s = refs[9][...], refs[10][...]
        o_ref = refs[11]
        for part in range(3):
            a_ref, b_ref, c_ref = refs[3 * part:3 * part + 3]
            for h in range(4):
                cols = slice(h * A_HEAD, (h + 1) * A_HEAD)
                d = a_ref[:, cols] + b_ref[:, cols] + c_ref[:, cols]
                if part < 2:
                    d = d * c - pltpu.roll(d, A_HEAD // 2, 1) * s
                o_ref[:, part * 512 + h * A_HEAD:part * 512 + (h + 1) * A_HEAD] = d

    blk = pl.BlockSpec((tm, 512), lambda i: (i, 0))
    tab = pl.BlockSpec((tm, A_HEAD), lambda i: (i % nl, 0))
    return pl.pallas_call(
        body, name=name, grid=(T // tm,), in_specs=[blk] * 9 + [tab, tab],
        out_specs=pl.BlockSpec((tm, 1536), lambda i: (i, 0)),
        out_shape=jax.ShapeDtypeStruct((T, 1536), F32), compiler_params=_cparams("parallel"),
    )(*dqs, *dks, *dvs, cos2, sin2)


def _band_masks():
    i = lax.broadcasted_iota(jnp.int32, (B_SPAN, B_SPAN), 0)
    j = lax.broadcasted_iota(jnp.int32, (B_SPAN, B_SPAN), 1)
    return i <= j, j <= i


class _DilPlan:
    def __init__(self, dil, B, L):
        self.dil, self.B, self.L = dil, B, L
        self.rows = 4 * B_SPAN if dil == 1 else B_SPAN * dil
        self.n = L // self.rows
        self.hr = 4 if dil == 4 else 1
        self.cw = 512 if dil == 1 else A_HEAD
        self.has_other = dil != 16
        self.grid = (B, self.n, 4 if dil == 16 else 1)
        if dil == 1:
            self.items = [(j, 0, h) for j in range(4) for h in range(4)]
        elif dil == 4:
            self.items = [(0, r, h) for r in range(4) for h in range(4)]
        else:
            self.items = [(0, r, 0) for r in range(16)]
        self.groups = [self.items[i:i + 4] for i in range(0, 16, 4)]

    def operands(self, arr):
        return [arr] * self.hr

    def specs(self, col0, role="cur"):
        n, nb128 = self.n, self.L // B_SPAN
        out = []
        for h in range(self.hr):
            cb = col0 // self.cw + h
            if role == "cur":
                out.append(pl.BlockSpec((self.rows, self.cw), lambda b, i, hh, cb=cb: (b * n + i, cb + hh)))
            elif self.dil == 1:
                shift = -1 if role == "prev" else 4
                out.append(pl.BlockSpec((B_SPAN, self.cw),
                                        lambda b, i, hh, cb=cb, shift=shift: (b * nb128 + jnp.clip(4 * i + shift, 0, nb128 - 1), cb)))
            else:
                shift = -1 if role == "prev" else 1
                out.append(pl.BlockSpec((self.rows, self.cw),
                                        lambda b, i, hh, cb=cb, shift=shift: (b * n + jnp.clip(i + shift, 0, n - 1), cb)))
        return out

    def out_spec(self):
        n = self.n
        if self.dil == 4:
            return pl.BlockSpec((self.rows, 512), lambda b, i, hh: (b * n + i, 0))
        return pl.BlockSpec((self.rows, self.cw), lambda b, i, hh: (b * n + i, hh))

    def scratch(self, n_out):
        return [pltpu.VMEM((4, self.rows, A_HEAD), F32)] * n_out if self.dil == 4 else []

    def store(self, out_ref, scr, item, val):
        j, r, h = item
        if self.dil == 1:
            out_ref[pl.ds(j * B_SPAN, B_SPAN), pl.ds(h * A_HEAD, A_HEAD)] = val
        elif self.dil == 4:
            scr.at[h][pl.ds(r, B_SPAN, stride=4), :] = val
        else:
            out_ref[pl.ds(r, B_SPAN, stride=self.dil), :] = val

    def flush(self, out_ref, scr):
        if self.dil == 4:
            for h in range(4):
                out_ref[:, h * A_HEAD:(h + 1) * A_HEAD] = scr[h]

    def cur(self, refs, item):
        j, r, h = item
        if self.dil == 1:
            return refs[0], (pl.ds(j * B_SPAN, B_SPAN), pl.ds(h * A_HEAD, A_HEAD))
        return refs[h], (pl.ds(r, B_SPAN, stride=self.dil), slice(None))

    def other(self, refs, other_refs, item, role):
        j, r, h = item
        if self.dil == 1:
            cols = pl.ds(h * A_HEAD, A_HEAD)
            jj = j - 1 if role == "prev" else j + 1
            if 0 <= jj < 4:
                return refs[0], (pl.ds(jj * B_SPAN, B_SPAN), cols)
            return other_refs[0], (pl.ds(0, B_SPAN), cols)
        return other_refs[h], (pl.ds(r, B_SPAN, stride=self.dil), slice(None))

    def other_valid(self, item, role):
        j = item[0]
        i = pl.program_id(1)
        if role == "prev":
            return True if (self.dil == 1 and j > 0) else i > 0
        return True if (self.dil == 1 and j < 3) else i < self.n - 1


def _ld(pair):
    ref, idx = pair
    return ref[idx]


def _dil_fwd(qr, kr, z, dil, B, L, *, name):
    T = B * L
    plan = _DilPlan(dil, B, L)
    hr, has_prev = plan.hr, plan.has_other
    scale = A_HEAD ** -0.5

    n_t = 5 if has_prev else 3

    def body(*refs):
        lists = [refs[i * hr:(i + 1) * hr] for i in range(n_t)]
        o_ref, l_ref = refs[n_t * hr], refs[n_t * hr + 1]
        scr = refs[n_t * hr + 2:] or (None, None)
        if has_prev:
            q_r, kc_r, vc_r, kp_r, vp_r = lists
        else:
            q_r, kc_r, vc_r = lists
        mp0, mc = _band_masks()

        for group in plan.groups:
            qs = [_ld(plan.cur(q_r, i)).astype(BF16) for i in group]
            sc = [jnp.where(mc, _dot(q, _ld(plan.cur(kc_r, i)).astype(BF16), _NT) * scale, NEG_BIG) for q, i in zip(qs, group)]
            m = [jnp.max(s, axis=-1, keepdims=True) for s in sc]
            if has_prev:
                mps = [mp0 & plan.other_valid(i, "prev") for i in group]
                sp = [jnp.where(mk, _dot(q, _ld(plan.other(kc_r, kp_r, i, "prev")).astype(BF16), _NT) * scale, NEG_BIG)
                      for q, i, mk in zip(qs, group, mps)]
                m = [jnp.maximum(a, jnp.max(s, axis=-1, keepdims=True)) for a, s in zip(m, sp)]
            pc = [jnp.exp(s - a) for s, a in zip(sc, m)]
            l = [jnp.sum(p, axis=-1, keepdims=True) for p in pc]
            o = [_dot(p.astype(BF16), _ld(plan.cur(vc_r, i)).astype(BF16)) for p, i in zip(pc, group)]
            if has_prev:
                pp = [jnp.exp(s - a) for s, a in zip(sp, m)]
                l = [a + jnp.sum(p, axis=-1, keepdims=True) for a, p in zip(l, pp)]
                o = [a + _dot(p.astype(BF16), _ld(plan.other(vc_r, vp_r, i, "prev")).astype(BF16)) for a, p, i in zip(o, pp, group)]
            for i, oi, li, mi in zip(group, o, l, m):
                plan.store(o_ref, scr[0], i, oi / li)
                plan.store(l_ref, scr[1], i, jnp.broadcast_to(mi + jnp.log(li), (B_SPAN, A_HEAD)))
        plan.flush(o_ref, scr[0])
        plan.flush(l_ref, scr[1])

    tensors = [(qr, 0, "cur"), (kr, 0, "cur"), (z, 3072, "cur")] + ([(kr, 0, "prev"), (z, 3072, "prev")] if has_prev else [])
    return pl.pallas_call(
        body, name=name, grid=plan.grid,
        in_specs=[sp for _, c, role in tensors for sp in plan.specs(c, role)],
        out_specs=[plan.out_spec()] * 2, out_shape=[jax.ShapeDtypeStruct((T, 512), F32)] * 2,
        scratch_shapes=plan.scratch(2), compiler_params=_cparams("parallel", "parallel", "parallel"),
    )(*[a for arr, _, _ in tensors for a in plan.operands(arr)])


def _dil_combine(os_, ls_, *, name):
    T = os_[0].shape[0]
    tm = 256

    def body(o1, o2, o3, l1, l2, l3, ob_ref, lse_ref):
        a1, a2, a3 = l1[...], l2[...], l3[...]
        m = jnp.maximum(jnp.maximum(a1, a2), a3)
        e1, e2, e3 = jnp.exp(a1 - m), jnp.exp(a2 - m), jnp.exp(a3 - m)
        den = e1 + e2 + e3
        ob_ref[...] = (e1 * o1[...] + e2 * o2[...] + e3 * o3[...]) / den
        lse_ref[...] = m + jnp.log(den)

    blk = pl.BlockSpec((tm, 512), lambda i: (i, 0))
    return pl.pallas_call(
        body, name=name, grid=(T // tm,), in_specs=[blk] * 6, out_specs=[blk, blk],
        out_shape=[jax.ShapeDtypeStruct((T, 512), F32)] * 2, compiler_params=_cparams("parallel"),
    )(*[o.reshape(T, 512) for o in os_], *[l.reshape(T, 512) for l in ls_])


def _dil_bwd_q(qr, kr, z, dymix, out, lse, dil, B, L, *, name):
    T = B * L
    plan = _DilPlan(dil, B, L)
    hr, has_prev = plan.hr, plan.has_other
    scale = A_HEAD ** -0.5
    n_t = 8 if has_prev else 6

    def body(*refs):
        lists = [refs[i * hr:(i + 1) * hr] for i in range(n_t)]
        dq_ref = refs[n_t * hr]
        scr = refs[n_t * hr + 1:] or (None,)
        q_r, kc_r, vc_r, do_r, out_r, lse_r = lists[:6]
        mp0, mc = _band_masks()

        for group in plan.groups:
            qs = [_ld(plan.cur(q_r, i)).astype(BF16) for i in group]
            dos = [_ld(plan.cur(do_r, i)) for i in group]
            delta = [jnp.sum(d * _ld(plan.cur(out_r, i)), axis=-1, keepdims=True) for d, i in zip(dos, group)]
            dob = [d.astype(BF16) for d in dos]
            lse = [_ld(plan.cur(lse_r, i)) for i in group]
            kc = [_ld(plan.cur(kc_r, i)).astype(BF16) for i in group]
            pc = [jnp.where(mc, jnp.exp(_dot(q, k, _NT) * scale - a), 0.0) for q, k, a in zip(qs, kc, lse)]
            dsc = [p * (_dot(d, _ld(plan.cur(vc_r, i)).astype(BF16), _NT) - dl) * scale
                   for p, d, i, dl in zip(pc, dob, group, delta)]
            dq = [_dot(d.astype(BF16), k) for d, k in zip(dsc, kc)]
            if has_prev:
                kp_r, vp_r = lists[6], lists[7]
                mps = [mp0 & plan.other_valid(i, "prev") for i in group]
                kp = [_ld(plan.other(kc_r, kp_r, i, "prev")).astype(BF16) for i in group]
                pp = [jnp.where(mk, jnp.exp(_dot(q, k, _NT) * scale - a), 0.0) for q, k, a, mk in zip(qs, kp, lse, mps)]
                dsp = [p * (_dot(d, _ld(plan.other(vc_r, vp_r, i, "prev")).astype(BF16), _NT) - dl) * scale
                       for p, d, i, dl in zip(pp, dob, group, delta)]
                dq = [a + _dot(d.astype(BF16), k) for a, d, k in zip(dq, dsp, kp)]
            for i, d in zip(group, dq):
                plan.store(dq_ref, scr[0], i, d)
        plan.flush(dq_ref, scr[0])

    tensors = ([(qr, 0, "cur"), (kr, 0, "cur"), (z, 3072, "cur"), (dymix, 512, "cur"), (out, 0, "cur"), (lse, 0, "cur")]
               + ([(kr, 0, "prev"), (z, 3072, "prev")] if has_prev else []))
    return pl.pallas_call(
        body, name=name, grid=plan.grid,
        in_specs=[sp for _, c, role in tensors for sp in plan.specs(c, role)],
        out_specs=plan.out_spec(), out_shape=jax.ShapeDtypeStruct((T, 512), F32),
        scratch_shapes=plan.scratch(1), compiler_params=_cparams("parallel", "parallel", "parallel"),
    )(*[a for arr, _, _ in tensors for a in plan.operands(arr)])


def _dil_bwd_kv(qr, kr, z, dymix, out, lse, dil, B, L, *, name):
    T = B * L
    plan = _DilPlan(dil, B, L)
    hr, has_next = plan.hr, plan.has_other
    scale = A_HEAD ** -0.5
    n_t = 10 if has_next else 6

    def body(*refs):
        lists = [refs[i * hr:(i + 1) * hr] for i in range(n_t)]
        dk_ref, dv_ref = refs[n_t * hr], refs[n_t * hr + 1]
        scr = refs[n_t * hr + 2:] or (None, None)
        k_r, v_r = lists[0], lists[1]
        own = lists[2:6]
        mp0, mc = _band_masks()

        for group in plan.groups:
            kh = [_ld(plan.cur(k_r, i)).astype(BF16) for i in group]
            vh = [_ld(plan.cur(v_r, i)).astype(BF16) for i in group]
            dk, dv = [None] * len(group), [None] * len(group)
            for role in ("own", "next") if has_next else ("own",):
                if role == "own":
                    get = lambda t, i: _ld(plan.cur(own[t], i))
                    masks = [mc] * len(group)
                else:
                    get = lambda t, i: _ld(plan.other(own[t], lists[6 + t], i, "next"))
                    masks = [mp0 & plan.other_valid(i, "next") for i in group]
                qs = [get(0, i).astype(BF16) for i in group]
                dos = [get(1, i) for i in group]
                delta = [jnp.sum(d * get(2, i), axis=-1, keepdims=True) for d, i in zip(dos, group)]
                dob = [d.astype(BF16) for d in dos]
                p = [jnp.where(mk, jnp.exp(_dot(q, k, _NT) * scale - get(3, i)), 0.0) for q, k, i, mk in zip(qs, kh, group, masks)]
                dvn = [_dot(a.astype(BF16), d, _TN) for a, d in zip(p, dob)]
                ds = [a * (_dot(d, v, _NT) - dl) * scale for a, d, v, dl in zip(p, dob, vh, delta)]
                dkn = [_dot(d.astype(BF16), q, _TN) for d, q in zip(ds, qs)]
                dv = [n if o is None else o + n for o, n in zip(dv, dvn)]
                dk = [n if o is None else o + n for o, n in zip(dk, dkn)]
            for i, a, b in zip(group, dk, dv):
                plan.store(dk_ref, scr[0], i, a)
                plan.store(dv_ref, scr[1], i, b)
        plan.flush(dk_ref, scr[0])
        plan.flush(dv_ref, scr[1])

    queries = [(qr, 0), (dymix, 512), (out, 0), (lse, 0)]
    tensors = ([(kr, 0, "cur"), (z, 3072, "cur")] + [(a, c, "cur") for a, c in queries]
               + ([(a, c, "next") for a, c in queries] if has_next else []))
    return pl.pallas_call(
        body, name=name, grid=plan.grid,
        in_specs=[sp for _, c, role in tensors for sp in plan.specs(c, role)],
        out_specs=[plan.out_spec()] * 2, out_shape=[jax.ShapeDtypeStruct((T, 512), F32)] * 2,
        scratch_shapes=plan.scratch(2), compiler_params=_cparams("parallel", "parallel", "parallel"),
    )(*[a for arr, _, _ in tensors for a in plan.operands(arr)])


def _s5_build(lam_re, lam_im, log_dt, b_re, b_im, c_re, c_im):
    G, P, TC = C_GROUPS, C_STATE, C_TC
    lr = jnp.minimum(lam_re, C_MIN_NEG_RE)
    li = lam_im
    dt = jnp.exp(log_dt)[:, None]
    mag = jnp.exp(dt * lr)
    ar, ai = mag * jnp.cos(dt * li), mag * jnp.sin(dt * li)
    den = lr * lr + li * li
    zr = ((ar - 1.0) * lr + ai * li) / den
    zi = (ai * lr - (ar - 1.0) * li) / den
    bbr = zr[..., None] * b_re - zi[..., None] * b_im
    bbi = zr[..., None] * b_im + zi[..., None] * b_re
    ks = jnp.arange(TC + 1, dtype=F32)[:, None, None]
    pmag = jnp.exp(ks * (dt * lr)[None])
    pr, pi = pmag * jnp.cos(ks * (dt * li)[None]), pmag * jnp.sin(ks * (dt * li)[None])
    car = c_re[None] * pr[:, :, None, :] - c_im[None] * pi[:, :, None, :]
    cai = c_re[None] * pi[:, :, None, :] + c_im[None] * pr[:, :, None, :]
    kern = (jnp.einsum('lgop,gpc->lgco', car[:TC], bbr, precision=HI)
            - jnp.einsum('lgop,gpc->lgco', cai[:TC], bbi, precision=HI))
    pr_e, pi_e = pr[TC - 1 - jnp.arange(TC)], pi[TC - 1 - jnp.arange(TC)]
    er = pr_e[:, :, :, None] * bbr[None] - pi_e[:, :, :, None] * bbi[None]
    ei = pr_e[:, :, :, None] * bbi[None] + pi_e[:, :, :, None] * bbr[None]
    ez = jnp.stack([er, ei], axis=2).reshape(TC, C_NB, C_GB, 2, P, C_GROUP).transpose(1, 0, 2, 5, 3, 4)
    fz = jnp.stack([car[1:], -cai[1:]], axis=0).reshape(2, TC, C_NB, C_GB, C_GROUP, P).transpose(2, 0, 3, 5, 1, 4)
    return kern, ez, fz, pr[TC], pi[TC]


def _s5_lag_blocks(kern):
    eye = jnp.eye(C_GB, dtype=kern.dtype)
    return (kern.reshape(C_TC, C_NB, C_GB, C_GROUP, C_GROUP)[:, :, :, :, None, :]
            * eye[None, None, :, None, :, None]).reshape(C_TC, C_NB, LANES, LANES)


def _s5_dense(kbd, ez, fz, *, name):
    ezc = ez.reshape(C_NB, C_W8, LANES)
    fzc = fz.reshape(C_NB, C_S8, LANES)
    half = C_GB * C_STATE

    def body(k_ref, e_ref, f_ref, m8_ref, e8_ref, f8_ref):
        zero = jnp.zeros((LANES, LANES), BF16)
        for s in range(C_TC):
            for t in range(C_TC):
                m8_ref[0, s * LANES:(s + 1) * LANES, t * LANES:(t + 1) * LANES] = (
                    k_ref[t - s, 0].astype(BF16) if t >= s else zero)
        lane = lax.broadcasted_iota(jnp.int32, (1, LANES), 1)
        src = e_ref[0]
        swapped = pltpu.roll(src, C_STATE, 1)
        rowg = jnp.bitwise_and(jnp.right_shift(lax.broadcasted_iota(jnp.int32, (C_W8, LANES), 0), 4), C_GB - 1)
        for kb in range(C_S8 // LANES):
            z, g0 = kb // (C_GB // 2), 2 * (kb % (C_GB // 2))
            first = jnp.where(lane < C_STATE, src if z == 0 else swapped, 0.0)
            second = jnp.where(lane >= C_STATE, swapped if z == 0 else src, 0.0)
            e8_ref[0, :, kb * LANES:(kb + 1) * LANES] = jnp.where(
                rowg == g0, first, jnp.where(rowg == g0 + 1, second, 0.0)).astype(BF16)
        for z in range(2):
            for g in range(C_GB):
                rows = slice(z * half + g * C_STATE, z * half + (g + 1) * C_STATE)
                piece = f_ref[0, rows, :]
                mine = (lane >= g * C_GROUP) & (lane < (g + 1) * C_GROUP)
                for t in range(C_TC):
                    f8_ref[0, rows, t * LANES:(t + 1) * LANES] = jnp.where(
                        mine, pltpu.roll(piece, ((g - t) * C_GROUP) % LANES, 1), 0.0).astype(BF16)

    blk = lambda r, c: pl.BlockSpec((1, r, c), lambda b: (b, 0, 0))
    return pl.pallas_call(
        body, name=name, grid=(C_NB,),
        in_specs=[pl.BlockSpec((C_TC, 1, LANES, LANES), lambda b: (0, b, 0, 0)), blk(C_W8, LANES), blk(C_S8, LANES)],
        out_specs=[blk(C_W8, C_W8), blk(C_W8, C_S8), blk(C_S8, C_W8)],
        out_shape=[jax.ShapeDtypeStruct((C_NB, C_W8, C_W8), BF16), jax.ShapeDtypeStruct((C_NB, C_W8, C_S8), BF16),
                   jax.ShapeDtypeStruct((C_NB, C_S8, C_W8), BF16)],
        compiler_params=_cparams("parallel"),
    )(kbd, ezc, fzc)


C_NB = C_GROUPS * C_GROUP // LANES
C_GB = C_GROUPS // C_NB
C_W8 = C_TC * LANES
C_S8 = 2 * C_GB * C_STATE


def _s5_scan_tables(lam_re, lam_im, log_dt, nsteps):
    lr = jnp.minimum(lam_re, C_MIN_NEG_RE)
    dt = jnp.exp(log_dt)[:, None]
    ks = (C_TC * 2.0 ** jnp.arange(8, dtype=F32))[None, :, None]
    keep = (jnp.arange(8) < nsteps)[None, :, None]
    pmag = jnp.exp(ks * (dt * lr)[:, None, :])
    ang = ks * (dt * lam_im)[:, None, :]

    def blocks(t):
        return t.reshape(C_NB, C_GB, 8, C_STATE).transpose(0, 2, 1, 3).reshape(C_NB, 8, C_GB * C_STATE)

    pr = blocks(jnp.where(keep, pmag * jnp.cos(ang), 0.0))
    pi = blocks(jnp.where(keep, pmag * jnp.sin(ang), 0.0))
    return jnp.concatenate([pr, pr], axis=-1), jnp.concatenate([-pi, pi], axis=-1)


def _s5_rows(t, R):
    return pl.ds(t, R, stride=C_TC)


def _s5_fwd(u, dsk, m8, e8, f8, tab_r, tab_i, B, L, *, name, gather=()):
    T = B * L
    R = L // C_TC
    nsteps = int(math.log2(R))
    ex = _Exchange(gather=gather)
    grid = (C_NB, B)

    def body(u_ref, d_ref, m_ref, e_ref, f_ref, tr_ref, ti_ref, gl_ref, y_ref, x8_ref, xs_ref):
        for t in range(C_TC):
            x8_ref[0, :, t * LANES:(t + 1) * LANES] = u_ref[_s5_rows(t, R), :].astype(BF16)
        x8 = x8_ref[0]
        x = _dot(x8, e_ref[0])
        row = lax.broadcasted_iota(jnp.int32, (R, C_S8), 0)
        for k in range(nsteps):
            s = 1 << k
            sh = pltpu.roll(x, s, 0)
            upd = tr_ref[0, k:k + 1, :] * sh + ti_ref[0, k:k + 1, :] * pltpu.roll(sh, C_S8 // 2, 1)
            x = x + jnp.where(row >= s, upd, 0.0)
        xs = jnp.where(row >= 1, pltpu.roll(x, 1, 0), 0.0)
        xs_ref[0] = xs
        y8 = _dot(x8, m_ref[0]) + _dot(xs.astype(BF16), f_ref[0])
        d = d_ref[...]
        for t in range(C_TC):
            rows = _s5_rows(t, R)
            y = y8[:, t * LANES:(t + 1) * LANES] + d * u_ref[rows, :]
            y_ref[rows, :] = y
            gl_ref[rows, :] = 0.5 * y * (1.0 + _erf(y * (2.0 ** -0.5)))

    tok = pl.BlockSpec((L, LANES), lambda c, b: (b, c))
    per_block = lambda shape: pl.BlockSpec((1,) + shape, lambda c, b: (c, 0, 0))
    per_step = lambda shape: pl.BlockSpec((1,) + shape, lambda c, b: (c * B + b, 0, 0))
    res = pl.pallas_call(
        _carrying(body, 7, 4, ex, grid) if ex.n else body, name=name, grid=grid,
        in_specs=[tok, pl.BlockSpec((1, LANES), lambda c, b: (0, c)), per_block((C_W8, C_W8)), per_block((C_W8, C_S8)),
                  per_block((C_S8, C_W8)), per_block((8, C_S8)), per_block((8, C_S8))] + ex.in_specs(),
        out_specs=[tok, tok, per_step((R, C_W8)), per_step((R, C_S8))] + ex.in_specs(),
        out_shape=[jax.ShapeDtypeStruct((T, D_MODEL), F32), jax.ShapeDtypeStruct((T, D_MODEL), F32),
                   jax.ShapeDtypeStruct((C_NB * B, R, C_W8), BF16), jax.ShapeDtypeStruct((C_NB * B, R, C_S8), F32)] + ex.out_shapes(),
        scratch_shapes=ex.scratch(),
        compiler_params=_cparams(*(("arbitrary",) * 2 if ex.n else ("parallel", "parallel"))),
    )(u, dsk, m8, e8, f8, tab_r, tab_i, *ex.operands())
    return tuple(res[:4]) + (list(res[4:]),)


def _s5_bwd(dgl, y, u, dsk, xs, m8, e8, f8, tab_r, tab_i, B, L, *, name):
    T = B * L
    R = L // C_TC
    nsteps = int(math.log2(R))

    def body(dgl_ref, y_ref, u_ref, d_ref, xs_ref, m_ref, e_ref, f_ref, tr_ref, ti_ref,
             du_ref, dy8_ref, de_ref, da_ref, dd_ref, dyf_ref):
        @pl.when(pl.program_id(1) == 0)
        def _():
            da_ref[...] = jnp.zeros_like(da_ref)
            dd_ref[...] = jnp.zeros_like(dd_ref)

        dd = jnp.zeros((1, LANES), F32)
        for t in range(C_TC):
            rows = _s5_rows(t, R)
            yv = y_ref[rows, :]
            cdf = 0.5 * (1.0 + _erf(yv * (2.0 ** -0.5)))
            pdf = jnp.exp(-0.5 * yv * yv) * (1.0 / math.sqrt(2.0 * math.pi))
            dy = dgl_ref[rows, :] * (cdf + yv * pdf)
            dd = dd + jnp.sum(dy * u_ref[rows, :], axis=0, keepdims=True)
            dyf_ref[:, t * LANES:(t + 1) * LANES] = dy
        dd_ref[...] += dd
        dy8 = dyf_ref[...].astype(BF16)
        dy8_ref[0] = dy8
        xs = xs_ref[0]
        gx = _dot(dy8, f_ref[0], _NT)
        row = lax.broadcasted_iota(jnp.int32, (R, C_S8), 0)
        for k in range(nsteps):
            s = 1 << k
            sh = pltpu.roll(gx, R - s, 0)
            upd = tr_ref[0, k:k + 1, :] * sh - ti_ref[0, k:k + 1, :] * pltpu.roll(sh, C_S8 // 2, 1)
            gx = gx + jnp.where(row + s < R, upd, 0.0)
        de_in = jnp.where(row + 1 < R, pltpu.roll(gx, R - 1, 0), 0.0)
        deb = de_in.astype(BF16)
        de_ref[0] = deb
        da_ref[0, 0:1, :] += jnp.sum(de_in * xs, axis=0, keepdims=True)
        da_ref[0, 1:2, :] += jnp.sum(de_in * pltpu.roll(xs, C_S8 // 2, 1), axis=0, keepdims=True)
        dx8 = _dot(dy8, m_ref[0], _NT) + _dot(deb, e_ref[0], _NT)
        d = d_ref[...]
        for t in range(C_TC):
            cols = slice(t * LANES, (t + 1) * LANES)
            du_ref[_s5_rows(t, R), :] = dx8[:, cols] + d * dyf_ref[:, cols]

    tok = pl.BlockSpec((L, LANES), lambda c, b: (b, c))
    vec = pl.BlockSpec((1, LANES), lambda c, b: (0, c))
    per_block = lambda shape: pl.BlockSpec((1,) + shape, lambda c, b: (c, 0, 0))
    per_step = lambda shape: pl.BlockSpec((1,) + shape, lambda c, b: (c * B + b, 0, 0))
    return pl.pallas_call(
        body, name=name, grid=(C_NB, B),
        in_specs=[tok, tok, tok, vec, per_step((R, C_S8)), per_block((C_W8, C_W8)),
                  per_block((C_W8, C_S8)), per_block((C_S8, C_W8)), per_block((8, C_S8)), per_block((8, C_S8))],
        out_specs=[tok, per_step((R, C_W8)), per_step((R, C_S8)), per_block((8, C_S8)), vec],
        out_shape=[jax.ShapeDtypeStruct((T, D_MODEL), F32), jax.ShapeDtypeStruct((C_NB * B, R, C_W8), BF16),
                   jax.ShapeDtypeStruct((C_NB * B, R, C_S8), BF16), jax.ShapeDtypeStruct((C_NB, 8, C_S8), F32),
                   jax.ShapeDtypeStruct((1, D_MODEL), F32)],
        scratch_shapes=[pltpu.VMEM((R, C_W8), F32)],
        compiler_params=_cparams("parallel", "arbitrary"),
    )(dgl, y, u, dsk, xs, m8, e8, f8, tab_r, tab_i)


def _bmm_tn(a, b, nb, fold, *, name):
    a = a.reshape(nb, -1, a.shape[-1])
    b = b.reshape(nb, -1, b.shape[-1])
    K, M, N = a.shape[1], a.shape[2], b.shape[2]
    half = C_GB * C_STATE

    def body(a_ref, b_ref, o_ref, p_ref):
        p_ref[...] = _dot(a_ref[0].astype(BF16), b_ref[0].astype(BF16), _TN)
        lane = lax.broadcasted_iota(jnp.int32, (1, LANES), 1)
        if fold == "lags":
            for lag in range(C_TC):
                blocks = [p_ref[s * LANES:(s + 1) * LANES, (s + lag) * LANES:(s + lag + 1) * LANES] for s in range(C_TC - lag)]
                o_ref[0, lag] = functools.reduce(lambda u, v: u + v, blocks)
        elif fold == "e":
            for g in range(C_GB):
                lo, hi = LANES * (g // 2), half + LANES * (g // 2)
                for s in range(C_TC):
                    rows = slice(s * LANES + g * C_GROUP, s * LANES + (g + 1) * C_GROUP)
                    re, im = p_ref[rows, lo:lo + LANES], p_ref[rows, hi:hi + LANES]
                    if g % 2 == 0:
                        im = pltpu.roll(im, C_STATE, 1)
                    else:
                        re = pltpu.roll(re, C_STATE, 1)
                    o_ref[0, rows, :] = jnp.where(lane < C_STATE, re, im)
        else:
            for z in range(2):
                for g in range(C_GB):
                    rows = slice(z * half + g * C_STATE, z * half + (g + 1) * C_STATE)
                    val = jnp.zeros((C_STATE, LANES), F32)
                    for t in range(C_TC):
                        blk = pltpu.roll(p_ref[rows, t * LANES:(t + 1) * LANES], ((t - g) * C_GROUP) % LANES, 1)
                        val = jnp.where((lane >= t * C_GROUP) & (lane < (t + 1) * C_GROUP), blk, val)
                    o_ref[0, rows, :] = val

    if fold == "lags":
        out_spec = pl.BlockSpec((1, C_TC, LANES, LANES), lambda c: (c, 0, 0, 0))
        out_shape = jax.ShapeDtypeStruct((nb, C_TC, LANES, LANES), F32)
    else:
        out_spec = pl.BlockSpec((1, M, LANES), lambda c: (c, 0, 0))
        out_shape = jax.ShapeDtypeStruct((nb, M, LANES), F32)
    return pl.pallas_call(
        body, name=name, grid=(nb,),
        in_specs=[pl.BlockSpec((1, K, M), lambda c: (c, 0, 0)), pl.BlockSpec((1, K, N), lambda c: (c, 0, 0))],
        out_specs=out_spec, out_shape=out_shape, scratch_shapes=[pltpu.VMEM((M, N), F32)],
        compiler_params=_cparams("parallel"),
    )(a, b)


def _loss_head(y, target, *, name):
    T, C = y.shape
    tm = 256

    def body(y_ref, t_ref, l_ref, d_ref):
        err = y_ref[...] - t_ref[...]
        d_ref[...] = err * (1.0 / C)
        sq = err * err
        part = jnp.zeros((8, LANES), F32)
        for r in range(0, tm, 8):
            for c in range(0, C, LANES):
                part = part + sq[r:r + 8, c:c + LANES]

        @pl.when(pl.program_id(0) == 0)
        def _():
            l_ref[...] = part

        @pl.when(pl.program_id(0) > 0)
        def _():
            l_ref[...] += part

    row = pl.BlockSpec((tm, C), lambda i: (i, 0))
    acc = pl.BlockSpec((8, LANES), lambda i: (0, 0))
    return pl.pallas_call(
        body, name=name, grid=(T // tm,), in_specs=[row, row], out_specs=[acc, row],
        out_shape=[jax.ShapeDtypeStruct((8, LANES), F32), jax.ShapeDtypeStruct((T, C), F32)],
        compiler_params=_cparams("arbitrary"),
    )(y, target)


def _adamw(w, g, m, v, *, name):
    shape = w.shape
    size = int(np.prod(shape))
    cols = LANES if (shape[-1] < LANES and size % LANES == 0) else shape[-1]
    rows = size // cols
    tm = _tile(rows, 256) if rows % 8 == 0 else rows
    w2, g2, m2, v2 = (t.reshape(rows, cols) for t in (w, g, m, v))

    def body(w_ref, g_ref, m_ref, v_ref, d_ref, nm_ref, nv_ref):
        gg = g_ref[...]
        nm = ADAM_B1 * m_ref[...] + (1.0 - ADAM_B1) * gg
        nv = ADAM_B2 * v_ref[...] + (1.0 - ADAM_B2) * (gg * gg)
        m_hat = nm / (1.0 - ADAM_B1 ** ADAM_STEP)
        v_hat = nv / (1.0 - ADAM_B2 ** ADAM_STEP)
        d_ref[...] = -ADAM_LR * (m_hat / (jnp.sqrt(v_hat) + ADAM_EPS) + ADAM_WD * w_ref[...])
        nm_ref[...] = nm
        nv_ref[...] = nv

    blk = pl.BlockSpec((tm, cols), lambda i: (i, 0))
    outs = pl.pallas_call(
        body, name=name, grid=(rows // tm,), in_specs=[blk] * 4, out_specs=[blk] * 3,
        out_shape=[jax.ShapeDtypeStruct((rows, cols), F32)] * 3, compiler_params=_cparams("parallel"),
    )(w2, g2, m2, v2)
    return tuple(o.reshape(shape) for o in outs)


def _all_gather(shard, *, name):
    R, C = shard.shape

    def body(x_ref, out_ref, send_sems, recv_sems, local_sem):
        x, y, c = lax.axis_index("x"), lax.axis_index("y"), lax.axis_index("c")
        me, sibling = (x, y, c), (x, y, 1 - c)
        chips = [(1 - x, y), (x, 1 - y), (1 - x, 1 - y)]

        def rows(px, py, pc):
            return out_ref.at[_logical(px, py, pc)]

        def copy(k, block, to, src=None):
            return pltpu.make_async_remote_copy(
                src_ref=rows(*block) if src is None else src, dst_ref=rows(*block),
                send_sem=send_sems.at[k], recv_sem=recv_sems.at[k], device_id=to, device_id_type=_MESH)

        mine = pltpu.make_async_copy(x_ref, rows(*me), local_sem)
        mine.start()
        first = [copy(0, me, sibling, src=x_ref)]
        first += [copy(1 + j, me, (*chip, c), src=x_ref) for j, chip in enumerate(chips)]
        for cp in first:
            cp.start()
        passed = [copy(4 + j, (*chip, c), sibling) for j, chip in enumerate(chips)]
        for j, chip in enumerate(chips):
            copy(1 + j, (*chip, c), me).wait_recv()
            passed[j].start()
        copy(0, sibling, me).wait_recv()
        for j, chip in enumerate(chips):
            copy(4 + j, (*chip, 1 - c), me).wait_recv()
        for cp in first + passed:
            cp.wait_send()
        mine.wait()

    return pl.pallas_call(
        body, name=name, out_shape=jax.ShapeDtypeStruct((N_DEV, R, C), shard.dtype),
        in_specs=[_HBM], out_specs=_HBM,
        scratch_shapes=[pltpu.SemaphoreType.DMA((7,)), pltpu.SemaphoreType.DMA((7,)), pltpu.SemaphoreType.DMA],
    )(shard)


def _gather_weights(shards, *, name):
    nt = len(shards)

    def body(*refs):
        ins, outs = refs[:nt], refs[nt:2 * nt]
        send_sems, recv_sems, local_sems = refs[2 * nt:]
        x, y, c = lax.axis_index("x"), lax.axis_index("y"), lax.axis_index("c")
        me, sibling = (x, y, c), (x, y, 1 - c)
        chips = [(1 - x, y), (x, 1 - y), (1 - x, 1 - y)]

        def copy(t, k, block, to, src=None):
            rows = outs[t].at[_logical(*block)]
            return pltpu.make_async_remote_copy(
                src_ref=rows if src is None else src, dst_ref=rows,
                send_sem=send_sems.at[t, k], recv_sem=recv_sems.at[t, k], device_id=to, device_id_type=_MESH)

        mine = [pltpu.make_async_copy(ins[t], outs[t].at[_logical(*me)], local_sems.at[t]) for t in range(nt)]
        for cp in mine:
            cp.start()
        started = []
        for t in range(nt):
            started.append(copy(t, 0, me, sibling, src=ins[t]))
            started += [copy(t, 1 + j, me, (*chip, c), src=ins[t]) for j, chip in enumerate(chips)]
        for cp in started:
            cp.start()
        for j, chip in enumerate(chips):
            for t in range(nt):
                copy(t, 1 + j, (*chip, c), me).wait_recv()
                fwd = copy(t, 4 + j, (*chip, c), sibling)
                fwd.start()
                started.append(fwd)
        for t in range(nt):
            copy(t, 0, sibling, me).wait_recv()
            for j, chip in enumerate(chips):
                copy(t, 4 + j, (*chip, 1 - c), me).wait_recv()
        for cp in started:
            cp.wait_send()
        for cp in mine:
            cp.wait()

    return pl.pallas_call(
        body, name=name, out_shape=[jax.ShapeDtypeStruct((N_DEV,) + s.shape, s.dtype) for s in shards],
        in_specs=[_HBM] * nt, out_specs=[_HBM] * nt,
        scratch_shapes=[pltpu.SemaphoreType.DMA((nt, 7)), pltpu.SemaphoreType.DMA((nt, 7)), pltpu.SemaphoreType.DMA((nt,))],
    )(*shards)


def _sum_rows(stacked, *, name):
    _, R, C = stacked.shape
    tr = R
    if N_DEV * R * C * stacked.dtype.itemsize > 12 * 1024 * 1024:
        for cand in range(512, 15, -16):
            if R % cand == 0:
                tr = cand
                break

    def body(s_ref, o_ref):
        acc = s_ref[0].astype(F32)
        for k in range(1, N_DEV):
            acc = acc + s_ref[k].astype(F32)
        o_ref[...] = acc

    return pl.pallas_call(
        body, name=name, grid=(R // tr,),
        in_specs=[pl.BlockSpec((N_DEV, tr, C), lambda i: (0, i, 0))], out_specs=pl.BlockSpec((tr, C), lambda i: (i, 0)),
        out_shape=jax.ShapeDtypeStruct((R, C), F32), compiler_params=_cparams("parallel"),
    )(stacked)


_LARGE = (("ab_w_in", 1, True), ("ab_w_out", 1, False), ("s5_w_glu", 1, True), ("xattn_wq", 2, False),
          ("xattn_wkv", 2, True), ("xattn_wo", 2, False), ("ffn_w_in", 2, True), ("ffn_w_out", 2, False))
_LARGE_KEYS = tuple((n, l) for n, layers, _ in _LARGE for l in range(layers))
_TRANSPOSED = {n: t for n, _, t in _LARGE}


def _owner_major(name, w):
    return w.T if _TRANSPOSED[name] else w


def _lb_from_logits(logits):
    return jnp.cumsum(jax.nn.softmax(logits, axis=0), axis=0)[0:1]


def _local_step(x, mem, target, W, shards=None):
    B, L, _ = x.shape
    T = B * L
    x0 = x.reshape(T, D_MODEL)
    memf = mem.reshape(B * MEM_LEN, D_MODEL)
    nw = W["norm_w"]
    cos2, sin2 = _rope_tables(L)
    W = dict(W)
    G, received = {}, {}

    def mmx(a, b, gather=(), scatter=(), **kw):
        if shards is None or not (gather or scatter):
            return _mm(a, b, **kw)
        out, gathered, got = _mm(a, b, gather=[shards[k] for k in gather],
                                 scatter=[G[k].reshape(N_DEV, -1, D_MODEL) for k in scatter], **kw)
        for k, g in zip(gather, gathered):
            W[k] = g.reshape(-1, D_MODEL)
        for k, r in zip(scatter, got):
            received[k] = r
        return out

    def vec(v):
        return v.reshape(1, -1)

    saved = []
    xin = x0
    for layer in range(2):
        s = {"x0": xin}
        tag = f"l{layer}"
        if layer == 0:
            h1 = _rms_fwd(xin, vec(nw[0, 0]), name="norm_pre_mix_l0", out_dtype=BF16)
        s["h1"] = h1
        if layer == 0:
            lb, lb_vjp = jax.vjp(_lb_from_logits, W["hgrn_lb_logits"])
            onw = W["hgrn_out_norm_w"].reshape(1, A_WIDTH)
            z = mmx(h1, W["ab_w_in", 0], tb=True, name="ab_in",
                    gather=[("ab_w_out", 0), ("xattn_wq", 0), ("xattn_wkv", 0), ("xattn_wo", 0)])
            ffn0 = [("ffn_w_in", 0), ("ffn_w_out", 0)] if shards is not None else []
            oa, o_raw, s_start, gathered = _hgrn_fwd(z, lb, onw, B, L, name="hgrn_fwd", gather=[shards[k] for k in ffn0])
            for key, g in zip(ffn0, gathered):
                W[key] = g.reshape(-1, D_MODEL)
            qr, kr = _rope_fwd(z, cos2, sin2, B, L, name="rope_qk")
            os_, ls_ = [], []
            for dil in B_DILS:
                o_g, l_g = _dil_fwd(qr, kr, z, dil, B, L, name=f"dil_fwd_{dil}")
                os_.append(o_g)
                ls_.append(l_g)
            ob, lse = _dil_combine(os_, ls_, name="dil_combine")
            ymix = jnp.concatenate([oa, ob], axis=-1).astype(BF16)
            y1 = _mm(ymix, W["ab_w_out", 0], out_dtype=BF16, name="ab_out")
            s.update(z=z, lb=lb, lb_vjp=lb_vjp, onw=onw, o_raw=o_raw, s_start=s_start, qr=qr, kr=kr, ob=ob, lse=lse, ymix=ymix)
        else:
            p5 = tuple(W[n][0] for n in ("s5_lambda_re", "s5_lambda_im", "s5_log_dt", "s5_b_re", "s5_b_im", "s5_c_re", "s5_c_im"))
            (kern, ez, fz, _, _), s5_vjp = jax.vjp(_s5_build, *p5)
            tab_r, tab_i = _s5_scan_tables(p5[0], p5[1], p5[2], int(math.log2(L // C_TC)))
            kbd, lag_vjp = jax.vjp(_s5_lag_blocks, kern)
            mats = tuple(_s5_dense(kbd, ez, fz, name="s5_maps")) + (tab_r, tab_i)
            dsk = W["s5_d"].reshape(1, D_MODEL)
            ffn1 = [("ffn_w_in", 1)] if shards is not None else []
            gl, ypre, x8, xs, gathered = _s5_fwd(h1, dsk, *mats, B, L, name="s5_fwd", gather=[shards[k] for k in ffn1])
            for key, g in zip(ffn1, gathered):
                W[key] = g.reshape(-1, D_MODEL)
            w_glu = _interleave_rows(W["s5_w_glu", 0])
            zg, y1 = _mm(gl, w_glu, tb=True, name="s5_glu_in", gate=("fwd", "glu"))
            s.update(s5_vjp=s5_vjp, lag_vjp=lag_vjp, mats=mats, x8=x8, xs=xs, dsk=dsk, gl=gl, ypre=ypre, zg=zg, w_glu=w_glu)
        x1, h2 = _rms_fwd(y1, vec(nw[layer, 1]), xin, name=f"norm_post_mix_{tag}", then=(vec(nw[layer, 2]), BF16))
        memn = _rms_fwd(memf, vec(W["mem_norm_w"][layer]), name=f"norm_mem_{tag}", out_dtype=BF16)
        q = _mm(h2, W["xattn_wq", layer], out_dtype=BF16, name=f"x_q_{tag}")
        kv = _mm(memn, W["xattn_wkv", layer], tb=True, out_dtype=BF16, name=f"x_kv_{tag}")
        o = _xattn_fwd(q, kv, B, L, name=f"x_attn_{tag}")
        y2 = _mm(o, W["xattn_wo", layer], out_dtype=BF16, name=f"x_o_{tag}")
        x2, h3 = _rms_fwd(y2, vec(nw[layer, 3]), x1, name=f"norm_post_x_{tag}", then=(vec(nw[layer, 4]), BF16))
        w_ffn_in = _interleave_rows(W["ffn_w_in", layer])
        zf, u = mmx(h3, w_ffn_in, tb=True, name=f"ffn_in_{tag}", gate=("fwd", "swiglu"),
                    gather=[("s5_w_glu", 0), ("xattn_wq", 1), ("xattn_wkv", 1), ("xattn_wo", 1)] if layer == 0 else [])
        y3 = mmx(u, W["ffn_w_out", layer], out_dtype=BF16, name=f"ffn_out_{tag}", gather=[("ffn_w_out", 1)] if layer == 0 else [])
        if layer == 0:
            x3, h1 = _rms_fwd(y3, vec(nw[0, 5]), x2, name="norm_post_ffn_l0", then=(vec(nw[1, 0]), F32))
        else:
            x3 = _rms_fwd(y3, vec(nw[layer, 5]), x2, name=f"norm_post_ffn_{tag}")
        s.update(y1=y1, x1=x1, h2=h2, memn=memn, q=q, kv=kv, o=o, y2=y2, x2=x2, h3=h3, zf=zf, u=u, y3=y3, w_ffn_in=w_ffn_in)
        saved.append(s)
        xin = x3

    loss_parts, dx = _loss_head(xin, target.reshape(T, D_MODEL), name="loss_head")

    d_norm = [[None] * 6 for _ in range(2)]
    d_memn = [None, None]
    for layer in (1, 0):
        s = saved[layer]
        tag = f"l{layer}"
        if layer == 1:
            dy3, d_norm[1][5] = _rms_bwd(s["y3"], vec(nw[1, 5]), dx, name="bnorm_post_ffn_l1", out_dtype=BF16)
        dzf = mmx(dy3, W["ffn_w_out", layer], tb=True, name=f"b_ffn_out_dx_{tag}", gate=("bwd", "swiglu", s["zf"]),
                  scatter=[("xattn_wkv", 1), ("s5_w_glu", 0)] if layer == 0 else [])
        G["ffn_w_out", layer] = _mm(s["u"], dy3, ta=True, out_dtype=BF16, name=f"b_ffn_out_dw_{tag}")
        early = []
        if shards is not None and layer == 0:
            shards = {**shards, "small_early": _pack_small([G[n] for n, _ in _SMALL_EARLY])}
            early = ["small_early"]
        G["ffn_w_in", layer] = _interleave_rows(
            mmx(dzf, s["h3"], ta=True, out_dtype=BF16, name=f"b_ffn_in_dw_{tag}", gather=early,
                scatter=[("ffn_w_out", layer)]), inverse=True)
        if early:
            received["small_early"] = W["small_early"].reshape(N_DEV, -1, LANES)
        dh3 = mmx(dzf, s["w_ffn_in"], out_dtype=BF16, name=f"b_ffn_in_dx_{tag}", scatter=[("ffn_w_in", layer)])
        dx, d_norm[layer][4], dy2, d_norm[layer][3] = _rms_bwd(
            s["x2"], vec(nw[layer, 4]), dh3, dx, name=f"bnorm_pre_ffn_{tag}", then=(s["y2"], vec(nw[layer, 3])))
        do = _mm(dy2, W["xattn_wo", layer], tb=True, out_dtype=BF16, name=f"b_x_o_dx_{tag}")
        G["xattn_wo", layer] = _mm(s["o"], dy2, ta=True, out_dtype=BF16, name=f"b_x_o_dw_{tag}")
        dq, dkv = _xattn_bwd(s["q"], s["kv"], do, B, L, name=f"b_x_attn_{tag}")
        G["xattn_wq", layer] = _mm(s["h2"], dq, ta=True, out_dtype=BF16, name=f"b_x_q_dw_{tag}")
        dh2 = _mm(dq, W["xattn_wq", layer], tb=True, out_dtype=BF16, name=f"b_x_q_dx_{tag}")
        G["xattn_wkv", layer] = _mm(dkv, s["memn"], ta=True, out_dtype=BF16, name=f"b_x_kv_dw_{tag}")
        dmemn = _mm(dkv, W["xattn_wkv", layer], out_dtype=BF16, name=f"b_x_kv_dx_{tag}")
        _, d_memn[layer] = _rms_bwd(memf, vec(W["mem_norm_w"][layer]), dmemn, name=f"bnorm_mem_{tag}", out_dtype=BF16)
        dx, d_norm[layer][2], dy1, d_norm[layer][1] = _rms_bwd(
            s["x1"], vec(nw[layer, 2]), dh2, dx, name=f"bnorm_pre_x_{tag}", then=(s["y1"], vec(nw[layer, 1])))
        if layer == 0:
            z = s["z"]
            dymix = _mm(dy1, W["ab_w_out", 0], tb=True, name="b_ab_out_dx")
            G["ab_w_out", 0] = _mm(s["ymix"], dy1, ta=True, out_dtype=BF16, name="b_ab_out_dw")
            early = [("xattn_wo", 0), ("xattn_wq", 0), ("xattn_wkv", 0), ("ab_w_out", 0)] if shards is not None else []
            dqa, dfa, dia, dga, d_onw, d_lb, got = _hgrn_bwd(
                z, s["lb"], s["onw"], s["o_raw"], s["s_start"], dymix, B, L, name="hgrn_bwd",
                scatter=[G[k].reshape(N_DEV, -1, D_MODEL) for k in early])
            received.update(zip(early, got))
            dqs, dks, dvs = [], [], []
            for dil in B_DILS:
                dqs.append(_dil_bwd_q(s["qr"], s["kr"], z, dymix, s["ob"], s["lse"], dil, B, L, name=f"dil_bwd_q_{dil}"))
                dk_g, dv_g = _dil_bwd_kv(s["qr"], s["kr"], z, dymix, s["ob"], s["lse"], dil, B, L, name=f"dil_bwd_kv_{dil}")
                dks.append(dk_g)
                dvs.append(dv_g)
            dqkv = _rope_bwd(dqs, dks, dvs, cos2, sin2, B, L, name="b_rope")
            dz = jnp.concatenate([dqa, dfa, dia, dga, dqkv], axis=-1).astype(BF16)
            G["ab_w_in", 0] = _mm(dz, s["h1"], ta=True, out_dtype=BF16, name="b_ab_in_dw")
            dh1 = mmx(dz, W["ab_w_in", 0], out_dtype=BF16, name="b_ab_in_dx", scatter=[("ab_w_in", 0)])
            G["hgrn_out_norm_w"] = jnp.sum(d_onw.reshape(B, A_WIDTH), axis=0, keepdims=True)
            d_lb_row = jnp.sum(d_lb.reshape(B, A_WIDTH), axis=0, keepdims=True)
            G["hgrn_lb_logits"] = s["lb_vjp"](d_lb_row)[0]
        else:
            dzg = _gated_bwd(s["zg"], dy1, "glu", name="b_s5_glu")
            G["s5_w_glu", 0] = _interleave_rows(
                _mm(dzg, s["gl"], ta=True, out_dtype=BF16, name="b_s5_glu_dw"), inverse=True)
            dgl = mmx(dzg, s["w_glu"], name="b_s5_glu_dx", scatter=[("xattn_wo", 1), ("xattn_wq", 1)])
            dh1, dy8, de_in, da, d_dsk = _s5_bwd(dgl, s["ypre"], s["h1"], s["dsk"], s["xs"], *s["mats"], B, L, name="s5_bwd")
            dkbd = _bmm_tn(s["x8"], dy8, C_NB, "lags", name="s5_bwd_dm").transpose(1, 0, 2, 3)
            dfz = _bmm_tn(s["xs"], dy8, C_NB, "f", name="s5_bwd_df").reshape(C_NB, 2, C_GB, C_STATE, C_TC, C_GROUP)
            dez = _bmm_tn(s["x8"], de_in, C_NB, "e", name="s5_bwd_de").reshape(C_NB, C_TC, C_GB, C_GROUP, 2, C_STATE)
            half = C_S8 // 2
            da_r = (da[:, 0, :half] + da[:, 0, half:]).reshape(C_GROUPS, C_STATE)
            da_i = (da[:, 1, half:] - da[:, 1, :half]).reshape(C_GROUPS, C_STATE)
            gp = s["s5_vjp"](s["lag_vjp"](dkbd) + (dez, dfz, da_r, da_i))
            for n, gv in zip(("s5_lambda_re", "s5_lambda_im", "s5_log_dt", "s5_b_re", "s5_b_im", "s5_c_re", "s5_c_im"), gp):
                G[n] = gv[None]
            G["s5_d"] = d_dsk
        if layer == 1:
            dx, d_norm[1][0], dy3, d_norm[0][5] = _rms_bwd(
                s["x0"], vec(nw[1, 0]), dh1, dx, name="bnorm_pre_mix_l1", then=(saved[0]["y3"], vec(nw[0, 5])))
        else:
            dx, d_norm[0][0] = _rms_bwd(s["x0"], vec(nw[0, 0]), dh1, dx, name="bnorm_pre_mix_l0")

    G["norm_w"] = jnp.stack([jnp.concatenate(d_norm[l], axis=0) for l in range(2)])
    G["mem_norm_w"] = jnp.concatenate(d_memn, axis=0)
    if shards is not None:
        G.update(received)
    return loss_parts, dx.reshape(B, L, D_MODEL), G


_SMALL_LATE = (("norm_w", (2, 6, 1024)), ("mem_norm_w", (2, 1024)), ("hgrn_lb_logits", (3, 512)), ("hgrn_out_norm_w", (1, 512)))
_SMALL_EARLY = (("s5_lambda_re", (1, 64, 64)), ("s5_lambda_im", (1, 64, 64)), ("s5_log_dt", (1, 64)),
                ("s5_b_re", (1, 64, 64, 16)), ("s5_b_im", (1, 64, 64, 16)), ("s5_c_re", (1, 64, 16, 64)),
                ("s5_c_im", (1, 64, 16, 64)), ("s5_d", (1, 1024)))
_SMALL = _SMALL_LATE + _SMALL_EARLY


def _pack_small(values):
    flat = jnp.concatenate([v.reshape(-1) for v in values])
    return jnp.pad(flat, (0, (-flat.shape[0]) % (8 * LANES))).reshape(-1, LANES)


def _unpack_small(flat, spec):
    out, off = {}, 0
    for n, shp in spec:
        size = int(np.prod(shp))
        out[n] = flat[off:off + size].reshape(shp)
        off += size
    return out, off

_WEIGHT_ORDER = ('norm_w', 'mem_norm_w', 'ab_w_in', 'ab_w_out', 'hgrn_lb_logits', 'hgrn_out_norm_w', 's5_lambda_re',
                 's5_lambda_im', 's5_log_dt', 's5_b_re', 's5_b_im', 's5_c_re', 's5_c_im', 's5_d', 's5_w_glu', 'xattn_wq',
                 'xattn_wkv', 'xattn_wo', 'ffn_w_in', 'ffn_w_out')


def kernel(x, mem, norm_w, mem_norm_w, ab_w_in, ab_w_out, hgrn_lb_logits, hgrn_out_norm_w, s5_lambda_re, s5_lambda_im, s5_log_dt, s5_b_re, s5_b_im, s5_c_re, s5_c_im, s5_d, s5_w_glu, xattn_wq, xattn_wkv, xattn_wo, ffn_w_in, ffn_w_out, loss_target, m_norm_w, m_mem_norm_w, m_ab_w_in, m_ab_w_out, m_hgrn_lb_logits, m_hgrn_out_norm_w, m_s5_lambda_re, m_s5_lambda_im, m_s5_log_dt, m_s5_b_re, m_s5_b_im, m_s5_c_re, m_s5_c_im, m_s5_d, m_s5_w_glu, m_xattn_wq, m_xattn_wkv, m_xattn_wo, m_ffn_w_in, m_ffn_w_out, v_norm_w, v_mem_norm_w, v_ab_w_in, v_ab_w_out, v_hgrn_lb_logits, v_hgrn_out_norm_w, v_s5_lambda_re, v_s5_lambda_im, v_s5_log_dt, v_s5_b_re, v_s5_b_im, v_s5_c_re, v_s5_c_im, v_s5_d, v_s5_w_glu, v_xattn_wq, v_xattn_wkv, v_xattn_wo, v_ffn_w_in, v_ffn_w_out):
    local = dict(norm_w=norm_w, mem_norm_w=mem_norm_w, ab_w_in=ab_w_in, ab_w_out=ab_w_out, hgrn_lb_logits=hgrn_lb_logits,
                 hgrn_out_norm_w=hgrn_out_norm_w, s5_lambda_re=s5_lambda_re, s5_lambda_im=s5_lambda_im, s5_log_dt=s5_log_dt,
                 s5_b_re=s5_b_re, s5_b_im=s5_b_im, s5_c_re=s5_c_re, s5_c_im=s5_c_im, s5_d=s5_d, s5_w_glu=s5_w_glu,
                 xattn_wq=xattn_wq, xattn_wkv=xattn_wkv, xattn_wo=xattn_wo, ffn_w_in=ffn_w_in, ffn_w_out=ffn_w_out)
    mom_m = dict(zip(_WEIGHT_ORDER, (m_norm_w, m_mem_norm_w, m_ab_w_in, m_ab_w_out, m_hgrn_lb_logits, m_hgrn_out_norm_w, m_s5_lambda_re, m_s5_lambda_im, m_s5_log_dt, m_s5_b_re, m_s5_b_im, m_s5_c_re, m_s5_c_im, m_s5_d, m_s5_w_glu, m_xattn_wq, m_xattn_wkv, m_xattn_wo, m_ffn_w_in, m_ffn_w_out)))
    mom_v = dict(zip(_WEIGHT_ORDER, (v_norm_w, v_mem_norm_w, v_ab_w_in, v_ab_w_out, v_hgrn_lb_logits, v_hgrn_out_norm_w, v_s5_lambda_re, v_s5_lambda_im, v_s5_log_dt, v_s5_b_re, v_s5_b_im, v_s5_c_re, v_s5_c_im, v_s5_d, v_s5_w_glu, v_xattn_wq, v_xattn_wkv, v_xattn_wo, v_ffn_w_in, v_ffn_w_out)))
    dev = 4 * lax.axis_index("x") + 2 * lax.axis_index("y") + lax.axis_index("c")

    shards = {(n, l): _owner_major(n, local[n][l]).astype(BF16) for n, l in _LARGE_KEYS}
    first = ("ab_w_in", 0)
    W = {first: _gather_weights([shards[first]], name="gather_first")[0].reshape(-1, D_MODEL)}
    tiny =jnp.concatenate([norm_w.reshape(-1), s5_d.reshape(-1)])
    tiny = jnp.pad(tiny, (0, 16 * LANES - tiny.shape[0])).reshape(16, LANES)
    tiny_all = _all_gather(tiny, name="gather_tiny").reshape(N_DEV, 16 * LANES)
    W["norm_w"] = tiny_all[:, :12 * LANES].reshape(N_DEV, 2, 6, LANES).transpose(1, 2, 0, 3).reshape(2, 6, D_MODEL)
    W["s5_d"] = tiny_all[:, 12 * LANES:13 * LANES].reshape(1, D_MODEL)
    for n in ("mem_norm_w", "hgrn_lb_logits", "hgrn_out_norm_w", "s5_lambda_re", "s5_lambda_im", "s5_log_dt",
              "s5_b_re", "s5_b_im", "s5_c_re", "s5_c_im"):
        W[n] = local[n]

    loss_parts, grad_x, G = _local_step(x, mem, loss_target, W, shards)

    g_layers = {}
    for n, l in _LARGE_KEYS:
        g = _sum_rows(G[n, l], name=f"sum_grads_{n}_{l}")
        g_layers.setdefault(n, []).append(g.T if _TRANSPOSED[n] else g)
    g_local = {n: jnp.stack(gl) for n, gl in g_layers.items()}
    small = _pack_small([G[n] for n, _ in _SMALL_LATE] + [0.5 / D_MODEL * jnp.sum(loss_parts)])
    small_sum = _sum_rows(_all_gather(small, name="gather_small"), name="sum_small").reshape(-1)
    g_full, off = _unpack_small(small_sum, _SMALL_LATE)
    loss = small_sum[off]
    early_sum = _sum_rows(G["small_early"], name="sum_small_early").reshape(-1)
    g_full.update(_unpack_small(early_sum, _SMALL_EARLY)[0])
    grads = dict(g_local)
    for n, shp in _SMALL:
        if n == "norm_w":
            grads[n] = lax.dynamic_slice_in_dim(g_full[n], dev * LANES, LANES, axis=2)
        elif n == "s5_d":
            grads[n] = lax.dynamic_slice_in_dim(g_full[n], dev * LANES, LANES, axis=1)
        else:
            grads[n] = g_full[n]

    delta, new_m, new_v = {}, {}, {}
    for n in _WEIGHT_ORDER:
        delta[n], new_m[n], new_v[n] = _adamw(local[n], grads[n], mom_m[n], mom_v[n], name=f"adamw_{n}")
    return (loss, grad_x, *[grads[n] for n in _WEIGHT_ORDER], *[delta[n] for n in _WEIGHT_ORDER],
            *[new_m[n] for n in _WEIGHT_ORDER], *[new_v[n] for n in _WEIGHT_ORDER])
```

```python
import functools
import math

import numpy as np
import jax
import jax.numpy as jnp
from jax import lax
from jax.experimental import pallas as pl
from jax.experimental.pallas import tpu as pltpu

F32 = jnp.float32
BF16 = jnp.bfloat16
HI = lax.Precision.HIGHEST

D_MODEL = 1024
NORM_EPS = 1e-6
A_WIDTH = 512
A_HEAD = 128
A_CHUNK = 32
A_SUPER = 256
B_SPAN = 128
B_DILS = (1, 4, 16)
ROPE_THETA = 10000.0
C_GROUPS = 64
C_GROUP = 16
C_STATE = 64
C_TC = 8
C_MIN_NEG_RE = -1e-4
MEM_LEN = 256
X_HEADS = 4
X_HD = 256
D_FF = 2816
N_DEV = 8
LANES = 128

ADAM_LR, ADAM_B1, ADAM_B2, ADAM_EPS, ADAM_WD, ADAM_STEP = 0.001, 0.9, 0.999, 1e-08, 0.01, 10

NEG_BIG = -1e30


def _tile(n, pref):
    for d in range(min(pref, n) // LANES * LANES, 0, -LANES):
        if n % d == 0:
            return d
    return n


def _cparams(*sem):
    return pltpu.CompilerParams(dimension_semantics=sem, vmem_limit_bytes=56 * 1024 * 1024)


def _sigmoid(x):
    return 0.5 * jnp.tanh(0.5 * x) + 0.5


def _erf(x):
    ax = jnp.abs(x)
    t = 1.0 / (1.0 + 0.3275911 * ax)
    poly = t * (0.254829592 + t * (-0.284496736 + t * (1.421413741 + t * (-1.453152027 + t * 1.061405429))))
    y = 1.0 - poly * jnp.exp(-ax * ax)
    return jnp.where(x < 0, -y, y)


_HBM = pl.BlockSpec(memory_space=pltpu.HBM)
_MESH = pl.DeviceIdType.MESH


def _logical(px, py, pc):
    return 4 * px + 2 * py + pc


class _Exchange:
    def __init__(self, gather=(), scatter=()):
        self.gather, self.scatter = list(gather), list(scatter)
        self.ng, self.n = len(self.gather), len(self.gather) + len(self.scatter)

    def operands(self):
        return self.gather + self.scatter

    def in_specs(self):
        return [_HBM] * self.n

    def out_shapes(self):
        return ([jax.ShapeDtypeStruct((N_DEV,) + g.shape, g.dtype) for g in self.gather]
                + [jax.ShapeDtypeStruct(s.shape, s.dtype) for s in self.scatter])

    def scratch(self):
        if not self.n:
            return []
        return [pltpu.SemaphoreType.DMA((self.n, 7)), pltpu.SemaphoreType.DMA((self.n, 7)), pltpu.SemaphoreType.DMA((self.n,))]

    def split(self, results):
        return list(results[:self.ng]), list(results[self.ng:])

    def run(self, ins, outs, sems, first, last):
        if not self.n:
            return
        send_sems, recv_sems, local_sems = sems
        x, y, c = lax.axis_index("x"), lax.axis_index("y"), lax.axis_index("c")
        me, sibling = _logical(x, y, c), (x, y, 1 - c)
        chips = [(1 - x, y), (x, 1 - y), (1 - x, 1 - y)]
        peers = [(x ^ (k >> 2), y ^ ((k >> 1) & 1), c ^ (k & 1)) for k in range(1, N_DEV)]

        def remote(t, k, src, dst, to):
            return pltpu.make_async_remote_copy(src_ref=src, dst_ref=dst, send_sem=send_sems.at[t, k],
                                                recv_sem=recv_sems.at[t, k], device_id=to, device_id_type=_MESH)

        def local(t):
            src = ins[t] if t < self.ng else ins[t].at[me]
            return pltpu.make_async_copy(src, outs[t].at[me], local_sems.at[t])

        @pl.when(first)
        def _():
            for t in range(self.n):
                local(t).start()
                if t < self.ng:
                    remote(t, 0, ins[t], outs[t].at[me], sibling).start()
                    for j, chip in enumerate(chips):
                        remote(t, 1 + j, ins[t], outs[t].at[me], (*chip, c)).start()
                else:
                    for k, peer in enumerate(peers):
                        remote(t, k, ins[t].at[_logical(*peer)], outs[t].at[me], peer).start()

        @pl.when(last)
        def _():
            for j, chip in enumerate(chips):
                for t in range(self.ng):
                    landed = outs[t].at[_logical(*chip, c)]
                    remote(t, 1 + j, ins[t], landed, sibling).wait_recv()
                    remote(t, 4 + j, landed, landed, sibling).start()
            for t in range(self.n):
                if t < self.ng:
                    remote(t, 0, ins[t], outs[t].at[_logical(*sibling)], sibling).wait_recv()
                    for j, chip in enumerate(chips):
                        remote(t, 4 + j, ins[t], outs[t].at[_logical(*chip, 1 - c)], sibling).wait_recv()
                    for k in range(7):
                        remote(t, k, ins[t], outs[t].at[me], sibling).wait_send()
                else:
                    for k, peer in enumerate(peers):
                        remote(t, k, ins[t].at[me], outs[t].at[_logical(*peer)], peer).wait_recv()
                    for k, peer in enumerate(peers):
                        remote(t, k, ins[t].at[_logical(*peer)], outs[t].at[me], peer).wait_send()
                local(t).wait()


def _carrying(body, n_in, n_out, ex, grid):
    n_sems = len(ex.scratch())

    def wrapped(*refs):
        ins, ex_in = refs[:n_in], refs[n_in:n_in + ex.n]
        outs = refs[n_in + ex.n:n_in + ex.n + n_out]
        ex_out = refs[n_in + ex.n + n_out:n_in + 2 * ex.n + n_out]
        rest = refs[n_in + 2 * ex.n + n_out:]
        scratch, sems = rest[:len(rest) - n_sems], rest[len(rest) - n_sems:]
        ids = [pl.program_id(a) for a in range(len(grid))]
        first = functools.reduce(lambda u, v: u & v, [i == 0 for i in ids])
        last = functools.reduce(lambda u, v: u & v, [i == g - 1 for i, g in zip(ids, grid)])
        ex.run(ex_in, ex_out, sems, first, last)
        body(*ins, *outs, *scratch)

    return wrapped


_MM_VMEM_BUDGET = 36 * 1024 * 1024


def _mm(a, b, *, ta=False, tb=False, out_dtype=F32, name, tiles=(1408, 1408, 4096), gather=(), scatter=(), gate=None):
    M, K = (a.shape[1], a.shape[0]) if ta else a.shape
    N = b.shape[0] if tb else b.shape[1]
    assert (b.shape[1] if tb else b.shape[0]) == K
    gate_mode = gate[0] if gate else None
    tm, tn, tk = _tile(M, tiles[0]), _tile(N, tiles[1]), _tile(K, tiles[2])
    if gate_mode == "fwd":
        bs = _gate_block(N // 2)
        tn = 2 * bs
    elif gate_mode == "bwd":
        bs = _gate_block(N)
        tn = bs

    def vmem_bytes():
        acc = 4 * tm * tn if tk < K else 0
        if gate_mode == "fwd":
            io = (2 * (2 + 1) + 4) * tm * tn
        elif gate_mode == "bwd":
            io = (2 * (4 + 4) + 4) * tm * tn
        else:
            io = 2 * tm * tn * jnp.dtype(out_dtype).itemsize
        return 2 * 2 * (tm * tk + tk * tn) + acc + io

    while vmem_bytes() > _MM_VMEM_BUDGET:
        if gate_mode and tm > 256:
            tm = _tile(M, tm - LANES)
        elif tk > 512:
            tk = _tile(K, tk - LANES)
        elif tn > 256 and not gate_mode:
            tn = _tile(N, tn - LANES)
        else:
            tm = _tile(M, tm - LANES)
    ni, nj, nk = M // tm, N // tn, K // tk
    ex = _Exchange(gather, scatter)

    a_spec = pl.BlockSpec((tk, tm), lambda i, j, k: (k, i)) if ta else pl.BlockSpec((tm, tk), lambda i, j, k: (i, k))
    b_spec = pl.BlockSpec((tn, tk), lambda i, j, k: (j, k)) if tb else pl.BlockSpec((tk, tn), lambda i, j, k: (k, j))
    dims = (((0 if ta else 1,), (1 if tb else 0,)), ((), ()))
    n_acc = 1 if nk > 1 else 0
    n_in = 3 if gate_mode == "bwd" else 2
    n_out = 2 if gate_mode == "fwd" else 1

    def finish(val, in_refs, out_refs, rows=slice(None)):
        if gate_mode is None:
            out_refs[0][rows, :] = val.astype(out_refs[0].dtype)
        elif gate_mode == "fwd":
            out_refs[0][rows, :] = val.astype(BF16)
            for p in range(tn // (2 * bs)):
                a_, b_ = val[:, 2 * p * bs:(2 * p + 1) * bs], val[:, (2 * p + 1) * bs:(2 * p + 2) * bs]
                out_refs[1][rows, p * bs:(p + 1) * bs] = _gate_value(a_, b_, gate[1]).astype(BF16)
        else:
            z_ref = in_refs[2]
            for p in range(tn // bs):
                a_ = z_ref[rows, 2 * p * bs:(2 * p + 1) * bs].astype(F32)
                b_ = z_ref[rows, (2 * p + 1) * bs:(2 * p + 2) * bs].astype(F32)
                da, db = _gate_grads(a_, b_, val[:, p * bs:(p + 1) * bs], gate[1])
                out_refs[0][rows, 2 * p * bs:(2 * p + 1) * bs] = da.astype(BF16)
                out_refs[0][rows, (2 * p + 1) * bs:(2 * p + 2) * bs] = db.astype(BF16)

    halves = 2 if (gate_mode and nk == 1 and not ta and tm % 32 == 0) else 1

    def body(*refs):
        in_refs, rest = refs[:n_in], refs[n_in:]
        ex_in, out_refs = rest[:ex.n], rest[ex.n:ex.n + n_out]
        ex_out, scratch = rest[ex.n + n_out:2 * ex.n + n_out], rest[2 * ex.n + n_out:]
        i, j, k = pl.program_id(0), pl.program_id(1), pl.program_id(2)
        ex.run(ex_in, ex_out, scratch[n_acc:], (i == 0) & (j == 0) & (k == 0), (i == ni - 1) & (j == nj - 1) & (k == nk - 1))
        if halves > 1:
            rh = tm // halves
            rhs = in_refs[1][...].astype(BF16)
            parts = [lax.dot_general(in_refs[0][r * rh:(r + 1) * rh, :].astype(BF16), rhs, dims, preferred_element_type=F32)
                     for r in range(halves)]
            for r, p in enumerate(parts):
                finish(p, in_refs, out_refs, slice(r * rh, (r + 1) * rh))
            return
        part = lax.dot_general(in_refs[0][...].astype(BF16), in_refs[1][...].astype(BF16), dims, preferred_element_type=F32)
        if nk == 1:
            finish(part, in_refs, out_refs)
            return
        acc_ref = scratch[0]

        @pl.when(k == 0)
        def _():
            acc_ref[...] = part

        @pl.when(k > 0)
        def _():
            acc_ref[...] += part

        @pl.when(k == nk - 1)
        def _():
            finish(acc_ref[...], in_refs, out_refs)

    tile = lambda width: pl.BlockSpec((tm, width), lambda i, j, k: (i, j))
    if gate_mode == "fwd":
        out_specs, out_shape = [tile(tn), tile(tn // 2)], [jax.ShapeDtypeStruct((M, N), BF16), jax.ShapeDtypeStruct((M, N // 2), BF16)]
    elif gate_mode == "bwd":
        out_specs, out_shape = [tile(2 * tn)], [jax.ShapeDtypeStruct((M, 2 * N), BF16)]
    else:
        out_specs, out_shape = [tile(tn)], [jax.ShapeDtypeStruct((M, N), out_dtype)]
    operands = [a, b] + ([gate[2]] if gate_mode == "bwd" else [])
    sem = ("arbitrary",) * 3 if ex.n else ("parallel", "parallel", "arbitrary")
    res = pl.pallas_call(
        body, name=name, grid=(ni, nj, nk),
        in_specs=[a_spec, b_spec] + ([tile(2 * tn)] if gate_mode == "bwd" else []) + ex.in_specs(),
        out_specs=out_specs + ex.in_specs(),
        out_shape=out_shape + ex.out_shapes(),
        scratch_shapes=([pltpu.VMEM((tm, tn), F32)] if nk > 1 else []) + ex.scratch(),
        compiler_params=_cparams(*sem),
    )(*operands, *ex.operands())
    main = res[0] if n_out == 1 else tuple(res[:n_out])
    if not ex.n:
        return main
    return (main,) + tuple(ex.split(res[n_out:]))


def _rms_fwd(x, w, res=None, *, name, out_dtype=F32, then=None):
    T, C = x.shape
    tm = _tile(T, 512)
    has_res = res is not None

    def norm(v, w_ref):
        return v * lax.rsqrt(jnp.mean(v * v, axis=-1, keepdims=True) + NORM_EPS) * w_ref[...]

    def body(*refs):
        x_ref, w_ref = refs[0], refs[1]
        y = norm(x_ref[...].astype(F32), w_ref)
        if has_res:
            y = y + refs[2][...]
        if then is None:
            refs[-1][...] = y.astype(refs[-1].dtype)
        else:
            refs[-2][...] = y.astype(refs[-2].dtype)
            refs[-1][...] = norm(y, refs[-3]).astype(refs[-1].dtype)

    row = pl.BlockSpec((tm, C), lambda i: (i, 0))
    vec = pl.BlockSpec((1, C), lambda i: (0, 0))
    ins = [x, w] + ([res] if has_res else []) + ([then[0]] if then else [])
    in_specs = [row, vec] + ([row] if has_res else []) + ([vec] if then else [])
    out_shape = [jax.ShapeDtypeStruct((T, C), out_dtype)] + ([jax.ShapeDtypeStruct((T, C), then[1])] if then else [])
    res_ = pl.pallas_call(
        body, name=name, grid=(T // tm,), in_specs=in_specs, out_specs=[row] * len(out_shape),
        out_shape=out_shape, compiler_params=_cparams("parallel"),
    )(*ins)
    return tuple(res_) if then else res_[0]


def _rms_bwd(x, w, dy, add=None, *, name, out_dtype=F32, then=None):
    T, C = x.shape
    tm = _tile(T, 512)
    has_add = add is not None
    n_in = 3 + has_add + (2 if then else 0)

    def grads(x_ref, w_ref, g, dw_ref):
        xv = x_ref[...].astype(F32)
        r = lax.rsqrt(jnp.mean(xv * xv, axis=-1, keepdims=True) + NORM_EPS)
        xh = xv * r
        part = jnp.sum(g * xh, axis=0, keepdims=True)

        @pl.when(pl.program_id(0) == 0)
        def _():
            dw_ref[...] = part

        @pl.when(pl.program_id(0) > 0)
        def _():
            dw_ref[...] += part

        gx = g * w_ref[...]
        return r * (gx - xh * jnp.mean(gx * xh, axis=-1, keepdims=True))

    def body(*refs):
        ins, outs = refs[:n_in], refs[n_in:]
        dx = grads(ins[0], ins[1], ins[2][...].astype(F32), outs[1])
        if has_add:
            dx = dx + ins[3][...]
        outs[0][...] = dx.astype(outs[0].dtype)
        if then:
            outs[2][...] = grads(ins[-2], ins[-1], dx, outs[3]).astype(BF16)

    row = pl.BlockSpec((tm, C), lambda i: (i, 0))
    vec = pl.BlockSpec((1, C), lambda i: (0, 0))
    ins = [x, w, dy] + ([add] if has_add else []) + (list(then) if then else [])
    big, small = jax.ShapeDtypeStruct((T, C), out_dtype), jax.ShapeDtypeStruct((1, C), F32)
    return pl.pallas_call(
        body, name=name, grid=(T // tm,),
        in_specs=[row, vec, row] + ([row] if has_add else []) + ([row, vec] if then else []),
        out_specs=[row, vec] + ([row, vec] if then else []),
        out_shape=[big, small] + ([jax.ShapeDtypeStruct((T, C), BF16), small] if then else []),
        compiler_params=_cparams("arbitrary"),
    )(*ins)


def _gate_block(width):
    return _tile(width, 1408)


def _gate_value(a, b, kind):
    return a * _sigmoid(a) * b if kind == "swiglu" else a * _sigmoid(b)


def _gate_grads(a, b, d, kind):
    if kind == "swiglu":
        s = _sigmoid(a)
        return d * b * (s * (1.0 + a * (1.0 - s))), d * a * s
    s = _sigmoid(b)
    return d * s, d * a * s * (1.0 - s)


def _interleave_rows(w, inverse=False):
    W2, C = w.shape
    bs = _gate_block(W2 // 2)
    nb = W2 // 2 // bs
    shape = (nb, 2, bs, C) if inverse else (2, nb, bs, C)
    return w.reshape(shape).transpose(1, 0, 2, 3).reshape(W2, C)


def _gated_bwd(z, dout, kind, *, name):
    T, W2 = z.shape
    W = W2 // 2
    tm, bs = _tile(T, 512), _gate_block(W)

    def body(z_ref, d_ref, o_ref):
        da, db = _gate_grads(z_ref[:, :bs].astype(F32), z_ref[:, bs:].astype(F32), d_ref[...].astype(F32), kind)
        o_ref[:, :bs] = da.astype(o_ref.dtype)
        o_ref[:, bs:] = db.astype(o_ref.dtype)

    return pl.pallas_call(
        body, name=name, grid=(T // tm, W // bs),
        in_specs=[pl.BlockSpec((tm, 2 * bs), lambda i, j: (i, j)), pl.BlockSpec((tm, bs), lambda i, j: (i, j))],
        out_specs=pl.BlockSpec((tm, 2 * bs), lambda i, j: (i, j)),
        out_shape=jax.ShapeDtypeStruct((T, W2), BF16), compiler_params=_cparams("parallel", "parallel"),
    )(z, dout)


_NT = (((1,), (1,)), ((), ()))
_TN = (((0,), (0,)), ((), ()))


def _dot(a, b, dims=None, precision=None):
    if dims is None:
        return jnp.dot(a, b, preferred_element_type=F32, precision=precision)
    return lax.dot_general(a, b, dims, preferred_element_type=F32, precision=precision)


def _xattn_fwd(q, kv, B, L, *, name):
    T = q.shape[0]
    tq = 256
    nq = L // tq
    scale = X_HD ** -0.5

    def body(q_ref, k_ref, v_ref, o_ref):
        heads = [slice(h * X_HD, (h + 1) * X_HD) for h in range(X_HEADS)]
        s = [_dot(q_ref[:, sl].astype(BF16), k_ref[:, sl].astype(BF16), _NT) * scale for sl in heads]
        m = [jnp.max(a, axis=-1, keepdims=True) for a in s]
        p = [jnp.exp(a - b) for a, b in zip(s, m)]
        l = [jnp.sum(a, axis=-1, keepdims=True) for a in p]
        o = [_dot(a.astype(BF16), v_ref[:, sl].astype(BF16)) for a, sl in zip(p, heads)]
        for sl, a, b in zip(heads, o, l):
            o_ref[:, sl] = (a / b).astype(BF16)

    return pl.pallas_call(
        body, name=name, grid=(B, nq),
        in_specs=[pl.BlockSpec((tq, D_MODEL), lambda b, i: (b * nq + i, 0)),
                  pl.BlockSpec((MEM_LEN, D_MODEL), lambda b, i: (b, 0)),
                  pl.BlockSpec((MEM_LEN, D_MODEL), lambda b, i: (b, 1))],
        out_specs=pl.BlockSpec((tq, D_MODEL), lambda b, i: (b * nq + i, 0)),
        out_shape=jax.ShapeDtypeStruct((T, D_MODEL), BF16), compiler_params=_cparams("parallel", "parallel"),
    )(q, kv, kv)


def _xattn_bwd(q, kv, do, B, L, *, name):
    T = q.shape[0]
    tq = 256
    nq = L // tq
    scale = X_HD ** -0.5

    def body(q_ref, k_ref, v_ref, do_ref, dq_ref, dkv_ref):
        @pl.when(pl.program_id(1) == 0)
        def _():
            dkv_ref[...] = jnp.zeros_like(dkv_ref)

        heads = [slice(h * X_HD, (h + 1) * X_HD) for h in range(X_HEADS)]
        qs = [q_ref[:, sl].astype(BF16) for sl in heads]
        ks = [k_ref[:, sl].astype(BF16) for sl in heads]
        dos = [do_ref[:, sl].astype(BF16) for sl in heads]
        s = [_dot(a, b, _NT) * scale for a, b in zip(qs, ks)]
        dp = [_dot(a, v_ref[:, sl].astype(BF16), _NT) for a, sl in zip(dos, heads)]
        e = [jnp.exp(a - jnp.max(a, axis=-1, keepdims=True)) for a in s]
        p = [a / jnp.sum(a, axis=-1, keepdims=True) for a in e]
        ds = [(a * (b - jnp.sum(b * a, axis=-1, keepdims=True)) * scale).astype(BF16) for a, b in zip(p, dp)]
        dv = [_dot(a.astype(BF16), b, _TN) for a, b in zip(p, dos)]
        dq = [_dot(a, b) for a, b in zip(ds, ks)]
        dk = [_dot(a, b, _TN) for a, b in zip(ds, qs)]
        for h, sl in enumerate(heads):
            dq_ref[:, sl] = dq[h].astype(BF16)
            dkv_ref[:, sl] += dk[h]
            dkv_ref[:, D_MODEL + h * X_HD:D_MODEL + (h + 1) * X_HD] += dv[h]

    return pl.pallas_call(
        body, name=name, grid=(B, nq),
        in_specs=[pl.BlockSpec((tq, D_MODEL), lambda b, i: (b * nq + i, 0)),
                  pl.BlockSpec((MEM_LEN, D_MODEL), lambda b, i: (b, 0)),
                  pl.BlockSpec((MEM_LEN, D_MODEL), lambda b, i: (b, 1)),
                  pl.BlockSpec((tq, D_MODEL), lambda b, i: (b * nq + i, 0))],
        out_specs=[pl.BlockSpec((tq, D_MODEL), lambda b, i: (b * nq + i, 0)),
                   pl.BlockSpec((MEM_LEN, 2 * D_MODEL), lambda b, i: (b, 0))],
        out_shape=[jax.ShapeDtypeStruct((T, D_MODEL), BF16), jax.ShapeDtypeStruct((B * MEM_LEN, 2 * D_MODEL), F32)],
        compiler_params=_cparams("parallel", "arbitrary"),
    )(q, kv, kv, do)


def _chunk_masks():
    row = lax.broadcasted_iota(jnp.int32, (A_SUPER, A_SUPER), 0)
    col = lax.broadcasted_iota(jnp.int32, (A_SUPER, A_SUPER), 1)
    same = jnp.right_shift(row, 5) == jnp.right_shift(col, 5)
    return same, same & (col <= row), same & (col >= row)


def _dot_mask(mask, x):
    m = mask.astype(BF16)
    hi = x.astype(BF16)
    rest = x - hi.astype(F32)
    mid = rest.astype(BF16)
    lo = (rest - mid.astype(F32)).astype(BF16)
    return _dot(m, hi) + _dot(m, mid) + _dot(m, lo)


def _chunk_row(x, which):
    rows = [x[c * A_CHUNK + which % A_CHUNK:c * A_CHUNK + which % A_CHUNK + 1, :] for c in range(A_SUPER // A_CHUNK)]
    return jnp.concatenate([jnp.broadcast_to(r, (A_CHUNK, x.shape[1])) for r in rows], axis=0)


def _hgrn_gates(fa, lb):
    sig = _sigmoid(fa)
    f = lb + (1.0 - lb) * sig
    return sig, f, jnp.log(f), 1.0 - f


def _hgrn_fwd(z, lb, onw, B, L, *, name, gather=()):
    T = B * L
    ns = L // A_SUPER
    nch = A_SUPER // A_CHUNK
    ex = _Exchange(gather=gather)
    grid = (B, 4, ns)

    def body(q_ref, f_ref, v_ref, g_ref, lb_ref, w_ref, oa_ref, o_ref, s_ref, st_ref, sc_ref):
        @pl.when(pl.program_id(2) == 0)
        def _():
            st_ref[...] = jnp.zeros_like(st_ref)

        s_ref[0] = st_ref[...]
        same, tril, _ = _chunk_masks()
        q, v = q_ref[...], v_ref[...]
        _, _, lf, k = _hgrn_gates(f_ref[...], lb_ref[...])
        bcs = _dot_mask(tril, lf)
        bl = _chunk_row(bcs, -1)
        qd = (q * jnp.exp(bcs)).astype(BF16)
        ki = (k * jnp.exp(-bcs)).astype(BF16)
        ke = (k * jnp.exp(bl - bcs)).astype(BF16)
        dec = jnp.exp(bl)
        vb = v.astype(BF16)
        a = jnp.where(tril, _dot(qd, ki, _NT), 0.0)
        o_ref[...] = _dot(a.astype(BF16), vb)
        chunks = [slice(c * A_CHUNK, (c + 1) * A_CHUNK) for c in range(nch)]
        outer = [_dot(vb[rs], ke[rs], _TN) for rs in chunks]
        st = st_ref[...]
        for c, rs in enumerate(chunks):
            sc_ref[c] = st.astype(BF16)
            st = st * dec[c * A_CHUNK:c * A_CHUNK + 1, :] + outer[c]
        st_ref[...] = st
        for c, rs in enumerate(chunks):
            o_ref[rs, :] += _dot(qd[rs], sc_ref[c], _NT)
        o = o_ref[...]
        r = lax.rsqrt(jnp.mean(o * o, axis=-1, keepdims=True) + NORM_EPS)
        g = g_ref[...]
        oa_ref[...] = (o * r * w_ref[...] * (g * _sigmoid(g))).astype(BF16)

    def zspec(off):
        return pl.BlockSpec((A_SUPER, A_HEAD), lambda b, h, n: (b * ns + n, off + h))

    hvec = pl.BlockSpec((1, A_HEAD), lambda b, h, n: (0, h))
    ospec = pl.BlockSpec((A_SUPER, A_HEAD), lambda b, h, n: (b * ns + n, h))
    res = pl.pallas_call(
        _carrying(body, 6, 3, ex, grid) if ex.n else body, name=name, grid=grid,
        in_specs=[zspec(0), zspec(4), zspec(8), zspec(12), hvec, hvec] + ex.in_specs(),
        out_specs=[ospec, ospec, pl.BlockSpec((1, A_HEAD, A_HEAD), lambda b, h, n: ((b * 4 + h) * ns + n, 0, 0))] + ex.in_specs(),
        out_shape=[jax.ShapeDtypeStruct((T, A_WIDTH), BF16), jax.ShapeDtypeStruct((T, A_WIDTH), F32),
                   jax.ShapeDtypeStruct((B * 4 * ns, A_HEAD, A_HEAD), F32)] + ex.out_shapes(),
        scratch_shapes=[pltpu.VMEM((A_HEAD, A_HEAD), F32), pltpu.VMEM((nch, A_HEAD, A_HEAD), BF16)] + ex.scratch(),
        compiler_params=_cparams(*(("arbitrary",) * 3 if ex.n else ("parallel", "parallel", "arbitrary"))),
    )(z, z, z, z, lb, onw, *ex.operands())
    return tuple(res[:3]) + (list(res[3:]),)


def _hgrn_bwd(z, lb, onw, o_raw, s_start, doa, B, L, *, name, scatter=()):
    T = B * L
    ns = L // A_SUPER
    nch = A_SUPER // A_CHUNK
    ex = _Exchange(scatter=scatter)
    grid = (B, 4, ns)

    def body(q_ref, f_ref, v_ref, g_ref, lb_ref, w_ref, o_ref, s_ref, doa_ref,
             dq_ref, df_ref, dv_ref, dg_ref, dw_ref, dlb_ref, dst_ref, sc_ref, dsc_ref, dqd_ref, dke_ref, dblx_ref, dvacc_ref):
        @pl.when(pl.program_id(2) == 0)
        def _():
            dst_ref[...] = jnp.zeros_like(dst_ref)
            dw_ref[...] = jnp.zeros_like(dw_ref)
            dlb_ref[...] = jnp.zeros_like(dlb_ref)

        same, tril, triu = _chunk_masks()
        q, v, g, lb, w = q_ref[...], v_ref[...], g_ref[...], lb_ref[...], w_ref[...]
        sig, f, lf, k = _hgrn_gates(f_ref[...], lb)
        bcs = _dot_mask(tril, lf)
        bl = _chunk_row(bcs, -1)
        eb, enb, eeb = jnp.exp(bcs), jnp.exp(-bcs), jnp.exp(bl - bcs)
        qd, ki, ke = q * eb, k * enb, k * eeb
        qdb, kib, keb, vb = qd.astype(BF16), ki.astype(BF16), ke.astype(BF16), v.astype(BF16)
        dec = jnp.exp(bl)
        o = o_ref[...]
        r = lax.rsqrt(jnp.mean(o * o, axis=-1, keepdims=True) + NORM_EPS)
        on = o * r
        sg = _sigmoid(g)
        silu_g = g * sg
        doa = doa_ref[...]
        dg_ref[...] = (doa * on * w * (sg * (1.0 + g * (1.0 - sg)))).astype(BF16)
        dw_ref[0] += jnp.sum(doa * on * silu_g, axis=0, keepdims=True)
        don = doa * w * silu_g
        do = r * (don - on * jnp.mean(don * on, axis=-1, keepdims=True))
        dob = do.astype(BF16)
        a = jnp.where(tril, _dot(qdb, kib, _NT), 0.0).astype(BF16)
        da = jnp.where(tril, _dot(dob, vb, _NT), 0.0).astype(BF16)
        dvacc_ref[...] = _dot(a, dob, _TN)
        dqd_ref[...] = _dot(da, kib)
        dki = _dot(da, qdb, _TN)
        chunks = [slice(c * A_CHUNK, (c + 1) * A_CHUNK) for c in range(nch)]
        outer = [_dot(vb[rs], keb[rs], _TN) for rs in chunks]
        st = s_ref[0]
        for c in range(nch):
            sc_ref[c] = st
            st = st * dec[c * A_CHUNK:c * A_CHUNK + 1, :] + outer[c]
        outer_g = [_dot(dob[rs], qdb[rs], _TN) for rs in chunks]
        dst = dst_ref[...]
        for c in reversed(range(nch)):
            dsc_ref[c] = dst
            dst = dst * dec[c * A_CHUNK:c * A_CHUNK + 1, :] + outer_g[c]
        dst_ref[...] = dst
        for c, rs in enumerate(chunks):
            dec_c = dec[c * A_CHUNK:c * A_CHUNK + 1, :]
            dsc, stc = dsc_ref[c], sc_ref[c]
            dscb = dsc.astype(BF16)
            dvacc_ref[rs, :] += _dot(keb[rs], dscb, _NT)
            dke_ref[rs, :] = _dot(vb[rs], dscb)
            ddec = jnp.sum(dsc * stc, axis=0, keepdims=True)
            dqd_ref[rs, :] += _dot(dob[rs], stc.astype(BF16))
            dblx_ref[rs, :] = jnp.broadcast_to(ddec * dec_c, (A_CHUNK, A_HEAD))
        dqd, dke = dqd_ref[...], dke_ref[...]
        dv_ref[...] = dvacc_ref[...].astype(BF16)
        dq_ref[...] = (dqd * eb).astype(BF16)
        keke = dke * ke
        db = dqd * qd - dki * ki - keke
        sums = _dot_mask(triu, jnp.concatenate([db, keke], axis=1))
        dk = dki * enb + dke * eeb
        dlf = sums[:, :A_HEAD] + _chunk_row(sums[:, A_HEAD:], 0) + dblx_ref[...]
        dff = dlf / f - dk
        df_ref[...] = (dff * (1.0 - lb) * sig * (1.0 - sig)).astype(BF16)
        dlb_ref[0] += jnp.sum(dff * (1.0 - sig), axis=0, keepdims=True)

    def rev(n):
        return ns - 1 - n

    def zspec(off):
        return pl.BlockSpec((A_SUPER, A_HEAD), lambda b, h, n: (b * ns + rev(n), off + h))

    hvec = pl.BlockSpec((1, A_HEAD), lambda b, h, n: (0, h))
    ospec = pl.BlockSpec((A_SUPER, A_HEAD), lambda b, h, n: (b * ns + rev(n), h))
    acc = pl.BlockSpec((1, 1, A_HEAD), lambda b, h, n: (b * 4 + h, 0, 0))
    big = jax.ShapeDtypeStruct((T, A_WIDTH), BF16)
    small = jax.ShapeDtypeStruct((B * 4, 1, A_HEAD), F32)
    res = pl.pallas_call(
        _carrying(body, 9, 6, ex, grid) if ex.n else body, name=name, grid=grid,
        in_specs=[zspec(0), zspec(4), zspec(8), zspec(12), hvec, hvec, ospec,
                  pl.BlockSpec((1, A_HEAD, A_HEAD), lambda b, h, n: ((b * 4 + h) * ns + rev(n), 0, 0)), ospec] + ex.in_specs(),
        out_specs=[ospec, ospec, ospec, ospec, acc, acc] + ex.in_specs(),
        out_shape=[big, big, big, big, small, small] + ex.out_shapes(),
        scratch_shapes=[pltpu.VMEM((A_HEAD, A_HEAD), F32), pltpu.VMEM((nch, A_HEAD, A_HEAD), F32),
                        pltpu.VMEM((nch, A_HEAD, A_HEAD), F32),
                        pltpu.VMEM((A_SUPER, A_HEAD), F32), pltpu.VMEM((A_SUPER, A_HEAD), F32),
                        pltpu.VMEM((A_SUPER, A_HEAD), F32), pltpu.VMEM((A_SUPER, A_HEAD), F32)] + ex.scratch(),
        compiler_params=_cparams(*(("arbitrary",) * 3 if ex.n else ("parallel", "parallel", "arbitrary"))),
    )(z, z, z, z, lb, onw, o_raw, s_start, doa, *ex.operands())
    return tuple(res[:6]) + (list(res[6:]),)


def _rope_tables(L):
    half = A_HEAD // 2
    inv_freq = ROPE_THETA ** (-jnp.arange(half, dtype=F32) / half)
    ang = jnp.arange(L, dtype=F32)[:, None] * inv_freq[None, :]
    cos, sin = jnp.cos(ang), jnp.sin(ang)
    return jnp.concatenate([cos, cos], axis=-1), jnp.concatenate([-sin, sin], axis=-1)


def _rope_fwd(z, cos2, sin2, B, L, *, name):
    T = B * L
    tm = 512
    nl = L // tm

    def body(x_ref, c_ref, s_ref, q_ref, k_ref):
        c, s = c_ref[...], s_ref[...]
        for h in range(8):
            x = x_ref[:, h * A_HEAD:(h + 1) * A_HEAD]
            out = x * c + pltpu.roll(x, A_HEAD // 2, 1) * s
            o_ref = q_ref if h < 4 else k_ref
            o_ref[:, (h % 4) * A_HEAD:(h % 4 + 1) * A_HEAD] = out

    tab = pl.BlockSpec((tm, A_HEAD), lambda i: (i % nl, 0))
    out = pl.BlockSpec((tm, 512), lambda i: (i, 0))
    return pl.pallas_call(
        body, name=name, grid=(T // tm,),
        in_specs=[pl.BlockSpec((tm, 1024), lambda i: (i, 2)), tab, tab], out_specs=[out, out],
        out_shape=[jax.ShapeDtypeStruct((T, 512), F32)] * 2, compiler_params=_cparams("parallel"),
    )(z, cos2, sin2)


def _rope_bwd(dqs, dks, dvs, cos2, sin2, B, L, *, name):
    T = B * L
    tm = 256
    nl = L // tm

    def body(*refs):
        c, s = refs[9][...], refs[10][...]
        o_ref = refs[11]
        for part in range(3):
            a_ref, b_ref, c_ref = refs[3 * part:3 * part + 3]
            for h in range(4):
                cols = slice(h * A_HEAD, (h + 1) * A_HEAD)
                d = a_ref[:, cols] + b_ref[:, cols] + c_ref[:, cols]
                if part < 2:
                    d = d * c - pltpu.roll(d, A_HEAD // 2, 1) * s
                o_ref[:, part * 512 + h * A_HEAD:part * 512 + (h + 1) * A_HEAD] = d.astype(BF16)

    blk = pl.BlockSpec((tm, 512), lambda i: (i, 0))
    tab = pl.BlockSpec((tm, A_HEAD), lambda i: (i % nl, 0))
    return pl.pallas_call(
        body, name=name, grid=(T // tm,), in_specs=[blk] * 9 + [tab, tab],
        out_specs=pl.BlockSpec((tm, 1536), lambda i: (i, 0)),
        out_shape=jax.ShapeDtypeStruct((T, 1536), BF16), compiler_params=_cparams("parallel"),
    )(*dqs, *dks, *dvs, cos2, sin2)


def _band_masks():
    i = lax.broadcasted_iota(jnp.int32, (B_SPAN, B_SPAN), 0)
    j = lax.broadcasted_iota(jnp.int32, (B_SPAN, B_SPAN), 1)
    return i <= j, j <= i


class _DilPlan:
    def __init__(self, dil, B, L):
        self.dil, self.B, self.L = dil, B, L
        self.rows = 4 * B_SPAN if dil == 1 else B_SPAN * dil
        self.n = L // self.rows
        self.hr = 4 if dil == 4 else 1
        self.cw = 512 if dil == 1 else A_HEAD
        self.has_other = dil != 16
        self.grid = (B, self.n, 4 if dil == 16 else 1)
        if dil == 1:
            self.items = [(j, 0, h) for j in range(4) for h in range(4)]
        elif dil == 4:
            self.items = [(0, r, h) for r in range(4) for h in range(4)]
        else:
            self.items = [(0, r, 0) for r in range(16)]
        self.groups = [self.items[i:i + 4] for i in range(0, 16, 4)]

    def operands(self, arr):
        return [arr] * self.hr

    def specs(self, col0, role="cur"):
        n, nb128 = self.n, self.L // B_SPAN
        out = []
        for h in range(self.hr):
            cb = col0 // self.cw + h
            if role == "cur":
                out.append(pl.BlockSpec((self.rows, self.cw), lambda b, i, hh, cb=cb: (b * n + i, cb + hh)))
            elif self.dil == 1:
                shift = -1 if role == "prev" else 4
                out.append(pl.BlockSpec((B_SPAN, self.cw),
                                        lambda b, i, hh, cb=cb, shift=shift: (b * nb128 + jnp.clip(4 * i + shift, 0, nb128 - 1), cb)))
            else:
                shift = -1 if role == "prev" else 1
                out.append(pl.BlockSpec((self.rows, self.cw),
                                        lambda b, i, hh, cb=cb, shift=shift: (b * n + jnp.clip(i + shift, 0, n - 1), cb)))
        return out

    def out_spec(self):
        n = self.n
        if self.dil == 4:
            return pl.BlockSpec((self.rows, 512), lambda b, i, hh: (b * n + i, 0))
        return pl.BlockSpec((self.rows, self.cw), lambda b, i, hh: (b * n + i, hh))

    def scratch(self, n_out):
        return [pltpu.VMEM((4, self.rows, A_HEAD), F32)] * n_out if self.dil == 4 else []

    def store(self, out_ref, scr, item, val):
        j, r, h = item
        if self.dil == 1:
            out_ref[pl.ds(j * B_SPAN, B_SPAN), pl.ds(h * A_HEAD, A_HEAD)] = val
        elif self.dil == 4:
            scr.at[h][pl.ds(r, B_SPAN, stride=4), :] = val
        else:
            out_ref[pl.ds(r, B_SPAN, stride=self.dil), :] = val

    def flush(self, out_ref, scr):
        if self.dil == 4:
            for h in range(4):
                out_ref[:, h * A_HEAD:(h + 1) * A_HEAD] = scr[h]

    def cur(self, refs, item):
        j, r, h = item
        if self.dil == 1:
            return refs[0], (pl.ds(j * B_SPAN, B_SPAN), pl.ds(h * A_HEAD, A_HEAD))
        return refs[h], (pl.ds(r, B_SPAN, stride=self.dil), slice(None))

    def other(self, refs, other_refs, item, role):
        j, r, h = item
        if self.dil == 1:
            cols = pl.ds(h * A_HEAD, A_HEAD)
            jj = j - 1 if role == "prev" else j + 1
            if 0 <= jj < 4:
                return refs[0], (pl.ds(jj * B_SPAN, B_SPAN), cols)
            return other_refs[0], (pl.ds(0, B_SPAN), cols)
        return other_refs[h], (pl.ds(r, B_SPAN, stride=self.dil), slice(None))

    def other_valid(self, item, role):
        j = item[0]
        i = pl.program_id(1)
        if role == "prev":
            return True if (self.dil == 1 and j > 0) else i > 0
        return True if (self.dil == 1 and j < 3) else i < self.n - 1


def _ld(pair):
    ref, idx = pair
    return ref[idx]


def _dil_fwd(qr, kr, z, dil, B, L, *, name):
    T = B * L
    plan = _DilPlan(dil, B, L)
    hr, has_prev = plan.hr, plan.has_other
    scale = A_HEAD ** -0.5

    n_t = 5 if has_prev else 3

    def body(*refs):
        lists = [refs[i * hr:(i + 1) * hr] for i in range(n_t)]
        o_ref, l_ref = refs[n_t * hr], refs[n_t * hr + 1]
        scr = refs[n_t * hr + 2:] or (None, None)
        if has_prev:
            q_r, kc_r, vc_r, kp_r, vp_r = lists
        else:
            q_r, kc_r, vc_r = lists
        mp0, mc = _band_masks()

        for group in plan.groups:
            qs = [_ld(plan.cur(q_r, i)).astype(BF16) for i in group]
            sc = [jnp.where(mc, _dot(q, _ld(plan.cur(kc_r, i)).astype(BF16), _NT) * scale, NEG_BIG) for q, i in zip(qs, group)]
            m = [jnp.max(s, axis=-1, keepdims=True) for s in sc]
            if has_prev:
                mps = [mp0 & plan.other_valid(i, "prev") for i in group]
                sp = [jnp.where(mk, _dot(q, _ld(plan.other(kc_r, kp_r, i, "prev")).astype(BF16), _NT) * scale, NEG_BIG)
                      for q, i, mk in zip(qs, group, mps)]
                m = [jnp.maximum(a, jnp.max(s, axis=-1, keepdims=True)) for a, s in zip(m, sp)]
            pc = [jnp.exp(s - a) for s, a in zip(sc, m)]
            l = [jnp.sum(p, axis=-1, keepdims=True) for p in pc]
            o = [_dot(p.astype(BF16), _ld(plan.cur(vc_r, i)).astype(BF16)) for p, i in zip(pc, group)]
            if has_prev:
                pp = [jnp.exp(s - a) for s, a in zip(sp, m)]
                l = [a + jnp.sum(p, axis=-1, keepdims=True) for a, p in zip(l, pp)]
                o = [a + _dot(p.astype(BF16), _ld(plan.other(vc_r, vp_r, i, "prev")).astype(BF16)) for a, p, i in zip(o, pp, group)]
            for i, oi, li, mi in zip(group, o, l, m):
                plan.store(o_ref, scr[0], i, oi / li)
                plan.store(l_ref, scr[1], i, jnp.broadcast_to(mi + jnp.log(li), (B_SPAN, A_HEAD)))
        plan.flush(o_ref, scr[0])
        plan.flush(l_ref, scr[1])

    tensors = [(qr, 0, "cur"), (kr, 0, "cur"), (z, 3072, "cur")] + ([(kr, 0, "prev"), (z, 3072, "prev")] if has_prev else [])
    return pl.pallas_call(
        body, name=name, grid=plan.grid,
        in_specs=[sp for _, c, role in tensors for sp in plan.specs(c, role)],
        out_specs=[plan.out_spec()] * 2, out_shape=[jax.ShapeDtypeStruct((T, 512), F32)] * 2,
        scratch_shapes=plan.scratch(2), compiler_params=_cparams("parallel", "parallel", "parallel"),
    )(*[a for arr, _, _ in tensors for a in plan.operands(arr)])


def _dil_combine(os_, ls_, *, name):
    T = os_[0].shape[0]
    tm = 256

    def body(o1, o2, o3, l1, l2, l3, ob_ref, lse_ref):
        a1, a2, a3 = l1[...], l2[...], l3[...]
        m = jnp.maximum(jnp.maximum(a1, a2), a3)
        e1, e2, e3 = jnp.exp(a1 - m), jnp.exp(a2 - m), jnp.exp(a3 - m)
        den = e1 + e2 + e3
        ob_ref[...] = (e1 * o1[...] + e2 * o2[...] + e3 * o3[...]) / den
        lse_ref[...] = m + jnp.log(den)

    blk = pl.BlockSpec((tm, 512), lambda i: (i, 0))
    return pl.pallas_call(
        body, name=name, grid=(T // tm,), in_specs=[blk] * 6, out_specs=[blk, blk],
        out_shape=[jax.ShapeDtypeStruct((T, 512), F32)] * 2, compiler_params=_cparams("parallel"),
    )(*[o.reshape(T, 512) for o in os_], *[l.reshape(T, 512) for l in ls_])


def _dil_bwd_q(qr, kr, z, dymix, out, lse, dil, B, L, *, name):
    T = B * L
    plan = _DilPlan(dil, B, L)
    hr, has_prev = plan.hr, plan.has_other
    scale = A_HEAD ** -0.5
    n_t = 8 if has_prev else 6

    def body(*refs):
        lists = [refs[i * hr:(i + 1) * hr] for i in range(n_t)]
        dq_ref = refs[n_t * hr]
        scr = refs[n_t * hr + 1:] or (None,)
        q_r, kc_r, vc_r, do_r, out_r, lse_r = lists[:6]
        mp0, mc = _band_masks()

        for group in plan.groups:
            qs = [_ld(plan.cur(q_r, i)).astype(BF16) for i in group]
            dos = [_ld(plan.cur(do_r, i)) for i in group]
            delta = [jnp.sum(d * _ld(plan.cur(out_r, i)), axis=-1, keepdims=True) for d, i in zip(dos, group)]
            dob = [d.astype(BF16) for d in dos]
            lse = [_ld(plan.cur(lse_r, i)) for i in group]
            kc = [_ld(plan.cur(kc_r, i)).astype(BF16) for i in group]
            pc = [jnp.where(mc, jnp.exp(_dot(q, k, _NT) * scale - a), 0.0) for q, k, a in zip(qs, kc, lse)]
            dsc = [p * (_dot(d, _ld(plan.cur(vc_r, i)).astype(BF16), _NT) - dl) * scale
                   for p, d, i, dl in zip(pc, dob, group, delta)]
            dq = [_dot(d.astype(BF16), k) for d, k in zip(dsc, kc)]
            if has_prev:
                kp_r, vp_r = lists[6], lists[7]
                mps = [mp0 & plan.other_valid(i, "prev") for i in group]
                kp = [_ld(plan.other(kc_r, kp_r, i, "prev")).astype(BF16) for i in group]
                pp = [jnp.where(mk, jnp.exp(_dot(q, k, _NT) * scale - a), 0.0) for q, k, a, mk in zip(qs, kp, lse, mps)]
                dsp = [p * (_dot(d, _ld(plan.other(vc_r, vp_r, i, "prev")).astype(BF16), _NT) - dl) * scale
                       for p, d, i, dl in zip(pp, dob, group, delta)]
                dq = [a + _dot(d.astype(BF16), k) for a, d, k in zip(dq, dsp, kp)]
            for i, d in zip(group, dq):
                plan.store(dq_ref, scr[0], i, d)
        plan.flush(dq_ref, scr[0])

    tensors = ([(qr, 0, "cur"), (kr, 0, "cur"), (z, 3072, "cur"), (dymix, 512, "cur"), (out, 0, "cur"), (lse, 0, "cur")]
               + ([(kr, 0, "prev"), (z, 3072, "prev")] if has_prev else []))
    return pl.pallas_call(
        body, name=name, grid=plan.grid,
        in_specs=[sp for _, c, role in tensors for sp in plan.specs(c, role)],
        out_specs=plan.out_spec(), out_shape=jax.ShapeDtypeStruct((T, 512), F32),
        scratch_shapes=plan.scratch(1), compiler_params=_cparams("parallel", "parallel", "parallel"),
    )(*[a for arr, _, _ in tensors for a in plan.operands(arr)])


def _dil_bwd_kv(qr, kr, z, dymix, out, lse, dil, B, L, *, name):
    T = B * L
    plan = _DilPlan(dil, B, L)
    hr, has_next = plan.hr, plan.has_other
    scale = A_HEAD ** -0.5
    n_t = 10 if has_next else 6

    def body(*refs):
        lists = [refs[i * hr:(i + 1) * hr] for i in range(n_t)]
        dk_ref, dv_ref = refs[n_t * hr], refs[n_t * hr + 1]
        scr = refs[n_t * hr + 2:] or (None, None)
        k_r, v_r = lists[0], lists[1]
        own = lists[2:6]
        mp0, mc = _band_masks()

        for group in plan.groups:
            kh = [_ld(plan.cur(k_r, i)).astype(BF16) for i in group]
            vh = [_ld(plan.cur(v_r, i)).astype(BF16) for i in group]
            dk, dv = [None] * len(group), [None] * len(group)
            for role in ("own", "next") if has_next else ("own",):
                if role == "own":
                    get = lambda t, i: _ld(plan.cur(own[t], i))
                    masks = [mc] * len(group)
                else:
                    get = lambda t, i: _ld(plan.other(own[t], lists[6 + t], i, "next"))
                    masks = [mp0 & plan.other_valid(i, "next") for i in group]
                qs = [get(0, i).astype(BF16) for i in group]
                dos = [get(1, i) for i in group]
                delta = [jnp.sum(d * get(2, i), axis=-1, keepdims=True) for d, i in zip(dos, group)]
                dob = [d.astype(BF16) for d in dos]
                p = [jnp.where(mk, jnp.exp(_dot(q, k, _NT) * scale - get(3, i)), 0.0) for q, k, i, mk in zip(qs, kh, group, masks)]
                dvn = [_dot(a.astype(BF16), d, _TN) for a, d in zip(p, dob)]
                ds = [a * (_dot(d, v, _NT) - dl) * scale for a, d, v, dl in zip(p, dob, vh, delta)]
                dkn = [_dot(d.astype(BF16), q, _TN) for d, q in zip(ds, qs)]
                dv = [n if o is None else o + n for o, n in zip(dv, dvn)]
                dk = [n if o is None else o + n for o, n in zip(dk, dkn)]
            for i, a, b in zip(group, dk, dv):
                plan.store(dk_ref, scr[0], i, a)
                plan.store(dv_ref, scr[1], i, b)
        plan.flush(dk_ref, scr[0])
        plan.flush(dv_ref, scr[1])

    queries = [(qr, 0), (dymix, 512), (out, 0), (lse, 0)]
    tensors = ([(kr, 0, "cur"), (z, 3072, "cur")] + [(a, c, "cur") for a, c in queries]
               + ([(a, c, "next") for a, c in queries] if has_next else []))
    return pl.pallas_call(
        body, name=name, grid=plan.grid,
        in_specs=[sp for _, c, role in tensors for sp in plan.specs(c, role)],
        out_specs=[plan.out_spec()] * 2, out_shape=[jax.ShapeDtypeStruct((T, 512), F32)] * 2,
        scratch_shapes=plan.scratch(2), compiler_params=_cparams("parallel", "parallel", "parallel"),
    )(*[a for arr, _, _ in tensors for a in plan.operands(arr)])


def _s5_build(lam_re, lam_im, log_dt, b_re, b_im, c_re, c_im):
    G, P, TC = C_GROUPS, C_STATE, C_TC
    lr = jnp.minimum(lam_re, C_MIN_NEG_RE)
    li = lam_im
    dt = jnp.exp(log_dt)[:, None]
    mag = jnp.exp(dt * lr)
    ar, ai = mag * jnp.cos(dt * li), mag * jnp.sin(dt * li)
    den = lr * lr + li * li
    zr = ((ar - 1.0) * lr + ai * li) / den
    zi = (ai * lr - (ar - 1.0) * li) / den
    bbr = zr[..., None] * b_re - zi[..., None] * b_im
    bbi = zr[..., None] * b_im + zi[..., None] * b_re
    ks = jnp.arange(TC + 1, dtype=F32)[:, None, None]
    pmag = jnp.exp(ks * (dt * lr)[None])
    pr, pi = pmag * jnp.cos(ks * (dt * li)[None]), pmag * jnp.sin(ks * (dt * li)[None])
    car = c_re[None] * pr[:, :, None, :] - c_im[None] * pi[:, :, None, :]
    cai = c_re[None] * pi[:, :, None, :] + c_im[None] * pr[:, :, None, :]
    kern = (jnp.einsum('lgop,gpc->lgco', car[:TC], bbr, precision=HI)
            - jnp.einsum('lgop,gpc->lgco', cai[:TC], bbi, precision=HI))
    pr_e, pi_e = pr[TC - 1 - jnp.arange(TC)], pi[TC - 1 - jnp.arange(TC)]
    er = pr_e[:, :, :, None] * bbr[None] - pi_e[:, :, :, None] * bbi[None]
    ei = pr_e[:, :, :, None] * bbi[None] + pi_e[:, :, :, None] * bbr[None]
    ez = jnp.stack([er, ei], axis=2).reshape(TC, C_NB, C_GB, 2, P, C_GROUP).transpose(1, 0, 2, 5, 3, 4)
    fz = jnp.stack([car[1:], -cai[1:]], axis=0).reshape(2, TC, C_NB, C_GB, C_GROUP, P).transpose(2, 0, 3, 5, 1, 4)
    return kern, ez, fz, pr[TC], pi[TC]


def _s5_lag_blocks(kern):
    eye = jnp.eye(C_GB, dtype=kern.dtype)
    return (kern.reshape(C_TC, C_NB, C_GB, C_GROUP, C_GROUP)[:, :, :, :, None, :]
            * eye[None, None, :, None, :, None]).reshape(C_TC, C_NB, LANES, LANES)


def _s5_dense(kbd, ez, fz, *, name):
    ezc = ez.reshape(C_NB, C_W8, LANES)
    fzc = fz.reshape(C_NB, C_S8, LANES)
    half = C_GB * C_STATE

    def body(k_ref, e_ref, f_ref, m8_ref, e8_ref, f8_ref):
        zero = jnp.zeros((LANES, LANES), BF16)
        for s in range(C_TC):
            for t in range(C_TC):
                m8_ref[0, s * LANES:(s + 1) * LANES, t * LANES:(t + 1) * LANES] = (
                    k_ref[t - s, 0].astype(BF16) if t >= s else zero)
        lane = lax.broadcasted_iota(jnp.int32, (1, LANES), 1)
        src = e_ref[0]
        swapped = pltpu.roll(src, C_STATE, 1)
        rowg = jnp.bitwise_and(jnp.right_shift(lax.broadcasted_iota(jnp.int32, (C_W8, LANES), 0), 4), C_GB - 1)
        for kb in range(C_S8 // LANES):
            z, g0 = kb // (C_GB // 2), 2 * (kb % (C_GB // 2))
            first = jnp.where(lane < C_STATE, src if z == 0 else swapped, 0.0)
            second = jnp.where(lane >= C_STATE, swapped if z == 0 else src, 0.0)
            e8_ref[0, :, kb * LANES:(kb + 1) * LANES] = jnp.where(
                rowg == g0, first, jnp.where(rowg == g0 + 1, second, 0.0)).astype(BF16)
        for z in range(2):
            for g in range(C_GB):
                rows = slice(z * half + g * C_STATE, z * half + (g + 1) * C_STATE)
                piece = f_ref[0, rows, :]
                mine = (lane >= g * C_GROUP) & (lane < (g + 1) * C_GROUP)
                for t in range(C_TC):
                    f8_ref[0, rows, t * LANES:(t + 1) * LANES] = jnp.where(
                        mine, pltpu.roll(piece, ((g - t) * C_GROUP) % LANES, 1), 0.0).astype(BF16)

    blk = lambda r, c: pl.BlockSpec((1, r, c), lambda b: (b, 0, 0))
    return pl.pallas_call(
        body, name=name, grid=(C_NB,),
        in_specs=[pl.BlockSpec((C_TC, 1, LANES, LANES), lambda b: (0, b, 0, 0)), blk(C_W8, LANES), blk(C_S8, LANES)],
        out_specs=[blk(C_W8, C_W8), blk(C_W8, C_S8), blk(C_S8, C_W8)],
        out_shape=[jax.ShapeDtypeStruct((C_NB, C_W8, C_W8), BF16), jax.ShapeDtypeStruct((C_NB, C_W8, C_S8), BF16),
                   jax.ShapeDtypeStruct((C_NB, C_S8, C_W8), BF16)],
        compiler_params=_cparams("parallel"),
    )(kbd, ezc, fzc)


C_NB = C_GROUPS * C_GROUP // LANES
C_GB = C_GROUPS // C_NB
C_W8 = C_TC * LANES
C_S8 = 2 * C_GB * C_STATE


def _s5_scan_tables(lam_re, lam_im, log_dt, nsteps):
    lr = jnp.minimum(lam_re, C_MIN_NEG_RE)
    dt = jnp.exp(log_dt)[:, None]
    ks = (C_TC * 2.0 ** jnp.arange(8, dtype=F32))[None, :, None]
    keep = (jnp.arange(8) < nsteps)[None, :, None]
    pmag = jnp.exp(ks * (dt * lr)[:, None, :])
    ang = ks * (dt * lam_im)[:, None, :]

    def blocks(t):
        return t.reshape(C_NB, C_GB, 8, C_STATE).transpose(0, 2, 1, 3).reshape(C_NB, 8, C_GB * C_STATE)

    pr = blocks(jnp.where(keep, pmag * jnp.cos(ang), 0.0))
    pi = blocks(jnp.where(keep, pmag * jnp.sin(ang), 0.0))
    return jnp.concatenate([pr, pr], axis=-1), jnp.concatenate([-pi, pi], axis=-1)


def _s5_rows(t, R):
    return pl.ds(t, R, stride=C_TC)


def _s5_fwd(u, dsk, m8, e8, f8, tab_r, tab_i, B, L, *, name, gather=()):
    T = B * L
    R = L // C_TC
    nsteps = int(math.log2(R))
    ex = _Exchange(gather=gather)
    grid = (C_NB, B)

    def body(u_ref, d_ref, m_ref, e_ref, f_ref, tr_ref, ti_ref, gl_ref, y_ref, x8_ref, xs_ref):
        for t in range(C_TC):
            x8_ref[0, :, t * LANES:(t + 1) * LANES] = u_ref[_s5_rows(t, R), :].astype(BF16)
        x8 = x8_ref[0]
        x = _dot(x8, e_ref[0])
        row = lax.broadcasted_iota(jnp.int32, (R, C_S8), 0)
        for k in range(nsteps):
            s = 1 << k
            sh = pltpu.roll(x, s, 0)
            upd = tr_ref[0, k:k + 1, :] * sh + ti_ref[0, k:k + 1, :] * pltpu.roll(sh, C_S8 // 2, 1)
            x = x + jnp.where(row >= s, upd, 0.0)
        xs = jnp.where(row >= 1, pltpu.roll(x, 1, 0), 0.0)
        xs_ref[0] = xs
        y8 = _dot(x8, m_ref[0]) + _dot(xs.astype(BF16), f_ref[0])
        d = d_ref[...]
        for t in range(C_TC):
            rows = _s5_rows(t, R)
            y = y8[:, t * LANES:(t + 1) * LANES] + d * u_ref[rows, :]
            y_ref[rows, :] = y
            gl_ref[rows, :] = 0.5 * y * (1.0 + _erf(y * (2.0 ** -0.5)))

    tok = pl.BlockSpec((L, LANES), lambda c, b: (b, c))
    per_block = lambda shape: pl.BlockSpec((1,) + shape, lambda c, b: (c, 0, 0))
    per_step = lambda shape: pl.BlockSpec((1,) + shape, lambda c, b: (c * B + b, 0, 0))
    res = pl.pallas_call(
        _carrying(body, 7, 4, ex, grid) if ex.n else body, name=name, grid=grid,
        in_specs=[tok, pl.BlockSpec((1, LANES), lambda c, b: (0, c)), per_block((C_W8, C_W8)), per_block((C_W8, C_S8)),
                  per_block((C_S8, C_W8)), per_block((8, C_S8)), per_block((8, C_S8))] + ex.in_specs(),
        out_specs=[tok, tok, per_step((R, C_W8)), per_step((R, C_S8))] + ex.in_specs(),
        out_shape=[jax.ShapeDtypeStruct((T, D_MODEL), F32), jax.ShapeDtypeStruct((T, D_MODEL), F32),
                   jax.ShapeDtypeStruct((C_NB * B, R, C_W8), BF16), jax.ShapeDtypeStruct((C_NB * B, R, C_S8), F32)] + ex.out_shapes(),
        scratch_shapes=ex.scratch(),
        compiler_params=_cparams(*(("arbitrary",) * 2 if ex.n else ("parallel", "parallel"))),
    )(u, dsk, m8, e8, f8, tab_r, tab_i, *ex.operands())
    return tuple(res[:4]) + (list(res[4:]),)


def _s5_bwd(dgl, y, u, dsk, xs, m8, e8, f8, tab_r, tab_i, B, L, *, name):
    T = B * L
    R = L // C_TC
    nsteps = int(math.log2(R))

    def body(dgl_ref, y_ref, u_ref, d_ref, xs_ref, m_ref, e_ref, f_ref, tr_ref, ti_ref,
             du_ref, dy8_ref, de_ref, da_ref, dd_ref, dyf_ref):
        @pl.when(pl.program_id(1) == 0)
        def _():
            da_ref[...] = jnp.zeros_like(da_ref)
            dd_ref[...] = jnp.zeros_like(dd_ref)

        dd = jnp.zeros((1, LANES), F32)
        for t in range(C_TC):
            rows = _s5_rows(t, R)
            yv = y_ref[rows, :]
            cdf = 0.5 * (1.0 + _erf(yv * (2.0 ** -0.5)))
            pdf = jnp.exp(-0.5 * yv * yv) * (1.0 / math.sqrt(2.0 * math.pi))
            dy = dgl_ref[rows, :] * (cdf + yv * pdf)
            dd = dd + jnp.sum(dy * u_ref[rows, :], axis=0, keepdims=True)
            dyf_ref[:, t * LANES:(t + 1) * LANES] = dy
        dd_ref[...] += dd
        dy8 = dyf_ref[...].astype(BF16)
        dy8_ref[0] = dy8
        xs = xs_ref[0]
        gx = _dot(dy8, f_ref[0], _NT)
        row = lax.broadcasted_iota(jnp.int32, (R, C_S8), 0)
        for k in range(nsteps):
            s = 1 << k
            sh = pltpu.roll(gx, R - s, 0)
            upd = tr_ref[0, k:k + 1, :] * sh - ti_ref[0, k:k + 1, :] * pltpu.roll(sh, C_S8 // 2, 1)
            gx = gx + jnp.where(row + s < R, upd, 0.0)
        de_in = jnp.where(row + 1 < R, pltpu.roll(gx, R - 1, 0), 0.0)
        deb = de_in.astype(BF16)
        de_ref[0] = deb
        da_ref[0, 0:1, :] += jnp.sum(de_in * xs, axis=0, keepdims=True)
        da_ref[0, 1:2, :] += jnp.sum(de_in * pltpu.roll(xs, C_S8 // 2, 1), axis=0, keepdims=True)
        dx8 = _dot(dy8, m_ref[0], _NT) + _dot(deb, e_ref[0], _NT)
        d = d_ref[...]
        for t in range(C_TC):
            cols = slice(t * LANES, (t + 1) * LANES)
            du_ref[_s5_rows(t, R), :] = dx8[:, cols] + d * dyf_ref[:, cols]

    tok = pl.BlockSpec((L, LANES), lambda c, b: (b, c))
    vec = pl.BlockSpec((1, LANES), lambda c, b: (0, c))
    per_block = lambda shape: pl.BlockSpec((1,) + shape, lambda c, b: (c, 0, 0))
    per_step = lambda shape: pl.BlockSpec((1,) + shape, lambda c, b: (c * B + b, 0, 0))
    return pl.pallas_call(
        body, name=name, grid=(C_NB, B),
        in_specs=[tok, tok, tok, vec, per_step((R, C_S8)), per_block((C_W8, C_W8)),
                  per_block((C_W8, C_S8)), per_block((C_S8, C_W8)), per_block((8, C_S8)), per_block((8, C_S8))],
        out_specs=[tok, per_step((R, C_W8)), per_step((R, C_S8)), per_block((8, C_S8)), vec],
        out_shape=[jax.ShapeDtypeStruct((T, D_MODEL), F32), jax.ShapeDtypeStruct((C_NB * B, R, C_W8), BF16),
                   jax.ShapeDtypeStruct((C_NB * B, R, C_S8), BF16), jax.ShapeDtypeStruct((C_NB, 8, C_S8), F32),
                   jax.ShapeDtypeStruct((1, D_MODEL), F32)],
        scratch_shapes=[pltpu.VMEM((R, C_W8), F32)],
        compiler_params=_cparams("parallel", "arbitrary"),
    )(dgl, y, u, dsk, xs, m8, e8, f8, tab_r, tab_i)


def _bmm_tn(a, b, nb, fold, *, name):
    a = a.reshape(nb, -1, a.shape[-1])
    b = b.reshape(nb, -1, b.shape[-1])
    K, M, N = a.shape[1], a.shape[2], b.shape[2]
    half = C_GB * C_STATE

    def body(a_ref, b_ref, o_ref, p_ref):
        p_ref[...] = _dot(a_ref[0].astype(BF16), b_ref[0].astype(BF16), _TN)
        lane = lax.broadcasted_iota(jnp.int32, (1, LANES), 1)
        if fold == "lags":
            for lag in range(C_TC):
                blocks = [p_ref[s * LANES:(s + 1) * LANES, (s + lag) * LANES:(s + lag + 1) * LANES] for s in range(C_TC - lag)]
                o_ref[0, lag] = functools.reduce(lambda u, v: u + v, blocks)
        elif fold == "e":
            for g in range(C_GB):
                lo, hi = LANES * (g // 2), half + LANES * (g // 2)
                for s in range(C_TC):
                    rows = slice(s * LANES + g * C_GROUP, s * LANES + (g + 1) * C_GROUP)
                    re, im = p_ref[rows, lo:lo + LANES], p_ref[rows, hi:hi + LANES]
                    if g % 2 == 0:
                        im = pltpu.roll(im, C_STATE, 1)
                    else:
                        re = pltpu.roll(re, C_STATE, 1)
                    o_ref[0, rows, :] = jnp.where(lane < C_STATE, re, im)
        else:
            for z in range(2):
                for g in range(C_GB):
                    rows = slice(z * half + g * C_STATE, z * half + (g + 1) * C_STATE)
                    val = jnp.zeros((C_STATE, LANES), F32)
                    for t in range(C_TC):
                        blk = pltpu.roll(p_ref[rows, t * LANES:(t + 1) * LANES], ((t - g) * C_GROUP) % LANES, 1)
                        val = jnp.where((lane >= t * C_GROUP) & (lane < (t + 1) * C_GROUP), blk, val)
                    o_ref[0, rows, :] = val

    if fold == "lags":
        out_spec = pl.BlockSpec((1, C_TC, LANES, LANES), lambda c: (c, 0, 0, 0))
        out_shape = jax.ShapeDtypeStruct((nb, C_TC, LANES, LANES), F32)
    else:
        out_spec = pl.BlockSpec((1, M, LANES), lambda c: (c, 0, 0))
        out_shape = jax.ShapeDtypeStruct((nb, M, LANES), F32)
    return pl.pallas_call(
        body, name=name, grid=(nb,),
        in_specs=[pl.BlockSpec((1, K, M), lambda c: (c, 0, 0)), pl.BlockSpec((1, K, N), lambda c: (c, 0, 0))],
        out_specs=out_spec, out_shape=out_shape, scratch_shapes=[pltpu.VMEM((M, N), F32)],
        compiler_params=_cparams("parallel"),
    )(a, b)


def _loss_head(y, target, *, name):
    T, C = y.shape
    tm = 256

    def body(y_ref, t_ref, l_ref, d_ref):
        err = y_ref[...] - t_ref[...]
        d_ref[...] = err * (1.0 / C)
        sq = err * err
        part = jnp.zeros((8, LANES), F32)
        for r in range(0, tm, 8):
            for c in range(0, C, LANES):
                part = part + sq[r:r + 8, c:c + LANES]

        @pl.when(pl.program_id(0) == 0)
        def _():
            l_ref[...] = part

        @pl.when(pl.program_id(0) > 0)
        def _():
            l_ref[...] += part

    row = pl.BlockSpec((tm, C), lambda i: (i, 0))
    acc = pl.BlockSpec((8, LANES), lambda i: (0, 0))
    return pl.pallas_call(
        body, name=name, grid=(T // tm,), in_specs=[row, row], out_specs=[acc, row],
        out_shape=[jax.ShapeDtypeStruct((8, LANES), F32), jax.ShapeDtypeStruct((T, C), F32)],
        compiler_params=_cparams("arbitrary"),
    )(y, target)


def _adamw(w, g, m, v, *, name):
    shape = w.shape
    size = int(np.prod(shape))
    cols = LANES if (shape[-1] < LANES and size % LANES == 0) else shape[-1]
    rows = size // cols
    tm = _tile(rows, 256) if rows % 8 == 0 else rows
    w2, g2, m2, v2 = (t.reshape(rows, cols) for t in (w, g, m, v))

    def body(w_ref, g_ref, m_ref, v_ref, d_ref, nm_ref, nv_ref):
        gg = g_ref[...]
        nm = ADAM_B1 * m_ref[...] + (1.0 - ADAM_B1) * gg
        nv = ADAM_B2 * v_ref[...] + (1.0 - ADAM_B2) * (gg * gg)
        m_hat = nm / (1.0 - ADAM_B1 ** ADAM_STEP)
        v_hat = nv / (1.0 - ADAM_B2 ** ADAM_STEP)
        d_ref[...] = -ADAM_LR * (m_hat / (jnp.sqrt(v_hat) + ADAM_EPS) + ADAM_WD * w_ref[...])
        nm_ref[...] = nm
        nv_ref[...] = nv

    blk = pl.BlockSpec((tm, cols), lambda i: (i, 0))
    outs = pl.pallas_call(
        body, name=name, grid=(rows // tm,), in_specs=[blk] * 4, out_specs=[blk] * 3,
        out_shape=[jax.ShapeDtypeStruct((rows, cols), F32)] * 3, compiler_params=_cparams("parallel"),
    )(w2, g2, m2, v2)
    return tuple(o.reshape(shape) for o in outs)


def _all_gather(shard, *, name):
    R, C = shard.shape

    def body(x_ref, out_ref, send_sems, recv_sems, local_sem):
        x, y, c = lax.axis_index("x"), lax.axis_index("y"), lax.axis_index("c")
        me, sibling = (x, y, c), (x, y, 1 - c)
        chips = [(1 - x, y), (x, 1 - y), (1 - x, 1 - y)]

        def rows(px, py, pc):
            return out_ref.at[_logical(px, py, pc)]

        def copy(k, block, to, src=None):
            return pltpu.make_async_remote_copy(
                src_ref=rows(*block) if src is None else src, dst_ref=rows(*block),
                send_sem=send_sems.at[k], recv_sem=recv_sems.at[k], device_id=to, device_id_type=_MESH)

        mine = pltpu.make_async_copy(x_ref, rows(*me), local_sem)
        mine.start()
        first = [copy(0, me, sibling, src=x_ref)]
        first += [copy(1 + j, me, (*chip, c), src=x_ref) for j, chip in enumerate(chips)]
        for cp in first:
            cp.start()
        passed = [copy(4 + j, (*chip, c), sibling) for j, chip in enumerate(chips)]
        for j, chip in enumerate(chips):
            copy(1 + j, (*chip, c), me).wait_recv()
            passed[j].start()
        copy(0, sibling, me).wait_recv()
        for j, chip in enumerate(chips):
            copy(4 + j, (*chip, 1 - c), me).wait_recv()
        for cp in first + passed:
            cp.wait_send()
        mine.wait()

    return pl.pallas_call(
        body, name=name, out_shape=jax.ShapeDtypeStruct((N_DEV, R, C), shard.dtype),
        in_specs=[_HBM], out_specs=_HBM,
        scratch_shapes=[pltpu.SemaphoreType.DMA((7,)), pltpu.SemaphoreType.DMA((7,)), pltpu.SemaphoreType.DMA],
    )(shard)


def _gather_weights(shards, *, name):
    nt = len(shards)

    def body(*refs):
        ins, outs = refs[:nt], refs[nt:2 * nt]
        send_sems, recv_sems, local_sems = refs[2 * nt:]
        x, y, c = lax.axis_index("x"), lax.axis_index("y"), lax.axis_index("c")
        me, sibling = (x, y, c), (x, y, 1 - c)
        chips = [(1 - x, y), (x, 1 - y), (1 - x, 1 - y)]

        def copy(t, k, block, to, src=None):
            rows = outs[t].at[_logical(*block)]
            return pltpu.make_async_remote_copy(
                src_ref=rows if src is None else src, dst_ref=rows,
                send_sem=send_sems.at[t, k], recv_sem=recv_sems.at[t, k], device_id=to, device_id_type=_MESH)

        mine = [pltpu.make_async_copy(ins[t], outs[t].at[_logical(*me)], local_sems.at[t]) for t in range(nt)]
        for cp in mine:
            cp.start()
        started = []
        for t in range(nt):
            started.append(copy(t, 0, me, sibling, src=ins[t]))
            started += [copy(t, 1 + j, me, (*chip, c), src=ins[t]) for j, chip in enumerate(chips)]
        for cp in started:
            cp.start()
        for j, chip in enumerate(chips):
            for t in range(nt):
                copy(t, 1 + j, (*chip, c), me).wait_recv()
                fwd = copy(t, 4 + j, (*chip, c), sibling)
                fwd.start()
                started.append(fwd)
        for t in range(nt):
            copy(t, 0, sibling, me).wait_recv()
            for j, chip in enumerate(chips):
                copy(t, 4 + j, (*chip, 1 - c), me).wait_recv()
        for cp in started:
            cp.wait_send()
        for cp in mine:
            cp.wait()

    return pl.pallas_call(
        body, name=name, out_shape=[jax.ShapeDtypeStruct((N_DEV,) + s.shape, s.dtype) for s in shards],
        in_specs=[_HBM] * nt, out_specs=[_HBM] * nt,
        scratch_shapes=[pltpu.SemaphoreType.DMA((nt, 7)), pltpu.SemaphoreType.DMA((nt, 7)), pltpu.SemaphoreType.DMA((nt,))],
    )(*shards)


def _sum_rows(stacked, *, name):
    _, R, C = stacked.shape
    tr = R
    if N_DEV * R * C * stacked.dtype.itemsize > 12 * 1024 * 1024:
        for cand in range(512, 15, -16):
            if R % cand == 0:
                tr = cand
                break

    def body(s_ref, o_ref):
        acc = s_ref[0].astype(F32)
        for k in range(1, N_DEV):
            acc = acc + s_ref[k].astype(F32)
        o_ref[...] = acc

    return pl.pallas_call(
        body, name=name, grid=(R // tr,),
        in_specs=[pl.BlockSpec((N_DEV, tr, C), lambda i: (0, i, 0))], out_specs=pl.BlockSpec((tr, C), lambda i: (i, 0)),
        out_shape=jax.ShapeDtypeStruct((R, C), F32), compiler_params=_cparams("parallel"),
    )(stacked)


_LARGE = (("ab_w_in", 1, True), ("ab_w_out", 1, False), ("s5_w_glu", 1, True), ("xattn_wq", 2, False),
          ("xattn_wkv", 2, True), ("xattn_wo", 2, False), ("ffn_w_in", 2, True), ("ffn_w_out", 2, False))
_LARGE_KEYS = tuple((n, l) for n, layers, _ in _LARGE for l in range(layers))
_TRANSPOSED = {n: t for n, _, t in _LARGE}


def _owner_major(name, w):
    return w.T if _TRANSPOSED[name] else w


def _lb_from_logits(logits):
    return jnp.cumsum(jax.nn.softmax(logits, axis=0), axis=0)[0:1]


def _local_step(x, mem, target, W, shards=None):
    B, L, _ = x.shape
    T = B * L
    x0 = x.reshape(T, D_MODEL)
    memf = mem.reshape(B * MEM_LEN, D_MODEL)
    nw = W["norm_w"]
    cos2, sin2 = _rope_tables(L)
    W = dict(W)
    G, received = {}, {}

    def mmx(a, b, gather=(), scatter=(), **kw):
        if shards is None or not (gather or scatter):
            return _mm(a, b, **kw)
        out, gathered, got = _mm(a, b, gather=[shards[k] for k in gather],
                                 scatter=[G[k].reshape(N_DEV, -1, D_MODEL) for k in scatter], **kw)
        for k, g in zip(gather, gathered):
            W[k] = g.reshape(-1, D_MODEL)
        for k, r in zip(scatter, got):
            received[k] = r
        return out

    def vec(v):
        return v.reshape(1, -1)

    saved = []
    xin = x0
    for layer in range(2):
        s = {"x0": xin}
        tag = f"l{layer}"
        if layer == 0:
            h1 = _rms_fwd(xin, vec(nw[0, 0]), name="norm_pre_mix_l0", out_dtype=BF16)
        s["h1"] = h1
        if layer == 0:
            lb, lb_vjp = jax.vjp(_lb_from_logits, W["hgrn_lb_logits"])
            onw = W["hgrn_out_norm_w"].reshape(1, A_WIDTH)
            z = mmx(h1, W["ab_w_in", 0], tb=True, name="ab_in",
                    gather=[("ab_w_out", 0), ("xattn_wq", 0), ("xattn_wkv", 0), ("xattn_wo", 0)])
            ffn0 = [("ffn_w_in", 0), ("ffn_w_out", 0)] if shards is not None else []
            oa, o_raw, s_start, gathered = _hgrn_fwd(z, lb, onw, B, L, name="hgrn_fwd", gather=[shards[k] for k in ffn0])
            for key, g in zip(ffn0, gathered):
                W[key] = g.reshape(-1, D_MODEL)
            qr, kr = _rope_fwd(z, cos2, sin2, B, L, name="rope_qk")
            os_, ls_ = [], []
            for dil in B_DILS:
                o_g, l_g = _dil_fwd(qr, kr, z, dil, B, L, name=f"dil_fwd_{dil}")
                os_.append(o_g)
                ls_.append(l_g)
            ob, lse = _dil_combine(os_, ls_, name="dil_combine")
            ymix = jnp.concatenate([oa, ob.astype(BF16)], axis=-1)
            y1 = _mm(ymix, W["ab_w_out", 0], out_dtype=BF16, name="ab_out")
            s.update(z=z, lb=lb, lb_vjp=lb_vjp, onw=onw, o_raw=o_raw, s_start=s_start, qr=qr, kr=kr, ob=ob, lse=lse, ymix=ymix)
        else:
            p5 = tuple(W[n][0] for n in ("s5_lambda_re", "s5_lambda_im", "s5_log_dt", "s5_b_re", "s5_b_im", "s5_c_re", "s5_c_im"))
            (kern, ez, fz, _, _), s5_vjp = jax.vjp(_s5_build, *p5)
            tab_r, tab_i = _s5_scan_tables(p5[0], p5[1], p5[2], int(math.log2(L // C_TC)))
            kbd, lag_vjp = jax.vjp(_s5_lag_blocks, kern)
            mats = tuple(_s5_dense(kbd, ez, fz, name="s5_maps")) + (tab_r, tab_i)
            dsk = W["s5_d"].reshape(1, D_MODEL)
            ffn1 = [("ffn_w_in", 1)] if shards is not None else []
            gl, ypre, x8, xs, gathered = _s5_fwd(h1, dsk, *mats, B, L, name="s5_fwd", gather=[shards[k] for k in ffn1])
            for key, g in zip(ffn1, gathered):
                W[key] = g.reshape(-1, D_MODEL)
            w_glu = _interleave_rows(W["s5_w_glu", 0])
            zg, y1 = _mm(gl, w_glu, tb=True, name="s5_glu_in", gate=("fwd", "glu"))
            s.update(s5_vjp=s5_vjp, lag_vjp=lag_vjp, mats=mats, x8=x8, xs=xs, dsk=dsk, gl=gl, ypre=ypre, zg=zg, w_glu=w_glu)
        x1, h2 = _rms_fwd(y1, vec(nw[layer, 1]), xin, name=f"norm_post_mix_{tag}", then=(vec(nw[layer, 2]), BF16))
        memn = _rms_fwd(memf, vec(W["mem_norm_w"][layer]), name=f"norm_mem_{tag}", out_dtype=BF16)
        q = _mm(h2, W["xattn_wq", layer], out_dtype=BF16, name=f"x_q_{tag}")
        kv = _mm(memn, W["xattn_wkv", layer], tb=True, out_dtype=BF16, name=f"x_kv_{tag}")
        o = _xattn_fwd(q, kv, B, L, name=f"x_attn_{tag}")
        y2 = _mm(o, W["xattn_wo", layer], out_dtype=BF16, name=f"x_o_{tag}")
        x2, h3 = _rms_fwd(y2, vec(nw[layer, 3]), x1, name=f"norm_post_x_{tag}", then=(vec(nw[layer, 4]), BF16))
        w_ffn_in = _interleave_rows(W["ffn_w_in", layer])
        zf, u = mmx(h3, w_ffn_in, tb=True, name=f"ffn_in_{tag}", gate=("fwd", "swiglu"),
                    gather=[("s5_w_glu", 0), ("xattn_wq", 1), ("xattn_wkv", 1), ("xattn_wo", 1)] if layer == 0 else [])
        y3 = mmx(u, W["ffn_w_out", layer], out_dtype=BF16, name=f"ffn_out_{tag}", gather=[("ffn_w_out", 1)] if layer == 0 else [])
        if layer == 0:
            x3, h1 = _rms_fwd(y3, vec(nw[0, 5]), x2, name="norm_post_ffn_l0", then=(vec(nw[1, 0]), F32))
        else:
            x3 = _rms_fwd(y3, vec(nw[layer, 5]), x2, name=f"norm_post_ffn_{tag}")
        s.update(y1=y1, x1=x1, h2=h2, memn=memn, q=q, kv=kv, o=o, y2=y2, x2=x2, h3=h3, zf=zf, u=u, y3=y3, w_ffn_in=w_ffn_in)
        saved.append(s)
        xin = x3

    loss_parts, dx = _loss_head(xin, target.reshape(T, D_MODEL), name="loss_head")

    d_norm = [[None] * 6 for _ in range(2)]
    d_memn = [None, None]
    for layer in (1, 0):
        s = saved[layer]
        tag = f"l{layer}"
        if layer == 1:
            dy3, d_norm[1][5] = _rms_bwd(s["y3"], vec(nw[1, 5]), dx, name="bnorm_post_ffn_l1", out_dtype=BF16)
        dzf = mmx(dy3, W["ffn_w_out", layer], tb=True, name=f"b_ffn_out_dx_{tag}", gate=("bwd", "swiglu", s["zf"]),
                  scatter=[("xattn_wkv", 1), ("s5_w_glu", 0)] if layer == 0 else [])
        G["ffn_w_out", layer] = _mm(s["u"], dy3, ta=True, out_dtype=BF16, name=f"b_ffn_out_dw_{tag}")
        early = []
        if shards is not None and layer == 0:
            shards = {**shards, "small_early": _pack_small([G[n] for n, _ in _SMALL_EARLY])}
            early = ["small_early"]
        G["ffn_w_in", layer] = _interleave_rows(
            mmx(dzf, s["h3"], ta=True, out_dtype=BF16, name=f"b_ffn_in_dw_{tag}", gather=early,
                scatter=[("ffn_w_out", layer)]), inverse=True)
        if early:
            received["small_early"] = W["small_early"].reshape(N_DEV, -1, LANES)
        dh3 = mmx(dzf, s["w_ffn_in"], out_dtype=BF16, name=f"b_ffn_in_dx_{tag}", scatter=[("ffn_w_in", layer)])
        dx, d_norm[layer][4], dy2, d_norm[layer][3] = _rms_bwd(
            s["x2"], vec(nw[layer, 4]), dh3, dx, name=f"bnorm_pre_ffn_{tag}", then=(s["y2"], vec(nw[layer, 3])))
        do = _mm(dy2, W["xattn_wo", layer], tb=True, out_dtype=BF16, name=f"b_x_o_dx_{tag}")
        G["xattn_wo", layer] = _mm(s["o"], dy2, ta=True, out_dtype=BF16, name=f"b_x_o_dw_{tag}")
        dq, dkv = _xattn_bwd(s["q"], s["kv"], do, B, L, name=f"b_x_attn_{tag}")
        G["xattn_wq", layer] = _mm(s["h2"], dq, ta=True, out_dtype=BF16, name=f"b_x_q_dw_{tag}")
        dh2 = _mm(dq, W["xattn_wq", layer], tb=True, out_dtype=BF16, name=f"b_x_q_dx_{tag}")
        G["xattn_wkv", layer] = _mm(dkv, s["memn"], ta=True, out_dtype=BF16, name=f"b_x_kv_dw_{tag}")
        dmemn = _mm(dkv, W["xattn_wkv", layer], out_dtype=BF16, name=f"b_x_kv_dx_{tag}")
        _, d_memn[layer] = _rms_bwd(memf, vec(W["mem_norm_w"][layer]), dmemn, name=f"bnorm_mem_{tag}", out_dtype=BF16)
        dx, d_norm[layer][2], dy1, d_norm[layer][1] = _rms_bwd(
            s["x1"], vec(nw[layer, 2]), dh2, dx, name=f"bnorm_pre_x_{tag}", then=(s["y1"], vec(nw[layer, 1])))
        if layer == 0:
            z = s["z"]
            dymix = _mm(dy1, W["ab_w_out", 0], tb=True, name="b_ab_out_dx")
            G["ab_w_out", 0] = _mm(s["ymix"], dy1, ta=True, out_dtype=BF16, name="b_ab_out_dw")
            early = [("xattn_wo", 0), ("xattn_wq", 0), ("xattn_wkv", 0), ("ab_w_out", 0)] if shards is not None else []
            dqa, dfa, dia, dga, d_onw, d_lb, got = _hgrn_bwd(
                z, s["lb"], s["onw"], s["o_raw"], s["s_start"], dymix, B, L, name="hgrn_bwd",
                scatter=[G[k].reshape(N_DEV, -1, D_MODEL) for k in early])
            received.update(zip(early, got))
            dqs, dks, dvs = [], [], []
            for dil in B_DILS:
                dqs.append(_dil_bwd_q(s["qr"], s["kr"], z, dymix, s["ob"], s["lse"], dil, B, L, name=f"dil_bwd_q_{dil}"))
                dk_g, dv_g = _dil_bwd_kv(s["qr"], s["kr"], z, dymix, s["ob"], s["lse"], dil, B, L, name=f"dil_bwd_kv_{dil}")
                dks.append(dk_g)
                dvs.append(dv_g)
            dqkv = _rope_bwd(dqs, dks, dvs, cos2, sin2, B, L, name="b_rope")
            dz = jnp.concatenate([dqa, dfa, dia, dga, dqkv], axis=-1)
            G["ab_w_in", 0] = _mm(dz, s["h1"], ta=True, out_dtype=BF16, name="b_ab_in_dw")
            dh1 = mmx(dz, W["ab_w_in", 0], out_dtype=BF16, name="b_ab_in_dx", scatter=[("ab_w_in", 0)])
            G["hgrn_out_norm_w"] = jnp.sum(d_onw.reshape(B, A_WIDTH), axis=0, keepdims=True)
            d_lb_row = jnp.sum(d_lb.reshape(B, A_WIDTH), axis=0, keepdims=True)
            G["hgrn_lb_logits"] = s["lb_vjp"](d_lb_row)[0]
        else:
            dzg = _gated_bwd(s["zg"], dy1, "glu", name="b_s5_glu")
            G["s5_w_glu", 0] = _interleave_rows(
                _mm(dzg, s["gl"], ta=True, out_dtype=BF16, name="b_s5_glu_dw"), inverse=True)
            dgl = mmx(dzg, s["w_glu"], name="b_s5_glu_dx", scatter=[("xattn_wo", 1), ("xattn_wq", 1)])
            dh1, dy8, de_in, da, d_dsk = _s5_bwd(dgl, s["ypre"], s["h1"], s["dsk"], s["xs"], *s["mats"], B, L, name="s5_bwd")
            dkbd = _bmm_tn(s["x8"], dy8, C_NB, "lags", name="s5_bwd_dm").transpose(1, 0, 2, 3)
            dfz = _bmm_tn(s["xs"], dy8, C_NB, "f", name="s5_bwd_df").reshape(C_NB, 2, C_GB, C_STATE, C_TC, C_GROUP)
            dez = _bmm_tn(s["x8"], de_in, C_NB, "e", name="s5_bwd_de").reshape(C_NB, C_TC, C_GB, C_GROUP, 2, C_STATE)
            half = C_S8 // 2
            da_r = (da[:, 0, :half] + da[:, 0, half:]).reshape(C_GROUPS, C_STATE)
            da_i = (da[:, 1, half:] - da[:, 1, :half]).reshape(C_GROUPS, C_STATE)
            gp = s["s5_vjp"](s["lag_vjp"](dkbd) + (dez, dfz, da_r, da_i))
            for n, gv in zip(("s5_lambda_re", "s5_lambda_im", "s5_log_dt", "s5_b_re", "s5_b_im", "s5_c_re", "s5_c_im"), gp):
                G[n] = gv[None]
            G["s5_d"] = d_dsk
        if layer == 1:
            dx, d_norm[1][0], dy3, d_norm[0][5] = _rms_bwd(
                s["x0"], vec(nw[1, 0]), dh1, dx, name="bnorm_pre_mix_l1", then=(saved[0]["y3"], vec(nw[0, 5])))
        else:
            dx, d_norm[0][0] = _rms_bwd(s["x0"], vec(nw[0, 0]), dh1, dx, name="bnorm_pre_mix_l0")

    G["norm_w"] = jnp.stack([jnp.concatenate(d_norm[l], axis=0) for l in range(2)])
    G["mem_norm_w"] = jnp.concatenate(d_memn, axis=0)
    if shards is not None:
        G.update(received)
    return loss_parts, dx.reshape(B, L, D_MODEL), G


_SMALL_LATE = (("norm_w", (2, 6, 1024)), ("mem_norm_w", (2, 1024)), ("hgrn_lb_logits", (3, 512)), ("hgrn_out_norm_w", (1, 512)))
_SMALL_EARLY = (("s5_lambda_re", (1, 64, 64)), ("s5_lambda_im", (1, 64, 64)), ("s5_log_dt", (1, 64)),
                ("s5_b_re", (1, 64, 64, 16)), ("s5_b_im", (1, 64, 64, 16)), ("s5_c_re", (1, 64, 16, 64)),
                ("s5_c_im", (1, 64, 16, 64)), ("s5_d", (1, 1024)))
_SMALL = _SMALL_LATE + _SMALL_EARLY


def _pack_small(values):
    flat = jnp.concatenate([v.reshape(-1) for v in values])
    return jnp.pad(flat, (0, (-flat.shape[0]) % (8 * LANES))).reshape(-1, LANES)


def _unpack_small(flat, spec):
    out, off = {}, 0
    for n, shp in spec:
        size = int(np.prod(shp))
        out[n] = flat[off:off + size].reshape(shp)
        off += size
    return out, off

_WEIGHT_ORDER = ('norm_w', 'mem_norm_w', 'ab_w_in', 'ab_w_out', 'hgrn_lb_logits', 'hgrn_out_norm_w', 's5_lambda_re',
                 's5_lambda_im', 's5_log_dt', 's5_b_re', 's5_b_im', 's5_c_re', 's5_c_im', 's5_d', 's5_w_glu', 'xattn_wq',
                 'xattn_wkv', 'xattn_wo', 'ffn_w_in', 'ffn_w_out')


def kernel(x, mem, norm_w, mem_norm_w, ab_w_in, ab_w_out, hgrn_lb_logits, hgrn_out_norm_w, s5_lambda_re, s5_lambda_im, s5_log_dt, s5_b_re, s5_b_im, s5_c_re, s5_c_im, s5_d, s5_w_glu, xattn_wq, xattn_wkv, xattn_wo, ffn_w_in, ffn_w_out, loss_target, m_norm_w, m_mem_norm_w, m_ab_w_in, m_ab_w_out, m_hgrn_lb_logits, m_hgrn_out_norm_w, m_s5_lambda_re, m_s5_lambda_im, m_s5_log_dt, m_s5_b_re, m_s5_b_im, m_s5_c_re, m_s5_c_im, m_s5_d, m_s5_w_glu, m_xattn_wq, m_xattn_wkv, m_xattn_wo, m_ffn_w_in, m_ffn_w_out, v_norm_w, v_mem_norm_w, v_ab_w_in, v_ab_w_out, v_hgrn_lb_logits, v_hgrn_out_norm_w, v_s5_lambda_re, v_s5_lambda_im, v_s5_log_dt, v_s5_b_re, v_s5_b_im, v_s5_c_re, v_s5_c_im, v_s5_d, v_s5_w_glu, v_xattn_wq, v_xattn_wkv, v_xattn_wo, v_ffn_w_in, v_ffn_w_out):
    local = dict(norm_w=norm_w, mem_norm_w=mem_norm_w, ab_w_in=ab_w_in, ab_w_out=ab_w_out, hgrn_lb_logits=hgrn_lb_logits,
                 hgrn_out_norm_w=hgrn_out_norm_w, s5_lambda_re=s5_lambda_re, s5_lambda_im=s5_lambda_im, s5_log_dt=s5_log_dt,
                 s5_b_re=s5_b_re, s5_b_im=s5_b_im, s5_c_re=s5_c_re, s5_c_im=s5_c_im, s5_d=s5_d, s5_w_glu=s5_w_glu,
                 xattn_wq=xattn_wq, xattn_wkv=xattn_wkv, xattn_wo=xattn_wo, ffn_w_in=ffn_w_in, ffn_w_out=ffn_w_out)
    mom_m = dict(zip(_WEIGHT_ORDER, (m_norm_w, m_mem_norm_w, m_ab_w_in, m_ab_w_out, m_hgrn_lb_logits, m_hgrn_out_norm_w, m_s5_lambda_re, m_s5_lambda_im, m_s5_log_dt, m_s5_b_re, m_s5_b_im, m_s5_c_re, m_s5_c_im, m_s5_d, m_s5_w_glu, m_xattn_wq, m_xattn_wkv, m_xattn_wo, m_ffn_w_in, m_ffn_w_out)))
    mom_v = dict(zip(_WEIGHT_ORDER, (v_norm_w, v_mem_norm_w, v_ab_w_in, v_ab_w_out, v_hgrn_lb_logits, v_hgrn_out_norm_w, v_s5_lambda_re, v_s5_lambda_im, v_s5_log_dt, v_s5_b_re, v_s5_b_im, v_s5_c_re, v_s5_c_im, v_s5_d, v_s5_w_glu, v_xattn_wq, v_xattn_wkv, v_xattn_wo, v_ffn_w_in, v_ffn_w_out)))
    dev = 4 * lax.axis_index("x") + 2 * lax.axis_index("y") + lax.axis_index("c")

    shards = {(n, l): _owner_major(n, local[n][l]).astype(BF16) for n, l in _LARGE_KEYS}
    first = ("ab_w_in", 0)
    W = {first: _gather_weights([shards[first]], name="gather_first")[0].reshape(-1, D_MODEL)}
    tiny = jnp.concatenate([norm_w.reshape(-1), s5_d.reshape(-1)])
    tiny = jnp.pad(tiny, (0, 16 * LANES - tiny.shape[0])).reshape(16, LANES)
    tiny_all = _all_gather(tiny, name="gather_tiny").reshape(N_DEV, 16 * LANES)
    W["norm_w"] = tiny_all[:, :12 * LANES].reshape(N_DEV, 2, 6, LANES).transpose(1, 2, 0, 3).reshape(2, 6, D_MODEL)
    W["s5_d"] = tiny_all[:, 12 * LANES:13 * LANES].reshape(1, D_MODEL)
    for n in ("mem_norm_w", "hgrn_lb_logits", "hgrn_out_norm_w", "s5_lambda_re", "s5_lambda_im", "s5_log_dt",
              "s5_b_re", "s5_b_im", "s5_c_re", "s5_c_im"):
        W[n] = local[n]

    loss_parts, grad_x, G = _local_step(x, mem, loss_target, W, shards)

    g_layers = {}
    for n, l in _LARGE_KEYS:
        g = _sum_rows(G[n, l], name=f"sum_grads_{n}_{l}")
        g_layers.setdefault(n, []).append(g.T if _TRANSPOSED[n] else g)
    g_local = {n: jnp.stack(gl) for n, gl in g_layers.items()}
    small = _pack_small([G[n] for n, _ in _SMALL_LATE] + [0.5 / D_MODEL * jnp.sum(loss_parts)])
    small_sum = _sum_rows(_all_gather(small, name="gather_small"), name="sum_small").reshape(-1)
    g_full, off = _unpack_small(small_sum, _SMALL_LATE)
    loss = small_sum[off]
    early_sum = _sum_rows(G["small_early"], name="sum_small_early").reshape(-1)
    g_full.update(_unpack_small(early_sum, _SMALL_EARLY)[0])
    grads = dict(g_local)
    for n, shp in _SMALL:
        if n == "norm_w":
            grads[n] = lax.dynamic_slice_in_dim(g_full[n], dev * LANES, LANES, axis=2)
        elif n == "s5_d":
            grads[n] = lax.dynamic_slice_in_dim(g_full[n], dev * LANES, LANES, axis=1)
        else:
            grads[n] = g_full[n]

    delta, new_m, new_v = {}, {}, {}
    for n in _WEIGHT_ORDER:
        delta[n], new_m[n], new_v[n] = _adamw(local[n], grads[n], mom_m[n], mom_v[n], name=f"adamw_{n}")
    return (loss, grad_x, *[grads[n] for n in _WEIGHT_ORDER], *[delta[n] for n in _WEIGHT_ORDER],
            *[new_m[n] for n in _WEIGHT_ORDER], *[new_v[n] for n in _WEIGHT_ORDER])
```

```python
import functools
import math

import numpy as np
import jax
import jax.numpy as jnp
from jax import lax
from jax.experimental import pallas as pl
from jax.experimental.pallas import tpu as pltpu

F32 = jnp.float32
BF16 = jnp.bfloat16
HI = lax.Precision.HIGHEST

D_MODEL = 1024
NORM_EPS = 1e-6
A_WIDTH = 512
A_HEAD = 128
A_CHUNK = 32
A_SUPER = 256
B_SPAN = 128
B_DILS = (1, 4, 16)
ROPE_THETA = 10000.0
C_GROUPS = 64
C_GROUP = 16
C_STATE = 64
C_TC = 8
C_MIN_NEG_RE = -1e-4
MEM_LEN = 256
X_HEADS = 4
X_HD = 256
D_FF = 2816
N_DEV = 8
LANES = 128

ADAM_LR, ADAM_B1, ADAM_B2, ADAM_EPS, ADAM_WD, ADAM_STEP = 0.001, 0.9, 0.999, 1e-08, 0.01, 10

NEG_BIG = -1e30


def _tile(n, pref):
    for d in range(min(pref, n) // LANES * LANES, 0, -LANES):
        if n % d == 0:
            return d
    return n


def _cparams(*sem):
    return pltpu.CompilerParams(dimension_semantics=sem, vmem_limit_bytes=56 * 1024 * 1024)


def _sigmoid(x):
    return 0.5 * jnp.tanh(0.5 * x) + 0.5


def _erf(x):
    ax = jnp.abs(x)
    t = 1.0 / (1.0 + 0.3275911 * ax)
    poly = t * (0.254829592 + t * (-0.284496736 + t * (1.421413741 + t * (-1.453152027 + t * 1.061405429))))
    y = 1.0 - poly * jnp.exp(-ax * ax)
    return jnp.where(x < 0, -y, y)


_HBM = pl.BlockSpec(memory_space=pltpu.HBM)
_MESH = pl.DeviceIdType.MESH


def _logical(px, py, pc):
    return 4 * px + 2 * py + pc


class _Exchange:
    def __init__(self, gather=(), scatter=()):
        self.gather, self.scatter = list(gather), list(scatter)
        self.ng, self.n = len(self.gather), len(self.gather) + len(self.scatter)

    def operands(self):
        return self.gather + self.scatter

    def in_specs(self):
        return [_HBM] * self.n

    def out_shapes(self):
        return ([jax.ShapeDtypeStruct((N_DEV,) + g.shape, g.dtype) for g in self.gather]
                + [jax.ShapeDtypeStruct(s.shape, s.dtype) for s in self.scatter])

    def scratch(self):
        if not self.n:
            return []
        return [pltpu.SemaphoreType.DMA((self.n, 7)), pltpu.SemaphoreType.DMA((self.n, 7)), pltpu.SemaphoreType.DMA((self.n,))]

    def split(self, results):
        return list(results[:self.ng]), list(results[self.ng:])

    def run(self, ins, outs, sems, first, last):
        if not self.n:
            return
        send_sems, recv_sems, local_sems = sems
        x, y, c = lax.axis_index("x"), lax.axis_index("y"), lax.axis_index("c")
        me, sibling = _logical(x, y, c), (x, y, 1 - c)
        chips = [(1 - x, y), (x, 1 - y), (1 - x, 1 - y)]
        peers = [(x ^ (k >> 2), y ^ ((k >> 1) & 1), c ^ (k & 1)) for k in range(1, N_DEV)]

        def remote(t, k, src, dst, to):
            return pltpu.make_async_remote_copy(src_ref=src, dst_ref=dst, send_sem=send_sems.at[t, k],
                                                recv_sem=recv_sems.at[t, k], device_id=to, device_id_type=_MESH)

        def local(t):
            src = ins[t] if t < self.ng else ins[t].at[me]
            return pltpu.make_async_copy(src, outs[t].at[me], local_sems.at[t])

        @pl.when(first)
        def _():
            for t in range(self.n):
                local(t).start()
                if t < self.ng:
                    remote(t, 0, ins[t], outs[t].at[me], sibling).start()
                    for j, chip in enumerate(chips):
                        remote(t, 1 + j, ins[t], outs[t].at[me], (*chip, c)).start()
                else:
                    for k, peer in enumerate(peers):
                        remote(t, k, ins[t].at[_logical(*peer)], outs[t].at[me], peer).start()

        @pl.when(last)
        def _():
            for j, chip in enumerate(chips):
                for t in range(self.ng):
                    landed = outs[t].at[_logical(*chip, c)]
                    remote(t, 1 + j, ins[t], landed, sibling).wait_recv()
                    remote(t, 4 + j, landed, landed, sibling).start()
            for t in range(self.n):
                if t < self.ng:
                    remote(t, 0, ins[t], outs[t].at[_logical(*sibling)], sibling).wait_recv()
                    for j, chip in enumerate(chips):
                        remote(t, 4 + j, ins[t], outs[t].at[_logical(*chip, 1 - c)], sibling).wait_recv()
                    for k in range(7):
                        remote(t, k, ins[t], outs[t].at[me], sibling).wait_send()
                else:
                    for k, peer in enumerate(peers):
                        remote(t, k, ins[t].at[me], outs[t].at[_logical(*peer)], peer).wait_recv()
                    for k, peer in enumerate(peers):
                        remote(t, k, ins[t].at[_logical(*peer)], outs[t].at[me], peer).wait_send()
                local(t).wait()


def _carrying(body, n_in, n_out, ex, grid):
    n_sems = len(ex.scratch())

    def wrapped(*refs):
        ins, ex_in = refs[:n_in], refs[n_in:n_in + ex.n]
        outs = refs[n_in + ex.n:n_in + ex.n + n_out]
        ex_out = refs[n_in + ex.n + n_out:n_in + 2 * ex.n + n_out]
        rest = refs[n_in + 2 * ex.n + n_out:]
        scratch, sems = rest[:len(rest) - n_sems], rest[len(rest) - n_sems:]
        ids = [pl.program_id(a) for a in range(len(grid))]
        first = functools.reduce(lambda u, v: u & v, [i == 0 for i in ids])
        last = functools.reduce(lambda u, v: u & v, [i == g - 1 for i, g in zip(ids, grid)])
        ex.run(ex_in, ex_out, sems, first, last)
        body(*ins, *outs, *scratch)

    return wrapped


_MM_VMEM_BUDGET = 36 * 1024 * 1024


def _mm(a, b, *, ta=False, tb=False, out_dtype=F32, name, tiles=(1408, 1408, 4096), gather=(), scatter=(), gate=None):
    M, K = (a.shape[1], a.shape[0]) if ta else a.shape
    N = b.shape[0] if tb else b.shape[1]
    assert (b.shape[1] if tb else b.shape[0]) == K
    gate_mode = gate[0] if gate else None
    tm, tn, tk = _tile(M, tiles[0]), _tile(N, tiles[1]), _tile(K, tiles[2])
    if gate_mode == "fwd":
        bs = _gate_block(N // 2)
        tn = 2 * bs
    elif gate_mode == "bwd":
        bs = _gate_block(N)
        tn = bs

    def vmem_bytes():
        acc = 4 * tm * tn if tk < K else 0
        if gate_mode == "fwd":
            io = (2 * (2 + 1) + 4) * tm * tn
        elif gate_mode == "bwd":
            io = (2 * (4 + 4) + 4) * tm * tn
        else:
            io = 2 * tm * tn * jnp.dtype(out_dtype).itemsize
        return 2 * 2 * (tm * tk + tk * tn) + acc + io

    while vmem_bytes() > _MM_VMEM_BUDGET:
        if gate_mode and tm > 256:
            tm = _tile(M, tm - LANES)
        elif tk > 512:
            tk = _tile(K, tk - LANES)
        elif tn > 256 and not gate_mode:
            tn = _tile(N, tn - LANES)
        else:
            tm = _tile(M, tm - LANES)
    ni, nj, nk = M // tm, N // tn, K // tk
    ex = _Exchange(gather, scatter)

    a_spec = pl.BlockSpec((tk, tm), lambda i, j, k: (k, i)) if ta else pl.BlockSpec((tm, tk), lambda i, j, k: (i, k))
    b_spec = pl.BlockSpec((tn, tk), lambda i, j, k: (j, k)) if tb else pl.BlockSpec((tk, tn), lambda i, j, k: (k, j))
    dims = (((0 if ta else 1,), (1 if tb else 0,)), ((), ()))
    n_acc = 1 if nk > 1 else 0
    n_in = 3 if gate_mode == "bwd" else 2
    n_out = 2 if gate_mode == "fwd" else 1

    def finish(val, in_refs, out_refs, rows=slice(None)):
        if gate_mode is None:
            out_refs[0][rows, :] = val.astype(out_refs[0].dtype)
        elif gate_mode == "fwd":
            out_refs[0][rows, :] = val.astype(BF16)
            for p in range(tn // (2 * bs)):
                a_, b_ = val[:, 2 * p * bs:(2 * p + 1) * bs], val[:, (2 * p + 1) * bs:(2 * p + 2) * bs]
                out_refs[1][rows, p * bs:(p + 1) * bs] = _gate_value(a_, b_, gate[1]).astype(BF16)
        else:
            z_ref = in_refs[2]
            for p in range(tn // bs):
                a_ = z_ref[rows, 2 * p * bs:(2 * p + 1) * bs].astype(F32)
                b_ = z_ref[rows, (2 * p + 1) * bs:(2 * p + 2) * bs].astype(F32)
                da, db = _gate_grads(a_, b_, val[:, p * bs:(p + 1) * bs], gate[1])
                out_refs[0][rows, 2 * p * bs:(2 * p + 1) * bs] = da.astype(BF16)
                out_refs[0][rows, (2 * p + 1) * bs:(2 * p + 2) * bs] = db.astype(BF16)

    halves = 2 if (gate_mode and nk == 1 and not ta and tm % 32 == 0) else 1

    def body(*refs):
        in_refs, rest = refs[:n_in], refs[n_in:]
        ex_in, out_refs = rest[:ex.n], rest[ex.n:ex.n + n_out]
        ex_out, scratch = rest[ex.n + n_out:2 * ex.n + n_out], rest[2 * ex.n + n_out:]
        i, j, k = pl.program_id(0), pl.program_id(1), pl.program_id(2)
        ex.run(ex_in, ex_out, scratch[n_acc:], (i == 0) & (j == 0) & (k == 0), (i == ni - 1) & (j == nj - 1) & (k == nk - 1))
        if halves > 1:
            rh = tm // halves
            rhs = in_refs[1][...].astype(BF16)
            parts = [lax.dot_general(in_refs[0][r * rh:(r + 1) * rh, :].astype(BF16), rhs, dims, preferred_element_type=F32)
                     for r in range(halves)]
            for r, p in enumerate(parts):
                finish(p, in_refs, out_refs, slice(r * rh, (r + 1) * rh))
            return
        part = lax.dot_general(in_refs[0][...].astype(BF16), in_refs[1][...].astype(BF16), dims, preferred_element_type=F32)
        if nk == 1:
            finish(part, in_refs, out_refs)
            return
        acc_ref = scratch[0]

        @pl.when(k == 0)
        def _():
            acc_ref[...] = part

        @pl.when(k > 0)
        def _():
            acc_ref[...] += part

        @pl.when(k == nk - 1)
        def _():
            finish(acc_ref[...], in_refs, out_refs)

    tile = lambda width: pl.BlockSpec((tm, width), lambda i, j, k: (i, j))
    if gate_mode == "fwd":
        out_specs, out_shape = [tile(tn), tile(tn // 2)], [jax.ShapeDtypeStruct((M, N), BF16), jax.ShapeDtypeStruct((M, N // 2), BF16)]
    elif gate_mode == "bwd":
        out_specs, out_shape = [tile(2 * tn)], [jax.ShapeDtypeStruct((M, 2 * N), BF16)]
    else:
        out_specs, out_shape = [tile(tn)], [jax.ShapeDtypeStruct((M, N), out_dtype)]
    operands = [a, b] + ([gate[2]] if gate_mode == "bwd" else [])
    sem = ("arbitrary",) * 3 if ex.n else ("parallel", "parallel", "arbitrary")
    res = pl.pallas_call(
        body, name=name, grid=(ni, nj, nk),
        in_specs=[a_spec, b_spec] + ([tile(2 * tn)] if gate_mode == "bwd" else []) + ex.in_specs(),
        out_specs=out_specs + ex.in_specs(),
        out_shape=out_shape + ex.out_shapes(),
        scratch_shapes=([pltpu.VMEM((tm, tn), F32)] if nk > 1 else []) + ex.scratch(),
        compiler_params=_cparams(*sem),
    )(*operands, *ex.operands())
    main = res[0] if n_out == 1 else tuple(res[:n_out])
    if not ex.n:
        return main
    return (main,) + tuple(ex.split(res[n_out:]))


def _rms_fwd(x, w, res=None, *, name, out_dtype=F32, then=None):
    T, C = x.shape
    tm = _tile(T, 1024)
    has_res = res is not None

    def norm(v, w_ref):
        return v * lax.rsqrt(jnp.mean(v * v, axis=-1, keepdims=True) + NORM_EPS) * w_ref[...]

    def body(*refs):
        x_ref, w_ref = refs[0], refs[1]
        y = norm(x_ref[...].astype(F32), w_ref)
        if has_res:
            y = y + refs[2][...]
        if then is None:
            refs[-1][...] = y.astype(refs[-1].dtype)
        else:
            refs[-2][...] = y.astype(refs[-2].dtype)
            refs[-1][...] = norm(y, refs[-3]).astype(refs[-1].dtype)

    row = pl.BlockSpec((tm, C), lambda i: (i, 0))
    vec = pl.BlockSpec((1, C), lambda i: (0, 0))
    ins = [x, w] + ([res] if has_res else []) + ([then[0]] if then else [])
    in_specs = [row, vec] + ([row] if has_res else []) + ([vec] if then else [])
    out_shape = [jax.ShapeDtypeStruct((T, C), out_dtype)] + ([jax.ShapeDtypeStruct((T, C), then[1])] if then else [])
    res_ = pl.pallas_call(
        body, name=name, grid=(T // tm,), in_specs=in_specs, out_specs=[row] * len(out_shape),
        out_shape=out_shape, compiler_params=_cparams("parallel"),
    )(*ins)
    return tuple(res_) if then else res_[0]


def _rms_bwd(x, w, dy, add=None, *, name, out_dtype=F32, then=None):
    T, C = x.shape
    tm = _tile(T, 512)
    has_add = add is not None
    n_in = 3 + has_add + (2 if then else 0)

    def grads(x_ref, w_ref, g, dw_ref):
        xv = x_ref[...].astype(F32)
        r = lax.rsqrt(jnp.mean(xv * xv, axis=-1, keepdims=True) + NORM_EPS)
        xh = xv * r
        part = jnp.sum(g * xh, axis=0, keepdims=True)

        @pl.when(pl.program_id(0) == 0)
        def _():
            dw_ref[...] = part

        @pl.when(pl.program_id(0) > 0)
        def _():
            dw_ref[...] += part

        gx = g * w_ref[...]
        return r * (gx - xh * jnp.mean(gx * xh, axis=-1, keepdims=True))

    def body(*refs):
        ins, outs = refs[:n_in], refs[n_in:]
        dx = grads(ins[0], ins[1], ins[2][...].astype(F32), outs[1])
        if has_add:
            dx = dx + ins[3][...]
        outs[0][...] = dx.astype(outs[0].dtype)
        if then:
            outs[2][...] = grads(ins[-2], ins[-1], dx, outs[3]).astype(BF16)

    row = pl.BlockSpec((tm, C), lambda i: (i, 0))
    vec = pl.BlockSpec((1, C), lambda i: (0, 0))
    ins = [x, w, dy] + ([add] if has_add else []) + (list(then) if then else [])
    big, small = jax.ShapeDtypeStruct((T, C), out_dtype), jax.ShapeDtypeStruct((1, C), F32)
    return pl.pallas_call(
        body, name=name, grid=(T // tm,),
        in_specs=[row, vec, row] + ([row] if has_add else []) + ([row, vec] if then else []),
        out_specs=[row, vec] + ([row, vec] if then else []),
        out_shape=[big, small] + ([jax.ShapeDtypeStruct((T, C), BF16), small] if then else []),
        compiler_params=_cparams("arbitrary"),
    )(*ins)


def _gate_block(width):
    return _tile(width, 1408)


def _gate_value(a, b, kind):
    return a * _sigmoid(a) * b if kind == "swiglu" else a * _sigmoid(b)


def _gate_grads(a, b, d, kind):
    if kind == "swiglu":
        s = _sigmoid(a)
        return d * b * (s * (1.0 + a * (1.0 - s))), d * a * s
    s = _sigmoid(b)
    return d * s, d * a * s * (1.0 - s)


def _interleave_rows(w, inverse=False):
    W2, C = w.shape
    bs = _gate_block(W2 // 2)
    nb = W2 // 2 // bs
    shape = (nb, 2, bs, C) if inverse else (2, nb, bs, C)
    return w.reshape(shape).transpose(1, 0, 2, 3).reshape(W2, C)


def _gated_bwd(z, dout, kind, *, name):
    T, W2 = z.shape
    W = W2 // 2
    tm, bs = _tile(T, 512), _gate_block(W)

    def body(z_ref, d_ref, o_ref):
        da, db = _gate_grads(z_ref[:, :bs].astype(F32), z_ref[:, bs:].astype(F32), d_ref[...].astype(F32), kind)
        o_ref[:, :bs] = da.astype(o_ref.dtype)
        o_ref[:, bs:] = db.astype(o_ref.dtype)

    return pl.pallas_call(
        body, name=name, grid=(T // tm, W // bs),
        in_specs=[pl.BlockSpec((tm, 2 * bs), lambda i, j: (i, j)), pl.BlockSpec((tm, bs), lambda i, j: (i, j))],
        out_specs=pl.BlockSpec((tm, 2 * bs), lambda i, j: (i, j)),
        out_shape=jax.ShapeDtypeStruct((T, W2), BF16), compiler_params=_cparams("parallel", "parallel"),
    )(z, dout)


_NT = (((1,), (1,)), ((), ()))
_TN = (((0,), (0,)), ((), ()))


def _dot(a, b, dims=None, precision=None):
    if dims is None:
        return jnp.dot(a, b, preferred_element_type=F32, precision=precision)
    return lax.dot_general(a, b, dims, preferred_element_type=F32, precision=precision)


def _xattn_fwd(q, kv, B, L, *, name):
    T = q.shape[0]
    tq = 256
    nq = L // tq
    scale = X_HD ** -0.5

    def body(q_ref, k_ref, v_ref, o_ref):
        heads = [slice(h * X_HD, (h + 1) * X_HD) for h in range(X_HEADS)]
        s = [_dot(q_ref[:, sl].astype(BF16), k_ref[:, sl].astype(BF16), _NT) * scale for sl in heads]
        m = [jnp.max(a, axis=-1, keepdims=True) for a in s]
        p = [jnp.exp(a - b) for a, b in zip(s, m)]
        l = [jnp.sum(a, axis=-1, keepdims=True) for a in p]
        o = [_dot(a.astype(BF16), v_ref[:, sl].astype(BF16)) for a, sl in zip(p, heads)]
        for sl, a, b in zip(heads, o, l):
            o_ref[:, sl] = (a / b).astype(BF16)

    return pl.pallas_call(
        body, name=name, grid=(B, nq),
        in_specs=[pl.BlockSpec((tq, D_MODEL), lambda b, i: (b * nq + i, 0)),
                  pl.BlockSpec((MEM_LEN, D_MODEL), lambda b, i: (b, 0)),
                  pl.BlockSpec((MEM_LEN, D_MODEL), lambda b, i: (b, 1))],
        out_specs=pl.BlockSpec((tq, D_MODEL), lambda b, i: (b * nq + i, 0)),
        out_shape=jax.ShapeDtypeStruct((T, D_MODEL), BF16), compiler_params=_cparams("parallel", "parallel"),
    )(q, kv, kv)


def _xattn_bwd(q, kv, do, B, L, *, name):
    T = q.shape[0]
    tq = 256
    nq = L // tq
    scale = X_HD ** -0.5

    def body(q_ref, k_ref, v_ref, do_ref, dq_ref, dkv_ref):
        @pl.when(pl.program_id(1) == 0)
        def _():
            dkv_ref[...] = jnp.zeros_like(dkv_ref)

        heads = [slice(h * X_HD, (h + 1) * X_HD) for h in range(X_HEADS)]
        qs = [q_ref[:, sl].astype(BF16) for sl in heads]
        ks = [k_ref[:, sl].astype(BF16) for sl in heads]
        dos = [do_ref[:, sl].astype(BF16) for sl in heads]
        s = [_dot(a, b, _NT) * scale for a, b in zip(qs, ks)]
        dp = [_dot(a, v_ref[:, sl].astype(BF16), _NT) for a, sl in zip(dos, heads)]
        e = [jnp.exp(a - jnp.max(a, axis=-1, keepdims=True)) for a in s]
        p = [a / jnp.sum(a, axis=-1, keepdims=True) for a in e]
        ds = [(a * (b - jnp.sum(b * a, axis=-1, keepdims=True)) * scale).astype(BF16) for a, b in zip(p, dp)]
        dv = [_dot(a.astype(BF16), b, _TN) for a, b in zip(p, dos)]
        dq = [_dot(a, b) for a, b in zip(ds, ks)]
        dk = [_dot(a, b, _TN) for a, b in zip(ds, qs)]
        for h, sl in enumerate(heads):
            dq_ref[:, sl] = dq[h].astype(BF16)
            dkv_ref[:, sl] += dk[h]
            dkv_ref[:, D_MODEL + h * X_HD:D_MODEL + (h + 1) * X_HD] += dv[h]

    return pl.pallas_call(
        body, name=name, grid=(B, nq),
        in_specs=[pl.BlockSpec((tq, D_MODEL), lambda b, i: (b * nq + i, 0)),
                  pl.BlockSpec((MEM_LEN, D_MODEL), lambda b, i: (b, 0)),
                  pl.BlockSpec((MEM_LEN, D_MODEL), lambda b, i: (b, 1)),
                  pl.BlockSpec((tq, D_MODEL), lambda b, i: (b * nq + i, 0))],
        out_specs=[pl.BlockSpec((tq, D_MODEL), lambda b, i: (b * nq + i, 0)),
                   pl.BlockSpec((MEM_LEN, 2 * D_MODEL), lambda b, i: (b, 0))],
        out_shape=[jax.ShapeDtypeStruct((T, D_MODEL), BF16), jax.ShapeDtypeStruct((B * MEM_LEN, 2 * D_MODEL), F32)],
        compiler_params=_cparams("parallel", "arbitrary"),
    )(q, kv, kv, do)


def _chunk_masks():
    row = lax.broadcasted_iota(jnp.int32, (A_SUPER, A_SUPER), 0)
    col = lax.broadcasted_iota(jnp.int32, (A_SUPER, A_SUPER), 1)
    same = jnp.right_shift(row, 5) == jnp.right_shift(col, 5)
    return same, same & (col <= row), same & (col >= row)


def _dot_mask(mask, x):
    m = mask.astype(BF16)
    hi = x.astype(BF16)
    rest = x - hi.astype(F32)
    mid = rest.astype(BF16)
    lo = (rest - mid.astype(F32)).astype(BF16)
    return _dot(m, hi) + _dot(m, mid) + _dot(m, lo)


def _chunk_row(x, which):
    rows = [x[c * A_CHUNK + which % A_CHUNK:c * A_CHUNK + which % A_CHUNK + 1, :] for c in range(A_SUPER // A_CHUNK)]
    return jnp.concatenate([jnp.broadcast_to(r, (A_CHUNK, x.shape[1])) for r in rows], axis=0)


def _hgrn_gates(fa, lb):
    sig = _sigmoid(fa)
    f = lb + (1.0 - lb) * sig
    return sig, f, jnp.log(f), 1.0 - f


def _hgrn_fwd(z, lb, onw, B, L, *, name, gather=()):
    T = B * L
    ns = L // A_SUPER
    nch = A_SUPER // A_CHUNK
    ex = _Exchange(gather=gather)
    grid = (B, 4, ns)

    def body(q_ref, f_ref, v_ref, g_ref, lb_ref, w_ref, oa_ref, o_ref, s_ref, st_ref, sc_ref):
        @pl.when(pl.program_id(2) == 0)
        def _():
            st_ref[...] = jnp.zeros_like(st_ref)

        s_ref[0] = st_ref[...]
        same, tril, _ = _chunk_masks()
        q, v = q_ref[...], v_ref[...]
        _, _, lf, k = _hgrn_gates(f_ref[...], lb_ref[...])
        bcs = _dot_mask(tril, lf)
        bl = _chunk_row(bcs, -1)
        qd = (q * jnp.exp(bcs)).astype(BF16)
        ki = (k * jnp.exp(-bcs)).astype(BF16)
        ke = (k * jnp.exp(bl - bcs)).astype(BF16)
        dec = jnp.exp(bl)
        vb = v.astype(BF16)
        a = jnp.where(tril, _dot(qd, ki, _NT), 0.0)
        o_ref[...] = _dot(a.astype(BF16), vb)
        chunks = [slice(c * A_CHUNK, (c + 1) * A_CHUNK) for c in range(nch)]
        outer = [_dot(vb[rs], ke[rs], _TN) for rs in chunks]
        st = st_ref[...]
        for c, rs in enumerate(chunks):
            sc_ref[c] = st.astype(BF16)
            st = st * dec[c * A_CHUNK:c * A_CHUNK + 1, :] + outer[c]
        st_ref[...] = st
        for c, rs in enumerate(chunks):
            o_ref[rs, :] += _dot(qd[rs], sc_ref[c], _NT)
        o = o_ref[...]
        r = lax.rsqrt(jnp.mean(o * o, axis=-1, keepdims=True) + NORM_EPS)
        g = g_ref[...]
        oa_ref[...] = (o * r * w_ref[...] * (g * _sigmoid(g))).astype(BF16)

    def zspec(off):
        return pl.BlockSpec((A_SUPER, A_HEAD), lambda b, h, n: (b * ns + n, off + h))

    hvec = pl.BlockSpec((1, A_HEAD), lambda b, h, n: (0, h))
    ospec = pl.BlockSpec((A_SUPER, A_HEAD), lambda b, h, n: (b * ns + n, h))
    res = pl.pallas_call(
        _carrying(body, 6, 3, ex, grid) if ex.n else body, name=name, grid=grid,
        in_specs=[zspec(0), zspec(4), zspec(8), zspec(12), hvec, hvec] + ex.in_specs(),
        out_specs=[ospec, ospec, pl.BlockSpec((1, A_HEAD, A_HEAD), lambda b, h, n: ((b * 4 + h) * ns + n, 0, 0))] + ex.in_specs(),
        out_shape=[jax.ShapeDtypeStruct((T, A_WIDTH), BF16), jax.ShapeDtypeStruct((T, A_WIDTH), F32),
                   jax.ShapeDtypeStruct((B * 4 * ns, A_HEAD, A_HEAD), F32)] + ex.out_shapes(),
        scratch_shapes=[pltpu.VMEM((A_HEAD, A_HEAD), F32), pltpu.VMEM((nch, A_HEAD, A_HEAD), BF16)] + ex.scratch(),
        compiler_params=_cparams(*(("arbitrary",) * 3 if ex.n else ("parallel", "parallel", "arbitrary"))),
    )(z, z, z, z, lb, onw, *ex.operands())
    return tuple(res[:3]) + (list(res[3:]),)


def _hgrn_bwd(z, lb, onw, o_raw, s_start, doa, B, L, *, name, scatter=()):
    T = B * L
    ns = L // A_SUPER
    nch = A_SUPER // A_CHUNK
    ex = _Exchange(scatter=scatter)
    grid = (B, 4, ns)

    def body(q_ref, f_ref, v_ref, g_ref, lb_ref, w_ref, o_ref, s_ref, doa_ref,
             dq_ref, df_ref, dv_ref, dg_ref, dw_ref, dlb_ref, dst_ref, sc_ref, dsc_ref, dqd_ref, dke_ref, dblx_ref, dvacc_ref):
        @pl.when(pl.program_id(2) == 0)
        def _():
            dst_ref[...] = jnp.zeros_like(dst_ref)
            dw_ref[...] = jnp.zeros_like(dw_ref)
            dlb_ref[...] = jnp.zeros_like(dlb_ref)

        same, tril, triu = _chunk_masks()
        q, v, g, lb, w = q_ref[...], v_ref[...], g_ref[...], lb_ref[...], w_ref[...]
        sig, f, lf, k = _hgrn_gates(f_ref[...], lb)
        bcs = _dot_mask(tril, lf)
        bl = _chunk_row(bcs, -1)
        eb, enb, eeb = jnp.exp(bcs), jnp.exp(-bcs), jnp.exp(bl - bcs)
        qd, ki, ke = q * eb, k * enb, k * eeb
        qdb, kib, keb, vb = qd.astype(BF16), ki.astype(BF16), ke.astype(BF16), v.astype(BF16)
        dec = jnp.exp(bl)
        o = o_ref[...]
        r = lax.rsqrt(jnp.mean(o * o, axis=-1, keepdims=True) + NORM_EPS)
        on = o * r
        sg = _sigmoid(g)
        silu_g = g * sg
        doa = doa_ref[...]
        dg_ref[...] = (doa * on * w * (sg * (1.0 + g * (1.0 - sg)))).astype(BF16)
        dw_ref[0] += jnp.sum(doa * on * silu_g, axis=0, keepdims=True)
        don = doa * w * silu_g
        do = r * (don - on * jnp.mean(don * on, axis=-1, keepdims=True))
        dob = do.astype(BF16)
        a = jnp.where(tril, _dot(qdb, kib, _NT), 0.0).astype(BF16)
        da = jnp.where(tril, _dot(dob, vb, _NT), 0.0).astype(BF16)
        dvacc_ref[...] = _dot(a, dob, _TN)
        dqd_ref[...] = _dot(da, kib)
        dki = _dot(da, qdb, _TN)
        chunks = [slice(c * A_CHUNK, (c + 1) * A_CHUNK) for c in range(nch)]
        outer = [_dot(vb[rs], keb[rs], _TN) for rs in chunks]
        st = s_ref[0]
        for c in range(nch):
            sc_ref[c] = st
            st = st * dec[c * A_CHUNK:c * A_CHUNK + 1, :] + outer[c]
        outer_g = [_dot(dob[rs], qdb[rs], _TN) for rs in chunks]
        dst = dst_ref[...]
        for c in reversed(range(nch)):
            dsc_ref[c] = dst
            dst = dst * dec[c * A_CHUNK:c * A_CHUNK + 1, :] + outer_g[c]
        dst_ref[...] = dst
        for c, rs in enumerate(chunks):
            dec_c = dec[c * A_CHUNK:c * A_CHUNK + 1, :]
            dsc, stc = dsc_ref[c], sc_ref[c]
            dscb = dsc.astype(BF16)
            dvacc_ref[rs, :] += _dot(keb[rs], dscb, _NT)
            dke_ref[rs, :] = _dot(vb[rs], dscb)
            ddec = jnp.sum(dsc * stc, axis=0, keepdims=True)
            dqd_ref[rs, :] += _dot(dob[rs], stc.astype(BF16))
            dblx_ref[rs, :] = jnp.broadcast_to(ddec * dec_c, (A_CHUNK, A_HEAD))
        dqd, dke = dqd_ref[...], dke_ref[...]
        dv_ref[...] = dvacc_ref[...].astype(BF16)
        dq_ref[...] = (dqd * eb).astype(BF16)
        keke = dke * ke
        db = dqd * qd - dki * ki - keke
        sums = _dot_mask(triu, jnp.concatenate([db, keke], axis=1))
        dk = dki * enb + dke * eeb
        dlf = sums[:, :A_HEAD] + _chunk_row(sums[:, A_HEAD:], 0) + dblx_ref[...]
        dff = dlf / f - dk
        df_ref[...] = (dff * (1.0 - lb) * sig * (1.0 - sig)).astype(BF16)
        dlb_ref[0] += jnp.sum(dff * (1.0 - sig), axis=0, keepdims=True)

    def rev(n):
        return ns - 1 - n

    def zspec(off):
        return pl.BlockSpec((A_SUPER, A_HEAD), lambda b, h, n: (b * ns + rev(n), off + h))

    hvec = pl.BlockSpec((1, A_HEAD), lambda b, h, n: (0, h))
    ospec = pl.BlockSpec((A_SUPER, A_HEAD), lambda b, h, n: (b * ns + rev(n), h))
    acc = pl.BlockSpec((1, 1, A_HEAD), lambda b, h, n: (b * 4 + h, 0, 0))
    big = jax.ShapeDtypeStruct((T, A_WIDTH), BF16)
    small = jax.ShapeDtypeStruct((B * 4, 1, A_HEAD), F32)
    res = pl.pallas_call(
        _carrying(body, 9, 6, ex, grid) if ex.n else body, name=name, grid=grid,
        in_specs=[zspec(0), zspec(4), zspec(8), zspec(12), hvec, hvec, ospec,
                  pl.BlockSpec((1, A_HEAD, A_HEAD), lambda b, h, n: ((b * 4 + h) * ns + rev(n), 0, 0)), ospec] + ex.in_specs(),
        out_specs=[ospec, ospec, ospec, ospec, acc, acc] + ex.in_specs(),
        out_shape=[big, big, big, big, small, small] + ex.out_shapes(),
        scratch_shapes=[pltpu.VMEM((A_HEAD, A_HEAD), F32), pltpu.VMEM((nch, A_HEAD, A_HEAD), F32),
                        pltpu.VMEM((nch, A_HEAD, A_HEAD), F32),
                        pltpu.VMEM((A_SUPER, A_HEAD), F32), pltpu.VMEM((A_SUPER, A_HEAD), F32),
                        pltpu.VMEM((A_SUPER, A_HEAD), F32), pltpu.VMEM((A_SUPER, A_HEAD), F32)] + ex.scratch(),
        compiler_params=_cparams(*(("arbitrary",) * 3 if ex.n else ("parallel", "parallel", "arbitrary"))),
    )(z, z, z, z, lb, onw, o_raw, s_start, doa, *ex.operands())
    return tuple(res[:6]) + (list(res[6:]),)


def _rope_tables(L):
    half = A_HEAD // 2
    inv_freq = ROPE_THETA ** (-jnp.arange(half, dtype=F32) / half)
    ang = jnp.arange(L, dtype=F32)[:, None] * inv_freq[None, :]
    cos, sin = jnp.cos(ang), jnp.sin(ang)
    return jnp.concatenate([cos, cos], axis=-1), jnp.concatenate([-sin, sin], axis=-1)


def _rope_fwd(z, cos2, sin2, B, L, *, name):
    T = B * L
    tm = 512
    nl = L // tm

    def body(x_ref, c_ref, s_ref, q_ref, k_ref):
        c, s = c_ref[...], s_ref[...]
        for h in range(8):
            x = x_ref[:, h * A_HEAD:(h + 1) * A_HEAD]
            out = x * c + pltpu.roll(x, A_HEAD // 2, 1) * s
            o_ref = q_ref if h < 4 else k_ref
            o_ref[:, (h % 4) * A_HEAD:(h % 4 + 1) * A_HEAD] = out

    tab = pl.BlockSpec((tm, A_HEAD), lambda i: (i % nl, 0))
    out = pl.BlockSpec((tm, 512), lambda i: (i, 0))
    return pl.pallas_call(
        body, name=name, grid=(T // tm,),
        in_specs=[pl.BlockSpec((tm, 1024), lambda i: (i, 2)), tab, tab], out_specs=[out, out],
        out_shape=[jax.ShapeDtypeStruct((T, 512), F32)] * 2, compiler_params=_cparams("parallel"),
    )(z, cos2, sin2)


def _rope_bwd(dqs, dks, dvs, cos2, sin2, B, L, *, name):
    T = B * L
    tm = 512
    nl = L // tm

    def body(*refs):
        c, s = refs[9][...], refs[10][...]
        o_ref = refs[11]
        for part in range(3):
            a_ref, b_ref, c_ref = refs[3 * part:3 * part + 3]
            for h in range(4):
                cols = slice(h * A_HEAD, (h + 1) * A_HEAD)
                d = a_ref[:, cols] + b_ref[:, cols] + c_ref[:, cols]
                if part < 2:
                    d = d * c - pltpu.roll(d, A_HEAD // 2, 1) * s
                o_ref[:, part * 512 + h * A_HEAD:part * 512 + (h + 1) * A_HEAD] = d.astype(BF16)

    blk = pl.BlockSpec((tm, 512), lambda i: (i, 0))
    tab = pl.BlockSpec((tm, A_HEAD), lambda i: (i % nl, 0))
    return pl.pallas_call(
        body, name=name, grid=(T // tm,), in_specs=[blk] * 9 + [tab, tab],
        out_specs=pl.BlockSpec((tm, 1536), lambda i: (i, 0)),
        out_shape=jax.ShapeDtypeStruct((T, 1536), BF16), compiler_params=_cparams("parallel"),
    )(*dqs, *dks, *dvs, cos2, sin2)


def _band_masks():
    i = lax.broadcasted_iota(jnp.int32, (B_SPAN, B_SPAN), 0)
    j = lax.broadcasted_iota(jnp.int32, (B_SPAN, B_SPAN), 1)
    return i <= j, j <= i


class _DilPlan:
    def __init__(self, dil, B, L):
        self.dil, self.B, self.L = dil, B, L
        self.rows = 4 * B_SPAN if dil == 1 else B_SPAN * dil
        self.n = L // self.rows
        self.hr = 4 if dil == 4 else 1
        self.cw = 512 if dil == 1 else A_HEAD
        self.has_other = dil != 16
        self.grid = (B, self.n, 4 if dil == 16 else 1)
        if dil == 1:
            self.items = [(j, 0, h) for j in range(4) for h in range(4)]
        elif dil == 4:
            self.items = [(0, r, h) for r in range(4) for h in range(4)]
        else:
            self.items = [(0, r, 0) for r in range(16)]
        self.groups = [self.items[i:i + 4] for i in range(0, 16, 4)]

    def operands(self, arr):
        return [arr] * self.hr

    def specs(self, col0, role="cur"):
        n, nb128 = self.n, self.L // B_SPAN
        out = []
        for h in range(self.hr):
            cb = col0 // self.cw + h
            if role == "cur":
                out.append(pl.BlockSpec((self.rows, self.cw), lambda b, i, hh, cb=cb: (b * n + i, cb + hh)))
            elif self.dil == 1:
                shift = -1 if role == "prev" else 4
                out.append(pl.BlockSpec((B_SPAN, self.cw),
                                        lambda b, i, hh, cb=cb, shift=shift: (b * nb128 + jnp.clip(4 * i + shift, 0, nb128 - 1), cb)))
            else:
                shift = -1 if role == "prev" else 1
                out.append(pl.BlockSpec((self.rows, self.cw),
                                        lambda b, i, hh, cb=cb, shift=shift: (b * n + jnp.clip(i + shift, 0, n - 1), cb)))
        return out

    def out_spec(self):
        n = self.n
        if self.dil == 4:
            return pl.BlockSpec((self.rows, 512), lambda b, i, hh: (b * n + i, 0))
        return pl.BlockSpec((self.rows, self.cw), lambda b, i, hh: (b * n + i, hh))

    def scratch(self, n_out):
        return [pltpu.VMEM((4, self.rows, A_HEAD), F32)] * n_out if self.dil == 4 else []

    def store(self, out_ref, scr, item, val):
        j, r, h = item
        if self.dil == 1:
            out_ref[pl.ds(j * B_SPAN, B_SPAN), pl.ds(h * A_HEAD, A_HEAD)] = val
        elif self.dil == 4:
            scr.at[h][pl.ds(r, B_SPAN, stride=4), :] = val
        else:
            out_ref[pl.ds(r, B_SPAN, stride=self.dil), :] = val

    def flush(self, out_ref, scr):
        if self.dil == 4:
            for h in range(4):
                out_ref[:, h * A_HEAD:(h + 1) * A_HEAD] = scr[h]

    def cur(self, refs, item):
        j, r, h = item
        if self.dil == 1:
            return refs[0], (pl.ds(j * B_SPAN, B_SPAN), pl.ds(h * A_HEAD, A_HEAD))
        return refs[h], (pl.ds(r, B_SPAN, stride=self.dil), slice(None))

    def other(self, refs, other_refs, item, role):
        j, r, h = item
        if self.dil == 1:
            cols = pl.ds(h * A_HEAD, A_HEAD)
            jj = j - 1 if role == "prev" else j + 1
            if 0 <= jj < 4:
                return refs[0], (pl.ds(jj * B_SPAN, B_SPAN), cols)
            return other_refs[0], (pl.ds(0, B_SPAN), cols)
        return other_refs[h], (pl.ds(r, B_SPAN, stride=self.dil), slice(None))

    def other_valid(self, item, role):
        j = item[0]
        i = pl.program_id(1)
        if role == "prev":
            return True if (self.dil == 1 and j > 0) else i > 0
        return True if (self.dil == 1 and j < 3) else i < self.n - 1


def _ld(pair):
    ref, idx = pair
    return ref[idx]


def _dil_fwd(qr, kr, z, dil, B, L, *, name):
    T = B * L
    plan = _DilPlan(dil, B, L)
    hr, has_prev = plan.hr, plan.has_other
    scale = A_HEAD ** -0.5

    n_t = 5 if has_prev else 3

    def body(*refs):
        lists = [refs[i * hr:(i + 1) * hr] for i in range(n_t)]
        o_ref, l_ref = refs[n_t * hr], refs[n_t * hr + 1]
        scr = refs[n_t * hr + 2:] or (None, None)
        if has_prev:
            q_r, kc_r, vc_r, kp_r, vp_r = lists
        else:
            q_r, kc_r, vc_r = lists
        mp0, mc = _band_masks()

        for group in plan.groups:
            qs = [_ld(plan.cur(q_r, i)).astype(BF16) for i in group]
            sc = [jnp.where(mc, _dot(q, _ld(plan.cur(kc_r, i)).astype(BF16), _NT) * scale, NEG_BIG) for q, i in zip(qs, group)]
            m = [jnp.max(s, axis=-1, keepdims=True) for s in sc]
            if has_prev:
                mps = [mp0 & plan.other_valid(i, "prev") for i in group]
                sp = [jnp.where(mk, _dot(q, _ld(plan.other(kc_r, kp_r, i, "prev")).astype(BF16), _NT) * scale, NEG_BIG)
                      for q, i, mk in zip(qs, group, mps)]
                m = [jnp.maximum(a, jnp.max(s, axis=-1, keepdims=True)) for a, s in zip(m, sp)]
            pc = [jnp.exp(s - a) for s, a in zip(sc, m)]
            l = [jnp.sum(p, axis=-1, keepdims=True) for p in pc]
            o = [_dot(p.astype(BF16), _ld(plan.cur(vc_r, i)).astype(BF16)) for p, i in zip(pc, group)]
            if has_prev:
                pp = [jnp.exp(s - a) for s, a in zip(sp, m)]
                l = [a + jnp.sum(p, axis=-1, keepdims=True) for a, p in zip(l, pp)]
                o = [a + _dot(p.astype(BF16), _ld(plan.other(vc_r, vp_r, i, "prev")).astype(BF16)) for a, p, i in zip(o, pp, group)]
            for i, oi, li, mi in zip(group, o, l, m):
                plan.store(o_ref, scr[0], i, oi / li)
                plan.store(l_ref, scr[1], i, jnp.broadcast_to(mi + jnp.log(li), (B_SPAN, A_HEAD)))
        plan.flush(o_ref, scr[0])
        plan.flush(l_ref, scr[1])

    tensors = [(qr, 0, "cur"), (kr, 0, "cur"), (z, 3072, "cur")] + ([(kr, 0, "prev"), (z, 3072, "prev")] if has_prev else [])
    return pl.pallas_call(
        body, name=name, grid=plan.grid,
        in_specs=[sp for _, c, role in tensors for sp in plan.specs(c, role)],
        out_specs=[plan.out_spec()] * 2, out_shape=[jax.ShapeDtypeStruct((T, 512), F32)] * 2,
        scratch_shapes=plan.scratch(2), compiler_params=_cparams("parallel", "parallel", "parallel"),
    )(*[a for arr, _, _ in tensors for a in plan.operands(arr)])


def _dil_combine(os_, ls_, *, name):
    T = os_[0].shape[0]
    tm = 512

    def body(o1, o2, o3, l1, l2, l3, ob_ref, lse_ref):
        a1, a2, a3 = l1[...], l2[...], l3[...]
        m = jnp.maximum(jnp.maximum(a1, a2), a3)
        e1, e2, e3 = jnp.exp(a1 - m), jnp.exp(a2 - m), jnp.exp(a3 - m)
        den = e1 + e2 + e3
        ob_ref[...] = (e1 * o1[...] + e2 * o2[...] + e3 * o3[...]) / den
        lse_ref[...] = m + jnp.log(den)

    blk = pl.BlockSpec((tm, 512), lambda i: (i, 0))
    return pl.pallas_call(
        body, name=name, grid=(T // tm,), in_specs=[blk] * 6, out_specs=[blk, blk],
        out_shape=[jax.ShapeDtypeStruct((T, 512), F32)] * 2, compiler_params=_cparams("parallel"),
    )(*[o.reshape(T, 512) for o in os_], *[l.reshape(T, 512) for l in ls_])


def _dil_bwd_q(qr, kr, z, dymix, out, lse, dil, B, L, *, name):
    T = B * L
    plan = _DilPlan(dil, B, L)
    hr, has_prev = plan.hr, plan.has_other
    scale = A_HEAD ** -0.5
    n_t = 8 if has_prev else 6

    def body(*refs):
        lists = [refs[i * hr:(i + 1) * hr] for i in range(n_t)]
        dq_ref = refs[n_t * hr]
        scr = refs[n_t * hr + 1:] or (None,)
        q_r, kc_r, vc_r, do_r, out_r, lse_r = lists[:6]
        mp0, mc = _band_masks()

        for group in plan.groups:
            qs = [_ld(plan.cur(q_r, i)).astype(BF16) for i in group]
            dos = [_ld(plan.cur(do_r, i)) for i in group]
            delta = [jnp.sum(d * _ld(plan.cur(out_r, i)), axis=-1, keepdims=True) for d, i in zip(dos, group)]
            dob = [d.astype(BF16) for d in dos]
            lse = [_ld(plan.cur(lse_r, i)) for i in group]
            kc = [_ld(plan.cur(kc_r, i)).astype(BF16) for i in group]
            pc = [jnp.where(mc, jnp.exp(_dot(q, k, _NT) * scale - a), 0.0) for q, k, a in zip(qs, kc, lse)]
            dsc = [p * (_dot(d, _ld(plan.cur(vc_r, i)).astype(BF16), _NT) - dl) * scale
                   for p, d, i, dl in zip(pc, dob, group, delta)]
            dq = [_dot(d.astype(BF16), k) for d, k in zip(dsc, kc)]
            if has_prev:
                kp_r, vp_r = lists[6], lists[7]
                mps = [mp0 & plan.other_valid(i, "prev") for i in group]
                kp = [_ld(plan.other(kc_r, kp_r, i, "prev")).astype(BF16) for i in group]
                pp = [jnp.where(mk, jnp.exp(_dot(q, k, _NT) * scale - a), 0.0) for q, k, a, mk in zip(qs, kp, lse, mps)]
                dsp = [p * (_dot(d, _ld(plan.other(vc_r, vp_r, i, "prev")).astype(BF16), _NT) - dl) * scale
                       for p, d, i, dl in zip(pp, dob, group, delta)]
                dq = [a + _dot(d.astype(BF16), k) for a, d, k in zip(dq, dsp, kp)]
            for i, d in zip(group, dq):
                plan.store(dq_ref, scr[0], i, d)
        plan.flush(dq_ref, scr[0])

    tensors = ([(qr, 0, "cur"), (kr, 0, "cur"), (z, 3072, "cur"), (dymix, 512, "cur"), (out, 0, "cur"), (lse, 0, "cur")]
               + ([(kr, 0, "prev"), (z, 3072, "prev")] if has_prev else []))
    return pl.pallas_call(
        body, name=name, grid=plan.grid,
        in_specs=[sp for _, c, role in tensors for sp in plan.specs(c, role)],
        out_specs=plan.out_spec(), out_shape=jax.ShapeDtypeStruct((T, 512), F32),
        scratch_shapes=plan.scratch(1), compiler_params=_cparams("parallel", "parallel", "parallel"),
    )(*[a for arr, _, _ in tensors for a in plan.operands(arr)])


def _dil_bwd_kv(qr, kr, z, dymix, out, lse, dil, B, L, *, name):
    T = B * L
    plan = _DilPlan(dil, B, L)
    hr, has_next = plan.hr, plan.has_other
    scale = A_HEAD ** -0.5
    n_t = 10 if has_next else 6

    def body(*refs):
        lists = [refs[i * hr:(i + 1) * hr] for i in range(n_t)]
        dk_ref, dv_ref = refs[n_t * hr], refs[n_t * hr + 1]
        scr = refs[n_t * hr + 2:] or (None, None)
        k_r, v_r = lists[0], lists[1]
        own = lists[2:6]
        mp0, mc = _band_masks()

        for group in plan.groups:
            kh = [_ld(plan.cur(k_r, i)).astype(BF16) for i in group]
            vh = [_ld(plan.cur(v_r, i)).astype(BF16) for i in group]
            dk, dv = [None] * len(group), [None] * len(group)
            for role in ("own", "next") if has_next else ("own",):
                if role == "own":
                    get = lambda t, i: _ld(plan.cur(own[t], i))
                    masks = [mc] * len(group)
                else:
                    get = lambda t, i: _ld(plan.other(own[t], lists[6 + t], i, "next"))
                    masks = [mp0 & plan.other_valid(i, "next") for i in group]
                qs = [get(0, i).astype(BF16) for i in group]
                dos = [get(1, i) for i in group]
                delta = [jnp.sum(d * get(2, i), axis=-1, keepdims=True) for d, i in zip(dos, group)]
                dob = [d.astype(BF16) for d in dos]
                p = [jnp.where(mk, jnp.exp(_dot(q, k, _NT) * scale - get(3, i)), 0.0) for q, k, i, mk in zip(qs, kh, group, masks)]
                dvn = [_dot(a.astype(BF16), d, _TN) for a, d in zip(p, dob)]
                ds = [a * (_dot(d, v, _NT) - dl) * scale for a, d, v, dl in zip(p, dob, vh, delta)]
                dkn = [_dot(d.astype(BF16), q, _TN) for d, q in zip(ds, qs)]
                dv = [n if o is None else o + n for o, n in zip(dv, dvn)]
                dk = [n if o is None else o + n for o, n in zip(dk, dkn)]
            for i, a, b in zip(group, dk, dv):
                plan.store(dk_ref, scr[0], i, a)
                plan.store(dv_ref, scr[1], i, b)
        plan.flush(dk_ref, scr[0])
        plan.flush(dv_ref, scr[1])

    queries = [(qr, 0), (dymix, 512), (out, 0), (lse, 0)]
    tensors = ([(kr, 0, "cur"), (z, 3072, "cur")] + [(a, c, "cur") for a, c in queries]
               + ([(a, c, "next") for a, c in queries] if has_next else []))
    return pl.pallas_call(
        body, name=name, grid=plan.grid,
        in_specs=[sp for _, c, role in tensors for sp in plan.specs(c, role)],
        out_specs=[plan.out_spec()] * 2, out_shape=[jax.ShapeDtypeStruct((T, 512), F32)] * 2,
        scratch_shapes=plan.scratch(2), compiler_params=_cparams("parallel", "parallel", "parallel"),
    )(*[a for arr, _, _ in tensors for a in plan.operands(arr)])


def _s5_build(lam_re, lam_im, log_dt, b_re, b_im, c_re, c_im):
    G, P, TC = C_GROUPS, C_STATE, C_TC
    lr = jnp.minimum(lam_re, C_MIN_NEG_RE)
    li = lam_im
    dt = jnp.exp(log_dt)[:, None]
    mag = jnp.exp(dt * lr)
    ar, ai = mag * jnp.cos(dt * li), mag * jnp.sin(dt * li)
    den = lr * lr + li * li
    zr = ((ar - 1.0) * lr + ai * li) / den
    zi = (ai * lr - (ar - 1.0) * li) / den
    bbr = zr[..., None] * b_re - zi[..., None] * b_im
    bbi = zr[..., None] * b_im + zi[..., None] * b_re
    ks = jnp.arange(TC + 1, dtype=F32)[:, None, None]
    pmag = jnp.exp(ks * (dt * lr)[None])
    pr, pi = pmag * jnp.cos(ks * (dt * li)[None]), pmag * jnp.sin(ks * (dt * li)[None])
    car = c_re[None] * pr[:, :, None, :] - c_im[None] * pi[:, :, None, :]
    cai = c_re[None] * pi[:, :, None, :] + c_im[None] * pr[:, :, None, :]
    kern = (jnp.einsum('lgop,gpc->lgco', car[:TC], bbr, precision=HI)
            - jnp.einsum('lgop,gpc->lgco', cai[:TC], bbi, precision=HI))
    pr_e, pi_e = pr[TC - 1 - jnp.arange(TC)], pi[TC - 1 - jnp.arange(TC)]
    er = pr_e[:, :, :, None] * bbr[None] - pi_e[:, :, :, None] * bbi[None]
    ei = pr_e[:, :, :, None] * bbi[None] + pi_e[:, :, :, None] * bbr[None]
    ez = jnp.stack([er, ei], axis=2).reshape(TC, C_NB, C_GB, 2, P, C_GROUP).transpose(1, 0, 2, 5, 3, 4)
    fz = jnp.stack([car[1:], -cai[1:]], axis=0).reshape(2, TC, C_NB, C_GB, C_GROUP, P).transpose(2, 0, 3, 5, 1, 4)
    return kern, ez, fz, pr[TC], pi[TC]


def _s5_lag_blocks(kern):
    eye = jnp.eye(C_GB, dtype=kern.dtype)
    return (kern.reshape(C_TC, C_NB, C_GB, C_GROUP, C_GROUP)[:, :, :, :, None, :]
            * eye[None, None, :, None, :, None]).reshape(C_TC, C_NB, LANES, LANES)


def _s5_dense(kbd, ez, fz, *, name):
    ezc = ez.reshape(C_NB, C_W8, LANES)
    fzc = fz.reshape(C_NB, C_S8, LANES)
    half = C_GB * C_STATE

    def body(k_ref, e_ref, f_ref, m8_ref, e8_ref, f8_ref):
        zero = jnp.zeros((LANES, LANES), BF16)
        for s in range(C_TC):
            for t in range(C_TC):
                m8_ref[0, s * LANES:(s + 1) * LANES, t * LANES:(t + 1) * LANES] = (
                    k_ref[t - s, 0].astype(BF16) if t >= s else zero)
        lane = lax.broadcasted_iota(jnp.int32, (1, LANES), 1)
        src = e_ref[0]
        swapped = pltpu.roll(src, C_STATE, 1)
        rowg = jnp.bitwise_and(jnp.right_shift(lax.broadcasted_iota(jnp.int32, (C_W8, LANES), 0), 4), C_GB - 1)
        for kb in range(C_S8 // LANES):
            z, g0 = kb // (C_GB // 2), 2 * (kb % (C_GB // 2))
            first = jnp.where(lane < C_STATE, src if z == 0 else swapped, 0.0)
            second = jnp.where(lane >= C_STATE, swapped if z == 0 else src, 0.0)
            e8_ref[0, :, kb * LANES:(kb + 1) * LANES] = jnp.where(
                rowg == g0, first, jnp.where(rowg == g0 + 1, second, 0.0)).astype(BF16)
        for z in range(2):
            for g in range(C_GB):
                rows = slice(z * half + g * C_STATE, z * half + (g + 1) * C_STATE)
                piece = f_ref[0, rows, :]
                mine = (lane >= g * C_GROUP) & (lane < (g + 1) * C_GROUP)
                for t in range(C_TC):
                    f8_ref[0, rows, t * LANES:(t + 1) * LANES] = jnp.where(
                        mine, pltpu.roll(piece, ((g - t) * C_GROUP) % LANES, 1), 0.0).astype(BF16)

    blk = lambda r, c: pl.BlockSpec((1, r, c), lambda b: (b, 0, 0))
    return pl.pallas_call(
        body, name=name, grid=(C_NB,),
        in_specs=[pl.BlockSpec((C_TC, 1, LANES, LANES), lambda b: (0, b, 0, 0)), blk(C_W8, LANES), blk(C_S8, LANES)],
        out_specs=[blk(C_W8, C_W8), blk(C_W8, C_S8), blk(C_S8, C_W8)],
        out_shape=[jax.ShapeDtypeStruct((C_NB, C_W8, C_W8), BF16), jax.ShapeDtypeStruct((C_NB, C_W8, C_S8), BF16),
                   jax.ShapeDtypeStruct((C_NB, C_S8, C_W8), BF16)],
        compiler_params=_cparams("parallel"),
    )(kbd, ezc, fzc)


C_NB = C_GROUPS * C_GROUP // LANES
C_GB = C_GROUPS // C_NB
C_W8 = C_TC * LANES
C_S8 = 2 * C_GB * C_STATE


def _s5_scan_tables(lam_re, lam_im, log_dt, nsteps):
    lr = jnp.minimum(lam_re, C_MIN_NEG_RE)
    dt = jnp.exp(log_dt)[:, None]
    ks = (C_TC * 2.0 ** jnp.arange(8, dtype=F32))[None, :, None]
    keep = (jnp.arange(8) < nsteps)[None, :, None]
    pmag = jnp.exp(ks * (dt * lr)[:, None, :])
    ang = ks * (dt * lam_im)[:, None, :]

    def blocks(t):
        return t.reshape(C_NB, C_GB, 8, C_STATE).transpose(0, 2, 1, 3).reshape(C_NB, 8, C_GB * C_STATE)

    pr = blocks(jnp.where(keep, pmag * jnp.cos(ang), 0.0))
    pi = blocks(jnp.where(keep, pmag * jnp.sin(ang), 0.0))
    return jnp.concatenate([pr, pr], axis=-1), jnp.concatenate([-pi, pi], axis=-1)


def _s5_rows(t, R):
    return pl.ds(t, R, stride=C_TC)


def _s5_fwd(u, dsk, m8, e8, f8, tab_r, tab_i, B, L, *, name, gather=()):
    T = B * L
    R = L // C_TC
    nsteps = int(math.log2(R))
    ex = _Exchange(gather=gather)
    grid = (C_NB, B)

    def body(u_ref, d_ref, m_ref, e_ref, f_ref, tr_ref, ti_ref, gl_ref, y_ref, x8_ref, xs_ref):
        for t in range(C_TC):
            x8_ref[0, :, t * LANES:(t + 1) * LANES] = u_ref[_s5_rows(t, R), :].astype(BF16)
        x8 = x8_ref[0]
        x = _dot(x8, e_ref[0])
        row = lax.broadcasted_iota(jnp.int32, (R, C_S8), 0)
        for k in range(nsteps):
            s = 1 << k
            sh = pltpu.roll(x, s, 0)
            upd = tr_ref[0, k:k + 1, :] * sh + ti_ref[0, k:k + 1, :] * pltpu.roll(sh, C_S8 // 2, 1)
            x = x + jnp.where(row >= s, upd, 0.0)
        xs = jnp.where(row >= 1, pltpu.roll(x, 1, 0), 0.0)
        xs_ref[0] = xs
        y8 = _dot(x8, m_ref[0]) + _dot(xs.astype(BF16), f_ref[0])
        d = d_ref[...]
        for t in range(C_TC):
            rows = _s5_rows(t, R)
            y = y8[:, t * LANES:(t + 1) * LANES] + d * u_ref[rows, :]
            y_ref[rows, :] = y
            gl_ref[rows, :] = 0.5 * y * (1.0 + _erf(y * (2.0 ** -0.5)))

    tok = pl.BlockSpec((L, LANES), lambda c, b: (b, c))
    per_block = lambda shape: pl.BlockSpec((1,) + shape, lambda c, b: (c, 0, 0))
    per_step = lambda shape: pl.BlockSpec((1,) + shape, lambda c, b: (c * B + b, 0, 0))
    res = pl.pallas_call(
        _carrying(body, 7, 4, ex, grid) if ex.n else body, name=name, grid=grid,
        in_specs=[tok, pl.BlockSpec((1, LANES), lambda c, b: (0, c)), per_block((C_W8, C_W8)), per_block((C_W8, C_S8)),
                  per_block((C_S8, C_W8)), per_block((8, C_S8)), per_block((8, C_S8))] + ex.in_specs(),
        out_specs=[tok, tok, per_step((R, C_W8)), per_step((R, C_S8))] + ex.in_specs(),
        out_shape=[jax.ShapeDtypeStruct((T, D_MODEL), F32), jax.ShapeDtypeStruct((T, D_MODEL), F32),
                   jax.ShapeDtypeStruct((C_NB * B, R, C_W8), BF16), jax.ShapeDtypeStruct((C_NB * B, R, C_S8), F32)] + ex.out_shapes(),
        scratch_shapes=ex.scratch(),
        compiler_params=_cparams(*(("arbitrary",) * 2 if ex.n else ("parallel", "parallel"))),
    )(u, dsk, m8, e8, f8, tab_r, tab_i, *ex.operands())
    return tuple(res[:4]) + (list(res[4:]),)


def _s5_bwd(dgl, y, u, dsk, xs, m8, e8, f8, tab_r, tab_i, B, L, *, name):
    T = B * L
    R = L // C_TC
    nsteps = int(math.log2(R))

    def body(dgl_ref, y_ref, u_ref, d_ref, xs_ref, m_ref, e_ref, f_ref, tr_ref, ti_ref,
             du_ref, dy8_ref, de_ref, da_ref, dd_ref, dyf_ref):
        @pl.when(pl.program_id(1) == 0)
        def _():
            da_ref[...] = jnp.zeros_like(da_ref)
            dd_ref[...] = jnp.zeros_like(dd_ref)

        dd = jnp.zeros((1, LANES), F32)
        for t in range(C_TC):
            rows = _s5_rows(t, R)
            yv = y_ref[rows, :]
            cdf = 0.5 * (1.0 + _erf(yv * (2.0 ** -0.5)))
            pdf = jnp.exp(-0.5 * yv * yv) * (1.0 / math.sqrt(2.0 * math.pi))
            dy = dgl_ref[rows, :] * (cdf + yv * pdf)
            dd = dd + jnp.sum(dy * u_ref[rows, :], axis=0, keepdims=True)
            dyf_ref[:, t * LANES:(t + 1) * LANES] = dy
        dd_ref[...] += dd
        dy8 = dyf_ref[...].astype(BF16)
        dy8_ref[0] = dy8
        xs = xs_ref[0]
        gx = _dot(dy8, f_ref[0], _NT)
        row = lax.broadcasted_iota(jnp.int32, (R, C_S8), 0)
        for k in range(nsteps):
            s = 1 << k
            sh = pltpu.roll(gx, R - s, 0)
            upd = tr_ref[0, k:k + 1, :] * sh - ti_ref[0, k:k + 1, :] * pltpu.roll(sh, C_S8 // 2, 1)
            gx = gx + jnp.where(row + s < R, upd, 0.0)
        de_in = jnp.where(row + 1 < R, pltpu.roll(gx, R - 1, 0), 0.0)
        deb = de_in.astype(BF16)
        de_ref[0] = deb
        da_ref[0, 0:1, :] += jnp.sum(de_in * xs, axis=0, keepdims=True)
        da_ref[0, 1:2, :] += jnp.sum(de_in * pltpu.roll(xs, C_S8 // 2, 1), axis=0, keepdims=True)
        dx8 = _dot(dy8, m_ref[0], _NT) + _dot(deb, e_ref[0], _NT)
        d = d_ref[...]
        for t in range(C_TC):
            cols = slice(t * LANES, (t + 1) * LANES)
            du_ref[_s5_rows(t, R), :] = dx8[:, cols] + d * dyf_ref[:, cols]

    tok = pl.BlockSpec((L, LANES), lambda c, b: (b, c))
    vec = pl.BlockSpec((1, LANES), lambda c, b: (0, c))
    per_block = lambda shape: pl.BlockSpec((1,) + shape, lambda c, b: (c, 0, 0))
    per_step = lambda shape: pl.BlockSpec((1,) + shape, lambda c, b: (c * B + b, 0, 0))
    return pl.pallas_call(
        body, name=name, grid=(C_NB, B),
        in_specs=[tok, tok, tok, vec, per_step((R, C_S8)), per_block((C_W8, C_W8)),
                  per_block((C_W8, C_S8)), per_block((C_S8, C_W8)), per_block((8, C_S8)), per_block((8, C_S8))],
        out_specs=[tok, per_step((R, C_W8)), per_step((R, C_S8)), per_block((8, C_S8)), vec],
        out_shape=[jax.ShapeDtypeStruct((T, D_MODEL), F32), jax.ShapeDtypeStruct((C_NB * B, R, C_W8), BF16),
                   jax.ShapeDtypeStruct((C_NB * B, R, C_S8), BF16), jax.ShapeDtypeStruct((C_NB, 8, C_S8), F32),
                   jax.ShapeDtypeStruct((1, D_MODEL), F32)],
        scratch_shapes=[pltpu.VMEM((R, C_W8), F32)],
        compiler_params=_cparams("parallel", "arbitrary"),
    )(dgl, y, u, dsk, xs, m8, e8, f8, tab_r, tab_i)


def _bmm_tn(a, b, nb, fold, *, name):
    a = a.reshape(nb, -1, a.shape[-1])
    b = b.reshape(nb, -1, b.shape[-1])
    K, M, N = a.shape[1], a.shape[2], b.shape[2]
    half = C_GB * C_STATE

    def body(a_ref, b_ref, o_ref, p_ref):
        p_ref[...] = _dot(a_ref[0].astype(BF16), b_ref[0].astype(BF16), _TN)
        lane = lax.broadcasted_iota(jnp.int32, (1, LANES), 1)
        if fold == "lags":
            for lag in range(C_TC):
                blocks = [p_ref[s * LANES:(s + 1) * LANES, (s + lag) * LANES:(s + lag + 1) * LANES] for s in range(C_TC - lag)]
                o_ref[0, lag] = functools.reduce(lambda u, v: u + v, blocks)
        elif fold == "e":
            for g in range(C_GB):
                lo, hi = LANES * (g // 2), half + LANES * (g // 2)
                for s in range(C_TC):
                    rows = slice(s * LANES + g * C_GROUP, s * LANES + (g + 1) * C_GROUP)
                    re, im = p_ref[rows, lo:lo + LANES], p_ref[rows, hi:hi + LANES]
                    if g % 2 == 0:
                        im = pltpu.roll(im, C_STATE, 1)
                    else:
                        re = pltpu.roll(re, C_STATE, 1)
                    o_ref[0, rows, :] = jnp.where(lane < C_STATE, re, im)
        else:
            for z in range(2):
                for g in range(C_GB):
                    rows = slice(z * half + g * C_STATE, z * half + (g + 1) * C_STATE)
                    val = jnp.zeros((C_STATE, LANES), F32)
                    for t in range(C_TC):
                        blk = pltpu.roll(p_ref[rows, t * LANES:(t + 1) * LANES], ((t - g) * C_GROUP) % LANES, 1)
                        val = jnp.where((lane >= t * C_GROUP) & (lane < (t + 1) * C_GROUP), blk, val)
                    o_ref[0, rows, :] = val

    if fold == "lags":
        out_spec = pl.BlockSpec((1, C_TC, LANES, LANES), lambda c: (c, 0, 0, 0))
        out_shape = jax.ShapeDtypeStruct((nb, C_TC, LANES, LANES), F32)
    else:
        out_spec = pl.BlockSpec((1, M, LANES), lambda c: (c, 0, 0))
        out_shape = jax.ShapeDtypeStruct((nb, M, LANES), F32)
    return pl.pallas_call(
        body, name=name, grid=(nb,),
        in_specs=[pl.BlockSpec((1, K, M), lambda c: (c, 0, 0)), pl.BlockSpec((1, K, N), lambda c: (c, 0, 0))],
        out_specs=out_spec, out_shape=out_shape, scratch_shapes=[pltpu.VMEM((M, N), F32)],
        compiler_params=_cparams("parallel"),
    )(a, b)


def _loss_head(y, target, *, name):
    T, C = y.shape
    tm = 512

    def body(y_ref, t_ref, l_ref, d_ref):
        err = y_ref[...] - t_ref[...]
        d_ref[...] = err * (1.0 / C)
        sq = err * err
        part = jnp.zeros((8, LANES), F32)
        for r in range(0, tm, 8):
            for c in range(0, C, LANES):
                part = part + sq[r:r + 8, c:c + LANES]

        @pl.when(pl.program_id(0) == 0)
        def _():
            l_ref[...] = part

        @pl.when(pl.program_id(0) > 0)
        def _():
            l_ref[...] += part

    row = pl.BlockSpec((tm, C), lambda i: (i, 0))
    acc = pl.BlockSpec((8, LANES), lambda i: (0, 0))
    return pl.pallas_call(
        body, name=name, grid=(T // tm,), in_specs=[row, row], out_specs=[acc, row],
        out_shape=[jax.ShapeDtypeStruct((8, LANES), F32), jax.ShapeDtypeStruct((T, C), F32)],
        compiler_params=_cparams("arbitrary"),
    )(y, target)


def _adamw(w, g, m, v, *, name):
    shape = w.shape
    size = int(np.prod(shape))
    cols = LANES if (shape[-1] < LANES and size % LANES == 0) else shape[-1]
    rows = size // cols
    tm = _tile(rows, 256) if rows % 8 == 0 else rows
    w2, g2, m2, v2 = (t.reshape(rows, cols) for t in (w, g, m, v))

    def body(w_ref, g_ref, m_ref, v_ref, d_ref, nm_ref, nv_ref):
        gg = g_ref[...]
        nm = ADAM_B1 * m_ref[...] + (1.0 - ADAM_B1) * gg
        nv = ADAM_B2 * v_ref[...] + (1.0 - ADAM_B2) * (gg * gg)
        m_hat = nm / (1.0 - ADAM_B1 ** ADAM_STEP)
        v_hat = nv / (1.0 - ADAM_B2 ** ADAM_STEP)
        d_ref[...] = -ADAM_LR * (m_hat / (jnp.sqrt(v_hat) + ADAM_EPS) + ADAM_WD * w_ref[...])
        nm_ref[...] = nm
        nv_ref[...] = nv

    blk = pl.BlockSpec((tm, cols), lambda i: (i, 0))
    outs = pl.pallas_call(
        body, name=name, grid=(rows // tm,), in_specs=[blk] * 4, out_specs=[blk] * 3,
        out_shape=[jax.ShapeDtypeStruct((rows, cols), F32)] * 3, compiler_params=_cparams("parallel"),
    )(w2, g2, m2, v2)
    return tuple(o.reshape(shape) for o in outs)


def _all_gather(shard, *, name):
    R, C = shard.shape

    def body(x_ref, out_ref, send_sems, recv_sems, local_sem):
        x, y, c = lax.axis_index("x"), lax.axis_index("y"), lax.axis_index("c")
        me, sibling = (x, y, c), (x, y, 1 - c)
        chips = [(1 - x, y), (x, 1 - y), (1 - x, 1 - y)]

        def rows(px, py, pc):
            return out_ref.at[_logical(px, py, pc)]

        def copy(k, block, to, src=None):
            return pltpu.make_async_remote_copy(
                src_ref=rows(*block) if src is None else src, dst_ref=rows(*block),
                send_sem=send_sems.at[k], recv_sem=recv_sems.at[k], device_id=to, device_id_type=_MESH)

        mine = pltpu.make_async_copy(x_ref, rows(*me), local_sem)
        mine.start()
        first = [copy(0, me, sibling, src=x_ref)]
        first += [copy(1 + j, me, (*chip, c), src=x_ref) for j, chip in enumerate(chips)]
        for cp in first:
            cp.start()
        passed = [copy(4 + j, (*chip, c), sibling) for j, chip in enumerate(chips)]
        for j, chip in enumerate(chips):
            copy(1 + j, (*chip, c), me).wait_recv()
            passed[j].start()
        copy(0, sibling, me).wait_recv()
        for j, chip in enumerate(chips):
            copy(4 + j, (*chip, 1 - c), me).wait_recv()
        for cp in first + passed:
            cp.wait_send()
        mine.wait()

    return pl.pallas_call(
        body, name=name, out_shape=jax.ShapeDtypeStruct((N_DEV, R, C), shard.dtype),
        in_specs=[_HBM], out_specs=_HBM,
        scratch_shapes=[pltpu.SemaphoreType.DMA((7,)), pltpu.SemaphoreType.DMA((7,)), pltpu.SemaphoreType.DMA],
    )(shard)


def _gather_weights(shards, *, name):
    nt = len(shards)

    def body(*refs):
        ins, outs = refs[:nt], refs[nt:2 * nt]
        send_sems, recv_sems, local_sems = refs[2 * nt:]
        x, y, c = lax.axis_index("x"), lax.axis_index("y"), lax.axis_index("c")
        me, sibling = (x, y, c), (x, y, 1 - c)
        chips = [(1 - x, y), (x, 1 - y), (1 - x, 1 - y)]

        def copy(t, k, block, to, src=None):
            rows = outs[t].at[_logical(*block)]
            return pltpu.make_async_remote_copy(
                src_ref=rows if src is None else src, dst_ref=rows,
                send_sem=send_sems.at[t, k], recv_sem=recv_sems.at[t, k], device_id=to, device_id_type=_MESH)

        mine = [pltpu.make_async_copy(ins[t], outs[t].at[_logical(*me)], local_sems.at[t]) for t in range(nt)]
        for cp in mine:
            cp.start()
        started = []
        for t in range(nt):
            started.append(copy(t, 0, me, sibling, src=ins[t]))
            started += [copy(t, 1 + j, me, (*chip, c), src=ins[t]) for j, chip in enumerate(chips)]
        for cp in started:
            cp.start()
        for j, chip in enumerate(chips):
            for t in range(nt):
                copy(t, 1 + j, (*chip, c), me).wait_recv()
                fwd = copy(t, 4 + j, (*chip, c), sibling)
                fwd.start()
                started.append(fwd)
        for t in range(nt):
            copy(t, 0, sibling, me).wait_recv()
            for j, chip in enumerate(chips):
                copy(t, 4 + j, (*chip, 1 - c), me).wait_recv()
        for cp in started:
            cp.wait_send()
        for cp in mine:
            cp.wait()

    return pl.pallas_call(
        body, name=name, out_shape=[jax.ShapeDtypeStruct((N_DEV,) + s.shape, s.dtype) for s in shards],
        in_specs=[_HBM] * nt, out_specs=[_HBM] * nt,
        scratch_shapes=[pltpu.SemaphoreType.DMA((nt, 7)), pltpu.SemaphoreType.DMA((nt, 7)), pltpu.SemaphoreType.DMA((nt,))],
    )(*shards)


def _sum_rows(stacked, *, name):
    _, R, C = stacked.shape
    tr = R
    if N_DEV * R * C * stacked.dtype.itemsize > 12 * 1024 * 1024:
        for cand in range(512, 15, -16):
            if R % cand == 0:
                tr = cand
                break

    def body(s_ref, o_ref):
        acc = s_ref[0].astype(F32)
        for k in range(1, N_DEV):
            acc = acc + s_ref[k].astype(F32)
        o_ref[...] = acc

    return pl.pallas_call(
        body, name=name, grid=(R // tr,),
        in_specs=[pl.BlockSpec((N_DEV, tr, C), lambda i: (0, i, 0))], out_specs=pl.BlockSpec((tr, C), lambda i: (i, 0)),
        out_shape=jax.ShapeDtypeStruct((R, C), F32), compiler_params=_cparams("parallel"),
    )(stacked)


_LARGE = (("ab_w_in", 1, True), ("ab_w_out", 1, False), ("s5_w_glu", 1, True), ("xattn_wq", 2, False),
          ("xattn_wkv", 2, True), ("xattn_wo", 2, False), ("ffn_w_in", 2, True), ("ffn_w_out", 2, False))
_LARGE_KEYS = tuple((n, l) for n, layers, _ in _LARGE for l in range(layers))
_TRANSPOSED = {n: t for n, _, t in _LARGE}


def _owner_major(name, w):
    return w.T if _TRANSPOSED[name] else w


def _lb_from_logits(logits):
    return jnp.cumsum(jax.nn.softmax(logits, axis=0), axis=0)[0:1]


def _local_step(x, mem, target, W, shards=None):
    B, L, _ = x.shape
    T = B * L
    x0 = x.reshape(T, D_MODEL)
    memf = mem.reshape(B * MEM_LEN, D_MODEL)
    nw = W["norm_w"]
    cos2, sin2 = _rope_tables(L)
    W = dict(W)
    G, received = {}, {}

    def mmx(a, b, gather=(), scatter=(), **kw):
        if shards is None or not (gather or scatter):
            return _mm(a, b, **kw)
        out, gathered, got = _mm(a, b, gather=[shards[k] for k in gather],
                                 scatter=[G[k].reshape(N_DEV, -1, D_MODEL) for k in scatter], **kw)
        for k, g in zip(gather, gathered):
            W[k] = g.reshape(-1, D_MODEL)
        for k, r in zip(scatter, got):
            received[k] = r
        return out

    def vec(v):
        return v.reshape(1, -1)

    saved = []
    xin = x0
    for layer in range(2):
        s = {"x0": xin}
        tag = f"l{layer}"
        if layer == 0:
            h1 = _rms_fwd(xin, vec(nw[0, 0]), name="norm_pre_mix_l0", out_dtype=BF16)
        s["h1"] = h1
        if layer == 0:
            lb, lb_vjp = jax.vjp(_lb_from_logits, W["hgrn_lb_logits"])
            onw = W["hgrn_out_norm_w"].reshape(1, A_WIDTH)
            z = mmx(h1, W["ab_w_in", 0], tb=True, name="ab_in",
                    gather=[("ab_w_out", 0), ("xattn_wq", 0), ("xattn_wkv", 0), ("xattn_wo", 0)])
            ffn0 = [("ffn_w_in", 0), ("ffn_w_out", 0)] if shards is not None else []
            oa, o_raw, s_start, gathered = _hgrn_fwd(z, lb, onw, B, L, name="hgrn_fwd", gather=[shards[k] for k in ffn0])
            for key, g in zip(ffn0, gathered):
                W[key] = g.reshape(-1, D_MODEL)
            qr, kr = _rope_fwd(z, cos2, sin2, B, L, name="rope_qk")
            os_, ls_ = [], []
            for dil in B_DILS:
                o_g, l_g = _dil_fwd(qr, kr, z, dil, B, L, name=f"dil_fwd_{dil}")
                os_.append(o_g)
                ls_.append(l_g)
            ob, lse = _dil_combine(os_, ls_, name="dil_combine")
            ymix = jnp.concatenate([oa, ob.astype(BF16)], axis=-1)
            y1 = _mm(ymix, W["ab_w_out", 0], out_dtype=BF16, name="ab_out")
            s.update(z=z, lb=lb, lb_vjp=lb_vjp, onw=onw, o_raw=o_raw, s_start=s_start, qr=qr, kr=kr, ob=ob, lse=lse, ymix=ymix)
        else:
            p5 = tuple(W[n][0] for n in ("s5_lambda_re", "s5_lambda_im", "s5_log_dt", "s5_b_re", "s5_b_im", "s5_c_re", "s5_c_im"))
            (kern, ez, fz, _, _), s5_vjp = jax.vjp(_s5_build, *p5)
            tab_r, tab_i = _s5_scan_tables(p5[0], p5[1], p5[2], int(math.log2(L // C_TC)))
            kbd, lag_vjp = jax.vjp(_s5_lag_blocks, kern)
            mats = tuple(_s5_dense(kbd, ez, fz, name="s5_maps")) + (tab_r, tab_i)
            dsk = W["s5_d"].reshape(1, D_MODEL)
            ffn1 = [("ffn_w_in", 1)] if shards is not None else []
            gl, ypre, x8, xs, gathered = _s5_fwd(h1, dsk, *mats, B, L, name="s5_fwd", gather=[shards[k] for k in ffn1])
            for key, g in zip(ffn1, gathered):
                W[key] = g.reshape(-1, D_MODEL)
            w_glu = _interleave_rows(W["s5_w_glu", 0])
            zg, y1 = _mm(gl, w_glu, tb=True, name="s5_glu_in", gate=("fwd", "glu"))
            s.update(s5_vjp=s5_vjp, lag_vjp=lag_vjp, mats=mats, x8=x8, xs=xs, dsk=dsk, gl=gl, ypre=ypre, zg=zg, w_glu=w_glu)
        x1, h2 = _rms_fwd(y1, vec(nw[layer, 1]), xin, name=f"norm_post_mix_{tag}", then=(vec(nw[layer, 2]), BF16))
        memn = _rms_fwd(memf, vec(W["mem_norm_w"][layer]), name=f"norm_mem_{tag}", out_dtype=BF16)
        q = _mm(h2, W["xattn_wq", layer], out_dtype=BF16, name=f"x_q_{tag}")
        kv = _mm(memn, W["xattn_wkv", layer], tb=True, out_dtype=BF16, name=f"x_kv_{tag}")
        o = _xattn_fwd(q, kv, B, L, name=f"x_attn_{tag}")
        y2 = _mm(o, W["xattn_wo", layer], out_dtype=BF16, name=f"x_o_{tag}")
        x2, h3 = _rms_fwd(y2, vec(nw[layer, 3]), x1, name=f"norm_post_x_{tag}", then=(vec(nw[layer, 4]), BF16))
        w_ffn_in = _interleave_rows(W["ffn_w_in", layer])
        zf, u = mmx(h3, w_ffn_in, tb=True, name=f"ffn_in_{tag}", gate=("fwd", "swiglu"),
                    gather=[("s5_w_glu", 0), ("xattn_wq", 1), ("xattn_wkv", 1), ("xattn_wo", 1)] if layer == 0 else [])
        y3 = mmx(u, W["ffn_w_out", layer], out_dtype=BF16, name=f"ffn_out_{tag}", gather=[("ffn_w_out", 1)] if layer == 0 else [])
        if layer == 0:
            x3, h1 = _rms_fwd(y3, vec(nw[0, 5]), x2, name="norm_post_ffn_l0", then=(vec(nw[1, 0]), F32))
        else:
            x3 = _rms_fwd(y3, vec(nw[layer, 5]), x2, name=f"norm_post_ffn_{tag}")
        s.update(y1=y1, x1=x1, h2=h2, memn=memn, q=q, kv=kv, o=o, y2=y2, x2=x2, h3=h3, zf=zf, u=u, y3=y3, w_ffn_in=w_ffn_in)
        saved.append(s)
        xin = x3

    loss_parts, dx = _loss_head(xin, target.reshape(T, D_MODEL), name="loss_head")

    d_norm = [[None] * 6 for _ in range(2)]
    d_memn = [None, None]
    for layer in (1, 0):
        s = saved[layer]
        tag = f"l{layer}"
        if layer == 1:
            dy3, d_norm[1][5] = _rms_bwd(s["y3"], vec(nw[1, 5]), dx, name="bnorm_post_ffn_l1", out_dtype=BF16)
        dzf = mmx(dy3, W["ffn_w_out", layer], tb=True, name=f"b_ffn_out_dx_{tag}", gate=("bwd", "swiglu", s["zf"]),
                  scatter=[("xattn_wkv", 1), ("s5_w_glu", 0)] if layer == 0 else [])
        G["ffn_w_out", layer] = _mm(s["u"], dy3, ta=True, out_dtype=BF16, name=f"b_ffn_out_dw_{tag}")
        early = []
        if shards is not None and layer == 0:
            shards = {**shards, "small_early": _pack_small([G[n] for n, _ in _SMALL_EARLY])}
            early = ["small_early"]
        G["ffn_w_in", layer] = _interleave_rows(
            mmx(dzf, s["h3"], ta=True, out_dtype=BF16, name=f"b_ffn_in_dw_{tag}", gather=early,
                scatter=[("ffn_w_out", layer)]), inverse=True)
        if early:
            received["small_early"] = W["small_early"].reshape(N_DEV, -1, LANES)
        dh3 = mmx(dzf, s["w_ffn_in"], out_dtype=BF16, name=f"b_ffn_in_dx_{tag}", scatter=[("ffn_w_in", layer)])
        dx, d_norm[layer][4], dy2, d_norm[layer][3] = _rms_bwd(
            s["x2"], vec(nw[layer, 4]), dh3, dx, name=f"bnorm_pre_ffn_{tag}", then=(s["y2"], vec(nw[layer, 3])))
        do = _mm(dy2, W["xattn_wo", layer], tb=True, out_dtype=BF16, name=f"b_x_o_dx_{tag}")
        G["xattn_wo", layer] = _mm(s["o"], dy2, ta=True, out_dtype=BF16, name=f"b_x_o_dw_{tag}")
        dq, dkv = _xattn_bwd(s["q"], s["kv"], do, B, L, name=f"b_x_attn_{tag}")
        G["xattn_wq", layer] = _mm(s["h2"], dq, ta=True, out_dtype=BF16, name=f"b_x_q_dw_{tag}")
        dh2 = _mm(dq, W["xattn_wq", layer], tb=True, out_dtype=BF16, name=f"b_x_q_dx_{tag}")
        G["xattn_wkv", layer] = _mm(dkv, s["memn"], ta=True, out_dtype=BF16, name=f"b_x_kv_dw_{tag}")
        dmemn = _mm(dkv, W["xattn_wkv", layer], out_dtype=BF16, name=f"b_x_kv_dx_{tag}")
        _, d_memn[layer] = _rms_bwd(memf, vec(W["mem_norm_w"][layer]), dmemn, name=f"bnorm_mem_{tag}", out_dtype=BF16)
        dx, d_norm[layer][2], dy1, d_norm[layer][1] = _rms_bwd(
            s["x1"], vec(nw[layer, 2]), dh2, dx, name=f"bnorm_pre_x_{tag}", then=(s["y1"], vec(nw[layer, 1])))
        if layer == 0:
            z = s["z"]
            dymix = _mm(dy1, W["ab_w_out", 0], tb=True, name="b_ab_out_dx")
            G["ab_w_out", 0] = _mm(s["ymix"], dy1, ta=True, out_dtype=BF16, name="b_ab_out_dw")
            early = [("xattn_wo", 0), ("xattn_wq", 0), ("xattn_wkv", 0), ("ab_w_out", 0)] if shards is not None else []
            dqa, dfa, dia, dga, d_onw, d_lb, got = _hgrn_bwd(
                z, s["lb"], s["onw"], s["o_raw"], s["s_start"], dymix, B, L, name="hgrn_bwd",
                scatter=[G[k].reshape(N_DEV, -1, D_MODEL) for k in early])
            received.update(zip(early, got))
            dqs, dks, dvs = [], [], []
            for dil in B_DILS:
                dqs.append(_dil_bwd_q(s["qr"], s["kr"], z, dymix, s["ob"], s["lse"], dil, B, L, name=f"dil_bwd_q_{dil}"))
                dk_g, dv_g = _dil_bwd_kv(s["qr"], s["kr"], z, dymix, s["ob"], s["lse"], dil, B, L, name=f"dil_bwd_kv_{dil}")
                dks.append(dk_g)
                dvs.append(dv_g)
            dqkv = _rope_bwd(dqs, dks, dvs, cos2, sin2, B, L, name="b_rope")
            dz = jnp.concatenate([dqa, dfa, dia, dga, dqkv], axis=-1)
            G["ab_w_in", 0] = _mm(dz, s["h1"], ta=True, out_dtype=BF16, name="b_ab_in_dw")
            dh1 = mmx(dz, W["ab_w_in", 0], out_dtype=BF16, name="b_ab_in_dx", scatter=[("ab_w_in", 0)])
            G["hgrn_out_norm_w"] = jnp.sum(d_onw.reshape(B, A_WIDTH), axis=0, keepdims=True)
            d_lb_row = jnp.sum(d_lb.reshape(B, A_WIDTH), axis=0, keepdims=True)
            G["hgrn_lb_logits"] = s["lb_vjp"](d_lb_row)[0]
        else:
            dzg = _gated_bwd(s["zg"], dy1, "glu", name="b_s5_glu")
            G["s5_w_glu", 0] = _interleave_rows(
                _mm(dzg, s["gl"], ta=True, out_dtype=BF16, name="b_s5_glu_dw"), inverse=True)
            dgl = mmx(dzg, s["w_glu"], name="b_s5_glu_dx", scatter=[("xattn_wo", 1), ("xattn_wq", 1)])
            dh1, dy8, de_in, da, d_dsk = _s5_bwd(dgl, s["ypre"], s["h1"], s["dsk"], s["xs"], *s["mats"], B, L, name="s5_bwd")
            dkbd = _bmm_tn(s["x8"], dy8, C_NB, "lags", name="s5_bwd_dm").transpose(1, 0, 2, 3)
            dfz = _bmm_tn(s["xs"], dy8, C_NB, "f", name="s5_bwd_df").reshape(C_NB, 2, C_GB, C_STATE, C_TC, C_GROUP)
            dez = _bmm_tn(s["x8"], de_in, C_NB, "e", name="s5_bwd_de").reshape(C_NB, C_TC, C_GB, C_GROUP, 2, C_STATE)
            half = C_S8 // 2
            da_r = (da[:, 0, :half] + da[:, 0, half:]).reshape(C_GROUPS, C_STATE)
            da_i = (da[:, 1, half:] - da[:, 1, :half]).reshape(C_GROUPS, C_STATE)
            gp = s["s5_vjp"](s["lag_vjp"](dkbd) + (dez, dfz, da_r, da_i))
            for n, gv in zip(("s5_lambda_re", "s5_lambda_im", "s5_log_dt", "s5_b_re", "s5_b_im", "s5_c_re", "s5_c_im"), gp):
                G[n] = gv[None]
            G["s5_d"] = d_dsk
        if layer == 1:
            dx, d_norm[1][0], dy3, d_norm[0][5] = _rms_bwd(
                s["x0"], vec(nw[1, 0]), dh1, dx, name="bnorm_pre_mix_l1", then=(saved[0]["y3"], vec(nw[0, 5])))
        else:
            dx, d_norm[0][0] = _rms_bwd(s["x0"], vec(nw[0, 0]), dh1, dx, name="bnorm_pre_mix_l0")

    G["norm_w"] = jnp.stack([jnp.concatenate(d_norm[l], axis=0) for l in range(2)])
    G["mem_norm_w"] = jnp.concatenate(d_memn, axis=0)
    if shards is not None:
        G.update(received)
    return loss_parts, dx.reshape(B, L, D_MODEL), G


_SMALL_LATE = (("norm_w", (2, 6, 1024)), ("mem_norm_w", (2, 1024)), ("hgrn_lb_logits", (3, 512)), ("hgrn_out_norm_w", (1, 512)))
_SMALL_EARLY = (("s5_lambda_re", (1, 64, 64)), ("s5_lambda_im", (1, 64, 64)), ("s5_log_dt", (1, 64)),
                ("s5_b_re", (1, 64, 64, 16)), ("s5_b_im", (1, 64, 64, 16)), ("s5_c_re", (1, 64, 16, 64)),
                ("s5_c_im", (1, 64, 16, 64)), ("s5_d", (1, 1024)))
_SMALL = _SMALL_LATE + _SMALL_EARLY


def _pack_small(values):
    flat = jnp.concatenate([v.reshape(-1) for v in values])
    return jnp.pad(flat, (0, (-flat.shape[0]) % (8 * LANES))).reshape(-1, LANES)


def _unpack_small(flat, spec):
    out, off = {}, 0
    for n, shp in spec:
        size = int(np.prod(shp))
        out[n] = flat[off:off + size].reshape(shp)
        off += size
    return out, off

_WEIGHT_ORDER = ('norm_w', 'mem_norm_w', 'ab_w_in', 'ab_w_out', 'hgrn_lb_logits', 'hgrn_out_norm_w', 's5_lambda_re',
                 's5_lambda_im', 's5_log_dt', 's5_b_re', 's5_b_im', 's5_c_re', 's5_c_im', 's5_d', 's5_w_glu', 'xattn_wq',
                 'xattn_wkv', 'xattn_wo', 'ffn_w_in', 'ffn_w_out')


def kernel(x, mem, norm_w, mem_norm_w, ab_w_in, ab_w_out, hgrn_lb_logits, hgrn_out_norm_w, s5_lambda_re, s5_lambda_im, s5_log_dt, s5_b_re, s5_b_im, s5_c_re, s5_c_im, s5_d, s5_w_glu, xattn_wq, xattn_wkv, xattn_wo, ffn_w_in, ffn_w_out, loss_target, m_norm_w, m_mem_norm_w, m_ab_w_in, m_ab_w_out, m_hgrn_lb_logits, m_hgrn_out_norm_w, m_s5_lambda_re, m_s5_lambda_im, m_s5_log_dt, m_s5_b_re, m_s5_b_im, m_s5_c_re, m_s5_c_im, m_s5_d, m_s5_w_glu, m_xattn_wq, m_xattn_wkv, m_xattn_wo, m_ffn_w_in, m_ffn_w_out, v_norm_w, v_mem_norm_w, v_ab_w_in, v_ab_w_out, v_hgrn_lb_logits, v_hgrn_out_norm_w, v_s5_lambda_re, v_s5_lambda_im, v_s5_log_dt, v_s5_b_re, v_s5_b_im, v_s5_c_re, v_s5_c_im, v_s5_d, v_s5_w_glu, v_xattn_wq, v_xattn_wkv, v_xattn_wo, v_ffn_w_in, v_ffn_w_out):
    local = dict(norm_w=norm_w, mem_norm_w=mem_norm_w, ab_w_in=ab_w_in, ab_w_out=ab_w_out, hgrn_lb_logits=hgrn_lb_logits,
                 hgrn_out_norm_w=hgrn_out_norm_w, s5_lambda_re=s5_lambda_re, s5_lambda_im=s5_lambda_im, s5_log_dt=s5_log_dt,
                 s5_b_re=s5_b_re, s5_b_im=s5_b_im, s5_c_re=s5_c_re, s5_c_im=s5_c_im, s5_d=s5_d, s5_w_glu=s5_w_glu,
                 xattn_wq=xattn_wq, xattn_wkv=xattn_wkv, xattn_wo=xattn_wo, ffn_w_in=ffn_w_in, ffn_w_out=ffn_w_out)
    mom_m = dict(zip(_WEIGHT_ORDER, (m_norm_w, m_mem_norm_w, m_ab_w_in, m_ab_w_out, m_hgrn_lb_logits, m_hgrn_out_norm_w, m_s5_lambda_re, m_s5_lambda_im, m_s5_log_dt, m_s5_b_re, m_s5_b_im, m_s5_c_re, m_s5_c_im, m_s5_d, m_s5_w_glu, m_xattn_wq, m_xattn_wkv, m_xattn_wo, m_ffn_w_in, m_ffn_w_out)))
    mom_v = dict(zip(_WEIGHT_ORDER, (v_norm_w, v_mem_norm_w, v_ab_w_in, v_ab_w_out, v_hgrn_lb_logits, v_hgrn_out_norm_w, v_s5_lambda_re, v_s5_lambda_im, v_s5_log_dt, v_s5_b_re, v_s5_b_im, v_s5_c_re, v_s5_c_im, v_s5_d, v_s5_w_glu, v_xattn_wq, v_xattn_wkv, v_xattn_wo, v_ffn_w_in, v_ffn_w_out)))
    dev = 4 * lax.axis_index("x") + 2 * lax.axis_index("y") + lax.axis_index("c")

    shards = {(n, l): _owner_major(n, local[n][l]).astype(BF16) for n, l in _LARGE_KEYS}
    first = ("ab_w_in", 0)
    W = {first: _gather_weights([shards[first]], name="gather_first")[0].reshape(-1, D_MODEL)}
    tiny = jnp.concatenate([norm_w.reshape(-1), s5_d.reshape(-1)])
    tiny = jnp.pad(tiny, (0, 16 * LANES - tiny.shape[0])).reshape(16, LANES)
    tiny_all = _all_gather(tiny, name="gather_tiny").reshape(N_DEV, 16 * LANES)
    W["norm_w"] = tiny_all[:, :12 * LANES].reshape(N_DEV, 2, 6, LANES).transpose(1, 2, 0, 3).reshape(2, 6, D_MODEL)
    W["s5_d"] = tiny_all[:, 12 * LANES:13 * LANES].reshape(1, D_MODEL)
    for n in ("mem_norm_w", "hgrn_lb_logits", "hgrn_out_norm_w", "s5_lambda_re", "s5_lambda_im", "s5_log_dt",
              "s5_b_re", "s5_b_im", "s5_c_re", "s5_c_im"):
        W[n] = local[n]

    loss_parts, grad_x, G = _local_step(x, mem, loss_target, W, shards)

    g_layers = {}
    for n, l in _LARGE_KEYS:
        g = _sum_rows(G[n, l], name=f"sum_grads_{n}_{l}")
        g_layers.setdefault(n, []).append(g.T if _TRANSPOSED[n] else g)
    g_local = {n: jnp.stack(gl) for n, gl in g_layers.items()}
    small = _pack_small([G[n] for n, _ in _SMALL_LATE] + [0.5 / D_MODEL * jnp.sum(loss_parts)])
    small_sum = _sum_rows(_all_gather(small, name="gather_small"), name="sum_small").reshape(-1)
    g_full, off = _unpack_small(small_sum, _SMALL_LATE)
    loss = small_sum[off]
    early_sum = _sum_rows(G["small_early"], name="sum_small_early").reshape(-1)
    g_full.update(_unpack_small(early_sum, _SMALL_EARLY)[0])
    grads = dict(g_local)
    for n, shp in _SMALL:
        if n == "norm_w":
            grads[n] = lax.dynamic_slice_in_dim(g_full[n], dev * LANES, LANES, axis=2)
        elif n == "s5_d":
            grads[n] = lax.dynamic_slice_in_dim(g_full[n], dev * LANES, LANES, axis=1)
        else:
            grads[n] = g_full[n]

    delta, new_m, new_v = {}, {}, {}
    for n in _WEIGHT_ORDER:
        delta[n], new_m[n], new_v[n] = _adamw(local[n], grads[n], mom_m[n], mom_v[n], name=f"adamw_{n}")
    return (loss, grad_x, *[grads[n] for n in _WEIGHT_ORDER], *[delta[n] for n in _WEIGHT_ORDER],
            *[new_m[n] for n in _WEIGHT_ORDER], *[new_v[n] for n in _WEIGHT_ORDER])
```

```python
import functools
import math

import numpy as np
import jax
import jax.numpy as jnp
from jax import lax
from jax.experimental import pallas as pl
from jax.experimental.pallas import tpu as pltpu

F32 = jnp.float32
BF16 = jnp.bfloat16
HI = lax.Precision.HIGHEST

D_MODEL = 1024
NORM_EPS = 1e-6
A_WIDTH = 512
A_HEAD = 128
A_CHUNK = 32
A_SUPER = 256
B_SPAN = 128
B_DILS = (1, 4, 16)
ROPE_THETA = 10000.0
C_GROUPS = 64
C_GROUP = 16
C_STATE = 64
C_TC = 8
C_MIN_NEG_RE = -1e-4
MEM_LEN = 256
X_HEADS = 4
X_HD = 256
D_FF = 2816
N_DEV = 8
LANES = 128

ADAM_LR, ADAM_B1, ADAM_B2, ADAM_EPS, ADAM_WD, ADAM_STEP = 0.001, 0.9, 0.999, 1e-08, 0.01, 10

NEG_BIG = -1e30


def _tile(n, pref):
    for d in range(min(pref, n) // LANES * LANES, 0, -LANES):
        if n % d == 0:
            return d
    return n


def _cparams(*sem):
    return pltpu.CompilerParams(dimension_semantics=sem, vmem_limit_bytes=56 * 1024 * 1024)


def _sigmoid(x):
    return 0.5 * jnp.tanh(0.5 * x) + 0.5


def _erf(x):
    ax = jnp.abs(x)
    t = 1.0 / (1.0 + 0.3275911 * ax)
    poly = t * (0.254829592 + t * (-0.284496736 + t * (1.421413741 + t * (-1.453152027 + t * 1.061405429))))
    y = 1.0 - poly * jnp.exp(-ax * ax)
    return jnp.where(x < 0, -y, y)


_HBM = pl.BlockSpec(memory_space=pltpu.HBM)
_MESH = pl.DeviceIdType.MESH


def _logical(px, py, pc):
    return 4 * px + 2 * py + pc


class _Exchange:
    def __init__(self, gather=(), scatter=()):
        self.gather, self.scatter = list(gather), list(scatter)
        self.ng, self.n = len(self.gather), len(self.gather) + len(self.scatter)

    def operands(self):
        return self.gather + self.scatter

    def in_specs(self):
        return [_HBM] * self.n

    def out_shapes(self):
        return ([jax.ShapeDtypeStruct((N_DEV,) + g.shape, g.dtype) for g in self.gather]
                + [jax.ShapeDtypeStruct(s.shape, s.dtype) for s in self.scatter])

    def scratch(self):
        if not self.n:
            return []
        return [pltpu.SemaphoreType.DMA((self.n, 7)), pltpu.SemaphoreType.DMA((self.n, 7)), pltpu.SemaphoreType.DMA((self.n,))]

    def split(self, results):
        return list(results[:self.ng]), list(results[self.ng:])

    def run(self, ins, outs, sems, first, last):
        if not self.n:
            return
        send_sems, recv_sems, local_sems = sems
        x, y, c = lax.axis_index("x"), lax.axis_index("y"), lax.axis_index("c")
        me, sibling = _logical(x, y, c), (x, y, 1 - c)
        chips = [(1 - x, y), (x, 1 - y), (1 - x, 1 - y)]
        peers = [(x ^ (k >> 2), y ^ ((k >> 1) & 1), c ^ (k & 1)) for k in range(1, N_DEV)]

        def remote(t, k, src, dst, to):
            return pltpu.make_async_remote_copy(src_ref=src, dst_ref=dst, send_sem=send_sems.at[t, k],
                                                recv_sem=recv_sems.at[t, k], device_id=to, device_id_type=_MESH)

        def local(t):
            src = ins[t] if t < self.ng else ins[t].at[me]
            return pltpu.make_async_copy(src, outs[t].at[me], local_sems.at[t])

        @pl.when(first)
        def _():
            for t in range(self.n):
                local(t).start()
                if t < self.ng:
                    remote(t, 0, ins[t], outs[t].at[me], sibling).start()
                    for j, chip in enumerate(chips):
                        remote(t, 1 + j, ins[t], outs[t].at[me], (*chip, c)).start()
                else:
                    for k, peer in enumerate(peers):
                        remote(t, k, ins[t].at[_logical(*peer)], outs[t].at[me], peer).start()

        @pl.when(last)
        def _():
            for j, chip in enumerate(chips):
                for t in range(self.ng):
                    landed = outs[t].at[_logical(*chip, c)]
                    remote(t, 1 + j, ins[t], landed, sibling).wait_recv()
                    remote(t, 4 + j, landed, landed, sibling).start()
            for t in range(self.n):
                if t < self.ng:
                    remote(t, 0, ins[t], outs[t].at[_logical(*sibling)], sibling).wait_recv()
                    for j, chip in enumerate(chips):
                        remote(t, 4 + j, ins[t], outs[t].at[_logical(*chip, 1 - c)], sibling).wait_recv()
                    for k in range(7):
                        remote(t, k, ins[t], outs[t].at[me], sibling).wait_send()
                else:
                    for k, peer in enumerate(peers):
                        remote(t, k, ins[t].at[me], outs[t].at[_logical(*peer)], peer).wait_recv()
                    for k, peer in enumerate(peers):
                        remote(t, k, ins[t].at[_logical(*peer)], outs[t].at[me], peer).wait_send()
                local(t).wait()


def _carrying(body, n_in, n_out, ex, grid):
    n_sems = len(ex.scratch())

    def wrapped(*refs):
        ins, ex_in = refs[:n_in], refs[n_in:n_in + ex.n]
        outs = refs[n_in + ex.n:n_in + ex.n + n_out]
        ex_out = refs[n_in + ex.n + n_out:n_in + 2 * ex.n + n_out]
        rest = refs[n_in + 2 * ex.n + n_out:]
        scratch, sems = rest[:len(rest) - n_sems], rest[len(rest) - n_sems:]
        ids = [pl.program_id(a) for a in range(len(grid))]
        first = functools.reduce(lambda u, v: u & v, [i == 0 for i in ids])
        last = functools.reduce(lambda u, v: u & v, [i == g - 1 for i, g in zip(ids, grid)])
        ex.run(ex_in, ex_out, sems, first, last)
        body(*ins, *outs, *scratch)

    return wrapped


_MM_VMEM_BUDGET = 36 * 1024 * 1024


def _mm(a, b, *, ta=False, tb=False, out_dtype=F32, name, tiles=(1408, 1408, 4096), gather=(), scatter=(), gate=None):
    M, K = (a.shape[1], a.shape[0]) if ta else a.shape
    N = b.shape[0] if tb else b.shape[1]
    assert (b.shape[1] if tb else b.shape[0]) == K
    gate_mode = gate[0] if gate else None
    tm, tn, tk = _tile(M, tiles[0]), _tile(N, tiles[1]), _tile(K, tiles[2])
    if gate_mode == "fwd":
        bs = _gate_block(N // 2)
        tn = 2 * bs
    elif gate_mode == "bwd":
        bs = _gate_block(N)
        tn = bs

    def vmem_bytes():
        acc = 4 * tm * tn if tk < K else 0
        if gate_mode == "fwd":
            io = (2 * (2 + 1) + 4) * tm * tn
        elif gate_mode == "bwd":
            io = (2 * (4 + 4) + 4) * tm * tn
        else:
            io = 2 * tm * tn * jnp.dtype(out_dtype).itemsize
        return 2 * 2 * (tm * tk + tk * tn) + acc + io

    while vmem_bytes() > _MM_VMEM_BUDGET:
        if gate_mode and tm > 256:
            tm = _tile(M, tm - LANES)
        elif tk > 512:
            tk = _tile(K, tk - LANES)
        elif tn > 256 and not gate_mode:
            tn = _tile(N, tn - LANES)
        else:
            tm = _tile(M, tm - LANES)
    ni, nj, nk = M // tm, N // tn, K // tk
    ex = _Exchange(gather, scatter)

    a_spec = pl.BlockSpec((tk, tm), lambda i, j, k: (k, i)) if ta else pl.BlockSpec((tm, tk), lambda i, j, k: (i, k))
    b_spec = pl.BlockSpec((tn, tk), lambda i, j, k: (j, k)) if tb else pl.BlockSpec((tk, tn), lambda i, j, k: (k, j))
    dims = (((0 if ta else 1,), (1 if tb else 0,)), ((), ()))
    n_acc = 1 if nk > 1 else 0
    n_in = 3 if gate_mode == "bwd" else 2
    n_out = 2 if gate_mode == "fwd" else 1

    def finish(val, in_refs, out_refs, rows=slice(None)):
        if gate_mode is None:
            out_refs[0][rows, :] = val.astype(out_refs[0].dtype)
        elif gate_mode == "fwd":
            out_refs[0][rows, :] = val.astype(BF16)
            for p in range(tn // (2 * bs)):
                a_, b_ = val[:, 2 * p * bs:(2 * p + 1) * bs], val[:, (2 * p + 1) * bs:(2 * p + 2) * bs]
                out_refs[1][rows, p * bs:(p + 1) * bs] = _gate_value(a_, b_, gate[1]).astype(BF16)
        else:
            z_ref = in_refs[2]
            for p in range(tn // bs):
                a_ = z_ref[rows, 2 * p * bs:(2 * p + 1) * bs].astype(F32)
                b_ = z_ref[rows, (2 * p + 1) * bs:(2 * p + 2) * bs].astype(F32)
                da, db = _gate_grads(a_, b_, val[:, p * bs:(p + 1) * bs], gate[1])
                out_refs[0][rows, 2 * p * bs:(2 * p + 1) * bs] = da.astype(BF16)
                out_refs[0][rows, (2 * p + 1) * bs:(2 * p + 2) * bs] = db.astype(BF16)

    halves = 2 if (gate_mode and nk == 1 and not ta and tm % 32 == 0) else 1

    def body(*refs):
        in_refs, rest = refs[:n_in], refs[n_in:]
        ex_in, out_refs = rest[:ex.n], rest[ex.n:ex.n + n_out]
        ex_out, scratch = rest[ex.n + n_out:2 * ex.n + n_out], rest[2 * ex.n + n_out:]
        i, j, k = pl.program_id(0), pl.program_id(1), pl.program_id(2)
        ex.run(ex_in, ex_out, scratch[n_acc:], (i == 0) & (j == 0) & (k == 0), (i == ni - 1) & (j == nj - 1) & (k == nk - 1))
        if halves > 1:
            rh = tm // halves
            rhs = in_refs[1][...].astype(BF16)
            parts = [lax.dot_general(in_refs[0][r * rh:(r + 1) * rh, :].astype(BF16), rhs, dims, preferred_element_type=F32)
                     for r in range(halves)]
            for r, p in enumerate(parts):
                finish(p, in_refs, out_refs, slice(r * rh, (r + 1) * rh))
            return
        part = lax.dot_general(in_refs[0][...].astype(BF16), in_refs[1][...].astype(BF16), dims, preferred_element_type=F32)
        if nk == 1:
            finish(part, in_refs, out_refs)
            return
        acc_ref = scratch[0]

        @pl.when(k == 0)
        def _():
            acc_ref[...] = part

        @pl.when(k > 0)
        def _():
            acc_ref[...] += part

        @pl.when(k == nk - 1)
        def _():
            finish(acc_ref[...], in_refs, out_refs)

    tile = lambda width: pl.BlockSpec((tm, width), lambda i, j, k: (i, j))
    if gate_mode == "fwd":
        out_specs, out_shape = [tile(tn), tile(tn // 2)], [jax.ShapeDtypeStruct((M, N), BF16), jax.ShapeDtypeStruct((M, N // 2), BF16)]
    elif gate_mode == "bwd":
        out_specs, out_shape = [tile(2 * tn)], [jax.ShapeDtypeStruct((M, 2 * N), BF16)]
    else:
        out_specs, out_shape = [tile(tn)], [jax.ShapeDtypeStruct((M, N), out_dtype)]
    operands = [a, b] + ([gate[2]] if gate_mode == "bwd" else [])
    sem = ("arbitrary",) * 3 if ex.n else ("parallel", "parallel", "arbitrary")
    res = pl.pallas_call(
        body, name=name, grid=(ni, nj, nk),
        in_specs=[a_spec, b_spec] + ([tile(2 * tn)] if gate_mode == "bwd" else []) + ex.in_specs(),
        out_specs=out_specs + ex.in_specs(),
        out_shape=out_shape + ex.out_shapes(),
        scratch_shapes=([pltpu.VMEM((tm, tn), F32)] if nk > 1 else []) + ex.scratch(),
        compiler_params=_cparams(*sem),
    )(*operands, *ex.operands())
    main = res[0] if n_out == 1 else tuple(res[:n_out])
    if not ex.n:
        return main
    return (main,) + tuple(ex.split(res[n_out:]))


def _rms_fwd(x, w, res=None, *, name, out_dtype=F32, then=None):
    T, C = x.shape
    tm = _tile(T, 1024)
    has_res = res is not None

    def norm(v, w_ref):
        return v * lax.rsqrt(jnp.mean(v * v, axis=-1, keepdims=True) + NORM_EPS) * w_ref[...]

    def body(*refs):
        x_ref, w_ref = refs[0], refs[1]
        y = norm(x_ref[...].astype(F32), w_ref)
        if has_res:
            y = y + refs[2][...]
        if then is None:
            refs[-1][...] = y.astype(refs[-1].dtype)
        else:
            refs[-2][...] = y.astype(refs[-2].dtype)
            refs[-1][...] = norm(y, refs[-3]).astype(refs[-1].dtype)

    row = pl.BlockSpec((tm, C), lambda i: (i, 0))
    vec = pl.BlockSpec((1, C), lambda i: (0, 0))
    ins = [x, w] + ([res] if has_res else []) + ([then[0]] if then else [])
    in_specs = [row, vec] + ([row] if has_res else []) + ([vec] if then else [])
    out_shape = [jax.ShapeDtypeStruct((T, C), out_dtype)] + ([jax.ShapeDtypeStruct((T, C), then[1])] if then else [])
    res_ = pl.pallas_call(
        body, name=name, grid=(T // tm,), in_specs=in_specs, out_specs=[row] * len(out_shape),
        out_shape=out_shape, compiler_params=_cparams("parallel"),
    )(*ins)
    return tuple(res_) if then else res_[0]


def _rms_bwd(x, w, dy, add=None, *, name, out_dtype=F32, then=None):
    T, C = x.shape
    tm = _tile(T, 512)
    has_add = add is not None
    n_in = 3 + has_add + (2 if then else 0)

    def grads(x_ref, w_ref, g, dw_ref):
        xv = x_ref[...].astype(F32)
        r = lax.rsqrt(jnp.mean(xv * xv, axis=-1, keepdims=True) + NORM_EPS)
        xh = xv * r
        part = jnp.sum(g * xh, axis=0, keepdims=True)

        @pl.when(pl.program_id(0) == 0)
        def _():
            dw_ref[...] = part

        @pl.when(pl.program_id(0) > 0)
        def _():
            dw_ref[...] += part

        gx = g * w_ref[...]
        return r * (gx - xh * jnp.mean(gx * xh, axis=-1, keepdims=True))

    def body(*refs):
        ins, outs = refs[:n_in], refs[n_in:]
        dx = grads(ins[0], ins[1], ins[2][...].astype(F32), outs[1])
        if has_add:
            dx = dx + ins[3][...]
        outs[0][...] = dx.astype(outs[0].dtype)
        if then:
            outs[2][...] = grads(ins[-2], ins[-1], dx, outs[3]).astype(BF16)

    row = pl.BlockSpec((tm, C), lambda i: (i, 0))
    vec = pl.BlockSpec((1, C), lambda i: (0, 0))
    ins = [x, w, dy] + ([add] if has_add else []) + (list(then) if then else [])
    big, small = jax.ShapeDtypeStruct((T, C), out_dtype), jax.ShapeDtypeStruct((1, C), F32)
    return pl.pallas_call(
        body, name=name, grid=(T // tm,),
        in_specs=[row, vec, row] + ([row] if has_add else []) + ([row, vec] if then else []),
        out_specs=[row, vec] + ([row, vec] if then else []),
        out_shape=[big, small] + ([jax.ShapeDtypeStruct((T, C), BF16), small] if then else []),
        compiler_params=_cparams("arbitrary"),
    )(*ins)


def _gate_block(width):
    return _tile(width, 1408)


def _gate_value(a, b, kind):
    return a * _sigmoid(a) * b if kind == "swiglu" else a * _sigmoid(b)


def _gate_grads(a, b, d, kind):
    if kind == "swiglu":
        s = _sigmoid(a)
        return d * b * (s * (1.0 + a * (1.0 - s))), d * a * s
    s = _sigmoid(b)
    return d * s, d * a * s * (1.0 - s)


def _interleave_rows(w, inverse=False):
    W2, C = w.shape
    bs = _gate_block(W2 // 2)
    nb = W2 // 2 // bs
    shape = (nb, 2, bs, C) if inverse else (2, nb, bs, C)
    return w.reshape(shape).transpose(1, 0, 2, 3).reshape(W2, C)


def _gated_bwd(z, dout, kind, *, name):
    T, W2 = z.shape
    W = W2 // 2
    tm, bs = _tile(T, 512), _gate_block(W)

    def body(z_ref, d_ref, o_ref):
        da, db = _gate_grads(z_ref[:, :bs].astype(F32), z_ref[:, bs:].astype(F32), d_ref[...].astype(F32), kind)
        o_ref[:, :bs] = da.astype(o_ref.dtype)
        o_ref[:, bs:] = db.astype(o_ref.dtype)

    return pl.pallas_call(
        body, name=name, grid=(T // tm, W // bs),
        in_specs=[pl.BlockSpec((tm, 2 * bs), lambda i, j: (i, j)), pl.BlockSpec((tm, bs), lambda i, j: (i, j))],
        out_specs=pl.BlockSpec((tm, 2 * bs), lambda i, j: (i, j)),
        out_shape=jax.ShapeDtypeStruct((T, W2), BF16), compiler_params=_cparams("parallel", "parallel"),
    )(z, dout)


_NT = (((1,), (1,)), ((), ()))
_TN = (((0,), (0,)), ((), ()))


def _dot(a, b, dims=None, precision=None):
    if dims is None:
        return jnp.dot(a, b, preferred_element_type=F32, precision=precision)
    return lax.dot_general(a, b, dims, preferred_element_type=F32, precision=precision)


def _xattn_fwd(q, kv, B, L, *, name):
    T = q.shape[0]
    tq = 512
    nq = L // tq
    scale = X_HD ** -0.5

    def body(q_ref, k_ref, v_ref, o_ref):
        heads = [slice(h * X_HD, (h + 1) * X_HD) for h in range(X_HEADS)]
        s = [_dot(q_ref[:, sl].astype(BF16), k_ref[:, sl].astype(BF16), _NT) * scale for sl in heads]
        m = [jnp.max(a, axis=-1, keepdims=True) for a in s]
        p = [jnp.exp(a - b) for a, b in zip(s, m)]
        l = [jnp.sum(a, axis=-1, keepdims=True) for a in p]
        o = [_dot(a.astype(BF16), v_ref[:, sl].astype(BF16)) for a, sl in zip(p, heads)]
        for sl, a, b in zip(heads, o, l):
            o_ref[:, sl] = (a / b).astype(BF16)

    return pl.pallas_call(
        body, name=name, grid=(B, nq),
        in_specs=[pl.BlockSpec((tq, D_MODEL), lambda b, i: (b * nq + i, 0)),
                  pl.BlockSpec((MEM_LEN, D_MODEL), lambda b, i: (b, 0)),
                  pl.BlockSpec((MEM_LEN, D_MODEL), lambda b, i: (b, 1))],
        out_specs=pl.BlockSpec((tq, D_MODEL), lambda b, i: (b * nq + i, 0)),
        out_shape=jax.ShapeDtypeStruct((T, D_MODEL), BF16), compiler_params=_cparams("parallel", "parallel"),
    )(q, kv, kv)


def _xattn_bwd(q, kv, do, B, L, *, name):
    T = q.shape[0]
    tq = 512
    nq = L // tq
    scale = X_HD ** -0.5

    def body(q_ref, k_ref, v_ref, do_ref, dq_ref, dkv_ref):
        @pl.when(pl.program_id(1) == 0)
        def _():
            dkv_ref[...] = jnp.zeros_like(dkv_ref)

        heads = [slice(h * X_HD, (h + 1) * X_HD) for h in range(X_HEADS)]
        qs = [q_ref[:, sl].astype(BF16) for sl in heads]
        ks = [k_ref[:, sl].astype(BF16) for sl in heads]
        dos = [do_ref[:, sl].astype(BF16) for sl in heads]
        s = [_dot(a, b, _NT) * scale for a, b in zip(qs, ks)]
        dp = [_dot(a, v_ref[:, sl].astype(BF16), _NT) for a, sl in zip(dos, heads)]
        e = [jnp.exp(a - jnp.max(a, axis=-1, keepdims=True)) for a in s]
        p = [a / jnp.sum(a, axis=-1, keepdims=True) for a in e]
        ds = [(a * (b - jnp.sum(b * a, axis=-1, keepdims=True)) * scale).astype(BF16) for a, b in zip(p, dp)]
        dv = [_dot(a.astype(BF16), b, _TN) for a, b in zip(p, dos)]
        dq = [_dot(a, b) for a, b in zip(ds, ks)]
        dk = [_dot(a, b, _TN) for a, b in zip(ds, qs)]
        for h, sl in enumerate(heads):
            dq_ref[:, sl] = dq[h].astype(BF16)
            dkv_ref[:, sl] += dk[h]
            dkv_ref[:, D_MODEL + h * X_HD:D_MODEL + (h + 1) * X_HD] += dv[h]

    return pl.pallas_call(
        body, name=name, grid=(B, nq),
        in_specs=[pl.BlockSpec((tq, D_MODEL), lambda b, i: (b * nq + i, 0)),
                  pl.BlockSpec((MEM_LEN, D_MODEL), lambda b, i: (b, 0)),
                  pl.BlockSpec((MEM_LEN, D_MODEL), lambda b, i: (b, 1)),
                  pl.BlockSpec((tq, D_MODEL), lambda b, i: (b * nq + i, 0))],
        out_specs=[pl.BlockSpec((tq, D_MODEL), lambda b, i: (b * nq + i, 0)),
                   pl.BlockSpec((MEM_LEN, 2 * D_MODEL), lambda b, i: (b, 0))],
        out_shape=[jax.ShapeDtypeStruct((T, D_MODEL), BF16), jax.ShapeDtypeStruct((B * MEM_LEN, 2 * D_MODEL), F32)],
        compiler_params=_cparams("parallel", "arbitrary"),
    )(q, kv, kv, do)


def _chunk_masks():
    row = lax.broadcasted_iota(jnp.int32, (A_SUPER, A_SUPER), 0)
    col = lax.broadcasted_iota(jnp.int32, (A_SUPER, A_SUPER), 1)
    same = jnp.right_shift(row, 5) == jnp.right_shift(col, 5)
    return same, same & (col <= row), same & (col >= row)


def _dot_mask(mask, x):
    m = mask.astype(BF16)
    hi = x.astype(BF16)
    rest = x - hi.astype(F32)
    mid = rest.astype(BF16)
    lo = (rest - mid.astype(F32)).astype(BF16)
    return _dot(m, hi) + _dot(m, mid) + _dot(m, lo)


def _chunk_row(x, which):
    rows = [x[c * A_CHUNK + which % A_CHUNK:c * A_CHUNK + which % A_CHUNK + 1, :] for c in range(A_SUPER // A_CHUNK)]
    return jnp.concatenate([jnp.broadcast_to(r, (A_CHUNK, x.shape[1])) for r in rows], axis=0)


def _hgrn_gates(fa, lb):
    sig = _sigmoid(fa)
    f = lb + (1.0 - lb) * sig
    return sig, f, jnp.log(f), 1.0 - f


def _hgrn_fwd(z, lb, onw, B, L, *, name, gather=()):
    T = B * L
    ns = L // A_SUPER
    nch = A_SUPER // A_CHUNK
    ex = _Exchange(gather=gather)
    grid = (B, 4, ns)

    def body(q_ref, f_ref, v_ref, g_ref, lb_ref, w_ref, oa_ref, o_ref, s_ref, st_ref, sc_ref):
        @pl.when(pl.program_id(2) == 0)
        def _():
            st_ref[...] = jnp.zeros_like(st_ref)

        s_ref[0] = st_ref[...]
        same, tril, _ = _chunk_masks()
        q, v = q_ref[...], v_ref[...]
        _, _, lf, k = _hgrn_gates(f_ref[...], lb_ref[...])
        bcs = _dot_mask(tril, lf)
        bl = _chunk_row(bcs, -1)
        qd = (q * jnp.exp(bcs)).astype(BF16)
        ki = (k * jnp.exp(-bcs)).astype(BF16)
        ke = (k * jnp.exp(bl - bcs)).astype(BF16)
        dec = jnp.exp(bl)
        vb = v.astype(BF16)
        a = jnp.where(tril, _dot(qd, ki, _NT), 0.0)
        o_ref[...] = _dot(a.astype(BF16), vb)
        chunks = [slice(c * A_CHUNK, (c + 1) * A_CHUNK) for c in range(nch)]
        outer = [_dot(vb[rs], ke[rs], _TN) for rs in chunks]
        st = st_ref[...]
        for c, rs in enumerate(chunks):
            sc_ref[c] = st.astype(BF16)
            st = st * dec[c * A_CHUNK:c * A_CHUNK + 1, :] + outer[c]
        st_ref[...] = st
        for c, rs in enumerate(chunks):
            o_ref[rs, :] += _dot(qd[rs], sc_ref[c], _NT)
        o = o_ref[...]
        r = lax.rsqrt(jnp.mean(o * o, axis=-1, keepdims=True) + NORM_EPS)
        g = g_ref[...]
        oa_ref[...] = (o * r * w_ref[...] * (g * _sigmoid(g))).astype(BF16)

    def zspec(off):
        return pl.BlockSpec((A_SUPER, A_HEAD), lambda b, h, n: (b * ns + n, off + h))

    hvec = pl.BlockSpec((1, A_HEAD), lambda b, h, n: (0, h))
    ospec = pl.BlockSpec((A_SUPER, A_HEAD), lambda b, h, n: (b * ns + n, h))
    res = pl.pallas_call(
        _carrying(body, 6, 3, ex, grid) if ex.n else body, name=name, grid=grid,
        in_specs=[zspec(0), zspec(4), zspec(8), zspec(12), hvec, hvec] + ex.in_specs(),
        out_specs=[ospec, ospec, pl.BlockSpec((1, A_HEAD, A_HEAD), lambda b, h, n: ((b * 4 + h) * ns + n, 0, 0))] + ex.in_specs(),
        out_shape=[jax.ShapeDtypeStruct((T, A_WIDTH), BF16), jax.ShapeDtypeStruct((T, A_WIDTH), F32),
                   jax.ShapeDtypeStruct((B * 4 * ns, A_HEAD, A_HEAD), F32)] + ex.out_shapes(),
        scratch_shapes=[pltpu.VMEM((A_HEAD, A_HEAD), F32), pltpu.VMEM((nch, A_HEAD, A_HEAD), BF16)] + ex.scratch(),
        compiler_params=_cparams(*(("arbitrary",) * 3 if ex.n else ("parallel", "parallel", "arbitrary"))),
    )(z, z, z, z, lb, onw, *ex.operands())
    return tuple(res[:3]) + (list(res[3:]),)


def _hgrn_bwd(z, lb, onw, o_raw, s_start, doa, B, L, *, name, scatter=()):
    T = B * L
    ns = L // A_SUPER
    nch = A_SUPER // A_CHUNK
    ex = _Exchange(scatter=scatter)
    grid = (B, 4, ns)

    def body(q_ref, f_ref, v_ref, g_ref, lb_ref, w_ref, o_ref, s_ref, doa_ref,
             dq_ref, df_ref, dv_ref, dg_ref, dw_ref, dlb_ref, dst_ref, sc_ref, dsc_ref, dqd_ref, dke_ref, dblx_ref, dvacc_ref):
        @pl.when(pl.program_id(2) == 0)
        def _():
            dst_ref[...] = jnp.zeros_like(dst_ref)
            dw_ref[...] = jnp.zeros_like(dw_ref)
            dlb_ref[...] = jnp.zeros_like(dlb_ref)

        same, tril, triu = _chunk_masks()
        q, v, g, lb, w = q_ref[...], v_ref[...], g_ref[...], lb_ref[...], w_ref[...]
        sig, f, lf, k = _hgrn_gates(f_ref[...], lb)
        bcs = _dot_mask(tril, lf)
        bl = _chunk_row(bcs, -1)
        eb, enb, eeb = jnp.exp(bcs), jnp.exp(-bcs), jnp.exp(bl - bcs)
        qd, ki, ke = q * eb, k * enb, k * eeb
        qdb, kib, keb, vb = qd.astype(BF16), ki.astype(BF16), ke.astype(BF16), v.astype(BF16)
        dec = jnp.exp(bl)
        o = o_ref[...]
        r = lax.rsqrt(jnp.mean(o * o, axis=-1, keepdims=True) + NORM_EPS)
        on = o * r
        sg = _sigmoid(g)
        silu_g = g * sg
        doa = doa_ref[...]
        dg_ref[...] = (doa * on * w * (sg * (1.0 + g * (1.0 - sg)))).astype(BF16)
        dw_ref[0] += jnp.sum(doa * on * silu_g, axis=0, keepdims=True)
        don = doa * w * silu_g
        do = r * (don - on * jnp.mean(don * on, axis=-1, keepdims=True))
        dob = do.astype(BF16)
        a = jnp.where(tril, _dot(qdb, kib, _NT), 0.0).astype(BF16)
        da = jnp.where(tril, _dot(dob, vb, _NT), 0.0).astype(BF16)
        dvacc_ref[...] = _dot(a, dob, _TN)
        dqd_ref[...] = _dot(da, kib)
        dki = _dot(da, qdb, _TN)
        chunks = [slice(c * A_CHUNK, (c + 1) * A_CHUNK) for c in range(nch)]
        outer = [_dot(vb[rs], keb[rs], _TN) for rs in chunks]
        st = s_ref[0]
        for c in range(nch):
            sc_ref[c] = st
            st = st * dec[c * A_CHUNK:c * A_CHUNK + 1, :] + outer[c]
        outer_g = [_dot(dob[rs], qdb[rs], _TN) for rs in chunks]
        dst = dst_ref[...]
        for c in reversed(range(nch)):
            dsc_ref[c] = dst
            dst = dst * dec[c * A_CHUNK:c * A_CHUNK + 1, :] + outer_g[c]
        dst_ref[...] = dst
        for c, rs in enumerate(chunks):
            dec_c = dec[c * A_CHUNK:c * A_CHUNK + 1, :]
            dsc, stc = dsc_ref[c], sc_ref[c]
            dscb = dsc.astype(BF16)
            dvacc_ref[rs, :] += _dot(keb[rs], dscb, _NT)
            dke_ref[rs, :] = _dot(vb[rs], dscb)
            ddec = jnp.sum(dsc * stc, axis=0, keepdims=True)
            dqd_ref[rs, :] += _dot(dob[rs], stc.astype(BF16))
            dblx_ref[rs, :] = jnp.broadcast_to(ddec * dec_c, (A_CHUNK, A_HEAD))
        dqd, dke = dqd_ref[...], dke_ref[...]
        dv_ref[...] = dvacc_ref[...].astype(BF16)
        dq_ref[...] = (dqd * eb).astype(BF16)
        keke = dke * ke
        db = dqd * qd - dki * ki - keke
        sums = _dot_mask(triu, jnp.concatenate([db, keke], axis=1))
        dk = dki * enb + dke * eeb
        dlf = sums[:, :A_HEAD] + _chunk_row(sums[:, A_HEAD:], 0) + dblx_ref[...]
        dff = dlf / f - dk
        df_ref[...] = (dff * (1.0 - lb) * sig * (1.0 - sig)).astype(BF16)
        dlb_ref[0] += jnp.sum(dff * (1.0 - sig), axis=0, keepdims=True)

    def rev(n):
        return ns - 1 - n

    def zspec(off):
        return pl.BlockSpec((A_SUPER, A_HEAD), lambda b, h, n: (b * ns + rev(n), off + h))

    hvec = pl.BlockSpec((1, A_HEAD), lambda b, h, n: (0, h))
    ospec = pl.BlockSpec((A_SUPER, A_HEAD), lambda b, h, n: (b * ns + rev(n), h))
    acc = pl.BlockSpec((1, 1, A_HEAD), lambda b, h, n: (b * 4 + h, 0, 0))
    big = jax.ShapeDtypeStruct((T, A_WIDTH), BF16)
    small = jax.ShapeDtypeStruct((B * 4, 1, A_HEAD), F32)
    res = pl.pallas_call(
        _carrying(body, 9, 6, ex, grid) if ex.n else body, name=name, grid=grid,
        in_specs=[zspec(0), zspec(4), zspec(8), zspec(12), hvec, hvec, ospec,
                  pl.BlockSpec((1, A_HEAD, A_HEAD), lambda b, h, n: ((b * 4 + h) * ns + rev(n), 0, 0)), ospec] + ex.in_specs(),
        out_specs=[ospec, ospec, ospec, ospec, acc, acc] + ex.in_specs(),
        out_shape=[big, big, big, big, small, small] + ex.out_shapes(),
        scratch_shapes=[pltpu.VMEM((A_HEAD, A_HEAD), F32), pltpu.VMEM((nch, A_HEAD, A_HEAD), F32),
                        pltpu.VMEM((nch, A_HEAD, A_HEAD), F32),
                        pltpu.VMEM((A_SUPER, A_HEAD), F32), pltpu.VMEM((A_SUPER, A_HEAD), F32),
                        pltpu.VMEM((A_SUPER, A_HEAD), F32), pltpu.VMEM((A_SUPER, A_HEAD), F32)] + ex.scratch(),
        compiler_params=_cparams(*(("arbitrary",) * 3 if ex.n else ("parallel", "parallel", "arbitrary"))),
    )(z, z, z, z, lb, onw, o_raw, s_start, doa, *ex.operands())
    return tuple(res[:6]) + (list(res[6:]),)


def _rope_tables(L):
    half = A_HEAD // 2
    inv_freq = ROPE_THETA ** (-jnp.arange(half, dtype=F32) / half)
    ang = jnp.arange(L, dtype=F32)[:, None] * inv_freq[None, :]
    cos, sin = jnp.cos(ang), jnp.sin(ang)
    return jnp.concatenate([cos, cos], axis=-1), jnp.concatenate([-sin, sin], axis=-1)


def _rope_fwd(z, cos2, sin2, B, L, *, name):
    T = B * L
    tm = 512
    nl = L // tm

    def body(x_ref, c_ref, s_ref, q_ref, k_ref):
        c, s = c_ref[...], s_ref[...]
        for h in range(8):
            x = x_ref[:, h * A_HEAD:(h + 1) * A_HEAD]
            out = x * c + pltpu.roll(x, A_HEAD // 2, 1) * s
            o_ref = q_ref if h < 4 else k_ref
            o_ref[:, (h % 4) * A_HEAD:(h % 4 + 1) * A_HEAD] = out

    tab = pl.BlockSpec((tm, A_HEAD), lambda i: (i % nl, 0))
    out = pl.BlockSpec((tm, 512), lambda i: (i, 0))
    return pl.pallas_call(
        body, name=name, grid=(T // tm,),
        in_specs=[pl.BlockSpec((tm, 1024), lambda i: (i, 2)), tab, tab], out_specs=[out, out],
        out_shape=[jax.ShapeDtypeStruct((T, 512), F32)] * 2, compiler_params=_cparams("parallel"),
    )(z, cos2, sin2)


def _rope_bwd(dqs, dks, dvs, cos2, sin2, B, L, *, name):
    T = B * L
    tm = 512
    nl = L // tm

    def body(*refs):
        c, s = refs[9][...], refs[10][...]
        o_ref = refs[11]
        for part in range(3):
            a_ref, b_ref, c_ref = refs[3 * part:3 * part + 3]
            for h in range(4):
                cols = slice(h * A_HEAD, (h + 1) * A_HEAD)
                d = a_ref[:, cols] + b_ref[:, cols] + c_ref[:, cols]
                if part < 2:
                    d = d * c - pltpu.roll(d, A_HEAD // 2, 1) * s
                o_ref[:, part * 512 + h * A_HEAD:part * 512 + (h + 1) * A_HEAD] = d.astype(BF16)

    blk = pl.BlockSpec((tm, 512), lambda i: (i, 0))
    tab = pl.BlockSpec((tm, A_HEAD), lambda i: (i % nl, 0))
    return pl.pallas_call(
        body, name=name, grid=(T // tm,), in_specs=[blk] * 9 + [tab, tab],
        out_specs=pl.BlockSpec((tm, 1536), lambda i: (i, 0)),
        out_shape=jax.ShapeDtypeStruct((T, 1536), BF16), compiler_params=_cparams("parallel"),
    )(*dqs, *dks, *dvs, cos2, sin2)


def _band_masks():
    i = lax.broadcasted_iota(jnp.int32, (B_SPAN, B_SPAN), 0)
    j = lax.broadcasted_iota(jnp.int32, (B_SPAN, B_SPAN), 1)
    return i <= j, j <= i


class _DilPlan:
    def __init__(self, dil, B, L):
        self.dil, self.B, self.L = dil, B, L
        self.rows = 4 * B_SPAN if dil == 1 else B_SPAN * dil
        self.n = L // self.rows
        self.hr = 4 if dil == 4 else 1
        self.cw = 512 if dil == 1 else A_HEAD
        self.has_other = dil != 16
        self.grid = (B, self.n, 4 if dil == 16 else 1)
        if dil == 1:
            self.items = [(j, 0, h) for j in range(4) for h in range(4)]
        elif dil == 4:
            self.items = [(0, r, h) for r in range(4) for h in range(4)]
        else:
            self.items = [(0, r, 0) for r in range(16)]
        self.groups = [self.items[i:i + 4] for i in range(0, 16, 4)]

    def operands(self, arr):
        return [arr] * self.hr

    def specs(self, col0, role="cur"):
        n, nb128 = self.n, self.L // B_SPAN
        out = []
        for h in range(self.hr):
            cb = col0 // self.cw + h
            if role == "cur":
                out.append(pl.BlockSpec((self.rows, self.cw), lambda b, i, hh, cb=cb: (b * n + i, cb + hh)))
            elif self.dil == 1:
                shift = -1 if role == "prev" else 4
                out.append(pl.BlockSpec((B_SPAN, self.cw),
                                        lambda b, i, hh, cb=cb, shift=shift: (b * nb128 + jnp.clip(4 * i + shift, 0, nb128 - 1), cb)))
            else:
                shift = -1 if role == "prev" else 1
                out.append(pl.BlockSpec((self.rows, self.cw),
                                        lambda b, i, hh, cb=cb, shift=shift: (b * n + jnp.clip(i + shift, 0, n - 1), cb)))
        return out

    def out_spec(self):
        n = self.n
        if self.dil == 4:
            return pl.BlockSpec((self.rows, 512), lambda b, i, hh: (b * n + i, 0))
        return pl.BlockSpec((self.rows, self.cw), lambda b, i, hh: (b * n + i, hh))

    def scratch(self, n_out):
        return [pltpu.VMEM((4, self.rows, A_HEAD), F32)] * n_out if self.dil == 4 else []

    def store(self, out_ref, scr, item, val):
        j, r, h = item
        if self.dil == 1:
            out_ref[pl.ds(j * B_SPAN, B_SPAN), pl.ds(h * A_HEAD, A_HEAD)] = val
        elif self.dil == 4:
            scr.at[h][pl.ds(r, B_SPAN, stride=4), :] = val
        else:
            out_ref[pl.ds(r, B_SPAN, stride=self.dil), :] = val

    def flush(self, out_ref, scr):
        if self.dil == 4:
            for h in range(4):
                out_ref[:, h * A_HEAD:(h + 1) * A_HEAD] = scr[h]

    def cur(self, refs, item):
        j, r, h = item
        if self.dil == 1:
            return refs[0], (pl.ds(j * B_SPAN, B_SPAN), pl.ds(h * A_HEAD, A_HEAD))
        return refs[h], (pl.ds(r, B_SPAN, stride=self.dil), slice(None))

    def other(self, refs, other_refs, item, role):
        j, r, h = item
        if self.dil == 1:
            cols = pl.ds(h * A_HEAD, A_HEAD)
            jj = j - 1 if role == "prev" else j + 1
            if 0 <= jj < 4:
                return refs[0], (pl.ds(jj * B_SPAN, B_SPAN), cols)
            return other_refs[0], (pl.ds(0, B_SPAN), cols)
        return other_refs[h], (pl.ds(r, B_SPAN, stride=self.dil), slice(None))

    def other_valid(self, item, role):
        j = item[0]
        i = pl.program_id(1)
        if role == "prev":
            return True if (self.dil == 1 and j > 0) else i > 0
        return True if (self.dil == 1 and j < 3) else i < self.n - 1


def _ld(pair):
    ref, idx = pair
    return ref[idx]


def _dil_fwd(qr, kr, z, dil, B, L, *, name):
    T = B * L
    plan = _DilPlan(dil, B, L)
    hr, has_prev = plan.hr, plan.has_other
    scale = A_HEAD ** -0.5

    n_t = 5 if has_prev else 3

    def body(*refs):
        lists = [refs[i * hr:(i + 1) * hr] for i in range(n_t)]
        o_ref, l_ref = refs[n_t * hr], refs[n_t * hr + 1]
        scr = refs[n_t * hr + 2:] or (None, None)
        if has_prev:
            q_r, kc_r, vc_r, kp_r, vp_r = lists
        else:
            q_r, kc_r, vc_r = lists
        mp0, mc = _band_masks()

        for group in plan.groups:
            qs = [_ld(plan.cur(q_r, i)).astype(BF16) for i in group]
            sc = [jnp.where(mc, _dot(q, _ld(plan.cur(kc_r, i)).astype(BF16), _NT) * scale, NEG_BIG) for q, i in zip(qs, group)]
            m = [jnp.max(s, axis=-1, keepdims=True) for s in sc]
            if has_prev:
                mps = [mp0 & plan.other_valid(i, "prev") for i in group]
                sp = [jnp.where(mk, _dot(q, _ld(plan.other(kc_r, kp_r, i, "prev")).astype(BF16), _NT) * scale, NEG_BIG)
                      for q, i, mk in zip(qs, group, mps)]
                m = [jnp.maximum(a, jnp.max(s, axis=-1, keepdims=True)) for a, s in zip(m, sp)]
            pc = [jnp.exp(s - a) for s, a in zip(sc, m)]
            l = [jnp.sum(p, axis=-1, keepdims=True) for p in pc]
            o = [_dot(p.astype(BF16), _ld(plan.cur(vc_r, i)).astype(BF16)) for p, i in zip(pc, group)]
            if has_prev:
                pp = [jnp.exp(s - a) for s, a in zip(sp, m)]
                l = [a + jnp.sum(p, axis=-1, keepdims=True) for a, p in zip(l, pp)]
                o = [a + _dot(p.astype(BF16), _ld(plan.other(vc_r, vp_r, i, "prev")).astype(BF16)) for a, p, i in zip(o, pp, group)]
            for i, oi, li, mi in zip(group, o, l, m):
                plan.store(o_ref, scr[0], i, oi / li)
                plan.store(l_ref, scr[1], i, jnp.broadcast_to(mi + jnp.log(li), (B_SPAN, A_HEAD)))
        plan.flush(o_ref, scr[0])
        plan.flush(l_ref, scr[1])

    tensors = [(qr, 0, "cur"), (kr, 0, "cur"), (z, 3072, "cur")] + ([(kr, 0, "prev"), (z, 3072, "prev")] if has_prev else [])
    return pl.pallas_call(
        body, name=name, grid=plan.grid,
        in_specs=[sp for _, c, role in tensors for sp in plan.specs(c, role)],
        out_specs=[plan.out_spec()] * 2, out_shape=[jax.ShapeDtypeStruct((T, 512), F32)] * 2,
        scratch_shapes=plan.scratch(2), compiler_params=_cparams("parallel", "parallel", "parallel"),
    )(*[a for arr, _, _ in tensors for a in plan.operands(arr)])


def _dil_combine(os_, ls_, *, name):
    T = os_[0].shape[0]
    tm = 512

    def body(o1, o2, o3, l1, l2, l3, ob_ref, lse_ref):
        a1, a2, a3 = l1[...], l2[...], l3[...]
        m = jnp.maximum(jnp.maximum(a1, a2), a3)
        e1, e2, e3 = jnp.exp(a1 - m), jnp.exp(a2 - m), jnp.exp(a3 - m)
        den = e1 + e2 + e3
        ob_ref[...] = (e1 * o1[...] + e2 * o2[...] + e3 * o3[...]) / den
        lse_ref[...] = m + jnp.log(den)

    blk = pl.BlockSpec((tm, 512), lambda i: (i, 0))
    return pl.pallas_call(
        body, name=name, grid=(T // tm,), in_specs=[blk] * 6, out_specs=[blk, blk],
        out_shape=[jax.ShapeDtypeStruct((T, 512), F32)] * 2, compiler_params=_cparams("parallel"),
    )(*[o.reshape(T, 512) for o in os_], *[l.reshape(T, 512) for l in ls_])


def _dil_bwd_q(qr, kr, z, dymix, out, lse, dil, B, L, *, name):
    T = B * L
    plan = _DilPlan(dil, B, L)
    hr, has_prev = plan.hr, plan.has_other
    scale = A_HEAD ** -0.5
    n_t = 8 if has_prev else 6

    def body(*refs):
        lists = [refs[i * hr:(i + 1) * hr] for i in range(n_t)]
        dq_ref = refs[n_t * hr]
        scr = refs[n_t * hr + 1:] or (None,)
        q_r, kc_r, vc_r, do_r, out_r, lse_r = lists[:6]
        mp0, mc = _band_masks()

        for group in plan.groups:
            qs = [_ld(plan.cur(q_r, i)).astype(BF16) for i in group]
            dos = [_ld(plan.cur(do_r, i)) for i in group]
            delta = [jnp.sum(d * _ld(plan.cur(out_r, i)), axis=-1, keepdims=True) for d, i in zip(dos, group)]
            dob = [d.astype(BF16) for d in dos]
            lse = [_ld(plan.cur(lse_r, i)) for i in group]
            kc = [_ld(plan.cur(kc_r, i)).astype(BF16) for i in group]
            pc = [jnp.where(mc, jnp.exp(_dot(q, k, _NT) * scale - a), 0.0) for q, k, a in zip(qs, kc, lse)]
            dsc = [p * (_dot(d, _ld(plan.cur(vc_r, i)).astype(BF16), _NT) - dl) * scale
                   for p, d, i, dl in zip(pc, dob, group, delta)]
            dq = [_dot(d.astype(BF16), k) for d, k in zip(dsc, kc)]
            if has_prev:
                kp_r, vp_r = lists[6], lists[7]
                mps = [mp0 & plan.other_valid(i, "prev") for i in group]
                kp = [_ld(plan.other(kc_r, kp_r, i, "prev")).astype(BF16) for i in group]
                pp = [jnp.where(mk, jnp.exp(_dot(q, k, _NT) * scale - a), 0.0) for q, k, a, mk in zip(qs, kp, lse, mps)]
                dsp = [p * (_dot(d, _ld(plan.other(vc_r, vp_r, i, "prev")).astype(BF16), _NT) - dl) * scale
                       for p, d, i, dl in zip(pp, dob, group, delta)]
                dq = [a + _dot(d.astype(BF16), k) for a, d, k in zip(dq, dsp, kp)]
            for i, d in zip(group, dq):
                plan.store(dq_ref, scr[0], i, d)
        plan.flush(dq_ref, scr[0])

    tensors = ([(qr, 0, "cur"), (kr, 0, "cur"), (z, 3072, "cur"), (dymix, 512, "cur"), (out, 0, "cur"), (lse, 0, "cur")]
               + ([(kr, 0, "prev"), (z, 3072, "prev")] if has_prev else []))
    return pl.pallas_call(
        body, name=name, grid=plan.grid,
        in_specs=[sp for _, c, role in tensors for sp in plan.specs(c, role)],
        out_specs=plan.out_spec(), out_shape=jax.ShapeDtypeStruct((T, 512), F32),
        scratch_shapes=plan.scratch(1), compiler_params=_cparams("parallel", "parallel", "parallel"),
    )(*[a for arr, _, _ in tensors for a in plan.operands(arr)])


def _dil_bwd_kv(qr, kr, z, dymix, out, lse, dil, B, L, *, name):
    T = B * L
    plan = _DilPlan(dil, B, L)
    hr, has_next = plan.hr, plan.has_other
    scale = A_HEAD ** -0.5
    n_t = 10 if has_next else 6

    def body(*refs):
        lists = [refs[i * hr:(i + 1) * hr] for i in range(n_t)]
        dk_ref, dv_ref = refs[n_t * hr], refs[n_t * hr + 1]
        scr = refs[n_t * hr + 2:] or (None, None)
        k_r, v_r = lists[0], lists[1]
        own = lists[2:6]
        mp0, mc = _band_masks()

        for group in plan.groups:
            kh = [_ld(plan.cur(k_r, i)).astype(BF16) for i in group]
            vh = [_ld(plan.cur(v_r, i)).astype(BF16) for i in group]
            dk, dv = [None] * len(group), [None] * len(group)
            for role in ("own", "next") if has_next else ("own",):
                if role == "own":
                    get = lambda t, i: _ld(plan.cur(own[t], i))
                    masks = [mc] * len(group)
                else:
                    get = lambda t, i: _ld(plan.other(own[t], lists[6 + t], i, "next"))
                    masks = [mp0 & plan.other_valid(i, "next") for i in group]
                qs = [get(0, i).astype(BF16) for i in group]
                dos = [get(1, i) for i in group]
                delta = [jnp.sum(d * get(2, i), axis=-1, keepdims=True) for d, i in zip(dos, group)]
                dob = [d.astype(BF16) for d in dos]
                p = [jnp.where(mk, jnp.exp(_dot(q, k, _NT) * scale - get(3, i)), 0.0) for q, k, i, mk in zip(qs, kh, group, masks)]
                dvn = [_dot(a.astype(BF16), d, _TN) for a, d in zip(p, dob)]
                ds = [a * (_dot(d, v, _NT) - dl) * scale for a, d, v, dl in zip(p, dob, vh, delta)]
                dkn = [_dot(d.astype(BF16), q, _TN) for d, q in zip(ds, qs)]
                dv = [n if o is None else o + n for o, n in zip(dv, dvn)]
                dk = [n if o is None else o + n for o, n in zip(dk, dkn)]
            for i, a, b in zip(group, dk, dv):
                plan.store(dk_ref, scr[0], i, a)
                plan.store(dv_ref, scr[1], i, b)
        plan.flush(dk_ref, scr[0])
        plan.flush(dv_ref, scr[1])

    queries = [(qr, 0), (dymix, 512), (out, 0), (lse, 0)]
    tensors = ([(kr, 0, "cur"), (z, 3072, "cur")] + [(a, c, "cur") for a, c in queries]
               + ([(a, c, "next") for a, c in queries] if has_next else []))
    return pl.pallas_call(
        body, name=name, grid=plan.grid,
        in_specs=[sp for _, c, role in tensors for sp in plan.specs(c, role)],
        out_specs=[plan.out_spec()] * 2, out_shape=[jax.ShapeDtypeStruct((T, 512), F32)] * 2,
        scratch_shapes=plan.scratch(2), compiler_params=_cparams("parallel", "parallel", "parallel"),
    )(*[a for arr, _, _ in tensors for a in plan.operands(arr)])


def _s5_build(lam_re, lam_im, log_dt, b_re, b_im, c_re, c_im):
    G, P, TC = C_GROUPS, C_STATE, C_TC
    lr = jnp.minimum(lam_re, C_MIN_NEG_RE)
    li = lam_im
    dt = jnp.exp(log_dt)[:, None]
    mag = jnp.exp(dt * lr)
    ar, ai = mag * jnp.cos(dt * li), mag * jnp.sin(dt * li)
    den = lr * lr + li * li
    zr = ((ar - 1.0) * lr + ai * li) / den
    zi = (ai * lr - (ar - 1.0) * li) / den
    bbr = zr[..., None] * b_re - zi[..., None] * b_im
    bbi = zr[..., None] * b_im + zi[..., None] * b_re
    ks = jnp.arange(TC + 1, dtype=F32)[:, None, None]
    pmag = jnp.exp(ks * (dt * lr)[None])
    pr, pi = pmag * jnp.cos(ks * (dt * li)[None]), pmag * jnp.sin(ks * (dt * li)[None])
    car = c_re[None] * pr[:, :, None, :] - c_im[None] * pi[:, :, None, :]
    cai = c_re[None] * pi[:, :, None, :] + c_im[None] * pr[:, :, None, :]
    kern = (jnp.einsum('lgop,gpc->lgco', car[:TC], bbr, precision=HI)
            - jnp.einsum('lgop,gpc->lgco', cai[:TC], bbi, precision=HI))
    pr_e, pi_e = pr[TC - 1 - jnp.arange(TC)], pi[TC - 1 - jnp.arange(TC)]
    er = pr_e[:, :, :, None] * bbr[None] - pi_e[:, :, :, None] * bbi[None]
    ei = pr_e[:, :, :, None] * bbi[None] + pi_e[:, :, :, None] * bbr[None]
    ez = jnp.stack([er, ei], axis=2).reshape(TC, C_NB, C_GB, 2, P, C_GROUP).transpose(1, 0, 2, 5, 3, 4)
    fz = jnp.stack([car[1:], -cai[1:]], axis=0).reshape(2, TC, C_NB, C_GB, C_GROUP, P).transpose(2, 0, 3, 5, 1, 4)
    return kern, ez, fz, pr[TC], pi[TC]


def _s5_lag_blocks(kern):
    eye = jnp.eye(C_GB, dtype=kern.dtype)
    return (kern.reshape(C_TC, C_NB, C_GB, C_GROUP, C_GROUP)[:, :, :, :, None, :]
            * eye[None, None, :, None, :, None]).reshape(C_TC, C_NB, LANES, LANES)


def _s5_dense(kbd, ez, fz, *, name):
    ezc = ez.reshape(C_NB, C_W8, LANES)
    fzc = fz.reshape(C_NB, C_S8, LANES)
    half = C_GB * C_STATE

    def body(k_ref, e_ref, f_ref, m8_ref, e8_ref, f8_ref):
        zero = jnp.zeros((LANES, LANES), BF16)
        for s in range(C_TC):
            for t in range(C_TC):
                m8_ref[0, s * LANES:(s + 1) * LANES, t * LANES:(t + 1) * LANES] = (
                    k_ref[t - s, 0].astype(BF16) if t >= s else zero)
        lane = lax.broadcasted_iota(jnp.int32, (1, LANES), 1)
        src = e_ref[0]
        swapped = pltpu.roll(src, C_STATE, 1)
        rowg = jnp.bitwise_and(jnp.right_shift(lax.broadcasted_iota(jnp.int32, (C_W8, LANES), 0), 4), C_GB - 1)
        for kb in range(C_S8 // LANES):
            z, g0 = kb // (C_GB // 2), 2 * (kb % (C_GB // 2))
            first = jnp.where(lane < C_STATE, src if z == 0 else swapped, 0.0)
            second = jnp.where(lane >= C_STATE, swapped if z == 0 else src, 0.0)
            e8_ref[0, :, kb * LANES:(kb + 1) * LANES] = jnp.where(
                rowg == g0, first, jnp.where(rowg == g0 + 1, second, 0.0)).astype(BF16)
        for z in range(2):
            for g in range(C_GB):
                rows = slice(z * half + g * C_STATE, z * half + (g + 1) * C_STATE)
                piece = f_ref[0, rows, :]
                mine = (lane >= g * C_GROUP) & (lane < (g + 1) * C_GROUP)
                for t in range(C_TC):
                    f8_ref[0, rows, t * LANES:(t + 1) * LANES] = jnp.where(
                        mine, pltpu.roll(piece, ((g - t) * C_GROUP) % LANES, 1), 0.0).astype(BF16)

    blk = lambda r, c: pl.BlockSpec((1, r, c), lambda b: (b, 0, 0))
    return pl.pallas_call(
        body, name=name, grid=(C_NB,),
        in_specs=[pl.BlockSpec((C_TC, 1, LANES, LANES), lambda b: (0, b, 0, 0)), blk(C_W8, LANES), blk(C_S8, LANES)],
        out_specs=[blk(C_W8, C_W8), blk(C_W8, C_S8), blk(C_S8, C_W8)],
        out_shape=[jax.ShapeDtypeStruct((C_NB, C_W8, C_W8), BF16), jax.ShapeDtypeStruct((C_NB, C_W8, C_S8), BF16),
                   jax.ShapeDtypeStruct((C_NB, C_S8, C_W8), BF16)],
        compiler_params=_cparams("parallel"),
    )(kbd, ezc, fzc)


C_NB = C_GROUPS * C_GROUP // LANES
C_GB = C_GROUPS // C_NB
C_W8 = C_TC * LANES
C_S8 = 2 * C_GB * C_STATE


def _s5_scan_tables(lam_re, lam_im, log_dt, nsteps):
    lr = jnp.minimum(lam_re, C_MIN_NEG_RE)
    dt = jnp.exp(log_dt)[:, None]
    ks = (C_TC * 2.0 ** jnp.arange(8, dtype=F32))[None, :, None]
    keep = (jnp.arange(8) < nsteps)[None, :, None]
    pmag = jnp.exp(ks * (dt * lr)[:, None, :])
    ang = ks * (dt * lam_im)[:, None, :]

    def blocks(t):
        return t.reshape(C_NB, C_GB, 8, C_STATE).transpose(0, 2, 1, 3).reshape(C_NB, 8, C_GB * C_STATE)

    pr = blocks(jnp.where(keep, pmag * jnp.cos(ang), 0.0))
    pi = blocks(jnp.where(keep, pmag * jnp.sin(ang), 0.0))
    return jnp.concatenate([pr, pr], axis=-1), jnp.concatenate([-pi, pi], axis=-1)


def _s5_rows(t, R):
    return pl.ds(t, R, stride=C_TC)


def _s5_fwd(u, dsk, m8, e8, f8, tab_r, tab_i, B, L, *, name, gather=()):
    T = B * L
    R = L // C_TC
    nsteps = int(math.log2(R))
    ex = _Exchange(gather=gather)
    grid = (C_NB, B)

    def body(u_ref, d_ref, m_ref, e_ref, f_ref, tr_ref, ti_ref, gl_ref, y_ref, x8_ref, xs_ref):
        for t in range(C_TC):
            x8_ref[0, :, t * LANES:(t + 1) * LANES] = u_ref[_s5_rows(t, R), :].astype(BF16)
        x8 = x8_ref[0]
        x = _dot(x8, e_ref[0])
        row = lax.broadcasted_iota(jnp.int32, (R, C_S8), 0)
        for k in range(nsteps):
            s = 1 << k
            sh = pltpu.roll(x, s, 0)
            upd = tr_ref[0, k:k + 1, :] * sh + ti_ref[0, k:k + 1, :] * pltpu.roll(sh, C_S8 // 2, 1)
            x = x + jnp.where(row >= s, upd, 0.0)
        xs = jnp.where(row >= 1, pltpu.roll(x, 1, 0), 0.0)
        xs_ref[0] = xs
        y8 = _dot(x8, m_ref[0]) + _dot(xs.astype(BF16), f_ref[0])
        d = d_ref[...]
        for t in range(C_TC):
            rows = _s5_rows(t, R)
            y = y8[:, t * LANES:(t + 1) * LANES] + d * u_ref[rows, :]
            y_ref[rows, :] = y
            gl_ref[rows, :] = 0.5 * y * (1.0 + _erf(y * (2.0 ** -0.5)))

    tok = pl.BlockSpec((L, LANES), lambda c, b: (b, c))
    per_block = lambda shape: pl.BlockSpec((1,) + shape, lambda c, b: (c, 0, 0))
    per_step = lambda shape: pl.BlockSpec((1,) + shape, lambda c, b: (c * B + b, 0, 0))
    res = pl.pallas_call(
        _carrying(body, 7, 4, ex, grid) if ex.n else body, name=name, grid=grid,
        in_specs=[tok, pl.BlockSpec((1, LANES), lambda c, b: (0, c)), per_block((C_W8, C_W8)), per_block((C_W8, C_S8)),
                  per_block((C_S8, C_W8)), per_block((8, C_S8)), per_block((8, C_S8))] + ex.in_specs(),
        out_specs=[tok, tok, per_step((R, C_W8)), per_step((R, C_S8))] + ex.in_specs(),
        out_shape=[jax.ShapeDtypeStruct((T, D_MODEL), F32), jax.ShapeDtypeStruct((T, D_MODEL), F32),
                   jax.ShapeDtypeStruct((C_NB * B, R, C_W8), BF16), jax.ShapeDtypeStruct((C_NB * B, R, C_S8), F32)] + ex.out_shapes(),
        scratch_shapes=ex.scratch(),
        compiler_params=_cparams(*(("arbitrary",) * 2 if ex.n else ("parallel", "parallel"))),
    )(u, dsk, m8, e8, f8, tab_r, tab_i, *ex.operands())
    return tuple(res[:4]) + (list(res[4:]),)


def _s5_bwd(dgl, y, u, dsk, xs, m8, e8, f8, tab_r, tab_i, B, L, *, name):
    T = B * L
    R = L // C_TC
    nsteps = int(math.log2(R))

    def body(dgl_ref, y_ref, u_ref, d_ref, xs_ref, m_ref, e_ref, f_ref, tr_ref, ti_ref,
             du_ref, dy8_ref, de_ref, da_ref, dd_ref, dyf_ref):
        @pl.when(pl.program_id(1) == 0)
        def _():
            da_ref[...] = jnp.zeros_like(da_ref)
            dd_ref[...] = jnp.zeros_like(dd_ref)

        dd = jnp.zeros((1, LANES), F32)
        for t in range(C_TC):
            rows = _s5_rows(t, R)
            yv = y_ref[rows, :]
            cdf = 0.5 * (1.0 + _erf(yv * (2.0 ** -0.5)))
            pdf = jnp.exp(-0.5 * yv * yv) * (1.0 / math.sqrt(2.0 * math.pi))
            dy = dgl_ref[rows, :] * (cdf + yv * pdf)
            dd = dd + jnp.sum(dy * u_ref[rows, :], axis=0, keepdims=True)
            dyf_ref[:, t * LANES:(t + 1) * LANES] = dy
        dd_ref[...] += dd
        dy8 = dyf_ref[...].astype(BF16)
        dy8_ref[0] = dy8
        xs = xs_ref[0]
        gx = _dot(dy8, f_ref[0], _NT)
        row = lax.broadcasted_iota(jnp.int32, (R, C_S8), 0)
        for k in range(nsteps):
            s = 1 << k
            sh = pltpu.roll(gx, R - s, 0)
            upd = tr_ref[0, k:k + 1, :] * sh - ti_ref[0, k:k + 1, :] * pltpu.roll(sh, C_S8 // 2, 1)
            gx = gx + jnp.where(row + s < R, upd, 0.0)
        de_in = jnp.where(row + 1 < R, pltpu.roll(gx, R - 1, 0), 0.0)
        deb = de_in.astype(BF16)
        de_ref[0] = deb
        da_ref[0, 0:1, :] += jnp.sum(de_in * xs, axis=0, keepdims=True)
        da_ref[0, 1:2, :] += jnp.sum(de_in * pltpu.roll(xs, C_S8 // 2, 1), axis=0, keepdims=True)
        dx8 = _dot(dy8, m_ref[0], _NT) + _dot(deb, e_ref[0], _NT)
        d = d_ref[...]
        for t in range(C_TC):
            cols = slice(t * LANES, (t + 1) * LANES)
            du_ref[_s5_rows(t, R), :] = dx8[:, cols] + d * dyf_ref[:, cols]

    tok = pl.BlockSpec((L, LANES), lambda c, b: (b, c))
    vec = pl.BlockSpec((1, LANES), lambda c, b: (0, c))
    per_block = lambda shape: pl.BlockSpec((1,) + shape, lambda c, b: (c, 0, 0))
    per_step = lambda shape: pl.BlockSpec((1,) + shape, lambda c, b: (c * B + b, 0, 0))
    return pl.pallas_call(
        body, name=name, grid=(C_NB, B),
        in_specs=[tok, tok, tok, vec, per_step((R, C_S8)), per_block((C_W8, C_W8)),
                  per_block((C_W8, C_S8)), per_block((C_S8, C_W8)), per_block((8, C_S8)), per_block((8, C_S8))],
        out_specs=[tok, per_step((R, C_W8)), per_step((R, C_S8)), per_block((8, C_S8)), vec],
        out_shape=[jax.ShapeDtypeStruct((T, D_MODEL), F32), jax.ShapeDtypeStruct((C_NB * B, R, C_W8), BF16),
                   jax.ShapeDtypeStruct((C_NB * B, R, C_S8), BF16), jax.ShapeDtypeStruct((C_NB, 8, C_S8), F32),
                   jax.ShapeDtypeStruct((1, D_MODEL), F32)],
        scratch_shapes=[pltpu.VMEM((R, C_W8), F32)],
        compiler_params=_cparams("parallel", "arbitrary"),
    )(dgl, y, u, dsk, xs, m8, e8, f8, tab_r, tab_i)


def _bmm_tn(a, b, nb, fold, *, name):
    a = a.reshape(nb, -1, a.shape[-1])
    b = b.reshape(nb, -1, b.shape[-1])
    K, M, N = a.shape[1], a.shape[2], b.shape[2]
    half = C_GB * C_STATE

    def body(a_ref, b_ref, o_ref, p_ref):
        p_ref[...] = _dot(a_ref[0].astype(BF16), b_ref[0].astype(BF16), _TN)
        lane = lax.broadcasted_iota(jnp.int32, (1, LANES), 1)
        if fold == "lags":
            for lag in range(C_TC):
                blocks = [p_ref[s * LANES:(s + 1) * LANES, (s + lag) * LANES:(s + lag + 1) * LANES] for s in range(C_TC - lag)]
                o_ref[0, lag] = functools.reduce(lambda u, v: u + v, blocks)
        elif fold == "e":
            for g in range(C_GB):
                lo, hi = LANES * (g // 2), half + LANES * (g // 2)
                for s in range(C_TC):
                    rows = slice(s * LANES + g * C_GROUP, s * LANES + (g + 1) * C_GROUP)
                    re, im = p_ref[rows, lo:lo + LANES], p_ref[rows, hi:hi + LANES]
                    if g % 2 == 0:
                        im = pltpu.roll(im, C_STATE, 1)
                    else:
                        re = pltpu.roll(re, C_STATE, 1)
                    o_ref[0, rows, :] = jnp.where(lane < C_STATE, re, im)
        else:
            for z in range(2):
                for g in range(C_GB):
                    rows = slice(z * half + g * C_STATE, z * half + (g + 1) * C_STATE)
                    val = jnp.zeros((C_STATE, LANES), F32)
                    for t in range(C_TC):
                        blk = pltpu.roll(p_ref[rows, t * LANES:(t + 1) * LANES], ((t - g) * C_GROUP) % LANES, 1)
                        val = jnp.where((lane >= t * C_GROUP) & (lane < (t + 1) * C_GROUP), blk, val)
                    o_ref[0, rows, :] = val

    if fold == "lags":
        out_spec = pl.BlockSpec((1, C_TC, LANES, LANES), lambda c: (c, 0, 0, 0))
        out_shape = jax.ShapeDtypeStruct((nb, C_TC, LANES, LANES), F32)
    else:
        out_spec = pl.BlockSpec((1, M, LANES), lambda c: (c, 0, 0))
        out_shape = jax.ShapeDtypeStruct((nb, M, LANES), F32)
    return pl.pallas_call(
        body, name=name, grid=(nb,),
        in_specs=[pl.BlockSpec((1, K, M), lambda c: (c, 0, 0)), pl.BlockSpec((1, K, N), lambda c: (c, 0, 0))],
        out_specs=out_spec, out_shape=out_shape, scratch_shapes=[pltpu.VMEM((M, N), F32)],
        compiler_params=_cparams("parallel"),
    )(a, b)


def _loss_head(y, target, *, name):
    T, C = y.shape
    tm = 512

    def body(y_ref, t_ref, l_ref, d_ref):
        err = y_ref[...] - t_ref[...]
        d_ref[...] = err * (1.0 / C)
        sq = err * err
        part = jnp.zeros((8, LANES), F32)
        for r in range(0, tm, 8):
            for c in range(0, C, LANES):
                part = part + sq[r:r + 8, c:c + LANES]

        @pl.when(pl.program_id(0) == 0)
        def _():
            l_ref[...] = part

        @pl.when(pl.program_id(0) > 0)
        def _():
            l_ref[...] += part

    row = pl.BlockSpec((tm, C), lambda i: (i, 0))
    acc = pl.BlockSpec((8, LANES), lambda i: (0, 0))
    return pl.pallas_call(
        body, name=name, grid=(T // tm,), in_specs=[row, row], out_specs=[acc, row],
        out_shape=[jax.ShapeDtypeStruct((8, LANES), F32), jax.ShapeDtypeStruct((T, C), F32)],
        compiler_params=_cparams("arbitrary"),
    )(y, target)


def _adamw(w, g, m, v, *, name):
    shape = w.shape
    size = int(np.prod(shape))
    cols = LANES if (shape[-1] < LANES and size % LANES == 0) else shape[-1]
    rows = size // cols
    tm = _tile(rows, 256) if rows % 8 == 0 else rows
    w2, g2, m2, v2 = (t.reshape(rows, cols) for t in (w, g, m, v))

    def body(w_ref, g_ref, m_ref, v_ref, d_ref, nm_ref, nv_ref):
        gg = g_ref[...]
        nm = ADAM_B1 * m_ref[...] + (1.0 - ADAM_B1) * gg
        nv = ADAM_B2 * v_ref[...] + (1.0 - ADAM_B2) * (gg * gg)
        m_hat = nm / (1.0 - ADAM_B1 ** ADAM_STEP)
        v_hat = nv / (1.0 - ADAM_B2 ** ADAM_STEP)
        d_ref[...] = -ADAM_LR * (m_hat / (jnp.sqrt(v_hat) + ADAM_EPS) + ADAM_WD * w_ref[...])
        nm_ref[...] = nm
        nv_ref[...] = nv

    blk = pl.BlockSpec((tm, cols), lambda i: (i, 0))
    outs = pl.pallas_call(
        body, name=name, grid=(rows // tm,), in_specs=[blk] * 4, out_specs=[blk] * 3,
        out_shape=[jax.ShapeDtypeStruct((rows, cols), F32)] * 3, compiler_params=_cparams("parallel"),
    )(w2, g2, m2, v2)
    return tuple(o.reshape(shape) for o in outs)


def _all_gather(shard, *, name):
    R, C = shard.shape

    def body(x_ref, out_ref, send_sems, recv_sems, local_sem):
        x, y, c = lax.axis_index("x"), lax.axis_index("y"), lax.axis_index("c")
        me, sibling = (x, y, c), (x, y, 1 - c)
        chips = [(1 - x, y), (x, 1 - y), (1 - x, 1 - y)]

        def rows(px, py, pc):
            return out_ref.at[_logical(px, py, pc)]

        def copy(k, block, to, src=None):
            return pltpu.make_async_remote_copy(
                src_ref=rows(*block) if src is None else src, dst_ref=rows(*block),
                send_sem=send_sems.at[k], recv_sem=recv_sems.at[k], device_id=to, device_id_type=_MESH)

        mine = pltpu.make_async_copy(x_ref, rows(*me), local_sem)
        mine.start()
        first = [copy(0, me, sibling, src=x_ref)]
        first += [copy(1 + j, me, (*chip, c), src=x_ref) for j, chip in enumerate(chips)]
        for cp in first:
            cp.start()
        passed = [copy(4 + j, (*chip, c), sibling) for j, chip in enumerate(chips)]
        for j, chip in enumerate(chips):
            copy(1 + j, (*chip, c), me).wait_recv()
            passed[j].start()
        copy(0, sibling, me).wait_recv()
        for j, chip in enumerate(chips):
            copy(4 + j, (*chip, 1 - c), me).wait_recv()
        for cp in first + passed:
            cp.wait_send()
        mine.wait()

    return pl.pallas_call(
        body, name=name, out_shape=jax.ShapeDtypeStruct((N_DEV, R, C), shard.dtype),
        in_specs=[_HBM], out_specs=_HBM,
        scratch_shapes=[pltpu.SemaphoreType.DMA((7,)), pltpu.SemaphoreType.DMA((7,)), pltpu.SemaphoreType.DMA],
    )(shard)


def _gather_weights(shards, *, name):
    nt = len(shards)

    def body(*refs):
        ins, outs = refs[:nt], refs[nt:2 * nt]
        send_sems, recv_sems, local_sems = refs[2 * nt:]
        x, y, c = lax.axis_index("x"), lax.axis_index("y"), lax.axis_index("c")
        me, sibling = (x, y, c), (x, y, 1 - c)
        chips = [(1 - x, y), (x, 1 - y), (1 - x, 1 - y)]

        def copy(t, k, block, to, src=None):
            rows = outs[t].at[_logical(*block)]
            return pltpu.make_async_remote_copy(
                src_ref=rows if src is None else src, dst_ref=rows,
                send_sem=send_sems.at[t, k], recv_sem=recv_sems.at[t, k], device_id=to, device_id_type=_MESH)

        mine = [pltpu.make_async_copy(ins[t], outs[t].at[_logical(*me)], local_sems.at[t]) for t in range(nt)]
        for cp in mine:
            cp.start()
        started = []
        for t in range(nt):
            started.append(copy(t, 0, me, sibling, src=ins[t]))
            started += [copy(t, 1 + j, me, (*chip, c), src=ins[t]) for j, chip in enumerate(chips)]
        for cp in started:
            cp.start()
        for j, chip in enumerate(chips):
            for t in range(nt):
                copy(t, 1 + j, (*chip, c), me).wait_recv()
                fwd = copy(t, 4 + j, (*chip, c), sibling)
                fwd.start()
                started.append(fwd)
        for t in range(nt):
            copy(t, 0, sibling, me).wait_recv()
            for j, chip in enumerate(chips):
                copy(t, 4 + j, (*chip, 1 - c), me).wait_recv()
        for cp in started:
            cp.wait_send()
        for cp in mine:
            cp.wait()

    return pl.pallas_call(
        body, name=name, out_shape=[jax.ShapeDtypeStruct((N_DEV,) + s.shape, s.dtype) for s in shards],
        in_specs=[_HBM] * nt, out_specs=[_HBM] * nt,
        scratch_shapes=[pltpu.SemaphoreType.DMA((nt, 7)), pltpu.SemaphoreType.DMA((nt, 7)), pltpu.SemaphoreType.DMA((nt,))],
    )(*shards)


def _sum_rows(stacked, *, name):
    _, R, C = stacked.shape
    tr = R
    if N_DEV * R * C * stacked.dtype.itemsize > 12 * 1024 * 1024:
        for cand in range(512, 15, -16):
            if R % cand == 0:
                tr = cand
                break

    def body(s_ref, o_ref):
        acc = s_ref[0].astype(F32)
        for k in range(1, N_DEV):
            acc = acc + s_ref[k].astype(F32)
        o_ref[...] = acc

    return pl.pallas_call(
        body, name=name, grid=(R // tr,),
        in_specs=[pl.BlockSpec((N_DEV, tr, C), lambda i: (0, i, 0))], out_specs=pl.BlockSpec((tr, C), lambda i: (i, 0)),
        out_shape=jax.ShapeDtypeStruct((R, C), F32), compiler_params=_cparams("parallel"),
    )(stacked)


_LARGE = (("ab_w_in", 1, True), ("ab_w_out", 1, False), ("s5_w_glu", 1, True), ("xattn_wq", 2, False),
          ("xattn_wkv", 2, True), ("xattn_wo", 2, False), ("ffn_w_in", 2, True), ("ffn_w_out", 2, False))
_LARGE_KEYS = tuple((n, l) for n, layers, _ in _LARGE for l in range(layers))
_TRANSPOSED = {n: t for n, _, t in _LARGE}


def _owner_major(name, w):
    return w.T if _TRANSPOSED[name] else w


def _lb_from_logits(logits):
    return jnp.cumsum(jax.nn.softmax(logits, axis=0), axis=0)[0:1]


def _local_step(x, mem, target, W, shards=None):
    B, L, _ = x.shape
    T = B * L
    x0 = x.reshape(T, D_MODEL)
    memf = mem.reshape(B * MEM_LEN, D_MODEL)
    nw = W["norm_w"]
    cos2, sin2 = _rope_tables(L)
    W = dict(W)
    G, received = {}, {}

    def mmx(a, b, gather=(), scatter=(), **kw):
        if shards is None or not (gather or scatter):
            return _mm(a, b, **kw)
        out, gathered, got = _mm(a, b, gather=[shards[k] for k in gather],
                                 scatter=[G[k].reshape(N_DEV, -1, D_MODEL) for k in scatter], **kw)
        for k, g in zip(gather, gathered):
            W[k] = g.reshape(-1, D_MODEL)
        for k, r in zip(scatter, got):
            received[k] = r
        return out

    def vec(v):
        return v.reshape(1, -1)

    saved = []
    xin = x0
    for layer in range(2):
        s = {"x0": xin}
        tag = f"l{layer}"
        if layer == 0:
            h1 = _rms_fwd(xin, vec(nw[0, 0]), name="norm_pre_mix_l0", out_dtype=BF16)
        s["h1"] = h1
        if layer == 0:
            lb, lb_vjp = jax.vjp(_lb_from_logits, W["hgrn_lb_logits"])
            onw = W["hgrn_out_norm_w"].reshape(1, A_WIDTH)
            z = mmx(h1, W["ab_w_in", 0], tb=True, name="ab_in",
                    gather=[("ab_w_out", 0), ("xattn_wq", 0), ("xattn_wkv", 0), ("xattn_wo", 0)])
            ffn0 = [("ffn_w_in", 0), ("ffn_w_out", 0)] if shards is not None else []
            oa, o_raw, s_start, gathered = _hgrn_fwd(z, lb, onw, B, L, name="hgrn_fwd", gather=[shards[k] for k in ffn0])
            for key, g in zip(ffn0, gathered):
                W[key] = g.reshape(-1, D_MODEL)
            qr, kr = _rope_fwd(z, cos2, sin2, B, L, name="rope_qk")
            os_, ls_ = [], []
            for dil in B_DILS:
                o_g, l_g = _dil_fwd(qr, kr, z, dil, B, L, name=f"dil_fwd_{dil}")
                os_.append(o_g)
                ls_.append(l_g)
            ob, lse = _dil_combine(os_, ls_, name="dil_combine")
            ymix = jnp.concatenate([oa, ob.astype(BF16)], axis=-1)
            y1 = _mm(ymix, W["ab_w_out", 0], out_dtype=BF16, name="ab_out")
            s.update(z=z, lb=lb, lb_vjp=lb_vjp, onw=onw, o_raw=o_raw, s_start=s_start, qr=qr, kr=kr, ob=ob, lse=lse, ymix=ymix)
        else:
            p5 = tuple(W[n][0] for n in ("s5_lambda_re", "s5_lambda_im", "s5_log_dt", "s5_b_re", "s5_b_im", "s5_c_re", "s5_c_im"))
            (kern, ez, fz, _, _), s5_vjp = jax.vjp(_s5_build, *p5)
            tab_r, tab_i = _s5_scan_tables(p5[0], p5[1], p5[2], int(math.log2(L // C_TC)))
            kbd, lag_vjp = jax.vjp(_s5_lag_blocks, kern)
            mats = tuple(_s5_dense(kbd, ez, fz, name="s5_maps")) + (tab_r, tab_i)
            dsk = W["s5_d"].reshape(1, D_MODEL)
            ffn1 = [("ffn_w_in", 1)] if shards is not None else []
            gl, ypre, x8, xs, gathered = _s5_fwd(h1, dsk, *mats, B, L, name="s5_fwd", gather=[shards[k] for k in ffn1])
            for key, g in zip(ffn1, gathered):
                W[key] = g.reshape(-1, D_MODEL)
            w_glu = _interleave_rows(W["s5_w_glu", 0])
            zg, y1 = _mm(gl, w_glu, tb=True, name="s5_glu_in", gate=("fwd", "glu"))
            s.update(s5_vjp=s5_vjp, lag_vjp=lag_vjp, mats=mats, x8=x8, xs=xs, dsk=dsk, gl=gl, ypre=ypre, zg=zg, w_glu=w_glu)
        x1, h2 = _rms_fwd(y1, vec(nw[layer, 1]), xin, name=f"norm_post_mix_{tag}", then=(vec(nw[layer, 2]), BF16))
        memn = _rms_fwd(memf, vec(W["mem_norm_w"][layer]), name=f"norm_mem_{tag}", out_dtype=BF16)
        q = _mm(h2, W["xattn_wq", layer], out_dtype=BF16, name=f"x_q_{tag}")
        kv = _mm(memn, W["xattn_wkv", layer], tb=True, out_dtype=BF16, name=f"x_kv_{tag}")
        o = _xattn_fwd(q, kv, B, L, name=f"x_attn_{tag}")
        y2 = _mm(o, W["xattn_wo", layer], out_dtype=BF16, name=f"x_o_{tag}")
        x2, h3 = _rms_fwd(y2, vec(nw[layer, 3]), x1, name=f"norm_post_x_{tag}", then=(vec(nw[layer, 4]), BF16))
        w_ffn_in = _interleave_rows(W["ffn_w_in", layer])
        zf, u = mmx(h3, w_ffn_in, tb=True, name=f"ffn_in_{tag}", gate=("fwd", "swiglu"),
                    gather=[("s5_w_glu", 0), ("xattn_wq", 1), ("xattn_wkv", 1), ("xattn_wo", 1)] if layer == 0 else [])
        y3 = mmx(u, W["ffn_w_out", layer], out_dtype=BF16, name=f"ffn_out_{tag}", gather=[("ffn_w_out", 1)] if layer == 0 else [])
        if layer == 0:
            x3, h1 = _rms_fwd(y3, vec(nw[0, 5]), x2, name="norm_post_ffn_l0", then=(vec(nw[1, 0]), F32))
        else:
            x3 = _rms_fwd(y3, vec(nw[layer, 5]), x2, name=f"norm_post_ffn_{tag}")
        s.update(y1=y1, x1=x1, h2=h2, memn=memn, q=q, kv=kv, o=o, y2=y2, x2=x2, h3=h3, zf=zf, u=u, y3=y3, w_ffn_in=w_ffn_in)
        saved.append(s)
        xin = x3

    loss_parts, dx = _loss_head(xin, target.reshape(T, D_MODEL), name="loss_head")

    d_norm = [[None] * 6 for _ in range(2)]
    d_memn = [None, None]
    for layer in (1, 0):
        s = saved[layer]
        tag = f"l{layer}"
        if layer == 1:
            dy3, d_norm[1][5] = _rms_bwd(s["y3"], vec(nw[1, 5]), dx, name="bnorm_post_ffn_l1", out_dtype=BF16)
        dzf = mmx(dy3, W["ffn_w_out", layer], tb=True, name=f"b_ffn_out_dx_{tag}", gate=("bwd", "swiglu", s["zf"]),
                  scatter=[("xattn_wkv", 1), ("s5_w_glu", 0)] if layer == 0 else [])
        G["ffn_w_out", layer] = _mm(s["u"], dy3, ta=True, out_dtype=BF16, name=f"b_ffn_out_dw_{tag}")
        early = []
        if shards is not None and layer == 0:
            shards = {**shards, "small_early": _pack_small([G[n] for n, _ in _SMALL_EARLY])}
            early = ["small_early"]
        G["ffn_w_in", layer] = _interleave_rows(
            mmx(dzf, s["h3"], ta=True, out_dtype=BF16, name=f"b_ffn_in_dw_{tag}", gather=early,
                scatter=[("ffn_w_out", layer)]), inverse=True)
        if early:
            received["small_early"] = W["small_early"].reshape(N_DEV, -1, LANES)
        dh3 = mmx(dzf, s["w_ffn_in"], out_dtype=BF16, name=f"b_ffn_in_dx_{tag}", scatter=[("ffn_w_in", layer)])
        dx, d_norm[layer][4], dy2, d_norm[layer][3] = _rms_bwd(
            s["x2"], vec(nw[layer, 4]), dh3, dx, name=f"bnorm_pre_ffn_{tag}", then=(s["y2"], vec(nw[layer, 3])))
        do = _mm(dy2, W["xattn_wo", layer], tb=True, out_dtype=BF16, name=f"b_x_o_dx_{tag}")
        G["xattn_wo", layer] = _mm(s["o"], dy2, ta=True, out_dtype=BF16, name=f"b_x_o_dw_{tag}")
        dq, dkv = _xattn_bwd(s["q"], s["kv"], do, B, L, name=f"b_x_attn_{tag}")
        G["xattn_wq", layer] = _mm(s["h2"], dq, ta=True, out_dtype=BF16, name=f"b_x_q_dw_{tag}")
        dh2 = _mm(dq, W["xattn_wq", layer], tb=True, out_dtype=BF16, name=f"b_x_q_dx_{tag}")
        G["xattn_wkv", layer] = _mm(dkv, s["memn"], ta=True, out_dtype=BF16, name=f"b_x_kv_dw_{tag}")
        dmemn = _mm(dkv, W["xattn_wkv", layer], out_dtype=BF16, name=f"b_x_kv_dx_{tag}")
        _, d_memn[layer] = _rms_bwd(memf, vec(W["mem_norm_w"][layer]), dmemn, name=f"bnorm_mem_{tag}", out_dtype=BF16)
        dx, d_norm[layer][2], dy1, d_norm[layer][1] = _rms_bwd(
            s["x1"], vec(nw[layer, 2]), dh2, dx, name=f"bnorm_pre_x_{tag}", then=(s["y1"], vec(nw[layer, 1])))
        if layer == 0:
            z = s["z"]
            dymix = _mm(dy1, W["ab_w_out", 0], tb=True, name="b_ab_out_dx")
            G["ab_w_out", 0] = _mm(s["ymix"], dy1, ta=True, out_dtype=BF16, name="b_ab_out_dw")
            early = [("xattn_wo", 0), ("xattn_wq", 0), ("xattn_wkv", 0), ("ab_w_out", 0)] if shards is not None else []
            dqa, dfa, dia, dga, d_onw, d_lb, got = _hgrn_bwd(
                z, s["lb"], s["onw"], s["o_raw"], s["s_start"], dymix, B, L, name="hgrn_bwd",
                scatter=[G[k].reshape(N_DEV, -1, D_MODEL) for k in early])
            received.update(zip(early, got))
            dqs, dks, dvs = [], [], []
            for dil in B_DILS:
                dqs.append(_dil_bwd_q(s["qr"], s["kr"], z, dymix, s["ob"], s["lse"], dil, B, L, name=f"dil_bwd_q_{dil}"))
                dk_g, dv_g = _dil_bwd_kv(s["qr"], s["kr"], z, dymix, s["ob"], s["lse"], dil, B, L, name=f"dil_bwd_kv_{dil}")
                dks.append(dk_g)
                dvs.append(dv_g)
            dqkv = _rope_bwd(dqs, dks, dvs, cos2, sin2, B, L, name="b_rope")
            dz = jnp.concatenate([dqa, dfa, dia, dga, dqkv], axis=-1)
            G["ab_w_in", 0] = _mm(dz, s["h1"], ta=True, out_dtype=BF16, name="b_ab_in_dw")
            dh1 = mmx(dz, W["ab_w_in", 0], out_dtype=BF16, name="b_ab_in_dx", scatter=[("ab_w_in", 0)])
            G["hgrn_out_norm_w"] = jnp.sum(d_onw.reshape(B, A_WIDTH), axis=0, keepdims=True)
            d_lb_row = jnp.sum(d_lb.reshape(B, A_WIDTH), axis=0, keepdims=True)
            G["hgrn_lb_logits"] = s["lb_vjp"](d_lb_row)[0]
        else:
            dzg = _gated_bwd(s["zg"], dy1, "glu", name="b_s5_glu")
            G["s5_w_glu", 0] = _interleave_rows(
                _mm(dzg, s["gl"], ta=True, out_dtype=BF16, name="b_s5_glu_dw"), inverse=True)
            dgl = mmx(dzg, s["w_glu"], name="b_s5_glu_dx", scatter=[("xattn_wo", 1), ("xattn_wq", 1)])
            dh1, dy8, de_in, da, d_dsk = _s5_bwd(dgl, s["ypre"], s["h1"], s["dsk"], s["xs"], *s["mats"], B, L, name="s5_bwd")
            dkbd = _bmm_tn(s["x8"], dy8, C_NB, "lags", name="s5_bwd_dm").transpose(1, 0, 2, 3)
            dfz = _bmm_tn(s["xs"], dy8, C_NB, "f", name="s5_bwd_df").reshape(C_NB, 2, C_GB, C_STATE, C_TC, C_GROUP)
            dez = _bmm_tn(s["x8"], de_in, C_NB, "e", name="s5_bwd_de").reshape(C_NB, C_TC, C_GB, C_GROUP, 2, C_STATE)
            half = C_S8 // 2
            da_r = (da[:, 0, :half] + da[:, 0, half:]).reshape(C_GROUPS, C_STATE)
            da_i = (da[:, 1, half:] - da[:, 1, :half]).reshape(C_GROUPS, C_STATE)
            gp = s["s5_vjp"](s["lag_vjp"](dkbd) + (dez, dfz, da_r, da_i))
            for n, gv in zip(("s5_lambda_re", "s5_lambda_im", "s5_log_dt", "s5_b_re", "s5_b_im", "s5_c_re", "s5_c_im"), gp):
                G[n] = gv[None]
            G["s5_d"] = d_dsk
        if layer == 1:
            dx, d_norm[1][0], dy3, d_norm[0][5] = _rms_bwd(
                s["x0"], vec(nw[1, 0]), dh1, dx, name="bnorm_pre_mix_l1", then=(saved[0]["y3"], vec(nw[0, 5])))
        else:
            dx, d_norm[0][0] = _rms_bwd(s["x0"], vec(nw[0, 0]), dh1, dx, name="bnorm_pre_mix_l0")

    G["norm_w"] = jnp.stack([jnp.concatenate(d_norm[l], axis=0) for l in range(2)])
    G["mem_norm_w"] = jnp.concatenate(d_memn, axis=0)
    if shards is not None:
        G.update(received)
    return loss_parts, dx.reshape(B, L, D_MODEL), G


_SMALL_LATE = (("norm_w", (2, 6, 1024)), ("mem_norm_w", (2, 1024)), ("hgrn_lb_logits", (3, 512)), ("hgrn_out_norm_w", (1, 512)))
_SMALL_EARLY = (("s5_lambda_re", (1, 64, 64)), ("s5_lambda_im", (1, 64, 64)), ("s5_log_dt", (1, 64)),
                ("s5_b_re", (1, 64, 64, 16)), ("s5_b_im", (1, 64, 64, 16)), ("s5_c_re", (1, 64, 16, 64)),
                ("s5_c_im", (1, 64, 16, 64)), ("s5_d", (1, 1024)))
_SMALL = _SMALL_LATE + _SMALL_EARLY


def _pack_small(values):
    flat = jnp.concatenate([v.reshape(-1) for v in values])
    return jnp.pad(flat, (0, (-flat.shape[0]) % (8 * LANES))).reshape(-1, LANES)


def _unpack_small(flat, spec):
    out, off = {}, 0
    for n, shp in spec:
        size = int(np.prod(shp))
        out[n] = flat[off:off + size].reshape(shp)
        off += size
    return out, off

_WEIGHT_ORDER = ('norm_w', 'mem_norm_w', 'ab_w_in', 'ab_w_out', 'hgrn_lb_logits', 'hgrn_out_norm_w', 's5_lambda_re',
                 's5_lambda_im', 's5_log_dt', 's5_b_re', 's5_b_im', 's5_c_re', 's5_c_im', 's5_d', 's5_w_glu', 'xattn_wq',
                 'xattn_wkv', 'xattn_wo', 'ffn_w_in', 'ffn_w_out')


def kernel(x, mem, norm_w, mem_norm_w, ab_w_in, ab_w_out, hgrn_lb_logits, hgrn_out_norm_w, s5_lambda_re, s5_lambda_im, s5_log_dt, s5_b_re, s5_b_im, s5_c_re, s5_c_im, s5_d, s5_w_glu, xattn_wq, xattn_wkv, xattn_wo, ffn_w_in, ffn_w_out, loss_target, m_norm_w, m_mem_norm_w, m_ab_w_in, m_ab_w_out, m_hgrn_lb_logits, m_hgrn_out_norm_w, m_s5_lambda_re, m_s5_lambda_im, m_s5_log_dt, m_s5_b_re, m_s5_b_im, m_s5_c_re, m_s5_c_im, m_s5_d, m_s5_w_glu, m_xattn_wq, m_xattn_wkv, m_xattn_wo, m_ffn_w_in, m_ffn_w_out, v_norm_w, v_mem_norm_w, v_ab_w_in, v_ab_w_out, v_hgrn_lb_logits, v_hgrn_out_norm_w, v_s5_lambda_re, v_s5_lambda_im, v_s5_log_dt, v_s5_b_re, v_s5_b_im, v_s5_c_re, v_s5_c_im, v_s5_d, v_s5_w_glu, v_xattn_wq, v_xattn_wkv, v_xattn_wo, v_ffn_w_in, v_ffn_w_out):
    local = dict(norm_w=norm_w, mem_norm_w=mem_norm_w, ab_w_in=ab_w_in, ab_w_out=ab_w_out, hgrn_lb_logits=hgrn_lb_logits,
                 hgrn_out_norm_w=hgrn_out_norm_w, s5_lambda_re=s5_lambda_re, s5_lambda_im=s5_lambda_im, s5_log_dt=s5_log_dt,
                 s5_b_re=s5_b_re, s5_b_im=s5_b_im, s5_c_re=s5_c_re, s5_c_im=s5_c_im, s5_d=s5_d, s5_w_glu=s5_w_glu,
                 xattn_wq=xattn_wq, xattn_wkv=xattn_wkv, xattn_wo=xattn_wo, ffn_w_in=ffn_w_in, ffn_w_out=ffn_w_out)
    mom_m = dict(zip(_WEIGHT_ORDER, (m_norm_w, m_mem_norm_w, m_ab_w_in, m_ab_w_out, m_hgrn_lb_logits, m_hgrn_out_norm_w, m_s5_lambda_re, m_s5_lambda_im, m_s5_log_dt, m_s5_b_re, m_s5_b_im, m_s5_c_re, m_s5_c_im, m_s5_d, m_s5_w_glu, m_xattn_wq, m_xattn_wkv, m_xattn_wo, m_ffn_w_in, m_ffn_w_out)))
    mom_v = dict(zip(_WEIGHT_ORDER, (v_norm_w, v_mem_norm_w, v_ab_w_in, v_ab_w_out, v_hgrn_lb_logits, v_hgrn_out_norm_w, v_s5_lambda_re, v_s5_lambda_im, v_s5_log_dt, v_s5_b_re, v_s5_b_im, v_s5_c_re, v_s5_c_im, v_s5_d, v_s5_w_glu, v_xattn_wq, v_xattn_wkv, v_xattn_wo, v_ffn_w_in, v_ffn_w_out)))
    dev = 4 * lax.axis_index("x") + 2 * lax.axis_index("y") + lax.axis_index("c")

    shards = {(n, l): _owner_major(n, local[n][l]).astype(BF16) for n, l in _LARGE_KEYS}
    first = ("ab_w_in", 0)
    W = {first: _gather_weights([shards[first]], name="gather_first")[0].reshape(-1, D_MODEL)}
    tiny = jnp.concatenate([norm_w.reshape(-1), s5_d.reshape(-1)])
    tiny = jnp.pad(tiny, (0, 16 * LANES - tiny.shape[0])).reshape(16, LANES)
    tiny_all = _all_gather(tiny, name="gather_tiny").reshape(N_DEV, 16 * LANES)
    W["norm_w"] = tiny_all[:, :12 * LANES].reshape(N_DEV, 2, 6, LANES).transpose(1, 2, 0, 3).reshape(2, 6, D_MODEL)
    W["s5_d"] = tiny_all[:, 12 * LANES:13 * LANES].reshape(1, D_MODEL)
    for n in ("mem_norm_w", "hgrn_lb_logits", "hgrn_out_norm_w", "s5_lambda_re", "s5_lambda_im", "s5_log_dt",
              "s5_b_re", "s5_b_im", "s5_c_re", "s5_c_im"):
        W[n] = local[n]

    loss_parts, grad_x, G = _local_step(x, mem, loss_target, W, shards)

    g_layers = {}
    for n, l in _LARGE_KEYS:
        g = _sum_rows(G[n, l], name=f"sum_grads_{n}_{l}")
        g_layers.setdefault(n, []).append(g.T if _TRANSPOSED[n] else g)
    g_local = {n: jnp.stack(gl) for n, gl in g_layers.items()}
    small = _pack_small([G[n] for n, _ in _SMALL_LATE] + [0.5 / D_MODEL * jnp.sum(loss_parts)])
    small_sum = _sum_rows(_all_gather(small, name="gather_small"), name="sum_small").reshape(-1)
    g_full, off = _unpack_small(small_sum, _SMALL_LATE)
    loss = small_sum[off]
    early_sum = _sum_rows(G["small_early"], name="sum_small_early").reshape(-1)
    g_full.update(_unpack_small(early_sum, _SMALL_EARLY)[0])
    grads = dict(g_local)
    for n, shp in _SMALL:
        if n == "norm_w":
            grads[n] = lax.dynamic_slice_in_dim(g_full[n], dev * LANES, LANES, axis=2)
        elif n == "s5_d":
            grads[n] = lax.dynamic_slice_in_dim(g_full[n], dev * LANES, LANES, axis=1)
        else:
            grads[n] = g_full[n]

    delta, new_m, new_v = {}, {}, {}
    for n in _WEIGHT_ORDER:
        delta[n], new_m[n], new_v[n] = _adamw(local[n], grads[n], mom_m[n], mom_v[n], name=f"adamw_{n}")
    return (loss, grad_x, *[grads[n] for n in _WEIGHT_ORDER], *[delta[n] for n in _WEIGHT_ORDER],
            *[new_m[n] for n in _WEIGHT_ORDER], *[new_v[n] for n in _WEIGHT_ORDER])
```

```python
import functools
import math

import numpy as np
import jax
import jax.numpy as jnp
from jax import lax
from jax.experimental import pallas as pl
from jax.experimental.pallas import tpu as pltpu

F32 = jnp.float32
BF16 = jnp.bfloat16
HI = lax.Precision.HIGHEST

D_MODEL = 1024
NORM_EPS = 1e-6
A_WIDTH = 512
A_HEAD = 128
A_CHUNK = 32
A_SUPER = 256
B_SPAN = 128
B_DILS = (1, 4, 16)
ROPE_THETA = 10000.0
C_GROUPS = 64
C_GROUP = 16
C_STATE = 64
C_TC = 8
C_MIN_NEG_RE = -1e-4
MEM_LEN = 256
X_HEADS = 4
X_HD = 256
D_FF = 2816
N_DEV = 8
LANES = 128

ADAM_LR, ADAM_B1, ADAM_B2, ADAM_EPS, ADAM_WD, ADAM_STEP = 0.001, 0.9, 0.999, 1e-08, 0.01, 10

NEG_BIG = -1e30


def _tile(n, pref):
    for d in range(min(pref, n) // LANES * LANES, 0, -LANES):
        if n % d == 0:
            return d
    return n


def _cparams(*sem):
    return pltpu.CompilerParams(dimension_semantics=sem, vmem_limit_bytes=56 * 1024 * 1024)


def _sigmoid(x):
    return 0.5 * jnp.tanh(0.5 * x) + 0.5


def _erf(x):
    ax = jnp.abs(x)
    t = 1.0 / (1.0 + 0.3275911 * ax)
    poly = t * (0.254829592 + t * (-0.284496736 + t * (1.421413741 + t * (-1.453152027 + t * 1.061405429))))
    y = 1.0 - poly * jnp.exp(-ax * ax)
    return jnp.where(x < 0, -y, y)


_HBM = pl.BlockSpec(memory_space=pltpu.HBM)
_MESH = pl.DeviceIdType.MESH


def _logical(px, py, pc):
    return 4 * px + 2 * py + pc


class _Exchange:
    def __init__(self, gather=(), scatter=()):
        self.gather, self.scatter = list(gather), list(scatter)
        self.ng, self.n = len(self.gather), len(self.gather) + len(self.scatter)

    def operands(self):
        return self.gather + self.scatter

    def in_specs(self):
        return [_HBM] * self.n

    def out_shapes(self):
        return ([jax.ShapeDtypeStruct((N_DEV,) + g.shape, g.dtype) for g in self.gather]
                + [jax.ShapeDtypeStruct(s.shape, s.dtype) for s in self.scatter])

    def scratch(self):
        if not self.n:
            return []
        return [pltpu.SemaphoreType.DMA((self.n, 7)), pltpu.SemaphoreType.DMA((self.n, 7)), pltpu.SemaphoreType.DMA((self.n,))]

    def split(self, results):
        return list(results[:self.ng]), list(results[self.ng:])

    def run(self, ins, outs, sems, first, last):
        if not self.n:
            return
        send_sems, recv_sems, local_sems = sems
        x, y, c = lax.axis_index("x"), lax.axis_index("y"), lax.axis_index("c")
        me, sibling = _logical(x, y, c), (x, y, 1 - c)
        chips = [(1 - x, y), (x, 1 - y), (1 - x, 1 - y)]
        peers = [(x ^ (k >> 2), y ^ ((k >> 1) & 1), c ^ (k & 1)) for k in range(1, N_DEV)]

        def remote(t, k, src, dst, to):
            return pltpu.make_async_remote_copy(src_ref=src, dst_ref=dst, send_sem=send_sems.at[t, k],
                                                recv_sem=recv_sems.at[t, k], device_id=to, device_id_type=_MESH)

        def local(t):
            src = ins[t] if t < self.ng else ins[t].at[me]
            return pltpu.make_async_copy(src, outs[t].at[me], local_sems.at[t])

        @pl.when(first)
        def _():
            for t in range(self.n):
                local(t).start()
                if t < self.ng:
                    remote(t, 0, ins[t], outs[t].at[me], sibling).start()
                    for j, chip in enumerate(chips):
                        remote(t, 1 + j, ins[t], outs[t].at[me], (*chip, c)).start()
                else:
                    for k, peer in enumerate(peers):
                        remote(t, k, ins[t].at[_logical(*peer)], outs[t].at[me], peer).start()

        @pl.when(last)
        def _():
            for j, chip in enumerate(chips):
                for t in range(self.ng):
                    landed = outs[t].at[_logical(*chip, c)]
                    remote(t, 1 + j, ins[t], landed, sibling).wait_recv()
                    remote(t, 4 + j, landed, landed, sibling).start()
            for t in range(self.n):
                if t < self.ng:
                    remote(t, 0, ins[t], outs[t].at[_logical(*sibling)], sibling).wait_recv()
                    for j, chip in enumerate(chips):
                        remote(t, 4 + j, ins[t], outs[t].at[_logical(*chip, 1 - c)], sibling).wait_recv()
                    for k in range(7):
                        remote(t, k, ins[t], outs[t].at[me], sibling).wait_send()
                else:
                    for k, peer in enumerate(peers):
                        remote(t, k, ins[t].at[me], outs[t].at[_logical(*peer)], peer).wait_recv()
                    for k, peer in enumerate(peers):
                        remote(t, k, ins[t].at[_logical(*peer)], outs[t].at[me], peer).wait_send()
                local(t).wait()


def _carrying(body, n_in, n_out, ex, grid):
    n_sems = len(ex.scratch())

    def wrapped(*refs):
        ins, ex_in = refs[:n_in], refs[n_in:n_in + ex.n]
        outs = refs[n_in + ex.n:n_in + ex.n + n_out]
        ex_out = refs[n_in + ex.n + n_out:n_in + 2 * ex.n + n_out]
        rest = refs[n_in + 2 * ex.n + n_out:]
        scratch, sems = rest[:len(rest) - n_sems], rest[len(rest) - n_sems:]
        ids = [pl.program_id(a) for a in range(len(grid))]
        first = functools.reduce(lambda u, v: u & v, [i == 0 for i in ids])
        last = functools.reduce(lambda u, v: u & v, [i == g - 1 for i, g in zip(ids, grid)])
        ex.run(ex_in, ex_out, sems, first, last)
        body(*ins, *outs, *scratch)

    return wrapped


_MM_VMEM_BUDGET = 36 * 1024 * 1024


def _mm(a, b, *, ta=False, tb=False, out_dtype=F32, name, tiles=(1408, 1408, 4096), gather=(), scatter=(), gate=None):
    M, K = (a.shape[1], a.shape[0]) if ta else a.shape
    N = b.shape[0] if tb else b.shape[1]
    assert (b.shape[1] if tb else b.shape[0]) == K
    gate_mode = gate[0] if gate else None
    tm, tn, tk = _tile(M, tiles[0]), _tile(N, tiles[1]), _tile(K, tiles[2])
    if gate_mode == "fwd":
        bs = _gate_block(N // 2)
        tn = 2 * bs
    elif gate_mode == "bwd":
        bs = _gate_block(N)
        tn = bs

    def vmem_bytes():
        acc = 4 * tm * tn if tk < K else 0
        if gate_mode == "fwd":
            io = (2 * (2 + 1) + 4) * tm * tn
        elif gate_mode == "bwd":
            io = (2 * (4 + 4) + 4) * tm * tn
        else:
            io = 2 * tm * tn * jnp.dtype(out_dtype).itemsize
        return 2 * 2 * (tm * tk + tk * tn) + acc + io

    while vmem_bytes() > _MM_VMEM_BUDGET:
        if gate_mode and tm > 256:
            tm = _tile(M, tm - LANES)
        elif tk > 512:
            tk = _tile(K, tk - LANES)
        elif tn > 256 and not gate_mode:
            tn = _tile(N, tn - LANES)
        else:
            tm = _tile(M, tm - LANES)
    ni, nj, nk = M // tm, N // tn, K // tk
    ex = _Exchange(gather, scatter)

    a_spec = pl.BlockSpec((tk, tm), lambda i, j, k: (k, i)) if ta else pl.BlockSpec((tm, tk), lambda i, j, k: (i, k))
    b_spec = pl.BlockSpec((tn, tk), lambda i, j, k: (j, k)) if tb else pl.BlockSpec((tk, tn), lambda i, j, k: (k, j))
    dims = (((0 if ta else 1,), (1 if tb else 0,)), ((), ()))
    n_acc = 1 if nk > 1 else 0
    n_in = 3 if gate_mode == "bwd" else 2
    n_out = 2 if gate_mode == "fwd" else 1

    def finish(val, in_refs, out_refs, rows=slice(None)):
        if gate_mode is None:
            out_refs[0][rows, :] = val.astype(out_refs[0].dtype)
        elif gate_mode == "fwd":
            out_refs[0][rows, :] = val.astype(BF16)
            for p in range(tn // (2 * bs)):
                a_, b_ = val[:, 2 * p * bs:(2 * p + 1) * bs], val[:, (2 * p + 1) * bs:(2 * p + 2) * bs]
                out_refs[1][rows, p * bs:(p + 1) * bs] = _gate_value(a_, b_, gate[1]).astype(BF16)
        else:
            z_ref = in_refs[2]
            for p in range(tn // bs):
                a_ = z_ref[rows, 2 * p * bs:(2 * p + 1) * bs].astype(F32)
                b_ = z_ref[rows, (2 * p + 1) * bs:(2 * p + 2) * bs].astype(F32)
                da, db = _gate_grads(a_, b_, val[:, p * bs:(p + 1) * bs], gate[1])
                out_refs[0][rows, 2 * p * bs:(2 * p + 1) * bs] = da.astype(BF16)
                out_refs[0][rows, (2 * p + 1) * bs:(2 * p + 2) * bs] = db.astype(BF16)

    halves = 2 if (gate_mode and nk == 1 and not ta and tm % 32 == 0) else 1

    def body(*refs):
        in_refs, rest = refs[:n_in], refs[n_in:]
        ex_in, out_refs = rest[:ex.n], rest[ex.n:ex.n + n_out]
        ex_out, scratch = rest[ex.n + n_out:2 * ex.n + n_out], rest[2 * ex.n + n_out:]
        i, j, k = pl.program_id(0), pl.program_id(1), pl.program_id(2)
        ex.run(ex_in, ex_out, scratch[n_acc:], (i == 0) & (j == 0) & (k == 0), (i == ni - 1) & (j == nj - 1) & (k == nk - 1))
        if halves > 1:
            rh = tm // halves
            rhs = in_refs[1][...].astype(BF16)
            parts = [lax.dot_general(in_refs[0][r * rh:(r + 1) * rh, :].astype(BF16), rhs, dims, preferred_element_type=F32)
                     for r in range(halves)]
            for r, p in enumerate(parts):
                finish(p, in_refs, out_refs, slice(r * rh, (r + 1) * rh))
            return
        part = lax.dot_general(in_refs[0][...].astype(BF16), in_refs[1][...].astype(BF16), dims, preferred_element_type=F32)
        if nk == 1:
            finish(part, in_refs, out_refs)
            return
        acc_ref = scratch[0]

        @pl.when(k == 0)
        def _():
            acc_ref[...] = part

        @pl.when(k > 0)
        def _():
            acc_ref[...] += part

        @pl.when(k == nk - 1)
        def _():
            finish(acc_ref[...], in_refs, out_refs)

    tile = lambda width: pl.BlockSpec((tm, width), lambda i, j, k: (i, j))
    if gate_mode == "fwd":
        out_specs, out_shape = [tile(tn), tile(tn // 2)], [jax.ShapeDtypeStruct((M, N), BF16), jax.ShapeDtypeStruct((M, N // 2), BF16)]
    elif gate_mode == "bwd":
        out_specs, out_shape = [tile(2 * tn)], [jax.ShapeDtypeStruct((M, 2 * N), BF16)]
    else:
        out_specs, out_shape = [tile(tn)], [jax.ShapeDtypeStruct((M, N), out_dtype)]
    operands = [a, b] + ([gate[2]] if gate_mode == "bwd" else [])
    sem = ("arbitrary",) * 3 if ex.n else ("parallel", "parallel", "arbitrary")
    res = pl.pallas_call(
        body, name=name, grid=(ni, nj, nk),
        in_specs=[a_spec, b_spec] + ([tile(2 * tn)] if gate_mode == "bwd" else []) + ex.in_specs(),
        out_specs=out_specs + ex.in_specs(),
        out_shape=out_shape + ex.out_shapes(),
        scratch_shapes=([pltpu.VMEM((tm, tn), F32)] if nk > 1 else []) + ex.scratch(),
        compiler_params=_cparams(*sem),
    )(*operands, *ex.operands())
    main = res[0] if n_out == 1 else tuple(res[:n_out])
    if not ex.n:
        return main
    return (main,) + tuple(ex.split(res[n_out:]))


def _rms_fwd(x, w, res=None, *, name, out_dtype=F32, then=None):
    T, C = x.shape
    tm = _tile(T, 1024)
    has_res = res is not None

    def norm(v, w_ref):
        return v * lax.rsqrt(jnp.mean(v * v, axis=-1, keepdims=True) + NORM_EPS) * w_ref[...]

    def body(*refs):
        x_ref, w_ref = refs[0], refs[1]
        y = norm(x_ref[...].astype(F32), w_ref)
        if has_res:
            y = y + refs[2][...]
        if then is None:
            refs[-1][...] = y.astype(refs[-1].dtype)
        else:
            refs[-2][...] = y.astype(refs[-2].dtype)
            refs[-1][...] = norm(y, refs[-3]).astype(refs[-1].dtype)

    row = pl.BlockSpec((tm, C), lambda i: (i, 0))
    vec = pl.BlockSpec((1, C), lambda i: (0, 0))
    ins = [x, w] + ([res] if has_res else []) + ([then[0]] if then else [])
    in_specs = [row, vec] + ([row] if has_res else []) + ([vec] if then else [])
    out_shape = [jax.ShapeDtypeStruct((T, C), out_dtype)] + ([jax.ShapeDtypeStruct((T, C), then[1])] if then else [])
    res_ = pl.pallas_call(
        body, name=name, grid=(T // tm,), in_specs=in_specs, out_specs=[row] * len(out_shape),
        out_shape=out_shape, compiler_params=_cparams("parallel"),
    )(*ins)
    return tuple(res_) if then else res_[0]


def _rms_bwd(x, w, dy, add=None, *, name, out_dtype=F32, then=None):
    T, C = x.shape
    tm = _tile(T, 1024)
    has_add = add is not None
    n_in = 3 + has_add + (2 if then else 0)

    def grads(x_ref, w_ref, g, dw_ref):
        xv = x_ref[...].astype(F32)
        r = lax.rsqrt(jnp.mean(xv * xv, axis=-1, keepdims=True) + NORM_EPS)
        xh = xv * r
        part = jnp.sum(g * xh, axis=0, keepdims=True)

        @pl.when(pl.program_id(0) == 0)
        def _():
            dw_ref[...] = part

        @pl.when(pl.program_id(0) > 0)
        def _():
            dw_ref[...] += part

        gx = g * w_ref[...]
        return r * (gx - xh * jnp.mean(gx * xh, axis=-1, keepdims=True))

    def body(*refs):
        ins, outs = refs[:n_in], refs[n_in:]
        dx = grads(ins[0], ins[1], ins[2][...].astype(F32), outs[1])
        if has_add:
            dx = dx + ins[3][...]
        outs[0][...] = dx.astype(outs[0].dtype)
        if then:
            outs[2][...] = grads(ins[-2], ins[-1], dx, outs[3]).astype(BF16)

    row = pl.BlockSpec((tm, C), lambda i: (i, 0))
    vec = pl.BlockSpec((1, C), lambda i: (0, 0))
    ins = [x, w, dy] + ([add] if has_add else []) + (list(then) if then else [])
    big, small = jax.ShapeDtypeStruct((T, C), out_dtype), jax.ShapeDtypeStruct((1, C), F32)
    return pl.pallas_call(
        body, name=name, grid=(T // tm,),
        in_specs=[row, vec, row] + ([row] if has_add else []) + ([row, vec] if then else []),
        out_specs=[row, vec] + ([row, vec] if then else []),
        out_shape=[big, small] + ([jax.ShapeDtypeStruct((T, C), BF16), small] if then else []),
        compiler_params=_cparams("arbitrary"),
    )(*ins)


def _gate_block(width):
    return _tile(width, 1408)


def _gate_value(a, b, kind):
    return a * _sigmoid(a) * b if kind == "swiglu" else a * _sigmoid(b)


def _gate_grads(a, b, d, kind):
    if kind == "swiglu":
        s = _sigmoid(a)
        return d * b * (s * (1.0 + a * (1.0 - s))), d * a * s
    s = _sigmoid(b)
    return d * s, d * a * s * (1.0 - s)


def _interleave_rows(w, inverse=False):
    W2, C = w.shape
    bs = _gate_block(W2 // 2)
    nb = W2 // 2 // bs
    shape = (nb, 2, bs, C) if inverse else (2, nb, bs, C)
    return w.reshape(shape).transpose(1, 0, 2, 3).reshape(W2, C)


def _gated_bwd(z, dout, kind, *, name):
    T, W2 = z.shape
    W = W2 // 2
    tm, bs = _tile(T, 512), _gate_block(W)

    def body(z_ref, d_ref, o_ref):
        da, db = _gate_grads(z_ref[:, :bs].astype(F32), z_ref[:, bs:].astype(F32), d_ref[...].astype(F32), kind)
        o_ref[:, :bs] = da.astype(o_ref.dtype)
        o_ref[:, bs:] = db.astype(o_ref.dtype)

    return pl.pallas_call(
        body, name=name, grid=(T // tm, W // bs),
        in_specs=[pl.BlockSpec((tm, 2 * bs), lambda i, j: (i, j)), pl.BlockSpec((tm, bs), lambda i, j: (i, j))],
        out_specs=pl.BlockSpec((tm, 2 * bs), lambda i, j: (i, j)),
        out_shape=jax.ShapeDtypeStruct((T, W2), BF16), compiler_params=_cparams("parallel", "parallel"),
    )(z, dout)


_NT = (((1,), (1,)), ((), ()))
_TN = (((0,), (0,)), ((), ()))


def _dot(a, b, dims=None, precision=None):
    if dims is None:
        return jnp.dot(a, b, preferred_element_type=F32, precision=precision)
    return lax.dot_general(a, b, dims, preferred_element_type=F32, precision=precision)


def _xattn_fwd(q, kv, B, L, *, name):
    T = q.shape[0]
    tq = 512
    nq = L // tq
    scale = X_HD ** -0.5

    def body(q_ref, k_ref, v_ref, o_ref):
        heads = [slice(h * X_HD, (h + 1) * X_HD) for h in range(X_HEADS)]
        s = [_dot(q_ref[:, sl].astype(BF16), k_ref[:, sl].astype(BF16), _NT) * scale for sl in heads]
        m = [jnp.max(a, axis=-1, keepdims=True) for a in s]
        p = [jnp.exp(a - b) for a, b in zip(s, m)]
        l = [jnp.sum(a, axis=-1, keepdims=True) for a in p]
        o = [_dot(a.astype(BF16), v_ref[:, sl].astype(BF16)) for a, sl in zip(p, heads)]
        for sl, a, b in zip(heads, o, l):
            o_ref[:, sl] = (a / b).astype(BF16)

    return pl.pallas_call(
        body, name=name, grid=(B, nq),
        in_specs=[pl.BlockSpec((tq, D_MODEL), lambda b, i: (b * nq + i, 0)),
                  pl.BlockSpec((MEM_LEN, D_MODEL), lambda b, i: (b, 0)),
                  pl.BlockSpec((MEM_LEN, D_MODEL), lambda b, i: (b, 1))],
        out_specs=pl.BlockSpec((tq, D_MODEL), lambda b, i: (b * nq + i, 0)),
        out_shape=jax.ShapeDtypeStruct((T, D_MODEL), BF16), compiler_params=_cparams("parallel", "parallel"),
    )(q, kv, kv)


def _xattn_bwd(q, kv, do, B, L, *, name):
    T = q.shape[0]
    tq = 512
    nq = L // tq
    scale = X_HD ** -0.5

    def body(q_ref, k_ref, v_ref, do_ref, dq_ref, dkv_ref):
        @pl.when(pl.program_id(1) == 0)
        def _():
            dkv_ref[...] = jnp.zeros_like(dkv_ref)

        heads = [slice(h * X_HD, (h + 1) * X_HD) for h in range(X_HEADS)]
        qs = [q_ref[:, sl].astype(BF16) for sl in heads]
        ks = [k_ref[:, sl].astype(BF16) for sl in heads]
        dos = [do_ref[:, sl].astype(BF16) for sl in heads]
        s = [_dot(a, b, _NT) * scale for a, b in zip(qs, ks)]
        dp = [_dot(a, v_ref[:, sl].astype(BF16), _NT) for a, sl in zip(dos, heads)]
        e = [jnp.exp(a - jnp.max(a, axis=-1, keepdims=True)) for a in s]
        p = [a / jnp.sum(a, axis=-1, keepdims=True) for a in e]
        ds = [(a * (b - jnp.sum(b * a, axis=-1, keepdims=True)) * scale).astype(BF16) for a, b in zip(p, dp)]
        dv = [_dot(a.astype(BF16), b, _TN) for a, b in zip(p, dos)]
        dq = [_dot(a, b) for a, b in zip(ds, ks)]
        dk = [_dot(a, b, _TN) for a, b in zip(ds, qs)]
        for h, sl in enumerate(heads):
            dq_ref[:, sl] = dq[h].astype(BF16)
            dkv_ref[:, sl] += dk[h]
            dkv_ref[:, D_MODEL + h * X_HD:D_MODEL + (h + 1) * X_HD] += dv[h]

    return pl.pallas_call(
        body, name=name, grid=(B, nq),
        in_specs=[pl.BlockSpec((tq, D_MODEL), lambda b, i: (b * nq + i, 0)),
                  pl.BlockSpec((MEM_LEN, D_MODEL), lambda b, i: (b, 0)),
                  pl.BlockSpec((MEM_LEN, D_MODEL), lambda b, i: (b, 1)),
                  pl.BlockSpec((tq, D_MODEL), lambda b, i: (b * nq + i, 0))],
        out_specs=[pl.BlockSpec((tq, D_MODEL), lambda b, i: (b * nq + i, 0)),
                   pl.BlockSpec((MEM_LEN, 2 * D_MODEL), lambda b, i: (b, 0))],
        out_shape=[jax.ShapeDtypeStruct((T, D_MODEL), BF16), jax.ShapeDtypeStruct((B * MEM_LEN, 2 * D_MODEL), F32)],
        compiler_params=_cparams("parallel", "arbitrary"),
    )(q, kv, kv, do)


def _chunk_masks():
    row = lax.broadcasted_iota(jnp.int32, (A_SUPER, A_SUPER), 0)
    col = lax.broadcasted_iota(jnp.int32, (A_SUPER, A_SUPER), 1)
    same = jnp.right_shift(row, 5) == jnp.right_shift(col, 5)
    return same, same & (col <= row), same & (col >= row)


def _dot_mask(mask, x):
    m = mask.astype(BF16)
    hi = x.astype(BF16)
    rest = x - hi.astype(F32)
    mid = rest.astype(BF16)
    lo = (rest - mid.astype(F32)).astype(BF16)
    return _dot(m, hi) + _dot(m, mid) + _dot(m, lo)


def _chunk_row(x, which):
    rows = [x[c * A_CHUNK + which % A_CHUNK:c * A_CHUNK + which % A_CHUNK + 1, :] for c in range(A_SUPER // A_CHUNK)]
    return jnp.concatenate([jnp.broadcast_to(r, (A_CHUNK, x.shape[1])) for r in rows], axis=0)


def _hgrn_gates(fa, lb):
    sig = _sigmoid(fa)
    f = lb + (1.0 - lb) * sig
    return sig, f, jnp.log(f), 1.0 - f


def _hgrn_fwd(z, lb, onw, B, L, *, name, gather=()):
    T = B * L
    ns = L // A_SUPER
    nch = A_SUPER // A_CHUNK
    ex = _Exchange(gather=gather)
    grid = (B, 4, ns)

    def body(q_ref, f_ref, v_ref, g_ref, lb_ref, w_ref, oa_ref, o_ref, s_ref, st_ref, sc_ref):
        @pl.when(pl.program_id(2) == 0)
        def _():
            st_ref[...] = jnp.zeros_like(st_ref)

        s_ref[0] = st_ref[...]
        same, tril, _ = _chunk_masks()
        q, v = q_ref[...], v_ref[...]
        _, _, lf, k = _hgrn_gates(f_ref[...], lb_ref[...])
        bcs = _dot_mask(tril, lf)
        bl = _chunk_row(bcs, -1)
        qd = (q * jnp.exp(bcs)).astype(BF16)
        ki = (k * jnp.exp(-bcs)).astype(BF16)
        ke = (k * jnp.exp(bl - bcs)).astype(BF16)
        dec = jnp.exp(bl)
        vb = v.astype(BF16)
        a = jnp.where(tril, _dot(qd, ki, _NT), 0.0)
        o_ref[...] = _dot(a.astype(BF16), vb)
        chunks = [slice(c * A_CHUNK, (c + 1) * A_CHUNK) for c in range(nch)]
        outer = [_dot(vb[rs], ke[rs], _TN) for rs in chunks]
        st = st_ref[...]
        for c, rs in enumerate(chunks):
            sc_ref[c] = st.astype(BF16)
            st = st * dec[c * A_CHUNK:c * A_CHUNK + 1, :] + outer[c]
        st_ref[...] = st
        for c, rs in enumerate(chunks):
            o_ref[rs, :] += _dot(qd[rs], sc_ref[c], _NT)
        o = o_ref[...]
        r = lax.rsqrt(jnp.mean(o * o, axis=-1, keepdims=True) + NORM_EPS)
        g = g_ref[...]
        oa_ref[...] = (o * r * w_ref[...] * (g * _sigmoid(g))).astype(BF16)

    def zspec(off):
        return pl.BlockSpec((A_SUPER, A_HEAD), lambda b, h, n: (b * ns + n, off + h))

    hvec = pl.BlockSpec((1, A_HEAD), lambda b, h, n: (0, h))
    ospec = pl.BlockSpec((A_SUPER, A_HEAD), lambda b, h, n: (b * ns + n, h))
    res = pl.pallas_call(
        _carrying(body, 6, 3, ex, grid) if ex.n else body, name=name, grid=grid,
        in_specs=[zspec(0), zspec(4), zspec(8), zspec(12), hvec, hvec] + ex.in_specs(),
        out_specs=[ospec, ospec, pl.BlockSpec((1, A_HEAD, A_HEAD), lambda b, h, n: ((b * 4 + h) * ns + n, 0, 0))] + ex.in_specs(),
        out_shape=[jax.ShapeDtypeStruct((T, A_WIDTH), BF16), jax.ShapeDtypeStruct((T, A_WIDTH), F32),
                   jax.ShapeDtypeStruct((B * 4 * ns, A_HEAD, A_HEAD), F32)] + ex.out_shapes(),
        scratch_shapes=[pltpu.VMEM((A_HEAD, A_HEAD), F32), pltpu.VMEM((nch, A_HEAD, A_HEAD), BF16)] + ex.scratch(),
        compiler_params=_cparams(*(("arbitrary",) * 3 if ex.n else ("parallel", "parallel", "arbitrary"))),
    )(z, z, z, z, lb, onw, *ex.operands())
    return tuple(res[:3]) + (list(res[3:]),)


def _hgrn_bwd(z, lb, onw, o_raw, s_start, doa, B, L, *, name, scatter=()):
    T = B * L
    ns = L // A_SUPER
    nch = A_SUPER // A_CHUNK
    ex = _Exchange(scatter=scatter)
    grid = (B, 4, ns)

    def body(q_ref, f_ref, v_ref, g_ref, lb_ref, w_ref, o_ref, s_ref, doa_ref,
             dq_ref, df_ref, dv_ref, dg_ref, dw_ref, dlb_ref, dst_ref, sc_ref, dsc_ref, dqd_ref, dke_ref, dblx_ref, dvacc_ref):
        @pl.when(pl.program_id(2) == 0)
        def _():
            dst_ref[...] = jnp.zeros_like(dst_ref)
            dw_ref[...] = jnp.zeros_like(dw_ref)
            dlb_ref[...] = jnp.zeros_like(dlb_ref)

        same, tril, triu = _chunk_masks()
        q, v, g, lb, w = q_ref[...], v_ref[...], g_ref[...], lb_ref[...], w_ref[...]
        sig, f, lf, k = _hgrn_gates(f_ref[...], lb)
        bcs = _dot_mask(tril, lf)
        bl = _chunk_row(bcs, -1)
        eb, enb, eeb = jnp.exp(bcs), jnp.exp(-bcs), jnp.exp(bl - bcs)
        qd, ki, ke = q * eb, k * enb, k * eeb
        qdb, kib, keb, vb = qd.astype(BF16), ki.astype(BF16), ke.astype(BF16), v.astype(BF16)
        dec = jnp.exp(bl)
        o = o_ref[...]
        r = lax.rsqrt(jnp.mean(o * o, axis=-1, keepdims=True) + NORM_EPS)
        on = o * r
        sg = _sigmoid(g)
        silu_g = g * sg
        doa = doa_ref[...]
        dg_ref[...] = (doa * on * w * (sg * (1.0 + g * (1.0 - sg)))).astype(BF16)
        dw_ref[0] += jnp.sum(doa * on * silu_g, axis=0, keepdims=True)
        don = doa * w * silu_g
        do = r * (don - on * jnp.mean(don * on, axis=-1, keepdims=True))
        dob = do.astype(BF16)
        a = jnp.where(tril, _dot(qdb, kib, _NT), 0.0).astype(BF16)
        da = jnp.where(tril, _dot(dob, vb, _NT), 0.0).astype(BF16)
        dvacc_ref[...] = _dot(a, dob, _TN)
        dqd_ref[...] = _dot(da, kib)
        dki = _dot(da, qdb, _TN)
        chunks = [slice(c * A_CHUNK, (c + 1) * A_CHUNK) for c in range(nch)]
        outer = [_dot(vb[rs], keb[rs], _TN) for rs in chunks]
        st = s_ref[0]
        for c in range(nch):
            sc_ref[c] = st
            st = st * dec[c * A_CHUNK:c * A_CHUNK + 1, :] + outer[c]
        outer_g = [_dot(dob[rs], qdb[rs], _TN) for rs in chunks]
        dst = dst_ref[...]
        for c in reversed(range(nch)):
            dsc_ref[c] = dst
            dst = dst * dec[c * A_CHUNK:c * A_CHUNK + 1, :] + outer_g[c]
        dst_ref[...] = dst
        for c, rs in enumerate(chunks):
            dec_c = dec[c * A_CHUNK:c * A_CHUNK + 1, :]
            dsc, stc = dsc_ref[c], sc_ref[c]
            dscb = dsc.astype(BF16)
            dvacc_ref[rs, :] += _dot(keb[rs], dscb, _NT)
            dke_ref[rs, :] = _dot(vb[rs], dscb)
            ddec = jnp.sum(dsc * stc, axis=0, keepdims=True)
            dqd_ref[rs, :] += _dot(dob[rs], stc.astype(BF16))
            dblx_ref[rs, :] = jnp.broadcast_to(ddec * dec_c, (A_CHUNK, A_HEAD))
        dqd, dke = dqd_ref[...], dke_ref[...]
        dv_ref[...] = dvacc_ref[...].astype(BF16)
        dq_ref[...] = (dqd * eb).astype(BF16)
        keke = dke * ke
        db = dqd * qd - dki * ki - keke
        sums = _dot_mask(triu, jnp.concatenate([db, keke], axis=1))
        dk = dki * enb + dke * eeb
        dlf = sums[:, :A_HEAD] + _chunk_row(sums[:, A_HEAD:], 0) + dblx_ref[...]
        dff = dlf / f - dk
        df_ref[...] = (dff * (1.0 - lb) * sig * (1.0 - sig)).astype(BF16)
        dlb_ref[0] += jnp.sum(dff * (1.0 - sig), axis=0, keepdims=True)

    def rev(n):
        return ns - 1 - n

    def zspec(off):
        return pl.BlockSpec((A_SUPER, A_HEAD), lambda b, h, n: (b * ns + rev(n), off + h))

    hvec = pl.BlockSpec((1, A_HEAD), lambda b, h, n: (0, h))
    ospec = pl.BlockSpec((A_SUPER, A_HEAD), lambda b, h, n: (b * ns + rev(n), h))
    acc = pl.BlockSpec((1, 1, A_HEAD), lambda b, h, n: (b * 4 + h, 0, 0))
    big = jax.ShapeDtypeStruct((T, A_WIDTH), BF16)
    small = jax.ShapeDtypeStruct((B * 4, 1, A_HEAD), F32)
    res = pl.pallas_call(
        _carrying(body, 9, 6, ex, grid) if ex.n else body, name=name, grid=grid,
        in_specs=[zspec(0), zspec(4), zspec(8), zspec(12), hvec, hvec, ospec,
                  pl.BlockSpec((1, A_HEAD, A_HEAD), lambda b, h, n: ((b * 4 + h) * ns + rev(n), 0, 0)), ospec] + ex.in_specs(),
        out_specs=[ospec, ospec, ospec, ospec, acc, acc] + ex.in_specs(),
        out_shape=[big, big, big, big, small, small] + ex.out_shapes(),
        scratch_shapes=[pltpu.VMEM((A_HEAD, A_HEAD), F32), pltpu.VMEM((nch, A_HEAD, A_HEAD), F32),
                        pltpu.VMEM((nch, A_HEAD, A_HEAD), F32),
                        pltpu.VMEM((A_SUPER, A_HEAD), F32), pltpu.VMEM((A_SUPER, A_HEAD), F32),
                        pltpu.VMEM((A_SUPER, A_HEAD), F32), pltpu.VMEM((A_SUPER, A_HEAD), F32)] + ex.scratch(),
        compiler_params=_cparams(*(("arbitrary",) * 3 if ex.n else ("parallel", "parallel", "arbitrary"))),
    )(z, z, z, z, lb, onw, o_raw, s_start, doa, *ex.operands())
    return tuple(res[:6]) + (list(res[6:]),)


def _rope_tables(L):
    half = A_HEAD // 2
    inv_freq = ROPE_THETA ** (-jnp.arange(half, dtype=F32) / half)
    ang = jnp.arange(L, dtype=F32)[:, None] * inv_freq[None, :]
    cos, sin = jnp.cos(ang), jnp.sin(ang)
    return jnp.concatenate([cos, cos], axis=-1), jnp.concatenate([-sin, sin], axis=-1)


def _rope_fwd(z, cos2, sin2, B, L, *, name):
    T = B * L
    tm = 512
    nl = L // tm

    def body(x_ref, c_ref, s_ref, q_ref, k_ref):
        c, s = c_ref[...], s_ref[...]
        for h in range(8):
            x = x_ref[:, h * A_HEAD:(h + 1) * A_HEAD]
            out = x * c + pltpu.roll(x, A_HEAD // 2, 1) * s
            o_ref = q_ref if h < 4 else k_ref
            o_ref[:, (h % 4) * A_HEAD:(h % 4 + 1) * A_HEAD] = out

    tab = pl.BlockSpec((tm, A_HEAD), lambda i: (i % nl, 0))
    out = pl.BlockSpec((tm, 512), lambda i: (i, 0))
    return pl.pallas_call(
        body, name=name, grid=(T // tm,),
        in_specs=[pl.BlockSpec((tm, 1024), lambda i: (i, 2)), tab, tab], out_specs=[out, out],
        out_shape=[jax.ShapeDtypeStruct((T, 512), F32)] * 2, compiler_params=_cparams("parallel"),
    )(z, cos2, sin2)


def _rope_bwd(dqs, dks, dvs, cos2, sin2, B, L, *, name):
    T = B * L
    tm = 512
    nl = L // tm

    def body(*refs):
        c, s = refs[9][...], refs[10][...]
        o_ref = refs[11]
        for part in range(3):
            a_ref, b_ref, c_ref = refs[3 * part:3 * part + 3]
            for h in range(4):
                cols = slice(h * A_HEAD, (h + 1) * A_HEAD)
                d = a_ref[:, cols] + b_ref[:, cols] + c_ref[:, cols]
                if part < 2:
                    d = d * c - pltpu.roll(d, A_HEAD // 2, 1) * s
                o_ref[:, part * 512 + h * A_HEAD:part * 512 + (h + 1) * A_HEAD] = d.astype(BF16)

    blk = pl.BlockSpec((tm, 512), lambda i: (i, 0))
    tab = pl.BlockSpec((tm, A_HEAD), lambda i: (i % nl, 0))
    return pl.pallas_call(
        body, name=name, grid=(T // tm,), in_specs=[blk] * 9 + [tab, tab],
        out_specs=pl.BlockSpec((tm, 1536), lambda i: (i, 0)),
        out_shape=jax.ShapeDtypeStruct((T, 1536), BF16), compiler_params=_cparams("parallel"),
    )(*dqs, *dks, *dvs, cos2, sin2)


def _band_masks():
    i = lax.broadcasted_iota(jnp.int32, (B_SPAN, B_SPAN), 0)
    j = lax.broadcasted_iota(jnp.int32, (B_SPAN, B_SPAN), 1)
    return i <= j, j <= i


class _DilPlan:
    def __init__(self, dil, B, L):
        self.dil, self.B, self.L = dil, B, L
        self.rows = 4 * B_SPAN if dil == 1 else B_SPAN * dil
        self.n = L // self.rows
        self.hr = 4 if dil == 4 else 1
        self.cw = 512 if dil == 1 else A_HEAD
        self.has_other = dil != 16
        self.grid = (B, self.n, 4 if dil == 16 else 1)
        if dil == 1:
            self.items = [(j, 0, h) for j in range(4) for h in range(4)]
        elif dil == 4:
            self.items = [(0, r, h) for r in range(4) for h in range(4)]
        else:
            self.items = [(0, r, 0) for r in range(16)]
        self.groups = [self.items[i:i + 4] for i in range(0, 16, 4)]

    def operands(self, arr):
        return [arr] * self.hr

    def specs(self, col0, role="cur"):
        n, nb128 = self.n, self.L // B_SPAN
        out = []
        for h in range(self.hr):
            cb = col0 // self.cw + h
            if role == "cur":
                out.append(pl.BlockSpec((self.rows, self.cw), lambda b, i, hh, cb=cb: (b * n + i, cb + hh)))
            elif self.dil == 1:
                shift = -1 if role == "prev" else 4
                out.append(pl.BlockSpec((B_SPAN, self.cw),
                                        lambda b, i, hh, cb=cb, shift=shift: (b * nb128 + jnp.clip(4 * i + shift, 0, nb128 - 1), cb)))
            else:
                shift = -1 if role == "prev" else 1
                out.append(pl.BlockSpec((self.rows, self.cw),
                                        lambda b, i, hh, cb=cb, shift=shift: (b * n + jnp.clip(i + shift, 0, n - 1), cb)))
        return out

    def out_spec(self):
        n = self.n
        if self.dil == 4:
            return pl.BlockSpec((self.rows, 512), lambda b, i, hh: (b * n + i, 0))
        return pl.BlockSpec((self.rows, self.cw), lambda b, i, hh: (b * n + i, hh))

    def scratch(self, n_out):
        return [pltpu.VMEM((4, self.rows, A_HEAD), F32)] * n_out if self.dil == 4 else []

    def store(self, out_ref, scr, item, val):
        j, r, h = item
        if self.dil == 1:
            out_ref[pl.ds(j * B_SPAN, B_SPAN), pl.ds(h * A_HEAD, A_HEAD)] = val
        elif self.dil == 4:
            scr.at[h][pl.ds(r, B_SPAN, stride=4), :] = val
        else:
            out_ref[pl.ds(r, B_SPAN, stride=self.dil), :] = val

    def flush(self, out_ref, scr):
        if self.dil == 4:
            for h in range(4):
                out_ref[:, h * A_HEAD:(h + 1) * A_HEAD] = scr[h]

    def cur(self, refs, item):
        j, r, h = item
        if self.dil == 1:
            return refs[0], (pl.ds(j * B_SPAN, B_SPAN), pl.ds(h * A_HEAD, A_HEAD))
        return refs[h], (pl.ds(r, B_SPAN, stride=self.dil), slice(None))

    def other(self, refs, other_refs, item, role):
        j, r, h = item
        if self.dil == 1:
            cols = pl.ds(h * A_HEAD, A_HEAD)
            jj = j - 1 if role == "prev" else j + 1
            if 0 <= jj < 4:
                return refs[0], (pl.ds(jj * B_SPAN, B_SPAN), cols)
            return other_refs[0], (pl.ds(0, B_SPAN), cols)
        return other_refs[h], (pl.ds(r, B_SPAN, stride=self.dil), slice(None))

    def other_valid(self, item, role):
        j = item[0]
        i = pl.program_id(1)
        if role == "prev":
            return True if (self.dil == 1 and j > 0) else i > 0
        return True if (self.dil == 1 and j < 3) else i < self.n - 1


def _ld(pair):
    ref, idx = pair
    return ref[idx]


def _dil_fwd(qr, kr, z, dil, B, L, *, name):
    T = B * L
    plan = _DilPlan(dil, B, L)
    hr, has_prev = plan.hr, plan.has_other
    scale = A_HEAD ** -0.5

    n_t = 5 if has_prev else 3

    def body(*refs):
        lists = [refs[i * hr:(i + 1) * hr] for i in range(n_t)]
        o_ref, l_ref = refs[n_t * hr], refs[n_t * hr + 1]
        scr = refs[n_t * hr + 2:] or (None, None)
        if has_prev:
            q_r, kc_r, vc_r, kp_r, vp_r = lists
        else:
            q_r, kc_r, vc_r = lists
        mp0, mc = _band_masks()

        for group in plan.groups:
            qs = [_ld(plan.cur(q_r, i)).astype(BF16) for i in group]
            sc = [jnp.where(mc, _dot(q, _ld(plan.cur(kc_r, i)).astype(BF16), _NT) * scale, NEG_BIG) for q, i in zip(qs, group)]
            m = [jnp.max(s, axis=-1, keepdims=True) for s in sc]
            if has_prev:
                mps = [mp0 & plan.other_valid(i, "prev") for i in group]
                sp = [jnp.where(mk, _dot(q, _ld(plan.other(kc_r, kp_r, i, "prev")).astype(BF16), _NT) * scale, NEG_BIG)
                      for q, i, mk in zip(qs, group, mps)]
                m = [jnp.maximum(a, jnp.max(s, axis=-1, keepdims=True)) for a, s in zip(m, sp)]
            pc = [jnp.exp(s - a) for s, a in zip(sc, m)]
            l = [jnp.sum(p, axis=-1, keepdims=True) for p in pc]
            o = [_dot(p.astype(BF16), _ld(plan.cur(vc_r, i)).astype(BF16)) for p, i in zip(pc, group)]
            if has_prev:
                pp = [jnp.exp(s - a) for s, a in zip(sp, m)]
                l = [a + jnp.sum(p, axis=-1, keepdims=True) for a, p in zip(l, pp)]
                o = [a + _dot(p.astype(BF16), _ld(plan.other(vc_r, vp_r, i, "prev")).astype(BF16)) for a, p, i in zip(o, pp, group)]
            for i, oi, li, mi in zip(group, o, l, m):
                plan.store(o_ref, scr[0], i, oi / li)
                plan.store(l_ref, scr[1], i, jnp.broadcast_to(mi + jnp.log(li), (B_SPAN, A_HEAD)))
        plan.flush(o_ref, scr[0])
        plan.flush(l_ref, scr[1])

    tensors = [(qr, 0, "cur"), (kr, 0, "cur"), (z, 3072, "cur")] + ([(kr, 0, "prev"), (z, 3072, "prev")] if has_prev else [])
    return pl.pallas_call(
        body, name=name, grid=plan.grid,
        in_specs=[sp for _, c, role in tensors for sp in plan.specs(c, role)],
        out_specs=[plan.out_spec()] * 2, out_shape=[jax.ShapeDtypeStruct((T, 512), F32)] * 2,
        scratch_shapes=plan.scratch(2), compiler_params=_cparams("parallel", "parallel", "parallel"),
    )(*[a for arr, _, _ in tensors for a in plan.operands(arr)])


def _dil_combine(os_, ls_, *, name):
    T = os_[0].shape[0]
    tm = 512

    def body(o1, o2, o3, l1, l2, l3, ob_ref, lse_ref):
        a1, a2, a3 = l1[...], l2[...], l3[...]
        m = jnp.maximum(jnp.maximum(a1, a2), a3)
        e1, e2, e3 = jnp.exp(a1 - m), jnp.exp(a2 - m), jnp.exp(a3 - m)
        den = e1 + e2 + e3
        ob_ref[...] = (e1 * o1[...] + e2 * o2[...] + e3 * o3[...]) / den
        lse_ref[...] = m + jnp.log(den)

    blk = pl.BlockSpec((tm, 512), lambda i: (i, 0))
    return pl.pallas_call(
        body, name=name, grid=(T // tm,), in_specs=[blk] * 6, out_specs=[blk, blk],
        out_shape=[jax.ShapeDtypeStruct((T, 512), F32)] * 2, compiler_params=_cparams("parallel"),
    )(*[o.reshape(T, 512) for o in os_], *[l.reshape(T, 512) for l in ls_])


def _dil_bwd_q(qr, kr, z, dymix, out, lse, dil, B, L, *, name):
    T = B * L
    plan = _DilPlan(dil, B, L)
    hr, has_prev = plan.hr, plan.has_other
    scale = A_HEAD ** -0.5
    n_t = 8 if has_prev else 6

    def body(*refs):
        lists = [refs[i * hr:(i + 1) * hr] for i in range(n_t)]
        dq_ref = refs[n_t * hr]
        scr = refs[n_t * hr + 1:] or (None,)
        q_r, kc_r, vc_r, do_r, out_r, lse_r = lists[:6]
        mp0, mc = _band_masks()

        for group in plan.groups:
            qs = [_ld(plan.cur(q_r, i)).astype(BF16) for i in group]
            dos = [_ld(plan.cur(do_r, i)) for i in group]
            delta = [jnp.sum(d * _ld(plan.cur(out_r, i)), axis=-1, keepdims=True) for d, i in zip(dos, group)]
            dob = [d.astype(BF16) for d in dos]
            lse = [_ld(plan.cur(lse_r, i)) for i in group]
            kc = [_ld(plan.cur(kc_r, i)).astype(BF16) for i in group]
            pc = [jnp.where(mc, jnp.exp(_dot(q, k, _NT) * scale - a), 0.0) for q, k, a in zip(qs, kc, lse)]
            dsc = [p * (_dot(d, _ld(plan.cur(vc_r, i)).astype(BF16), _NT) - dl) * scale
                   for p, d, i, dl in zip(pc, dob, group, delta)]
            dq = [_dot(d.astype(BF16), k) for d, k in zip(dsc, kc)]
            if has_prev:
                kp_r, vp_r = lists[6], lists[7]
                mps = [mp0 & plan.other_valid(i, "prev") for i in group]
                kp = [_ld(plan.other(kc_r, kp_r, i, "prev")).astype(BF16) for i in group]
                pp = [jnp.where(mk, jnp.exp(_dot(q, k, _NT) * scale - a), 0.0) for q, k, a, mk in zip(qs, kp, lse, mps)]
                dsp = [p * (_dot(d, _ld(plan.other(vc_r, vp_r, i, "prev")).astype(BF16), _NT) - dl) * scale
                       for p, d, i, dl in zip(pp, dob, group, delta)]
                dq = [a + _dot(d.astype(BF16), k) for a, d, k in zip(dq, dsp, kp)]
            for i, d in zip(group, dq):
                plan.store(dq_ref, scr[0], i, d)
        plan.flush(dq_ref, scr[0])

    tensors = ([(qr, 0, "cur"), (kr, 0, "cur"), (z, 3072, "cur"), (dymix, 512, "cur"), (out, 0, "cur"), (lse, 0, "cur")]
               + ([(kr, 0, "prev"), (z, 3072, "prev")] if has_prev else []))
    return pl.pallas_call(
        body, name=name, grid=plan.grid,
        in_specs=[sp for _, c, role in tensors for sp in plan.specs(c, role)],
        out_specs=plan.out_spec(), out_shape=jax.ShapeDtypeStruct((T, 512), F32),
        scratch_shapes=plan.scratch(1), compiler_params=_cparams("parallel", "parallel", "parallel"),
    )(*[a for arr, _, _ in tensors for a in plan.operands(arr)])


def _dil_bwd_kv(qr, kr, z, dymix, out, lse, dil, B, L, *, name):
    T = B * L
    plan = _DilPlan(dil, B, L)
    hr, has_next = plan.hr, plan.has_other
    scale = A_HEAD ** -0.5
    n_t = 10 if has_next else 6

    def body(*refs):
        lists = [refs[i * hr:(i + 1) * hr] for i in range(n_t)]
        dk_ref, dv_ref = refs[n_t * hr], refs[n_t * hr + 1]
        scr = refs[n_t * hr + 2:] or (None, None)
        k_r, v_r = lists[0], lists[1]
        own = lists[2:6]
        mp0, mc = _band_masks()

        for group in plan.groups:
            kh = [_ld(plan.cur(k_r, i)).astype(BF16) for i in group]
            vh = [_ld(plan.cur(v_r, i)).astype(BF16) for i in group]
            dk, dv = [None] * len(group), [None] * len(group)
            for role in ("own", "next") if has_next else ("own",):
                if role == "own":
                    get = lambda t, i: _ld(plan.cur(own[t], i))
                    masks = [mc] * len(group)
                else:
                    get = lambda t, i: _ld(plan.other(own[t], lists[6 + t], i, "next"))
                    masks = [mp0 & plan.other_valid(i, "next") for i in group]
                qs = [get(0, i).astype(BF16) for i in group]
                dos = [get(1, i) for i in group]
                delta = [jnp.sum(d * get(2, i), axis=-1, keepdims=True) for d, i in zip(dos, group)]
                dob = [d.astype(BF16) for d in dos]
                p = [jnp.where(mk, jnp.exp(_dot(q, k, _NT) * scale - get(3, i)), 0.0) for q, k, i, mk in zip(qs, kh, group, masks)]
                dvn = [_dot(a.astype(BF16), d, _TN) for a, d in zip(p, dob)]
                ds = [a * (_dot(d, v, _NT) - dl) * scale for a, d, v, dl in zip(p, dob, vh, delta)]
                dkn = [_dot(d.astype(BF16), q, _TN) for d, q in zip(ds, qs)]
                dv = [n if o is None else o + n for o, n in zip(dv, dvn)]
                dk = [n if o is None else o + n for o, n in zip(dk, dkn)]
            for i, a, b in zip(group, dk, dv):
                plan.store(dk_ref, scr[0], i, a)
                plan.store(dv_ref, scr[1], i, b)
        plan.flush(dk_ref, scr[0])
        plan.flush(dv_ref, scr[1])

    queries = [(qr, 0), (dymix, 512), (out, 0), (lse, 0)]
    tensors = ([(kr, 0, "cur"), (z, 3072, "cur")] + [(a, c, "cur") for a, c in queries]
               + ([(a, c, "next") for a, c in queries] if has_next else []))
    return pl.pallas_call(
        body, name=name, grid=plan.grid,
        in_specs=[sp for _, c, role in tensors for sp in plan.specs(c, role)],
        out_specs=[plan.out_spec()] * 2, out_shape=[jax.ShapeDtypeStruct((T, 512), F32)] * 2,
        scratch_shapes=plan.scratch(2), compiler_params=_cparams("parallel", "parallel", "parallel"),
    )(*[a for arr, _, _ in tensors for a in plan.operands(arr)])


def _s5_build(lam_re, lam_im, log_dt, b_re, b_im, c_re, c_im):
    G, P, TC = C_GROUPS, C_STATE, C_TC
    lr = jnp.minimum(lam_re, C_MIN_NEG_RE)
    li = lam_im
    dt = jnp.exp(log_dt)[:, None]
    mag = jnp.exp(dt * lr)
    ar, ai = mag * jnp.cos(dt * li), mag * jnp.sin(dt * li)
    den = lr * lr + li * li
    zr = ((ar - 1.0) * lr + ai * li) / den
    zi = (ai * lr - (ar - 1.0) * li) / den
    bbr = zr[..., None] * b_re - zi[..., None] * b_im
    bbi = zr[..., None] * b_im + zi[..., None] * b_re
    ks = jnp.arange(TC + 1, dtype=F32)[:, None, None]
    pmag = jnp.exp(ks * (dt * lr)[None])
    pr, pi = pmag * jnp.cos(ks * (dt * li)[None]), pmag * jnp.sin(ks * (dt * li)[None])
    car = c_re[None] * pr[:, :, None, :] - c_im[None] * pi[:, :, None, :]
    cai = c_re[None] * pi[:, :, None, :] + c_im[None] * pr[:, :, None, :]
    kern = (jnp.einsum('lgop,gpc->lgco', car[:TC], bbr, precision=HI)
            - jnp.einsum('lgop,gpc->lgco', cai[:TC], bbi, precision=HI))
    pr_e, pi_e = pr[TC - 1 - jnp.arange(TC)], pi[TC - 1 - jnp.arange(TC)]
    er = pr_e[:, :, :, None] * bbr[None] - pi_e[:, :, :, None] * bbi[None]
    ei = pr_e[:, :, :, None] * bbi[None] + pi_e[:, :, :, None] * bbr[None]
    ez = jnp.stack([er, ei], axis=2).reshape(TC, C_NB, C_GB, 2, P, C_GROUP).transpose(1, 0, 2, 5, 3, 4)
    fz = jnp.stack([car[1:], -cai[1:]], axis=0).reshape(2, TC, C_NB, C_GB, C_GROUP, P).transpose(2, 0, 3, 5, 1, 4)
    return kern, ez, fz, pr[TC], pi[TC]


def _s5_lag_blocks(kern):
    eye = jnp.eye(C_GB, dtype=kern.dtype)
    return (kern.reshape(C_TC, C_NB, C_GB, C_GROUP, C_GROUP)[:, :, :, :, None, :]
            * eye[None, None, :, None, :, None]).reshape(C_TC, C_NB, LANES, LANES)


def _s5_dense(kbd, ez, fz, *, name):
    ezc = ez.reshape(C_NB, C_W8, LANES)
    fzc = fz.reshape(C_NB, C_S8, LANES)
    half = C_GB * C_STATE

    def body(k_ref, e_ref, f_ref, m8_ref, e8_ref, f8_ref):
        zero = jnp.zeros((LANES, LANES), BF16)
        for s in range(C_TC):
            for t in range(C_TC):
                m8_ref[0, s * LANES:(s + 1) * LANES, t * LANES:(t + 1) * LANES] = (
                    k_ref[t - s, 0].astype(BF16) if t >= s else zero)
        lane = lax.broadcasted_iota(jnp.int32, (1, LANES), 1)
        src = e_ref[0]
        swapped = pltpu.roll(src, C_STATE, 1)
        rowg = jnp.bitwise_and(jnp.right_shift(lax.broadcasted_iota(jnp.int32, (C_W8, LANES), 0), 4), C_GB - 1)
        for kb in range(C_S8 // LANES):
            z, g0 = kb // (C_GB // 2), 2 * (kb % (C_GB // 2))
            first = jnp.where(lane < C_STATE, src if z == 0 else swapped, 0.0)
            second = jnp.where(lane >= C_STATE, swapped if z == 0 else src, 0.0)
            e8_ref[0, :, kb * LANES:(kb + 1) * LANES] = jnp.where(
                rowg == g0, first, jnp.where(rowg == g0 + 1, second, 0.0)).astype(BF16)
        for z in range(2):
            for g in range(C_GB):
                rows = slice(z * half + g * C_STATE, z * half + (g + 1) * C_STATE)
                piece = f_ref[0, rows, :]
                mine = (lane >= g * C_GROUP) & (lane < (g + 1) * C_GROUP)
                for t in range(C_TC):
                    f8_ref[0, rows, t * LANES:(t + 1) * LANES] = jnp.where(
                        mine, pltpu.roll(piece, ((g - t) * C_GROUP) % LANES, 1), 0.0).astype(BF16)

    blk = lambda r, c: pl.BlockSpec((1, r, c), lambda b: (b, 0, 0))
    return pl.pallas_call(
        body, name=name, grid=(C_NB,),
        in_specs=[pl.BlockSpec((C_TC, 1, LANES, LANES), lambda b: (0, b, 0, 0)), blk(C_W8, LANES), blk(C_S8, LANES)],
        out_specs=[blk(C_W8, C_W8), blk(C_W8, C_S8), blk(C_S8, C_W8)],
        out_shape=[jax.ShapeDtypeStruct((C_NB, C_W8, C_W8), BF16), jax.ShapeDtypeStruct((C_NB, C_W8, C_S8), BF16),
                   jax.ShapeDtypeStruct((C_NB, C_S8, C_W8), BF16)],
        compiler_params=_cparams("parallel"),
    )(kbd, ezc, fzc)


C_NB = C_GROUPS * C_GROUP // LANES
C_GB = C_GROUPS // C_NB
C_W8 = C_TC * LANES
C_S8 = 2 * C_GB * C_STATE


def _s5_scan_tables(lam_re, lam_im, log_dt, nsteps):
    lr = jnp.minimum(lam_re, C_MIN_NEG_RE)
    dt = jnp.exp(log_dt)[:, None]
    ks = (C_TC * 2.0 ** jnp.arange(8, dtype=F32))[None, :, None]
    keep = (jnp.arange(8) < nsteps)[None, :, None]
    pmag = jnp.exp(ks * (dt * lr)[:, None, :])
    ang = ks * (dt * lam_im)[:, None, :]

    def blocks(t):
        return t.reshape(C_NB, C_GB, 8, C_STATE).transpose(0, 2, 1, 3).reshape(C_NB, 8, C_GB * C_STATE)

    pr = blocks(jnp.where(keep, pmag * jnp.cos(ang), 0.0))
    pi = blocks(jnp.where(keep, pmag * jnp.sin(ang), 0.0))
    return jnp.concatenate([pr, pr], axis=-1), jnp.concatenate([-pi, pi], axis=-1)


def _s5_rows(t, R):
    return pl.ds(t, R, stride=C_TC)


def _s5_fwd(u, dsk, m8, e8, f8, tab_r, tab_i, B, L, *, name, gather=()):
    T = B * L
    R = L // C_TC
    nsteps = int(math.log2(R))
    ex = _Exchange(gather=gather)
    grid = (C_NB, B)

    def body(u_ref, d_ref, m_ref, e_ref, f_ref, tr_ref, ti_ref, gl_ref, y_ref, x8_ref, xs_ref):
        for t in range(C_TC):
            x8_ref[0, :, t * LANES:(t + 1) * LANES] = u_ref[_s5_rows(t, R), :].astype(BF16)
        x8 = x8_ref[0]
        x = _dot(x8, e_ref[0])
        row = lax.broadcasted_iota(jnp.int32, (R, C_S8), 0)
        for k in range(nsteps):
            s = 1 << k
            sh = pltpu.roll(x, s, 0)
            upd = tr_ref[0, k:k + 1, :] * sh + ti_ref[0, k:k + 1, :] * pltpu.roll(sh, C_S8 // 2, 1)
            x = x + jnp.where(row >= s, upd, 0.0)
        xs = jnp.where(row >= 1, pltpu.roll(x, 1, 0), 0.0)
        xs_ref[0] = xs
        y8 = _dot(x8, m_ref[0]) + _dot(xs.astype(BF16), f_ref[0])
        d = d_ref[...]
        for t in range(C_TC):
            rows = _s5_rows(t, R)
            y = y8[:, t * LANES:(t + 1) * LANES] + d * u_ref[rows, :]
            y_ref[rows, :] = y
            gl_ref[rows, :] = 0.5 * y * (1.0 + _erf(y * (2.0 ** -0.5)))

    tok = pl.BlockSpec((L, LANES), lambda c, b: (b, c))
    per_block = lambda shape: pl.BlockSpec((1,) + shape, lambda c, b: (c, 0, 0))
    per_step = lambda shape: pl.BlockSpec((1,) + shape, lambda c, b: (c * B + b, 0, 0))
    res = pl.pallas_call(
        _carrying(body, 7, 4, ex, grid) if ex.n else body, name=name, grid=grid,
        in_specs=[tok, pl.BlockSpec((1, LANES), lambda c, b: (0, c)), per_block((C_W8, C_W8)), per_block((C_W8, C_S8)),
                  per_block((C_S8, C_W8)), per_block((8, C_S8)), per_block((8, C_S8))] + ex.in_specs(),
        out_specs=[tok, tok, per_step((R, C_W8)), per_step((R, C_S8))] + ex.in_specs(),
        out_shape=[jax.ShapeDtypeStruct((T, D_MODEL), F32), jax.ShapeDtypeStruct((T, D_MODEL), F32),
                   jax.ShapeDtypeStruct((C_NB * B, R, C_W8), BF16), jax.ShapeDtypeStruct((C_NB * B, R, C_S8), F32)] + ex.out_shapes(),
        scratch_shapes=ex.scratch(),
        compiler_params=_cparams(*(("arbitrary",) * 2 if ex.n else ("parallel", "parallel"))),
    )(u, dsk, m8, e8, f8, tab_r, tab_i, *ex.operands())
    return tuple(res[:4]) + (list(res[4:]),)


def _s5_bwd(dgl, y, u, dsk, xs, m8, e8, f8, tab_r, tab_i, B, L, *, name):
    T = B * L
    R = L // C_TC
    nsteps = int(math.log2(R))

    def body(dgl_ref, y_ref, u_ref, d_ref, xs_ref, m_ref, e_ref, f_ref, tr_ref, ti_ref,
             du_ref, dy8_ref, de_ref, da_ref, dd_ref, dyf_ref):
        @pl.when(pl.program_id(1) == 0)
        def _():
            da_ref[...] = jnp.zeros_like(da_ref)
            dd_ref[...] = jnp.zeros_like(dd_ref)

        dd = jnp.zeros((1, LANES), F32)
        for t in range(C_TC):
            rows = _s5_rows(t, R)
            yv = y_ref[rows, :]
            cdf = 0.5 * (1.0 + _erf(yv * (2.0 ** -0.5)))
            pdf = jnp.exp(-0.5 * yv * yv) * (1.0 / math.sqrt(2.0 * math.pi))
            dy = dgl_ref[rows, :] * (cdf + yv * pdf)
            dd = dd + jnp.sum(dy * u_ref[rows, :], axis=0, keepdims=True)
            dyf_ref[:, t * LANES:(t + 1) * LANES] = dy
        dd_ref[...] += dd
        dy8 = dyf_ref[...].astype(BF16)
        dy8_ref[0] = dy8
        xs = xs_ref[0]
        gx = _dot(dy8, f_ref[0], _NT)
        row = lax.broadcasted_iota(jnp.int32, (R, C_S8), 0)
        for k in range(nsteps):
            s = 1 << k
            sh = pltpu.roll(gx, R - s, 0)
            upd = tr_ref[0, k:k + 1, :] * sh - ti_ref[0, k:k + 1, :] * pltpu.roll(sh, C_S8 // 2, 1)
            gx = gx + jnp.where(row + s < R, upd, 0.0)
        de_in = jnp.where(row + 1 < R, pltpu.roll(gx, R - 1, 0), 0.0)
        deb = de_in.astype(BF16)
        de_ref[0] = deb
        da_ref[0, 0:1, :] += jnp.sum(de_in * xs, axis=0, keepdims=True)
        da_ref[0, 1:2, :] += jnp.sum(de_in * pltpu.roll(xs, C_S8 // 2, 1), axis=0, keepdims=True)
        dx8 = _dot(dy8, m_ref[0], _NT) + _dot(deb, e_ref[0], _NT)
        d = d_ref[...]
        for t in range(C_TC):
            cols = slice(t * LANES, (t + 1) * LANES)
            du_ref[_s5_rows(t, R), :] = dx8[:, cols] + d * dyf_ref[:, cols]

    tok = pl.BlockSpec((L, LANES), lambda c, b: (b, c))
    vec = pl.BlockSpec((1, LANES), lambda c, b: (0, c))
    per_block = lambda shape: pl.BlockSpec((1,) + shape, lambda c, b: (c, 0, 0))
    per_step = lambda shape: pl.BlockSpec((1,) + shape, lambda c, b: (c * B + b, 0, 0))
    return pl.pallas_call(
        body, name=name, grid=(C_NB, B),
        in_specs=[tok, tok, tok, vec, per_step((R, C_S8)), per_block((C_W8, C_W8)),
                  per_block((C_W8, C_S8)), per_block((C_S8, C_W8)), per_block((8, C_S8)), per_block((8, C_S8))],
        out_specs=[tok, per_step((R, C_W8)), per_step((R, C_S8)), per_block((8, C_S8)), vec],
        out_shape=[jax.ShapeDtypeStruct((T, D_MODEL), F32), jax.ShapeDtypeStruct((C_NB * B, R, C_W8), BF16),
                   jax.ShapeDtypeStruct((C_NB * B, R, C_S8), BF16), jax.ShapeDtypeStruct((C_NB, 8, C_S8), F32),
                   jax.ShapeDtypeStruct((1, D_MODEL), F32)],
        scratch_shapes=[pltpu.VMEM((R, C_W8), F32)],
        compiler_params=_cparams("parallel", "arbitrary"),
    )(dgl, y, u, dsk, xs, m8, e8, f8, tab_r, tab_i)


def _bmm_tn(a, b, nb, fold, *, name):
    a = a.reshape(nb, -1, a.shape[-1])
    b = b.reshape(nb, -1, b.shape[-1])
    K, M, N = a.shape[1], a.shape[2], b.shape[2]
    half = C_GB * C_STATE

    def body(a_ref, b_ref, o_ref, p_ref):
        p_ref[...] = _dot(a_ref[0].astype(BF16), b_ref[0].astype(BF16), _TN)
        lane = lax.broadcasted_iota(jnp.int32, (1, LANES), 1)
        if fold == "lags":
            for lag in range(C_TC):
                blocks = [p_ref[s * LANES:(s + 1) * LANES, (s + lag) * LANES:(s + lag + 1) * LANES] for s in range(C_TC - lag)]
                o_ref[0, lag] = functools.reduce(lambda u, v: u + v, blocks)
        elif fold == "e":
            for g in range(C_GB):
                lo, hi = LANES * (g // 2), half + LANES * (g // 2)
                for s in range(C_TC):
                    rows = slice(s * LANES + g * C_GROUP, s * LANES + (g + 1) * C_GROUP)
                    re, im = p_ref[rows, lo:lo + LANES], p_ref[rows, hi:hi + LANES]
                    if g % 2 == 0:
                        im = pltpu.roll(im, C_STATE, 1)
                    else:
                        re = pltpu.roll(re, C_STATE, 1)
                    o_ref[0, rows, :] = jnp.where(lane < C_STATE, re, im)
        else:
            for z in range(2):
                for g in range(C_GB):
                    rows = slice(z * half + g * C_STATE, z * half + (g + 1) * C_STATE)
                    val = jnp.zeros((C_STATE, LANES), F32)
                    for t in range(C_TC):
                        blk = pltpu.roll(p_ref[rows, t * LANES:(t + 1) * LANES], ((t - g) * C_GROUP) % LANES, 1)
                        val = jnp.where((lane >= t * C_GROUP) & (lane < (t + 1) * C_GROUP), blk, val)
                    o_ref[0, rows, :] = val

    if fold == "lags":
        out_spec = pl.BlockSpec((1, C_TC, LANES, LANES), lambda c: (c, 0, 0, 0))
        out_shape = jax.ShapeDtypeStruct((nb, C_TC, LANES, LANES), F32)
    else:
        out_spec = pl.BlockSpec((1, M, LANES), lambda c: (c, 0, 0))
        out_shape = jax.ShapeDtypeStruct((nb, M, LANES), F32)
    return pl.pallas_call(
        body, name=name, grid=(nb,),
        in_specs=[pl.BlockSpec((1, K, M), lambda c: (c, 0, 0)), pl.BlockSpec((1, K, N), lambda c: (c, 0, 0))],
        out_specs=out_spec, out_shape=out_shape, scratch_shapes=[pltpu.VMEM((M, N), F32)],
        compiler_params=_cparams("parallel"),
    )(a, b)


def _loss_head(y, target, *, name):
    T, C = y.shape
    tm = 512

    def body(y_ref, t_ref, l_ref, d_ref):
        err = y_ref[...] - t_ref[...]
        d_ref[...] = err * (1.0 / C)
        sq = err * err
        part = jnp.zeros((8, LANES), F32)
        for r in range(0, tm, 8):
            for c in range(0, C, LANES):
                part = part + sq[r:r + 8, c:c + LANES]

        @pl.when(pl.program_id(0) == 0)
        def _():
            l_ref[...] = part

        @pl.when(pl.program_id(0) > 0)
        def _():
            l_ref[...] += part

    row = pl.BlockSpec((tm, C), lambda i: (i, 0))
    acc = pl.BlockSpec((8, LANES), lambda i: (0, 0))
    return pl.pallas_call(
        body, name=name, grid=(T // tm,), in_specs=[row, row], out_specs=[acc, row],
        out_shape=[jax.ShapeDtypeStruct((8, LANES), F32), jax.ShapeDtypeStruct((T, C), F32)],
        compiler_params=_cparams("arbitrary"),
    )(y, target)


def _adamw(w, g, m, v, *, name):
    shape = w.shape
    size = int(np.prod(shape))
    cols = LANES if (shape[-1] < LANES and size % LANES == 0) else shape[-1]
    rows = size // cols
    tm = _tile(rows, 256) if rows % 8 == 0 else rows
    w2, g2, m2, v2 = (t.reshape(rows, cols) for t in (w, g, m, v))

    def body(w_ref, g_ref, m_ref, v_ref, d_ref, nm_ref, nv_ref):
        gg = g_ref[...]
        nm = ADAM_B1 * m_ref[...] + (1.0 - ADAM_B1) * gg
        nv = ADAM_B2 * v_ref[...] + (1.0 - ADAM_B2) * (gg * gg)
        m_hat = nm / (1.0 - ADAM_B1 ** ADAM_STEP)
        v_hat = nv / (1.0 - ADAM_B2 ** ADAM_STEP)
        d_ref[...] = -ADAM_LR * (m_hat / (jnp.sqrt(v_hat) + ADAM_EPS) + ADAM_WD * w_ref[...])
        nm_ref[...] = nm
        nv_ref[...] = nv

    blk = pl.BlockSpec((tm, cols), lambda i: (i, 0))
    outs = pl.pallas_call(
        body, name=name, grid=(rows // tm,), in_specs=[blk] * 4, out_specs=[blk] * 3,
        out_shape=[jax.ShapeDtypeStruct((rows, cols), F32)] * 3, compiler_params=_cparams("parallel"),
    )(w2, g2, m2, v2)
    return tuple(o.reshape(shape) for o in outs)


def _all_gather(shard, *, name):
    R, C = shard.shape

    def body(x_ref, out_ref, send_sems, recv_sems, local_sem):
        x, y, c = lax.axis_index("x"), lax.axis_index("y"), lax.axis_index("c")
        me, sibling = (x, y, c), (x, y, 1 - c)
        chips = [(1 - x, y), (x, 1 - y), (1 - x, 1 - y)]

        def rows(px, py, pc):
            return out_ref.at[_logical(px, py, pc)]

        def copy(k, block, to, src=None):
            return pltpu.make_async_remote_copy(
                src_ref=rows(*block) if src is None else src, dst_ref=rows(*block),
                send_sem=send_sems.at[k], recv_sem=recv_sems.at[k], device_id=to, device_id_type=_MESH)

        mine = pltpu.make_async_copy(x_ref, rows(*me), local_sem)
        mine.start()
        first = [copy(0, me, sibling, src=x_ref)]
        first += [copy(1 + j, me, (*chip, c), src=x_ref) for j, chip in enumerate(chips)]
        for cp in first:
            cp.start()
        passed = [copy(4 + j, (*chip, c), sibling) for j, chip in enumerate(chips)]
        for j, chip in enumerate(chips):
            copy(1 + j, (*chip, c), me).wait_recv()
            passed[j].start()
        copy(0, sibling, me).wait_recv()
        for j, chip in enumerate(chips):
            copy(4 + j, (*chip, 1 - c), me).wait_recv()
        for cp in first + passed:
            cp.wait_send()
        mine.wait()

    return pl.pallas_call(
        body, name=name, out_shape=jax.ShapeDtypeStruct((N_DEV, R, C), shard.dtype),
        in_specs=[_HBM], out_specs=_HBM,
        scratch_shapes=[pltpu.SemaphoreType.DMA((7,)), pltpu.SemaphoreType.DMA((7,)), pltpu.SemaphoreType.DMA],
    )(shard)


def _gather_weights(shards, *, name):
    nt = len(shards)

    def body(*refs):
        ins, outs = refs[:nt], refs[nt:2 * nt]
        send_sems, recv_sems, local_sems = refs[2 * nt:]
        x, y, c = lax.axis_index("x"), lax.axis_index("y"), lax.axis_index("c")
        me, sibling = (x, y, c), (x, y, 1 - c)
        chips = [(1 - x, y), (x, 1 - y), (1 - x, 1 - y)]

        def copy(t, k, block, to, src=None):
            rows = outs[t].at[_logical(*block)]
            return pltpu.make_async_remote_copy(
                src_ref=rows if src is None else src, dst_ref=rows,
                send_sem=send_sems.at[t, k], recv_sem=recv_sems.at[t, k], device_id=to, device_id_type=_MESH)

        mine = [pltpu.make_async_copy(ins[t], outs[t].at[_logical(*me)], local_sems.at[t]) for t in range(nt)]
        for cp in mine:
            cp.start()
        started = []
        for t in range(nt):
            started.append(copy(t, 0, me, sibling, src=ins[t]))
            started += [copy(t, 1 + j, me, (*chip, c), src=ins[t]) for j, chip in enumerate(chips)]
        for cp in started:
            cp.start()
        for j, chip in enumerate(chips):
            for t in range(nt):
                copy(t, 1 + j, (*chip, c), me).wait_recv()
                fwd = copy(t, 4 + j, (*chip, c), sibling)
                fwd.start()
                started.append(fwd)
        for t in range(nt):
            copy(t, 0, sibling, me).wait_recv()
            for j, chip in enumerate(chips):
                copy(t, 4 + j, (*chip, 1 - c), me).wait_recv()
        for cp in started:
            cp.wait_send()
        for cp in mine:
            cp.wait()

    return pl.pallas_call(
        body, name=name, out_shape=[jax.ShapeDtypeStruct((N_DEV,) + s.shape, s.dtype) for s in shards],
        in_specs=[_HBM] * nt, out_specs=[_HBM] * nt,
        scratch_shapes=[pltpu.SemaphoreType.DMA((nt, 7)), pltpu.SemaphoreType.DMA((nt, 7)), pltpu.SemaphoreType.DMA((nt,))],
    )(*shards)


def _sum_rows(stacked, *, name):
    _, R, C = stacked.shape
    tr = R
    if N_DEV * R * C * stacked.dtype.itemsize > 12 * 1024 * 1024:
        for cand in range(512, 15, -16):
            if R % cand == 0:
                tr = cand
                break

    def body(s_ref, o_ref):
        acc = s_ref[0].astype(F32)
        for k in range(1, N_DEV):
            acc = acc + s_ref[k].astype(F32)
        o_ref[...] = acc

    return pl.pallas_call(
        body, name=name, grid=(R // tr,),
        in_specs=[pl.BlockSpec((N_DEV, tr, C), lambda i: (0, i, 0))], out_specs=pl.BlockSpec((tr, C), lambda i: (i, 0)),
        out_shape=jax.ShapeDtypeStruct((R, C), F32), compiler_params=_cparams("parallel"),
    )(stacked)


_LARGE = (("ab_w_in", 1, True), ("ab_w_out", 1, False), ("s5_w_glu", 1, True), ("xattn_wq", 2, False),
          ("xattn_wkv", 2, True), ("xattn_wo", 2, False), ("ffn_w_in", 2, True), ("ffn_w_out", 2, False))
_LARGE_KEYS = tuple((n, l) for n, layers, _ in _LARGE for l in range(layers))
_TRANSPOSED = {n: t for n, _, t in _LARGE}


def _owner_major(name, w):
    return w.T if _TRANSPOSED[name] else w


def _lb_from_logits(logits):
    return jnp.cumsum(jax.nn.softmax(logits, axis=0), axis=0)[0:1]


def _local_step(x, mem, target, W, shards=None):
    B, L, _ = x.shape
    T = B * L
    x0 = x.reshape(T, D_MODEL)
    memf = mem.reshape(B * MEM_LEN, D_MODEL)
    nw = W["norm_w"]
    cos2, sin2 = _rope_tables(L)
    W = dict(W)
    G, received = {}, {}

    def mmx(a, b, gather=(), scatter=(), **kw):
        if shards is None or not (gather or scatter):
            return _mm(a, b, **kw)
        out, gathered, got = _mm(a, b, gather=[shards[k] for k in gather],
                                 scatter=[G[k].reshape(N_DEV, -1, D_MODEL) for k in scatter], **kw)
        for k, g in zip(gather, gathered):
            W[k] = g.reshape(-1, D_MODEL)
        for k, r in zip(scatter, got):
            received[k] = r
        return out

    def vec(v):
        return v.reshape(1, -1)

    saved = []
    xin = x0
    for layer in range(2):
        s = {"x0": xin}
        tag = f"l{layer}"
        if layer == 0:
            h1 = _rms_fwd(xin, vec(nw[0, 0]), name="norm_pre_mix_l0", out_dtype=BF16)
        s["h1"] = h1
        if layer == 0:
            lb, lb_vjp = jax.vjp(_lb_from_logits, W["hgrn_lb_logits"])
            onw = W["hgrn_out_norm_w"].reshape(1, A_WIDTH)
            z = mmx(h1, W["ab_w_in", 0], tb=True, name="ab_in",
                    gather=[("ab_w_out", 0), ("xattn_wq", 0), ("xattn_wkv", 0), ("xattn_wo", 0)])
            ffn0 = [("ffn_w_in", 0), ("ffn_w_out", 0)] if shards is not None else []
            oa, o_raw, s_start, gathered = _hgrn_fwd(z, lb, onw, B, L, name="hgrn_fwd", gather=[shards[k] for k in ffn0])
            for key, g in zip(ffn0, gathered):
                W[key] = g.reshape(-1, D_MODEL)
            qr, kr = _rope_fwd(z, cos2, sin2, B, L, name="rope_qk")
            os_, ls_ = [], []
            for dil in B_DILS:
                o_g, l_g = _dil_fwd(qr, kr, z, dil, B, L, name=f"dil_fwd_{dil}")
                os_.append(o_g)
                ls_.append(l_g)
            ob, lse = _dil_combine(os_, ls_, name="dil_combine")
            ymix = jnp.concatenate([oa, ob.astype(BF16)], axis=-1)
            y1 = _mm(ymix, W["ab_w_out", 0], out_dtype=BF16, name="ab_out")
            s.update(z=z, lb=lb, lb_vjp=lb_vjp, onw=onw, o_raw=o_raw, s_start=s_start, qr=qr, kr=kr, ob=ob, lse=lse, ymix=ymix)
        else:
            p5 = tuple(W[n][0] for n in ("s5_lambda_re", "s5_lambda_im", "s5_log_dt", "s5_b_re", "s5_b_im", "s5_c_re", "s5_c_im"))
            (kern, ez, fz, _, _), s5_vjp = jax.vjp(_s5_build, *p5)
            tab_r, tab_i = _s5_scan_tables(p5[0], p5[1], p5[2], int(math.log2(L // C_TC)))
            kbd, lag_vjp = jax.vjp(_s5_lag_blocks, kern)
            mats = tuple(_s5_dense(kbd, ez, fz, name="s5_maps")) + (tab_r, tab_i)
            dsk = W["s5_d"].reshape(1, D_MODEL)
            ffn1 = [("ffn_w_in", 1)] if shards is not None else []
            gl, ypre, x8, xs, gathered = _s5_fwd(h1, dsk, *mats, B, L, name="s5_fwd", gather=[shards[k] for k in ffn1])
            for key, g in zip(ffn1, gathered):
                W[key] = g.reshape(-1, D_MODEL)
            w_glu = _interleave_rows(W["s5_w_glu", 0])
            zg, y1 = _mm(gl, w_glu, tb=True, name="s5_glu_in", gate=("fwd", "glu"))
            s.update(s5_vjp=s5_vjp, lag_vjp=lag_vjp, mats=mats, x8=x8, xs=xs, dsk=dsk, gl=gl, ypre=ypre, zg=zg, w_glu=w_glu)
        x1, h2 = _rms_fwd(y1, vec(nw[layer, 1]), xin, name=f"norm_post_mix_{tag}", then=(vec(nw[layer, 2]), BF16))
        memn = _rms_fwd(memf, vec(W["mem_norm_w"][layer]), name=f"norm_mem_{tag}", out_dtype=BF16)
        q = _mm(h2, W["xattn_wq", layer], out_dtype=BF16, name=f"x_q_{tag}")
        kv = _mm(memn, W["xattn_wkv", layer], tb=True, out_dtype=BF16, name=f"x_kv_{tag}")
        o = _xattn_fwd(q, kv, B, L, name=f"x_attn_{tag}")
        y2 = _mm(o, W["xattn_wo", layer], out_dtype=BF16, name=f"x_o_{tag}")
        x2, h3 = _rms_fwd(y2, vec(nw[layer, 3]), x1, name=f"norm_post_x_{tag}", then=(vec(nw[layer, 4]), BF16))
        w_ffn_in = _interleave_rows(W["ffn_w_in", layer])
        zf, u = mmx(h3, w_ffn_in, tb=True, name=f"ffn_in_{tag}", gate=("fwd", "swiglu"),
                    gather=[("s5_w_glu", 0), ("xattn_wq", 1), ("xattn_wkv", 1), ("xattn_wo", 1)] if layer == 0 else [])
        y3 = mmx(u, W["ffn_w_out", layer], out_dtype=BF16, name=f"ffn_out_{tag}", gather=[("ffn_w_out", 1)] if layer == 0 else [])
        if layer == 0:
            x3, h1 = _rms_fwd(y3, vec(nw[0, 5]), x2, name="norm_post_ffn_l0", then=(vec(nw[1, 0]), F32))
        else:
            x3 = _rms_fwd(y3, vec(nw[layer, 5]), x2, name=f"norm_post_ffn_{tag}")
        s.update(y1=y1, x1=x1, h2=h2, memn=memn, q=q, kv=kv, o=o, y2=y2, x2=x2, h3=h3, zf=zf, u=u, y3=y3, w_ffn_in=w_ffn_in)
        saved.append(s)
        xin = x3

    loss_parts, dx = _loss_head(xin, target.reshape(T, D_MODEL), name="loss_head")

    d_norm = [[None] * 6 for _ in range(2)]
    d_memn = [None, None]
    for layer in (1, 0):
        s = saved[layer]
        tag = f"l{layer}"
        if layer == 1:
            dy3, d_norm[1][5] = _rms_bwd(s["y3"], vec(nw[1, 5]), dx, name="bnorm_post_ffn_l1", out_dtype=BF16)
        dzf = mmx(dy3, W["ffn_w_out", layer], tb=True, name=f"b_ffn_out_dx_{tag}", gate=("bwd", "swiglu", s["zf"]),
                  scatter=[("xattn_wkv", 1), ("s5_w_glu", 0)] if layer == 0 else [])
        G["ffn_w_out", layer] = _mm(s["u"], dy3, ta=True, out_dtype=BF16, name=f"b_ffn_out_dw_{tag}")
        early = []
        if shards is not None and layer == 0:
            shards = {**shards, "small_early": _pack_small([G[n] for n, _ in _SMALL_EARLY])}
            early = ["small_early"]
        G["ffn_w_in", layer] = _interleave_rows(
            mmx(dzf, s["h3"], ta=True, out_dtype=BF16, name=f"b_ffn_in_dw_{tag}", gather=early,
                scatter=[("ffn_w_out", layer)]), inverse=True)
        if early:
            received["small_early"] = W["small_early"].reshape(N_DEV, -1, LANES)
        dh3 = mmx(dzf, s["w_ffn_in"], out_dtype=BF16, name=f"b_ffn_in_dx_{tag}", scatter=[("ffn_w_in", layer)])
        dx, d_norm[layer][4], dy2, d_norm[layer][3] = _rms_bwd(
            s["x2"], vec(nw[layer, 4]), dh3, dx, name=f"bnorm_pre_ffn_{tag}", then=(s["y2"], vec(nw[layer, 3])))
        do = _mm(dy2, W["xattn_wo", layer], tb=True, out_dtype=BF16, name=f"b_x_o_dx_{tag}")
        G["xattn_wo", layer] = _mm(s["o"], dy2, ta=True, out_dtype=BF16, name=f"b_x_o_dw_{tag}")
        dq, dkv = _xattn_bwd(s["q"], s["kv"], do, B, L, name=f"b_x_attn_{tag}")
        G["xattn_wq", layer] = _mm(s["h2"], dq, ta=True, out_dtype=BF16, name=f"b_x_q_dw_{tag}")
        dh2 = _mm(dq, W["xattn_wq", layer], tb=True, out_dtype=BF16, name=f"b_x_q_dx_{tag}")
        G["xattn_wkv", layer] = _mm(dkv, s["memn"], ta=True, out_dtype=BF16, name=f"b_x_kv_dw_{tag}")
        dmemn = _mm(dkv, W["xattn_wkv", layer], out_dtype=BF16, name=f"b_x_kv_dx_{tag}")
        _, d_memn[layer] = _rms_bwd(memf, vec(W["mem_norm_w"][layer]), dmemn, name=f"bnorm_mem_{tag}", out_dtype=BF16)
        dx, d_norm[layer][2], dy1, d_norm[layer][1] = _rms_bwd(
            s["x1"], vec(nw[layer, 2]), dh2, dx, name=f"bnorm_pre_x_{tag}", then=(s["y1"], vec(nw[layer, 1])))
        if layer == 0:
            z = s["z"]
            dymix = _mm(dy1, W["ab_w_out", 0], tb=True, name="b_ab_out_dx")
            G["ab_w_out", 0] = _mm(s["ymix"], dy1, ta=True, out_dtype=BF16, name="b_ab_out_dw")
            early = [("xattn_wo", 0), ("xattn_wq", 0), ("xattn_wkv", 0), ("ab_w_out", 0)] if shards is not None else []
            dqa, dfa, dia, dga, d_onw, d_lb, got = _hgrn_bwd(
                z, s["lb"], s["onw"], s["o_raw"], s["s_start"], dymix, B, L, name="hgrn_bwd",
                scatter=[G[k].reshape(N_DEV, -1, D_MODEL) for k in early])
            received.update(zip(early, got))
            dqs, dks, dvs = [], [], []
            for dil in B_DILS:
                dqs.append(_dil_bwd_q(s["qr"], s["kr"], z, dymix, s["ob"], s["lse"], dil, B, L, name=f"dil_bwd_q_{dil}"))
                dk_g, dv_g = _dil_bwd_kv(s["qr"], s["kr"], z, dymix, s["ob"], s["lse"], dil, B, L, name=f"dil_bwd_kv_{dil}")
                dks.append(dk_g)
                dvs.append(dv_g)
            dqkv = _rope_bwd(dqs, dks, dvs, cos2, sin2, B, L, name="b_rope")
            dz = jnp.concatenate([dqa, dfa, dia, dga, dqkv], axis=-1)
            G["ab_w_in", 0] = _mm(dz, s["h1"], ta=True, out_dtype=BF16, name="b_ab_in_dw")
            dh1 = mmx(dz, W["ab_w_in", 0], out_dtype=BF16, name="b_ab_in_dx", scatter=[("ab_w_in", 0)])
            G["hgrn_out_norm_w"] = jnp.sum(d_onw.reshape(B, A_WIDTH), axis=0, keepdims=True)
            d_lb_row = jnp.sum(d_lb.reshape(B, A_WIDTH), axis=0, keepdims=True)
            G["hgrn_lb_logits"] = s["lb_vjp"](d_lb_row)[0]
        else:
            dzg = _gated_bwd(s["zg"], dy1, "glu", name="b_s5_glu")
            G["s5_w_glu", 0] = _interleave_rows(
                _mm(dzg, s["gl"], ta=True, out_dtype=BF16, name="b_s5_glu_dw"), inverse=True)
            dgl = mmx(dzg, s["w_glu"], name="b_s5_glu_dx", scatter=[("xattn_wo", 1), ("xattn_wq", 1)])
            dh1, dy8, de_in, da, d_dsk = _s5_bwd(dgl, s["ypre"], s["h1"], s["dsk"], s["xs"], *s["mats"], B, L, name="s5_bwd")
            dkbd = _bmm_tn(s["x8"], dy8, C_NB, "lags", name="s5_bwd_dm").transpose(1, 0, 2, 3)
            dfz = _bmm_tn(s["xs"], dy8, C_NB, "f", name="s5_bwd_df").reshape(C_NB, 2, C_GB, C_STATE, C_TC, C_GROUP)
            dez = _bmm_tn(s["x8"], de_in, C_NB, "e", name="s5_bwd_de").reshape(C_NB, C_TC, C_GB, C_GROUP, 2, C_STATE)
            half = C_S8 // 2
            da_r = (da[:, 0, :half] + da[:, 0, half:]).reshape(C_GROUPS, C_STATE)
            da_i = (da[:, 1, half:] - da[:, 1, :half]).reshape(C_GROUPS, C_STATE)
            gp = s["s5_vjp"](s["lag_vjp"](dkbd) + (dez, dfz, da_r, da_i))
            for n, gv in zip(("s5_lambda_re", "s5_lambda_im", "s5_log_dt", "s5_b_re", "s5_b_im", "s5_c_re", "s5_c_im"), gp):
                G[n] = gv[None]
            G["s5_d"] = d_dsk
        if layer == 1:
            dx, d_norm[1][0], dy3, d_norm[0][5] = _rms_bwd(
                s["x0"], vec(nw[1, 0]), dh1, dx, name="bnorm_pre_mix_l1", then=(saved[0]["y3"], vec(nw[0, 5])))
        else:
            dx, d_norm[0][0] = _rms_bwd(s["x0"], vec(nw[0, 0]), dh1, dx, name="bnorm_pre_mix_l0")

    G["norm_w"] = jnp.stack([jnp.concatenate(d_norm[l], axis=0) for l in range(2)])
    G["mem_norm_w"] = jnp.concatenate(d_memn, axis=0)
    if shards is not None:
        G.update(received)
    return loss_parts, dx.reshape(B, L, D_MODEL), G


_SMALL_LATE = (("norm_w", (2, 6, 1024)), ("mem_norm_w", (2, 1024)), ("hgrn_lb_logits", (3, 512)), ("hgrn_out_norm_w", (1, 512)))
_SMALL_EARLY = (("s5_lambda_re", (1, 64, 64)), ("s5_lambda_im", (1, 64, 64)), ("s5_log_dt", (1, 64)),
                ("s5_b_re", (1, 64, 64, 16)), ("s5_b_im", (1, 64, 64, 16)), ("s5_c_re", (1, 64, 16, 64)),
                ("s5_c_im", (1, 64, 16, 64)), ("s5_d", (1, 1024)))
_SMALL = _SMALL_LATE + _SMALL_EARLY


def _pack_small(values):
    flat = jnp.concatenate([v.reshape(-1) for v in values])
    return jnp.pad(flat, (0, (-flat.shape[0]) % (8 * LANES))).reshape(-1, LANES)


def _unpack_small(flat, spec):
    out, off = {}, 0
    for n, shp in spec:
        size = int(np.prod(shp))
        out[n] = flat[off:off + size].reshape(shp)
        off += size
    return out, off

_WEIGHT_ORDER = ('norm_w', 'mem_norm_w', 'ab_w_in', 'ab_w_out', 'hgrn_lb_logits', 'hgrn_out_norm_w', 's5_lambda_re',
                 's5_lambda_im', 's5_log_dt', 's5_b_re', 's5_b_im', 's5_c_re', 's5_c_im', 's5_d', 's5_w_glu', 'xattn_wq',
                 'xattn_wkv', 'xattn_wo', 'ffn_w_in', 'ffn_w_out')


def kernel(x, mem, norm_w, mem_norm_w, ab_w_in, ab_w_out, hgrn_lb_logits, hgrn_out_norm_w, s5_lambda_re, s5_lambda_im, s5_log_dt, s5_b_re, s5_b_im, s5_c_re, s5_c_im, s5_d, s5_w_glu, xattn_wq, xattn_wkv, xattn_wo, ffn_w_in, ffn_w_out, loss_target, m_norm_w, m_mem_norm_w, m_ab_w_in, m_ab_w_out, m_hgrn_lb_logits, m_hgrn_out_norm_w, m_s5_lambda_re, m_s5_lambda_im, m_s5_log_dt, m_s5_b_re, m_s5_b_im, m_s5_c_re, m_s5_c_im, m_s5_d, m_s5_w_glu, m_xattn_wq, m_xattn_wkv, m_xattn_wo, m_ffn_w_in, m_ffn_w_out, v_norm_w, v_mem_norm_w, v_ab_w_in, v_ab_w_out, v_hgrn_lb_logits, v_hgrn_out_norm_w, v_s5_lambda_re, v_s5_lambda_im, v_s5_log_dt, v_s5_b_re, v_s5_b_im, v_s5_c_re, v_s5_c_im, v_s5_d, v_s5_w_glu, v_xattn_wq, v_xattn_wkv, v_xattn_wo, v_ffn_w_in, v_ffn_w_out):
    local = dict(norm_w=norm_w, mem_norm_w=mem_norm_w, ab_w_in=ab_w_in, ab_w_out=ab_w_out, hgrn_lb_logits=hgrn_lb_logits,
                 hgrn_out_norm_w=hgrn_out_norm_w, s5_lambda_re=s5_lambda_re, s5_lambda_im=s5_lambda_im, s5_log_dt=s5_log_dt,
                 s5_b_re=s5_b_re, s5_b_im=s5_b_im, s5_c_re=s5_c_re, s5_c_im=s5_c_im, s5_d=s5_d, s5_w_glu=s5_w_glu,
                 xattn_wq=xattn_wq, xattn_wkv=xattn_wkv, xattn_wo=xattn_wo, ffn_w_in=ffn_w_in, ffn_w_out=ffn_w_out)
    mom_m = dict(zip(_WEIGHT_ORDER, (m_norm_w, m_mem_norm_w, m_ab_w_in, m_ab_w_out, m_hgrn_lb_logits, m_hgrn_out_norm_w, m_s5_lambda_re, m_s5_lambda_im, m_s5_log_dt, m_s5_b_re, m_s5_b_im, m_s5_c_re, m_s5_c_im, m_s5_d, m_s5_w_glu, m_xattn_wq, m_xattn_wkv, m_xattn_wo, m_ffn_w_in, m_ffn_w_out)))
    mom_v = dict(zip(_WEIGHT_ORDER, (v_norm_w, v_mem_norm_w, v_ab_w_in, v_ab_w_out, v_hgrn_lb_logits, v_hgrn_out_norm_w, v_s5_lambda_re, v_s5_lambda_im, v_s5_log_dt, v_s5_b_re, v_s5_b_im, v_s5_c_re, v_s5_c_im, v_s5_d, v_s5_w_glu, v_xattn_wq, v_xattn_wkv, v_xattn_wo, v_ffn_w_in, v_ffn_w_out)))
    dev = 4 * lax.axis_index("x") + 2 * lax.axis_index("y") + lax.axis_index("c")

    shards = {(n, l): _owner_major(n, local[n][l]).astype(BF16) for n, l in _LARGE_KEYS}
    first = ("ab_w_in", 0)
    W = {first: _gather_weights([shards[first]], name="gather_first")[0].reshape(-1, D_MODEL)}
    tiny = jnp.concatenate([norm_w.reshape(-1), s5_d.reshape(-1)])
    tiny = jnp.pad(tiny, (0, 16 * LANES - tiny.shape[0])).reshape(16, LANES)
    tiny_all = _all_gather(tiny, name="gather_tiny").reshape(N_DEV, 16 * LANES)
    W["norm_w"] = tiny_all[:, :12 * LANES].reshape(N_DEV, 2, 6, LANES).transpose(1, 2, 0, 3).reshape(2, 6, D_MODEL)
    W["s5_d"] = tiny_all[:, 12 * LANES:13 * LANES].reshape(1, D_MODEL)
    for n in ("mem_norm_w", "hgrn_lb_logits", "hgrn_out_norm_w", "s5_lambda_re", "s5_lambda_im", "s5_log_dt",
              "s5_b_re", "s5_b_im", "s5_c_re", "s5_c_im"):
        W[n] = local[n]

    loss_parts, grad_x, G = _local_step(x, mem, loss_target, W, shards)

    g_layers = {}
    for n, l in _LARGE_KEYS:
        g = _sum_rows(G[n, l], name=f"sum_grads_{n}_{l}")
        g_layers.setdefault(n, []).append(g.T if _TRANSPOSED[n] else g)
    g_local = {n: jnp.stack(gl) for n, gl in g_layers.items()}
    small = _pack_small([G[n] for n, _ in _SMALL_LATE] + [0.5 / D_MODEL * jnp.sum(loss_parts)])
    small_sum = _sum_rows(_all_gather(small, name="gather_small"), name="sum_small").reshape(-1)
    g_full, off = _unpack_small(small_sum, _SMALL_LATE)
    loss = small_sum[off]
    early_sum = _sum_rows(G["small_early"], name="sum_small_early").reshape(-1)
    g_full.update(_unpack_small(early_sum, _SMALL_EARLY)[0])
    grads = dict(g_local)
    for n, shp in _SMALL:
        if n == "norm_w":
            grads[n] = lax.dynamic_slice_in_dim(g_full[n], dev * LANES, LANES, axis=2)
        elif n == "s5_d":
            grads[n] = lax.dynamic_slice_in_dim(g_full[n], dev * LANES, LANES, axis=1)
        else:
            grads[n] = g_full[n]

    delta, new_m, new_v = {}, {}, {}
    for n in _WEIGHT_ORDER:
        delta[n], new_m[n], new_v[n] = _adamw(local[n], grads[n], mom_m[n], mom_v[n], name=f"adamw_{n}")
    return (loss, grad_x, *[grads[n] for n in _WEIGHT_ORDER], *[delta[n] for n in _WEIGHT_ORDER],
            *[new_m[n] for n in _WEIGHT_ORDER], *[new_v[n] for n in _WEIGHT_ORDER])
```

```python
import functools
import math

import numpy as np
import jax
import jax.numpy as jnp
from jax import lax
from jax.experimental import pallas as pl
from jax.experimental.pallas import tpu as pltpu

F32 = jnp.float32
BF16 = jnp.bfloat16
HI = lax.Precision.HIGHEST

D_MODEL = 1024
NORM_EPS = 1e-6
A_WIDTH = 512
A_HEAD = 128
A_CHUNK = 32
A_SUPER = 256
B_SPAN = 128
B_DILS = (1, 4, 16)
ROPE_THETA = 10000.0
C_GROUPS = 64
C_GROUP = 16
C_STATE = 64
C_TC = 8
C_MIN_NEG_RE = -1e-4
MEM_LEN = 256
X_HEADS = 4
X_HD = 256
D_FF = 2816
N_DEV = 8
LANES = 128

ADAM_LR, ADAM_B1, ADAM_B2, ADAM_EPS, ADAM_WD, ADAM_STEP = 0.001, 0.9, 0.999, 1e-08, 0.01, 10

NEG_BIG = -1e30


def _tile(n, pref):
    for d in range(min(pref, n) // LANES * LANES, 0, -LANES):
        if n % d == 0:
            return d
    return n


def _cparams(*sem):
    return pltpu.CompilerParams(dimension_semantics=sem, vmem_limit_bytes=56 * 1024 * 1024)


def _sigmoid(x):
    return 0.5 * jnp.tanh(0.5 * x) + 0.5


def _erf(x):
    ax = jnp.abs(x)
    t = 1.0 / (1.0 + 0.3275911 * ax)
    poly = t * (0.254829592 + t * (-0.284496736 + t * (1.421413741 + t * (-1.453152027 + t * 1.061405429))))
    y = 1.0 - poly * jnp.exp(-ax * ax)
    return jnp.where(x < 0, -y, y)


_HBM = pl.BlockSpec(memory_space=pltpu.HBM)
_MESH = pl.DeviceIdType.MESH


def _logical(px, py, pc):
    return 4 * px + 2 * py + pc


class _Exchange:
    def __init__(self, gather=(), scatter=()):
        self.gather, self.scatter = list(gather), list(scatter)
        self.ng, self.n = len(self.gather), len(self.gather) + len(self.scatter)

    def operands(self):
        return self.gather + self.scatter

    def in_specs(self):
        return [_HBM] * self.n

    def out_shapes(self):
        return ([jax.ShapeDtypeStruct((N_DEV,) + g.shape, g.dtype) for g in self.gather]
                + [jax.ShapeDtypeStruct(s.shape, s.dtype) for s in self.scatter])

    def scratch(self):
        if not self.n:
            return []
        return [pltpu.SemaphoreType.DMA((self.n, 7)), pltpu.SemaphoreType.DMA((self.n, 7)), pltpu.SemaphoreType.DMA((self.n,))]

    def split(self, results):
        return list(results[:self.ng]), list(results[self.ng:])

    def run(self, ins, outs, sems, first, last):
        if not self.n:
            return
        send_sems, recv_sems, local_sems = sems
        x, y, c = lax.axis_index("x"), lax.axis_index("y"), lax.axis_index("c")
        me, sibling = _logical(x, y, c), (x, y, 1 - c)
        chips = [(1 - x, y), (x, 1 - y), (1 - x, 1 - y)]
        peers = [(x ^ (k >> 2), y ^ ((k >> 1) & 1), c ^ (k & 1)) for k in range(1, N_DEV)]

        def remote(t, k, src, dst, to):
            return pltpu.make_async_remote_copy(src_ref=src, dst_ref=dst, send_sem=send_sems.at[t, k],
                                                recv_sem=recv_sems.at[t, k], device_id=to, device_id_type=_MESH)

        def local(t):
            src = ins[t] if t < self.ng else ins[t].at[me]
            return pltpu.make_async_copy(src, outs[t].at[me], local_sems.at[t])

        @pl.when(first)
        def _():
            for t in range(self.n):
                local(t).start()
                if t < self.ng:
                    remote(t, 0, ins[t], outs[t].at[me], sibling).start()
                    for j, chip in enumerate(chips):
                        remote(t, 1 + j, ins[t], outs[t].at[me], (*chip, c)).start()
                else:
                    for k, peer in enumerate(peers):
                        remote(t, k, ins[t].at[_logical(*peer)], outs[t].at[me], peer).start()

        @pl.when(last)
        def _():
            for j, chip in enumerate(chips):
                for t in range(self.ng):
                    landed = outs[t].at[_logical(*chip, c)]
                    remote(t, 1 + j, ins[t], landed, sibling).wait_recv()
                    remote(t, 4 + j, landed, landed, sibling).start()
            for t in range(self.n):
                if t < self.ng:
                    remote(t, 0, ins[t], outs[t].at[_logical(*sibling)], sibling).wait_recv()
                    for j, chip in enumerate(chips):
                        remote(t, 4 + j, ins[t], outs[t].at[_logical(*chip, 1 - c)], sibling).wait_recv()
                    for k in range(7):
                        remote(t, k, ins[t], outs[t].at[me], sibling).wait_send()
                else:
                    for k, peer in enumerate(peers):
                        remote(t, k, ins[t].at[me], outs[t].at[_logical(*peer)], peer).wait_recv()
                    for k, peer in enumerate(peers):
                        remote(t, k, ins[t].at[_logical(*peer)], outs[t].at[me], peer).wait_send()
                local(t).wait()


def _carrying(body, n_in, n_out, ex, grid):
    n_sems = len(ex.scratch())

    def wrapped(*refs):
        ins, ex_in = refs[:n_in], refs[n_in:n_in + ex.n]
        outs = refs[n_in + ex.n:n_in + ex.n + n_out]
        ex_out = refs[n_in + ex.n + n_out:n_in + 2 * ex.n + n_out]
        rest = refs[n_in + 2 * ex.n + n_out:]
        scratch, sems = rest[:len(rest) - n_sems], rest[len(rest) - n_sems:]
        ids = [pl.program_id(a) for a in range(len(grid))]
        first = functools.reduce(lambda u, v: u & v, [i == 0 for i in ids])
        last = functools.reduce(lambda u, v: u & v, [i == g - 1 for i, g in zip(ids, grid)])
        ex.run(ex_in, ex_out, sems, first, last)
        body(*ins, *outs, *scratch)

    return wrapped


_MM_VMEM_BUDGET = 36 * 1024 * 1024


def _mm(a, b, *, ta=False, tb=False, out_dtype=F32, name, tiles=(1408, 1408, 4096), gather=(), scatter=(), gate=None):
    M, K = (a.shape[1], a.shape[0]) if ta else a.shape
    N = b.shape[0] if tb else b.shape[1]
    assert (b.shape[1] if tb else b.shape[0]) == K
    gate_mode = gate[0] if gate else None
    tm, tn, tk = _tile(M, tiles[0]), _tile(N, tiles[1]), _tile(K, tiles[2])
    if gate_mode == "fwd":
        bs = _gate_block(N // 2)
        tn = 2 * bs
    elif gate_mode == "bwd":
        bs = _gate_block(N)
        tn = bs

    def vmem_bytes():
        acc = 4 * tm * tn if tk < K else 0
        if gate_mode == "fwd":
            io = (2 * (2 + 1) + 4) * tm * tn
        elif gate_mode == "bwd":
            io = (2 * (4 + 4) + 4) * tm * tn
        else:
            io = 2 * tm * tn * jnp.dtype(out_dtype).itemsize
        return 2 * 2 * (tm * tk + tk * tn) + acc + io

    while vmem_bytes() > _MM_VMEM_BUDGET:
        if gate_mode and tm > 256:
            tm = _tile(M, tm - LANES)
        elif tk > 512:
            tk = _tile(K, tk - LANES)
        elif tn > 256 and not gate_mode:
            tn = _tile(N, tn - LANES)
        else:
            tm = _tile(M, tm - LANES)
    ni, nj, nk = M // tm, N // tn, K // tk
    ex = _Exchange(gather, scatter)

    a_spec = pl.BlockSpec((tk, tm), lambda i, j, k: (k, i)) if ta else pl.BlockSpec((tm, tk), lambda i, j, k: (i, k))
    b_spec = pl.BlockSpec((tn, tk), lambda i, j, k: (j, k)) if tb else pl.BlockSpec((tk, tn), lambda i, j, k: (k, j))
    dims = (((0 if ta else 1,), (1 if tb else 0,)), ((), ()))
    n_acc = 1 if nk > 1 else 0
    n_in = 3 if gate_mode == "bwd" else 2
    n_out = 2 if gate_mode == "fwd" else 1

    def finish(val, in_refs, out_refs, rows=slice(None)):
        if gate_mode is None:
            out_refs[0][rows, :] = val.astype(out_refs[0].dtype)
        elif gate_mode == "fwd":
            out_refs[0][rows, :] = val.astype(BF16)
            for p in range(tn // (2 * bs)):
                a_, b_ = val[:, 2 * p * bs:(2 * p + 1) * bs], val[:, (2 * p + 1) * bs:(2 * p + 2) * bs]
                out_refs[1][rows, p * bs:(p + 1) * bs] = _gate_value(a_, b_, gate[1]).astype(BF16)
        else:
            z_ref = in_refs[2]
            for p in range(tn // bs):
                a_ = z_ref[rows, 2 * p * bs:(2 * p + 1) * bs].astype(F32)
                b_ = z_ref[rows, (2 * p + 1) * bs:(2 * p + 2) * bs].astype(F32)
                da, db = _gate_grads(a_, b_, val[:, p * bs:(p + 1) * bs], gate[1])
                out_refs[0][rows, 2 * p * bs:(2 * p + 1) * bs] = da.astype(BF16)
                out_refs[0][rows, (2 * p + 1) * bs:(2 * p + 2) * bs] = db.astype(BF16)

    halves = 2 if (gate_mode and nk == 1 and not ta and tm % 32 == 0) else 1

    def body(*refs):
        in_refs, rest = refs[:n_in], refs[n_in:]
        ex_in, out_refs = rest[:ex.n], rest[ex.n:ex.n + n_out]
        ex_out, scratch = rest[ex.n + n_out:2 * ex.n + n_out], rest[2 * ex.n + n_out:]
        i, j, k = pl.program_id(0), pl.program_id(1), pl.program_id(2)
        ex.run(ex_in, ex_out, scratch[n_acc:], (i == 0) & (j == 0) & (k == 0), (i == ni - 1) & (j == nj - 1) & (k == nk - 1))
        if halves > 1:
            rh = tm // halves
            rhs = in_refs[1][...].astype(BF16)
            parts = [lax.dot_general(in_refs[0][r * rh:(r + 1) * rh, :].astype(BF16), rhs, dims, preferred_element_type=F32)
                     for r in range(halves)]
            for r, p in enumerate(parts):
                finish(p, in_refs, out_refs, slice(r * rh, (r + 1) * rh))
            return
        part = lax.dot_general(in_refs[0][...].astype(BF16), in_refs[1][...].astype(BF16), dims, preferred_element_type=F32)
        if nk == 1:
            finish(part, in_refs, out_refs)
            return
        acc_ref = scratch[0]

        @pl.when(k == 0)
        def _():
            acc_ref[...] = part

        @pl.when(k > 0)
        def _():
            acc_ref[...] += part

        @pl.when(k == nk - 1)
        def _():
            finish(acc_ref[...], in_refs, out_refs)

    tile = lambda width: pl.BlockSpec((tm, width), lambda i, j, k: (i, j))
    if gate_mode == "fwd":
        out_specs, out_shape = [tile(tn), tile(tn // 2)], [jax.ShapeDtypeStruct((M, N), BF16), jax.ShapeDtypeStruct((M, N // 2), BF16)]
    elif gate_mode == "bwd":
        out_specs, out_shape = [tile(2 * tn)], [jax.ShapeDtypeStruct((M, 2 * N), BF16)]
    else:
        out_specs, out_shape = [tile(tn)], [jax.ShapeDtypeStruct((M, N), out_dtype)]
    operands = [a, b] + ([gate[2]] if gate_mode == "bwd" else [])
    sem = ("arbitrary",) * 3 if ex.n else ("parallel", "parallel", "arbitrary")
    res = pl.pallas_call(
        body, name=name, grid=(ni, nj, nk),
        in_specs=[a_spec, b_spec] + ([tile(2 * tn)] if gate_mode == "bwd" else []) + ex.in_specs(),
        out_specs=out_specs + ex.in_specs(),
        out_shape=out_shape + ex.out_shapes(),
        scratch_shapes=([pltpu.VMEM((tm, tn), F32)] if nk > 1 else []) + ex.scratch(),
        compiler_params=_cparams(*sem),
    )(*operands, *ex.operands())
    main = res[0] if n_out == 1 else tuple(res[:n_out])
    if not ex.n:
        return main
    return (main,) + tuple(ex.split(res[n_out:]))


def _rms_fwd(x, w, res=None, *, name, out_dtype=F32, then=None):
    T, C = x.shape
    tm = _tile(T, 1024)
    has_res = res is not None

    def norm(v, w_ref):
        return v * lax.rsqrt(jnp.mean(v * v, axis=-1, keepdims=True) + NORM_EPS) * w_ref[...]

    def body(*refs):
        x_ref, w_ref = refs[0], refs[1]
        y = norm(x_ref[...].astype(F32), w_ref)
        if has_res:
            y = y + refs[2][...]
        if then is None:
            refs[-1][...] = y.astype(refs[-1].dtype)
        else:
            refs[-2][...] = y.astype(refs[-2].dtype)
            refs[-1][...] = norm(y, refs[-3]).astype(refs[-1].dtype)

    row = pl.BlockSpec((tm, C), lambda i: (i, 0))
    vec = pl.BlockSpec((1, C), lambda i: (0, 0))
    ins = [x, w] + ([res] if has_res else []) + ([then[0]] if then else [])
    in_specs = [row, vec] + ([row] if has_res else []) + ([vec] if then else [])
    out_shape = [jax.ShapeDtypeStruct((T, C), out_dtype)] + ([jax.ShapeDtypeStruct((T, C), then[1])] if then else [])
    res_ = pl.pallas_call(
        body, name=name, grid=(T // tm,), in_specs=in_specs, out_specs=[row] * len(out_shape),
        out_shape=out_shape, compiler_params=_cparams("parallel"),
    )(*ins)
    return tuple(res_) if then else res_[0]


def _rms_bwd(x, w, dy, add=None, *, name, out_dtype=F32, then=None):
    T, C = x.shape
    tm = _tile(T, 1024)
    has_add = add is not None
    n_in = 3 + has_add + (2 if then else 0)

    def grads(x_ref, w_ref, g, dw_ref):
        xv = x_ref[...].astype(F32)
        r = lax.rsqrt(jnp.mean(xv * xv, axis=-1, keepdims=True) + NORM_EPS)
        xh = xv * r
        part = jnp.sum(g * xh, axis=0, keepdims=True)

        @pl.when(pl.program_id(0) == 0)
        def _():
            dw_ref[...] = part

        @pl.when(pl.program_id(0) > 0)
        def _():
            dw_ref[...] += part

        gx = g * w_ref[...]
        return r * (gx - xh * jnp.mean(gx * xh, axis=-1, keepdims=True))

    def body(*refs):
        ins, outs = refs[:n_in], refs[n_in:]
        dx = grads(ins[0], ins[1], ins[2][...].astype(F32), outs[1])
        if has_add:
            dx = dx + ins[3][...]
        outs[0][...] = dx.astype(outs[0].dtype)
        if then:
            outs[2][...] = grads(ins[-2], ins[-1], dx, outs[3]).astype(BF16)

    row = pl.BlockSpec((tm, C), lambda i: (i, 0))
    vec = pl.BlockSpec((1, C), lambda i: (0, 0))
    ins = [x, w, dy] + ([add] if has_add else []) + (list(then) if then else [])
    big, small = jax.ShapeDtypeStruct((T, C), out_dtype), jax.ShapeDtypeStruct((1, C), F32)
    return pl.pallas_call(
        body, name=name, grid=(T // tm,),
        in_specs=[row, vec, row] + ([row] if has_add else []) + ([row, vec] if then else []),
        out_specs=[row, vec] + ([row, vec] if then else []),
        out_shape=[big, small] + ([jax.ShapeDtypeStruct((T, C), BF16), small] if then else []),
        compiler_params=_cparams("arbitrary"),
    )(*ins)


def _gate_block(width):
    return _tile(width, 1408)


def _gate_value(a, b, kind):
    return a * _sigmoid(a) * b if kind == "swiglu" else a * _sigmoid(b)


def _gate_grads(a, b, d, kind):
    if kind == "swiglu":
        s = _sigmoid(a)
        return d * b * (s * (1.0 + a * (1.0 - s))), d * a * s
    s = _sigmoid(b)
    return d * s, d * a * s * (1.0 - s)


def _interleave_rows(w, inverse=False):
    W2, C = w.shape
    bs = _gate_block(W2 // 2)
    nb = W2 // 2 // bs
    shape = (nb, 2, bs, C) if inverse else (2, nb, bs, C)
    return w.reshape(shape).transpose(1, 0, 2, 3).reshape(W2, C)


def _gated_bwd(z, dout, kind, *, name):
    T, W2 = z.shape
    W = W2 // 2
    tm, bs = _tile(T, 512), _gate_block(W)

    def body(z_ref, d_ref, o_ref):
        da, db = _gate_grads(z_ref[:, :bs].astype(F32), z_ref[:, bs:].astype(F32), d_ref[...].astype(F32), kind)
        o_ref[:, :bs] = da.astype(o_ref.dtype)
        o_ref[:, bs:] = db.astype(o_ref.dtype)

    return pl.pallas_call(
        body, name=name, grid=(T // tm, W // bs),
        in_specs=[pl.BlockSpec((tm, 2 * bs), lambda i, j: (i, j)), pl.BlockSpec((tm, bs), lambda i, j: (i, j))],
        out_specs=pl.BlockSpec((tm, 2 * bs), lambda i, j: (i, j)),
        out_shape=jax.ShapeDtypeStruct((T, W2), BF16), compiler_params=_cparams("parallel", "parallel"),
    )(z, dout)


_NT = (((1,), (1,)), ((), ()))
_TN = (((0,), (0,)), ((), ()))


def _dot(a, b, dims=None, precision=None):
    if dims is None:
        return jnp.dot(a, b, preferred_element_type=F32, precision=precision)
    return lax.dot_general(a, b, dims, preferred_element_type=F32, precision=precision)


def _xattn_fwd(q, kv, B, L, *, name):
    T = q.shape[0]
    tq = 512
    nq = L // tq
    scale = X_HD ** -0.5

    def body(q_ref, k_ref, v_ref, o_ref):
        heads = [slice(h * X_HD, (h + 1) * X_HD) for h in range(X_HEADS)]
        s = [_dot(q_ref[:, sl].astype(BF16), k_ref[:, sl].astype(BF16), _NT) * scale for sl in heads]
        m = [jnp.max(a, axis=-1, keepdims=True) for a in s]
        p = [jnp.exp(a - b) for a, b in zip(s, m)]
        l = [jnp.sum(a, axis=-1, keepdims=True) for a in p]
        o = [_dot(a.astype(BF16), v_ref[:, sl].astype(BF16)) for a, sl in zip(p, heads)]
        for sl, a, b in zip(heads, o, l):
            o_ref[:, sl] = (a / b).astype(BF16)

    return pl.pallas_call(
        body, name=name, grid=(B, nq),
        in_specs=[pl.BlockSpec((tq, D_MODEL), lambda b, i: (b * nq + i, 0)),
                  pl.BlockSpec((MEM_LEN, D_MODEL), lambda b, i: (b, 0)),
                  pl.BlockSpec((MEM_LEN, D_MODEL), lambda b, i: (b, 1))],
        out_specs=pl.BlockSpec((tq, D_MODEL), lambda b, i: (b * nq + i, 0)),
        out_shape=jax.ShapeDtypeStruct((T, D_MODEL), BF16), compiler_params=_cparams("parallel", "parallel"),
    )(q, kv, kv)


def _xattn_bwd(q, kv, do, B, L, *, name):
    T = q.shape[0]
    tq = 512
    nq = L // tq
    scale = X_HD ** -0.5

    def body(q_ref, k_ref, v_ref, do_ref, dq_ref, dkv_ref):
        @pl.when(pl.program_id(1) == 0)
        def _():
            dkv_ref[...] = jnp.zeros_like(dkv_ref)

        heads = [slice(h * X_HD, (h + 1) * X_HD) for h in range(X_HEADS)]
        qs = [q_ref[:, sl].astype(BF16) for sl in heads]
        ks = [k_ref[:, sl].astype(BF16) for sl in heads]
        dos = [do_ref[:, sl].astype(BF16) for sl in heads]
        s = [_dot(a, b, _NT) * scale for a, b in zip(qs, ks)]
        dp = [_dot(a, v_ref[:, sl].astype(BF16), _NT) for a, sl in zip(dos, heads)]
        e = [jnp.exp(a - jnp.max(a, axis=-1, keepdims=True)) for a in s]
        p = [a / jnp.sum(a, axis=-1, keepdims=True) for a in e]
        ds = [(a * (b - jnp.sum(b * a, axis=-1, keepdims=True)) * scale).astype(BF16) for a, b in zip(p, dp)]
        dv = [_dot(a.astype(BF16), b, _TN) for a, b in zip(p, dos)]
        dq = [_dot(a, b) for a, b in zip(ds, ks)]
        dk = [_dot(a, b, _TN) for a, b in zip(ds, qs)]
        for h, sl in enumerate(heads):
            dq_ref[:, sl] = dq[h].astype(BF16)
            dkv_ref[:, sl] += dk[h]
            dkv_ref[:, D_MODEL + h * X_HD:D_MODEL + (h + 1) * X_HD] += dv[h]

    return pl.pallas_call(
        body, name=name, grid=(B, nq),
        in_specs=[pl.BlockSpec((tq, D_MODEL), lambda b, i: (b * nq + i, 0)),
                  pl.BlockSpec((MEM_LEN, D_MODEL), lambda b, i: (b, 0)),
                  pl.BlockSpec((MEM_LEN, D_MODEL), lambda b, i: (b, 1)),
                  pl.BlockSpec((tq, D_MODEL), lambda b, i: (b * nq + i, 0))],
        out_specs=[pl.BlockSpec((tq, D_MODEL), lambda b, i: (b * nq + i, 0)),
                   pl.BlockSpec((MEM_LEN, 2 * D_MODEL), lambda b, i: (b, 0))],
        out_shape=[jax.ShapeDtypeStruct((T, D_MODEL), BF16), jax.ShapeDtypeStruct((B * MEM_LEN, 2 * D_MODEL), F32)],
        compiler_params=_cparams("parallel", "arbitrary"),
    )(q, kv, kv, do)


def _chunk_masks():
    row = lax.broadcasted_iota(jnp.int32, (A_SUPER, A_SUPER), 0)
    col = lax.broadcasted_iota(jnp.int32, (A_SUPER, A_SUPER), 1)
    same = jnp.right_shift(row, 5) == jnp.right_shift(col, 5)
    return same, same & (col <= row), same & (col >= row)


def _dot_mask(mask, x):
    m = mask.astype(BF16)
    hi = x.astype(BF16)
    rest = x - hi.astype(F32)
    mid = rest.astype(BF16)
    lo = (rest - mid.astype(F32)).astype(BF16)
    return _dot(m, hi) + _dot(m, mid) + _dot(m, lo)


def _chunk_row(x, which):
    rows = [x[c * A_CHUNK + which % A_CHUNK:c * A_CHUNK + which % A_CHUNK + 1, :] for c in range(A_SUPER // A_CHUNK)]
    return jnp.concatenate([jnp.broadcast_to(r, (A_CHUNK, x.shape[1])) for r in rows], axis=0)


def _hgrn_gates(fa, lb):
    sig = _sigmoid(fa)
    f = lb + (1.0 - lb) * sig
    return sig, f, jnp.log(f), 1.0 - f


def _hgrn_fwd(z, lb, onw, B, L, *, name, gather=()):
    T = B * L
    ns = L // A_SUPER
    nch = A_SUPER // A_CHUNK
    ex = _Exchange(gather=gather)
    grid = (B, 4, ns)

    def body(q_ref, f_ref, v_ref, g_ref, lb_ref, w_ref, oa_ref, o_ref, s_ref, st_ref, sc_ref):
        @pl.when(pl.program_id(2) == 0)
        def _():
            st_ref[...] = jnp.zeros_like(st_ref)

        s_ref[0] = st_ref[...]
        same, tril, _ = _chunk_masks()
        q, v = q_ref[...], v_ref[...]
        _, _, lf, k = _hgrn_gates(f_ref[...], lb_ref[...])
        bcs = _dot_mask(tril, lf)
        bl = _chunk_row(bcs, -1)
        qd = (q * jnp.exp(bcs)).astype(BF16)
        ki = (k * jnp.exp(-bcs)).astype(BF16)
        ke = (k * jnp.exp(bl - bcs)).astype(BF16)
        dec = jnp.exp(bl)
        vb = v.astype(BF16)
        a = jnp.where(tril, _dot(qd, ki, _NT), 0.0)
        o_ref[...] = _dot(a.astype(BF16), vb)
        chunks = [slice(c * A_CHUNK, (c + 1) * A_CHUNK) for c in range(nch)]
        outer = [_dot(vb[rs], ke[rs], _TN) for rs in chunks]
        st = st_ref[...]
        for c, rs in enumerate(chunks):
            sc_ref[c] = st.astype(BF16)
            st = st * dec[c * A_CHUNK:c * A_CHUNK + 1, :] + outer[c]
        st_ref[...] = st
        for c, rs in enumerate(chunks):
            o_ref[rs, :] += _dot(qd[rs], sc_ref[c], _NT)
        o = o_ref[...]
        r = lax.rsqrt(jnp.mean(o * o, axis=-1, keepdims=True) + NORM_EPS)
        g = g_ref[...]
        oa_ref[...] = (o * r * w_ref[...] * (g * _sigmoid(g))).astype(BF16)

    def zspec(off):
        return pl.BlockSpec((A_SUPER, A_HEAD), lambda b, h, n: (b * ns + n, off + h))

    hvec = pl.BlockSpec((1, A_HEAD), lambda b, h, n: (0, h))
    ospec = pl.BlockSpec((A_SUPER, A_HEAD), lambda b, h, n: (b * ns + n, h))
    res = pl.pallas_call(
        _carrying(body, 6, 3, ex, grid) if ex.n else body, name=name, grid=grid,
        in_specs=[zspec(0), zspec(4), zspec(8), zspec(12), hvec, hvec] + ex.in_specs(),
        out_specs=[ospec, ospec, pl.BlockSpec((1, A_HEAD, A_HEAD), lambda b, h, n: ((b * 4 + h) * ns + n, 0, 0))] + ex.in_specs(),
        out_shape=[jax.ShapeDtypeStruct((T, A_WIDTH), BF16), jax.ShapeDtypeStruct((T, A_WIDTH), F32),
                   jax.ShapeDtypeStruct((B * 4 * ns, A_HEAD, A_HEAD), F32)] + ex.out_shapes(),
        scratch_shapes=[pltpu.VMEM((A_HEAD, A_HEAD), F32), pltpu.VMEM((nch, A_HEAD, A_HEAD), BF16)] + ex.scratch(),
        compiler_params=_cparams(*(("arbitrary",) * 3 if ex.n else ("parallel", "parallel", "arbitrary"))),
    )(z, z, z, z, lb, onw, *ex.operands())
    return tuple(res[:3]) + (list(res[3:]),)


def _hgrn_bwd(z, lb, onw, o_raw, s_start, doa, B, L, *, name, scatter=()):
    T = B * L
    ns = L // A_SUPER
    nch = A_SUPER // A_CHUNK
    ex = _Exchange(scatter=scatter)
    grid = (B, 4, ns)

    def body(q_ref, f_ref, v_ref, g_ref, lb_ref, w_ref, o_ref, s_ref, doa_ref,
             dq_ref, df_ref, dv_ref, dg_ref, dw_ref, dlb_ref, dst_ref, sc_ref, dsc_ref, dqd_ref, dke_ref, dblx_ref, dvacc_ref):
        @pl.when(pl.program_id(2) == 0)
        def _():
            dst_ref[...] = jnp.zeros_like(dst_ref)
            dw_ref[...] = jnp.zeros_like(dw_ref)
            dlb_ref[...] = jnp.zeros_like(dlb_ref)

        same, tril, triu = _chunk_masks()
        q, v, g, lb, w = q_ref[...], v_ref[...], g_ref[...], lb_ref[...], w_ref[...]
        sig, f, lf, k = _hgrn_gates(f_ref[...], lb)
        bcs = _dot_mask(tril, lf)
        bl = _chunk_row(bcs, -1)
        eb, enb, eeb = jnp.exp(bcs), jnp.exp(-bcs), jnp.exp(bl - bcs)
        qd, ki, ke = q * eb, k * enb, k * eeb
        qdb, kib, keb, vb = qd.astype(BF16), ki.astype(BF16), ke.astype(BF16), v.astype(BF16)
        dec = jnp.exp(bl)
        o = o_ref[...]
        r = lax.rsqrt(jnp.mean(o * o, axis=-1, keepdims=True) + NORM_EPS)
        on = o * r
        sg = _sigmoid(g)
        silu_g = g * sg
        doa = doa_ref[...]
        dg_ref[...] = (doa * on * w * (sg * (1.0 + g * (1.0 - sg)))).astype(BF16)
        dw_ref[0] += jnp.sum(doa * on * silu_g, axis=0, keepdims=True)
        don = doa * w * silu_g
        do = r * (don - on * jnp.mean(don * on, axis=-1, keepdims=True))
        dob = do.astype(BF16)
        a = jnp.where(tril, _dot(qdb, kib, _NT), 0.0).astype(BF16)
        da = jnp.where(tril, _dot(dob, vb, _NT), 0.0).astype(BF16)
        dvacc_ref[...] = _dot(a, dob, _TN)
        dqd_ref[...] = _dot(da, kib)
        dki = _dot(da, qdb, _TN)
        chunks = [slice(c * A_CHUNK, (c + 1) * A_CHUNK) for c in range(nch)]
        outer = [_dot(vb[rs], keb[rs], _TN) for rs in chunks]
        st = s_ref[0]
        for c in range(nch):
            sc_ref[c] = st
            st = st * dec[c * A_CHUNK:c * A_CHUNK + 1, :] + outer[c]
        outer_g = [_dot(dob[rs], qdb[rs], _TN) for rs in chunks]
        dst = dst_ref[...]
        for c in reversed(range(nch)):
            dsc_ref[c] = dst
            dst = dst * dec[c * A_CHUNK:c * A_CHUNK + 1, :] + outer_g[c]
        dst_ref[...] = dst
        for c, rs in enumerate(chunks):
            dec_c = dec[c * A_CHUNK:c * A_CHUNK + 1, :]
            dsc, stc = dsc_ref[c], sc_ref[c]
            dscb = dsc.astype(BF16)
            dvacc_ref[rs, :] += _dot(keb[rs], dscb, _NT)
            dke_ref[rs, :] = _dot(vb[rs], dscb)
            ddec = jnp.sum(dsc * stc, axis=0, keepdims=True)
            dqd_ref[rs, :] += _dot(dob[rs], stc.astype(BF16))
            dblx_ref[rs, :] = jnp.broadcast_to(ddec * dec_c, (A_CHUNK, A_HEAD))
        dqd, dke = dqd_ref[...], dke_ref[...]
        dv_ref[...] = dvacc_ref[...].astype(BF16)
        dq_ref[...] = (dqd * eb).astype(BF16)
        keke = dke * ke
        db = dqd * qd - dki * ki - keke
        sums = _dot_mask(triu, jnp.concatenate([db, keke], axis=1))
        dk = dki * enb + dke * eeb
        dlf = sums[:, :A_HEAD] + _chunk_row(sums[:, A_HEAD:], 0) + dblx_ref[...]
        dff = dlf / f - dk
        df_ref[...] = (dff * (1.0 - lb) * sig * (1.0 - sig)).astype(BF16)
        dlb_ref[0] += jnp.sum(dff * (1.0 - sig), axis=0, keepdims=True)

    def rev(n):
        return ns - 1 - n

    def zspec(off):
        return pl.BlockSpec((A_SUPER, A_HEAD), lambda b, h, n: (b * ns + rev(n), off + h))

    hvec = pl.BlockSpec((1, A_HEAD), lambda b, h, n: (0, h))
    ospec = pl.BlockSpec((A_SUPER, A_HEAD), lambda b, h, n: (b * ns + rev(n), h))
    acc = pl.BlockSpec((1, 1, A_HEAD), lambda b, h, n: (b * 4 + h, 0, 0))
    big = jax.ShapeDtypeStruct((T, A_WIDTH), BF16)
    small = jax.ShapeDtypeStruct((B * 4, 1, A_HEAD), F32)
    res = pl.pallas_call(
        _carrying(body, 9, 6, ex, grid) if ex.n else body, name=name, grid=grid,
        in_specs=[zspec(0), zspec(4), zspec(8), zspec(12), hvec, hvec, ospec,
                  pl.BlockSpec((1, A_HEAD, A_HEAD), lambda b, h, n: ((b * 4 + h) * ns + rev(n), 0, 0)), ospec] + ex.in_specs(),
        out_specs=[ospec, ospec, ospec, ospec, acc, acc] + ex.in_specs(),
        out_shape=[big, big, big, big, small, small] + ex.out_shapes(),
        scratch_shapes=[pltpu.VMEM((A_HEAD, A_HEAD), F32), pltpu.VMEM((nch, A_HEAD, A_HEAD), F32),
                        pltpu.VMEM((nch, A_HEAD, A_HEAD), F32),
                        pltpu.VMEM((A_SUPER, A_HEAD), F32), pltpu.VMEM((A_SUPER, A_HEAD), F32),
                        pltpu.VMEM((A_SUPER, A_HEAD), F32), pltpu.VMEM((A_SUPER, A_HEAD), F32)] + ex.scratch(),
        compiler_params=_cparams(*(("arbitrary",) * 3 if ex.n else ("parallel", "parallel", "arbitrary"))),
    )(z, z, z, z, lb, onw, o_raw, s_start, doa, *ex.operands())
    return tuple(res[:6]) + (list(res[6:]),)


def _rope_tables(L):
    half = A_HEAD // 2
    inv_freq = ROPE_THETA ** (-jnp.arange(half, dtype=F32) / half)
    ang = jnp.arange(L, dtype=F32)[:, None] * inv_freq[None, :]
    cos, sin = jnp.cos(ang), jnp.sin(ang)
    return jnp.concatenate([cos, cos], axis=-1), jnp.concatenate([-sin, sin], axis=-1)


def _rope_fwd(z, cos2, sin2, B, L, *, name):
    T = B * L
    tm = 512
    nl = L // tm

    def body(x_ref, c_ref, s_ref, q_ref, k_ref):
        c, s = c_ref[...], s_ref[...]
        for h in range(8):
            x = x_ref[:, h * A_HEAD:(h + 1) * A_HEAD]
            out = x * c + pltpu.roll(x, A_HEAD // 2, 1) * s
            o_ref = q_ref if h < 4 else k_ref
            o_ref[:, (h % 4) * A_HEAD:(h % 4 + 1) * A_HEAD] = out

    tab = pl.BlockSpec((tm, A_HEAD), lambda i: (i % nl, 0))
    out = pl.BlockSpec((tm, 512), lambda i: (i, 0))
    return pl.pallas_call(
        body, name=name, grid=(T // tm,),
        in_specs=[pl.BlockSpec((tm, 1024), lambda i: (i, 2)), tab, tab], out_specs=[out, out],
        out_shape=[jax.ShapeDtypeStruct((T, 512), F32)] * 2, compiler_params=_cparams("parallel"),
    )(z, cos2, sin2)


def _rope_bwd(dqs, dks, dvs, cos2, sin2, B, L, *, name):
    T = B * L
    tm = 512
    nl = L // tm

    def body(*refs):
        c, s = refs[9][...], refs[10][...]
        o_ref = refs[11]
        for part in range(3):
            a_ref, b_ref, c_ref = refs[3 * part:3 * part + 3]
            for h in range(4):
                cols = slice(h * A_HEAD, (h + 1) * A_HEAD)
                d = a_ref[:, cols] + b_ref[:, cols] + c_ref[:, cols]
                if part < 2:
                    d = d * c - pltpu.roll(d, A_HEAD // 2, 1) * s
                o_ref[:, part * 512 + h * A_HEAD:part * 512 + (h + 1) * A_HEAD] = d.astype(BF16)

    blk = pl.BlockSpec((tm, 512), lambda i: (i, 0))
    tab = pl.BlockSpec((tm, A_HEAD), lambda i: (i % nl, 0))
    return pl.pallas_call(
        body, name=name, grid=(T // tm,), in_specs=[blk] * 9 + [tab, tab],
        out_specs=pl.BlockSpec((tm, 1536), lambda i: (i, 0)),
        out_shape=jax.ShapeDtypeStruct((T, 1536), BF16), compiler_params=_cparams("parallel"),
    )(*dqs, *dks, *dvs, cos2, sin2)


def _band_masks():
    i = lax.broadcasted_iota(jnp.int32, (B_SPAN, B_SPAN), 0)
    j = lax.broadcasted_iota(jnp.int32, (B_SPAN, B_SPAN), 1)
    return i <= j, j <= i


class _DilPlan:
    def __init__(self, dil, B, L):
        self.dil, self.B, self.L = dil, B, L
        self.rows = 4 * B_SPAN if dil == 1 else B_SPAN * dil
        self.n = L // self.rows
        self.hr = 4 if dil == 4 else 1
        self.cw = 512 if dil == 1 else A_HEAD
        self.has_other = dil != 16
        self.grid = (B, self.n, 4 if dil == 16 else 1)
        if dil == 1:
            self.items = [(j, 0, h) for j in range(4) for h in range(4)]
        elif dil == 4:
            self.items = [(0, r, h) for r in range(4) for h in range(4)]
        else:
            self.items = [(0, r, 0) for r in range(16)]
        self.groups = [self.items[i:i + 4] for i in range(0, 16, 4)]

    def operands(self, arr):
        return [arr] * self.hr

    def specs(self, col0, role="cur"):
        n, nb128 = self.n, self.L // B_SPAN
        out = []
        for h in range(self.hr):
            cb = col0 // self.cw + h
            if role == "cur":
                out.append(pl.BlockSpec((self.rows, self.cw), lambda b, i, hh, cb=cb: (b * n + i, cb + hh)))
            elif self.dil == 1:
                shift = -1 if role == "prev" else 4
                out.append(pl.BlockSpec((B_SPAN, self.cw),
                                        lambda b, i, hh, cb=cb, shift=shift: (b * nb128 + jnp.clip(4 * i + shift, 0, nb128 - 1), cb)))
            else:
                shift = -1 if role == "prev" else 1
                out.append(pl.BlockSpec((self.rows, self.cw),
                                        lambda b, i, hh, cb=cb, shift=shift: (b * n + jnp.clip(i + shift, 0, n - 1), cb)))
        return out

    def out_spec(self):
        n = self.n
        if self.dil == 4:
            return pl.BlockSpec((self.rows, 512), lambda b, i, hh: (b * n + i, 0))
        return pl.BlockSpec((self.rows, self.cw), lambda b, i, hh: (b * n + i, hh))

    def scratch(self, n_out):
        return [pltpu.VMEM((4, self.rows, A_HEAD), F32)] * n_out if self.dil == 4 else []

    def store(self, out_ref, scr, item, val):
        j, r, h = item
        if self.dil == 1:
            out_ref[pl.ds(j * B_SPAN, B_SPAN), pl.ds(h * A_HEAD, A_HEAD)] = val
        elif self.dil == 4:
            scr.at[h][pl.ds(r, B_SPAN, stride=4), :] = val
        else:
            out_ref[pl.ds(r, B_SPAN, stride=self.dil), :] = val

    def flush(self, out_ref, scr):
        if self.dil == 4:
            for h in range(4):
                out_ref[:, h * A_HEAD:(h + 1) * A_HEAD] = scr[h]

    def cur(self, refs, item):
        j, r, h = item
        if self.dil == 1:
            return refs[0], (pl.ds(j * B_SPAN, B_SPAN), pl.ds(h * A_HEAD, A_HEAD))
        return refs[h], (pl.ds(r, B_SPAN, stride=self.dil), slice(None))

    def other(self, refs, other_refs, item, role):
        j, r, h = item
        if self.dil == 1:
            cols = pl.ds(h * A_HEAD, A_HEAD)
            jj = j - 1 if role == "prev" else j + 1
            if 0 <= jj < 4:
                return refs[0], (pl.ds(jj * B_SPAN, B_SPAN), cols)
            return other_refs[0], (pl.ds(0, B_SPAN), cols)
        return other_refs[h], (pl.ds(r, B_SPAN, stride=self.dil), slice(None))

    def other_valid(self, item, role):
        j = item[0]
        i = pl.program_id(1)
        if role == "prev":
            return True if (self.dil == 1 and j > 0) else i > 0
        return True if (self.dil == 1 and j < 3) else i < self.n - 1


def _ld(pair):
    ref, idx = pair
    return ref[idx]


def _dil_fwd(qr, kr, z, dil, B, L, *, name):
    T = B * L
    plan = _DilPlan(dil, B, L)
    hr, has_prev = plan.hr, plan.has_other
    scale = A_HEAD ** -0.5

    n_t = 5 if has_prev else 3

    def body(*refs):
        lists = [refs[i * hr:(i + 1) * hr] for i in range(n_t)]
        o_ref, l_ref = refs[n_t * hr], refs[n_t * hr + 1]
        scr = refs[n_t * hr + 2:] or (None, None)
        if has_prev:
            q_r, kc_r, vc_r, kp_r, vp_r = lists
        else:
            q_r, kc_r, vc_r = lists
        mp0, mc = _band_masks()

        for group in plan.groups:
            qs = [_ld(plan.cur(q_r, i)).astype(BF16) for i in group]
            sc = [jnp.where(mc, _dot(q, _ld(plan.cur(kc_r, i)).astype(BF16), _NT) * scale, NEG_BIG) for q, i in zip(qs, group)]
            m = [jnp.max(s, axis=-1, keepdims=True) for s in sc]
            if has_prev:
                mps = [mp0 & plan.other_valid(i, "prev") for i in group]
                sp = [jnp.where(mk, _dot(q, _ld(plan.other(kc_r, kp_r, i, "prev")).astype(BF16), _NT) * scale, NEG_BIG)
                      for q, i, mk in zip(qs, group, mps)]
                m = [jnp.maximum(a, jnp.max(s, axis=-1, keepdims=True)) for a, s in zip(m, sp)]
            pc = [jnp.exp(s - a) for s, a in zip(sc, m)]
            l = [jnp.sum(p, axis=-1, keepdims=True) for p in pc]
            o = [_dot(p.astype(BF16), _ld(plan.cur(vc_r, i)).astype(BF16)) for p, i in zip(pc, group)]
            if has_prev:
                pp = [jnp.exp(s - a) for s, a in zip(sp, m)]
                l = [a + jnp.sum(p, axis=-1, keepdims=True) for a, p in zip(l, pp)]
                o = [a + _dot(p.astype(BF16), _ld(plan.other(vc_r, vp_r, i, "prev")).astype(BF16)) for a, p, i in zip(o, pp, group)]
            for i, oi, li, mi in zip(group, o, l, m):
                plan.store(o_ref, scr[0], i, oi / li)
                plan.store(l_ref, scr[1], i, jnp.broadcast_to(mi + jnp.log(li), (B_SPAN, A_HEAD)))
        plan.flush(o_ref, scr[0])
        plan.flush(l_ref, scr[1])

    tensors = [(qr, 0, "cur"), (kr, 0, "cur"), (z, 3072, "cur")] + ([(kr, 0, "prev"), (z, 3072, "prev")] if has_prev else [])
    return pl.pallas_call(
        body, name=name, grid=plan.grid,
        in_specs=[sp for _, c, role in tensors for sp in plan.specs(c, role)],
        out_specs=[plan.out_spec()] * 2, out_shape=[jax.ShapeDtypeStruct((T, 512), F32)] * 2,
        scratch_shapes=plan.scratch(2), compiler_params=_cparams("parallel", "parallel", "parallel"),
    )(*[a for arr, _, _ in tensors for a in plan.operands(arr)])


def _dil_combine(os_, ls_, *, name):
    T = os_[0].shape[0]
    tm = 512

    def body(o1, o2, o3, l1, l2, l3, ob_ref, lse_ref):
        a1, a2, a3 = l1[...], l2[...], l3[...]
        m = jnp.maximum(jnp.maximum(a1, a2), a3)
        e1, e2, e3 = jnp.exp(a1 - m), jnp.exp(a2 - m), jnp.exp(a3 - m)
        den = e1 + e2 + e3
        ob_ref[...] = (e1 * o1[...] + e2 * o2[...] + e3 * o3[...]) / den
        lse_ref[...] = m + jnp.log(den)

    blk = pl.BlockSpec((tm, 512), lambda i: (i, 0))
    return pl.pallas_call(
        body, name=name, grid=(T // tm,), in_specs=[blk] * 6, out_specs=[blk, blk],
        out_shape=[jax.ShapeDtypeStruct((T, 512), F32)] * 2, compiler_params=_cparams("parallel"),
    )(*[o.reshape(T, 512) for o in os_], *[l.reshape(T, 512) for l in ls_])


def _dil_bwd_q(qr, kr, z, dymix, out, lse, dil, B, L, *, name):
    T = B * L
    plan = _DilPlan(dil, B, L)
    hr, has_prev = plan.hr, plan.has_other
    scale = A_HEAD ** -0.5
    n_t = 8 if has_prev else 6

    def body(*refs):
        lists = [refs[i * hr:(i + 1) * hr] for i in range(n_t)]
        dq_ref = refs[n_t * hr]
        scr = refs[n_t * hr + 1:] or (None,)
        q_r, kc_r, vc_r, do_r, out_r, lse_r = lists[:6]
        mp0, mc = _band_masks()

        for group in plan.groups:
            qs = [_ld(plan.cur(q_r, i)).astype(BF16) for i in group]
            dos = [_ld(plan.cur(do_r, i)) for i in group]
            delta = [jnp.sum(d * _ld(plan.cur(out_r, i)), axis=-1, keepdims=True) for d, i in zip(dos, group)]
            dob = [d.astype(BF16) for d in dos]
            lse = [_ld(plan.cur(lse_r, i)) for i in group]
            kc = [_ld(plan.cur(kc_r, i)).astype(BF16) for i in group]
            pc = [jnp.where(mc, jnp.exp(_dot(q, k, _NT) * scale - a), 0.0) for q, k, a in zip(qs, kc, lse)]
            dsc = [p * (_dot(d, _ld(plan.cur(vc_r, i)).astype(BF16), _NT) - dl) * scale
                   for p, d, i, dl in zip(pc, dob, group, delta)]
            dq = [_dot(d.astype(BF16), k) for d, k in zip(dsc, kc)]
            if has_prev:
                kp_r, vp_r = lists[6], lists[7]
                mps = [mp0 & plan.other_valid(i, "prev") for i in group]
                kp = [_ld(plan.other(kc_r, kp_r, i, "prev")).astype(BF16) for i in group]
                pp = [jnp.where(mk, jnp.exp(_dot(q, k, _NT) * scale - a), 0.0) for q, k, a, mk in zip(qs, kp, lse, mps)]
                dsp = [p * (_dot(d, _ld(plan.other(vc_r, vp_r, i, "prev")).astype(BF16), _NT) - dl) * scale
                       for p, d, i, dl in zip(pp, dob, group, delta)]
                dq = [a + _dot(d.astype(BF16), k) for a, d, k in zip(dq, dsp, kp)]
            for i, d in zip(group, dq):
                plan.store(dq_ref, scr[0], i, d)
        plan.flush(dq_ref, scr[0])

    tensors = ([(qr, 0, "cur"), (kr, 0, "cur"), (z, 3072, "cur"), (dymix, 512, "cur"), (out, 0, "cur"), (lse, 0, "cur")]
               + ([(kr, 0, "prev"), (z, 3072, "prev")] if has_prev else []))
    return pl.pallas_call(
        body, name=name, grid=plan.grid,
        in_specs=[sp for _, c, role in tensors for sp in plan.specs(c, role)],
        out_specs=plan.out_spec(), out_shape=jax.ShapeDtypeStruct((T, 512), F32),
        scratch_shapes=plan.scratch(1), compiler_params=_cparams("parallel", "parallel", "parallel"),
    )(*[a for arr, _, _ in tensors for a in plan.operands(arr)])


def _dil_bwd_kv(qr, kr, z, dymix, out, lse, dil, B, L, *, name):
    T = B * L
    plan = _DilPlan(dil, B, L)
    hr, has_next = plan.hr, plan.has_other
    scale = A_HEAD ** -0.5
    n_t = 10 if has_next else 6

    def body(*refs):
        lists = [refs[i * hr:(i + 1) * hr] for i in range(n_t)]
        dk_ref, dv_ref = refs[n_t * hr], refs[n_t * hr + 1]
        scr = refs[n_t * hr + 2:] or (None, None)
        k_r, v_r = lists[0], lists[1]
        own = lists[2:6]
        mp0, mc = _band_masks()

        for group in plan.groups:
            kh = [_ld(plan.cur(k_r, i)).astype(BF16) for i in group]
            vh = [_ld(plan.cur(v_r, i)).astype(BF16) for i in group]
            dk, dv = [None] * len(group), [None] * len(group)
            for role in ("own", "next") if has_next else ("own",):
                if role == "own":
                    get = lambda t, i: _ld(plan.cur(own[t], i))
                    masks = [mc] * len(group)
                else:
                    get = lambda t, i: _ld(plan.other(own[t], lists[6 + t], i, "next"))
                    masks = [mp0 & plan.other_valid(i, "next") for i in group]
                qs = [get(0, i).astype(BF16) for i in group]
                dos = [get(1, i) for i in group]
                delta = [jnp.sum(d * get(2, i), axis=-1, keepdims=True) for d, i in zip(dos, group)]
                dob = [d.astype(BF16) for d in dos]
                p = [jnp.where(mk, jnp.exp(_dot(q, k, _NT) * scale - get(3, i)), 0.0) for q, k, i, mk in zip(qs, kh, group, masks)]
                dvn = [_dot(a.astype(BF16), d, _TN) for a, d in zip(p, dob)]
                ds = [a * (_dot(d, v, _NT) - dl) * scale for a, d, v, dl in zip(p, dob, vh, delta)]
                dkn = [_dot(d.astype(BF16), q, _TN) for d, q in zip(ds, qs)]
                dv = [n if o is None else o + n for o, n in zip(dv, dvn)]
                dk = [n if o is None else o + n for o, n in zip(dk, dkn)]
            for i, a, b in zip(group, dk, dv):
                plan.store(dk_ref, scr[0], i, a)
                plan.store(dv_ref, scr[1], i, b)
        plan.flush(dk_ref, scr[0])
        plan.flush(dv_ref, scr[1])

    queries = [(qr, 0), (dymix, 512), (out, 0), (lse, 0)]
    tensors = ([(kr, 0, "cur"), (z, 3072, "cur")] + [(a, c, "cur") for a, c in queries]
               + ([(a, c, "next") for a, c in queries] if has_next else []))
    return pl.pallas_call(
        body, name=name, grid=plan.grid,
        in_specs=[sp for _, c, role in tensors for sp in plan.specs(c, role)],
        out_specs=[plan.out_spec()] * 2, out_shape=[jax.ShapeDtypeStruct((T, 512), F32)] * 2,
        scratch_shapes=plan.scratch(2), compiler_params=_cparams("parallel", "parallel", "parallel"),
    )(*[a for arr, _, _ in tensors for a in plan.operands(arr)])


def _s5_build(lam_re, lam_im, log_dt, b_re, b_im, c_re, c_im):
    G, P, TC = C_GROUPS, C_STATE, C_TC
    lr = jnp.minimum(lam_re, C_MIN_NEG_RE)
    li = lam_im
    dt = jnp.exp(log_dt)[:, None]
    mag = jnp.exp(dt * lr)
    ar, ai = mag * jnp.cos(dt * li), mag * jnp.sin(dt * li)
    den = lr * lr + li * li
    zr = ((ar - 1.0) * lr + ai * li) / den
    zi = (ai * lr - (ar - 1.0) * li) / den
    bbr = zr[..., None] * b_re - zi[..., None] * b_im
    bbi = zr[..., None] * b_im + zi[..., None] * b_re
    ks = jnp.arange(TC + 1, dtype=F32)[:, None, None]
    pmag = jnp.exp(ks * (dt * lr)[None])
    pr, pi = pmag * jnp.cos(ks * (dt * li)[None]), pmag * jnp.sin(ks * (dt * li)[None])
    car = c_re[None] * pr[:, :, None, :] - c_im[None] * pi[:, :, None, :]
    cai = c_re[None] * pi[:, :, None, :] + c_im[None] * pr[:, :, None, :]
    kern = (jnp.einsum('lgop,gpc->lgco', car[:TC], bbr, precision=HI)
            - jnp.einsum('lgop,gpc->lgco', cai[:TC], bbi, precision=HI))
    pr_e, pi_e = pr[TC - 1 - jnp.arange(TC)], pi[TC - 1 - jnp.arange(TC)]
    er = pr_e[:, :, :, None] * bbr[None] - pi_e[:, :, :, None] * bbi[None]
    ei = pr_e[:, :, :, None] * bbi[None] + pi_e[:, :, :, None] * bbr[None]
    ez = jnp.stack([er, ei], axis=2).reshape(TC, C_NB, C_GB, 2, P, C_GROUP).transpose(1, 0, 2, 5, 3, 4)
    fz = jnp.stack([car[1:], -cai[1:]], axis=0).reshape(2, TC, C_NB, C_GB, C_GROUP, P).transpose(2, 0, 3, 5, 1, 4)
    return kern, ez, fz, pr[TC], pi[TC]


def _s5_lag_blocks(kern):
    eye = jnp.eye(C_GB, dtype=kern.dtype)
    return (kern.reshape(C_TC, C_NB, C_GB, C_GROUP, C_GROUP)[:, :, :, :, None, :]
            * eye[None, None, :, None, :, None]).reshape(C_TC, C_NB, LANES, LANES)


def _s5_dense(kbd, ez, fz, *, name):
    ezc = ez.reshape(C_NB, C_W8, LANES)
    fzc = fz.reshape(C_NB, C_S8, LANES)
    half = C_GB * C_STATE

    def body(k_ref, e_ref, f_ref, m8_ref, e8_ref, f8_ref):
        zero = jnp.zeros((LANES, LANES), BF16)
        for s in range(C_TC):
            for t in range(C_TC):
                m8_ref[0, s * LANES:(s + 1) * LANES, t * LANES:(t + 1) * LANES] = (
                    k_ref[t - s, 0].astype(BF16) if t >= s else zero)
        lane = lax.broadcasted_iota(jnp.int32, (1, LANES), 1)
        src = e_ref[0]
        swapped = pltpu.roll(src, C_STATE, 1)
        rowg = jnp.bitwise_and(jnp.right_shift(lax.broadcasted_iota(jnp.int32, (C_W8, LANES), 0), 4), C_GB - 1)
        for kb in range(C_S8 // LANES):
            z, g0 = kb // (C_GB // 2), 2 * (kb % (C_GB // 2))
            first = jnp.where(lane < C_STATE, src if z == 0 else swapped, 0.0)
            second = jnp.where(lane >= C_STATE, swapped if z == 0 else src, 0.0)
            e8_ref[0, :, kb * LANES:(kb + 1) * LANES] = jnp.where(
                rowg == g0, first, jnp.where(rowg == g0 + 1, second, 0.0)).astype(BF16)
        for z in range(2):
            for g in range(C_GB):
                rows = slice(z * half + g * C_STATE, z * half + (g + 1) * C_STATE)
                piece = f_ref[0, rows, :]
                mine = (lane >= g * C_GROUP) & (lane < (g + 1) * C_GROUP)
                for t in range(C_TC):
                    f8_ref[0, rows, t * LANES:(t + 1) * LANES] = jnp.where(
                        mine, pltpu.roll(piece, ((g - t) * C_GROUP) % LANES, 1), 0.0).astype(BF16)

    blk = lambda r, c: pl.BlockSpec((1, r, c), lambda b: (b, 0, 0))
    return pl.pallas_call(
        body, name=name, grid=(C_NB,),
        in_specs=[pl.BlockSpec((C_TC, 1, LANES, LANES), lambda b: (0, b, 0, 0)), blk(C_W8, LANES), blk(C_S8, LANES)],
        out_specs=[blk(C_W8, C_W8), blk(C_W8, C_S8), blk(C_S8, C_W8)],
        out_shape=[jax.ShapeDtypeStruct((C_NB, C_W8, C_W8), BF16), jax.ShapeDtypeStruct((C_NB, C_W8, C_S8), BF16),
                   jax.ShapeDtypeStruct((C_NB, C_S8, C_W8), BF16)],
        compiler_params=_cparams("parallel"),
    )(kbd, ezc, fzc)


C_NB = C_GROUPS * C_GROUP // LANES
C_GB = C_GROUPS // C_NB
C_W8 = C_TC * LANES
C_S8 = 2 * C_GB * C_STATE


def _s5_scan_tables(lam_re, lam_im, log_dt, nsteps):
    lr = jnp.minimum(lam_re, C_MIN_NEG_RE)
    dt = jnp.exp(log_dt)[:, None]
    ks = (C_TC * 2.0 ** jnp.arange(8, dtype=F32))[None, :, None]
    keep = (jnp.arange(8) < nsteps)[None, :, None]
    pmag = jnp.exp(ks * (dt * lr)[:, None, :])
    ang = ks * (dt * lam_im)[:, None, :]

    def blocks(t):
        return t.reshape(C_NB, C_GB, 8, C_STATE).transpose(0, 2, 1, 3).reshape(C_NB, 8, C_GB * C_STATE)

    pr = blocks(jnp.where(keep, pmag * jnp.cos(ang), 0.0))
    pi = blocks(jnp.where(keep, pmag * jnp.sin(ang), 0.0))
    return jnp.concatenate([pr, pr], axis=-1), jnp.concatenate([-pi, pi], axis=-1)


def _s5_rows(t, R):
    return pl.ds(t, R, stride=C_TC)


def _s5_fwd(u, dsk, m8, e8, f8, tab_r, tab_i, B, L, *, name, gather=()):
    T = B * L
    R = L // C_TC
    nsteps = int(math.log2(R))
    ex = _Exchange(gather=gather)
    grid = (C_NB, B)

    def body(u_ref, d_ref, m_ref, e_ref, f_ref, tr_ref, ti_ref, gl_ref, y_ref, x8_ref, xs_ref):
        for t in range(C_TC):
            x8_ref[0, :, t * LANES:(t + 1) * LANES] = u_ref[_s5_rows(t, R), :].astype(BF16)
        x8 = x8_ref[0]
        x = _dot(x8, e_ref[0])
        row = lax.broadcasted_iota(jnp.int32, (R, C_S8), 0)
        for k in range(nsteps):
            s = 1 << k
            sh = pltpu.roll(x, s, 0)
            upd = tr_ref[0, k:k + 1, :] * sh + ti_ref[0, k:k + 1, :] * pltpu.roll(sh, C_S8 // 2, 1)
            x = x + jnp.where(row >= s, upd, 0.0)
        xs = jnp.where(row >= 1, pltpu.roll(x, 1, 0), 0.0)
        xs_ref[0] = xs
        y8 = _dot(x8, m_ref[0]) + _dot(xs.astype(BF16), f_ref[0])
        d = d_ref[...]
        for t in range(C_TC):
            rows = _s5_rows(t, R)
            y = y8[:, t * LANES:(t + 1) * LANES] + d * u_ref[rows, :]
            y_ref[rows, :] = y
            gl_ref[rows, :] = 0.5 * y * (1.0 + _erf(y * (2.0 ** -0.5)))

    tok = pl.BlockSpec((L, LANES), lambda c, b: (b, c))
    per_block = lambda shape: pl.BlockSpec((1,) + shape, lambda c, b: (c, 0, 0))
    per_step = lambda shape: pl.BlockSpec((1,) + shape, lambda c, b: (c * B + b, 0, 0))
    res = pl.pallas_call(
        _carrying(body, 7, 4, ex, grid) if ex.n else body, name=name, grid=grid,
        in_specs=[tok, pl.BlockSpec((1, LANES), lambda c, b: (0, c)), per_block((C_W8, C_W8)), per_block((C_W8, C_S8)),
                  per_block((C_S8, C_W8)), per_block((8, C_S8)), per_block((8, C_S8))] + ex.in_specs(),
        out_specs=[tok, tok, per_step((R, C_W8)), per_step((R, C_S8))] + ex.in_specs(),
        out_shape=[jax.ShapeDtypeStruct((T, D_MODEL), F32), jax.ShapeDtypeStruct((T, D_MODEL), F32),
                   jax.ShapeDtypeStruct((C_NB * B, R, C_W8), BF16), jax.ShapeDtypeStruct((C_NB * B, R, C_S8), F32)] + ex.out_shapes(),
        scratch_shapes=ex.scratch(),
        compiler_params=_cparams(*(("arbitrary",) * 2 if ex.n else ("parallel", "parallel"))),
    )(u, dsk, m8, e8, f8, tab_r, tab_i, *ex.operands())
    return tuple(res[:4]) + (list(res[4:]),)


def _s5_bwd(dgl, y, u, dsk, xs, m8, e8, f8, tab_r, tab_i, B, L, *, name):
    T = B * L
    R = L // C_TC
    nsteps = int(math.log2(R))

    def body(dgl_ref, y_ref, u_ref, d_ref, xs_ref, m_ref, e_ref, f_ref, tr_ref, ti_ref,
             du_ref, dy8_ref, de_ref, da_ref, dd_ref, dyf_ref):
        @pl.when(pl.program_id(1) == 0)
        def _():
            da_ref[...] = jnp.zeros_like(da_ref)
            dd_ref[...] = jnp.zeros_like(dd_ref)

        dd = jnp.zeros((1, LANES), F32)
        for t in range(C_TC):
            rows = _s5_rows(t, R)
            yv = y_ref[rows, :]
            cdf = 0.5 * (1.0 + _erf(yv * (2.0 ** -0.5)))
            pdf = jnp.exp(-0.5 * yv * yv) * (1.0 / math.sqrt(2.0 * math.pi))
            dy = dgl_ref[rows, :] * (cdf + yv * pdf)
            dd = dd + jnp.sum(dy * u_ref[rows, :], axis=0, keepdims=True)
            dyf_ref[:, t * LANES:(t + 1) * LANES] = dy
        dd_ref[...] += dd
        dy8 = dyf_ref[...].astype(BF16)
        dy8_ref[0] = dy8
        xs = xs_ref[0]
        gx = _dot(dy8, f_ref[0], _NT)
        row = lax.broadcasted_iota(jnp.int32, (R, C_S8), 0)
        for k in range(nsteps):
            s = 1 << k
            sh = pltpu.roll(gx, R - s, 0)
            upd = tr_ref[0, k:k + 1, :] * sh - ti_ref[0, k:k + 1, :] * pltpu.roll(sh, C_S8 // 2, 1)
            gx = gx + jnp.where(row + s < R, upd, 0.0)
        de_in = jnp.where(row + 1 < R, pltpu.roll(gx, R - 1, 0), 0.0)
        deb = de_in.astype(BF16)
        de_ref[0] = deb
        da_ref[0, 0:1, :] += jnp.sum(de_in * xs, axis=0, keepdims=True)
        da_ref[0, 1:2, :] += jnp.sum(de_in * pltpu.roll(xs, C_S8 // 2, 1), axis=0, keepdims=True)
        dx8 = _dot(dy8, m_ref[0], _NT) + _dot(deb, e_ref[0], _NT)
        d = d_ref[...]
        for t in range(C_TC):
            cols = slice(t * LANES, (t + 1) * LANES)
            du_ref[_s5_rows(t, R), :] = dx8[:, cols] + d * dyf_ref[:, cols]

    tok = pl.BlockSpec((L, LANES), lambda c, b: (b, c))
    vec = pl.BlockSpec((1, LANES), lambda c, b: (0, c))
    per_block = lambda shape: pl.BlockSpec((1,) + shape, lambda c, b: (c, 0, 0))
    per_step = lambda shape: pl.BlockSpec((1,) + shape, lambda c, b: (c * B + b, 0, 0))
    return pl.pallas_call(
        body, name=name, grid=(C_NB, B),
        in_specs=[tok, tok, tok, vec, per_step((R, C_S8)), per_block((C_W8, C_W8)),
                  per_block((C_W8, C_S8)), per_block((C_S8, C_W8)), per_block((8, C_S8)), per_block((8, C_S8))],
        out_specs=[tok, per_step((R, C_W8)), per_step((R, C_S8)), per_block((8, C_S8)), vec],
        out_shape=[jax.ShapeDtypeStruct((T, D_MODEL), F32), jax.ShapeDtypeStruct((C_NB * B, R, C_W8), BF16),
                   jax.ShapeDtypeStruct((C_NB * B, R, C_S8), BF16), jax.ShapeDtypeStruct((C_NB, 8, C_S8), F32),
                   jax.ShapeDtypeStruct((1, D_MODEL), F32)],
        scratch_shapes=[pltpu.VMEM((R, C_W8), F32)],
        compiler_params=_cparams("parallel", "arbitrary"),
    )(dgl, y, u, dsk, xs, m8, e8, f8, tab_r, tab_i)


def _bmm_tn(a, b, nb, fold, *, name):
    a = a.reshape(nb, -1, a.shape[-1])
    b = b.reshape(nb, -1, b.shape[-1])
    K, M, N = a.shape[1], a.shape[2], b.shape[2]
    half = C_GB * C_STATE

    def body(a_ref, b_ref, o_ref, p_ref):
        p_ref[...] = _dot(a_ref[0].astype(BF16), b_ref[0].astype(BF16), _TN)
        lane = lax.broadcasted_iota(jnp.int32, (1, LANES), 1)
        if fold == "lags":
            for lag in range(C_TC):
                blocks = [p_ref[s * LANES:(s + 1) * LANES, (s + lag) * LANES:(s + lag + 1) * LANES] for s in range(C_TC - lag)]
                o_ref[0, lag] = functools.reduce(lambda u, v: u + v, blocks)
        elif fold == "e":
            for g in range(C_GB):
                lo, hi = LANES * (g // 2), half + LANES * (g // 2)
                for s in range(C_TC):
                    rows = slice(s * LANES + g * C_GROUP, s * LANES + (g + 1) * C_GROUP)
                    re, im = p_ref[rows, lo:lo + LANES], p_ref[rows, hi:hi + LANES]
                    if g % 2 == 0:
                        im = pltpu.roll(im, C_STATE, 1)
                    else:
                        re = pltpu.roll(re, C_STATE, 1)
                    o_ref[0, rows, :] = jnp.where(lane < C_STATE, re, im)
        else:
            for z in range(2):
                for g in range(C_GB):
                    rows = slice(z * half + g * C_STATE, z * half + (g + 1) * C_STATE)
                    val = jnp.zeros((C_STATE, LANES), F32)
                    for t in range(C_TC):
                        blk = pltpu.roll(p_ref[rows, t * LANES:(t + 1) * LANES], ((t - g) * C_GROUP) % LANES, 1)
                        val = jnp.where((lane >= t * C_GROUP) & (lane < (t + 1) * C_GROUP), blk, val)
                    o_ref[0, rows, :] = val

    if fold == "lags":
        out_spec = pl.BlockSpec((1, C_TC, LANES, LANES), lambda c: (c, 0, 0, 0))
        out_shape = jax.ShapeDtypeStruct((nb, C_TC, LANES, LANES), F32)
    else:
        out_spec = pl.BlockSpec((1, M, LANES), lambda c: (c, 0, 0))
        out_shape = jax.ShapeDtypeStruct((nb, M, LANES), F32)
    return pl.pallas_call(
        body, name=name, grid=(nb,),
        in_specs=[pl.BlockSpec((1, K, M), lambda c: (c, 0, 0)), pl.BlockSpec((1, K, N), lambda c: (c, 0, 0))],
        out_specs=out_spec, out_shape=out_shape, scratch_shapes=[pltpu.VMEM((M, N), F32)],
        compiler_params=_cparams("parallel"),
    )(a, b)


def _loss_head(y, w, res, target, *, name):
    T, C = y.shape
    tm = 512

    def body(y_ref, w_ref, r_ref, t_ref, l_ref, d_ref, dy_ref, dw_ref):
        yv = y_ref[...].astype(F32)
        r = lax.rsqrt(jnp.mean(yv * yv, axis=-1, keepdims=True) + NORM_EPS)
        yh = yv * r
        err = yh * w_ref[...] + r_ref[...] - t_ref[...]
        d = err * (1.0 / C)
        d_ref[...] = d
        sq = err * err
        part = jnp.zeros((8, LANES), F32)
        for rr in range(0, tm, 8):
            for c in range(0, C, LANES):
                part = part + sq[rr:rr + 8, c:c + LANES]
        dw = jnp.sum(d * yh, axis=0, keepdims=True)

        @pl.when(pl.program_id(0) == 0)
        def _():
            l_ref[...] = part
            dw_ref[...] = dw

        @pl.when(pl.program_id(0) > 0)
        def _():
            l_ref[...] += part
            dw_ref[...] += dw

        gx = d * w_ref[...]
        dy_ref[...] = (r * (gx - yh * jnp.mean(gx * yh, axis=-1, keepdims=True))).astype(BF16)

    row = pl.BlockSpec((tm, C), lambda i: (i, 0))
    vec = pl.BlockSpec((1, C), lambda i: (0, 0))
    acc = pl.BlockSpec((8, LANES), lambda i: (0, 0))
    return pl.pallas_call(
        body, name=name, grid=(T // tm,), in_specs=[row, vec, row, row], out_specs=[acc, row, row, vec],
        out_shape=[jax.ShapeDtypeStruct((8, LANES), F32), jax.ShapeDtypeStruct((T, C), F32),
                   jax.ShapeDtypeStruct((T, C), BF16), jax.ShapeDtypeStruct((1, C), F32)],
        compiler_params=_cparams("arbitrary"),
    )(y, w, res, target)


def _adamw(w, g, m, v, *, name):
    shape = w.shape
    size = int(np.prod(shape))
    cols = LANES if (shape[-1] < LANES and size % LANES == 0) else shape[-1]
    rows = size // cols
    tm = _tile(rows, 256) if rows % 8 == 0 else rows
    w2, g2, m2, v2 = (t.reshape(rows, cols) for t in (w, g, m, v))

    def body(w_ref, g_ref, m_ref, v_ref, d_ref, nm_ref, nv_ref):
        gg = g_ref[...]
        nm = ADAM_B1 * m_ref[...] + (1.0 - ADAM_B1) * gg
        nv = ADAM_B2 * v_ref[...] + (1.0 - ADAM_B2) * (gg * gg)
        m_hat = nm / (1.0 - ADAM_B1 ** ADAM_STEP)
        v_hat = nv / (1.0 - ADAM_B2 ** ADAM_STEP)
        d_ref[...] = -ADAM_LR * (m_hat / (jnp.sqrt(v_hat) + ADAM_EPS) + ADAM_WD * w_ref[...])
        nm_ref[...] = nm
        nv_ref[...] = nv

    blk = pl.BlockSpec((tm, cols), lambda i: (i, 0))
    outs = pl.pallas_call(
        body, name=name, grid=(rows // tm,), in_specs=[blk] * 4, out_specs=[blk] * 3,
        out_shape=[jax.ShapeDtypeStruct((rows, cols), F32)] * 3, compiler_params=_cparams("parallel"),
    )(w2, g2, m2, v2)
    return tuple(o.reshape(shape) for o in outs)


def _all_gather(shard, *, name):
    R, C = shard.shape

    def body(x_ref, out_ref, send_sems, recv_sems, local_sem):
        x, y, c = lax.axis_index("x"), lax.axis_index("y"), lax.axis_index("c")
        me, sibling = (x, y, c), (x, y, 1 - c)
        chips = [(1 - x, y), (x, 1 - y), (1 - x, 1 - y)]

        def rows(px, py, pc):
            return out_ref.at[_logical(px, py, pc)]

        def copy(k, block, to, src=None):
            return pltpu.make_async_remote_copy(
                src_ref=rows(*block) if src is None else src, dst_ref=rows(*block),
                send_sem=send_sems.at[k], recv_sem=recv_sems.at[k], device_id=to, device_id_type=_MESH)

        mine = pltpu.make_async_copy(x_ref, rows(*me), local_sem)
        mine.start()
        first = [copy(0, me, sibling, src=x_ref)]
        first += [copy(1 + j, me, (*chip, c), src=x_ref) for j, chip in enumerate(chips)]
        for cp in first:
            cp.start()
        passed = [copy(4 + j, (*chip, c), sibling) for j, chip in enumerate(chips)]
        for j, chip in enumerate(chips):
            copy(1 + j, (*chip, c), me).wait_recv()
            passed[j].start()
        copy(0, sibling, me).wait_recv()
        for j, chip in enumerate(chips):
            copy(4 + j, (*chip, 1 - c), me).wait_recv()
        for cp in first + passed:
            cp.wait_send()
        mine.wait()

    return pl.pallas_call(
        body, name=name, out_shape=jax.ShapeDtypeStruct((N_DEV, R, C), shard.dtype),
        in_specs=[_HBM], out_specs=_HBM,
        scratch_shapes=[pltpu.SemaphoreType.DMA((7,)), pltpu.SemaphoreType.DMA((7,)), pltpu.SemaphoreType.DMA],
    )(shard)


def _gather_weights(shards, *, name):
    nt = len(shards)

    def body(*refs):
        ins, outs = refs[:nt], refs[nt:2 * nt]
        send_sems, recv_sems, local_sems = refs[2 * nt:]
        x, y, c = lax.axis_index("x"), lax.axis_index("y"), lax.axis_index("c")
        me, sibling = (x, y, c), (x, y, 1 - c)
        chips = [(1 - x, y), (x, 1 - y), (1 - x, 1 - y)]

        def copy(t, k, block, to, src=None):
            rows = outs[t].at[_logical(*block)]
            return pltpu.make_async_remote_copy(
                src_ref=rows if src is None else src, dst_ref=rows,
                send_sem=send_sems.at[t, k], recv_sem=recv_sems.at[t, k], device_id=to, device_id_type=_MESH)

        mine = [pltpu.make_async_copy(ins[t], outs[t].at[_logical(*me)], local_sems.at[t]) for t in range(nt)]
        for cp in mine:
            cp.start()
        started = []
        for t in range(nt):
            started.append(copy(t, 0, me, sibling, src=ins[t]))
            started += [copy(t, 1 + j, me, (*chip, c), src=ins[t]) for j, chip in enumerate(chips)]
        for cp in started:
            cp.start()
        for j, chip in enumerate(chips):
            for t in range(nt):
                copy(t, 1 + j, (*chip, c), me).wait_recv()
                fwd = copy(t, 4 + j, (*chip, c), sibling)
                fwd.start()
                started.append(fwd)
        for t in range(nt):
            copy(t, 0, sibling, me).wait_recv()
            for j, chip in enumerate(chips):
                copy(t, 4 + j, (*chip, 1 - c), me).wait_recv()
        for cp in started:
            cp.wait_send()
        for cp in mine:
            cp.wait()

    return pl.pallas_call(
        body, name=name, out_shape=[jax.ShapeDtypeStruct((N_DEV,) + s.shape, s.dtype) for s in shards],
        in_specs=[_HBM] * nt, out_specs=[_HBM] * nt,
        scratch_shapes=[pltpu.SemaphoreType.DMA((nt, 7)), pltpu.SemaphoreType.DMA((nt, 7)), pltpu.SemaphoreType.DMA((nt,))],
    )(*shards)


def _sum_rows(stacked, *, name):
    _, R, C = stacked.shape
    tr = R
    if N_DEV * R * C * stacked.dtype.itemsize > 12 * 1024 * 1024:
        for cand in range(512, 15, -16):
            if R % cand == 0:
                tr = cand
                break

    def body(s_ref, o_ref):
        acc = s_ref[0].astype(F32)
        for k in range(1, N_DEV):
            acc = acc + s_ref[k].astype(F32)
        o_ref[...] = acc

    return pl.pallas_call(
        body, name=name, grid=(R // tr,),
        in_specs=[pl.BlockSpec((N_DEV, tr, C), lambda i: (0, i, 0))], out_specs=pl.BlockSpec((tr, C), lambda i: (i, 0)),
        out_shape=jax.ShapeDtypeStruct((R, C), F32), compiler_params=_cparams("parallel"),
    )(stacked)


_LARGE = (("ab_w_in", 1, True), ("ab_w_out", 1, False), ("s5_w_glu", 1, True), ("xattn_wq", 2, False),
          ("xattn_wkv", 2, True), ("xattn_wo", 2, False), ("ffn_w_in", 2, True), ("ffn_w_out", 2, False))
_LARGE_KEYS = tuple((n, l) for n, layers, _ in _LARGE for l in range(layers))
_TRANSPOSED = {n: t for n, _, t in _LARGE}


def _owner_major(name, w):
    return w.T if _TRANSPOSED[name] else w


def _lb_from_logits(logits):
    return jnp.cumsum(jax.nn.softmax(logits, axis=0), axis=0)[0:1]


def _local_step(x, mem, target, W, shards=None):
    B, L, _ = x.shape
    T = B * L
    x0 = x.reshape(T, D_MODEL)
    memf = mem.reshape(B * MEM_LEN, D_MODEL)
    nw = W["norm_w"]
    cos2, sin2 = _rope_tables(L)
    W = dict(W)
    G, received = {}, {}

    def mmx(a, b, gather=(), scatter=(), **kw):
        if shards is None or not (gather or scatter):
            return _mm(a, b, **kw)
        out, gathered, got = _mm(a, b, gather=[shards[k] for k in gather],
                                 scatter=[G[k].reshape(N_DEV, -1, D_MODEL) for k in scatter], **kw)
        for k, g in zip(gather, gathered):
            W[k] = g.reshape(-1, D_MODEL)
        for k, r in zip(scatter, got):
            received[k] = r
        return out

    def vec(v):
        return v.reshape(1, -1)

    saved = []
    xin = x0
    for layer in range(2):
        s = {"x0": xin}
        tag = f"l{layer}"
        if layer == 0:
            h1 = _rms_fwd(xin, vec(nw[0, 0]), name="norm_pre_mix_l0", out_dtype=BF16)
        s["h1"] = h1
        if layer == 0:
            lb, lb_vjp = jax.vjp(_lb_from_logits, W["hgrn_lb_logits"])
            onw = W["hgrn_out_norm_w"].reshape(1, A_WIDTH)
            z = mmx(h1, W["ab_w_in", 0], tb=True, name="ab_in",
                    gather=[("ab_w_out", 0), ("xattn_wq", 0), ("xattn_wkv", 0), ("xattn_wo", 0)])
            ffn0 = [("ffn_w_in", 0), ("ffn_w_out", 0)] if shards is not None else []
            oa, o_raw, s_start, gathered = _hgrn_fwd(z, lb, onw, B, L, name="hgrn_fwd", gather=[shards[k] for k in ffn0])
            for key, g in zip(ffn0, gathered):
                W[key] = g.reshape(-1, D_MODEL)
            qr, kr = _rope_fwd(z, cos2, sin2, B, L, name="rope_qk")
            os_, ls_ = [], []
            for dil in B_DILS:
                o_g, l_g = _dil_fwd(qr, kr, z, dil, B, L, name=f"dil_fwd_{dil}")
                os_.append(o_g)
                ls_.append(l_g)
            ob, lse = _dil_combine(os_, ls_, name="dil_combine")
            ymix = jnp.concatenate([oa, ob.astype(BF16)], axis=-1)
            y1 = _mm(ymix, W["ab_w_out", 0], out_dtype=BF16, name="ab_out")
            s.update(z=z, lb=lb, lb_vjp=lb_vjp, onw=onw, o_raw=o_raw, s_start=s_start, qr=qr, kr=kr, ob=ob, lse=lse, ymix=ymix)
        else:
            p5 = tuple(W[n][0] for n in ("s5_lambda_re", "s5_lambda_im", "s5_log_dt", "s5_b_re", "s5_b_im", "s5_c_re", "s5_c_im"))
            (kern, ez, fz, _, _), s5_vjp = jax.vjp(_s5_build, *p5)
            tab_r, tab_i = _s5_scan_tables(p5[0], p5[1], p5[2], int(math.log2(L // C_TC)))
            kbd, lag_vjp = jax.vjp(_s5_lag_blocks, kern)
            mats = tuple(_s5_dense(kbd, ez, fz, name="s5_maps")) + (tab_r, tab_i)
            dsk = W["s5_d"].reshape(1, D_MODEL)
            ffn1 = [("ffn_w_in", 1)] if shards is not None else []
            gl, ypre, x8, xs, gathered = _s5_fwd(h1, dsk, *mats, B, L, name="s5_fwd", gather=[shards[k] for k in ffn1])
            for key, g in zip(ffn1, gathered):
                W[key] = g.reshape(-1, D_MODEL)
            w_glu = _interleave_rows(W["s5_w_glu", 0])
            zg, y1 = _mm(gl, w_glu, tb=True, name="s5_glu_in", gate=("fwd", "glu"))
            s.update(s5_vjp=s5_vjp, lag_vjp=lag_vjp, mats=mats, x8=x8, xs=xs, dsk=dsk, gl=gl, ypre=ypre, zg=zg, w_glu=w_glu)
        x1, h2 = _rms_fwd(y1, vec(nw[layer, 1]), xin, name=f"norm_post_mix_{tag}", then=(vec(nw[layer, 2]), BF16))
        memn = _rms_fwd(memf, vec(W["mem_norm_w"][layer]), name=f"norm_mem_{tag}", out_dtype=BF16)
        q = _mm(h2, W["xattn_wq", layer], out_dtype=BF16, name=f"x_q_{tag}")
        kv = _mm(memn, W["xattn_wkv", layer], tb=True, out_dtype=BF16, name=f"x_kv_{tag}")
        o = _xattn_fwd(q, kv, B, L, name=f"x_attn_{tag}")
        y2 = _mm(o, W["xattn_wo", layer], out_dtype=BF16, name=f"x_o_{tag}")
        x2, h3 = _rms_fwd(y2, vec(nw[layer, 3]), x1, name=f"norm_post_x_{tag}", then=(vec(nw[layer, 4]), BF16))
        w_ffn_in = _interleave_rows(W["ffn_w_in", layer])
        zf, u = mmx(h3, w_ffn_in, tb=True, name=f"ffn_in_{tag}", gate=("fwd", "swiglu"),
                    gather=[("s5_w_glu", 0), ("xattn_wq", 1), ("xattn_wkv", 1), ("xattn_wo", 1)] if layer == 0 else [])
        y3 = mmx(u, W["ffn_w_out", layer], out_dtype=BF16, name=f"ffn_out_{tag}", gather=[("ffn_w_out", 1)] if layer == 0 else [])
        if layer == 0:
            xin, h1 = _rms_fwd(y3, vec(nw[0, 5]), x2, name="norm_post_ffn_l0", then=(vec(nw[1, 0]), F32))
        s.update(y1=y1, x1=x1, h2=h2, memn=memn, q=q, kv=kv, o=o, y2=y2, x2=x2, h3=h3, zf=zf, u=u, y3=y3, w_ffn_in=w_ffn_in)
        saved.append(s)

    d_norm = [[None] * 6 for _ in range(2)]
    d_memn = [None, None]
    loss_parts, dx, dy3, d_norm[1][5] = _loss_head(saved[1]["y3"], vec(nw[1, 5]), saved[1]["x2"],
                                                   target.reshape(T, D_MODEL), name="loss_head")
    for layer in (1, 0):
        s = saved[layer]
        tag = f"l{layer}"
        dzf = mmx(dy3, W["ffn_w_out", layer], tb=True, name=f"b_ffn_out_dx_{tag}", gate=("bwd", "swiglu", s["zf"]),
                  scatter=[("xattn_wkv", 1), ("s5_w_glu", 0)] if layer == 0 else [])
        G["ffn_w_out", layer] = _mm(s["u"], dy3, ta=True, out_dtype=BF16, name=f"b_ffn_out_dw_{tag}")
        early = []
        if shards is not None and layer == 0:
            shards = {**shards, "small_early": _pack_small([G[n] for n, _ in _SMALL_EARLY])}
            early = ["small_early"]
        G["ffn_w_in", layer] = _interleave_rows(
            mmx(dzf, s["h3"], ta=True, out_dtype=BF16, name=f"b_ffn_in_dw_{tag}", gather=early,
                scatter=[("ffn_w_out", layer)]), inverse=True)
        if early:
            received["small_early"] = W["small_early"].reshape(N_DEV, -1, LANES)
        dh3 = mmx(dzf, s["w_ffn_in"], out_dtype=BF16, name=f"b_ffn_in_dx_{tag}", scatter=[("ffn_w_in", layer)])
        dx, d_norm[layer][4], dy2, d_norm[layer][3] = _rms_bwd(
            s["x2"], vec(nw[layer, 4]), dh3, dx, name=f"bnorm_pre_ffn_{tag}", then=(s["y2"], vec(nw[layer, 3])))
        do = _mm(dy2, W["xattn_wo", layer], tb=True, out_dtype=BF16, name=f"b_x_o_dx_{tag}")
        G["xattn_wo", layer] = _mm(s["o"], dy2, ta=True, out_dtype=BF16, name=f"b_x_o_dw_{tag}")
        dq, dkv = _xattn_bwd(s["q"], s["kv"], do, B, L, name=f"b_x_attn_{tag}")
        G["xattn_wq", layer] = _mm(s["h2"], dq, ta=True, out_dtype=BF16, name=f"b_x_q_dw_{tag}")
        dh2 = _mm(dq, W["xattn_wq", layer], tb=True, out_dtype=BF16, name=f"b_x_q_dx_{tag}")
        G["xattn_wkv", layer] = _mm(dkv, s["memn"], ta=True, out_dtype=BF16, name=f"b_x_kv_dw_{tag}")
        dmemn = _mm(dkv, W["xattn_wkv", layer], out_dtype=BF16, name=f"b_x_kv_dx_{tag}")
        _, d_memn[layer] = _rms_bwd(memf, vec(W["mem_norm_w"][layer]), dmemn, name=f"bnorm_mem_{tag}", out_dtype=BF16)
        dx, d_norm[layer][2], dy1, d_norm[layer][1] = _rms_bwd(
            s["x1"], vec(nw[layer, 2]), dh2, dx, name=f"bnorm_pre_x_{tag}", then=(s["y1"], vec(nw[layer, 1])))
        if layer == 0:
            z = s["z"]
            dymix = _mm(dy1, W["ab_w_out", 0], tb=True, name="b_ab_out_dx")
            G["ab_w_out", 0] = _mm(s["ymix"], dy1, ta=True, out_dtype=BF16, name="b_ab_out_dw")
            early = [("xattn_wo", 0), ("xattn_wq", 0), ("xattn_wkv", 0), ("ab_w_out", 0)] if shards is not None else []
            dqa, dfa, dia, dga, d_onw, d_lb, got = _hgrn_bwd(
                z, s["lb"], s["onw"], s["o_raw"], s["s_start"], dymix, B, L, name="hgrn_bwd",
                scatter=[G[k].reshape(N_DEV, -1, D_MODEL) for k in early])
            received.update(zip(early, got))
            dqs, dks, dvs = [], [], []
            for dil in B_DILS:
                dqs.append(_dil_bwd_q(s["qr"], s["kr"], z, dymix, s["ob"], s["lse"], dil, B, L, name=f"dil_bwd_q_{dil}"))
                dk_g, dv_g = _dil_bwd_kv(s["qr"], s["kr"], z, dymix, s["ob"], s["lse"], dil, B, L, name=f"dil_bwd_kv_{dil}")
                dks.append(dk_g)
                dvs.append(dv_g)
            dqkv = _rope_bwd(dqs, dks, dvs, cos2, sin2, B, L, name="b_rope")
            dz = jnp.concatenate([dqa, dfa, dia, dga, dqkv], axis=-1)
            G["ab_w_in", 0] = _mm(dz, s["h1"], ta=True, out_dtype=BF16, name="b_ab_in_dw")
            dh1 = mmx(dz, W["ab_w_in", 0], out_dtype=BF16, name="b_ab_in_dx", scatter=[("ab_w_in", 0)])
            G["hgrn_out_norm_w"] = jnp.sum(d_onw.reshape(B, A_WIDTH), axis=0, keepdims=True)
            d_lb_row = jnp.sum(d_lb.reshape(B, A_WIDTH), axis=0, keepdims=True)
            G["hgrn_lb_logits"] = s["lb_vjp"](d_lb_row)[0]
        else:
            dzg = _gated_bwd(s["zg"], dy1, "glu", name="b_s5_glu")
            G["s5_w_glu", 0] = _interleave_rows(
                _mm(dzg, s["gl"], ta=True, out_dtype=BF16, name="b_s5_glu_dw"), inverse=True)
            dgl = mmx(dzg, s["w_glu"], name="b_s5_glu_dx", scatter=[("xattn_wo", 1), ("xattn_wq", 1)])
            dh1, dy8, de_in, da, d_dsk = _s5_bwd(dgl, s["ypre"], s["h1"], s["dsk"], s["xs"], *s["mats"], B, L, name="s5_bwd")
            dkbd = _bmm_tn(s["x8"], dy8, C_NB, "lags", name="s5_bwd_dm").transpose(1, 0, 2, 3)
            dfz = _bmm_tn(s["xs"], dy8, C_NB, "f", name="s5_bwd_df").reshape(C_NB, 2, C_GB, C_STATE, C_TC, C_GROUP)
            dez = _bmm_tn(s["x8"], de_in, C_NB, "e", name="s5_bwd_de").reshape(C_NB, C_TC, C_GB, C_GROUP, 2, C_STATE)
            half = C_S8 // 2
            da_r = (da[:, 0, :half] + da[:, 0, half:]).reshape(C_GROUPS, C_STATE)
            da_i = (da[:, 1, half:] - da[:, 1, :half]).reshape(C_GROUPS, C_STATE)
            gp = s["s5_vjp"](s["lag_vjp"](dkbd) + (dez, dfz, da_r, da_i))
            for n, gv in zip(("s5_lambda_re", "s5_lambda_im", "s5_log_dt", "s5_b_re", "s5_b_im", "s5_c_re", "s5_c_im"), gp):
                G[n] = gv[None]
            G["s5_d"] = d_dsk
        if layer == 1:
            dx, d_norm[1][0], dy3, d_norm[0][5] = _rms_bwd(
                s["x0"], vec(nw[1, 0]), dh1, dx, name="bnorm_pre_mix_l1", then=(saved[0]["y3"], vec(nw[0, 5])))
        else:
            dx, d_norm[0][0] = _rms_bwd(s["x0"], vec(nw[0, 0]), dh1, dx, name="bnorm_pre_mix_l0")

    G["norm_w"] = jnp.stack([jnp.concatenate(d_norm[l], axis=0) for l in range(2)])
    G["mem_norm_w"] = jnp.concatenate(d_memn, axis=0)
    if shards is not None:
        G.update(received)
    return loss_parts, dx.reshape(B, L, D_MODEL), G


_SMALL_LATE = (("norm_w", (2, 6, 1024)), ("mem_norm_w", (2, 1024)), ("hgrn_lb_logits", (3, 512)), ("hgrn_out_norm_w", (1, 512)))
_SMALL_EARLY = (("s5_lambda_re", (1, 64, 64)), ("s5_lambda_im", (1, 64, 64)), ("s5_log_dt", (1, 64)),
                ("s5_b_re", (1, 64, 64, 16)), ("s5_b_im", (1, 64, 64, 16)), ("s5_c_re", (1, 64, 16, 64)),
                ("s5_c_im", (1, 64, 16, 64)), ("s5_d", (1, 1024)))
_SMALL = _SMALL_LATE + _SMALL_EARLY


def _pack_small(values):
    flat = jnp.concatenate([v.reshape(-1) for v in values])
    return jnp.pad(flat, (0, (-flat.shape[0]) % (8 * LANES))).reshape(-1, LANES)


def _unpack_small(flat, spec):
    out, off = {}, 0
    for n, shp in spec:
        size = int(np.prod(shp))
        out[n] = flat[off:off + size].reshape(shp)
        off += size
    return out, off

_WEIGHT_ORDER = ('norm_w', 'mem_norm_w', 'ab_w_in', 'ab_w_out', 'hgrn_lb_logits', 'hgrn_out_norm_w', 's5_lambda_re',
                 's5_lambda_im', 's5_log_dt', 's5_b_re', 's5_b_im', 's5_c_re', 's5_c_im', 's5_d', 's5_w_glu', 'xattn_wq',
                 'xattn_wkv', 'xattn_wo', 'ffn_w_in', 'ffn_w_out')


def kernel(x, mem, norm_w, mem_norm_w, ab_w_in, ab_w_out, hgrn_lb_logits, hgrn_out_norm_w, s5_lambda_re, s5_lambda_im, s5_log_dt, s5_b_re, s5_b_im, s5_c_re, s5_c_im, s5_d, s5_w_glu, xattn_wq, xattn_wkv, xattn_wo, ffn_w_in, ffn_w_out, loss_target, m_norm_w, m_mem_norm_w, m_ab_w_in, m_ab_w_out, m_hgrn_lb_logits, m_hgrn_out_norm_w, m_s5_lambda_re, m_s5_lambda_im, m_s5_log_dt, m_s5_b_re, m_s5_b_im, m_s5_c_re, m_s5_c_im, m_s5_d, m_s5_w_glu, m_xattn_wq, m_xattn_wkv, m_xattn_wo, m_ffn_w_in, m_ffn_w_out, v_norm_w, v_mem_norm_w, v_ab_w_in, v_ab_w_out, v_hgrn_lb_logits, v_hgrn_out_norm_w, v_s5_lambda_re, v_s5_lambda_im, v_s5_log_dt, v_s5_b_re, v_s5_b_im, v_s5_c_re, v_s5_c_im, v_s5_d, v_s5_w_glu, v_xattn_wq, v_xattn_wkv, v_xattn_wo, v_ffn_w_in, v_ffn_w_out):
    local = dict(norm_w=norm_w, mem_norm_w=mem_norm_w, ab_w_in=ab_w_in, ab_w_out=ab_w_out, hgrn_lb_logits=hgrn_lb_logits,
                 hgrn_out_norm_w=hgrn_out_norm_w, s5_lambda_re=s5_lambda_re, s5_lambda_im=s5_lambda_im, s5_log_dt=s5_log_dt,
                 s5_b_re=s5_b_re, s5_b_im=s5_b_im, s5_c_re=s5_c_re, s5_c_im=s5_c_im, s5_d=s5_d, s5_w_glu=s5_w_glu,
                 xattn_wq=xattn_wq, xattn_wkv=xattn_wkv, xattn_wo=xattn_wo, ffn_w_in=ffn_w_in, ffn_w_out=ffn_w_out)
    mom_m = dict(zip(_WEIGHT_ORDER, (m_norm_w, m_mem_norm_w, m_ab_w_in, m_ab_w_out, m_hgrn_lb_logits, m_hgrn_out_norm_w, m_s5_lambda_re, m_s5_lambda_im, m_s5_log_dt, m_s5_b_re, m_s5_b_im, m_s5_c_re, m_s5_c_im, m_s5_d, m_s5_w_glu, m_xattn_wq, m_xattn_wkv, m_xattn_wo, m_ffn_w_in, m_ffn_w_out)))
    mom_v = dict(zip(_WEIGHT_ORDER, (v_norm_w, v_mem_norm_w, v_ab_w_in, v_ab_w_out, v_hgrn_lb_logits, v_hgrn_out_norm_w, v_s5_lambda_re, v_s5_lambda_im, v_s5_log_dt, v_s5_b_re, v_s5_b_im, v_s5_c_re, v_s5_c_im, v_s5_d, v_s5_w_glu, v_xattn_wq, v_xattn_wkv, v_xattn_wo, v_ffn_w_in, v_ffn_w_out)))
    dev = 4 * lax.axis_index("x") + 2 * lax.axis_index("y") + lax.axis_index("c")

    shards = {(n, l): _owner_major(n, local[n][l]).astype(BF16) for n, l in _LARGE_KEYS}
    first = ("ab_w_in", 0)
    W = {first: _gather_weights([shards[first]], name="gather_first")[0].reshape(-1, D_MODEL)}
    tiny = jnp.concatenate([norm_w.reshape(-1), s5_d.reshape(-1)])
    tiny = jnp.pad(tiny, (0, 16 * LANES - tiny.shape[0])).reshape(16, LANES)
    tiny_all = _all_gather(tiny, name="gather_tiny").reshape(N_DEV, 16 * LANES)
    W["norm_w"] = tiny_all[:, :12 * LANES].reshape(N_DEV, 2, 6, LANES).transpose(1, 2, 0, 3).reshape(2, 6, D_MODEL)
    W["s5_d"] = tiny_all[:, 12 * LANES:13 * LANES].reshape(1, D_MODEL)
    for n in ("mem_norm_w", "hgrn_lb_logits", "hgrn_out_norm_w", "s5_lambda_re", "s5_lambda_im", "s5_log_dt",
              "s5_b_re", "s5_b_im", "s5_c_re", "s5_c_im"):
        W[n] = local[n]

    loss_parts, grad_x, G = _local_step(x, mem, loss_target, W, shards)

    g_layers = {}
    for n, l in _LARGE_KEYS:
        g = _sum_rows(G[n, l], name=f"sum_grads_{n}_{l}")
        g_layers.setdefault(n, []).append(g.T if _TRANSPOSED[n] else g)
    g_local = {n: jnp.stack(gl) for n, gl in g_layers.items()}
    small = _pack_small([G[n] for n, _ in _SMALL_LATE] + [0.5 / D_MODEL * jnp.sum(loss_parts)])
    small_sum = _sum_rows(_all_gather(small, name="gather_small"), name="sum_small").reshape(-1)
    g_full, off = _unpack_small(small_sum, _SMALL_LATE)
    loss = small_sum[off]
    early_sum = _sum_rows(G["small_early"], name="sum_small_early").reshape(-1)
    g_full.update(_unpack_small(early_sum, _SMALL_EARLY)[0])
    grads = dict(g_local)
    for n, shp in _SMALL:
        if n == "norm_w":
            grads[n] = lax.dynamic_slice_in_dim(g_full[n], dev * LANES, LANES, axis=2)
        elif n == "s5_d":
            grads[n] = lax.dynamic_slice_in_dim(g_full[n], dev * LANES, LANES, axis=1)
        else:
            grads[n] = g_full[n]

    delta, new_m, new_v = {}, {}, {}
    for n in _WEIGHT_ORDER:
        delta[n], new_m[n], new_v[n] = _adamw(local[n], grads[n], mom_m[n], mom_v[n], name=f"adamw_{n}")
    return (loss, grad_x, *[grads[n] for n in _WEIGHT_ORDER], *[delta[n] for n in _WEIGHT_ORDER],
            *[new_m[n] for n in _WEIGHT_ORDER], *[new_v[n] for n in _WEIGHT_ORDER])
```
